```python
import jax, jax.numpy as jnp
from jax import lax
import numpy as np

D_MODEL = 1024
BATCH = 16
SEQ = 2048
DEPTH = 1

D_MIX = D_MODEL
A_HEADS = 8
A_HEAD_DIM = 64
D_A = A_HEADS * A_HEAD_DIM
DILATED_BRANCHES = ((128, 1), (512, 4), (2048, 16))
BLK = 128
B_HEADS = 8
B_NOPE_DIM = 64
B_ROPE_DIM = 32
B_V_DIM = 64
D_B = B_HEADS * B_V_DIM
Q_LORA = 384
KV_LORA = 256
ROPE_THETA = 10000.0
P_IN = 3 * D_A + Q_LORA + KV_LORA + B_ROPE_DIM
N_BUCKETS = 32
MAX_DISTANCE = 2048
D_FF = ((-(-8 * D_MODEL // 3) + 255) // 256) * 256
N_MOD = 6
EPS = 1e-6
NEG = -1e30

kernel_name = 'hybrid_dilated_mla_adaln_layer'


def _rmsnorm(x, g):
    xf = x.astype(jnp.float32)
    y = xf * lax.rsqrt(jnp.mean(xf * xf, axis=-1, keepdims=True) + EPS)
    return (y * g.astype(jnp.float32)).astype(x.dtype)


def _t5_bucket(dist):
    max_exact = N_BUCKETS // 2
    d = np.maximum(dist, 1).astype(np.float64)
    large = max_exact + (np.log(d / max_exact) / np.log(MAX_DISTANCE / max_exact)
                         * (N_BUCKETS - max_exact)).astype(np.int64)
    large = np.minimum(large, N_BUCKETS - 1)
    return np.where(dist < max_exact, dist, large).astype(np.int32)


def _dilated_branch(q, k, v, rel_bias, window, dilation):
    B, S, H, E = q.shape
    span = window // dilation
    n = S // dilation
    nb = -(-n // BLK)
    n_pad = nb * BLK

    def to_residue(t):
        t = t.reshape(B, n, dilation, H, E).transpose(0, 2, 3, 1, 4)
        return jnp.pad(t, ((0, 0), (0, 0), (0, 0), (0, n_pad - n), (0, 0)))

    def band(t):
        t = jnp.pad(to_residue(t), ((0, 0), (0, 0), (0, 0), (BLK, 0), (0, 0)))
        t = t.reshape(B, dilation, H, nb + 1, BLK, E)
        return jnp.concatenate([t[:, :, :, :-1], t[:, :, :, 1:]], axis=4)

    qb = to_residue(q).reshape(B, dilation, H, nb, BLK, E)
    kb, vb = band(k), band(v)
    a = np.arange(BLK)[:, None]
    bk = np.arange(2 * BLK)[None, :]
    steps = BLK + a - bk
    valid = (steps >= 0) & (steps <= span)
    first = valid & (bk >= BLK)
    mask = np.concatenate([first[None], np.broadcast_to(valid, (nb - 1,) + valid.shape)], axis=0)
    bucket = _t5_bucket(np.clip(steps, 0, span) * dilation)
    bias = jnp.transpose(rel_bias[bucket], (2, 0, 1)).astype(jnp.float32)

    logits = jnp.einsum('bdhnqe,bdhnke->bdhnqk', qb, kb,
                        preferred_element_type=jnp.float32) * (E ** -0.5)
    logits = logits + bias[None, None, :, None]
    logits = jnp.where(jnp.asarray(mask)[None, None, None], logits, NEG)
    m = jnp.max(logits, axis=-1, keepdims=True)
    p = jnp.exp(logits - m)
    s = jnp.sum(p, axis=-1, keepdims=True)
    o = jnp.einsum('bdhnqk,bdhnke->bdhnqe', p, vb.astype(jnp.float32)) / s
    lse = (m + jnp.log(s))[..., 0]

    def from_residue(t):
        t = t.reshape((B, dilation, H, n_pad) + t.shape[5:])[:, :, :, :n]
        t = jnp.moveaxis(t, 3, 1)
        return t.reshape((B, S, H) + t.shape[4:])

    return from_residue(o), from_residue(lse)


def _rope(t):
    S, R = t.shape[1], t.shape[-1]
    half = R // 2
    inv = ROPE_THETA ** (-jnp.arange(half, dtype=jnp.float32) / half)
    ang = jnp.arange(S, dtype=jnp.float32)[:, None] * inv[None, :]
    cos, sin = jnp.cos(ang)[None, :, None], jnp.sin(ang)[None, :, None]
    t1, t2 = t[..., :half].astype(jnp.float32), t[..., half:].astype(jnp.float32)
    return jnp.concatenate([t1 * cos - t2 * sin, t1 * sin + t2 * cos], axis=-1).astype(t.dtype)


def _mla_attention(q_nope, q_rope, k_nope, k_rope, v):
    B, S, H, _ = q_nope.shape
    nq = S // BLK
    scale = (B_NOPE_DIM + B_ROPE_DIM) ** -0.5
    qn = jnp.moveaxis(q_nope.reshape(B, nq, BLK, H, -1), 1, 0)
    qr = jnp.moveaxis(q_rope.reshape(B, nq, BLK, H, -1), 1, 0)
    kpos = jnp.arange(S)

    def block(args):
        qn_b, qr_b, i = args
        logits = (jnp.einsum('bqhe,bkhe->bhqk', qn_b, k_nope, preferred_element_type=jnp.float32)
                  + jnp.einsum('bqhe,bke->bhqk', qr_b, k_rope, preferred_element_type=jnp.float32)) * scale
        qpos = i * BLK + jnp.arange(BLK)
        logits = jnp.where(qpos[:, None] >= kpos[None, :], logits, NEG)
        p = jax.nn.softmax(logits, axis=-1)
        return jnp.einsum('bhqk,bkhe->bqhe', p.astype(v.dtype), v)

    o = lax.map(block, (qn, qr, jnp.arange(nq)))
    return jnp.moveaxis(o, 0, 1).reshape(B, S, H, -1)


def _mixer(h, w_in, g_cq, w_uq, g_ckv, w_ukv, rel_bias, g_out_a, g_out_b, w_out):
    B, S, _ = h.shape
    proj = h @ w_in
    i1, i2, i3 = D_A, 2 * D_A, 3 * D_A
    i4, i5 = i3 + Q_LORA, i3 + Q_LORA + KV_LORA
    qa, ka, va, cq, ckv, kr = jnp.split(proj, [i1, i2, i3, i4, i5], axis=-1)
    qa = qa.reshape(B, S, A_HEADS, A_HEAD_DIM)
    ka = ka.reshape(B, S, A_HEADS, A_HEAD_DIM)
    va = va.reshape(B, S, A_HEADS, A_HEAD_DIM)
    branches = [_dilated_branch(qa, ka, va, rel_bias, w, d) for (w, d) in DILATED_BRANCHES]
    o_stack = jnp.stack([br[0] for br in branches])
    lse = jnp.stack([br[1] for br in branches])
    wts = jax.nn.softmax(lse, axis=0)
    out_a = jnp.einsum('rbsh,rbshe->bshe', wts, o_stack).reshape(B, S, D_A).astype(h.dtype)
    q = (_rmsnorm(cq, g_cq) @ w_uq).reshape(B, S, B_HEADS, B_NOPE_DIM + B_ROPE_DIM)
    q_nope, q_rope = q[..., :B_NOPE_DIM], _rope(q[..., B_NOPE_DIM:])
    kv = (_rmsnorm(ckv, g_ckv) @ w_ukv).reshape(B, S, B_HEADS, B_NOPE_DIM + B_V_DIM)
    k_nope, v = kv[..., :B_NOPE_DIM], kv[..., B_NOPE_DIM:]
    k_rope = _rope(kr[:, :, None, :])[:, :, 0]
    out_b = _mla_attention(q_nope, q_rope, k_nope, k_rope, v).reshape(B, S, D_B)
    y = jnp.concatenate([_rmsnorm(out_a, g_out_a), _rmsnorm(out_b, g_out_b)], axis=-1)
    return y @ w_out


def _swiglu(h, w_ffn_in, w_ffn_out):
    g, u = jnp.split(h @ w_ffn_in, 2, axis=-1)
    return (jax.nn.silu(g) * u) @ w_ffn_out


def _fwd_setup_inputs(seed: int = 0) -> dict:
    key = jax.random.key(seed)
    ks = jax.random.split(key, 20)
    f32 = jnp.float32

    def nrm(k, shape, scale):
        return jax.random.normal(k, shape, f32) * scale

    def gain(k, shape):
        return 1.0 + 0.05 * jax.random.normal(k, shape, f32)

    L = DEPTH
    return {
        'x': nrm(ks[0], (BATCH, SEQ, D_MODEL), 1.0),
        'c': nrm(ks[1], (BATCH, D_MODEL), 1.0),
        'w_ada': nrm(ks[2], (L, D_MODEL, N_MOD * D_MODEL), 0.5 * D_MODEL ** -0.5),
        'b_ada': nrm(ks[3], (L, N_MOD * D_MODEL), 0.01),
        'g_norm1': gain(ks[4], (L, D_MODEL)),
        'w_in': nrm(ks[5], (L, D_MODEL, P_IN), D_MODEL ** -0.5),
        'g_cq': gain(ks[6], (L, Q_LORA)),
        'w_uq': nrm(ks[7], (L, Q_LORA, B_HEADS * (B_NOPE_DIM + B_ROPE_DIM)), Q_LORA ** -0.5),
        'g_ckv': gain(ks[8], (L, KV_LORA)),
        'w_ukv': nrm(ks[9], (L, KV_LORA, B_HEADS * (B_NOPE_DIM + B_V_DIM)), KV_LORA ** -0.5),
        'rel_bias': nrm(ks[10], (N_BUCKETS, A_HEADS), 0.5),
        'g_out_a': gain(ks[11], (L, D_A)),
        'g_out_b': gain(ks[12], (L, D_B)),
        'w_out': nrm(ks[13], (L, D_MIX, D_MODEL), D_MIX ** -0.5),
        'g_norm2': gain(ks[14], (L, D_MODEL)),
        'w_ffn_in': nrm(ks[15], (L, D_MODEL, 2 * D_FF), D_MODEL ** -0.5),
        'w_ffn_out': nrm(ks[16], (L, D_FF, D_MODEL), D_FF ** -0.5),
        'g_final': gain(ks[17], (D_MODEL,)),
    }


def _fwd_reference(x, c, w_ada, b_ada, g_norm1, w_in, g_cq, w_uq, g_ckv, w_ukv, rel_bias,
              g_out_a, g_out_b, w_out, g_norm2, w_ffn_in, w_ffn_out, g_final):
    cond = jax.nn.silu(c)
    for l in range(DEPTH):
        mod = (cond @ w_ada[l] + b_ada[l])[:, None, :]
        sh1, sc1, g1, sh2, sc2, g2 = jnp.split(mod, N_MOD, axis=-1)
        h = _rmsnorm(x, g_norm1[l]) * (1.0 + sc1) + sh1
        x = x + g1 * _mixer(h, w_in[l], g_cq[l], w_uq[l], g_ckv[l], w_ukv[l], rel_bias,
                            g_out_a[l], g_out_b[l], w_out[l])
        h = _rmsnorm(x, g_norm2[l]) * (1.0 + sc2) + sh2
        x = x + g2 * _swiglu(h, w_ffn_in[l], w_ffn_out[l])
    return _rmsnorm(x, g_final)


import jax as _jax
import jax.numpy as _jnp

TWIN_FORMAT = 'train_step'
FWD_PARAMS = ['x', 'c', 'w_ada', 'b_ada', 'g_norm1', 'w_in', 'g_cq', 'w_uq', 'g_ckv', 'w_ukv', 'rel_bias', 'g_out_a', 'g_out_b', 'w_out', 'g_norm2', 'w_ffn_in', 'w_ffn_out', 'g_final']
TWIN_WEIGHTS = ['w_ada', 'b_ada', 'g_norm1', 'w_in', 'g_cq', 'w_uq', 'g_ckv', 'w_ukv', 'rel_bias', 'g_out_a', 'g_out_b', 'w_out', 'g_norm2', 'w_ffn_in', 'w_ffn_out', 'g_final']
TWIN_DIFF_INPUT = 'x'
TWIN_INPUTS = ['x', 'c', 'w_ada', 'b_ada', 'g_norm1', 'w_in', 'g_cq', 'w_uq', 'g_ckv', 'w_ukv', 'rel_bias', 'g_out_a', 'g_out_b', 'w_out', 'g_norm2', 'w_ffn_in', 'w_ffn_out', 'g_final', 'loss_target', 'm_w_ada', 'm_b_ada', 'm_g_norm1', 'm_w_in', 'm_g_cq', 'm_w_uq', 'm_g_ckv', 'm_w_ukv', 'm_rel_bias', 'm_g_out_a', 'm_g_out_b', 'm_w_out', 'm_g_norm2', 'm_w_ffn_in', 'm_w_ffn_out', 'm_g_final', 'v_w_ada', 'v_b_ada', 'v_g_norm1', 'v_w_in', 'v_g_cq', 'v_w_uq', 'v_g_ckv', 'v_w_ukv', 'v_rel_bias', 'v_g_out_a', 'v_g_out_b', 'v_w_out', 'v_g_norm2', 'v_w_ffn_in', 'v_w_ffn_out', 'v_g_final']
TWIN_OUTPUTS = ['loss', 'grad_x', 'grad_w_ada', 'grad_b_ada', 'grad_g_norm1', 'grad_w_in', 'grad_g_cq', 'grad_w_uq', 'grad_g_ckv', 'grad_w_ukv', 'grad_rel_bias', 'grad_g_out_a', 'grad_g_out_b', 'grad_w_out', 'grad_g_norm2', 'grad_w_ffn_in', 'grad_w_ffn_out', 'grad_g_final', 'delta_w_ada', 'delta_b_ada', 'delta_g_norm1', 'delta_w_in', 'delta_g_cq', 'delta_w_uq', 'delta_g_ckv', 'delta_w_ukv', 'delta_rel_bias', 'delta_g_out_a', 'delta_g_out_b', 'delta_w_out', 'delta_g_norm2', 'delta_w_ffn_in', 'delta_w_ffn_out', 'delta_g_final', 'new_m_w_ada', 'new_m_b_ada', 'new_m_g_norm1', 'new_m_w_in', 'new_m_g_cq', 'new_m_w_uq', 'new_m_g_ckv', 'new_m_w_ukv', 'new_m_rel_bias', 'new_m_g_out_a', 'new_m_g_out_b', 'new_m_w_out', 'new_m_g_norm2', 'new_m_w_ffn_in', 'new_m_w_ffn_out', 'new_m_g_final', 'new_v_w_ada', 'new_v_b_ada', 'new_v_g_norm1', 'new_v_w_in', 'new_v_g_cq', 'new_v_w_uq', 'new_v_g_ckv', 'new_v_w_ukv', 'new_v_rel_bias', 'new_v_g_out_a', 'new_v_g_out_b', 'new_v_w_out', 'new_v_g_norm2', 'new_v_w_ffn_in', 'new_v_w_ffn_out', 'new_v_g_final']
TWIN_LEAF_KINDS = {'loss': 'loss', 'grad_x': 'grad_x', 'grad_w_ada': 'grad_w', 'grad_b_ada': 'grad_w', 'grad_g_norm1': 'grad_w', 'grad_w_in': 'grad_w', 'grad_g_cq': 'grad_w', 'grad_w_uq': 'grad_w', 'grad_g_ckv': 'grad_w', 'grad_w_ukv': 'grad_w', 'grad_rel_bias': 'grad_w', 'grad_g_out_a': 'grad_w', 'grad_g_out_b': 'grad_w', 'grad_w_out': 'grad_w', 'grad_g_norm2': 'grad_w', 'grad_w_ffn_in': 'grad_w', 'grad_w_ffn_out': 'grad_w', 'grad_g_final': 'grad_w', 'delta_w_ada': 'delta_w', 'delta_b_ada': 'delta_w', 'delta_g_norm1': 'delta_w', 'delta_w_in': 'delta_w', 'delta_g_cq': 'delta_w', 'delta_w_uq': 'delta_w', 'delta_g_ckv': 'delta_w', 'delta_w_ukv': 'delta_w', 'delta_rel_bias': 'delta_w', 'delta_g_out_a': 'delta_w', 'delta_g_out_b': 'delta_w', 'delta_w_out': 'delta_w', 'delta_g_norm2': 'delta_w', 'delta_w_ffn_in': 'delta_w', 'delta_w_ffn_out': 'delta_w', 'delta_g_final': 'delta_w', 'new_m_w_ada': 'new_m', 'new_m_b_ada': 'new_m', 'new_m_g_norm1': 'new_m', 'new_m_w_in': 'new_m', 'new_m_g_cq': 'new_m', 'new_m_w_uq': 'new_m', 'new_m_g_ckv': 'new_m', 'new_m_w_ukv': 'new_m', 'new_m_rel_bias': 'new_m', 'new_m_g_out_a': 'new_m', 'new_m_g_out_b': 'new_m', 'new_m_w_out': 'new_m', 'new_m_g_norm2': 'new_m', 'new_m_w_ffn_in': 'new_m', 'new_m_w_ffn_out': 'new_m', 'new_m_g_final': 'new_m', 'new_v_w_ada': 'new_v', 'new_v_b_ada': 'new_v', 'new_v_g_norm1': 'new_v', 'new_v_w_in': 'new_v', 'new_v_g_cq': 'new_v', 'new_v_w_uq': 'new_v', 'new_v_g_ckv': 'new_v', 'new_v_w_ukv': 'new_v', 'new_v_rel_bias': 'new_v', 'new_v_g_out_a': 'new_v', 'new_v_g_out_b': 'new_v', 'new_v_w_out': 'new_v', 'new_v_g_norm2': 'new_v', 'new_v_w_ffn_in': 'new_v', 'new_v_w_ffn_out': 'new_v', 'new_v_g_final': 'new_v'}


def _forward(args):
    return _fwd_reference(*[args[k] for k in FWD_PARAMS])


def _output_shape():
    out = _jax.eval_shape(lambda: _forward(_fwd_setup_inputs(0)))
    return out.shape, out.dtype

N_MICROBATCH = 1
ADAM_LR = 0.001
ADAM_B1 = 0.9
ADAM_B2 = 0.999
ADAM_EPS = 1e-08
ADAM_WD = 0.01
ADAM_STEP = 10
PER_EXAMPLE_BATCH_AXIS = {'x': 0, 'c': 0, 'loss_target': 0}
SHARED_INPUTS = []
_WEIGHT_DTYPES = {'w_ada': _jnp.float32, 'b_ada': _jnp.float32, 'g_norm1': _jnp.float32, 'w_in': _jnp.float32, 'g_cq': _jnp.float32, 'w_uq': _jnp.float32, 'g_ckv': _jnp.float32, 'w_ukv': _jnp.float32, 'rel_bias': _jnp.float32, 'g_out_a': _jnp.float32, 'g_out_b': _jnp.float32, 'w_out': _jnp.float32, 'g_norm2': _jnp.float32, 'w_ffn_in': _jnp.float32, 'w_ffn_out': _jnp.float32, 'g_final': _jnp.float32}
MOMENT_SCALE = {'w_ada': 1.754174e-01, 'b_ada': 2.960241e-01, 'g_norm1': 5.022492e-02, 'w_in': 8.354572e-02, 'g_cq': 2.912596e-02, 'w_uq': 2.077624e-02, 'g_ckv': 1.733903e-01, 'w_ukv': 9.905460e-02, 'rel_bias': 3.694429e-02, 'g_out_a': 1.304238e-01, 'g_out_b': 1.568939e-01, 'w_out': 1.357775e-01, 'g_norm2': 5.275337e-02, 'w_ffn_in': 2.311433e-02, 'w_ffn_out': 3.789022e-02, 'g_final': 3.235063e+01}


def _to_microbatches(a, axis):
    t = _jnp.moveaxis(a, axis, 0)
    t = t.reshape((N_MICROBATCH, t.shape[0] // N_MICROBATCH) + t.shape[1:])
    return _jnp.moveaxis(t, 1, axis + 1)


def setup_inputs(seed: int = 0) -> dict:
    inp = _fwd_setup_inputs(seed)
    key = _jax.random.fold_in(_jax.random.key(seed), 7919)
    shape, _ = _output_shape()
    out = dict(inp)
    out["loss_target"] = _jax.random.normal(_jax.random.fold_in(key, 0), shape, _jnp.float32)
    for i, name in enumerate(TWIN_WEIGHTS):
        w = inp[name].astype(_jnp.float32)
        if MOMENT_SCALE is None:
            s = _jnp.sqrt(_jnp.mean(_jnp.square(w)) + 1e-30)
        else:
            s = MOMENT_SCALE[name]
        km, kv = _jax.random.split(_jax.random.fold_in(key, i + 1))
        out[name] = w
        out["m_" + name] = s * _jax.random.normal(km, w.shape, _jnp.float32)
        out["v_" + name] = (s * s) * _jax.random.uniform(kv, w.shape, _jnp.float32, 0.5, 1.5)
    if N_MICROBATCH > 1:
        for name, axis in PER_EXAMPLE_BATCH_AXIS.items():
            out[name] = _to_microbatches(out[name], axis)
    return {'x': out['x'], 'c': out['c'], 'w_ada': out['w_ada'], 'b_ada': out['b_ada'], 'g_norm1': out['g_norm1'], 'w_in': out['w_in'], 'g_cq': out['g_cq'], 'w_uq': out['w_uq'], 'g_ckv': out['g_ckv'], 'w_ukv': out['w_ukv'], 'rel_bias': out['rel_bias'], 'g_out_a': out['g_out_a'], 'g_out_b': out['g_out_b'], 'w_out': out['w_out'], 'g_norm2': out['g_norm2'], 'w_ffn_in': out['w_ffn_in'], 'w_ffn_out': out['w_ffn_out'], 'g_final': out['g_final'], 'loss_target': out['loss_target'], 'm_w_ada': out['m_w_ada'], 'm_b_ada': out['m_b_ada'], 'm_g_norm1': out['m_g_norm1'], 'm_w_in': out['m_w_in'], 'm_g_cq': out['m_g_cq'], 'm_w_uq': out['m_w_uq'], 'm_g_ckv': out['m_g_ckv'], 'm_w_ukv': out['m_w_ukv'], 'm_rel_bias': out['m_rel_bias'], 'm_g_out_a': out['m_g_out_a'], 'm_g_out_b': out['m_g_out_b'], 'm_w_out': out['m_w_out'], 'm_g_norm2': out['m_g_norm2'], 'm_w_ffn_in': out['m_w_ffn_in'], 'm_w_ffn_out': out['m_w_ffn_out'], 'm_g_final': out['m_g_final'], 'v_w_ada': out['v_w_ada'], 'v_b_ada': out['v_b_ada'], 'v_g_norm1': out['v_g_norm1'], 'v_w_in': out['v_w_in'], 'v_g_cq': out['v_g_cq'], 'v_w_uq': out['v_w_uq'], 'v_g_ckv': out['v_g_ckv'], 'v_w_ukv': out['v_w_ukv'], 'v_rel_bias': out['v_rel_bias'], 'v_g_out_a': out['v_g_out_a'], 'v_g_out_b': out['v_g_out_b'], 'v_w_out': out['v_w_out'], 'v_g_norm2': out['v_g_norm2'], 'v_w_ffn_in': out['v_w_ffn_in'], 'v_w_ffn_out': out['v_w_ffn_out'], 'v_g_final': out['v_g_final']}


def _loss(weights, diff, rest, loss_target):
    with _jax.named_scope("forward"):
        args = {**rest, TWIN_DIFF_INPUT: diff, **{k: w.astype(_WEIGHT_DTYPES[k]) for k, w in weights.items()}}
        y = _forward(args)
    with _jax.named_scope("loss_head"):
        err = _jnp.square(y.astype(_jnp.float32) - loss_target)
        return 0.5 * _jnp.sum(_jnp.mean(err, axis=-1)) if err.ndim else 0.5 * err


def _adamw(w, g, m, v):
    m = ADAM_B1 * m + (1.0 - ADAM_B1) * g
    v = ADAM_B2 * v + (1.0 - ADAM_B2) * _jnp.square(g)
    m_hat = m / (1.0 - ADAM_B1 ** ADAM_STEP)
    v_hat = v / (1.0 - ADAM_B2 ** ADAM_STEP)
    delta = -ADAM_LR * (m_hat / (_jnp.sqrt(v_hat) + ADAM_EPS) + ADAM_WD * w)
    return delta, m, v


def reference(x, c, w_ada, b_ada, g_norm1, w_in, g_cq, w_uq, g_ckv, w_ukv, rel_bias, g_out_a, g_out_b, w_out, g_norm2, w_ffn_in, w_ffn_out, g_final, loss_target, m_w_ada, m_b_ada, m_g_norm1, m_w_in, m_g_cq, m_w_uq, m_g_ckv, m_w_ukv, m_rel_bias, m_g_out_a, m_g_out_b, m_w_out, m_g_norm2, m_w_ffn_in, m_w_ffn_out, m_g_final, v_w_ada, v_b_ada, v_g_norm1, v_w_in, v_g_cq, v_w_uq, v_g_ckv, v_w_ukv, v_rel_bias, v_g_out_a, v_g_out_b, v_w_out, v_g_norm2, v_w_ffn_in, v_w_ffn_out, v_g_final):
    given = dict(x=x, c=c, w_ada=w_ada, b_ada=b_ada, g_norm1=g_norm1, w_in=w_in, g_cq=g_cq, w_uq=w_uq, g_ckv=g_ckv, w_ukv=w_ukv, rel_bias=rel_bias, g_out_a=g_out_a, g_out_b=g_out_b, w_out=w_out, g_norm2=g_norm2, w_ffn_in=w_ffn_in, w_ffn_out=w_ffn_out, g_final=g_final, loss_target=loss_target, m_w_ada=m_w_ada, m_b_ada=m_b_ada, m_g_norm1=m_g_norm1, m_w_in=m_w_in, m_g_cq=m_g_cq, m_w_uq=m_w_uq, m_g_ckv=m_g_ckv, m_w_ukv=m_w_ukv, m_rel_bias=m_rel_bias, m_g_out_a=m_g_out_a, m_g_out_b=m_g_out_b, m_w_out=m_w_out, m_g_norm2=m_g_norm2, m_w_ffn_in=m_w_ffn_in, m_w_ffn_out=m_w_ffn_out, m_g_final=m_g_final, v_w_ada=v_w_ada, v_b_ada=v_b_ada, v_g_norm1=v_g_norm1, v_w_in=v_w_in, v_g_cq=v_g_cq, v_w_uq=v_w_uq, v_g_ckv=v_g_ckv, v_w_ukv=v_w_ukv, v_rel_bias=v_rel_bias, v_g_out_a=v_g_out_a, v_g_out_b=v_g_out_b, v_w_out=v_w_out, v_g_norm2=v_g_norm2, v_w_ffn_in=v_w_ffn_in, v_w_ffn_out=v_w_ffn_out, v_g_final=v_g_final)
    weights = {n: given[n] for n in TWIN_WEIGHTS}
    shared = {n: given[n] for n in SHARED_INPUTS}
    per_example = {n: given[n] for n in ['x', 'c']}
    grad_fn = _jax.value_and_grad(_loss, argnums=(0, 1))

    def one_microbatch(ex, loss_target):
        ex = dict(ex)
        diff = ex.pop(TWIN_DIFF_INPUT)
        return grad_fn(weights, diff, {**shared, **ex}, loss_target)

    if N_MICROBATCH == 1:
        loss, (grad_w, grad_x) = one_microbatch(per_example, given["loss_target"])
    else:
        def body(carry, xs):
            loss_sum, grad_sum = carry
            l_k, (gw_k, gx_k) = one_microbatch(xs[0], xs[1])
            with _jax.named_scope("update"):
                return (loss_sum + l_k, _jax.tree.map(_jnp.add, grad_sum, gw_k)), gx_k

        init = (_jnp.zeros((), _jnp.float32), _jax.tree.map(_jnp.zeros_like, weights))
        (loss, grad_w), grad_x = _jax.lax.scan(body, init, (per_example, given["loss_target"]))
    with _jax.named_scope("update"):
        delta_w, new_m, new_v = {}, {}, {}
        for n in TWIN_WEIGHTS:
            delta_w[n], new_m[n], new_v[n] = _adamw(weights[n], grad_w[n], given["m_" + n], given["v_" + n])
    return (loss, grad_x, *[grad_w[n] for n in TWIN_WEIGHTS], *[delta_w[n] for n in TWIN_WEIGHTS],
            *[new_m[n] for n in TWIN_WEIGHTS], *[new_v[n] for n in TWIN_WEIGHTS])
```

```python
import functools
import math

import numpy as np
import jax
import jax.numpy as jnp
from jax import lax
from jax.experimental import pallas as pl
from jax.experimental.pallas import tpu as pltpu

F32 = jnp.float32
BF16 = jnp.bfloat16

D_MODEL = 1024
SEQ = 2048
N_HEADS = 8
HEAD_DIM = 64
D_A = 512
D_B = 512
Q_LORA = 384
KV_LORA = 256
ROPE_DIM = 32
NOPE_DIM = 64
D_FF = 2816
N_MOD = 6
N_BUCKETS = 32
MAX_DISTANCE = 2048
ROPE_THETA = 10000.0
EPS = 1e-6
NEG = -1e30
BLK = 128
DILATIONS = (1, 4, 16)
SPAN = 128
MLA_SCALE = (NOPE_DIM + ROPE_DIM) ** -0.5
DIL_SCALE = HEAD_DIM ** -0.5

ADAM_LR = 0.001
ADAM_B1 = 0.9
ADAM_B2 = 0.999
ADAM_EPS = 1e-08
ADAM_WD = 0.01
ADAM_STEP = 10

N_DEV = 8
N_CHIP = 4
LANES = 128
VMEM_LIMIT = 48 * 1024 * 1024

P_QKV = 3 * D_A
P_REST = KV_LORA + LANES + Q_LORA
PACK_ROWS = 3072


def _cparams(sem=None):
    return pltpu.CompilerParams(dimension_semantics=sem, vmem_limit_bytes=VMEM_LIMIT)


def _pick(n, cands):
    for c in cands:
        if n % c == 0:
            return c
    raise ValueError(f"no tile for {n} in {cands}")


def _mm(a, b, mode, out_dtype, name):
    if mode == "nn":
        (M, K), (K2, N) = a.shape, b.shape
    elif mode == "nt":
        (M, K), (N, K2) = a.shape, b.shape
    else:
        (K, M), (K2, N) = a.shape, b.shape
    assert K == K2, (a.shape, b.shape, mode)
    tm = _pick(M, (512, 384, 256, 128))
    tn = _pick(N, (1408, 1024, 768, 512, 384, 256, 128))
    tk = _pick(K, (1024, 512, 384, 256, 128))
    nk = K // tk
    if mode == "nn":
        a_spec = pl.BlockSpec((tm, tk), lambda i, j, k: (i, k))
        b_spec = pl.BlockSpec((tk, tn), lambda i, j, k: (k, j))
        dn = (((1,), (0,)), ((), ()))
    elif mode == "nt":
        a_spec = pl.BlockSpec((tm, tk), lambda i, j, k: (i, k))
        b_spec = pl.BlockSpec((tn, tk), lambda i, j, k: (j, k))
        dn = (((1,), (1,)), ((), ()))
    else:
        a_spec = pl.BlockSpec((tk, tm), lambda i, j, k: (k, i))
        b_spec = pl.BlockSpec((tk, tn), lambda i, j, k: (k, j))
        dn = (((0,), (0,)), ((), ()))

    def body(a_ref, b_ref, o_ref, acc_ref):
        k = pl.program_id(2)

        @pl.when(k == 0)
        def _():
            acc_ref[...] = jnp.zeros_like(acc_ref)

        acc_ref[...] += lax.dot_general(a_ref[...].astype(BF16), b_ref[...].astype(BF16), dn,
                                        preferred_element_type=F32)

        @pl.when(k == nk - 1)
        def _():
            o_ref[...] = acc_ref[...].astype(o_ref.dtype)

    return pl.pallas_call(
        body, name=name,
        out_shape=jax.ShapeDtypeStruct((M, N), out_dtype),
        grid=(M // tm, N // tn, nk),
        in_specs=[a_spec, b_spec],
        out_specs=pl.BlockSpec((tm, tn), lambda i, j, k: (i, j)),
        scratch_shapes=[pltpu.VMEM((tm, tn), F32)],
        compiler_params=_cparams(("parallel", "parallel", "arbitrary")),
    )(a, b)


ROW_TILE = 256


def _adaln_fwd(x, g, sc, sh, name, mix=None, gate=None):
    B, S, D = x.shape
    ts = ROW_TILE
    has_res = mix is not None

    def body(*refs):
        if has_res:
            x_ref, g_ref, sc_ref, sh_ref, mix_ref, gate_ref, h_ref, xr_ref = refs
            xr = x_ref[0] + gate_ref[0] * mix_ref[0]
            xr_ref[0] = xr
        else:
            x_ref, g_ref, sc_ref, sh_ref, h_ref = refs
            xr = x_ref[0]
        r = lax.rsqrt(jnp.mean(xr * xr, axis=-1, keepdims=True) + EPS)
        xn = (xr * r) * g_ref[...]
        h_ref[0] = (xn * (1.0 + sc_ref[0]) + sh_ref[0]).astype(h_ref.dtype)

    tok = pl.BlockSpec((1, ts, D), lambda b, s: (b, s, 0))
    per_b = pl.BlockSpec((1, 1, D), lambda b, s: (b, 0, 0))
    vec = pl.BlockSpec((1, D), lambda b, s: (0, 0))
    in_specs = [tok, vec, per_b, per_b]
    args = [x, g, sc, sh]
    out_shape = [jax.ShapeDtypeStruct((B, S, D), BF16)]
    out_specs = [tok]
    if has_res:
        in_specs += [tok, per_b]
        args += [mix, gate]
        out_shape.append(jax.ShapeDtypeStruct((B, S, D), F32))
        out_specs.append(tok)
    out = pl.pallas_call(
        body, name=name, out_shape=out_shape, grid=(B, S // ts),
        in_specs=in_specs, out_specs=out_specs,
        compiler_params=_cparams(("parallel", "parallel")),
    )(*args)
    return out if has_res else out[0]


def _adaln_bwd(dh, x, g, sc, dres, name, mix=None, gate=None):
    B, S, D = x.shape
    ts = ROW_TILE
    has_res = mix is not None

    def body(*refs):
        if has_res:
            (dh_ref, x_ref, g_ref, sc_ref, dres_ref, mix_ref, gate_ref,
             dx_ref, dsh_ref, dsc_ref, dg_ref, dgate_ref, dmix_ref) = refs
        else:
            (dh_ref, x_ref, g_ref, sc_ref, dres_ref, dx_ref, dsh_ref, dsc_ref, dg_ref) = refs
        b, s = pl.program_id(0), pl.program_id(1)
        xv = x_ref[0]
        dhv = dh_ref[0]
        gv = g_ref[...]
        r = lax.rsqrt(jnp.mean(xv * xv, axis=-1, keepdims=True) + EPS)
        n = xv * r
        xn = n * gv
        dxn = dhv * (1.0 + sc_ref[0])
        dn = dxn * gv
        dx = r * (dn - n * jnp.mean(dn * n, axis=-1, keepdims=True)) + dres_ref[0]
        dx_ref[0] = dx

        @pl.when(s == 0)
        def _():
            dsh_ref[...] = jnp.zeros_like(dsh_ref)
            dsc_ref[...] = jnp.zeros_like(dsc_ref)
            if has_res:
                dgate_ref[...] = jnp.zeros_like(dgate_ref)

        @pl.when((s == 0) & (b == 0))
        def _():
            dg_ref[...] = jnp.zeros_like(dg_ref)

        dsh_ref[0] += jnp.sum(dhv, axis=0, keepdims=True)
        dsc_ref[0] += jnp.sum(dhv * xn, axis=0, keepdims=True)
        dg_ref[...] += jnp.sum(dxn * n, axis=0, keepdims=True)
        if has_res:
            dgate_ref[0] += jnp.sum(dx * mix_ref[0], axis=0, keepdims=True)
            dmix_ref[0] = (dx * gate_ref[0]).astype(dmix_ref.dtype)

    tok = pl.BlockSpec((1, ts, D), lambda b, s: (b, s, 0))
    per_b = pl.BlockSpec((1, 1, D), lambda b, s: (b, 0, 0))
    vec = pl.BlockSpec((1, D), lambda b, s: (0, 0))
    in_specs = [tok, tok, vec, per_b, tok]
    args = [dh, x, g, sc, dres]
    out_shape = [jax.ShapeDtypeStruct((B, S, D), F32), jax.ShapeDtypeStruct((B, 1, D), F32),
                 jax.ShapeDtypeStruct((B, 1, D), F32), jax.ShapeDtypeStruct((1, D), F32)]
    out_specs = [tok, per_b, per_b, vec]
    if has_res:
        in_specs += [tok, per_b]
        args += [mix, gate]
        out_shape += [jax.ShapeDtypeStruct((B, 1, D), F32), jax.ShapeDtypeStruct((B, S, D), BF16)]
        out_specs += [per_b, tok]
    return pl.pallas_call(
        body, name=name, out_shape=out_shape, grid=(B, S // ts),
        in_specs=in_specs, out_specs=out_specs,
        compiler_params=_cparams(("arbitrary", "arbitrary")),
    )(*args)


def _rms_fwd(x, col_blk, n, g, name, n_real=None):
    T = x.shape[0]
    tr = 512
    nr = float(n_real or n)

    def body(x_ref, g_ref, y_ref):
        xv = x_ref[...]
        r = lax.rsqrt(jnp.sum(xv * xv, axis=-1, keepdims=True) / nr + EPS)
        y_ref[...] = ((xv * r) * g_ref[...]).astype(y_ref.dtype)

    return pl.pallas_call(
        body, name=name, out_shape=jax.ShapeDtypeStruct((T, n), BF16), grid=(T // tr,),
        in_specs=[pl.BlockSpec((tr, n), lambda i: (i, col_blk)), pl.BlockSpec((1, n), lambda i: (0, 0))],
        out_specs=pl.BlockSpec((tr, n), lambda i: (i, 0)),
        compiler_params=_cparams(("parallel",)),
    )(x, g)


def _rms_bwd(dy, dy_blk, x, x_blk, n, g, name, out_dtype=BF16):
    T = x.shape[0]
    tr = 512

    def body(dy_ref, x_ref, g_ref, dx_ref, dg_ref):
        xv = x_ref[...]
        dyv = dy_ref[...].astype(F32)
        r = lax.rsqrt(jnp.mean(xv * xv, axis=-1, keepdims=True) + EPS)
        nrm = xv * r
        dn = dyv * g_ref[...]
        dx_ref[...] = (r * (dn - nrm * jnp.mean(dn * nrm, axis=-1, keepdims=True))).astype(dx_ref.dtype)

        @pl.when(pl.program_id(0) == 0)
        def _():
            dg_ref[...] = jnp.zeros_like(dg_ref)

        dg_ref[...] += jnp.sum(dyv * nrm, axis=0, keepdims=True)

    return pl.pallas_call(
        body, name=name,
        out_shape=[jax.ShapeDtypeStruct((T, n), out_dtype), jax.ShapeDtypeStruct((1, n), F32)],
        grid=(T // tr,),
        in_specs=[pl.BlockSpec((tr, n), lambda i: (i, dy_blk)), pl.BlockSpec((tr, n), lambda i: (i, x_blk)),
                  pl.BlockSpec((1, n), lambda i: (0, 0))],
        out_specs=[pl.BlockSpec((tr, n), lambda i: (i, 0)), pl.BlockSpec((1, n), lambda i: (0, 0))],
        compiler_params=_cparams(("arbitrary",)),
    )(dy, x, g)


def _swiglu_fwd(gu, name):
    T = gu.shape[0]
    tr, tc = 512, 1408
    nc = D_FF // tc

    def body(g_ref, u_ref, a_ref):
        gv = g_ref[...]
        a_ref[...] = (gv * jax.nn.sigmoid(gv) * u_ref[...]).astype(a_ref.dtype)

    return pl.pallas_call(
        body, name=name, out_shape=jax.ShapeDtypeStruct((T, D_FF), BF16), grid=(T // tr, nc),
        in_specs=[pl.BlockSpec((tr, tc), lambda i, j: (i, j)), pl.BlockSpec((tr, tc), lambda i, j: (i, j + nc))],
        out_specs=pl.BlockSpec((tr, tc), lambda i, j: (i, j)),
        compiler_params=_cparams(("parallel", "parallel")),
    )(gu, gu)


def _swiglu_bwd(da, gu, name):
    T = gu.shape[0]
    tr, tc = 512, 1408
    nc = D_FF // tc

    def body(da_ref, g_ref, u_ref, dgu_ref):
        j = pl.program_id(1)
        gv, uv, dav = g_ref[...], u_ref[...], da_ref[...]
        sg = jax.nn.sigmoid(gv)

        @pl.when(j < nc)
        def _():
            dgu_ref[...] = (dav * uv * (sg * (1.0 + gv * (1.0 - sg)))).astype(dgu_ref.dtype)

        @pl.when(j >= nc)
        def _():
            dgu_ref[...] = (dav * (gv * sg)).astype(dgu_ref.dtype)

    return pl.pallas_call(
        body, name=name, out_shape=jax.ShapeDtypeStruct((T, 2 * D_FF), BF16), grid=(T // tr, 2 * nc),
        in_specs=[pl.BlockSpec((tr, tc), lambda i, j: (i, j % nc)),
                  pl.BlockSpec((tr, tc), lambda i, j: (i, j % nc)),
                  pl.BlockSpec((tr, tc), lambda i, j: (i, j % nc + nc))],
        out_specs=pl.BlockSpec((tr, tc), lambda i, j: (i, j)),
        compiler_params=_cparams(("parallel", "parallel")),
    )(da, gu, gu)


def _final_loss(x1, f, g2, gf, target, name):
    B, S, D = x1.shape
    ts = ROW_TILE

    def body(x1_ref, f_ref, g2_ref, gf_ref, t_ref, dx_ref, df_ref, dg2_ref, dgf_ref, loss_ref):
        b, s = pl.program_id(0), pl.program_id(1)
        fv = f_ref[0]
        g2v = g2_ref[0]
        gfv = gf_ref[...]
        x2 = x1_ref[0] + g2v * fv
        r = lax.rsqrt(jnp.mean(x2 * x2, axis=-1, keepdims=True) + EPS)
        n = x2 * r
        e = n * gfv - t_ref[0]
        dy = e * (1.0 / D)
        dn = dy * gfv
        dx = r * (dn - n * jnp.mean(dn * n, axis=-1, keepdims=True))
        dx_ref[0] = dx
        df_ref[0] = (dx * g2v).astype(df_ref.dtype)

        @pl.when(s == 0)
        def _():
            dg2_ref[...] = jnp.zeros_like(dg2_ref)

        @pl.when((s == 0) & (b == 0))
        def _():
            dgf_ref[...] = jnp.zeros_like(dgf_ref)
            loss_ref[...] = jnp.zeros_like(loss_ref)

        dg2_ref[0] += jnp.sum(dx * fv, axis=0, keepdims=True)
        dgf_ref[...] += jnp.sum(dy * n, axis=0, keepdims=True)
        loss_ref[...] += 0.5 * jnp.sum(jnp.mean(e * e, axis=-1, keepdims=True), axis=0, keepdims=True)

    tok = pl.BlockSpec((1, ts, D), lambda b, s: (b, s, 0))
    per_b = pl.BlockSpec((1, 1, D), lambda b, s: (b, 0, 0))
    vec = pl.BlockSpec((1, D), lambda b, s: (0, 0))
    return pl.pallas_call(
        body, name=name,
        out_shape=[jax.ShapeDtypeStruct((B, S, D), F32), jax.ShapeDtypeStruct((B, S, D), BF16),
                   jax.ShapeDtypeStruct((B, 1, D), F32), jax.ShapeDtypeStruct((1, D), F32),
                   jax.ShapeDtypeStruct((1, LANES), F32)],
        grid=(B, S // ts),
        in_specs=[tok, tok, per_b, vec, tok],
        out_specs=[tok, tok, per_b, vec, pl.BlockSpec((1, LANES), lambda b, s: (0, 0))],
        compiler_params=_cparams(("arbitrary", "arbitrary")),
    )(x1, f, g2, gf, target)


def _rope_tables():
    half = ROPE_DIM // 2
    inv = ROPE_THETA ** (-jnp.arange(half, dtype=F32) / half)
    ang = jnp.arange(SEQ, dtype=F32)[:, None] * inv[None, :]
    cos, sin = jnp.cos(ang), jnp.sin(ang)
    one = jnp.ones((SEQ, NOPE_DIM), F32)
    zero = jnp.zeros((SEQ, NOPE_DIM), F32)
    cs = jnp.concatenate([one, cos, cos, one[:, :LANES - NOPE_DIM - ROPE_DIM]], axis=1)
    sn = jnp.concatenate([zero, -sin, sin, zero[:, :LANES - NOPE_DIM - ROPE_DIM]], axis=1)
    return cs, sn


def _rope_group(t, cs, sn):
    half = ROPE_DIM // 2
    lane = lax.broadcasted_iota(jnp.int32, t.shape, 1)
    partner = jnp.where(lane < NOPE_DIM + half, pltpu.roll(t, LANES - half, 1), pltpu.roll(t, half, 1))
    return t * cs + partner * sn


def _rope_apply(t, cs, sn, out_dtype, name, add=None, add_blk=0):
    B, S, W = t.shape
    G = W // LANES
    ts = ROW_TILE

    def body(*refs):
        if add is None:
            t_ref, cs_ref, sn_ref, o_ref = refs
            for gi in range(G):
                sl = slice(gi * LANES, (gi + 1) * LANES)
                o_ref[0, :, sl] = _rope_group(t_ref[0, :, sl], cs_ref[...], sn_ref[...]).astype(o_ref.dtype)
        else:
            t_ref, a_ref, cs_ref, sn_ref, o_ref = refs
            ra = _rope_group(a_ref[0], cs_ref[...], sn_ref[...])
            for gi in range(G):
                sl = slice(gi * LANES, (gi + 1) * LANES)
                o_ref[0, :, sl] = (t_ref[0, :, sl] + ra).astype(o_ref.dtype)

    tok = pl.BlockSpec((1, ts, W), lambda b, s: (b, s, 0))
    tab = pl.BlockSpec((ts, LANES), lambda b, s: (s, 0))
    in_specs, args = [tok], [t]
    if add is not None:
        in_specs.append(pl.BlockSpec((1, ts, LANES), lambda b, s: (b, s, add_blk)))
        args.append(add)
    in_specs += [tab, tab]
    args += [cs, sn]
    return pl.pallas_call(
        body, name=name, out_shape=jax.ShapeDtypeStruct((B, S, W), out_dtype), grid=(B, S // ts),
        in_specs=in_specs, out_specs=tok, compiler_params=_cparams(("parallel", "parallel")),
    )(*args)


def _krope_bwd(dkc, cs, sn_neg, name):
    B, S, W = dkc.shape
    G = W // LANES
    ts = ROW_TILE

    def body(d_ref, cs_ref, sn_ref, o_ref):
        acc = d_ref[0, :, 0:LANES]
        for gi in range(1, G):
            acc = acc + d_ref[0, :, gi * LANES:(gi + 1) * LANES]
        lane = lax.broadcasted_iota(jnp.int32, acc.shape, 1)
        rot = (lane >= NOPE_DIM) & (lane < NOPE_DIM + ROPE_DIM)
        acc = jnp.where(rot, acc, 0.0)
        o_ref[0] = _rope_group(acc, cs_ref[...], sn_ref[...]).astype(o_ref.dtype)

    tab = pl.BlockSpec((ts, LANES), lambda b, s: (s, 0))
    return pl.pallas_call(
        body, name=name, out_shape=jax.ShapeDtypeStruct((B, S, LANES), BF16), grid=(B, S // ts),
        in_specs=[pl.BlockSpec((1, ts, W), lambda b, s: (b, s, 0)), tab, tab],
        out_specs=pl.BlockSpec((1, ts, LANES), lambda b, s: (b, s, 0)),
        compiler_params=_cparams(("parallel", "parallel")),
    )(dkc, cs, sn_neg)


def _t5_bucket(dist):
    max_exact = N_BUCKETS // 2
    d = np.maximum(dist, 1).astype(np.float64)
    large = max_exact + (np.log(d / max_exact) / np.log(MAX_DISTANCE / max_exact)
                         * (N_BUCKETS - max_exact)).astype(np.int64)
    large = np.minimum(large, N_BUCKETS - 1)
    return np.where(dist < max_exact, dist, large).astype(np.int32)


def _band_buckets(dilation):
    a = np.arange(BLK)[:, None]
    bk = np.arange(2 * BLK)[None, :]
    steps = BLK + a - bk
    return _t5_bucket(np.clip(steps, 0, SPAN) * dilation)


def _head_mask(shape, hh):
    lane = lax.broadcasted_iota(jnp.int32, shape, 1)
    return (lane >= hh * HEAD_DIM) & (lane < (hh + 1) * HEAD_DIM)


def _dot_nt(a, b):
    return lax.dot_general(a, b, (((1,), (1,)), ((), ())), preferred_element_type=F32)


def _dot_tn(a, b):
    return lax.dot_general(a, b, (((0,), (0,)), ((), ())), preferred_element_type=F32)


def _dot_nn(a, b):
    return lax.dot_general(a, b, (((1,), (0,)), ((), ())), preferred_element_type=F32)


def _band_valid():
    a = lax.broadcasted_iota(jnp.int32, (BLK, BLK), 0)
    bk = lax.broadcasted_iota(jnp.int32, (BLK, BLK), 1)
    return bk >= a, bk <= a


def _dil_fwd(qkv, bias, dilation, name):
    B, S, _ = qkv.shape
    d = dilation
    n = S // d
    nb = n // BLK
    qkv_v = qkv.reshape(B, n, d * P_QKV)
    npair = N_HEADS // 2

    def body(cur_ref, prev_ref, bias_ref, o_ref, lse_ref):
        i = pl.program_id(2)
        vprev, vcur = _band_valid()
        vprev = vprev & (i > 0)
        for p in range(npair):
            sl = slice(p * LANES, (p + 1) * LANES)
            q = cur_ref[0, :, sl]
            kc = cur_ref[0, :, D_A + p * LANES:D_A + (p + 1) * LANES]
            vc = cur_ref[0, :, 2 * D_A + p * LANES:2 * D_A + (p + 1) * LANES]
            kp = prev_ref[0, :, D_A + p * LANES:D_A + (p + 1) * LANES]
            vp = prev_ref[0, :, 2 * D_A + p * LANES:2 * D_A + (p + 1) * LANES]
            o_pair = jnp.zeros((BLK, LANES), F32)
            lse_pair = jnp.zeros((BLK, LANES), F32)
            for hh in range(2):
                h = 2 * p + hh
                hm = _head_mask((BLK, LANES), hh)
                qm = jnp.where(hm, q, jnp.zeros_like(q))
                s_p = _dot_nt(qm, kp) * DIL_SCALE + bias_ref[h, :, 0:BLK]
                s_c = _dot_nt(qm, kc) * DIL_SCALE + bias_ref[h, :, BLK:2 * BLK]
                s_p = jnp.where(vprev, s_p, NEG)
                s_c = jnp.where(vcur, s_c, NEG)
                m = jnp.maximum(jnp.max(s_p, axis=-1, keepdims=True), jnp.max(s_c, axis=-1, keepdims=True))
                e_p = jnp.exp(s_p - m)
                e_c = jnp.exp(s_c - m)
                l = jnp.sum(e_p, axis=-1, keepdims=True) + jnp.sum(e_c, axis=-1, keepdims=True)
                vpm = jnp.where(hm, vp, jnp.zeros_like(vp))
                vcm = jnp.where(hm, vc, jnp.zeros_like(vc))
                o_h = _dot_nn(e_p.astype(BF16), vpm) + _dot_nn(e_c.astype(BF16), vcm)
                o_pair = o_pair + o_h / l
                lse_pair = jnp.where(hm, m + jnp.log(l), lse_pair)
            o_ref[0, :, sl] = o_pair
            lse_ref[0, :, sl] = lse_pair

    cur = pl.BlockSpec((1, BLK, P_QKV), lambda b, r, i: (b, i, r))
    prev = pl.BlockSpec((1, BLK, P_QKV), lambda b, r, i: (b, jnp.maximum(i - 1, 0), r))
    out = pl.BlockSpec((1, BLK, D_A), lambda b, r, i: (b, i, r))
    o, lse = pl.pallas_call(
        body, name=name,
        out_shape=[jax.ShapeDtypeStruct((B, n, d * D_A), F32)] * 2,
        grid=(B, d, nb),
        in_specs=[cur, prev, pl.BlockSpec((N_HEADS, BLK, 2 * BLK), lambda b, r, i: (0, 0, 0))],
        out_specs=[out, out],
        compiler_params=_cparams(("parallel", "parallel", "arbitrary")),
    )(qkv_v, qkv_v, bias)
    return o.reshape(B, S, D_A), lse.reshape(B, S, D_A)


def _dil_merge(os_, lses, name):
    B, S, W = os_[0].shape
    ts = 512

    def body(o0, o1, o2, l0, l1, l2, out_ref, L_ref):
        a0, a1, a2 = l0[0], l1[0], l2[0]
        m = jnp.maximum(jnp.maximum(a0, a1), a2)
        e0, e1, e2 = jnp.exp(a0 - m), jnp.exp(a1 - m), jnp.exp(a2 - m)
        ssum = e0 + e1 + e2
        out_ref[0] = (e0 * o0[0] + e1 * o1[0] + e2 * o2[0]) / ssum
        L_ref[0] = m + jnp.log(ssum)

    tok = pl.BlockSpec((1, ts, W), lambda b, s: (b, s, 0))
    return pl.pallas_call(
        body, name=name, out_shape=[jax.ShapeDtypeStruct((B, S, W), F32)] * 2, grid=(B, S // ts),
        in_specs=[tok] * 6, out_specs=[tok, tok], compiler_params=_cparams(("parallel", "parallel")),
    )(*os_, *lses)


def _dil_bwd(qkv, do, out_a, L, bias, dilation, name):
    B, S, _ = qkv.shape
    d = dilation
    n = S // d
    nb = n // BLK
    qkv_v = qkv.reshape(B, n, d * P_QKV)
    do_v = do.reshape(B, n, d * D_A)
    oa_v = out_a.reshape(B, n, d * D_A)
    L_v = L.reshape(B, n, d * D_A)
    npair = N_HEADS // 2
    multi = nb > 1

    def body(*refs):
        if multi:
            (cur_ref, prev_ref, next_ref, do_ref, don_ref, oa_ref, oan_ref, L_ref, Ln_ref, bias_ref,
             dqkv_ref, dbias_ref) = refs
        else:
            cur_ref, do_ref, oa_ref, L_ref, bias_ref, dqkv_ref, dbias_ref = refs
        b, r, i = pl.program_id(0), pl.program_id(1), pl.program_id(2)

        @pl.when((b == 0) & (r == 0) & (i == 0))
        def _():
            dbias_ref[...] = jnp.zeros_like(dbias_ref)

        vprev, vcur = _band_valid()
        has_prev = i > 0
        has_next = i < nb - 1
        for p in range(npair):
            sl = slice(p * LANES, (p + 1) * LANES)
            ksl = slice(D_A + p * LANES, D_A + (p + 1) * LANES)
            vsl = slice(2 * D_A + p * LANES, 2 * D_A + (p + 1) * LANES)
            q, kc, vc = cur_ref[0, :, sl], cur_ref[0, :, ksl], cur_ref[0, :, vsl]
            dov = do_ref[0, :, sl]
            dd = dov.astype(F32) * oa_ref[0, :, sl]
            Lv = L_ref[0, :, sl]
            if multi:
                kp, vp = prev_ref[0, :, ksl], prev_ref[0, :, vsl]
                qn = next_ref[0, :, sl]
                donv = don_ref[0, :, sl]
                ddn = donv.astype(F32) * oan_ref[0, :, sl]
                Lnv = Ln_ref[0, :, sl]
            dq_pair = jnp.zeros((BLK, LANES), F32)
            dk_pair = jnp.zeros((BLK, LANES), F32)
            dv_pair = jnp.zeros((BLK, LANES), F32)
            for hh in range(2):
                h = 2 * p + hh
                hm = _head_mask((BLK, LANES), hh)
                zero = jnp.zeros_like(q)
                qm = jnp.where(hm, q, zero)
                dom = jnp.where(hm, dov, zero)
                delta = jnp.sum(jnp.where(hm, dd, 0.0), axis=-1, keepdims=True)
                lse = Lv[:, hh * HEAD_DIM:hh * HEAD_DIM + 1]
                s_c = _dot_nt(qm, kc) * DIL_SCALE + bias_ref[h, :, BLK:2 * BLK]
                p_c = jnp.where(vcur, jnp.exp(s_c - lse), 0.0)
                ds_c = p_c * (_dot_nt(dom, vc) - delta)
                ds_cb = ds_c.astype(BF16)
                dq_h = _dot_nn(ds_cb, kc)
                dk_h = _dot_tn(ds_cb, qm)
                dv_h = _dot_tn(p_c.astype(BF16), dom)
                dbias_ref[h, :, BLK:2 * BLK] += ds_c
                if multi:
                    s_p = _dot_nt(qm, kp) * DIL_SCALE + bias_ref[h, :, 0:BLK]
                    p_p = jnp.where(vprev & has_prev, jnp.exp(s_p - lse), 0.0)
                    ds_p = p_p * (_dot_nt(dom, vp) - delta)
                    dq_h = dq_h + _dot_nn(ds_p.astype(BF16), kp)
                    dbias_ref[h, :, 0:BLK] += ds_p
                    qnm = jnp.where(hm, qn, zero)
                    donm = jnp.where(hm, donv, zero)
                    delta_n = jnp.sum(jnp.where(hm, ddn, 0.0), axis=-1, keepdims=True)
                    lse_n = Lnv[:, hh * HEAD_DIM:hh * HEAD_DIM + 1]
                    s_n = _dot_nt(qnm, kc) * DIL_SCALE + bias_ref[h, :, 0:BLK]
                    p_n = jnp.where(vprev & has_next, jnp.exp(s_n - lse_n), 0.0)
                    ds_n = p_n * (_dot_nt(donm, vc) - delta_n)
                    dk_h = dk_h + _dot_tn(ds_n.astype(BF16), qnm)
                    dv_h = dv_h + _dot_tn(p_n.astype(BF16), donm)
                dq_pair = dq_pair + jnp.where(hm, dq_h, 0.0) * DIL_SCALE
                dk_pair = dk_pair + jnp.where(hm, dk_h, 0.0) * DIL_SCALE
                dv_pair = dv_pair + jnp.where(hm, dv_h, 0.0)
            dqkv_ref[0, :, sl] = dq_pair
            dqkv_ref[0, :, ksl] = dk_pair
            dqkv_ref[0, :, vsl] = dv_pair

    def at(off):
        return lambda b, r, i: (b, jnp.clip(i + off, 0, nb - 1), r)

    qkv_spec = lambda off: pl.BlockSpec((1, BLK, P_QKV), at(off))
    da_spec = lambda off: pl.BlockSpec((1, BLK, D_A), at(off))
    bias_spec = pl.BlockSpec((N_HEADS, BLK, 2 * BLK), lambda b, r, i: (0, 0, 0))
    if multi:
        in_specs = [qkv_spec(0), qkv_spec(-1), qkv_spec(1), da_spec(0), da_spec(1), da_spec(0), da_spec(1),
                    da_spec(0), da_spec(1), bias_spec]
        args = [qkv_v, qkv_v, qkv_v, do_v, do_v, oa_v, oa_v, L_v, L_v, bias]
    else:
        in_specs = [qkv_spec(0), da_spec(0), da_spec(0), da_spec(0), bias_spec]
        args = [qkv_v, do_v, oa_v, L_v, bias]
    dqkv, dbias = pl.pallas_call(
        body, name=name,
        out_shape=[jax.ShapeDtypeStruct((B, n, d * P_QKV), F32),
                   jax.ShapeDtypeStruct((N_HEADS, BLK, 2 * BLK), F32)],
        grid=(B, d, nb),
        in_specs=in_specs,
        out_specs=[qkv_spec(0), bias_spec],
        compiler_params=_cparams(("arbitrary", "arbitrary", "arbitrary")),
    )(*args)
    return dqkv.reshape(B, S, P_QKV), dbias


def _sum3_bf16(a, b, c, name):
    B, S, W = a.shape
    ts = 512

    def body(a_ref, b_ref, c_ref, o_ref):
        o_ref[...] = (a_ref[...] + b_ref[...] + c_ref[...]).astype(o_ref.dtype)

    tok = pl.BlockSpec((1, ts, W), lambda b, s: (b, s, 0))
    return pl.pallas_call(
        body, name=name, out_shape=jax.ShapeDtypeStruct((B, S, W), BF16), grid=(B, S // ts),
        in_specs=[tok] * 3, out_specs=tok, compiler_params=_cparams(("parallel", "parallel")),
    )(a, b, c)


def _bias_grad(dbias_list, buckets, name):
    nbr = len(dbias_list)

    def body(*refs):
        d_refs, bk_ref, o_ref = refs[:nbr], refs[nbr], refs[nbr + 1]
        lane = lax.broadcasted_iota(jnp.int32, (1, LANES), 1)
        for h in range(N_HEADS):
            def step(bkt, acc):
                tot = jnp.zeros((1, 1), F32)
                for bi in range(nbr):
                    sel = jnp.where(bk_ref[bi] == bkt, d_refs[bi][h], 0.0)
                    tot = tot + jnp.sum(jnp.sum(sel, axis=1, keepdims=True), axis=0, keepdims=True)
                return acc + jnp.where(lane == bkt, tot, 0.0)

            o_ref[h:h + 1, :] = lax.fori_loop(0, N_BUCKETS, step, jnp.zeros((1, LANES), F32))

    band = pl.BlockSpec((N_HEADS, BLK, 2 * BLK), lambda i: (0, 0, 0))
    return pl.pallas_call(
        body, name=name, out_shape=jax.ShapeDtypeStruct((N_HEADS, LANES), F32), grid=(1,),
        in_specs=[band] * nbr + [pl.BlockSpec((nbr, BLK, 2 * BLK), lambda i: (0, 0, 0))],
        out_specs=pl.BlockSpec((N_HEADS, LANES), lambda i: (0, 0)),
        compiler_params=_cparams(("arbitrary",)),
    )(*dbias_list, buckets)


MLA_TQ = 256
MLA_TK = 256


def _causal(i, j, tq, tk):
    qpos = i * tq + lax.broadcasted_iota(jnp.int32, (tq, tk), 0)
    kpos = j * tk + lax.broadcasted_iota(jnp.int32, (tq, tk), 1)
    return qpos >= kpos


def _mla_fwd(q, k, v, name):
    B, S, _ = q.shape
    tq, tk = MLA_TQ, MLA_TK
    npair = N_HEADS // 2

    def body(q_ref, k_ref, v_ref, o_ref, lse_ref, m_s, l_s, acc_s):
        i = pl.program_id(2)
        o_pair = jnp.zeros((tq, LANES), F32)
        lse_pair = jnp.zeros((tq, LANES), F32)
        for hh in range(2):
            hsl = slice(hh * LANES, (hh + 1) * LANES)
            hm_v = _head_mask((tk, LANES), hh)
            qh = q_ref[0, :, hsl]
            m_s[...] = jnp.full_like(m_s, NEG)
            l_s[...] = jnp.zeros_like(l_s)
            acc_s[...] = jnp.zeros_like(acc_s)

            def step(j, carry):
                kj = k_ref[0, pl.ds(pl.multiple_of(j * tk, tk), tk), hsl]
                vj = v_ref[0, pl.ds(pl.multiple_of(j * tk, tk), tk), :]
                vj = jnp.where(hm_v, vj, jnp.zeros_like(vj))
                s = _dot_nt(qh, kj) * MLA_SCALE
                s = jnp.where(_causal(i, j, tq, tk), s, NEG)
                m_old = m_s[...]
                m_new = jnp.maximum(m_old, jnp.max(s, axis=-1, keepdims=True))
                alpha = jnp.exp(m_old - m_new)
                e = jnp.exp(s - m_new)
                l_s[...] = alpha * l_s[...] + jnp.sum(e, axis=-1, keepdims=True)
                acc_s[...] = alpha * acc_s[...] + _dot_nn(e.astype(BF16), vj)
                m_s[...] = m_new
                return carry

            lax.fori_loop(0, (i * tq) // tk + tq // tk, step, 0)
            o_pair = o_pair + acc_s[...] / l_s[...]
            lse_pair = jnp.where(_head_mask((tq, LANES), hh), m_s[...] + jnp.log(l_s[...]), lse_pair)
        o_ref[0] = o_pair
        lse_ref[0] = lse_pair

    o_spec = pl.BlockSpec((1, tq, LANES), lambda b, p, i: (b, i, p))
    return pl.pallas_call(
        body, name=name,
        out_shape=[jax.ShapeDtypeStruct((B, S, D_B), F32)] * 2,
        grid=(B, npair, S // tq),
        in_specs=[pl.BlockSpec((1, tq, 2 * LANES), lambda b, p, i: (b, i, p)),
                  pl.BlockSpec((1, S, 2 * LANES), lambda b, p, i: (b, 0, p)),
                  pl.BlockSpec((1, S, LANES), lambda b, p, i: (b, 0, p))],
        out_specs=[o_spec, o_spec],
        scratch_shapes=[pltpu.VMEM((tq, 1), F32), pltpu.VMEM((tq, 1), F32), pltpu.VMEM((tq, LANES), F32)],
        compiler_params=_cparams(("parallel", "parallel", "arbitrary")),
    )(q, k, v)


def _mla_bwd_dq(q, k, v, do, o, lse, name):
    B, S, _ = q.shape
    tq, tk = MLA_TQ, MLA_TK
    npair = N_HEADS // 2

    def body(q_ref, k_ref, v_ref, do_ref, o_ref, lse_ref, dq_ref, acc_s):
        i = pl.program_id(2)
        dov = do_ref[0]
        dd = dov.astype(F32) * o_ref[0]
        for hh in range(2):
            hsl = slice(hh * LANES, (hh + 1) * LANES)
            hm = _head_mask((tq, LANES), hh)
            qh = q_ref[0, :, hsl]
            dom = jnp.where(hm, dov, jnp.zeros_like(dov))
            delta = jnp.sum(jnp.where(hm, dd, 0.0), axis=-1, keepdims=True)
            lse_h = lse_ref[0, :, hh * HEAD_DIM:hh * HEAD_DIM + 1]
            acc_s[...] = jnp.zeros_like(acc_s)

            def step(j, carry):
                kj = k_ref[0, pl.ds(pl.multiple_of(j * tk, tk), tk), hsl]
                vj = v_ref[0, pl.ds(pl.multiple_of(j * tk, tk), tk), :]
                s = _dot_nt(qh, kj) * MLA_SCALE
                pr = jnp.where(_causal(i, j, tq, tk), jnp.exp(s - lse_h), 0.0)
                ds = pr * (_dot_nt(dom, vj) - delta)
                acc_s[...] += _dot_nn(ds.astype(BF16), kj)
                return carry

            lax.fori_loop(0, (i * tq) // tk + tq // tk, step, 0)
            dq_ref[0, :, hsl] = acc_s[...] * MLA_SCALE

    return pl.pallas_call(
        body, name=name,
        out_shape=jax.ShapeDtypeStruct((B, S, N_HEADS * LANES), F32),
        grid=(B, npair, S // tq),
        in_specs=[pl.BlockSpec((1, tq, 2 * LANES), lambda b, p, i: (b, i, p)),
                  pl.BlockSpec((1, S, 2 * LANES), lambda b, p, i: (b, 0, p)),
                  pl.BlockSpec((1, S, LANES), lambda b, p, i: (b, 0, p)),
                  pl.BlockSpec((1, tq, LANES), lambda b, p, i: (b, i, p)),
                  pl.BlockSpec((1, tq, LANES), lambda b, p, i: (b, i, p)),
                  pl.BlockSpec((1, tq, LANES), lambda b, p, i: (b, i, p))],
        out_specs=pl.BlockSpec((1, tq, 2 * LANES), lambda b, p, i: (b, i, p)),
        scratch_shapes=[pltpu.VMEM((tq, LANES), F32)],
        compiler_params=_cparams(("parallel", "parallel", "arbitrary")),
    )(q, k, v, do, o, lse)


def _mla_bwd_dkv(q, k, v, do, o, lse, name):
    B, S, _ = q.shape
    tq, tk = MLA_TQ, MLA_TK
    npair = N_HEADS // 2
    nq = S // tq

    def body(q_ref, k_ref, v_ref, do_ref, o_ref, lse_ref, dk_ref, dv_ref, dk_s, dv_s):
        j = pl.program_id(2)
        vj = v_ref[0]
        dv_s[...] = jnp.zeros_like(dv_s)
        for hh in range(2):
            hsl = slice(hh * LANES, (hh + 1) * LANES)
            hm = _head_mask((tq, LANES), hh)
            kj = k_ref[0, :, hsl]
            dk_s[...] = jnp.zeros_like(dk_s)

            def step(i, carry):
                rows = pl.ds(pl.multiple_of(i * tq, tq), tq)
                qi = q_ref[0, rows, hsl]
                dov = do_ref[0, rows, :]
                dom = jnp.where(hm, dov, jnp.zeros_like(dov))
                delta = jnp.sum(jnp.where(hm, dov.astype(F32) * o_ref[0, rows, :], 0.0), axis=-1, keepdims=True)
                lse_h = lse_ref[0, rows, hh * HEAD_DIM:hh * HEAD_DIM + 1]
                s = _dot_nt(qi, kj) * MLA_SCALE
                pr = jnp.where(_causal(i, j, tq, tk), jnp.exp(s - lse_h), 0.0)
                dv_s[...] += _dot_tn(pr.astype(BF16), dom)
                ds = pr * (_dot_nt(dom, vj) - delta)
                dk_s[...] += _dot_tn(ds.astype(BF16), qi)
                return carry

            lax.fori_loop((j * tk) // tq, nq, step, 0)
            dk_ref[0, :, hsl] = dk_s[...] * MLA_SCALE
        dv_ref[0] = dv_s[...]

    return pl.pallas_call(
        body, name=name,
        out_shape=[jax.ShapeDtypeStruct((B, S, N_HEADS * LANES), F32), jax.ShapeDtypeStruct((B, S, D_B), F32)],
        grid=(B, npair, S // tk),
        in_specs=[pl.BlockSpec((1, S, 2 * LANES), lambda b, p, j: (b, 0, p)),
                  pl.BlockSpec((1, tk, 2 * LANES), lambda b, p, j: (b, j, p)),
                  pl.BlockSpec((1, tk, LANES), lambda b, p, j: (b, j, p)),
                  pl.BlockSpec((1, S, LANES), lambda b, p, j: (b, 0, p)),
                  pl.BlockSpec((1, S, LANES), lambda b, p, j: (b, 0, p)),
                  pl.BlockSpec((1, S, LANES), lambda b, p, j: (b, 0, p))],
        out_specs=[pl.BlockSpec((1, tk, 2 * LANES), lambda b, p, j: (b, j, p)),
                   pl.BlockSpec((1, tk, LANES), lambda b, p, j: (b, j, p))],
        scratch_shapes=[pltpu.VMEM((tk, LANES), F32), pltpu.VMEM((tk, LANES), F32)],
        compiler_params=_cparams(("parallel", "parallel", "arbitrary")),
    )(q, k, v, do, o, lse)


def _local_step(x, target, mod, wts, gains, rel_bias):
    B, S, D = x.shape
    T = B * S
    sh1, sc1, g1, sh2, sc2, g2 = [mod[:, i * D:(i + 1) * D].reshape(B, 1, D) for i in range(N_MOD)]
    cs, sn = _rope_tables()
    buckets = np.stack([_band_buckets(d) for d in DILATIONS])
    bias = [jnp.transpose(rel_bias[buckets[i]], (2, 0, 1)) for i in range(len(DILATIONS))]
    w_in = wts["w_in"]

    h1 = _adaln_fwd(x, gains["g_norm1"], sc1, sh1, "adaln1_fwd")
    h1f = h1.reshape(T, D)
    qkv = _mm(h1f, w_in[:, :P_QKV], "nn", BF16, "mm_qkv").reshape(B, S, P_QKV)
    rest = _mm(h1f, w_in[:, P_QKV:], "nn", F32, "mm_rest")
    o_d, lse_d = [], []
    for i, d in enumerate(DILATIONS):
        o_i, lse_i = _dil_fwd(qkv, bias[i], d, f"dil_fwd_{d}")
        o_d.append(o_i)
        lse_d.append(lse_i)
    out_a, lse_a = _dil_merge(o_d, lse_d, "dil_merge")
    cqn = _rms_fwd(rest, 1, Q_LORA, gains["g_cq"], "rms_cq_fwd")
    ckvn = _rms_fwd(rest, 0, KV_LORA, gains["g_ckv"], "rms_ckv_fwd")
    rest3 = rest.reshape(B, S, P_REST)
    q_raw = _mm(cqn, wts["w_uq"], "nn", F32, "mm_uq").reshape(B, S, N_HEADS * LANES)
    qc = _rope_apply(q_raw, cs, sn, BF16, "rope_q")
    kn_raw = _mm(ckvn, wts["w_kv"][:, :N_HEADS * LANES], "nn", F32, "mm_uk").reshape(B, S, N_HEADS * LANES)
    kc = _rope_apply(kn_raw, cs, sn, BF16, "rope_k", add=rest3, add_blk=KV_LORA // LANES)
    v = _mm(ckvn, wts["w_kv"][:, N_HEADS * LANES:], "nn", BF16, "mm_uv").reshape(B, S, D_B)
    out_b, lse_b = _mla_fwd(qc, kc, v, "mla_fwd")
    out_af, out_bf = out_a.reshape(T, D_A), out_b.reshape(T, D_B)
    ya = _rms_fwd(out_af, 0, D_A, gains["g_out_a"], "rms_outa_fwd")
    yb = _rms_fwd(out_bf, 0, D_B, gains["g_out_b"], "rms_outb_fwd")
    y = jnp.concatenate([ya, yb], axis=1)
    mix = _mm(y, wts["w_out"], "nn", F32, "mm_out").reshape(B, S, D)
    h2, x1 = _adaln_fwd(x, gains["g_norm2"], sc2, sh2, "adaln2_fwd", mix=mix, gate=g1)
    h2f = h2.reshape(T, D)
    gu = _mm(h2f, wts["w_ffn_in"], "nn", F32, "mm_ffn_in")
    act = _swiglu_fwd(gu, "swiglu_fwd")
    f = _mm(act, wts["w_ffn_out"], "nn", F32, "mm_ffn_out").reshape(B, S, D)
    dx2, df, dg2, dg_final, loss = _final_loss(x1, f, g2, gains["g_final"], target, "final_loss")

    dff = df.reshape(T, D)
    da = _mm(dff, wts["w_ffn_out"], "nt", F32, "mm_ffn_out_dx")
    gw_ffn_out = _mm(act, dff, "tn", F32, "mm_ffn_out_dw")
    dgu = _swiglu_bwd(da, gu, "swiglu_bwd")
    dh2 = _mm(dgu, wts["w_ffn_in"], "nt", F32, "mm_ffn_in_dx").reshape(B, S, D)
    gw_ffn_in = _mm(h2f, dgu, "tn", F32, "mm_ffn_in_dw")
    dx1, dsh2, dsc2, dg_norm2, dg1, dmix = _adaln_bwd(dh2, x1, gains["g_norm2"], sc2, dx2, "adaln2_bwd",
                                                      mix=mix, gate=g1)
    dmixf = dmix.reshape(T, D)
    dy = _mm(dmixf, wts["w_out"], "nt", F32, "mm_out_dx")
    gw_out = _mm(y, dmixf, "tn", F32, "mm_out_dw")
    do_a, dg_out_a = _rms_bwd(dy, 0, out_af, 0, D_A, gains["g_out_a"], "rms_outa_bwd")
    do_b, dg_out_b = _rms_bwd(dy, 1, out_bf, 0, D_B, gains["g_out_b"], "rms_outb_bwd")
    do_b3 = do_b.reshape(B, S, D_B)
    dqc = _mla_bwd_dq(qc, kc, v, do_b3, out_b, lse_b, "mla_bwd_dq")
    dkc, dv = _mla_bwd_dkv(qc, kc, v, do_b3, out_b, lse_b, "mla_bwd_dkv")
    dq_raw = _rope_apply(dqc, cs, -sn, BF16, "rope_q_bwd").reshape(T, N_HEADS * LANES)
    dkrw = _krope_bwd(dkc, cs, -sn, "rope_k_bwd").reshape(T, LANES)
    dcqn = _mm(dq_raw, wts["w_uq"], "nt", F32, "mm_uq_dx")
    gw_uq = _mm(cqn, dq_raw, "tn", F32, "mm_uq_dw")
    dkv = jnp.concatenate([dkc.reshape(T, -1), dv.reshape(T, -1)], axis=1).astype(BF16)
    dckvn = _mm(dkv, wts["w_kv"], "nt", F32, "mm_ukv_dx")
    gw_kv = _mm(ckvn, dkv, "tn", F32, "mm_ukv_dw")
    dcq, dg_cq = _rms_bwd(dcqn, 0, rest, 1, Q_LORA, gains["g_cq"], "rms_cq_bwd")
    dckv, dg_ckv = _rms_bwd(dckvn, 0, rest, 0, KV_LORA, gains["g_ckv"], "rms_ckv_bwd")
    do_a3 = do_a.reshape(B, S, D_A)
    dqkv_d, dbias_d = [], []
    for i, d in enumerate(DILATIONS):
        dqkv_i, dbias_i = _dil_bwd(qkv, do_a3, out_a, lse_a, bias[i], d, f"dil_bwd_{d}")
        dqkv_d.append(dqkv_i)
        dbias_d.append(dbias_i)
    dqkv = _sum3_bf16(*dqkv_d, "dil_bwd_sum").reshape(T, P_QKV)
    g_rel_bias = _bias_grad(dbias_d, jnp.asarray(buckets), "rel_bias_grad")[:, :N_BUCKETS].T
    dproj = jnp.concatenate([dqkv, dckv, dkrw, dcq], axis=1)
    dh1 = _mm(dproj, w_in, "nt", F32, "mm_in_dx").reshape(B, S, D)
    gw_in = _mm(h1f, dproj, "tn", F32, "mm_in_dw")
    grad_x, dsh1, dsc1, dg_norm1 = _adaln_bwd(dh1, x, gains["g_norm1"], sc1, dx1, "adaln1_bwd")
    gmod = jnp.concatenate([dsh1, dsc1, dg1, dsh2, dsc2, dg2], axis=-1).reshape(B, N_MOD * D)
    grads = dict(w_in=gw_in, w_uq=gw_uq, w_kv=gw_kv, w_out=gw_out, w_ffn_in=gw_ffn_in, w_ffn_out=gw_ffn_out,
                 g_norm1=dg_norm1, g_cq=dg_cq, g_ckv=dg_ckv, rel_bias=g_rel_bias, g_out_a=dg_out_a,
                 g_out_b=dg_out_b, g_norm2=dg_norm2, g_final=dg_final)
    return loss, grad_x, gmod, grads


def _w_in_to_kernel(w):
    z = lambda n: jnp.zeros((w.shape[0], n), w.dtype)
    i3, i4, i5 = 3 * D_A, 3 * D_A + Q_LORA, 3 * D_A + Q_LORA + KV_LORA
    return jnp.concatenate([w[:, :i3], w[:, i4:i5], z(NOPE_DIM), w[:, i5:], z(LANES - NOPE_DIM - ROPE_DIM),
                            w[:, i3:i4]], axis=1)


def _w_in_from_kernel(g):
    o = P_QKV + KV_LORA
    return jnp.concatenate([g[:, :P_QKV], g[:, o + LANES:], g[:, P_QKV:o],
                            g[:, o + NOPE_DIM:o + NOPE_DIM + ROPE_DIM]], axis=1)


def _w_uq_to_kernel(w):
    w3 = w.reshape(Q_LORA, N_HEADS, NOPE_DIM + ROPE_DIM)
    return jnp.pad(w3, ((0, 0), (0, 0), (0, LANES - NOPE_DIM - ROPE_DIM))).reshape(Q_LORA, N_HEADS * LANES)


def _w_uq_from_kernel(g):
    return g.reshape(Q_LORA, N_HEADS, LANES)[:, :, :NOPE_DIM + ROPE_DIM].reshape(Q_LORA, -1)


def _w_ukv_to_kernel(w):
    w3 = w.reshape(KV_LORA, N_HEADS, 2 * HEAD_DIM)
    wk = jnp.pad(w3[:, :, :NOPE_DIM], ((0, 0), (0, 0), (0, LANES - NOPE_DIM))).reshape(KV_LORA, N_HEADS * LANES)
    wv = w3[:, :, NOPE_DIM:].reshape(KV_LORA, D_B)
    return jnp.concatenate([wk, wv], axis=1)


def _w_ukv_from_kernel(g):
    gk = g[:, :N_HEADS * LANES].reshape(KV_LORA, N_HEADS, LANES)[:, :, :NOPE_DIM]
    gv = g[:, N_HEADS * LANES:].reshape(KV_LORA, N_HEADS, HEAD_DIM)
    return jnp.concatenate([gk, gv], axis=2).reshape(KV_LORA, -1)


MESH = pl.DeviceIdType.MESH


def _my_place():
    return lax.axis_index("x"), lax.axis_index("y"), lax.axis_index("c")


def _other_chips(x, y):
    return [(1 - x, y), (x, 1 - y), (1 - x, 1 - y)]


def _allgather8(x_shard, name, in_hbm):
    m_per, n = x_shard.shape
    space = pl.ANY if in_hbm else pltpu.VMEM

    def body(x_ref, out_ref, send_sems, recv_sems, local_sem):
        x, y, c = _my_place()
        me, sibling = (x, y, c), (x, y, 1 - c)
        chips = _other_chips(x, y)

        def rows(px, py, pc):
            return out_ref.at[pl.ds((4 * px + 2 * py + pc) * m_per, m_per), :]

        def copy(k, block, to, src=None):
            return pltpu.make_async_remote_copy(
                src_ref=rows(*block) if src is None else src, dst_ref=rows(*block),
                send_sem=send_sems.at[k], recv_sem=recv_sems.at[k], device_id=to, device_id_type=MESH)

        mine = pltpu.make_async_copy(x_ref, rows(*me), local_sem)
        mine.start()
        first = [copy(0, me, sibling, src=x_ref)]
        first += [copy(1 + j, me, (*chip, c), src=x_ref) for j, chip in enumerate(chips)]
        for cp in first:
            cp.start()
        passed = [copy(4 + j, (*chip, c), sibling) for j, chip in enumerate(chips)]
        for j, chip in enumerate(chips):
            copy(1 + j, (*chip, c), me).wait_recv()
            passed[j].start()
        copy(0, sibling, me).wait_recv()
        for j, chip in enumerate(chips):
            copy(4 + j, (*chip, 1 - c), me).wait_recv()
        for cp in first + passed:
            cp.wait_send()
        mine.wait()

    return pl.pallas_call(
        body, name=name,
        out_shape=jax.ShapeDtypeStruct((N_DEV * m_per, n), x_shard.dtype),
        in_specs=[pl.BlockSpec(memory_space=space)],
        out_specs=pl.BlockSpec(memory_space=space),
        scratch_shapes=[pltpu.SemaphoreType.DMA((7,)), pltpu.SemaphoreType.DMA((7,)), pltpu.SemaphoreType.DMA],
        compiler_params=pltpu.CompilerParams(vmem_limit_bytes=VMEM_LIMIT),
    )(x_shard)


def _rs_to_sibling(g8, name):
    _, R, N = g8.shape

    def body(g_ref, r_ref, send_sems, recv_sems):
        x, y, c = _my_place()
        copies = [pltpu.make_async_remote_copy(
            src_ref=g_ref.at[2 * s + 1 - c], dst_ref=r_ref.at[s], send_sem=send_sems.at[s],
            recv_sem=recv_sems.at[s], device_id=(x, y, 1 - c), device_id_type=MESH) for s in range(N_CHIP)]
        for cp in copies:
            cp.start()
        for cp in copies:
            cp.wait()

    return pl.pallas_call(
        body, name=name, out_shape=jax.ShapeDtypeStruct((N_CHIP, R, N), g8.dtype),
        in_specs=[pl.BlockSpec(memory_space=pl.ANY)], out_specs=pl.BlockSpec(memory_space=pl.ANY),
        scratch_shapes=[pltpu.SemaphoreType.DMA((N_CHIP,)), pltpu.SemaphoreType.DMA((N_CHIP,))],
    )(g8)


def _rs_to_chips(a4, name):
    _, R, N = a4.shape

    def body(a_ref, r_ref, send_sems, recv_sems):
        x, y, c = _my_place()
        copies = [pltpu.make_async_remote_copy(
            src_ref=a_ref.at[2 * cx + cy], dst_ref=r_ref.at[j], send_sem=send_sems.at[j],
            recv_sem=recv_sems.at[j], device_id=(cx, cy, c), device_id_type=MESH)
            for j, (cx, cy) in enumerate(_other_chips(x, y))]
        for cp in copies:
            cp.start()
        for cp in copies:
            cp.wait()

    return pl.pallas_call(
        body, name=name, out_shape=jax.ShapeDtypeStruct((N_CHIP - 1, R, N), a4.dtype),
        in_specs=[pl.BlockSpec(memory_space=pl.ANY)], out_specs=pl.BlockSpec(memory_space=pl.ANY),
        scratch_shapes=[pltpu.SemaphoreType.DMA((N_CHIP - 1,)), pltpu.SemaphoreType.DMA((N_CHIP - 1,))],
    )(a4)


def _swap_halves(h, name):
    R, N = h.shape

    def body(h_ref, o_ref, send_sem, recv_sem, local_sem):
        x, y, c = _my_place()
        mine = pltpu.make_async_copy(h_ref, o_ref.at[c], local_sem)
        mine.start()
        cp = pltpu.make_async_remote_copy(src_ref=h_ref, dst_ref=o_ref.at[c], send_sem=send_sem,
                                          recv_sem=recv_sem, device_id=(x, y, 1 - c), device_id_type=MESH)
        cp.start()
        pltpu.make_async_remote_copy(src_ref=h_ref, dst_ref=o_ref.at[1 - c], send_sem=send_sem,
                                     recv_sem=recv_sem, device_id=(x, y, 1 - c), device_id_type=MESH).wait_recv()
        cp.wait_send()
        mine.wait()

    return pl.pallas_call(
        body, name=name, out_shape=jax.ShapeDtypeStruct((2, R, N), h.dtype),
        in_specs=[pl.BlockSpec(memory_space=pl.ANY)], out_specs=pl.BlockSpec(memory_space=pl.ANY),
        scratch_shapes=[pltpu.SemaphoreType.DMA, pltpu.SemaphoreType.DMA, pltpu.SemaphoreType.DMA],
    )(h)


def _add_blocks(a, a_idx_fn, others, sel, name):
    _, R, N = a.shape
    ns = others[0][0].shape[0] if others[0][1] is None else 1
    tr = 512
    n_o = len(others)

    def body(sel_ref, a_ref, *refs):
        o_ref = refs[n_o]
        acc = a_ref[0]
        for r in refs[:n_o]:
            acc = acc + r[0]
        o_ref[0] = acc

    in_specs = [pl.BlockSpec((1, tr, N), lambda s, i, sel_ref: (a_idx_fn(s, sel_ref[0]), i, 0))]
    args = [a]
    for arr, fixed in others:
        if fixed is None:
            in_specs.append(pl.BlockSpec((1, tr, N), lambda s, i, sel_ref: (s, i, 0)))
        else:
            in_specs.append(pl.BlockSpec((1, tr, N), lambda s, i, sel_ref, fixed=fixed: (fixed, i, 0)))
        args.append(arr)
    grid_spec = pltpu.PrefetchScalarGridSpec(
        num_scalar_prefetch=1, grid=(ns, R // tr), in_specs=in_specs,
        out_specs=pl.BlockSpec((1, tr, N), lambda s, i, sel_ref: (s, i, 0)))
    return pl.pallas_call(
        body, name=name, out_shape=jax.ShapeDtypeStruct((ns, R, N), a.dtype), grid_spec=grid_spec,
        compiler_params=_cparams(("parallel", "parallel")),
    )(sel, *args)


def _reduce_scatter(g8):
    x, y, c = _my_place()
    c_sel = jnp.reshape(c, (1,)).astype(jnp.int32)
    s_sel = jnp.reshape(2 * x + y, (1,)).astype(jnp.int32)
    r1 = _rs_to_sibling(g8, "rs_to_sibling")
    a4 = _add_blocks(g8, lambda s, cc: 2 * s + cc, [(r1, None)], c_sel, "rs_add_sibling")
    r2 = _rs_to_chips(a4, "rs_to_chips")
    h = _add_blocks(a4, lambda s, ss: ss, [(r2, 0), (r2, 1), (r2, 2)], s_sel, "rs_add_chips")
    full = _swap_halves(h[0], "rs_swap_halves")
    return full.reshape(2 * g8.shape[1], g8.shape[2])


def _ada_fwd(c_all, w_ada, b_ada, name):
    nb, D = c_all.shape
    ncol = w_ada.shape[1]
    tc = 512

    def body(c_ref, w_ref, b_ref, o_ref):
        cv = c_ref[...]
        cond = (cv * jax.nn.sigmoid(cv)).astype(BF16)
        o_ref[...] = jnp.dot(cond, w_ref[...].astype(BF16), preferred_element_type=F32) + b_ref[...]

    return pl.pallas_call(
        body, name=name, out_shape=jax.ShapeDtypeStruct((nb, ncol), F32), grid=(ncol // tc,),
        in_specs=[pl.BlockSpec((nb, D), lambda j: (0, 0)), pl.BlockSpec((D, tc), lambda j: (0, j)),
                  pl.BlockSpec((1, tc), lambda j: (0, j))],
        out_specs=pl.BlockSpec((nb, tc), lambda j: (0, j)),
        compiler_params=_cparams(("parallel",)),
    )(c_all, w_ada, b_ada)


def _ada_bwd(c_all, gmod_cols, name):
    nb, D = c_all.shape
    ncol = gmod_cols.shape[1]
    tc = 512

    def body(c_ref, g_ref, o_ref):
        cv = c_ref[...]
        cond = (cv * jax.nn.sigmoid(cv)).astype(BF16)
        o_ref[...] = _dot_tn(cond, g_ref[...].astype(BF16))

    return pl.pallas_call(
        body, name=name, out_shape=jax.ShapeDtypeStruct((D, ncol), F32), grid=(ncol // tc,),
        in_specs=[pl.BlockSpec((nb, D), lambda j: (0, 0)), pl.BlockSpec((nb, tc), lambda j: (0, j))],
        out_specs=pl.BlockSpec((D, tc), lambda j: (0, j)),
        compiler_params=_cparams(("parallel",)),
    )(c_all, gmod_cols)


def _adam_math(w, g, m, v):
    m = ADAM_B1 * m + (1.0 - ADAM_B1) * g
    v = ADAM_B2 * v + (1.0 - ADAM_B2) * (g * g)
    m_hat = m / (1.0 - ADAM_B1 ** ADAM_STEP)
    v_hat = v / (1.0 - ADAM_B2 ** ADAM_STEP)
    delta = -ADAM_LR * (m_hat / (jnp.sqrt(v_hat) + ADAM_EPS) + ADAM_WD * w)
    return delta, m, v


def _adamw(w, g, m, v, name):
    rows, cols = w.shape
    tr = _pick(rows, (256, 192, 176, 128, 64, 8))

    def body(w_ref, g_ref, m_ref, v_ref, d_ref, mo_ref, vo_ref):
        d, mn, vn = _adam_math(w_ref[...], g_ref[...], m_ref[...], v_ref[...])
        d_ref[...] = d
        mo_ref[...] = mn
        vo_ref[...] = vn

    spec = pl.BlockSpec((tr, cols), lambda i: (i, 0))
    return pl.pallas_call(
        body, name=name, out_shape=[jax.ShapeDtypeStruct((rows, cols), F32)] * 3, grid=(rows // tr,),
        in_specs=[spec] * 4, out_specs=[spec] * 3, compiler_params=_cparams(("parallel",)),
    )(w, g, m, v)


VEC_ROWS = 8


def _adamw_rows(w, parts, m, v, name):
    n = w.shape[1]
    P = parts.shape[0]
    assert n % (VEC_ROWS * LANES) == 0, n
    shp = (VEC_ROWS, n // VEC_ROWS)

    def body(w_ref, p_ref, m_ref, v_ref, g_ref, d_ref, mo_ref, vo_ref):
        g = p_ref[0]
        for k in range(1, P):
            g = g + p_ref[k]
        d, mn, vn = _adam_math(w_ref[...], g, m_ref[...], v_ref[...])
        g_ref[...] = g
        d_ref[...] = d
        mo_ref[...] = mn
        vo_ref[...] = vn

    vec = pl.BlockSpec(shp, lambda i: (0, 0))
    out = pl.pallas_call(
        body, name=name, out_shape=[jax.ShapeDtypeStruct(shp, F32)] * 4, grid=(1,),
        in_specs=[vec, pl.BlockSpec((P,) + shp, lambda i: (0, 0, 0)), vec, vec], out_specs=[vec] * 4,
        compiler_params=_cparams(("arbitrary",)),
    )(w.reshape(shp), parts.reshape((P,) + shp), m.reshape(shp), v.reshape(shp))
    return [o.reshape(1, n) for o in out]


_SHARDED = (("w_in", 1024, 552, "col"), ("w_uq", 384, 192, "col"), ("w_ukv", 256, 256, "col"),
            ("w_out", 256, 1024, "row"), ("w_ffn_in", 1024, 1408, "col"), ("w_ffn_out", 704, 1024, "row"))
_SMALL = (("g_norm1", 1024), ("g_cq", 384), ("g_ckv", 256), ("rel_bias", 256), ("g_out_a", 512),
          ("g_out_b", 512), ("g_norm2", 1024), ("g_final", 1024))
_SMALL_PAD = 5120
_PACK_ELEMS = PACK_ROWS * D_MODEL


def _pack_shards(shards, dtype):
    flat = jnp.concatenate([shards[n].astype(dtype).reshape(-1) for n, _, _, _ in _SHARDED])
    return jnp.pad(flat, (0, _PACK_ELEMS - flat.shape[0])).reshape(PACK_ROWS, D_MODEL)


def _unpack_shards(packed):
    out, off = {}, 0
    for n, r, c, _ in _SHARDED:
        out[n] = packed[..., off:off + r * c].reshape(packed.shape[:-1] + (r, c))
        off += r * c
    return out


def _full_from_shards(sh, kind):
    if kind == "row":
        return sh.reshape(-1, sh.shape[-1])
    return jnp.transpose(sh, (1, 0, 2)).reshape(sh.shape[1], -1)


def _shards_from_full(full, rows, cols, kind):
    if kind == "row":
        return full.reshape(N_CHIP, rows * cols)
    return jnp.transpose(full.reshape(rows, N_CHIP, cols), (1, 0, 2)).reshape(N_CHIP, rows * cols)


def kernel(x, c, w_ada, b_ada, g_norm1, w_in, g_cq, w_uq, g_ckv, w_ukv, rel_bias, g_out_a, g_out_b, w_out, g_norm2, w_ffn_in, w_ffn_out, g_final, loss_target, m_w_ada, m_b_ada, m_g_norm1, m_w_in, m_g_cq, m_w_uq, m_g_ckv, m_w_ukv, m_rel_bias, m_g_out_a, m_g_out_b, m_w_out, m_g_norm2, m_w_ffn_in, m_w_ffn_out, m_g_final, v_w_ada, v_b_ada, v_g_norm1, v_w_in, v_g_cq, v_w_uq, v_g_ckv, v_w_ukv, v_rel_bias, v_g_out_a, v_g_out_b, v_w_out, v_g_norm2, v_w_ffn_in, v_w_ffn_out, v_g_final):
    names = ["w_ada", "b_ada", "g_norm1", "w_in", "g_cq", "w_uq", "g_ckv", "w_ukv", "rel_bias", "g_out_a",
             "g_out_b", "w_out", "g_norm2", "w_ffn_in", "w_ffn_out", "g_final"]
    W = dict(zip(names, [w_ada, b_ada, g_norm1, w_in, g_cq, w_uq, g_ckv, w_ukv, rel_bias, g_out_a, g_out_b,
                         w_out, g_norm2, w_ffn_in, w_ffn_out, g_final]))
    M = dict(zip(names, [m_w_ada, m_b_ada, m_g_norm1, m_w_in, m_g_cq, m_w_uq, m_g_ckv, m_w_ukv, m_rel_bias,
                         m_g_out_a, m_g_out_b, m_w_out, m_g_norm2, m_w_ffn_in, m_w_ffn_out, m_g_final]))
    V = dict(zip(names, [v_w_ada, v_b_ada, v_g_norm1, v_w_in, v_g_cq, v_w_uq, v_g_ckv, v_w_ukv, v_rel_bias,
                         v_g_out_a, v_g_out_b, v_w_out, v_g_norm2, v_w_ffn_in, v_w_ffn_out, v_g_final]))
    B, S, D = x.shape
    mx, my, mc = _my_place()
    dev = 4 * mx + 2 * my + mc
    chip = 2 * mx + my
    half_rows = PACK_ROWS // 2
    pad_rows = 8

    c_all = _allgather8(jnp.pad(c, ((0, pad_rows - B), (0, 0))), "ag_c", False)
    c_all = c_all.reshape(N_DEV, pad_rows, D)[:, :B].reshape(N_DEV * B, D)
    ada_cols = w_ada.shape[-1]
    b_cols = lax.dynamic_slice_in_dim(b_ada, chip * ada_cols, ada_cols, axis=1)
    mod_cols = _ada_fwd(c_all, w_ada[0], b_cols, "ada_fwd")
    mod_all = _allgather8(mod_cols, "ag_mod", False).reshape(N_DEV, N_DEV * B, ada_cols)[0::2]
    mod_all = jnp.transpose(mod_all, (1, 0, 2)).reshape(N_DEV * B, N_MOD * D)
    mod = lax.dynamic_slice_in_dim(mod_all, dev * B, B, axis=0)

    packed = _pack_shards({n: W[n][0] for n, _, _, _ in _SHARDED}, BF16)
    my_half = lax.dynamic_slice_in_dim(packed, mc * half_rows, half_rows, axis=0)
    gathered = _allgather8(my_half, "ag_weights", True).reshape(N_CHIP, _PACK_ELEMS)
    full = {n: _full_from_shards(sh, k) for (n, _, _, k), sh in
            zip(_SHARDED, _unpack_shards(gathered).values())}
    wts = dict(w_in=_w_in_to_kernel(full["w_in"]), w_uq=_w_uq_to_kernel(full["w_uq"]),
               w_kv=_w_ukv_to_kernel(full["w_ukv"]), w_out=full["w_out"], w_ffn_in=full["w_ffn_in"],
               w_ffn_out=full["w_ffn_out"])
    gains = dict(g_norm1=g_norm1, g_cq=g_cq, g_ckv=g_ckv, g_out_a=g_out_a, g_out_b=g_out_b, g_norm2=g_norm2,
                 g_final=g_final.reshape(1, D))

    loss, grad_x, gmod, grads = _local_step(x, loss_target, mod, wts, gains, rel_bias)
    loss = lax.psum(loss[0, 0], ("x", "y", "c"))

    n_small = _SMALL_PAD
    cat = lambda dct: jnp.concatenate([dct[n].reshape(1, -1) for n, _ in _SMALL]
                                      + [jnp.zeros((1, _SMALL_PAD - sum(s for _, s in _SMALL)), F32)], axis=1)
    small = cat(grads)
    rows = jnp.concatenate([gmod, jnp.pad(small, ((0, 0), (0, N_MOD * D - n_small))),
                            jnp.zeros((pad_rows - B - 1, N_MOD * D), F32)], axis=0)
    rows_all = _allgather8(rows, "ag_small", False).reshape(N_DEV, pad_rows, N_MOD * D)
    gmod_all = rows_all[:, :B].reshape(N_DEV * B, N_MOD * D)
    small_parts = rows_all[:, B, :n_small]

    nat = dict(w_in=_w_in_from_kernel(grads["w_in"]), w_uq=_w_uq_from_kernel(grads["w_uq"]),
               w_ukv=_w_ukv_from_kernel(grads["w_kv"]), w_out=grads["w_out"], w_ffn_in=grads["w_ffn_in"],
               w_ffn_out=grads["w_ffn_out"])
    g_flat = jnp.concatenate([_shards_from_full(nat[n], r, cc, k) for n, r, cc, k in _SHARDED], axis=1)
    g_flat = jnp.pad(g_flat, ((0, 0), (0, _PACK_ELEMS - g_flat.shape[1])))
    reduced = _reduce_scatter(g_flat.reshape(N_DEV, half_rows, D)).reshape(_PACK_ELEMS)
    G = _unpack_shards(reduced)

    gmod_cols = lax.dynamic_slice_in_dim(gmod_all, chip * ada_cols, ada_cols, axis=1)
    G["w_ada"] = _ada_bwd(c_all, gmod_cols, "ada_bwd")
    delta, new_m, new_v = {}, {}, {}
    for n in ["w_ada"] + [s[0] for s in _SHARDED]:
        shp = W[n].shape
        w2 = W[n].reshape(shp[-2], shp[-1])
        d_, m_, v_ = _adamw(w2, G[n], M[n].reshape(w2.shape), V[n].reshape(w2.shape), f"adamw_{n}")
        G[n], delta[n], new_m[n], new_v[n] = [a.reshape(shp) for a in (G[n], d_, m_, v_)]
    gs, ds_, ms_, vs_ = _adamw_rows(cat(W), small_parts, cat(M), cat(V), "adamw_small")
    off = 0
    for n, sz in _SMALL:
        shp = W[n].shape
        G[n], delta[n], new_m[n], new_v[n] = [a[:, off:off + sz].reshape(shp) for a in (gs, ds_, ms_, vs_)]
        off += sz
    G["b_ada"], delta["b_ada"], new_m["b_ada"], new_v["b_ada"] = _adamw_rows(b_ada, gmod_all, m_b_ada, v_b_ada,
                                                                          "adamw_b_ada")
    return (loss, grad_x, *[G[n] for n in names], *[delta[n] for n in names], *[new_m[n] for n in names],
            *[new_v[n] for n in names])
```

```python
import functools
import math

import numpy as np
import jax
import jax.numpy as jnp
from jax import lax
from jax.experimental import pallas as pl
from jax.experimental.pallas import tpu as pltpu

F32 = jnp.float32
BF16 = jnp.bfloat16

D_MODEL = 1024
SEQ = 2048
N_HEADS = 8
HEAD_DIM = 64
D_A = 512
D_B = 512
Q_LORA = 384
KV_LORA = 256
ROPE_DIM = 32
NOPE_DIM = 64
D_FF = 2816
N_MOD = 6
N_BUCKETS = 32
MAX_DISTANCE = 2048
ROPE_THETA = 10000.0
EPS = 1e-6
NEG = -1e30
BLK = 128
DILATIONS = (1, 4, 16)
SPAN = 128
MLA_SCALE = (NOPE_DIM + ROPE_DIM) ** -0.5
DIL_SCALE = HEAD_DIM ** -0.5

ADAM_LR = 0.001
ADAM_B1 = 0.9
ADAM_B2 = 0.999
ADAM_EPS = 1e-08
ADAM_WD = 0.01
ADAM_STEP = 10

N_DEV = 8
N_CHIP = 4
LANES = 128
VMEM_LIMIT = 48 * 1024 * 1024

P_QKV = 3 * D_A
P_REST = KV_LORA + LANES + Q_LORA


def _cparams(sem=None):
    return pltpu.CompilerParams(dimension_semantics=sem, vmem_limit_bytes=VMEM_LIMIT)


def _pick(n, cands):
    for c in cands:
        if n % c == 0:
            return c
    raise ValueError(f"no tile for {n} in {cands}")


def _mm(a, b, mode, out_dtype, name, col_blocks=None):
    blocked = col_blocks is not None
    if mode == "nn":
        (M, K) = a.shape
        K2, N = (b.shape[1], b.shape[0] * b.shape[2]) if blocked else b.shape
    elif mode == "nt":
        (M, K) = a.shape
        N, K2 = (b.shape[1], b.shape[0] * b.shape[2]) if blocked else b.shape
    else:
        (K, M), (K2, N) = a.shape, b.shape
    assert K == K2, (a.shape, b.shape, mode)
    tm = _pick(M, (512, 384, 256, 128))
    tn = _pick(N, (1408, 1024, 768, 512, 384, 256, 128))
    tk = _pick(K, (1024, 512, 384, 256, 128))
    if blocked and mode == "nt":
        tk = K // col_blocks
    elif blocked:
        tn = N // col_blocks
    nk = K // tk
    out_shape = (M, N)
    out_spec = pl.BlockSpec((tm, tn), lambda i, j, k: (i, j))
    if mode == "nn":
        a_spec = pl.BlockSpec((tm, tk), lambda i, j, k: (i, k))
        b_spec = (pl.BlockSpec((None, tk, tn), lambda i, j, k: (j, k, 0)) if blocked
                  else pl.BlockSpec((tk, tn), lambda i, j, k: (k, j)))
        dn = (((1,), (0,)), ((), ()))
    elif mode == "nt":
        a_spec = pl.BlockSpec((tm, tk), lambda i, j, k: (i, k))
        b_spec = (pl.BlockSpec((None, tn, tk), lambda i, j, k: (k, j, 0)) if blocked
                  else pl.BlockSpec((tn, tk), lambda i, j, k: (j, k)))
        dn = (((1,), (1,)), ((), ()))
    else:
        a_spec = pl.BlockSpec((tk, tm), lambda i, j, k: (k, i))
        b_spec = pl.BlockSpec((tk, tn), lambda i, j, k: (k, j))
        dn = (((0,), (0,)), ((), ()))
        if blocked:
            out_shape = (col_blocks, M, tn)
            out_spec = pl.BlockSpec((None, tm, tn), lambda i, j, k: (j, i, 0))

    def body(a_ref, b_ref, o_ref, acc_ref):
        k = pl.program_id(2)

        @pl.when(k == 0)
        def _():
            acc_ref[...] = jnp.zeros_like(acc_ref)

        acc_ref[...] += lax.dot_general(a_ref[...].astype(BF16), b_ref[...].astype(BF16), dn,
                                        preferred_element_type=F32)

        @pl.when(k == nk - 1)
        def _():
            o_ref[...] = acc_ref[...].astype(o_ref.dtype)

    return pl.pallas_call(
        body, name=name,
        out_shape=jax.ShapeDtypeStruct(out_shape, out_dtype),
        grid=(M // tm, N // tn, nk),
        in_specs=[a_spec, b_spec],
        out_specs=out_spec,
        scratch_shapes=[pltpu.VMEM((tm, tn), F32)],
        compiler_params=_cparams(("parallel", "parallel", "arbitrary")),
    )(a, b)


ROW_TILE = 256


def _adaln_fwd(x, g, sc, sh, name, mix=None, gate=None):
    B, S, D = x.shape
    ts = ROW_TILE
    has_res = mix is not None

    def body(*refs):
        if has_res:
            x_ref, g_ref, sc_ref, sh_ref, mix_ref, gate_ref, h_ref, xr_ref = refs
            xr = x_ref[0] + gate_ref[0] * mix_ref[0]
            xr_ref[0] = xr
        else:
            x_ref, g_ref, sc_ref, sh_ref, h_ref = refs
            xr = x_ref[0]
        r = lax.rsqrt(jnp.mean(xr * xr, axis=-1, keepdims=True) + EPS)
        xn = (xr * r) * g_ref[...]
        h_ref[0] = (xn * (1.0 + sc_ref[0]) + sh_ref[0]).astype(h_ref.dtype)

    tok = pl.BlockSpec((1, ts, D), lambda b, s: (b, s, 0))
    per_b = pl.BlockSpec((1, 1, D), lambda b, s: (b, 0, 0))
    vec = pl.BlockSpec((1, D), lambda b, s: (0, 0))
    in_specs = [tok, vec, per_b, per_b]
    args = [x, g, sc, sh]
    out_shape = [jax.ShapeDtypeStruct((B, S, D), BF16)]
    out_specs = [tok]
    if has_res:
        in_specs += [tok, per_b]
        args += [mix, gate]
        out_shape.append(jax.ShapeDtypeStruct((B, S, D), F32))
        out_specs.append(tok)
    out = pl.pallas_call(
        body, name=name, out_shape=out_shape, grid=(B, S // ts),
        in_specs=in_specs, out_specs=out_specs,
        compiler_params=_cparams(("parallel", "parallel")),
    )(*args)
    return out if has_res else out[0]


def _adaln_bwd(dh, x, g, sc, dres, name, mix=None, gate=None):
    B, S, D = x.shape
    ts = ROW_TILE
    has_res = mix is not None

    def body(*refs):
        if has_res:
            (dh_ref, x_ref, g_ref, sc_ref, dres_ref, mix_ref, gate_ref,
             dx_ref, dsh_ref, dsc_ref, dg_ref, dgate_ref, dmix_ref) = refs
        else:
            (dh_ref, x_ref, g_ref, sc_ref, dres_ref, dx_ref, dsh_ref, dsc_ref, dg_ref) = refs
        b, s = pl.program_id(0), pl.program_id(1)
        xv = x_ref[0]
        dhv = dh_ref[0]
        gv = g_ref[...]
        r = lax.rsqrt(jnp.mean(xv * xv, axis=-1, keepdims=True) + EPS)
        n = xv * r
        xn = n * gv
        dxn = dhv * (1.0 + sc_ref[0])
        dn = dxn * gv
        dx = r * (dn - n * jnp.mean(dn * n, axis=-1, keepdims=True)) + dres_ref[0]
        dx_ref[0] = dx

        @pl.when(s == 0)
        def _():
            dsh_ref[...] = jnp.zeros_like(dsh_ref)
            dsc_ref[...] = jnp.zeros_like(dsc_ref)
            if has_res:
                dgate_ref[...] = jnp.zeros_like(dgate_ref)

        @pl.when((s == 0) & (b == 0))
        def _():
            dg_ref[...] = jnp.zeros_like(dg_ref)

        dsh_ref[0] += jnp.sum(dhv, axis=0, keepdims=True)
        dsc_ref[0] += jnp.sum(dhv * xn, axis=0, keepdims=True)
        dg_ref[...] += jnp.sum(dxn * n, axis=0, keepdims=True)
        if has_res:
            dgate_ref[0] += jnp.sum(dx * mix_ref[0], axis=0, keepdims=True)
            dmix_ref[0] = (dx * gate_ref[0]).astype(dmix_ref.dtype)

    tok = pl.BlockSpec((1, ts, D), lambda b, s: (b, s, 0))
    per_b = pl.BlockSpec((1, 1, D), lambda b, s: (b, 0, 0))
    vec = pl.BlockSpec((1, D), lambda b, s: (0, 0))
    in_specs = [tok, tok, vec, per_b, tok]
    args = [dh, x, g, sc, dres]
    out_shape = [jax.ShapeDtypeStruct((B, S, D), F32), jax.ShapeDtypeStruct((B, 1, D), F32),
                 jax.ShapeDtypeStruct((B, 1, D), F32), jax.ShapeDtypeStruct((1, D), F32)]
    out_specs = [tok, per_b, per_b, vec]
    if has_res:
        in_specs += [tok, per_b]
        args += [mix, gate]
        out_shape += [jax.ShapeDtypeStruct((B, 1, D), F32), jax.ShapeDtypeStruct((B, S, D), BF16)]
        out_specs += [per_b, tok]
    return pl.pallas_call(
        body, name=name, out_shape=out_shape, grid=(B, S // ts),
        in_specs=in_specs, out_specs=out_specs,
        compiler_params=_cparams(("arbitrary", "arbitrary")),
    )(*args)


def _rms_fwd(x, col_blk, n, g, name, n_real=None):
    T = x.shape[0]
    tr = 512
    nr = float(n_real or n)

    def body(x_ref, g_ref, y_ref):
        xv = x_ref[...]
        r = lax.rsqrt(jnp.sum(xv * xv, axis=-1, keepdims=True) / nr + EPS)
        y_ref[...] = ((xv * r) * g_ref[...]).astype(y_ref.dtype)

    return pl.pallas_call(
        body, name=name, out_shape=jax.ShapeDtypeStruct((T, n), BF16), grid=(T // tr,),
        in_specs=[pl.BlockSpec((tr, n), lambda i: (i, col_blk)), pl.BlockSpec((1, n), lambda i: (0, 0))],
        out_specs=pl.BlockSpec((tr, n), lambda i: (i, 0)),
        compiler_params=_cparams(("parallel",)),
    )(x, g)


def _rms_bwd(dy, dy_blk, x, x_blk, n, g, name, out_dtype=BF16):
    T = x.shape[0]
    tr = 512

    def body(dy_ref, x_ref, g_ref, dx_ref, dg_ref):
        xv = x_ref[...]
        dyv = dy_ref[...].astype(F32)
        r = lax.rsqrt(jnp.mean(xv * xv, axis=-1, keepdims=True) + EPS)
        nrm = xv * r
        dn = dyv * g_ref[...]
        dx_ref[...] = (r * (dn - nrm * jnp.mean(dn * nrm, axis=-1, keepdims=True))).astype(dx_ref.dtype)

        @pl.when(pl.program_id(0) == 0)
        def _():
            dg_ref[...] = jnp.zeros_like(dg_ref)

        dg_ref[...] += jnp.sum(dyv * nrm, axis=0, keepdims=True)

    return pl.pallas_call(
        body, name=name,
        out_shape=[jax.ShapeDtypeStruct((T, n), out_dtype), jax.ShapeDtypeStruct((1, n), F32)],
        grid=(T // tr,),
        in_specs=[pl.BlockSpec((tr, n), lambda i: (i, dy_blk)), pl.BlockSpec((tr, n), lambda i: (i, x_blk)),
                  pl.BlockSpec((1, n), lambda i: (0, 0))],
        out_specs=[pl.BlockSpec((tr, n), lambda i: (i, 0)), pl.BlockSpec((1, n), lambda i: (0, 0))],
        compiler_params=_cparams(("arbitrary",)),
    )(dy, x, g)


def _swiglu_fwd(gu, name):
    T = gu.shape[0]
    tr, tc = 512, 1408
    nc = D_FF // tc

    def body(g_ref, u_ref, a_ref):
        gv = g_ref[...]
        a_ref[...] = (gv * jax.nn.sigmoid(gv) * u_ref[...]).astype(a_ref.dtype)

    return pl.pallas_call(
        body, name=name, out_shape=jax.ShapeDtypeStruct((T, D_FF), BF16), grid=(T // tr, nc),
        in_specs=[pl.BlockSpec((tr, tc), lambda i, j: (i, j)), pl.BlockSpec((tr, tc), lambda i, j: (i, j + nc))],
        out_specs=pl.BlockSpec((tr, tc), lambda i, j: (i, j)),
        compiler_params=_cparams(("parallel", "parallel")),
    )(gu, gu)


def _swiglu_bwd(da, gu, name):
    T = gu.shape[0]
    tr, tc = 512, 1408
    nc = D_FF // tc

    def body(da_ref, g_ref, u_ref, dgu_ref):
        j = pl.program_id(1)
        gv, uv, dav = g_ref[...], u_ref[...], da_ref[...]
        sg = jax.nn.sigmoid(gv)

        @pl.when(j < nc)
        def _():
            dgu_ref[...] = (dav * uv * (sg * (1.0 + gv * (1.0 - sg)))).astype(dgu_ref.dtype)

        @pl.when(j >= nc)
        def _():
            dgu_ref[...] = (dav * (gv * sg)).astype(dgu_ref.dtype)

    return pl.pallas_call(
        body, name=name, out_shape=jax.ShapeDtypeStruct((T, 2 * D_FF), BF16), grid=(T // tr, 2 * nc),
        in_specs=[pl.BlockSpec((tr, tc), lambda i, j: (i, j % nc)),
                  pl.BlockSpec((tr, tc), lambda i, j: (i, j % nc)),
                  pl.BlockSpec((tr, tc), lambda i, j: (i, j % nc + nc))],
        out_specs=pl.BlockSpec((tr, tc), lambda i, j: (i, j)),
        compiler_params=_cparams(("parallel", "parallel")),
    )(da, gu, gu)


def _final_loss(x1, f, g2, gf, target, name):
    B, S, D = x1.shape
    ts = ROW_TILE

    def body(x1_ref, f_ref, g2_ref, gf_ref, t_ref, dx_ref, df_ref, dg2_ref, dgf_ref, loss_ref):
        b, s = pl.program_id(0), pl.program_id(1)
        fv = f_ref[0]
        g2v = g2_ref[0]
        gfv = gf_ref[...]
        x2 = x1_ref[0] + g2v * fv
        r = lax.rsqrt(jnp.mean(x2 * x2, axis=-1, keepdims=True) + EPS)
        n = x2 * r
        e = n * gfv - t_ref[0]
        dy = e * (1.0 / D)
        dn = dy * gfv
        dx = r * (dn - n * jnp.mean(dn * n, axis=-1, keepdims=True))
        dx_ref[0] = dx
        df_ref[0] = (dx * g2v).astype(df_ref.dtype)

        @pl.when(s == 0)
        def _():
            dg2_ref[...] = jnp.zeros_like(dg2_ref)

        @pl.when((s == 0) & (b == 0))
        def _():
            dgf_ref[...] = jnp.zeros_like(dgf_ref)
            loss_ref[...] = jnp.zeros_like(loss_ref)

        dg2_ref[0] += jnp.sum(dx * fv, axis=0, keepdims=True)
        dgf_ref[...] += jnp.sum(dy * n, axis=0, keepdims=True)
        loss_ref[...] += 0.5 * jnp.sum(jnp.mean(e * e, axis=-1, keepdims=True), axis=0, keepdims=True)

    tok = pl.BlockSpec((1, ts, D), lambda b, s: (b, s, 0))
    per_b = pl.BlockSpec((1, 1, D), lambda b, s: (b, 0, 0))
    vec = pl.BlockSpec((1, D), lambda b, s: (0, 0))
    return pl.pallas_call(
        body, name=name,
        out_shape=[jax.ShapeDtypeStruct((B, S, D), F32), jax.ShapeDtypeStruct((B, S, D), BF16),
                   jax.ShapeDtypeStruct((B, 1, D), F32), jax.ShapeDtypeStruct((1, D), F32),
                   jax.ShapeDtypeStruct((1, LANES), F32)],
        grid=(B, S // ts),
        in_specs=[tok, tok, per_b, vec, tok],
        out_specs=[tok, tok, per_b, vec, pl.BlockSpec((1, LANES), lambda b, s: (0, 0))],
        compiler_params=_cparams(("arbitrary", "arbitrary")),
    )(x1, f, g2, gf, target)


def _rope_tables():
    half = ROPE_DIM // 2
    inv = ROPE_THETA ** (-jnp.arange(half, dtype=F32) / half)
    ang = jnp.arange(SEQ, dtype=F32)[:, None] * inv[None, :]
    cos, sin = jnp.cos(ang), jnp.sin(ang)
    one = jnp.ones((SEQ, NOPE_DIM), F32)
    zero = jnp.zeros((SEQ, NOPE_DIM), F32)
    cs = jnp.concatenate([one, cos, cos, one[:, :LANES - NOPE_DIM - ROPE_DIM]], axis=1)
    sn = jnp.concatenate([zero, -sin, sin, zero[:, :LANES - NOPE_DIM - ROPE_DIM]], axis=1)
    return cs, sn


def _rope_group(t, cs, sn):
    half = ROPE_DIM // 2
    lane = lax.broadcasted_iota(jnp.int32, t.shape, 1)
    partner = jnp.where(lane < NOPE_DIM + half, pltpu.roll(t, LANES - half, 1), pltpu.roll(t, half, 1))
    return t * cs + partner * sn


def _rope_apply(t, cs, sn, out_dtype, name, add=None, add_blk=0):
    B, S, W = t.shape
    G = W // LANES
    ts = ROW_TILE

    def body(*refs):
        if add is None:
            t_ref, cs_ref, sn_ref, o_ref = refs
            for gi in range(G):
                sl = slice(gi * LANES, (gi + 1) * LANES)
                o_ref[0, :, sl] = _rope_group(t_ref[0, :, sl], cs_ref[...], sn_ref[...]).astype(o_ref.dtype)
        else:
            t_ref, a_ref, cs_ref, sn_ref, o_ref = refs
            ra = _rope_group(a_ref[0], cs_ref[...], sn_ref[...])
            for gi in range(G):
                sl = slice(gi * LANES, (gi + 1) * LANES)
                o_ref[0, :, sl] = (t_ref[0, :, sl] + ra).astype(o_ref.dtype)

    tok = pl.BlockSpec((1, ts, W), lambda b, s: (b, s, 0))
    tab = pl.BlockSpec((ts, LANES), lambda b, s: (s, 0))
    in_specs, args = [tok], [t]
    if add is not None:
        in_specs.append(pl.BlockSpec((1, ts, LANES), lambda b, s: (b, s, add_blk)))
        args.append(add)
    in_specs += [tab, tab]
    args += [cs, sn]
    return pl.pallas_call(
        body, name=name, out_shape=jax.ShapeDtypeStruct((B, S, W), out_dtype), grid=(B, S // ts),
        in_specs=in_specs, out_specs=tok, compiler_params=_cparams(("parallel", "parallel")),
    )(*args)


def _krope_bwd(dkc, cs, sn_neg, name):
    B, S, W = dkc.shape
    G = W // LANES
    ts = ROW_TILE

    def body(d_ref, cs_ref, sn_ref, o_ref):
        acc = d_ref[0, :, 0:LANES]
        for gi in range(1, G):
            acc = acc + d_ref[0, :, gi * LANES:(gi + 1) * LANES]
        lane = lax.broadcasted_iota(jnp.int32, acc.shape, 1)
        rot = (lane >= NOPE_DIM) & (lane < NOPE_DIM + ROPE_DIM)
        acc = jnp.where(rot, acc, 0.0)
        o_ref[0] = _rope_group(acc, cs_ref[...], sn_ref[...]).astype(o_ref.dtype)

    tab = pl.BlockSpec((ts, LANES), lambda b, s: (s, 0))
    return pl.pallas_call(
        body, name=name, out_shape=jax.ShapeDtypeStruct((B, S, LANES), BF16), grid=(B, S // ts),
        in_specs=[pl.BlockSpec((1, ts, W), lambda b, s: (b, s, 0)), tab, tab],
        out_specs=pl.BlockSpec((1, ts, LANES), lambda b, s: (b, s, 0)),
        compiler_params=_cparams(("parallel", "parallel")),
    )(dkc, cs, sn_neg)


def _t5_bucket(dist):
    max_exact = N_BUCKETS // 2
    d = np.maximum(dist, 1).astype(np.float64)
    large = max_exact + (np.log(d / max_exact) / np.log(MAX_DISTANCE / max_exact)
                         * (N_BUCKETS - max_exact)).astype(np.int64)
    large = np.minimum(large, N_BUCKETS - 1)
    return np.where(dist < max_exact, dist, large).astype(np.int32)


def _band_buckets(dilation):
    a = np.arange(BLK)[:, None]
    bk = np.arange(2 * BLK)[None, :]
    steps = BLK + a - bk
    return _t5_bucket(np.clip(steps, 0, SPAN) * dilation)


def _head_mask(shape, hh):
    lane = lax.broadcasted_iota(jnp.int32, shape, 1)
    return (lane >= hh * HEAD_DIM) & (lane < (hh + 1) * HEAD_DIM)


def _dot_nt(a, b):
    return lax.dot_general(a, b, (((1,), (1,)), ((), ())), preferred_element_type=F32)


def _dot_tn(a, b):
    return lax.dot_general(a, b, (((0,), (0,)), ((), ())), preferred_element_type=F32)


def _dot_nn(a, b):
    return lax.dot_general(a, b, (((1,), (0,)), ((), ())), preferred_element_type=F32)


def _band_valid():
    a = lax.broadcasted_iota(jnp.int32, (BLK, BLK), 0)
    bk = lax.broadcasted_iota(jnp.int32, (BLK, BLK), 1)
    return bk >= a, bk <= a


def _dil_fwd(qkv, bias, branch, dilation, name):
    B, S, _ = qkv.shape
    d = dilation
    n = S // d
    nb = n // BLK
    qkv_v = qkv.reshape(B, n, d * P_QKV)
    npair = N_HEADS // 2

    def body(cur_ref, prev_ref, bias_ref, o_ref, lse_ref):
        i = pl.program_id(2)
        vprev, vcur = _band_valid()
        vprev = vprev & (i > 0)
        for p in range(npair):
            sl = slice(p * LANES, (p + 1) * LANES)
            q = cur_ref[0, :, sl]
            kc = cur_ref[0, :, D_A + p * LANES:D_A + (p + 1) * LANES]
            vc = cur_ref[0, :, 2 * D_A + p * LANES:2 * D_A + (p + 1) * LANES]
            kp = prev_ref[0, :, D_A + p * LANES:D_A + (p + 1) * LANES]
            vp = prev_ref[0, :, 2 * D_A + p * LANES:2 * D_A + (p + 1) * LANES]
            o_pair = jnp.zeros((BLK, LANES), F32)
            lse_pair = jnp.zeros((BLK, LANES), F32)
            for hh in range(2):
                h = 2 * p + hh
                hm = _head_mask((BLK, LANES), hh)
                qm = jnp.where(hm, q, jnp.zeros_like(q))
                s_p = _dot_nt(qm, kp) * DIL_SCALE + bias_ref[h, :, 0:BLK]
                s_c = _dot_nt(qm, kc) * DIL_SCALE + bias_ref[h, :, BLK:2 * BLK]
                s_p = jnp.where(vprev, s_p, NEG)
                s_c = jnp.where(vcur, s_c, NEG)
                m = jnp.maximum(jnp.max(s_p, axis=-1, keepdims=True), jnp.max(s_c, axis=-1, keepdims=True))
                e_p = jnp.exp(s_p - m)
                e_c = jnp.exp(s_c - m)
                l = jnp.sum(e_p, axis=-1, keepdims=True) + jnp.sum(e_c, axis=-1, keepdims=True)
                vpm = jnp.where(hm, vp, jnp.zeros_like(vp))
                vcm = jnp.where(hm, vc, jnp.zeros_like(vc))
                o_h = _dot_nn(e_p.astype(BF16), vpm) + _dot_nn(e_c.astype(BF16), vcm)
                o_pair = o_pair + o_h / l
                lse_pair = jnp.where(hm, m + jnp.log(l), lse_pair)
            o_ref[0, :, sl] = o_pair
            lse_ref[0, :, sl] = lse_pair

    cur = pl.BlockSpec((1, BLK, P_QKV), lambda b, r, i: (b, i, r))
    prev = pl.BlockSpec((1, BLK, P_QKV), lambda b, r, i: (b, jnp.maximum(i - 1, 0), r))
    out = pl.BlockSpec((1, BLK, D_A), lambda b, r, i: (b, i, r))
    o, lse = pl.pallas_call(
        body, name=name,
        out_shape=[jax.ShapeDtypeStruct((B, n, d * D_A), F32)] * 2,
        grid=(B, d, nb),
        in_specs=[cur, prev,
                  pl.BlockSpec((None, N_HEADS, BLK, 2 * BLK), lambda b, r, i: (branch, 0, 0, 0))],
        out_specs=[out, out],
        compiler_params=_cparams(("parallel", "parallel", "arbitrary")),
    )(qkv_v, qkv_v, bias)
    return o.reshape(B, S, D_A), lse.reshape(B, S, D_A)


def _dil_merge(os_, lses, name):
    B, S, W = os_[0].shape
    ts = 512

    def body(o0, o1, o2, l0, l1, l2, out_ref, L_ref):
        a0, a1, a2 = l0[0], l1[0], l2[0]
        m = jnp.maximum(jnp.maximum(a0, a1), a2)
        e0, e1, e2 = jnp.exp(a0 - m), jnp.exp(a1 - m), jnp.exp(a2 - m)
        ssum = e0 + e1 + e2
        out_ref[0] = (e0 * o0[0] + e1 * o1[0] + e2 * o2[0]) / ssum
        L_ref[0] = m + jnp.log(ssum)

    tok = pl.BlockSpec((1, ts, W), lambda b, s: (b, s, 0))
    return pl.pallas_call(
        body, name=name, out_shape=[jax.ShapeDtypeStruct((B, S, W), F32)] * 2, grid=(B, S // ts),
        in_specs=[tok] * 6, out_specs=[tok, tok], compiler_params=_cparams(("parallel", "parallel")),
    )(*os_, *lses)


def _dil_bwd(qkv, do, out_a, L, bias, branch, dilation, name):
    B, S, _ = qkv.shape
    d = dilation
    n = S // d
    nb = n // BLK
    qkv_v = qkv.reshape(B, n, d * P_QKV)
    do_v = do.reshape(B, n, d * D_A)
    oa_v = out_a.reshape(B, n, d * D_A)
    L_v = L.reshape(B, n, d * D_A)
    npair = N_HEADS // 2
    multi = nb > 1

    def body(*refs):
        if multi:
            (cur_ref, prev_ref, next_ref, do_ref, don_ref, oa_ref, oan_ref, L_ref, Ln_ref, bias_ref,
             dqkv_ref, dbias_ref) = refs
        else:
            cur_ref, do_ref, oa_ref, L_ref, bias_ref, dqkv_ref, dbias_ref = refs
        b, r, i = pl.program_id(0), pl.program_id(1), pl.program_id(2)

        @pl.when((b == 0) & (r == 0) & (i == 0))
        def _():
            dbias_ref[...] = jnp.zeros_like(dbias_ref)

        vprev, vcur = _band_valid()
        has_prev = i > 0
        has_next = i < nb - 1
        for p in range(npair):
            sl = slice(p * LANES, (p + 1) * LANES)
            ksl = slice(D_A + p * LANES, D_A + (p + 1) * LANES)
            vsl = slice(2 * D_A + p * LANES, 2 * D_A + (p + 1) * LANES)
            q, kc, vc = cur_ref[0, :, sl], cur_ref[0, :, ksl], cur_ref[0, :, vsl]
            dov = do_ref[0, :, sl]
            dd = dov.astype(F32) * oa_ref[0, :, sl]
            Lv = L_ref[0, :, sl]
            if multi:
                kp, vp = prev_ref[0, :, ksl], prev_ref[0, :, vsl]
                qn = next_ref[0, :, sl]
                donv = don_ref[0, :, sl]
                ddn = donv.astype(F32) * oan_ref[0, :, sl]
                Lnv = Ln_ref[0, :, sl]
            dq_pair = jnp.zeros((BLK, LANES), F32)
            dk_pair = jnp.zeros((BLK, LANES), F32)
            dv_pair = jnp.zeros((BLK, LANES), F32)
            for hh in range(2):
                h = 2 * p + hh
                hm = _head_mask((BLK, LANES), hh)
                zero = jnp.zeros_like(q)
                qm = jnp.where(hm, q, zero)
                dom = jnp.where(hm, dov, zero)
                delta = jnp.sum(jnp.where(hm, dd, 0.0), axis=-1, keepdims=True)
                lse = Lv[:, hh * HEAD_DIM:hh * HEAD_DIM + 1]
                s_c = _dot_nt(qm, kc) * DIL_SCALE + bias_ref[h, :, BLK:2 * BLK]
                p_c = jnp.where(vcur, jnp.exp(s_c - lse), 0.0)
                ds_c = p_c * (_dot_nt(dom, vc) - delta)
                ds_cb = ds_c.astype(BF16)
                dq_h = _dot_nn(ds_cb, kc)
                dk_h = _dot_tn(ds_cb, qm)
                dv_h = _dot_tn(p_c.astype(BF16), dom)
                dbias_ref[h, :, BLK:2 * BLK] += ds_c
                if multi:
                    s_p = _dot_nt(qm, kp) * DIL_SCALE + bias_ref[h, :, 0:BLK]
                    p_p = jnp.where(vprev & has_prev, jnp.exp(s_p - lse), 0.0)
                    ds_p = p_p * (_dot_nt(dom, vp) - delta)
                    dq_h = dq_h + _dot_nn(ds_p.astype(BF16), kp)
                    dbias_ref[h, :, 0:BLK] += ds_p
                    qnm = jnp.where(hm, qn, zero)
                    donm = jnp.where(hm, donv, zero)
                    delta_n = jnp.sum(jnp.where(hm, ddn, 0.0), axis=-1, keepdims=True)
                    lse_n = Lnv[:, hh * HEAD_DIM:hh * HEAD_DIM + 1]
                    s_n = _dot_nt(qnm, kc) * DIL_SCALE + bias_ref[h, :, 0:BLK]
                    p_n = jnp.where(vprev & has_next, jnp.exp(s_n - lse_n), 0.0)
                    ds_n = p_n * (_dot_nt(donm, vc) - delta_n)
                    dk_h = dk_h + _dot_tn(ds_n.astype(BF16), qnm)
                    dv_h = dv_h + _dot_tn(p_n.astype(BF16), donm)
                dq_pair = dq_pair + jnp.where(hm, dq_h, 0.0) * DIL_SCALE
                dk_pair = dk_pair + jnp.where(hm, dk_h, 0.0) * DIL_SCALE
                dv_pair = dv_pair + jnp.where(hm, dv_h, 0.0)
            dqkv_ref[0, :, sl] = dq_pair
            dqkv_ref[0, :, ksl] = dk_pair
            dqkv_ref[0, :, vsl] = dv_pair

    def at(off):
        return lambda b, r, i: (b, jnp.clip(i + off, 0, nb - 1), r)

    qkv_spec = lambda off: pl.BlockSpec((1, BLK, P_QKV), at(off))
    da_spec = lambda off: pl.BlockSpec((1, BLK, D_A), at(off))
    bias_spec = pl.BlockSpec((None, N_HEADS, BLK, 2 * BLK), lambda b, r, i: (branch, 0, 0, 0))
    dbias_spec = pl.BlockSpec((N_HEADS, BLK, 2 * BLK), lambda b, r, i: (0, 0, 0))
    if multi:
        in_specs = [qkv_spec(0), qkv_spec(-1), qkv_spec(1), da_spec(0), da_spec(1), da_spec(0), da_spec(1),
                    da_spec(0), da_spec(1), bias_spec]
        args = [qkv_v, qkv_v, qkv_v, do_v, do_v, oa_v, oa_v, L_v, L_v, bias]
    else:
        in_specs = [qkv_spec(0), da_spec(0), da_spec(0), da_spec(0), bias_spec]
        args = [qkv_v, do_v, oa_v, L_v, bias]
    dqkv, dbias = pl.pallas_call(
        body, name=name,
        out_shape=[jax.ShapeDtypeStruct((B, n, d * P_QKV), F32),
                   jax.ShapeDtypeStruct((N_HEADS, BLK, 2 * BLK), F32)],
        grid=(B, d, nb),
        in_specs=in_specs,
        out_specs=[qkv_spec(0), dbias_spec],
        compiler_params=_cparams(("arbitrary", "arbitrary", "arbitrary")),
    )(*args)
    return dqkv.reshape(B, S, P_QKV), dbias


def _sum3_bf16(a, b, c, name):
    B, S, W = a.shape
    ts = 512

    def body(a_ref, b_ref, c_ref, o_ref):
        o_ref[...] = (a_ref[...] + b_ref[...] + c_ref[...]).astype(o_ref.dtype)

    tok = pl.BlockSpec((1, ts, W), lambda b, s: (b, s, 0))
    return pl.pallas_call(
        body, name=name, out_shape=jax.ShapeDtypeStruct((B, S, W), BF16), grid=(B, S // ts),
        in_specs=[tok] * 3, out_specs=tok, compiler_params=_cparams(("parallel", "parallel")),
    )(a, b, c)


def _bias_tables(rel_bias, buckets, name):
    nbr = buckets.shape[0]

    def body(rb_ref, bk_ref, o_ref):
        h = pl.program_id(1)
        tab = bk_ref[0]

        def step(bkt, acc):
            return jnp.where(tab == bkt, rb_ref[bkt, h], acc)

        o_ref[0, 0] = lax.fori_loop(0, N_BUCKETS, step, jnp.zeros((BLK, 2 * BLK), F32))

    return pl.pallas_call(
        body, name=name, out_shape=jax.ShapeDtypeStruct((nbr, N_HEADS, BLK, 2 * BLK), F32),
        grid=(nbr, N_HEADS),
        in_specs=[pl.BlockSpec(memory_space=pltpu.SMEM),
                  pl.BlockSpec((1, BLK, 2 * BLK), lambda i, h: (i, 0, 0))],
        out_specs=pl.BlockSpec((1, 1, BLK, 2 * BLK), lambda i, h: (i, h, 0, 0)),
        compiler_params=_cparams(("parallel", "arbitrary")),
    )(rel_bias, buckets)


def _bias_grad(dbias_list, buckets, name):
    nbr = len(dbias_list)

    def body(*refs):
        d_refs, bk_ref, o_ref = refs[:nbr], refs[nbr], refs[nbr + 1]
        lane = lax.broadcasted_iota(jnp.int32, (1, LANES), 1)
        for h in range(N_HEADS):
            def step(bkt, acc):
                tot = jnp.zeros((1, 1), F32)
                for bi in range(nbr):
                    sel = jnp.where(bk_ref[bi] == bkt, d_refs[bi][h], 0.0)
                    tot = tot + jnp.sum(jnp.sum(sel, axis=1, keepdims=True), axis=0, keepdims=True)
                return acc + jnp.where(lane == bkt, tot, 0.0)

            o_ref[h:h + 1, :] = lax.fori_loop(0, N_BUCKETS, step, jnp.zeros((1, LANES), F32))

    band = pl.BlockSpec((N_HEADS, BLK, 2 * BLK), lambda i: (0, 0, 0))
    return pl.pallas_call(
        body, name=name, out_shape=jax.ShapeDtypeStruct((N_HEADS, LANES), F32), grid=(1,),
        in_specs=[band] * nbr + [pl.BlockSpec((nbr, BLK, 2 * BLK), lambda i: (0, 0, 0))],
        out_specs=pl.BlockSpec((N_HEADS, LANES), lambda i: (0, 0)),
        compiler_params=_cparams(("arbitrary",)),
    )(*dbias_list, buckets)


MLA_TQ = 256
MLA_TK = 256


def _causal(i, j, tq, tk):
    qpos = i * tq + lax.broadcasted_iota(jnp.int32, (tq, tk), 0)
    kpos = j * tk + lax.broadcasted_iota(jnp.int32, (tq, tk), 1)
    return qpos >= kpos


def _mla_fwd(q, k, v, name):
    B, S, _ = q.shape
    tq, tk = MLA_TQ, MLA_TK
    npair = N_HEADS // 2

    def body(q_ref, k_ref, v_ref, o_ref, lse_ref, m_s, l_s, acc_s):
        i = pl.program_id(2)
        o_pair = jnp.zeros((tq, LANES), F32)
        lse_pair = jnp.zeros((tq, LANES), F32)
        for hh in range(2):
            hsl = slice(hh * LANES, (hh + 1) * LANES)
            hm_v = _head_mask((tk, LANES), hh)
            qh = q_ref[0, :, hsl]
            m_s[...] = jnp.full_like(m_s, NEG)
            l_s[...] = jnp.zeros_like(l_s)
            acc_s[...] = jnp.zeros_like(acc_s)

            def step(j, carry):
                kj = k_ref[0, pl.ds(pl.multiple_of(j * tk, tk), tk), hsl]
                vj = v_ref[0, pl.ds(pl.multiple_of(j * tk, tk), tk), :]
                vj = jnp.where(hm_v, vj, jnp.zeros_like(vj))
                s = _dot_nt(qh, kj) * MLA_SCALE
                s = jnp.where(_causal(i, j, tq, tk), s, NEG)
                m_old = m_s[...]
                m_new = jnp.maximum(m_old, jnp.max(s, axis=-1, keepdims=True))
                alpha = jnp.exp(m_old - m_new)
                e = jnp.exp(s - m_new)
                l_s[...] = alpha * l_s[...] + jnp.sum(e, axis=-1, keepdims=True)
                acc_s[...] = alpha * acc_s[...] + _dot_nn(e.astype(BF16), vj)
                m_s[...] = m_new
                return carry

            lax.fori_loop(0, (i * tq) // tk + tq // tk, step, 0)
            o_pair = o_pair + acc_s[...] / l_s[...]
            lse_pair = jnp.where(_head_mask((tq, LANES), hh), m_s[...] + jnp.log(l_s[...]), lse_pair)
        o_ref[0] = o_pair
        lse_ref[0] = lse_pair

    o_spec = pl.BlockSpec((1, tq, LANES), lambda b, p, i: (b, i, p))
    return pl.pallas_call(
        body, name=name,
        out_shape=[jax.ShapeDtypeStruct((B, S, D_B), F32)] * 2,
        grid=(B, npair, S // tq),
        in_specs=[pl.BlockSpec((1, tq, 2 * LANES), lambda b, p, i: (b, i, p)),
                  pl.BlockSpec((1, S, 2 * LANES), lambda b, p, i: (b, 0, p)),
                  pl.BlockSpec((1, S, LANES), lambda b, p, i: (b, 0, p))],
        out_specs=[o_spec, o_spec],
        scratch_shapes=[pltpu.VMEM((tq, 1), F32), pltpu.VMEM((tq, 1), F32), pltpu.VMEM((tq, LANES), F32)],
        compiler_params=_cparams(("parallel", "parallel", "arbitrary")),
    )(q, k, v)


def _mla_bwd_dq(q, k, v, do, o, lse, name):
    B, S, _ = q.shape
    tq, tk = MLA_TQ, MLA_TK
    npair = N_HEADS // 2

    def body(q_ref, k_ref, v_ref, do_ref, o_ref, lse_ref, dq_ref, acc_s):
        i = pl.program_id(2)
        dov = do_ref[0]
        dd = dov.astype(F32) * o_ref[0]
        for hh in range(2):
            hsl = slice(hh * LANES, (hh + 1) * LANES)
            hm = _head_mask((tq, LANES), hh)
            qh = q_ref[0, :, hsl]
            dom = jnp.where(hm, dov, jnp.zeros_like(dov))
            delta = jnp.sum(jnp.where(hm, dd, 0.0), axis=-1, keepdims=True)
            lse_h = lse_ref[0, :, hh * HEAD_DIM:hh * HEAD_DIM + 1]
            acc_s[...] = jnp.zeros_like(acc_s)

            def step(j, carry):
                kj = k_ref[0, pl.ds(pl.multiple_of(j * tk, tk), tk), hsl]
                vj = v_ref[0, pl.ds(pl.multiple_of(j * tk, tk), tk), :]
                s = _dot_nt(qh, kj) * MLA_SCALE
                pr = jnp.where(_causal(i, j, tq, tk), jnp.exp(s - lse_h), 0.0)
                ds = pr * (_dot_nt(dom, vj) - delta)
                acc_s[...] += _dot_nn(ds.astype(BF16), kj)
                return carry

            lax.fori_loop(0, (i * tq) // tk + tq // tk, step, 0)
            dq_ref[0, :, hsl] = acc_s[...] * MLA_SCALE

    return pl.pallas_call(
        body, name=name,
        out_shape=jax.ShapeDtypeStruct((B, S, N_HEADS * LANES), F32),
        grid=(B, npair, S // tq),
        in_specs=[pl.BlockSpec((1, tq, 2 * LANES), lambda b, p, i: (b, i, p)),
                  pl.BlockSpec((1, S, 2 * LANES), lambda b, p, i: (b, 0, p)),
                  pl.BlockSpec((1, S, LANES), lambda b, p, i: (b, 0, p)),
                  pl.BlockSpec((1, tq, LANES), lambda b, p, i: (b, i, p)),
                  pl.BlockSpec((1, tq, LANES), lambda b, p, i: (b, i, p)),
                  pl.BlockSpec((1, tq, LANES), lambda b, p, i: (b, i, p))],
        out_specs=pl.BlockSpec((1, tq, 2 * LANES), lambda b, p, i: (b, i, p)),
        scratch_shapes=[pltpu.VMEM((tq, LANES), F32)],
        compiler_params=_cparams(("parallel", "parallel", "arbitrary")),
    )(q, k, v, do, o, lse)


def _mla_bwd_dkv(q, k, v, do, o, lse, name):
    B, S, _ = q.shape
    tq, tk = MLA_TQ, MLA_TK
    npair = N_HEADS // 2
    nq = S // tq

    def body(q_ref, k_ref, v_ref, do_ref, o_ref, lse_ref, dk_ref, dv_ref, dk_s, dv_s):
        j = pl.program_id(2)
        vj = v_ref[0]
        dv_s[...] = jnp.zeros_like(dv_s)
        for hh in range(2):
            hsl = slice(hh * LANES, (hh + 1) * LANES)
            hm = _head_mask((tq, LANES), hh)
            kj = k_ref[0, :, hsl]
            dk_s[...] = jnp.zeros_like(dk_s)

            def step(i, carry):
                rows = pl.ds(pl.multiple_of(i * tq, tq), tq)
                qi = q_ref[0, rows, hsl]
                dov = do_ref[0, rows, :]
                dom = jnp.where(hm, dov, jnp.zeros_like(dov))
                delta = jnp.sum(jnp.where(hm, dov.astype(F32) * o_ref[0, rows, :], 0.0), axis=-1, keepdims=True)
                lse_h = lse_ref[0, rows, hh * HEAD_DIM:hh * HEAD_DIM + 1]
                s = _dot_nt(qi, kj) * MLA_SCALE
                pr = jnp.where(_causal(i, j, tq, tk), jnp.exp(s - lse_h), 0.0)
                dv_s[...] += _dot_tn(pr.astype(BF16), dom)
                ds = pr * (_dot_nt(dom, vj) - delta)
                dk_s[...] += _dot_tn(ds.astype(BF16), qi)
                return carry

            lax.fori_loop((j * tk) // tq, nq, step, 0)
            dk_ref[0, :, hsl] = dk_s[...] * MLA_SCALE
        dv_ref[0] = dv_s[...]

    return pl.pallas_call(
        body, name=name,
        out_shape=[jax.ShapeDtypeStruct((B, S, N_HEADS * LANES), F32), jax.ShapeDtypeStruct((B, S, D_B), F32)],
        grid=(B, npair, S // tk),
        in_specs=[pl.BlockSpec((1, S, 2 * LANES), lambda b, p, j: (b, 0, p)),
                  pl.BlockSpec((1, tk, 2 * LANES), lambda b, p, j: (b, j, p)),
                  pl.BlockSpec((1, tk, LANES), lambda b, p, j: (b, j, p)),
                  pl.BlockSpec((1, S, LANES), lambda b, p, j: (b, 0, p)),
                  pl.BlockSpec((1, S, LANES), lambda b, p, j: (b, 0, p)),
                  pl.BlockSpec((1, S, LANES), lambda b, p, j: (b, 0, p))],
        out_specs=[pl.BlockSpec((1, tk, 2 * LANES), lambda b, p, j: (b, j, p)),
                   pl.BlockSpec((1, tk, LANES), lambda b, p, j: (b, j, p))],
        scratch_shapes=[pltpu.VMEM((tk, LANES), F32), pltpu.VMEM((tk, LANES), F32)],
        compiler_params=_cparams(("parallel", "parallel", "arbitrary")),
    )(q, k, v, do, o, lse)


def _local_step(x, target, mod, wts, gains, rel_bias):
    B, S, D = x.shape
    T = B * S
    sh1, sc1, g1, sh2, sc2, g2 = [mod[:, i * D:(i + 1) * D].reshape(B, 1, D) for i in range(N_MOD)]
    cs, sn = _rope_tables()
    buckets = np.stack([_band_buckets(d) for d in DILATIONS])
    buckets_dev = jnp.asarray(buckets)
    bias = _bias_tables(rel_bias, buckets_dev, "rel_bias_tables")
    w_in = wts["w_in"]

    h1 = _adaln_fwd(x, gains["g_norm1"], sc1, sh1, "adaln1_fwd")
    h1f = h1.reshape(T, D)
    qkv = _mm(h1f, w_in[:, :P_QKV], "nn", BF16, "mm_qkv").reshape(B, S, P_QKV)
    rest = _mm(h1f, w_in[:, P_QKV:], "nn", F32, "mm_rest")
    o_d, lse_d = [], []
    for i, d in enumerate(DILATIONS):
        o_i, lse_i = _dil_fwd(qkv, bias, i, d, f"dil_fwd_{d}")
        o_d.append(o_i)
        lse_d.append(lse_i)
    out_a, lse_a = _dil_merge(o_d, lse_d, "dil_merge")
    cqn = _rms_fwd(rest, 1, Q_LORA, gains["g_cq"], "rms_cq_fwd")
    ckvn = _rms_fwd(rest, 0, KV_LORA, gains["g_ckv"], "rms_ckv_fwd")
    rest3 = rest.reshape(B, S, P_REST)
    q_raw = _mm(cqn, wts["w_uq"], "nn", F32, "mm_uq").reshape(B, S, N_HEADS * LANES)
    qc = _rope_apply(q_raw, cs, sn, BF16, "rope_q")
    kn_raw = _mm(ckvn, wts["w_kv"][:, :N_HEADS * LANES], "nn", F32, "mm_uk").reshape(B, S, N_HEADS * LANES)
    kc = _rope_apply(kn_raw, cs, sn, BF16, "rope_k", add=rest3, add_blk=KV_LORA // LANES)
    v = _mm(ckvn, wts["w_kv"][:, N_HEADS * LANES:], "nn", BF16, "mm_uv").reshape(B, S, D_B)
    out_b, lse_b = _mla_fwd(qc, kc, v, "mla_fwd")
    out_af, out_bf = out_a.reshape(T, D_A), out_b.reshape(T, D_B)
    ya = _rms_fwd(out_af, 0, D_A, gains["g_out_a"], "rms_outa_fwd")
    yb = _rms_fwd(out_bf, 0, D_B, gains["g_out_b"], "rms_outb_fwd")
    y = jnp.concatenate([ya, yb], axis=1)
    mix = _mm(y, wts["w_out"], "nn", F32, "mm_out").reshape(B, S, D)
    h2, x1 = _adaln_fwd(x, gains["g_norm2"], sc2, sh2, "adaln2_fwd", mix=mix, gate=g1)
    h2f = h2.reshape(T, D)
    gu = _mm(h2f, wts["w_ffn_in"], "nn", F32, "mm_ffn_in", col_blocks=N_CHIP)
    act = _swiglu_fwd(gu, "swiglu_fwd")
    f = _mm(act, wts["w_ffn_out"], "nn", F32, "mm_ffn_out").reshape(B, S, D)
    dx2, df, dg2, dg_final, loss = _final_loss(x1, f, g2, gains["g_final"], target, "final_loss")

    dff = df.reshape(T, D)
    da = _mm(dff, wts["w_ffn_out"], "nt", F32, "mm_ffn_out_dx")
    gw_ffn_out = _mm(act, dff, "tn", F32, "mm_ffn_out_dw")
    dgu = _swiglu_bwd(da, gu, "swiglu_bwd")
    dh2 = _mm(dgu, wts["w_ffn_in"], "nt", F32, "mm_ffn_in_dx", col_blocks=N_CHIP).reshape(B, S, D)
    gw_ffn_in = _mm(h2f, dgu, "tn", F32, "mm_ffn_in_dw", col_blocks=N_CHIP)
    dx1, dsh2, dsc2, dg_norm2, dg1, dmix = _adaln_bwd(dh2, x1, gains["g_norm2"], sc2, dx2, "adaln2_bwd",
                                                      mix=mix, gate=g1)
    dmixf = dmix.reshape(T, D)
    dy = _mm(dmixf, wts["w_out"], "nt", F32, "mm_out_dx")
    gw_out = _mm(y, dmixf, "tn", F32, "mm_out_dw")
    do_a, dg_out_a = _rms_bwd(dy, 0, out_af, 0, D_A, gains["g_out_a"], "rms_outa_bwd")
    do_b, dg_out_b = _rms_bwd(dy, 1, out_bf, 0, D_B, gains["g_out_b"], "rms_outb_bwd")
    do_b3 = do_b.reshape(B, S, D_B)
    dqc = _mla_bwd_dq(qc, kc, v, do_b3, out_b, lse_b, "mla_bwd_dq")
    dkc, dv = _mla_bwd_dkv(qc, kc, v, do_b3, out_b, lse_b, "mla_bwd_dkv")
    dq_raw = _rope_apply(dqc, cs, -sn, BF16, "rope_q_bwd").reshape(T, N_HEADS * LANES)
    dkrw = _krope_bwd(dkc, cs, -sn, "rope_k_bwd").reshape(T, LANES)
    dcqn = _mm(dq_raw, wts["w_uq"], "nt", F32, "mm_uq_dx")
    gw_uq = _mm(cqn, dq_raw, "tn", F32, "mm_uq_dw")
    dkv = jnp.concatenate([dkc.reshape(T, -1), dv.reshape(T, -1)], axis=1).astype(BF16)
    dckvn = _mm(dkv, wts["w_kv"], "nt", F32, "mm_ukv_dx")
    gw_kv = _mm(ckvn, dkv, "tn", F32, "mm_ukv_dw")
    dcq, dg_cq = _rms_bwd(dcqn, 0, rest, 1, Q_LORA, gains["g_cq"], "rms_cq_bwd")
    dckv, dg_ckv = _rms_bwd(dckvn, 0, rest, 0, KV_LORA, gains["g_ckv"], "rms_ckv_bwd")
    do_a3 = do_a.reshape(B, S, D_A)
    dqkv_d, dbias_d = [], []
    for i, d in enumerate(DILATIONS):
        dqkv_i, dbias_i = _dil_bwd(qkv, do_a3, out_a, lse_a, bias, i, d, f"dil_bwd_{d}")
        dqkv_d.append(dqkv_i)
        dbias_d.append(dbias_i)
    dqkv = _sum3_bf16(*dqkv_d, "dil_bwd_sum").reshape(T, P_QKV)
    g_rel_bias = _bias_grad(dbias_d, buckets_dev, "rel_bias_grad")[:, :N_BUCKETS].T
    dproj = jnp.concatenate([dqkv, dckv, dkrw, dcq], axis=1)
    dh1 = _mm(dproj, w_in, "nt", F32, "mm_in_dx").reshape(B, S, D)
    gw_in = _mm(h1f, dproj, "tn", F32, "mm_in_dw")
    grad_x, dsh1, dsc1, dg_norm1 = _adaln_bwd(dh1, x, gains["g_norm1"], sc1, dx1, "adaln1_bwd")
    gmod = jnp.concatenate([dsh1, dsc1, dg1, dsh2, dsc2, dg2], axis=-1).reshape(B, N_MOD * D)
    grads = dict(w_in=gw_in, w_uq=gw_uq, w_kv=gw_kv, w_out=gw_out, w_ffn_in=gw_ffn_in, w_ffn_out=gw_ffn_out,
                 g_norm1=dg_norm1, g_cq=dg_cq, g_ckv=dg_ckv, rel_bias=g_rel_bias, g_out_a=dg_out_a,
                 g_out_b=dg_out_b, g_norm2=dg_norm2, g_final=dg_final)
    return loss, grad_x, gmod, grads


def _w_in_to_kernel(w):
    z = lambda n: jnp.zeros((w.shape[0], n), w.dtype)
    i3, i4, i5 = 3 * D_A, 3 * D_A + Q_LORA, 3 * D_A + Q_LORA + KV_LORA
    return jnp.concatenate([w[:, :i3], w[:, i4:i5], z(NOPE_DIM), w[:, i5:], z(LANES - NOPE_DIM - ROPE_DIM),
                            w[:, i3:i4]], axis=1)


def _w_in_from_kernel(g):
    o = P_QKV + KV_LORA
    return jnp.concatenate([g[:, :P_QKV], g[:, o + LANES:], g[:, P_QKV:o],
                            g[:, o + NOPE_DIM:o + NOPE_DIM + ROPE_DIM]], axis=1)


def _w_uq_to_kernel(w):
    w3 = w.reshape(Q_LORA, N_HEADS, NOPE_DIM + ROPE_DIM)
    return jnp.pad(w3, ((0, 0), (0, 0), (0, LANES - NOPE_DIM - ROPE_DIM))).reshape(Q_LORA, N_HEADS * LANES)


def _w_uq_from_kernel(g):
    return g.reshape(Q_LORA, N_HEADS, LANES)[:, :, :NOPE_DIM + ROPE_DIM].reshape(Q_LORA, -1)


def _w_ukv_to_kernel(w):
    w3 = w.reshape(KV_LORA, N_HEADS, 2 * HEAD_DIM)
    wk = jnp.pad(w3[:, :, :NOPE_DIM], ((0, 0), (0, 0), (0, LANES - NOPE_DIM))).reshape(KV_LORA, N_HEADS * LANES)
    wv = w3[:, :, NOPE_DIM:].reshape(KV_LORA, D_B)
    return jnp.concatenate([wk, wv], axis=1)


def _w_ukv_from_kernel(g):
    gk = g[:, :N_HEADS * LANES].reshape(KV_LORA, N_HEADS, LANES)[:, :, :NOPE_DIM]
    gv = g[:, N_HEADS * LANES:].reshape(KV_LORA, N_HEADS, HEAD_DIM)
    return jnp.concatenate([gk, gv], axis=2).reshape(KV_LORA, -1)


MESH = pl.DeviceIdType.MESH


def _my_place():
    return lax.axis_index("x"), lax.axis_index("y"), lax.axis_index("c")


def _other_chips(x, y):
    return [(1 - x, y), (x, 1 - y), (1 - x, 1 - y)]


def _allgather8(x_shard, name, in_hbm):
    m_per, n = x_shard.shape
    space = pl.ANY if in_hbm else pltpu.VMEM

    def body(x_ref, out_ref, send_sems, recv_sems, local_sem):
        x, y, c = _my_place()
        me, sibling = (x, y, c), (x, y, 1 - c)
        chips = _other_chips(x, y)

        def rows(px, py, pc):
            return out_ref.at[pl.ds((4 * px + 2 * py + pc) * m_per, m_per), :]

        def copy(k, block, to, src=None):
            return pltpu.make_async_remote_copy(
                src_ref=rows(*block) if src is None else src, dst_ref=rows(*block),
                send_sem=send_sems.at[k], recv_sem=recv_sems.at[k], device_id=to, device_id_type=MESH)

        mine = pltpu.make_async_copy(x_ref, rows(*me), local_sem)
        mine.start()
        first = [copy(0, me, sibling, src=x_ref)]
        first += [copy(1 + j, me, (*chip, c), src=x_ref) for j, chip in enumerate(chips)]
        for cp in first:
            cp.start()
        passed = [copy(4 + j, (*chip, c), sibling) for j, chip in enumerate(chips)]
        for j, chip in enumerate(chips):
            copy(1 + j, (*chip, c), me).wait_recv()
            passed[j].start()
        copy(0, sibling, me).wait_recv()
        for j, chip in enumerate(chips):
            copy(4 + j, (*chip, 1 - c), me).wait_recv()
        for cp in first + passed:
            cp.wait_send()
        mine.wait()

    return pl.pallas_call(
        body, name=name,
        out_shape=jax.ShapeDtypeStruct((N_DEV * m_per, n), x_shard.dtype),
        in_specs=[pl.BlockSpec(memory_space=space)],
        out_specs=pl.BlockSpec(memory_space=space),
        scratch_shapes=[pltpu.SemaphoreType.DMA((7,)), pltpu.SemaphoreType.DMA((7,)), pltpu.SemaphoreType.DMA],
        compiler_params=pltpu.CompilerParams(vmem_limit_bytes=VMEM_LIMIT),
    )(x_shard)


def _hbm_specs(n):
    return [pl.BlockSpec(memory_space=pl.ANY)] * n


def _gather_weights(shards, name):
    n = len(shards)
    halves = [s.shape[0] // 2 for s in shards]

    def body(*refs):
        xs, outs = refs[:n], refs[n:2 * n]
        send_sems, recv_sems, local_sems = refs[2 * n:]
        x, y, c = _my_place()
        me, sibling = (x, y, c), (x, y, 1 - c)
        chips = _other_chips(x, y)

        def blk(k, px, py, pc):
            return outs[k].at[4 * px + 2 * py + pc]

        def mine(k):
            return xs[k].at[pl.ds(pl.multiple_of(c * halves[k], 16), halves[k]), :]

        def copy(k, kind, block, to, own=False):
            return pltpu.make_async_remote_copy(
                src_ref=mine(k) if own else blk(k, *block), dst_ref=blk(k, *block),
                send_sem=send_sems.at[7 * k + kind], recv_sem=recv_sems.at[7 * k + kind],
                device_id=to, device_id_type=MESH)

        local = [pltpu.make_async_copy(mine(k), blk(k, *me), local_sems.at[k]) for k in range(n)]
        for cp in local:
            cp.start()
        first = []
        for k in range(n):
            first.append(copy(k, 0, me, sibling, own=True))
            first += [copy(k, 1 + j, me, (*chip, c), own=True) for j, chip in enumerate(chips)]
        for cp in first:
            cp.start()
        passed = []
        for j, chip in enumerate(chips):
            for k in range(n):
                copy(k, 1 + j, (*chip, c), me).wait_recv()
                fwd = copy(k, 4 + j, (*chip, c), sibling)
                fwd.start()
                passed.append(fwd)
        for k in range(n):
            copy(k, 0, sibling, me).wait_recv()
        for j, chip in enumerate(chips):
            for k in range(n):
                copy(k, 4 + j, (*chip, 1 - c), me).wait_recv()
        for cp in first + passed:
            cp.wait_send()
        for cp in local:
            cp.wait()

    return pl.pallas_call(
        body, name=name,
        out_shape=[jax.ShapeDtypeStruct((N_DEV, h, s.shape[1]), s.dtype) for h, s in zip(halves, shards)],
        in_specs=_hbm_specs(n), out_specs=_hbm_specs(n),
        scratch_shapes=[pltpu.SemaphoreType.DMA((7 * n,)), pltpu.SemaphoreType.DMA((7 * n,)),
                        pltpu.SemaphoreType.DMA((n,))],
    )(*shards)


def _rs_to_sibling(g8s, name):
    n = len(g8s)

    def body(*refs):
        gs, rs = refs[:n], refs[n:2 * n]
        send_sems, recv_sems = refs[2 * n:]
        x, y, c = _my_place()
        copies = [pltpu.make_async_remote_copy(
            src_ref=gs[k].at[2 * s + 1 - c], dst_ref=rs[k].at[s], send_sem=send_sems.at[N_CHIP * k + s],
            recv_sem=recv_sems.at[N_CHIP * k + s], device_id=(x, y, 1 - c), device_id_type=MESH)
            for k in range(n) for s in range(N_CHIP)]
        for cp in copies:
            cp.start()
        for cp in copies:
            cp.wait()

    return pl.pallas_call(
        body, name=name,
        out_shape=[jax.ShapeDtypeStruct((N_CHIP,) + g.shape[1:], g.dtype) for g in g8s],
        in_specs=_hbm_specs(n), out_specs=_hbm_specs(n),
        scratch_shapes=[pltpu.SemaphoreType.DMA((N_CHIP * n,)), pltpu.SemaphoreType.DMA((N_CHIP * n,))],
    )(*g8s)


def _rs_to_chips(a4s, name):
    n = len(a4s)
    nc = N_CHIP - 1

    def body(*refs):
        as_, rs = refs[:n], refs[n:2 * n]
        send_sems, recv_sems = refs[2 * n:]
        x, y, c = _my_place()
        copies = [pltpu.make_async_remote_copy(
            src_ref=as_[k].at[2 * cx + cy], dst_ref=rs[k].at[j], send_sem=send_sems.at[nc * k + j],
            recv_sem=recv_sems.at[nc * k + j], device_id=(cx, cy, c), device_id_type=MESH)
            for k in range(n) for j, (cx, cy) in enumerate(_other_chips(x, y))]
        for cp in copies:
            cp.start()
        for cp in copies:
            cp.wait()

    return pl.pallas_call(
        body, name=name,
        out_shape=[jax.ShapeDtypeStruct((nc,) + a.shape[1:], a.dtype) for a in a4s],
        in_specs=_hbm_specs(n), out_specs=_hbm_specs(n),
        scratch_shapes=[pltpu.SemaphoreType.DMA((nc * n,)), pltpu.SemaphoreType.DMA((nc * n,))],
    )(*a4s)


def _swap_halves(hs, name):
    n = len(hs)

    def body(*refs):
        h_refs, o_refs = refs[:n], refs[n:2 * n]
        send_sems, recv_sems, local_sems = refs[2 * n:]
        x, y, c = _my_place()

        def remote(k, slot):
            return pltpu.make_async_remote_copy(
                src_ref=h_refs[k], dst_ref=o_refs[k].at[slot], send_sem=send_sems.at[k],
                recv_sem=recv_sems.at[k], device_id=(x, y, 1 - c), device_id_type=MESH)

        local = [pltpu.make_async_copy(h_refs[k], o_refs[k].at[c], local_sems.at[k]) for k in range(n)]
        sends = [remote(k, c) for k in range(n)]
        for cp in local + sends:
            cp.start()
        for k in range(n):
            remote(k, 1 - c).wait_recv()
        for cp in sends:
            cp.wait_send()
        for cp in local:
            cp.wait()

    return pl.pallas_call(
        body, name=name,
        out_shape=[jax.ShapeDtypeStruct((2,) + h.shape, h.dtype) for h in hs],
        in_specs=_hbm_specs(n), out_specs=_hbm_specs(n),
        scratch_shapes=[pltpu.SemaphoreType.DMA((n,)), pltpu.SemaphoreType.DMA((n,)), pltpu.SemaphoreType.DMA((n,))],
    )(*hs)


ADD_TILES = 4


def _add_blocks(a_list, a_idx_fn, others_list, ns, sel, name):
    n = len(a_list)
    n_o = len(others_list[0])
    per = 1 + n_o

    def body(sel_ref, *refs):
        for k in range(n):
            ins = refs[k * per:(k + 1) * per]
            o_ref = refs[n * per + k]
            acc = ins[0][0]
            for r in ins[1:]:
                acc = acc + r[0]
            o_ref[0] = acc

    in_specs, args, out_specs, out_shape = [], [], [], []
    for a, others in zip(a_list, others_list):
        _, R, N = a.shape
        tr = R // ADD_TILES
        assert tr % 8 == 0, a.shape
        in_specs.append(pl.BlockSpec((1, tr, N), lambda s, i, sel_ref: (a_idx_fn(s, sel_ref[0]), i, 0)))
        args.append(a)
        for arr, fixed in others:
            if fixed is None:
                in_specs.append(pl.BlockSpec((1, tr, N), lambda s, i, sel_ref: (s, i, 0)))
            else:
                in_specs.append(pl.BlockSpec((1, tr, N), lambda s, i, sel_ref, fixed=fixed: (fixed, i, 0)))
            args.append(arr)
        out_specs.append(pl.BlockSpec((1, tr, N), lambda s, i, sel_ref: (s, i, 0)))
        out_shape.append(jax.ShapeDtypeStruct((ns, R, N), a.dtype))
    grid_spec = pltpu.PrefetchScalarGridSpec(num_scalar_prefetch=1, grid=(ns, ADD_TILES), in_specs=in_specs,
                                             out_specs=out_specs)
    return pl.pallas_call(
        body, name=name, out_shape=out_shape, grid_spec=grid_spec,
        compiler_params=_cparams(("parallel", "parallel")),
    )(sel, *args)


def _reduce_scatter(g8s, tag):
    x, y, c = _my_place()
    c_sel = jnp.reshape(c, (1,)).astype(jnp.int32)
    s_sel = jnp.reshape(2 * x + y, (1,)).astype(jnp.int32)
    r1 = _rs_to_sibling(g8s, f"rs_to_sibling_{tag}")
    a4 = _add_blocks(g8s, lambda s, cc: 2 * s + cc, [[(r, None)] for r in r1], N_CHIP, c_sel,
                     f"rs_add_sibling_{tag}")
    r2 = _rs_to_chips(a4, f"rs_to_chips_{tag}")
    h = _add_blocks(a4, lambda s, ss: ss, [[(r, 0), (r, 1), (r, 2)] for r in r2], 1, s_sel,
                    f"rs_add_chips_{tag}")
    full = _swap_halves([hk.reshape(hk.shape[1:]) for hk in h], f"rs_swap_halves_{tag}")
    return [f.reshape(2 * f.shape[1], f.shape[2]) for f in full]


def _ada_fwd(c_all, w_ada, b_ada, name):
    nb, D = c_all.shape
    ncol = w_ada.shape[1]
    tc = 512

    def body(c_ref, w_ref, b_ref, o_ref):
        cv = c_ref[...]
        cond = (cv * jax.nn.sigmoid(cv)).astype(BF16)
        o_ref[...] = jnp.dot(cond, w_ref[...].astype(BF16), preferred_element_type=F32) + b_ref[...]

    return pl.pallas_call(
        body, name=name, out_shape=jax.ShapeDtypeStruct((nb, ncol), F32), grid=(ncol // tc,),
        in_specs=[pl.BlockSpec((nb, D), lambda j: (0, 0)), pl.BlockSpec((D, tc), lambda j: (0, j)),
                  pl.BlockSpec((1, tc), lambda j: (0, j))],
        out_specs=pl.BlockSpec((nb, tc), lambda j: (0, j)),
        compiler_params=_cparams(("parallel",)),
    )(c_all, w_ada, b_ada)


def _ada_bwd(c_all, gmod_cols, name):
    nb, D = c_all.shape
    ncol = gmod_cols.shape[1]
    tc = 512

    def body(c_ref, g_ref, o_ref):
        cv = c_ref[...]
        cond = (cv * jax.nn.sigmoid(cv)).astype(BF16)
        o_ref[...] = _dot_tn(cond, g_ref[...].astype(BF16))

    return pl.pallas_call(
        body, name=name, out_shape=jax.ShapeDtypeStruct((D, ncol), F32), grid=(ncol // tc,),
        in_specs=[pl.BlockSpec((nb, D), lambda j: (0, 0)), pl.BlockSpec((nb, tc), lambda j: (0, j))],
        out_specs=pl.BlockSpec((D, tc), lambda j: (0, j)),
        compiler_params=_cparams(("parallel",)),
    )(c_all, gmod_cols)


def _adam_math(w, g, m, v):
    m = ADAM_B1 * m + (1.0 - ADAM_B1) * g
    v = ADAM_B2 * v + (1.0 - ADAM_B2) * (g * g)
    m_hat = m / (1.0 - ADAM_B1 ** ADAM_STEP)
    v_hat = v / (1.0 - ADAM_B2 ** ADAM_STEP)
    delta = -ADAM_LR * (m_hat / (jnp.sqrt(v_hat) + ADAM_EPS) + ADAM_WD * w)
    return delta, m, v


def _adamw(w, g, m, v, name):
    rows, cols = w.shape
    tr = _pick(rows, (256, 192, 176, 128, 64, 8))

    def body(w_ref, g_ref, m_ref, v_ref, d_ref, mo_ref, vo_ref):
        d, mn, vn = _adam_math(w_ref[...], g_ref[...], m_ref[...], v_ref[...])
        d_ref[...] = d
        mo_ref[...] = mn
        vo_ref[...] = vn

    spec = pl.BlockSpec((tr, cols), lambda i: (i, 0))
    return pl.pallas_call(
        body, name=name, out_shape=[jax.ShapeDtypeStruct((rows, cols), F32)] * 3, grid=(rows // tr,),
        in_specs=[spec] * 4, out_specs=[spec] * 3, compiler_params=_cparams(("parallel",)),
    )(w, g, m, v)


VEC_ROWS = 8


def _adamw_rows(w, parts, m, v, name):
    n = w.shape[1]
    P = parts.shape[0]
    assert n % (VEC_ROWS * LANES) == 0, n
    shp = (VEC_ROWS, n // VEC_ROWS)

    def body(w_ref, p_ref, m_ref, v_ref, g_ref, d_ref, mo_ref, vo_ref):
        g = p_ref[0]
        for k in range(1, P):
            g = g + p_ref[k]
        d, mn, vn = _adam_math(w_ref[...], g, m_ref[...], v_ref[...])
        g_ref[...] = g
        d_ref[...] = d
        mo_ref[...] = mn
        vo_ref[...] = vn

    vec = pl.BlockSpec(shp, lambda i: (0, 0))
    out = pl.pallas_call(
        body, name=name, out_shape=[jax.ShapeDtypeStruct(shp, F32)] * 4, grid=(1,),
        in_specs=[vec, pl.BlockSpec((P,) + shp, lambda i: (0, 0, 0)), vec, vec], out_specs=[vec] * 4,
        compiler_params=_cparams(("arbitrary",)),
    )(w.reshape(shp), parts.reshape((P,) + shp), m.reshape(shp), v.reshape(shp))
    return [o.reshape(1, n) for o in out]


_PACKED = (("w_in", 1024, 552), ("w_uq", 384, 192), ("w_ukv", 256, 256))
_SHARDED = ("w_in", "w_uq", "w_ukv", "w_out", "w_ffn_in", "w_ffn_out")
_SMALL = (("g_norm1", 1024), ("g_cq", 384), ("g_ckv", 256), ("rel_bias", 256), ("g_out_a", 512),
          ("g_out_b", 512), ("g_norm2", 1024), ("g_final", 1024))
_SMALL_PAD = 5120
PACK_ROWS = 704
_PACK_ELEMS = PACK_ROWS * D_MODEL


def _pack_shards(shards, dtype):
    lead = shards["w_in"].shape[:-2]
    flat = jnp.concatenate([shards[n].astype(dtype).reshape(lead + (-1,)) for n, _, _ in _PACKED], axis=-1)
    pad = [(0, 0)] * len(lead) + [(0, _PACK_ELEMS - flat.shape[-1])]
    return jnp.pad(flat, pad).reshape(lead + (PACK_ROWS, D_MODEL))


def _unpack_shards(packed):
    out, off = {}, 0
    for n, r, c in _PACKED:
        out[n] = packed[..., off:off + r * c].reshape(packed.shape[:-1] + (r, c))
        off += r * c
    return out


def _full_from_shards(sh):
    return jnp.transpose(sh, (1, 0, 2)).reshape(sh.shape[1], -1)


def _shards_from_full(full):
    rows, cols = full.shape
    return jnp.transpose(full.reshape(rows, N_CHIP, cols // N_CHIP), (1, 0, 2))


def kernel(x, c, w_ada, b_ada, g_norm1, w_in, g_cq, w_uq, g_ckv, w_ukv, rel_bias, g_out_a, g_out_b, w_out, g_norm2, w_ffn_in, w_ffn_out, g_final, loss_target, m_w_ada, m_b_ada, m_g_norm1, m_w_in, m_g_cq, m_w_uq, m_g_ckv, m_w_ukv, m_rel_bias, m_g_out_a, m_g_out_b, m_w_out, m_g_norm2, m_w_ffn_in, m_w_ffn_out, m_g_final, v_w_ada, v_b_ada, v_g_norm1, v_w_in, v_g_cq, v_w_uq, v_g_ckv, v_w_ukv, v_rel_bias, v_g_out_a, v_g_out_b, v_w_out, v_g_norm2, v_w_ffn_in, v_w_ffn_out, v_g_final):
    names = ["w_ada", "b_ada", "g_norm1", "w_in", "g_cq", "w_uq", "g_ckv", "w_ukv", "rel_bias", "g_out_a",
             "g_out_b", "w_out", "g_norm2", "w_ffn_in", "w_ffn_out", "g_final"]
    W = dict(zip(names, [w_ada, b_ada, g_norm1, w_in, g_cq, w_uq, g_ckv, w_ukv, rel_bias, g_out_a, g_out_b,
                         w_out, g_norm2, w_ffn_in, w_ffn_out, g_final]))
    M = dict(zip(names, [m_w_ada, m_b_ada, m_g_norm1, m_w_in, m_g_cq, m_w_uq, m_g_ckv, m_w_ukv, m_rel_bias,
                         m_g_out_a, m_g_out_b, m_w_out, m_g_norm2, m_w_ffn_in, m_w_ffn_out, m_g_final]))
    V = dict(zip(names, [v_w_ada, v_b_ada, v_g_norm1, v_w_in, v_g_cq, v_w_uq, v_g_ckv, v_w_ukv, v_rel_bias,
                         v_g_out_a, v_g_out_b, v_w_out, v_g_norm2, v_w_ffn_in, v_w_ffn_out, v_g_final]))
    B, S, D = x.shape
    mx, my, mc = _my_place()
    dev = 4 * mx + 2 * my + mc
    chip = 2 * mx + my
    pad_rows = 8

    c_all = _allgather8(jnp.pad(c, ((0, pad_rows - B), (0, 0))), "ag_c", False)
    c_all = c_all.reshape(N_DEV, pad_rows, D)[:, :B].reshape(N_DEV * B, D)
    ada_cols = w_ada.shape[-1]
    b_cols = lax.dynamic_slice_in_dim(b_ada, chip * ada_cols, ada_cols, axis=1)
    mod_cols = _ada_fwd(c_all, w_ada[0], b_cols, "ada_fwd")
    mod_all = _allgather8(mod_cols, "ag_mod", False).reshape(N_DEV, N_DEV * B, ada_cols)[0::2]
    mod_all = jnp.transpose(mod_all, (1, 0, 2)).reshape(N_DEV * B, N_MOD * D)
    mod = lax.dynamic_slice_in_dim(mod_all, dev * B, B, axis=0)

    packed = _pack_shards({n: W[n][0] for n, _, _ in _PACKED}, BF16)
    g_packed, g_out, g_ffn_in, g_ffn_out = _gather_weights(
        [packed, w_out[0].astype(BF16), w_ffn_in[0].astype(BF16), w_ffn_out[0].astype(BF16)], "ag_weights")
    full = {n: _full_from_shards(sh) for n, sh in _unpack_shards(g_packed.reshape(N_CHIP, _PACK_ELEMS)).items()}
    wts = dict(w_in=_w_in_to_kernel(full["w_in"]), w_uq=_w_uq_to_kernel(full["w_uq"]),
               w_kv=_w_ukv_to_kernel(full["w_ukv"]), w_out=g_out.reshape(D, D),
               w_ffn_in=g_ffn_in.reshape(N_CHIP, D, -1), w_ffn_out=g_ffn_out.reshape(D_FF, D))
    gains = dict(g_norm1=g_norm1, g_cq=g_cq, g_ckv=g_ckv, g_out_a=g_out_a, g_out_b=g_out_b, g_norm2=g_norm2,
                 g_final=g_final.reshape(1, D))

    loss, grad_x, gmod, grads = _local_step(x, loss_target, mod, wts, gains, rel_bias)
    loss = lax.psum(loss[0, 0], ("x", "y", "c"))

    n_small = _SMALL_PAD
    cat = lambda dct: jnp.concatenate([dct[n].reshape(1, -1) for n, _ in _SMALL]
                                      + [jnp.zeros((1, _SMALL_PAD - sum(s for _, s in _SMALL)), F32)], axis=1)
    small = cat(grads)
    rows = jnp.concatenate([gmod, jnp.pad(small, ((0, 0), (0, N_MOD * D - n_small))),
                            jnp.zeros((pad_rows - B - 1, N_MOD * D), F32)], axis=0)
    rows_all = _allgather8(rows, "ag_small", False).reshape(N_DEV, pad_rows, N_MOD * D)
    gmod_all = rows_all[:, :B].reshape(N_DEV * B, N_MOD * D)
    small_parts = rows_all[:, B, :n_small]

    nat = dict(w_in=_w_in_from_kernel(grads["w_in"]), w_uq=_w_uq_from_kernel(grads["w_uq"]),
               w_ukv=_w_ukv_from_kernel(grads["w_kv"]))
    gp = _pack_shards({n: _shards_from_full(nat[n]) for n, _, _ in _PACKED}, F32)
    as_halves = lambda a: a.reshape(N_DEV, -1, a.shape[-1])
    r_packed, r_out, r_ffn_in, r_ffn_out = _reduce_scatter(
        [as_halves(gp), as_halves(grads["w_out"]), as_halves(grads["w_ffn_in"]), as_halves(grads["w_ffn_out"])],
        "all")
    G = _unpack_shards(r_packed.reshape(_PACK_ELEMS))
    G.update(w_out=r_out, w_ffn_in=r_ffn_in, w_ffn_out=r_ffn_out)

    gmod_cols = lax.dynamic_slice_in_dim(gmod_all, chip * ada_cols, ada_cols, axis=1)
    G["w_ada"] = _ada_bwd(c_all, gmod_cols, "ada_bwd")
    delta, new_m, new_v = {}, {}, {}
    for n in ("w_ada",) + _SHARDED:
        shp = W[n].shape
        w2 = W[n].reshape(shp[-2], shp[-1])
        d_, m_, v_ = _adamw(w2, G[n], M[n].reshape(w2.shape), V[n].reshape(w2.shape), f"adamw_{n}")
        G[n], delta[n], new_m[n], new_v[n] = [a.reshape(shp) for a in (G[n], d_, m_, v_)]
    gs, ds_, ms_, vs_ = _adamw_rows(cat(W), small_parts, cat(M), cat(V), "adamw_small")
    off = 0
    for n, sz in _SMALL:
        shp = W[n].shape
        G[n], delta[n], new_m[n], new_v[n] = [a[:, off:off + sz].reshape(shp) for a in (gs, ds_, ms_, vs_)]
        off += sz
    G["b_ada"], delta["b_ada"], new_m["b_ada"], new_v["b_ada"] = _adamw_rows(b_ada, gmod_all, m_b_ada, v_b_ada,
                                                                          "adamw_b_ada")
    return (loss, grad_x, *[G[n] for n in names], *[delta[n] for n in names], *[new_m[n] for n in names],
            *[new_v[n] for n in names])
```

```python
import functools
import math

import numpy as np
import jax
import jax.numpy as jnp
from jax import lax
from jax.experimental import pallas as pl
from jax.experimental.pallas import tpu as pltpu

F32 = jnp.float32
BF16 = jnp.bfloat16

D_MODEL = 1024
SEQ = 2048
N_HEADS = 8
HEAD_DIM = 64
D_A = 512
D_B = 512
Q_LORA = 384
KV_LORA = 256
ROPE_DIM = 32
NOPE_DIM = 64
D_FF = 2816
N_MOD = 6
N_BUCKETS = 32
MAX_DISTANCE = 2048
ROPE_THETA = 10000.0
EPS = 1e-6
NEG = -1e30
BLK = 128
DILATIONS = (1, 4, 16)
SPAN = 128
MLA_SCALE = (NOPE_DIM + ROPE_DIM) ** -0.5
DIL_SCALE = HEAD_DIM ** -0.5

ADAM_LR = 0.001
ADAM_B1 = 0.9
ADAM_B2 = 0.999
ADAM_EPS = 1e-08
ADAM_WD = 0.01
ADAM_STEP = 10

N_DEV = 8
N_CHIP = 4
LANES = 128
VMEM_LIMIT = 48 * 1024 * 1024

P_QKV = 3 * D_A
P_REST = KV_LORA + LANES + Q_LORA


def _cparams(sem=None):
    return pltpu.CompilerParams(dimension_semantics=sem, vmem_limit_bytes=VMEM_LIMIT)


def _pick(n, cands):
    for c in cands:
        if n % c == 0:
            return c
    raise ValueError(f"no tile for {n} in {cands}")


def _mm(a, b, mode, out_dtype, name, col_blocks=None):
    blocked = col_blocks is not None
    if mode == "nn":
        (M, K) = a.shape
        K2, N = (b.shape[1], b.shape[0] * b.shape[2]) if blocked else b.shape
    elif mode == "nt":
        (M, K) = a.shape
        N, K2 = (b.shape[1], b.shape[0] * b.shape[2]) if blocked else b.shape
    else:
        (K, M), (K2, N) = a.shape, b.shape
    assert K == K2, (a.shape, b.shape, mode)
    tm = _pick(M, (512, 384, 256, 128))
    tn = _pick(N, (1408, 1024, 768, 512, 384, 256, 128))
    tk = _pick(K, (1024, 512, 384, 256, 128))
    if blocked and mode == "nt":
        tk = K // col_blocks
    elif blocked:
        tn = N // col_blocks
    nk = K // tk
    out_shape = (M, N)
    out_spec = pl.BlockSpec((tm, tn), lambda i, j, k: (i, j))
    if mode == "nn":
        a_spec = pl.BlockSpec((tm, tk), lambda i, j, k: (i, k))
        b_spec = (pl.BlockSpec((None, tk, tn), lambda i, j, k: (j, k, 0)) if blocked
                  else pl.BlockSpec((tk, tn), lambda i, j, k: (k, j)))
        dn = (((1,), (0,)), ((), ()))
    elif mode == "nt":
        a_spec = pl.BlockSpec((tm, tk), lambda i, j, k: (i, k))
        b_spec = (pl.BlockSpec((None, tn, tk), lambda i, j, k: (k, j, 0)) if blocked
                  else pl.BlockSpec((tn, tk), lambda i, j, k: (j, k)))
        dn = (((1,), (1,)), ((), ()))
    else:
        a_spec = pl.BlockSpec((tk, tm), lambda i, j, k: (k, i))
        b_spec = pl.BlockSpec((tk, tn), lambda i, j, k: (k, j))
        dn = (((0,), (0,)), ((), ()))
        if blocked:
            out_shape = (col_blocks, M, tn)
            out_spec = pl.BlockSpec((None, tm, tn), lambda i, j, k: (j, i, 0))

    def body(a_ref, b_ref, o_ref, acc_ref):
        k = pl.program_id(2)

        @pl.when(k == 0)
        def _():
            acc_ref[...] = jnp.zeros_like(acc_ref)

        acc_ref[...] += lax.dot_general(a_ref[...].astype(BF16), b_ref[...].astype(BF16), dn,
                                        preferred_element_type=F32)

        @pl.when(k == nk - 1)
        def _():
            o_ref[...] = acc_ref[...].astype(o_ref.dtype)

    return pl.pallas_call(
        body, name=name,
        out_shape=jax.ShapeDtypeStruct(out_shape, out_dtype),
        grid=(M // tm, N // tn, nk),
        in_specs=[a_spec, b_spec],
        out_specs=out_spec,
        scratch_shapes=[pltpu.VMEM((tm, tn), F32)],
        compiler_params=_cparams(("parallel", "parallel", "arbitrary")),
    )(a, b)


ROW_TILE = 256


def _adaln_fwd(x, g, sc, sh, name, mix=None, gate=None):
    B, S, D = x.shape
    ts = ROW_TILE
    has_res = mix is not None

    def body(*refs):
        if has_res:
            x_ref, g_ref, sc_ref, sh_ref, mix_ref, gate_ref, h_ref, xr_ref = refs
            xr = x_ref[0] + gate_ref[0] * mix_ref[0]
            xr_ref[0] = xr
        else:
            x_ref, g_ref, sc_ref, sh_ref, h_ref = refs
            xr = x_ref[0]
        r = lax.rsqrt(jnp.mean(xr * xr, axis=-1, keepdims=True) + EPS)
        xn = (xr * r) * g_ref[...]
        h_ref[0] = (xn * (1.0 + sc_ref[0]) + sh_ref[0]).astype(h_ref.dtype)

    tok = pl.BlockSpec((1, ts, D), lambda b, s: (b, s, 0))
    per_b = pl.BlockSpec((1, 1, D), lambda b, s: (b, 0, 0))
    vec = pl.BlockSpec((1, D), lambda b, s: (0, 0))
    in_specs = [tok, vec, per_b, per_b]
    args = [x, g, sc, sh]
    out_shape = [jax.ShapeDtypeStruct((B, S, D), BF16)]
    out_specs = [tok]
    if has_res:
        in_specs += [tok, per_b]
        args += [mix, gate]
        out_shape.append(jax.ShapeDtypeStruct((B, S, D), F32))
        out_specs.append(tok)
    out = pl.pallas_call(
        body, name=name, out_shape=out_shape, grid=(B, S // ts),
        in_specs=in_specs, out_specs=out_specs,
        compiler_params=_cparams(("parallel", "parallel")),
    )(*args)
    return out if has_res else out[0]


def _adaln_bwd(dh, x, g, sc, dres, name, mix=None, gate=None):
    B, S, D = x.shape
    ts = ROW_TILE
    has_res = mix is not None

    def body(*refs):
        if has_res:
            (dh_ref, x_ref, g_ref, sc_ref, dres_ref, mix_ref, gate_ref,
             dx_ref, dsh_ref, dsc_ref, dg_ref, dgate_ref, dmix_ref) = refs
        else:
            (dh_ref, x_ref, g_ref, sc_ref, dres_ref, dx_ref, dsh_ref, dsc_ref, dg_ref) = refs
        b, s = pl.program_id(0), pl.program_id(1)
        xv = x_ref[0]
        dhv = dh_ref[0]
        gv = g_ref[...]
        r = lax.rsqrt(jnp.mean(xv * xv, axis=-1, keepdims=True) + EPS)
        n = xv * r
        xn = n * gv
        dxn = dhv * (1.0 + sc_ref[0])
        dn = dxn * gv
        dx = r * (dn - n * jnp.mean(dn * n, axis=-1, keepdims=True)) + dres_ref[0]
        dx_ref[0] = dx

        @pl.when(s == 0)
        def _():
            dsh_ref[...] = jnp.zeros_like(dsh_ref)
            dsc_ref[...] = jnp.zeros_like(dsc_ref)
            if has_res:
                dgate_ref[...] = jnp.zeros_like(dgate_ref)

        @pl.when((s == 0) & (b == 0))
        def _():
            dg_ref[...] = jnp.zeros_like(dg_ref)

        dsh_ref[0] += jnp.sum(dhv, axis=0, keepdims=True)
        dsc_ref[0] += jnp.sum(dhv * xn, axis=0, keepdims=True)
        dg_ref[...] += jnp.sum(dxn * n, axis=0, keepdims=True)
        if has_res:
            dgate_ref[0] += jnp.sum(dx * mix_ref[0], axis=0, keepdims=True)
            dmix_ref[0] = (dx * gate_ref[0]).astype(dmix_ref.dtype)

    tok = pl.BlockSpec((1, ts, D), lambda b, s: (b, s, 0))
    per_b = pl.BlockSpec((1, 1, D), lambda b, s: (b, 0, 0))
    vec = pl.BlockSpec((1, D), lambda b, s: (0, 0))
    in_specs = [tok, tok, vec, per_b, tok]
    args = [dh, x, g, sc, dres]
    out_shape = [jax.ShapeDtypeStruct((B, S, D), F32), jax.ShapeDtypeStruct((B, 1, D), F32),
                 jax.ShapeDtypeStruct((B, 1, D), F32), jax.ShapeDtypeStruct((1, D), F32)]
    out_specs = [tok, per_b, per_b, vec]
    if has_res:
        in_specs += [tok, per_b]
        args += [mix, gate]
        out_shape += [jax.ShapeDtypeStruct((B, 1, D), F32), jax.ShapeDtypeStruct((B, S, D), BF16)]
        out_specs += [per_b, tok]
    return pl.pallas_call(
        body, name=name, out_shape=out_shape, grid=(B, S // ts),
        in_specs=in_specs, out_specs=out_specs,
        compiler_params=_cparams(("arbitrary", "arbitrary")),
    )(*args)


def _rms_fwd(x, col_blk, n, g, name, n_real=None):
    T = x.shape[0]
    tr = 512
    nr = float(n_real or n)

    def body(x_ref, g_ref, y_ref):
        xv = x_ref[...]
        r = lax.rsqrt(jnp.sum(xv * xv, axis=-1, keepdims=True) / nr + EPS)
        y_ref[...] = ((xv * r) * g_ref[...]).astype(y_ref.dtype)

    return pl.pallas_call(
        body, name=name, out_shape=jax.ShapeDtypeStruct((T, n), BF16), grid=(T // tr,),
        in_specs=[pl.BlockSpec((tr, n), lambda i: (i, col_blk)), pl.BlockSpec((1, n), lambda i: (0, 0))],
        out_specs=pl.BlockSpec((tr, n), lambda i: (i, 0)),
        compiler_params=_cparams(("parallel",)),
    )(x, g)


def _rms_bwd(dy, dy_blk, x, x_blk, n, g, name, out_dtype=BF16):
    T = x.shape[0]
    tr = 512

    def body(dy_ref, x_ref, g_ref, dx_ref, dg_ref):
        xv = x_ref[...]
        dyv = dy_ref[...].astype(F32)
        r = lax.rsqrt(jnp.mean(xv * xv, axis=-1, keepdims=True) + EPS)
        nrm = xv * r
        dn = dyv * g_ref[...]
        dx_ref[...] = (r * (dn - nrm * jnp.mean(dn * nrm, axis=-1, keepdims=True))).astype(dx_ref.dtype)

        @pl.when(pl.program_id(0) == 0)
        def _():
            dg_ref[...] = jnp.zeros_like(dg_ref)

        dg_ref[...] += jnp.sum(dyv * nrm, axis=0, keepdims=True)

    return pl.pallas_call(
        body, name=name,
        out_shape=[jax.ShapeDtypeStruct((T, n), out_dtype), jax.ShapeDtypeStruct((1, n), F32)],
        grid=(T // tr,),
        in_specs=[pl.BlockSpec((tr, n), lambda i: (i, dy_blk)), pl.BlockSpec((tr, n), lambda i: (i, x_blk)),
                  pl.BlockSpec((1, n), lambda i: (0, 0))],
        out_specs=[pl.BlockSpec((tr, n), lambda i: (i, 0)), pl.BlockSpec((1, n), lambda i: (0, 0))],
        compiler_params=_cparams(("arbitrary",)),
    )(dy, x, g)


def _swiglu_fwd(gu, name):
    T = gu.shape[0]
    tr, tc = 512, 1408
    nc = D_FF // tc

    def body(g_ref, u_ref, a_ref):
        gv = g_ref[...]
        a_ref[...] = (gv * jax.nn.sigmoid(gv) * u_ref[...]).astype(a_ref.dtype)

    return pl.pallas_call(
        body, name=name, out_shape=jax.ShapeDtypeStruct((T, D_FF), BF16), grid=(T // tr, nc),
        in_specs=[pl.BlockSpec((tr, tc), lambda i, j: (i, j)), pl.BlockSpec((tr, tc), lambda i, j: (i, j + nc))],
        out_specs=pl.BlockSpec((tr, tc), lambda i, j: (i, j)),
        compiler_params=_cparams(("parallel", "parallel")),
    )(gu, gu)


def _swiglu_bwd(da, gu, name):
    T = gu.shape[0]
    tr, tc = 512, 1408
    nc = D_FF // tc

    def body(da_ref, g_ref, u_ref, dgu_ref):
        j = pl.program_id(1)
        gv, uv, dav = g_ref[...], u_ref[...], da_ref[...]
        sg = jax.nn.sigmoid(gv)

        @pl.when(j < nc)
        def _():
            dgu_ref[...] = (dav * uv * (sg * (1.0 + gv * (1.0 - sg)))).astype(dgu_ref.dtype)

        @pl.when(j >= nc)
        def _():
            dgu_ref[...] = (dav * (gv * sg)).astype(dgu_ref.dtype)

    return pl.pallas_call(
        body, name=name, out_shape=jax.ShapeDtypeStruct((T, 2 * D_FF), BF16), grid=(T // tr, 2 * nc),
        in_specs=[pl.BlockSpec((tr, tc), lambda i, j: (i, j % nc)),
                  pl.BlockSpec((tr, tc), lambda i, j: (i, j % nc)),
                  pl.BlockSpec((tr, tc), lambda i, j: (i, j % nc + nc))],
        out_specs=pl.BlockSpec((tr, tc), lambda i, j: (i, j)),
        compiler_params=_cparams(("parallel", "parallel")),
    )(da, gu, gu)


def _final_loss(x1, f, g2, gf, target, name):
    B, S, D = x1.shape
    ts = ROW_TILE

    def body(x1_ref, f_ref, g2_ref, gf_ref, t_ref, dx_ref, df_ref, dg2_ref, dgf_ref, loss_ref):
        b, s = pl.program_id(0), pl.program_id(1)
        fv = f_ref[0]
        g2v = g2_ref[0]
        gfv = gf_ref[...]
        x2 = x1_ref[0] + g2v * fv
        r = lax.rsqrt(jnp.mean(x2 * x2, axis=-1, keepdims=True) + EPS)
        n = x2 * r
        e = n * gfv - t_ref[0]
        dy = e * (1.0 / D)
        dn = dy * gfv
        dx = r * (dn - n * jnp.mean(dn * n, axis=-1, keepdims=True))
        dx_ref[0] = dx
        df_ref[0] = (dx * g2v).astype(df_ref.dtype)

        @pl.when(s == 0)
        def _():
            dg2_ref[...] = jnp.zeros_like(dg2_ref)

        @pl.when((s == 0) & (b == 0))
        def _():
            dgf_ref[...] = jnp.zeros_like(dgf_ref)
            loss_ref[...] = jnp.zeros_like(loss_ref)

        dg2_ref[0] += jnp.sum(dx * fv, axis=0, keepdims=True)
        dgf_ref[...] += jnp.sum(dy * n, axis=0, keepdims=True)
        loss_ref[...] += 0.5 * jnp.sum(jnp.mean(e * e, axis=-1, keepdims=True), axis=0, keepdims=True)

    tok = pl.BlockSpec((1, ts, D), lambda b, s: (b, s, 0))
    per_b = pl.BlockSpec((1, 1, D), lambda b, s: (b, 0, 0))
    vec = pl.BlockSpec((1, D), lambda b, s: (0, 0))
    return pl.pallas_call(
        body, name=name,
        out_shape=[jax.ShapeDtypeStruct((B, S, D), F32), jax.ShapeDtypeStruct((B, S, D), BF16),
                   jax.ShapeDtypeStruct((B, 1, D), F32), jax.ShapeDtypeStruct((1, D), F32),
                   jax.ShapeDtypeStruct((1, LANES), F32)],
        grid=(B, S // ts),
        in_specs=[tok, tok, per_b, vec, tok],
        out_specs=[tok, tok, per_b, vec, pl.BlockSpec((1, LANES), lambda b, s: (0, 0))],
        compiler_params=_cparams(("arbitrary", "arbitrary")),
    )(x1, f, g2, gf, target)


def _rope_tables():
    half = ROPE_DIM // 2
    inv = ROPE_THETA ** (-jnp.arange(half, dtype=F32) / half)
    ang = jnp.arange(SEQ, dtype=F32)[:, None] * inv[None, :]
    cos, sin = jnp.cos(ang), jnp.sin(ang)
    one = jnp.ones((SEQ, NOPE_DIM), F32)
    zero = jnp.zeros((SEQ, NOPE_DIM), F32)
    cs = jnp.concatenate([one, cos, cos, one[:, :LANES - NOPE_DIM - ROPE_DIM]], axis=1)
    sn = jnp.concatenate([zero, -sin, sin, zero[:, :LANES - NOPE_DIM - ROPE_DIM]], axis=1)
    return cs, sn


def _rope_group(t, cs, sn):
    half = ROPE_DIM // 2
    lane = lax.broadcasted_iota(jnp.int32, t.shape, 1)
    partner = jnp.where(lane < NOPE_DIM + half, pltpu.roll(t, LANES - half, 1), pltpu.roll(t, half, 1))
    return t * cs + partner * sn


def _rope_apply(t, cs, sn, out_dtype, name, add=None, add_blk=0):
    B, S, W = t.shape
    G = W // LANES
    ts = ROW_TILE

    def body(*refs):
        if add is None:
            t_ref, cs_ref, sn_ref, o_ref = refs
            for gi in range(G):
                sl = slice(gi * LANES, (gi + 1) * LANES)
                o_ref[0, :, sl] = _rope_group(t_ref[0, :, sl], cs_ref[...], sn_ref[...]).astype(o_ref.dtype)
        else:
            t_ref, a_ref, cs_ref, sn_ref, o_ref = refs
            ra = _rope_group(a_ref[0], cs_ref[...], sn_ref[...])
            for gi in range(G):
                sl = slice(gi * LANES, (gi + 1) * LANES)
                o_ref[0, :, sl] = (t_ref[0, :, sl] + ra).astype(o_ref.dtype)

    tok = pl.BlockSpec((1, ts, W), lambda b, s: (b, s, 0))
    tab = pl.BlockSpec((ts, LANES), lambda b, s: (s, 0))
    in_specs, args = [tok], [t]
    if add is not None:
        in_specs.append(pl.BlockSpec((1, ts, LANES), lambda b, s: (b, s, add_blk)))
        args.append(add)
    in_specs += [tab, tab]
    args += [cs, sn]
    return pl.pallas_call(
        body, name=name, out_shape=jax.ShapeDtypeStruct((B, S, W), out_dtype), grid=(B, S // ts),
        in_specs=in_specs, out_specs=tok, compiler_params=_cparams(("parallel", "parallel")),
    )(*args)


def _krope_bwd(dkc, cs, sn_neg, name):
    B, S, W = dkc.shape
    G = W // LANES
    ts = ROW_TILE

    def body(d_ref, cs_ref, sn_ref, o_ref):
        acc = d_ref[0, :, 0:LANES]
        for gi in range(1, G):
            acc = acc + d_ref[0, :, gi * LANES:(gi + 1) * LANES]
        lane = lax.broadcasted_iota(jnp.int32, acc.shape, 1)
        rot = (lane >= NOPE_DIM) & (lane < NOPE_DIM + ROPE_DIM)
        acc = jnp.where(rot, acc, 0.0)
        o_ref[0] = _rope_group(acc, cs_ref[...], sn_ref[...]).astype(o_ref.dtype)

    tab = pl.BlockSpec((ts, LANES), lambda b, s: (s, 0))
    return pl.pallas_call(
        body, name=name, out_shape=jax.ShapeDtypeStruct((B, S, LANES), BF16), grid=(B, S // ts),
        in_specs=[pl.BlockSpec((1, ts, W), lambda b, s: (b, s, 0)), tab, tab],
        out_specs=pl.BlockSpec((1, ts, LANES), lambda b, s: (b, s, 0)),
        compiler_params=_cparams(("parallel", "parallel")),
    )(dkc, cs, sn_neg)


def _t5_bucket(dist):
    max_exact = N_BUCKETS // 2
    d = np.maximum(dist, 1).astype(np.float64)
    large = max_exact + (np.log(d / max_exact) / np.log(MAX_DISTANCE / max_exact)
                         * (N_BUCKETS - max_exact)).astype(np.int64)
    large = np.minimum(large, N_BUCKETS - 1)
    return np.where(dist < max_exact, dist, large).astype(np.int32)


def _band_buckets(dilation):
    a = np.arange(BLK)[:, None]
    bk = np.arange(2 * BLK)[None, :]
    steps = BLK + a - bk
    return _t5_bucket(np.clip(steps, 0, SPAN) * dilation)


def _head_mask(shape, hh):
    lane = lax.broadcasted_iota(jnp.int32, shape, 1)
    return (lane >= hh * HEAD_DIM) & (lane < (hh + 1) * HEAD_DIM)


def _dot_nt(a, b):
    return lax.dot_general(a, b, (((1,), (1,)), ((), ())), preferred_element_type=F32)


def _dot_tn(a, b):
    return lax.dot_general(a, b, (((0,), (0,)), ((), ())), preferred_element_type=F32)


def _dot_nn(a, b):
    return lax.dot_general(a, b, (((1,), (0,)), ((), ())), preferred_element_type=F32)


def _band_valid():
    a = lax.broadcasted_iota(jnp.int32, (BLK, BLK), 0)
    bk = lax.broadcasted_iota(jnp.int32, (BLK, BLK), 1)
    return bk >= a, bk <= a


def _dil_fwd(qkv, bias, branch, dilation, name):
    B, S, _ = qkv.shape
    d = dilation
    n = S // d
    nb = n // BLK
    qkv_v = qkv.reshape(B, n, d * P_QKV)
    npair = N_HEADS // 2

    def body(cur_ref, prev_ref, bias_ref, o_ref, lse_ref):
        i = pl.program_id(2)
        vprev, vcur = _band_valid()
        vprev = vprev & (i > 0)
        for p in range(npair):
            sl = slice(p * LANES, (p + 1) * LANES)
            q = cur_ref[0, :, sl]
            kc = cur_ref[0, :, D_A + p * LANES:D_A + (p + 1) * LANES]
            vc = cur_ref[0, :, 2 * D_A + p * LANES:2 * D_A + (p + 1) * LANES]
            kp = prev_ref[0, :, D_A + p * LANES:D_A + (p + 1) * LANES]
            vp = prev_ref[0, :, 2 * D_A + p * LANES:2 * D_A + (p + 1) * LANES]
            o_pair = jnp.zeros((BLK, LANES), F32)
            lse_pair = jnp.zeros((BLK, LANES), F32)
            for hh in range(2):
                h = 2 * p + hh
                hm = _head_mask((BLK, LANES), hh)
                qm = jnp.where(hm, q, jnp.zeros_like(q))
                s_p = _dot_nt(qm, kp) * DIL_SCALE + bias_ref[h, :, 0:BLK]
                s_c = _dot_nt(qm, kc) * DIL_SCALE + bias_ref[h, :, BLK:2 * BLK]
                s_p = jnp.where(vprev, s_p, NEG)
                s_c = jnp.where(vcur, s_c, NEG)
                m = jnp.maximum(jnp.max(s_p, axis=-1, keepdims=True), jnp.max(s_c, axis=-1, keepdims=True))
                e_p = jnp.exp(s_p - m)
                e_c = jnp.exp(s_c - m)
                l = jnp.sum(e_p, axis=-1, keepdims=True) + jnp.sum(e_c, axis=-1, keepdims=True)
                vpm = jnp.where(hm, vp, jnp.zeros_like(vp))
                vcm = jnp.where(hm, vc, jnp.zeros_like(vc))
                o_h = _dot_nn(e_p.astype(BF16), vpm) + _dot_nn(e_c.astype(BF16), vcm)
                o_pair = o_pair + o_h / l
                lse_pair = jnp.where(hm, m + jnp.log(l), lse_pair)
            o_ref[0, :, sl] = o_pair
            lse_ref[0, :, sl] = lse_pair

    cur = pl.BlockSpec((1, BLK, P_QKV), lambda b, r, i: (b, i, r))
    prev = pl.BlockSpec((1, BLK, P_QKV), lambda b, r, i: (b, jnp.maximum(i - 1, 0), r))
    out = pl.BlockSpec((1, BLK, D_A), lambda b, r, i: (b, i, r))
    o, lse = pl.pallas_call(
        body, name=name,
        out_shape=[jax.ShapeDtypeStruct((B, n, d * D_A), F32)] * 2,
        grid=(B, d, nb),
        in_specs=[cur, prev,
                  pl.BlockSpec((None, N_HEADS, BLK, 2 * BLK), lambda b, r, i: (branch, 0, 0, 0))],
        out_specs=[out, out],
        compiler_params=_cparams(("parallel", "parallel", "arbitrary")),
    )(qkv_v, qkv_v, bias)
    return o.reshape(B, S, D_A), lse.reshape(B, S, D_A)


def _dil_merge(os_, lses, name):
    B, S, W = os_[0].shape
    ts = 512

    def body(o0, o1, o2, l0, l1, l2, out_ref, L_ref):
        a0, a1, a2 = l0[0], l1[0], l2[0]
        m = jnp.maximum(jnp.maximum(a0, a1), a2)
        e0, e1, e2 = jnp.exp(a0 - m), jnp.exp(a1 - m), jnp.exp(a2 - m)
        ssum = e0 + e1 + e2
        out_ref[0] = (e0 * o0[0] + e1 * o1[0] + e2 * o2[0]) / ssum
        L_ref[0] = m + jnp.log(ssum)

    tok = pl.BlockSpec((1, ts, W), lambda b, s: (b, s, 0))
    return pl.pallas_call(
        body, name=name, out_shape=[jax.ShapeDtypeStruct((B, S, W), F32)] * 2, grid=(B, S // ts),
        in_specs=[tok] * 6, out_specs=[tok, tok], compiler_params=_cparams(("parallel", "parallel")),
    )(*os_, *lses)


def _dil_bwd(qkv, do, out_a, L, bias, branch, dilation, name):
    B, S, _ = qkv.shape
    d = dilation
    n = S // d
    nb = n // BLK
    qkv_v = qkv.reshape(B, n, d * P_QKV)
    do_v = do.reshape(B, n, d * D_A)
    oa_v = out_a.reshape(B, n, d * D_A)
    L_v = L.reshape(B, n, d * D_A)
    npair = N_HEADS // 2
    multi = nb > 1

    def body(*refs):
        if multi:
            (cur_ref, prev_ref, next_ref, do_ref, don_ref, oa_ref, oan_ref, L_ref, Ln_ref, bias_ref,
             dqkv_ref, dbias_ref) = refs
        else:
            cur_ref, do_ref, oa_ref, L_ref, bias_ref, dqkv_ref, dbias_ref = refs
        b, r, i = pl.program_id(0), pl.program_id(1), pl.program_id(2)

        @pl.when((b == 0) & (r == 0) & (i == 0))
        def _():
            dbias_ref[...] = jnp.zeros_like(dbias_ref)

        vprev, vcur = _band_valid()
        has_prev = i > 0
        has_next = i < nb - 1
        for p in range(npair):
            sl = slice(p * LANES, (p + 1) * LANES)
            ksl = slice(D_A + p * LANES, D_A + (p + 1) * LANES)
            vsl = slice(2 * D_A + p * LANES, 2 * D_A + (p + 1) * LANES)
            q, kc, vc = cur_ref[0, :, sl], cur_ref[0, :, ksl], cur_ref[0, :, vsl]
            dov = do_ref[0, :, sl]
            dd = dov.astype(F32) * oa_ref[0, :, sl]
            Lv = L_ref[0, :, sl]
            if multi:
                kp, vp = prev_ref[0, :, ksl], prev_ref[0, :, vsl]
                qn = next_ref[0, :, sl]
                donv = don_ref[0, :, sl]
                ddn = donv.astype(F32) * oan_ref[0, :, sl]
                Lnv = Ln_ref[0, :, sl]
            dq_pair = jnp.zeros((BLK, LANES), F32)
            dk_pair = jnp.zeros((BLK, LANES), F32)
            dv_pair = jnp.zeros((BLK, LANES), F32)
            for hh in range(2):
                h = 2 * p + hh
                hm = _head_mask((BLK, LANES), hh)
                zero = jnp.zeros_like(q)
                qm = jnp.where(hm, q, zero)
                dom = jnp.where(hm, dov, zero)
                delta = jnp.sum(jnp.where(hm, dd, 0.0), axis=-1, keepdims=True)
                lse = Lv[:, hh * HEAD_DIM:hh * HEAD_DIM + 1]
                s_c = _dot_nt(qm, kc) * DIL_SCALE + bias_ref[h, :, BLK:2 * BLK]
                p_c = jnp.where(vcur, jnp.exp(s_c - lse), 0.0)
                ds_c = p_c * (_dot_nt(dom, vc) - delta)
                ds_cb = ds_c.astype(BF16)
                dq_h = _dot_nn(ds_cb, kc)
                dk_h = _dot_tn(ds_cb, qm)
                dv_h = _dot_tn(p_c.astype(BF16), dom)
                dbias_ref[h, :, BLK:2 * BLK] += ds_c
                if multi:
                    s_p = _dot_nt(qm, kp) * DIL_SCALE + bias_ref[h, :, 0:BLK]
                    p_p = jnp.where(vprev & has_prev, jnp.exp(s_p - lse), 0.0)
                    ds_p = p_p * (_dot_nt(dom, vp) - delta)
                    dq_h = dq_h + _dot_nn(ds_p.astype(BF16), kp)
                    dbias_ref[h, :, 0:BLK] += ds_p
                    qnm = jnp.where(hm, qn, zero)
                    donm = jnp.where(hm, donv, zero)
                    delta_n = jnp.sum(jnp.where(hm, ddn, 0.0), axis=-1, keepdims=True)
                    lse_n = Lnv[:, hh * HEAD_DIM:hh * HEAD_DIM + 1]
                    s_n = _dot_nt(qnm, kc) * DIL_SCALE + bias_ref[h, :, 0:BLK]
                    p_n = jnp.where(vprev & has_next, jnp.exp(s_n - lse_n), 0.0)
                    ds_n = p_n * (_dot_nt(donm, vc) - delta_n)
                    dk_h = dk_h + _dot_tn(ds_n.astype(BF16), qnm)
                    dv_h = dv_h + _dot_tn(p_n.astype(BF16), donm)
                dq_pair = dq_pair + jnp.where(hm, dq_h, 0.0) * DIL_SCALE
                dk_pair = dk_pair + jnp.where(hm, dk_h, 0.0) * DIL_SCALE
                dv_pair = dv_pair + jnp.where(hm, dv_h, 0.0)
            dqkv_ref[0, :, sl] = dq_pair
            dqkv_ref[0, :, ksl] = dk_pair
            dqkv_ref[0, :, vsl] = dv_pair

    def at(off):
        return lambda b, r, i: (b, jnp.clip(i + off, 0, nb - 1), r)

    qkv_spec = lambda off: pl.BlockSpec((1, BLK, P_QKV), at(off))
    da_spec = lambda off: pl.BlockSpec((1, BLK, D_A), at(off))
    bias_spec = pl.BlockSpec((None, N_HEADS, BLK, 2 * BLK), lambda b, r, i: (branch, 0, 0, 0))
    dbias_spec = pl.BlockSpec((N_HEADS, BLK, 2 * BLK), lambda b, r, i: (0, 0, 0))
    if multi:
        in_specs = [qkv_spec(0), qkv_spec(-1), qkv_spec(1), da_spec(0), da_spec(1), da_spec(0), da_spec(1),
                    da_spec(0), da_spec(1), bias_spec]
        args = [qkv_v, qkv_v, qkv_v, do_v, do_v, oa_v, oa_v, L_v, L_v, bias]
    else:
        in_specs = [qkv_spec(0), da_spec(0), da_spec(0), da_spec(0), bias_spec]
        args = [qkv_v, do_v, oa_v, L_v, bias]
    dqkv, dbias = pl.pallas_call(
        body, name=name,
        out_shape=[jax.ShapeDtypeStruct((B, n, d * P_QKV), F32),
                   jax.ShapeDtypeStruct((N_HEADS, BLK, 2 * BLK), F32)],
        grid=(B, d, nb),
        in_specs=in_specs,
        out_specs=[qkv_spec(0), dbias_spec],
        compiler_params=_cparams(("arbitrary", "arbitrary", "arbitrary")),
    )(*args)
    return dqkv.reshape(B, S, P_QKV), dbias


def _sum3_bf16(a, b, c, name):
    B, S, W = a.shape
    ts = 512

    def body(a_ref, b_ref, c_ref, o_ref):
        o_ref[...] = (a_ref[...] + b_ref[...] + c_ref[...]).astype(o_ref.dtype)

    tok = pl.BlockSpec((1, ts, W), lambda b, s: (b, s, 0))
    return pl.pallas_call(
        body, name=name, out_shape=jax.ShapeDtypeStruct((B, S, W), BF16), grid=(B, S // ts),
        in_specs=[tok] * 3, out_specs=tok, compiler_params=_cparams(("parallel", "parallel")),
    )(a, b, c)


def _bias_tables(rel_bias, buckets, name):
    nbr = buckets.shape[0]

    def body(rb_ref, bk_ref, o_ref):
        h = pl.program_id(1)
        tab = bk_ref[0]

        def step(bkt, acc):
            return jnp.where(tab == bkt, rb_ref[bkt, h], acc)

        o_ref[0, 0] = lax.fori_loop(0, N_BUCKETS, step, jnp.zeros((BLK, 2 * BLK), F32))

    return pl.pallas_call(
        body, name=name, out_shape=jax.ShapeDtypeStruct((nbr, N_HEADS, BLK, 2 * BLK), F32),
        grid=(nbr, N_HEADS),
        in_specs=[pl.BlockSpec(memory_space=pltpu.SMEM),
                  pl.BlockSpec((1, BLK, 2 * BLK), lambda i, h: (i, 0, 0))],
        out_specs=pl.BlockSpec((1, 1, BLK, 2 * BLK), lambda i, h: (i, h, 0, 0)),
        compiler_params=_cparams(("parallel", "arbitrary")),
    )(rel_bias, buckets)


def _bias_grad(dbias_list, buckets, name):
    nbr = len(dbias_list)

    def body(*refs):
        d_refs, bk_ref, o_ref = refs[:nbr], refs[nbr], refs[nbr + 1]
        lane = lax.broadcasted_iota(jnp.int32, (1, LANES), 1)
        for h in range(N_HEADS):
            def step(bkt, acc):
                tot = jnp.zeros((1, 1), F32)
                for bi in range(nbr):
                    sel = jnp.where(bk_ref[bi] == bkt, d_refs[bi][h], 0.0)
                    tot = tot + jnp.sum(jnp.sum(sel, axis=1, keepdims=True), axis=0, keepdims=True)
                return acc + jnp.where(lane == bkt, tot, 0.0)

            o_ref[h:h + 1, :] = lax.fori_loop(0, N_BUCKETS, step, jnp.zeros((1, LANES), F32))

    band = pl.BlockSpec((N_HEADS, BLK, 2 * BLK), lambda i: (0, 0, 0))
    return pl.pallas_call(
        body, name=name, out_shape=jax.ShapeDtypeStruct((N_HEADS, LANES), F32), grid=(1,),
        in_specs=[band] * nbr + [pl.BlockSpec((nbr, BLK, 2 * BLK), lambda i: (0, 0, 0))],
        out_specs=pl.BlockSpec((N_HEADS, LANES), lambda i: (0, 0)),
        compiler_params=_cparams(("arbitrary",)),
    )(*dbias_list, buckets)


MLA_TQ = 256
MLA_TK = 256


LOG2E = math.log2(math.e)
MLA_C = MLA_SCALE * LOG2E


def _key_le_query(tk, tq):
    return lax.broadcasted_iota(jnp.int32, (tk, tq), 0) <= lax.broadcasted_iota(jnp.int32, (tk, tq), 1)


def _row_mask(shape, hh):
    row = lax.broadcasted_iota(jnp.int32, shape, 0)
    return (row >= hh * HEAD_DIM) & (row < (hh + 1) * HEAD_DIM)


def _mla_fwd_t(q, k, vt, name):
    B, S, _ = q.shape
    tq, tk = MLA_TQ, MLA_TK
    assert tq == tk
    npair = N_HEADS // 2
    nq = S // tq

    def body(q_ref, k_ref, vt_ref, o_ref, lse_ref, acc_s):
        i = pl.program_id(2)
        acc_s[...] = jnp.zeros_like(acc_s)
        qs = [q_ref[0, :, hh * LANES:(hh + 1) * LANES] for hh in range(2)]
        diag = _key_le_query(tk, tq)

        def step(j, ms, masked):
            kj = k_ref[0, pl.ds(pl.multiple_of(j * tk, tk), tk), :]
            vj = vt_ref[0, 0, j]
            out = []
            for hh in range(2):
                s = _dot_nt(kj[:, hh * LANES:(hh + 1) * LANES], qs[hh])
                if masked:
                    s = jnp.where(diag, s, NEG)
                m_new = jnp.maximum(ms[hh], jnp.max(s, axis=0, keepdims=True))
                alpha = jnp.exp2((ms[hh] - m_new) * MLA_C)
                e = jnp.exp2((s - m_new) * MLA_C).astype(BF16)
                vh = jnp.where(_row_mask(vj.shape, hh), vj, jnp.ones_like(vj))
                acc_s[hh] = acc_s[hh] * alpha + _dot_nn(vh, e)
                out.append(m_new)
            return tuple(out)

        m0 = jnp.full((1, tq), NEG, F32)
        ms = lax.fori_loop(0, i, lambda j, c: step(j, c, False), (m0, m0))
        ms = step(i, ms, True)
        rows0 = _row_mask((LANES, tq), 0)
        l0 = acc_s[0, HEAD_DIM:HEAD_DIM + 1, :]
        l1 = acc_s[1, 0:1, :]
        o_ref[0] = jnp.where(rows0, acc_s[0] / l0, acc_s[1] / l1)
        lse_ref[0, 0, 0] = jnp.zeros((8, tq), F32)
        lse_ref[0, 0, 0, 0:1, :] = ms[0] * MLA_C + jnp.log(l0) * LOG2E
        lse_ref[0, 0, 0, 1:2, :] = ms[1] * MLA_C + jnp.log(l1) * LOG2E

    return pl.pallas_call(
        body, name=name,
        out_shape=[jax.ShapeDtypeStruct((B, D_B, S), F32), jax.ShapeDtypeStruct((B, npair, nq, 8, tq), F32)],
        grid=(B, npair, nq),
        in_specs=[pl.BlockSpec((1, tq, 2 * LANES), lambda b, p, i: (b, i, p)),
                  pl.BlockSpec((1, S, 2 * LANES), lambda b, p, i: (b, 0, p)),
                  pl.BlockSpec((1, 1, S // tk, LANES, tk), lambda b, p, i: (b, p, 0, 0, 0))],
        out_specs=[pl.BlockSpec((1, LANES, tq), lambda b, p, i: (b, p, i)),
                   pl.BlockSpec((1, 1, 1, 8, tq), lambda b, p, i: (b, p, i, 0, 0))],
        scratch_shapes=[pltpu.VMEM((2, LANES, tq), F32)],
        compiler_params=_cparams(("arbitrary", "arbitrary", "arbitrary")),
    )(q, k, vt)


def _mla_delta(do, o, name):
    B, S, _ = o.shape
    tq = MLA_TQ
    npair = N_HEADS // 2

    def body(do_ref, o_ref, d_ref):
        prod_t = jnp.transpose(do_ref[0].astype(F32) * o_ref[0])
        d_ref[0, 0, 0] = jnp.zeros((8, tq), F32)
        d_ref[0, 0, 0, 0:1, :] = jnp.sum(prod_t[:HEAD_DIM], axis=0, keepdims=True)
        d_ref[0, 0, 0, 1:2, :] = jnp.sum(prod_t[HEAD_DIM:], axis=0, keepdims=True)

    tok = pl.BlockSpec((1, tq, LANES), lambda b, p, i: (b, i, p))
    return pl.pallas_call(
        body, name=name, out_shape=jax.ShapeDtypeStruct((B, npair, S // tq, 8, tq), F32),
        grid=(B, npair, S // tq), in_specs=[tok, tok],
        out_specs=pl.BlockSpec((1, 1, 1, 8, tq), lambda b, p, i: (b, p, i, 0, 0)),
        compiler_params=_cparams(("parallel", "parallel", "parallel")),
    )(do, o)


def _mla_bwd_t(q, k, v, do, lse, delta, name):
    B, S, _ = q.shape
    tq, tk = MLA_TQ, MLA_TK
    assert tq == tk
    npair = N_HEADS // 2
    nq = S // tq

    def body(q_ref, do_ref, lse_ref, dl_ref, k_ref, v_ref, dk_ref, dv_ref, dq_ref, dk_s, dv_s):
        j = pl.program_id(2)

        @pl.when(j == 0)
        def _():
            dq_ref[...] = jnp.zeros_like(dq_ref)

        vj = v_ref[0]
        dv_s[...] = jnp.zeros_like(dv_s)
        diag = _key_le_query(tk, tq)
        for hh in range(2):
            hsl = slice(hh * LANES, (hh + 1) * LANES)
            hm = _head_mask((tq, LANES), hh)
            kh = k_ref[0, :, hsl]
            kt = jnp.transpose(kh.astype(F32)).astype(BF16)
            dk_s[...] = jnp.zeros_like(dk_s)

            def step(i, masked):
                rows = pl.ds(pl.multiple_of(i * tq, tq), tq)
                qi = q_ref[0, rows, hsl]
                dov = do_ref[0, rows, :]
                dom = jnp.where(hm, dov, jnp.zeros_like(dov))
                s = _dot_nt(kh, qi)
                pr = jnp.exp2(s * MLA_C - lse_ref[0, 0, i, hh:hh + 1, :])
                if masked:
                    pr = jnp.where(diag, pr, 0.0)
                dv_s[...] += _dot_nn(pr.astype(BF16), dom)
                ds = (pr * (_dot_nt(vj, dom) - dl_ref[0, 0, i, hh:hh + 1, :])).astype(BF16)
                dk_s[...] += _dot_nn(ds, qi)
                dq_ref[0, 0, i, hsl, :] += _dot_nn(kt, ds) * MLA_SCALE

            step(j, True)

            def loop_body(i, carry):
                step(i, False)
                return carry

            lax.fori_loop(j + 1, nq, loop_body, 0)
            dk_ref[0, :, hsl] = dk_s[...] * MLA_SCALE
        dv_ref[0] = dv_s[...]

    stat = pl.BlockSpec((1, 1, nq, 8, tq), lambda b, p, j: (b, p, 0, 0, 0))
    return pl.pallas_call(
        body, name=name,
        out_shape=[jax.ShapeDtypeStruct((B, S, N_HEADS * LANES), F32), jax.ShapeDtypeStruct((B, S, D_B), F32),
                   jax.ShapeDtypeStruct((B, npair, nq, 2 * LANES, tq), F32)],
        grid=(B, npair, S // tk),
        in_specs=[pl.BlockSpec((1, S, 2 * LANES), lambda b, p, j: (b, 0, p)),
                  pl.BlockSpec((1, S, LANES), lambda b, p, j: (b, 0, p)),
                  stat, stat,
                  pl.BlockSpec((1, tk, 2 * LANES), lambda b, p, j: (b, j, p)),
                  pl.BlockSpec((1, tk, LANES), lambda b, p, j: (b, j, p))],
        out_specs=[pl.BlockSpec((1, tk, 2 * LANES), lambda b, p, j: (b, j, p)),
                   pl.BlockSpec((1, tk, LANES), lambda b, p, j: (b, j, p)),
                   pl.BlockSpec((1, 1, nq, 2 * LANES, tq), lambda b, p, j: (b, p, 0, 0, 0))],
        scratch_shapes=[pltpu.VMEM((tk, LANES), F32), pltpu.VMEM((tk, LANES), F32)],
        compiler_params=_cparams(("arbitrary", "arbitrary", "arbitrary")),
    )(q, do, lse, delta, k, v)


def _local_step(x, target, mod, wts, gains, rel_bias):
    B, S, D = x.shape
    T = B * S
    sh1, sc1, g1, sh2, sc2, g2 = [mod[:, i * D:(i + 1) * D].reshape(B, 1, D) for i in range(N_MOD)]
    cs, sn = _rope_tables()
    buckets = np.stack([_band_buckets(d) for d in DILATIONS])
    buckets_dev = jnp.asarray(buckets)
    bias = _bias_tables(rel_bias, buckets_dev, "rel_bias_tables")
    w_in = wts["w_in"]

    h1 = _adaln_fwd(x, gains["g_norm1"], sc1, sh1, "adaln1_fwd")
    h1f = h1.reshape(T, D)
    qkv = _mm(h1f, w_in[:, :P_QKV], "nn", BF16, "mm_qkv").reshape(B, S, P_QKV)
    rest = _mm(h1f, w_in[:, P_QKV:], "nn", F32, "mm_rest")
    o_d, lse_d = [], []
    for i, d in enumerate(DILATIONS):
        o_i, lse_i = _dil_fwd(qkv, bias, i, d, f"dil_fwd_{d}")
        o_d.append(o_i)
        lse_d.append(lse_i)
    out_a, lse_a = _dil_merge(o_d, lse_d, "dil_merge")
    cqn = _rms_fwd(rest, 1, Q_LORA, gains["g_cq"], "rms_cq_fwd")
    ckvn = _rms_fwd(rest, 0, KV_LORA, gains["g_ckv"], "rms_ckv_fwd")
    rest3 = rest.reshape(B, S, P_REST)
    q_raw = _mm(cqn, wts["w_uq"], "nn", F32, "mm_uq").reshape(B, S, N_HEADS * LANES)
    qc = _rope_apply(q_raw, cs, sn, BF16, "rope_q")
    kn_raw = _mm(ckvn, wts["w_kv"][:, :N_HEADS * LANES], "nn", F32, "mm_uk").reshape(B, S, N_HEADS * LANES)
    kc = _rope_apply(kn_raw, cs, sn, BF16, "rope_k", add=rest3, add_blk=KV_LORA // LANES)
    v = _mm(ckvn, wts["w_kv"][:, N_HEADS * LANES:], "nn", BF16, "mm_uv").reshape(B, S, D_B)
    vt = jnp.transpose(v.reshape(B, S // MLA_TK, MLA_TK, N_HEADS // 2, LANES), (0, 3, 1, 4, 2))
    o_t, lse_b = _mla_fwd_t(qc, kc, vt, "mla_fwd")
    out_b = jnp.transpose(o_t, (0, 2, 1))
    out_af, out_bf = out_a.reshape(T, D_A), out_b.reshape(T, D_B)
    ya = _rms_fwd(out_af, 0, D_A, gains["g_out_a"], "rms_outa_fwd")
    yb = _rms_fwd(out_bf, 0, D_B, gains["g_out_b"], "rms_outb_fwd")
    y = jnp.concatenate([ya, yb], axis=1)
    mix = _mm(y, wts["w_out"], "nn", F32, "mm_out").reshape(B, S, D)
    h2, x1 = _adaln_fwd(x, gains["g_norm2"], sc2, sh2, "adaln2_fwd", mix=mix, gate=g1)
    h2f = h2.reshape(T, D)
    gu = _mm(h2f, wts["w_ffn_in"], "nn", F32, "mm_ffn_in", col_blocks=N_CHIP)
    act = _swiglu_fwd(gu, "swiglu_fwd")
    f = _mm(act, wts["w_ffn_out"], "nn", F32, "mm_ffn_out").reshape(B, S, D)
    dx2, df, dg2, dg_final, loss = _final_loss(x1, f, g2, gains["g_final"], target, "final_loss")

    dff = df.reshape(T, D)
    da = _mm(dff, wts["w_ffn_out"], "nt", F32, "mm_ffn_out_dx")
    gw_ffn_out = _mm(act, dff, "tn", F32, "mm_ffn_out_dw")
    dgu = _swiglu_bwd(da, gu, "swiglu_bwd")
    dh2 = _mm(dgu, wts["w_ffn_in"], "nt", F32, "mm_ffn_in_dx", col_blocks=N_CHIP).reshape(B, S, D)
    gw_ffn_in = _mm(h2f, dgu, "tn", F32, "mm_ffn_in_dw", col_blocks=N_CHIP)
    dx1, dsh2, dsc2, dg_norm2, dg1, dmix = _adaln_bwd(dh2, x1, gains["g_norm2"], sc2, dx2, "adaln2_bwd",
                                                      mix=mix, gate=g1)
    dmixf = dmix.reshape(T, D)
    dy = _mm(dmixf, wts["w_out"], "nt", F32, "mm_out_dx")
    gw_out = _mm(y, dmixf, "tn", F32, "mm_out_dw")
    do_a, dg_out_a = _rms_bwd(dy, 0, out_af, 0, D_A, gains["g_out_a"], "rms_outa_bwd")
    do_b, dg_out_b = _rms_bwd(dy, 1, out_bf, 0, D_B, gains["g_out_b"], "rms_outb_bwd")
    do_b3 = do_b.reshape(B, S, D_B)
    delta_b = _mla_delta(do_b3, out_b, "mla_delta")
    dkc, dv, dq_t = _mla_bwd_t(qc, kc, v, do_b3, lse_b, delta_b, "mla_bwd")
    dqc = jnp.transpose(dq_t, (0, 2, 4, 1, 3)).reshape(B, S, N_HEADS * LANES)
    dq_raw = _rope_apply(dqc, cs, -sn, BF16, "rope_q_bwd").reshape(T, N_HEADS * LANES)
    dkrw = _krope_bwd(dkc, cs, -sn, "rope_k_bwd").reshape(T, LANES)
    dcqn = _mm(dq_raw, wts["w_uq"], "nt", F32, "mm_uq_dx")
    gw_uq = _mm(cqn, dq_raw, "tn", F32, "mm_uq_dw")
    dkv = jnp.concatenate([dkc.reshape(T, -1), dv.reshape(T, -1)], axis=1).astype(BF16)
    dckvn = _mm(dkv, wts["w_kv"], "nt", F32, "mm_ukv_dx")
    gw_kv = _mm(ckvn, dkv, "tn", F32, "mm_ukv_dw")
    dcq, dg_cq = _rms_bwd(dcqn, 0, rest, 1, Q_LORA, gains["g_cq"], "rms_cq_bwd")
    dckv, dg_ckv = _rms_bwd(dckvn, 0, rest, 0, KV_LORA, gains["g_ckv"], "rms_ckv_bwd")
    do_a3 = do_a.reshape(B, S, D_A)
    dqkv_d, dbias_d = [], []
    for i, d in enumerate(DILATIONS):
        dqkv_i, dbias_i = _dil_bwd(qkv, do_a3, out_a, lse_a, bias, i, d, f"dil_bwd_{d}")
        dqkv_d.append(dqkv_i)
        dbias_d.append(dbias_i)
    dqkv = _sum3_bf16(*dqkv_d, "dil_bwd_sum").reshape(T, P_QKV)
    g_rel_bias = _bias_grad(dbias_d, buckets_dev, "rel_bias_grad")[:, :N_BUCKETS].T
    dproj = jnp.concatenate([dqkv, dckv, dkrw, dcq], axis=1)
    dh1 = _mm(dproj, w_in, "nt", F32, "mm_in_dx").reshape(B, S, D)
    gw_in = _mm(h1f, dproj, "tn", F32, "mm_in_dw")
    grad_x, dsh1, dsc1, dg_norm1 = _adaln_bwd(dh1, x, gains["g_norm1"], sc1, dx1, "adaln1_bwd")
    gmod = jnp.concatenate([dsh1, dsc1, dg1, dsh2, dsc2, dg2], axis=-1).reshape(B, N_MOD * D)
    grads = dict(w_in=gw_in, w_uq=gw_uq, w_kv=gw_kv, w_out=gw_out, w_ffn_in=gw_ffn_in, w_ffn_out=gw_ffn_out,
                 g_norm1=dg_norm1, g_cq=dg_cq, g_ckv=dg_ckv, rel_bias=g_rel_bias, g_out_a=dg_out_a,
                 g_out_b=dg_out_b, g_norm2=dg_norm2, g_final=dg_final)
    return loss, grad_x, gmod, grads


def _w_in_to_kernel(w):
    z = lambda n: jnp.zeros((w.shape[0], n), w.dtype)
    i3, i4, i5 = 3 * D_A, 3 * D_A + Q_LORA, 3 * D_A + Q_LORA + KV_LORA
    return jnp.concatenate([w[:, :i3], w[:, i4:i5], z(NOPE_DIM), w[:, i5:], z(LANES - NOPE_DIM - ROPE_DIM),
                            w[:, i3:i4]], axis=1)


def _w_in_from_kernel(g):
    o = P_QKV + KV_LORA
    return jnp.concatenate([g[:, :P_QKV], g[:, o + LANES:], g[:, P_QKV:o],
                            g[:, o + NOPE_DIM:o + NOPE_DIM + ROPE_DIM]], axis=1)


def _w_uq_to_kernel(w):
    w3 = w.reshape(Q_LORA, N_HEADS, NOPE_DIM + ROPE_DIM)
    return jnp.pad(w3, ((0, 0), (0, 0), (0, LANES - NOPE_DIM - ROPE_DIM))).reshape(Q_LORA, N_HEADS * LANES)


def _w_uq_from_kernel(g):
    return g.reshape(Q_LORA, N_HEADS, LANES)[:, :, :NOPE_DIM + ROPE_DIM].reshape(Q_LORA, -1)


def _w_ukv_to_kernel(w):
    w3 = w.reshape(KV_LORA, N_HEADS, 2 * HEAD_DIM)
    wk = jnp.pad(w3[:, :, :NOPE_DIM], ((0, 0), (0, 0), (0, LANES - NOPE_DIM))).reshape(KV_LORA, N_HEADS * LANES)
    wv = w3[:, :, NOPE_DIM:].reshape(KV_LORA, D_B)
    return jnp.concatenate([wk, wv], axis=1)


def _w_ukv_from_kernel(g):
    gk = g[:, :N_HEADS * LANES].reshape(KV_LORA, N_HEADS, LANES)[:, :, :NOPE_DIM]
    gv = g[:, N_HEADS * LANES:].reshape(KV_LORA, N_HEADS, HEAD_DIM)
    return jnp.concatenate([gk, gv], axis=2).reshape(KV_LORA, -1)


MESH = pl.DeviceIdType.MESH


def _my_place():
    return lax.axis_index("x"), lax.axis_index("y"), lax.axis_index("c")


def _other_chips(x, y):
    return [(1 - x, y), (x, 1 - y), (1 - x, 1 - y)]


def _allgather8(x_shard, name, in_hbm):
    m_per, n = x_shard.shape
    space = pl.ANY if in_hbm else pltpu.VMEM

    def body(x_ref, out_ref, send_sems, recv_sems, local_sem):
        x, y, c = _my_place()
        me, sibling = (x, y, c), (x, y, 1 - c)
        chips = _other_chips(x, y)

        def rows(px, py, pc):
            return out_ref.at[pl.ds((4 * px + 2 * py + pc) * m_per, m_per), :]

        def copy(k, block, to, src=None):
            return pltpu.make_async_remote_copy(
                src_ref=rows(*block) if src is None else src, dst_ref=rows(*block),
                send_sem=send_sems.at[k], recv_sem=recv_sems.at[k], device_id=to, device_id_type=MESH)

        mine = pltpu.make_async_copy(x_ref, rows(*me), local_sem)
        mine.start()
        first = [copy(0, me, sibling, src=x_ref)]
        first += [copy(1 + j, me, (*chip, c), src=x_ref) for j, chip in enumerate(chips)]
        for cp in first:
            cp.start()
        passed = [copy(4 + j, (*chip, c), sibling) for j, chip in enumerate(chips)]
        for j, chip in enumerate(chips):
            copy(1 + j, (*chip, c), me).wait_recv()
            passed[j].start()
        copy(0, sibling, me).wait_recv()
        for j, chip in enumerate(chips):
            copy(4 + j, (*chip, 1 - c), me).wait_recv()
        for cp in first + passed:
            cp.wait_send()
        mine.wait()

    return pl.pallas_call(
        body, name=name,
        out_shape=jax.ShapeDtypeStruct((N_DEV * m_per, n), x_shard.dtype),
        in_specs=[pl.BlockSpec(memory_space=space)],
        out_specs=pl.BlockSpec(memory_space=space),
        scratch_shapes=[pltpu.SemaphoreType.DMA((7,)), pltpu.SemaphoreType.DMA((7,)), pltpu.SemaphoreType.DMA],
        compiler_params=pltpu.CompilerParams(vmem_limit_bytes=VMEM_LIMIT),
    )(x_shard)


def _hbm_specs(n):
    return [pl.BlockSpec(memory_space=pl.ANY)] * n


def _gather_weights(shards, name):
    n = len(shards)
    halves = [s.shape[0] // 2 for s in shards]

    def body(*refs):
        xs, outs = refs[:n], refs[n:2 * n]
        send_sems, recv_sems, local_sems = refs[2 * n:]
        x, y, c = _my_place()
        me, sibling = (x, y, c), (x, y, 1 - c)
        chips = _other_chips(x, y)

        def blk(k, px, py, pc):
            return outs[k].at[4 * px + 2 * py + pc]

        def mine(k):
            return xs[k].at[pl.ds(pl.multiple_of(c * halves[k], 16), halves[k]), :]

        def copy(k, kind, block, to, own=False):
            return pltpu.make_async_remote_copy(
                src_ref=mine(k) if own else blk(k, *block), dst_ref=blk(k, *block),
                send_sem=send_sems.at[7 * k + kind], recv_sem=recv_sems.at[7 * k + kind],
                device_id=to, device_id_type=MESH)

        local = [pltpu.make_async_copy(mine(k), blk(k, *me), local_sems.at[k]) for k in range(n)]
        for cp in local:
            cp.start()
        first = []
        for k in range(n):
            first.append(copy(k, 0, me, sibling, own=True))
            first += [copy(k, 1 + j, me, (*chip, c), own=True) for j, chip in enumerate(chips)]
        for cp in first:
            cp.start()
        passed = []
        for j, chip in enumerate(chips):
            for k in range(n):
                copy(k, 1 + j, (*chip, c), me).wait_recv()
                fwd = copy(k, 4 + j, (*chip, c), sibling)
                fwd.start()
                passed.append(fwd)
        for k in range(n):
            copy(k, 0, sibling, me).wait_recv()
        for j, chip in enumerate(chips):
            for k in range(n):
                copy(k, 4 + j, (*chip, 1 - c), me).wait_recv()
        for cp in first + passed:
            cp.wait_send()
        for cp in local:
            cp.wait()

    return pl.pallas_call(
        body, name=name,
        out_shape=[jax.ShapeDtypeStruct((N_DEV, h, s.shape[1]), s.dtype) for h, s in zip(halves, shards)],
        in_specs=_hbm_specs(n), out_specs=_hbm_specs(n),
        scratch_shapes=[pltpu.SemaphoreType.DMA((7 * n,)), pltpu.SemaphoreType.DMA((7 * n,)),
                        pltpu.SemaphoreType.DMA((n,))],
    )(*shards)


def _rs_to_sibling(g8s, name):
    n = len(g8s)

    def body(*refs):
        gs, rs = refs[:n], refs[n:2 * n]
        send_sems, recv_sems = refs[2 * n:]
        x, y, c = _my_place()
        copies = [pltpu.make_async_remote_copy(
            src_ref=gs[k].at[2 * s + 1 - c], dst_ref=rs[k].at[s], send_sem=send_sems.at[N_CHIP * k + s],
            recv_sem=recv_sems.at[N_CHIP * k + s], device_id=(x, y, 1 - c), device_id_type=MESH)
            for k in range(n) for s in range(N_CHIP)]
        for cp in copies:
            cp.start()
        for cp in copies:
            cp.wait()

    return pl.pallas_call(
        body, name=name,
        out_shape=[jax.ShapeDtypeStruct((N_CHIP,) + g.shape[1:], g.dtype) for g in g8s],
        in_specs=_hbm_specs(n), out_specs=_hbm_specs(n),
        scratch_shapes=[pltpu.SemaphoreType.DMA((N_CHIP * n,)), pltpu.SemaphoreType.DMA((N_CHIP * n,))],
    )(*g8s)


def _rs_to_chips(a4s, name):
    n = len(a4s)
    nc = N_CHIP - 1

    def body(*refs):
        as_, rs = refs[:n], refs[n:2 * n]
        send_sems, recv_sems = refs[2 * n:]
        x, y, c = _my_place()
        copies = [pltpu.make_async_remote_copy(
            src_ref=as_[k].at[2 * cx + cy], dst_ref=rs[k].at[j], send_sem=send_sems.at[nc * k + j],
            recv_sem=recv_sems.at[nc * k + j], device_id=(cx, cy, c), device_id_type=MESH)
            for k in range(n) for j, (cx, cy) in enumerate(_other_chips(x, y))]
        for cp in copies:
            cp.start()
        for cp in copies:
            cp.wait()

    return pl.pallas_call(
        body, name=name,
        out_shape=[jax.ShapeDtypeStruct((nc,) + a.shape[1:], a.dtype) for a in a4s],
        in_specs=_hbm_specs(n), out_specs=_hbm_specs(n),
        scratch_shapes=[pltpu.SemaphoreType.DMA((nc * n,)), pltpu.SemaphoreType.DMA((nc * n,))],
    )(*a4s)


def _swap_halves(hs, name):
    n = len(hs)

    def body(*refs):
        h_refs, o_refs = refs[:n], refs[n:2 * n]
        send_sems, recv_sems, local_sems = refs[2 * n:]
        x, y, c = _my_place()

        def remote(k, slot):
            return pltpu.make_async_remote_copy(
                src_ref=h_refs[k], dst_ref=o_refs[k].at[slot], send_sem=send_sems.at[k],
                recv_sem=recv_sems.at[k], device_id=(x, y, 1 - c), device_id_type=MESH)

        local = [pltpu.make_async_copy(h_refs[k], o_refs[k].at[c], local_sems.at[k]) for k in range(n)]
        sends = [remote(k, c) for k in range(n)]
        for cp in local + sends:
            cp.start()
        for k in range(n):
            remote(k, 1 - c).wait_recv()
        for cp in sends:
            cp.wait_send()
        for cp in local:
            cp.wait()

    return pl.pallas_call(
        body, name=name,
        out_shape=[jax.ShapeDtypeStruct((2,) + h.shape, h.dtype) for h in hs],
        in_specs=_hbm_specs(n), out_specs=_hbm_specs(n),
        scratch_shapes=[pltpu.SemaphoreType.DMA((n,)), pltpu.SemaphoreType.DMA((n,)), pltpu.SemaphoreType.DMA((n,))],
    )(*hs)


ADD_TILES = 4


def _add_blocks(a_list, a_idx_fn, others_list, ns, sel, name):
    n = len(a_list)
    n_o = len(others_list[0])
    per = 1 + n_o

    def body(sel_ref, *refs):
        for k in range(n):
            ins = refs[k * per:(k + 1) * per]
            o_ref = refs[n * per + k]
            acc = ins[0][0]
            for r in ins[1:]:
                acc = acc + r[0]
            o_ref[0] = acc

    in_specs, args, out_specs, out_shape = [], [], [], []
    for a, others in zip(a_list, others_list):
        _, R, N = a.shape
        tr = R // ADD_TILES
        assert tr % 8 == 0, a.shape
        in_specs.append(pl.BlockSpec((1, tr, N), lambda s, i, sel_ref: (a_idx_fn(s, sel_ref[0]), i, 0)))
        args.append(a)
        for arr, fixed in others:
            if fixed is None:
                in_specs.append(pl.BlockSpec((1, tr, N), lambda s, i, sel_ref: (s, i, 0)))
            else:
                in_specs.append(pl.BlockSpec((1, tr, N), lambda s, i, sel_ref, fixed=fixed: (fixed, i, 0)))
            args.append(arr)
        out_specs.append(pl.BlockSpec((1, tr, N), lambda s, i, sel_ref: (s, i, 0)))
        out_shape.append(jax.ShapeDtypeStruct((ns, R, N), a.dtype))
    grid_spec = pltpu.PrefetchScalarGridSpec(num_scalar_prefetch=1, grid=(ns, ADD_TILES), in_specs=in_specs,
                                             out_specs=out_specs)
    return pl.pallas_call(
        body, name=name, out_shape=out_shape, grid_spec=grid_spec,
        compiler_params=_cparams(("parallel", "parallel")),
    )(sel, *args)


def _reduce_scatter(g8s, tag):
    x, y, c = _my_place()
    c_sel = jnp.reshape(c, (1,)).astype(jnp.int32)
    s_sel = jnp.reshape(2 * x + y, (1,)).astype(jnp.int32)
    r1 = _rs_to_sibling(g8s, f"rs_to_sibling_{tag}")
    a4 = _add_blocks(g8s, lambda s, cc: 2 * s + cc, [[(r, None)] for r in r1], N_CHIP, c_sel,
                     f"rs_add_sibling_{tag}")
    r2 = _rs_to_chips(a4, f"rs_to_chips_{tag}")
    h = _add_blocks(a4, lambda s, ss: ss, [[(r, 0), (r, 1), (r, 2)] for r in r2], 1, s_sel,
                    f"rs_add_chips_{tag}")
    full = _swap_halves([hk.reshape(hk.shape[1:]) for hk in h], f"rs_swap_halves_{tag}")
    return [f.reshape(2 * f.shape[1], f.shape[2]) for f in full]


def _ada_fwd(c_all, w_ada, b_ada, name):
    nb, D = c_all.shape
    ncol = w_ada.shape[1]
    tc = 512

    def body(c_ref, w_ref, b_ref, o_ref):
        cv = c_ref[...]
        cond = (cv * jax.nn.sigmoid(cv)).astype(BF16)
        o_ref[...] = jnp.dot(cond, w_ref[...].astype(BF16), preferred_element_type=F32) + b_ref[...]

    return pl.pallas_call(
        body, name=name, out_shape=jax.ShapeDtypeStruct((nb, ncol), F32), grid=(ncol // tc,),
        in_specs=[pl.BlockSpec((nb, D), lambda j: (0, 0)), pl.BlockSpec((D, tc), lambda j: (0, j)),
                  pl.BlockSpec((1, tc), lambda j: (0, j))],
        out_specs=pl.BlockSpec((nb, tc), lambda j: (0, j)),
        compiler_params=_cparams(("parallel",)),
    )(c_all, w_ada, b_ada)


def _ada_bwd(c_all, gmod_cols, name):
    nb, D = c_all.shape
    ncol = gmod_cols.shape[1]
    tc = 512

    def body(c_ref, g_ref, o_ref):
        cv = c_ref[...]
        cond = (cv * jax.nn.sigmoid(cv)).astype(BF16)
        o_ref[...] = _dot_tn(cond, g_ref[...].astype(BF16))

    return pl.pallas_call(
        body, name=name, out_shape=jax.ShapeDtypeStruct((D, ncol), F32), grid=(ncol // tc,),
        in_specs=[pl.BlockSpec((nb, D), lambda j: (0, 0)), pl.BlockSpec((nb, tc), lambda j: (0, j))],
        out_specs=pl.BlockSpec((D, tc), lambda j: (0, j)),
        compiler_params=_cparams(("parallel",)),
    )(c_all, gmod_cols)


def _adam_math(w, g, m, v):
    m = ADAM_B1 * m + (1.0 - ADAM_B1) * g
    v = ADAM_B2 * v + (1.0 - ADAM_B2) * (g * g)
    m_hat = m / (1.0 - ADAM_B1 ** ADAM_STEP)
    v_hat = v / (1.0 - ADAM_B2 ** ADAM_STEP)
    delta = -ADAM_LR * (m_hat / (jnp.sqrt(v_hat) + ADAM_EPS) + ADAM_WD * w)
    return delta, m, v


def _adamw(w, g, m, v, name):
    rows, cols = w.shape
    tr = _pick(rows, (256, 192, 176, 128, 64, 8))

    def body(w_ref, g_ref, m_ref, v_ref, d_ref, mo_ref, vo_ref):
        d, mn, vn = _adam_math(w_ref[...], g_ref[...], m_ref[...], v_ref[...])
        d_ref[...] = d
        mo_ref[...] = mn
        vo_ref[...] = vn

    spec = pl.BlockSpec((tr, cols), lambda i: (i, 0))
    return pl.pallas_call(
        body, name=name, out_shape=[jax.ShapeDtypeStruct((rows, cols), F32)] * 3, grid=(rows // tr,),
        in_specs=[spec] * 4, out_specs=[spec] * 3, compiler_params=_cparams(("parallel",)),
    )(w, g, m, v)


VEC_ROWS = 8


def _adamw_rows(w, parts, m, v, name):
    n = w.shape[1]
    P = parts.shape[0]
    assert n % (VEC_ROWS * LANES) == 0, n
    shp = (VEC_ROWS, n // VEC_ROWS)

    def body(w_ref, p_ref, m_ref, v_ref, g_ref, d_ref, mo_ref, vo_ref):
        g = p_ref[0]
        for k in range(1, P):
            g = g + p_ref[k]
        d, mn, vn = _adam_math(w_ref[...], g, m_ref[...], v_ref[...])
        g_ref[...] = g
        d_ref[...] = d
        mo_ref[...] = mn
        vo_ref[...] = vn

    vec = pl.BlockSpec(shp, lambda i: (0, 0))
    out = pl.pallas_call(
        body, name=name, out_shape=[jax.ShapeDtypeStruct(shp, F32)] * 4, grid=(1,),
        in_specs=[vec, pl.BlockSpec((P,) + shp, lambda i: (0, 0, 0)), vec, vec], out_specs=[vec] * 4,
        compiler_params=_cparams(("arbitrary",)),
    )(w.reshape(shp), parts.reshape((P,) + shp), m.reshape(shp), v.reshape(shp))
    return [o.reshape(1, n) for o in out]


_PACKED = (("w_in", 1024, 552), ("w_uq", 384, 192), ("w_ukv", 256, 256))
_SHARDED = ("w_in", "w_uq", "w_ukv", "w_out", "w_ffn_in", "w_ffn_out")
_SMALL = (("g_norm1", 1024), ("g_cq", 384), ("g_ckv", 256), ("rel_bias", 256), ("g_out_a", 512),
          ("g_out_b", 512), ("g_norm2", 1024), ("g_final", 1024))
_SMALL_PAD = 5120
PACK_ROWS = 704
_PACK_ELEMS = PACK_ROWS * D_MODEL


def _pack_shards(shards, dtype):
    lead = shards["w_in"].shape[:-2]
    flat = jnp.concatenate([shards[n].astype(dtype).reshape(lead + (-1,)) for n, _, _ in _PACKED], axis=-1)
    pad = [(0, 0)] * len(lead) + [(0, _PACK_ELEMS - flat.shape[-1])]
    return jnp.pad(flat, pad).reshape(lead + (PACK_ROWS, D_MODEL))


def _unpack_shards(packed):
    out, off = {}, 0
    for n, r, c in _PACKED:
        out[n] = packed[..., off:off + r * c].reshape(packed.shape[:-1] + (r, c))
        off += r * c
    return out


def _full_from_shards(sh):
    return jnp.transpose(sh, (1, 0, 2)).reshape(sh.shape[1], -1)


def _shards_from_full(full):
    rows, cols = full.shape
    return jnp.transpose(full.reshape(rows, N_CHIP, cols // N_CHIP), (1, 0, 2))


def kernel(x, c, w_ada, b_ada, g_norm1, w_in, g_cq, w_uq, g_ckv, w_ukv, rel_bias, g_out_a, g_out_b, w_out, g_norm2, w_ffn_in, w_ffn_out, g_final, loss_target, m_w_ada, m_b_ada, m_g_norm1, m_w_in, m_g_cq, m_w_uq, m_g_ckv, m_w_ukv, m_rel_bias, m_g_out_a, m_g_out_b, m_w_out, m_g_norm2, m_w_ffn_in, m_w_ffn_out, m_g_final, v_w_ada, v_b_ada, v_g_norm1, v_w_in, v_g_cq, v_w_uq, v_g_ckv, v_w_ukv, v_rel_bias, v_g_out_a, v_g_out_b, v_w_out, v_g_norm2, v_w_ffn_in, v_w_ffn_out, v_g_final):
    names = ["w_ada", "b_ada", "g_norm1", "w_in", "g_cq", "w_uq", "g_ckv", "w_ukv", "rel_bias", "g_out_a",
             "g_out_b", "w_out", "g_norm2", "w_ffn_in", "w_ffn_out", "g_final"]
    W = dict(zip(names, [w_ada, b_ada, g_norm1, w_in, g_cq, w_uq, g_ckv, w_ukv, rel_bias, g_out_a, g_out_b,
                         w_out, g_norm2, w_ffn_in, w_ffn_out, g_final]))
    M = dict(zip(names, [m_w_ada, m_b_ada, m_g_norm1, m_w_in, m_g_cq, m_w_uq, m_g_ckv, m_w_ukv, m_rel_bias,
                         m_g_out_a, m_g_out_b, m_w_out, m_g_norm2, m_w_ffn_in, m_w_ffn_out, m_g_final]))
    V = dict(zip(names, [v_w_ada, v_b_ada, v_g_norm1, v_w_in, v_g_cq, v_w_uq, v_g_ckv, v_w_ukv, v_rel_bias,
                         v_g_out_a, v_g_out_b, v_w_out, v_g_norm2, v_w_ffn_in, v_w_ffn_out, v_g_final]))
    B, S, D = x.shape
    mx, my, mc = _my_place()
    dev = 4 * mx + 2 * my + mc
    chip = 2 * mx + my
    pad_rows = 8

    c_all = _allgather8(jnp.pad(c, ((0, pad_rows - B), (0, 0))), "ag_c", False)
    c_all = c_all.reshape(N_DEV, pad_rows, D)[:, :B].reshape(N_DEV * B, D)
    ada_cols = w_ada.shape[-1]
    b_cols = lax.dynamic_slice_in_dim(b_ada, chip * ada_cols, ada_cols, axis=1)
    mod_cols = _ada_fwd(c_all, w_ada[0], b_cols, "ada_fwd")
    mod_all = _allgather8(mod_cols, "ag_mod", False).reshape(N_DEV, N_DEV * B, ada_cols)[0::2]
    mod_all = jnp.transpose(mod_all, (1, 0, 2)).reshape(N_DEV * B, N_MOD * D)
    mod = lax.dynamic_slice_in_dim(mod_all, dev * B, B, axis=0)

    packed = _pack_shards({n: W[n][0] for n, _, _ in _PACKED}, BF16)
    g_packed, g_out, g_ffn_in, g_ffn_out = _gather_weights(
        [packed, w_out[0].astype(BF16), w_ffn_in[0].astype(BF16), w_ffn_out[0].astype(BF16)], "ag_weights")
    full = {n: _full_from_shards(sh) for n, sh in _unpack_shards(g_packed.reshape(N_CHIP, _PACK_ELEMS)).items()}
    wts = dict(w_in=_w_in_to_kernel(full["w_in"]), w_uq=_w_uq_to_kernel(full["w_uq"]),
               w_kv=_w_ukv_to_kernel(full["w_ukv"]), w_out=g_out.reshape(D, D),
               w_ffn_in=g_ffn_in.reshape(N_CHIP, D, -1), w_ffn_out=g_ffn_out.reshape(D_FF, D))
    gains = dict(g_norm1=g_norm1, g_cq=g_cq, g_ckv=g_ckv, g_out_a=g_out_a, g_out_b=g_out_b, g_norm2=g_norm2,
                 g_final=g_final.reshape(1, D))

    loss, grad_x, gmod, grads = _local_step(x, loss_target, mod, wts, gains, rel_bias)
    loss = lax.psum(loss[0, 0], ("x", "y", "c"))

    n_small = _SMALL_PAD
    cat = lambda dct: jnp.concatenate([dct[n].reshape(1, -1) for n, _ in _SMALL]
                                      + [jnp.zeros((1, _SMALL_PAD - sum(s for _, s in _SMALL)), F32)], axis=1)
    small = cat(grads)
    rows = jnp.concatenate([gmod, jnp.pad(small, ((0, 0), (0, N_MOD * D - n_small))),
                            jnp.zeros((pad_rows - B - 1, N_MOD * D), F32)], axis=0)
    rows_all = _allgather8(rows, "ag_small", False).reshape(N_DEV, pad_rows, N_MOD * D)
    gmod_all = rows_all[:, :B].reshape(N_DEV * B, N_MOD * D)
    small_parts = rows_all[:, B, :n_small]

    nat = dict(w_in=_w_in_from_kernel(grads["w_in"]), w_uq=_w_uq_from_kernel(grads["w_uq"]),
               w_ukv=_w_ukv_from_kernel(grads["w_kv"]))
    gp = _pack_shards({n: _shards_from_full(nat[n]) for n, _, _ in _PACKED}, F32)
    as_halves = lambda a: a.reshape(N_DEV, -1, a.shape[-1])
    r_packed, r_out, r_ffn_in, r_ffn_out = _reduce_scatter(
        [as_halves(gp), as_halves(grads["w_out"]), as_halves(grads["w_ffn_in"]), as_halves(grads["w_ffn_out"])],
        "all")
    G = _unpack_shards(r_packed.reshape(_PACK_ELEMS))
    G.update(w_out=r_out, w_ffn_in=r_ffn_in, w_ffn_out=r_ffn_out)

    gmod_cols = lax.dynamic_slice_in_dim(gmod_all, chip * ada_cols, ada_cols, axis=1)
    G["w_ada"] = _ada_bwd(c_all, gmod_cols, "ada_bwd")
    delta, new_m, new_v = {}, {}, {}
    for n in ("w_ada",) + _SHARDED:
        shp = W[n].shape
        w2 = W[n].reshape(shp[-2], shp[-1])
        d_, m_, v_ = _adamw(w2, G[n], M[n].reshape(w2.shape), V[n].reshape(w2.shape), f"adamw_{n}")
        G[n], delta[n], new_m[n], new_v[n] = [a.reshape(shp) for a in (G[n], d_, m_, v_)]
    gs, ds_, ms_, vs_ = _adamw_rows(cat(W), small_parts, cat(M), cat(V), "adamw_small")
    off = 0
    for n, sz in _SMALL:
        shp = W[n].shape
        G[n], delta[n], new_m[n], new_v[n] = [a[:, off:off + sz].reshape(shp) for a in (gs, ds_, ms_, vs_)]
        off += sz
    G["b_ada"], delta["b_ada"], new_m["b_ada"], new_v["b_ada"] = _adamw_rows(b_ada, gmod_all, m_b_ada, v_b_ada,
                                                                          "adamw_b_ada")
    return (loss, grad_x, *[G[n] for n in names], *[delta[n] for n in names], *[new_m[n] for n in names],
            *[new_v[n] for n in names])
```

```python
import functools
import math

import numpy as np
import jax
import jax.numpy as jnp
from jax import lax
from jax.experimental import pallas as pl
from jax.experimental.pallas import tpu as pltpu

F32 = jnp.float32
BF16 = jnp.bfloat16

D_MODEL = 1024
SEQ = 2048
N_HEADS = 8
HEAD_DIM = 64
D_A = 512
D_B = 512
Q_LORA = 384
KV_LORA = 256
ROPE_DIM = 32
NOPE_DIM = 64
D_FF = 2816
N_MOD = 6
N_BUCKETS = 32
MAX_DISTANCE = 2048
ROPE_THETA = 10000.0
EPS = 1e-6
NEG = -1e30
BLK = 128
DILATIONS = (1, 4, 16)
SPAN = 128
MLA_SCALE = (NOPE_DIM + ROPE_DIM) ** -0.5
DIL_SCALE = HEAD_DIM ** -0.5

ADAM_LR = 0.001
ADAM_B1 = 0.9
ADAM_B2 = 0.999
ADAM_EPS = 1e-08
ADAM_WD = 0.01
ADAM_STEP = 10

N_DEV = 8
N_CHIP = 4
LANES = 128
VMEM_LIMIT = 48 * 1024 * 1024

P_QKV = 3 * D_A
P_REST = KV_LORA + LANES + Q_LORA


def _cparams(sem=None):
    return pltpu.CompilerParams(dimension_semantics=sem, vmem_limit_bytes=VMEM_LIMIT)


def _pick(n, cands):
    for c in cands:
        if n % c == 0:
            return c
    raise ValueError(f"no tile for {n} in {cands}")


def _mm(a, b, mode, out_dtype, name, col_blocks=None):
    blocked = col_blocks is not None
    if mode == "nn":
        (M, K) = a.shape
        K2, N = (b.shape[1], b.shape[0] * b.shape[2]) if blocked else b.shape
    elif mode == "nt":
        (M, K) = a.shape
        N, K2 = (b.shape[1], b.shape[0] * b.shape[2]) if blocked else b.shape
    else:
        (K, M), (K2, N) = a.shape, b.shape
    assert K == K2, (a.shape, b.shape, mode)
    tm = _pick(M, (512, 384, 256, 128))
    tn = _pick(N, (1408, 1024, 768, 512, 384, 256, 128))
    tk = _pick(K, (1024, 512, 384, 256, 128))
    if blocked and mode == "nt":
        tk = K // col_blocks
    elif blocked:
        tn = N // col_blocks
    nk = K // tk
    out_shape = (M, N)
    out_spec = pl.BlockSpec((tm, tn), lambda i, j, k: (i, j))
    if mode == "nn":
        a_spec = pl.BlockSpec((tm, tk), lambda i, j, k: (i, k))
        b_spec = (pl.BlockSpec((None, tk, tn), lambda i, j, k: (j, k, 0)) if blocked
                  else pl.BlockSpec((tk, tn), lambda i, j, k: (k, j)))
        dn = (((1,), (0,)), ((), ()))
    elif mode == "nt":
        a_spec = pl.BlockSpec((tm, tk), lambda i, j, k: (i, k))
        b_spec = (pl.BlockSpec((None, tn, tk), lambda i, j, k: (k, j, 0)) if blocked
                  else pl.BlockSpec((tn, tk), lambda i, j, k: (j, k)))
        dn = (((1,), (1,)), ((), ()))
    else:
        a_spec = pl.BlockSpec((tk, tm), lambda i, j, k: (k, i))
        b_spec = pl.BlockSpec((tk, tn), lambda i, j, k: (k, j))
        dn = (((0,), (0,)), ((), ()))
        if blocked:
            out_shape = (col_blocks, M, tn)
            out_spec = pl.BlockSpec((None, tm, tn), lambda i, j, k: (j, i, 0))

    def body(a_ref, b_ref, o_ref, acc_ref):
        k = pl.program_id(2)

        @pl.when(k == 0)
        def _():
            acc_ref[...] = jnp.zeros_like(acc_ref)

        acc_ref[...] += lax.dot_general(a_ref[...].astype(BF16), b_ref[...].astype(BF16), dn,
                                        preferred_element_type=F32)

        @pl.when(k == nk - 1)
        def _():
            o_ref[...] = acc_ref[...].astype(o_ref.dtype)

    return pl.pallas_call(
        body, name=name,
        out_shape=jax.ShapeDtypeStruct(out_shape, out_dtype),
        grid=(M // tm, N // tn, nk),
        in_specs=[a_spec, b_spec],
        out_specs=out_spec,
        scratch_shapes=[pltpu.VMEM((tm, tn), F32)],
        compiler_params=_cparams(("parallel", "parallel", "arbitrary")),
    )(a, b)


ROW_TILE = 256


def _adaln_fwd(x, g, sc, sh, name, mix=None, gate=None):
    B, S, D = x.shape
    ts = ROW_TILE
    has_res = mix is not None

    def body(*refs):
        if has_res:
            x_ref, g_ref, sc_ref, sh_ref, mix_ref, gate_ref, h_ref, xr_ref = refs
            xr = x_ref[0] + gate_ref[0] * mix_ref[0]
            xr_ref[0] = xr
        else:
            x_ref, g_ref, sc_ref, sh_ref, h_ref = refs
            xr = x_ref[0]
        r = lax.rsqrt(jnp.mean(xr * xr, axis=-1, keepdims=True) + EPS)
        xn = (xr * r) * g_ref[...]
        h_ref[0] = (xn * (1.0 + sc_ref[0]) + sh_ref[0]).astype(h_ref.dtype)

    tok = pl.BlockSpec((1, ts, D), lambda b, s: (b, s, 0))
    per_b = pl.BlockSpec((1, 1, D), lambda b, s: (b, 0, 0))
    vec = pl.BlockSpec((1, D), lambda b, s: (0, 0))
    in_specs = [tok, vec, per_b, per_b]
    args = [x, g, sc, sh]
    out_shape = [jax.ShapeDtypeStruct((B, S, D), BF16)]
    out_specs = [tok]
    if has_res:
        in_specs += [tok, per_b]
        args += [mix, gate]
        out_shape.append(jax.ShapeDtypeStruct((B, S, D), F32))
        out_specs.append(tok)
    out = pl.pallas_call(
        body, name=name, out_shape=out_shape, grid=(B, S // ts),
        in_specs=in_specs, out_specs=out_specs,
        compiler_params=_cparams(("parallel", "parallel")),
    )(*args)
    return out if has_res else out[0]


def _adaln_bwd(dh, x, g, sc, dres, name, mix=None, gate=None):
    B, S, D = x.shape
    ts = ROW_TILE
    has_res = mix is not None

    def body(*refs):
        if has_res:
            (dh_ref, x_ref, g_ref, sc_ref, dres_ref, mix_ref, gate_ref,
             dx_ref, dsh_ref, dsc_ref, dg_ref, dgate_ref, dmix_ref) = refs
        else:
            (dh_ref, x_ref, g_ref, sc_ref, dres_ref, dx_ref, dsh_ref, dsc_ref, dg_ref) = refs
        b, s = pl.program_id(0), pl.program_id(1)
        xv = x_ref[0]
        dhv = dh_ref[0]
        gv = g_ref[...]
        r = lax.rsqrt(jnp.mean(xv * xv, axis=-1, keepdims=True) + EPS)
        n = xv * r
        xn = n * gv
        dxn = dhv * (1.0 + sc_ref[0])
        dn = dxn * gv
        dx = r * (dn - n * jnp.mean(dn * n, axis=-1, keepdims=True)) + dres_ref[0]
        dx_ref[0] = dx

        @pl.when(s == 0)
        def _():
            dsh_ref[...] = jnp.zeros_like(dsh_ref)
            dsc_ref[...] = jnp.zeros_like(dsc_ref)
            if has_res:
                dgate_ref[...] = jnp.zeros_like(dgate_ref)

        @pl.when((s == 0) & (b == 0))
        def _():
            dg_ref[...] = jnp.zeros_like(dg_ref)

        dsh_ref[0] += jnp.sum(dhv, axis=0, keepdims=True)
        dsc_ref[0] += jnp.sum(dhv * xn, axis=0, keepdims=True)
        dg_ref[...] += jnp.sum(dxn * n, axis=0, keepdims=True)
        if has_res:
            dgate_ref[0] += jnp.sum(dx * mix_ref[0], axis=0, keepdims=True)
            dmix_ref[0] = (dx * gate_ref[0]).astype(dmix_ref.dtype)

    tok = pl.BlockSpec((1, ts, D), lambda b, s: (b, s, 0))
    per_b = pl.BlockSpec((1, 1, D), lambda b, s: (b, 0, 0))
    vec = pl.BlockSpec((1, D), lambda b, s: (0, 0))
    in_specs = [tok, tok, vec, per_b, tok]
    args = [dh, x, g, sc, dres]
    out_shape = [jax.ShapeDtypeStruct((B, S, D), F32), jax.ShapeDtypeStruct((B, 1, D), F32),
                 jax.ShapeDtypeStruct((B, 1, D), F32), jax.ShapeDtypeStruct((1, D), F32)]
    out_specs = [tok, per_b, per_b, vec]
    if has_res:
        in_specs += [tok, per_b]
        args += [mix, gate]
        out_shape += [jax.ShapeDtypeStruct((B, 1, D), F32), jax.ShapeDtypeStruct((B, S, D), BF16)]
        out_specs += [per_b, tok]
    return pl.pallas_call(
        body, name=name, out_shape=out_shape, grid=(B, S // ts),
        in_specs=in_specs, out_specs=out_specs,
        compiler_params=_cparams(("arbitrary", "arbitrary")),
    )(*args)


def _rms_fwd(x, col_blk, n, g, name, n_real=None):
    T = x.shape[0]
    tr = 512
    nr = float(n_real or n)

    def body(x_ref, g_ref, y_ref):
        xv = x_ref[...]
        r = lax.rsqrt(jnp.sum(xv * xv, axis=-1, keepdims=True) / nr + EPS)
        y_ref[...] = ((xv * r) * g_ref[...]).astype(y_ref.dtype)

    return pl.pallas_call(
        body, name=name, out_shape=jax.ShapeDtypeStruct((T, n), BF16), grid=(T // tr,),
        in_specs=[pl.BlockSpec((tr, n), lambda i: (i, col_blk)), pl.BlockSpec((1, n), lambda i: (0, 0))],
        out_specs=pl.BlockSpec((tr, n), lambda i: (i, 0)),
        compiler_params=_cparams(("parallel",)),
    )(x, g)


def _rms_bwd(dy, dy_blk, x, x_blk, n, g, name, out_dtype=BF16):
    T = x.shape[0]
    tr = 512

    def body(dy_ref, x_ref, g_ref, dx_ref, dg_ref):
        xv = x_ref[...]
        dyv = dy_ref[...].astype(F32)
        r = lax.rsqrt(jnp.mean(xv * xv, axis=-1, keepdims=True) + EPS)
        nrm = xv * r
        dn = dyv * g_ref[...]
        dx_ref[...] = (r * (dn - nrm * jnp.mean(dn * nrm, axis=-1, keepdims=True))).astype(dx_ref.dtype)

        @pl.when(pl.program_id(0) == 0)
        def _():
            dg_ref[...] = jnp.zeros_like(dg_ref)

        dg_ref[...] += jnp.sum(dyv * nrm, axis=0, keepdims=True)

    return pl.pallas_call(
        body, name=name,
        out_shape=[jax.ShapeDtypeStruct((T, n), out_dtype), jax.ShapeDtypeStruct((1, n), F32)],
        grid=(T // tr,),
        in_specs=[pl.BlockSpec((tr, n), lambda i: (i, dy_blk)), pl.BlockSpec((tr, n), lambda i: (i, x_blk)),
                  pl.BlockSpec((1, n), lambda i: (0, 0))],
        out_specs=[pl.BlockSpec((tr, n), lambda i: (i, 0)), pl.BlockSpec((1, n), lambda i: (0, 0))],
        compiler_params=_cparams(("arbitrary",)),
    )(dy, x, g)


def _swiglu_fwd(gu, name):
    T = gu.shape[0]
    tr, tc = 512, 1408
    nc = D_FF // tc

    def body(g_ref, u_ref, a_ref):
        gv = g_ref[...]
        a_ref[...] = (gv * jax.nn.sigmoid(gv) * u_ref[...]).astype(a_ref.dtype)

    return pl.pallas_call(
        body, name=name, out_shape=jax.ShapeDtypeStruct((T, D_FF), BF16), grid=(T // tr, nc),
        in_specs=[pl.BlockSpec((tr, tc), lambda i, j: (i, j)), pl.BlockSpec((tr, tc), lambda i, j: (i, j + nc))],
        out_specs=pl.BlockSpec((tr, tc), lambda i, j: (i, j)),
        compiler_params=_cparams(("parallel", "parallel")),
    )(gu, gu)


def _swiglu_bwd(da, gu, name):
    T = gu.shape[0]
    tr, tc = 512, 1408
    nc = D_FF // tc

    def body(da_ref, g_ref, u_ref, dgu_ref):
        j = pl.program_id(1)
        gv, uv, dav = g_ref[...], u_ref[...], da_ref[...]
        sg = jax.nn.sigmoid(gv)

        @pl.when(j < nc)
        def _():
            dgu_ref[...] = (dav * uv * (sg * (1.0 + gv * (1.0 - sg)))).astype(dgu_ref.dtype)

        @pl.when(j >= nc)
        def _():
            dgu_ref[...] = (dav * (gv * sg)).astype(dgu_ref.dtype)

    return pl.pallas_call(
        body, name=name, out_shape=jax.ShapeDtypeStruct((T, 2 * D_FF), BF16), grid=(T // tr, 2 * nc),
        in_specs=[pl.BlockSpec((tr, tc), lambda i, j: (i, j % nc)),
                  pl.BlockSpec((tr, tc), lambda i, j: (i, j % nc)),
                  pl.BlockSpec((tr, tc), lambda i, j: (i, j % nc + nc))],
        out_specs=pl.BlockSpec((tr, tc), lambda i, j: (i, j)),
        compiler_params=_cparams(("parallel", "parallel")),
    )(da, gu, gu)


def _final_loss(x1, f, g2, gf, target, name):
    B, S, D = x1.shape
    ts = ROW_TILE

    def body(x1_ref, f_ref, g2_ref, gf_ref, t_ref, dx_ref, df_ref, dg2_ref, dgf_ref, loss_ref):
        b, s = pl.program_id(0), pl.program_id(1)
        fv = f_ref[0]
        g2v = g2_ref[0]
        gfv = gf_ref[...]
        x2 = x1_ref[0] + g2v * fv
        r = lax.rsqrt(jnp.mean(x2 * x2, axis=-1, keepdims=True) + EPS)
        n = x2 * r
        e = n * gfv - t_ref[0]
        dy = e * (1.0 / D)
        dn = dy * gfv
        dx = r * (dn - n * jnp.mean(dn * n, axis=-1, keepdims=True))
        dx_ref[0] = dx
        df_ref[0] = (dx * g2v).astype(df_ref.dtype)

        @pl.when(s == 0)
        def _():
            dg2_ref[...] = jnp.zeros_like(dg2_ref)

        @pl.when((s == 0) & (b == 0))
        def _():
            dgf_ref[...] = jnp.zeros_like(dgf_ref)
            loss_ref[...] = jnp.zeros_like(loss_ref)

        dg2_ref[0] += jnp.sum(dx * fv, axis=0, keepdims=True)
        dgf_ref[...] += jnp.sum(dy * n, axis=0, keepdims=True)
        loss_ref[...] += 0.5 * jnp.sum(jnp.mean(e * e, axis=-1, keepdims=True), axis=0, keepdims=True)

    tok = pl.BlockSpec((1, ts, D), lambda b, s: (b, s, 0))
    per_b = pl.BlockSpec((1, 1, D), lambda b, s: (b, 0, 0))
    vec = pl.BlockSpec((1, D), lambda b, s: (0, 0))
    return pl.pallas_call(
        body, name=name,
        out_shape=[jax.ShapeDtypeStruct((B, S, D), F32), jax.ShapeDtypeStruct((B, S, D), BF16),
                   jax.ShapeDtypeStruct((B, 1, D), F32), jax.ShapeDtypeStruct((1, D), F32),
                   jax.ShapeDtypeStruct((1, LANES), F32)],
        grid=(B, S // ts),
        in_specs=[tok, tok, per_b, vec, tok],
        out_specs=[tok, tok, per_b, vec, pl.BlockSpec((1, LANES), lambda b, s: (0, 0))],
        compiler_params=_cparams(("arbitrary", "arbitrary")),
    )(x1, f, g2, gf, target)


def _rope_tables():
    half = ROPE_DIM // 2
    inv = ROPE_THETA ** (-jnp.arange(half, dtype=F32) / half)
    ang = jnp.arange(SEQ, dtype=F32)[:, None] * inv[None, :]
    cos, sin = jnp.cos(ang), jnp.sin(ang)
    one = jnp.ones((SEQ, NOPE_DIM), F32)
    zero = jnp.zeros((SEQ, NOPE_DIM), F32)
    cs = jnp.concatenate([one, cos, cos, one[:, :LANES - NOPE_DIM - ROPE_DIM]], axis=1)
    sn = jnp.concatenate([zero, -sin, sin, zero[:, :LANES - NOPE_DIM - ROPE_DIM]], axis=1)
    return cs, sn


def _rope_group(t, cs, sn):
    half = ROPE_DIM // 2
    lane = lax.broadcasted_iota(jnp.int32, t.shape, 1)
    partner = jnp.where(lane < NOPE_DIM + half, pltpu.roll(t, LANES - half, 1), pltpu.roll(t, half, 1))
    return t * cs + partner * sn


def _rope_apply(t, cs, sn, out_dtype, name, add=None, add_blk=0):
    B, S, W = t.shape
    G = W // LANES
    ts = ROW_TILE

    def body(*refs):
        if add is None:
            t_ref, cs_ref, sn_ref, o_ref = refs
            for gi in range(G):
                sl = slice(gi * LANES, (gi + 1) * LANES)
                o_ref[0, :, sl] = _rope_group(t_ref[0, :, sl], cs_ref[...], sn_ref[...]).astype(o_ref.dtype)
        else:
            t_ref, a_ref, cs_ref, sn_ref, o_ref = refs
            ra = _rope_group(a_ref[0], cs_ref[...], sn_ref[...])
            for gi in range(G):
                sl = slice(gi * LANES, (gi + 1) * LANES)
                o_ref[0, :, sl] = (t_ref[0, :, sl] + ra).astype(o_ref.dtype)

    tok = pl.BlockSpec((1, ts, W), lambda b, s: (b, s, 0))
    tab = pl.BlockSpec((ts, LANES), lambda b, s: (s, 0))
    in_specs, args = [tok], [t]
    if add is not None:
        in_specs.append(pl.BlockSpec((1, ts, LANES), lambda b, s: (b, s, add_blk)))
        args.append(add)
    in_specs += [tab, tab]
    args += [cs, sn]
    return pl.pallas_call(
        body, name=name, out_shape=jax.ShapeDtypeStruct((B, S, W), out_dtype), grid=(B, S // ts),
        in_specs=in_specs, out_specs=tok, compiler_params=_cparams(("parallel", "parallel")),
    )(*args)


def _krope_bwd(dkc, cs, sn_neg, name):
    B, S, W = dkc.shape
    G = W // LANES
    ts = ROW_TILE

    def body(d_ref, cs_ref, sn_ref, o_ref):
        acc = d_ref[0, :, 0:LANES]
        for gi in range(1, G):
            acc = acc + d_ref[0, :, gi * LANES:(gi + 1) * LANES]
        lane = lax.broadcasted_iota(jnp.int32, acc.shape, 1)
        rot = (lane >= NOPE_DIM) & (lane < NOPE_DIM + ROPE_DIM)
        acc = jnp.where(rot, acc, 0.0)
        o_ref[0] = _rope_group(acc, cs_ref[...], sn_ref[...]).astype(o_ref.dtype)

    tab = pl.BlockSpec((ts, LANES), lambda b, s: (s, 0))
    return pl.pallas_call(
        body, name=name, out_shape=jax.ShapeDtypeStruct((B, S, LANES), BF16), grid=(B, S // ts),
        in_specs=[pl.BlockSpec((1, ts, W), lambda b, s: (b, s, 0)), tab, tab],
        out_specs=pl.BlockSpec((1, ts, LANES), lambda b, s: (b, s, 0)),
        compiler_params=_cparams(("parallel", "parallel")),
    )(dkc, cs, sn_neg)


def _t5_bucket(dist):
    max_exact = N_BUCKETS // 2
    d = np.maximum(dist, 1).astype(np.float64)
    large = max_exact + (np.log(d / max_exact) / np.log(MAX_DISTANCE / max_exact)
                         * (N_BUCKETS - max_exact)).astype(np.int64)
    large = np.minimum(large, N_BUCKETS - 1)
    return np.where(dist < max_exact, dist, large).astype(np.int32)


def _band_buckets(dilation):
    a = np.arange(BLK)[:, None]
    bk = np.arange(2 * BLK)[None, :]
    steps = BLK + a - bk
    return _t5_bucket(np.clip(steps, 0, SPAN) * dilation)


def _head_mask(shape, hh):
    lane = lax.broadcasted_iota(jnp.int32, shape, 1)
    return (lane >= hh * HEAD_DIM) & (lane < (hh + 1) * HEAD_DIM)


def _dot_nt(a, b):
    return lax.dot_general(a, b, (((1,), (1,)), ((), ())), preferred_element_type=F32)


def _dot_tn(a, b):
    return lax.dot_general(a, b, (((0,), (0,)), ((), ())), preferred_element_type=F32)


def _dot_nn(a, b):
    return lax.dot_general(a, b, (((1,), (0,)), ((), ())), preferred_element_type=F32)


def _band_valid():
    a = lax.broadcasted_iota(jnp.int32, (BLK, BLK), 0)
    bk = lax.broadcasted_iota(jnp.int32, (BLK, BLK), 1)
    return bk >= a, bk <= a


def _dil_fwd(qkv, bias, branch, dilation, name):
    B, S, _ = qkv.shape
    d = dilation
    n = S // d
    nb = n // BLK
    qkv_v = qkv.reshape(B, n, d * P_QKV)
    npair = N_HEADS // 2

    def body(cur_ref, prev_ref, bias_ref, o_ref, lse_ref):
        i = pl.program_id(2)
        vprev, vcur = _band_valid()
        vprev = vprev & (i > 0)
        for p in range(npair):
            sl = slice(p * LANES, (p + 1) * LANES)
            q = cur_ref[0, :, sl]
            kc = cur_ref[0, :, D_A + p * LANES:D_A + (p + 1) * LANES]
            vc = cur_ref[0, :, 2 * D_A + p * LANES:2 * D_A + (p + 1) * LANES]
            kp = prev_ref[0, :, D_A + p * LANES:D_A + (p + 1) * LANES]
            vp = prev_ref[0, :, 2 * D_A + p * LANES:2 * D_A + (p + 1) * LANES]
            o_pair = jnp.zeros((BLK, LANES), F32)
            lse_pair = jnp.zeros((BLK, LANES), F32)
            for hh in range(2):
                h = 2 * p + hh
                hm = _head_mask((BLK, LANES), hh)
                qm = jnp.where(hm, q, jnp.zeros_like(q))
                s_p = _dot_nt(qm, kp) * DIL_SCALE + bias_ref[h, :, 0:BLK]
                s_c = _dot_nt(qm, kc) * DIL_SCALE + bias_ref[h, :, BLK:2 * BLK]
                s_p = jnp.where(vprev, s_p, NEG)
                s_c = jnp.where(vcur, s_c, NEG)
                m = jnp.maximum(jnp.max(s_p, axis=-1, keepdims=True), jnp.max(s_c, axis=-1, keepdims=True))
                e_p = jnp.exp(s_p - m)
                e_c = jnp.exp(s_c - m)
                l = jnp.sum(e_p, axis=-1, keepdims=True) + jnp.sum(e_c, axis=-1, keepdims=True)
                vpm = jnp.where(hm, vp, jnp.zeros_like(vp))
                vcm = jnp.where(hm, vc, jnp.zeros_like(vc))
                o_h = _dot_nn(e_p.astype(BF16), vpm) + _dot_nn(e_c.astype(BF16), vcm)
                o_pair = o_pair + o_h / l
                lse_pair = jnp.where(hm, m + jnp.log(l), lse_pair)
            o_ref[0, :, sl] = o_pair
            lse_ref[0, :, sl] = lse_pair

    cur = pl.BlockSpec((1, BLK, P_QKV), lambda b, r, i: (b, i, r))
    prev = pl.BlockSpec((1, BLK, P_QKV), lambda b, r, i: (b, jnp.maximum(i - 1, 0), r))
    out = pl.BlockSpec((1, BLK, D_A), lambda b, r, i: (b, i, r))
    o, lse = pl.pallas_call(
        body, name=name,
        out_shape=[jax.ShapeDtypeStruct((B, n, d * D_A), F32)] * 2,
        grid=(B, d, nb),
        in_specs=[cur, prev,
                  pl.BlockSpec((None, N_HEADS, BLK, 2 * BLK), lambda b, r, i: (branch, 0, 0, 0))],
        out_specs=[out, out],
        compiler_params=_cparams(("parallel", "parallel", "arbitrary")),
    )(qkv_v, qkv_v, bias)
    return o.reshape(B, S, D_A), lse.reshape(B, S, D_A)


def _dil_merge(os_, lses, name):
    B, S, W = os_[0].shape
    ts = 512

    def body(o0, o1, o2, l0, l1, l2, out_ref, L_ref):
        a0, a1, a2 = l0[0], l1[0], l2[0]
        m = jnp.maximum(jnp.maximum(a0, a1), a2)
        e0, e1, e2 = jnp.exp(a0 - m), jnp.exp(a1 - m), jnp.exp(a2 - m)
        ssum = e0 + e1 + e2
        out_ref[0] = (e0 * o0[0] + e1 * o1[0] + e2 * o2[0]) / ssum
        L_ref[0] = m + jnp.log(ssum)

    tok = pl.BlockSpec((1, ts, W), lambda b, s: (b, s, 0))
    return pl.pallas_call(
        body, name=name, out_shape=[jax.ShapeDtypeStruct((B, S, W), F32)] * 2, grid=(B, S // ts),
        in_specs=[tok] * 6, out_specs=[tok, tok], compiler_params=_cparams(("parallel", "parallel")),
    )(*os_, *lses)


def _dil_bwd(qkv, do, out_a, L, bias, branch, dilation, name):
    B, S, _ = qkv.shape
    d = dilation
    n = S // d
    nb = n // BLK
    qkv_v = qkv.reshape(B, n, d * P_QKV)
    do_v = do.reshape(B, n, d * D_A)
    oa_v = out_a.reshape(B, n, d * D_A)
    L_v = L.reshape(B, n, d * D_A)
    npair = N_HEADS // 2
    multi = nb > 1

    def body(*refs):
        if multi:
            (cur_ref, prev_ref, next_ref, do_ref, don_ref, oa_ref, oan_ref, L_ref, Ln_ref, bias_ref,
             dqkv_ref, dbias_ref) = refs
        else:
            cur_ref, do_ref, oa_ref, L_ref, bias_ref, dqkv_ref, dbias_ref = refs
        b, r, i = pl.program_id(0), pl.program_id(1), pl.program_id(2)

        @pl.when((b == 0) & (r == 0) & (i == 0))
        def _():
            dbias_ref[...] = jnp.zeros_like(dbias_ref)

        vprev, vcur = _band_valid()
        has_prev = i > 0
        has_next = i < nb - 1
        for p in range(npair):
            sl = slice(p * LANES, (p + 1) * LANES)
            ksl = slice(D_A + p * LANES, D_A + (p + 1) * LANES)
            vsl = slice(2 * D_A + p * LANES, 2 * D_A + (p + 1) * LANES)
            q, kc, vc = cur_ref[0, :, sl], cur_ref[0, :, ksl], cur_ref[0, :, vsl]
            dov = do_ref[0, :, sl]
            dd = dov.astype(F32) * oa_ref[0, :, sl]
            Lv = L_ref[0, :, sl]
            if multi:
                kp, vp = prev_ref[0, :, ksl], prev_ref[0, :, vsl]
                qn = next_ref[0, :, sl]
                donv = don_ref[0, :, sl]
                ddn = donv.astype(F32) * oan_ref[0, :, sl]
                Lnv = Ln_ref[0, :, sl]
            dq_pair = jnp.zeros((BLK, LANES), F32)
            dk_pair = jnp.zeros((BLK, LANES), F32)
            dv_pair = jnp.zeros((BLK, LANES), F32)
            for hh in range(2):
                h = 2 * p + hh
                hm = _head_mask((BLK, LANES), hh)
                zero = jnp.zeros_like(q)
                qm = jnp.where(hm, q, zero)
                dom = jnp.where(hm, dov, zero)
                delta = jnp.sum(jnp.where(hm, dd, 0.0), axis=-1, keepdims=True)
                lse = Lv[:, hh * HEAD_DIM:hh * HEAD_DIM + 1]
                s_c = _dot_nt(qm, kc) * DIL_SCALE + bias_ref[h, :, BLK:2 * BLK]
                p_c = jnp.where(vcur, jnp.exp(s_c - lse), 0.0)
                ds_c = p_c * (_dot_nt(dom, vc) - delta)
                ds_cb = ds_c.astype(BF16)
                dq_h = _dot_nn(ds_cb, kc)
                dk_h = _dot_tn(ds_cb, qm)
                dv_h = _dot_tn(p_c.astype(BF16), dom)
                dbias_ref[h, :, BLK:2 * BLK] += ds_c
                if multi:
                    s_p = _dot_nt(qm, kp) * DIL_SCALE + bias_ref[h, :, 0:BLK]
                    p_p = jnp.where(vprev & has_prev, jnp.exp(s_p - lse), 0.0)
                    ds_p = p_p * (_dot_nt(dom, vp) - delta)
                    dq_h = dq_h + _dot_nn(ds_p.astype(BF16), kp)
                    dbias_ref[h, :, 0:BLK] += ds_p
                    qnm = jnp.where(hm, qn, zero)
                    donm = jnp.where(hm, donv, zero)
                    delta_n = jnp.sum(jnp.where(hm, ddn, 0.0), axis=-1, keepdims=True)
                    lse_n = Lnv[:, hh * HEAD_DIM:hh * HEAD_DIM + 1]
                    s_n = _dot_nt(qnm, kc) * DIL_SCALE + bias_ref[h, :, 0:BLK]
                    p_n = jnp.where(vprev & has_next, jnp.exp(s_n - lse_n), 0.0)
                    ds_n = p_n * (_dot_nt(donm, vc) - delta_n)
                    dk_h = dk_h + _dot_tn(ds_n.astype(BF16), qnm)
                    dv_h = dv_h + _dot_tn(p_n.astype(BF16), donm)
                dq_pair = dq_pair + jnp.where(hm, dq_h, 0.0) * DIL_SCALE
                dk_pair = dk_pair + jnp.where(hm, dk_h, 0.0) * DIL_SCALE
                dv_pair = dv_pair + jnp.where(hm, dv_h, 0.0)
            dqkv_ref[0, :, sl] = dq_pair
            dqkv_ref[0, :, ksl] = dk_pair
            dqkv_ref[0, :, vsl] = dv_pair

    def at(off):
        return lambda b, r, i: (b, jnp.clip(i + off, 0, nb - 1), r)

    qkv_spec = lambda off: pl.BlockSpec((1, BLK, P_QKV), at(off))
    da_spec = lambda off: pl.BlockSpec((1, BLK, D_A), at(off))
    bias_spec = pl.BlockSpec((None, N_HEADS, BLK, 2 * BLK), lambda b, r, i: (branch, 0, 0, 0))
    dbias_spec = pl.BlockSpec((N_HEADS, BLK, 2 * BLK), lambda b, r, i: (0, 0, 0))
    if multi:
        in_specs = [qkv_spec(0), qkv_spec(-1), qkv_spec(1), da_spec(0), da_spec(1), da_spec(0), da_spec(1),
                    da_spec(0), da_spec(1), bias_spec]
        args = [qkv_v, qkv_v, qkv_v, do_v, do_v, oa_v, oa_v, L_v, L_v, bias]
    else:
        in_specs = [qkv_spec(0), da_spec(0), da_spec(0), da_spec(0), bias_spec]
        args = [qkv_v, do_v, oa_v, L_v, bias]
    dqkv, dbias = pl.pallas_call(
        body, name=name,
        out_shape=[jax.ShapeDtypeStruct((B, n, d * P_QKV), F32),
                   jax.ShapeDtypeStruct((N_HEADS, BLK, 2 * BLK), F32)],
        grid=(B, d, nb),
        in_specs=in_specs,
        out_specs=[qkv_spec(0), dbias_spec],
        compiler_params=_cparams(("arbitrary", "arbitrary", "arbitrary")),
    )(*args)
    return dqkv.reshape(B, S, P_QKV), dbias


def _sum3_bf16(a, b, c, name):
    B, S, W = a.shape
    ts = 512

    def body(a_ref, b_ref, c_ref, o_ref):
        o_ref[...] = (a_ref[...] + b_ref[...] + c_ref[...]).astype(o_ref.dtype)

    tok = pl.BlockSpec((1, ts, W), lambda b, s: (b, s, 0))
    return pl.pallas_call(
        body, name=name, out_shape=jax.ShapeDtypeStruct((B, S, W), BF16), grid=(B, S // ts),
        in_specs=[tok] * 3, out_specs=tok, compiler_params=_cparams(("parallel", "parallel")),
    )(a, b, c)


def _bias_tables(rel_bias, buckets, name):
    nbr = buckets.shape[0]

    def body(rb_ref, bk_ref, o_ref):
        h = pl.program_id(1)
        tab = bk_ref[0]

        def step(bkt, acc):
            return jnp.where(tab == bkt, rb_ref[bkt, h], acc)

        o_ref[0, 0] = lax.fori_loop(0, N_BUCKETS, step, jnp.zeros((BLK, 2 * BLK), F32))

    return pl.pallas_call(
        body, name=name, out_shape=jax.ShapeDtypeStruct((nbr, N_HEADS, BLK, 2 * BLK), F32),
        grid=(nbr, N_HEADS),
        in_specs=[pl.BlockSpec(memory_space=pltpu.SMEM),
                  pl.BlockSpec((1, BLK, 2 * BLK), lambda i, h: (i, 0, 0))],
        out_specs=pl.BlockSpec((1, 1, BLK, 2 * BLK), lambda i, h: (i, h, 0, 0)),
        compiler_params=_cparams(("parallel", "arbitrary")),
    )(rel_bias, buckets)


def _bias_grad(dbias_list, buckets, name):
    nbr = len(dbias_list)

    def body(*refs):
        d_refs, bk_ref, o_ref = refs[:nbr], refs[nbr], refs[nbr + 1]
        lane = lax.broadcasted_iota(jnp.int32, (1, LANES), 1)
        for h in range(N_HEADS):
            def step(bkt, acc):
                tot = jnp.zeros((1, 1), F32)
                for bi in range(nbr):
                    sel = jnp.where(bk_ref[bi] == bkt, d_refs[bi][h], 0.0)
                    tot = tot + jnp.sum(jnp.sum(sel, axis=1, keepdims=True), axis=0, keepdims=True)
                return acc + jnp.where(lane == bkt, tot, 0.0)

            o_ref[h:h + 1, :] = lax.fori_loop(0, N_BUCKETS, step, jnp.zeros((1, LANES), F32))

    band = pl.BlockSpec((N_HEADS, BLK, 2 * BLK), lambda i: (0, 0, 0))
    return pl.pallas_call(
        body, name=name, out_shape=jax.ShapeDtypeStruct((N_HEADS, LANES), F32), grid=(1,),
        in_specs=[band] * nbr + [pl.BlockSpec((nbr, BLK, 2 * BLK), lambda i: (0, 0, 0))],
        out_specs=pl.BlockSpec((N_HEADS, LANES), lambda i: (0, 0)),
        compiler_params=_cparams(("arbitrary",)),
    )(*dbias_list, buckets)


MLA_TQ = 256
MLA_TK = 256


LOG2E = math.log2(math.e)
MLA_C = MLA_SCALE * LOG2E


def _key_le_query(tk, tq):
    return lax.broadcasted_iota(jnp.int32, (tk, tq), 0) <= lax.broadcasted_iota(jnp.int32, (tk, tq), 1)


def _row_mask(shape, hh):
    row = lax.broadcasted_iota(jnp.int32, shape, 0)
    return (row >= hh * HEAD_DIM) & (row < (hh + 1) * HEAD_DIM)


def _host_call(body, comm, *, name, grid, in_specs, out_specs, out_shape, scratch_shapes, args):
    sem = ("arbitrary",) * len(grid)
    if comm is None:
        res = pl.pallas_call(body, name=name, grid=grid, in_specs=in_specs, out_specs=out_specs,
                             out_shape=out_shape, scratch_shapes=scratch_shapes,
                             compiler_params=_cparams(sem))(*args)
        return res, []
    n_in, n_out, n_s, cn = len(in_specs), len(out_specs), len(scratch_shapes), comm.n

    def hosted(*refs):
        ins, refs = refs[:n_in], refs[n_in:]
        c_ins, refs = refs[:cn], refs[cn:]
        outs, refs = refs[:n_out], refs[n_out:]
        c_outs, refs = refs[:cn], refs[cn:]
        scr, c_sems = refs[:n_s], refs[n_s:]
        ids = [pl.program_id(a) for a in range(len(grid))]
        first = functools.reduce(jnp.logical_and, [i == 0 for i in ids])
        last = functools.reduce(jnp.logical_and, [i == g - 1 for i, g in zip(ids, grid)])

        @pl.when(first)
        def _():
            comm.start(c_ins, c_outs, c_sems)

        body(*ins, *outs, *scr)

        @pl.when(last)
        def _():
            comm.finish(c_ins, c_outs, c_sems)

    res = pl.pallas_call(
        hosted, name=name, grid=grid, in_specs=list(in_specs) + _hbm_specs(cn),
        out_specs=list(out_specs) + _hbm_specs(cn), out_shape=list(out_shape) + list(comm.out_shape),
        scratch_shapes=list(scratch_shapes) + list(comm.scratch), compiler_params=_cparams(sem),
    )(*args, *comm.inputs)
    return res[:n_out], res[n_out:]


def _mla_fwd_t(q, k, vt, name, comm=None):
    B, S, _ = q.shape
    tq, tk = MLA_TQ, MLA_TK
    assert tq == tk
    npair = N_HEADS // 2
    nq = S // tq

    def body(q_ref, k_ref, vt_ref, o_ref, lse_ref, acc_s):
        i = pl.program_id(2)
        acc_s[...] = jnp.zeros_like(acc_s)
        qs = [q_ref[0, :, hh * LANES:(hh + 1) * LANES] for hh in range(2)]
        diag = _key_le_query(tk, tq)

        def step(j, ms, masked):
            kj = k_ref[0, pl.ds(pl.multiple_of(j * tk, tk), tk), :]
            vj = vt_ref[0, 0, j]
            out = []
            for hh in range(2):
                s = _dot_nt(kj[:, hh * LANES:(hh + 1) * LANES], qs[hh])
                if masked:
                    s = jnp.where(diag, s, NEG)
                m_new = jnp.maximum(ms[hh], jnp.max(s, axis=0, keepdims=True))
                alpha = jnp.exp2((ms[hh] - m_new) * MLA_C)
                e = jnp.exp2((s - m_new) * MLA_C).astype(BF16)
                vh = jnp.where(_row_mask(vj.shape, hh), vj, jnp.ones_like(vj))
                acc_s[hh] = acc_s[hh] * alpha + _dot_nn(vh, e)
                out.append(m_new)
            return tuple(out)

        m0 = jnp.full((1, tq), NEG, F32)
        ms = lax.fori_loop(0, i, lambda j, c: step(j, c, False), (m0, m0))
        ms = step(i, ms, True)
        rows0 = _row_mask((LANES, tq), 0)
        l0 = acc_s[0, HEAD_DIM:HEAD_DIM + 1, :]
        l1 = acc_s[1, 0:1, :]
        o_ref[0] = jnp.where(rows0, acc_s[0] / l0, acc_s[1] / l1)
        lse_ref[0, 0, 0] = jnp.zeros((8, tq), F32)
        lse_ref[0, 0, 0, 0:1, :] = ms[0] * MLA_C + jnp.log(l0) * LOG2E
        lse_ref[0, 0, 0, 1:2, :] = ms[1] * MLA_C + jnp.log(l1) * LOG2E

    return _host_call(
        body, comm, name=name,
        out_shape=[jax.ShapeDtypeStruct((B, D_B, S), F32), jax.ShapeDtypeStruct((B, npair, nq, 8, tq), F32)],
        grid=(B, npair, nq),
        in_specs=[pl.BlockSpec((1, tq, 2 * LANES), lambda b, p, i: (b, i, p)),
                  pl.BlockSpec((1, S, 2 * LANES), lambda b, p, i: (b, 0, p)),
                  pl.BlockSpec((1, 1, S // tk, LANES, tk), lambda b, p, i: (b, p, 0, 0, 0))],
        out_specs=[pl.BlockSpec((1, LANES, tq), lambda b, p, i: (b, p, i)),
                   pl.BlockSpec((1, 1, 1, 8, tq), lambda b, p, i: (b, p, i, 0, 0))],
        scratch_shapes=[pltpu.VMEM((2, LANES, tq), F32)],
        args=(q, k, vt))


def _mla_delta(do, o, name):
    B, S, _ = o.shape
    tq = MLA_TQ
    npair = N_HEADS // 2

    def body(do_ref, o_ref, d_ref):
        prod_t = jnp.transpose(do_ref[0].astype(F32) * o_ref[0])
        d_ref[0, 0, 0] = jnp.zeros((8, tq), F32)
        d_ref[0, 0, 0, 0:1, :] = jnp.sum(prod_t[:HEAD_DIM], axis=0, keepdims=True)
        d_ref[0, 0, 0, 1:2, :] = jnp.sum(prod_t[HEAD_DIM:], axis=0, keepdims=True)

    tok = pl.BlockSpec((1, tq, LANES), lambda b, p, i: (b, i, p))
    return pl.pallas_call(
        body, name=name, out_shape=jax.ShapeDtypeStruct((B, npair, S // tq, 8, tq), F32),
        grid=(B, npair, S // tq), in_specs=[tok, tok],
        out_specs=pl.BlockSpec((1, 1, 1, 8, tq), lambda b, p, i: (b, p, i, 0, 0)),
        compiler_params=_cparams(("parallel", "parallel", "parallel")),
    )(do, o)


def _mla_bwd_t(q, k, v, do, lse, delta, name, comm=None):
    B, S, _ = q.shape
    tq, tk = MLA_TQ, MLA_TK
    assert tq == tk
    npair = N_HEADS // 2
    nq = S // tq

    def body(q_ref, do_ref, lse_ref, dl_ref, k_ref, v_ref, dk_ref, dv_ref, dq_ref, dk_s, dv_s):
        j = pl.program_id(2)

        @pl.when(j == 0)
        def _():
            dq_ref[...] = jnp.zeros_like(dq_ref)

        vj = v_ref[0]
        dv_s[...] = jnp.zeros_like(dv_s)
        diag = _key_le_query(tk, tq)
        for hh in range(2):
            hsl = slice(hh * LANES, (hh + 1) * LANES)
            hm = _head_mask((tq, LANES), hh)
            kh = k_ref[0, :, hsl]
            kt = jnp.transpose(kh.astype(F32)).astype(BF16)
            dk_s[...] = jnp.zeros_like(dk_s)

            def step(i, masked):
                rows = pl.ds(pl.multiple_of(i * tq, tq), tq)
                qi = q_ref[0, rows, hsl]
                dov = do_ref[0, rows, :]
                dom = jnp.where(hm, dov, jnp.zeros_like(dov))
                s = _dot_nt(kh, qi)
                pr = jnp.exp2(s * MLA_C - lse_ref[0, 0, i, hh:hh + 1, :])
                if masked:
                    pr = jnp.where(diag, pr, 0.0)
                dv_s[...] += _dot_nn(pr.astype(BF16), dom)
                ds = (pr * (_dot_nt(vj, dom) - dl_ref[0, 0, i, hh:hh + 1, :])).astype(BF16)
                dk_s[...] += _dot_nn(ds, qi)
                dq_ref[0, 0, i, hsl, :] += _dot_nn(kt, ds) * MLA_SCALE

            step(j, True)

            def loop_body(i, carry):
                step(i, False)
                return carry

            lax.fori_loop(j + 1, nq, loop_body, 0)
            dk_ref[0, :, hsl] = dk_s[...] * MLA_SCALE
        dv_ref[0] = dv_s[...]

    stat = pl.BlockSpec((1, 1, nq, 8, tq), lambda b, p, j: (b, p, 0, 0, 0))
    return _host_call(
        body, comm, name=name,
        out_shape=[jax.ShapeDtypeStruct((B, S, N_HEADS * LANES), F32), jax.ShapeDtypeStruct((B, S, D_B), F32),
                   jax.ShapeDtypeStruct((B, npair, nq, 2 * LANES, tq), F32)],
        grid=(B, npair, S // tk),
        in_specs=[pl.BlockSpec((1, S, 2 * LANES), lambda b, p, j: (b, 0, p)),
                  pl.BlockSpec((1, S, LANES), lambda b, p, j: (b, 0, p)),
                  stat, stat,
                  pl.BlockSpec((1, tk, 2 * LANES), lambda b, p, j: (b, j, p)),
                  pl.BlockSpec((1, tk, LANES), lambda b, p, j: (b, j, p))],
        out_specs=[pl.BlockSpec((1, tk, 2 * LANES), lambda b, p, j: (b, j, p)),
                   pl.BlockSpec((1, tk, LANES), lambda b, p, j: (b, j, p)),
                   pl.BlockSpec((1, 1, nq, 2 * LANES, tq), lambda b, p, j: (b, p, 0, 0, 0))],
        scratch_shapes=[pltpu.VMEM((tk, LANES), F32), pltpu.VMEM((tk, LANES), F32)],
        args=(q, do, lse, delta, k, v))


def _local_step(x, target, mod, wts, gains, rel_bias, ffn_shards=None):
    B, S, D = x.shape
    T = B * S
    sh1, sc1, g1, sh2, sc2, g2 = [mod[:, i * D:(i + 1) * D].reshape(B, 1, D) for i in range(N_MOD)]
    cs, sn = _rope_tables()
    buckets = np.stack([_band_buckets(d) for d in DILATIONS])
    buckets_dev = jnp.asarray(buckets)
    bias = _bias_tables(rel_bias, buckets_dev, "rel_bias_tables")
    w_in = wts["w_in"]

    h1 = _adaln_fwd(x, gains["g_norm1"], sc1, sh1, "adaln1_fwd")
    h1f = h1.reshape(T, D)
    qkv = _mm(h1f, w_in[:, :P_QKV], "nn", BF16, "mm_qkv").reshape(B, S, P_QKV)
    rest = _mm(h1f, w_in[:, P_QKV:], "nn", F32, "mm_rest")
    o_d, lse_d = [], []
    for i, d in enumerate(DILATIONS):
        o_i, lse_i = _dil_fwd(qkv, bias, i, d, f"dil_fwd_{d}")
        o_d.append(o_i)
        lse_d.append(lse_i)
    out_a, lse_a = _dil_merge(o_d, lse_d, "dil_merge")
    cqn = _rms_fwd(rest, 1, Q_LORA, gains["g_cq"], "rms_cq_fwd")
    ckvn = _rms_fwd(rest, 0, KV_LORA, gains["g_ckv"], "rms_ckv_fwd")
    rest3 = rest.reshape(B, S, P_REST)
    q_raw = _mm(cqn, wts["w_uq"], "nn", F32, "mm_uq").reshape(B, S, N_HEADS * LANES)
    qc = _rope_apply(q_raw, cs, sn, BF16, "rope_q")
    kn_raw = _mm(ckvn, wts["w_kv"][:, :N_HEADS * LANES], "nn", F32, "mm_uk").reshape(B, S, N_HEADS * LANES)
    kc = _rope_apply(kn_raw, cs, sn, BF16, "rope_k", add=rest3, add_blk=KV_LORA // LANES)
    v = _mm(ckvn, wts["w_kv"][:, N_HEADS * LANES:], "nn", BF16, "mm_uv").reshape(B, S, D_B)
    vt = jnp.transpose(v.reshape(B, S // MLA_TK, MLA_TK, N_HEADS // 2, LANES), (0, 3, 1, 4, 2))
    (o_t, lse_b), got = _mla_fwd_t(qc, kc, vt, "mla_fwd", _GatherComm(ffn_shards) if ffn_shards else None)
    if ffn_shards:
        wts = dict(wts, w_ffn_in=got[0].reshape(N_CHIP, D, -1), w_ffn_out=got[1].reshape(D_FF, D))
    out_b = jnp.transpose(o_t, (0, 2, 1))
    out_af, out_bf = out_a.reshape(T, D_A), out_b.reshape(T, D_B)
    ya = _rms_fwd(out_af, 0, D_A, gains["g_out_a"], "rms_outa_fwd")
    yb = _rms_fwd(out_bf, 0, D_B, gains["g_out_b"], "rms_outb_fwd")
    y = jnp.concatenate([ya, yb], axis=1)
    mix = _mm(y, wts["w_out"], "nn", F32, "mm_out").reshape(B, S, D)
    h2, x1 = _adaln_fwd(x, gains["g_norm2"], sc2, sh2, "adaln2_fwd", mix=mix, gate=g1)
    h2f = h2.reshape(T, D)
    gu = _mm(h2f, wts["w_ffn_in"], "nn", F32, "mm_ffn_in", col_blocks=N_CHIP)
    act = _swiglu_fwd(gu, "swiglu_fwd")
    f = _mm(act, wts["w_ffn_out"], "nn", F32, "mm_ffn_out").reshape(B, S, D)
    dx2, df, dg2, dg_final, loss = _final_loss(x1, f, g2, gains["g_final"], target, "final_loss")

    dff = df.reshape(T, D)
    da = _mm(dff, wts["w_ffn_out"], "nt", F32, "mm_ffn_out_dx")
    gw_ffn_out = _mm(act, dff, "tn", F32, "mm_ffn_out_dw")
    dgu = _swiglu_bwd(da, gu, "swiglu_bwd")
    dh2 = _mm(dgu, wts["w_ffn_in"], "nt", F32, "mm_ffn_in_dx", col_blocks=N_CHIP).reshape(B, S, D)
    gw_ffn_in = _mm(h2f, dgu, "tn", F32, "mm_ffn_in_dw", col_blocks=N_CHIP)
    dx1, dsh2, dsc2, dg_norm2, dg1, dmix = _adaln_bwd(dh2, x1, gains["g_norm2"], sc2, dx2, "adaln2_bwd",
                                                      mix=mix, gate=g1)
    dmixf = dmix.reshape(T, D)
    dy = _mm(dmixf, wts["w_out"], "nt", F32, "mm_out_dx")
    gw_out = _mm(y, dmixf, "tn", F32, "mm_out_dw")
    do_a, dg_out_a = _rms_bwd(dy, 0, out_af, 0, D_A, gains["g_out_a"], "rms_outa_bwd")
    do_b, dg_out_b = _rms_bwd(dy, 1, out_bf, 0, D_B, gains["g_out_b"], "rms_outb_bwd")
    do_b3 = do_b.reshape(B, S, D_B)
    delta_b = _mla_delta(do_b3, out_b, "mla_delta")
    ffn_a4 = None
    if ffn_shards:
        ffn_a4 = _rs_first([gw_ffn_in.reshape(N_DEV, -1, gw_ffn_in.shape[-1]), gw_ffn_out.reshape(N_DEV, -1, D)],
                           "ffn")
    (dkc, dv, dq_t), ffn_r2 = _mla_bwd_t(qc, kc, v, do_b3, lse_b, delta_b, "mla_bwd",
                                         _ToChipsComm(ffn_a4) if ffn_shards else None)
    dqc = jnp.transpose(dq_t, (0, 2, 4, 1, 3)).reshape(B, S, N_HEADS * LANES)
    dq_raw = _rope_apply(dqc, cs, -sn, BF16, "rope_q_bwd").reshape(T, N_HEADS * LANES)
    dkrw = _krope_bwd(dkc, cs, -sn, "rope_k_bwd").reshape(T, LANES)
    dcqn = _mm(dq_raw, wts["w_uq"], "nt", F32, "mm_uq_dx")
    gw_uq = _mm(cqn, dq_raw, "tn", F32, "mm_uq_dw")
    dkv = jnp.concatenate([dkc.reshape(T, -1), dv.reshape(T, -1)], axis=1).astype(BF16)
    dckvn = _mm(dkv, wts["w_kv"], "nt", F32, "mm_ukv_dx")
    gw_kv = _mm(ckvn, dkv, "tn", F32, "mm_ukv_dw")
    dcq, dg_cq = _rms_bwd(dcqn, 0, rest, 1, Q_LORA, gains["g_cq"], "rms_cq_bwd")
    dckv, dg_ckv = _rms_bwd(dckvn, 0, rest, 0, KV_LORA, gains["g_ckv"], "rms_ckv_bwd")
    do_a3 = do_a.reshape(B, S, D_A)
    dqkv_d, dbias_d = [], []
    for i, d in enumerate(DILATIONS):
        dqkv_i, dbias_i = _dil_bwd(qkv, do_a3, out_a, lse_a, bias, i, d, f"dil_bwd_{d}")
        dqkv_d.append(dqkv_i)
        dbias_d.append(dbias_i)
    dqkv = _sum3_bf16(*dqkv_d, "dil_bwd_sum").reshape(T, P_QKV)
    g_rel_bias = _bias_grad(dbias_d, buckets_dev, "rel_bias_grad")[:, :N_BUCKETS].T
    dproj = jnp.concatenate([dqkv, dckv, dkrw, dcq], axis=1)
    dh1 = _mm(dproj, w_in, "nt", F32, "mm_in_dx").reshape(B, S, D)
    gw_in = _mm(h1f, dproj, "tn", F32, "mm_in_dw")
    grad_x, dsh1, dsc1, dg_norm1 = _adaln_bwd(dh1, x, gains["g_norm1"], sc1, dx1, "adaln1_bwd")
    gmod = jnp.concatenate([dsh1, dsc1, dg1, dsh2, dsc2, dg2], axis=-1).reshape(B, N_MOD * D)
    grads = dict(w_in=gw_in, w_uq=gw_uq, w_kv=gw_kv, w_out=gw_out, w_ffn_in=gw_ffn_in, w_ffn_out=gw_ffn_out,
                 g_norm1=dg_norm1, g_cq=dg_cq, g_ckv=dg_ckv, rel_bias=g_rel_bias, g_out_a=dg_out_a,
                 g_out_b=dg_out_b, g_norm2=dg_norm2, g_final=dg_final, ffn_pending=(ffn_a4, ffn_r2))
    return loss, grad_x, gmod, grads


def _w_in_to_kernel(w):
    z = lambda n: jnp.zeros((w.shape[0], n), w.dtype)
    i3, i4, i5 = 3 * D_A, 3 * D_A + Q_LORA, 3 * D_A + Q_LORA + KV_LORA
    return jnp.concatenate([w[:, :i3], w[:, i4:i5], z(NOPE_DIM), w[:, i5:], z(LANES - NOPE_DIM - ROPE_DIM),
                            w[:, i3:i4]], axis=1)


def _w_in_from_kernel(g):
    o = P_QKV + KV_LORA
    return jnp.concatenate([g[:, :P_QKV], g[:, o + LANES:], g[:, P_QKV:o],
                            g[:, o + NOPE_DIM:o + NOPE_DIM + ROPE_DIM]], axis=1)


def _w_uq_to_kernel(w):
    w3 = w.reshape(Q_LORA, N_HEADS, NOPE_DIM + ROPE_DIM)
    return jnp.pad(w3, ((0, 0), (0, 0), (0, LANES - NOPE_DIM - ROPE_DIM))).reshape(Q_LORA, N_HEADS * LANES)


def _w_uq_from_kernel(g):
    return g.reshape(Q_LORA, N_HEADS, LANES)[:, :, :NOPE_DIM + ROPE_DIM].reshape(Q_LORA, -1)


def _w_ukv_to_kernel(w):
    w3 = w.reshape(KV_LORA, N_HEADS, 2 * HEAD_DIM)
    wk = jnp.pad(w3[:, :, :NOPE_DIM], ((0, 0), (0, 0), (0, LANES - NOPE_DIM))).reshape(KV_LORA, N_HEADS * LANES)
    wv = w3[:, :, NOPE_DIM:].reshape(KV_LORA, D_B)
    return jnp.concatenate([wk, wv], axis=1)


def _w_ukv_from_kernel(g):
    gk = g[:, :N_HEADS * LANES].reshape(KV_LORA, N_HEADS, LANES)[:, :, :NOPE_DIM]
    gv = g[:, N_HEADS * LANES:].reshape(KV_LORA, N_HEADS, HEAD_DIM)
    return jnp.concatenate([gk, gv], axis=2).reshape(KV_LORA, -1)


MESH = pl.DeviceIdType.MESH


def _my_place():
    return lax.axis_index("x"), lax.axis_index("y"), lax.axis_index("c")


def _other_chips(x, y):
    return [(1 - x, y), (x, 1 - y), (1 - x, 1 - y)]


def _allgather8(x_shard, name, in_hbm):
    m_per, n = x_shard.shape
    space = pl.ANY if in_hbm else pltpu.VMEM

    def body(x_ref, out_ref, send_sems, recv_sems, local_sem):
        x, y, c = _my_place()
        me, sibling = (x, y, c), (x, y, 1 - c)
        chips = _other_chips(x, y)

        def rows(px, py, pc):
            return out_ref.at[pl.ds((4 * px + 2 * py + pc) * m_per, m_per), :]

        def copy(k, block, to, src=None):
            return pltpu.make_async_remote_copy(
                src_ref=rows(*block) if src is None else src, dst_ref=rows(*block),
                send_sem=send_sems.at[k], recv_sem=recv_sems.at[k], device_id=to, device_id_type=MESH)

        mine = pltpu.make_async_copy(x_ref, rows(*me), local_sem)
        mine.start()
        first = [copy(0, me, sibling, src=x_ref)]
        first += [copy(1 + j, me, (*chip, c), src=x_ref) for j, chip in enumerate(chips)]
        for cp in first:
            cp.start()
        passed = [copy(4 + j, (*chip, c), sibling) for j, chip in enumerate(chips)]
        for j, chip in enumerate(chips):
            copy(1 + j, (*chip, c), me).wait_recv()
            passed[j].start()
        copy(0, sibling, me).wait_recv()
        for j, chip in enumerate(chips):
            copy(4 + j, (*chip, 1 - c), me).wait_recv()
        for cp in first + passed:
            cp.wait_send()
        mine.wait()

    return pl.pallas_call(
        body, name=name,
        out_shape=jax.ShapeDtypeStruct((N_DEV * m_per, n), x_shard.dtype),
        in_specs=[pl.BlockSpec(memory_space=space)],
        out_specs=pl.BlockSpec(memory_space=space),
        scratch_shapes=[pltpu.SemaphoreType.DMA((7,)), pltpu.SemaphoreType.DMA((7,)), pltpu.SemaphoreType.DMA],
        compiler_params=pltpu.CompilerParams(vmem_limit_bytes=VMEM_LIMIT),
    )(x_shard)


def _hbm_specs(n):
    return [pl.BlockSpec(memory_space=pl.ANY)] * n


class _GatherComm:
    def __init__(self, shards):
        self.inputs = list(shards)
        self.n = n = len(shards)
        self.halves = [s.shape[0] // 2 for s in shards]
        self.out_shape = [jax.ShapeDtypeStruct((N_DEV, h, s.shape[1]), s.dtype)
                          for h, s in zip(self.halves, shards)]
        self.scratch = [pltpu.SemaphoreType.DMA((7 * n,)), pltpu.SemaphoreType.DMA((7 * n,)),
                        pltpu.SemaphoreType.DMA((n,))]

    def _parts(self, xs, outs, sems):
        send_sems, recv_sems, local_sems = sems
        x, y, c = _my_place()
        halves = self.halves

        def blk(k, px, py, pc):
            return outs[k].at[4 * px + 2 * py + pc]

        def mine(k):
            return xs[k].at[pl.ds(pl.multiple_of(c * halves[k], 16), halves[k]), :]

        def copy(k, kind, block, to, own=False):
            return pltpu.make_async_remote_copy(
                src_ref=mine(k) if own else blk(k, *block), dst_ref=blk(k, *block),
                send_sem=send_sems.at[7 * k + kind], recv_sem=recv_sems.at[7 * k + kind],
                device_id=to, device_id_type=MESH)

        me, sibling = (x, y, c), (x, y, 1 - c)
        chips = _other_chips(x, y)
        local = [pltpu.make_async_copy(mine(k), blk(k, *me), local_sems.at[k]) for k in range(self.n)]
        first = []
        for k in range(self.n):
            first.append(copy(k, 0, me, sibling, own=True))
            first += [copy(k, 1 + j, me, (*chip, c), own=True) for j, chip in enumerate(chips)]
        return copy, me, sibling, chips, c, local, first

    def start(self, xs, outs, sems):
        _, _, _, _, _, local, first = self._parts(xs, outs, sems)
        for cp in local + first:
            cp.start()

    def finish(self, xs, outs, sems):
        copy, me, sibling, chips, c, local, first = self._parts(xs, outs, sems)
        passed = []
        for j, chip in enumerate(chips):
            for k in range(self.n):
                copy(k, 1 + j, (*chip, c), me).wait_recv()
                fwd = copy(k, 4 + j, (*chip, c), sibling)
                fwd.start()
                passed.append(fwd)
        for k in range(self.n):
            copy(k, 0, sibling, me).wait_recv()
        for j, chip in enumerate(chips):
            for k in range(self.n):
                copy(k, 4 + j, (*chip, 1 - c), me).wait_recv()
        for cp in first + passed:
            cp.wait_send()
        for cp in local:
            cp.wait()


class _ToChipsComm:
    def __init__(self, a4s):
        self.inputs = list(a4s)
        self.n = n = len(a4s)
        nc = N_CHIP - 1
        self.out_shape = [jax.ShapeDtypeStruct((nc,) + a.shape[1:], a.dtype) for a in a4s]
        self.scratch = [pltpu.SemaphoreType.DMA((nc * n,)), pltpu.SemaphoreType.DMA((nc * n,))]

    def _copies(self, as_, rs, sems):
        send_sems, recv_sems = sems
        x, y, c = _my_place()
        nc = N_CHIP - 1
        return [pltpu.make_async_remote_copy(
            src_ref=as_[k].at[2 * cx + cy], dst_ref=rs[k].at[j], send_sem=send_sems.at[nc * k + j],
            recv_sem=recv_sems.at[nc * k + j], device_id=(cx, cy, c), device_id_type=MESH)
            for k in range(self.n) for j, (cx, cy) in enumerate(_other_chips(x, y))]

    def start(self, as_, rs, sems):
        for cp in self._copies(as_, rs, sems):
            cp.start()

    def finish(self, as_, rs, sems):
        for cp in self._copies(as_, rs, sems):
            cp.wait()


def _run_comm(comm, name):
    n = comm.n

    def body(*refs):
        ins, outs, sems = refs[:n], refs[n:2 * n], refs[2 * n:]
        comm.start(ins, outs, sems)
        comm.finish(ins, outs, sems)

    return pl.pallas_call(
        body, name=name, out_shape=comm.out_shape, in_specs=_hbm_specs(n), out_specs=_hbm_specs(n),
        scratch_shapes=comm.scratch,
    )(*comm.inputs)


def _gather_weights(shards, name):
    return _run_comm(_GatherComm(shards), name)


def _rs_to_sibling(g8s, name):
    n = len(g8s)

    def body(*refs):
        gs, rs = refs[:n], refs[n:2 * n]
        send_sems, recv_sems = refs[2 * n:]
        x, y, c = _my_place()
        copies = [pltpu.make_async_remote_copy(
            src_ref=gs[k].at[2 * s + 1 - c], dst_ref=rs[k].at[s], send_sem=send_sems.at[N_CHIP * k + s],
            recv_sem=recv_sems.at[N_CHIP * k + s], device_id=(x, y, 1 - c), device_id_type=MESH)
            for k in range(n) for s in range(N_CHIP)]
        for cp in copies:
            cp.start()
        for cp in copies:
            cp.wait()

    return pl.pallas_call(
        body, name=name,
        out_shape=[jax.ShapeDtypeStruct((N_CHIP,) + g.shape[1:], g.dtype) for g in g8s],
        in_specs=_hbm_specs(n), out_specs=_hbm_specs(n),
        scratch_shapes=[pltpu.SemaphoreType.DMA((N_CHIP * n,)), pltpu.SemaphoreType.DMA((N_CHIP * n,))],
    )(*g8s)


def _rs_to_chips(a4s, name):
    return _run_comm(_ToChipsComm(a4s), name)


def _swap_halves(hs, name):
    n = len(hs)

    def body(*refs):
        h_refs, o_refs = refs[:n], refs[n:2 * n]
        send_sems, recv_sems, local_sems = refs[2 * n:]
        x, y, c = _my_place()

        def remote(k, slot):
            return pltpu.make_async_remote_copy(
                src_ref=h_refs[k], dst_ref=o_refs[k].at[slot], send_sem=send_sems.at[k],
                recv_sem=recv_sems.at[k], device_id=(x, y, 1 - c), device_id_type=MESH)

        local = [pltpu.make_async_copy(h_refs[k], o_refs[k].at[c], local_sems.at[k]) for k in range(n)]
        sends = [remote(k, c) for k in range(n)]
        for cp in local + sends:
            cp.start()
        for k in range(n):
            remote(k, 1 - c).wait_recv()
        for cp in sends:
            cp.wait_send()
        for cp in local:
            cp.wait()

    return pl.pallas_call(
        body, name=name,
        out_shape=[jax.ShapeDtypeStruct((2,) + h.shape, h.dtype) for h in hs],
        in_specs=_hbm_specs(n), out_specs=_hbm_specs(n),
        scratch_shapes=[pltpu.SemaphoreType.DMA((n,)), pltpu.SemaphoreType.DMA((n,)), pltpu.SemaphoreType.DMA((n,))],
    )(*hs)


ADD_TILES = 4


def _add_blocks(a_list, a_idx_fn, others_list, ns, sel, name):
    n = len(a_list)
    n_o = len(others_list[0])
    per = 1 + n_o

    def body(sel_ref, *refs):
        for k in range(n):
            ins = refs[k * per:(k + 1) * per]
            o_ref = refs[n * per + k]
            acc = ins[0][0]
            for r in ins[1:]:
                acc = acc + r[0]
            o_ref[0] = acc

    in_specs, args, out_specs, out_shape = [], [], [], []
    for a, others in zip(a_list, others_list):
        _, R, N = a.shape
        tr = R // ADD_TILES
        assert tr % 8 == 0, a.shape
        in_specs.append(pl.BlockSpec((1, tr, N), lambda s, i, sel_ref: (a_idx_fn(s, sel_ref[0]), i, 0)))
        args.append(a)
        for arr, fixed in others:
            if fixed is None:
                in_specs.append(pl.BlockSpec((1, tr, N), lambda s, i, sel_ref: (s, i, 0)))
            else:
                in_specs.append(pl.BlockSpec((1, tr, N), lambda s, i, sel_ref, fixed=fixed: (fixed, i, 0)))
            args.append(arr)
        out_specs.append(pl.BlockSpec((1, tr, N), lambda s, i, sel_ref: (s, i, 0)))
        out_shape.append(jax.ShapeDtypeStruct((ns, R, N), a.dtype))
    grid_spec = pltpu.PrefetchScalarGridSpec(num_scalar_prefetch=1, grid=(ns, ADD_TILES), in_specs=in_specs,
                                             out_specs=out_specs)
    return pl.pallas_call(
        body, name=name, out_shape=out_shape, grid_spec=grid_spec,
        compiler_params=_cparams(("parallel", "parallel")),
    )(sel, *args)


def _rs_first(g8s, tag):
    c_sel = jnp.reshape(lax.axis_index("c"), (1,)).astype(jnp.int32)
    r1 = _rs_to_sibling(g8s, f"rs_to_sibling_{tag}")
    return _add_blocks(g8s, lambda s, cc: 2 * s + cc, [[(r, None)] for r in r1], N_CHIP, c_sel,
                       f"rs_add_sibling_{tag}")


def _rs_last(a4s, r2s, tag):
    s_sel = jnp.reshape(2 * lax.axis_index("x") + lax.axis_index("y"), (1,)).astype(jnp.int32)
    h = _add_blocks(a4s, lambda s, ss: ss, [[(r, 0), (r, 1), (r, 2)] for r in r2s], 1, s_sel,
                    f"rs_add_chips_{tag}")
    full = _swap_halves([hk.reshape(hk.shape[1:]) for hk in h], f"rs_swap_halves_{tag}")
    return [f.reshape(2 * f.shape[1], f.shape[2]) for f in full]


def _reduce_scatter(g8s, tag):
    a4 = _rs_first(g8s, tag)
    return _rs_last(a4, _rs_to_chips(a4, f"rs_to_chips_{tag}"), tag)


def _ada_fwd(c_all, w_ada, b_ada, name):
    nb, D = c_all.shape
    ncol = w_ada.shape[1]
    tc = 512

    def body(c_ref, w_ref, b_ref, o_ref):
        cv = c_ref[...]
        cond = (cv * jax.nn.sigmoid(cv)).astype(BF16)
        o_ref[...] = jnp.dot(cond, w_ref[...].astype(BF16), preferred_element_type=F32) + b_ref[...]

    return pl.pallas_call(
        body, name=name, out_shape=jax.ShapeDtypeStruct((nb, ncol), F32), grid=(ncol // tc,),
        in_specs=[pl.BlockSpec((nb, D), lambda j: (0, 0)), pl.BlockSpec((D, tc), lambda j: (0, j)),
                  pl.BlockSpec((1, tc), lambda j: (0, j))],
        out_specs=pl.BlockSpec((nb, tc), lambda j: (0, j)),
        compiler_params=_cparams(("parallel",)),
    )(c_all, w_ada, b_ada)


def _ada_bwd(c_all, gmod_cols, name):
    nb, D = c_all.shape
    ncol = gmod_cols.shape[1]
    tc = 512

    def body(c_ref, g_ref, o_ref):
        cv = c_ref[...]
        cond = (cv * jax.nn.sigmoid(cv)).astype(BF16)
        o_ref[...] = _dot_tn(cond, g_ref[...].astype(BF16))

    return pl.pallas_call(
        body, name=name, out_shape=jax.ShapeDtypeStruct((D, ncol), F32), grid=(ncol // tc,),
        in_specs=[pl.BlockSpec((nb, D), lambda j: (0, 0)), pl.BlockSpec((nb, tc), lambda j: (0, j))],
        out_specs=pl.BlockSpec((D, tc), lambda j: (0, j)),
        compiler_params=_cparams(("parallel",)),
    )(c_all, gmod_cols)


def _adam_math(w, g, m, v):
    m = ADAM_B1 * m + (1.0 - ADAM_B1) * g
    v = ADAM_B2 * v + (1.0 - ADAM_B2) * (g * g)
    m_hat = m / (1.0 - ADAM_B1 ** ADAM_STEP)
    v_hat = v / (1.0 - ADAM_B2 ** ADAM_STEP)
    delta = -ADAM_LR * (m_hat / (jnp.sqrt(v_hat) + ADAM_EPS) + ADAM_WD * w)
    return delta, m, v


def _adamw(w, g, m, v, name):
    rows, cols = w.shape
    tr = _pick(rows, (256, 192, 176, 128, 64, 8))

    def body(w_ref, g_ref, m_ref, v_ref, d_ref, mo_ref, vo_ref):
        d, mn, vn = _adam_math(w_ref[...], g_ref[...], m_ref[...], v_ref[...])
        d_ref[...] = d
        mo_ref[...] = mn
        vo_ref[...] = vn

    spec = pl.BlockSpec((tr, cols), lambda i: (i, 0))
    return pl.pallas_call(
        body, name=name, out_shape=[jax.ShapeDtypeStruct((rows, cols), F32)] * 3, grid=(rows // tr,),
        in_specs=[spec] * 4, out_specs=[spec] * 3, compiler_params=_cparams(("parallel",)),
    )(w, g, m, v)


VEC_ROWS = 8


def _adamw_rows(w, parts, m, v, name):
    n = w.shape[1]
    P = parts.shape[0]
    assert n % (VEC_ROWS * LANES) == 0, n
    shp = (VEC_ROWS, n // VEC_ROWS)

    def body(w_ref, p_ref, m_ref, v_ref, g_ref, d_ref, mo_ref, vo_ref):
        g = p_ref[0]
        for k in range(1, P):
            g = g + p_ref[k]
        d, mn, vn = _adam_math(w_ref[...], g, m_ref[...], v_ref[...])
        g_ref[...] = g
        d_ref[...] = d
        mo_ref[...] = mn
        vo_ref[...] = vn

    vec = pl.BlockSpec(shp, lambda i: (0, 0))
    out = pl.pallas_call(
        body, name=name, out_shape=[jax.ShapeDtypeStruct(shp, F32)] * 4, grid=(1,),
        in_specs=[vec, pl.BlockSpec((P,) + shp, lambda i: (0, 0, 0)), vec, vec], out_specs=[vec] * 4,
        compiler_params=_cparams(("arbitrary",)),
    )(w.reshape(shp), parts.reshape((P,) + shp), m.reshape(shp), v.reshape(shp))
    return [o.reshape(1, n) for o in out]


_PACKED = (("w_in", 1024, 552), ("w_uq", 384, 192), ("w_ukv", 256, 256))
_SHARDED = ("w_in", "w_uq", "w_ukv", "w_out", "w_ffn_in", "w_ffn_out")
_SMALL = (("g_norm1", 1024), ("g_cq", 384), ("g_ckv", 256), ("rel_bias", 256), ("g_out_a", 512),
          ("g_out_b", 512), ("g_norm2", 1024), ("g_final", 1024))
_SMALL_PAD = 5120
PACK_ROWS = 704
_PACK_ELEMS = PACK_ROWS * D_MODEL


def _pack_shards(shards, dtype):
    lead = shards["w_in"].shape[:-2]
    flat = jnp.concatenate([shards[n].astype(dtype).reshape(lead + (-1,)) for n, _, _ in _PACKED], axis=-1)
    pad = [(0, 0)] * len(lead) + [(0, _PACK_ELEMS - flat.shape[-1])]
    return jnp.pad(flat, pad).reshape(lead + (PACK_ROWS, D_MODEL))


def _unpack_shards(packed):
    out, off = {}, 0
    for n, r, c in _PACKED:
        out[n] = packed[..., off:off + r * c].reshape(packed.shape[:-1] + (r, c))
        off += r * c
    return out


def _full_from_shards(sh):
    return jnp.transpose(sh, (1, 0, 2)).reshape(sh.shape[1], -1)


def _shards_from_full(full):
    rows, cols = full.shape
    return jnp.transpose(full.reshape(rows, N_CHIP, cols // N_CHIP), (1, 0, 2))


def kernel(x, c, w_ada, b_ada, g_norm1, w_in, g_cq, w_uq, g_ckv, w_ukv, rel_bias, g_out_a, g_out_b, w_out, g_norm2, w_ffn_in, w_ffn_out, g_final, loss_target, m_w_ada, m_b_ada, m_g_norm1, m_w_in, m_g_cq, m_w_uq, m_g_ckv, m_w_ukv, m_rel_bias, m_g_out_a, m_g_out_b, m_w_out, m_g_norm2, m_w_ffn_in, m_w_ffn_out, m_g_final, v_w_ada, v_b_ada, v_g_norm1, v_w_in, v_g_cq, v_w_uq, v_g_ckv, v_w_ukv, v_rel_bias, v_g_out_a, v_g_out_b, v_w_out, v_g_norm2, v_w_ffn_in, v_w_ffn_out, v_g_final):
    names = ["w_ada", "b_ada", "g_norm1", "w_in", "g_cq", "w_uq", "g_ckv", "w_ukv", "rel_bias", "g_out_a",
             "g_out_b", "w_out", "g_norm2", "w_ffn_in", "w_ffn_out", "g_final"]
    W = dict(zip(names, [w_ada, b_ada, g_norm1, w_in, g_cq, w_uq, g_ckv, w_ukv, rel_bias, g_out_a, g_out_b,
                         w_out, g_norm2, w_ffn_in, w_ffn_out, g_final]))
    M = dict(zip(names, [m_w_ada, m_b_ada, m_g_norm1, m_w_in, m_g_cq, m_w_uq, m_g_ckv, m_w_ukv, m_rel_bias,
                         m_g_out_a, m_g_out_b, m_w_out, m_g_norm2, m_w_ffn_in, m_w_ffn_out, m_g_final]))
    V = dict(zip(names, [v_w_ada, v_b_ada, v_g_norm1, v_w_in, v_g_cq, v_w_uq, v_g_ckv, v_w_ukv, v_rel_bias,
                         v_g_out_a, v_g_out_b, v_w_out, v_g_norm2, v_w_ffn_in, v_w_ffn_out, v_g_final]))
    B, S, D = x.shape
    mx, my, mc = _my_place()
    dev = 4 * mx + 2 * my + mc
    chip = 2 * mx + my
    pad_rows = 8

    c_all = _allgather8(jnp.pad(c, ((0, pad_rows - B), (0, 0))), "ag_c", False)
    c_all = c_all.reshape(N_DEV, pad_rows, D)[:, :B].reshape(N_DEV * B, D)
    ada_cols = w_ada.shape[-1]
    b_cols = lax.dynamic_slice_in_dim(b_ada, chip * ada_cols, ada_cols, axis=1)
    mod_cols = _ada_fwd(c_all, w_ada[0], b_cols, "ada_fwd")
    mod_all = _allgather8(mod_cols, "ag_mod", False).reshape(N_DEV, N_DEV * B, ada_cols)[0::2]
    mod_all = jnp.transpose(mod_all, (1, 0, 2)).reshape(N_DEV * B, N_MOD * D)
    mod = lax.dynamic_slice_in_dim(mod_all, dev * B, B, axis=0)

    packed = _pack_shards({n: W[n][0] for n, _, _ in _PACKED}, BF16)
    g_packed, g_out = _gather_weights([packed, w_out[0].astype(BF16)], "ag_weights")
    full = {n: _full_from_shards(sh) for n, sh in _unpack_shards(g_packed.reshape(N_CHIP, _PACK_ELEMS)).items()}
    wts = dict(w_in=_w_in_to_kernel(full["w_in"]), w_uq=_w_uq_to_kernel(full["w_uq"]),
               w_kv=_w_ukv_to_kernel(full["w_ukv"]), w_out=g_out.reshape(D, D))
    gains = dict(g_norm1=g_norm1, g_cq=g_cq, g_ckv=g_ckv, g_out_a=g_out_a, g_out_b=g_out_b, g_norm2=g_norm2,
                 g_final=g_final.reshape(1, D))

    loss, grad_x, gmod, grads = _local_step(x, loss_target, mod, wts, gains, rel_bias,
                                            ffn_shards=[w_ffn_in[0].astype(BF16), w_ffn_out[0].astype(BF16)])
    loss = lax.psum(loss[0, 0], ("x", "y", "c"))

    n_small = _SMALL_PAD
    cat = lambda dct: jnp.concatenate([dct[n].reshape(1, -1) for n, _ in _SMALL]
                                      + [jnp.zeros((1, _SMALL_PAD - sum(s for _, s in _SMALL)), F32)], axis=1)
    small = cat(grads)
    rows = jnp.concatenate([gmod, jnp.pad(small, ((0, 0), (0, N_MOD * D - n_small))),
                            jnp.zeros((pad_rows - B - 1, N_MOD * D), F32)], axis=0)
    rows_all = _allgather8(rows, "ag_small", False).reshape(N_DEV, pad_rows, N_MOD * D)
    gmod_all = rows_all[:, :B].reshape(N_DEV * B, N_MOD * D)
    small_parts = rows_all[:, B, :n_small]

    nat = dict(w_in=_w_in_from_kernel(grads["w_in"]), w_uq=_w_uq_from_kernel(grads["w_uq"]),
               w_ukv=_w_ukv_from_kernel(grads["w_kv"]))
    gp = _pack_shards({n: _shards_from_full(nat[n]) for n, _, _ in _PACKED}, F32)
    as_halves = lambda a: a.reshape(N_DEV, -1, a.shape[-1])
    a4 = _rs_first([as_halves(gp), as_halves(grads["w_out"])], "mix")
    r2 = _rs_to_chips(a4, "rs_to_chips_mix")
    ffn_a4, ffn_r2 = grads["ffn_pending"]
    r_packed, r_out, r_ffn_in, r_ffn_out = _rs_last(list(a4) + list(ffn_a4), list(r2) + list(ffn_r2), "all")
    G = _unpack_shards(r_packed.reshape(_PACK_ELEMS))
    G.update(w_out=r_out, w_ffn_in=r_ffn_in, w_ffn_out=r_ffn_out)

    gmod_cols = lax.dynamic_slice_in_dim(gmod_all, chip * ada_cols, ada_cols, axis=1)
    G["w_ada"] = _ada_bwd(c_all, gmod_cols, "ada_bwd")
    delta, new_m, new_v = {}, {}, {}
    for n in ("w_ada",) + _SHARDED:
        shp = W[n].shape
        w2 = W[n].reshape(shp[-2], shp[-1])
        d_, m_, v_ = _adamw(w2, G[n], M[n].reshape(w2.shape), V[n].reshape(w2.shape), f"adamw_{n}")
        G[n], delta[n], new_m[n], new_v[n] = [a.reshape(shp) for a in (G[n], d_, m_, v_)]
    gs, ds_, ms_, vs_ = _adamw_rows(cat(W), small_parts, cat(M), cat(V), "adamw_small")
    off = 0
    for n, sz in _SMALL:
        shp = W[n].shape
        G[n], delta[n], new_m[n], new_v[n] = [a[:, off:off + sz].reshape(shp) for a in (gs, ds_, ms_, vs_)]
        off += sz
    G["b_ada"], delta["b_ada"], new_m["b_ada"], new_v["b_ada"] = _adamw_rows(b_ada, gmod_all, m_b_ada, v_b_ada,
                                                                          "adamw_b_ada")
    return (loss, grad_x, *[G[n] for n in names], *[delta[n] for n in names], *[new_m[n] for n in names],
            *[new_v[n] for n in names])
```

```python
import functools
import math

import numpy as np
import jax
import jax.numpy as jnp
from jax import lax
from jax.experimental import pallas as pl
from jax.experimental.pallas import tpu as pltpu

F32 = jnp.float32
BF16 = jnp.bfloat16

D_MODEL = 1024
SEQ = 2048
N_HEADS = 8
HEAD_DIM = 64
D_A = 512
D_B = 512
Q_LORA = 384
KV_LORA = 256
ROPE_DIM = 32
NOPE_DIM = 64
D_FF = 2816
N_MOD = 6
N_BUCKETS = 32
MAX_DISTANCE = 2048
ROPE_THETA = 10000.0
EPS = 1e-6
NEG = -1e30
BLK = 128
DILATIONS = (1, 4, 16)
SPAN = 128
MLA_SCALE = (NOPE_DIM + ROPE_DIM) ** -0.5
DIL_SCALE = HEAD_DIM ** -0.5

ADAM_LR = 0.001
ADAM_B1 = 0.9
ADAM_B2 = 0.999
ADAM_EPS = 1e-08
ADAM_WD = 0.01
ADAM_STEP = 10

N_DEV = 8
N_CHIP = 4
LANES = 128
VMEM_LIMIT = 48 * 1024 * 1024

P_QKV = 3 * D_A
P_REST = KV_LORA + LANES + Q_LORA


def _cparams(sem=None):
    return pltpu.CompilerParams(dimension_semantics=sem, vmem_limit_bytes=VMEM_LIMIT)


def _pick(n, cands):
    for c in cands:
        if n % c == 0:
            return c
    raise ValueError(f"no tile for {n} in {cands}")


def _mm(a, b, mode, out_dtype, name, col_blocks=None):
    blocked = col_blocks is not None
    if mode == "nn":
        (M, K) = a.shape
        K2, N = (b.shape[1], b.shape[0] * b.shape[2]) if blocked else b.shape
    elif mode == "nt":
        (M, K) = a.shape
        N, K2 = (b.shape[1], b.shape[0] * b.shape[2]) if blocked else b.shape
    else:
        (K, M), (K2, N) = a.shape, b.shape
    assert K == K2, (a.shape, b.shape, mode)
    tm = _pick(M, (512, 384, 256, 128))
    tn = _pick(N, (1408, 1024, 768, 512, 384, 256, 128))
    tk = _pick(K, (1024, 512, 384, 256, 128))
    if blocked and mode == "nt":
        tk = K // col_blocks
    elif blocked:
        tn = N // col_blocks
    nk = K // tk
    out_shape = (M, N)
    out_spec = pl.BlockSpec((tm, tn), lambda i, j, k: (i, j))
    if mode == "nn":
        a_spec = pl.BlockSpec((tm, tk), lambda i, j, k: (i, k))
        b_spec = (pl.BlockSpec((None, tk, tn), lambda i, j, k: (j, k, 0)) if blocked
                  else pl.BlockSpec((tk, tn), lambda i, j, k: (k, j)))
        dn = (((1,), (0,)), ((), ()))
    elif mode == "nt":
        a_spec = pl.BlockSpec((tm, tk), lambda i, j, k: (i, k))
        b_spec = (pl.BlockSpec((None, tn, tk), lambda i, j, k: (k, j, 0)) if blocked
                  else pl.BlockSpec((tn, tk), lambda i, j, k: (j, k)))
        dn = (((1,), (1,)), ((), ()))
    else:
        a_spec = pl.BlockSpec((tk, tm), lambda i, j, k: (k, i))
        b_spec = pl.BlockSpec((tk, tn), lambda i, j, k: (k, j))
        dn = (((0,), (0,)), ((), ()))
        if blocked:
            out_shape = (col_blocks, M, tn)
            out_spec = pl.BlockSpec((None, tm, tn), lambda i, j, k: (j, i, 0))

    def body(a_ref, b_ref, o_ref, acc_ref):
        k = pl.program_id(2)

        @pl.when(k == 0)
        def _():
            acc_ref[...] = jnp.zeros_like(acc_ref)

        acc_ref[...] += lax.dot_general(a_ref[...].astype(BF16), b_ref[...].astype(BF16), dn,
                                        preferred_element_type=F32)

        @pl.when(k == nk - 1)
        def _():
            o_ref[...] = acc_ref[...].astype(o_ref.dtype)

    return pl.pallas_call(
        body, name=name,
        out_shape=jax.ShapeDtypeStruct(out_shape, out_dtype),
        grid=(M // tm, N // tn, nk),
        in_specs=[a_spec, b_spec],
        out_specs=out_spec,
        scratch_shapes=[pltpu.VMEM((tm, tn), F32)],
        compiler_params=_cparams(("parallel", "parallel", "arbitrary")),
    )(a, b)


ROW_TILE = 256


def _adaln_fwd(x, g, sc, sh, name, mix=None, gate=None):
    B, S, D = x.shape
    ts = ROW_TILE
    has_res = mix is not None

    def body(*refs):
        if has_res:
            x_ref, g_ref, sc_ref, sh_ref, mix_ref, gate_ref, h_ref, xr_ref = refs
            xr = x_ref[0] + gate_ref[0] * mix_ref[0]
            xr_ref[0] = xr
        else:
            x_ref, g_ref, sc_ref, sh_ref, h_ref = refs
            xr = x_ref[0]
        r = lax.rsqrt(jnp.mean(xr * xr, axis=-1, keepdims=True) + EPS)
        xn = (xr * r) * g_ref[...]
        h_ref[0] = (xn * (1.0 + sc_ref[0]) + sh_ref[0]).astype(h_ref.dtype)

    tok = pl.BlockSpec((1, ts, D), lambda b, s: (b, s, 0))
    per_b = pl.BlockSpec((1, 1, D), lambda b, s: (b, 0, 0))
    vec = pl.BlockSpec((1, D), lambda b, s: (0, 0))
    in_specs = [tok, vec, per_b, per_b]
    args = [x, g, sc, sh]
    out_shape = [jax.ShapeDtypeStruct((B, S, D), BF16)]
    out_specs = [tok]
    if has_res:
        in_specs += [tok, per_b]
        args += [mix, gate]
        out_shape.append(jax.ShapeDtypeStruct((B, S, D), F32))
        out_specs.append(tok)
    out = pl.pallas_call(
        body, name=name, out_shape=out_shape, grid=(B, S // ts),
        in_specs=in_specs, out_specs=out_specs,
        compiler_params=_cparams(("parallel", "parallel")),
    )(*args)
    return out if has_res else out[0]


def _adaln_bwd(dh, x, g, sc, dres, name, mix=None, gate=None):
    B, S, D = x.shape
    ts = ROW_TILE
    has_res = mix is not None

    def body(*refs):
        if has_res:
            (dh_ref, x_ref, g_ref, sc_ref, dres_ref, mix_ref, gate_ref,
             dx_ref, dsh_ref, dsc_ref, dg_ref, dgate_ref, dmix_ref) = refs
        else:
            (dh_ref, x_ref, g_ref, sc_ref, dres_ref, dx_ref, dsh_ref, dsc_ref, dg_ref) = refs
        b, s = pl.program_id(0), pl.program_id(1)
        xv = x_ref[0]
        dhv = dh_ref[0]
        gv = g_ref[...]
        r = lax.rsqrt(jnp.mean(xv * xv, axis=-1, keepdims=True) + EPS)
        n = xv * r
        xn = n * gv
        dxn = dhv * (1.0 + sc_ref[0])
        dn = dxn * gv
        dx = r * (dn - n * jnp.mean(dn * n, axis=-1, keepdims=True)) + dres_ref[0]
        dx_ref[0] = dx

        @pl.when(s == 0)
        def _():
            dsh_ref[...] = jnp.zeros_like(dsh_ref)
            dsc_ref[...] = jnp.zeros_like(dsc_ref)
            if has_res:
                dgate_ref[...] = jnp.zeros_like(dgate_ref)

        @pl.when((s == 0) & (b == 0))
        def _():
            dg_ref[...] = jnp.zeros_like(dg_ref)

        dsh_ref[0] += jnp.sum(dhv, axis=0, keepdims=True)
        dsc_ref[0] += jnp.sum(dhv * xn, axis=0, keepdims=True)
        dg_ref[...] += jnp.sum(dxn * n, axis=0, keepdims=True)
        if has_res:
            dgate_ref[0] += jnp.sum(dx * mix_ref[0], axis=0, keepdims=True)
            dmix_ref[0] = (dx * gate_ref[0]).astype(dmix_ref.dtype)

    tok = pl.BlockSpec((1, ts, D), lambda b, s: (b, s, 0))
    per_b = pl.BlockSpec((1, 1, D), lambda b, s: (b, 0, 0))
    vec = pl.BlockSpec((1, D), lambda b, s: (0, 0))
    in_specs = [tok, tok, vec, per_b, tok]
    args = [dh, x, g, sc, dres]
    out_shape = [jax.ShapeDtypeStruct((B, S, D), F32), jax.ShapeDtypeStruct((B, 1, D), F32),
                 jax.ShapeDtypeStruct((B, 1, D), F32), jax.ShapeDtypeStruct((1, D), F32)]
    out_specs = [tok, per_b, per_b, vec]
    if has_res:
        in_specs += [tok, per_b]
        args += [mix, gate]
        out_shape += [jax.ShapeDtypeStruct((B, 1, D), F32), jax.ShapeDtypeStruct((B, S, D), BF16)]
        out_specs += [per_b, tok]
    return pl.pallas_call(
        body, name=name, out_shape=out_shape, grid=(B, S // ts),
        in_specs=in_specs, out_specs=out_specs,
        compiler_params=_cparams(("arbitrary", "arbitrary")),
    )(*args)


def _rms_fwd(x, col_blk, n, g, name, n_real=None):
    T = x.shape[0]
    tr = 512
    nr = float(n_real or n)

    def body(x_ref, g_ref, y_ref):
        xv = x_ref[...]
        r = lax.rsqrt(jnp.sum(xv * xv, axis=-1, keepdims=True) / nr + EPS)
        y_ref[...] = ((xv * r) * g_ref[...]).astype(y_ref.dtype)

    return pl.pallas_call(
        body, name=name, out_shape=jax.ShapeDtypeStruct((T, n), BF16), grid=(T // tr,),
        in_specs=[pl.BlockSpec((tr, n), lambda i: (i, col_blk)), pl.BlockSpec((1, n), lambda i: (0, 0))],
        out_specs=pl.BlockSpec((tr, n), lambda i: (i, 0)),
        compiler_params=_cparams(("parallel",)),
    )(x, g)


def _rms_bwd(dy, dy_blk, x, x_blk, n, g, name, out_dtype=BF16):
    T = x.shape[0]
    tr = 512

    def body(dy_ref, x_ref, g_ref, dx_ref, dg_ref):
        xv = x_ref[...]
        dyv = dy_ref[...].astype(F32)
        r = lax.rsqrt(jnp.mean(xv * xv, axis=-1, keepdims=True) + EPS)
        nrm = xv * r
        dn = dyv * g_ref[...]
        dx_ref[...] = (r * (dn - nrm * jnp.mean(dn * nrm, axis=-1, keepdims=True))).astype(dx_ref.dtype)

        @pl.when(pl.program_id(0) == 0)
        def _():
            dg_ref[...] = jnp.zeros_like(dg_ref)

        dg_ref[...] += jnp.sum(dyv * nrm, axis=0, keepdims=True)

    return pl.pallas_call(
        body, name=name,
        out_shape=[jax.ShapeDtypeStruct((T, n), out_dtype), jax.ShapeDtypeStruct((1, n), F32)],
        grid=(T // tr,),
        in_specs=[pl.BlockSpec((tr, n), lambda i: (i, dy_blk)), pl.BlockSpec((tr, n), lambda i: (i, x_blk)),
                  pl.BlockSpec((1, n), lambda i: (0, 0))],
        out_specs=[pl.BlockSpec((tr, n), lambda i: (i, 0)), pl.BlockSpec((1, n), lambda i: (0, 0))],
        compiler_params=_cparams(("arbitrary",)),
    )(dy, x, g)


def _swiglu_fwd(gu, name):
    T = gu.shape[0]
    tr, tc = 512, 1408
    nc = D_FF // tc

    def body(g_ref, u_ref, a_ref):
        gv = g_ref[...]
        a_ref[...] = (gv * jax.nn.sigmoid(gv) * u_ref[...]).astype(a_ref.dtype)

    return pl.pallas_call(
        body, name=name, out_shape=jax.ShapeDtypeStruct((T, D_FF), BF16), grid=(T // tr, nc),
        in_specs=[pl.BlockSpec((tr, tc), lambda i, j: (i, j)), pl.BlockSpec((tr, tc), lambda i, j: (i, j + nc))],
        out_specs=pl.BlockSpec((tr, tc), lambda i, j: (i, j)),
        compiler_params=_cparams(("parallel", "parallel")),
    )(gu, gu)


def _swiglu_bwd(da, gu, name):
    T = gu.shape[0]
    tr, tc = 512, 1408
    nc = D_FF // tc

    def body(da_ref, g_ref, u_ref, dgu_ref):
        j = pl.program_id(1)
        gv, uv, dav = g_ref[...], u_ref[...], da_ref[...]
        sg = jax.nn.sigmoid(gv)

        @pl.when(j < nc)
        def _():
            dgu_ref[...] = (dav * uv * (sg * (1.0 + gv * (1.0 - sg)))).astype(dgu_ref.dtype)

        @pl.when(j >= nc)
        def _():
            dgu_ref[...] = (dav * (gv * sg)).astype(dgu_ref.dtype)

    return pl.pallas_call(
        body, name=name, out_shape=jax.ShapeDtypeStruct((T, 2 * D_FF), BF16), grid=(T // tr, 2 * nc),
        in_specs=[pl.BlockSpec((tr, tc), lambda i, j: (i, j % nc)),
                  pl.BlockSpec((tr, tc), lambda i, j: (i, j % nc)),
                  pl.BlockSpec((tr, tc), lambda i, j: (i, j % nc + nc))],
        out_specs=pl.BlockSpec((tr, tc), lambda i, j: (i, j)),
        compiler_params=_cparams(("parallel", "parallel")),
    )(da, gu, gu)


def _final_loss(x1, f, g2, gf, target, name):
    B, S, D = x1.shape
    ts = ROW_TILE

    def body(x1_ref, f_ref, g2_ref, gf_ref, t_ref, dx_ref, df_ref, dg2_ref, dgf_ref, loss_ref):
        b, s = pl.program_id(0), pl.program_id(1)
        fv = f_ref[0]
        g2v = g2_ref[0]
        gfv = gf_ref[...]
        x2 = x1_ref[0] + g2v * fv
        r = lax.rsqrt(jnp.mean(x2 * x2, axis=-1, keepdims=True) + EPS)
        n = x2 * r
        e = n * gfv - t_ref[0]
        dy = e * (1.0 / D)
        dn = dy * gfv
        dx = r * (dn - n * jnp.mean(dn * n, axis=-1, keepdims=True))
        dx_ref[0] = dx
        df_ref[0] = (dx * g2v).astype(df_ref.dtype)

        @pl.when(s == 0)
        def _():
            dg2_ref[...] = jnp.zeros_like(dg2_ref)

        @pl.when((s == 0) & (b == 0))
        def _():
            dgf_ref[...] = jnp.zeros_like(dgf_ref)
            loss_ref[...] = jnp.zeros_like(loss_ref)

        dg2_ref[0] += jnp.sum(dx * fv, axis=0, keepdims=True)
        dgf_ref[...] += jnp.sum(dy * n, axis=0, keepdims=True)
        loss_ref[...] += 0.5 * jnp.sum(jnp.mean(e * e, axis=-1, keepdims=True), axis=0, keepdims=True)

    tok = pl.BlockSpec((1, ts, D), lambda b, s: (b, s, 0))
    per_b = pl.BlockSpec((1, 1, D), lambda b, s: (b, 0, 0))
    vec = pl.BlockSpec((1, D), lambda b, s: (0, 0))
    return pl.pallas_call(
        body, name=name,
        out_shape=[jax.ShapeDtypeStruct((B, S, D), F32), jax.ShapeDtypeStruct((B, S, D), BF16),
                   jax.ShapeDtypeStruct((B, 1, D), F32), jax.ShapeDtypeStruct((1, D), F32),
                   jax.ShapeDtypeStruct((1, LANES), F32)],
        grid=(B, S // ts),
        in_specs=[tok, tok, per_b, vec, tok],
        out_specs=[tok, tok, per_b, vec, pl.BlockSpec((1, LANES), lambda b, s: (0, 0))],
        compiler_params=_cparams(("arbitrary", "arbitrary")),
    )(x1, f, g2, gf, target)


def _rope_tables():
    half = ROPE_DIM // 2
    inv = ROPE_THETA ** (-jnp.arange(half, dtype=F32) / half)
    ang = jnp.arange(SEQ, dtype=F32)[:, None] * inv[None, :]
    cos, sin = jnp.cos(ang), jnp.sin(ang)
    one = jnp.ones((SEQ, NOPE_DIM), F32)
    zero = jnp.zeros((SEQ, NOPE_DIM), F32)
    cs = jnp.concatenate([one, cos, cos, one[:, :LANES - NOPE_DIM - ROPE_DIM]], axis=1)
    sn = jnp.concatenate([zero, -sin, sin, zero[:, :LANES - NOPE_DIM - ROPE_DIM]], axis=1)
    return cs, sn


def _rope_group(t, cs, sn):
    half = ROPE_DIM // 2
    lane = lax.broadcasted_iota(jnp.int32, t.shape, 1)
    partner = jnp.where(lane < NOPE_DIM + half, pltpu.roll(t, LANES - half, 1), pltpu.roll(t, half, 1))
    return t * cs + partner * sn


def _rope_apply(t, cs, sn, out_dtype, name, add=None, add_blk=0):
    B, S, W = t.shape
    G = W // LANES
    ts = ROW_TILE

    def body(*refs):
        if add is None:
            t_ref, cs_ref, sn_ref, o_ref = refs
            for gi in range(G):
                sl = slice(gi * LANES, (gi + 1) * LANES)
                o_ref[0, :, sl] = _rope_group(t_ref[0, :, sl], cs_ref[...], sn_ref[...]).astype(o_ref.dtype)
        else:
            t_ref, a_ref, cs_ref, sn_ref, o_ref = refs
            ra = _rope_group(a_ref[0], cs_ref[...], sn_ref[...])
            for gi in range(G):
                sl = slice(gi * LANES, (gi + 1) * LANES)
                o_ref[0, :, sl] = (t_ref[0, :, sl] + ra).astype(o_ref.dtype)

    tok = pl.BlockSpec((1, ts, W), lambda b, s: (b, s, 0))
    tab = pl.BlockSpec((ts, LANES), lambda b, s: (s, 0))
    in_specs, args = [tok], [t]
    if add is not None:
        in_specs.append(pl.BlockSpec((1, ts, LANES), lambda b, s: (b, s, add_blk)))
        args.append(add)
    in_specs += [tab, tab]
    args += [cs, sn]
    return pl.pallas_call(
        body, name=name, out_shape=jax.ShapeDtypeStruct((B, S, W), out_dtype), grid=(B, S // ts),
        in_specs=in_specs, out_specs=tok, compiler_params=_cparams(("parallel", "parallel")),
    )(*args)


def _krope_bwd(dkc, cs, sn_neg, name):
    B, S, W = dkc.shape
    G = W // LANES
    ts = ROW_TILE

    def body(d_ref, cs_ref, sn_ref, o_ref):
        acc = d_ref[0, :, 0:LANES]
        for gi in range(1, G):
            acc = acc + d_ref[0, :, gi * LANES:(gi + 1) * LANES]
        lane = lax.broadcasted_iota(jnp.int32, acc.shape, 1)
        rot = (lane >= NOPE_DIM) & (lane < NOPE_DIM + ROPE_DIM)
        acc = jnp.where(rot, acc, 0.0)
        o_ref[0] = _rope_group(acc, cs_ref[...], sn_ref[...]).astype(o_ref.dtype)

    tab = pl.BlockSpec((ts, LANES), lambda b, s: (s, 0))
    return pl.pallas_call(
        body, name=name, out_shape=jax.ShapeDtypeStruct((B, S, LANES), BF16), grid=(B, S // ts),
        in_specs=[pl.BlockSpec((1, ts, W), lambda b, s: (b, s, 0)), tab, tab],
        out_specs=pl.BlockSpec((1, ts, LANES), lambda b, s: (b, s, 0)),
        compiler_params=_cparams(("parallel", "parallel")),
    )(dkc, cs, sn_neg)


def _t5_bucket(dist):
    max_exact = N_BUCKETS // 2
    d = np.maximum(dist, 1).astype(np.float64)
    large = max_exact + (np.log(d / max_exact) / np.log(MAX_DISTANCE / max_exact)
                         * (N_BUCKETS - max_exact)).astype(np.int64)
    large = np.minimum(large, N_BUCKETS - 1)
    return np.where(dist < max_exact, dist, large).astype(np.int32)


def _band_buckets(dilation):
    a = np.arange(BLK)[:, None]
    bk = np.arange(2 * BLK)[None, :]
    steps = BLK + a - bk
    return _t5_bucket(np.clip(steps, 0, SPAN) * dilation)


def _head_mask(shape, hh):
    lane = lax.broadcasted_iota(jnp.int32, shape, 1)
    return (lane >= hh * HEAD_DIM) & (lane < (hh + 1) * HEAD_DIM)


def _dot_nt(a, b):
    return lax.dot_general(a, b, (((1,), (1,)), ((), ())), preferred_element_type=F32)


def _dot_tn(a, b):
    return lax.dot_general(a, b, (((0,), (0,)), ((), ())), preferred_element_type=F32)


def _dot_nn(a, b):
    return lax.dot_general(a, b, (((1,), (0,)), ((), ())), preferred_element_type=F32)


def _band_valid():
    a = lax.broadcasted_iota(jnp.int32, (BLK, BLK), 0)
    bk = lax.broadcasted_iota(jnp.int32, (BLK, BLK), 1)
    return bk >= a, bk <= a


def _dil_fwd(qkv, bias, branch, dilation, name):
    B, S, _ = qkv.shape
    d = dilation
    n = S // d
    nb = n // BLK
    qkv_v = qkv.reshape(B, n, d * P_QKV)
    npair = N_HEADS // 2

    def body(cur_ref, prev_ref, bias_ref, o_ref, lse_ref):
        i = pl.program_id(2)
        vprev, vcur = _band_valid()
        vprev = vprev & (i > 0)
        for p in range(npair):
            sl = slice(p * LANES, (p + 1) * LANES)
            q = cur_ref[0, :, sl]
            kc = cur_ref[0, :, D_A + p * LANES:D_A + (p + 1) * LANES]
            vc = cur_ref[0, :, 2 * D_A + p * LANES:2 * D_A + (p + 1) * LANES]
            kp = prev_ref[0, :, D_A + p * LANES:D_A + (p + 1) * LANES]
            vp = prev_ref[0, :, 2 * D_A + p * LANES:2 * D_A + (p + 1) * LANES]
            o_pair = jnp.zeros((BLK, LANES), F32)
            lse_pair = jnp.zeros((BLK, LANES), F32)
            for hh in range(2):
                h = 2 * p + hh
                hm = _head_mask((BLK, LANES), hh)
                qm = jnp.where(hm, q, jnp.zeros_like(q))
                s_p = _dot_nt(qm, kp) * DIL_SCALE + bias_ref[h, :, 0:BLK]
                s_c = _dot_nt(qm, kc) * DIL_SCALE + bias_ref[h, :, BLK:2 * BLK]
                s_p = jnp.where(vprev, s_p, NEG)
                s_c = jnp.where(vcur, s_c, NEG)
                m = jnp.maximum(jnp.max(s_p, axis=-1, keepdims=True), jnp.max(s_c, axis=-1, keepdims=True))
                e_p = jnp.exp(s_p - m)
                e_c = jnp.exp(s_c - m)
                l = jnp.sum(e_p, axis=-1, keepdims=True) + jnp.sum(e_c, axis=-1, keepdims=True)
                vpm = jnp.where(hm, vp, jnp.zeros_like(vp))
                vcm = jnp.where(hm, vc, jnp.zeros_like(vc))
                o_h = _dot_nn(e_p.astype(BF16), vpm) + _dot_nn(e_c.astype(BF16), vcm)
                o_pair = o_pair + o_h / l
                lse_pair = jnp.where(hm, m + jnp.log(l), lse_pair)
            o_ref[0, :, sl] = o_pair
            lse_ref[0, :, sl] = lse_pair

    cur = pl.BlockSpec((1, BLK, P_QKV), lambda b, r, i: (b, i, r))
    prev = pl.BlockSpec((1, BLK, P_QKV), lambda b, r, i: (b, jnp.maximum(i - 1, 0), r))
    out = pl.BlockSpec((1, BLK, D_A), lambda b, r, i: (b, i, r))
    o, lse = pl.pallas_call(
        body, name=name,
        out_shape=[jax.ShapeDtypeStruct((B, n, d * D_A), F32)] * 2,
        grid=(B, d, nb),
        in_specs=[cur, prev,
                  pl.BlockSpec((None, N_HEADS, BLK, 2 * BLK), lambda b, r, i: (branch, 0, 0, 0))],
        out_specs=[out, out],
        compiler_params=_cparams(("parallel", "parallel", "arbitrary")),
    )(qkv_v, qkv_v, bias)
    return o.reshape(B, S, D_A), lse.reshape(B, S, D_A)


def _dil_merge(os_, lses, name):
    B, S, W = os_[0].shape
    ts = 512

    def body(o0, o1, o2, l0, l1, l2, out_ref, L_ref):
        a0, a1, a2 = l0[0], l1[0], l2[0]
        m = jnp.maximum(jnp.maximum(a0, a1), a2)
        e0, e1, e2 = jnp.exp(a0 - m), jnp.exp(a1 - m), jnp.exp(a2 - m)
        ssum = e0 + e1 + e2
        out_ref[0] = (e0 * o0[0] + e1 * o1[0] + e2 * o2[0]) / ssum
        L_ref[0] = m + jnp.log(ssum)

    tok = pl.BlockSpec((1, ts, W), lambda b, s: (b, s, 0))
    return pl.pallas_call(
        body, name=name, out_shape=[jax.ShapeDtypeStruct((B, S, W), F32)] * 2, grid=(B, S // ts),
        in_specs=[tok] * 6, out_specs=[tok, tok], compiler_params=_cparams(("parallel", "parallel")),
    )(*os_, *lses)


def _dil_bwd(qkv, do, out_a, L, bias, branch, dilation, name):
    B, S, _ = qkv.shape
    d = dilation
    n = S // d
    nb = n // BLK
    qkv_v = qkv.reshape(B, n, d * P_QKV)
    do_v = do.reshape(B, n, d * D_A)
    oa_v = out_a.reshape(B, n, d * D_A)
    L_v = L.reshape(B, n, d * D_A)
    npair = N_HEADS // 2
    multi = nb > 1

    def body(*refs):
        if multi:
            (cur_ref, prev_ref, next_ref, do_ref, don_ref, oa_ref, oan_ref, L_ref, Ln_ref, bias_ref,
             dqkv_ref, dbias_ref) = refs
        else:
            cur_ref, do_ref, oa_ref, L_ref, bias_ref, dqkv_ref, dbias_ref = refs
        b, r, i = pl.program_id(0), pl.program_id(1), pl.program_id(2)

        @pl.when((b == 0) & (r == 0) & (i == 0))
        def _():
            dbias_ref[...] = jnp.zeros_like(dbias_ref)

        vprev, vcur = _band_valid()
        has_prev = i > 0
        has_next = i < nb - 1
        for p in range(npair):
            sl = slice(p * LANES, (p + 1) * LANES)
            ksl = slice(D_A + p * LANES, D_A + (p + 1) * LANES)
            vsl = slice(2 * D_A + p * LANES, 2 * D_A + (p + 1) * LANES)
            q, kc, vc = cur_ref[0, :, sl], cur_ref[0, :, ksl], cur_ref[0, :, vsl]
            dov = do_ref[0, :, sl]
            dd = dov.astype(F32) * oa_ref[0, :, sl]
            Lv = L_ref[0, :, sl]
            if multi:
                kp, vp = prev_ref[0, :, ksl], prev_ref[0, :, vsl]
                qn = next_ref[0, :, sl]
                donv = don_ref[0, :, sl]
                ddn = donv.astype(F32) * oan_ref[0, :, sl]
                Lnv = Ln_ref[0, :, sl]
            dq_pair = jnp.zeros((BLK, LANES), F32)
            dk_pair = jnp.zeros((BLK, LANES), F32)
            dv_pair = jnp.zeros((BLK, LANES), F32)
            for hh in range(2):
                h = 2 * p + hh
                hm = _head_mask((BLK, LANES), hh)
                zero = jnp.zeros_like(q)
                qm = jnp.where(hm, q, zero)
                dom = jnp.where(hm, dov, zero)
                delta = jnp.sum(jnp.where(hm, dd, 0.0), axis=-1, keepdims=True)
                lse = Lv[:, hh * HEAD_DIM:hh * HEAD_DIM + 1]
                s_c = _dot_nt(qm, kc) * DIL_SCALE + bias_ref[h, :, BLK:2 * BLK]
                p_c = jnp.where(vcur, jnp.exp(s_c - lse), 0.0)
                ds_c = p_c * (_dot_nt(dom, vc) - delta)
                ds_cb = ds_c.astype(BF16)
                dq_h = _dot_nn(ds_cb, kc)
                dk_h = _dot_tn(ds_cb, qm)
                dv_h = _dot_tn(p_c.astype(BF16), dom)
                dbias_ref[h, :, BLK:2 * BLK] += ds_c
                if multi:
                    s_p = _dot_nt(qm, kp) * DIL_SCALE + bias_ref[h, :, 0:BLK]
                    p_p = jnp.where(vprev & has_prev, jnp.exp(s_p - lse), 0.0)
                    ds_p = p_p * (_dot_nt(dom, vp) - delta)
                    dq_h = dq_h + _dot_nn(ds_p.astype(BF16), kp)
                    dbias_ref[h, :, 0:BLK] += ds_p
                    qnm = jnp.where(hm, qn, zero)
                    donm = jnp.where(hm, donv, zero)
                    delta_n = jnp.sum(jnp.where(hm, ddn, 0.0), axis=-1, keepdims=True)
                    lse_n = Lnv[:, hh * HEAD_DIM:hh * HEAD_DIM + 1]
                    s_n = _dot_nt(qnm, kc) * DIL_SCALE + bias_ref[h, :, 0:BLK]
                    p_n = jnp.where(vprev & has_next, jnp.exp(s_n - lse_n), 0.0)
                    ds_n = p_n * (_dot_nt(donm, vc) - delta_n)
                    dk_h = dk_h + _dot_tn(ds_n.astype(BF16), qnm)
                    dv_h = dv_h + _dot_tn(p_n.astype(BF16), donm)
                dq_pair = dq_pair + jnp.where(hm, dq_h, 0.0) * DIL_SCALE
                dk_pair = dk_pair + jnp.where(hm, dk_h, 0.0) * DIL_SCALE
                dv_pair = dv_pair + jnp.where(hm, dv_h, 0.0)
            dqkv_ref[0, :, sl] = dq_pair
            dqkv_ref[0, :, ksl] = dk_pair
            dqkv_ref[0, :, vsl] = dv_pair

    def at(off):
        return lambda b, r, i: (b, jnp.clip(i + off, 0, nb - 1), r)

    qkv_spec = lambda off: pl.BlockSpec((1, BLK, P_QKV), at(off))
    da_spec = lambda off: pl.BlockSpec((1, BLK, D_A), at(off))
    bias_spec = pl.BlockSpec((None, N_HEADS, BLK, 2 * BLK), lambda b, r, i: (branch, 0, 0, 0))
    dbias_spec = pl.BlockSpec((N_HEADS, BLK, 2 * BLK), lambda b, r, i: (0, 0, 0))
    if multi:
        in_specs = [qkv_spec(0), qkv_spec(-1), qkv_spec(1), da_spec(0), da_spec(1), da_spec(0), da_spec(1),
                    da_spec(0), da_spec(1), bias_spec]
        args = [qkv_v, qkv_v, qkv_v, do_v, do_v, oa_v, oa_v, L_v, L_v, bias]
    else:
        in_specs = [qkv_spec(0), da_spec(0), da_spec(0), da_spec(0), bias_spec]
        args = [qkv_v, do_v, oa_v, L_v, bias]
    dqkv, dbias = pl.pallas_call(
        body, name=name,
        out_shape=[jax.ShapeDtypeStruct((B, n, d * P_QKV), F32),
                   jax.ShapeDtypeStruct((N_HEADS, BLK, 2 * BLK), F32)],
        grid=(B, d, nb),
        in_specs=in_specs,
        out_specs=[qkv_spec(0), dbias_spec],
        compiler_params=_cparams(("arbitrary", "arbitrary", "arbitrary")),
    )(*args)
    return dqkv.reshape(B, S, P_QKV), dbias


def _sum3_bf16(a, b, c, name):
    B, S, W = a.shape
    ts = 512

    def body(a_ref, b_ref, c_ref, o_ref):
        o_ref[...] = (a_ref[...] + b_ref[...] + c_ref[...]).astype(o_ref.dtype)

    tok = pl.BlockSpec((1, ts, W), lambda b, s: (b, s, 0))
    return pl.pallas_call(
        body, name=name, out_shape=jax.ShapeDtypeStruct((B, S, W), BF16), grid=(B, S // ts),
        in_specs=[tok] * 3, out_specs=tok, compiler_params=_cparams(("parallel", "parallel")),
    )(a, b, c)


def _bias_tables(rel_bias, buckets, name):
    nbr = buckets.shape[0]

    def body(rb_ref, bk_ref, o_ref):
        h = pl.program_id(1)
        tab = bk_ref[0]

        def step(bkt, acc):
            return jnp.where(tab == bkt, rb_ref[bkt, h], acc)

        o_ref[0, 0] = lax.fori_loop(0, N_BUCKETS, step, jnp.zeros((BLK, 2 * BLK), F32))

    return pl.pallas_call(
        body, name=name, out_shape=jax.ShapeDtypeStruct((nbr, N_HEADS, BLK, 2 * BLK), F32),
        grid=(nbr, N_HEADS),
        in_specs=[pl.BlockSpec(memory_space=pltpu.SMEM),
                  pl.BlockSpec((1, BLK, 2 * BLK), lambda i, h: (i, 0, 0))],
        out_specs=pl.BlockSpec((1, 1, BLK, 2 * BLK), lambda i, h: (i, h, 0, 0)),
        compiler_params=_cparams(("parallel", "arbitrary")),
    )(rel_bias, buckets)


def _bias_grad(dbias_list, buckets, name):
    nbr = len(dbias_list)

    def body(*refs):
        d_refs, bk_ref, o_ref = refs[:nbr], refs[nbr], refs[nbr + 1]
        lane = lax.broadcasted_iota(jnp.int32, (1, LANES), 1)
        for h in range(N_HEADS):
            def step(bkt, acc):
                tot = jnp.zeros((1, 1), F32)
                for bi in range(nbr):
                    sel = jnp.where(bk_ref[bi] == bkt, d_refs[bi][h], 0.0)
                    tot = tot + jnp.sum(jnp.sum(sel, axis=1, keepdims=True), axis=0, keepdims=True)
                return acc + jnp.where(lane == bkt, tot, 0.0)

            o_ref[h:h + 1, :] = lax.fori_loop(0, N_BUCKETS, step, jnp.zeros((1, LANES), F32))

    band = pl.BlockSpec((N_HEADS, BLK, 2 * BLK), lambda i: (0, 0, 0))
    return pl.pallas_call(
        body, name=name, out_shape=jax.ShapeDtypeStruct((N_HEADS, LANES), F32), grid=(1,),
        in_specs=[band] * nbr + [pl.BlockSpec((nbr, BLK, 2 * BLK), lambda i: (0, 0, 0))],
        out_specs=pl.BlockSpec((N_HEADS, LANES), lambda i: (0, 0)),
        compiler_params=_cparams(("arbitrary",)),
    )(*dbias_list, buckets)


MLA_TQ = 256
MLA_TK = 256


LOG2E = math.log2(math.e)
MLA_C = MLA_SCALE * LOG2E


def _key_le_query(tk, tq):
    return lax.broadcasted_iota(jnp.int32, (tk, tq), 0) <= lax.broadcasted_iota(jnp.int32, (tk, tq), 1)


def _row_mask(shape, hh):
    row = lax.broadcasted_iota(jnp.int32, shape, 0)
    return (row >= hh * HEAD_DIM) & (row < (hh + 1) * HEAD_DIM)


def _host_call(body, comm, *, name, grid, in_specs, out_specs, out_shape, scratch_shapes, args):
    sem = ("arbitrary",) * len(grid)
    if comm is None:
        res = pl.pallas_call(body, name=name, grid=grid, in_specs=in_specs, out_specs=out_specs,
                             out_shape=out_shape, scratch_shapes=scratch_shapes,
                             compiler_params=_cparams(sem))(*args)
        return res, []
    n_in, n_out, n_s, cn = len(in_specs), len(out_specs), len(scratch_shapes), comm.n

    def hosted(*refs):
        ins, refs = refs[:n_in], refs[n_in:]
        c_ins, refs = refs[:cn], refs[cn:]
        outs, refs = refs[:n_out], refs[n_out:]
        c_outs, refs = refs[:cn], refs[cn:]
        scr, c_sems = refs[:n_s], refs[n_s:]
        ids = [pl.program_id(a) for a in range(len(grid))]
        first = functools.reduce(jnp.logical_and, [i == 0 for i in ids])
        last = functools.reduce(jnp.logical_and, [i == g - 1 for i, g in zip(ids, grid)])

        @pl.when(first)
        def _():
            comm.start(c_ins, c_outs, c_sems)

        body(*ins, *outs, *scr)

        @pl.when(last)
        def _():
            comm.finish(c_ins, c_outs, c_sems)

    res = pl.pallas_call(
        hosted, name=name, grid=grid, in_specs=list(in_specs) + _hbm_specs(cn),
        out_specs=list(out_specs) + _hbm_specs(cn), out_shape=list(out_shape) + list(comm.out_shape),
        scratch_shapes=list(scratch_shapes) + list(comm.scratch), compiler_params=_cparams(sem),
    )(*args, *comm.inputs)
    return res[:n_out], res[n_out:]


def _mla_fwd_t(q, k, vt, name, comm=None):
    B, S, _ = q.shape
    tq, tk = MLA_TQ, MLA_TK
    assert tq == tk
    npair = N_HEADS // 2
    nq = S // tq

    def body(q_ref, k_ref, vt_ref, o_ref, lse_ref, acc_s):
        i = pl.program_id(2)
        acc_s[...] = jnp.zeros_like(acc_s)
        qs = [q_ref[0, :, hh * LANES:(hh + 1) * LANES] for hh in range(2)]
        diag = _key_le_query(tk, tq)

        def step(j, ms, masked):
            kj = k_ref[0, pl.ds(pl.multiple_of(j * tk, tk), tk), :]
            vj = vt_ref[0, 0, j]
            out = []
            for hh in range(2):
                s = _dot_nt(kj[:, hh * LANES:(hh + 1) * LANES], qs[hh])
                if masked:
                    s = jnp.where(diag, s, NEG)
                m_new = jnp.maximum(ms[hh], jnp.max(s, axis=0, keepdims=True))
                alpha = jnp.exp2((ms[hh] - m_new) * MLA_C)
                e = jnp.exp2((s - m_new) * MLA_C).astype(BF16)
                vh = jnp.where(_row_mask(vj.shape, hh), vj, jnp.ones_like(vj))
                acc_s[hh] = acc_s[hh] * alpha + _dot_nn(vh, e)
                out.append(m_new)
            return tuple(out)

        m0 = jnp.full((1, tq), NEG, F32)
        ms = lax.fori_loop(0, i, lambda j, c: step(j, c, False), (m0, m0))
        ms = step(i, ms, True)
        rows0 = _row_mask((LANES, tq), 0)
        l0 = acc_s[0, HEAD_DIM:HEAD_DIM + 1, :]
        l1 = acc_s[1, 0:1, :]
        o_ref[0] = jnp.where(rows0, acc_s[0] / l0, acc_s[1] / l1)
        lse_ref[0, 0, 0] = jnp.zeros((8, tq), F32)
        lse_ref[0, 0, 0, 0:1, :] = ms[0] * MLA_C + jnp.log(l0) * LOG2E
        lse_ref[0, 0, 0, 1:2, :] = ms[1] * MLA_C + jnp.log(l1) * LOG2E

    return _host_call(
        body, comm, name=name,
        out_shape=[jax.ShapeDtypeStruct((B, D_B, S), F32), jax.ShapeDtypeStruct((B, npair, nq, 8, tq), F32)],
        grid=(B, npair, nq),
        in_specs=[pl.BlockSpec((1, tq, 2 * LANES), lambda b, p, i: (b, i, p)),
                  pl.BlockSpec((1, S, 2 * LANES), lambda b, p, i: (b, 0, p)),
                  pl.BlockSpec((1, 1, S // tk, LANES, tk), lambda b, p, i: (b, p, 0, 0, 0))],
        out_specs=[pl.BlockSpec((1, LANES, tq), lambda b, p, i: (b, p, i)),
                   pl.BlockSpec((1, 1, 1, 8, tq), lambda b, p, i: (b, p, i, 0, 0))],
        scratch_shapes=[pltpu.VMEM((2, LANES, tq), F32)],
        args=(q, k, vt))


def _mla_delta(do, o, name):
    B, S, _ = o.shape
    tq = MLA_TQ
    npair = N_HEADS // 2

    def body(do_ref, o_ref, d_ref):
        d_ref[...] = jnp.zeros_like(d_ref)
        for p in range(npair):
            sl = slice(p * LANES, (p + 1) * LANES)
            prod_t = jnp.transpose(do_ref[0, :, sl].astype(F32) * o_ref[0, :, sl])
            d_ref[0, p, 0, 0:1, :] = jnp.sum(prod_t[:HEAD_DIM], axis=0, keepdims=True)
            d_ref[0, p, 0, 1:2, :] = jnp.sum(prod_t[HEAD_DIM:], axis=0, keepdims=True)

    tok = pl.BlockSpec((1, tq, D_B), lambda b, i: (b, i, 0))
    return pl.pallas_call(
        body, name=name, out_shape=jax.ShapeDtypeStruct((B, npair, S // tq, 8, tq), F32),
        grid=(B, S // tq), in_specs=[tok, tok],
        out_specs=pl.BlockSpec((1, npair, 1, 8, tq), lambda b, i: (b, 0, i, 0, 0)),
        compiler_params=_cparams(("parallel", "parallel")),
    )(do, o)


def _mla_bwd_t(q, k, v, do, lse, delta, name, comm=None):
    B, S, _ = q.shape
    tq, tk = MLA_TQ, MLA_TK
    assert tq == tk
    npair = N_HEADS // 2
    nq = S // tq

    def body(q_ref, do_ref, lse_ref, dl_ref, k_ref, v_ref, dk_ref, dv_ref, dq_ref, dk_s, dv_s):
        j = pl.program_id(2)

        @pl.when(j == 0)
        def _():
            dq_ref[...] = jnp.zeros_like(dq_ref)

        vj = v_ref[0]
        dv_s[...] = jnp.zeros_like(dv_s)
        diag = _key_le_query(tk, tq)
        for hh in range(2):
            hsl = slice(hh * LANES, (hh + 1) * LANES)
            hm = _head_mask((tq, LANES), hh)
            kh = k_ref[0, :, hsl]
            kt = jnp.transpose(kh.astype(F32)).astype(BF16)
            dk_s[...] = jnp.zeros_like(dk_s)

            def step(i, masked):
                rows = pl.ds(pl.multiple_of(i * tq, tq), tq)
                qi = q_ref[0, rows, hsl]
                dov = do_ref[0, rows, :]
                dom = jnp.where(hm, dov, jnp.zeros_like(dov))
                s = _dot_nt(kh, qi)
                pr = jnp.exp2(s * MLA_C - lse_ref[0, 0, i, hh:hh + 1, :])
                if masked:
                    pr = jnp.where(diag, pr, 0.0)
                dv_s[...] += _dot_nn(pr.astype(BF16), dom)
                ds = (pr * (_dot_nt(vj, dom) - dl_ref[0, 0, i, hh:hh + 1, :])).astype(BF16)
                dk_s[...] += _dot_nn(ds, qi)
                dq_ref[0, 0, i, hsl, :] += _dot_nn(kt, ds) * MLA_SCALE

            step(j, True)

            def loop_body(i, carry):
                step(i, False)
                return carry

            lax.fori_loop(j + 1, nq, loop_body, 0)
            dk_ref[0, :, hsl] = dk_s[...] * MLA_SCALE
        dv_ref[0] = dv_s[...]

    stat = pl.BlockSpec((1, 1, nq, 8, tq), lambda b, p, j: (b, p, 0, 0, 0))
    return _host_call(
        body, comm, name=name,
        out_shape=[jax.ShapeDtypeStruct((B, S, N_HEADS * LANES), F32), jax.ShapeDtypeStruct((B, S, D_B), F32),
                   jax.ShapeDtypeStruct((B, npair, nq, 2 * LANES, tq), F32)],
        grid=(B, npair, S // tk),
        in_specs=[pl.BlockSpec((1, S, 2 * LANES), lambda b, p, j: (b, 0, p)),
                  pl.BlockSpec((1, S, LANES), lambda b, p, j: (b, 0, p)),
                  stat, stat,
                  pl.BlockSpec((1, tk, 2 * LANES), lambda b, p, j: (b, j, p)),
                  pl.BlockSpec((1, tk, LANES), lambda b, p, j: (b, j, p))],
        out_specs=[pl.BlockSpec((1, tk, 2 * LANES), lambda b, p, j: (b, j, p)),
                   pl.BlockSpec((1, tk, LANES), lambda b, p, j: (b, j, p)),
                   pl.BlockSpec((1, 1, nq, 2 * LANES, tq), lambda b, p, j: (b, p, 0, 0, 0))],
        scratch_shapes=[pltpu.VMEM((tk, LANES), F32), pltpu.VMEM((tk, LANES), F32)],
        args=(q, do, lse, delta, k, v))


def _local_step(x, target, mod, wts, gains, rel_bias, ffn_shards=None):
    B, S, D = x.shape
    T = B * S
    sh1, sc1, g1, sh2, sc2, g2 = [mod[:, i * D:(i + 1) * D].reshape(B, 1, D) for i in range(N_MOD)]
    cs, sn = _rope_tables()
    buckets = np.stack([_band_buckets(d) for d in DILATIONS])
    buckets_dev = jnp.asarray(buckets)
    bias = _bias_tables(rel_bias, buckets_dev, "rel_bias_tables")
    w_in = wts["w_in"]

    h1 = _adaln_fwd(x, gains["g_norm1"], sc1, sh1, "adaln1_fwd")
    h1f = h1.reshape(T, D)
    qkv = _mm(h1f, w_in[:, :P_QKV], "nn", BF16, "mm_qkv").reshape(B, S, P_QKV)
    rest = _mm(h1f, w_in[:, P_QKV:], "nn", F32, "mm_rest")
    o_d, lse_d = [], []
    for i, d in enumerate(DILATIONS):
        o_i, lse_i = _dil_fwd(qkv, bias, i, d, f"dil_fwd_{d}")
        o_d.append(o_i)
        lse_d.append(lse_i)
    out_a, lse_a = _dil_merge(o_d, lse_d, "dil_merge")
    cqn = _rms_fwd(rest, 1, Q_LORA, gains["g_cq"], "rms_cq_fwd")
    ckvn = _rms_fwd(rest, 0, KV_LORA, gains["g_ckv"], "rms_ckv_fwd")
    rest3 = rest.reshape(B, S, P_REST)
    q_raw = _mm(cqn, wts["w_uq"], "nn", F32, "mm_uq").reshape(B, S, N_HEADS * LANES)
    qc = _rope_apply(q_raw, cs, sn, BF16, "rope_q")
    kn_raw = _mm(ckvn, wts["w_kv"][:, :N_HEADS * LANES], "nn", F32, "mm_uk").reshape(B, S, N_HEADS * LANES)
    kc = _rope_apply(kn_raw, cs, sn, BF16, "rope_k", add=rest3, add_blk=KV_LORA // LANES)
    v = _mm(ckvn, wts["w_kv"][:, N_HEADS * LANES:], "nn", BF16, "mm_uv").reshape(B, S, D_B)
    vt = jnp.transpose(v.reshape(B, S // MLA_TK, MLA_TK, N_HEADS // 2, LANES), (0, 3, 1, 4, 2))
    (o_t, lse_b), got = _mla_fwd_t(qc, kc, vt, "mla_fwd", _GatherComm(ffn_shards) if ffn_shards else None)
    if ffn_shards:
        wts = dict(wts, w_ffn_in=got[0].reshape(N_CHIP, D, -1), w_ffn_out=got[1].reshape(D_FF, D))
    out_b = jnp.transpose(o_t, (0, 2, 1))
    out_af, out_bf = out_a.reshape(T, D_A), out_b.reshape(T, D_B)
    ya = _rms_fwd(out_af, 0, D_A, gains["g_out_a"], "rms_outa_fwd")
    yb = _rms_fwd(out_bf, 0, D_B, gains["g_out_b"], "rms_outb_fwd")
    y = jnp.concatenate([ya, yb], axis=1)
    mix = _mm(y, wts["w_out"], "nn", F32, "mm_out").reshape(B, S, D)
    h2, x1 = _adaln_fwd(x, gains["g_norm2"], sc2, sh2, "adaln2_fwd", mix=mix, gate=g1)
    h2f = h2.reshape(T, D)
    gu = _mm(h2f, wts["w_ffn_in"], "nn", F32, "mm_ffn_in", col_blocks=N_CHIP)
    act = _swiglu_fwd(gu, "swiglu_fwd")
    f = _mm(act, wts["w_ffn_out"], "nn", F32, "mm_ffn_out").reshape(B, S, D)
    dx2, df, dg2, dg_final, loss = _final_loss(x1, f, g2, gains["g_final"], target, "final_loss")

    dff = df.reshape(T, D)
    da = _mm(dff, wts["w_ffn_out"], "nt", F32, "mm_ffn_out_dx")
    gw_ffn_out = _mm(act, dff, "tn", F32, "mm_ffn_out_dw")
    dgu = _swiglu_bwd(da, gu, "swiglu_bwd")
    dh2 = _mm(dgu, wts["w_ffn_in"], "nt", F32, "mm_ffn_in_dx", col_blocks=N_CHIP).reshape(B, S, D)
    gw_ffn_in = _mm(h2f, dgu, "tn", F32, "mm_ffn_in_dw", col_blocks=N_CHIP)
    dx1, dsh2, dsc2, dg_norm2, dg1, dmix = _adaln_bwd(dh2, x1, gains["g_norm2"], sc2, dx2, "adaln2_bwd",
                                                      mix=mix, gate=g1)
    dmixf = dmix.reshape(T, D)
    dy = _mm(dmixf, wts["w_out"], "nt", F32, "mm_out_dx")
    gw_out = _mm(y, dmixf, "tn", F32, "mm_out_dw")
    do_a, dg_out_a = _rms_bwd(dy, 0, out_af, 0, D_A, gains["g_out_a"], "rms_outa_bwd")
    do_b, dg_out_b = _rms_bwd(dy, 1, out_bf, 0, D_B, gains["g_out_b"], "rms_outb_bwd")
    do_b3 = do_b.reshape(B, S, D_B)
    delta_b = _mla_delta(do_b3, out_b, "mla_delta")
    ffn_a4 = None
    if ffn_shards:
        ffn_a4 = _rs_first([gw_ffn_in.reshape(N_DEV, -1, gw_ffn_in.shape[-1]), gw_ffn_out.reshape(N_DEV, -1, D)],
                           "ffn")
    (dkc, dv, dq_t), ffn_r2 = _mla_bwd_t(qc, kc, v, do_b3, lse_b, delta_b, "mla_bwd",
                                         _ToChipsComm(ffn_a4) if ffn_shards else None)
    dqc = jnp.transpose(dq_t, (0, 2, 4, 1, 3)).reshape(B, S, N_HEADS * LANES)
    dq_raw = _rope_apply(dqc, cs, -sn, BF16, "rope_q_bwd").reshape(T, N_HEADS * LANES)
    dkrw = _krope_bwd(dkc, cs, -sn, "rope_k_bwd").reshape(T, LANES)
    dcqn = _mm(dq_raw, wts["w_uq"], "nt", F32, "mm_uq_dx")
    gw_uq = _mm(cqn, dq_raw, "tn", F32, "mm_uq_dw")
    dkv = jnp.concatenate([dkc.reshape(T, -1), dv.reshape(T, -1)], axis=1).astype(BF16)
    dckvn = _mm(dkv, wts["w_kv"], "nt", F32, "mm_ukv_dx")
    gw_kv = _mm(ckvn, dkv, "tn", F32, "mm_ukv_dw")
    dcq, dg_cq = _rms_bwd(dcqn, 0, rest, 1, Q_LORA, gains["g_cq"], "rms_cq_bwd")
    dckv, dg_ckv = _rms_bwd(dckvn, 0, rest, 0, KV_LORA, gains["g_ckv"], "rms_ckv_bwd")
    do_a3 = do_a.reshape(B, S, D_A)
    dqkv_d, dbias_d = [], []
    for i, d in enumerate(DILATIONS):
        dqkv_i, dbias_i = _dil_bwd(qkv, do_a3, out_a, lse_a, bias, i, d, f"dil_bwd_{d}")
        dqkv_d.append(dqkv_i)
        dbias_d.append(dbias_i)
    dqkv = _sum3_bf16(*dqkv_d, "dil_bwd_sum").reshape(T, P_QKV)
    g_rel_bias = _bias_grad(dbias_d, buckets_dev, "rel_bias_grad")[:, :N_BUCKETS].T
    dproj = jnp.concatenate([dqkv, dckv, dkrw, dcq], axis=1)
    dh1 = _mm(dproj, w_in, "nt", F32, "mm_in_dx").reshape(B, S, D)
    gw_in = _mm(h1f, dproj, "tn", F32, "mm_in_dw")
    grad_x, dsh1, dsc1, dg_norm1 = _adaln_bwd(dh1, x, gains["g_norm1"], sc1, dx1, "adaln1_bwd")
    gmod = jnp.concatenate([dsh1, dsc1, dg1, dsh2, dsc2, dg2], axis=-1).reshape(B, N_MOD * D)
    grads = dict(w_in=gw_in, w_uq=gw_uq, w_kv=gw_kv, w_out=gw_out, w_ffn_in=gw_ffn_in, w_ffn_out=gw_ffn_out,
                 g_norm1=dg_norm1, g_cq=dg_cq, g_ckv=dg_ckv, rel_bias=g_rel_bias, g_out_a=dg_out_a,
                 g_out_b=dg_out_b, g_norm2=dg_norm2, g_final=dg_final, ffn_pending=(ffn_a4, ffn_r2))
    return loss, grad_x, gmod, grads


def _w_in_to_kernel(w):
    z = lambda n: jnp.zeros((w.shape[0], n), w.dtype)
    i3, i4, i5 = 3 * D_A, 3 * D_A + Q_LORA, 3 * D_A + Q_LORA + KV_LORA
    return jnp.concatenate([w[:, :i3], w[:, i4:i5], z(NOPE_DIM), w[:, i5:], z(LANES - NOPE_DIM - ROPE_DIM),
                            w[:, i3:i4]], axis=1)


def _w_in_from_kernel(g):
    o = P_QKV + KV_LORA
    return jnp.concatenate([g[:, :P_QKV], g[:, o + LANES:], g[:, P_QKV:o],
                            g[:, o + NOPE_DIM:o + NOPE_DIM + ROPE_DIM]], axis=1)


def _w_uq_to_kernel(w):
    w3 = w.reshape(Q_LORA, N_HEADS, NOPE_DIM + ROPE_DIM)
    return jnp.pad(w3, ((0, 0), (0, 0), (0, LANES - NOPE_DIM - ROPE_DIM))).reshape(Q_LORA, N_HEADS * LANES)


def _w_uq_from_kernel(g):
    return g.reshape(Q_LORA, N_HEADS, LANES)[:, :, :NOPE_DIM + ROPE_DIM].reshape(Q_LORA, -1)


def _w_ukv_to_kernel(w):
    w3 = w.reshape(KV_LORA, N_HEADS, 2 * HEAD_DIM)
    wk = jnp.pad(w3[:, :, :NOPE_DIM], ((0, 0), (0, 0), (0, LANES - NOPE_DIM))).reshape(KV_LORA, N_HEADS * LANES)
    wv = w3[:, :, NOPE_DIM:].reshape(KV_LORA, D_B)
    return jnp.concatenate([wk, wv], axis=1)


def _w_ukv_from_kernel(g):
    gk = g[:, :N_HEADS * LANES].reshape(KV_LORA, N_HEADS, LANES)[:, :, :NOPE_DIM]
    gv = g[:, N_HEADS * LANES:].reshape(KV_LORA, N_HEADS, HEAD_DIM)
    return jnp.concatenate([gk, gv], axis=2).reshape(KV_LORA, -1)


MESH = pl.DeviceIdType.MESH


def _my_place():
    return lax.axis_index("x"), lax.axis_index("y"), lax.axis_index("c")


def _other_chips(x, y):
    return [(1 - x, y), (x, 1 - y), (1 - x, 1 - y)]


def _allgather8(x_shard, name, in_hbm):
    m_per, n = x_shard.shape
    space = pl.ANY if in_hbm else pltpu.VMEM

    def body(x_ref, out_ref, send_sems, recv_sems, local_sem):
        x, y, c = _my_place()
        me, sibling = (x, y, c), (x, y, 1 - c)
        chips = _other_chips(x, y)

        def rows(px, py, pc):
            return out_ref.at[pl.ds((4 * px + 2 * py + pc) * m_per, m_per), :]

        def copy(k, block, to, src=None):
            return pltpu.make_async_remote_copy(
                src_ref=rows(*block) if src is None else src, dst_ref=rows(*block),
                send_sem=send_sems.at[k], recv_sem=recv_sems.at[k], device_id=to, device_id_type=MESH)

        mine = pltpu.make_async_copy(x_ref, rows(*me), local_sem)
        mine.start()
        first = [copy(0, me, sibling, src=x_ref)]
        first += [copy(1 + j, me, (*chip, c), src=x_ref) for j, chip in enumerate(chips)]
        for cp in first:
            cp.start()
        passed = [copy(4 + j, (*chip, c), sibling) for j, chip in enumerate(chips)]
        for j, chip in enumerate(chips):
            copy(1 + j, (*chip, c), me).wait_recv()
            passed[j].start()
        copy(0, sibling, me).wait_recv()
        for j, chip in enumerate(chips):
            copy(4 + j, (*chip, 1 - c), me).wait_recv()
        for cp in first + passed:
            cp.wait_send()
        mine.wait()

    return pl.pallas_call(
        body, name=name,
        out_shape=jax.ShapeDtypeStruct((N_DEV * m_per, n), x_shard.dtype),
        in_specs=[pl.BlockSpec(memory_space=space)],
        out_specs=pl.BlockSpec(memory_space=space),
        scratch_shapes=[pltpu.SemaphoreType.DMA((7,)), pltpu.SemaphoreType.DMA((7,)), pltpu.SemaphoreType.DMA],
        compiler_params=pltpu.CompilerParams(vmem_limit_bytes=VMEM_LIMIT),
    )(x_shard)


def _hbm_specs(n):
    return [pl.BlockSpec(memory_space=pl.ANY)] * n


class _GatherComm:
    def __init__(self, shards):
        self.n = n = len(shards)
        self.inputs = [s.reshape(2, s.shape[0] // 2, s.shape[1]) for s in shards]
        self.out_shape = [jax.ShapeDtypeStruct((N_DEV,) + s.shape[1:], s.dtype) for s in self.inputs]
        self.scratch = [pltpu.SemaphoreType.DMA((7 * n,)), pltpu.SemaphoreType.DMA((7 * n,))]

    def _parts(self, xs, outs, sems):
        send_sems, recv_sems = sems
        x, y, c = _my_place()

        def blk(k, px, py, pc):
            return outs[k].at[4 * px + 2 * py + pc]

        def copy(k, kind, block, to, own=False):
            return pltpu.make_async_remote_copy(
                src_ref=xs[k].at[c] if own else blk(k, *block), dst_ref=blk(k, *block),
                send_sem=send_sems.at[7 * k + kind], recv_sem=recv_sems.at[7 * k + kind],
                device_id=to, device_id_type=MESH)

        def whole(k):
            return pltpu.make_async_remote_copy(
                src_ref=xs[k], dst_ref=outs[k].at[pl.ds(4 * x + 2 * y, 2)],
                send_sem=send_sems.at[7 * k], recv_sem=recv_sems.at[7 * k],
                device_id=(x, y, 1 - c), device_id_type=MESH)

        me, sibling = (x, y, c), (x, y, 1 - c)
        chips = _other_chips(x, y)
        first = []
        for k in range(self.n):
            first.append(whole(k))
            first += [copy(k, 1 + j, me, (*chip, c), own=True) for j, chip in enumerate(chips)]
        return copy, whole, me, sibling, chips, c, first

    def start(self, xs, outs, sems):
        for cp in self._parts(xs, outs, sems)[-1]:
            cp.start()

    def finish(self, xs, outs, sems):
        copy, whole, me, sibling, chips, c, first = self._parts(xs, outs, sems)
        passed = []
        for j, chip in enumerate(chips):
            for k in range(self.n):
                copy(k, 1 + j, (*chip, c), me).wait_recv()
                fwd = copy(k, 4 + j, (*chip, c), sibling)
                fwd.start()
                passed.append(fwd)
        for k in range(self.n):
            whole(k).wait_recv()
        for j, chip in enumerate(chips):
            for k in range(self.n):
                copy(k, 4 + j, (*chip, 1 - c), me).wait_recv()
        for cp in first + passed:
            cp.wait_send()


class _ToChipsComm:
    def __init__(self, a4s):
        self.inputs = list(a4s)
        self.n = n = len(a4s)
        nc = N_CHIP - 1
        self.out_shape = [jax.ShapeDtypeStruct((nc,) + a.shape[1:], a.dtype) for a in a4s]
        self.scratch = [pltpu.SemaphoreType.DMA((nc * n,)), pltpu.SemaphoreType.DMA((nc * n,))]

    def _copies(self, as_, rs, sems):
        send_sems, recv_sems = sems
        x, y, c = _my_place()
        nc = N_CHIP - 1
        return [pltpu.make_async_remote_copy(
            src_ref=as_[k].at[2 * cx + cy], dst_ref=rs[k].at[j], send_sem=send_sems.at[nc * k + j],
            recv_sem=recv_sems.at[nc * k + j], device_id=(cx, cy, c), device_id_type=MESH)
            for k in range(self.n) for j, (cx, cy) in enumerate(_other_chips(x, y))]

    def start(self, as_, rs, sems):
        for cp in self._copies(as_, rs, sems):
            cp.start()

    def finish(self, as_, rs, sems):
        for cp in self._copies(as_, rs, sems):
            cp.wait()


def _run_comm(comm, name):
    n = comm.n

    def body(*refs):
        ins, outs, sems = refs[:n], refs[n:2 * n], refs[2 * n:]
        comm.start(ins, outs, sems)
        comm.finish(ins, outs, sems)

    return pl.pallas_call(
        body, name=name, out_shape=comm.out_shape, in_specs=_hbm_specs(n), out_specs=_hbm_specs(n),
        scratch_shapes=comm.scratch,
    )(*comm.inputs)


def _gather_weights(shards, name):
    return _run_comm(_GatherComm(shards), name)


def _rs_to_sibling(g8s, name):
    n = len(g8s)

    def body(*refs):
        gs, rs = refs[:n], refs[n:2 * n]
        send_sems, recv_sems = refs[2 * n:]
        x, y, c = _my_place()
        copies = [pltpu.make_async_remote_copy(
            src_ref=gs[k].at[2 * s + 1 - c], dst_ref=rs[k].at[s], send_sem=send_sems.at[N_CHIP * k + s],
            recv_sem=recv_sems.at[N_CHIP * k + s], device_id=(x, y, 1 - c), device_id_type=MESH)
            for k in range(n) for s in range(N_CHIP)]
        for cp in copies:
            cp.start()
        for cp in copies:
            cp.wait()

    return pl.pallas_call(
        body, name=name,
        out_shape=[jax.ShapeDtypeStruct((N_CHIP,) + g.shape[1:], g.dtype) for g in g8s],
        in_specs=_hbm_specs(n), out_specs=_hbm_specs(n),
        scratch_shapes=[pltpu.SemaphoreType.DMA((N_CHIP * n,)), pltpu.SemaphoreType.DMA((N_CHIP * n,))],
    )(*g8s)


def _rs_to_chips(a4s, name):
    return _run_comm(_ToChipsComm(a4s), name)


def _swap_halves(hs, name):
    n = len(hs)

    def body(*refs):
        o_refs = refs[n:2 * n]
        send_sems, recv_sems = refs[2 * n:]
        x, y, c = _my_place()

        def remote(k, slot):
            return pltpu.make_async_remote_copy(
                src_ref=o_refs[k].at[slot], dst_ref=o_refs[k].at[slot], send_sem=send_sems.at[k],
                recv_sem=recv_sems.at[k], device_id=(x, y, 1 - c), device_id_type=MESH)

        sends = [remote(k, c) for k in range(n)]
        for cp in sends:
            cp.start()
        for k in range(n):
            remote(k, 1 - c).wait_recv()
        for cp in sends:
            cp.wait_send()

    return pl.pallas_call(
        body, name=name,
        out_shape=[jax.ShapeDtypeStruct(h.shape, h.dtype) for h in hs],
        in_specs=_hbm_specs(n), out_specs=_hbm_specs(n),
        input_output_aliases={k: k for k in range(n)},
        scratch_shapes=[pltpu.SemaphoreType.DMA((n,)), pltpu.SemaphoreType.DMA((n,))],
    )(*hs)


ADD_TILES = 4


def _add_blocks(a_list, a_idx_fn, others_list, ns, sel, name, out_blocks=None, out_idx_fn=None):
    out_blocks = out_blocks or ns
    out_idx_fn = out_idx_fn or (lambda s, sel_ref: s)
    n = len(a_list)
    n_o = len(others_list[0])
    per = 1 + n_o

    def body(sel_ref, *refs):
        for k in range(n):
            ins = refs[k * per:(k + 1) * per]
            o_ref = refs[n * per + k]
            acc = ins[0][0]
            for r in ins[1:]:
                acc = acc + r[0]
            o_ref[0] = acc

    in_specs, args, out_specs, out_shape = [], [], [], []
    for a, others in zip(a_list, others_list):
        _, R, N = a.shape
        tr = R // ADD_TILES
        assert tr % 8 == 0, a.shape
        in_specs.append(pl.BlockSpec((1, tr, N), lambda s, i, sel_ref: (a_idx_fn(s, sel_ref), i, 0)))
        args.append(a)
        for arr, fixed in others:
            if fixed is None:
                in_specs.append(pl.BlockSpec((1, tr, N), lambda s, i, sel_ref: (s, i, 0)))
            else:
                in_specs.append(pl.BlockSpec((1, tr, N), lambda s, i, sel_ref, fixed=fixed: (fixed, i, 0)))
            args.append(arr)
        out_specs.append(pl.BlockSpec((1, tr, N), lambda s, i, sel_ref: (out_idx_fn(s, sel_ref), i, 0)))
        out_shape.append(jax.ShapeDtypeStruct((out_blocks, R, N), a.dtype))
    grid_spec = pltpu.PrefetchScalarGridSpec(num_scalar_prefetch=1, grid=(ns, ADD_TILES), in_specs=in_specs,
                                             out_specs=out_specs)
    return pl.pallas_call(
        body, name=name, out_shape=out_shape, grid_spec=grid_spec,
        compiler_params=_cparams(("parallel", "parallel")),
    )(sel, *args)


def _rs_first(g8s, tag):
    c_sel = jnp.reshape(lax.axis_index("c"), (1,)).astype(jnp.int32)
    r1 = _rs_to_sibling(g8s, f"rs_to_sibling_{tag}")
    return _add_blocks(g8s, lambda s, sel: 2 * s + sel[0], [[(r, None)] for r in r1], N_CHIP, c_sel,
                       f"rs_add_sibling_{tag}")


def _rs_last(a4s, r2s, tag):
    sel = jnp.stack([2 * lax.axis_index("x") + lax.axis_index("y"), lax.axis_index("c")]).astype(jnp.int32)
    h = _add_blocks(a4s, lambda s, sel: sel[0], [[(r, 0), (r, 1), (r, 2)] for r in r2s], 1, sel,
                    f"rs_add_chips_{tag}", out_blocks=2, out_idx_fn=lambda s, sel: sel[1])
    full = _swap_halves(h, f"rs_swap_halves_{tag}")
    return [f.reshape(2 * f.shape[1], f.shape[2]) for f in full]


def _reduce_scatter(g8s, tag):
    a4 = _rs_first(g8s, tag)
    return _rs_last(a4, _rs_to_chips(a4, f"rs_to_chips_{tag}"), tag)


def _ada_fwd(c_all, w_ada, b_ada, name):
    nb, D = c_all.shape
    ncol = w_ada.shape[1]
    tc = 512

    def body(c_ref, w_ref, b_ref, o_ref):
        cv = c_ref[...]
        cond = (cv * jax.nn.sigmoid(cv)).astype(BF16)
        o_ref[...] = jnp.dot(cond, w_ref[...].astype(BF16), preferred_element_type=F32) + b_ref[...]

    return pl.pallas_call(
        body, name=name, out_shape=jax.ShapeDtypeStruct((nb, ncol), F32), grid=(ncol // tc,),
        in_specs=[pl.BlockSpec((nb, D), lambda j: (0, 0)), pl.BlockSpec((D, tc), lambda j: (0, j)),
                  pl.BlockSpec((1, tc), lambda j: (0, j))],
        out_specs=pl.BlockSpec((nb, tc), lambda j: (0, j)),
        compiler_params=_cparams(("parallel",)),
    )(c_all, w_ada, b_ada)


def _ada_bwd(c_all, gmod_cols, name):
    nb, D = c_all.shape
    ncol = gmod_cols.shape[1]
    tc = 512

    def body(c_ref, g_ref, o_ref):
        cv = c_ref[...]
        cond = (cv * jax.nn.sigmoid(cv)).astype(BF16)
        o_ref[...] = _dot_tn(cond, g_ref[...].astype(BF16))

    return pl.pallas_call(
        body, name=name, out_shape=jax.ShapeDtypeStruct((D, ncol), F32), grid=(ncol // tc,),
        in_specs=[pl.BlockSpec((nb, D), lambda j: (0, 0)), pl.BlockSpec((nb, tc), lambda j: (0, j))],
        out_specs=pl.BlockSpec((D, tc), lambda j: (0, j)),
        compiler_params=_cparams(("parallel",)),
    )(c_all, gmod_cols)


def _adam_math(w, g, m, v):
    m = ADAM_B1 * m + (1.0 - ADAM_B1) * g
    v = ADAM_B2 * v + (1.0 - ADAM_B2) * (g * g)
    m_hat = m / (1.0 - ADAM_B1 ** ADAM_STEP)
    v_hat = v / (1.0 - ADAM_B2 ** ADAM_STEP)
    delta = -ADAM_LR * (m_hat / (jnp.sqrt(v_hat) + ADAM_EPS) + ADAM_WD * w)
    return delta, m, v


def _adamw(w, g, m, v, name):
    rows, cols = w.shape
    tr = _pick(rows, (256, 192, 176, 128, 64, 8))

    def body(w_ref, g_ref, m_ref, v_ref, d_ref, mo_ref, vo_ref):
        d, mn, vn = _adam_math(w_ref[...], g_ref[...], m_ref[...], v_ref[...])
        d_ref[...] = d
        mo_ref[...] = mn
        vo_ref[...] = vn

    spec = pl.BlockSpec((tr, cols), lambda i: (i, 0))
    return pl.pallas_call(
        body, name=name, out_shape=[jax.ShapeDtypeStruct((rows, cols), F32)] * 3, grid=(rows // tr,),
        in_specs=[spec] * 4, out_specs=[spec] * 3, compiler_params=_cparams(("parallel",)),
    )(w, g, m, v)


VEC_ROWS = 8


def _adamw_rows(w, parts, m, v, name):
    n = w.shape[1]
    P = parts.shape[0]
    assert n % (VEC_ROWS * LANES) == 0, n
    shp = (VEC_ROWS, n // VEC_ROWS)

    def body(w_ref, p_ref, m_ref, v_ref, g_ref, d_ref, mo_ref, vo_ref):
        g = p_ref[0]
        for k in range(1, P):
            g = g + p_ref[k]
        d, mn, vn = _adam_math(w_ref[...], g, m_ref[...], v_ref[...])
        g_ref[...] = g
        d_ref[...] = d
        mo_ref[...] = mn
        vo_ref[...] = vn

    vec = pl.BlockSpec(shp, lambda i: (0, 0))
    out = pl.pallas_call(
        body, name=name, out_shape=[jax.ShapeDtypeStruct(shp, F32)] * 4, grid=(1,),
        in_specs=[vec, pl.BlockSpec((P,) + shp, lambda i: (0, 0, 0)), vec, vec], out_specs=[vec] * 4,
        compiler_params=_cparams(("arbitrary",)),
    )(w.reshape(shp), parts.reshape((P,) + shp), m.reshape(shp), v.reshape(shp))
    return [o.reshape(1, n) for o in out]


_PACKED = (("w_in", 1024, 552), ("w_uq", 384, 192), ("w_ukv", 256, 256))
_SHARDED = ("w_in", "w_uq", "w_ukv", "w_out", "w_ffn_in", "w_ffn_out")
_SMALL = (("g_norm1", 1024), ("g_cq", 384), ("g_ckv", 256), ("rel_bias", 256), ("g_out_a", 512),
          ("g_out_b", 512), ("g_norm2", 1024), ("g_final", 1024))
_SMALL_PAD = 5120
PACK_ROWS = 704
_PACK_ELEMS = PACK_ROWS * D_MODEL


def _pack_shards(shards, dtype):
    lead = shards["w_in"].shape[:-2]
    flat = jnp.concatenate([shards[n].astype(dtype).reshape(lead + (-1,)) for n, _, _ in _PACKED], axis=-1)
    pad = [(0, 0)] * len(lead) + [(0, _PACK_ELEMS - flat.shape[-1])]
    return jnp.pad(flat, pad).reshape(lead + (PACK_ROWS, D_MODEL))


def _unpack_shards(packed):
    out, off = {}, 0
    for n, r, c in _PACKED:
        out[n] = packed[..., off:off + r * c].reshape(packed.shape[:-1] + (r, c))
        off += r * c
    return out


def _full_from_shards(sh):
    return jnp.transpose(sh, (1, 0, 2)).reshape(sh.shape[1], -1)


def _shards_from_full(full):
    rows, cols = full.shape
    return jnp.transpose(full.reshape(rows, N_CHIP, cols // N_CHIP), (1, 0, 2))


def kernel(x, c, w_ada, b_ada, g_norm1, w_in, g_cq, w_uq, g_ckv, w_ukv, rel_bias, g_out_a, g_out_b, w_out, g_norm2, w_ffn_in, w_ffn_out, g_final, loss_target, m_w_ada, m_b_ada, m_g_norm1, m_w_in, m_g_cq, m_w_uq, m_g_ckv, m_w_ukv, m_rel_bias, m_g_out_a, m_g_out_b, m_w_out, m_g_norm2, m_w_ffn_in, m_w_ffn_out, m_g_final, v_w_ada, v_b_ada, v_g_norm1, v_w_in, v_g_cq, v_w_uq, v_g_ckv, v_w_ukv, v_rel_bias, v_g_out_a, v_g_out_b, v_w_out, v_g_norm2, v_w_ffn_in, v_w_ffn_out, v_g_final):
    names = ["w_ada", "b_ada", "g_norm1", "w_in", "g_cq", "w_uq", "g_ckv", "w_ukv", "rel_bias", "g_out_a",
             "g_out_b", "w_out", "g_norm2", "w_ffn_in", "w_ffn_out", "g_final"]
    W = dict(zip(names, [w_ada, b_ada, g_norm1, w_in, g_cq, w_uq, g_ckv, w_ukv, rel_bias, g_out_a, g_out_b,
                         w_out, g_norm2, w_ffn_in, w_ffn_out, g_final]))
    M = dict(zip(names, [m_w_ada, m_b_ada, m_g_norm1, m_w_in, m_g_cq, m_w_uq, m_g_ckv, m_w_ukv, m_rel_bias,
                         m_g_out_a, m_g_out_b, m_w_out, m_g_norm2, m_w_ffn_in, m_w_ffn_out, m_g_final]))
    V = dict(zip(names, [v_w_ada, v_b_ada, v_g_norm1, v_w_in, v_g_cq, v_w_uq, v_g_ckv, v_w_ukv, v_rel_bias,
                         v_g_out_a, v_g_out_b, v_w_out, v_g_norm2, v_w_ffn_in, v_w_ffn_out, v_g_final]))
    B, S, D = x.shape
    mx, my, mc = _my_place()
    dev = 4 * mx + 2 * my + mc
    chip = 2 * mx + my
    pad_rows = 8

    c_all = _allgather8(jnp.pad(c, ((0, pad_rows - B), (0, 0))), "ag_c", False)
    c_all = c_all.reshape(N_DEV, pad_rows, D)[:, :B].reshape(N_DEV * B, D)
    ada_cols = w_ada.shape[-1]
    b_cols = lax.dynamic_slice_in_dim(b_ada, chip * ada_cols, ada_cols, axis=1)
    mod_cols = _ada_fwd(c_all, w_ada[0], b_cols, "ada_fwd")
    mod_all = _allgather8(mod_cols, "ag_mod", False).reshape(N_DEV, N_DEV * B, ada_cols)[0::2]
    mod_all = jnp.transpose(mod_all, (1, 0, 2)).reshape(N_DEV * B, N_MOD * D)
    mod = lax.dynamic_slice_in_dim(mod_all, dev * B, B, axis=0)

    packed = _pack_shards({n: W[n][0] for n, _, _ in _PACKED}, BF16)
    g_packed, g_out = _gather_weights([packed, w_out[0].astype(BF16)], "ag_weights")
    full = {n: _full_from_shards(sh) for n, sh in _unpack_shards(g_packed.reshape(N_CHIP, _PACK_ELEMS)).items()}
    wts = dict(w_in=_w_in_to_kernel(full["w_in"]), w_uq=_w_uq_to_kernel(full["w_uq"]),
               w_kv=_w_ukv_to_kernel(full["w_ukv"]), w_out=g_out.reshape(D, D))
    gains = dict(g_norm1=g_norm1, g_cq=g_cq, g_ckv=g_ckv, g_out_a=g_out_a, g_out_b=g_out_b, g_norm2=g_norm2,
                 g_final=g_final.reshape(1, D))

    loss, grad_x, gmod, grads = _local_step(x, loss_target, mod, wts, gains, rel_bias,
                                            ffn_shards=[w_ffn_in[0].astype(BF16), w_ffn_out[0].astype(BF16)])
    loss = lax.psum(loss[0, 0], ("x", "y", "c"))

    n_small = _SMALL_PAD
    cat = lambda dct: jnp.concatenate([dct[n].reshape(1, -1) for n, _ in _SMALL]
                                      + [jnp.zeros((1, _SMALL_PAD - sum(s for _, s in _SMALL)), F32)], axis=1)
    small = cat(grads)
    rows = jnp.concatenate([gmod, jnp.pad(small, ((0, 0), (0, N_MOD * D - n_small))),
                            jnp.zeros((pad_rows - B - 1, N_MOD * D), F32)], axis=0)
    rows_all = _allgather8(rows, "ag_small", False).reshape(N_DEV, pad_rows, N_MOD * D)
    gmod_all = rows_all[:, :B].reshape(N_DEV * B, N_MOD * D)
    small_parts = rows_all[:, B, :n_small]

    nat = dict(w_in=_w_in_from_kernel(grads["w_in"]), w_uq=_w_uq_from_kernel(grads["w_uq"]),
               w_ukv=_w_ukv_from_kernel(grads["w_kv"]))
    gp = _pack_shards({n: _shards_from_full(nat[n]) for n, _, _ in _PACKED}, F32)
    as_halves = lambda a: a.reshape(N_DEV, -1, a.shape[-1])
    a4 = _rs_first([as_halves(gp), as_halves(grads["w_out"])], "mix")
    r2 = _rs_to_chips(a4, "rs_to_chips_mix")
    ffn_a4, ffn_r2 = grads["ffn_pending"]
    r_packed, r_out, r_ffn_in, r_ffn_out = _rs_last(list(a4) + list(ffn_a4), list(r2) + list(ffn_r2), "all")
    G = _unpack_shards(r_packed.reshape(_PACK_ELEMS))
    G.update(w_out=r_out, w_ffn_in=r_ffn_in, w_ffn_out=r_ffn_out)

    gmod_cols = lax.dynamic_slice_in_dim(gmod_all, chip * ada_cols, ada_cols, axis=1)
    G["w_ada"] = _ada_bwd(c_all, gmod_cols, "ada_bwd")
    delta, new_m, new_v = {}, {}, {}
    for n in ("w_ada",) + _SHARDED:
        shp = W[n].shape
        w2 = W[n].reshape(shp[-2], shp[-1])
        d_, m_, v_ = _adamw(w2, G[n], M[n].reshape(w2.shape), V[n].reshape(w2.shape), f"adamw_{n}")
        G[n], delta[n], new_m[n], new_v[n] = [a.reshape(shp) for a in (G[n], d_, m_, v_)]
    gs, ds_, ms_, vs_ = _adamw_rows(cat(W), small_parts, cat(M), cat(V), "adamw_small")
    off = 0
    for n, sz in _SMALL:
        shp = W[n].shape
        G[n], delta[n], new_m[n], new_v[n] = [a[:, off:off + sz].reshape(shp) for a in (gs, ds_, ms_, vs_)]
        off += sz
    G["b_ada"], delta["b_ada"], new_m["b_ada"], new_v["b_ada"] = _adamw_rows(b_ada, gmod_all, m_b_ada, v_b_ada,
                                                                          "adamw_b_ada")
    return (loss, grad_x, *[G[n] for n in names], *[delta[n] for n in names], *[new_m[n] for n in names],
            *[new_v[n] for n in names])
```

```python
import functools
import math

import numpy as np
import jax
import jax.numpy as jnp
from jax import lax
from jax.experimental import pallas as pl
from jax.experimental.pallas import tpu as pltpu

F32 = jnp.float32
BF16 = jnp.bfloat16

D_MODEL = 1024
SEQ = 2048
N_HEADS = 8
HEAD_DIM = 64
D_A = 512
D_B = 512
Q_LORA = 384
KV_LORA = 256
ROPE_DIM = 32
NOPE_DIM = 64
D_FF = 2816
N_MOD = 6
N_BUCKETS = 32
MAX_DISTANCE = 2048
ROPE_THETA = 10000.0
EPS = 1e-6
NEG = -1e30
BLK = 128
DILATIONS = (1, 4, 16)
SPAN = 128
MLA_SCALE = (NOPE_DIM + ROPE_DIM) ** -0.5
DIL_SCALE = HEAD_DIM ** -0.5

ADAM_LR = 0.001
ADAM_B1 = 0.9
ADAM_B2 = 0.999
ADAM_EPS = 1e-08
ADAM_WD = 0.01
ADAM_STEP = 10

N_DEV = 8
N_CHIP = 4
LANES = 128
VMEM_LIMIT = 48 * 1024 * 1024

P_QKV = 3 * D_A
P_REST = KV_LORA + LANES + Q_LORA


def _cparams(sem=None):
    return pltpu.CompilerParams(dimension_semantics=sem, vmem_limit_bytes=VMEM_LIMIT)


def _pick(n, cands):
    for c in cands:
        if n % c == 0:
            return c
    raise ValueError(f"no tile for {n} in {cands}")


def _mm(a, b, mode, out_dtype, name, col_blocks=None):
    blocked = col_blocks is not None
    if mode == "nn":
        (M, K) = a.shape
        K2, N = (b.shape[1], b.shape[0] * b.shape[2]) if blocked else b.shape
    elif mode == "nt":
        (M, K) = a.shape
        N, K2 = (b.shape[1], b.shape[0] * b.shape[2]) if blocked else b.shape
    else:
        (K, M), (K2, N) = a.shape, b.shape
    assert K == K2, (a.shape, b.shape, mode)
    tm = _pick(M, (512, 384, 256, 128))
    tn = _pick(N, (1408, 1024, 768, 512, 384, 256, 128))
    tk = _pick(K, (1024, 512, 384, 256, 128))
    if blocked and mode == "nt":
        tk = K // col_blocks
    elif blocked:
        tn = N // col_blocks
    nk = K // tk
    out_shape = (M, N)
    out_spec = pl.BlockSpec((tm, tn), lambda i, j, k: (i, j))
    if mode == "nn":
        a_spec = pl.BlockSpec((tm, tk), lambda i, j, k: (i, k))
        b_spec = (pl.BlockSpec((None, tk, tn), lambda i, j, k: (j, k, 0)) if blocked
                  else pl.BlockSpec((tk, tn), lambda i, j, k: (k, j)))
        dn = (((1,), (0,)), ((), ()))
    elif mode == "nt":
        a_spec = pl.BlockSpec((tm, tk), lambda i, j, k: (i, k))
        b_spec = (pl.BlockSpec((None, tn, tk), lambda i, j, k: (k, j, 0)) if blocked
                  else pl.BlockSpec((tn, tk), lambda i, j, k: (j, k)))
        dn = (((1,), (1,)), ((), ()))
    else:
        a_spec = pl.BlockSpec((tk, tm), lambda i, j, k: (k, i))
        b_spec = pl.BlockSpec((tk, tn), lambda i, j, k: (k, j))
        dn = (((0,), (0,)), ((), ()))
        if blocked:
            out_shape = (col_blocks, M, tn)
            out_spec = pl.BlockSpec((None, tm, tn), lambda i, j, k: (j, i, 0))

    def body(a_ref, b_ref, o_ref, acc_ref):
        k = pl.program_id(2)

        @pl.when(k == 0)
        def _():
            acc_ref[...] = jnp.zeros_like(acc_ref)

        acc_ref[...] += lax.dot_general(a_ref[...].astype(BF16), b_ref[...].astype(BF16), dn,
                                        preferred_element_type=F32)

        @pl.when(k == nk - 1)
        def _():
            o_ref[...] = acc_ref[...].astype(o_ref.dtype)

    return pl.pallas_call(
        body, name=name,
        out_shape=jax.ShapeDtypeStruct(out_shape, out_dtype),
        grid=(M // tm, N // tn, nk),
        in_specs=[a_spec, b_spec],
        out_specs=out_spec,
        scratch_shapes=[pltpu.VMEM((tm, tn), F32)],
        compiler_params=_cparams(("parallel", "parallel", "arbitrary")),
    )(a, b)


ROW_TILE = 256


def _adaln_fwd(x, g, sc, sh, name, mix=None, gate=None):
    B, S, D = x.shape
    ts = ROW_TILE
    has_res = mix is not None

    def body(*refs):
        if has_res:
            x_ref, g_ref, sc_ref, sh_ref, mix_ref, gate_ref, h_ref, xr_ref = refs
            xr = x_ref[0] + gate_ref[0] * mix_ref[0]
            xr_ref[0] = xr
        else:
            x_ref, g_ref, sc_ref, sh_ref, h_ref = refs
            xr = x_ref[0]
        r = lax.rsqrt(jnp.mean(xr * xr, axis=-1, keepdims=True) + EPS)
        xn = (xr * r) * g_ref[...]
        h_ref[0] = (xn * (1.0 + sc_ref[0]) + sh_ref[0]).astype(h_ref.dtype)

    tok = pl.BlockSpec((1, ts, D), lambda b, s: (b, s, 0))
    per_b = pl.BlockSpec((1, 1, D), lambda b, s: (b, 0, 0))
    vec = pl.BlockSpec((1, D), lambda b, s: (0, 0))
    in_specs = [tok, vec, per_b, per_b]
    args = [x, g, sc, sh]
    out_shape = [jax.ShapeDtypeStruct((B, S, D), BF16)]
    out_specs = [tok]
    if has_res:
        in_specs += [tok, per_b]
        args += [mix, gate]
        out_shape.append(jax.ShapeDtypeStruct((B, S, D), F32))
        out_specs.append(tok)
    out = pl.pallas_call(
        body, name=name, out_shape=out_shape, grid=(B, S // ts),
        in_specs=in_specs, out_specs=out_specs,
        compiler_params=_cparams(("parallel", "parallel")),
    )(*args)
    return out if has_res else out[0]


def _adaln_bwd(dh, x, g, sc, dres, name, mix=None, gate=None):
    B, S, D = x.shape
    ts = ROW_TILE
    has_res = mix is not None

    def body(*refs):
        if has_res:
            (dh_ref, x_ref, g_ref, sc_ref, dres_ref, mix_ref, gate_ref,
             dx_ref, dsh_ref, dsc_ref, dg_ref, dgate_ref, dmix_ref) = refs
        else:
            (dh_ref, x_ref, g_ref, sc_ref, dres_ref, dx_ref, dsh_ref, dsc_ref, dg_ref) = refs
        b, s = pl.program_id(0), pl.program_id(1)
        xv = x_ref[0]
        dhv = dh_ref[0]
        gv = g_ref[...]
        r = lax.rsqrt(jnp.mean(xv * xv, axis=-1, keepdims=True) + EPS)
        n = xv * r
        xn = n * gv
        dxn = dhv * (1.0 + sc_ref[0])
        dn = dxn * gv
        dx = r * (dn - n * jnp.mean(dn * n, axis=-1, keepdims=True)) + dres_ref[0]
        dx_ref[0] = dx

        @pl.when(s == 0)
        def _():
            dsh_ref[...] = jnp.zeros_like(dsh_ref)
            dsc_ref[...] = jnp.zeros_like(dsc_ref)
            if has_res:
                dgate_ref[...] = jnp.zeros_like(dgate_ref)

        @pl.when((s == 0) & (b == 0))
        def _():
            dg_ref[...] = jnp.zeros_like(dg_ref)

        dsh_ref[0] += jnp.sum(dhv, axis=0, keepdims=True)
        dsc_ref[0] += jnp.sum(dhv * xn, axis=0, keepdims=True)
        dg_ref[...] += jnp.sum(dxn * n, axis=0, keepdims=True)
        if has_res:
            dgate_ref[0] += jnp.sum(dx * mix_ref[0], axis=0, keepdims=True)
            dmix_ref[0] = (dx * gate_ref[0]).astype(dmix_ref.dtype)

    tok = pl.BlockSpec((1, ts, D), lambda b, s: (b, s, 0))
    per_b = pl.BlockSpec((1, 1, D), lambda b, s: (b, 0, 0))
    vec = pl.BlockSpec((1, D), lambda b, s: (0, 0))
    in_specs = [tok, tok, vec, per_b, tok]
    args = [dh, x, g, sc, dres]
    out_shape = [jax.ShapeDtypeStruct((B, S, D), F32), jax.ShapeDtypeStruct((B, 1, D), F32),
                 jax.ShapeDtypeStruct((B, 1, D), F32), jax.ShapeDtypeStruct((1, D), F32)]
    out_specs = [tok, per_b, per_b, vec]
    if has_res:
        in_specs += [tok, per_b]
        args += [mix, gate]
        out_shape += [jax.ShapeDtypeStruct((B, 1, D), F32), jax.ShapeDtypeStruct((B, S, D), BF16)]
        out_specs += [per_b, tok]
    return pl.pallas_call(
        body, name=name, out_shape=out_shape, grid=(B, S // ts),
        in_specs=in_specs, out_specs=out_specs,
        compiler_params=_cparams(("arbitrary", "arbitrary")),
    )(*args)


def _rms_fwd(x, col_blk, n, g, name, n_real=None):
    T = x.shape[0]
    tr = 512
    nr = float(n_real or n)

    def body(x_ref, g_ref, y_ref):
        xv = x_ref[...]
        r = lax.rsqrt(jnp.sum(xv * xv, axis=-1, keepdims=True) / nr + EPS)
        y_ref[...] = ((xv * r) * g_ref[...]).astype(y_ref.dtype)

    return pl.pallas_call(
        body, name=name, out_shape=jax.ShapeDtypeStruct((T, n), BF16), grid=(T // tr,),
        in_specs=[pl.BlockSpec((tr, n), lambda i: (i, col_blk)), pl.BlockSpec((1, n), lambda i: (0, 0))],
        out_specs=pl.BlockSpec((tr, n), lambda i: (i, 0)),
        compiler_params=_cparams(("parallel",)),
    )(x, g)


def _rms_bwd(dy, dy_blk, x, x_blk, n, g, name, out_dtype=BF16):
    T = x.shape[0]
    tr = 512

    def body(dy_ref, x_ref, g_ref, dx_ref, dg_ref):
        xv = x_ref[...]
        dyv = dy_ref[...].astype(F32)
        r = lax.rsqrt(jnp.mean(xv * xv, axis=-1, keepdims=True) + EPS)
        nrm = xv * r
        dn = dyv * g_ref[...]
        dx_ref[...] = (r * (dn - nrm * jnp.mean(dn * nrm, axis=-1, keepdims=True))).astype(dx_ref.dtype)

        @pl.when(pl.program_id(0) == 0)
        def _():
            dg_ref[...] = jnp.zeros_like(dg_ref)

        dg_ref[...] += jnp.sum(dyv * nrm, axis=0, keepdims=True)

    return pl.pallas_call(
        body, name=name,
        out_shape=[jax.ShapeDtypeStruct((T, n), out_dtype), jax.ShapeDtypeStruct((1, n), F32)],
        grid=(T // tr,),
        in_specs=[pl.BlockSpec((tr, n), lambda i: (i, dy_blk)), pl.BlockSpec((tr, n), lambda i: (i, x_blk)),
                  pl.BlockSpec((1, n), lambda i: (0, 0))],
        out_specs=[pl.BlockSpec((tr, n), lambda i: (i, 0)), pl.BlockSpec((1, n), lambda i: (0, 0))],
        compiler_params=_cparams(("arbitrary",)),
    )(dy, x, g)


def _rms_bwd_views(dy, dy_blk, x, g, name):
    B, S, n = x.shape
    tiles = S // VIEW_TILE

    def body(dy_ref, x_ref, g_ref, d1_ref, d4_ref, d16_ref, dg_ref, dx_s):
        xv = x_ref[0]
        dyv = dy_ref[...]
        r = lax.rsqrt(jnp.mean(xv * xv, axis=-1, keepdims=True) + EPS)
        nrm = xv * r
        dn = dyv * g_ref[...]
        dx = r * (dn - nrm * jnp.mean(dn * nrm, axis=-1, keepdims=True))
        d1_ref[0] = dx.astype(d1_ref.dtype)
        _put_tile(dx_s, dx)
        _tile_to_view(dx_s, d4_ref, DILATIONS[1], n)
        _tile_to_view(dx_s, d16_ref, DILATIONS[2], n)

        @pl.when((pl.program_id(0) == 0) & (pl.program_id(1) == 0))
        def _():
            dg_ref[...] = jnp.zeros_like(dg_ref)

        dg_ref[...] += jnp.sum(dyv * nrm, axis=0, keepdims=True)

    res = pl.pallas_call(
        body, name=name,
        out_shape=[_view_shape(B, S, d, n, BF16) for d in DILATIONS] + [jax.ShapeDtypeStruct((1, n), F32)],
        grid=(B, tiles),
        in_specs=[pl.BlockSpec((VIEW_TILE, n), lambda b, t: (b * tiles + t, dy_blk)), _view_spec(1, n),
                  pl.BlockSpec((1, n), lambda b, t: (0, 0))],
        out_specs=[_view_spec(d, n) for d in DILATIONS] + [pl.BlockSpec((1, n), lambda b, t: (0, 0))],
        scratch_shapes=[_tile_scratch(n)],
        compiler_params=_cparams(("arbitrary", "arbitrary")),
    )(dy, x, g)
    return res[:len(DILATIONS)], res[len(DILATIONS)]


def _swiglu_fwd(gu, name):
    T = gu.shape[0]
    tr, tc = 512, 1408
    nc = D_FF // tc

    def body(g_ref, u_ref, a_ref):
        gv = g_ref[...]
        a_ref[...] = (gv * jax.nn.sigmoid(gv) * u_ref[...]).astype(a_ref.dtype)

    return pl.pallas_call(
        body, name=name, out_shape=jax.ShapeDtypeStruct((T, D_FF), BF16), grid=(T // tr, nc),
        in_specs=[pl.BlockSpec((tr, tc), lambda i, j: (i, j)), pl.BlockSpec((tr, tc), lambda i, j: (i, j + nc))],
        out_specs=pl.BlockSpec((tr, tc), lambda i, j: (i, j)),
        compiler_params=_cparams(("parallel", "parallel")),
    )(gu, gu)


def _swiglu_bwd(da, gu, name):
    T = gu.shape[0]
    tr, tc = 512, 1408
    nc = D_FF // tc

    def body(da_ref, g_ref, u_ref, dgu_ref):
        j = pl.program_id(1)
        gv, uv, dav = g_ref[...], u_ref[...], da_ref[...]
        sg = jax.nn.sigmoid(gv)

        @pl.when(j < nc)
        def _():
            dgu_ref[...] = (dav * uv * (sg * (1.0 + gv * (1.0 - sg)))).astype(dgu_ref.dtype)

        @pl.when(j >= nc)
        def _():
            dgu_ref[...] = (dav * (gv * sg)).astype(dgu_ref.dtype)

    return pl.pallas_call(
        body, name=name, out_shape=jax.ShapeDtypeStruct((T, 2 * D_FF), BF16), grid=(T // tr, 2 * nc),
        in_specs=[pl.BlockSpec((tr, tc), lambda i, j: (i, j % nc)),
                  pl.BlockSpec((tr, tc), lambda i, j: (i, j % nc)),
                  pl.BlockSpec((tr, tc), lambda i, j: (i, j % nc + nc))],
        out_specs=pl.BlockSpec((tr, tc), lambda i, j: (i, j)),
        compiler_params=_cparams(("parallel", "parallel")),
    )(da, gu, gu)


def _final_loss(x1, f, g2, gf, target, name):
    B, S, D = x1.shape
    ts = ROW_TILE

    def body(x1_ref, f_ref, g2_ref, gf_ref, t_ref, dx_ref, df_ref, dg2_ref, dgf_ref, loss_ref):
        b, s = pl.program_id(0), pl.program_id(1)
        fv = f_ref[0]
        g2v = g2_ref[0]
        gfv = gf_ref[...]
        x2 = x1_ref[0] + g2v * fv
        r = lax.rsqrt(jnp.mean(x2 * x2, axis=-1, keepdims=True) + EPS)
        n = x2 * r
        e = n * gfv - t_ref[0]
        dy = e * (1.0 / D)
        dn = dy * gfv
        dx = r * (dn - n * jnp.mean(dn * n, axis=-1, keepdims=True))
        dx_ref[0] = dx
        df_ref[0] = (dx * g2v).astype(df_ref.dtype)

        @pl.when(s == 0)
        def _():
            dg2_ref[...] = jnp.zeros_like(dg2_ref)

        @pl.when((s == 0) & (b == 0))
        def _():
            dgf_ref[...] = jnp.zeros_like(dgf_ref)
            loss_ref[...] = jnp.zeros_like(loss_ref)

        dg2_ref[0] += jnp.sum(dx * fv, axis=0, keepdims=True)
        dgf_ref[...] += jnp.sum(dy * n, axis=0, keepdims=True)
        loss_ref[...] += 0.5 * jnp.sum(jnp.mean(e * e, axis=-1, keepdims=True), axis=0, keepdims=True)

    tok = pl.BlockSpec((1, ts, D), lambda b, s: (b, s, 0))
    per_b = pl.BlockSpec((1, 1, D), lambda b, s: (b, 0, 0))
    vec = pl.BlockSpec((1, D), lambda b, s: (0, 0))
    return pl.pallas_call(
        body, name=name,
        out_shape=[jax.ShapeDtypeStruct((B, S, D), F32), jax.ShapeDtypeStruct((B, S, D), BF16),
                   jax.ShapeDtypeStruct((B, 1, D), F32), jax.ShapeDtypeStruct((1, D), F32),
                   jax.ShapeDtypeStruct((1, LANES), F32)],
        grid=(B, S // ts),
        in_specs=[tok, tok, per_b, vec, tok],
        out_specs=[tok, tok, per_b, vec, pl.BlockSpec((1, LANES), lambda b, s: (0, 0))],
        compiler_params=_cparams(("arbitrary", "arbitrary")),
    )(x1, f, g2, gf, target)


def _rope_tables():
    half = ROPE_DIM // 2
    inv = ROPE_THETA ** (-jnp.arange(half, dtype=F32) / half)
    ang = jnp.arange(SEQ, dtype=F32)[:, None] * inv[None, :]
    cos, sin = jnp.cos(ang), jnp.sin(ang)
    one = jnp.ones((SEQ, NOPE_DIM), F32)
    zero = jnp.zeros((SEQ, NOPE_DIM), F32)
    cs = jnp.concatenate([one, cos, cos, one[:, :LANES - NOPE_DIM - ROPE_DIM]], axis=1)
    sn = jnp.concatenate([zero, -sin, sin, zero[:, :LANES - NOPE_DIM - ROPE_DIM]], axis=1)
    return cs, sn


def _rope_group(t, cs, sn):
    half = ROPE_DIM // 2
    lane = lax.broadcasted_iota(jnp.int32, t.shape, 1)
    partner = jnp.where(lane < NOPE_DIM + half, pltpu.roll(t, LANES - half, 1), pltpu.roll(t, half, 1))
    return t * cs + partner * sn


def _rope_apply(t, cs, sn, out_dtype, name, add=None, add_blk=0):
    B, S, W = t.shape
    G = W // LANES
    ts = ROW_TILE

    def body(*refs):
        if add is None:
            t_ref, cs_ref, sn_ref, o_ref = refs
            for gi in range(G):
                sl = slice(gi * LANES, (gi + 1) * LANES)
                o_ref[0, :, sl] = _rope_group(t_ref[0, :, sl], cs_ref[...], sn_ref[...]).astype(o_ref.dtype)
        else:
            t_ref, a_ref, cs_ref, sn_ref, o_ref = refs
            ra = _rope_group(a_ref[0], cs_ref[...], sn_ref[...])
            for gi in range(G):
                sl = slice(gi * LANES, (gi + 1) * LANES)
                o_ref[0, :, sl] = (t_ref[0, :, sl] + ra).astype(o_ref.dtype)

    tok = pl.BlockSpec((1, ts, W), lambda b, s: (b, s, 0))
    tab = pl.BlockSpec((ts, LANES), lambda b, s: (s, 0))
    in_specs, args = [tok], [t]
    if add is not None:
        in_specs.append(pl.BlockSpec((1, ts, LANES), lambda b, s: (b, s, add_blk)))
        args.append(add)
    in_specs += [tab, tab]
    args += [cs, sn]
    return pl.pallas_call(
        body, name=name, out_shape=jax.ShapeDtypeStruct((B, S, W), out_dtype), grid=(B, S // ts),
        in_specs=in_specs, out_specs=tok, compiler_params=_cparams(("parallel", "parallel")),
    )(*args)


def _krope_bwd(dkc, cs, sn_neg, name):
    B, S, W = dkc.shape
    G = W // LANES
    ts = ROW_TILE

    def body(d_ref, cs_ref, sn_ref, o_ref):
        acc = d_ref[0, :, 0:LANES]
        for gi in range(1, G):
            acc = acc + d_ref[0, :, gi * LANES:(gi + 1) * LANES]
        lane = lax.broadcasted_iota(jnp.int32, acc.shape, 1)
        rot = (lane >= NOPE_DIM) & (lane < NOPE_DIM + ROPE_DIM)
        acc = jnp.where(rot, acc, 0.0)
        o_ref[0] = _rope_group(acc, cs_ref[...], sn_ref[...]).astype(o_ref.dtype)

    tab = pl.BlockSpec((ts, LANES), lambda b, s: (s, 0))
    return pl.pallas_call(
        body, name=name, out_shape=jax.ShapeDtypeStruct((B, S, LANES), BF16), grid=(B, S // ts),
        in_specs=[pl.BlockSpec((1, ts, W), lambda b, s: (b, s, 0)), tab, tab],
        out_specs=pl.BlockSpec((1, ts, LANES), lambda b, s: (b, s, 0)),
        compiler_params=_cparams(("parallel", "parallel")),
    )(dkc, cs, sn_neg)


def _t5_bucket(dist):
    max_exact = N_BUCKETS // 2
    d = np.maximum(dist, 1).astype(np.float64)
    large = max_exact + (np.log(d / max_exact) / np.log(MAX_DISTANCE / max_exact)
                         * (N_BUCKETS - max_exact)).astype(np.int64)
    large = np.minimum(large, N_BUCKETS - 1)
    return np.where(dist < max_exact, dist, large).astype(np.int32)


def _band_buckets(dilation):
    a = np.arange(BLK)[:, None]
    bk = np.arange(2 * BLK)[None, :]
    steps = BLK + a - bk
    return _t5_bucket(np.clip(steps, 0, SPAN) * dilation)


def _head_mask(shape, hh):
    lane = lax.broadcasted_iota(jnp.int32, shape, 1)
    return (lane >= hh * HEAD_DIM) & (lane < (hh + 1) * HEAD_DIM)


def _dot_nt(a, b):
    return lax.dot_general(a, b, (((1,), (1,)), ((), ())), preferred_element_type=F32)


def _dot_tn(a, b):
    return lax.dot_general(a, b, (((0,), (0,)), ((), ())), preferred_element_type=F32)


def _dot_nn(a, b):
    return lax.dot_general(a, b, (((1,), (0,)), ((), ())), preferred_element_type=F32)


def _band_valid():
    a = lax.broadcasted_iota(jnp.int32, (BLK, BLK), 0)
    bk = lax.broadcasted_iota(jnp.int32, (BLK, BLK), 1)
    return bk >= a, bk <= a


def _dil_fwd(qkv, bias, branch, dilation, name):
    B, n, _ = qkv.shape
    d = dilation
    nb = n // BLK
    qkv_v = qkv
    npair = N_HEADS // 2

    def body(cur_ref, prev_ref, bias_ref, o_ref, lse_ref):
        i = pl.program_id(2)
        vprev, vcur = _band_valid()
        vprev = vprev & (i > 0)
        for p in range(npair):
            sl = slice(p * LANES, (p + 1) * LANES)
            q = cur_ref[0, :, sl]
            kc = cur_ref[0, :, D_A + p * LANES:D_A + (p + 1) * LANES]
            vc = cur_ref[0, :, 2 * D_A + p * LANES:2 * D_A + (p + 1) * LANES]
            kp = prev_ref[0, :, D_A + p * LANES:D_A + (p + 1) * LANES]
            vp = prev_ref[0, :, 2 * D_A + p * LANES:2 * D_A + (p + 1) * LANES]
            o_pair = jnp.zeros((BLK, LANES), F32)
            lse_pair = jnp.zeros((BLK, LANES), F32)
            for hh in range(2):
                h = 2 * p + hh
                hm = _head_mask((BLK, LANES), hh)
                qm = jnp.where(hm, q, jnp.zeros_like(q))
                s_p = _dot_nt(qm, kp) * DIL_SCALE + bias_ref[h, :, 0:BLK]
                s_c = _dot_nt(qm, kc) * DIL_SCALE + bias_ref[h, :, BLK:2 * BLK]
                s_p = jnp.where(vprev, s_p, NEG)
                s_c = jnp.where(vcur, s_c, NEG)
                m = jnp.maximum(jnp.max(s_p, axis=-1, keepdims=True), jnp.max(s_c, axis=-1, keepdims=True))
                e_p = jnp.exp(s_p - m)
                e_c = jnp.exp(s_c - m)
                l = jnp.sum(e_p, axis=-1, keepdims=True) + jnp.sum(e_c, axis=-1, keepdims=True)
                vpm = jnp.where(hm, vp, jnp.zeros_like(vp))
                vcm = jnp.where(hm, vc, jnp.zeros_like(vc))
                o_h = _dot_nn(e_p.astype(BF16), vpm) + _dot_nn(e_c.astype(BF16), vcm)
                o_pair = o_pair + o_h / l
                lse_pair = jnp.where(hm, m + jnp.log(l), lse_pair)
            o_ref[0, :, sl] = o_pair
            lse_ref[0, :, sl] = lse_pair

    cur = pl.BlockSpec((1, BLK, P_QKV), lambda b, r, i: (b, i, r))
    prev = pl.BlockSpec((1, BLK, P_QKV), lambda b, r, i: (b, jnp.maximum(i - 1, 0), r))
    out = pl.BlockSpec((1, BLK, D_A), lambda b, r, i: (b, i, r))
    o, lse = pl.pallas_call(
        body, name=name,
        out_shape=[jax.ShapeDtypeStruct((B, n, d * D_A), F32)] * 2,
        grid=(B, d, nb),
        in_specs=[cur, prev,
                  pl.BlockSpec((None, N_HEADS, BLK, 2 * BLK), lambda b, r, i: (branch, 0, 0, 0))],
        out_specs=[out, out],
        compiler_params=_cparams(("parallel", "parallel", "arbitrary")),
    )(qkv_v, qkv_v, bias)
    return o, lse


VIEW_TILE = 512


def _view_spec(d, w):
    return pl.BlockSpec((1, VIEW_TILE // d, d * w), lambda b, t: (b, t, 0))


def _view_shape(B, S, d, w, dtype):
    return jax.ShapeDtypeStruct((B, S // d, d * w), dtype)


def _tile_scratch(w):
    return pltpu.VMEM((w // LANES, VIEW_TILE, LANES), F32)


def _put_tile(tile_ref, val):
    for c in range(tile_ref.shape[0]):
        tile_ref[c] = val[:, c * LANES:(c + 1) * LANES]


def _get_tile(tile_ref):
    return jnp.concatenate([tile_ref[c] for c in range(tile_ref.shape[0])], axis=1)


def _tile_to_view(tile_ref, view_ref, d, w):
    for c in range(w // LANES):
        for r in range(d):
            lo = r * w + c * LANES
            rows = tile_ref.at[c][pl.ds(r, VIEW_TILE // d, stride=d), :]
            view_ref[0, :, lo:lo + LANES] = rows.astype(view_ref.dtype)


def _view_to_tile(view_ref, tile_ref, d, w):
    for c in range(w // LANES):
        for r in range(d):
            lo = r * w + c * LANES
            tile_ref.at[c][pl.ds(r, VIEW_TILE // d, stride=d), :] = view_ref[0, :, lo:lo + LANES].astype(F32)


def _mm_qkv_views(h, w, name):
    B, S, D = h.shape
    N = w.shape[1]

    def body(h_ref, w_ref, o1_ref, o4_ref, o16_ref, acc_ref):
        acc = jnp.dot(h_ref[0], w_ref[...], preferred_element_type=F32)
        o1_ref[0] = acc.astype(o1_ref.dtype)
        _put_tile(acc_ref, acc)
        _tile_to_view(acc_ref, o4_ref, DILATIONS[1], N)
        _tile_to_view(acc_ref, o16_ref, DILATIONS[2], N)

    return pl.pallas_call(
        body, name=name,
        out_shape=[_view_shape(B, S, d, N, BF16) for d in DILATIONS],
        grid=(B, S // VIEW_TILE),
        in_specs=[pl.BlockSpec((1, VIEW_TILE, D), lambda b, t: (b, t, 0)), pl.BlockSpec((D, N), lambda b, t: (0, 0))],
        out_specs=[_view_spec(d, N) for d in DILATIONS],
        scratch_shapes=[_tile_scratch(N)],
        compiler_params=_cparams(("parallel", "parallel")),
    )(h, w)


def _dil_merge(os_, lses, name):
    B, S, W = os_[0].shape
    nd = len(DILATIONS)

    def body(*refs):
        o_refs, l_refs = refs[:nd], refs[nd:2 * nd]
        out_refs, L_refs = refs[2 * nd:3 * nd], refs[3 * nd:4 * nd]
        scr = refs[4 * nd:]
        o_tok, l_tok = [o_refs[0][0]], [l_refs[0][0]]
        for i, d in enumerate(DILATIONS[1:]):
            _view_to_tile(o_refs[i + 1], scr[2 * i], d, W)
            _view_to_tile(l_refs[i + 1], scr[2 * i + 1], d, W)
            o_tok.append(_get_tile(scr[2 * i]))
            l_tok.append(_get_tile(scr[2 * i + 1]))
        a0, a1, a2 = l_tok
        m = jnp.maximum(jnp.maximum(a0, a1), a2)
        e0, e1, e2 = jnp.exp(a0 - m), jnp.exp(a1 - m), jnp.exp(a2 - m)
        ssum = e0 + e1 + e2
        out = (e0 * o_tok[0] + e1 * o_tok[1] + e2 * o_tok[2]) / ssum
        lse = m + jnp.log(ssum)
        out_refs[0][0] = out
        L_refs[0][0] = lse
        res_o, res_l = scr[2 * (nd - 1)], scr[2 * (nd - 1) + 1]
        _put_tile(res_o, out)
        _put_tile(res_l, lse)
        for i, d in enumerate(DILATIONS[1:]):
            _tile_to_view(res_o, out_refs[i + 1], d, W)
            _tile_to_view(res_l, L_refs[i + 1], d, W)

    specs = [_view_spec(d, W) for d in DILATIONS]
    shapes = [_view_shape(B, S * DILATIONS[0], d, W, F32) for d in DILATIONS]
    res = pl.pallas_call(
        body, name=name, out_shape=shapes * 2, grid=(B, S // VIEW_TILE),
        in_specs=specs * 2, out_specs=specs * 2,
        scratch_shapes=[_tile_scratch(W)] * (2 * nd),
        compiler_params=_cparams(("parallel", "parallel")),
    )(*os_, *lses)
    return res[:nd], res[nd:]


def _dil_bwd(qkv, do, out_a, L, bias, branch, dilation, name):
    B, n, _ = qkv.shape
    d = dilation
    nb = n // BLK
    qkv_v, do_v, oa_v, L_v = qkv, do, out_a, L
    npair = N_HEADS // 2
    multi = nb > 1

    def body(*refs):
        if multi:
            (cur_ref, prev_ref, next_ref, do_ref, don_ref, oa_ref, oan_ref, L_ref, Ln_ref, bias_ref,
             dqkv_ref, dbias_ref) = refs
        else:
            cur_ref, do_ref, oa_ref, L_ref, bias_ref, dqkv_ref, dbias_ref = refs
        b, r, i = pl.program_id(0), pl.program_id(1), pl.program_id(2)

        @pl.when((b == 0) & (r == 0) & (i == 0))
        def _():
            dbias_ref[...] = jnp.zeros_like(dbias_ref)

        vprev, vcur = _band_valid()
        has_prev = i > 0
        has_next = i < nb - 1
        for p in range(npair):
            sl = slice(p * LANES, (p + 1) * LANES)
            ksl = slice(D_A + p * LANES, D_A + (p + 1) * LANES)
            vsl = slice(2 * D_A + p * LANES, 2 * D_A + (p + 1) * LANES)
            q, kc, vc = cur_ref[0, :, sl], cur_ref[0, :, ksl], cur_ref[0, :, vsl]
            dov = do_ref[0, :, sl]
            dd = dov.astype(F32) * oa_ref[0, :, sl]
            Lv = L_ref[0, :, sl]
            if multi:
                kp, vp = prev_ref[0, :, ksl], prev_ref[0, :, vsl]
                qn = next_ref[0, :, sl]
                donv = don_ref[0, :, sl]
                ddn = donv.astype(F32) * oan_ref[0, :, sl]
                Lnv = Ln_ref[0, :, sl]
            dq_pair = jnp.zeros((BLK, LANES), F32)
            dk_pair = jnp.zeros((BLK, LANES), F32)
            dv_pair = jnp.zeros((BLK, LANES), F32)
            for hh in range(2):
                h = 2 * p + hh
                hm = _head_mask((BLK, LANES), hh)
                zero = jnp.zeros_like(q)
                qm = jnp.where(hm, q, zero)
                dom = jnp.where(hm, dov, zero)
                delta = jnp.sum(jnp.where(hm, dd, 0.0), axis=-1, keepdims=True)
                lse = Lv[:, hh * HEAD_DIM:hh * HEAD_DIM + 1]
                s_c = _dot_nt(qm, kc) * DIL_SCALE + bias_ref[h, :, BLK:2 * BLK]
                p_c = jnp.where(vcur, jnp.exp(s_c - lse), 0.0)
                ds_c = p_c * (_dot_nt(dom, vc) - delta)
                ds_cb = ds_c.astype(BF16)
                dq_h = _dot_nn(ds_cb, kc)
                dk_h = _dot_tn(ds_cb, qm)
                dv_h = _dot_tn(p_c.astype(BF16), dom)
                dbias_ref[h, :, BLK:2 * BLK] += ds_c
                if multi:
                    s_p = _dot_nt(qm, kp) * DIL_SCALE + bias_ref[h, :, 0:BLK]
                    p_p = jnp.where(vprev & has_prev, jnp.exp(s_p - lse), 0.0)
                    ds_p = p_p * (_dot_nt(dom, vp) - delta)
                    dq_h = dq_h + _dot_nn(ds_p.astype(BF16), kp)
                    dbias_ref[h, :, 0:BLK] += ds_p
                    qnm = jnp.where(hm, qn, zero)
                    donm = jnp.where(hm, donv, zero)
                    delta_n = jnp.sum(jnp.where(hm, ddn, 0.0), axis=-1, keepdims=True)
                    lse_n = Lnv[:, hh * HEAD_DIM:hh * HEAD_DIM + 1]
                    s_n = _dot_nt(qnm, kc) * DIL_SCALE + bias_ref[h, :, 0:BLK]
                    p_n = jnp.where(vprev & has_next, jnp.exp(s_n - lse_n), 0.0)
                    ds_n = p_n * (_dot_nt(donm, vc) - delta_n)
                    dk_h = dk_h + _dot_tn(ds_n.astype(BF16), qnm)
                    dv_h = dv_h + _dot_tn(p_n.astype(BF16), donm)
                dq_pair = dq_pair + jnp.where(hm, dq_h, 0.0) * DIL_SCALE
                dk_pair = dk_pair + jnp.where(hm, dk_h, 0.0) * DIL_SCALE
                dv_pair = dv_pair + jnp.where(hm, dv_h, 0.0)
            dqkv_ref[0, :, sl] = dq_pair
            dqkv_ref[0, :, ksl] = dk_pair
            dqkv_ref[0, :, vsl] = dv_pair

    def at(off):
        return lambda b, r, i: (b, jnp.clip(i + off, 0, nb - 1), r)

    qkv_spec = lambda off: pl.BlockSpec((1, BLK, P_QKV), at(off))
    da_spec = lambda off: pl.BlockSpec((1, BLK, D_A), at(off))
    bias_spec = pl.BlockSpec((None, N_HEADS, BLK, 2 * BLK), lambda b, r, i: (branch, 0, 0, 0))
    dbias_spec = pl.BlockSpec((N_HEADS, BLK, 2 * BLK), lambda b, r, i: (0, 0, 0))
    if multi:
        in_specs = [qkv_spec(0), qkv_spec(-1), qkv_spec(1), da_spec(0), da_spec(1), da_spec(0), da_spec(1),
                    da_spec(0), da_spec(1), bias_spec]
        args = [qkv_v, qkv_v, qkv_v, do_v, do_v, oa_v, oa_v, L_v, L_v, bias]
    else:
        in_specs = [qkv_spec(0), da_spec(0), da_spec(0), da_spec(0), bias_spec]
        args = [qkv_v, do_v, oa_v, L_v, bias]
    dqkv, dbias = pl.pallas_call(
        body, name=name,
        out_shape=[jax.ShapeDtypeStruct((B, n, d * P_QKV), F32),
                   jax.ShapeDtypeStruct((N_HEADS, BLK, 2 * BLK), F32)],
        grid=(B, d, nb),
        in_specs=in_specs,
        out_specs=[qkv_spec(0), dbias_spec],
        compiler_params=_cparams(("arbitrary", "arbitrary", "arbitrary")),
    )(*args)
    return dqkv, dbias


def _sum_views_bf16(parts, name):
    B, S, W = parts[0].shape

    def body(a_ref, b_ref, c_ref, o_ref, sb, sc):
        _view_to_tile(b_ref, sb, DILATIONS[1], W)
        _view_to_tile(c_ref, sc, DILATIONS[2], W)
        o_ref[0] = (a_ref[0] + _get_tile(sb) + _get_tile(sc)).astype(o_ref.dtype)

    return pl.pallas_call(
        body, name=name, out_shape=jax.ShapeDtypeStruct((B, S, W), BF16), grid=(B, S // VIEW_TILE),
        in_specs=[_view_spec(d, W) for d in DILATIONS], out_specs=_view_spec(1, W),
        scratch_shapes=[_tile_scratch(W)] * 2,
        compiler_params=_cparams(("parallel", "parallel")),
    )(*parts)


def _bias_tables(rel_bias, buckets, name):
    nbr = buckets.shape[0]

    def body(rb_ref, bk_ref, o_ref):
        h = pl.program_id(1)
        tab = bk_ref[0]

        def step(bkt, acc):
            return jnp.where(tab == bkt, rb_ref[bkt, h], acc)

        o_ref[0, 0] = lax.fori_loop(0, N_BUCKETS, step, jnp.zeros((BLK, 2 * BLK), F32))

    return pl.pallas_call(
        body, name=name, out_shape=jax.ShapeDtypeStruct((nbr, N_HEADS, BLK, 2 * BLK), F32),
        grid=(nbr, N_HEADS),
        in_specs=[pl.BlockSpec(memory_space=pltpu.SMEM),
                  pl.BlockSpec((1, BLK, 2 * BLK), lambda i, h: (i, 0, 0))],
        out_specs=pl.BlockSpec((1, 1, BLK, 2 * BLK), lambda i, h: (i, h, 0, 0)),
        compiler_params=_cparams(("parallel", "arbitrary")),
    )(rel_bias, buckets)


def _bias_grad(dbias_list, buckets, name):
    nbr = len(dbias_list)

    def body(*refs):
        d_refs, bk_ref, o_ref = refs[:nbr], refs[nbr], refs[nbr + 1]
        lane = lax.broadcasted_iota(jnp.int32, (1, LANES), 1)
        for h in range(N_HEADS):
            def step(bkt, acc):
                tot = jnp.zeros((1, 1), F32)
                for bi in range(nbr):
                    sel = jnp.where(bk_ref[bi] == bkt, d_refs[bi][h], 0.0)
                    tot = tot + jnp.sum(jnp.sum(sel, axis=1, keepdims=True), axis=0, keepdims=True)
                return acc + jnp.where(lane == bkt, tot, 0.0)

            o_ref[h:h + 1, :] = lax.fori_loop(0, N_BUCKETS, step, jnp.zeros((1, LANES), F32))

    band = pl.BlockSpec((N_HEADS, BLK, 2 * BLK), lambda i: (0, 0, 0))
    return pl.pallas_call(
        body, name=name, out_shape=jax.ShapeDtypeStruct((N_HEADS, LANES), F32), grid=(1,),
        in_specs=[band] * nbr + [pl.BlockSpec((nbr, BLK, 2 * BLK), lambda i: (0, 0, 0))],
        out_specs=pl.BlockSpec((N_HEADS, LANES), lambda i: (0, 0)),
        compiler_params=_cparams(("arbitrary",)),
    )(*dbias_list, buckets)


MLA_TQ = 256
MLA_TK = 256


LOG2E = math.log2(math.e)
MLA_C = MLA_SCALE * LOG2E


def _key_le_query(tk, tq):
    return lax.broadcasted_iota(jnp.int32, (tk, tq), 0) <= lax.broadcasted_iota(jnp.int32, (tk, tq), 1)


def _row_mask(shape, hh):
    row = lax.broadcasted_iota(jnp.int32, shape, 0)
    return (row >= hh * HEAD_DIM) & (row < (hh + 1) * HEAD_DIM)


def _host_call(body, comm, *, name, grid, in_specs, out_specs, out_shape, scratch_shapes, args):
    sem = ("arbitrary",) * len(grid)
    if comm is None:
        res = pl.pallas_call(body, name=name, grid=grid, in_specs=in_specs, out_specs=out_specs,
                             out_shape=out_shape, scratch_shapes=scratch_shapes,
                             compiler_params=_cparams(sem))(*args)
        return res, []
    n_in, n_out, n_s, cn = len(in_specs), len(out_specs), len(scratch_shapes), comm.n

    def hosted(*refs):
        ins, refs = refs[:n_in], refs[n_in:]
        c_ins, refs = refs[:cn], refs[cn:]
        outs, refs = refs[:n_out], refs[n_out:]
        c_outs, refs = refs[:cn], refs[cn:]
        scr, c_sems = refs[:n_s], refs[n_s:]
        ids = [pl.program_id(a) for a in range(len(grid))]
        first = functools.reduce(jnp.logical_and, [i == 0 for i in ids])
        last = functools.reduce(jnp.logical_and, [i == g - 1 for i, g in zip(ids, grid)])

        @pl.when(first)
        def _():
            comm.start(c_ins, c_outs, c_sems)

        body(*ins, *outs, *scr)

        @pl.when(last)
        def _():
            comm.finish(c_ins, c_outs, c_sems)

    res = pl.pallas_call(
        hosted, name=name, grid=grid, in_specs=list(in_specs) + _hbm_specs(cn),
        out_specs=list(out_specs) + _hbm_specs(cn), out_shape=list(out_shape) + list(comm.out_shape),
        scratch_shapes=list(scratch_shapes) + list(comm.scratch), compiler_params=_cparams(sem),
    )(*args, *comm.inputs)
    return res[:n_out], res[n_out:]


def _mla_fwd_t(q, k, vt, name, comm=None):
    B, S, _ = q.shape
    tq, tk = MLA_TQ, MLA_TK
    assert tq == tk
    npair = N_HEADS // 2
    nq = S // tq

    def body(q_ref, k_ref, vt_ref, o_ref, lse_ref, acc_s):
        i = pl.program_id(2)
        acc_s[...] = jnp.zeros_like(acc_s)
        qs = [q_ref[0, :, hh * LANES:(hh + 1) * LANES] for hh in range(2)]
        diag = _key_le_query(tk, tq)

        def step(j, ms, masked):
            kj = k_ref[0, pl.ds(pl.multiple_of(j * tk, tk), tk), :]
            vj = vt_ref[0, 0, j]
            out = []
            for hh in range(2):
                s = _dot_nt(kj[:, hh * LANES:(hh + 1) * LANES], qs[hh])
                if masked:
                    s = jnp.where(diag, s, NEG)
                m_new = jnp.maximum(ms[hh], jnp.max(s, axis=0, keepdims=True))
                alpha = jnp.exp2((ms[hh] - m_new) * MLA_C)
                e = jnp.exp2((s - m_new) * MLA_C).astype(BF16)
                vh = jnp.where(_row_mask(vj.shape, hh), vj, jnp.ones_like(vj))
                acc_s[hh] = acc_s[hh] * alpha + _dot_nn(vh, e)
                out.append(m_new)
            return tuple(out)

        m0 = jnp.full((1, tq), NEG, F32)
        ms = lax.fori_loop(0, i, lambda j, c: step(j, c, False), (m0, m0))
        ms = step(i, ms, True)
        rows0 = _row_mask((LANES, tq), 0)
        l0 = acc_s[0, HEAD_DIM:HEAD_DIM + 1, :]
        l1 = acc_s[1, 0:1, :]
        o_ref[0] = jnp.where(rows0, acc_s[0] / l0, acc_s[1] / l1)
        lse_ref[0, 0, 0] = jnp.zeros((8, tq), F32)
        lse_ref[0, 0, 0, 0:1, :] = ms[0] * MLA_C + jnp.log(l0) * LOG2E
        lse_ref[0, 0, 0, 1:2, :] = ms[1] * MLA_C + jnp.log(l1) * LOG2E

    return _host_call(
        body, comm, name=name,
        out_shape=[jax.ShapeDtypeStruct((B, D_B, S), F32), jax.ShapeDtypeStruct((B, npair, nq, 8, tq), F32)],
        grid=(B, npair, nq),
        in_specs=[pl.BlockSpec((1, tq, 2 * LANES), lambda b, p, i: (b, i, p)),
                  pl.BlockSpec((1, S, 2 * LANES), lambda b, p, i: (b, 0, p)),
                  pl.BlockSpec((1, 1, S // tk, LANES, tk), lambda b, p, i: (b, p, 0, 0, 0))],
        out_specs=[pl.BlockSpec((1, LANES, tq), lambda b, p, i: (b, p, i)),
                   pl.BlockSpec((1, 1, 1, 8, tq), lambda b, p, i: (b, p, i, 0, 0))],
        scratch_shapes=[pltpu.VMEM((2, LANES, tq), F32)],
        args=(q, k, vt))


def _mla_delta(do, o, name):
    B, S, _ = o.shape
    tq = MLA_TQ
    npair = N_HEADS // 2

    def body(do_ref, o_ref, d_ref):
        d_ref[...] = jnp.zeros_like(d_ref)
        for p in range(npair):
            sl = slice(p * LANES, (p + 1) * LANES)
            prod_t = jnp.transpose(do_ref[0, :, sl].astype(F32) * o_ref[0, :, sl])
            d_ref[0, p, 0, 0:1, :] = jnp.sum(prod_t[:HEAD_DIM], axis=0, keepdims=True)
            d_ref[0, p, 0, 1:2, :] = jnp.sum(prod_t[HEAD_DIM:], axis=0, keepdims=True)

    tok = pl.BlockSpec((1, tq, D_B), lambda b, i: (b, i, 0))
    return pl.pallas_call(
        body, name=name, out_shape=jax.ShapeDtypeStruct((B, npair, S // tq, 8, tq), F32),
        grid=(B, S // tq), in_specs=[tok, tok],
        out_specs=pl.BlockSpec((1, npair, 1, 8, tq), lambda b, i: (b, 0, i, 0, 0)),
        compiler_params=_cparams(("parallel", "parallel")),
    )(do, o)


def _mla_bwd_t(q, k, v, do, lse, delta, name, comm=None):
    B, S, _ = q.shape
    tq, tk = MLA_TQ, MLA_TK
    assert tq == tk
    npair = N_HEADS // 2
    nq = S // tq

    def body(q_ref, do_ref, lse_ref, dl_ref, k_ref, v_ref, dk_ref, dv_ref, dq_ref, dk_s, dv_s):
        j = pl.program_id(2)

        @pl.when(j == 0)
        def _():
            dq_ref[...] = jnp.zeros_like(dq_ref)

        vj = v_ref[0]
        dv_s[...] = jnp.zeros_like(dv_s)
        diag = _key_le_query(tk, tq)
        for hh in range(2):
            hsl = slice(hh * LANES, (hh + 1) * LANES)
            hm = _head_mask((tq, LANES), hh)
            kh = k_ref[0, :, hsl]
            kt = jnp.transpose(kh.astype(F32)).astype(BF16)
            dk_s[...] = jnp.zeros_like(dk_s)

            def step(i, masked):
                rows = pl.ds(pl.multiple_of(i * tq, tq), tq)
                qi = q_ref[0, rows, hsl]
                dov = do_ref[0, rows, :]
                dom = jnp.where(hm, dov, jnp.zeros_like(dov))
                s = _dot_nt(kh, qi)
                pr = jnp.exp2(s * MLA_C - lse_ref[0, 0, i, hh:hh + 1, :])
                if masked:
                    pr = jnp.where(diag, pr, 0.0)
                dv_s[...] += _dot_nn(pr.astype(BF16), dom)
                ds = (pr * (_dot_nt(vj, dom) - dl_ref[0, 0, i, hh:hh + 1, :])).astype(BF16)
                dk_s[...] += _dot_nn(ds, qi)
                dq_ref[0, 0, i, hsl, :] += _dot_nn(kt, ds) * MLA_SCALE

            step(j, True)

            def loop_body(i, carry):
                step(i, False)
                return carry

            lax.fori_loop(j + 1, nq, loop_body, 0)
            dk_ref[0, :, hsl] = dk_s[...] * MLA_SCALE
        dv_ref[0] = dv_s[...]

    stat = pl.BlockSpec((1, 1, nq, 8, tq), lambda b, p, j: (b, p, 0, 0, 0))
    return _host_call(
        body, comm, name=name,
        out_shape=[jax.ShapeDtypeStruct((B, S, N_HEADS * LANES), F32), jax.ShapeDtypeStruct((B, S, D_B), F32),
                   jax.ShapeDtypeStruct((B, npair, nq, 2 * LANES, tq), F32)],
        grid=(B, npair, S // tk),
        in_specs=[pl.BlockSpec((1, S, 2 * LANES), lambda b, p, j: (b, 0, p)),
                  pl.BlockSpec((1, S, LANES), lambda b, p, j: (b, 0, p)),
                  stat, stat,
                  pl.BlockSpec((1, tk, 2 * LANES), lambda b, p, j: (b, j, p)),
                  pl.BlockSpec((1, tk, LANES), lambda b, p, j: (b, j, p))],
        out_specs=[pl.BlockSpec((1, tk, 2 * LANES), lambda b, p, j: (b, j, p)),
                   pl.BlockSpec((1, tk, LANES), lambda b, p, j: (b, j, p)),
                   pl.BlockSpec((1, 1, nq, 2 * LANES, tq), lambda b, p, j: (b, p, 0, 0, 0))],
        scratch_shapes=[pltpu.VMEM((tk, LANES), F32), pltpu.VMEM((tk, LANES), F32)],
        args=(q, do, lse, delta, k, v))


def _local_step(x, target, mod, wts, gains, rel_bias, ffn_shards=None):
    B, S, D = x.shape
    T = B * S
    sh1, sc1, g1, sh2, sc2, g2 = [mod[:, i * D:(i + 1) * D].reshape(B, 1, D) for i in range(N_MOD)]
    cs, sn = _rope_tables()
    buckets = np.stack([_band_buckets(d) for d in DILATIONS])
    buckets_dev = jnp.asarray(buckets)
    bias = _bias_tables(rel_bias, buckets_dev, "rel_bias_tables")
    w_in = wts["w_in"]

    h1 = _adaln_fwd(x, gains["g_norm1"], sc1, sh1, "adaln1_fwd")
    h1f = h1.reshape(T, D)
    qkv_v = _mm_qkv_views(h1, w_in[:, :P_QKV], "mm_qkv")
    rest = _mm(h1f, w_in[:, P_QKV:], "nn", F32, "mm_rest")
    o_d, lse_d = [], []
    for i, d in enumerate(DILATIONS):
        o_i, lse_i = _dil_fwd(qkv_v[i], bias, i, d, f"dil_fwd_{d}")
        o_d.append(o_i)
        lse_d.append(lse_i)
    out_a_v, lse_a_v = _dil_merge(o_d, lse_d, "dil_merge")
    out_a = out_a_v[0]
    cqn = _rms_fwd(rest, 1, Q_LORA, gains["g_cq"], "rms_cq_fwd")
    ckvn = _rms_fwd(rest, 0, KV_LORA, gains["g_ckv"], "rms_ckv_fwd")
    rest3 = rest.reshape(B, S, P_REST)
    q_raw = _mm(cqn, wts["w_uq"], "nn", F32, "mm_uq").reshape(B, S, N_HEADS * LANES)
    qc = _rope_apply(q_raw, cs, sn, BF16, "rope_q")
    kn_raw = _mm(ckvn, wts["w_kv"][:, :N_HEADS * LANES], "nn", F32, "mm_uk").reshape(B, S, N_HEADS * LANES)
    kc = _rope_apply(kn_raw, cs, sn, BF16, "rope_k", add=rest3, add_blk=KV_LORA // LANES)
    v = _mm(ckvn, wts["w_kv"][:, N_HEADS * LANES:], "nn", BF16, "mm_uv").reshape(B, S, D_B)
    vt = jnp.transpose(v.reshape(B, S // MLA_TK, MLA_TK, N_HEADS // 2, LANES), (0, 3, 1, 4, 2))
    (o_t, lse_b), got = _mla_fwd_t(qc, kc, vt, "mla_fwd", _GatherComm(ffn_shards) if ffn_shards else None)
    if ffn_shards:
        wts = dict(wts, w_ffn_in=got[0].reshape(N_CHIP, D, -1), w_ffn_out=got[1].reshape(D_FF, D))
    out_b = jnp.transpose(o_t, (0, 2, 1))
    out_af, out_bf = out_a.reshape(T, D_A), out_b.reshape(T, D_B)
    ya = _rms_fwd(out_af, 0, D_A, gains["g_out_a"], "rms_outa_fwd")
    yb = _rms_fwd(out_bf, 0, D_B, gains["g_out_b"], "rms_outb_fwd")
    y = jnp.concatenate([ya, yb], axis=1)
    mix = _mm(y, wts["w_out"], "nn", F32, "mm_out").reshape(B, S, D)
    h2, x1 = _adaln_fwd(x, gains["g_norm2"], sc2, sh2, "adaln2_fwd", mix=mix, gate=g1)
    h2f = h2.reshape(T, D)
    gu = _mm(h2f, wts["w_ffn_in"], "nn", F32, "mm_ffn_in", col_blocks=N_CHIP)
    act = _swiglu_fwd(gu, "swiglu_fwd")
    f = _mm(act, wts["w_ffn_out"], "nn", F32, "mm_ffn_out").reshape(B, S, D)
    dx2, df, dg2, dg_final, loss = _final_loss(x1, f, g2, gains["g_final"], target, "final_loss")

    dff = df.reshape(T, D)
    da = _mm(dff, wts["w_ffn_out"], "nt", F32, "mm_ffn_out_dx")
    gw_ffn_out = _mm(act, dff, "tn", F32, "mm_ffn_out_dw")
    dgu = _swiglu_bwd(da, gu, "swiglu_bwd")
    dh2 = _mm(dgu, wts["w_ffn_in"], "nt", F32, "mm_ffn_in_dx", col_blocks=N_CHIP).reshape(B, S, D)
    gw_ffn_in = _mm(h2f, dgu, "tn", F32, "mm_ffn_in_dw", col_blocks=N_CHIP)
    dx1, dsh2, dsc2, dg_norm2, dg1, dmix = _adaln_bwd(dh2, x1, gains["g_norm2"], sc2, dx2, "adaln2_bwd",
                                                      mix=mix, gate=g1)
    dmixf = dmix.reshape(T, D)
    dy = _mm(dmixf, wts["w_out"], "nt", F32, "mm_out_dx")
    gw_out = _mm(y, dmixf, "tn", F32, "mm_out_dw")
    do_a_v, dg_out_a = _rms_bwd_views(dy, 0, out_a, gains["g_out_a"], "rms_outa_bwd")
    do_b, dg_out_b = _rms_bwd(dy, 1, out_bf, 0, D_B, gains["g_out_b"], "rms_outb_bwd")
    do_b3 = do_b.reshape(B, S, D_B)
    delta_b = _mla_delta(do_b3, out_b, "mla_delta")
    ffn_a4 = None
    if ffn_shards:
        ffn_a4 = _rs_first([gw_ffn_in.reshape(N_DEV, -1, gw_ffn_in.shape[-1]), gw_ffn_out.reshape(N_DEV, -1, D)],
                           "ffn")
    (dkc, dv, dq_t), ffn_r2 = _mla_bwd_t(qc, kc, v, do_b3, lse_b, delta_b, "mla_bwd",
                                         _ToChipsComm(ffn_a4) if ffn_shards else None)
    dqc = jnp.transpose(dq_t, (0, 2, 4, 1, 3)).reshape(B, S, N_HEADS * LANES)
    dq_raw = _rope_apply(dqc, cs, -sn, BF16, "rope_q_bwd").reshape(T, N_HEADS * LANES)
    dkrw = _krope_bwd(dkc, cs, -sn, "rope_k_bwd").reshape(T, LANES)
    dcqn = _mm(dq_raw, wts["w_uq"], "nt", F32, "mm_uq_dx")
    gw_uq = _mm(cqn, dq_raw, "tn", F32, "mm_uq_dw")
    dkv = jnp.concatenate([dkc.reshape(T, -1), dv.reshape(T, -1)], axis=1).astype(BF16)
    dckvn = _mm(dkv, wts["w_kv"], "nt", F32, "mm_ukv_dx")
    gw_kv = _mm(ckvn, dkv, "tn", F32, "mm_ukv_dw")
    dcq, dg_cq = _rms_bwd(dcqn, 0, rest, 1, Q_LORA, gains["g_cq"], "rms_cq_bwd")
    dckv, dg_ckv = _rms_bwd(dckvn, 0, rest, 0, KV_LORA, gains["g_ckv"], "rms_ckv_bwd")
    dqkv_d, dbias_d = [], []
    for i, d in enumerate(DILATIONS):
        dqkv_i, dbias_i = _dil_bwd(qkv_v[i], do_a_v[i], out_a_v[i], lse_a_v[i], bias, i, d, f"dil_bwd_{d}")
        dqkv_d.append(dqkv_i)
        dbias_d.append(dbias_i)
    dqkv = _sum_views_bf16(dqkv_d, "dil_bwd_sum").reshape(T, P_QKV)
    g_rel_bias = _bias_grad(dbias_d, buckets_dev, "rel_bias_grad")[:, :N_BUCKETS].T
    dproj = jnp.concatenate([dqkv, dckv, dkrw, dcq], axis=1)
    dh1 = _mm(dproj, w_in, "nt", F32, "mm_in_dx").reshape(B, S, D)
    gw_in = _mm(h1f, dproj, "tn", F32, "mm_in_dw")
    grad_x, dsh1, dsc1, dg_norm1 = _adaln_bwd(dh1, x, gains["g_norm1"], sc1, dx1, "adaln1_bwd")
    gmod = jnp.concatenate([dsh1, dsc1, dg1, dsh2, dsc2, dg2], axis=-1).reshape(B, N_MOD * D)
    grads = dict(w_in=gw_in, w_uq=gw_uq, w_kv=gw_kv, w_out=gw_out, w_ffn_in=gw_ffn_in, w_ffn_out=gw_ffn_out,
                 g_norm1=dg_norm1, g_cq=dg_cq, g_ckv=dg_ckv, rel_bias=g_rel_bias, g_out_a=dg_out_a,
                 g_out_b=dg_out_b, g_norm2=dg_norm2, g_final=dg_final, ffn_pending=(ffn_a4, ffn_r2))
    return loss, grad_x, gmod, grads


def _w_in_to_kernel(w):
    z = lambda n: jnp.zeros((w.shape[0], n), w.dtype)
    i3, i4, i5 = 3 * D_A, 3 * D_A + Q_LORA, 3 * D_A + Q_LORA + KV_LORA
    return jnp.concatenate([w[:, :i3], w[:, i4:i5], z(NOPE_DIM), w[:, i5:], z(LANES - NOPE_DIM - ROPE_DIM),
                            w[:, i3:i4]], axis=1)


def _w_in_from_kernel(g):
    o = P_QKV + KV_LORA
    return jnp.concatenate([g[:, :P_QKV], g[:, o + LANES:], g[:, P_QKV:o],
                            g[:, o + NOPE_DIM:o + NOPE_DIM + ROPE_DIM]], axis=1)


def _w_uq_to_kernel(w):
    w3 = w.reshape(Q_LORA, N_HEADS, NOPE_DIM + ROPE_DIM)
    return jnp.pad(w3, ((0, 0), (0, 0), (0, LANES - NOPE_DIM - ROPE_DIM))).reshape(Q_LORA, N_HEADS * LANES)


def _w_uq_from_kernel(g):
    return g.reshape(Q_LORA, N_HEADS, LANES)[:, :, :NOPE_DIM + ROPE_DIM].reshape(Q_LORA, -1)


def _w_ukv_to_kernel(w):
    w3 = w.reshape(KV_LORA, N_HEADS, 2 * HEAD_DIM)
    wk = jnp.pad(w3[:, :, :NOPE_DIM], ((0, 0), (0, 0), (0, LANES - NOPE_DIM))).reshape(KV_LORA, N_HEADS * LANES)
    wv = w3[:, :, NOPE_DIM:].reshape(KV_LORA, D_B)
    return jnp.concatenate([wk, wv], axis=1)


def _w_ukv_from_kernel(g):
    gk = g[:, :N_HEADS * LANES].reshape(KV_LORA, N_HEADS, LANES)[:, :, :NOPE_DIM]
    gv = g[:, N_HEADS * LANES:].reshape(KV_LORA, N_HEADS, HEAD_DIM)
    return jnp.concatenate([gk, gv], axis=2).reshape(KV_LORA, -1)


MESH = pl.DeviceIdType.MESH


def _my_place():
    return lax.axis_index("x"), lax.axis_index("y"), lax.axis_index("c")


def _other_chips(x, y):
    return [(1 - x, y), (x, 1 - y), (1 - x, 1 - y)]


def _allgather8(x_shard, name, in_hbm):
    m_per, n = x_shard.shape
    space = pl.ANY if in_hbm else pltpu.VMEM

    def body(x_ref, out_ref, send_sems, recv_sems, local_sem):
        x, y, c = _my_place()
        me, sibling = (x, y, c), (x, y, 1 - c)
        chips = _other_chips(x, y)

        def rows(px, py, pc):
            return out_ref.at[pl.ds((4 * px + 2 * py + pc) * m_per, m_per), :]

        def copy(k, block, to, src=None):
            return pltpu.make_async_remote_copy(
                src_ref=rows(*block) if src is None else src, dst_ref=rows(*block),
                send_sem=send_sems.at[k], recv_sem=recv_sems.at[k], device_id=to, device_id_type=MESH)

        mine = pltpu.make_async_copy(x_ref, rows(*me), local_sem)
        mine.start()
        first = [copy(0, me, sibling, src=x_ref)]
        first += [copy(1 + j, me, (*chip, c), src=x_ref) for j, chip in enumerate(chips)]
        for cp in first:
            cp.start()
        passed = [copy(4 + j, (*chip, c), sibling) for j, chip in enumerate(chips)]
        for j, chip in enumerate(chips):
            copy(1 + j, (*chip, c), me).wait_recv()
            passed[j].start()
        copy(0, sibling, me).wait_recv()
        for j, chip in enumerate(chips):
            copy(4 + j, (*chip, 1 - c), me).wait_recv()
        for cp in first + passed:
            cp.wait_send()
        mine.wait()

    return pl.pallas_call(
        body, name=name,
        out_shape=jax.ShapeDtypeStruct((N_DEV * m_per, n), x_shard.dtype),
        in_specs=[pl.BlockSpec(memory_space=space)],
        out_specs=pl.BlockSpec(memory_space=space),
        scratch_shapes=[pltpu.SemaphoreType.DMA((7,)), pltpu.SemaphoreType.DMA((7,)), pltpu.SemaphoreType.DMA],
        compiler_params=pltpu.CompilerParams(vmem_limit_bytes=VMEM_LIMIT),
    )(x_shard)


def _hbm_specs(n):
    return [pl.BlockSpec(memory_space=pl.ANY)] * n


class _GatherComm:
    def __init__(self, shards):
        self.n = n = len(shards)
        self.inputs = [s.reshape(2, s.shape[0] // 2, s.shape[1]) for s in shards]
        self.out_shape = [jax.ShapeDtypeStruct((N_DEV,) + s.shape[1:], s.dtype) for s in self.inputs]
        self.scratch = [pltpu.SemaphoreType.DMA((7 * n,)), pltpu.SemaphoreType.DMA((7 * n,))]

    def _parts(self, xs, outs, sems):
        send_sems, recv_sems = sems
        x, y, c = _my_place()

        def blk(k, px, py, pc):
            return outs[k].at[4 * px + 2 * py + pc]

        def copy(k, kind, block, to, own=False):
            return pltpu.make_async_remote_copy(
                src_ref=xs[k].at[c] if own else blk(k, *block), dst_ref=blk(k, *block),
                send_sem=send_sems.at[7 * k + kind], recv_sem=recv_sems.at[7 * k + kind],
                device_id=to, device_id_type=MESH)

        def whole(k):
            return pltpu.make_async_remote_copy(
                src_ref=xs[k], dst_ref=outs[k].at[pl.ds(4 * x + 2 * y, 2)],
                send_sem=send_sems.at[7 * k], recv_sem=recv_sems.at[7 * k],
                device_id=(x, y, 1 - c), device_id_type=MESH)

        me, sibling = (x, y, c), (x, y, 1 - c)
        chips = _other_chips(x, y)
        first = []
        for k in range(self.n):
            first.append(whole(k))
            first += [copy(k, 1 + j, me, (*chip, c), own=True) for j, chip in enumerate(chips)]
        return copy, whole, me, sibling, chips, c, first

    def start(self, xs, outs, sems):
        for cp in self._parts(xs, outs, sems)[-1]:
            cp.start()

    def finish(self, xs, outs, sems):
        copy, whole, me, sibling, chips, c, first = self._parts(xs, outs, sems)
        passed = []
        for j, chip in enumerate(chips):
            for k in range(self.n):
                copy(k, 1 + j, (*chip, c), me).wait_recv()
                fwd = copy(k, 4 + j, (*chip, c), sibling)
                fwd.start()
                passed.append(fwd)
        for k in range(self.n):
            whole(k).wait_recv()
        for j, chip in enumerate(chips):
            for k in range(self.n):
                copy(k, 4 + j, (*chip, 1 - c), me).wait_recv()
        for cp in first + passed:
            cp.wait_send()


class _ToChipsComm:
    def __init__(self, a4s):
        self.inputs = list(a4s)
        self.n = n = len(a4s)
        nc = N_CHIP - 1
        self.out_shape = [jax.ShapeDtypeStruct((nc,) + a.shape[1:], a.dtype) for a in a4s]
        self.scratch = [pltpu.SemaphoreType.DMA((nc * n,)), pltpu.SemaphoreType.DMA((nc * n,))]

    def _copies(self, as_, rs, sems):
        send_sems, recv_sems = sems
        x, y, c = _my_place()
        nc = N_CHIP - 1
        return [pltpu.make_async_remote_copy(
            src_ref=as_[k].at[2 * cx + cy], dst_ref=rs[k].at[j], send_sem=send_sems.at[nc * k + j],
            recv_sem=recv_sems.at[nc * k + j], device_id=(cx, cy, c), device_id_type=MESH)
            for k in range(self.n) for j, (cx, cy) in enumerate(_other_chips(x, y))]

    def start(self, as_, rs, sems):
        for cp in self._copies(as_, rs, sems):
            cp.start()

    def finish(self, as_, rs, sems):
        for cp in self._copies(as_, rs, sems):
            cp.wait()


def _run_comm(comm, name):
    n = comm.n

    def body(*refs):
        ins, outs, sems = refs[:n], refs[n:2 * n], refs[2 * n:]
        comm.start(ins, outs, sems)
        comm.finish(ins, outs, sems)

    return pl.pallas_call(
        body, name=name, out_shape=comm.out_shape, in_specs=_hbm_specs(n), out_specs=_hbm_specs(n),
        scratch_shapes=comm.scratch,
    )(*comm.inputs)


def _gather_weights(shards, name):
    return _run_comm(_GatherComm(shards), name)


def _rs_to_sibling(g8s, name):
    n = len(g8s)

    def body(*refs):
        gs, rs = refs[:n], refs[n:2 * n]
        send_sems, recv_sems = refs[2 * n:]
        x, y, c = _my_place()
        copies = [pltpu.make_async_remote_copy(
            src_ref=gs[k].at[2 * s + 1 - c], dst_ref=rs[k].at[s], send_sem=send_sems.at[N_CHIP * k + s],
            recv_sem=recv_sems.at[N_CHIP * k + s], device_id=(x, y, 1 - c), device_id_type=MESH)
            for k in range(n) for s in range(N_CHIP)]
        for cp in copies:
            cp.start()
        for cp in copies:
            cp.wait()

    return pl.pallas_call(
        body, name=name,
        out_shape=[jax.ShapeDtypeStruct((N_CHIP,) + g.shape[1:], g.dtype) for g in g8s],
        in_specs=_hbm_specs(n), out_specs=_hbm_specs(n),
        scratch_shapes=[pltpu.SemaphoreType.DMA((N_CHIP * n,)), pltpu.SemaphoreType.DMA((N_CHIP * n,))],
    )(*g8s)


def _rs_to_chips(a4s, name):
    return _run_comm(_ToChipsComm(a4s), name)


def _swap_halves(hs, name):
    n = len(hs)

    def body(*refs):
        o_refs = refs[n:2 * n]
        send_sems, recv_sems = refs[2 * n:]
        x, y, c = _my_place()

        def remote(k, slot):
            return pltpu.make_async_remote_copy(
                src_ref=o_refs[k].at[slot], dst_ref=o_refs[k].at[slot], send_sem=send_sems.at[k],
                recv_sem=recv_sems.at[k], device_id=(x, y, 1 - c), device_id_type=MESH)

        sends = [remote(k, c) for k in range(n)]
        for cp in sends:
            cp.start()
        for k in range(n):
            remote(k, 1 - c).wait_recv()
        for cp in sends:
            cp.wait_send()

    return pl.pallas_call(
        body, name=name,
        out_shape=[jax.ShapeDtypeStruct(h.shape, h.dtype) for h in hs],
        in_specs=_hbm_specs(n), out_specs=_hbm_specs(n),
        input_output_aliases={k: k for k in range(n)},
        scratch_shapes=[pltpu.SemaphoreType.DMA((n,)), pltpu.SemaphoreType.DMA((n,))],
    )(*hs)


ADD_TILES = 4


def _add_blocks(a_list, a_idx_fn, others_list, ns, sel, name, out_blocks=None, out_idx_fn=None):
    out_blocks = out_blocks or ns
    out_idx_fn = out_idx_fn or (lambda s, sel_ref: s)
    n = len(a_list)
    n_o = len(others_list[0])
    per = 1 + n_o

    def body(sel_ref, *refs):
        for k in range(n):
            ins = refs[k * per:(k + 1) * per]
            o_ref = refs[n * per + k]
            acc = ins[0][0]
            for r in ins[1:]:
                acc = acc + r[0]
            o_ref[0] = acc

    in_specs, args, out_specs, out_shape = [], [], [], []
    for a, others in zip(a_list, others_list):
        _, R, N = a.shape
        tr = R // ADD_TILES
        assert tr % 8 == 0, a.shape
        in_specs.append(pl.BlockSpec((1, tr, N), lambda s, i, sel_ref: (a_idx_fn(s, sel_ref), i, 0)))
        args.append(a)
        for arr, fixed in others:
            if fixed is None:
                in_specs.append(pl.BlockSpec((1, tr, N), lambda s, i, sel_ref: (s, i, 0)))
            else:
                in_specs.append(pl.BlockSpec((1, tr, N), lambda s, i, sel_ref, fixed=fixed: (fixed, i, 0)))
            args.append(arr)
        out_specs.append(pl.BlockSpec((1, tr, N), lambda s, i, sel_ref: (out_idx_fn(s, sel_ref), i, 0)))
        out_shape.append(jax.ShapeDtypeStruct((out_blocks, R, N), a.dtype))
    grid_spec = pltpu.PrefetchScalarGridSpec(num_scalar_prefetch=1, grid=(ns, ADD_TILES), in_specs=in_specs,
                                             out_specs=out_specs)
    return pl.pallas_call(
        body, name=name, out_shape=out_shape, grid_spec=grid_spec,
        compiler_params=_cparams(("parallel", "parallel")),
    )(sel, *args)


def _rs_first(g8s, tag):
    c_sel = jnp.reshape(lax.axis_index("c"), (1,)).astype(jnp.int32)
    r1 = _rs_to_sibling(g8s, f"rs_to_sibling_{tag}")
    return _add_blocks(g8s, lambda s, sel: 2 * s + sel[0], [[(r, None)] for r in r1], N_CHIP, c_sel,
                       f"rs_add_sibling_{tag}")


def _rs_last(a4s, r2s, tag):
    sel = jnp.stack([2 * lax.axis_index("x") + lax.axis_index("y"), lax.axis_index("c")]).astype(jnp.int32)
    h = _add_blocks(a4s, lambda s, sel: sel[0], [[(r, 0), (r, 1), (r, 2)] for r in r2s], 1, sel,
                    f"rs_add_chips_{tag}", out_blocks=2, out_idx_fn=lambda s, sel: sel[1])
    full = _swap_halves(h, f"rs_swap_halves_{tag}")
    return [f.reshape(2 * f.shape[1], f.shape[2]) for f in full]


def _reduce_scatter(g8s, tag):
    a4 = _rs_first(g8s, tag)
    return _rs_last(a4, _rs_to_chips(a4, f"rs_to_chips_{tag}"), tag)


def _ada_fwd(c_all, w_ada, b_ada, name):
    nb, D = c_all.shape
    ncol = w_ada.shape[1]
    tc = 512

    def body(c_ref, w_ref, b_ref, o_ref):
        cv = c_ref[...]
        cond = (cv * jax.nn.sigmoid(cv)).astype(BF16)
        o_ref[...] = jnp.dot(cond, w_ref[...].astype(BF16), preferred_element_type=F32) + b_ref[...]

    return pl.pallas_call(
        body, name=name, out_shape=jax.ShapeDtypeStruct((nb, ncol), F32), grid=(ncol // tc,),
        in_specs=[pl.BlockSpec((nb, D), lambda j: (0, 0)), pl.BlockSpec((D, tc), lambda j: (0, j)),
                  pl.BlockSpec((1, tc), lambda j: (0, j))],
        out_specs=pl.BlockSpec((nb, tc), lambda j: (0, j)),
        compiler_params=_cparams(("parallel",)),
    )(c_all, w_ada, b_ada)


def _ada_bwd(c_all, gmod_cols, name):
    nb, D = c_all.shape
    ncol = gmod_cols.shape[1]
    tc = 512

    def body(c_ref, g_ref, o_ref):
        cv = c_ref[...]
        cond = (cv * jax.nn.sigmoid(cv)).astype(BF16)
        o_ref[...] = _dot_tn(cond, g_ref[...].astype(BF16))

    return pl.pallas_call(
        body, name=name, out_shape=jax.ShapeDtypeStruct((D, ncol), F32), grid=(ncol // tc,),
        in_specs=[pl.BlockSpec((nb, D), lambda j: (0, 0)), pl.BlockSpec((nb, tc), lambda j: (0, j))],
        out_specs=pl.BlockSpec((D, tc), lambda j: (0, j)),
        compiler_params=_cparams(("parallel",)),
    )(c_all, gmod_cols)


def _adam_math(w, g, m, v):
    m = ADAM_B1 * m + (1.0 - ADAM_B1) * g
    v = ADAM_B2 * v + (1.0 - ADAM_B2) * (g * g)
    m_hat = m / (1.0 - ADAM_B1 ** ADAM_STEP)
    v_hat = v / (1.0 - ADAM_B2 ** ADAM_STEP)
    delta = -ADAM_LR * (m_hat / (jnp.sqrt(v_hat) + ADAM_EPS) + ADAM_WD * w)
    return delta, m, v


def _adamw(w, g, m, v, name):
    rows, cols = w.shape
    tr = _pick(rows, (256, 192, 176, 128, 64, 8))

    def body(w_ref, g_ref, m_ref, v_ref, d_ref, mo_ref, vo_ref):
        d, mn, vn = _adam_math(w_ref[...], g_ref[...], m_ref[...], v_ref[...])
        d_ref[...] = d
        mo_ref[...] = mn
        vo_ref[...] = vn

    spec = pl.BlockSpec((tr, cols), lambda i: (i, 0))
    return pl.pallas_call(
        body, name=name, out_shape=[jax.ShapeDtypeStruct((rows, cols), F32)] * 3, grid=(rows // tr,),
        in_specs=[spec] * 4, out_specs=[spec] * 3, compiler_params=_cparams(("parallel",)),
    )(w, g, m, v)


VEC_ROWS = 8


def _adamw_rows(w, parts, m, v, name):
    n = w.shape[1]
    P = parts.shape[0]
    assert n % (VEC_ROWS * LANES) == 0, n
    shp = (VEC_ROWS, n // VEC_ROWS)

    def body(w_ref, p_ref, m_ref, v_ref, g_ref, d_ref, mo_ref, vo_ref):
        g = p_ref[0]
        for k in range(1, P):
            g = g + p_ref[k]
        d, mn, vn = _adam_math(w_ref[...], g, m_ref[...], v_ref[...])
        g_ref[...] = g
        d_ref[...] = d
        mo_ref[...] = mn
        vo_ref[...] = vn

    vec = pl.BlockSpec(shp, lambda i: (0, 0))
    out = pl.pallas_call(
        body, name=name, out_shape=[jax.ShapeDtypeStruct(shp, F32)] * 4, grid=(1,),
        in_specs=[vec, pl.BlockSpec((P,) + shp, lambda i: (0, 0, 0)), vec, vec], out_specs=[vec] * 4,
        compiler_params=_cparams(("arbitrary",)),
    )(w.reshape(shp), parts.reshape((P,) + shp), m.reshape(shp), v.reshape(shp))
    return [o.reshape(1, n) for o in out]


_PACKED = (("w_in", 1024, 552), ("w_uq", 384, 192), ("w_ukv", 256, 256))
_SHARDED = ("w_in", "w_uq", "w_ukv", "w_out", "w_ffn_in", "w_ffn_out")
_SMALL = (("g_norm1", 1024), ("g_cq", 384), ("g_ckv", 256), ("rel_bias", 256), ("g_out_a", 512),
          ("g_out_b", 512), ("g_norm2", 1024), ("g_final", 1024))
_SMALL_PAD = 5120
PACK_ROWS = 704
_PACK_ELEMS = PACK_ROWS * D_MODEL


def _pack_shards(shards, dtype):
    lead = shards["w_in"].shape[:-2]
    flat = jnp.concatenate([shards[n].astype(dtype).reshape(lead + (-1,)) for n, _, _ in _PACKED], axis=-1)
    pad = [(0, 0)] * len(lead) + [(0, _PACK_ELEMS - flat.shape[-1])]
    return jnp.pad(flat, pad).reshape(lead + (PACK_ROWS, D_MODEL))


def _unpack_shards(packed):
    out, off = {}, 0
    for n, r, c in _PACKED:
        out[n] = packed[..., off:off + r * c].reshape(packed.shape[:-1] + (r, c))
        off += r * c
    return out


def _full_from_shards(sh):
    return jnp.transpose(sh, (1, 0, 2)).reshape(sh.shape[1], -1)


def _shards_from_full(full):
    rows, cols = full.shape
    return jnp.transpose(full.reshape(rows, N_CHIP, cols // N_CHIP), (1, 0, 2))


def kernel(x, c, w_ada, b_ada, g_norm1, w_in, g_cq, w_uq, g_ckv, w_ukv, rel_bias, g_out_a, g_out_b, w_out, g_norm2, w_ffn_in, w_ffn_out, g_final, loss_target, m_w_ada, m_b_ada, m_g_norm1, m_w_in, m_g_cq, m_w_uq, m_g_ckv, m_w_ukv, m_rel_bias, m_g_out_a, m_g_out_b, m_w_out, m_g_norm2, m_w_ffn_in, m_w_ffn_out, m_g_final, v_w_ada, v_b_ada, v_g_norm1, v_w_in, v_g_cq, v_w_uq, v_g_ckv, v_w_ukv, v_rel_bias, v_g_out_a, v_g_out_b, v_w_out, v_g_norm2, v_w_ffn_in, v_w_ffn_out, v_g_final):
    names = ["w_ada", "b_ada", "g_norm1", "w_in", "g_cq", "w_uq", "g_ckv", "w_ukv", "rel_bias", "g_out_a",
             "g_out_b", "w_out", "g_norm2", "w_ffn_in", "w_ffn_out", "g_final"]
    W = dict(zip(names, [w_ada, b_ada, g_norm1, w_in, g_cq, w_uq, g_ckv, w_ukv, rel_bias, g_out_a, g_out_b,
                         w_out, g_norm2, w_ffn_in, w_ffn_out, g_final]))
    M = dict(zip(names, [m_w_ada, m_b_ada, m_g_norm1, m_w_in, m_g_cq, m_w_uq, m_g_ckv, m_w_ukv, m_rel_bias,
                         m_g_out_a, m_g_out_b, m_w_out, m_g_norm2, m_w_ffn_in, m_w_ffn_out, m_g_final]))
    V = dict(zip(names, [v_w_ada, v_b_ada, v_g_norm1, v_w_in, v_g_cq, v_w_uq, v_g_ckv, v_w_ukv, v_rel_bias,
                         v_g_out_a, v_g_out_b, v_w_out, v_g_norm2, v_w_ffn_in, v_w_ffn_out, v_g_final]))
    B, S, D = x.shape
    mx, my, mc = _my_place()
    dev = 4 * mx + 2 * my + mc
    chip = 2 * mx + my
    pad_rows = 8

    c_all = _allgather8(jnp.pad(c, ((0, pad_rows - B), (0, 0))), "ag_c", False)
    c_all = c_all.reshape(N_DEV, pad_rows, D)[:, :B].reshape(N_DEV * B, D)
    ada_cols = w_ada.shape[-1]
    b_cols = lax.dynamic_slice_in_dim(b_ada, chip * ada_cols, ada_cols, axis=1)
    mod_cols = _ada_fwd(c_all, w_ada[0], b_cols, "ada_fwd")
    mod_all = _allgather8(mod_cols, "ag_mod", False).reshape(N_DEV, N_DEV * B, ada_cols)[0::2]
    mod_all = jnp.transpose(mod_all, (1, 0, 2)).reshape(N_DEV * B, N_MOD * D)
    mod = lax.dynamic_slice_in_dim(mod_all, dev * B, B, axis=0)

    packed = _pack_shards({n: W[n][0] for n, _, _ in _PACKED}, BF16)
    g_packed, g_out = _gather_weights([packed, w_out[0].astype(BF16)], "ag_weights")
    full = {n: _full_from_shards(sh) for n, sh in _unpack_shards(g_packed.reshape(N_CHIP, _PACK_ELEMS)).items()}
    wts = dict(w_in=_w_in_to_kernel(full["w_in"]), w_uq=_w_uq_to_kernel(full["w_uq"]),
               w_kv=_w_ukv_to_kernel(full["w_ukv"]), w_out=g_out.reshape(D, D))
    gains = dict(g_norm1=g_norm1, g_cq=g_cq, g_ckv=g_ckv, g_out_a=g_out_a, g_out_b=g_out_b, g_norm2=g_norm2,
                 g_final=g_final.reshape(1, D))

    loss, grad_x, gmod, grads = _local_step(x, loss_target, mod, wts, gains, rel_bias,
                                            ffn_shards=[w_ffn_in[0].astype(BF16), w_ffn_out[0].astype(BF16)])
    loss = lax.psum(loss[0, 0], ("x", "y", "c"))

    n_small = _SMALL_PAD
    cat = lambda dct: jnp.concatenate([dct[n].reshape(1, -1) for n, _ in _SMALL]
                                      + [jnp.zeros((1, _SMALL_PAD - sum(s for _, s in _SMALL)), F32)], axis=1)
    small = cat(grads)
    rows = jnp.concatenate([gmod, jnp.pad(small, ((0, 0), (0, N_MOD * D - n_small))),
                            jnp.zeros((pad_rows - B - 1, N_MOD * D), F32)], axis=0)
    rows_all = _allgather8(rows, "ag_small", False).reshape(N_DEV, pad_rows, N_MOD * D)
    gmod_all = rows_all[:, :B].reshape(N_DEV * B, N_MOD * D)
    small_parts = rows_all[:, B, :n_small]

    nat = dict(w_in=_w_in_from_kernel(grads["w_in"]), w_uq=_w_uq_from_kernel(grads["w_uq"]),
               w_ukv=_w_ukv_from_kernel(grads["w_kv"]))
    gp = _pack_shards({n: _shards_from_full(nat[n]) for n, _, _ in _PACKED}, F32)
    as_halves = lambda a: a.reshape(N_DEV, -1, a.shape[-1])
    a4 = _rs_first([as_halves(gp), as_halves(grads["w_out"])], "mix")
    r2 = _rs_to_chips(a4, "rs_to_chips_mix")
    ffn_a4, ffn_r2 = grads["ffn_pending"]
    r_packed, r_out, r_ffn_in, r_ffn_out = _rs_last(list(a4) + list(ffn_a4), list(r2) + list(ffn_r2), "all")
    G = _unpack_shards(r_packed.reshape(_PACK_ELEMS))
    G.update(w_out=r_out, w_ffn_in=r_ffn_in, w_ffn_out=r_ffn_out)

    gmod_cols = lax.dynamic_slice_in_dim(gmod_all, chip * ada_cols, ada_cols, axis=1)
    G["w_ada"] = _ada_bwd(c_all, gmod_cols, "ada_bwd")
    delta, new_m, new_v = {}, {}, {}
    for n in ("w_ada",) + _SHARDED:
        shp = W[n].shape
        w2 = W[n].reshape(shp[-2], shp[-1])
        d_, m_, v_ = _adamw(w2, G[n], M[n].reshape(w2.shape), V[n].reshape(w2.shape), f"adamw_{n}")
        G[n], delta[n], new_m[n], new_v[n] = [a.reshape(shp) for a in (G[n], d_, m_, v_)]
    gs, ds_, ms_, vs_ = _adamw_rows(cat(W), small_parts, cat(M), cat(V), "adamw_small")
    off = 0
    for n, sz in _SMALL:
        shp = W[n].shape
        G[n], delta[n], new_m[n], new_v[n] = [a[:, off:off + sz].reshape(shp) for a in (gs, ds_, ms_, vs_)]
        off += sz
    G["b_ada"], delta["b_ada"], new_m["b_ada"], new_v["b_ada"] = _adamw_rows(b_ada, gmod_all, m_b_ada, v_b_ada,
                                                                          "adamw_b_ada")
    return (loss, grad_x, *[G[n] for n in names], *[delta[n] for n in names], *[new_m[n] for n in names],
            *[new_v[n] for n in names])
```

```python
import functools
import math

import numpy as np
import jax
import jax.numpy as jnp
from jax import lax
from jax.experimental import pallas as pl
from jax.experimental.pallas import tpu as pltpu

F32 = jnp.float32
BF16 = jnp.bfloat16

D_MODEL = 1024
SEQ = 2048
N_HEADS = 8
HEAD_DIM = 64
D_A = 512
D_B = 512
Q_LORA = 384
KV_LORA = 256
ROPE_DIM = 32
NOPE_DIM = 64
D_FF = 2816
N_MOD = 6
N_BUCKETS = 32
MAX_DISTANCE = 2048
ROPE_THETA = 10000.0
EPS = 1e-6
NEG = -1e30
BLK = 128
DILATIONS = (1, 4, 16)
SPAN = 128
MLA_SCALE = (NOPE_DIM + ROPE_DIM) ** -0.5
DIL_SCALE = HEAD_DIM ** -0.5

ADAM_LR = 0.001
ADAM_B1 = 0.9
ADAM_B2 = 0.999
ADAM_EPS = 1e-08
ADAM_WD = 0.01
ADAM_STEP = 10

N_DEV = 8
N_CHIP = 4
LANES = 128
VMEM_LIMIT = 48 * 1024 * 1024

P_QKV = 3 * D_A
P_REST = KV_LORA + LANES + Q_LORA


def _cparams(sem=None):
    return pltpu.CompilerParams(dimension_semantics=sem, vmem_limit_bytes=VMEM_LIMIT)


def _pick(n, cands):
    for c in cands:
        if n % c == 0:
            return c
    raise ValueError(f"no tile for {n} in {cands}")


def _mm(a, b, mode, out_dtype, name, col_blocks=None):
    blocked = col_blocks is not None
    if mode == "nn":
        (M, K) = a.shape
        K2, N = (b.shape[1], b.shape[0] * b.shape[2]) if blocked else b.shape
    elif mode == "nt":
        (M, K) = a.shape
        N, K2 = (b.shape[1], b.shape[0] * b.shape[2]) if blocked else b.shape
    else:
        (K, M), (K2, N) = a.shape, b.shape
    assert K == K2, (a.shape, b.shape, mode)
    tm = _pick(M, (512, 384, 256, 128))
    tn = _pick(N, (1408, 1024, 768, 512, 384, 256, 128))
    tk = _pick(K, (1024, 512, 384, 256, 128))
    if blocked and mode == "nt":
        tk = K // col_blocks
    elif blocked:
        tn = N // col_blocks
    nk = K // tk
    out_shape = (M, N)
    out_spec = pl.BlockSpec((tm, tn), lambda i, j, k: (i, j))
    if mode == "nn":
        a_spec = pl.BlockSpec((tm, tk), lambda i, j, k: (i, k))
        b_spec = (pl.BlockSpec((None, tk, tn), lambda i, j, k: (j, k, 0)) if blocked
                  else pl.BlockSpec((tk, tn), lambda i, j, k: (k, j)))
        dn = (((1,), (0,)), ((), ()))
    elif mode == "nt":
        a_spec = pl.BlockSpec((tm, tk), lambda i, j, k: (i, k))
        b_spec = (pl.BlockSpec((None, tn, tk), lambda i, j, k: (k, j, 0)) if blocked
                  else pl.BlockSpec((tn, tk), lambda i, j, k: (j, k)))
        dn = (((1,), (1,)), ((), ()))
    else:
        a_spec = pl.BlockSpec((tk, tm), lambda i, j, k: (k, i))
        b_spec = pl.BlockSpec((tk, tn), lambda i, j, k: (k, j))
        dn = (((0,), (0,)), ((), ()))
        if blocked:
            out_shape = (col_blocks, M, tn)
            out_spec = pl.BlockSpec((None, tm, tn), lambda i, j, k: (j, i, 0))

    def body(a_ref, b_ref, o_ref, acc_ref):
        k = pl.program_id(2)

        @pl.when(k == 0)
        def _():
            acc_ref[...] = jnp.zeros_like(acc_ref)

        acc_ref[...] += lax.dot_general(a_ref[...].astype(BF16), b_ref[...].astype(BF16), dn,
                                        preferred_element_type=F32)

        @pl.when(k == nk - 1)
        def _():
            o_ref[...] = acc_ref[...].astype(o_ref.dtype)

    return pl.pallas_call(
        body, name=name,
        out_shape=jax.ShapeDtypeStruct(out_shape, out_dtype),
        grid=(M // tm, N // tn, nk),
        in_specs=[a_spec, b_spec],
        out_specs=out_spec,
        scratch_shapes=[pltpu.VMEM((tm, tn), F32)],
        compiler_params=_cparams(("parallel", "parallel", "arbitrary")),
    )(a, b)


ROW_TILE = 256


def _adaln_fwd(x, g, sc, sh, name, mix=None, gate=None):
    B, S, D = x.shape
    ts = ROW_TILE
    has_res = mix is not None

    def body(*refs):
        if has_res:
            x_ref, g_ref, sc_ref, sh_ref, mix_ref, gate_ref, h_ref, xr_ref = refs
            xr = x_ref[0] + gate_ref[0] * mix_ref[0]
            xr_ref[0] = xr
        else:
            x_ref, g_ref, sc_ref, sh_ref, h_ref = refs
            xr = x_ref[0]
        r = lax.rsqrt(jnp.mean(xr * xr, axis=-1, keepdims=True) + EPS)
        xn = (xr * r) * g_ref[...]
        h_ref[0] = (xn * (1.0 + sc_ref[0]) + sh_ref[0]).astype(h_ref.dtype)

    tok = pl.BlockSpec((1, ts, D), lambda b, s: (b, s, 0))
    per_b = pl.BlockSpec((1, 1, D), lambda b, s: (b, 0, 0))
    vec = pl.BlockSpec((1, D), lambda b, s: (0, 0))
    in_specs = [tok, vec, per_b, per_b]
    args = [x, g, sc, sh]
    out_shape = [jax.ShapeDtypeStruct((B, S, D), BF16)]
    out_specs = [tok]
    if has_res:
        in_specs += [tok, per_b]
        args += [mix, gate]
        out_shape.append(jax.ShapeDtypeStruct((B, S, D), F32))
        out_specs.append(tok)
    out = pl.pallas_call(
        body, name=name, out_shape=out_shape, grid=(B, S // ts),
        in_specs=in_specs, out_specs=out_specs,
        compiler_params=_cparams(("parallel", "parallel")),
    )(*args)
    return out if has_res else out[0]


def _adaln_bwd(dh, x, g, sc, dres, name, mix=None, gate=None):
    B, S, D = x.shape
    ts = ROW_TILE
    has_res = mix is not None

    def body(*refs):
        if has_res:
            (dh_ref, x_ref, g_ref, sc_ref, dres_ref, mix_ref, gate_ref,
             dx_ref, dsh_ref, dsc_ref, dg_ref, dgate_ref, dmix_ref) = refs
        else:
            (dh_ref, x_ref, g_ref, sc_ref, dres_ref, dx_ref, dsh_ref, dsc_ref, dg_ref) = refs
        b, s = pl.program_id(0), pl.program_id(1)
        xv = x_ref[0]
        dhv = dh_ref[0]
        gv = g_ref[...]
        r = lax.rsqrt(jnp.mean(xv * xv, axis=-1, keepdims=True) + EPS)
        n = xv * r
        xn = n * gv
        dxn = dhv * (1.0 + sc_ref[0])
        dn = dxn * gv
        dx = r * (dn - n * jnp.mean(dn * n, axis=-1, keepdims=True)) + dres_ref[0]
        dx_ref[0] = dx

        @pl.when(s == 0)
        def _():
            dsh_ref[...] = jnp.zeros_like(dsh_ref)
            dsc_ref[...] = jnp.zeros_like(dsc_ref)
            if has_res:
                dgate_ref[...] = jnp.zeros_like(dgate_ref)

        @pl.when((s == 0) & (b == 0))
        def _():
            dg_ref[...] = jnp.zeros_like(dg_ref)

        dsh_ref[0] += jnp.sum(dhv, axis=0, keepdims=True)
        dsc_ref[0] += jnp.sum(dhv * xn, axis=0, keepdims=True)
        dg_ref[...] += jnp.sum(dxn * n, axis=0, keepdims=True)
        if has_res:
            dgate_ref[0] += jnp.sum(dx * mix_ref[0], axis=0, keepdims=True)
            dmix_ref[0] = (dx * gate_ref[0]).astype(dmix_ref.dtype)

    tok = pl.BlockSpec((1, ts, D), lambda b, s: (b, s, 0))
    per_b = pl.BlockSpec((1, 1, D), lambda b, s: (b, 0, 0))
    vec = pl.BlockSpec((1, D), lambda b, s: (0, 0))
    in_specs = [tok, tok, vec, per_b, tok]
    args = [dh, x, g, sc, dres]
    out_shape = [jax.ShapeDtypeStruct((B, S, D), F32), jax.ShapeDtypeStruct((B, 1, D), F32),
                 jax.ShapeDtypeStruct((B, 1, D), F32), jax.ShapeDtypeStruct((1, D), F32)]
    out_specs = [tok, per_b, per_b, vec]
    if has_res:
        in_specs += [tok, per_b]
        args += [mix, gate]
        out_shape += [jax.ShapeDtypeStruct((B, 1, D), F32), jax.ShapeDtypeStruct((B, S, D), BF16)]
        out_specs += [per_b, tok]
    return pl.pallas_call(
        body, name=name, out_shape=out_shape, grid=(B, S // ts),
        in_specs=in_specs, out_specs=out_specs,
        compiler_params=_cparams(("arbitrary", "arbitrary")),
    )(*args)


def _rms_fwd(x, col_blk, n, g, name, n_real=None):
    T = x.shape[0]
    tr = 512
    nr = float(n_real or n)

    def body(x_ref, g_ref, y_ref):
        xv = x_ref[...]
        r = lax.rsqrt(jnp.sum(xv * xv, axis=-1, keepdims=True) / nr + EPS)
        y_ref[...] = ((xv * r) * g_ref[...]).astype(y_ref.dtype)

    return pl.pallas_call(
        body, name=name, out_shape=jax.ShapeDtypeStruct((T, n), BF16), grid=(T // tr,),
        in_specs=[pl.BlockSpec((tr, n), lambda i: (i, col_blk)), pl.BlockSpec((1, n), lambda i: (0, 0))],
        out_specs=pl.BlockSpec((tr, n), lambda i: (i, 0)),
        compiler_params=_cparams(("parallel",)),
    )(x, g)


def _rms_bwd(dy, dy_blk, x, x_blk, n, g, name, out_dtype=BF16):
    T = x.shape[0]
    tr = 512

    def body(dy_ref, x_ref, g_ref, dx_ref, dg_ref):
        xv = x_ref[...]
        dyv = dy_ref[...].astype(F32)
        r = lax.rsqrt(jnp.mean(xv * xv, axis=-1, keepdims=True) + EPS)
        nrm = xv * r
        dn = dyv * g_ref[...]
        dx_ref[...] = (r * (dn - nrm * jnp.mean(dn * nrm, axis=-1, keepdims=True))).astype(dx_ref.dtype)

        @pl.when(pl.program_id(0) == 0)
        def _():
            dg_ref[...] = jnp.zeros_like(dg_ref)

        dg_ref[...] += jnp.sum(dyv * nrm, axis=0, keepdims=True)

    return pl.pallas_call(
        body, name=name,
        out_shape=[jax.ShapeDtypeStruct((T, n), out_dtype), jax.ShapeDtypeStruct((1, n), F32)],
        grid=(T // tr,),
        in_specs=[pl.BlockSpec((tr, n), lambda i: (i, dy_blk)), pl.BlockSpec((tr, n), lambda i: (i, x_blk)),
                  pl.BlockSpec((1, n), lambda i: (0, 0))],
        out_specs=[pl.BlockSpec((tr, n), lambda i: (i, 0)), pl.BlockSpec((1, n), lambda i: (0, 0))],
        compiler_params=_cparams(("arbitrary",)),
    )(dy, x, g)


def _rms_bwd_views(dy, dy_blk, x, g, name):
    B, S, n = x.shape
    tiles = S // VIEW_TILE

    def body(dy_ref, x_ref, g_ref, d1_ref, d4_ref, d16_ref, dg_ref, dx_s):
        xv = x_ref[0]
        dyv = dy_ref[...]
        r = lax.rsqrt(jnp.mean(xv * xv, axis=-1, keepdims=True) + EPS)
        nrm = xv * r
        dn = dyv * g_ref[...]
        dx = r * (dn - nrm * jnp.mean(dn * nrm, axis=-1, keepdims=True))
        d1_ref[0] = dx.astype(d1_ref.dtype)
        _put_tile(dx_s, dx)
        _tile_to_view(dx_s, d4_ref, DILATIONS[1], n)
        _tile_to_view(dx_s, d16_ref, DILATIONS[2], n)

        @pl.when((pl.program_id(0) == 0) & (pl.program_id(1) == 0))
        def _():
            dg_ref[...] = jnp.zeros_like(dg_ref)

        dg_ref[...] += jnp.sum(dyv * nrm, axis=0, keepdims=True)

    res = pl.pallas_call(
        body, name=name,
        out_shape=[_view_shape(B, S, d, n, BF16) for d in DILATIONS] + [jax.ShapeDtypeStruct((1, n), F32)],
        grid=(B, tiles),
        in_specs=[pl.BlockSpec((VIEW_TILE, n), lambda b, t: (b * tiles + t, dy_blk)), _view_spec(1, n),
                  pl.BlockSpec((1, n), lambda b, t: (0, 0))],
        out_specs=[_view_spec(d, n) for d in DILATIONS] + [pl.BlockSpec((1, n), lambda b, t: (0, 0))],
        scratch_shapes=[_tile_scratch(n)],
        compiler_params=_cparams(("arbitrary", "arbitrary")),
    )(dy, x, g)
    return res[:len(DILATIONS)], res[len(DILATIONS)]


def _swiglu_fwd(gu, name):
    T = gu.shape[0]
    tr, tc = 512, 1408
    nc = D_FF // tc

    def body(g_ref, u_ref, a_ref):
        gv = g_ref[...]
        a_ref[...] = (gv * jax.nn.sigmoid(gv) * u_ref[...]).astype(a_ref.dtype)

    return pl.pallas_call(
        body, name=name, out_shape=jax.ShapeDtypeStruct((T, D_FF), BF16), grid=(T // tr, nc),
        in_specs=[pl.BlockSpec((tr, tc), lambda i, j: (i, j)), pl.BlockSpec((tr, tc), lambda i, j: (i, j + nc))],
        out_specs=pl.BlockSpec((tr, tc), lambda i, j: (i, j)),
        compiler_params=_cparams(("parallel", "parallel")),
    )(gu, gu)


def _swiglu_bwd(da, gu, name):
    T = gu.shape[0]
    tr, tc = 512, 1408
    nc = D_FF // tc

    def body(da_ref, g_ref, u_ref, dgu_ref):
        j = pl.program_id(1)
        gv, uv, dav = g_ref[...], u_ref[...], da_ref[...]
        sg = jax.nn.sigmoid(gv)

        @pl.when(j < nc)
        def _():
            dgu_ref[...] = (dav * uv * (sg * (1.0 + gv * (1.0 - sg)))).astype(dgu_ref.dtype)

        @pl.when(j >= nc)
        def _():
            dgu_ref[...] = (dav * (gv * sg)).astype(dgu_ref.dtype)

    return pl.pallas_call(
        body, name=name, out_shape=jax.ShapeDtypeStruct((T, 2 * D_FF), BF16), grid=(T // tr, 2 * nc),
        in_specs=[pl.BlockSpec((tr, tc), lambda i, j: (i, j % nc)),
                  pl.BlockSpec((tr, tc), lambda i, j: (i, j % nc)),
                  pl.BlockSpec((tr, tc), lambda i, j: (i, j % nc + nc))],
        out_specs=pl.BlockSpec((tr, tc), lambda i, j: (i, j)),
        compiler_params=_cparams(("parallel", "parallel")),
    )(da, gu, gu)


def _final_loss(x1, f, g2, gf, target, name):
    B, S, D = x1.shape
    ts = ROW_TILE

    def body(x1_ref, f_ref, g2_ref, gf_ref, t_ref, dx_ref, df_ref, dg2_ref, dgf_ref, loss_ref):
        b, s = pl.program_id(0), pl.program_id(1)
        fv = f_ref[0]
        g2v = g2_ref[0]
        gfv = gf_ref[...]
        x2 = x1_ref[0] + g2v * fv
        r = lax.rsqrt(jnp.mean(x2 * x2, axis=-1, keepdims=True) + EPS)
        n = x2 * r
        e = n * gfv - t_ref[0]
        dy = e * (1.0 / D)
        dn = dy * gfv
        dx = r * (dn - n * jnp.mean(dn * n, axis=-1, keepdims=True))
        dx_ref[0] = dx
        df_ref[0] = (dx * g2v).astype(df_ref.dtype)

        @pl.when(s == 0)
        def _():
            dg2_ref[...] = jnp.zeros_like(dg2_ref)

        @pl.when((s == 0) & (b == 0))
        def _():
            dgf_ref[...] = jnp.zeros_like(dgf_ref)
            loss_ref[...] = jnp.zeros_like(loss_ref)

        dg2_ref[0] += jnp.sum(dx * fv, axis=0, keepdims=True)
        dgf_ref[...] += jnp.sum(dy * n, axis=0, keepdims=True)
        loss_ref[...] += 0.5 * jnp.sum(jnp.mean(e * e, axis=-1, keepdims=True), axis=0, keepdims=True)

    tok = pl.BlockSpec((1, ts, D), lambda b, s: (b, s, 0))
    per_b = pl.BlockSpec((1, 1, D), lambda b, s: (b, 0, 0))
    vec = pl.BlockSpec((1, D), lambda b, s: (0, 0))
    return pl.pallas_call(
        body, name=name,
        out_shape=[jax.ShapeDtypeStruct((B, S, D), F32), jax.ShapeDtypeStruct((B, S, D), BF16),
                   jax.ShapeDtypeStruct((B, 1, D), F32), jax.ShapeDtypeStruct((1, D), F32),
                   jax.ShapeDtypeStruct((1, LANES), F32)],
        grid=(B, S // ts),
        in_specs=[tok, tok, per_b, vec, tok],
        out_specs=[tok, tok, per_b, vec, pl.BlockSpec((1, LANES), lambda b, s: (0, 0))],
        compiler_params=_cparams(("arbitrary", "arbitrary")),
    )(x1, f, g2, gf, target)


def _rope_tables():
    half = ROPE_DIM // 2
    inv = ROPE_THETA ** (-jnp.arange(half, dtype=F32) / half)
    ang = jnp.arange(SEQ, dtype=F32)[:, None] * inv[None, :]
    cos, sin = jnp.cos(ang), jnp.sin(ang)
    one = jnp.ones((SEQ, NOPE_DIM), F32)
    zero = jnp.zeros((SEQ, NOPE_DIM), F32)
    cs = jnp.concatenate([one, cos, cos, one[:, :LANES - NOPE_DIM - ROPE_DIM]], axis=1)
    sn = jnp.concatenate([zero, -sin, sin, zero[:, :LANES - NOPE_DIM - ROPE_DIM]], axis=1)
    return cs, sn


def _rope_group(t, cs, sn):
    half = ROPE_DIM // 2
    lane = lax.broadcasted_iota(jnp.int32, t.shape, 1)
    partner = jnp.where(lane < NOPE_DIM + half, pltpu.roll(t, LANES - half, 1), pltpu.roll(t, half, 1))
    return t * cs + partner * sn


def _rope_apply(t, cs, sn, out_dtype, name, add=None, add_blk=0):
    B, S, W = t.shape
    G = W // LANES
    ts = ROW_TILE

    def body(*refs):
        if add is None:
            t_ref, cs_ref, sn_ref, o_ref = refs
            for gi in range(G):
                sl = slice(gi * LANES, (gi + 1) * LANES)
                o_ref[0, :, sl] = _rope_group(t_ref[0, :, sl], cs_ref[...], sn_ref[...]).astype(o_ref.dtype)
        else:
            t_ref, a_ref, cs_ref, sn_ref, o_ref = refs
            ra = _rope_group(a_ref[0], cs_ref[...], sn_ref[...])
            for gi in range(G):
                sl = slice(gi * LANES, (gi + 1) * LANES)
                o_ref[0, :, sl] = (t_ref[0, :, sl] + ra).astype(o_ref.dtype)

    tok = pl.BlockSpec((1, ts, W), lambda b, s: (b, s, 0))
    tab = pl.BlockSpec((ts, LANES), lambda b, s: (s, 0))
    in_specs, args = [tok], [t]
    if add is not None:
        in_specs.append(pl.BlockSpec((1, ts, LANES), lambda b, s: (b, s, add_blk)))
        args.append(add)
    in_specs += [tab, tab]
    args += [cs, sn]
    return pl.pallas_call(
        body, name=name, out_shape=jax.ShapeDtypeStruct((B, S, W), out_dtype), grid=(B, S // ts),
        in_specs=in_specs, out_specs=tok, compiler_params=_cparams(("parallel", "parallel")),
    )(*args)


def _krope_bwd(dkc, cs, sn_neg, name):
    B, S, W = dkc.shape
    G = W // LANES
    ts = ROW_TILE

    def body(d_ref, cs_ref, sn_ref, o_ref):
        acc = d_ref[0, :, 0:LANES]
        for gi in range(1, G):
            acc = acc + d_ref[0, :, gi * LANES:(gi + 1) * LANES]
        lane = lax.broadcasted_iota(jnp.int32, acc.shape, 1)
        rot = (lane >= NOPE_DIM) & (lane < NOPE_DIM + ROPE_DIM)
        acc = jnp.where(rot, acc, 0.0)
        o_ref[0] = _rope_group(acc, cs_ref[...], sn_ref[...]).astype(o_ref.dtype)

    tab = pl.BlockSpec((ts, LANES), lambda b, s: (s, 0))
    return pl.pallas_call(
        body, name=name, out_shape=jax.ShapeDtypeStruct((B, S, LANES), BF16), grid=(B, S // ts),
        in_specs=[pl.BlockSpec((1, ts, W), lambda b, s: (b, s, 0)), tab, tab],
        out_specs=pl.BlockSpec((1, ts, LANES), lambda b, s: (b, s, 0)),
        compiler_params=_cparams(("parallel", "parallel")),
    )(dkc, cs, sn_neg)


def _t5_bucket(dist):
    max_exact = N_BUCKETS // 2
    d = np.maximum(dist, 1).astype(np.float64)
    large = max_exact + (np.log(d / max_exact) / np.log(MAX_DISTANCE / max_exact)
                         * (N_BUCKETS - max_exact)).astype(np.int64)
    large = np.minimum(large, N_BUCKETS - 1)
    return np.where(dist < max_exact, dist, large).astype(np.int32)


def _band_buckets(dilation):
    a = np.arange(BLK)[None, :]
    bk = np.arange(2 * BLK)[:, None]
    steps = BLK + a - bk
    return _t5_bucket(np.clip(steps, 0, SPAN) * dilation)


def _head_mask(shape, hh):
    lane = lax.broadcasted_iota(jnp.int32, shape, 1)
    return (lane >= hh * HEAD_DIM) & (lane < (hh + 1) * HEAD_DIM)


def _dot_nt(a, b):
    return lax.dot_general(a, b, (((1,), (1,)), ((), ())), preferred_element_type=F32)


def _dot_tn(a, b):
    return lax.dot_general(a, b, (((0,), (0,)), ((), ())), preferred_element_type=F32)


def _dot_nn(a, b):
    return lax.dot_general(a, b, (((1,), (0,)), ((), ())), preferred_element_type=F32)


def _band_valid_t():
    bk = lax.broadcasted_iota(jnp.int32, (BLK, BLK), 0)
    a = lax.broadcasted_iota(jnp.int32, (BLK, BLK), 1)
    return bk >= a, bk <= a


def _dil_fwd(qkv, bias, branch, dilation, name):
    B, n, _ = qkv.shape
    d = dilation
    nb = n // BLK
    qkv_v = qkv
    npair = N_HEADS // 2

    def body(cur_ref, prev_ref, bias_ref, o_ref, lse_ref, s_scr, e_scr):
        i = pl.program_id(2)
        vprev, vcur = _band_valid_t()
        vprev = vprev & (i > 0)
        for p in range(npair):
            q = cur_ref[0, :, p * LANES:(p + 1) * LANES]
            kc = cur_ref[0, :, D_A + p * LANES:D_A + (p + 1) * LANES]
            kp = prev_ref[0, :, D_A + p * LANES:D_A + (p + 1) * LANES]
            for hh in range(2):
                h = 2 * p + hh
                qm = jnp.where(_head_mask((BLK, LANES), hh), q, jnp.zeros_like(q))
                s_scr[h, 0:BLK, :] = _dot_nt(kp, qm)
                s_scr[h, BLK:2 * BLK, :] = _dot_nt(kc, qm)
        ms = []
        for h in range(N_HEADS):
            s_p = jnp.where(vprev, s_scr[h, 0:BLK, :] * DIL_SCALE + bias_ref[h, 0:BLK, :], NEG)
            s_c = jnp.where(vcur, s_scr[h, BLK:2 * BLK, :] * DIL_SCALE + bias_ref[h, BLK:2 * BLK, :], NEG)
            m = jnp.maximum(jnp.max(s_p, axis=0, keepdims=True), jnp.max(s_c, axis=0, keepdims=True))
            e_scr[h, 0:BLK, :] = jnp.exp(s_p - m).astype(BF16)
            e_scr[h, BLK:2 * BLK, :] = jnp.exp(s_c - m).astype(BF16)
            ms.append(m)
        rows0 = _row_mask((LANES, BLK), 0)
        for p in range(npair):
            sl = slice(p * LANES, (p + 1) * LANES)
            vsl = slice(2 * D_A + p * LANES, 2 * D_A + (p + 1) * LANES)
            vct = jnp.transpose(cur_ref[0, :, vsl].astype(F32)).astype(BF16)
            vpt = jnp.transpose(prev_ref[0, :, vsl].astype(F32)).astype(BF16)
            acc = []
            for hh in range(2):
                h = 2 * p + hh
                mine = _row_mask((LANES, BLK), hh)
                one = jnp.ones_like(vct)
                acc.append(_dot_nn(jnp.where(mine, vpt, one), e_scr[h, 0:BLK, :])
                           + _dot_nn(jnp.where(mine, vct, one), e_scr[h, BLK:2 * BLK, :]))
            l0 = acc[0][HEAD_DIM:HEAD_DIM + 1, :]
            l1 = acc[1][0:1, :]
            o_t = jnp.where(rows0, acc[0] / l0, acc[1] / l1)
            lse_t = jnp.where(rows0, ms[2 * p] + jnp.log(l0), ms[2 * p + 1] + jnp.log(l1))
            o_ref[0, :, sl] = jnp.transpose(o_t)
            lse_ref[0, :, sl] = jnp.transpose(lse_t)

    cur = pl.BlockSpec((1, BLK, P_QKV), lambda b, r, i: (b, i, r))
    prev = pl.BlockSpec((1, BLK, P_QKV), lambda b, r, i: (b, jnp.maximum(i - 1, 0), r))
    out = pl.BlockSpec((1, BLK, D_A), lambda b, r, i: (b, i, r))
    o, lse = pl.pallas_call(
        body, name=name,
        out_shape=[jax.ShapeDtypeStruct((B, n, d * D_A), F32)] * 2,
        grid=(B, d, nb),
        in_specs=[cur, prev,
                  pl.BlockSpec((None, N_HEADS, 2 * BLK, BLK), lambda b, r, i: (branch, 0, 0, 0))],
        out_specs=[out, out],
        scratch_shapes=[pltpu.VMEM((N_HEADS, 2 * BLK, BLK), F32), pltpu.VMEM((N_HEADS, 2 * BLK, BLK), BF16)],
        compiler_params=_cparams(("parallel", "parallel", "arbitrary")),
    )(qkv_v, qkv_v, bias)
    return o, lse


VIEW_TILE = 512


def _view_spec(d, w):
    return pl.BlockSpec((1, VIEW_TILE // d, d * w), lambda b, t: (b, t, 0))


def _view_shape(B, S, d, w, dtype):
    return jax.ShapeDtypeStruct((B, S // d, d * w), dtype)


def _tile_scratch(w):
    return pltpu.VMEM((w // LANES, VIEW_TILE, LANES), F32)


def _put_tile(tile_ref, val):
    for c in range(tile_ref.shape[0]):
        tile_ref[c] = val[:, c * LANES:(c + 1) * LANES]


def _get_tile(tile_ref):
    return jnp.concatenate([tile_ref[c] for c in range(tile_ref.shape[0])], axis=1)


def _tile_to_view(tile_ref, view_ref, d, w):
    for c in range(w // LANES):
        for r in range(d):
            lo = r * w + c * LANES
            rows = tile_ref.at[c][pl.ds(r, VIEW_TILE // d, stride=d), :]
            view_ref[0, :, lo:lo + LANES] = rows.astype(view_ref.dtype)


def _view_to_tile(view_ref, tile_ref, d, w):
    for c in range(w // LANES):
        for r in range(d):
            lo = r * w + c * LANES
            tile_ref.at[c][pl.ds(r, VIEW_TILE // d, stride=d), :] = view_ref[0, :, lo:lo + LANES].astype(F32)


def _mm_qkv_views(h, w, name):
    B, S, D = h.shape
    N = w.shape[1]

    def body(h_ref, w_ref, o1_ref, o4_ref, o16_ref, acc_ref):
        acc = jnp.dot(h_ref[0], w_ref[...], preferred_element_type=F32)
        o1_ref[0] = acc.astype(o1_ref.dtype)
        _put_tile(acc_ref, acc)
        _tile_to_view(acc_ref, o4_ref, DILATIONS[1], N)
        _tile_to_view(acc_ref, o16_ref, DILATIONS[2], N)

    return pl.pallas_call(
        body, name=name,
        out_shape=[_view_shape(B, S, d, N, BF16) for d in DILATIONS],
        grid=(B, S // VIEW_TILE),
        in_specs=[pl.BlockSpec((1, VIEW_TILE, D), lambda b, t: (b, t, 0)), pl.BlockSpec((D, N), lambda b, t: (0, 0))],
        out_specs=[_view_spec(d, N) for d in DILATIONS],
        scratch_shapes=[_tile_scratch(N)],
        compiler_params=_cparams(("parallel", "parallel")),
    )(h, w)


def _dil_merge(os_, lses, name):
    B, S, W = os_[0].shape
    nd = len(DILATIONS)

    def body(*refs):
        o_refs, l_refs = refs[:nd], refs[nd:2 * nd]
        out_refs, L_refs = refs[2 * nd:3 * nd], refs[3 * nd:4 * nd]
        scr = refs[4 * nd:]
        o_tok, l_tok = [o_refs[0][0]], [l_refs[0][0]]
        for i, d in enumerate(DILATIONS[1:]):
            _view_to_tile(o_refs[i + 1], scr[2 * i], d, W)
            _view_to_tile(l_refs[i + 1], scr[2 * i + 1], d, W)
            o_tok.append(_get_tile(scr[2 * i]))
            l_tok.append(_get_tile(scr[2 * i + 1]))
        a0, a1, a2 = l_tok
        m = jnp.maximum(jnp.maximum(a0, a1), a2)
        e0, e1, e2 = jnp.exp(a0 - m), jnp.exp(a1 - m), jnp.exp(a2 - m)
        ssum = e0 + e1 + e2
        out = (e0 * o_tok[0] + e1 * o_tok[1] + e2 * o_tok[2]) / ssum
        lse = m + jnp.log(ssum)
        out_refs[0][0] = out
        L_refs[0][0] = lse
        res_o, res_l = scr[2 * (nd - 1)], scr[2 * (nd - 1) + 1]
        _put_tile(res_o, out)
        _put_tile(res_l, lse)
        for i, d in enumerate(DILATIONS[1:]):
            _tile_to_view(res_o, out_refs[i + 1], d, W)
            _tile_to_view(res_l, L_refs[i + 1], d, W)

    specs = [_view_spec(d, W) for d in DILATIONS]
    shapes = [_view_shape(B, S * DILATIONS[0], d, W, F32) for d in DILATIONS]
    res = pl.pallas_call(
        body, name=name, out_shape=shapes * 2, grid=(B, S // VIEW_TILE),
        in_specs=specs * 2, out_specs=specs * 2,
        scratch_shapes=[_tile_scratch(W)] * (2 * nd),
        compiler_params=_cparams(("parallel", "parallel")),
    )(*os_, *lses)
    return res[:nd], res[nd:]


def _dil_bwd(qkv, do, out_a, L, bias, branch, dilation, name):
    B, n, _ = qkv.shape
    d = dilation
    nb = n // BLK
    qkv_v, do_v, oa_v, L_v = qkv, do, out_a, L
    npair = N_HEADS // 2
    multi = nb > 1

    tiles = ("P", "C", "N") if multi else ("C",)
    n_t = len(tiles)

    def body(*refs):
        if multi:
            (cur_ref, prev_ref, next_ref, do_ref, don_ref, oa_ref, oan_ref, L_ref, Ln_ref, bias_ref,
             dqkv_ref, dbias_ref, s_scr, dp_scr, p_scr, ds_scr) = refs
        else:
            cur_ref, do_ref, oa_ref, L_ref, bias_ref, dqkv_ref, dbias_ref, s_scr, dp_scr, p_scr, ds_scr = refs
        b, r, i = pl.program_id(0), pl.program_id(1), pl.program_id(2)

        @pl.when((b == 0) & (r == 0) & (i == 0))
        def _():
            dbias_ref[...] = jnp.zeros_like(dbias_ref)

        vprev, vcur = _band_valid_t()
        valid = {"P": vprev & (i > 0), "C": vcur, "N": vprev & (i < nb - 1)}
        band = {"P": slice(0, BLK), "C": slice(BLK, 2 * BLK), "N": slice(0, BLK)}
        psl = lambda p: slice(p * LANES, (p + 1) * LANES)
        ksl = lambda p: slice(D_A + p * LANES, D_A + (p + 1) * LANES)
        vsl = lambda p: slice(2 * D_A + p * LANES, 2 * D_A + (p + 1) * LANES)

        def operands(p, hh):
            hm = _head_mask((BLK, LANES), hh)
            mask = lambda x: jnp.where(hm, x, jnp.zeros_like(x))
            qm, dom = mask(cur_ref[0, :, psl(p)]), mask(do_ref[0, :, psl(p)])
            ops = {"C": (cur_ref[0, :, ksl(p)], cur_ref[0, :, vsl(p)], qm, dom)}
            if multi:
                ops["P"] = (prev_ref[0, :, ksl(p)], prev_ref[0, :, vsl(p)], qm, dom)
                ops["N"] = (cur_ref[0, :, ksl(p)], cur_ref[0, :, vsl(p)], mask(next_ref[0, :, psl(p)]),
                            mask(don_ref[0, :, psl(p)]))
            return ops

        for p in range(npair):
            for hh in range(2):
                h = 2 * p + hh
                ops = operands(p, hh)
                for t, name_t in enumerate(tiles):
                    k_t, v_t, q_t, do_t = ops[name_t]
                    s_scr[h, t] = _dot_nt(k_t, q_t)
                    dp_scr[h, t] = _dot_nt(v_t, do_t)

        def rows(L_r, do_r, oa_r, p):
            lt = jnp.transpose(L_r[0, :, psl(p)])
            dt = jnp.transpose(do_r[0, :, psl(p)].astype(F32) * oa_r[0, :, psl(p)])
            return ([lt[0:1, :], lt[HEAD_DIM:HEAD_DIM + 1, :]],
                    [jnp.sum(dt[:HEAD_DIM], axis=0, keepdims=True), jnp.sum(dt[HEAD_DIM:], axis=0, keepdims=True)])

        for p in range(npair):
            lse_c, delta_c = rows(L_ref, do_ref, oa_ref, p)
            if multi:
                lse_n, delta_n = rows(Ln_ref, don_ref, oan_ref, p)
            for hh in range(2):
                h = 2 * p + hh
                for t, name_t in enumerate(tiles):
                    lse, delta = (lse_n[hh], delta_n[hh]) if name_t == "N" else (lse_c[hh], delta_c[hh])
                    s = s_scr[h, t] * DIL_SCALE + bias_ref[h, band[name_t], :]
                    pr = jnp.where(valid[name_t], jnp.exp(s - lse), 0.0)
                    ds = pr * (dp_scr[h, t] - delta)
                    p_scr[h, t] = pr.astype(BF16)
                    ds_scr[h, t] = ds.astype(BF16)
                    if name_t != "N":
                        dbias_ref[h, band[name_t], :] += ds

        for p in range(npair):
            dqt = jnp.zeros((LANES, BLK), F32)
            dk = jnp.zeros((BLK, LANES), F32)
            dv = jnp.zeros((BLK, LANES), F32)
            kct = jnp.transpose(cur_ref[0, :, ksl(p)].astype(F32)).astype(BF16)
            if multi:
                kpt = jnp.transpose(prev_ref[0, :, ksl(p)].astype(F32)).astype(BF16)
            for hh in range(2):
                h = 2 * p + hh
                ops = operands(p, hh)
                mine = _row_mask((LANES, BLK), hh)
                for t, name_t in enumerate(tiles):
                    _, _, q_t, do_t = ops[name_t]
                    if name_t != "P":
                        dv = dv + _dot_nn(p_scr[h, t], do_t)
                        dk = dk + _dot_nn(ds_scr[h, t], q_t)
                    if name_t != "N":
                        kt = kpt if name_t == "P" else kct
                        dqt = dqt + _dot_nn(jnp.where(mine, kt, jnp.zeros_like(kt)), ds_scr[h, t])
            dqkv_ref[0, :, psl(p)] = jnp.transpose(dqt) * DIL_SCALE
            dqkv_ref[0, :, ksl(p)] = dk * DIL_SCALE
            dqkv_ref[0, :, vsl(p)] = dv

    def at(off):
        return lambda b, r, i: (b, jnp.clip(i + off, 0, nb - 1), r)

    qkv_spec = lambda off: pl.BlockSpec((1, BLK, P_QKV), at(off))
    da_spec = lambda off: pl.BlockSpec((1, BLK, D_A), at(off))
    bias_spec = pl.BlockSpec((None, N_HEADS, 2 * BLK, BLK), lambda b, r, i: (branch, 0, 0, 0))
    dbias_spec = pl.BlockSpec((N_HEADS, 2 * BLK, BLK), lambda b, r, i: (0, 0, 0))
    if multi:
        in_specs = [qkv_spec(0), qkv_spec(-1), qkv_spec(1), da_spec(0), da_spec(1), da_spec(0), da_spec(1),
                    da_spec(0), da_spec(1), bias_spec]
        args = [qkv_v, qkv_v, qkv_v, do_v, do_v, oa_v, oa_v, L_v, L_v, bias]
    else:
        in_specs = [qkv_spec(0), da_spec(0), da_spec(0), da_spec(0), bias_spec]
        args = [qkv_v, do_v, oa_v, L_v, bias]
    dqkv, dbias = pl.pallas_call(
        body, name=name,
        out_shape=[jax.ShapeDtypeStruct((B, n, d * P_QKV), F32),
                   jax.ShapeDtypeStruct((N_HEADS, 2 * BLK, BLK), F32)],
        grid=(B, d, nb),
        in_specs=in_specs,
        out_specs=[qkv_spec(0), dbias_spec],
        scratch_shapes=[pltpu.VMEM((N_HEADS, n_t, BLK, BLK), F32), pltpu.VMEM((N_HEADS, n_t, BLK, BLK), F32),
                        pltpu.VMEM((N_HEADS, n_t, BLK, BLK), BF16), pltpu.VMEM((N_HEADS, n_t, BLK, BLK), BF16)],
        compiler_params=_cparams(("arbitrary", "arbitrary", "arbitrary")),
    )(*args)
    return dqkv, dbias


def _sum_views_bf16(parts, name):
    B, S, W = parts[0].shape

    def body(a_ref, b_ref, c_ref, o_ref, sb, sc):
        _view_to_tile(b_ref, sb, DILATIONS[1], W)
        _view_to_tile(c_ref, sc, DILATIONS[2], W)
        o_ref[0] = (a_ref[0] + _get_tile(sb) + _get_tile(sc)).astype(o_ref.dtype)

    return pl.pallas_call(
        body, name=name, out_shape=jax.ShapeDtypeStruct((B, S, W), BF16), grid=(B, S // VIEW_TILE),
        in_specs=[_view_spec(d, W) for d in DILATIONS], out_specs=_view_spec(1, W),
        scratch_shapes=[_tile_scratch(W)] * 2,
        compiler_params=_cparams(("parallel", "parallel")),
    )(*parts)


def _bias_tables(rel_bias, buckets, name):
    nbr = buckets.shape[0]

    def body(rb_ref, bk_ref, o_ref):
        h = pl.program_id(1)
        tab = bk_ref[0]

        def step(bkt, acc):
            return jnp.where(tab == bkt, rb_ref[bkt, h], acc)

        o_ref[0, 0] = lax.fori_loop(0, N_BUCKETS, step, jnp.zeros((2 * BLK, BLK), F32))

    return pl.pallas_call(
        body, name=name, out_shape=jax.ShapeDtypeStruct((nbr, N_HEADS, 2 * BLK, BLK), F32),
        grid=(nbr, N_HEADS),
        in_specs=[pl.BlockSpec(memory_space=pltpu.SMEM),
                  pl.BlockSpec((1, 2 * BLK, BLK), lambda i, h: (i, 0, 0))],
        out_specs=pl.BlockSpec((1, 1, 2 * BLK, BLK), lambda i, h: (i, h, 0, 0)),
        compiler_params=_cparams(("parallel", "arbitrary")),
    )(rel_bias, buckets)


def _bias_grad(dbias_list, buckets, name):
    nbr = len(dbias_list)

    def body(*refs):
        d_refs, bk_ref, o_ref = refs[:nbr], refs[nbr], refs[nbr + 1]
        lane = lax.broadcasted_iota(jnp.int32, (1, LANES), 1)
        for h in range(N_HEADS):
            def step(bkt, acc):
                tot = jnp.zeros((1, 1), F32)
                for bi in range(nbr):
                    sel = jnp.where(bk_ref[bi] == bkt, d_refs[bi][h], 0.0)
                    tot = tot + jnp.sum(jnp.sum(sel, axis=1, keepdims=True), axis=0, keepdims=True)
                return acc + jnp.where(lane == bkt, tot, 0.0)

            o_ref[h:h + 1, :] = lax.fori_loop(0, N_BUCKETS, step, jnp.zeros((1, LANES), F32))

    band = pl.BlockSpec((N_HEADS, 2 * BLK, BLK), lambda i: (0, 0, 0))
    return pl.pallas_call(
        body, name=name, out_shape=jax.ShapeDtypeStruct((N_HEADS, LANES), F32), grid=(1,),
        in_specs=[band] * nbr + [pl.BlockSpec((nbr, 2 * BLK, BLK), lambda i: (0, 0, 0))],
        out_specs=pl.BlockSpec((N_HEADS, LANES), lambda i: (0, 0)),
        compiler_params=_cparams(("arbitrary",)),
    )(*dbias_list, buckets)


MLA_TQ = 256
MLA_TK = 256


LOG2E = math.log2(math.e)
MLA_C = MLA_SCALE * LOG2E


def _key_le_query(tk, tq):
    return lax.broadcasted_iota(jnp.int32, (tk, tq), 0) <= lax.broadcasted_iota(jnp.int32, (tk, tq), 1)


def _row_mask(shape, hh):
    row = lax.broadcasted_iota(jnp.int32, shape, 0)
    return (row >= hh * HEAD_DIM) & (row < (hh + 1) * HEAD_DIM)


def _host_call(body, comm, *, name, grid, in_specs, out_specs, out_shape, scratch_shapes, args):
    sem = ("arbitrary",) * len(grid)
    if comm is None:
        res = pl.pallas_call(body, name=name, grid=grid, in_specs=in_specs, out_specs=out_specs,
                             out_shape=out_shape, scratch_shapes=scratch_shapes,
                             compiler_params=_cparams(sem))(*args)
        return res, []
    n_in, n_out, n_s, cn = len(in_specs), len(out_specs), len(scratch_shapes), comm.n

    def hosted(*refs):
        ins, refs = refs[:n_in], refs[n_in:]
        c_ins, refs = refs[:cn], refs[cn:]
        outs, refs = refs[:n_out], refs[n_out:]
        c_outs, refs = refs[:cn], refs[cn:]
        scr, c_sems = refs[:n_s], refs[n_s:]
        ids = [pl.program_id(a) for a in range(len(grid))]
        first = functools.reduce(jnp.logical_and, [i == 0 for i in ids])
        last = functools.reduce(jnp.logical_and, [i == g - 1 for i, g in zip(ids, grid)])

        @pl.when(first)
        def _():
            comm.start(c_ins, c_outs, c_sems)

        body(*ins, *outs, *scr)

        @pl.when(last)
        def _():
            comm.finish(c_ins, c_outs, c_sems)

    res = pl.pallas_call(
        hosted, name=name, grid=grid, in_specs=list(in_specs) + _hbm_specs(cn),
        out_specs=list(out_specs) + _hbm_specs(cn), out_shape=list(out_shape) + list(comm.out_shape),
        scratch_shapes=list(scratch_shapes) + list(comm.scratch), compiler_params=_cparams(sem),
    )(*args, *comm.inputs)
    return res[:n_out], res[n_out:]


def _mla_fwd_t(q, k, vt, name, comm=None):
    B, S, _ = q.shape
    tq, tk = MLA_TQ, MLA_TK
    assert tq == tk
    npair = N_HEADS // 2
    nq = S // tq

    def body(q_ref, k_ref, vt_ref, o_ref, lse_ref):
        i = pl.program_id(2)
        qs = [q_ref[0, :, hh * LANES:(hh + 1) * LANES] for hh in range(2)]
        diag = _key_le_query(tk, tq)

        def step(j, carry, masked):
            kj = k_ref[0, pl.ds(pl.multiple_of(j * tk, tk), tk), :]
            vj = vt_ref[0, 0, j]
            out = []
            for hh in range(2):
                m_old, acc = carry[hh]
                s = _dot_nt(kj[:, hh * LANES:(hh + 1) * LANES], qs[hh])
                if masked:
                    s = jnp.where(diag, s, NEG)
                m_new = jnp.maximum(m_old, jnp.max(s, axis=0, keepdims=True))
                alpha = jnp.exp2((m_old - m_new) * MLA_C)
                e = jnp.exp2((s - m_new) * MLA_C).astype(BF16)
                vh = jnp.where(_row_mask(vj.shape, hh), vj, jnp.ones_like(vj))
                out.append((m_new, acc * alpha + _dot_nn(vh, e)))
            return tuple(out)

        init = (jnp.full((1, tq), NEG, F32), jnp.zeros((LANES, tq), F32))
        carry = lax.fori_loop(0, i, lambda j, c: step(j, c, False), (init, init))
        (m0, acc0), (m1, acc1) = step(i, carry, True)
        l0 = acc0[HEAD_DIM:HEAD_DIM + 1, :]
        l1 = acc1[0:1, :]
        o_ref[0] = jnp.where(_row_mask((LANES, tq), 0), acc0 / l0, acc1 / l1)
        lse_ref[0, 0, 0] = jnp.zeros((8, tq), F32)
        lse_ref[0, 0, 0, 0:1, :] = m0 * MLA_C + jnp.log(l0) * LOG2E
        lse_ref[0, 0, 0, 1:2, :] = m1 * MLA_C + jnp.log(l1) * LOG2E

    return _host_call(
        body, comm, name=name,
        out_shape=[jax.ShapeDtypeStruct((B, D_B, S), F32), jax.ShapeDtypeStruct((B, npair, nq, 8, tq), F32)],
        grid=(B, npair, nq),
        in_specs=[pl.BlockSpec((1, tq, 2 * LANES), lambda b, p, i: (b, i, p)),
                  pl.BlockSpec((1, S, 2 * LANES), lambda b, p, i: (b, 0, p)),
                  pl.BlockSpec((1, 1, S // tk, LANES, tk), lambda b, p, i: (b, p, 0, 0, 0))],
        out_specs=[pl.BlockSpec((1, LANES, tq), lambda b, p, i: (b, p, i)),
                   pl.BlockSpec((1, 1, 1, 8, tq), lambda b, p, i: (b, p, i, 0, 0))],
        scratch_shapes=[],
        args=(q, k, vt))


def _mla_delta(do, o, name):
    B, S, _ = o.shape
    tq = MLA_TQ
    npair = N_HEADS // 2

    def body(do_ref, o_ref, d_ref):
        d_ref[...] = jnp.zeros_like(d_ref)
        for p in range(npair):
            sl = slice(p * LANES, (p + 1) * LANES)
            prod_t = jnp.transpose(do_ref[0, :, sl].astype(F32) * o_ref[0, :, sl])
            d_ref[0, p, 0, 0:1, :] = jnp.sum(prod_t[:HEAD_DIM], axis=0, keepdims=True)
            d_ref[0, p, 0, 1:2, :] = jnp.sum(prod_t[HEAD_DIM:], axis=0, keepdims=True)

    tok = pl.BlockSpec((1, tq, D_B), lambda b, i: (b, i, 0))
    return pl.pallas_call(
        body, name=name, out_shape=jax.ShapeDtypeStruct((B, npair, S // tq, 8, tq), F32),
        grid=(B, S // tq), in_specs=[tok, tok],
        out_specs=pl.BlockSpec((1, npair, 1, 8, tq), lambda b, i: (b, 0, i, 0, 0)),
        compiler_params=_cparams(("parallel", "parallel")),
    )(do, o)


def _mla_bwd_t(q, k, v, do, lse, delta, name, comm=None):
    B, S, _ = q.shape
    tq, tk = MLA_TQ, MLA_TK
    assert tq == tk
    npair = N_HEADS // 2
    nq = S // tq

    def body(q_ref, do_ref, lse_ref, dl_ref, k_ref, v_ref, dk_ref, dv_ref, dq_ref, dk_s, dv_s):
        j = pl.program_id(2)

        @pl.when(j == 0)
        def _():
            dq_ref[...] = jnp.zeros_like(dq_ref)

        vj = v_ref[0]
        dv_s[...] = jnp.zeros_like(dv_s)
        diag = _key_le_query(tk, tq)
        for hh in range(2):
            hsl = slice(hh * LANES, (hh + 1) * LANES)
            hm = _head_mask((tq, LANES), hh)
            kh = k_ref[0, :, hsl]
            kt = jnp.transpose(kh.astype(F32)).astype(BF16)
            dk_s[...] = jnp.zeros_like(dk_s)

            def step(i, masked):
                rows = pl.ds(pl.multiple_of(i * tq, tq), tq)
                qi = q_ref[0, rows, hsl]
                dov = do_ref[0, rows, :]
                dom = jnp.where(hm, dov, jnp.zeros_like(dov))
                s = _dot_nt(kh, qi)
                pr = jnp.exp2(s * MLA_C - lse_ref[0, 0, i, hh:hh + 1, :])
                if masked:
                    pr = jnp.where(diag, pr, 0.0)
                dv_s[...] += _dot_nn(pr.astype(BF16), dom)
                ds = (pr * (_dot_nt(vj, dom) - dl_ref[0, 0, i, hh:hh + 1, :])).astype(BF16)
                dk_s[...] += _dot_nn(ds, qi)
                dq_ref[0, 0, i, hsl, :] += _dot_nn(kt, ds) * MLA_SCALE

            step(j, True)

            def loop_body(i, carry):
                step(i, False)
                return carry

            lax.fori_loop(j + 1, nq, loop_body, 0)
            dk_ref[0, :, hsl] = dk_s[...] * MLA_SCALE
        dv_ref[0] = dv_s[...]

    stat = pl.BlockSpec((1, 1, nq, 8, tq), lambda b, p, j: (b, p, 0, 0, 0))
    return _host_call(
        body, comm, name=name,
        out_shape=[jax.ShapeDtypeStruct((B, S, N_HEADS * LANES), F32), jax.ShapeDtypeStruct((B, S, D_B), F32),
                   jax.ShapeDtypeStruct((B, npair, nq, 2 * LANES, tq), F32)],
        grid=(B, npair, S // tk),
        in_specs=[pl.BlockSpec((1, S, 2 * LANES), lambda b, p, j: (b, 0, p)),
                  pl.BlockSpec((1, S, LANES), lambda b, p, j: (b, 0, p)),
                  stat, stat,
                  pl.BlockSpec((1, tk, 2 * LANES), lambda b, p, j: (b, j, p)),
                  pl.BlockSpec((1, tk, LANES), lambda b, p, j: (b, j, p))],
        out_specs=[pl.BlockSpec((1, tk, 2 * LANES), lambda b, p, j: (b, j, p)),
                   pl.BlockSpec((1, tk, LANES), lambda b, p, j: (b, j, p)),
                   pl.BlockSpec((1, 1, nq, 2 * LANES, tq), lambda b, p, j: (b, p, 0, 0, 0))],
        scratch_shapes=[pltpu.VMEM((tk, LANES), F32), pltpu.VMEM((tk, LANES), F32)],
        args=(q, do, lse, delta, k, v))


def _local_step(x, target, mod, wts, gains, rel_bias, ffn_shards=None):
    B, S, D = x.shape
    T = B * S
    sh1, sc1, g1, sh2, sc2, g2 = [mod[:, i * D:(i + 1) * D].reshape(B, 1, D) for i in range(N_MOD)]
    cs, sn = _rope_tables()
    buckets = np.stack([_band_buckets(d) for d in DILATIONS])
    buckets_dev = jnp.asarray(buckets)
    bias = _bias_tables(rel_bias, buckets_dev, "rel_bias_tables")
    w_in = wts["w_in"]

    h1 = _adaln_fwd(x, gains["g_norm1"], sc1, sh1, "adaln1_fwd")
    h1f = h1.reshape(T, D)
    qkv_v = _mm_qkv_views(h1, w_in[:, :P_QKV], "mm_qkv")
    rest = _mm(h1f, w_in[:, P_QKV:], "nn", F32, "mm_rest")
    o_d, lse_d = [], []
    for i, d in enumerate(DILATIONS):
        o_i, lse_i = _dil_fwd(qkv_v[i], bias, i, d, f"dil_fwd_{d}")
        o_d.append(o_i)
        lse_d.append(lse_i)
    out_a_v, lse_a_v = _dil_merge(o_d, lse_d, "dil_merge")
    out_a = out_a_v[0]
    cqn = _rms_fwd(rest, 1, Q_LORA, gains["g_cq"], "rms_cq_fwd")
    ckvn = _rms_fwd(rest, 0, KV_LORA, gains["g_ckv"], "rms_ckv_fwd")
    rest3 = rest.reshape(B, S, P_REST)
    q_raw = _mm(cqn, wts["w_uq"], "nn", F32, "mm_uq").reshape(B, S, N_HEADS * LANES)
    qc = _rope_apply(q_raw, cs, sn, BF16, "rope_q")
    kn_raw = _mm(ckvn, wts["w_kv"][:, :N_HEADS * LANES], "nn", F32, "mm_uk").reshape(B, S, N_HEADS * LANES)
    kc = _rope_apply(kn_raw, cs, sn, BF16, "rope_k", add=rest3, add_blk=KV_LORA // LANES)
    v = _mm(ckvn, wts["w_kv"][:, N_HEADS * LANES:], "nn", BF16, "mm_uv").reshape(B, S, D_B)
    vt = jnp.transpose(v.reshape(B, S // MLA_TK, MLA_TK, N_HEADS // 2, LANES), (0, 3, 1, 4, 2))
    (o_t, lse_b), got = _mla_fwd_t(qc, kc, vt, "mla_fwd", _GatherComm(ffn_shards) if ffn_shards else None)
    if ffn_shards:
        wts = dict(wts, w_ffn_in=got[0].reshape(N_CHIP, D, -1), w_ffn_out=got[1].reshape(D_FF, D))
    out_b = jnp.transpose(o_t, (0, 2, 1))
    out_af, out_bf = out_a.reshape(T, D_A), out_b.reshape(T, D_B)
    ya = _rms_fwd(out_af, 0, D_A, gains["g_out_a"], "rms_outa_fwd")
    yb = _rms_fwd(out_bf, 0, D_B, gains["g_out_b"], "rms_outb_fwd")
    y = jnp.concatenate([ya, yb], axis=1)
    mix = _mm(y, wts["w_out"], "nn", F32, "mm_out").reshape(B, S, D)
    h2, x1 = _adaln_fwd(x, gains["g_norm2"], sc2, sh2, "adaln2_fwd", mix=mix, gate=g1)
    h2f = h2.reshape(T, D)
    gu = _mm(h2f, wts["w_ffn_in"], "nn", F32, "mm_ffn_in", col_blocks=N_CHIP)
    act = _swiglu_fwd(gu, "swiglu_fwd")
    f = _mm(act, wts["w_ffn_out"], "nn", F32, "mm_ffn_out").reshape(B, S, D)
    dx2, df, dg2, dg_final, loss = _final_loss(x1, f, g2, gains["g_final"], target, "final_loss")

    dff = df.reshape(T, D)
    da = _mm(dff, wts["w_ffn_out"], "nt", F32, "mm_ffn_out_dx")
    gw_ffn_out = _mm(act, dff, "tn", F32, "mm_ffn_out_dw")
    dgu = _swiglu_bwd(da, gu, "swiglu_bwd")
    dh2 = _mm(dgu, wts["w_ffn_in"], "nt", F32, "mm_ffn_in_dx", col_blocks=N_CHIP).reshape(B, S, D)
    gw_ffn_in = _mm(h2f, dgu, "tn", F32, "mm_ffn_in_dw", col_blocks=N_CHIP)
    dx1, dsh2, dsc2, dg_norm2, dg1, dmix = _adaln_bwd(dh2, x1, gains["g_norm2"], sc2, dx2, "adaln2_bwd",
                                                      mix=mix, gate=g1)
    dmixf = dmix.reshape(T, D)
    dy = _mm(dmixf, wts["w_out"], "nt", F32, "mm_out_dx")
    gw_out = _mm(y, dmixf, "tn", F32, "mm_out_dw")
    do_a_v, dg_out_a = _rms_bwd_views(dy, 0, out_a, gains["g_out_a"], "rms_outa_bwd")
    do_b, dg_out_b = _rms_bwd(dy, 1, out_bf, 0, D_B, gains["g_out_b"], "rms_outb_bwd")
    do_b3 = do_b.reshape(B, S, D_B)
    delta_b = _mla_delta(do_b3, out_b, "mla_delta")
    ffn_a4 = None
    if ffn_shards:
        ffn_a4 = _rs_first([gw_ffn_in.reshape(N_DEV, -1, gw_ffn_in.shape[-1]), gw_ffn_out.reshape(N_DEV, -1, D)],
                           "ffn")
    (dkc, dv, dq_t), ffn_r2 = _mla_bwd_t(qc, kc, v, do_b3, lse_b, delta_b, "mla_bwd",
                                         _ToChipsComm(ffn_a4) if ffn_shards else None)
    dqc = jnp.transpose(dq_t, (0, 2, 4, 1, 3)).reshape(B, S, N_HEADS * LANES)
    dq_raw = _rope_apply(dqc, cs, -sn, BF16, "rope_q_bwd").reshape(T, N_HEADS * LANES)
    dkrw = _krope_bwd(dkc, cs, -sn, "rope_k_bwd").reshape(T, LANES)
    dcqn = _mm(dq_raw, wts["w_uq"], "nt", F32, "mm_uq_dx")
    gw_uq = _mm(cqn, dq_raw, "tn", F32, "mm_uq_dw")
    dkv = jnp.concatenate([dkc.reshape(T, -1), dv.reshape(T, -1)], axis=1).astype(BF16)
    dckvn = _mm(dkv, wts["w_kv"], "nt", F32, "mm_ukv_dx")
    gw_kv = _mm(ckvn, dkv, "tn", F32, "mm_ukv_dw")
    dcq, dg_cq = _rms_bwd(dcqn, 0, rest, 1, Q_LORA, gains["g_cq"], "rms_cq_bwd")
    dckv, dg_ckv = _rms_bwd(dckvn, 0, rest, 0, KV_LORA, gains["g_ckv"], "rms_ckv_bwd")
    dqkv_d, dbias_d = [], []
    for i, d in enumerate(DILATIONS):
        dqkv_i, dbias_i = _dil_bwd(qkv_v[i], do_a_v[i], out_a_v[i], lse_a_v[i], bias, i, d, f"dil_bwd_{d}")
        dqkv_d.append(dqkv_i)
        dbias_d.append(dbias_i)
    dqkv = _sum_views_bf16(dqkv_d, "dil_bwd_sum").reshape(T, P_QKV)
    g_rel_bias = _bias_grad(dbias_d, buckets_dev, "rel_bias_grad")[:, :N_BUCKETS].T
    dproj = jnp.concatenate([dqkv, dckv, dkrw, dcq], axis=1)
    dh1 = _mm(dproj, w_in, "nt", F32, "mm_in_dx").reshape(B, S, D)
    gw_in = _mm(h1f, dproj, "tn", F32, "mm_in_dw")
    grad_x, dsh1, dsc1, dg_norm1 = _adaln_bwd(dh1, x, gains["g_norm1"], sc1, dx1, "adaln1_bwd")
    gmod = jnp.concatenate([dsh1, dsc1, dg1, dsh2, dsc2, dg2], axis=-1).reshape(B, N_MOD * D)
    grads = dict(w_in=gw_in, w_uq=gw_uq, w_kv=gw_kv, w_out=gw_out, w_ffn_in=gw_ffn_in, w_ffn_out=gw_ffn_out,
                 g_norm1=dg_norm1, g_cq=dg_cq, g_ckv=dg_ckv, rel_bias=g_rel_bias, g_out_a=dg_out_a,
                 g_out_b=dg_out_b, g_norm2=dg_norm2, g_final=dg_final, ffn_pending=(ffn_a4, ffn_r2))
    return loss, grad_x, gmod, grads


def _w_in_to_kernel(w):
    z = lambda n: jnp.zeros((w.shape[0], n), w.dtype)
    i3, i4, i5 = 3 * D_A, 3 * D_A + Q_LORA, 3 * D_A + Q_LORA + KV_LORA
    return jnp.concatenate([w[:, :i3], w[:, i4:i5], z(NOPE_DIM), w[:, i5:], z(LANES - NOPE_DIM - ROPE_DIM),
                            w[:, i3:i4]], axis=1)


def _w_in_from_kernel(g):
    o = P_QKV + KV_LORA
    return jnp.concatenate([g[:, :P_QKV], g[:, o + LANES:], g[:, P_QKV:o],
                            g[:, o + NOPE_DIM:o + NOPE_DIM + ROPE_DIM]], axis=1)


def _w_uq_to_kernel(w):
    w3 = w.reshape(Q_LORA, N_HEADS, NOPE_DIM + ROPE_DIM)
    return jnp.pad(w3, ((0, 0), (0, 0), (0, LANES - NOPE_DIM - ROPE_DIM))).reshape(Q_LORA, N_HEADS * LANES)


def _w_uq_from_kernel(g):
    return g.reshape(Q_LORA, N_HEADS, LANES)[:, :, :NOPE_DIM + ROPE_DIM].reshape(Q_LORA, -1)


def _w_ukv_to_kernel(w):
    w3 = w.reshape(KV_LORA, N_HEADS, 2 * HEAD_DIM)
    wk = jnp.pad(w3[:, :, :NOPE_DIM], ((0, 0), (0, 0), (0, LANES - NOPE_DIM))).reshape(KV_LORA, N_HEADS * LANES)
    wv = w3[:, :, NOPE_DIM:].reshape(KV_LORA, D_B)
    return jnp.concatenate([wk, wv], axis=1)


def _w_ukv_from_kernel(g):
    gk = g[:, :N_HEADS * LANES].reshape(KV_LORA, N_HEADS, LANES)[:, :, :NOPE_DIM]
    gv = g[:, N_HEADS * LANES:].reshape(KV_LORA, N_HEADS, HEAD_DIM)
    return jnp.concatenate([gk, gv], axis=2).reshape(KV_LORA, -1)


MESH = pl.DeviceIdType.MESH


def _my_place():
    return lax.axis_index("x"), lax.axis_index("y"), lax.axis_index("c")


def _other_chips(x, y):
    return [(1 - x, y), (x, 1 - y), (1 - x, 1 - y)]


def _allgather8(x_shard, name, in_hbm):
    m_per, n = x_shard.shape
    space = pl.ANY if in_hbm else pltpu.VMEM

    def body(x_ref, out_ref, send_sems, recv_sems, local_sem):
        x, y, c = _my_place()
        me, sibling = (x, y, c), (x, y, 1 - c)
        chips = _other_chips(x, y)

        def rows(px, py, pc):
            return out_ref.at[pl.ds((4 * px + 2 * py + pc) * m_per, m_per), :]

        def copy(k, block, to, src=None):
            return pltpu.make_async_remote_copy(
                src_ref=rows(*block) if src is None else src, dst_ref=rows(*block),
                send_sem=send_sems.at[k], recv_sem=recv_sems.at[k], device_id=to, device_id_type=MESH)

        mine = pltpu.make_async_copy(x_ref, rows(*me), local_sem)
        mine.start()
        first = [copy(0, me, sibling, src=x_ref)]
        first += [copy(1 + j, me, (*chip, c), src=x_ref) for j, chip in enumerate(chips)]
        for cp in first:
            cp.start()
        passed = [copy(4 + j, (*chip, c), sibling) for j, chip in enumerate(chips)]
        for j, chip in enumerate(chips):
            copy(1 + j, (*chip, c), me).wait_recv()
            passed[j].start()
        copy(0, sibling, me).wait_recv()
        for j, chip in enumerate(chips):
            copy(4 + j, (*chip, 1 - c), me).wait_recv()
        for cp in first + passed:
            cp.wait_send()
        mine.wait()

    return pl.pallas_call(
        body, name=name,
        out_shape=jax.ShapeDtypeStruct((N_DEV * m_per, n), x_shard.dtype),
        in_specs=[pl.BlockSpec(memory_space=space)],
        out_specs=pl.BlockSpec(memory_space=space),
        scratch_shapes=[pltpu.SemaphoreType.DMA((7,)), pltpu.SemaphoreType.DMA((7,)), pltpu.SemaphoreType.DMA],
        compiler_params=pltpu.CompilerParams(vmem_limit_bytes=VMEM_LIMIT),
    )(x_shard)


def _hbm_specs(n):
    return [pl.BlockSpec(memory_space=pl.ANY)] * n


class _GatherComm:
    def __init__(self, shards):
        self.n = n = len(shards)
        self.inputs = [s.reshape(2, s.shape[0] // 2, s.shape[1]) for s in shards]
        self.out_shape = [jax.ShapeDtypeStruct((N_DEV,) + s.shape[1:], s.dtype) for s in self.inputs]
        self.scratch = [pltpu.SemaphoreType.DMA((7 * n,)), pltpu.SemaphoreType.DMA((7 * n,))]

    def _parts(self, xs, outs, sems):
        send_sems, recv_sems = sems
        x, y, c = _my_place()

        def blk(k, px, py, pc):
            return outs[k].at[4 * px + 2 * py + pc]

        def copy(k, kind, block, to, own=False):
            return pltpu.make_async_remote_copy(
                src_ref=xs[k].at[c] if own else blk(k, *block), dst_ref=blk(k, *block),
                send_sem=send_sems.at[7 * k + kind], recv_sem=recv_sems.at[7 * k + kind],
                device_id=to, device_id_type=MESH)

        def whole(k):
            return pltpu.make_async_remote_copy(
                src_ref=xs[k], dst_ref=outs[k].at[pl.ds(4 * x + 2 * y, 2)],
                send_sem=send_sems.at[7 * k], recv_sem=recv_sems.at[7 * k],
                device_id=(x, y, 1 - c), device_id_type=MESH)

        me, sibling = (x, y, c), (x, y, 1 - c)
        chips = _other_chips(x, y)
        first = []
        for k in range(self.n):
            first.append(whole(k))
            first += [copy(k, 1 + j, me, (*chip, c), own=True) for j, chip in enumerate(chips)]
        return copy, whole, me, sibling, chips, c, first

    def start(self, xs, outs, sems):
        for cp in self._parts(xs, outs, sems)[-1]:
            cp.start()

    def finish(self, xs, outs, sems):
        copy, whole, me, sibling, chips, c, first = self._parts(xs, outs, sems)
        passed = []
        for j, chip in enumerate(chips):
            for k in range(self.n):
                copy(k, 1 + j, (*chip, c), me).wait_recv()
                fwd = copy(k, 4 + j, (*chip, c), sibling)
                fwd.start()
                passed.append(fwd)
        for k in range(self.n):
            whole(k).wait_recv()
        for j, chip in enumerate(chips):
            for k in range(self.n):
                copy(k, 4 + j, (*chip, 1 - c), me).wait_recv()
        for cp in first + passed:
            cp.wait_send()


class _ToChipsComm:
    def __init__(self, a4s):
        self.inputs = list(a4s)
        self.n = n = len(a4s)
        nc = N_CHIP - 1
        self.out_shape = [jax.ShapeDtypeStruct((nc,) + a.shape[1:], a.dtype) for a in a4s]
        self.scratch = [pltpu.SemaphoreType.DMA((nc * n,)), pltpu.SemaphoreType.DMA((nc * n,))]

    def _copies(self, as_, rs, sems):
        send_sems, recv_sems = sems
        x, y, c = _my_place()
        nc = N_CHIP - 1
        return [pltpu.make_async_remote_copy(
            src_ref=as_[k].at[2 * cx + cy], dst_ref=rs[k].at[j], send_sem=send_sems.at[nc * k + j],
            recv_sem=recv_sems.at[nc * k + j], device_id=(cx, cy, c), device_id_type=MESH)
            for k in range(self.n) for j, (cx, cy) in enumerate(_other_chips(x, y))]

    def start(self, as_, rs, sems):
        for cp in self._copies(as_, rs, sems):
            cp.start()

    def finish(self, as_, rs, sems):
        for cp in self._copies(as_, rs, sems):
            cp.wait()


def _run_comm(comm, name):
    n = comm.n

    def body(*refs):
        ins, outs, sems = refs[:n], refs[n:2 * n], refs[2 * n:]
        comm.start(ins, outs, sems)
        comm.finish(ins, outs, sems)

    return pl.pallas_call(
        body, name=name, out_shape=comm.out_shape, in_specs=_hbm_specs(n), out_specs=_hbm_specs(n),
        scratch_shapes=comm.scratch,
    )(*comm.inputs)


def _gather_weights(shards, name):
    return _run_comm(_GatherComm(shards), name)


def _rs_to_sibling(g8s, name):
    n = len(g8s)

    def body(*refs):
        gs, rs = refs[:n], refs[n:2 * n]
        send_sems, recv_sems = refs[2 * n:]
        x, y, c = _my_place()
        copies = [pltpu.make_async_remote_copy(
            src_ref=gs[k].at[2 * s + 1 - c], dst_ref=rs[k].at[s], send_sem=send_sems.at[N_CHIP * k + s],
            recv_sem=recv_sems.at[N_CHIP * k + s], device_id=(x, y, 1 - c), device_id_type=MESH)
            for k in range(n) for s in range(N_CHIP)]
        for cp in copies:
            cp.start()
        for cp in copies:
            cp.wait()

    return pl.pallas_call(
        body, name=name,
        out_shape=[jax.ShapeDtypeStruct((N_CHIP,) + g.shape[1:], g.dtype) for g in g8s],
        in_specs=_hbm_specs(n), out_specs=_hbm_specs(n),
        scratch_shapes=[pltpu.SemaphoreType.DMA((N_CHIP * n,)), pltpu.SemaphoreType.DMA((N_CHIP * n,))],
    )(*g8s)


def _rs_to_chips(a4s, name):
    return _run_comm(_ToChipsComm(a4s), name)


def _swap_halves(hs, name):
    n = len(hs)

    def body(*refs):
        o_refs = refs[n:2 * n]
        send_sems, recv_sems = refs[2 * n:]
        x, y, c = _my_place()

        def remote(k, slot):
            return pltpu.make_async_remote_copy(
                src_ref=o_refs[k].at[slot], dst_ref=o_refs[k].at[slot], send_sem=send_sems.at[k],
                recv_sem=recv_sems.at[k], device_id=(x, y, 1 - c), device_id_type=MESH)

        sends = [remote(k, c) for k in range(n)]
        for cp in sends:
            cp.start()
        for k in range(n):
            remote(k, 1 - c).wait_recv()
        for cp in sends:
            cp.wait_send()

    return pl.pallas_call(
        body, name=name,
        out_shape=[jax.ShapeDtypeStruct(h.shape, h.dtype) for h in hs],
        in_specs=_hbm_specs(n), out_specs=_hbm_specs(n),
        input_output_aliases={k: k for k in range(n)},
        scratch_shapes=[pltpu.SemaphoreType.DMA((n,)), pltpu.SemaphoreType.DMA((n,))],
    )(*hs)


ADD_TILES = 4


def _add_blocks(a_list, a_idx_fn, others_list, ns, sel, name, out_blocks=None, out_idx_fn=None):
    out_blocks = out_blocks or ns
    out_idx_fn = out_idx_fn or (lambda s, sel_ref: s)
    n = len(a_list)
    n_o = len(others_list[0])
    per = 1 + n_o

    def body(sel_ref, *refs):
        for k in range(n):
            ins = refs[k * per:(k + 1) * per]
            o_ref = refs[n * per + k]
            acc = ins[0][0]
            for r in ins[1:]:
                acc = acc + r[0]
            o_ref[0] = acc

    in_specs, args, out_specs, out_shape = [], [], [], []
    for a, others in zip(a_list, others_list):
        _, R, N = a.shape
        tr = R // ADD_TILES
        assert tr % 8 == 0, a.shape
        in_specs.append(pl.BlockSpec((1, tr, N), lambda s, i, sel_ref: (a_idx_fn(s, sel_ref), i, 0)))
        args.append(a)
        for arr, fixed in others:
            if fixed is None:
                in_specs.append(pl.BlockSpec((1, tr, N), lambda s, i, sel_ref: (s, i, 0)))
            else:
                in_specs.append(pl.BlockSpec((1, tr, N), lambda s, i, sel_ref, fixed=fixed: (fixed, i, 0)))
            args.append(arr)
        out_specs.append(pl.BlockSpec((1, tr, N), lambda s, i, sel_ref: (out_idx_fn(s, sel_ref), i, 0)))
        out_shape.append(jax.ShapeDtypeStruct((out_blocks, R, N), a.dtype))
    grid_spec = pltpu.PrefetchScalarGridSpec(num_scalar_prefetch=1, grid=(ns, ADD_TILES), in_specs=in_specs,
                                             out_specs=out_specs)
    return pl.pallas_call(
        body, name=name, out_shape=out_shape, grid_spec=grid_spec,
        compiler_params=_cparams(("parallel", "parallel")),
    )(sel, *args)


def _rs_first(g8s, tag):
    c_sel = jnp.reshape(lax.axis_index("c"), (1,)).astype(jnp.int32)
    r1 = _rs_to_sibling(g8s, f"rs_to_sibling_{tag}")
    return _add_blocks(g8s, lambda s, sel: 2 * s + sel[0], [[(r, None)] for r in r1], N_CHIP, c_sel,
                       f"rs_add_sibling_{tag}")


def _rs_last(a4s, r2s, tag):
    sel = jnp.stack([2 * lax.axis_index("x") + lax.axis_index("y"), lax.axis_index("c")]).astype(jnp.int32)
    h = _add_blocks(a4s, lambda s, sel: sel[0], [[(r, 0), (r, 1), (r, 2)] for r in r2s], 1, sel,
                    f"rs_add_chips_{tag}", out_blocks=2, out_idx_fn=lambda s, sel: sel[1])
    full = _swap_halves(h, f"rs_swap_halves_{tag}")
    return [f.reshape(2 * f.shape[1], f.shape[2]) for f in full]


def _reduce_scatter(g8s, tag):
    a4 = _rs_first(g8s, tag)
    return _rs_last(a4, _rs_to_chips(a4, f"rs_to_chips_{tag}"), tag)


def _ada_fwd(c_all, w_ada, b_ada, name):
    nb, D = c_all.shape
    ncol = w_ada.shape[1]
    tc = 512

    def body(c_ref, w_ref, b_ref, o_ref):
        cv = c_ref[...]
        cond = (cv * jax.nn.sigmoid(cv)).astype(BF16)
        o_ref[...] = jnp.dot(cond, w_ref[...].astype(BF16), preferred_element_type=F32) + b_ref[...]

    return pl.pallas_call(
        body, name=name, out_shape=jax.ShapeDtypeStruct((nb, ncol), F32), grid=(ncol // tc,),
        in_specs=[pl.BlockSpec((nb, D), lambda j: (0, 0)), pl.BlockSpec((D, tc), lambda j: (0, j)),
                  pl.BlockSpec((1, tc), lambda j: (0, j))],
        out_specs=pl.BlockSpec((nb, tc), lambda j: (0, j)),
        compiler_params=_cparams(("parallel",)),
    )(c_all, w_ada, b_ada)


def _ada_bwd(c_all, gmod_cols, name):
    nb, D = c_all.shape
    ncol = gmod_cols.shape[1]
    tc = 512

    def body(c_ref, g_ref, o_ref):
        cv = c_ref[...]
        cond = (cv * jax.nn.sigmoid(cv)).astype(BF16)
        o_ref[...] = _dot_tn(cond, g_ref[...].astype(BF16))

    return pl.pallas_call(
        body, name=name, out_shape=jax.ShapeDtypeStruct((D, ncol), F32), grid=(ncol // tc,),
        in_specs=[pl.BlockSpec((nb, D), lambda j: (0, 0)), pl.BlockSpec((nb, tc), lambda j: (0, j))],
        out_specs=pl.BlockSpec((D, tc), lambda j: (0, j)),
        compiler_params=_cparams(("parallel",)),
    )(c_all, gmod_cols)


def _adam_math(w, g, m, v):
    m = ADAM_B1 * m + (1.0 - ADAM_B1) * g
    v = ADAM_B2 * v + (1.0 - ADAM_B2) * (g * g)
    m_hat = m / (1.0 - ADAM_B1 ** ADAM_STEP)
    v_hat = v / (1.0 - ADAM_B2 ** ADAM_STEP)
    delta = -ADAM_LR * (m_hat / (jnp.sqrt(v_hat) + ADAM_EPS) + ADAM_WD * w)
    return delta, m, v


def _adamw(w, g, m, v, name):
    rows, cols = w.shape
    tr = _pick(rows, (256, 192, 176, 128, 64, 8))

    def body(w_ref, g_ref, m_ref, v_ref, d_ref, mo_ref, vo_ref):
        d, mn, vn = _adam_math(w_ref[...], g_ref[...], m_ref[...], v_ref[...])
        d_ref[...] = d
        mo_ref[...] = mn
        vo_ref[...] = vn

    spec = pl.BlockSpec((tr, cols), lambda i: (i, 0))
    return pl.pallas_call(
        body, name=name, out_shape=[jax.ShapeDtypeStruct((rows, cols), F32)] * 3, grid=(rows // tr,),
        in_specs=[spec] * 4, out_specs=[spec] * 3, compiler_params=_cparams(("parallel",)),
    )(w, g, m, v)


VEC_ROWS = 8


def _adamw_rows(w, parts, m, v, name):
    n = w.shape[1]
    P = parts.shape[0]
    assert n % (VEC_ROWS * LANES) == 0, n
    shp = (VEC_ROWS, n // VEC_ROWS)

    def body(w_ref, p_ref, m_ref, v_ref, g_ref, d_ref, mo_ref, vo_ref):
        g = p_ref[0]
        for k in range(1, P):
            g = g + p_ref[k]
        d, mn, vn = _adam_math(w_ref[...], g, m_ref[...], v_ref[...])
        g_ref[...] = g
        d_ref[...] = d
        mo_ref[...] = mn
        vo_ref[...] = vn

    vec = pl.BlockSpec(shp, lambda i: (0, 0))
    out = pl.pallas_call(
        body, name=name, out_shape=[jax.ShapeDtypeStruct(shp, F32)] * 4, grid=(1,),
        in_specs=[vec, pl.BlockSpec((P,) + shp, lambda i: (0, 0, 0)), vec, vec], out_specs=[vec] * 4,
        compiler_params=_cparams(("arbitrary",)),
    )(w.reshape(shp), parts.reshape((P,) + shp), m.reshape(shp), v.reshape(shp))
    return [o.reshape(1, n) for o in out]


_PACKED = (("w_in", 1024, 552), ("w_uq", 384, 192), ("w_ukv", 256, 256))
_SHARDED = ("w_in", "w_uq", "w_ukv", "w_out", "w_ffn_in", "w_ffn_out")
_SMALL = (("g_norm1", 1024), ("g_cq", 384), ("g_ckv", 256), ("rel_bias", 256), ("g_out_a", 512),
          ("g_out_b", 512), ("g_norm2", 1024), ("g_final", 1024))
_SMALL_PAD = 5120
PACK_ROWS = 704
_PACK_ELEMS = PACK_ROWS * D_MODEL


def _pack_shards(shards, dtype):
    lead = shards["w_in"].shape[:-2]
    flat = jnp.concatenate([shards[n].astype(dtype).reshape(lead + (-1,)) for n, _, _ in _PACKED], axis=-1)
    pad = [(0, 0)] * len(lead) + [(0, _PACK_ELEMS - flat.shape[-1])]
    return jnp.pad(flat, pad).reshape(lead + (PACK_ROWS, D_MODEL))


def _unpack_shards(packed):
    out, off = {}, 0
    for n, r, c in _PACKED:
        out[n] = packed[..., off:off + r * c].reshape(packed.shape[:-1] + (r, c))
        off += r * c
    return out


def _full_from_shards(sh):
    return jnp.transpose(sh, (1, 0, 2)).reshape(sh.shape[1], -1)


def _shards_from_full(full):
    rows, cols = full.shape
    return jnp.transpose(full.reshape(rows, N_CHIP, cols // N_CHIP), (1, 0, 2))


def kernel(x, c, w_ada, b_ada, g_norm1, w_in, g_cq, w_uq, g_ckv, w_ukv, rel_bias, g_out_a, g_out_b, w_out, g_norm2, w_ffn_in, w_ffn_out, g_final, loss_target, m_w_ada, m_b_ada, m_g_norm1, m_w_in, m_g_cq, m_w_uq, m_g_ckv, m_w_ukv, m_rel_bias, m_g_out_a, m_g_out_b, m_w_out, m_g_norm2, m_w_ffn_in, m_w_ffn_out, m_g_final, v_w_ada, v_b_ada, v_g_norm1, v_w_in, v_g_cq, v_w_uq, v_g_ckv, v_w_ukv, v_rel_bias, v_g_out_a, v_g_out_b, v_w_out, v_g_norm2, v_w_ffn_in, v_w_ffn_out, v_g_final):
    names = ["w_ada", "b_ada", "g_norm1", "w_in", "g_cq", "w_uq", "g_ckv", "w_ukv", "rel_bias", "g_out_a",
             "g_out_b", "w_out", "g_norm2", "w_ffn_in", "w_ffn_out", "g_final"]
    W = dict(zip(names, [w_ada, b_ada, g_norm1, w_in, g_cq, w_uq, g_ckv, w_ukv, rel_bias, g_out_a, g_out_b,
                         w_out, g_norm2, w_ffn_in, w_ffn_out, g_final]))
    M = dict(zip(names, [m_w_ada, m_b_ada, m_g_norm1, m_w_in, m_g_cq, m_w_uq, m_g_ckv, m_w_ukv, m_rel_bias,
                         m_g_out_a, m_g_out_b, m_w_out, m_g_norm2, m_w_ffn_in, m_w_ffn_out, m_g_final]))
    V = dict(zip(names, [v_w_ada, v_b_ada, v_g_norm1, v_w_in, v_g_cq, v_w_uq, v_g_ckv, v_w_ukv, v_rel_bias,
                         v_g_out_a, v_g_out_b, v_w_out, v_g_norm2, v_w_ffn_in, v_w_ffn_out, v_g_final]))
    B, S, D = x.shape
    mx, my, mc = _my_place()
    dev = 4 * mx + 2 * my + mc
    chip = 2 * mx + my
    pad_rows = 8

    c_all = _allgather8(jnp.pad(c, ((0, pad_rows - B), (0, 0))), "ag_c", False)
    c_all = c_all.reshape(N_DEV, pad_rows, D)[:, :B].reshape(N_DEV * B, D)
    ada_cols = w_ada.shape[-1]
    b_cols = lax.dynamic_slice_in_dim(b_ada, chip * ada_cols, ada_cols, axis=1)
    mod_cols = _ada_fwd(c_all, w_ada[0], b_cols, "ada_fwd")
    mod_all = _allgather8(mod_cols, "ag_mod", False).reshape(N_DEV, N_DEV * B, ada_cols)[0::2]
    mod_all = jnp.transpose(mod_all, (1, 0, 2)).reshape(N_DEV * B, N_MOD * D)
    mod = lax.dynamic_slice_in_dim(mod_all, dev * B, B, axis=0)

    packed = _pack_shards({n: W[n][0] for n, _, _ in _PACKED}, BF16)
    g_packed, g_out = _gather_weights([packed, w_out[0].astype(BF16)], "ag_weights")
    full = {n: _full_from_shards(sh) for n, sh in _unpack_shards(g_packed.reshape(N_CHIP, _PACK_ELEMS)).items()}
    wts = dict(w_in=_w_in_to_kernel(full["w_in"]), w_uq=_w_uq_to_kernel(full["w_uq"]),
               w_kv=_w_ukv_to_kernel(full["w_ukv"]), w_out=g_out.reshape(D, D))
    gains = dict(g_norm1=g_norm1, g_cq=g_cq, g_ckv=g_ckv, g_out_a=g_out_a, g_out_b=g_out_b, g_norm2=g_norm2,
                 g_final=g_final.reshape(1, D))

    loss, grad_x, gmod, grads = _local_step(x, loss_target, mod, wts, gains, rel_bias,
                                            ffn_shards=[w_ffn_in[0].astype(BF16), w_ffn_out[0].astype(BF16)])
    loss = lax.psum(loss[0, 0], ("x", "y", "c"))

    n_small = _SMALL_PAD
    cat = lambda dct: jnp.concatenate([dct[n].reshape(1, -1) for n, _ in _SMALL]
                                      + [jnp.zeros((1, _SMALL_PAD - sum(s for _, s in _SMALL)), F32)], axis=1)
    small = cat(grads)
    rows = jnp.concatenate([gmod, jnp.pad(small, ((0, 0), (0, N_MOD * D - n_small))),
                            jnp.zeros((pad_rows - B - 1, N_MOD * D), F32)], axis=0)
    rows_all = _allgather8(rows, "ag_small", False).reshape(N_DEV, pad_rows, N_MOD * D)
    gmod_all = rows_all[:, :B].reshape(N_DEV * B, N_MOD * D)
    small_parts = rows_all[:, B, :n_small]

    nat = dict(w_in=_w_in_from_kernel(grads["w_in"]), w_uq=_w_uq_from_kernel(grads["w_uq"]),
               w_ukv=_w_ukv_from_kernel(grads["w_kv"]))
    gp = _pack_shards({n: _shards_from_full(nat[n]) for n, _, _ in _PACKED}, F32)
    as_halves = lambda a: a.reshape(N_DEV, -1, a.shape[-1])
    a4 = _rs_first([as_halves(gp), as_halves(grads["w_out"])], "mix")
    r2 = _rs_to_chips(a4, "rs_to_chips_mix")
    ffn_a4, ffn_r2 = grads["ffn_pending"]
    r_packed, r_out, r_ffn_in, r_ffn_out = _rs_last(list(a4) + list(ffn_a4), list(r2) + list(ffn_r2), "all")
    G = _unpack_shards(r_packed.reshape(_PACK_ELEMS))
    G.update(w_out=r_out, w_ffn_in=r_ffn_in, w_ffn_out=r_ffn_out)

    gmod_cols = lax.dynamic_slice_in_dim(gmod_all, chip * ada_cols, ada_cols, axis=1)
    G["w_ada"] = _ada_bwd(c_all, gmod_cols, "ada_bwd")
    delta, new_m, new_v = {}, {}, {}
    for n in ("w_ada",) + _SHARDED:
        shp = W[n].shape
        w2 = W[n].reshape(shp[-2], shp[-1])
        d_, m_, v_ = _adamw(w2, G[n], M[n].reshape(w2.shape), V[n].reshape(w2.shape), f"adamw_{n}")
        G[n], delta[n], new_m[n], new_v[n] = [a.reshape(shp) for a in (G[n], d_, m_, v_)]
    gs, ds_, ms_, vs_ = _adamw_rows(cat(W), small_parts, cat(M), cat(V), "adamw_small")
    off = 0
    for n, sz in _SMALL:
        shp = W[n].shape
        G[n], delta[n], new_m[n], new_v[n] = [a[:, off:off + sz].reshape(shp) for a in (gs, ds_, ms_, vs_)]
        off += sz
    G["b_ada"], delta["b_ada"], new_m["b_ada"], new_v["b_ada"] = _adamw_rows(b_ada, gmod_all, m_b_ada, v_b_ada,
                                                                          "adamw_b_ada")
    return (loss, grad_x, *[G[n] for n in names], *[delta[n] for n in names], *[new_m[n] for n in names],
            *[new_v[n] for n in names])
```

```python
import functools
import math

import numpy as np
import jax
import jax.numpy as jnp
from jax import lax
from jax.experimental import pallas as pl
from jax.experimental.pallas import tpu as pltpu

F32 = jnp.float32
BF16 = jnp.bfloat16

D_MODEL = 1024
SEQ = 2048
N_HEADS = 8
HEAD_DIM = 64
D_A = 512
D_B = 512
Q_LORA = 384
KV_LORA = 256
ROPE_DIM = 32
NOPE_DIM = 64
D_FF = 2816
N_MOD = 6
N_BUCKETS = 32
MAX_DISTANCE = 2048
ROPE_THETA = 10000.0
EPS = 1e-6
NEG = -1e30
BLK = 128
DILATIONS = (1, 4, 16)
SPAN = 128
MLA_SCALE = (NOPE_DIM + ROPE_DIM) ** -0.5
DIL_SCALE = HEAD_DIM ** -0.5

ADAM_LR = 0.001
ADAM_B1 = 0.9
ADAM_B2 = 0.999
ADAM_EPS = 1e-08
ADAM_WD = 0.01
ADAM_STEP = 10

N_DEV = 8
N_CHIP = 4
LANES = 128
VMEM_LIMIT = 48 * 1024 * 1024

P_QKV = 3 * D_A
P_REST = KV_LORA + LANES + Q_LORA


def _cparams(sem=None):
    return pltpu.CompilerParams(dimension_semantics=sem, vmem_limit_bytes=VMEM_LIMIT)


def _pick(n, cands):
    for c in cands:
        if n % c == 0:
            return c
    raise ValueError(f"no tile for {n} in {cands}")


def _mm(a, b, mode, out_dtype, name, col_blocks=None):
    blocked = col_blocks is not None
    if mode == "nn":
        (M, K) = a.shape
        K2, N = (b.shape[1], b.shape[0] * b.shape[2]) if blocked else b.shape
    elif mode == "nt":
        (M, K) = a.shape
        N, K2 = (b.shape[1], b.shape[0] * b.shape[2]) if blocked else b.shape
    else:
        (K, M), (K2, N) = a.shape, b.shape
    assert K == K2, (a.shape, b.shape, mode)
    tm = _pick(M, (512, 384, 256, 128))
    tn = _pick(N, (1408, 1024, 768, 512, 384, 256, 128))
    tk = _pick(K, (1024, 512, 384, 256, 128))
    if blocked and mode == "nt":
        tk = K // col_blocks
    elif blocked:
        tn = N // col_blocks
    nk = K // tk
    out_shape = (M, N)
    out_spec = pl.BlockSpec((tm, tn), lambda i, j, k: (i, j))
    if mode == "nn":
        a_spec = pl.BlockSpec((tm, tk), lambda i, j, k: (i, k))
        b_spec = (pl.BlockSpec((None, tk, tn), lambda i, j, k: (j, k, 0)) if blocked
                  else pl.BlockSpec((tk, tn), lambda i, j, k: (k, j)))
        dn = (((1,), (0,)), ((), ()))
    elif mode == "nt":
        a_spec = pl.BlockSpec((tm, tk), lambda i, j, k: (i, k))
        b_spec = (pl.BlockSpec((None, tn, tk), lambda i, j, k: (k, j, 0)) if blocked
                  else pl.BlockSpec((tn, tk), lambda i, j, k: (j, k)))
        dn = (((1,), (1,)), ((), ()))
    else:
        a_spec = pl.BlockSpec((tk, tm), lambda i, j, k: (k, i))
        b_spec = pl.BlockSpec((tk, tn), lambda i, j, k: (k, j))
        dn = (((0,), (0,)), ((), ()))
        if blocked:
            out_shape = (col_blocks, M, tn)
            out_spec = pl.BlockSpec((None, tm, tn), lambda i, j, k: (j, i, 0))

    def body(a_ref, b_ref, o_ref, acc_ref):
        k = pl.program_id(2)

        @pl.when(k == 0)
        def _():
            acc_ref[...] = jnp.zeros_like(acc_ref)

        acc_ref[...] += lax.dot_general(a_ref[...].astype(BF16), b_ref[...].astype(BF16), dn,
                                        preferred_element_type=F32)

        @pl.when(k == nk - 1)
        def _():
            o_ref[...] = acc_ref[...].astype(o_ref.dtype)

    return pl.pallas_call(
        body, name=name,
        out_shape=jax.ShapeDtypeStruct(out_shape, out_dtype),
        grid=(M // tm, N // tn, nk),
        in_specs=[a_spec, b_spec],
        out_specs=out_spec,
        scratch_shapes=[pltpu.VMEM((tm, tn), F32)],
        compiler_params=_cparams(("parallel", "parallel", "arbitrary")),
    )(a, b)


ROW_TILE = 256


def _adaln_fwd(x, g, sc, sh, name, mix=None, gate=None):
    B, S, D = x.shape
    ts = ROW_TILE
    has_res = mix is not None

    def body(*refs):
        if has_res:
            x_ref, g_ref, sc_ref, sh_ref, mix_ref, gate_ref, h_ref, xr_ref = refs
            xr = x_ref[0] + gate_ref[0] * mix_ref[0]
            xr_ref[0] = xr
        else:
            x_ref, g_ref, sc_ref, sh_ref, h_ref = refs
            xr = x_ref[0]
        r = lax.rsqrt(jnp.mean(xr * xr, axis=-1, keepdims=True) + EPS)
        xn = (xr * r) * g_ref[...]
        h_ref[0] = (xn * (1.0 + sc_ref[0]) + sh_ref[0]).astype(h_ref.dtype)

    tok = pl.BlockSpec((1, ts, D), lambda b, s: (b, s, 0))
    per_b = pl.BlockSpec((1, 1, D), lambda b, s: (b, 0, 0))
    vec = pl.BlockSpec((1, D), lambda b, s: (0, 0))
    in_specs = [tok, vec, per_b, per_b]
    args = [x, g, sc, sh]
    out_shape = [jax.ShapeDtypeStruct((B, S, D), BF16)]
    out_specs = [tok]
    if has_res:
        in_specs += [tok, per_b]
        args += [mix, gate]
        out_shape.append(jax.ShapeDtypeStruct((B, S, D), F32))
        out_specs.append(tok)
    out = pl.pallas_call(
        body, name=name, out_shape=out_shape, grid=(B, S // ts),
        in_specs=in_specs, out_specs=out_specs,
        compiler_params=_cparams(("parallel", "parallel")),
    )(*args)
    return out if has_res else out[0]


def _adaln_bwd(dh, x, g, sc, dres, name, mix=None, gate=None):
    B, S, D = x.shape
    ts = ROW_TILE
    has_res = mix is not None

    def body(*refs):
        if has_res:
            (dh_ref, x_ref, g_ref, sc_ref, dres_ref, mix_ref, gate_ref,
             dx_ref, dsh_ref, dsc_ref, dg_ref, dgate_ref, dmix_ref) = refs
        else:
            (dh_ref, x_ref, g_ref, sc_ref, dres_ref, dx_ref, dsh_ref, dsc_ref, dg_ref) = refs
        b, s = pl.program_id(0), pl.program_id(1)
        xv = x_ref[0]
        dhv = dh_ref[0]
        gv = g_ref[...]
        r = lax.rsqrt(jnp.mean(xv * xv, axis=-1, keepdims=True) + EPS)
        n = xv * r
        xn = n * gv
        dxn = dhv * (1.0 + sc_ref[0])
        dn = dxn * gv
        dx = r * (dn - n * jnp.mean(dn * n, axis=-1, keepdims=True)) + dres_ref[0]
        dx_ref[0] = dx

        @pl.when(s == 0)
        def _():
            dsh_ref[...] = jnp.zeros_like(dsh_ref)
            dsc_ref[...] = jnp.zeros_like(dsc_ref)
            if has_res:
                dgate_ref[...] = jnp.zeros_like(dgate_ref)

        @pl.when((s == 0) & (b == 0))
        def _():
            dg_ref[...] = jnp.zeros_like(dg_ref)

        dsh_ref[0] += jnp.sum(dhv, axis=0, keepdims=True)
        dsc_ref[0] += jnp.sum(dhv * xn, axis=0, keepdims=True)
        dg_ref[...] += jnp.sum(dxn * n, axis=0, keepdims=True)
        if has_res:
            dgate_ref[0] += jnp.sum(dx * mix_ref[0], axis=0, keepdims=True)
            dmix_ref[0] = (dx * gate_ref[0]).astype(dmix_ref.dtype)

    tok = pl.BlockSpec((1, ts, D), lambda b, s: (b, s, 0))
    per_b = pl.BlockSpec((1, 1, D), lambda b, s: (b, 0, 0))
    vec = pl.BlockSpec((1, D), lambda b, s: (0, 0))
    in_specs = [tok, tok, vec, per_b, tok]
    args = [dh, x, g, sc, dres]
    out_shape = [jax.ShapeDtypeStruct((B, S, D), F32), jax.ShapeDtypeStruct((B, 1, D), F32),
                 jax.ShapeDtypeStruct((B, 1, D), F32), jax.ShapeDtypeStruct((1, D), F32)]
    out_specs = [tok, per_b, per_b, vec]
    if has_res:
        in_specs += [tok, per_b]
        args += [mix, gate]
        out_shape += [jax.ShapeDtypeStruct((B, 1, D), F32), jax.ShapeDtypeStruct((B, S, D), BF16)]
        out_specs += [per_b, tok]
    return pl.pallas_call(
        body, name=name, out_shape=out_shape, grid=(B, S // ts),
        in_specs=in_specs, out_specs=out_specs,
        compiler_params=_cparams(("arbitrary", "arbitrary")),
    )(*args)


def _rms_fwd(x, col_blk, n, g, name, n_real=None):
    T = x.shape[0]
    tr = 512
    nr = float(n_real or n)

    def body(x_ref, g_ref, y_ref):
        xv = x_ref[...]
        r = lax.rsqrt(jnp.sum(xv * xv, axis=-1, keepdims=True) / nr + EPS)
        y_ref[...] = ((xv * r) * g_ref[...]).astype(y_ref.dtype)

    return pl.pallas_call(
        body, name=name, out_shape=jax.ShapeDtypeStruct((T, n), BF16), grid=(T // tr,),
        in_specs=[pl.BlockSpec((tr, n), lambda i: (i, col_blk)), pl.BlockSpec((1, n), lambda i: (0, 0))],
        out_specs=pl.BlockSpec((tr, n), lambda i: (i, 0)),
        compiler_params=_cparams(("parallel",)),
    )(x, g)


def _rms_bwd(dy, dy_blk, x, x_blk, n, g, name, out_dtype=BF16):
    T = x.shape[0]
    tr = 512

    def body(dy_ref, x_ref, g_ref, dx_ref, dg_ref):
        xv = x_ref[...]
        dyv = dy_ref[...].astype(F32)
        r = lax.rsqrt(jnp.mean(xv * xv, axis=-1, keepdims=True) + EPS)
        nrm = xv * r
        dn = dyv * g_ref[...]
        dx_ref[...] = (r * (dn - nrm * jnp.mean(dn * nrm, axis=-1, keepdims=True))).astype(dx_ref.dtype)

        @pl.when(pl.program_id(0) == 0)
        def _():
            dg_ref[...] = jnp.zeros_like(dg_ref)

        dg_ref[...] += jnp.sum(dyv * nrm, axis=0, keepdims=True)

    return pl.pallas_call(
        body, name=name,
        out_shape=[jax.ShapeDtypeStruct((T, n), out_dtype), jax.ShapeDtypeStruct((1, n), F32)],
        grid=(T // tr,),
        in_specs=[pl.BlockSpec((tr, n), lambda i: (i, dy_blk)), pl.BlockSpec((tr, n), lambda i: (i, x_blk)),
                  pl.BlockSpec((1, n), lambda i: (0, 0))],
        out_specs=[pl.BlockSpec((tr, n), lambda i: (i, 0)), pl.BlockSpec((1, n), lambda i: (0, 0))],
        compiler_params=_cparams(("arbitrary",)),
    )(dy, x, g)


def _rms_bwd_views(dy, dy_blk, x, g, name):
    B, S, n = x.shape
    tiles = S // VIEW_TILE

    def body(dy_ref, x_ref, g_ref, d1_ref, d4_ref, d16_ref, dg_ref, dx_s):
        xv = x_ref[0]
        dyv = dy_ref[...]
        r = lax.rsqrt(jnp.mean(xv * xv, axis=-1, keepdims=True) + EPS)
        nrm = xv * r
        dn = dyv * g_ref[...]
        dx = r * (dn - nrm * jnp.mean(dn * nrm, axis=-1, keepdims=True))
        d1_ref[0] = dx.astype(d1_ref.dtype)
        _put_tile(dx_s, dx)
        _tile_to_view(dx_s, d4_ref, DILATIONS[1], n)
        _tile_to_view(dx_s, d16_ref, DILATIONS[2], n)

        @pl.when((pl.program_id(0) == 0) & (pl.program_id(1) == 0))
        def _():
            dg_ref[...] = jnp.zeros_like(dg_ref)

        dg_ref[...] += jnp.sum(dyv * nrm, axis=0, keepdims=True)

    res = pl.pallas_call(
        body, name=name,
        out_shape=[_view_shape(B, S, d, n, BF16) for d in DILATIONS] + [jax.ShapeDtypeStruct((1, n), F32)],
        grid=(B, tiles),
        in_specs=[pl.BlockSpec((VIEW_TILE, n), lambda b, t: (b * tiles + t, dy_blk)), _view_spec(1, n),
                  pl.BlockSpec((1, n), lambda b, t: (0, 0))],
        out_specs=[_view_spec(d, n) for d in DILATIONS] + [pl.BlockSpec((1, n), lambda b, t: (0, 0))],
        scratch_shapes=[_tile_scratch(n)],
        compiler_params=_cparams(("arbitrary", "arbitrary")),
    )(dy, x, g)
    return res[:len(DILATIONS)], res[len(DILATIONS)]


def _swiglu_fwd(gu, name):
    T = gu.shape[0]
    tr, tc = 512, 1408
    nc = D_FF // tc

    def body(g_ref, u_ref, a_ref):
        gv = g_ref[...]
        a_ref[...] = (gv * jax.nn.sigmoid(gv) * u_ref[...]).astype(a_ref.dtype)

    return pl.pallas_call(
        body, name=name, out_shape=jax.ShapeDtypeStruct((T, D_FF), BF16), grid=(T // tr, nc),
        in_specs=[pl.BlockSpec((tr, tc), lambda i, j: (i, j)), pl.BlockSpec((tr, tc), lambda i, j: (i, j + nc))],
        out_specs=pl.BlockSpec((tr, tc), lambda i, j: (i, j)),
        compiler_params=_cparams(("parallel", "parallel")),
    )(gu, gu)


def _swiglu_bwd(da, gu, name):
    T = gu.shape[0]
    tr, tc = 512, 1408
    nc = D_FF // tc

    def body(da_ref, g_ref, u_ref, dgu_ref):
        j = pl.program_id(1)
        gv, uv, dav = g_ref[...], u_ref[...], da_ref[...]
        sg = jax.nn.sigmoid(gv)

        @pl.when(j < nc)
        def _():
            dgu_ref[...] = (dav * uv * (sg * (1.0 + gv * (1.0 - sg)))).astype(dgu_ref.dtype)

        @pl.when(j >= nc)
        def _():
            dgu_ref[...] = (dav * (gv * sg)).astype(dgu_ref.dtype)

    return pl.pallas_call(
        body, name=name, out_shape=jax.ShapeDtypeStruct((T, 2 * D_FF), BF16), grid=(T // tr, 2 * nc),
        in_specs=[pl.BlockSpec((tr, tc), lambda i, j: (i, j % nc)),
                  pl.BlockSpec((tr, tc), lambda i, j: (i, j % nc)),
                  pl.BlockSpec((tr, tc), lambda i, j: (i, j % nc + nc))],
        out_specs=pl.BlockSpec((tr, tc), lambda i, j: (i, j)),
        compiler_params=_cparams(("parallel", "parallel")),
    )(da, gu, gu)


def _final_loss(x1, f, g2, gf, target, name):
    B, S, D = x1.shape
    ts = ROW_TILE

    def body(x1_ref, f_ref, g2_ref, gf_ref, t_ref, dx_ref, df_ref, dg2_ref, dgf_ref, loss_ref):
        b, s = pl.program_id(0), pl.program_id(1)
        fv = f_ref[0]
        g2v = g2_ref[0]
        gfv = gf_ref[...]
        x2 = x1_ref[0] + g2v * fv
        r = lax.rsqrt(jnp.mean(x2 * x2, axis=-1, keepdims=True) + EPS)
        n = x2 * r
        e = n * gfv - t_ref[0]
        dy = e * (1.0 / D)
        dn = dy * gfv
        dx = r * (dn - n * jnp.mean(dn * n, axis=-1, keepdims=True))
        dx_ref[0] = dx
        df_ref[0] = (dx * g2v).astype(df_ref.dtype)

        @pl.when(s == 0)
        def _():
            dg2_ref[...] = jnp.zeros_like(dg2_ref)

        @pl.when((s == 0) & (b == 0))
        def _():
            dgf_ref[...] = jnp.zeros_like(dgf_ref)
            loss_ref[...] = jnp.zeros_like(loss_ref)

        dg2_ref[0] += jnp.sum(dx * fv, axis=0, keepdims=True)
        dgf_ref[...] += jnp.sum(dy * n, axis=0, keepdims=True)
        loss_ref[...] += 0.5 * jnp.sum(jnp.mean(e * e, axis=-1, keepdims=True), axis=0, keepdims=True)

    tok = pl.BlockSpec((1, ts, D), lambda b, s: (b, s, 0))
    per_b = pl.BlockSpec((1, 1, D), lambda b, s: (b, 0, 0))
    vec = pl.BlockSpec((1, D), lambda b, s: (0, 0))
    return pl.pallas_call(
        body, name=name,
        out_shape=[jax.ShapeDtypeStruct((B, S, D), F32), jax.ShapeDtypeStruct((B, S, D), BF16),
                   jax.ShapeDtypeStruct((B, 1, D), F32), jax.ShapeDtypeStruct((1, D), F32),
                   jax.ShapeDtypeStruct((1, LANES), F32)],
        grid=(B, S // ts),
        in_specs=[tok, tok, per_b, vec, tok],
        out_specs=[tok, tok, per_b, vec, pl.BlockSpec((1, LANES), lambda b, s: (0, 0))],
        compiler_params=_cparams(("arbitrary", "arbitrary")),
    )(x1, f, g2, gf, target)


def _rope_tables():
    half = ROPE_DIM // 2
    inv = ROPE_THETA ** (-jnp.arange(half, dtype=F32) / half)
    ang = jnp.arange(SEQ, dtype=F32)[:, None] * inv[None, :]
    cos, sin = jnp.cos(ang), jnp.sin(ang)
    one = jnp.ones((SEQ, NOPE_DIM), F32)
    zero = jnp.zeros((SEQ, NOPE_DIM), F32)
    cs = jnp.concatenate([one, cos, cos, one[:, :LANES - NOPE_DIM - ROPE_DIM]], axis=1)
    sn = jnp.concatenate([zero, -sin, sin, zero[:, :LANES - NOPE_DIM - ROPE_DIM]], axis=1)
    return cs, sn


def _rope_group(t, cs, sn):
    half = ROPE_DIM // 2
    lane = lax.broadcasted_iota(jnp.int32, t.shape, 1)
    partner = jnp.where(lane < NOPE_DIM + half, pltpu.roll(t, LANES - half, 1), pltpu.roll(t, half, 1))
    return t * cs + partner * sn


def _rope_apply(t, cs, sn, out_dtype, name, add=None, add_blk=0):
    B, S, W = t.shape
    G = W // LANES
    ts = ROW_TILE

    def body(*refs):
        if add is None:
            t_ref, cs_ref, sn_ref, o_ref = refs
            for gi in range(G):
                sl = slice(gi * LANES, (gi + 1) * LANES)
                o_ref[0, :, sl] = _rope_group(t_ref[0, :, sl], cs_ref[...], sn_ref[...]).astype(o_ref.dtype)
        else:
            t_ref, a_ref, cs_ref, sn_ref, o_ref = refs
            ra = _rope_group(a_ref[0], cs_ref[...], sn_ref[...])
            for gi in range(G):
                sl = slice(gi * LANES, (gi + 1) * LANES)
                o_ref[0, :, sl] = (t_ref[0, :, sl] + ra).astype(o_ref.dtype)

    tok = pl.BlockSpec((1, ts, W), lambda b, s: (b, s, 0))
    tab = pl.BlockSpec((ts, LANES), lambda b, s: (s, 0))
    in_specs, args = [tok], [t]
    if add is not None:
        in_specs.append(pl.BlockSpec((1, ts, LANES), lambda b, s: (b, s, add_blk)))
        args.append(add)
    in_specs += [tab, tab]
    args += [cs, sn]
    return pl.pallas_call(
        body, name=name, out_shape=jax.ShapeDtypeStruct((B, S, W), out_dtype), grid=(B, S // ts),
        in_specs=in_specs, out_specs=tok, compiler_params=_cparams(("parallel", "parallel")),
    )(*args)


def _krope_bwd(dkc, cs, sn_neg, name):
    B, S, W = dkc.shape
    G = W // LANES
    ts = ROW_TILE

    def body(d_ref, cs_ref, sn_ref, o_ref):
        acc = d_ref[0, :, 0:LANES]
        for gi in range(1, G):
            acc = acc + d_ref[0, :, gi * LANES:(gi + 1) * LANES]
        lane = lax.broadcasted_iota(jnp.int32, acc.shape, 1)
        rot = (lane >= NOPE_DIM) & (lane < NOPE_DIM + ROPE_DIM)
        acc = jnp.where(rot, acc, 0.0)
        o_ref[0] = _rope_group(acc, cs_ref[...], sn_ref[...]).astype(o_ref.dtype)

    tab = pl.BlockSpec((ts, LANES), lambda b, s: (s, 0))
    return pl.pallas_call(
        body, name=name, out_shape=jax.ShapeDtypeStruct((B, S, LANES), BF16), grid=(B, S // ts),
        in_specs=[pl.BlockSpec((1, ts, W), lambda b, s: (b, s, 0)), tab, tab],
        out_specs=pl.BlockSpec((1, ts, LANES), lambda b, s: (b, s, 0)),
        compiler_params=_cparams(("parallel", "parallel")),
    )(dkc, cs, sn_neg)


def _t5_bucket(dist):
    max_exact = N_BUCKETS // 2
    d = np.maximum(dist, 1).astype(np.float64)
    large = max_exact + (np.log(d / max_exact) / np.log(MAX_DISTANCE / max_exact)
                         * (N_BUCKETS - max_exact)).astype(np.int64)
    large = np.minimum(large, N_BUCKETS - 1)
    return np.where(dist < max_exact, dist, large).astype(np.int32)


def _band_buckets(dilation):
    a = np.arange(BLK)[None, :]
    bk = np.arange(2 * BLK)[:, None]
    steps = BLK + a - bk
    return _t5_bucket(np.clip(steps, 0, SPAN) * dilation)


def _head_mask(shape, hh):
    lane = lax.broadcasted_iota(jnp.int32, shape, 1)
    return (lane >= hh * HEAD_DIM) & (lane < (hh + 1) * HEAD_DIM)


def _dot_nt(a, b):
    return lax.dot_general(a, b, (((1,), (1,)), ((), ())), preferred_element_type=F32)


def _dot_tn(a, b):
    return lax.dot_general(a, b, (((0,), (0,)), ((), ())), preferred_element_type=F32)


def _dot_nn(a, b):
    return lax.dot_general(a, b, (((1,), (0,)), ((), ())), preferred_element_type=F32)


def _band_valid_t():
    bk = lax.broadcasted_iota(jnp.int32, (BLK, BLK), 0)
    a = lax.broadcasted_iota(jnp.int32, (BLK, BLK), 1)
    return bk >= a, bk <= a


def _dil_fwd(qkv, bias, branch, dilation, name):
    B, n, _ = qkv.shape
    d = dilation
    nb = n // BLK
    qkv_v = qkv
    npair = N_HEADS // 2

    def body(cur_ref, prev_ref, bias_ref, o_ref, lse_ref, s_scr, e_scr):
        i = pl.program_id(2)
        vprev, vcur = _band_valid_t()
        vprev = vprev & (i > 0)
        for p in range(npair):
            q = cur_ref[0, :, p * LANES:(p + 1) * LANES]
            kc = cur_ref[0, :, D_A + p * LANES:D_A + (p + 1) * LANES]
            kp = prev_ref[0, :, D_A + p * LANES:D_A + (p + 1) * LANES]
            for hh in range(2):
                h = 2 * p + hh
                qm = jnp.where(_head_mask((BLK, LANES), hh), q, jnp.zeros_like(q))
                s_scr[h, 0:BLK, :] = _dot_nt(kp, qm)
                s_scr[h, BLK:2 * BLK, :] = _dot_nt(kc, qm)
        ms = []
        for h in range(N_HEADS):
            s_p = jnp.where(vprev, s_scr[h, 0:BLK, :] * DIL_SCALE + bias_ref[h, 0:BLK, :], NEG)
            s_c = jnp.where(vcur, s_scr[h, BLK:2 * BLK, :] * DIL_SCALE + bias_ref[h, BLK:2 * BLK, :], NEG)
            m = jnp.maximum(jnp.max(s_p, axis=0, keepdims=True), jnp.max(s_c, axis=0, keepdims=True))
            e_scr[h, 0:BLK, :] = jnp.exp(s_p - m).astype(BF16)
            e_scr[h, BLK:2 * BLK, :] = jnp.exp(s_c - m).astype(BF16)
            ms.append(m)
        rows0 = _row_mask((LANES, BLK), 0)
        for p in range(npair):
            sl = slice(p * LANES, (p + 1) * LANES)
            vsl = slice(2 * D_A + p * LANES, 2 * D_A + (p + 1) * LANES)
            vct = jnp.transpose(cur_ref[0, :, vsl].astype(F32)).astype(BF16)
            vpt = jnp.transpose(prev_ref[0, :, vsl].astype(F32)).astype(BF16)
            acc = []
            for hh in range(2):
                h = 2 * p + hh
                mine = _row_mask((LANES, BLK), hh)
                one = jnp.ones_like(vct)
                acc.append(_dot_nn(jnp.where(mine, vpt, one), e_scr[h, 0:BLK, :])
                           + _dot_nn(jnp.where(mine, vct, one), e_scr[h, BLK:2 * BLK, :]))
            l0 = acc[0][HEAD_DIM:HEAD_DIM + 1, :]
            l1 = acc[1][0:1, :]
            o_t = jnp.where(rows0, acc[0] / l0, acc[1] / l1)
            lse_t = jnp.where(rows0, ms[2 * p] + jnp.log(l0), ms[2 * p + 1] + jnp.log(l1))
            o_ref[0, :, sl] = jnp.transpose(o_t)
            lse_ref[0, :, sl] = jnp.transpose(lse_t)

    cur = pl.BlockSpec((1, BLK, P_QKV), lambda b, r, i: (b, i, r))
    prev = pl.BlockSpec((1, BLK, P_QKV), lambda b, r, i: (b, jnp.maximum(i - 1, 0), r))
    out = pl.BlockSpec((1, BLK, D_A), lambda b, r, i: (b, i, r))
    o, lse = pl.pallas_call(
        body, name=name,
        out_shape=[jax.ShapeDtypeStruct((B, n, d * D_A), F32)] * 2,
        grid=(B, d, nb),
        in_specs=[cur, prev,
                  pl.BlockSpec((None, N_HEADS, 2 * BLK, BLK), lambda b, r, i: (branch, 0, 0, 0))],
        out_specs=[out, out],
        scratch_shapes=[pltpu.VMEM((N_HEADS, 2 * BLK, BLK), F32), pltpu.VMEM((N_HEADS, 2 * BLK, BLK), BF16)],
        compiler_params=_cparams(("parallel", "parallel", "arbitrary")),
    )(qkv_v, qkv_v, bias)
    return o, lse


VIEW_TILE = 512


def _view_spec(d, w):
    return pl.BlockSpec((1, VIEW_TILE // d, d * w), lambda b, t: (b, t, 0))


def _view_shape(B, S, d, w, dtype):
    return jax.ShapeDtypeStruct((B, S // d, d * w), dtype)


def _tile_scratch(w):
    return pltpu.VMEM((w // LANES, VIEW_TILE, LANES), F32)


def _put_tile(tile_ref, val):
    for c in range(tile_ref.shape[0]):
        tile_ref[c] = val[:, c * LANES:(c + 1) * LANES]


def _get_tile(tile_ref):
    return jnp.concatenate([tile_ref[c] for c in range(tile_ref.shape[0])], axis=1)


def _tile_to_view(tile_ref, view_ref, d, w):
    for c in range(w // LANES):
        for r in range(d):
            lo = r * w + c * LANES
            rows = tile_ref.at[c][pl.ds(r, VIEW_TILE // d, stride=d), :]
            view_ref[0, :, lo:lo + LANES] = rows.astype(view_ref.dtype)


def _view_to_tile(view_ref, tile_ref, d, w):
    for c in range(w // LANES):
        for r in range(d):
            lo = r * w + c * LANES
            tile_ref.at[c][pl.ds(r, VIEW_TILE // d, stride=d), :] = view_ref[0, :, lo:lo + LANES].astype(F32)


def _mm_qkv_views(h, w, name):
    B, S, D = h.shape
    N = w.shape[1]

    def body(h_ref, w_ref, o1_ref, o4_ref, o16_ref, acc_ref):
        acc = jnp.dot(h_ref[0], w_ref[...], preferred_element_type=F32)
        o1_ref[0] = acc.astype(o1_ref.dtype)
        _put_tile(acc_ref, acc)
        _tile_to_view(acc_ref, o4_ref, DILATIONS[1], N)
        _tile_to_view(acc_ref, o16_ref, DILATIONS[2], N)

    return pl.pallas_call(
        body, name=name,
        out_shape=[_view_shape(B, S, d, N, BF16) for d in DILATIONS],
        grid=(B, S // VIEW_TILE),
        in_specs=[pl.BlockSpec((1, VIEW_TILE, D), lambda b, t: (b, t, 0)), pl.BlockSpec((D, N), lambda b, t: (0, 0))],
        out_specs=[_view_spec(d, N) for d in DILATIONS],
        scratch_shapes=[_tile_scratch(N)],
        compiler_params=_cparams(("parallel", "parallel")),
    )(h, w)


def _dil_merge(os_, lses, name):
    B, S, W = os_[0].shape
    nd = len(DILATIONS)

    def body(*refs):
        o_refs, l_refs = refs[:nd], refs[nd:2 * nd]
        out_refs, L_refs = refs[2 * nd:3 * nd], refs[3 * nd:4 * nd]
        scr = refs[4 * nd:]
        o_tok, l_tok = [o_refs[0][0]], [l_refs[0][0]]
        for i, d in enumerate(DILATIONS[1:]):
            _view_to_tile(o_refs[i + 1], scr[2 * i], d, W)
            _view_to_tile(l_refs[i + 1], scr[2 * i + 1], d, W)
            o_tok.append(_get_tile(scr[2 * i]))
            l_tok.append(_get_tile(scr[2 * i + 1]))
        a0, a1, a2 = l_tok
        m = jnp.maximum(jnp.maximum(a0, a1), a2)
        e0, e1, e2 = jnp.exp(a0 - m), jnp.exp(a1 - m), jnp.exp(a2 - m)
        ssum = e0 + e1 + e2
        out = (e0 * o_tok[0] + e1 * o_tok[1] + e2 * o_tok[2]) / ssum
        lse = m + jnp.log(ssum)
        out_refs[0][0] = out
        L_refs[0][0] = lse
        res_o, res_l = scr[2 * (nd - 1)], scr[2 * (nd - 1) + 1]
        _put_tile(res_o, out)
        _put_tile(res_l, lse)
        for i, d in enumerate(DILATIONS[1:]):
            _tile_to_view(res_o, out_refs[i + 1], d, W)
            _tile_to_view(res_l, L_refs[i + 1], d, W)

    specs = [_view_spec(d, W) for d in DILATIONS]
    shapes = [_view_shape(B, S * DILATIONS[0], d, W, F32) for d in DILATIONS]
    res = pl.pallas_call(
        body, name=name, out_shape=shapes * 2, grid=(B, S // VIEW_TILE),
        in_specs=specs * 2, out_specs=specs * 2,
        scratch_shapes=[_tile_scratch(W)] * (2 * nd),
        compiler_params=_cparams(("parallel", "parallel")),
    )(*os_, *lses)
    return res[:nd], res[nd:]


def _dil_bwd(qkv, do, out_a, L, bias, branch, dilation, name):
    B, n, _ = qkv.shape
    d = dilation
    nb = n // BLK
    qkv_v, do_v, oa_v, L_v = qkv, do, out_a, L
    npair = N_HEADS // 2
    multi = nb > 1

    tiles = ("P", "C", "N") if multi else ("C",)
    n_t = len(tiles)

    def body(*refs):
        if multi:
            (cur_ref, prev_ref, next_ref, do_ref, don_ref, oa_ref, oan_ref, L_ref, Ln_ref, bias_ref,
             dqkv_ref, dbias_ref, s_scr, dp_scr, p_scr, ds_scr) = refs
        else:
            cur_ref, do_ref, oa_ref, L_ref, bias_ref, dqkv_ref, dbias_ref, s_scr, dp_scr, p_scr, ds_scr = refs
        b, r, i = pl.program_id(0), pl.program_id(1), pl.program_id(2)

        @pl.when((b == 0) & (r == 0) & (i == 0))
        def _():
            dbias_ref[...] = jnp.zeros_like(dbias_ref)

        vprev, vcur = _band_valid_t()
        valid = {"P": vprev & (i > 0), "C": vcur, "N": vprev & (i < nb - 1)}
        band = {"P": slice(0, BLK), "C": slice(BLK, 2 * BLK), "N": slice(0, BLK)}
        psl = lambda p: slice(p * LANES, (p + 1) * LANES)
        ksl = lambda p: slice(D_A + p * LANES, D_A + (p + 1) * LANES)
        vsl = lambda p: slice(2 * D_A + p * LANES, 2 * D_A + (p + 1) * LANES)

        def operands(p, hh):
            hm = _head_mask((BLK, LANES), hh)
            mask = lambda x: jnp.where(hm, x, jnp.zeros_like(x))
            qm, dom = mask(cur_ref[0, :, psl(p)]), mask(do_ref[0, :, psl(p)])
            ops = {"C": (cur_ref[0, :, ksl(p)], cur_ref[0, :, vsl(p)], qm, dom)}
            if multi:
                ops["P"] = (prev_ref[0, :, ksl(p)], prev_ref[0, :, vsl(p)], qm, dom)
                ops["N"] = (cur_ref[0, :, ksl(p)], cur_ref[0, :, vsl(p)], mask(next_ref[0, :, psl(p)]),
                            mask(don_ref[0, :, psl(p)]))
            return ops

        for p in range(npair):
            for hh in range(2):
                h = 2 * p + hh
                ops = operands(p, hh)
                for t, name_t in enumerate(tiles):
                    k_t, v_t, q_t, do_t = ops[name_t]
                    s_scr[h, t] = _dot_nt(k_t, q_t)
                    dp_scr[h, t] = _dot_nt(v_t, do_t)

        def rows(L_r, do_r, oa_r, p):
            lt = jnp.transpose(L_r[0, :, psl(p)])
            dt = jnp.transpose(do_r[0, :, psl(p)].astype(F32) * oa_r[0, :, psl(p)])
            return ([lt[0:1, :], lt[HEAD_DIM:HEAD_DIM + 1, :]],
                    [jnp.sum(dt[:HEAD_DIM], axis=0, keepdims=True), jnp.sum(dt[HEAD_DIM:], axis=0, keepdims=True)])

        for p in range(npair):
            lse_c, delta_c = rows(L_ref, do_ref, oa_ref, p)
            if multi:
                lse_n, delta_n = rows(Ln_ref, don_ref, oan_ref, p)
            for hh in range(2):
                h = 2 * p + hh
                for t, name_t in enumerate(tiles):
                    lse, delta = (lse_n[hh], delta_n[hh]) if name_t == "N" else (lse_c[hh], delta_c[hh])
                    s = s_scr[h, t] * DIL_SCALE + bias_ref[h, band[name_t], :]
                    pr = jnp.where(valid[name_t], jnp.exp(s - lse), 0.0)
                    ds = pr * (dp_scr[h, t] - delta)
                    p_scr[h, t] = pr.astype(BF16)
                    ds_scr[h, t] = ds.astype(BF16)
                    if name_t != "N":
                        dbias_ref[h, band[name_t], :] += ds

        for p in range(npair):
            dqt = jnp.zeros((LANES, BLK), F32)
            dk = jnp.zeros((BLK, LANES), F32)
            dv = jnp.zeros((BLK, LANES), F32)
            kct = jnp.transpose(cur_ref[0, :, ksl(p)].astype(F32)).astype(BF16)
            if multi:
                kpt = jnp.transpose(prev_ref[0, :, ksl(p)].astype(F32)).astype(BF16)
            for hh in range(2):
                h = 2 * p + hh
                ops = operands(p, hh)
                mine = _row_mask((LANES, BLK), hh)
                for t, name_t in enumerate(tiles):
                    _, _, q_t, do_t = ops[name_t]
                    if name_t != "P":
                        dv = dv + _dot_nn(p_scr[h, t], do_t)
                        dk = dk + _dot_nn(ds_scr[h, t], q_t)
                    if name_t != "N":
                        kt = kpt if name_t == "P" else kct
                        dqt = dqt + _dot_nn(jnp.where(mine, kt, jnp.zeros_like(kt)), ds_scr[h, t])
            dqkv_ref[0, :, psl(p)] = jnp.transpose(dqt) * DIL_SCALE
            dqkv_ref[0, :, ksl(p)] = dk * DIL_SCALE
            dqkv_ref[0, :, vsl(p)] = dv

    def at(off):
        return lambda b, r, i: (b, jnp.clip(i + off, 0, nb - 1), r)

    qkv_spec = lambda off: pl.BlockSpec((1, BLK, P_QKV), at(off))
    da_spec = lambda off: pl.BlockSpec((1, BLK, D_A), at(off))
    bias_spec = pl.BlockSpec((None, N_HEADS, 2 * BLK, BLK), lambda b, r, i: (branch, 0, 0, 0))
    dbias_spec = pl.BlockSpec((N_HEADS, 2 * BLK, BLK), lambda b, r, i: (0, 0, 0))
    if multi:
        in_specs = [qkv_spec(0), qkv_spec(-1), qkv_spec(1), da_spec(0), da_spec(1), da_spec(0), da_spec(1),
                    da_spec(0), da_spec(1), bias_spec]
        args = [qkv_v, qkv_v, qkv_v, do_v, do_v, oa_v, oa_v, L_v, L_v, bias]
    else:
        in_specs = [qkv_spec(0), da_spec(0), da_spec(0), da_spec(0), bias_spec]
        args = [qkv_v, do_v, oa_v, L_v, bias]
    dqkv, dbias = pl.pallas_call(
        body, name=name,
        out_shape=[jax.ShapeDtypeStruct((B, n, d * P_QKV), F32),
                   jax.ShapeDtypeStruct((N_HEADS, 2 * BLK, BLK), F32)],
        grid=(B, d, nb),
        in_specs=in_specs,
        out_specs=[qkv_spec(0), dbias_spec],
        scratch_shapes=[pltpu.VMEM((N_HEADS, n_t, BLK, BLK), F32), pltpu.VMEM((N_HEADS, n_t, BLK, BLK), F32),
                        pltpu.VMEM((N_HEADS, n_t, BLK, BLK), BF16), pltpu.VMEM((N_HEADS, n_t, BLK, BLK), BF16)],
        compiler_params=_cparams(("arbitrary", "arbitrary", "arbitrary")),
    )(*args)
    return dqkv, dbias


def _sum_views_bf16(parts, name):
    B, S, W = parts[0].shape

    def body(a_ref, b_ref, c_ref, o_ref, sb, sc):
        _view_to_tile(b_ref, sb, DILATIONS[1], W)
        _view_to_tile(c_ref, sc, DILATIONS[2], W)
        o_ref[0] = (a_ref[0] + _get_tile(sb) + _get_tile(sc)).astype(o_ref.dtype)

    return pl.pallas_call(
        body, name=name, out_shape=jax.ShapeDtypeStruct((B, S, W), BF16), grid=(B, S // VIEW_TILE),
        in_specs=[_view_spec(d, W) for d in DILATIONS], out_specs=_view_spec(1, W),
        scratch_shapes=[_tile_scratch(W)] * 2,
        compiler_params=_cparams(("parallel", "parallel")),
    )(*parts)


def _bias_tables(rel_bias, buckets, name):
    nbr = buckets.shape[0]

    def body(rb_ref, bk_ref, o_ref):
        h = pl.program_id(1)
        tab = bk_ref[0]

        def step(bkt, acc):
            return jnp.where(tab == bkt, rb_ref[bkt, h], acc)

        o_ref[0, 0] = lax.fori_loop(0, N_BUCKETS, step, jnp.zeros((2 * BLK, BLK), F32))

    return pl.pallas_call(
        body, name=name, out_shape=jax.ShapeDtypeStruct((nbr, N_HEADS, 2 * BLK, BLK), F32),
        grid=(nbr, N_HEADS),
        in_specs=[pl.BlockSpec(memory_space=pltpu.SMEM),
                  pl.BlockSpec((1, 2 * BLK, BLK), lambda i, h: (i, 0, 0))],
        out_specs=pl.BlockSpec((1, 1, 2 * BLK, BLK), lambda i, h: (i, h, 0, 0)),
        compiler_params=_cparams(("parallel", "arbitrary")),
    )(rel_bias, buckets)


def _bias_grad(dbias_list, buckets, name):
    nbr = len(dbias_list)

    def body(*refs):
        d_refs, bk_ref, o_ref = refs[:nbr], refs[nbr], refs[nbr + 1]
        lane = lax.broadcasted_iota(jnp.int32, (1, LANES), 1)
        for h in range(N_HEADS):
            def step(bkt, acc):
                tot = jnp.zeros((1, 1), F32)
                for bi in range(nbr):
                    sel = jnp.where(bk_ref[bi] == bkt, d_refs[bi][h], 0.0)
                    tot = tot + jnp.sum(jnp.sum(sel, axis=1, keepdims=True), axis=0, keepdims=True)
                return acc + jnp.where(lane == bkt, tot, 0.0)

            o_ref[h:h + 1, :] = lax.fori_loop(0, N_BUCKETS, step, jnp.zeros((1, LANES), F32))

    band = pl.BlockSpec((N_HEADS, 2 * BLK, BLK), lambda i: (0, 0, 0))
    return pl.pallas_call(
        body, name=name, out_shape=jax.ShapeDtypeStruct((N_HEADS, LANES), F32), grid=(1,),
        in_specs=[band] * nbr + [pl.BlockSpec((nbr, 2 * BLK, BLK), lambda i: (0, 0, 0))],
        out_specs=pl.BlockSpec((N_HEADS, LANES), lambda i: (0, 0)),
        compiler_params=_cparams(("arbitrary",)),
    )(*dbias_list, buckets)


MLA_TQ = 256
MLA_TK = 256


LOG2E = math.log2(math.e)
MLA_C = MLA_SCALE * LOG2E


def _key_le_query(tk, tq):
    return lax.broadcasted_iota(jnp.int32, (tk, tq), 0) <= lax.broadcasted_iota(jnp.int32, (tk, tq), 1)


def _row_mask(shape, hh):
    row = lax.broadcasted_iota(jnp.int32, shape, 0)
    return (row >= hh * HEAD_DIM) & (row < (hh + 1) * HEAD_DIM)


def _host_call(body, comm, *, name, grid, in_specs, out_specs, out_shape, scratch_shapes, args):
    sem = ("arbitrary",) * len(grid)
    if comm is None:
        res = pl.pallas_call(body, name=name, grid=grid, in_specs=in_specs, out_specs=out_specs,
                             out_shape=out_shape, scratch_shapes=scratch_shapes,
                             compiler_params=_cparams(sem))(*args)
        return res, []
    n_in, n_out, n_s, cn = len(in_specs), len(out_specs), len(scratch_shapes), comm.n

    def hosted(*refs):
        ins, refs = refs[:n_in], refs[n_in:]
        c_ins, refs = refs[:cn], refs[cn:]
        outs, refs = refs[:n_out], refs[n_out:]
        c_outs, refs = refs[:cn], refs[cn:]
        scr, c_sems = refs[:n_s], refs[n_s:]
        ids = [pl.program_id(a) for a in range(len(grid))]
        first = functools.reduce(jnp.logical_and, [i == 0 for i in ids])
        last = functools.reduce(jnp.logical_and, [i == g - 1 for i, g in zip(ids, grid)])

        @pl.when(first)
        def _():
            comm.start(c_ins, c_outs, c_sems)

        body(*ins, *outs, *scr)

        @pl.when(last)
        def _():
            comm.finish(c_ins, c_outs, c_sems)

    res = pl.pallas_call(
        hosted, name=name, grid=grid, in_specs=list(in_specs) + _hbm_specs(cn),
        out_specs=list(out_specs) + _hbm_specs(cn), out_shape=list(out_shape) + list(comm.out_shape),
        scratch_shapes=list(scratch_shapes) + list(comm.scratch), compiler_params=_cparams(sem),
    )(*args, *comm.inputs)
    return res[:n_out], res[n_out:]


def _mla_fwd_t(q, k, vt, name, comm=None):
    B, S, _ = q.shape
    tq, tk = MLA_TQ, MLA_TK
    assert tq == tk
    npair = N_HEADS // 2
    nq = S // tq

    def body(q_ref, k_ref, vt_ref, o_ref, lse_ref, s_scr, e_scr, acc_scr, m_scr, a_scr):
        i = pl.program_id(1)
        diag = _key_le_query(tk, tq)
        m_scr[...] = jnp.full_like(m_scr, NEG)
        acc_scr[...] = jnp.zeros_like(acc_scr)

        def step(j, masked):
            rows = pl.ds(pl.multiple_of(j * tk, tk), tk)
            for h in range(N_HEADS):
                hsl = slice(h * LANES, (h + 1) * LANES)
                s_scr[h] = _dot_nt(k_ref[0, rows, hsl], q_ref[0, :, hsl])
            for h in range(N_HEADS):
                s = s_scr[h]
                if masked:
                    s = jnp.where(diag, s, NEG)
                m_old = m_scr[h:h + 1, :]
                m_new = jnp.maximum(m_old, jnp.max(s, axis=0, keepdims=True))
                a_scr[h:h + 1, :] = jnp.exp2((m_old - m_new) * MLA_C)
                e_scr[h] = jnp.exp2((s - m_new) * MLA_C).astype(BF16)
                m_scr[h:h + 1, :] = m_new
            for h in range(N_HEADS):
                vj = vt_ref[0, h // 2, j]
                vh = jnp.where(_row_mask(vj.shape, h % 2), vj, jnp.ones_like(vj))
                acc_scr[h] = acc_scr[h] * a_scr[h:h + 1, :] + _dot_nn(vh, e_scr[h])

        def loop_body(j, carry):
            step(j, False)
            return carry

        lax.fori_loop(0, i, loop_body, 0)
        step(i, True)
        rows0 = _row_mask((LANES, tq), 0)
        for p in range(npair):
            l0 = acc_scr[2 * p, HEAD_DIM:HEAD_DIM + 1, :]
            l1 = acc_scr[2 * p + 1, 0:1, :]
            o_ref[0, p * LANES:(p + 1) * LANES, :] = jnp.where(rows0, acc_scr[2 * p] / l0, acc_scr[2 * p + 1] / l1)
            lse_ref[0, p, 0] = jnp.zeros((8, tq), F32)
            lse_ref[0, p, 0, 0:1, :] = m_scr[2 * p:2 * p + 1, :] * MLA_C + jnp.log(l0) * LOG2E
            lse_ref[0, p, 0, 1:2, :] = m_scr[2 * p + 1:2 * p + 2, :] * MLA_C + jnp.log(l1) * LOG2E

    return _host_call(
        body, comm, name=name,
        out_shape=[jax.ShapeDtypeStruct((B, D_B, S), F32), jax.ShapeDtypeStruct((B, npair, nq, 8, tq), F32)],
        grid=(B, nq),
        in_specs=[pl.BlockSpec((1, tq, N_HEADS * LANES), lambda b, i: (b, i, 0)),
                  pl.BlockSpec((1, S, N_HEADS * LANES), lambda b, i: (b, 0, 0)),
                  pl.BlockSpec((1, npair, S // tk, LANES, tk), lambda b, i: (b, 0, 0, 0, 0))],
        out_specs=[pl.BlockSpec((1, D_B, tq), lambda b, i: (b, 0, i)),
                   pl.BlockSpec((1, npair, 1, 8, tq), lambda b, i: (b, 0, i, 0, 0))],
        scratch_shapes=[pltpu.VMEM((N_HEADS, tk, tq), F32), pltpu.VMEM((N_HEADS, tk, tq), BF16),
                        pltpu.VMEM((N_HEADS, LANES, tq), F32), pltpu.VMEM((N_HEADS, tq), F32),
                        pltpu.VMEM((N_HEADS, tq), F32)],
        args=(q, k, vt))


def _mla_delta(do, o, name):
    B, S, _ = o.shape
    tq = MLA_TQ
    npair = N_HEADS // 2

    def body(do_ref, o_ref, d_ref):
        d_ref[...] = jnp.zeros_like(d_ref)
        for p in range(npair):
            sl = slice(p * LANES, (p + 1) * LANES)
            prod_t = jnp.transpose(do_ref[0, :, sl].astype(F32) * o_ref[0, :, sl])
            d_ref[0, p, 0, 0:1, :] = jnp.sum(prod_t[:HEAD_DIM], axis=0, keepdims=True)
            d_ref[0, p, 0, 1:2, :] = jnp.sum(prod_t[HEAD_DIM:], axis=0, keepdims=True)

    tok = pl.BlockSpec((1, tq, D_B), lambda b, i: (b, i, 0))
    return pl.pallas_call(
        body, name=name, out_shape=jax.ShapeDtypeStruct((B, npair, S // tq, 8, tq), F32),
        grid=(B, S // tq), in_specs=[tok, tok],
        out_specs=pl.BlockSpec((1, npair, 1, 8, tq), lambda b, i: (b, 0, i, 0, 0)),
        compiler_params=_cparams(("parallel", "parallel")),
    )(do, o)


def _mla_bwd_t(q, k, v, do, lse, delta, name, comm=None):
    B, S, _ = q.shape
    tq, tk = MLA_TQ, MLA_TK
    assert tq == tk
    npair = N_HEADS // 2
    nq = S // tq

    hg = 4
    pg = hg // 2
    ngroup = N_HEADS // hg

    def body(q_ref, do_ref, lse_ref, dl_ref, k_ref, v_ref, dk_ref, dv_ref, dq_ref,
             s_scr, dp_scr, p_scr, ds_scr, dk_s, dv_s, kt_s):
        j = pl.program_id(2)

        @pl.when(j == 0)
        def _():
            dq_ref[...] = jnp.zeros_like(dq_ref)

        dk_s[...] = jnp.zeros_like(dk_s)
        dv_s[...] = jnp.zeros_like(dv_s)
        diag = _key_le_query(tk, tq)
        hsl = lambda h: slice(h * LANES, (h + 1) * LANES)
        for h in range(hg):
            kt_s[h] = jnp.transpose(k_ref[0, :, hsl(h)].astype(F32)).astype(BF16)

        def step(i, masked):
            rows = pl.ds(pl.multiple_of(i * tq, tq), tq)

            def dom(h):
                dov = do_ref[0, rows, hsl(h // 2)]
                return jnp.where(_head_mask((tq, LANES), h % 2), dov, jnp.zeros_like(dov))

            for h in range(hg):
                s_scr[h] = _dot_nt(k_ref[0, :, hsl(h)], q_ref[0, rows, hsl(h)])
                dp_scr[h] = _dot_nt(v_ref[0, :, hsl(h // 2)], dom(h))
            for h in range(hg):
                pr = jnp.exp2(s_scr[h] * MLA_C - lse_ref[0, h // 2, i, h % 2:h % 2 + 1, :])
                if masked:
                    pr = jnp.where(diag, pr, 0.0)
                p_scr[h] = pr.astype(BF16)
                ds_scr[h] = (pr * (dp_scr[h] - dl_ref[0, h // 2, i, h % 2:h % 2 + 1, :])).astype(BF16)
            for h in range(hg):
                dv_s[h // 2] += _dot_nn(p_scr[h], dom(h))
                dk_s[h] += _dot_nn(ds_scr[h], q_ref[0, rows, hsl(h)])
                dq_ref[0, h // 2, i, hsl(h % 2), :] += _dot_nn(kt_s[h], ds_scr[h]) * MLA_SCALE

        step(j, True)

        def loop_body(i, carry):
            step(i, False)
            return carry

        lax.fori_loop(j + 1, nq, loop_body, 0)
        for h in range(hg):
            dk_ref[0, :, hsl(h)] = dk_s[h] * MLA_SCALE
        for p in range(pg):
            dv_ref[0, :, hsl(p)] = dv_s[p]

    stat = pl.BlockSpec((1, pg, nq, 8, tq), lambda b, g, j: (b, g, 0, 0, 0))
    return _host_call(
        body, comm, name=name,
        out_shape=[jax.ShapeDtypeStruct((B, S, N_HEADS * LANES), F32), jax.ShapeDtypeStruct((B, S, D_B), F32),
                   jax.ShapeDtypeStruct((B, npair, nq, 2 * LANES, tq), F32)],
        grid=(B, ngroup, S // tk),
        in_specs=[pl.BlockSpec((1, S, hg * LANES), lambda b, g, j: (b, 0, g)),
                  pl.BlockSpec((1, S, pg * LANES), lambda b, g, j: (b, 0, g)),
                  stat, stat,
                  pl.BlockSpec((1, tk, hg * LANES), lambda b, g, j: (b, j, g)),
                  pl.BlockSpec((1, tk, pg * LANES), lambda b, g, j: (b, j, g))],
        out_specs=[pl.BlockSpec((1, tk, hg * LANES), lambda b, g, j: (b, j, g)),
                   pl.BlockSpec((1, tk, pg * LANES), lambda b, g, j: (b, j, g)),
                   pl.BlockSpec((1, pg, nq, 2 * LANES, tq), lambda b, g, j: (b, g, 0, 0, 0))],
        scratch_shapes=[pltpu.VMEM((hg, tk, tq), F32), pltpu.VMEM((hg, tk, tq), F32),
                        pltpu.VMEM((hg, tk, tq), BF16), pltpu.VMEM((hg, tk, tq), BF16),
                        pltpu.VMEM((hg, tk, LANES), F32), pltpu.VMEM((pg, tk, LANES), F32),
                        pltpu.VMEM((hg, LANES, tk), BF16)],
        args=(q, do, lse, delta, k, v))


def _local_step(x, target, mod, wts, gains, rel_bias, ffn_shards=None):
    B, S, D = x.shape
    T = B * S
    sh1, sc1, g1, sh2, sc2, g2 = [mod[:, i * D:(i + 1) * D].reshape(B, 1, D) for i in range(N_MOD)]
    cs, sn = _rope_tables()
    buckets = np.stack([_band_buckets(d) for d in DILATIONS])
    buckets_dev = jnp.asarray(buckets)
    bias = _bias_tables(rel_bias, buckets_dev, "rel_bias_tables")
    w_in = wts["w_in"]

    h1 = _adaln_fwd(x, gains["g_norm1"], sc1, sh1, "adaln1_fwd")
    h1f = h1.reshape(T, D)
    qkv_v = _mm_qkv_views(h1, w_in[:, :P_QKV], "mm_qkv")
    rest = _mm(h1f, w_in[:, P_QKV:], "nn", F32, "mm_rest")
    o_d, lse_d = [], []
    for i, d in enumerate(DILATIONS):
        o_i, lse_i = _dil_fwd(qkv_v[i], bias, i, d, f"dil_fwd_{d}")
        o_d.append(o_i)
        lse_d.append(lse_i)
    out_a_v, lse_a_v = _dil_merge(o_d, lse_d, "dil_merge")
    out_a = out_a_v[0]
    cqn = _rms_fwd(rest, 1, Q_LORA, gains["g_cq"], "rms_cq_fwd")
    ckvn = _rms_fwd(rest, 0, KV_LORA, gains["g_ckv"], "rms_ckv_fwd")
    rest3 = rest.reshape(B, S, P_REST)
    q_raw = _mm(cqn, wts["w_uq"], "nn", F32, "mm_uq").reshape(B, S, N_HEADS * LANES)
    qc = _rope_apply(q_raw, cs, sn, BF16, "rope_q")
    kn_raw = _mm(ckvn, wts["w_kv"][:, :N_HEADS * LANES], "nn", F32, "mm_uk").reshape(B, S, N_HEADS * LANES)
    kc = _rope_apply(kn_raw, cs, sn, BF16, "rope_k", add=rest3, add_blk=KV_LORA // LANES)
    v = _mm(ckvn, wts["w_kv"][:, N_HEADS * LANES:], "nn", BF16, "mm_uv").reshape(B, S, D_B)
    vt = jnp.transpose(v.reshape(B, S // MLA_TK, MLA_TK, N_HEADS // 2, LANES), (0, 3, 1, 4, 2))
    (o_t, lse_b), got = _mla_fwd_t(qc, kc, vt, "mla_fwd", _GatherComm(ffn_shards) if ffn_shards else None)
    if ffn_shards:
        wts = dict(wts, w_ffn_in=got[0].reshape(N_CHIP, D, -1), w_ffn_out=got[1].reshape(D_FF, D))
    out_b = jnp.transpose(o_t, (0, 2, 1))
    out_af, out_bf = out_a.reshape(T, D_A), out_b.reshape(T, D_B)
    ya = _rms_fwd(out_af, 0, D_A, gains["g_out_a"], "rms_outa_fwd")
    yb = _rms_fwd(out_bf, 0, D_B, gains["g_out_b"], "rms_outb_fwd")
    y = jnp.concatenate([ya, yb], axis=1)
    mix = _mm(y, wts["w_out"], "nn", F32, "mm_out").reshape(B, S, D)
    h2, x1 = _adaln_fwd(x, gains["g_norm2"], sc2, sh2, "adaln2_fwd", mix=mix, gate=g1)
    h2f = h2.reshape(T, D)
    gu = _mm(h2f, wts["w_ffn_in"], "nn", F32, "mm_ffn_in", col_blocks=N_CHIP)
    act = _swiglu_fwd(gu, "swiglu_fwd")
    f = _mm(act, wts["w_ffn_out"], "nn", F32, "mm_ffn_out").reshape(B, S, D)
    dx2, df, dg2, dg_final, loss = _final_loss(x1, f, g2, gains["g_final"], target, "final_loss")

    dff = df.reshape(T, D)
    da = _mm(dff, wts["w_ffn_out"], "nt", F32, "mm_ffn_out_dx")
    gw_ffn_out = _mm(act, dff, "tn", F32, "mm_ffn_out_dw")
    dgu = _swiglu_bwd(da, gu, "swiglu_bwd")
    dh2 = _mm(dgu, wts["w_ffn_in"], "nt", F32, "mm_ffn_in_dx", col_blocks=N_CHIP).reshape(B, S, D)
    gw_ffn_in = _mm(h2f, dgu, "tn", F32, "mm_ffn_in_dw", col_blocks=N_CHIP)
    dx1, dsh2, dsc2, dg_norm2, dg1, dmix = _adaln_bwd(dh2, x1, gains["g_norm2"], sc2, dx2, "adaln2_bwd",
                                                      mix=mix, gate=g1)
    dmixf = dmix.reshape(T, D)
    dy = _mm(dmixf, wts["w_out"], "nt", F32, "mm_out_dx")
    gw_out = _mm(y, dmixf, "tn", F32, "mm_out_dw")
    do_a_v, dg_out_a = _rms_bwd_views(dy, 0, out_a, gains["g_out_a"], "rms_outa_bwd")
    do_b, dg_out_b = _rms_bwd(dy, 1, out_bf, 0, D_B, gains["g_out_b"], "rms_outb_bwd")
    do_b3 = do_b.reshape(B, S, D_B)
    delta_b = _mla_delta(do_b3, out_b, "mla_delta")
    ffn_a4 = None
    if ffn_shards:
        ffn_a4 = _rs_first([gw_ffn_in.reshape(N_DEV, -1, gw_ffn_in.shape[-1]), gw_ffn_out.reshape(N_DEV, -1, D)],
                           "ffn")
    (dkc, dv, dq_t), ffn_r2 = _mla_bwd_t(qc, kc, v, do_b3, lse_b, delta_b, "mla_bwd",
                                         _ToChipsComm(ffn_a4) if ffn_shards else None)
    dqc = jnp.transpose(dq_t, (0, 2, 4, 1, 3)).reshape(B, S, N_HEADS * LANES)
    dq_raw = _rope_apply(dqc, cs, -sn, BF16, "rope_q_bwd").reshape(T, N_HEADS * LANES)
    dkrw = _krope_bwd(dkc, cs, -sn, "rope_k_bwd").reshape(T, LANES)
    dcqn = _mm(dq_raw, wts["w_uq"], "nt", F32, "mm_uq_dx")
    gw_uq = _mm(cqn, dq_raw, "tn", F32, "mm_uq_dw")
    dkv = jnp.concatenate([dkc.reshape(T, -1), dv.reshape(T, -1)], axis=1).astype(BF16)
    dckvn = _mm(dkv, wts["w_kv"], "nt", F32, "mm_ukv_dx")
    gw_kv = _mm(ckvn, dkv, "tn", F32, "mm_ukv_dw")
    dcq, dg_cq = _rms_bwd(dcqn, 0, rest, 1, Q_LORA, gains["g_cq"], "rms_cq_bwd")
    dckv, dg_ckv = _rms_bwd(dckvn, 0, rest, 0, KV_LORA, gains["g_ckv"], "rms_ckv_bwd")
    dqkv_d, dbias_d = [], []
    for i, d in enumerate(DILATIONS):
        dqkv_i, dbias_i = _dil_bwd(qkv_v[i], do_a_v[i], out_a_v[i], lse_a_v[i], bias, i, d, f"dil_bwd_{d}")
        dqkv_d.append(dqkv_i)
        dbias_d.append(dbias_i)
    dqkv = _sum_views_bf16(dqkv_d, "dil_bwd_sum").reshape(T, P_QKV)
    g_rel_bias = _bias_grad(dbias_d, buckets_dev, "rel_bias_grad")[:, :N_BUCKETS].T
    dproj = jnp.concatenate([dqkv, dckv, dkrw, dcq], axis=1)
    dh1 = _mm(dproj, w_in, "nt", F32, "mm_in_dx").reshape(B, S, D)
    gw_in = _mm(h1f, dproj, "tn", F32, "mm_in_dw")
    grad_x, dsh1, dsc1, dg_norm1 = _adaln_bwd(dh1, x, gains["g_norm1"], sc1, dx1, "adaln1_bwd")
    gmod = jnp.concatenate([dsh1, dsc1, dg1, dsh2, dsc2, dg2], axis=-1).reshape(B, N_MOD * D)
    grads = dict(w_in=gw_in, w_uq=gw_uq, w_kv=gw_kv, w_out=gw_out, w_ffn_in=gw_ffn_in, w_ffn_out=gw_ffn_out,
                 g_norm1=dg_norm1, g_cq=dg_cq, g_ckv=dg_ckv, rel_bias=g_rel_bias, g_out_a=dg_out_a,
                 g_out_b=dg_out_b, g_norm2=dg_norm2, g_final=dg_final, ffn_pending=(ffn_a4, ffn_r2))
    return loss, grad_x, gmod, grads


def _w_in_to_kernel(w):
    z = lambda n: jnp.zeros((w.shape[0], n), w.dtype)
    i3, i4, i5 = 3 * D_A, 3 * D_A + Q_LORA, 3 * D_A + Q_LORA + KV_LORA
    return jnp.concatenate([w[:, :i3], w[:, i4:i5], z(NOPE_DIM), w[:, i5:], z(LANES - NOPE_DIM - ROPE_DIM),
                            w[:, i3:i4]], axis=1)


def _w_in_from_kernel(g):
    o = P_QKV + KV_LORA
    return jnp.concatenate([g[:, :P_QKV], g[:, o + LANES:], g[:, P_QKV:o],
                            g[:, o + NOPE_DIM:o + NOPE_DIM + ROPE_DIM]], axis=1)


def _w_uq_to_kernel(w):
    w3 = w.reshape(Q_LORA, N_HEADS, NOPE_DIM + ROPE_DIM)
    return jnp.pad(w3, ((0, 0), (0, 0), (0, LANES - NOPE_DIM - ROPE_DIM))).reshape(Q_LORA, N_HEADS * LANES)


def _w_uq_from_kernel(g):
    return g.reshape(Q_LORA, N_HEADS, LANES)[:, :, :NOPE_DIM + ROPE_DIM].reshape(Q_LORA, -1)


def _w_ukv_to_kernel(w):
    w3 = w.reshape(KV_LORA, N_HEADS, 2 * HEAD_DIM)
    wk = jnp.pad(w3[:, :, :NOPE_DIM], ((0, 0), (0, 0), (0, LANES - NOPE_DIM))).reshape(KV_LORA, N_HEADS * LANES)
    wv = w3[:, :, NOPE_DIM:].reshape(KV_LORA, D_B)
    return jnp.concatenate([wk, wv], axis=1)


def _w_ukv_from_kernel(g):
    gk = g[:, :N_HEADS * LANES].reshape(KV_LORA, N_HEADS, LANES)[:, :, :NOPE_DIM]
    gv = g[:, N_HEADS * LANES:].reshape(KV_LORA, N_HEADS, HEAD_DIM)
    return jnp.concatenate([gk, gv], axis=2).reshape(KV_LORA, -1)


MESH = pl.DeviceIdType.MESH


def _my_place():
    return lax.axis_index("x"), lax.axis_index("y"), lax.axis_index("c")


def _other_chips(x, y):
    return [(1 - x, y), (x, 1 - y), (1 - x, 1 - y)]


def _allgather8(x_shard, name, in_hbm):
    m_per, n = x_shard.shape
    space = pl.ANY if in_hbm else pltpu.VMEM

    def body(x_ref, out_ref, send_sems, recv_sems, local_sem):
        x, y, c = _my_place()
        me, sibling = (x, y, c), (x, y, 1 - c)
        chips = _other_chips(x, y)

        def rows(px, py, pc):
            return out_ref.at[pl.ds((4 * px + 2 * py + pc) * m_per, m_per), :]

        def copy(k, block, to, src=None):
            return pltpu.make_async_remote_copy(
                src_ref=rows(*block) if src is None else src, dst_ref=rows(*block),
                send_sem=send_sems.at[k], recv_sem=recv_sems.at[k], device_id=to, device_id_type=MESH)

        mine = pltpu.make_async_copy(x_ref, rows(*me), local_sem)
        mine.start()
        first = [copy(0, me, sibling, src=x_ref)]
        first += [copy(1 + j, me, (*chip, c), src=x_ref) for j, chip in enumerate(chips)]
        for cp in first:
            cp.start()
        passed = [copy(4 + j, (*chip, c), sibling) for j, chip in enumerate(chips)]
        for j, chip in enumerate(chips):
            copy(1 + j, (*chip, c), me).wait_recv()
            passed[j].start()
        copy(0, sibling, me).wait_recv()
        for j, chip in enumerate(chips):
            copy(4 + j, (*chip, 1 - c), me).wait_recv()
        for cp in first + passed:
            cp.wait_send()
        mine.wait()

    return pl.pallas_call(
        body, name=name,
        out_shape=jax.ShapeDtypeStruct((N_DEV * m_per, n), x_shard.dtype),
        in_specs=[pl.BlockSpec(memory_space=space)],
        out_specs=pl.BlockSpec(memory_space=space),
        scratch_shapes=[pltpu.SemaphoreType.DMA((7,)), pltpu.SemaphoreType.DMA((7,)), pltpu.SemaphoreType.DMA],
        compiler_params=pltpu.CompilerParams(vmem_limit_bytes=VMEM_LIMIT),
    )(x_shard)


def _hbm_specs(n):
    return [pl.BlockSpec(memory_space=pl.ANY)] * n


class _GatherComm:
    def __init__(self, shards):
        self.n = n = len(shards)
        self.inputs = [s.reshape(2, s.shape[0] // 2, s.shape[1]) for s in shards]
        self.out_shape = [jax.ShapeDtypeStruct((N_DEV,) + s.shape[1:], s.dtype) for s in self.inputs]
        self.scratch = [pltpu.SemaphoreType.DMA((7 * n,)), pltpu.SemaphoreType.DMA((7 * n,))]

    def _parts(self, xs, outs, sems):
        send_sems, recv_sems = sems
        x, y, c = _my_place()

        def blk(k, px, py, pc):
            return outs[k].at[4 * px + 2 * py + pc]

        def copy(k, kind, block, to, own=False):
            return pltpu.make_async_remote_copy(
                src_ref=xs[k].at[c] if own else blk(k, *block), dst_ref=blk(k, *block),
                send_sem=send_sems.at[7 * k + kind], recv_sem=recv_sems.at[7 * k + kind],
                device_id=to, device_id_type=MESH)

        def whole(k):
            return pltpu.make_async_remote_copy(
                src_ref=xs[k], dst_ref=outs[k].at[pl.ds(4 * x + 2 * y, 2)],
                send_sem=send_sems.at[7 * k], recv_sem=recv_sems.at[7 * k],
                device_id=(x, y, 1 - c), device_id_type=MESH)

        me, sibling = (x, y, c), (x, y, 1 - c)
        chips = _other_chips(x, y)
        first = []
        for k in range(self.n):
            first.append(whole(k))
            first += [copy(k, 1 + j, me, (*chip, c), own=True) for j, chip in enumerate(chips)]
        return copy, whole, me, sibling, chips, c, first

    def start(self, xs, outs, sems):
        for cp in self._parts(xs, outs, sems)[-1]:
            cp.start()

    def finish(self, xs, outs, sems):
        copy, whole, me, sibling, chips, c, first = self._parts(xs, outs, sems)
        passed = []
        for j, chip in enumerate(chips):
            for k in range(self.n):
                copy(k, 1 + j, (*chip, c), me).wait_recv()
                fwd = copy(k, 4 + j, (*chip, c), sibling)
                fwd.start()
                passed.append(fwd)
        for k in range(self.n):
            whole(k).wait_recv()
        for j, chip in enumerate(chips):
            for k in range(self.n):
                copy(k, 4 + j, (*chip, 1 - c), me).wait_recv()
        for cp in first + passed:
            cp.wait_send()


class _ToChipsComm:
    def __init__(self, a4s):
        self.inputs = list(a4s)
        self.n = n = len(a4s)
        nc = N_CHIP - 1
        self.out_shape = [jax.ShapeDtypeStruct((nc,) + a.shape[1:], a.dtype) for a in a4s]
        self.scratch = [pltpu.SemaphoreType.DMA((nc * n,)), pltpu.SemaphoreType.DMA((nc * n,))]

    def _copies(self, as_, rs, sems):
        send_sems, recv_sems = sems
        x, y, c = _my_place()
        nc = N_CHIP - 1
        return [pltpu.make_async_remote_copy(
            src_ref=as_[k].at[2 * cx + cy], dst_ref=rs[k].at[j], send_sem=send_sems.at[nc * k + j],
            recv_sem=recv_sems.at[nc * k + j], device_id=(cx, cy, c), device_id_type=MESH)
            for k in range(self.n) for j, (cx, cy) in enumerate(_other_chips(x, y))]

    def start(self, as_, rs, sems):
        for cp in self._copies(as_, rs, sems):
            cp.start()

    def finish(self, as_, rs, sems):
        for cp in self._copies(as_, rs, sems):
            cp.wait()


def _run_comm(comm, name):
    n = comm.n

    def body(*refs):
        ins, outs, sems = refs[:n], refs[n:2 * n], refs[2 * n:]
        comm.start(ins, outs, sems)
        comm.finish(ins, outs, sems)

    return pl.pallas_call(
        body, name=name, out_shape=comm.out_shape, in_specs=_hbm_specs(n), out_specs=_hbm_specs(n),
        scratch_shapes=comm.scratch,
    )(*comm.inputs)


def _gather_weights(shards, name):
    return _run_comm(_GatherComm(shards), name)


def _rs_to_sibling(g8s, name):
    n = len(g8s)

    def body(*refs):
        gs, rs = refs[:n], refs[n:2 * n]
        send_sems, recv_sems = refs[2 * n:]
        x, y, c = _my_place()
        copies = [pltpu.make_async_remote_copy(
            src_ref=gs[k].at[2 * s + 1 - c], dst_ref=rs[k].at[s], send_sem=send_sems.at[N_CHIP * k + s],
            recv_sem=recv_sems.at[N_CHIP * k + s], device_id=(x, y, 1 - c), device_id_type=MESH)
            for k in range(n) for s in range(N_CHIP)]
        for cp in copies:
            cp.start()
        for cp in copies:
            cp.wait()

    return pl.pallas_call(
        body, name=name,
        out_shape=[jax.ShapeDtypeStruct((N_CHIP,) + g.shape[1:], g.dtype) for g in g8s],
        in_specs=_hbm_specs(n), out_specs=_hbm_specs(n),
        scratch_shapes=[pltpu.SemaphoreType.DMA((N_CHIP * n,)), pltpu.SemaphoreType.DMA((N_CHIP * n,))],
    )(*g8s)


def _rs_to_chips(a4s, name):
    return _run_comm(_ToChipsComm(a4s), name)


def _swap_halves(hs, name):
    n = len(hs)

    def body(*refs):
        o_refs = refs[n:2 * n]
        send_sems, recv_sems = refs[2 * n:]
        x, y, c = _my_place()

        def remote(k, slot):
            return pltpu.make_async_remote_copy(
                src_ref=o_refs[k].at[slot], dst_ref=o_refs[k].at[slot], send_sem=send_sems.at[k],
                recv_sem=recv_sems.at[k], device_id=(x, y, 1 - c), device_id_type=MESH)

        sends = [remote(k, c) for k in range(n)]
        for cp in sends:
            cp.start()
        for k in range(n):
            remote(k, 1 - c).wait_recv()
        for cp in sends:
            cp.wait_send()

    return pl.pallas_call(
        body, name=name,
        out_shape=[jax.ShapeDtypeStruct(h.shape, h.dtype) for h in hs],
        in_specs=_hbm_specs(n), out_specs=_hbm_specs(n),
        input_output_aliases={k: k for k in range(n)},
        scratch_shapes=[pltpu.SemaphoreType.DMA((n,)), pltpu.SemaphoreType.DMA((n,))],
    )(*hs)


ADD_TILES = 4


def _add_blocks(a_list, a_idx_fn, others_list, ns, sel, name, out_blocks=None, out_idx_fn=None):
    out_blocks = out_blocks or ns
    out_idx_fn = out_idx_fn or (lambda s, sel_ref: s)
    n = len(a_list)
    n_o = len(others_list[0])
    per = 1 + n_o

    def body(sel_ref, *refs):
        for k in range(n):
            ins = refs[k * per:(k + 1) * per]
            o_ref = refs[n * per + k]
            acc = ins[0][0]
            for r in ins[1:]:
                acc = acc + r[0]
            o_ref[0] = acc

    in_specs, args, out_specs, out_shape = [], [], [], []
    for a, others in zip(a_list, others_list):
        _, R, N = a.shape
        tr = R // ADD_TILES
        assert tr % 8 == 0, a.shape
        in_specs.append(pl.BlockSpec((1, tr, N), lambda s, i, sel_ref: (a_idx_fn(s, sel_ref), i, 0)))
        args.append(a)
        for arr, fixed in others:
            if fixed is None:
                in_specs.append(pl.BlockSpec((1, tr, N), lambda s, i, sel_ref: (s, i, 0)))
            else:
                in_specs.append(pl.BlockSpec((1, tr, N), lambda s, i, sel_ref, fixed=fixed: (fixed, i, 0)))
            args.append(arr)
        out_specs.append(pl.BlockSpec((1, tr, N), lambda s, i, sel_ref: (out_idx_fn(s, sel_ref), i, 0)))
        out_shape.append(jax.ShapeDtypeStruct((out_blocks, R, N), a.dtype))
    grid_spec = pltpu.PrefetchScalarGridSpec(num_scalar_prefetch=1, grid=(ns, ADD_TILES), in_specs=in_specs,
                                             out_specs=out_specs)
    return pl.pallas_call(
        body, name=name, out_shape=out_shape, grid_spec=grid_spec,
        compiler_params=_cparams(("parallel", "parallel")),
    )(sel, *args)


def _rs_first(g8s, tag):
    c_sel = jnp.reshape(lax.axis_index("c"), (1,)).astype(jnp.int32)
    r1 = _rs_to_sibling(g8s, f"rs_to_sibling_{tag}")
    return _add_blocks(g8s, lambda s, sel: 2 * s + sel[0], [[(r, None)] for r in r1], N_CHIP, c_sel,
                       f"rs_add_sibling_{tag}")


def _rs_last(a4s, r2s, tag):
    sel = jnp.stack([2 * lax.axis_index("x") + lax.axis_index("y"), lax.axis_index("c")]).astype(jnp.int32)
    h = _add_blocks(a4s, lambda s, sel: sel[0], [[(r, 0), (r, 1), (r, 2)] for r in r2s], 1, sel,
                    f"rs_add_chips_{tag}", out_blocks=2, out_idx_fn=lambda s, sel: sel[1])
    full = _swap_halves(h, f"rs_swap_halves_{tag}")
    return [f.reshape(2 * f.shape[1], f.shape[2]) for f in full]


def _reduce_scatter(g8s, tag):
    a4 = _rs_first(g8s, tag)
    return _rs_last(a4, _rs_to_chips(a4, f"rs_to_chips_{tag}"), tag)


def _ada_fwd(c_all, w_ada, b_ada, name):
    nb, D = c_all.shape
    ncol = w_ada.shape[1]
    tc = 512

    def body(c_ref, w_ref, b_ref, o_ref):
        cv = c_ref[...]
        cond = (cv * jax.nn.sigmoid(cv)).astype(BF16)
        o_ref[...] = jnp.dot(cond, w_ref[...].astype(BF16), preferred_element_type=F32) + b_ref[...]

    return pl.pallas_call(
        body, name=name, out_shape=jax.ShapeDtypeStruct((nb, ncol), F32), grid=(ncol // tc,),
        in_specs=[pl.BlockSpec((nb, D), lambda j: (0, 0)), pl.BlockSpec((D, tc), lambda j: (0, j)),
                  pl.BlockSpec((1, tc), lambda j: (0, j))],
        out_specs=pl.BlockSpec((nb, tc), lambda j: (0, j)),
        compiler_params=_cparams(("parallel",)),
    )(c_all, w_ada, b_ada)


def _ada_bwd(c_all, gmod_cols, name):
    nb, D = c_all.shape
    ncol = gmod_cols.shape[1]
    tc = 512

    def body(c_ref, g_ref, o_ref):
        cv = c_ref[...]
        cond = (cv * jax.nn.sigmoid(cv)).astype(BF16)
        o_ref[...] = _dot_tn(cond, g_ref[...].astype(BF16))

    return pl.pallas_call(
        body, name=name, out_shape=jax.ShapeDtypeStruct((D, ncol), F32), grid=(ncol // tc,),
        in_specs=[pl.BlockSpec((nb, D), lambda j: (0, 0)), pl.BlockSpec((nb, tc), lambda j: (0, j))],
        out_specs=pl.BlockSpec((D, tc), lambda j: (0, j)),
        compiler_params=_cparams(("parallel",)),
    )(c_all, gmod_cols)


def _adam_math(w, g, m, v):
    m = ADAM_B1 * m + (1.0 - ADAM_B1) * g
    v = ADAM_B2 * v + (1.0 - ADAM_B2) * (g * g)
    m_hat = m / (1.0 - ADAM_B1 ** ADAM_STEP)
    v_hat = v / (1.0 - ADAM_B2 ** ADAM_STEP)
    delta = -ADAM_LR * (m_hat / (jnp.sqrt(v_hat) + ADAM_EPS) + ADAM_WD * w)
    return delta, m, v


def _adamw(w, g, m, v, name):
    rows, cols = w.shape
    tr = _pick(rows, (256, 192, 176, 128, 64, 8))

    def body(w_ref, g_ref, m_ref, v_ref, d_ref, mo_ref, vo_ref):
        d, mn, vn = _adam_math(w_ref[...], g_ref[...], m_ref[...], v_ref[...])
        d_ref[...] = d
        mo_ref[...] = mn
        vo_ref[...] = vn

    spec = pl.BlockSpec((tr, cols), lambda i: (i, 0))
    return pl.pallas_call(
        body, name=name, out_shape=[jax.ShapeDtypeStruct((rows, cols), F32)] * 3, grid=(rows // tr,),
        in_specs=[spec] * 4, out_specs=[spec] * 3, compiler_params=_cparams(("parallel",)),
    )(w, g, m, v)


VEC_ROWS = 8


def _adamw_rows(w, parts, m, v, name):
    n = w.shape[1]
    P = parts.shape[0]
    assert n % (VEC_ROWS * LANES) == 0, n
    shp = (VEC_ROWS, n // VEC_ROWS)

    def body(w_ref, p_ref, m_ref, v_ref, g_ref, d_ref, mo_ref, vo_ref):
        g = p_ref[0]
        for k in range(1, P):
            g = g + p_ref[k]
        d, mn, vn = _adam_math(w_ref[...], g, m_ref[...], v_ref[...])
        g_ref[...] = g
        d_ref[...] = d
        mo_ref[...] = mn
        vo_ref[...] = vn

    vec = pl.BlockSpec(shp, lambda i: (0, 0))
    out = pl.pallas_call(
        body, name=name, out_shape=[jax.ShapeDtypeStruct(shp, F32)] * 4, grid=(1,),
        in_specs=[vec, pl.BlockSpec((P,) + shp, lambda i: (0, 0, 0)), vec, vec], out_specs=[vec] * 4,
        compiler_params=_cparams(("arbitrary",)),
    )(w.reshape(shp), parts.reshape((P,) + shp), m.reshape(shp), v.reshape(shp))
    return [o.reshape(1, n) for o in out]


_PACKED = (("w_in", 1024, 552), ("w_uq", 384, 192), ("w_ukv", 256, 256))
_SHARDED = ("w_in", "w_uq", "w_ukv", "w_out", "w_ffn_in", "w_ffn_out")
_SMALL = (("g_norm1", 1024), ("g_cq", 384), ("g_ckv", 256), ("rel_bias", 256), ("g_out_a", 512),
          ("g_out_b", 512), ("g_norm2", 1024), ("g_final", 1024))
_SMALL_PAD = 5120
PACK_ROWS = 704
_PACK_ELEMS = PACK_ROWS * D_MODEL


def _pack_shards(shards, dtype):
    lead = shards["w_in"].shape[:-2]
    flat = jnp.concatenate([shards[n].astype(dtype).reshape(lead + (-1,)) for n, _, _ in _PACKED], axis=-1)
    pad = [(0, 0)] * len(lead) + [(0, _PACK_ELEMS - flat.shape[-1])]
    return jnp.pad(flat, pad).reshape(lead + (PACK_ROWS, D_MODEL))


def _unpack_shards(packed):
    out, off = {}, 0
    for n, r, c in _PACKED:
        out[n] = packed[..., off:off + r * c].reshape(packed.shape[:-1] + (r, c))
        off += r * c
    return out


def _full_from_shards(sh):
    return jnp.transpose(sh, (1, 0, 2)).reshape(sh.shape[1], -1)


def _shards_from_full(full):
    rows, cols = full.shape
    return jnp.transpose(full.reshape(rows, N_CHIP, cols // N_CHIP), (1, 0, 2))


def kernel(x, c, w_ada, b_ada, g_norm1, w_in, g_cq, w_uq, g_ckv, w_ukv, rel_bias, g_out_a, g_out_b, w_out, g_norm2, w_ffn_in, w_ffn_out, g_final, loss_target, m_w_ada, m_b_ada, m_g_norm1, m_w_in, m_g_cq, m_w_uq, m_g_ckv, m_w_ukv, m_rel_bias, m_g_out_a, m_g_out_b, m_w_out, m_g_norm2, m_w_ffn_in, m_w_ffn_out, m_g_final, v_w_ada, v_b_ada, v_g_norm1, v_w_in, v_g_cq, v_w_uq, v_g_ckv, v_w_ukv, v_rel_bias, v_g_out_a, v_g_out_b, v_w_out, v_g_norm2, v_w_ffn_in, v_w_ffn_out, v_g_final):
    names = ["w_ada", "b_ada", "g_norm1", "w_in", "g_cq", "w_uq", "g_ckv", "w_ukv", "rel_bias", "g_out_a",
             "g_out_b", "w_out", "g_norm2", "w_ffn_in", "w_ffn_out", "g_final"]
    W = dict(zip(names, [w_ada, b_ada, g_norm1, w_in, g_cq, w_uq, g_ckv, w_ukv, rel_bias, g_out_a, g_out_b,
                         w_out, g_norm2, w_ffn_in, w_ffn_out, g_final]))
    M = dict(zip(names, [m_w_ada, m_b_ada, m_g_norm1, m_w_in, m_g_cq, m_w_uq, m_g_ckv, m_w_ukv, m_rel_bias,
                         m_g_out_a, m_g_out_b, m_w_out, m_g_norm2, m_w_ffn_in, m_w_ffn_out, m_g_final]))
    V = dict(zip(names, [v_w_ada, v_b_ada, v_g_norm1, v_w_in, v_g_cq, v_w_uq, v_g_ckv, v_w_ukv, v_rel_bias,
                         v_g_out_a, v_g_out_b, v_w_out, v_g_norm2, v_w_ffn_in, v_w_ffn_out, v_g_final]))
    B, S, D = x.shape
    mx, my, mc = _my_place()
    dev = 4 * mx + 2 * my + mc
    chip = 2 * mx + my
    pad_rows = 8

    c_all = _allgather8(jnp.pad(c, ((0, pad_rows - B), (0, 0))), "ag_c", False)
    c_all = c_all.reshape(N_DEV, pad_rows, D)[:, :B].reshape(N_DEV * B, D)
    ada_cols = w_ada.shape[-1]
    b_cols = lax.dynamic_slice_in_dim(b_ada, chip * ada_cols, ada_cols, axis=1)
    mod_cols = _ada_fwd(c_all, w_ada[0], b_cols, "ada_fwd")
    mod_all = _allgather8(mod_cols, "ag_mod", False).reshape(N_DEV, N_DEV * B, ada_cols)[0::2]
    mod_all = jnp.transpose(mod_all, (1, 0, 2)).reshape(N_DEV * B, N_MOD * D)
    mod = lax.dynamic_slice_in_dim(mod_all, dev * B, B, axis=0)

    packed = _pack_shards({n: W[n][0] for n, _, _ in _PACKED}, BF16)
    g_packed, g_out = _gather_weights([packed, w_out[0].astype(BF16)], "ag_weights")
    full = {n: _full_from_shards(sh) for n, sh in _unpack_shards(g_packed.reshape(N_CHIP, _PACK_ELEMS)).items()}
    wts = dict(w_in=_w_in_to_kernel(full["w_in"]), w_uq=_w_uq_to_kernel(full["w_uq"]),
               w_kv=_w_ukv_to_kernel(full["w_ukv"]), w_out=g_out.reshape(D, D))
    gains = dict(g_norm1=g_norm1, g_cq=g_cq, g_ckv=g_ckv, g_out_a=g_out_a, g_out_b=g_out_b, g_norm2=g_norm2,
                 g_final=g_final.reshape(1, D))

    loss, grad_x, gmod, grads = _local_step(x, loss_target, mod, wts, gains, rel_bias,
                                            ffn_shards=[w_ffn_in[0].astype(BF16), w_ffn_out[0].astype(BF16)])
    loss = lax.psum(loss[0, 0], ("x", "y", "c"))

    n_small = _SMALL_PAD
    cat = lambda dct: jnp.concatenate([dct[n].reshape(1, -1) for n, _ in _SMALL]
                                      + [jnp.zeros((1, _SMALL_PAD - sum(s for _, s in _SMALL)), F32)], axis=1)
    small = cat(grads)
    rows = jnp.concatenate([gmod, jnp.pad(small, ((0, 0), (0, N_MOD * D - n_small))),
                            jnp.zeros((pad_rows - B - 1, N_MOD * D), F32)], axis=0)
    rows_all = _allgather8(rows, "ag_small", False).reshape(N_DEV, pad_rows, N_MOD * D)
    gmod_all = rows_all[:, :B].reshape(N_DEV * B, N_MOD * D)
    small_parts = rows_all[:, B, :n_small]

    nat = dict(w_in=_w_in_from_kernel(grads["w_in"]), w_uq=_w_uq_from_kernel(grads["w_uq"]),
               w_ukv=_w_ukv_from_kernel(grads["w_kv"]))
    gp = _pack_shards({n: _shards_from_full(nat[n]) for n, _, _ in _PACKED}, F32)
    as_halves = lambda a: a.reshape(N_DEV, -1, a.shape[-1])
    a4 = _rs_first([as_halves(gp), as_halves(grads["w_out"])], "mix")
    r2 = _rs_to_chips(a4, "rs_to_chips_mix")
    ffn_a4, ffn_r2 = grads["ffn_pending"]
    r_packed, r_out, r_ffn_in, r_ffn_out = _rs_last(list(a4) + list(ffn_a4), list(r2) + list(ffn_r2), "all")
    G = _unpack_shards(r_packed.reshape(_PACK_ELEMS))
    G.update(w_out=r_out, w_ffn_in=r_ffn_in, w_ffn_out=r_ffn_out)

    gmod_cols = lax.dynamic_slice_in_dim(gmod_all, chip * ada_cols, ada_cols, axis=1)
    G["w_ada"] = _ada_bwd(c_all, gmod_cols, "ada_bwd")
    delta, new_m, new_v = {}, {}, {}
    for n in ("w_ada",) + _SHARDED:
        shp = W[n].shape
        w2 = W[n].reshape(shp[-2], shp[-1])
        d_, m_, v_ = _adamw(w2, G[n], M[n].reshape(w2.shape), V[n].reshape(w2.shape), f"adamw_{n}")
        G[n], delta[n], new_m[n], new_v[n] = [a.reshape(shp) for a in (G[n], d_, m_, v_)]
    gs, ds_, ms_, vs_ = _adamw_rows(cat(W), small_parts, cat(M), cat(V), "adamw_small")
    off = 0
    for n, sz in _SMALL:
        shp = W[n].shape
        G[n], delta[n], new_m[n], new_v[n] = [a[:, off:off + sz].reshape(shp) for a in (gs, ds_, ms_, vs_)]
        off += sz
    G["b_ada"], delta["b_ada"], new_m["b_ada"], new_v["b_ada"] = _adamw_rows(b_ada, gmod_all, m_b_ada, v_b_ada,
                                                                          "adamw_b_ada")
    return (loss, grad_x, *[G[n] for n in names], *[delta[n] for n in names], *[new_m[n] for n in names],
            *[new_v[n] for n in names])
```

```python
import functools
import math

import numpy as np
import jax
import jax.numpy as jnp
from jax import lax
from jax.experimental import pallas as pl
from jax.experimental.pallas import tpu as pltpu

F32 = jnp.float32
BF16 = jnp.bfloat16

D_MODEL = 1024
SEQ = 2048
N_HEADS = 8
HEAD_DIM = 64
D_A = 512
D_B = 512
Q_LORA = 384
KV_LORA = 256
ROPE_DIM = 32
NOPE_DIM = 64
D_FF = 2816
N_MOD = 6
N_BUCKETS = 32
MAX_DISTANCE = 2048
ROPE_THETA = 10000.0
EPS = 1e-6
NEG = -1e30
BLK = 128
DILATIONS = (1, 4, 16)
SPAN = 128
MLA_SCALE = (NOPE_DIM + ROPE_DIM) ** -0.5
DIL_SCALE = HEAD_DIM ** -0.5

ADAM_LR = 0.001
ADAM_B1 = 0.9
ADAM_B2 = 0.999
ADAM_EPS = 1e-08
ADAM_WD = 0.01
ADAM_STEP = 10

N_DEV = 8
N_CHIP = 4
LANES = 128
VMEM_LIMIT = 48 * 1024 * 1024

P_QKV = 3 * D_A
P_REST = KV_LORA + LANES + Q_LORA


def _cparams(sem=None):
    return pltpu.CompilerParams(dimension_semantics=sem, vmem_limit_bytes=VMEM_LIMIT)


def _pick(n, cands):
    for c in cands:
        if n % c == 0:
            return c
    raise ValueError(f"no tile for {n} in {cands}")


def _mm(a, b, mode, out_dtype, name, col_blocks=None, comm=None):
    blocked = col_blocks is not None
    if mode == "nn":
        (M, K) = a.shape
        K2, N = (b.shape[1], b.shape[0] * b.shape[2]) if blocked else b.shape
    elif mode == "nt":
        (M, K) = a.shape
        N, K2 = (b.shape[1], b.shape[0] * b.shape[2]) if blocked else b.shape
    else:
        (K, M), (K2, N) = a.shape, b.shape
    assert K == K2, (a.shape, b.shape, mode)
    tm = _pick(M, (512, 384, 256, 128))
    tn = _pick(N, (1408, 1024, 768, 512, 384, 256, 128))
    tk = _pick(K, (1024, 512, 384, 256, 128))
    if blocked and mode == "nt":
        tk = K // col_blocks
    elif blocked:
        tn = N // col_blocks
    nk = K // tk
    out_shape = (M, N)
    out_spec = pl.BlockSpec((tm, tn), lambda i, j, k: (i, j))
    if mode == "nn":
        a_spec = pl.BlockSpec((tm, tk), lambda i, j, k: (i, k))
        b_spec = (pl.BlockSpec((None, tk, tn), lambda i, j, k: (j, k, 0)) if blocked
                  else pl.BlockSpec((tk, tn), lambda i, j, k: (k, j)))
        dn = (((1,), (0,)), ((), ()))
    elif mode == "nt":
        a_spec = pl.BlockSpec((tm, tk), lambda i, j, k: (i, k))
        b_spec = (pl.BlockSpec((None, tn, tk), lambda i, j, k: (k, j, 0)) if blocked
                  else pl.BlockSpec((tn, tk), lambda i, j, k: (j, k)))
        dn = (((1,), (1,)), ((), ()))
    else:
        a_spec = pl.BlockSpec((tk, tm), lambda i, j, k: (k, i))
        b_spec = pl.BlockSpec((tk, tn), lambda i, j, k: (k, j))
        dn = (((0,), (0,)), ((), ()))
        if blocked:
            out_shape = (col_blocks, M, tn)
            out_spec = pl.BlockSpec((None, tm, tn), lambda i, j, k: (j, i, 0))

    def body(a_ref, b_ref, o_ref, acc_ref):
        k = pl.program_id(2)

        @pl.when(k == 0)
        def _():
            acc_ref[...] = jnp.zeros_like(acc_ref)

        acc_ref[...] += lax.dot_general(a_ref[...].astype(BF16), b_ref[...].astype(BF16), dn,
                                        preferred_element_type=F32)

        @pl.when(k == nk - 1)
        def _():
            o_ref[...] = acc_ref[...].astype(o_ref.dtype)

    if comm is not None:
        (out,), got = _host_call(
            body, comm, name=name, out_shape=[jax.ShapeDtypeStruct(out_shape, out_dtype)],
            grid=(M // tm, N // tn, nk), in_specs=[a_spec, b_spec], out_specs=[out_spec],
            scratch_shapes=[pltpu.VMEM((tm, tn), F32)], args=(a, b))
        return out, got
    return pl.pallas_call(
        body, name=name,
        out_shape=jax.ShapeDtypeStruct(out_shape, out_dtype),
        grid=(M // tm, N // tn, nk),
        in_specs=[a_spec, b_spec],
        out_specs=out_spec,
        scratch_shapes=[pltpu.VMEM((tm, tn), F32)],
        compiler_params=_cparams(("parallel", "parallel", "arbitrary")),
    )(a, b)


ROW_TILE = 256


def _adaln_fwd(x, g, sc, sh, name, mix=None, gate=None):
    B, S, D = x.shape
    ts = ROW_TILE
    has_res = mix is not None

    def body(*refs):
        if has_res:
            x_ref, g_ref, sc_ref, sh_ref, mix_ref, gate_ref, h_ref, xr_ref = refs
            xr = x_ref[0] + gate_ref[0] * mix_ref[0]
            xr_ref[0] = xr
        else:
            x_ref, g_ref, sc_ref, sh_ref, h_ref = refs
            xr = x_ref[0]
        r = lax.rsqrt(jnp.mean(xr * xr, axis=-1, keepdims=True) + EPS)
        xn = (xr * r) * g_ref[...]
        h_ref[0] = (xn * (1.0 + sc_ref[0]) + sh_ref[0]).astype(h_ref.dtype)

    tok = pl.BlockSpec((1, ts, D), lambda b, s: (b, s, 0))
    per_b = pl.BlockSpec((1, 1, D), lambda b, s: (b, 0, 0))
    vec = pl.BlockSpec((1, D), lambda b, s: (0, 0))
    in_specs = [tok, vec, per_b, per_b]
    args = [x, g, sc, sh]
    out_shape = [jax.ShapeDtypeStruct((B, S, D), BF16)]
    out_specs = [tok]
    if has_res:
        in_specs += [tok, per_b]
        args += [mix, gate]
        out_shape.append(jax.ShapeDtypeStruct((B, S, D), F32))
        out_specs.append(tok)
    out = pl.pallas_call(
        body, name=name, out_shape=out_shape, grid=(B, S // ts),
        in_specs=in_specs, out_specs=out_specs,
        compiler_params=_cparams(("parallel", "parallel")),
    )(*args)
    return out if has_res else out[0]


def _adaln_bwd(dh, x, g, sc, dres, name, mix=None, gate=None):
    B, S, D = x.shape
    ts = ROW_TILE
    has_res = mix is not None

    def body(*refs):
        if has_res:
            (dh_ref, x_ref, g_ref, sc_ref, dres_ref, mix_ref, gate_ref,
             dx_ref, dsh_ref, dsc_ref, dg_ref, dgate_ref, dmix_ref) = refs
        else:
            (dh_ref, x_ref, g_ref, sc_ref, dres_ref, dx_ref, dsh_ref, dsc_ref, dg_ref) = refs
        b, s = pl.program_id(0), pl.program_id(1)
        xv = x_ref[0]
        dhv = dh_ref[0]
        gv = g_ref[...]
        r = lax.rsqrt(jnp.mean(xv * xv, axis=-1, keepdims=True) + EPS)
        n = xv * r
        xn = n * gv
        dxn = dhv * (1.0 + sc_ref[0])
        dn = dxn * gv
        dx = r * (dn - n * jnp.mean(dn * n, axis=-1, keepdims=True)) + dres_ref[0]
        dx_ref[0] = dx

        @pl.when(s == 0)
        def _():
            dsh_ref[...] = jnp.zeros_like(dsh_ref)
            dsc_ref[...] = jnp.zeros_like(dsc_ref)
            if has_res:
                dgate_ref[...] = jnp.zeros_like(dgate_ref)

        @pl.when((s == 0) & (b == 0))
        def _():
            dg_ref[...] = jnp.zeros_like(dg_ref)

        dsh_ref[0] += jnp.sum(dhv, axis=0, keepdims=True)
        dsc_ref[0] += jnp.sum(dhv * xn, axis=0, keepdims=True)
        dg_ref[...] += jnp.sum(dxn * n, axis=0, keepdims=True)
        if has_res:
            dgate_ref[0] += jnp.sum(dx * mix_ref[0], axis=0, keepdims=True)
            dmix_ref[0] = (dx * gate_ref[0]).astype(dmix_ref.dtype)

    tok = pl.BlockSpec((1, ts, D), lambda b, s: (b, s, 0))
    per_b = pl.BlockSpec((1, 1, D), lambda b, s: (b, 0, 0))
    vec = pl.BlockSpec((1, D), lambda b, s: (0, 0))
    in_specs = [tok, tok, vec, per_b, tok]
    args = [dh, x, g, sc, dres]
    out_shape = [jax.ShapeDtypeStruct((B, S, D), F32), jax.ShapeDtypeStruct((B, 1, D), F32),
                 jax.ShapeDtypeStruct((B, 1, D), F32), jax.ShapeDtypeStruct((1, D), F32)]
    out_specs = [tok, per_b, per_b, vec]
    if has_res:
        in_specs += [tok, per_b]
        args += [mix, gate]
        out_shape += [jax.ShapeDtypeStruct((B, 1, D), F32), jax.ShapeDtypeStruct((B, S, D), BF16)]
        out_specs += [per_b, tok]
    return pl.pallas_call(
        body, name=name, out_shape=out_shape, grid=(B, S // ts),
        in_specs=in_specs, out_specs=out_specs,
        compiler_params=_cparams(("arbitrary", "arbitrary")),
    )(*args)


def _rms_fwd(x, col_blk, n, g, name, n_real=None):
    T = x.shape[0]
    tr = 512
    nr = float(n_real or n)

    def body(x_ref, g_ref, y_ref):
        xv = x_ref[...]
        r = lax.rsqrt(jnp.sum(xv * xv, axis=-1, keepdims=True) / nr + EPS)
        y_ref[...] = ((xv * r) * g_ref[...]).astype(y_ref.dtype)

    return pl.pallas_call(
        body, name=name, out_shape=jax.ShapeDtypeStruct((T, n), BF16), grid=(T // tr,),
        in_specs=[pl.BlockSpec((tr, n), lambda i: (i, col_blk)), pl.BlockSpec((1, n), lambda i: (0, 0))],
        out_specs=pl.BlockSpec((tr, n), lambda i: (i, 0)),
        compiler_params=_cparams(("parallel",)),
    )(x, g)


def _rms_bwd(dy, dy_blk, x, x_blk, n, g, name, out_dtype=BF16):
    T = x.shape[0]
    tr = 512

    def body(dy_ref, x_ref, g_ref, dx_ref, dg_ref):
        xv = x_ref[...]
        dyv = dy_ref[...].astype(F32)
        r = lax.rsqrt(jnp.mean(xv * xv, axis=-1, keepdims=True) + EPS)
        nrm = xv * r
        dn = dyv * g_ref[...]
        dx_ref[...] = (r * (dn - nrm * jnp.mean(dn * nrm, axis=-1, keepdims=True))).astype(dx_ref.dtype)

        @pl.when(pl.program_id(0) == 0)
        def _():
            dg_ref[...] = jnp.zeros_like(dg_ref)

        dg_ref[...] += jnp.sum(dyv * nrm, axis=0, keepdims=True)

    return pl.pallas_call(
        body, name=name,
        out_shape=[jax.ShapeDtypeStruct((T, n), out_dtype), jax.ShapeDtypeStruct((1, n), F32)],
        grid=(T // tr,),
        in_specs=[pl.BlockSpec((tr, n), lambda i: (i, dy_blk)), pl.BlockSpec((tr, n), lambda i: (i, x_blk)),
                  pl.BlockSpec((1, n), lambda i: (0, 0))],
        out_specs=[pl.BlockSpec((tr, n), lambda i: (i, 0)), pl.BlockSpec((1, n), lambda i: (0, 0))],
        compiler_params=_cparams(("arbitrary",)),
    )(dy, x, g)


def _rms_bwd_views(dy, dy_blk, x, g, name):
    B, S, n = x.shape
    tiles = S // VIEW_TILE

    def body(dy_ref, x_ref, g_ref, d1_ref, d4_ref, d16_ref, dg_ref, dx_s):
        xv = x_ref[0]
        dyv = dy_ref[...]
        r = lax.rsqrt(jnp.mean(xv * xv, axis=-1, keepdims=True) + EPS)
        nrm = xv * r
        dn = dyv * g_ref[...]
        dx = r * (dn - nrm * jnp.mean(dn * nrm, axis=-1, keepdims=True))
        d1_ref[0] = dx.astype(d1_ref.dtype)
        _put_tile(dx_s, dx)
        _tile_to_view(dx_s, d4_ref, DILATIONS[1], n)
        _tile_to_view(dx_s, d16_ref, DILATIONS[2], n)

        @pl.when((pl.program_id(0) == 0) & (pl.program_id(1) == 0))
        def _():
            dg_ref[...] = jnp.zeros_like(dg_ref)

        dg_ref[...] += jnp.sum(dyv * nrm, axis=0, keepdims=True)

    res = pl.pallas_call(
        body, name=name,
        out_shape=[_view_shape(B, S, d, n, BF16) for d in DILATIONS] + [jax.ShapeDtypeStruct((1, n), F32)],
        grid=(B, tiles),
        in_specs=[pl.BlockSpec((VIEW_TILE, n), lambda b, t: (b * tiles + t, dy_blk)), _view_spec(1, n),
                  pl.BlockSpec((1, n), lambda b, t: (0, 0))],
        out_specs=[_view_spec(d, n) for d in DILATIONS] + [pl.BlockSpec((1, n), lambda b, t: (0, 0))],
        scratch_shapes=[_tile_scratch(n)],
        compiler_params=_cparams(("arbitrary", "arbitrary")),
    )(dy, x, g)
    return res[:len(DILATIONS)], res[len(DILATIONS)]


def _swiglu_fwd(gu, name):
    T = gu.shape[0]
    tr, tc = 512, 1408
    nc = D_FF // tc

    def body(g_ref, u_ref, a_ref):
        gv = g_ref[...]
        a_ref[...] = (gv * jax.nn.sigmoid(gv) * u_ref[...]).astype(a_ref.dtype)

    return pl.pallas_call(
        body, name=name, out_shape=jax.ShapeDtypeStruct((T, D_FF), BF16), grid=(T // tr, nc),
        in_specs=[pl.BlockSpec((tr, tc), lambda i, j: (i, j)), pl.BlockSpec((tr, tc), lambda i, j: (i, j + nc))],
        out_specs=pl.BlockSpec((tr, tc), lambda i, j: (i, j)),
        compiler_params=_cparams(("parallel", "parallel")),
    )(gu, gu)


def _swiglu_bwd(da, gu, name):
    T = gu.shape[0]
    tr, tc = 512, 1408
    nc = D_FF // tc

    def body(da_ref, g_ref, u_ref, dgu_ref):
        j = pl.program_id(1)
        gv, uv, dav = g_ref[...], u_ref[...], da_ref[...]
        sg = jax.nn.sigmoid(gv)

        @pl.when(j < nc)
        def _():
            dgu_ref[...] = (dav * uv * (sg * (1.0 + gv * (1.0 - sg)))).astype(dgu_ref.dtype)

        @pl.when(j >= nc)
        def _():
            dgu_ref[...] = (dav * (gv * sg)).astype(dgu_ref.dtype)

    return pl.pallas_call(
        body, name=name, out_shape=jax.ShapeDtypeStruct((T, 2 * D_FF), BF16), grid=(T // tr, 2 * nc),
        in_specs=[pl.BlockSpec((tr, tc), lambda i, j: (i, j % nc)),
                  pl.BlockSpec((tr, tc), lambda i, j: (i, j % nc)),
                  pl.BlockSpec((tr, tc), lambda i, j: (i, j % nc + nc))],
        out_specs=pl.BlockSpec((tr, tc), lambda i, j: (i, j)),
        compiler_params=_cparams(("parallel", "parallel")),
    )(da, gu, gu)


def _final_loss(x1, f, g2, gf, target, name):
    B, S, D = x1.shape
    ts = ROW_TILE

    def body(x1_ref, f_ref, g2_ref, gf_ref, t_ref, dx_ref, df_ref, dg2_ref, dgf_ref, loss_ref):
        b, s = pl.program_id(0), pl.program_id(1)
        fv = f_ref[0]
        g2v = g2_ref[0]
        gfv = gf_ref[...]
        x2 = x1_ref[0] + g2v * fv
        r = lax.rsqrt(jnp.mean(x2 * x2, axis=-1, keepdims=True) + EPS)
        n = x2 * r
        e = n * gfv - t_ref[0]
        dy = e * (1.0 / D)
        dn = dy * gfv
        dx = r * (dn - n * jnp.mean(dn * n, axis=-1, keepdims=True))
        dx_ref[0] = dx
        df_ref[0] = (dx * g2v).astype(df_ref.dtype)

        @pl.when(s == 0)
        def _():
            dg2_ref[...] = jnp.zeros_like(dg2_ref)

        @pl.when((s == 0) & (b == 0))
        def _():
            dgf_ref[...] = jnp.zeros_like(dgf_ref)
            loss_ref[...] = jnp.zeros_like(loss_ref)

        dg2_ref[0] += jnp.sum(dx * fv, axis=0, keepdims=True)
        dgf_ref[...] += jnp.sum(dy * n, axis=0, keepdims=True)
        loss_ref[...] += 0.5 * jnp.sum(jnp.mean(e * e, axis=-1, keepdims=True), axis=0, keepdims=True)

    tok = pl.BlockSpec((1, ts, D), lambda b, s: (b, s, 0))
    per_b = pl.BlockSpec((1, 1, D), lambda b, s: (b, 0, 0))
    vec = pl.BlockSpec((1, D), lambda b, s: (0, 0))
    return pl.pallas_call(
        body, name=name,
        out_shape=[jax.ShapeDtypeStruct((B, S, D), F32), jax.ShapeDtypeStruct((B, S, D), BF16),
                   jax.ShapeDtypeStruct((B, 1, D), F32), jax.ShapeDtypeStruct((1, D), F32),
                   jax.ShapeDtypeStruct((1, LANES), F32)],
        grid=(B, S // ts),
        in_specs=[tok, tok, per_b, vec, tok],
        out_specs=[tok, tok, per_b, vec, pl.BlockSpec((1, LANES), lambda b, s: (0, 0))],
        compiler_params=_cparams(("arbitrary", "arbitrary")),
    )(x1, f, g2, gf, target)


def _rope_tables():
    half = ROPE_DIM // 2
    inv = ROPE_THETA ** (-jnp.arange(half, dtype=F32) / half)
    ang = jnp.arange(SEQ, dtype=F32)[:, None] * inv[None, :]
    cos, sin = jnp.cos(ang), jnp.sin(ang)
    one = jnp.ones((SEQ, NOPE_DIM), F32)
    zero = jnp.zeros((SEQ, NOPE_DIM), F32)
    cs = jnp.concatenate([one, cos, cos, one[:, :LANES - NOPE_DIM - ROPE_DIM]], axis=1)
    sn = jnp.concatenate([zero, -sin, sin, zero[:, :LANES - NOPE_DIM - ROPE_DIM]], axis=1)
    return cs, sn


def _rope_group(t, cs, sn):
    half = ROPE_DIM // 2
    lane = lax.broadcasted_iota(jnp.int32, t.shape, 1)
    partner = jnp.where(lane < NOPE_DIM + half, pltpu.roll(t, LANES - half, 1), pltpu.roll(t, half, 1))
    return t * cs + partner * sn


def _rope_apply(t, cs, sn, out_dtype, name, add=None, add_blk=0):
    B, S, W = t.shape
    G = W // LANES
    ts = ROW_TILE

    def body(*refs):
        if add is None:
            t_ref, cs_ref, sn_ref, o_ref = refs
            for gi in range(G):
                sl = slice(gi * LANES, (gi + 1) * LANES)
                o_ref[0, :, sl] = _rope_group(t_ref[0, :, sl], cs_ref[...], sn_ref[...]).astype(o_ref.dtype)
        else:
            t_ref, a_ref, cs_ref, sn_ref, o_ref = refs
            ra = _rope_group(a_ref[0], cs_ref[...], sn_ref[...])
            for gi in range(G):
                sl = slice(gi * LANES, (gi + 1) * LANES)
                o_ref[0, :, sl] = (t_ref[0, :, sl] + ra).astype(o_ref.dtype)

    tok = pl.BlockSpec((1, ts, W), lambda b, s: (b, s, 0))
    tab = pl.BlockSpec((ts, LANES), lambda b, s: (s, 0))
    in_specs, args = [tok], [t]
    if add is not None:
        in_specs.append(pl.BlockSpec((1, ts, LANES), lambda b, s: (b, s, add_blk)))
        args.append(add)
    in_specs += [tab, tab]
    args += [cs, sn]
    return pl.pallas_call(
        body, name=name, out_shape=jax.ShapeDtypeStruct((B, S, W), out_dtype), grid=(B, S // ts),
        in_specs=in_specs, out_specs=tok, compiler_params=_cparams(("parallel", "parallel")),
    )(*args)


def _krope_bwd(dkc, cs, sn_neg, name):
    B, S, W = dkc.shape
    G = W // LANES
    ts = ROW_TILE

    def body(d_ref, cs_ref, sn_ref, o_ref):
        acc = d_ref[0, :, 0:LANES]
        for gi in range(1, G):
            acc = acc + d_ref[0, :, gi * LANES:(gi + 1) * LANES]
        lane = lax.broadcasted_iota(jnp.int32, acc.shape, 1)
        rot = (lane >= NOPE_DIM) & (lane < NOPE_DIM + ROPE_DIM)
        acc = jnp.where(rot, acc, 0.0)
        o_ref[0] = _rope_group(acc, cs_ref[...], sn_ref[...]).astype(o_ref.dtype)

    tab = pl.BlockSpec((ts, LANES), lambda b, s: (s, 0))
    return pl.pallas_call(
        body, name=name, out_shape=jax.ShapeDtypeStruct((B, S, LANES), BF16), grid=(B, S // ts),
        in_specs=[pl.BlockSpec((1, ts, W), lambda b, s: (b, s, 0)), tab, tab],
        out_specs=pl.BlockSpec((1, ts, LANES), lambda b, s: (b, s, 0)),
        compiler_params=_cparams(("parallel", "parallel")),
    )(dkc, cs, sn_neg)


def _t5_bucket(dist):
    max_exact = N_BUCKETS // 2
    d = np.maximum(dist, 1).astype(np.float64)
    large = max_exact + (np.log(d / max_exact) / np.log(MAX_DISTANCE / max_exact)
                         * (N_BUCKETS - max_exact)).astype(np.int64)
    large = np.minimum(large, N_BUCKETS - 1)
    return np.where(dist < max_exact, dist, large).astype(np.int32)


def _band_buckets(dilation):
    a = np.arange(BLK)[None, :]
    bk = np.arange(2 * BLK)[:, None]
    steps = BLK + a - bk
    return _t5_bucket(np.clip(steps, 0, SPAN) * dilation)


def _head_mask(shape, hh):
    lane = lax.broadcasted_iota(jnp.int32, shape, 1)
    return (lane >= hh * HEAD_DIM) & (lane < (hh + 1) * HEAD_DIM)


def _dot_nt(a, b):
    return lax.dot_general(a, b, (((1,), (1,)), ((), ())), preferred_element_type=F32)


def _dot_tn(a, b):
    return lax.dot_general(a, b, (((0,), (0,)), ((), ())), preferred_element_type=F32)


def _dot_nn(a, b):
    return lax.dot_general(a, b, (((1,), (0,)), ((), ())), preferred_element_type=F32)


def _band_valid_t():
    bk = lax.broadcasted_iota(jnp.int32, (BLK, BLK), 0)
    a = lax.broadcasted_iota(jnp.int32, (BLK, BLK), 1)
    return bk >= a, bk <= a


def _dil_fwd(qkv, bias, branch, dilation, name, comm=None):
    B, n, _ = qkv.shape
    d = dilation
    nb = n // BLK
    qkv_v = qkv
    npair = N_HEADS // 2

    def body(cur_ref, prev_ref, bias_ref, o_ref, lse_ref, s_scr, e_scr):
        i = pl.program_id(2)
        vprev, vcur = _band_valid_t()
        vprev = vprev & (i > 0)
        for p in range(npair):
            q = cur_ref[0, :, p * LANES:(p + 1) * LANES]
            kc = cur_ref[0, :, D_A + p * LANES:D_A + (p + 1) * LANES]
            kp = prev_ref[0, :, D_A + p * LANES:D_A + (p + 1) * LANES]
            for hh in range(2):
                h = 2 * p + hh
                qm = jnp.where(_head_mask((BLK, LANES), hh), q, jnp.zeros_like(q))
                s_scr[h, 0:BLK, :] = _dot_nt(kp, qm)
                s_scr[h, BLK:2 * BLK, :] = _dot_nt(kc, qm)
        ms = []
        for h in range(N_HEADS):
            s_p = jnp.where(vprev, s_scr[h, 0:BLK, :] * DIL_SCALE + bias_ref[h, 0:BLK, :], NEG)
            s_c = jnp.where(vcur, s_scr[h, BLK:2 * BLK, :] * DIL_SCALE + bias_ref[h, BLK:2 * BLK, :], NEG)
            m = jnp.maximum(jnp.max(s_p, axis=0, keepdims=True), jnp.max(s_c, axis=0, keepdims=True))
            e_scr[h, 0:BLK, :] = jnp.exp(s_p - m).astype(BF16)
            e_scr[h, BLK:2 * BLK, :] = jnp.exp(s_c - m).astype(BF16)
            ms.append(m)
        rows0 = _row_mask((LANES, BLK), 0)
        for p in range(npair):
            sl = slice(p * LANES, (p + 1) * LANES)
            vsl = slice(2 * D_A + p * LANES, 2 * D_A + (p + 1) * LANES)
            vct = jnp.transpose(cur_ref[0, :, vsl].astype(F32)).astype(BF16)
            vpt = jnp.transpose(prev_ref[0, :, vsl].astype(F32)).astype(BF16)
            acc = []
            for hh in range(2):
                h = 2 * p + hh
                mine = _row_mask((LANES, BLK), hh)
                one = jnp.ones_like(vct)
                acc.append(_dot_nn(jnp.where(mine, vpt, one), e_scr[h, 0:BLK, :])
                           + _dot_nn(jnp.where(mine, vct, one), e_scr[h, BLK:2 * BLK, :]))
            l0 = acc[0][HEAD_DIM:HEAD_DIM + 1, :]
            l1 = acc[1][0:1, :]
            o_t = jnp.where(rows0, acc[0] / l0, acc[1] / l1)
            lse_t = jnp.where(rows0, ms[2 * p] + jnp.log(l0), ms[2 * p + 1] + jnp.log(l1))
            o_ref[0, :, sl] = jnp.transpose(o_t)
            lse_ref[0, :, sl] = jnp.transpose(lse_t)

    cur = pl.BlockSpec((1, BLK, P_QKV), lambda b, r, i: (b, i, r))
    prev = pl.BlockSpec((1, BLK, P_QKV), lambda b, r, i: (b, jnp.maximum(i - 1, 0), r))
    out = pl.BlockSpec((1, BLK, D_A), lambda b, r, i: (b, i, r))
    return _host_call(
        body, comm, name=name,
        out_shape=[jax.ShapeDtypeStruct((B, n, d * D_A), F32)] * 2,
        grid=(B, d, nb),
        in_specs=[cur, prev,
                  pl.BlockSpec((None, N_HEADS, 2 * BLK, BLK), lambda b, r, i: (branch, 0, 0, 0))],
        out_specs=[out, out],
        scratch_shapes=[pltpu.VMEM((N_HEADS, 2 * BLK, BLK), F32), pltpu.VMEM((N_HEADS, 2 * BLK, BLK), BF16)],
        args=(qkv_v, qkv_v, bias))


VIEW_TILE = 512


def _view_spec(d, w):
    return pl.BlockSpec((1, VIEW_TILE // d, d * w), lambda b, t: (b, t, 0))


def _view_shape(B, S, d, w, dtype):
    return jax.ShapeDtypeStruct((B, S // d, d * w), dtype)


def _tile_scratch(w):
    return pltpu.VMEM((w // LANES, VIEW_TILE, LANES), F32)


def _put_tile(tile_ref, val):
    for c in range(tile_ref.shape[0]):
        tile_ref[c] = val[:, c * LANES:(c + 1) * LANES]


def _get_tile(tile_ref):
    return jnp.concatenate([tile_ref[c] for c in range(tile_ref.shape[0])], axis=1)


def _tile_to_view(tile_ref, view_ref, d, w):
    for c in range(w // LANES):
        for r in range(d):
            lo = r * w + c * LANES
            rows = tile_ref.at[c][pl.ds(r, VIEW_TILE // d, stride=d), :]
            view_ref[0, :, lo:lo + LANES] = rows.astype(view_ref.dtype)


def _view_to_tile(view_ref, tile_ref, d, w):
    for c in range(w // LANES):
        for r in range(d):
            lo = r * w + c * LANES
            tile_ref.at[c][pl.ds(r, VIEW_TILE // d, stride=d), :] = view_ref[0, :, lo:lo + LANES].astype(F32)


def _mm_qkv_views(h, w, name):
    B, S, D = h.shape
    N = w.shape[1]

    def body(h_ref, w_ref, o1_ref, o4_ref, o16_ref, acc_ref):
        acc = jnp.dot(h_ref[0], w_ref[...], preferred_element_type=F32)
        o1_ref[0] = acc.astype(o1_ref.dtype)
        _put_tile(acc_ref, acc)
        _tile_to_view(acc_ref, o4_ref, DILATIONS[1], N)
        _tile_to_view(acc_ref, o16_ref, DILATIONS[2], N)

    return pl.pallas_call(
        body, name=name,
        out_shape=[_view_shape(B, S, d, N, BF16) for d in DILATIONS],
        grid=(B, S // VIEW_TILE),
        in_specs=[pl.BlockSpec((1, VIEW_TILE, D), lambda b, t: (b, t, 0)), pl.BlockSpec((D, N), lambda b, t: (0, 0))],
        out_specs=[_view_spec(d, N) for d in DILATIONS],
        scratch_shapes=[_tile_scratch(N)],
        compiler_params=_cparams(("parallel", "parallel")),
    )(h, w)


def _dil_merge(os_, lses, name):
    B, S, W = os_[0].shape
    nd = len(DILATIONS)

    def body(*refs):
        o_refs, l_refs = refs[:nd], refs[nd:2 * nd]
        out_refs, L_refs = refs[2 * nd:3 * nd], refs[3 * nd:4 * nd]
        scr = refs[4 * nd:]
        o_tok, l_tok = [o_refs[0][0]], [l_refs[0][0]]
        for i, d in enumerate(DILATIONS[1:]):
            _view_to_tile(o_refs[i + 1], scr[2 * i], d, W)
            _view_to_tile(l_refs[i + 1], scr[2 * i + 1], d, W)
            o_tok.append(_get_tile(scr[2 * i]))
            l_tok.append(_get_tile(scr[2 * i + 1]))
        a0, a1, a2 = l_tok
        m = jnp.maximum(jnp.maximum(a0, a1), a2)
        e0, e1, e2 = jnp.exp(a0 - m), jnp.exp(a1 - m), jnp.exp(a2 - m)
        ssum = e0 + e1 + e2
        out = (e0 * o_tok[0] + e1 * o_tok[1] + e2 * o_tok[2]) / ssum
        lse = m + jnp.log(ssum)
        out_refs[0][0] = out
        L_refs[0][0] = lse
        res_o, res_l = scr[2 * (nd - 1)], scr[2 * (nd - 1) + 1]
        _put_tile(res_o, out)
        _put_tile(res_l, lse)
        for i, d in enumerate(DILATIONS[1:]):
            _tile_to_view(res_o, out_refs[i + 1], d, W)
            _tile_to_view(res_l, L_refs[i + 1], d, W)

    specs = [_view_spec(d, W) for d in DILATIONS]
    shapes = [_view_shape(B, S * DILATIONS[0], d, W, F32) for d in DILATIONS]
    res = pl.pallas_call(
        body, name=name, out_shape=shapes * 2, grid=(B, S // VIEW_TILE),
        in_specs=specs * 2, out_specs=specs * 2,
        scratch_shapes=[_tile_scratch(W)] * (2 * nd),
        compiler_params=_cparams(("parallel", "parallel")),
    )(*os_, *lses)
    return res[:nd], res[nd:]


def _dil_bwd(qkv, do, out_a, L, bias, branch, dilation, name, comm=None):
    B, n, _ = qkv.shape
    d = dilation
    nb = n // BLK
    qkv_v, do_v, oa_v, L_v = qkv, do, out_a, L
    npair = N_HEADS // 2
    multi = nb > 1

    tiles = ("P", "C", "N") if multi else ("C",)
    n_t = len(tiles)

    def body(*refs):
        if multi:
            (cur_ref, prev_ref, next_ref, do_ref, don_ref, oa_ref, oan_ref, L_ref, Ln_ref, bias_ref,
             dqkv_ref, dbias_ref, s_scr, dp_scr, p_scr, ds_scr) = refs
        else:
            cur_ref, do_ref, oa_ref, L_ref, bias_ref, dqkv_ref, dbias_ref, s_scr, dp_scr, p_scr, ds_scr = refs
        b, r, i = pl.program_id(0), pl.program_id(1), pl.program_id(2)

        @pl.when((b == 0) & (r == 0) & (i == 0))
        def _():
            dbias_ref[...] = jnp.zeros_like(dbias_ref)

        vprev, vcur = _band_valid_t()
        valid = {"P": vprev & (i > 0), "C": vcur, "N": vprev & (i < nb - 1)}
        band = {"P": slice(0, BLK), "C": slice(BLK, 2 * BLK), "N": slice(0, BLK)}
        psl = lambda p: slice(p * LANES, (p + 1) * LANES)
        ksl = lambda p: slice(D_A + p * LANES, D_A + (p + 1) * LANES)
        vsl = lambda p: slice(2 * D_A + p * LANES, 2 * D_A + (p + 1) * LANES)

        def operands(p, hh):
            hm = _head_mask((BLK, LANES), hh)
            mask = lambda x: jnp.where(hm, x, jnp.zeros_like(x))
            qm, dom = mask(cur_ref[0, :, psl(p)]), mask(do_ref[0, :, psl(p)])
            ops = {"C": (cur_ref[0, :, ksl(p)], cur_ref[0, :, vsl(p)], qm, dom)}
            if multi:
                ops["P"] = (prev_ref[0, :, ksl(p)], prev_ref[0, :, vsl(p)], qm, dom)
                ops["N"] = (cur_ref[0, :, ksl(p)], cur_ref[0, :, vsl(p)], mask(next_ref[0, :, psl(p)]),
                            mask(don_ref[0, :, psl(p)]))
            return ops

        for p in range(npair):
            for hh in range(2):
                h = 2 * p + hh
                ops = operands(p, hh)
                for t, name_t in enumerate(tiles):
                    k_t, v_t, q_t, do_t = ops[name_t]
                    s_scr[h, t] = _dot_nt(k_t, q_t)
                    dp_scr[h, t] = _dot_nt(v_t, do_t)

        def rows(L_r, do_r, oa_r, p):
            lt = jnp.transpose(L_r[0, :, psl(p)])
            dt = jnp.transpose(do_r[0, :, psl(p)].astype(F32) * oa_r[0, :, psl(p)])
            return ([lt[0:1, :], lt[HEAD_DIM:HEAD_DIM + 1, :]],
                    [jnp.sum(dt[:HEAD_DIM], axis=0, keepdims=True), jnp.sum(dt[HEAD_DIM:], axis=0, keepdims=True)])

        for p in range(npair):
            lse_c, delta_c = rows(L_ref, do_ref, oa_ref, p)
            if multi:
                lse_n, delta_n = rows(Ln_ref, don_ref, oan_ref, p)
            for hh in range(2):
                h = 2 * p + hh
                for t, name_t in enumerate(tiles):
                    lse, delta = (lse_n[hh], delta_n[hh]) if name_t == "N" else (lse_c[hh], delta_c[hh])
                    s = s_scr[h, t] * DIL_SCALE + bias_ref[h, band[name_t], :]
                    pr = jnp.where(valid[name_t], jnp.exp(s - lse), 0.0)
                    ds = pr * (dp_scr[h, t] - delta)
                    p_scr[h, t] = pr.astype(BF16)
                    ds_scr[h, t] = ds.astype(BF16)
                    if name_t != "N":
                        dbias_ref[h, band[name_t], :] += ds

        for p in range(npair):
            dqt = jnp.zeros((LANES, BLK), F32)
            dk = jnp.zeros((BLK, LANES), F32)
            dv = jnp.zeros((BLK, LANES), F32)
            kct = jnp.transpose(cur_ref[0, :, ksl(p)].astype(F32)).astype(BF16)
            if multi:
                kpt = jnp.transpose(prev_ref[0, :, ksl(p)].astype(F32)).astype(BF16)
            for hh in range(2):
                h = 2 * p + hh
                ops = operands(p, hh)
                mine = _row_mask((LANES, BLK), hh)
                for t, name_t in enumerate(tiles):
                    _, _, q_t, do_t = ops[name_t]
                    if name_t != "P":
                        dv = dv + _dot_nn(p_scr[h, t], do_t)
                        dk = dk + _dot_nn(ds_scr[h, t], q_t)
                    if name_t != "N":
                        kt = kpt if name_t == "P" else kct
                        dqt = dqt + _dot_nn(jnp.where(mine, kt, jnp.zeros_like(kt)), ds_scr[h, t])
            dqkv_ref[0, :, psl(p)] = jnp.transpose(dqt) * DIL_SCALE
            dqkv_ref[0, :, ksl(p)] = dk * DIL_SCALE
            dqkv_ref[0, :, vsl(p)] = dv

    def at(off):
        return lambda b, r, i: (b, jnp.clip(i + off, 0, nb - 1), r)

    qkv_spec = lambda off: pl.BlockSpec((1, BLK, P_QKV), at(off))
    da_spec = lambda off: pl.BlockSpec((1, BLK, D_A), at(off))
    bias_spec = pl.BlockSpec((None, N_HEADS, 2 * BLK, BLK), lambda b, r, i: (branch, 0, 0, 0))
    dbias_spec = pl.BlockSpec((N_HEADS, 2 * BLK, BLK), lambda b, r, i: (0, 0, 0))
    if multi:
        in_specs = [qkv_spec(0), qkv_spec(-1), qkv_spec(1), da_spec(0), da_spec(1), da_spec(0), da_spec(1),
                    da_spec(0), da_spec(1), bias_spec]
        args = [qkv_v, qkv_v, qkv_v, do_v, do_v, oa_v, oa_v, L_v, L_v, bias]
    else:
        in_specs = [qkv_spec(0), da_spec(0), da_spec(0), da_spec(0), bias_spec]
        args = [qkv_v, do_v, oa_v, L_v, bias]
    return _host_call(
        body, comm, name=name,
        out_shape=[jax.ShapeDtypeStruct((B, n, d * P_QKV), F32),
                   jax.ShapeDtypeStruct((N_HEADS, 2 * BLK, BLK), F32)],
        grid=(B, d, nb),
        in_specs=in_specs,
        out_specs=[qkv_spec(0), dbias_spec],
        scratch_shapes=[pltpu.VMEM((N_HEADS, n_t, BLK, BLK), F32), pltpu.VMEM((N_HEADS, n_t, BLK, BLK), F32),
                        pltpu.VMEM((N_HEADS, n_t, BLK, BLK), BF16), pltpu.VMEM((N_HEADS, n_t, BLK, BLK), BF16)],
        args=args)


def _sum_views_bf16(parts, name):
    B, S, W = parts[0].shape

    def body(a_ref, b_ref, c_ref, o_ref, sb, sc):
        _view_to_tile(b_ref, sb, DILATIONS[1], W)
        _view_to_tile(c_ref, sc, DILATIONS[2], W)
        o_ref[0] = (a_ref[0] + _get_tile(sb) + _get_tile(sc)).astype(o_ref.dtype)

    return pl.pallas_call(
        body, name=name, out_shape=jax.ShapeDtypeStruct((B, S, W), BF16), grid=(B, S // VIEW_TILE),
        in_specs=[_view_spec(d, W) for d in DILATIONS], out_specs=_view_spec(1, W),
        scratch_shapes=[_tile_scratch(W)] * 2,
        compiler_params=_cparams(("parallel", "parallel")),
    )(*parts)


def _bias_tables(rel_bias, buckets, name):
    nbr = buckets.shape[0]

    def body(rb_ref, bk_ref, o_ref):
        h = pl.program_id(1)
        tab = bk_ref[0]

        def step(bkt, acc):
            return jnp.where(tab == bkt, rb_ref[bkt, h], acc)

        o_ref[0, 0] = lax.fori_loop(0, N_BUCKETS, step, jnp.zeros((2 * BLK, BLK), F32))

    return pl.pallas_call(
        body, name=name, out_shape=jax.ShapeDtypeStruct((nbr, N_HEADS, 2 * BLK, BLK), F32),
        grid=(nbr, N_HEADS),
        in_specs=[pl.BlockSpec(memory_space=pltpu.SMEM),
                  pl.BlockSpec((1, 2 * BLK, BLK), lambda i, h: (i, 0, 0))],
        out_specs=pl.BlockSpec((1, 1, 2 * BLK, BLK), lambda i, h: (i, h, 0, 0)),
        compiler_params=_cparams(("parallel", "arbitrary")),
    )(rel_bias, buckets)


def _bias_grad(dbias_list, buckets, name):
    nbr = len(dbias_list)

    def body(*refs):
        d_refs, bk_ref, o_ref = refs[:nbr], refs[nbr], refs[nbr + 1]
        lane = lax.broadcasted_iota(jnp.int32, (1, LANES), 1)
        for h in range(N_HEADS):
            def step(bkt, acc):
                tot = jnp.zeros((1, 1), F32)
                for bi in range(nbr):
                    sel = jnp.where(bk_ref[bi] == bkt, d_refs[bi][h], 0.0)
                    tot = tot + jnp.sum(jnp.sum(sel, axis=1, keepdims=True), axis=0, keepdims=True)
                return acc + jnp.where(lane == bkt, tot, 0.0)

            o_ref[h:h + 1, :] = lax.fori_loop(0, N_BUCKETS, step, jnp.zeros((1, LANES), F32))

    band = pl.BlockSpec((N_HEADS, 2 * BLK, BLK), lambda i: (0, 0, 0))
    return pl.pallas_call(
        body, name=name, out_shape=jax.ShapeDtypeStruct((N_HEADS, LANES), F32), grid=(1,),
        in_specs=[band] * nbr + [pl.BlockSpec((nbr, 2 * BLK, BLK), lambda i: (0, 0, 0))],
        out_specs=pl.BlockSpec((N_HEADS, LANES), lambda i: (0, 0)),
        compiler_params=_cparams(("arbitrary",)),
    )(*dbias_list, buckets)


MLA_TQ = 256
MLA_TK = 256


LOG2E = math.log2(math.e)
MLA_C = MLA_SCALE * LOG2E


def _key_le_query(tk, tq):
    return lax.broadcasted_iota(jnp.int32, (tk, tq), 0) <= lax.broadcasted_iota(jnp.int32, (tk, tq), 1)


def _row_mask(shape, hh):
    row = lax.broadcasted_iota(jnp.int32, shape, 0)
    return (row >= hh * HEAD_DIM) & (row < (hh + 1) * HEAD_DIM)


def _host_call(body, comm, *, name, grid, in_specs, out_specs, out_shape, scratch_shapes, args):
    sem = ("arbitrary",) * len(grid)
    if comm is None:
        res = pl.pallas_call(body, name=name, grid=grid, in_specs=in_specs, out_specs=out_specs,
                             out_shape=out_shape, scratch_shapes=scratch_shapes,
                             compiler_params=_cparams(sem))(*args)
        return res, []
    n_in, n_out, n_s, cn = len(in_specs), len(out_specs), len(scratch_shapes), comm.n

    def hosted(*refs):
        ins, refs = refs[:n_in], refs[n_in:]
        c_ins, refs = refs[:cn], refs[cn:]
        outs, refs = refs[:n_out], refs[n_out:]
        c_outs, refs = refs[:cn], refs[cn:]
        scr, c_sems = refs[:n_s], refs[n_s:]
        ids = [pl.program_id(a) for a in range(len(grid))]
        first = functools.reduce(jnp.logical_and, [i == 0 for i in ids])
        last = functools.reduce(jnp.logical_and, [i == g - 1 for i, g in zip(ids, grid)])

        @pl.when(first)
        def _():
            comm.start(c_ins, c_outs, c_sems)

        body(*ins, *outs, *scr)

        @pl.when(last)
        def _():
            comm.finish(c_ins, c_outs, c_sems)

    res = pl.pallas_call(
        hosted, name=name, grid=grid, in_specs=list(in_specs) + _hbm_specs(cn),
        out_specs=list(out_specs) + _hbm_specs(cn), out_shape=list(out_shape) + list(comm.out_shape),
        scratch_shapes=list(scratch_shapes) + list(comm.scratch), compiler_params=_cparams(sem),
    )(*args, *comm.inputs)
    return res[:n_out], res[n_out:]


def _mla_fwd_t(q, k, vt, name, comm=None):
    B, S, _ = q.shape
    tq, tk = MLA_TQ, MLA_TK
    assert tq == tk
    npair = N_HEADS // 2
    nq = S // tq

    def body(q_ref, k_ref, vt_ref, o_ref, lse_ref, s_scr, e_scr, acc_scr, m_scr, a_scr):
        i = pl.program_id(1)
        diag = _key_le_query(tk, tq)
        m_scr[...] = jnp.full_like(m_scr, NEG)
        acc_scr[...] = jnp.zeros_like(acc_scr)

        def step(j, masked):
            rows = pl.ds(pl.multiple_of(j * tk, tk), tk)
            for h in range(N_HEADS):
                hsl = slice(h * LANES, (h + 1) * LANES)
                s_scr[h] = _dot_nt(k_ref[0, rows, hsl], q_ref[0, :, hsl])
            for h in range(N_HEADS):
                s = s_scr[h]
                if masked:
                    s = jnp.where(diag, s, NEG)
                m_old = m_scr[h:h + 1, :]
                m_new = jnp.maximum(m_old, jnp.max(s, axis=0, keepdims=True))
                a_scr[h:h + 1, :] = jnp.exp2((m_old - m_new) * MLA_C)
                e_scr[h] = jnp.exp2((s - m_new) * MLA_C).astype(BF16)
                m_scr[h:h + 1, :] = m_new
            for h in range(N_HEADS):
                vj = vt_ref[0, h // 2, j]
                vh = jnp.where(_row_mask(vj.shape, h % 2), vj, jnp.ones_like(vj))
                acc_scr[h] = acc_scr[h] * a_scr[h:h + 1, :] + _dot_nn(vh, e_scr[h])

        def loop_body(j, carry):
            step(j, False)
            return carry

        lax.fori_loop(0, i, loop_body, 0)
        step(i, True)
        rows0 = _row_mask((LANES, tq), 0)
        for p in range(npair):
            l0 = acc_scr[2 * p, HEAD_DIM:HEAD_DIM + 1, :]
            l1 = acc_scr[2 * p + 1, 0:1, :]
            o_ref[0, p * LANES:(p + 1) * LANES, :] = jnp.where(rows0, acc_scr[2 * p] / l0, acc_scr[2 * p + 1] / l1)
            lse_ref[0, p, 0] = jnp.zeros((8, tq), F32)
            lse_ref[0, p, 0, 0:1, :] = m_scr[2 * p:2 * p + 1, :] * MLA_C + jnp.log(l0) * LOG2E
            lse_ref[0, p, 0, 1:2, :] = m_scr[2 * p + 1:2 * p + 2, :] * MLA_C + jnp.log(l1) * LOG2E

    return _host_call(
        body, comm, name=name,
        out_shape=[jax.ShapeDtypeStruct((B, D_B, S), F32), jax.ShapeDtypeStruct((B, npair, nq, 8, tq), F32)],
        grid=(B, nq),
        in_specs=[pl.BlockSpec((1, tq, N_HEADS * LANES), lambda b, i: (b, i, 0)),
                  pl.BlockSpec((1, S, N_HEADS * LANES), lambda b, i: (b, 0, 0)),
                  pl.BlockSpec((1, npair, S // tk, LANES, tk), lambda b, i: (b, 0, 0, 0, 0))],
        out_specs=[pl.BlockSpec((1, D_B, tq), lambda b, i: (b, 0, i)),
                   pl.BlockSpec((1, npair, 1, 8, tq), lambda b, i: (b, 0, i, 0, 0))],
        scratch_shapes=[pltpu.VMEM((N_HEADS, tk, tq), F32), pltpu.VMEM((N_HEADS, tk, tq), BF16),
                        pltpu.VMEM((N_HEADS, LANES, tq), F32), pltpu.VMEM((N_HEADS, tq), F32),
                        pltpu.VMEM((N_HEADS, tq), F32)],
        args=(q, k, vt))


def _mla_delta(do, o, name):
    B, S, _ = o.shape
    tq = MLA_TQ
    npair = N_HEADS // 2

    def body(do_ref, o_ref, d_ref):
        d_ref[...] = jnp.zeros_like(d_ref)
        for p in range(npair):
            sl = slice(p * LANES, (p + 1) * LANES)
            prod_t = jnp.transpose(do_ref[0, :, sl].astype(F32) * o_ref[0, :, sl])
            d_ref[0, p, 0, 0:1, :] = jnp.sum(prod_t[:HEAD_DIM], axis=0, keepdims=True)
            d_ref[0, p, 0, 1:2, :] = jnp.sum(prod_t[HEAD_DIM:], axis=0, keepdims=True)

    tok = pl.BlockSpec((1, tq, D_B), lambda b, i: (b, i, 0))
    return pl.pallas_call(
        body, name=name, out_shape=jax.ShapeDtypeStruct((B, npair, S // tq, 8, tq), F32),
        grid=(B, S // tq), in_specs=[tok, tok],
        out_specs=pl.BlockSpec((1, npair, 1, 8, tq), lambda b, i: (b, 0, i, 0, 0)),
        compiler_params=_cparams(("parallel", "parallel")),
    )(do, o)


def _mla_bwd_t(q, k, v, do, lse, delta, name, comm=None):
    B, S, _ = q.shape
    tq, tk = MLA_TQ, MLA_TK
    assert tq == tk
    npair = N_HEADS // 2
    nq = S // tq

    hg = 4
    pg = hg // 2
    ngroup = N_HEADS // hg

    def body(q_ref, do_ref, lse_ref, dl_ref, k_ref, v_ref, dk_ref, dv_ref, dq_ref,
             s_scr, dp_scr, p_scr, ds_scr, dk_s, dv_s, kt_s):
        j = pl.program_id(2)

        @pl.when(j == 0)
        def _():
            dq_ref[...] = jnp.zeros_like(dq_ref)

        dk_s[...] = jnp.zeros_like(dk_s)
        dv_s[...] = jnp.zeros_like(dv_s)
        diag = _key_le_query(tk, tq)
        hsl = lambda h: slice(h * LANES, (h + 1) * LANES)
        for h in range(hg):
            kt_s[h] = jnp.transpose(k_ref[0, :, hsl(h)].astype(F32)).astype(BF16)

        def step(i, masked):
            rows = pl.ds(pl.multiple_of(i * tq, tq), tq)

            def dom(h):
                dov = do_ref[0, rows, hsl(h // 2)]
                return jnp.where(_head_mask((tq, LANES), h % 2), dov, jnp.zeros_like(dov))

            for h in range(hg):
                s_scr[h] = _dot_nt(k_ref[0, :, hsl(h)], q_ref[0, rows, hsl(h)])
                dp_scr[h] = _dot_nt(v_ref[0, :, hsl(h // 2)], dom(h))
            for h in range(hg):
                pr = jnp.exp2(s_scr[h] * MLA_C - lse_ref[0, h // 2, i, h % 2:h % 2 + 1, :])
                if masked:
                    pr = jnp.where(diag, pr, 0.0)
                p_scr[h] = pr.astype(BF16)
                ds_scr[h] = (pr * (dp_scr[h] - dl_ref[0, h // 2, i, h % 2:h % 2 + 1, :])).astype(BF16)
            for h in range(hg):
                dv_s[h // 2] += _dot_nn(p_scr[h], dom(h))
                dk_s[h] += _dot_nn(ds_scr[h], q_ref[0, rows, hsl(h)])
                dq_ref[0, h // 2, i, hsl(h % 2), :] += _dot_nn(kt_s[h], ds_scr[h]) * MLA_SCALE

        step(j, True)

        def loop_body(i, carry):
            step(i, False)
            return carry

        lax.fori_loop(j + 1, nq, loop_body, 0)
        for h in range(hg):
            dk_ref[0, :, hsl(h)] = dk_s[h] * MLA_SCALE
        for p in range(pg):
            dv_ref[0, :, hsl(p)] = dv_s[p]

    stat = pl.BlockSpec((1, pg, nq, 8, tq), lambda b, g, j: (b, g, 0, 0, 0))
    return _host_call(
        body, comm, name=name,
        out_shape=[jax.ShapeDtypeStruct((B, S, N_HEADS * LANES), F32), jax.ShapeDtypeStruct((B, S, D_B), F32),
                   jax.ShapeDtypeStruct((B, npair, nq, 2 * LANES, tq), F32)],
        grid=(B, ngroup, S // tk),
        in_specs=[pl.BlockSpec((1, S, hg * LANES), lambda b, g, j: (b, 0, g)),
                  pl.BlockSpec((1, S, pg * LANES), lambda b, g, j: (b, 0, g)),
                  stat, stat,
                  pl.BlockSpec((1, tk, hg * LANES), lambda b, g, j: (b, j, g)),
                  pl.BlockSpec((1, tk, pg * LANES), lambda b, g, j: (b, j, g))],
        out_specs=[pl.BlockSpec((1, tk, hg * LANES), lambda b, g, j: (b, j, g)),
                   pl.BlockSpec((1, tk, pg * LANES), lambda b, g, j: (b, j, g)),
                   pl.BlockSpec((1, pg, nq, 2 * LANES, tq), lambda b, g, j: (b, g, 0, 0, 0))],
        scratch_shapes=[pltpu.VMEM((hg, tk, tq), F32), pltpu.VMEM((hg, tk, tq), F32),
                        pltpu.VMEM((hg, tk, tq), BF16), pltpu.VMEM((hg, tk, tq), BF16),
                        pltpu.VMEM((hg, tk, LANES), F32), pltpu.VMEM((pg, tk, LANES), F32),
                        pltpu.VMEM((hg, LANES, tk), BF16)],
        args=(q, do, lse, delta, k, v))


def _local_step(x, target, mod, wts, gains, rel_bias, ffn_shards=None):
    B, S, D = x.shape
    T = B * S
    sh1, sc1, g1, sh2, sc2, g2 = [mod[:, i * D:(i + 1) * D].reshape(B, 1, D) for i in range(N_MOD)]
    cs, sn = _rope_tables()
    buckets = np.stack([_band_buckets(d) for d in DILATIONS])
    buckets_dev = jnp.asarray(buckets)
    bias = _bias_tables(rel_bias, buckets_dev, "rel_bias_tables")
    w_in = wts["w_in"]

    h1 = _adaln_fwd(x, gains["g_norm1"], sc1, sh1, "adaln1_fwd")
    h1f = h1.reshape(T, D)
    qkv_v = _mm_qkv_views(h1, w_in[:, :P_QKV], "mm_qkv")
    rest = _mm(h1f, w_in[:, P_QKV:], "nn", F32, "mm_rest")
    o_d, lse_d = [], []
    w_ffn_out_got = None
    for i, d in enumerate(DILATIONS):
        comm = _GatherComm(ffn_shards[1:]) if (ffn_shards and i == 0) else None
        (o_i, lse_i), got = _dil_fwd(qkv_v[i], bias, i, d, f"dil_fwd_{d}", comm)
        if comm is not None:
            w_ffn_out_got = got[0]
        o_d.append(o_i)
        lse_d.append(lse_i)
    out_a_v, lse_a_v = _dil_merge(o_d, lse_d, "dil_merge")
    out_a = out_a_v[0]
    cqn = _rms_fwd(rest, 1, Q_LORA, gains["g_cq"], "rms_cq_fwd")
    ckvn = _rms_fwd(rest, 0, KV_LORA, gains["g_ckv"], "rms_ckv_fwd")
    rest3 = rest.reshape(B, S, P_REST)
    q_raw = _mm(cqn, wts["w_uq"], "nn", F32, "mm_uq").reshape(B, S, N_HEADS * LANES)
    qc = _rope_apply(q_raw, cs, sn, BF16, "rope_q")
    kn_raw = _mm(ckvn, wts["w_kv"][:, :N_HEADS * LANES], "nn", F32, "mm_uk").reshape(B, S, N_HEADS * LANES)
    kc = _rope_apply(kn_raw, cs, sn, BF16, "rope_k", add=rest3, add_blk=KV_LORA // LANES)
    v = _mm(ckvn, wts["w_kv"][:, N_HEADS * LANES:], "nn", BF16, "mm_uv").reshape(B, S, D_B)
    vt = jnp.transpose(v.reshape(B, S // MLA_TK, MLA_TK, N_HEADS // 2, LANES), (0, 3, 1, 4, 2))
    (o_t, lse_b), got = _mla_fwd_t(qc, kc, vt, "mla_fwd", _GatherComm(ffn_shards[:1]) if ffn_shards else None)
    if ffn_shards:
        wts = dict(wts, w_ffn_in=got[0].reshape(N_CHIP, D, -1), w_ffn_out=w_ffn_out_got.reshape(D_FF, D))
    out_b = jnp.transpose(o_t, (0, 2, 1))
    out_af, out_bf = out_a.reshape(T, D_A), out_b.reshape(T, D_B)
    ya = _rms_fwd(out_af, 0, D_A, gains["g_out_a"], "rms_outa_fwd")
    yb = _rms_fwd(out_bf, 0, D_B, gains["g_out_b"], "rms_outb_fwd")
    y = jnp.concatenate([ya, yb], axis=1)
    mix = _mm(y, wts["w_out"], "nn", F32, "mm_out").reshape(B, S, D)
    h2, x1 = _adaln_fwd(x, gains["g_norm2"], sc2, sh2, "adaln2_fwd", mix=mix, gate=g1)
    h2f = h2.reshape(T, D)
    gu = _mm(h2f, wts["w_ffn_in"], "nn", F32, "mm_ffn_in", col_blocks=N_CHIP)
    act = _swiglu_fwd(gu, "swiglu_fwd")
    f = _mm(act, wts["w_ffn_out"], "nn", F32, "mm_ffn_out").reshape(B, S, D)
    dx2, df, dg2, dg_final, loss = _final_loss(x1, f, g2, gains["g_final"], target, "final_loss")

    dff = df.reshape(T, D)
    da = _mm(dff, wts["w_ffn_out"], "nt", F32, "mm_ffn_out_dx")
    gw_ffn_out = _mm(act, dff, "tn", F32, "mm_ffn_out_dw")
    dgu = _swiglu_bwd(da, gu, "swiglu_bwd")
    dh2 = _mm(dgu, wts["w_ffn_in"], "nt", F32, "mm_ffn_in_dx", col_blocks=N_CHIP).reshape(B, S, D)
    gw_ffn_in = _mm(h2f, dgu, "tn", F32, "mm_ffn_in_dw", col_blocks=N_CHIP)
    dx1, dsh2, dsc2, dg_norm2, dg1, dmix = _adaln_bwd(dh2, x1, gains["g_norm2"], sc2, dx2, "adaln2_bwd",
                                                      mix=mix, gate=g1)
    dmixf = dmix.reshape(T, D)
    dy = _mm(dmixf, wts["w_out"], "nt", F32, "mm_out_dx")
    gw_out = _mm(y, dmixf, "tn", F32, "mm_out_dw")
    do_a_v, dg_out_a = _rms_bwd_views(dy, 0, out_a, gains["g_out_a"], "rms_outa_bwd")
    do_b, dg_out_b = _rms_bwd(dy, 1, out_bf, 0, D_B, gains["g_out_b"], "rms_outb_bwd")
    do_b3 = do_b.reshape(B, S, D_B)
    delta_b = _mla_delta(do_b3, out_b, "mla_delta")
    ffn_a4 = None
    if ffn_shards:
        ffn_a4 = _rs_first([gw_ffn_in.reshape(N_DEV, -1, gw_ffn_in.shape[-1]), gw_ffn_out.reshape(N_DEV, -1, D)],
                           "ffn")
    (dkc, dv, dq_t), ffn_r2 = _mla_bwd_t(qc, kc, v, do_b3, lse_b, delta_b, "mla_bwd",
                                         _ToChipsComm(ffn_a4[:1]) if ffn_shards else None)
    dqc = jnp.transpose(dq_t, (0, 2, 4, 1, 3)).reshape(B, S, N_HEADS * LANES)
    dq_raw = _rope_apply(dqc, cs, -sn, BF16, "rope_q_bwd").reshape(T, N_HEADS * LANES)
    dkrw = _krope_bwd(dkc, cs, -sn, "rope_k_bwd").reshape(T, LANES)
    dcqn = _mm(dq_raw, wts["w_uq"], "nt", F32, "mm_uq_dx")
    gw_uq = _mm(cqn, dq_raw, "tn", F32, "mm_uq_dw")
    dkv = jnp.concatenate([dkc.reshape(T, -1), dv.reshape(T, -1)], axis=1).astype(BF16)
    dckvn = _mm(dkv, wts["w_kv"], "nt", F32, "mm_ukv_dx")
    gw_kv = _mm(ckvn, dkv, "tn", F32, "mm_ukv_dw")
    dcq, dg_cq = _rms_bwd(dcqn, 0, rest, 1, Q_LORA, gains["g_cq"], "rms_cq_bwd")
    dckv, dg_ckv = _rms_bwd(dckvn, 0, rest, 0, KV_LORA, gains["g_ckv"], "rms_ckv_bwd")
    dqkv_d, dbias_d = [], []
    for i, d in enumerate(DILATIONS):
        comm = _ToChipsComm(ffn_a4[1:]) if (ffn_shards and i == 0) else None
        (dqkv_i, dbias_i), got = _dil_bwd(qkv_v[i], do_a_v[i], out_a_v[i], lse_a_v[i], bias, i, d,
                                          f"dil_bwd_{d}", comm)
        if comm is not None:
            ffn_r2 = list(ffn_r2) + list(got)
        dqkv_d.append(dqkv_i)
        dbias_d.append(dbias_i)
    dqkv = _sum_views_bf16(dqkv_d, "dil_bwd_sum").reshape(T, P_QKV)
    g_rel_bias = _bias_grad(dbias_d, buckets_dev, "rel_bias_grad")[:, :N_BUCKETS].T
    dproj = jnp.concatenate([dqkv, dckv, dkrw, dcq], axis=1)
    gw_in = _mm(h1f, dproj, "tn", F32, "mm_in_dw")
    mix_a4 = mix_r2 = None
    if ffn_shards:
        nat = dict(w_in=_w_in_from_kernel(gw_in), w_uq=_w_uq_from_kernel(gw_uq), w_ukv=_w_ukv_from_kernel(gw_kv))
        gp = _pack_shards({n: _shards_from_full(nat[n]) for n, _, _ in _PACKED}, F32)
        mix_a4 = _rs_first([gp.reshape(N_DEV, -1, D), gw_out.reshape(N_DEV, -1, D)], "mix")
        dh1, mix_r2 = _mm(dproj, w_in, "nt", F32, "mm_in_dx", comm=_ToChipsComm(mix_a4))
    else:
        dh1 = _mm(dproj, w_in, "nt", F32, "mm_in_dx")
    dh1 = dh1.reshape(B, S, D)
    grad_x, dsh1, dsc1, dg_norm1 = _adaln_bwd(dh1, x, gains["g_norm1"], sc1, dx1, "adaln1_bwd")
    gmod = jnp.concatenate([dsh1, dsc1, dg1, dsh2, dsc2, dg2], axis=-1).reshape(B, N_MOD * D)
    grads = dict(w_in=gw_in, w_uq=gw_uq, w_kv=gw_kv, w_out=gw_out, w_ffn_in=gw_ffn_in, w_ffn_out=gw_ffn_out,
                 g_norm1=dg_norm1, g_cq=dg_cq, g_ckv=dg_ckv, rel_bias=g_rel_bias, g_out_a=dg_out_a,
                 g_out_b=dg_out_b, g_norm2=dg_norm2, g_final=dg_final, ffn_pending=(ffn_a4, ffn_r2),
                 mix_pending=(mix_a4, mix_r2))
    return loss, grad_x, gmod, grads


def _w_in_to_kernel(w):
    z = lambda n: jnp.zeros((w.shape[0], n), w.dtype)
    i3, i4, i5 = 3 * D_A, 3 * D_A + Q_LORA, 3 * D_A + Q_LORA + KV_LORA
    return jnp.concatenate([w[:, :i3], w[:, i4:i5], z(NOPE_DIM), w[:, i5:], z(LANES - NOPE_DIM - ROPE_DIM),
                            w[:, i3:i4]], axis=1)


def _w_in_from_kernel(g):
    o = P_QKV + KV_LORA
    return jnp.concatenate([g[:, :P_QKV], g[:, o + LANES:], g[:, P_QKV:o],
                            g[:, o + NOPE_DIM:o + NOPE_DIM + ROPE_DIM]], axis=1)


def _w_uq_to_kernel(w):
    w3 = w.reshape(Q_LORA, N_HEADS, NOPE_DIM + ROPE_DIM)
    return jnp.pad(w3, ((0, 0), (0, 0), (0, LANES - NOPE_DIM - ROPE_DIM))).reshape(Q_LORA, N_HEADS * LANES)


def _w_uq_from_kernel(g):
    return g.reshape(Q_LORA, N_HEADS, LANES)[:, :, :NOPE_DIM + ROPE_DIM].reshape(Q_LORA, -1)


def _w_ukv_to_kernel(w):
    w3 = w.reshape(KV_LORA, N_HEADS, 2 * HEAD_DIM)
    wk = jnp.pad(w3[:, :, :NOPE_DIM], ((0, 0), (0, 0), (0, LANES - NOPE_DIM))).reshape(KV_LORA, N_HEADS * LANES)
    wv = w3[:, :, NOPE_DIM:].reshape(KV_LORA, D_B)
    return jnp.concatenate([wk, wv], axis=1)


def _w_ukv_from_kernel(g):
    gk = g[:, :N_HEADS * LANES].reshape(KV_LORA, N_HEADS, LANES)[:, :, :NOPE_DIM]
    gv = g[:, N_HEADS * LANES:].reshape(KV_LORA, N_HEADS, HEAD_DIM)
    return jnp.concatenate([gk, gv], axis=2).reshape(KV_LORA, -1)


MESH = pl.DeviceIdType.MESH


def _my_place():
    return lax.axis_index("x"), lax.axis_index("y"), lax.axis_index("c")


def _other_chips(x, y):
    return [(1 - x, y), (x, 1 - y), (1 - x, 1 - y)]


def _allgather8(x_shard, name, in_hbm):
    m_per, n = x_shard.shape
    space = pl.ANY if in_hbm else pltpu.VMEM

    def body(x_ref, out_ref, send_sems, recv_sems, local_sem):
        x, y, c = _my_place()
        me, sibling = (x, y, c), (x, y, 1 - c)
        chips = _other_chips(x, y)

        def rows(px, py, pc):
            return out_ref.at[pl.ds((4 * px + 2 * py + pc) * m_per, m_per), :]

        def copy(k, block, to, src=None):
            return pltpu.make_async_remote_copy(
                src_ref=rows(*block) if src is None else src, dst_ref=rows(*block),
                send_sem=send_sems.at[k], recv_sem=recv_sems.at[k], device_id=to, device_id_type=MESH)

        mine = pltpu.make_async_copy(x_ref, rows(*me), local_sem)
        mine.start()
        first = [copy(0, me, sibling, src=x_ref)]
        first += [copy(1 + j, me, (*chip, c), src=x_ref) for j, chip in enumerate(chips)]
        for cp in first:
            cp.start()
        passed = [copy(4 + j, (*chip, c), sibling) for j, chip in enumerate(chips)]
        for j, chip in enumerate(chips):
            copy(1 + j, (*chip, c), me).wait_recv()
            passed[j].start()
        copy(0, sibling, me).wait_recv()
        for j, chip in enumerate(chips):
            copy(4 + j, (*chip, 1 - c), me).wait_recv()
        for cp in first + passed:
            cp.wait_send()
        mine.wait()

    return pl.pallas_call(
        body, name=name,
        out_shape=jax.ShapeDtypeStruct((N_DEV * m_per, n), x_shard.dtype),
        in_specs=[pl.BlockSpec(memory_space=space)],
        out_specs=pl.BlockSpec(memory_space=space),
        scratch_shapes=[pltpu.SemaphoreType.DMA((7,)), pltpu.SemaphoreType.DMA((7,)), pltpu.SemaphoreType.DMA],
        compiler_params=pltpu.CompilerParams(vmem_limit_bytes=VMEM_LIMIT),
    )(x_shard)


def _hbm_specs(n):
    return [pl.BlockSpec(memory_space=pl.ANY)] * n


class _GatherComm:
    def __init__(self, shards):
        self.n = n = len(shards)
        self.inputs = [s.reshape(2, s.shape[0] // 2, s.shape[1]) for s in shards]
        self.out_shape = [jax.ShapeDtypeStruct((N_DEV,) + s.shape[1:], s.dtype) for s in self.inputs]
        self.scratch = [pltpu.SemaphoreType.DMA((7 * n,)), pltpu.SemaphoreType.DMA((7 * n,))]

    def _parts(self, xs, outs, sems):
        send_sems, recv_sems = sems
        x, y, c = _my_place()

        def blk(k, px, py, pc):
            return outs[k].at[4 * px + 2 * py + pc]

        def copy(k, kind, block, to, own=False):
            return pltpu.make_async_remote_copy(
                src_ref=xs[k].at[c] if own else blk(k, *block), dst_ref=blk(k, *block),
                send_sem=send_sems.at[7 * k + kind], recv_sem=recv_sems.at[7 * k + kind],
                device_id=to, device_id_type=MESH)

        def whole(k):
            return pltpu.make_async_remote_copy(
                src_ref=xs[k], dst_ref=outs[k].at[pl.ds(4 * x + 2 * y, 2)],
                send_sem=send_sems.at[7 * k], recv_sem=recv_sems.at[7 * k],
                device_id=(x, y, 1 - c), device_id_type=MESH)

        me, sibling = (x, y, c), (x, y, 1 - c)
        chips = _other_chips(x, y)
        first = []
        for k in range(self.n):
            first.append(whole(k))
            first += [copy(k, 1 + j, me, (*chip, c), own=True) for j, chip in enumerate(chips)]
        return copy, whole, me, sibling, chips, c, first

    def start(self, xs, outs, sems):
        for cp in self._parts(xs, outs, sems)[-1]:
            cp.start()

    def finish(self, xs, outs, sems):
        copy, whole, me, sibling, chips, c, first = self._parts(xs, outs, sems)
        passed = []
        for j, chip in enumerate(chips):
            for k in range(self.n):
                copy(k, 1 + j, (*chip, c), me).wait_recv()
                fwd = copy(k, 4 + j, (*chip, c), sibling)
                fwd.start()
                passed.append(fwd)
        for k in range(self.n):
            whole(k).wait_recv()
        for j, chip in enumerate(chips):
            for k in range(self.n):
                copy(k, 4 + j, (*chip, 1 - c), me).wait_recv()
        for cp in first + passed:
            cp.wait_send()


class _ToChipsComm:
    def __init__(self, a4s):
        self.inputs = list(a4s)
        self.n = n = len(a4s)
        nc = N_CHIP - 1
        self.out_shape = [jax.ShapeDtypeStruct((nc,) + a.shape[1:], a.dtype) for a in a4s]
        self.scratch = [pltpu.SemaphoreType.DMA((nc * n,)), pltpu.SemaphoreType.DMA((nc * n,))]

    def _copies(self, as_, rs, sems):
        send_sems, recv_sems = sems
        x, y, c = _my_place()
        nc = N_CHIP - 1
        return [pltpu.make_async_remote_copy(
            src_ref=as_[k].at[2 * cx + cy], dst_ref=rs[k].at[j], send_sem=send_sems.at[nc * k + j],
            recv_sem=recv_sems.at[nc * k + j], device_id=(cx, cy, c), device_id_type=MESH)
            for k in range(self.n) for j, (cx, cy) in enumerate(_other_chips(x, y))]

    def start(self, as_, rs, sems):
        for cp in self._copies(as_, rs, sems):
            cp.start()

    def finish(self, as_, rs, sems):
        for cp in self._copies(as_, rs, sems):
            cp.wait()


def _run_comm(comm, name):
    n = comm.n

    def body(*refs):
        ins, outs, sems = refs[:n], refs[n:2 * n], refs[2 * n:]
        comm.start(ins, outs, sems)
        comm.finish(ins, outs, sems)

    return pl.pallas_call(
        body, name=name, out_shape=comm.out_shape, in_specs=_hbm_specs(n), out_specs=_hbm_specs(n),
        scratch_shapes=comm.scratch,
    )(*comm.inputs)


def _gather_weights(shards, name):
    return _run_comm(_GatherComm(shards), name)


def _rs_to_sibling(g8s, name):
    n = len(g8s)

    def body(*refs):
        gs, rs = refs[:n], refs[n:2 * n]
        send_sems, recv_sems = refs[2 * n:]
        x, y, c = _my_place()
        copies = [pltpu.make_async_remote_copy(
            src_ref=gs[k].at[2 * s + 1 - c], dst_ref=rs[k].at[s], send_sem=send_sems.at[N_CHIP * k + s],
            recv_sem=recv_sems.at[N_CHIP * k + s], device_id=(x, y, 1 - c), device_id_type=MESH)
            for k in range(n) for s in range(N_CHIP)]
        for cp in copies:
            cp.start()
        for cp in copies:
            cp.wait()

    return pl.pallas_call(
        body, name=name,
        out_shape=[jax.ShapeDtypeStruct((N_CHIP,) + g.shape[1:], g.dtype) for g in g8s],
        in_specs=_hbm_specs(n), out_specs=_hbm_specs(n),
        scratch_shapes=[pltpu.SemaphoreType.DMA((N_CHIP * n,)), pltpu.SemaphoreType.DMA((N_CHIP * n,))],
    )(*g8s)


def _rs_to_chips(a4s, name):
    return _run_comm(_ToChipsComm(a4s), name)


def _swap_halves(hs, name):
    n = len(hs)

    def body(*refs):
        o_refs = refs[n:2 * n]
        send_sems, recv_sems = refs[2 * n:]
        x, y, c = _my_place()

        def remote(k, slot):
            return pltpu.make_async_remote_copy(
                src_ref=o_refs[k].at[slot], dst_ref=o_refs[k].at[slot], send_sem=send_sems.at[k],
                recv_sem=recv_sems.at[k], device_id=(x, y, 1 - c), device_id_type=MESH)

        sends = [remote(k, c) for k in range(n)]
        for cp in sends:
            cp.start()
        for k in range(n):
            remote(k, 1 - c).wait_recv()
        for cp in sends:
            cp.wait_send()

    return pl.pallas_call(
        body, name=name,
        out_shape=[jax.ShapeDtypeStruct(h.shape, h.dtype) for h in hs],
        in_specs=_hbm_specs(n), out_specs=_hbm_specs(n),
        input_output_aliases={k: k for k in range(n)},
        scratch_shapes=[pltpu.SemaphoreType.DMA((n,)), pltpu.SemaphoreType.DMA((n,))],
    )(*hs)


ADD_TILES = 4


def _add_blocks(a_list, a_idx_fn, others_list, ns, sel, name, out_blocks=None, out_idx_fn=None):
    out_blocks = out_blocks or ns
    out_idx_fn = out_idx_fn or (lambda s, sel_ref: s)
    n = len(a_list)
    n_o = len(others_list[0])
    per = 1 + n_o

    def body(sel_ref, *refs):
        for k in range(n):
            ins = refs[k * per:(k + 1) * per]
            o_ref = refs[n * per + k]
            acc = ins[0][0]
            for r in ins[1:]:
                acc = acc + r[0]
            o_ref[0] = acc

    in_specs, args, out_specs, out_shape = [], [], [], []
    for a, others in zip(a_list, others_list):
        _, R, N = a.shape
        tr = R // ADD_TILES
        assert tr % 8 == 0, a.shape
        in_specs.append(pl.BlockSpec((1, tr, N), lambda s, i, sel_ref: (a_idx_fn(s, sel_ref), i, 0)))
        args.append(a)
        for arr, fixed in others:
            if fixed is None:
                in_specs.append(pl.BlockSpec((1, tr, N), lambda s, i, sel_ref: (s, i, 0)))
            else:
                in_specs.append(pl.BlockSpec((1, tr, N), lambda s, i, sel_ref, fixed=fixed: (fixed, i, 0)))
            args.append(arr)
        out_specs.append(pl.BlockSpec((1, tr, N), lambda s, i, sel_ref: (out_idx_fn(s, sel_ref), i, 0)))
        out_shape.append(jax.ShapeDtypeStruct((out_blocks, R, N), a.dtype))
    grid_spec = pltpu.PrefetchScalarGridSpec(num_scalar_prefetch=1, grid=(ns, ADD_TILES), in_specs=in_specs,
                                             out_specs=out_specs)
    return pl.pallas_call(
        body, name=name, out_shape=out_shape, grid_spec=grid_spec,
        compiler_params=_cparams(("parallel", "parallel")),
    )(sel, *args)


def _rs_first(g8s, tag):
    c_sel = jnp.reshape(lax.axis_index("c"), (1,)).astype(jnp.int32)
    r1 = _rs_to_sibling(g8s, f"rs_to_sibling_{tag}")
    return _add_blocks(g8s, lambda s, sel: 2 * s + sel[0], [[(r, None)] for r in r1], N_CHIP, c_sel,
                       f"rs_add_sibling_{tag}")


def _rs_last(a4s, r2s, tag):
    sel = jnp.stack([2 * lax.axis_index("x") + lax.axis_index("y"), lax.axis_index("c")]).astype(jnp.int32)
    h = _add_blocks(a4s, lambda s, sel: sel[0], [[(r, 0), (r, 1), (r, 2)] for r in r2s], 1, sel,
                    f"rs_add_chips_{tag}", out_blocks=2, out_idx_fn=lambda s, sel: sel[1])
    full = _swap_halves(h, f"rs_swap_halves_{tag}")
    return [f.reshape(2 * f.shape[1], f.shape[2]) for f in full]


def _reduce_scatter(g8s, tag):
    a4 = _rs_first(g8s, tag)
    return _rs_last(a4, _rs_to_chips(a4, f"rs_to_chips_{tag}"), tag)


def _ada_fwd(c_all, w_ada, b_ada, name):
    nb, D = c_all.shape
    ncol = w_ada.shape[1]
    tc = 512

    def body(c_ref, w_ref, b_ref, o_ref):
        cv = c_ref[...]
        cond = (cv * jax.nn.sigmoid(cv)).astype(BF16)
        o_ref[...] = jnp.dot(cond, w_ref[...].astype(BF16), preferred_element_type=F32) + b_ref[...]

    return pl.pallas_call(
        body, name=name, out_shape=jax.ShapeDtypeStruct((nb, ncol), F32), grid=(ncol // tc,),
        in_specs=[pl.BlockSpec((nb, D), lambda j: (0, 0)), pl.BlockSpec((D, tc), lambda j: (0, j)),
                  pl.BlockSpec((1, tc), lambda j: (0, j))],
        out_specs=pl.BlockSpec((nb, tc), lambda j: (0, j)),
        compiler_params=_cparams(("parallel",)),
    )(c_all, w_ada, b_ada)


def _ada_bwd(c_all, gmod_cols, name):
    nb, D = c_all.shape
    ncol = gmod_cols.shape[1]
    tc = 512

    def body(c_ref, g_ref, o_ref):
        cv = c_ref[...]
        cond = (cv * jax.nn.sigmoid(cv)).astype(BF16)
        o_ref[...] = _dot_tn(cond, g_ref[...].astype(BF16))

    return pl.pallas_call(
        body, name=name, out_shape=jax.ShapeDtypeStruct((D, ncol), F32), grid=(ncol // tc,),
        in_specs=[pl.BlockSpec((nb, D), lambda j: (0, 0)), pl.BlockSpec((nb, tc), lambda j: (0, j))],
        out_specs=pl.BlockSpec((D, tc), lambda j: (0, j)),
        compiler_params=_cparams(("parallel",)),
    )(c_all, gmod_cols)


def _adam_math(w, g, m, v):
    m = ADAM_B1 * m + (1.0 - ADAM_B1) * g
    v = ADAM_B2 * v + (1.0 - ADAM_B2) * (g * g)
    m_hat = m / (1.0 - ADAM_B1 ** ADAM_STEP)
    v_hat = v / (1.0 - ADAM_B2 ** ADAM_STEP)
    delta = -ADAM_LR * (m_hat / (jnp.sqrt(v_hat) + ADAM_EPS) + ADAM_WD * w)
    return delta, m, v


def _adamw(w, g, m, v, name):
    rows, cols = w.shape
    tr = _pick(rows, (256, 192, 176, 128, 64, 8))

    def body(w_ref, g_ref, m_ref, v_ref, d_ref, mo_ref, vo_ref):
        d, mn, vn = _adam_math(w_ref[...], g_ref[...], m_ref[...], v_ref[...])
        d_ref[...] = d
        mo_ref[...] = mn
        vo_ref[...] = vn

    spec = pl.BlockSpec((tr, cols), lambda i: (i, 0))
    return pl.pallas_call(
        body, name=name, out_shape=[jax.ShapeDtypeStruct((rows, cols), F32)] * 3, grid=(rows // tr,),
        in_specs=[spec] * 4, out_specs=[spec] * 3, compiler_params=_cparams(("parallel",)),
    )(w, g, m, v)


VEC_ROWS = 8


def _adamw_rows(w, parts, m, v, name):
    n = w.shape[1]
    P = parts.shape[0]
    assert n % (VEC_ROWS * LANES) == 0, n
    shp = (VEC_ROWS, n // VEC_ROWS)

    def body(w_ref, p_ref, m_ref, v_ref, g_ref, d_ref, mo_ref, vo_ref):
        g = p_ref[0]
        for k in range(1, P):
            g = g + p_ref[k]
        d, mn, vn = _adam_math(w_ref[...], g, m_ref[...], v_ref[...])
        g_ref[...] = g
        d_ref[...] = d
        mo_ref[...] = mn
        vo_ref[...] = vn

    vec = pl.BlockSpec(shp, lambda i: (0, 0))
    out = pl.pallas_call(
        body, name=name, out_shape=[jax.ShapeDtypeStruct(shp, F32)] * 4, grid=(1,),
        in_specs=[vec, pl.BlockSpec((P,) + shp, lambda i: (0, 0, 0)), vec, vec], out_specs=[vec] * 4,
        compiler_params=_cparams(("arbitrary",)),
    )(w.reshape(shp), parts.reshape((P,) + shp), m.reshape(shp), v.reshape(shp))
    return [o.reshape(1, n) for o in out]


_PACKED = (("w_in", 1024, 552), ("w_uq", 384, 192), ("w_ukv", 256, 256))
_SHARDED = ("w_in", "w_uq", "w_ukv", "w_out", "w_ffn_in", "w_ffn_out")
_SMALL = (("g_norm1", 1024), ("g_cq", 384), ("g_ckv", 256), ("rel_bias", 256), ("g_out_a", 512),
          ("g_out_b", 512), ("g_norm2", 1024), ("g_final", 1024))
_SMALL_PAD = 5120
PACK_ROWS = 704
_PACK_ELEMS = PACK_ROWS * D_MODEL


def _pack_shards(shards, dtype):
    lead = shards["w_in"].shape[:-2]
    flat = jnp.concatenate([shards[n].astype(dtype).reshape(lead + (-1,)) for n, _, _ in _PACKED], axis=-1)
    pad = [(0, 0)] * len(lead) + [(0, _PACK_ELEMS - flat.shape[-1])]
    return jnp.pad(flat, pad).reshape(lead + (PACK_ROWS, D_MODEL))


def _unpack_shards(packed):
    out, off = {}, 0
    for n, r, c in _PACKED:
        out[n] = packed[..., off:off + r * c].reshape(packed.shape[:-1] + (r, c))
        off += r * c
    return out


def _full_from_shards(sh):
    return jnp.transpose(sh, (1, 0, 2)).reshape(sh.shape[1], -1)


def _shards_from_full(full):
    rows, cols = full.shape
    return jnp.transpose(full.reshape(rows, N_CHIP, cols // N_CHIP), (1, 0, 2))


def kernel(x, c, w_ada, b_ada, g_norm1, w_in, g_cq, w_uq, g_ckv, w_ukv, rel_bias, g_out_a, g_out_b, w_out, g_norm2, w_ffn_in, w_ffn_out, g_final, loss_target, m_w_ada, m_b_ada, m_g_norm1, m_w_in, m_g_cq, m_w_uq, m_g_ckv, m_w_ukv, m_rel_bias, m_g_out_a, m_g_out_b, m_w_out, m_g_norm2, m_w_ffn_in, m_w_ffn_out, m_g_final, v_w_ada, v_b_ada, v_g_norm1, v_w_in, v_g_cq, v_w_uq, v_g_ckv, v_w_ukv, v_rel_bias, v_g_out_a, v_g_out_b, v_w_out, v_g_norm2, v_w_ffn_in, v_w_ffn_out, v_g_final):
    names = ["w_ada", "b_ada", "g_norm1", "w_in", "g_cq", "w_uq", "g_ckv", "w_ukv", "rel_bias", "g_out_a",
             "g_out_b", "w_out", "g_norm2", "w_ffn_in", "w_ffn_out", "g_final"]
    W = dict(zip(names, [w_ada, b_ada, g_norm1, w_in, g_cq, w_uq, g_ckv, w_ukv, rel_bias, g_out_a, g_out_b,
                         w_out, g_norm2, w_ffn_in, w_ffn_out, g_final]))
    M = dict(zip(names, [m_w_ada, m_b_ada, m_g_norm1, m_w_in, m_g_cq, m_w_uq, m_g_ckv, m_w_ukv, m_rel_bias,
                         m_g_out_a, m_g_out_b, m_w_out, m_g_norm2, m_w_ffn_in, m_w_ffn_out, m_g_final]))
    V = dict(zip(names, [v_w_ada, v_b_ada, v_g_norm1, v_w_in, v_g_cq, v_w_uq, v_g_ckv, v_w_ukv, v_rel_bias,
                         v_g_out_a, v_g_out_b, v_w_out, v_g_norm2, v_w_ffn_in, v_w_ffn_out, v_g_final]))
    B, S, D = x.shape
    mx, my, mc = _my_place()
    dev = 4 * mx + 2 * my + mc
    chip = 2 * mx + my
    pad_rows = 8

    c_all = _allgather8(jnp.pad(c, ((0, pad_rows - B), (0, 0))), "ag_c", False)
    c_all = c_all.reshape(N_DEV, pad_rows, D)[:, :B].reshape(N_DEV * B, D)
    ada_cols = w_ada.shape[-1]
    b_cols = lax.dynamic_slice_in_dim(b_ada, chip * ada_cols, ada_cols, axis=1)
    mod_cols = _ada_fwd(c_all, w_ada[0], b_cols, "ada_fwd")
    mod_all = _allgather8(mod_cols, "ag_mod", False).reshape(N_DEV, N_DEV * B, ada_cols)[0::2]
    mod_all = jnp.transpose(mod_all, (1, 0, 2)).reshape(N_DEV * B, N_MOD * D)
    mod = lax.dynamic_slice_in_dim(mod_all, dev * B, B, axis=0)

    packed = _pack_shards({n: W[n][0] for n, _, _ in _PACKED}, BF16)
    g_packed, g_out = _gather_weights([packed, w_out[0].astype(BF16)], "ag_weights")
    full = {n: _full_from_shards(sh) for n, sh in _unpack_shards(g_packed.reshape(N_CHIP, _PACK_ELEMS)).items()}
    wts = dict(w_in=_w_in_to_kernel(full["w_in"]), w_uq=_w_uq_to_kernel(full["w_uq"]),
               w_kv=_w_ukv_to_kernel(full["w_ukv"]), w_out=g_out.reshape(D, D))
    gains = dict(g_norm1=g_norm1, g_cq=g_cq, g_ckv=g_ckv, g_out_a=g_out_a, g_out_b=g_out_b, g_norm2=g_norm2,
                 g_final=g_final.reshape(1, D))

    loss, grad_x, gmod, grads = _local_step(x, loss_target, mod, wts, gains, rel_bias,
                                            ffn_shards=[w_ffn_in[0].astype(BF16), w_ffn_out[0].astype(BF16)])
    loss = lax.psum(loss[0, 0], ("x", "y", "c"))

    n_small = _SMALL_PAD
    cat = lambda dct: jnp.concatenate([dct[n].reshape(1, -1) for n, _ in _SMALL]
                                      + [jnp.zeros((1, _SMALL_PAD - sum(s for _, s in _SMALL)), F32)], axis=1)
    small = cat(grads)
    rows = jnp.concatenate([gmod, jnp.pad(small, ((0, 0), (0, N_MOD * D - n_small))),
                            jnp.zeros((pad_rows - B - 1, N_MOD * D), F32)], axis=0)
    rows_all = _allgather8(rows, "ag_small", False).reshape(N_DEV, pad_rows, N_MOD * D)
    gmod_all = rows_all[:, :B].reshape(N_DEV * B, N_MOD * D)
    small_parts = rows_all[:, B, :n_small]

    a4, r2 = grads["mix_pending"]
    ffn_a4, ffn_r2 = grads["ffn_pending"]
    r_packed, r_out, r_ffn_in, r_ffn_out = _rs_last(list(a4) + list(ffn_a4), list(r2) + list(ffn_r2), "all")
    G = _unpack_shards(r_packed.reshape(_PACK_ELEMS))
    G.update(w_out=r_out, w_ffn_in=r_ffn_in, w_ffn_out=r_ffn_out)

    gmod_cols = lax.dynamic_slice_in_dim(gmod_all, chip * ada_cols, ada_cols, axis=1)
    G["w_ada"] = _ada_bwd(c_all, gmod_cols, "ada_bwd")
    delta, new_m, new_v = {}, {}, {}
    for n in ("w_ada",) + _SHARDED:
        shp = W[n].shape
        w2 = W[n].reshape(shp[-2], shp[-1])
        d_, m_, v_ = _adamw(w2, G[n], M[n].reshape(w2.shape), V[n].reshape(w2.shape), f"adamw_{n}")
        G[n], delta[n], new_m[n], new_v[n] = [a.reshape(shp) for a in (G[n], d_, m_, v_)]
    gs, ds_, ms_, vs_ = _adamw_rows(cat(W), small_parts, cat(M), cat(V), "adamw_small")
    off = 0
    for n, sz in _SMALL:
        shp = W[n].shape
        G[n], delta[n], new_m[n], new_v[n] = [a[:, off:off + sz].reshape(shp) for a in (gs, ds_, ms_, vs_)]
        off += sz
    G["b_ada"], delta["b_ada"], new_m["b_ada"], new_v["b_ada"] = _adamw_rows(b_ada, gmod_all, m_b_ada, v_b_ada,
                                                                          "adamw_b_ada")
    return (loss, grad_x, *[G[n] for n in names], *[delta[n] for n in names], *[new_m[n] for n in names],
            *[new_v[n] for n in names])
```

```python
import functools
import math

import numpy as np
import jax
import jax.numpy as jnp
from jax import lax
from jax.experimental import pallas as pl
from jax.experimental.pallas import tpu as pltpu

F32 = jnp.float32
BF16 = jnp.bfloat16

D_MODEL = 1024
SEQ = 2048
N_HEADS = 8
HEAD_DIM = 64
D_A = 512
D_B = 512
Q_LORA = 384
KV_LORA = 256
ROPE_DIM = 32
NOPE_DIM = 64
D_FF = 2816
N_MOD = 6
N_BUCKETS = 32
MAX_DISTANCE = 2048
ROPE_THETA = 10000.0
EPS = 1e-6
NEG = -1e30
BLK = 128
DILATIONS = (1, 4, 16)
SPAN = 128
MLA_SCALE = (NOPE_DIM + ROPE_DIM) ** -0.5
DIL_SCALE = HEAD_DIM ** -0.5

ADAM_LR = 0.001
ADAM_B1 = 0.9
ADAM_B2 = 0.999
ADAM_EPS = 1e-08
ADAM_WD = 0.01
ADAM_STEP = 10

N_DEV = 8
N_CHIP = 4
LANES = 128
VMEM_LIMIT = 48 * 1024 * 1024

P_QKV = 3 * D_A
P_REST = KV_LORA + LANES + Q_LORA


def _cparams(sem=None):
    return pltpu.CompilerParams(dimension_semantics=sem, vmem_limit_bytes=VMEM_LIMIT)


def _pick(n, cands):
    for c in cands:
        if n % c == 0:
            return c
    raise ValueError(f"no tile for {n} in {cands}")


def _mm(a, b, mode, out_dtype, name, col_blocks=None, comm=None, halves=False):
    blocked = col_blocks is not None
    if mode == "nn":
        (M, K) = a.shape
        K2, N = (b.shape[1], b.shape[0] * b.shape[2]) if blocked else b.shape
    elif mode == "nt":
        (M, K) = (a.shape[1], 2 * a.shape[2]) if halves else a.shape
        N, K2 = (b.shape[1], b.shape[0] * b.shape[2]) if blocked else b.shape
    else:
        (K, M) = a.shape
        K2, N = (b.shape[1], 2 * b.shape[2]) if halves else b.shape
    assert K == K2, (a.shape, b.shape, mode)
    assert not halves or (blocked and col_blocks == 4 and mode in ("nt", "tn"))
    tm = _pick(M, (512, 384, 256, 128))
    tn = _pick(N, (1408, 1024, 768, 512, 384, 256, 128))
    tk = _pick(K, (1024, 512, 384, 256, 128))
    if blocked and mode == "nt":
        tk = K // col_blocks
    elif blocked:
        tn = N // col_blocks
    nk = K // tk
    out_shape = (M, N)
    out_spec = pl.BlockSpec((tm, tn), lambda i, j, k: (i, j))
    if mode == "nn":
        a_spec = pl.BlockSpec((tm, tk), lambda i, j, k: (i, k))
        b_spec = (pl.BlockSpec((None, tk, tn), lambda i, j, k: (j, k, 0)) if blocked
                  else pl.BlockSpec((tk, tn), lambda i, j, k: (k, j)))
        dn = (((1,), (0,)), ((), ()))
    elif mode == "nt":
        a_spec = (pl.BlockSpec((None, tm, tk), lambda i, j, k: (k // 2, i, k % 2)) if halves
                  else pl.BlockSpec((tm, tk), lambda i, j, k: (i, k)))
        b_spec = (pl.BlockSpec((None, tn, tk), lambda i, j, k: (k, j, 0)) if blocked
                  else pl.BlockSpec((tn, tk), lambda i, j, k: (j, k)))
        dn = (((1,), (1,)), ((), ()))
    else:
        a_spec = pl.BlockSpec((tk, tm), lambda i, j, k: (k, i))
        b_spec = (pl.BlockSpec((None, tk, tn), lambda i, j, k: (j // 2, k, j % 2)) if halves
                  else pl.BlockSpec((tk, tn), lambda i, j, k: (k, j)))
        dn = (((0,), (0,)), ((), ()))
        if blocked:
            out_shape = (col_blocks, M, tn)
            out_spec = pl.BlockSpec((None, tm, tn), lambda i, j, k: (j, i, 0))

    def body(a_ref, b_ref, o_ref, acc_ref):
        k = pl.program_id(2)

        @pl.when(k == 0)
        def _():
            acc_ref[...] = jnp.zeros_like(acc_ref)

        acc_ref[...] += lax.dot_general(a_ref[...].astype(BF16), b_ref[...].astype(BF16), dn,
                                        preferred_element_type=F32)

        @pl.when(k == nk - 1)
        def _():
            o_ref[...] = acc_ref[...].astype(o_ref.dtype)

    if comm is not None:
        (out,), got = _host_call(
            body, comm, name=name, out_shape=[jax.ShapeDtypeStruct(out_shape, out_dtype)],
            grid=(M // tm, N // tn, nk), in_specs=[a_spec, b_spec], out_specs=[out_spec],
            scratch_shapes=[pltpu.VMEM((tm, tn), F32)], args=(a, b))
        return out, got
    return pl.pallas_call(
        body, name=name,
        out_shape=jax.ShapeDtypeStruct(out_shape, out_dtype),
        grid=(M // tm, N // tn, nk),
        in_specs=[a_spec, b_spec],
        out_specs=out_spec,
        scratch_shapes=[pltpu.VMEM((tm, tn), F32)],
        compiler_params=_cparams(("parallel", "parallel", "arbitrary")),
    )(a, b)


ROW_TILE = 256


def _adaln_fwd(x, g, sc, sh, name, mix=None, gate=None):
    B, S, D = x.shape
    ts = ROW_TILE
    has_res = mix is not None

    def body(*refs):
        if has_res:
            x_ref, g_ref, sc_ref, sh_ref, mix_ref, gate_ref, h_ref, xr_ref = refs
            xr = x_ref[0] + gate_ref[0] * mix_ref[0]
            xr_ref[0] = xr
        else:
            x_ref, g_ref, sc_ref, sh_ref, h_ref = refs
            xr = x_ref[0]
        r = lax.rsqrt(jnp.mean(xr * xr, axis=-1, keepdims=True) + EPS)
        xn = (xr * r) * g_ref[...]
        h_ref[0] = (xn * (1.0 + sc_ref[0]) + sh_ref[0]).astype(h_ref.dtype)

    tok = pl.BlockSpec((1, ts, D), lambda b, s: (b, s, 0))
    per_b = pl.BlockSpec((1, 1, D), lambda b, s: (b, 0, 0))
    vec = pl.BlockSpec((1, D), lambda b, s: (0, 0))
    in_specs = [tok, vec, per_b, per_b]
    args = [x, g, sc, sh]
    out_shape = [jax.ShapeDtypeStruct((B, S, D), BF16)]
    out_specs = [tok]
    if has_res:
        in_specs += [tok, per_b]
        args += [mix, gate]
        out_shape.append(jax.ShapeDtypeStruct((B, S, D), F32))
        out_specs.append(tok)
    out = pl.pallas_call(
        body, name=name, out_shape=out_shape, grid=(B, S // ts),
        in_specs=in_specs, out_specs=out_specs,
        compiler_params=_cparams(("parallel", "parallel")),
    )(*args)
    return out if has_res else out[0]


def _adaln_bwd(dh, x, g, sc, dres, name, mix=None, gate=None):
    B, S, D = x.shape
    ts = ROW_TILE
    has_res = mix is not None

    def body(*refs):
        if has_res:
            (dh_ref, x_ref, g_ref, sc_ref, dres_ref, mix_ref, gate_ref,
             dx_ref, dsh_ref, dsc_ref, dg_ref, dgate_ref, dmix_ref) = refs
        else:
            (dh_ref, x_ref, g_ref, sc_ref, dres_ref, dx_ref, dsh_ref, dsc_ref, dg_ref) = refs
        b, s = pl.program_id(0), pl.program_id(1)
        xv = x_ref[0]
        dhv = dh_ref[0]
        gv = g_ref[...]
        r = lax.rsqrt(jnp.mean(xv * xv, axis=-1, keepdims=True) + EPS)
        n = xv * r
        xn = n * gv
        dxn = dhv * (1.0 + sc_ref[0])
        dn = dxn * gv
        dx = r * (dn - n * jnp.mean(dn * n, axis=-1, keepdims=True)) + dres_ref[0]
        dx_ref[0] = dx

        @pl.when(s == 0)
        def _():
            dsh_ref[...] = jnp.zeros_like(dsh_ref)
            dsc_ref[...] = jnp.zeros_like(dsc_ref)
            if has_res:
                dgate_ref[...] = jnp.zeros_like(dgate_ref)

        @pl.when((s == 0) & (b == 0))
        def _():
            dg_ref[...] = jnp.zeros_like(dg_ref)

        dsh_ref[0] += jnp.sum(dhv, axis=0, keepdims=True)
        dsc_ref[0] += jnp.sum(dhv * xn, axis=0, keepdims=True)
        dg_ref[...] += jnp.sum(dxn * n, axis=0, keepdims=True)
        if has_res:
            dgate_ref[0] += jnp.sum(dx * mix_ref[0], axis=0, keepdims=True)
            dmix_ref[0] = (dx * gate_ref[0]).astype(dmix_ref.dtype)

    tok = pl.BlockSpec((1, ts, D), lambda b, s: (b, s, 0))
    per_b = pl.BlockSpec((1, 1, D), lambda b, s: (b, 0, 0))
    vec = pl.BlockSpec((1, D), lambda b, s: (0, 0))
    in_specs = [tok, tok, vec, per_b, tok]
    args = [dh, x, g, sc, dres]
    out_shape = [jax.ShapeDtypeStruct((B, S, D), F32), jax.ShapeDtypeStruct((B, 1, D), F32),
                 jax.ShapeDtypeStruct((B, 1, D), F32), jax.ShapeDtypeStruct((1, D), F32)]
    out_specs = [tok, per_b, per_b, vec]
    if has_res:
        in_specs += [tok, per_b]
        args += [mix, gate]
        out_shape += [jax.ShapeDtypeStruct((B, 1, D), F32), jax.ShapeDtypeStruct((B, S, D), BF16)]
        out_specs += [per_b, tok]
    return pl.pallas_call(
        body, name=name, out_shape=out_shape, grid=(B, S // ts),
        in_specs=in_specs, out_specs=out_specs,
        compiler_params=_cparams(("arbitrary", "arbitrary")),
    )(*args)


def _rms_fwd(x, col_blk, n, g, name, n_real=None):
    T = x.shape[0]
    tr = 512
    nr = float(n_real or n)

    def body(x_ref, g_ref, y_ref):
        xv = x_ref[...]
        r = lax.rsqrt(jnp.sum(xv * xv, axis=-1, keepdims=True) / nr + EPS)
        y_ref[...] = ((xv * r) * g_ref[...]).astype(y_ref.dtype)

    return pl.pallas_call(
        body, name=name, out_shape=jax.ShapeDtypeStruct((T, n), BF16), grid=(T // tr,),
        in_specs=[pl.BlockSpec((tr, n), lambda i: (i, col_blk)), pl.BlockSpec((1, n), lambda i: (0, 0))],
        out_specs=pl.BlockSpec((tr, n), lambda i: (i, 0)),
        compiler_params=_cparams(("parallel",)),
    )(x, g)


def _rms_bwd(dy, dy_blk, x, x_blk, n, g, name, out_dtype=BF16):
    T = x.shape[0]
    tr = 512

    def body(dy_ref, x_ref, g_ref, dx_ref, dg_ref):
        xv = x_ref[...]
        dyv = dy_ref[...].astype(F32)
        r = lax.rsqrt(jnp.mean(xv * xv, axis=-1, keepdims=True) + EPS)
        nrm = xv * r
        dn = dyv * g_ref[...]
        dx_ref[...] = (r * (dn - nrm * jnp.mean(dn * nrm, axis=-1, keepdims=True))).astype(dx_ref.dtype)

        @pl.when(pl.program_id(0) == 0)
        def _():
            dg_ref[...] = jnp.zeros_like(dg_ref)

        dg_ref[...] += jnp.sum(dyv * nrm, axis=0, keepdims=True)

    return pl.pallas_call(
        body, name=name,
        out_shape=[jax.ShapeDtypeStruct((T, n), out_dtype), jax.ShapeDtypeStruct((1, n), F32)],
        grid=(T // tr,),
        in_specs=[pl.BlockSpec((tr, n), lambda i: (i, dy_blk)), pl.BlockSpec((tr, n), lambda i: (i, x_blk)),
                  pl.BlockSpec((1, n), lambda i: (0, 0))],
        out_specs=[pl.BlockSpec((tr, n), lambda i: (i, 0)), pl.BlockSpec((1, n), lambda i: (0, 0))],
        compiler_params=_cparams(("arbitrary",)),
    )(dy, x, g)


def _rms_bwd_views(dy, dy_blk, x, g, name):
    B, S, n = x.shape
    tiles = S // VIEW_TILE

    def body(dy_ref, x_ref, g_ref, d1_ref, d4_ref, d16_ref, dg_ref, dx_s):
        xv = x_ref[0]
        dyv = dy_ref[...]
        r = lax.rsqrt(jnp.mean(xv * xv, axis=-1, keepdims=True) + EPS)
        nrm = xv * r
        dn = dyv * g_ref[...]
        dx = r * (dn - nrm * jnp.mean(dn * nrm, axis=-1, keepdims=True))
        d1_ref[0] = dx.astype(d1_ref.dtype)
        _put_tile(dx_s, dx)
        _tile_to_view(dx_s, d4_ref, DILATIONS[1], n)
        _tile_to_view(dx_s, d16_ref, DILATIONS[2], n)

        @pl.when((pl.program_id(0) == 0) & (pl.program_id(1) == 0))
        def _():
            dg_ref[...] = jnp.zeros_like(dg_ref)

        dg_ref[...] += jnp.sum(dyv * nrm, axis=0, keepdims=True)

    res = pl.pallas_call(
        body, name=name,
        out_shape=[_view_shape(B, S, d, n, BF16) for d in DILATIONS] + [jax.ShapeDtypeStruct((1, n), F32)],
        grid=(B, tiles),
        in_specs=[pl.BlockSpec((VIEW_TILE, n), lambda b, t: (b * tiles + t, dy_blk)), _view_spec(1, n),
                  pl.BlockSpec((1, n), lambda b, t: (0, 0))],
        out_specs=[_view_spec(d, n) for d in DILATIONS] + [pl.BlockSpec((1, n), lambda b, t: (0, 0))],
        scratch_shapes=[_tile_scratch(n)],
        compiler_params=_cparams(("arbitrary", "arbitrary")),
    )(dy, x, g)
    return res[:len(DILATIONS)], res[len(DILATIONS)]


FFN_TILE = 1408


def _ffn_in_fwd(h, w4, name):
    T, D = h.shape
    tm, tc = 512, FFN_TILE
    nc = D_FF // tc

    def body(h_ref, wg_ref, wu_ref, gu_ref, act_ref):
        hv = h_ref[...]
        g = jnp.dot(hv, wg_ref[...], preferred_element_type=F32)
        u = jnp.dot(hv, wu_ref[...], preferred_element_type=F32)
        gu_ref[0] = g
        gu_ref[1] = u
        act_ref[...] = (g * jax.nn.sigmoid(g) * u).astype(act_ref.dtype)

    return pl.pallas_call(
        body, name=name,
        out_shape=[jax.ShapeDtypeStruct((2, T, D_FF), F32), jax.ShapeDtypeStruct((T, D_FF), BF16)],
        grid=(nc, T // tm),
        in_specs=[pl.BlockSpec((tm, D), lambda j, i: (i, 0)),
                  pl.BlockSpec((None, D, tc), lambda j, i: (j, 0, 0)),
                  pl.BlockSpec((None, D, tc), lambda j, i: (j + nc, 0, 0))],
        out_specs=[pl.BlockSpec((2, tm, tc), lambda j, i: (0, i, j)), pl.BlockSpec((tm, tc), lambda j, i: (i, j))],
        compiler_params=_cparams(("parallel", "parallel")),
    )(h, w4, w4)


def _ffn_out_bwd(df, w_out, gu, name):
    T, D = df.shape
    tm, tc = 512, FFN_TILE

    def body(df_ref, w_ref, gu_ref, dgu_ref):
        da = _dot_nt(df_ref[...], w_ref[...])
        g, u = gu_ref[0], gu_ref[1]
        sg = jax.nn.sigmoid(g)
        dgu_ref[0] = (da * u * (sg * (1.0 + g * (1.0 - sg)))).astype(dgu_ref.dtype)
        dgu_ref[1] = (da * (g * sg)).astype(dgu_ref.dtype)

    halves = pl.BlockSpec((2, tm, tc), lambda j, i: (0, i, j))
    return pl.pallas_call(
        body, name=name, out_shape=jax.ShapeDtypeStruct((2, T, D_FF), BF16), grid=(D_FF // tc, T // tm),
        in_specs=[pl.BlockSpec((tm, D), lambda j, i: (i, 0)), pl.BlockSpec((tc, D), lambda j, i: (j, 0)), halves],
        out_specs=halves,
        compiler_params=_cparams(("parallel", "parallel")),
    )(df, w_out, gu)


def _final_loss(x1, f, g2, gf, target, name):
    B, S, D = x1.shape
    ts = ROW_TILE

    def body(x1_ref, f_ref, g2_ref, gf_ref, t_ref, dx_ref, df_ref, dg2_ref, dgf_ref, loss_ref):
        b, s = pl.program_id(0), pl.program_id(1)
        fv = f_ref[0]
        g2v = g2_ref[0]
        gfv = gf_ref[...]
        x2 = x1_ref[0] + g2v * fv
        r = lax.rsqrt(jnp.mean(x2 * x2, axis=-1, keepdims=True) + EPS)
        n = x2 * r
        e = n * gfv - t_ref[0]
        dy = e * (1.0 / D)
        dn = dy * gfv
        dx = r * (dn - n * jnp.mean(dn * n, axis=-1, keepdims=True))
        dx_ref[0] = dx
        df_ref[0] = (dx * g2v).astype(df_ref.dtype)

        @pl.when(s == 0)
        def _():
            dg2_ref[...] = jnp.zeros_like(dg2_ref)

        @pl.when((s == 0) & (b == 0))
        def _():
            dgf_ref[...] = jnp.zeros_like(dgf_ref)
            loss_ref[...] = jnp.zeros_like(loss_ref)

        dg2_ref[0] += jnp.sum(dx * fv, axis=0, keepdims=True)
        dgf_ref[...] += jnp.sum(dy * n, axis=0, keepdims=True)
        loss_ref[...] += 0.5 * jnp.sum(jnp.mean(e * e, axis=-1, keepdims=True), axis=0, keepdims=True)

    tok = pl.BlockSpec((1, ts, D), lambda b, s: (b, s, 0))
    per_b = pl.BlockSpec((1, 1, D), lambda b, s: (b, 0, 0))
    vec = pl.BlockSpec((1, D), lambda b, s: (0, 0))
    return pl.pallas_call(
        body, name=name,
        out_shape=[jax.ShapeDtypeStruct((B, S, D), F32), jax.ShapeDtypeStruct((B, S, D), BF16),
                   jax.ShapeDtypeStruct((B, 1, D), F32), jax.ShapeDtypeStruct((1, D), F32),
                   jax.ShapeDtypeStruct((1, LANES), F32)],
        grid=(B, S // ts),
        in_specs=[tok, tok, per_b, vec, tok],
        out_specs=[tok, tok, per_b, vec, pl.BlockSpec((1, LANES), lambda b, s: (0, 0))],
        compiler_params=_cparams(("arbitrary", "arbitrary")),
    )(x1, f, g2, gf, target)


def _rope_tables():
    half = ROPE_DIM // 2
    inv = ROPE_THETA ** (-jnp.arange(half, dtype=F32) / half)
    ang = jnp.arange(SEQ, dtype=F32)[:, None] * inv[None, :]
    cos, sin = jnp.cos(ang), jnp.sin(ang)
    one = jnp.ones((SEQ, NOPE_DIM), F32)
    zero = jnp.zeros((SEQ, NOPE_DIM), F32)
    cs = jnp.concatenate([one, cos, cos, one[:, :LANES - NOPE_DIM - ROPE_DIM]], axis=1)
    sn = jnp.concatenate([zero, -sin, sin, zero[:, :LANES - NOPE_DIM - ROPE_DIM]], axis=1)
    return cs, sn


def _rope_group(t, cs, sn):
    half = ROPE_DIM // 2
    lane = lax.broadcasted_iota(jnp.int32, t.shape, 1)
    partner = jnp.where(lane < NOPE_DIM + half, pltpu.roll(t, LANES - half, 1), pltpu.roll(t, half, 1))
    return t * cs + partner * sn


def _rope_apply(t, cs, sn, out_dtype, name, add=None, add_blk=0):
    B, S, W = t.shape
    G = W // LANES
    ts = ROW_TILE

    def body(*refs):
        if add is None:
            t_ref, cs_ref, sn_ref, o_ref = refs
            for gi in range(G):
                sl = slice(gi * LANES, (gi + 1) * LANES)
                o_ref[0, :, sl] = _rope_group(t_ref[0, :, sl], cs_ref[...], sn_ref[...]).astype(o_ref.dtype)
        else:
            t_ref, a_ref, cs_ref, sn_ref, o_ref = refs
            ra = _rope_group(a_ref[0], cs_ref[...], sn_ref[...])
            for gi in range(G):
                sl = slice(gi * LANES, (gi + 1) * LANES)
                o_ref[0, :, sl] = (t_ref[0, :, sl] + ra).astype(o_ref.dtype)

    tok = pl.BlockSpec((1, ts, W), lambda b, s: (b, s, 0))
    tab = pl.BlockSpec((ts, LANES), lambda b, s: (s, 0))
    in_specs, args = [tok], [t]
    if add is not None:
        in_specs.append(pl.BlockSpec((1, ts, LANES), lambda b, s: (b, s, add_blk)))
        args.append(add)
    in_specs += [tab, tab]
    args += [cs, sn]
    return pl.pallas_call(
        body, name=name, out_shape=jax.ShapeDtypeStruct((B, S, W), out_dtype), grid=(B, S // ts),
        in_specs=in_specs, out_specs=tok, compiler_params=_cparams(("parallel", "parallel")),
    )(*args)


def _krope_bwd(dkc, cs, sn_neg, name):
    B, S, W = dkc.shape
    G = W // LANES
    ts = ROW_TILE

    def body(d_ref, cs_ref, sn_ref, o_ref):
        acc = d_ref[0, :, 0:LANES]
        for gi in range(1, G):
            acc = acc + d_ref[0, :, gi * LANES:(gi + 1) * LANES]
        lane = lax.broadcasted_iota(jnp.int32, acc.shape, 1)
        rot = (lane >= NOPE_DIM) & (lane < NOPE_DIM + ROPE_DIM)
        acc = jnp.where(rot, acc, 0.0)
        o_ref[0] = _rope_group(acc, cs_ref[...], sn_ref[...]).astype(o_ref.dtype)

    tab = pl.BlockSpec((ts, LANES), lambda b, s: (s, 0))
    return pl.pallas_call(
        body, name=name, out_shape=jax.ShapeDtypeStruct((B, S, LANES), BF16), grid=(B, S // ts),
        in_specs=[pl.BlockSpec((1, ts, W), lambda b, s: (b, s, 0)), tab, tab],
        out_specs=pl.BlockSpec((1, ts, LANES), lambda b, s: (b, s, 0)),
        compiler_params=_cparams(("parallel", "parallel")),
    )(dkc, cs, sn_neg)


def _t5_bucket(dist):
    max_exact = N_BUCKETS // 2
    d = np.maximum(dist, 1).astype(np.float64)
    large = max_exact + (np.log(d / max_exact) / np.log(MAX_DISTANCE / max_exact)
                         * (N_BUCKETS - max_exact)).astype(np.int64)
    large = np.minimum(large, N_BUCKETS - 1)
    return np.where(dist < max_exact, dist, large).astype(np.int32)


def _band_buckets(dilation):
    a = np.arange(BLK)[None, :]
    bk = np.arange(2 * BLK)[:, None]
    steps = BLK + a - bk
    return _t5_bucket(np.clip(steps, 0, SPAN) * dilation)


def _head_mask(shape, hh):
    lane = lax.broadcasted_iota(jnp.int32, shape, 1)
    return (lane >= hh * HEAD_DIM) & (lane < (hh + 1) * HEAD_DIM)


def _dot_nt(a, b):
    return lax.dot_general(a, b, (((1,), (1,)), ((), ())), preferred_element_type=F32)


def _dot_tn(a, b):
    return lax.dot_general(a, b, (((0,), (0,)), ((), ())), preferred_element_type=F32)


def _dot_nn(a, b):
    return lax.dot_general(a, b, (((1,), (0,)), ((), ())), preferred_element_type=F32)


def _band_valid_t():
    bk = lax.broadcasted_iota(jnp.int32, (BLK, BLK), 0)
    a = lax.broadcasted_iota(jnp.int32, (BLK, BLK), 1)
    return bk >= a, bk <= a


def _dil_fwd(qkv, bias, branch, dilation, name, comm=None):
    B, n, _ = qkv.shape
    d = dilation
    nb = n // BLK
    qkv_v = qkv
    npair = N_HEADS // 2

    def body(cur_ref, prev_ref, bias_ref, o_ref, lse_ref, s_scr, e_scr):
        i = pl.program_id(2)
        vprev, vcur = _band_valid_t()
        vprev = vprev & (i > 0)
        for p in range(npair):
            q = cur_ref[0, :, p * LANES:(p + 1) * LANES]
            kc = cur_ref[0, :, D_A + p * LANES:D_A + (p + 1) * LANES]
            kp = prev_ref[0, :, D_A + p * LANES:D_A + (p + 1) * LANES]
            for hh in range(2):
                h = 2 * p + hh
                qm = jnp.where(_head_mask((BLK, LANES), hh), q, jnp.zeros_like(q))
                s_scr[h, 0:BLK, :] = _dot_nt(kp, qm)
                s_scr[h, BLK:2 * BLK, :] = _dot_nt(kc, qm)
        ms = []
        for h in range(N_HEADS):
            s_p = jnp.where(vprev, s_scr[h, 0:BLK, :] * DIL_SCALE + bias_ref[h, 0:BLK, :], NEG)
            s_c = jnp.where(vcur, s_scr[h, BLK:2 * BLK, :] * DIL_SCALE + bias_ref[h, BLK:2 * BLK, :], NEG)
            m = jnp.maximum(jnp.max(s_p, axis=0, keepdims=True), jnp.max(s_c, axis=0, keepdims=True))
            e_scr[h, 0:BLK, :] = jnp.exp(s_p - m).astype(BF16)
            e_scr[h, BLK:2 * BLK, :] = jnp.exp(s_c - m).astype(BF16)
            ms.append(m)
        rows0 = _row_mask((LANES, BLK), 0)
        for p in range(npair):
            sl = slice(p * LANES, (p + 1) * LANES)
            vsl = slice(2 * D_A + p * LANES, 2 * D_A + (p + 1) * LANES)
            vct = jnp.transpose(cur_ref[0, :, vsl].astype(F32)).astype(BF16)
            vpt = jnp.transpose(prev_ref[0, :, vsl].astype(F32)).astype(BF16)
            acc = []
            for hh in range(2):
                h = 2 * p + hh
                mine = _row_mask((LANES, BLK), hh)
                one = jnp.ones_like(vct)
                acc.append(_dot_nn(jnp.where(mine, vpt, one), e_scr[h, 0:BLK, :])
                           + _dot_nn(jnp.where(mine, vct, one), e_scr[h, BLK:2 * BLK, :]))
            l0 = acc[0][HEAD_DIM:HEAD_DIM + 1, :]
            l1 = acc[1][0:1, :]
            o_t = jnp.where(rows0, acc[0] / l0, acc[1] / l1)
            lse_t = jnp.where(rows0, ms[2 * p] + jnp.log(l0), ms[2 * p + 1] + jnp.log(l1))
            o_ref[0, :, sl] = jnp.transpose(o_t)
            lse_ref[0, :, sl] = jnp.transpose(lse_t)

    cur = pl.BlockSpec((1, BLK, P_QKV), lambda b, r, i: (b, i, r))
    prev = pl.BlockSpec((1, BLK, P_QKV), lambda b, r, i: (b, jnp.maximum(i - 1, 0), r))
    out = pl.BlockSpec((1, BLK, D_A), lambda b, r, i: (b, i, r))
    return _host_call(
        body, comm, name=name,
        out_shape=[jax.ShapeDtypeStruct((B, n, d * D_A), F32)] * 2,
        grid=(B, d, nb),
        in_specs=[cur, prev,
                  pl.BlockSpec((None, N_HEADS, 2 * BLK, BLK), lambda b, r, i: (branch, 0, 0, 0))],
        out_specs=[out, out],
        scratch_shapes=[pltpu.VMEM((N_HEADS, 2 * BLK, BLK), F32), pltpu.VMEM((N_HEADS, 2 * BLK, BLK), BF16)],
        args=(qkv_v, qkv_v, bias))


VIEW_TILE = 512


def _view_spec(d, w):
    return pl.BlockSpec((1, VIEW_TILE // d, d * w), lambda b, t: (b, t, 0))


def _view_shape(B, S, d, w, dtype):
    return jax.ShapeDtypeStruct((B, S // d, d * w), dtype)


def _tile_scratch(w):
    return pltpu.VMEM((w // LANES, VIEW_TILE, LANES), F32)


def _put_tile(tile_ref, val):
    for c in range(tile_ref.shape[0]):
        tile_ref[c] = val[:, c * LANES:(c + 1) * LANES]


def _get_tile(tile_ref):
    return jnp.concatenate([tile_ref[c] for c in range(tile_ref.shape[0])], axis=1)


def _tile_to_view(tile_ref, view_ref, d, w):
    for c in range(w // LANES):
        for r in range(d):
            lo = r * w + c * LANES
            rows = tile_ref.at[c][pl.ds(r, VIEW_TILE // d, stride=d), :]
            view_ref[0, :, lo:lo + LANES] = rows.astype(view_ref.dtype)


def _view_to_tile(view_ref, tile_ref, d, w):
    for c in range(w // LANES):
        for r in range(d):
            lo = r * w + c * LANES
            tile_ref.at[c][pl.ds(r, VIEW_TILE // d, stride=d), :] = view_ref[0, :, lo:lo + LANES].astype(F32)


def _mm_qkv_views(h, w, name):
    B, S, D = h.shape
    N = w.shape[1]

    def body(h_ref, w_ref, o1_ref, o4_ref, o16_ref, acc_ref):
        acc = jnp.dot(h_ref[0], w_ref[...], preferred_element_type=F32)
        o1_ref[0] = acc.astype(o1_ref.dtype)
        _put_tile(acc_ref, acc)
        _tile_to_view(acc_ref, o4_ref, DILATIONS[1], N)
        _tile_to_view(acc_ref, o16_ref, DILATIONS[2], N)

    return pl.pallas_call(
        body, name=name,
        out_shape=[_view_shape(B, S, d, N, BF16) for d in DILATIONS],
        grid=(B, S // VIEW_TILE),
        in_specs=[pl.BlockSpec((1, VIEW_TILE, D), lambda b, t: (b, t, 0)), pl.BlockSpec((D, N), lambda b, t: (0, 0))],
        out_specs=[_view_spec(d, N) for d in DILATIONS],
        scratch_shapes=[_tile_scratch(N)],
        compiler_params=_cparams(("parallel", "parallel")),
    )(h, w)


def _dil_merge(os_, lses, name):
    B, S, W = os_[0].shape
    nd = len(DILATIONS)

    def body(*refs):
        o_refs, l_refs = refs[:nd], refs[nd:2 * nd]
        out_refs, L_refs = refs[2 * nd:3 * nd], refs[3 * nd:4 * nd]
        scr = refs[4 * nd:]
        o_tok, l_tok = [o_refs[0][0]], [l_refs[0][0]]
        for i, d in enumerate(DILATIONS[1:]):
            _view_to_tile(o_refs[i + 1], scr[2 * i], d, W)
            _view_to_tile(l_refs[i + 1], scr[2 * i + 1], d, W)
            o_tok.append(_get_tile(scr[2 * i]))
            l_tok.append(_get_tile(scr[2 * i + 1]))
        a0, a1, a2 = l_tok
        m = jnp.maximum(jnp.maximum(a0, a1), a2)
        e0, e1, e2 = jnp.exp(a0 - m), jnp.exp(a1 - m), jnp.exp(a2 - m)
        ssum = e0 + e1 + e2
        out = (e0 * o_tok[0] + e1 * o_tok[1] + e2 * o_tok[2]) / ssum
        lse = m + jnp.log(ssum)
        out_refs[0][0] = out
        L_refs[0][0] = lse
        res_o, res_l = scr[2 * (nd - 1)], scr[2 * (nd - 1) + 1]
        _put_tile(res_o, out)
        _put_tile(res_l, lse)
        for i, d in enumerate(DILATIONS[1:]):
            _tile_to_view(res_o, out_refs[i + 1], d, W)
            _tile_to_view(res_l, L_refs[i + 1], d, W)

    specs = [_view_spec(d, W) for d in DILATIONS]
    shapes = [_view_shape(B, S * DILATIONS[0], d, W, F32) for d in DILATIONS]
    res = pl.pallas_call(
        body, name=name, out_shape=shapes * 2, grid=(B, S // VIEW_TILE),
        in_specs=specs * 2, out_specs=specs * 2,
        scratch_shapes=[_tile_scratch(W)] * (2 * nd),
        compiler_params=_cparams(("parallel", "parallel")),
    )(*os_, *lses)
    return res[:nd], res[nd:]


def _dil_bwd(qkv, do, out_a, L, bias, branch, dilation, name, comm=None):
    B, n, _ = qkv.shape
    d = dilation
    nb = n // BLK
    qkv_v, do_v, oa_v, L_v = qkv, do, out_a, L
    npair = N_HEADS // 2
    multi = nb > 1

    tiles = ("P", "C", "N") if multi else ("C",)
    n_t = len(tiles)

    def body(*refs):
        if multi:
            (cur_ref, prev_ref, next_ref, do_ref, don_ref, oa_ref, oan_ref, L_ref, Ln_ref, bias_ref,
             dqkv_ref, dbias_ref, s_scr, dp_scr, p_scr, ds_scr) = refs
        else:
            cur_ref, do_ref, oa_ref, L_ref, bias_ref, dqkv_ref, dbias_ref, s_scr, dp_scr, p_scr, ds_scr = refs
        b, r, i = pl.program_id(0), pl.program_id(1), pl.program_id(2)

        @pl.when((b == 0) & (r == 0) & (i == 0))
        def _():
            dbias_ref[...] = jnp.zeros_like(dbias_ref)

        vprev, vcur = _band_valid_t()
        valid = {"P": vprev & (i > 0), "C": vcur, "N": vprev & (i < nb - 1)}
        band = {"P": slice(0, BLK), "C": slice(BLK, 2 * BLK), "N": slice(0, BLK)}
        psl = lambda p: slice(p * LANES, (p + 1) * LANES)
        ksl = lambda p: slice(D_A + p * LANES, D_A + (p + 1) * LANES)
        vsl = lambda p: slice(2 * D_A + p * LANES, 2 * D_A + (p + 1) * LANES)

        def operands(p, hh):
            hm = _head_mask((BLK, LANES), hh)
            mask = lambda x: jnp.where(hm, x, jnp.zeros_like(x))
            qm, dom = mask(cur_ref[0, :, psl(p)]), mask(do_ref[0, :, psl(p)])
            ops = {"C": (cur_ref[0, :, ksl(p)], cur_ref[0, :, vsl(p)], qm, dom)}
            if multi:
                ops["P"] = (prev_ref[0, :, ksl(p)], prev_ref[0, :, vsl(p)], qm, dom)
                ops["N"] = (cur_ref[0, :, ksl(p)], cur_ref[0, :, vsl(p)], mask(next_ref[0, :, psl(p)]),
                            mask(don_ref[0, :, psl(p)]))
            return ops

        for p in range(npair):
            for hh in range(2):
                h = 2 * p + hh
                ops = operands(p, hh)
                for t, name_t in enumerate(tiles):
                    k_t, v_t, q_t, do_t = ops[name_t]
                    s_scr[h, t] = _dot_nt(k_t, q_t)
                    dp_scr[h, t] = _dot_nt(v_t, do_t)

        def rows(L_r, do_r, oa_r, p):
            lt = jnp.transpose(L_r[0, :, psl(p)])
            dt = jnp.transpose(do_r[0, :, psl(p)].astype(F32) * oa_r[0, :, psl(p)])
            return ([lt[0:1, :], lt[HEAD_DIM:HEAD_DIM + 1, :]],
                    [jnp.sum(dt[:HEAD_DIM], axis=0, keepdims=True), jnp.sum(dt[HEAD_DIM:], axis=0, keepdims=True)])

        for p in range(npair):
            lse_c, delta_c = rows(L_ref, do_ref, oa_ref, p)
            if multi:
                lse_n, delta_n = rows(Ln_ref, don_ref, oan_ref, p)
            for hh in range(2):
                h = 2 * p + hh
                for t, name_t in enumerate(tiles):
                    lse, delta = (lse_n[hh], delta_n[hh]) if name_t == "N" else (lse_c[hh], delta_c[hh])
                    s = s_scr[h, t] * DIL_SCALE + bias_ref[h, band[name_t], :]
                    pr = jnp.where(valid[name_t], jnp.exp(s - lse), 0.0)
                    ds = pr * (dp_scr[h, t] - delta)
                    p_scr[h, t] = pr.astype(BF16)
                    ds_scr[h, t] = ds.astype(BF16)
                    if name_t != "N":
                        dbias_ref[h, band[name_t], :] += ds

        for p in range(npair):
            dqt = jnp.zeros((LANES, BLK), F32)
            dk = jnp.zeros((BLK, LANES), F32)
            dv = jnp.zeros((BLK, LANES), F32)
            kct = jnp.transpose(cur_ref[0, :, ksl(p)].astype(F32)).astype(BF16)
            if multi:
                kpt = jnp.transpose(prev_ref[0, :, ksl(p)].astype(F32)).astype(BF16)
            for hh in range(2):
                h = 2 * p + hh
                ops = operands(p, hh)
                mine = _row_mask((LANES, BLK), hh)
                for t, name_t in enumerate(tiles):
                    _, _, q_t, do_t = ops[name_t]
                    if name_t != "P":
                        dv = dv + _dot_nn(p_scr[h, t], do_t)
                        dk = dk + _dot_nn(ds_scr[h, t], q_t)
                    if name_t != "N":
                        kt = kpt if name_t == "P" else kct
                        dqt = dqt + _dot_nn(jnp.where(mine, kt, jnp.zeros_like(kt)), ds_scr[h, t])
            dqkv_ref[0, :, psl(p)] = jnp.transpose(dqt) * DIL_SCALE
            dqkv_ref[0, :, ksl(p)] = dk * DIL_SCALE
            dqkv_ref[0, :, vsl(p)] = dv

    def at(off):
        return lambda b, r, i: (b, jnp.clip(i + off, 0, nb - 1), r)

    qkv_spec = lambda off: pl.BlockSpec((1, BLK, P_QKV), at(off))
    da_spec = lambda off: pl.BlockSpec((1, BLK, D_A), at(off))
    bias_spec = pl.BlockSpec((None, N_HEADS, 2 * BLK, BLK), lambda b, r, i: (branch, 0, 0, 0))
    dbias_spec = pl.BlockSpec((N_HEADS, 2 * BLK, BLK), lambda b, r, i: (0, 0, 0))
    if multi:
        in_specs = [qkv_spec(0), qkv_spec(-1), qkv_spec(1), da_spec(0), da_spec(1), da_spec(0), da_spec(1),
                    da_spec(0), da_spec(1), bias_spec]
        args = [qkv_v, qkv_v, qkv_v, do_v, do_v, oa_v, oa_v, L_v, L_v, bias]
    else:
        in_specs = [qkv_spec(0), da_spec(0), da_spec(0), da_spec(0), bias_spec]
        args = [qkv_v, do_v, oa_v, L_v, bias]
    return _host_call(
        body, comm, name=name,
        out_shape=[jax.ShapeDtypeStruct((B, n, d * P_QKV), F32),
                   jax.ShapeDtypeStruct((N_HEADS, 2 * BLK, BLK), F32)],
        grid=(B, d, nb),
        in_specs=in_specs,
        out_specs=[qkv_spec(0), dbias_spec],
        scratch_shapes=[pltpu.VMEM((N_HEADS, n_t, BLK, BLK), F32), pltpu.VMEM((N_HEADS, n_t, BLK, BLK), F32),
                        pltpu.VMEM((N_HEADS, n_t, BLK, BLK), BF16), pltpu.VMEM((N_HEADS, n_t, BLK, BLK), BF16)],
        args=args)


def _sum_views_bf16(parts, name):
    B, S, W = parts[0].shape

    def body(a_ref, b_ref, c_ref, o_ref, sb, sc):
        _view_to_tile(b_ref, sb, DILATIONS[1], W)
        _view_to_tile(c_ref, sc, DILATIONS[2], W)
        o_ref[0] = (a_ref[0] + _get_tile(sb) + _get_tile(sc)).astype(o_ref.dtype)

    return pl.pallas_call(
        body, name=name, out_shape=jax.ShapeDtypeStruct((B, S, W), BF16), grid=(B, S // VIEW_TILE),
        in_specs=[_view_spec(d, W) for d in DILATIONS], out_specs=_view_spec(1, W),
        scratch_shapes=[_tile_scratch(W)] * 2,
        compiler_params=_cparams(("parallel", "parallel")),
    )(*parts)


def _bias_tables(rel_bias, buckets, name):
    nbr = buckets.shape[0]

    def body(rb_ref, bk_ref, o_ref):
        h = pl.program_id(1)
        tab = bk_ref[0]

        def step(bkt, acc):
            return jnp.where(tab == bkt, rb_ref[bkt, h], acc)

        o_ref[0, 0] = lax.fori_loop(0, N_BUCKETS, step, jnp.zeros((2 * BLK, BLK), F32))

    return pl.pallas_call(
        body, name=name, out_shape=jax.ShapeDtypeStruct((nbr, N_HEADS, 2 * BLK, BLK), F32),
        grid=(nbr, N_HEADS),
        in_specs=[pl.BlockSpec(memory_space=pltpu.SMEM),
                  pl.BlockSpec((1, 2 * BLK, BLK), lambda i, h: (i, 0, 0))],
        out_specs=pl.BlockSpec((1, 1, 2 * BLK, BLK), lambda i, h: (i, h, 0, 0)),
        compiler_params=_cparams(("parallel", "arbitrary")),
    )(rel_bias, buckets)


def _bias_grad(dbias_list, buckets, name):
    nbr = len(dbias_list)

    def body(*refs):
        d_refs, bk_ref, o_ref, part = refs[:nbr], refs[nbr], refs[nbr + 1], refs[nbr + 2]

        def step(bkt, carry):
            hit = [bk_ref[bi] == bkt for bi in range(nbr)]
            for h in range(N_HEADS):
                tot = jnp.zeros((1, BLK), F32)
                for bi in range(nbr):
                    tot = tot + jnp.sum(jnp.where(hit[bi], d_refs[bi][h], 0.0), axis=0, keepdims=True)
                part[bkt, h:h + 1, :] = tot
            return carry

        lax.fori_loop(0, N_BUCKETS, step, 0)
        lane = lax.broadcasted_iota(jnp.int32, (N_HEADS, LANES), 1)
        acc = jnp.zeros((N_HEADS, LANES), F32)
        for bkt in range(N_BUCKETS):
            acc = acc + jnp.where(lane == bkt, jnp.sum(part[bkt], axis=1, keepdims=True), 0.0)
        o_ref[...] = acc

    band = pl.BlockSpec((N_HEADS, 2 * BLK, BLK), lambda i: (0, 0, 0))
    return pl.pallas_call(
        body, name=name, out_shape=jax.ShapeDtypeStruct((N_HEADS, LANES), F32), grid=(1,),
        in_specs=[band] * nbr + [pl.BlockSpec((nbr, 2 * BLK, BLK), lambda i: (0, 0, 0))],
        out_specs=pl.BlockSpec((N_HEADS, LANES), lambda i: (0, 0)),
        scratch_shapes=[pltpu.VMEM((N_BUCKETS, N_HEADS, BLK), F32)],
        compiler_params=_cparams(("arbitrary",)),
    )(*dbias_list, buckets)


MLA_TQ = 256
MLA_TK = 256


LOG2E = math.log2(math.e)
MLA_C = MLA_SCALE * LOG2E


def _key_le_query(tk, tq):
    return lax.broadcasted_iota(jnp.int32, (tk, tq), 0) <= lax.broadcasted_iota(jnp.int32, (tk, tq), 1)


def _row_mask(shape, hh):
    row = lax.broadcasted_iota(jnp.int32, shape, 0)
    return (row >= hh * HEAD_DIM) & (row < (hh + 1) * HEAD_DIM)


def _host_call(body, comm, *, name, grid, in_specs, out_specs, out_shape, scratch_shapes, args):
    sem = ("arbitrary",) * len(grid)
    if comm is None:
        res = pl.pallas_call(body, name=name, grid=grid, in_specs=in_specs, out_specs=out_specs,
                             out_shape=out_shape, scratch_shapes=scratch_shapes,
                             compiler_params=_cparams(sem))(*args)
        return res, []
    n_in, n_out, n_s, cn = len(in_specs), len(out_specs), len(scratch_shapes), comm.n

    def hosted(*refs):
        ins, refs = refs[:n_in], refs[n_in:]
        c_ins, refs = refs[:cn], refs[cn:]
        outs, refs = refs[:n_out], refs[n_out:]
        c_outs, refs = refs[:cn], refs[cn:]
        scr, c_sems = refs[:n_s], refs[n_s:]
        ids = [pl.program_id(a) for a in range(len(grid))]
        first = functools.reduce(jnp.logical_and, [i == 0 for i in ids])
        last = functools.reduce(jnp.logical_and, [i == g - 1 for i, g in zip(ids, grid)])

        @pl.when(first)
        def _():
            comm.start(c_ins, c_outs, c_sems)

        body(*ins, *outs, *scr)

        @pl.when(last)
        def _():
            comm.finish(c_ins, c_outs, c_sems)

    res = pl.pallas_call(
        hosted, name=name, grid=grid, in_specs=list(in_specs) + _hbm_specs(cn),
        out_specs=list(out_specs) + _hbm_specs(cn), out_shape=list(out_shape) + list(comm.out_shape),
        scratch_shapes=list(scratch_shapes) + list(comm.scratch), compiler_params=_cparams(sem),
    )(*args, *comm.inputs)
    return res[:n_out], res[n_out:]


def _mla_fwd_t(q, k, vt, name, comm=None):
    B, S, _ = q.shape
    tq, tk = MLA_TQ, MLA_TK
    assert tq == tk
    npair = N_HEADS // 2
    nq = S // tq

    def body(q_ref, k_ref, vt_ref, o_ref, lse_ref, s_scr, e_scr, acc_scr, m_scr, a_scr):
        i = pl.program_id(1)
        diag = _key_le_query(tk, tq)
        m_scr[...] = jnp.full_like(m_scr, NEG)
        acc_scr[...] = jnp.zeros_like(acc_scr)

        def step(j, masked):
            rows = pl.ds(pl.multiple_of(j * tk, tk), tk)
            for h in range(N_HEADS):
                hsl = slice(h * LANES, (h + 1) * LANES)
                s_scr[h] = _dot_nt(k_ref[0, rows, hsl], q_ref[0, :, hsl])
            for h in range(N_HEADS):
                s = s_scr[h]
                if masked:
                    s = jnp.where(diag, s, NEG)
                m_old = m_scr[h:h + 1, :]
                m_new = jnp.maximum(m_old, jnp.max(s, axis=0, keepdims=True))
                a_scr[h:h + 1, :] = jnp.exp2((m_old - m_new) * MLA_C)
                e_scr[h] = jnp.exp2((s - m_new) * MLA_C).astype(BF16)
                m_scr[h:h + 1, :] = m_new
            for h in range(N_HEADS):
                vj = vt_ref[0, h // 2, j]
                vh = jnp.where(_row_mask(vj.shape, h % 2), vj, jnp.ones_like(vj))
                acc_scr[h] = acc_scr[h] * a_scr[h:h + 1, :] + _dot_nn(vh, e_scr[h])

        def loop_body(j, carry):
            step(j, False)
            return carry

        lax.fori_loop(0, i, loop_body, 0)
        step(i, True)
        rows0 = _row_mask((LANES, tq), 0)
        for p in range(npair):
            l0 = acc_scr[2 * p, HEAD_DIM:HEAD_DIM + 1, :]
            l1 = acc_scr[2 * p + 1, 0:1, :]
            o_ref[0, p * LANES:(p + 1) * LANES, :] = jnp.where(rows0, acc_scr[2 * p] / l0, acc_scr[2 * p + 1] / l1)
            lse_ref[0, p, 0] = jnp.zeros((8, tq), F32)
            lse_ref[0, p, 0, 0:1, :] = m_scr[2 * p:2 * p + 1, :] * MLA_C + jnp.log(l0) * LOG2E
            lse_ref[0, p, 0, 1:2, :] = m_scr[2 * p + 1:2 * p + 2, :] * MLA_C + jnp.log(l1) * LOG2E

    return _host_call(
        body, comm, name=name,
        out_shape=[jax.ShapeDtypeStruct((B, D_B, S), F32), jax.ShapeDtypeStruct((B, npair, nq, 8, tq), F32)],
        grid=(B, nq),
        in_specs=[pl.BlockSpec((1, tq, N_HEADS * LANES), lambda b, i: (b, i, 0)),
                  pl.BlockSpec((1, S, N_HEADS * LANES), lambda b, i: (b, 0, 0)),
                  pl.BlockSpec((1, npair, S // tk, LANES, tk), lambda b, i: (b, 0, 0, 0, 0))],
        out_specs=[pl.BlockSpec((1, D_B, tq), lambda b, i: (b, 0, i)),
                   pl.BlockSpec((1, npair, 1, 8, tq), lambda b, i: (b, 0, i, 0, 0))],
        scratch_shapes=[pltpu.VMEM((N_HEADS, tk, tq), F32), pltpu.VMEM((N_HEADS, tk, tq), BF16),
                        pltpu.VMEM((N_HEADS, LANES, tq), F32), pltpu.VMEM((N_HEADS, tq), F32),
                        pltpu.VMEM((N_HEADS, tq), F32)],
        args=(q, k, vt))


def _mla_delta(do, o, name):
    B, S, _ = o.shape
    tq = MLA_TQ
    npair = N_HEADS // 2

    def body(do_ref, o_ref, d_ref):
        d_ref[...] = jnp.zeros_like(d_ref)
        for p in range(npair):
            sl = slice(p * LANES, (p + 1) * LANES)
            prod_t = jnp.transpose(do_ref[0, :, sl].astype(F32) * o_ref[0, :, sl])
            d_ref[0, p, 0, 0:1, :] = jnp.sum(prod_t[:HEAD_DIM], axis=0, keepdims=True)
            d_ref[0, p, 0, 1:2, :] = jnp.sum(prod_t[HEAD_DIM:], axis=0, keepdims=True)

    tok = pl.BlockSpec((1, tq, D_B), lambda b, i: (b, i, 0))
    return pl.pallas_call(
        body, name=name, out_shape=jax.ShapeDtypeStruct((B, npair, S // tq, 8, tq), F32),
        grid=(B, S // tq), in_specs=[tok, tok],
        out_specs=pl.BlockSpec((1, npair, 1, 8, tq), lambda b, i: (b, 0, i, 0, 0)),
        compiler_params=_cparams(("parallel", "parallel")),
    )(do, o)


def _mla_bwd_t(q, k, v, do, lse, delta, name, comm=None):
    B, S, _ = q.shape
    tq, tk = MLA_TQ, MLA_TK
    assert tq == tk
    npair = N_HEADS // 2
    nq = S // tq

    hg = 4
    pg = hg // 2
    ngroup = N_HEADS // hg

    def body(q_ref, do_ref, lse_ref, dl_ref, k_ref, v_ref, dk_ref, dv_ref, dq_ref,
             s_scr, dp_scr, p_scr, ds_scr, dk_s, dv_s, kt_s):
        j = pl.program_id(2)

        @pl.when(j == 0)
        def _():
            dq_ref[...] = jnp.zeros_like(dq_ref)

        dk_s[...] = jnp.zeros_like(dk_s)
        dv_s[...] = jnp.zeros_like(dv_s)
        diag = _key_le_query(tk, tq)
        hsl = lambda h: slice(h * LANES, (h + 1) * LANES)
        for h in range(hg):
            kt_s[h] = jnp.transpose(k_ref[0, :, hsl(h)].astype(F32)).astype(BF16)

        def step(i, masked):
            rows = pl.ds(pl.multiple_of(i * tq, tq), tq)

            def dom(h):
                dov = do_ref[0, rows, hsl(h // 2)]
                return jnp.where(_head_mask((tq, LANES), h % 2), dov, jnp.zeros_like(dov))

            for h in range(hg):
                s_scr[h] = _dot_nt(k_ref[0, :, hsl(h)], q_ref[0, rows, hsl(h)])
                dp_scr[h] = _dot_nt(v_ref[0, :, hsl(h // 2)], dom(h))
            for h in range(hg):
                pr = jnp.exp2(s_scr[h] * MLA_C - lse_ref[0, h // 2, i, h % 2:h % 2 + 1, :])
                if masked:
                    pr = jnp.where(diag, pr, 0.0)
                p_scr[h] = pr.astype(BF16)
                ds_scr[h] = (pr * (dp_scr[h] - dl_ref[0, h // 2, i, h % 2:h % 2 + 1, :])).astype(BF16)
            for h in range(hg):
                dv_s[h // 2] += _dot_nn(p_scr[h], dom(h))
                dk_s[h] += _dot_nn(ds_scr[h], q_ref[0, rows, hsl(h)])
                dq_ref[0, h // 2, i, hsl(h % 2), :] += _dot_nn(kt_s[h], ds_scr[h]) * MLA_SCALE

        step(j, True)

        def loop_body(i, carry):
            step(i, False)
            return carry

        lax.fori_loop(j + 1, nq, loop_body, 0)
        for h in range(hg):
            dk_ref[0, :, hsl(h)] = dk_s[h] * MLA_SCALE
        for p in range(pg):
            dv_ref[0, :, hsl(p)] = dv_s[p]

    stat = pl.BlockSpec((1, pg, nq, 8, tq), lambda b, g, j: (b, g, 0, 0, 0))
    return _host_call(
        body, comm, name=name,
        out_shape=[jax.ShapeDtypeStruct((B, S, N_HEADS * LANES), F32), jax.ShapeDtypeStruct((B, S, D_B), F32),
                   jax.ShapeDtypeStruct((B, npair, nq, 2 * LANES, tq), F32)],
        grid=(B, ngroup, S // tk),
        in_specs=[pl.BlockSpec((1, S, hg * LANES), lambda b, g, j: (b, 0, g)),
                  pl.BlockSpec((1, S, pg * LANES), lambda b, g, j: (b, 0, g)),
                  stat, stat,
                  pl.BlockSpec((1, tk, hg * LANES), lambda b, g, j: (b, j, g)),
                  pl.BlockSpec((1, tk, pg * LANES), lambda b, g, j: (b, j, g))],
        out_specs=[pl.BlockSpec((1, tk, hg * LANES), lambda b, g, j: (b, j, g)),
                   pl.BlockSpec((1, tk, pg * LANES), lambda b, g, j: (b, j, g)),
                   pl.BlockSpec((1, pg, nq, 2 * LANES, tq), lambda b, g, j: (b, g, 0, 0, 0))],
        scratch_shapes=[pltpu.VMEM((hg, tk, tq), F32), pltpu.VMEM((hg, tk, tq), F32),
                        pltpu.VMEM((hg, tk, tq), BF16), pltpu.VMEM((hg, tk, tq), BF16),
                        pltpu.VMEM((hg, tk, LANES), F32), pltpu.VMEM((pg, tk, LANES), F32),
                        pltpu.VMEM((hg, LANES, tk), BF16)],
        args=(q, do, lse, delta, k, v))


def _local_step(x, target, mod, wts, gains, rel_bias, ffn_shards=None):
    B, S, D = x.shape
    T = B * S
    sh1, sc1, g1, sh2, sc2, g2 = [mod[:, i * D:(i + 1) * D].reshape(B, 1, D) for i in range(N_MOD)]
    cs, sn = _rope_tables()
    buckets = np.stack([_band_buckets(d) for d in DILATIONS])
    buckets_dev = jnp.asarray(buckets)
    bias = _bias_tables(rel_bias, buckets_dev, "rel_bias_tables")
    w_in = wts["w_in"]

    h1 = _adaln_fwd(x, gains["g_norm1"], sc1, sh1, "adaln1_fwd")
    h1f = h1.reshape(T, D)
    qkv_v = _mm_qkv_views(h1, w_in[:, :P_QKV], "mm_qkv")
    rest = _mm(h1f, w_in[:, P_QKV:], "nn", F32, "mm_rest")
    o_d, lse_d = [], []
    w_ffn_out_got = None
    for i, d in enumerate(DILATIONS):
        comm = _GatherComm(ffn_shards[1:]) if (ffn_shards and i == 0) else None
        (o_i, lse_i), got = _dil_fwd(qkv_v[i], bias, i, d, f"dil_fwd_{d}", comm)
        if comm is not None:
            w_ffn_out_got = got[0]
        o_d.append(o_i)
        lse_d.append(lse_i)
    out_a_v, lse_a_v = _dil_merge(o_d, lse_d, "dil_merge")
    out_a = out_a_v[0]
    cqn = _rms_fwd(rest, 1, Q_LORA, gains["g_cq"], "rms_cq_fwd")
    ckvn = _rms_fwd(rest, 0, KV_LORA, gains["g_ckv"], "rms_ckv_fwd")
    rest3 = rest.reshape(B, S, P_REST)
    q_raw = _mm(cqn, wts["w_uq"], "nn", F32, "mm_uq").reshape(B, S, N_HEADS * LANES)
    qc = _rope_apply(q_raw, cs, sn, BF16, "rope_q")
    kn_raw = _mm(ckvn, wts["w_kv"][:, :N_HEADS * LANES], "nn", F32, "mm_uk").reshape(B, S, N_HEADS * LANES)
    kc = _rope_apply(kn_raw, cs, sn, BF16, "rope_k", add=rest3, add_blk=KV_LORA // LANES)
    v = _mm(ckvn, wts["w_kv"][:, N_HEADS * LANES:], "nn", BF16, "mm_uv").reshape(B, S, D_B)
    vt = jnp.transpose(v.reshape(B, S // MLA_TK, MLA_TK, N_HEADS // 2, LANES), (0, 3, 1, 4, 2))
    (o_t, lse_b), got = _mla_fwd_t(qc, kc, vt, "mla_fwd", _GatherComm(ffn_shards[:1]) if ffn_shards else None)
    if ffn_shards:
        wts = dict(wts, w_ffn_in=got[0].reshape(N_CHIP, D, -1), w_ffn_out=w_ffn_out_got.reshape(D_FF, D))
    out_b = jnp.transpose(o_t, (0, 2, 1))
    out_af, out_bf = out_a.reshape(T, D_A), out_b.reshape(T, D_B)
    ya = _rms_fwd(out_af, 0, D_A, gains["g_out_a"], "rms_outa_fwd")
    yb = _rms_fwd(out_bf, 0, D_B, gains["g_out_b"], "rms_outb_fwd")
    y = jnp.concatenate([ya, yb], axis=1)
    mix = _mm(y, wts["w_out"], "nn", F32, "mm_out").reshape(B, S, D)
    h2, x1 = _adaln_fwd(x, gains["g_norm2"], sc2, sh2, "adaln2_fwd", mix=mix, gate=g1)
    h2f = h2.reshape(T, D)
    gu, act = _ffn_in_fwd(h2f, wts["w_ffn_in"], "mm_ffn_in")
    f = _mm(act, wts["w_ffn_out"], "nn", F32, "mm_ffn_out").reshape(B, S, D)
    dx2, df, dg2, dg_final, loss = _final_loss(x1, f, g2, gains["g_final"], target, "final_loss")

    dff = df.reshape(T, D)
    dgu = _ffn_out_bwd(dff, wts["w_ffn_out"], gu, "mm_ffn_out_dx")
    gw_ffn_out = _mm(act, dff, "tn", F32, "mm_ffn_out_dw")
    dh2 = _mm(dgu, wts["w_ffn_in"], "nt", F32, "mm_ffn_in_dx", col_blocks=N_CHIP, halves=True).reshape(B, S, D)
    gw_ffn_in = _mm(h2f, dgu, "tn", F32, "mm_ffn_in_dw", col_blocks=N_CHIP, halves=True)
    dx1, dsh2, dsc2, dg_norm2, dg1, dmix = _adaln_bwd(dh2, x1, gains["g_norm2"], sc2, dx2, "adaln2_bwd",
                                                      mix=mix, gate=g1)
    dmixf = dmix.reshape(T, D)
    dy = _mm(dmixf, wts["w_out"], "nt", F32, "mm_out_dx")
    gw_out = _mm(y, dmixf, "tn", F32, "mm_out_dw")
    do_a_v, dg_out_a = _rms_bwd_views(dy, 0, out_a, gains["g_out_a"], "rms_outa_bwd")
    do_b, dg_out_b = _rms_bwd(dy, 1, out_bf, 0, D_B, gains["g_out_b"], "rms_outb_bwd")
    do_b3 = do_b.reshape(B, S, D_B)
    delta_b = _mla_delta(do_b3, out_b, "mla_delta")
    ffn_a4 = None
    if ffn_shards:
        ffn_a4 = _rs_first([gw_ffn_in.reshape(N_DEV, -1, gw_ffn_in.shape[-1]), gw_ffn_out.reshape(N_DEV, -1, D)],
                           "ffn")
    (dkc, dv, dq_t), ffn_r2 = _mla_bwd_t(qc, kc, v, do_b3, lse_b, delta_b, "mla_bwd",
                                         _ToChipsComm(ffn_a4[:1]) if ffn_shards else None)
    dqc = jnp.transpose(dq_t, (0, 2, 4, 1, 3)).reshape(B, S, N_HEADS * LANES)
    dq_raw = _rope_apply(dqc, cs, -sn, BF16, "rope_q_bwd").reshape(T, N_HEADS * LANES)
    dkrw = _krope_bwd(dkc, cs, -sn, "rope_k_bwd").reshape(T, LANES)
    dcqn = _mm(dq_raw, wts["w_uq"], "nt", F32, "mm_uq_dx")
    gw_uq = _mm(cqn, dq_raw, "tn", F32, "mm_uq_dw")
    dkv = jnp.concatenate([dkc.reshape(T, -1), dv.reshape(T, -1)], axis=1).astype(BF16)
    dckvn = _mm(dkv, wts["w_kv"], "nt", F32, "mm_ukv_dx")
    gw_kv = _mm(ckvn, dkv, "tn", F32, "mm_ukv_dw")
    dcq, dg_cq = _rms_bwd(dcqn, 0, rest, 1, Q_LORA, gains["g_cq"], "rms_cq_bwd")
    dckv, dg_ckv = _rms_bwd(dckvn, 0, rest, 0, KV_LORA, gains["g_ckv"], "rms_ckv_bwd")
    dqkv_d, dbias_d = [], []
    for i, d in enumerate(DILATIONS):
        comm = _ToChipsComm(ffn_a4[1:]) if (ffn_shards and i == 0) else None
        (dqkv_i, dbias_i), got = _dil_bwd(qkv_v[i], do_a_v[i], out_a_v[i], lse_a_v[i], bias, i, d,
                                          f"dil_bwd_{d}", comm)
        if comm is not None:
            ffn_r2 = list(ffn_r2) + list(got)
        dqkv_d.append(dqkv_i)
        dbias_d.append(dbias_i)
    dqkv = _sum_views_bf16(dqkv_d, "dil_bwd_sum").reshape(T, P_QKV)
    g_rel_bias = _bias_grad(dbias_d, buckets_dev, "rel_bias_grad")[:, :N_BUCKETS].T
    dproj = jnp.concatenate([dqkv, dckv, dkrw, dcq], axis=1)
    gw_in = _mm(h1f, dproj, "tn", F32, "mm_in_dw")
    mix_a4 = mix_r2 = None
    if ffn_shards:
        nat = [_w_in_from_kernel(gw_in), _w_uq_from_kernel(gw_uq), _w_ukv_from_kernel(gw_kv)]
        g8 = [_shards_from_full(g) for g in nat] + [gw_out]
        mix_a4 = _rs_first([g.reshape(N_DEV, -1, g.shape[-1]) for g in g8], "mix")
        dh1, mix_r2 = _mm(dproj, w_in, "nt", F32, "mm_in_dx", comm=_ToChipsComm(mix_a4))
    else:
        dh1 = _mm(dproj, w_in, "nt", F32, "mm_in_dx")
    dh1 = dh1.reshape(B, S, D)
    grad_x, dsh1, dsc1, dg_norm1 = _adaln_bwd(dh1, x, gains["g_norm1"], sc1, dx1, "adaln1_bwd")
    gmod = jnp.concatenate([dsh1, dsc1, dg1, dsh2, dsc2, dg2], axis=-1).reshape(B, N_MOD * D)
    grads = dict(w_in=gw_in, w_uq=gw_uq, w_kv=gw_kv, w_out=gw_out, w_ffn_in=gw_ffn_in, w_ffn_out=gw_ffn_out,
                 g_norm1=dg_norm1, g_cq=dg_cq, g_ckv=dg_ckv, rel_bias=g_rel_bias, g_out_a=dg_out_a,
                 g_out_b=dg_out_b, g_norm2=dg_norm2, g_final=dg_final, ffn_pending=(ffn_a4, ffn_r2),
                 mix_pending=(mix_a4, mix_r2))
    return loss, grad_x, gmod, grads


def _w_in_to_kernel(w):
    z = lambda n: jnp.zeros((w.shape[0], n), w.dtype)
    i3, i4, i5 = 3 * D_A, 3 * D_A + Q_LORA, 3 * D_A + Q_LORA + KV_LORA
    return jnp.concatenate([w[:, :i3], w[:, i4:i5], z(NOPE_DIM), w[:, i5:], z(LANES - NOPE_DIM - ROPE_DIM),
                            w[:, i3:i4]], axis=1)


def _w_in_from_kernel(g):
    o = P_QKV + KV_LORA
    return jnp.concatenate([g[:, :P_QKV], g[:, o + LANES:], g[:, P_QKV:o],
                            g[:, o + NOPE_DIM:o + NOPE_DIM + ROPE_DIM]], axis=1)


def _w_uq_to_kernel(w):
    w3 = w.reshape(Q_LORA, N_HEADS, NOPE_DIM + ROPE_DIM)
    return jnp.pad(w3, ((0, 0), (0, 0), (0, LANES - NOPE_DIM - ROPE_DIM))).reshape(Q_LORA, N_HEADS * LANES)


def _w_uq_from_kernel(g):
    return g.reshape(Q_LORA, N_HEADS, LANES)[:, :, :NOPE_DIM + ROPE_DIM].reshape(Q_LORA, -1)


def _w_ukv_to_kernel(w):
    w3 = w.reshape(KV_LORA, N_HEADS, 2 * HEAD_DIM)
    wk = jnp.pad(w3[:, :, :NOPE_DIM], ((0, 0), (0, 0), (0, LANES - NOPE_DIM))).reshape(KV_LORA, N_HEADS * LANES)
    wv = w3[:, :, NOPE_DIM:].reshape(KV_LORA, D_B)
    return jnp.concatenate([wk, wv], axis=1)


def _w_ukv_from_kernel(g):
    gk = g[:, :N_HEADS * LANES].reshape(KV_LORA, N_HEADS, LANES)[:, :, :NOPE_DIM]
    gv = g[:, N_HEADS * LANES:].reshape(KV_LORA, N_HEADS, HEAD_DIM)
    return jnp.concatenate([gk, gv], axis=2).reshape(KV_LORA, -1)


MESH = pl.DeviceIdType.MESH


def _my_place():
    return lax.axis_index("x"), lax.axis_index("y"), lax.axis_index("c")


def _other_chips(x, y):
    return [(1 - x, y), (x, 1 - y), (1 - x, 1 - y)]


def _allgather8(x_shard, name, in_hbm):
    m_per, n = x_shard.shape
    space = pl.ANY if in_hbm else pltpu.VMEM

    def body(x_ref, out_ref, send_sems, recv_sems, local_sem):
        x, y, c = _my_place()
        me, sibling = (x, y, c), (x, y, 1 - c)
        chips = _other_chips(x, y)

        def rows(px, py, pc):
            return out_ref.at[pl.ds((4 * px + 2 * py + pc) * m_per, m_per), :]

        def copy(k, block, to, src=None):
            return pltpu.make_async_remote_copy(
                src_ref=rows(*block) if src is None else src, dst_ref=rows(*block),
                send_sem=send_sems.at[k], recv_sem=recv_sems.at[k], device_id=to, device_id_type=MESH)

        mine = pltpu.make_async_copy(x_ref, rows(*me), local_sem)
        mine.start()
        first = [copy(0, me, sibling, src=x_ref)]
        first += [copy(1 + j, me, (*chip, c), src=x_ref) for j, chip in enumerate(chips)]
        for cp in first:
            cp.start()
        passed = [copy(4 + j, (*chip, c), sibling) for j, chip in enumerate(chips)]
        for j, chip in enumerate(chips):
            copy(1 + j, (*chip, c), me).wait_recv()
            passed[j].start()
        copy(0, sibling, me).wait_recv()
        for j, chip in enumerate(chips):
            copy(4 + j, (*chip, 1 - c), me).wait_recv()
        for cp in first + passed:
            cp.wait_send()
        mine.wait()

    return pl.pallas_call(
        body, name=name,
        out_shape=jax.ShapeDtypeStruct((N_DEV * m_per, n), x_shard.dtype),
        in_specs=[pl.BlockSpec(memory_space=space)],
        out_specs=pl.BlockSpec(memory_space=space),
        scratch_shapes=[pltpu.SemaphoreType.DMA((7,)), pltpu.SemaphoreType.DMA((7,)), pltpu.SemaphoreType.DMA],
        compiler_params=pltpu.CompilerParams(vmem_limit_bytes=VMEM_LIMIT),
    )(x_shard)


def _hbm_specs(n):
    return [pl.BlockSpec(memory_space=pl.ANY)] * n


class _GatherComm:
    def __init__(self, shards):
        self.n = n = len(shards)
        self.inputs = [s.reshape(2, s.shape[0] // 2, s.shape[1]) for s in shards]
        self.out_shape = [jax.ShapeDtypeStruct((N_DEV,) + s.shape[1:], s.dtype) for s in self.inputs]
        self.scratch = [pltpu.SemaphoreType.DMA((7 * n,)), pltpu.SemaphoreType.DMA((7 * n,))]

    def _parts(self, xs, outs, sems):
        send_sems, recv_sems = sems
        x, y, c = _my_place()

        def blk(k, px, py, pc):
            return outs[k].at[4 * px + 2 * py + pc]

        def copy(k, kind, block, to, own=False):
            return pltpu.make_async_remote_copy(
                src_ref=xs[k].at[c] if own else blk(k, *block), dst_ref=blk(k, *block),
                send_sem=send_sems.at[7 * k + kind], recv_sem=recv_sems.at[7 * k + kind],
                device_id=to, device_id_type=MESH)

        def whole(k):
            return pltpu.make_async_remote_copy(
                src_ref=xs[k], dst_ref=outs[k].at[pl.ds(4 * x + 2 * y, 2)],
                send_sem=send_sems.at[7 * k], recv_sem=recv_sems.at[7 * k],
                device_id=(x, y, 1 - c), device_id_type=MESH)

        me, sibling = (x, y, c), (x, y, 1 - c)
        chips = _other_chips(x, y)
        first = []
        for k in range(self.n):
            first.append(whole(k))
            first += [copy(k, 1 + j, me, (*chip, c), own=True) for j, chip in enumerate(chips)]
        return copy, whole, me, sibling, chips, c, first

    def start(self, xs, outs, sems):
        for cp in self._parts(xs, outs, sems)[-1]:
            cp.start()

    def finish(self, xs, outs, sems):
        copy, whole, me, sibling, chips, c, first = self._parts(xs, outs, sems)
        passed = []
        for j, chip in enumerate(chips):
            for k in range(self.n):
                copy(k, 1 + j, (*chip, c), me).wait_recv()
                fwd = copy(k, 4 + j, (*chip, c), sibling)
                fwd.start()
                passed.append(fwd)
        for k in range(self.n):
            whole(k).wait_recv()
        for j, chip in enumerate(chips):
            for k in range(self.n):
                copy(k, 4 + j, (*chip, 1 - c), me).wait_recv()
        for cp in first + passed:
            cp.wait_send()


class _ToChipsComm:
    def __init__(self, a4s):
        self.inputs = list(a4s)
        self.n = n = len(a4s)
        nc = N_CHIP - 1
        self.out_shape = [jax.ShapeDtypeStruct((nc,) + a.shape[1:], a.dtype) for a in a4s]
        self.scratch = [pltpu.SemaphoreType.DMA((nc * n,)), pltpu.SemaphoreType.DMA((nc * n,))]

    def _copies(self, as_, rs, sems):
        send_sems, recv_sems = sems
        x, y, c = _my_place()
        nc = N_CHIP - 1
        return [pltpu.make_async_remote_copy(
            src_ref=as_[k].at[2 * cx + cy], dst_ref=rs[k].at[j], send_sem=send_sems.at[nc * k + j],
            recv_sem=recv_sems.at[nc * k + j], device_id=(cx, cy, c), device_id_type=MESH)
            for k in range(self.n) for j, (cx, cy) in enumerate(_other_chips(x, y))]

    def start(self, as_, rs, sems):
        for cp in self._copies(as_, rs, sems):
            cp.start()

    def finish(self, as_, rs, sems):
        for cp in self._copies(as_, rs, sems):
            cp.wait()


def _run_comm(comm, name):
    n = comm.n

    def body(*refs):
        ins, outs, sems = refs[:n], refs[n:2 * n], refs[2 * n:]
        comm.start(ins, outs, sems)
        comm.finish(ins, outs, sems)

    return pl.pallas_call(
        body, name=name, out_shape=comm.out_shape, in_specs=_hbm_specs(n), out_specs=_hbm_specs(n),
        scratch_shapes=comm.scratch,
    )(*comm.inputs)


def _gather_weights(shards, name):
    return _run_comm(_GatherComm(shards), name)


def _rs_to_sibling(g8s, name):
    n = len(g8s)

    def body(*refs):
        gs, rs = refs[:n], refs[n:2 * n]
        send_sems, recv_sems = refs[2 * n:]
        x, y, c = _my_place()
        copies = [pltpu.make_async_remote_copy(
            src_ref=gs[k].at[2 * s + 1 - c], dst_ref=rs[k].at[s], send_sem=send_sems.at[N_CHIP * k + s],
            recv_sem=recv_sems.at[N_CHIP * k + s], device_id=(x, y, 1 - c), device_id_type=MESH)
            for k in range(n) for s in range(N_CHIP)]
        for cp in copies:
            cp.start()
        for cp in copies:
            cp.wait()

    return pl.pallas_call(
        body, name=name,
        out_shape=[jax.ShapeDtypeStruct((N_CHIP,) + g.shape[1:], g.dtype) for g in g8s],
        in_specs=_hbm_specs(n), out_specs=_hbm_specs(n),
        scratch_shapes=[pltpu.SemaphoreType.DMA((N_CHIP * n,)), pltpu.SemaphoreType.DMA((N_CHIP * n,))],
    )(*g8s)


def _rs_to_chips(a4s, name):
    return _run_comm(_ToChipsComm(a4s), name)


def _swap_halves(hs, name):
    n = len(hs)

    def body(*refs):
        o_refs = refs[n:2 * n]
        send_sems, recv_sems = refs[2 * n:]
        x, y, c = _my_place()

        def remote(k, slot):
            return pltpu.make_async_remote_copy(
                src_ref=o_refs[k].at[slot], dst_ref=o_refs[k].at[slot], send_sem=send_sems.at[k],
                recv_sem=recv_sems.at[k], device_id=(x, y, 1 - c), device_id_type=MESH)

        sends = [remote(k, c) for k in range(n)]
        for cp in sends:
            cp.start()
        for k in range(n):
            remote(k, 1 - c).wait_recv()
        for cp in sends:
            cp.wait_send()

    return pl.pallas_call(
        body, name=name,
        out_shape=[jax.ShapeDtypeStruct(h.shape, h.dtype) for h in hs],
        in_specs=_hbm_specs(n), out_specs=_hbm_specs(n),
        input_output_aliases={k: k for k in range(n)},
        scratch_shapes=[pltpu.SemaphoreType.DMA((n,)), pltpu.SemaphoreType.DMA((n,))],
    )(*hs)


ADD_TILES = 4


def _add_blocks(a_list, a_idx_fn, others_list, ns, sel, name, out_blocks=None, out_idx_fn=None):
    out_blocks = out_blocks or ns
    out_idx_fn = out_idx_fn or (lambda s, sel_ref: s)
    n = len(a_list)
    n_o = len(others_list[0])
    per = 1 + n_o

    def body(sel_ref, *refs):
        for k in range(n):
            ins = refs[k * per:(k + 1) * per]
            o_ref = refs[n * per + k]
            acc = ins[0][0]
            for r in ins[1:]:
                acc = acc + r[0]
            o_ref[0] = acc

    in_specs, args, out_specs, out_shape = [], [], [], []
    for a, others in zip(a_list, others_list):
        _, R, N = a.shape
        tr = R // ADD_TILES
        assert tr % 8 == 0, a.shape
        in_specs.append(pl.BlockSpec((1, tr, N), lambda s, i, sel_ref: (a_idx_fn(s, sel_ref), i, 0)))
        args.append(a)
        for arr, fixed in others:
            if fixed is None:
                in_specs.append(pl.BlockSpec((1, tr, N), lambda s, i, sel_ref: (s, i, 0)))
            else:
                in_specs.append(pl.BlockSpec((1, tr, N), lambda s, i, sel_ref, fixed=fixed: (fixed, i, 0)))
            args.append(arr)
        out_specs.append(pl.BlockSpec((1, tr, N), lambda s, i, sel_ref: (out_idx_fn(s, sel_ref), i, 0)))
        out_shape.append(jax.ShapeDtypeStruct((out_blocks, R, N), a.dtype))
    grid_spec = pltpu.PrefetchScalarGridSpec(num_scalar_prefetch=1, grid=(ns, ADD_TILES), in_specs=in_specs,
                                             out_specs=out_specs)
    return pl.pallas_call(
        body, name=name, out_shape=out_shape, grid_spec=grid_spec,
        compiler_params=_cparams(("parallel", "parallel")),
    )(sel, *args)


def _rs_first(g8s, tag):
    c_sel = jnp.reshape(lax.axis_index("c"), (1,)).astype(jnp.int32)
    r1 = _rs_to_sibling(g8s, f"rs_to_sibling_{tag}")
    return _add_blocks(g8s, lambda s, sel: 2 * s + sel[0], [[(r, None)] for r in r1], N_CHIP, c_sel,
                       f"rs_add_sibling_{tag}")


def _rs_last(a4s, r2s, tag):
    sel = jnp.stack([2 * lax.axis_index("x") + lax.axis_index("y"), lax.axis_index("c")]).astype(jnp.int32)
    h = _add_blocks(a4s, lambda s, sel: sel[0], [[(r, 0), (r, 1), (r, 2)] for r in r2s], 1, sel,
                    f"rs_add_chips_{tag}", out_blocks=2, out_idx_fn=lambda s, sel: sel[1])
    full = _swap_halves(h, f"rs_swap_halves_{tag}")
    return [f.reshape(2 * f.shape[1], f.shape[2]) for f in full]


def _reduce_scatter(g8s, tag):
    a4 = _rs_first(g8s, tag)
    return _rs_last(a4, _rs_to_chips(a4, f"rs_to_chips_{tag}"), tag)


def _ada_fwd(c_all, w_ada, b_ada, name):
    nb, D = c_all.shape
    ncol = w_ada.shape[1]
    tc = 512

    def body(c_ref, w_ref, b_ref, o_ref):
        cv = c_ref[...]
        cond = (cv * jax.nn.sigmoid(cv)).astype(BF16)
        o_ref[...] = jnp.dot(cond, w_ref[...].astype(BF16), preferred_element_type=F32) + b_ref[...]

    return pl.pallas_call(
        body, name=name, out_shape=jax.ShapeDtypeStruct((nb, ncol), F32), grid=(ncol // tc,),
        in_specs=[pl.BlockSpec((nb, D), lambda j: (0, 0)), pl.BlockSpec((D, tc), lambda j: (0, j)),
                  pl.BlockSpec((1, tc), lambda j: (0, j))],
        out_specs=pl.BlockSpec((nb, tc), lambda j: (0, j)),
        compiler_params=_cparams(("parallel",)),
    )(c_all, w_ada, b_ada)


def _ada_bwd(c_all, gmod_cols, name):
    nb, D = c_all.shape
    ncol = gmod_cols.shape[1]
    tc = 512

    def body(c_ref, g_ref, o_ref):
        cv = c_ref[...]
        cond = (cv * jax.nn.sigmoid(cv)).astype(BF16)
        o_ref[...] = _dot_tn(cond, g_ref[...].astype(BF16))

    return pl.pallas_call(
        body, name=name, out_shape=jax.ShapeDtypeStruct((D, ncol), F32), grid=(ncol // tc,),
        in_specs=[pl.BlockSpec((nb, D), lambda j: (0, 0)), pl.BlockSpec((nb, tc), lambda j: (0, j))],
        out_specs=pl.BlockSpec((D, tc), lambda j: (0, j)),
        compiler_params=_cparams(("parallel",)),
    )(c_all, gmod_cols)


def _adam_math(w, g, m, v):
    m = ADAM_B1 * m + (1.0 - ADAM_B1) * g
    v = ADAM_B2 * v + (1.0 - ADAM_B2) * (g * g)
    m_hat = m / (1.0 - ADAM_B1 ** ADAM_STEP)
    v_hat = v / (1.0 - ADAM_B2 ** ADAM_STEP)
    delta = -ADAM_LR * (m_hat / (jnp.sqrt(v_hat) + ADAM_EPS) + ADAM_WD * w)
    return delta, m, v


def _adamw(w, g, m, v, name):
    rows, cols = w.shape
    tr = _pick(rows, (256, 192, 176, 128, 64, 8))

    def body(w_ref, g_ref, m_ref, v_ref, d_ref, mo_ref, vo_ref):
        d, mn, vn = _adam_math(w_ref[...], g_ref[...], m_ref[...], v_ref[...])
        d_ref[...] = d
        mo_ref[...] = mn
        vo_ref[...] = vn

    spec = pl.BlockSpec((tr, cols), lambda i: (i, 0))
    return pl.pallas_call(
        body, name=name, out_shape=[jax.ShapeDtypeStruct((rows, cols), F32)] * 3, grid=(rows // tr,),
        in_specs=[spec] * 4, out_specs=[spec] * 3, compiler_params=_cparams(("parallel",)),
    )(w, g, m, v)


VEC_ROWS = 8


def _adamw_rows(w, parts, m, v, name):
    n = w.shape[1]
    P = parts.shape[0]
    assert n % (VEC_ROWS * LANES) == 0, n
    shp = (VEC_ROWS, n // VEC_ROWS)

    def body(w_ref, p_ref, m_ref, v_ref, g_ref, d_ref, mo_ref, vo_ref):
        g = p_ref[0]
        for k in range(1, P):
            g = g + p_ref[k]
        d, mn, vn = _adam_math(w_ref[...], g, m_ref[...], v_ref[...])
        g_ref[...] = g
        d_ref[...] = d
        mo_ref[...] = mn
        vo_ref[...] = vn

    vec = pl.BlockSpec(shp, lambda i: (0, 0))
    out = pl.pallas_call(
        body, name=name, out_shape=[jax.ShapeDtypeStruct(shp, F32)] * 4, grid=(1,),
        in_specs=[vec, pl.BlockSpec((P,) + shp, lambda i: (0, 0, 0)), vec, vec], out_specs=[vec] * 4,
        compiler_params=_cparams(("arbitrary",)),
    )(w.reshape(shp), parts.reshape((P,) + shp), m.reshape(shp), v.reshape(shp))
    return [o.reshape(1, n) for o in out]


_SHARDED = ("w_in", "w_uq", "w_ukv", "w_out", "w_ffn_in", "w_ffn_out")
_SMALL = (("g_norm1", 1024), ("g_cq", 384), ("g_ckv", 256), ("rel_bias", 256), ("g_out_a", 512),
          ("g_out_b", 512), ("g_norm2", 1024), ("g_final", 1024))
_SMALL_PAD = 5120


def _full_from_shards(sh):
    return jnp.transpose(sh, (1, 0, 2)).reshape(sh.shape[1], -1)


def _shards_from_full(full):
    rows, cols = full.shape
    return jnp.transpose(full.reshape(rows, N_CHIP, cols // N_CHIP), (1, 0, 2))


def kernel(x, c, w_ada, b_ada, g_norm1, w_in, g_cq, w_uq, g_ckv, w_ukv, rel_bias, g_out_a, g_out_b, w_out, g_norm2, w_ffn_in, w_ffn_out, g_final, loss_target, m_w_ada, m_b_ada, m_g_norm1, m_w_in, m_g_cq, m_w_uq, m_g_ckv, m_w_ukv, m_rel_bias, m_g_out_a, m_g_out_b, m_w_out, m_g_norm2, m_w_ffn_in, m_w_ffn_out, m_g_final, v_w_ada, v_b_ada, v_g_norm1, v_w_in, v_g_cq, v_w_uq, v_g_ckv, v_w_ukv, v_rel_bias, v_g_out_a, v_g_out_b, v_w_out, v_g_norm2, v_w_ffn_in, v_w_ffn_out, v_g_final):
    names = ["w_ada", "b_ada", "g_norm1", "w_in", "g_cq", "w_uq", "g_ckv", "w_ukv", "rel_bias", "g_out_a",
             "g_out_b", "w_out", "g_norm2", "w_ffn_in", "w_ffn_out", "g_final"]
    W = dict(zip(names, [w_ada, b_ada, g_norm1, w_in, g_cq, w_uq, g_ckv, w_ukv, rel_bias, g_out_a, g_out_b,
                         w_out, g_norm2, w_ffn_in, w_ffn_out, g_final]))
    M = dict(zip(names, [m_w_ada, m_b_ada, m_g_norm1, m_w_in, m_g_cq, m_w_uq, m_g_ckv, m_w_ukv, m_rel_bias,
                         m_g_out_a, m_g_out_b, m_w_out, m_g_norm2, m_w_ffn_in, m_w_ffn_out, m_g_final]))
    V = dict(zip(names, [v_w_ada, v_b_ada, v_g_norm1, v_w_in, v_g_cq, v_w_uq, v_g_ckv, v_w_ukv, v_rel_bias,
                         v_g_out_a, v_g_out_b, v_w_out, v_g_norm2, v_w_ffn_in, v_w_ffn_out, v_g_final]))
    B, S, D = x.shape
    mx, my, mc = _my_place()
    dev = 4 * mx + 2 * my + mc
    chip = 2 * mx + my
    pad_rows = 8

    c_all = _allgather8(jnp.pad(c, ((0, pad_rows - B), (0, 0))), "ag_c", False)
    c_all = c_all.reshape(N_DEV, pad_rows, D)[:, :B].reshape(N_DEV * B, D)
    ada_cols = w_ada.shape[-1]
    b_cols = lax.dynamic_slice_in_dim(b_ada, chip * ada_cols, ada_cols, axis=1)
    mod_cols = _ada_fwd(c_all, w_ada[0], b_cols, "ada_fwd")
    mod_all = _allgather8(mod_cols, "ag_mod", False).reshape(N_DEV, N_DEV * B, ada_cols)[0::2]
    mod_all = jnp.transpose(mod_all, (1, 0, 2)).reshape(N_DEV * B, N_MOD * D)
    mod = lax.dynamic_slice_in_dim(mod_all, dev * B, B, axis=0)

    early = ("w_in", "w_uq", "w_ukv", "w_out")
    got = _gather_weights([W[n][0].astype(BF16) for n in early], "ag_weights")
    full = {n: g.reshape((N_CHIP,) + W[n].shape[1:]) for n, g in zip(early, got)}
    wts = dict(w_in=_w_in_to_kernel(_full_from_shards(full["w_in"])),
               w_uq=_w_uq_to_kernel(_full_from_shards(full["w_uq"])),
               w_kv=_w_ukv_to_kernel(_full_from_shards(full["w_ukv"])), w_out=full["w_out"].reshape(D, D))
    gains = dict(g_norm1=g_norm1, g_cq=g_cq, g_ckv=g_ckv, g_out_a=g_out_a, g_out_b=g_out_b, g_norm2=g_norm2,
                 g_final=g_final.reshape(1, D))

    loss, grad_x, gmod, grads = _local_step(x, loss_target, mod, wts, gains, rel_bias,
                                            ffn_shards=[w_ffn_in[0].astype(BF16), w_ffn_out[0].astype(BF16)])
    loss = lax.psum(loss[0, 0], ("x", "y", "c"))

    n_small = _SMALL_PAD
    cat = lambda dct: jnp.concatenate([dct[n].reshape(1, -1) for n, _ in _SMALL]
                                      + [jnp.zeros((1, _SMALL_PAD - sum(s for _, s in _SMALL)), F32)], axis=1)
    small = cat(grads)
    rows = jnp.concatenate([gmod, jnp.pad(small, ((0, 0), (0, N_MOD * D - n_small))),
                            jnp.zeros((pad_rows - B - 1, N_MOD * D), F32)], axis=0)
    rows_all = _allgather8(rows, "ag_small", False).reshape(N_DEV, pad_rows, N_MOD * D)
    gmod_all = rows_all[:, :B].reshape(N_DEV * B, N_MOD * D)
    small_parts = rows_all[:, B, :n_small]

    a4, r2 = grads["mix_pending"]
    ffn_a4, ffn_r2 = grads["ffn_pending"]
    G = dict(zip(_SHARDED, _rs_last(list(a4) + list(ffn_a4), list(r2) + list(ffn_r2), "all")))

    gmod_cols = lax.dynamic_slice_in_dim(gmod_all, chip * ada_cols, ada_cols, axis=1)
    G["w_ada"] = _ada_bwd(c_all, gmod_cols, "ada_bwd")
    delta, new_m, new_v = {}, {}, {}
    for n in ("w_ada",) + _SHARDED:
        shp = W[n].shape
        w2 = W[n].reshape(shp[-2], shp[-1])
        d_, m_, v_ = _adamw(w2, G[n], M[n].reshape(w2.shape), V[n].reshape(w2.shape), f"adamw_{n}")
        G[n], delta[n], new_m[n], new_v[n] = [a.reshape(shp) for a in (G[n], d_, m_, v_)]
    gs, ds_, ms_, vs_ = _adamw_rows(cat(W), small_parts, cat(M), cat(V), "adamw_small")
    off = 0
    for n, sz in _SMALL:
        shp = W[n].shape
        G[n], delta[n], new_m[n], new_v[n] = [a[:, off:off + sz].reshape(shp) for a in (gs, ds_, ms_, vs_)]
        off += sz
    G["b_ada"], delta["b_ada"], new_m["b_ada"], new_v["b_ada"] = _adamw_rows(b_ada, gmod_all, m_b_ada, v_b_ada,
                                                                          "adamw_b_ada")
    return (loss, grad_x, *[G[n] for n in names], *[delta[n] for n in names], *[new_m[n] for n in names],
            *[new_v[n] for n in names])
```

```python
import functools
import math

import numpy as np
import jax
import jax.numpy as jnp
from jax import lax
from jax.experimental import pallas as pl
from jax.experimental.pallas import tpu as pltpu

F32 = jnp.float32
BF16 = jnp.bfloat16

D_MODEL = 1024
SEQ = 2048
N_HEADS = 8
HEAD_DIM = 64
D_A = 512
D_B = 512
Q_LORA = 384
KV_LORA = 256
ROPE_DIM = 32
NOPE_DIM = 64
D_FF = 2816
N_MOD = 6
N_BUCKETS = 32
MAX_DISTANCE = 2048
ROPE_THETA = 10000.0
EPS = 1e-6
NEG = -1e30
BLK = 128
DILATIONS = (1, 4, 16)
SPAN = 128
MLA_SCALE = (NOPE_DIM + ROPE_DIM) ** -0.5
DIL_SCALE = HEAD_DIM ** -0.5

ADAM_LR = 0.001
ADAM_B1 = 0.9
ADAM_B2 = 0.999
ADAM_EPS = 1e-08
ADAM_WD = 0.01
ADAM_STEP = 10

N_DEV = 8
N_CHIP = 4
LANES = 128
VMEM_LIMIT = 48 * 1024 * 1024

P_QKV = 3 * D_A
P_REST = KV_LORA + LANES + Q_LORA


def _cparams(sem=None):
    return pltpu.CompilerParams(dimension_semantics=sem, vmem_limit_bytes=VMEM_LIMIT)


def _pick(n, cands):
    for c in cands:
        if n % c == 0:
            return c
    raise ValueError(f"no tile for {n} in {cands}")


def _mm(a, b, mode, out_dtype, name, col_blocks=None, comm=None, halves=False):
    blocked = col_blocks is not None
    if mode == "nn":
        (M, K) = a.shape
        K2, N = (b.shape[1], b.shape[0] * b.shape[2]) if blocked else b.shape
    elif mode == "nt":
        (M, K) = (a.shape[1], 2 * a.shape[2]) if halves else a.shape
        N, K2 = (b.shape[1], b.shape[0] * b.shape[2]) if blocked else b.shape
    else:
        (K, M) = a.shape
        K2, N = (b.shape[1], 2 * b.shape[2]) if halves else b.shape
    assert K == K2, (a.shape, b.shape, mode)
    assert not halves or (blocked and col_blocks == 4 and mode in ("nt", "tn"))
    tm = _pick(M, (1408, 512, 384, 256, 128) if mode == "tn" else (512, 384, 256, 128))
    tn = _pick(N, (1408, 1024, 768, 512, 384, 256, 128))
    tk = _pick(K, (1408, 1152, 1024, 768, 512, 384, 256, 128))
    if blocked and mode == "nt":
        tk = K // col_blocks
    elif blocked:
        tn = N // col_blocks
    nk = K // tk
    out_shape = (M, N)
    out_spec = pl.BlockSpec((tm, tn), lambda i, j, k: (i, j))
    if mode == "nn":
        a_spec = pl.BlockSpec((tm, tk), lambda i, j, k: (i, k))
        b_spec = (pl.BlockSpec((None, tk, tn), lambda i, j, k: (j, k, 0)) if blocked
                  else pl.BlockSpec((tk, tn), lambda i, j, k: (k, j)))
        dn = (((1,), (0,)), ((), ()))
    elif mode == "nt":
        a_spec = (pl.BlockSpec((None, tm, tk), lambda i, j, k: (k // 2, i, k % 2)) if halves
                  else pl.BlockSpec((tm, tk), lambda i, j, k: (i, k)))
        b_spec = (pl.BlockSpec((None, tn, tk), lambda i, j, k: (k, j, 0)) if blocked
                  else pl.BlockSpec((tn, tk), lambda i, j, k: (j, k)))
        dn = (((1,), (1,)), ((), ()))
    else:
        a_spec = pl.BlockSpec((tk, tm), lambda i, j, k: (k, i))
        b_spec = (pl.BlockSpec((None, tk, tn), lambda i, j, k: (j // 2, k, j % 2)) if halves
                  else pl.BlockSpec((tk, tn), lambda i, j, k: (k, j)))
        dn = (((0,), (0,)), ((), ()))
        if blocked:
            out_shape = (col_blocks, M, tn)
            out_spec = pl.BlockSpec((None, tm, tn), lambda i, j, k: (j, i, 0))

    def body(a_ref, b_ref, o_ref, acc_ref):
        k = pl.program_id(2)

        @pl.when(k == 0)
        def _():
            acc_ref[...] = jnp.zeros_like(acc_ref)

        acc_ref[...] += lax.dot_general(a_ref[...].astype(BF16), b_ref[...].astype(BF16), dn,
                                        preferred_element_type=F32)

        @pl.when(k == nk - 1)
        def _():
            o_ref[...] = acc_ref[...].astype(o_ref.dtype)

    if comm is not None:
        (out,), got = _host_call(
            body, comm, name=name, out_shape=[jax.ShapeDtypeStruct(out_shape, out_dtype)],
            grid=(M // tm, N // tn, nk), in_specs=[a_spec, b_spec], out_specs=[out_spec],
            scratch_shapes=[pltpu.VMEM((tm, tn), F32)], args=(a, b))
        return out, got
    return pl.pallas_call(
        body, name=name,
        out_shape=jax.ShapeDtypeStruct(out_shape, out_dtype),
        grid=(M // tm, N // tn, nk),
        in_specs=[a_spec, b_spec],
        out_specs=out_spec,
        scratch_shapes=[pltpu.VMEM((tm, tn), F32)],
        compiler_params=_cparams(("parallel", "parallel", "arbitrary")),
    )(a, b)


ROW_TILE = 256


def _adaln_fwd(x, g, sc, sh, name, mix=None, gate=None):
    B, S, D = x.shape
    ts = ROW_TILE
    has_res = mix is not None

    def body(*refs):
        if has_res:
            x_ref, g_ref, sc_ref, sh_ref, mix_ref, gate_ref, h_ref, xr_ref = refs
            xr = x_ref[0] + gate_ref[0] * mix_ref[0]
            xr_ref[0] = xr
        else:
            x_ref, g_ref, sc_ref, sh_ref, h_ref = refs
            xr = x_ref[0]
        r = lax.rsqrt(jnp.mean(xr * xr, axis=-1, keepdims=True) + EPS)
        xn = (xr * r) * g_ref[...]
        h_ref[0] = (xn * (1.0 + sc_ref[0]) + sh_ref[0]).astype(h_ref.dtype)

    tok = pl.BlockSpec((1, ts, D), lambda b, s: (b, s, 0))
    per_b = pl.BlockSpec((1, 1, D), lambda b, s: (b, 0, 0))
    vec = pl.BlockSpec((1, D), lambda b, s: (0, 0))
    in_specs = [tok, vec, per_b, per_b]
    args = [x, g, sc, sh]
    out_shape = [jax.ShapeDtypeStruct((B, S, D), BF16)]
    out_specs = [tok]
    if has_res:
        in_specs += [tok, per_b]
        args += [mix, gate]
        out_shape.append(jax.ShapeDtypeStruct((B, S, D), F32))
        out_specs.append(tok)
    out = pl.pallas_call(
        body, name=name, out_shape=out_shape, grid=(B, S // ts),
        in_specs=in_specs, out_specs=out_specs,
        compiler_params=_cparams(("parallel", "parallel")),
    )(*args)
    return out if has_res else out[0]


def _adaln_bwd(dh, x, g, sc, dres, name, mix=None, gate=None, comm=None):
    B, S, D = x.shape
    ts = ROW_TILE
    has_res = mix is not None

    def body(*refs):
        if has_res:
            (dh_ref, x_ref, g_ref, sc_ref, dres_ref, mix_ref, gate_ref,
             dx_ref, dsh_ref, dsc_ref, dg_ref, dgate_ref, dmix_ref) = refs
        else:
            (dh_ref, x_ref, g_ref, sc_ref, dres_ref, dx_ref, dsh_ref, dsc_ref, dg_ref) = refs
        b, s = pl.program_id(0), pl.program_id(1)
        xv = x_ref[0]
        dhv = dh_ref[0]
        gv = g_ref[...]
        r = lax.rsqrt(jnp.mean(xv * xv, axis=-1, keepdims=True) + EPS)
        n = xv * r
        xn = n * gv
        dxn = dhv * (1.0 + sc_ref[0])
        dn = dxn * gv
        dx = r * (dn - n * jnp.mean(dn * n, axis=-1, keepdims=True)) + dres_ref[0]
        dx_ref[0] = dx

        @pl.when(s == 0)
        def _():
            dsh_ref[...] = jnp.zeros_like(dsh_ref)
            dsc_ref[...] = jnp.zeros_like(dsc_ref)
            if has_res:
                dgate_ref[...] = jnp.zeros_like(dgate_ref)

        @pl.when((s == 0) & (b == 0))
        def _():
            dg_ref[...] = jnp.zeros_like(dg_ref)

        dsh_ref[0] += jnp.sum(dhv, axis=0, keepdims=True)
        dsc_ref[0] += jnp.sum(dhv * xn, axis=0, keepdims=True)
        dg_ref[...] += jnp.sum(dxn * n, axis=0, keepdims=True)
        if has_res:
            dgate_ref[0] += jnp.sum(dx * mix_ref[0], axis=0, keepdims=True)
            dmix_ref[0] = (dx * gate_ref[0]).astype(dmix_ref.dtype)

    tok = pl.BlockSpec((1, ts, D), lambda b, s: (b, s, 0))
    per_b = pl.BlockSpec((1, 1, D), lambda b, s: (b, 0, 0))
    vec = pl.BlockSpec((1, D), lambda b, s: (0, 0))
    in_specs = [tok, tok, vec, per_b, tok]
    args = [dh, x, g, sc, dres]
    out_shape = [jax.ShapeDtypeStruct((B, S, D), F32), jax.ShapeDtypeStruct((B, 1, D), F32),
                 jax.ShapeDtypeStruct((B, 1, D), F32), jax.ShapeDtypeStruct((1, D), F32)]
    out_specs = [tok, per_b, per_b, vec]
    if has_res:
        in_specs += [tok, per_b]
        args += [mix, gate]
        out_shape += [jax.ShapeDtypeStruct((B, 1, D), F32), jax.ShapeDtypeStruct((B, S, D), BF16)]
        out_specs += [per_b, tok]
    res, got = _host_call(body, comm, name=name, out_shape=out_shape, grid=(B, S // ts), in_specs=in_specs,
                          out_specs=out_specs, scratch_shapes=[], args=args)
    return (list(res) + [got]) if comm is not None else res


def _rms_fwd(x, col_blk, n, g, name, n_real=None):
    T = x.shape[0]
    tr = 512
    nr = float(n_real or n)

    def body(x_ref, g_ref, y_ref):
        xv = x_ref[...]
        r = lax.rsqrt(jnp.sum(xv * xv, axis=-1, keepdims=True) / nr + EPS)
        y_ref[...] = ((xv * r) * g_ref[...]).astype(y_ref.dtype)

    return pl.pallas_call(
        body, name=name, out_shape=jax.ShapeDtypeStruct((T, n), BF16), grid=(T // tr,),
        in_specs=[pl.BlockSpec((tr, n), lambda i: (i, col_blk)), pl.BlockSpec((1, n), lambda i: (0, 0))],
        out_specs=pl.BlockSpec((tr, n), lambda i: (i, 0)),
        compiler_params=_cparams(("parallel",)),
    )(x, g)


def _rms_bwd(dy, dy_blk, x, x_blk, n, g, name, out_dtype=BF16):
    T = x.shape[0]
    tr = 512

    def body(dy_ref, x_ref, g_ref, dx_ref, dg_ref):
        xv = x_ref[...]
        dyv = dy_ref[...].astype(F32)
        r = lax.rsqrt(jnp.mean(xv * xv, axis=-1, keepdims=True) + EPS)
        nrm = xv * r
        dn = dyv * g_ref[...]
        dx_ref[...] = (r * (dn - nrm * jnp.mean(dn * nrm, axis=-1, keepdims=True))).astype(dx_ref.dtype)

        @pl.when(pl.program_id(0) == 0)
        def _():
            dg_ref[...] = jnp.zeros_like(dg_ref)

        dg_ref[...] += jnp.sum(dyv * nrm, axis=0, keepdims=True)

    return pl.pallas_call(
        body, name=name,
        out_shape=[jax.ShapeDtypeStruct((T, n), out_dtype), jax.ShapeDtypeStruct((1, n), F32)],
        grid=(T // tr,),
        in_specs=[pl.BlockSpec((tr, n), lambda i: (i, dy_blk)), pl.BlockSpec((tr, n), lambda i: (i, x_blk)),
                  pl.BlockSpec((1, n), lambda i: (0, 0))],
        out_specs=[pl.BlockSpec((tr, n), lambda i: (i, 0)), pl.BlockSpec((1, n), lambda i: (0, 0))],
        compiler_params=_cparams(("arbitrary",)),
    )(dy, x, g)


def _rms_bwd_views(dy, dy_blk, x, g, name):
    B, S, n = x.shape
    tiles = S // VIEW_TILE

    def body(dy_ref, x_ref, g_ref, d1_ref, d4_ref, d16_ref, dg_ref, dx_s):
        xv = x_ref[0]
        dyv = dy_ref[...]
        r = lax.rsqrt(jnp.mean(xv * xv, axis=-1, keepdims=True) + EPS)
        nrm = xv * r
        dn = dyv * g_ref[...]
        dx = r * (dn - nrm * jnp.mean(dn * nrm, axis=-1, keepdims=True))
        d1_ref[0] = dx.astype(d1_ref.dtype)
        _put_tile(dx_s, dx)
        _tile_to_view(dx_s, d4_ref, DILATIONS[1], n)
        _tile_to_view(dx_s, d16_ref, DILATIONS[2], n)

        @pl.when((pl.program_id(0) == 0) & (pl.program_id(1) == 0))
        def _():
            dg_ref[...] = jnp.zeros_like(dg_ref)

        dg_ref[...] += jnp.sum(dyv * nrm, axis=0, keepdims=True)

    res = pl.pallas_call(
        body, name=name,
        out_shape=[_view_shape(B, S, d, n, BF16) for d in DILATIONS] + [jax.ShapeDtypeStruct((1, n), F32)],
        grid=(B, tiles),
        in_specs=[pl.BlockSpec((VIEW_TILE, n), lambda b, t: (b * tiles + t, dy_blk)), _view_spec(1, n),
                  pl.BlockSpec((1, n), lambda b, t: (0, 0))],
        out_specs=[_view_spec(d, n) for d in DILATIONS] + [pl.BlockSpec((1, n), lambda b, t: (0, 0))],
        scratch_shapes=[_tile_scratch(n)],
        compiler_params=_cparams(("arbitrary", "arbitrary")),
    )(dy, x, g)
    return res[:len(DILATIONS)], res[len(DILATIONS)]


FFN_TILE = 1408


def _ffn_in_fwd(h, w4, name):
    T, D = h.shape
    tm, tc = 512, FFN_TILE
    nc = D_FF // tc

    def body(h_ref, wg_ref, wu_ref, gu_ref, act_ref):
        hv = h_ref[...]
        g = jnp.dot(hv, wg_ref[...], preferred_element_type=F32)
        u = jnp.dot(hv, wu_ref[...], preferred_element_type=F32)
        gu_ref[0] = g
        gu_ref[1] = u
        act_ref[...] = (g * jax.nn.sigmoid(g) * u).astype(act_ref.dtype)

    return pl.pallas_call(
        body, name=name,
        out_shape=[jax.ShapeDtypeStruct((2, T, D_FF), F32), jax.ShapeDtypeStruct((T, D_FF), BF16)],
        grid=(nc, T // tm),
        in_specs=[pl.BlockSpec((tm, D), lambda j, i: (i, 0)),
                  pl.BlockSpec((None, D, tc), lambda j, i: (j, 0, 0)),
                  pl.BlockSpec((None, D, tc), lambda j, i: (j + nc, 0, 0))],
        out_specs=[pl.BlockSpec((2, tm, tc), lambda j, i: (0, i, j)), pl.BlockSpec((tm, tc), lambda j, i: (i, j))],
        compiler_params=_cparams(("parallel", "parallel")),
    )(h, w4, w4)


def _ffn_out_bwd(df, w_out, gu, name):
    T, D = df.shape
    tm, tc = 512, FFN_TILE

    def body(df_ref, w_ref, gu_ref, dgu_ref):
        da = _dot_nt(df_ref[...], w_ref[...])
        g, u = gu_ref[0], gu_ref[1]
        sg = jax.nn.sigmoid(g)
        dgu_ref[0] = (da * u * (sg * (1.0 + g * (1.0 - sg)))).astype(dgu_ref.dtype)
        dgu_ref[1] = (da * (g * sg)).astype(dgu_ref.dtype)

    halves = pl.BlockSpec((2, tm, tc), lambda j, i: (0, i, j))
    return pl.pallas_call(
        body, name=name, out_shape=jax.ShapeDtypeStruct((2, T, D_FF), BF16), grid=(D_FF // tc, T // tm),
        in_specs=[pl.BlockSpec((tm, D), lambda j, i: (i, 0)), pl.BlockSpec((tc, D), lambda j, i: (j, 0)), halves],
        out_specs=halves,
        compiler_params=_cparams(("parallel", "parallel")),
    )(df, w_out, gu)


def _final_loss(x1, f, g2, gf, target, name):
    B, S, D = x1.shape
    ts = ROW_TILE

    def body(x1_ref, f_ref, g2_ref, gf_ref, t_ref, dx_ref, df_ref, dg2_ref, dgf_ref, loss_ref):
        b, s = pl.program_id(0), pl.program_id(1)
        fv = f_ref[0]
        g2v = g2_ref[0]
        gfv = gf_ref[...]
        x2 = x1_ref[0] + g2v * fv
        r = lax.rsqrt(jnp.mean(x2 * x2, axis=-1, keepdims=True) + EPS)
        n = x2 * r
        e = n * gfv - t_ref[0]
        dy = e * (1.0 / D)
        dn = dy * gfv
        dx = r * (dn - n * jnp.mean(dn * n, axis=-1, keepdims=True))
        dx_ref[0] = dx
        df_ref[0] = (dx * g2v).astype(df_ref.dtype)

        @pl.when(s == 0)
        def _():
            dg2_ref[...] = jnp.zeros_like(dg2_ref)

        @pl.when((s == 0) & (b == 0))
        def _():
            dgf_ref[...] = jnp.zeros_like(dgf_ref)
            loss_ref[...] = jnp.zeros_like(loss_ref)

        dg2_ref[0] += jnp.sum(dx * fv, axis=0, keepdims=True)
        dgf_ref[...] += jnp.sum(dy * n, axis=0, keepdims=True)
        loss_ref[...] += 0.5 * jnp.sum(jnp.mean(e * e, axis=-1, keepdims=True), axis=0, keepdims=True)

    tok = pl.BlockSpec((1, ts, D), lambda b, s: (b, s, 0))
    per_b = pl.BlockSpec((1, 1, D), lambda b, s: (b, 0, 0))
    vec = pl.BlockSpec((1, D), lambda b, s: (0, 0))
    return pl.pallas_call(
        body, name=name,
        out_shape=[jax.ShapeDtypeStruct((B, S, D), F32), jax.ShapeDtypeStruct((B, S, D), BF16),
                   jax.ShapeDtypeStruct((B, 1, D), F32), jax.ShapeDtypeStruct((1, D), F32),
                   jax.ShapeDtypeStruct((1, LANES), F32)],
        grid=(B, S // ts),
        in_specs=[tok, tok, per_b, vec, tok],
        out_specs=[tok, tok, per_b, vec, pl.BlockSpec((1, LANES), lambda b, s: (0, 0))],
        compiler_params=_cparams(("arbitrary", "arbitrary")),
    )(x1, f, g2, gf, target)


def _rope_tables():
    half = ROPE_DIM // 2
    inv = ROPE_THETA ** (-jnp.arange(half, dtype=F32) / half)
    ang = jnp.arange(SEQ, dtype=F32)[:, None] * inv[None, :]
    cos, sin = jnp.cos(ang), jnp.sin(ang)
    one = jnp.ones((SEQ, NOPE_DIM), F32)
    zero = jnp.zeros((SEQ, NOPE_DIM), F32)
    cs = jnp.concatenate([one, cos, cos, one[:, :LANES - NOPE_DIM - ROPE_DIM]], axis=1)
    sn = jnp.concatenate([zero, -sin, sin, zero[:, :LANES - NOPE_DIM - ROPE_DIM]], axis=1)
    return cs, sn


def _rope_group(t, cs, sn):
    half = ROPE_DIM // 2
    lane = lax.broadcasted_iota(jnp.int32, t.shape, 1)
    partner = jnp.where(lane < NOPE_DIM + half, pltpu.roll(t, LANES - half, 1), pltpu.roll(t, half, 1))
    return t * cs + partner * sn


def _rope_apply(t, cs, sn, out_dtype, name, add=None, add_blk=0):
    B, S, W = t.shape
    G = W // LANES
    ts = ROW_TILE

    def body(*refs):
        if add is None:
            t_ref, cs_ref, sn_ref, o_ref = refs
            for gi in range(G):
                sl = slice(gi * LANES, (gi + 1) * LANES)
                o_ref[0, :, sl] = _rope_group(t_ref[0, :, sl], cs_ref[...], sn_ref[...]).astype(o_ref.dtype)
        else:
            t_ref, a_ref, cs_ref, sn_ref, o_ref = refs
            ra = _rope_group(a_ref[0], cs_ref[...], sn_ref[...])
            for gi in range(G):
                sl = slice(gi * LANES, (gi + 1) * LANES)
                o_ref[0, :, sl] = (t_ref[0, :, sl] + ra).astype(o_ref.dtype)

    tok = pl.BlockSpec((1, ts, W), lambda b, s: (b, s, 0))
    tab = pl.BlockSpec((ts, LANES), lambda b, s: (s, 0))
    in_specs, args = [tok], [t]
    if add is not None:
        in_specs.append(pl.BlockSpec((1, ts, LANES), lambda b, s: (b, s, add_blk)))
        args.append(add)
    in_specs += [tab, tab]
    args += [cs, sn]
    return pl.pallas_call(
        body, name=name, out_shape=jax.ShapeDtypeStruct((B, S, W), out_dtype), grid=(B, S // ts),
        in_specs=in_specs, out_specs=tok, compiler_params=_cparams(("parallel", "parallel")),
    )(*args)


def _krope_bwd(dkc, cs, sn_neg, name):
    B, S, W = dkc.shape
    G = W // LANES
    ts = ROW_TILE

    def body(d_ref, cs_ref, sn_ref, o_ref):
        acc = d_ref[0, :, 0:LANES]
        for gi in range(1, G):
            acc = acc + d_ref[0, :, gi * LANES:(gi + 1) * LANES]
        lane = lax.broadcasted_iota(jnp.int32, acc.shape, 1)
        rot = (lane >= NOPE_DIM) & (lane < NOPE_DIM + ROPE_DIM)
        acc = jnp.where(rot, acc, 0.0)
        o_ref[0] = _rope_group(acc, cs_ref[...], sn_ref[...]).astype(o_ref.dtype)

    tab = pl.BlockSpec((ts, LANES), lambda b, s: (s, 0))
    return pl.pallas_call(
        body, name=name, out_shape=jax.ShapeDtypeStruct((B, S, LANES), BF16), grid=(B, S // ts),
        in_specs=[pl.BlockSpec((1, ts, W), lambda b, s: (b, s, 0)), tab, tab],
        out_specs=pl.BlockSpec((1, ts, LANES), lambda b, s: (b, s, 0)),
        compiler_params=_cparams(("parallel", "parallel")),
    )(dkc, cs, sn_neg)


def _t5_bucket(dist):
    max_exact = N_BUCKETS // 2
    d = np.maximum(dist, 1).astype(np.float64)
    large = max_exact + (np.log(d / max_exact) / np.log(MAX_DISTANCE / max_exact)
                         * (N_BUCKETS - max_exact)).astype(np.int64)
    large = np.minimum(large, N_BUCKETS - 1)
    return np.where(dist < max_exact, dist, large).astype(np.int32)


def _band_buckets(dilation):
    a = np.arange(BLK)[None, :]
    bk = np.arange(2 * BLK)[:, None]
    steps = BLK + a - bk
    return _t5_bucket(np.clip(steps, 0, SPAN) * dilation)


def _head_mask(shape, hh):
    lane = lax.broadcasted_iota(jnp.int32, shape, 1)
    return (lane >= hh * HEAD_DIM) & (lane < (hh + 1) * HEAD_DIM)


def _dot_nt(a, b):
    return lax.dot_general(a, b, (((1,), (1,)), ((), ())), preferred_element_type=F32)


def _dot_tn(a, b):
    return lax.dot_general(a, b, (((0,), (0,)), ((), ())), preferred_element_type=F32)


def _dot_nn(a, b):
    return lax.dot_general(a, b, (((1,), (0,)), ((), ())), preferred_element_type=F32)


def _band_valid_t():
    bk = lax.broadcasted_iota(jnp.int32, (BLK, BLK), 0)
    a = lax.broadcasted_iota(jnp.int32, (BLK, BLK), 1)
    return bk >= a, bk <= a


def _dil_fwd(qkv, bias, branch, dilation, name, comm=None):
    B, n, _ = qkv.shape
    d = dilation
    nb = n // BLK
    qkv_v = qkv
    npair = N_HEADS // 2

    def body(cur_ref, prev_ref, bias_ref, o_ref, lse_ref, s_scr, e_scr):
        i = pl.program_id(2)
        vprev, vcur = _band_valid_t()
        vprev = vprev & (i > 0)
        for p in range(npair):
            q = cur_ref[0, :, p * LANES:(p + 1) * LANES]
            kc = cur_ref[0, :, D_A + p * LANES:D_A + (p + 1) * LANES]
            kp = prev_ref[0, :, D_A + p * LANES:D_A + (p + 1) * LANES]
            for hh in range(2):
                h = 2 * p + hh
                qm = jnp.where(_head_mask((BLK, LANES), hh), q, jnp.zeros_like(q))
                s_scr[h, 0:BLK, :] = _dot_nt(kp, qm)
                s_scr[h, BLK:2 * BLK, :] = _dot_nt(kc, qm)
        ms = []
        for h in range(N_HEADS):
            s_p = jnp.where(vprev, s_scr[h, 0:BLK, :] * DIL_SCALE + bias_ref[h, 0:BLK, :], NEG)
            s_c = jnp.where(vcur, s_scr[h, BLK:2 * BLK, :] * DIL_SCALE + bias_ref[h, BLK:2 * BLK, :], NEG)
            m = jnp.maximum(jnp.max(s_p, axis=0, keepdims=True), jnp.max(s_c, axis=0, keepdims=True))
            e_scr[h, 0:BLK, :] = jnp.exp(s_p - m).astype(BF16)
            e_scr[h, BLK:2 * BLK, :] = jnp.exp(s_c - m).astype(BF16)
            ms.append(m)
        rows0 = _row_mask((LANES, BLK), 0)
        for p in range(npair):
            sl = slice(p * LANES, (p + 1) * LANES)
            vsl = slice(2 * D_A + p * LANES, 2 * D_A + (p + 1) * LANES)
            vct = jnp.transpose(cur_ref[0, :, vsl].astype(F32)).astype(BF16)
            vpt = jnp.transpose(prev_ref[0, :, vsl].astype(F32)).astype(BF16)
            acc = []
            for hh in range(2):
                h = 2 * p + hh
                mine = _row_mask((LANES, BLK), hh)
                one = jnp.ones_like(vct)
                acc.append(_dot_nn(jnp.where(mine, vpt, one), e_scr[h, 0:BLK, :])
                           + _dot_nn(jnp.where(mine, vct, one), e_scr[h, BLK:2 * BLK, :]))
            l0 = acc[0][HEAD_DIM:HEAD_DIM + 1, :]
            l1 = acc[1][0:1, :]
            o_t = jnp.where(rows0, acc[0] / l0, acc[1] / l1)
            lse_t = jnp.where(rows0, ms[2 * p] + jnp.log(l0), ms[2 * p + 1] + jnp.log(l1))
            o_ref[0, :, sl] = jnp.transpose(o_t)
            lse_ref[0, :, sl] = jnp.transpose(lse_t)

    cur = pl.BlockSpec((1, BLK, P_QKV), lambda b, r, i: (b, i, r))
    prev = pl.BlockSpec((1, BLK, P_QKV), lambda b, r, i: (b, jnp.maximum(i - 1, 0), r))
    out = pl.BlockSpec((1, BLK, D_A), lambda b, r, i: (b, i, r))
    return _host_call(
        body, comm, name=name,
        out_shape=[jax.ShapeDtypeStruct((B, n, d * D_A), F32)] * 2,
        grid=(B, d, nb),
        in_specs=[cur, prev,
                  pl.BlockSpec((None, N_HEADS, 2 * BLK, BLK), lambda b, r, i: (branch, 0, 0, 0))],
        out_specs=[out, out],
        scratch_shapes=[pltpu.VMEM((N_HEADS, 2 * BLK, BLK), F32), pltpu.VMEM((N_HEADS, 2 * BLK, BLK), BF16)],
        args=(qkv_v, qkv_v, bias))


VIEW_TILE = 512


def _view_spec(d, w):
    return pl.BlockSpec((1, VIEW_TILE // d, d * w), lambda b, t: (b, t, 0))


def _view_shape(B, S, d, w, dtype):
    return jax.ShapeDtypeStruct((B, S // d, d * w), dtype)


def _tile_scratch(w):
    return pltpu.VMEM((w // LANES, VIEW_TILE, LANES), F32)


def _put_tile(tile_ref, val):
    for c in range(tile_ref.shape[0]):
        tile_ref[c] = val[:, c * LANES:(c + 1) * LANES]


def _get_tile(tile_ref):
    return jnp.concatenate([tile_ref[c] for c in range(tile_ref.shape[0])], axis=1)


def _tile_to_view(tile_ref, view_ref, d, w):
    for c in range(w // LANES):
        for r in range(d):
            lo = r * w + c * LANES
            rows = tile_ref.at[c][pl.ds(r, VIEW_TILE // d, stride=d), :]
            view_ref[0, :, lo:lo + LANES] = rows.astype(view_ref.dtype)


def _view_to_tile(view_ref, tile_ref, d, w):
    for c in range(w // LANES):
        for r in range(d):
            lo = r * w + c * LANES
            tile_ref.at[c][pl.ds(r, VIEW_TILE // d, stride=d), :] = view_ref[0, :, lo:lo + LANES].astype(F32)


def _mm_qkv_views(h, w, name):
    B, S, D = h.shape
    N = w.shape[1]

    def body(h_ref, w_ref, o1_ref, o4_ref, o16_ref, acc_ref):
        acc = jnp.dot(h_ref[0], w_ref[...], preferred_element_type=F32)
        o1_ref[0] = acc.astype(o1_ref.dtype)
        _put_tile(acc_ref, acc)
        _tile_to_view(acc_ref, o4_ref, DILATIONS[1], N)
        _tile_to_view(acc_ref, o16_ref, DILATIONS[2], N)

    return pl.pallas_call(
        body, name=name,
        out_shape=[_view_shape(B, S, d, N, BF16) for d in DILATIONS],
        grid=(B, S // VIEW_TILE),
        in_specs=[pl.BlockSpec((1, VIEW_TILE, D), lambda b, t: (b, t, 0)), pl.BlockSpec((D, N), lambda b, t: (0, 0))],
        out_specs=[_view_spec(d, N) for d in DILATIONS],
        scratch_shapes=[_tile_scratch(N)],
        compiler_params=_cparams(("parallel", "parallel")),
    )(h, w)


def _dil_merge(os_, lses, name):
    B, S, W = os_[0].shape
    nd = len(DILATIONS)

    def body(*refs):
        o_refs, l_refs = refs[:nd], refs[nd:2 * nd]
        out_refs, L_refs = refs[2 * nd:3 * nd], refs[3 * nd:4 * nd]
        scr = refs[4 * nd:]
        o_tok, l_tok = [o_refs[0][0]], [l_refs[0][0]]
        for i, d in enumerate(DILATIONS[1:]):
            _view_to_tile(o_refs[i + 1], scr[2 * i], d, W)
            _view_to_tile(l_refs[i + 1], scr[2 * i + 1], d, W)
            o_tok.append(_get_tile(scr[2 * i]))
            l_tok.append(_get_tile(scr[2 * i + 1]))
        a0, a1, a2 = l_tok
        m = jnp.maximum(jnp.maximum(a0, a1), a2)
        e0, e1, e2 = jnp.exp(a0 - m), jnp.exp(a1 - m), jnp.exp(a2 - m)
        ssum = e0 + e1 + e2
        out = (e0 * o_tok[0] + e1 * o_tok[1] + e2 * o_tok[2]) / ssum
        lse = m + jnp.log(ssum)
        out_refs[0][0] = out
        L_refs[0][0] = lse
        res_o, res_l = scr[2 * (nd - 1)], scr[2 * (nd - 1) + 1]
        _put_tile(res_o, out)
        _put_tile(res_l, lse)
        for i, d in enumerate(DILATIONS[1:]):
            _tile_to_view(res_o, out_refs[i + 1], d, W)
            _tile_to_view(res_l, L_refs[i + 1], d, W)

    specs = [_view_spec(d, W) for d in DILATIONS]
    shapes = [_view_shape(B, S * DILATIONS[0], d, W, F32) for d in DILATIONS]
    res = pl.pallas_call(
        body, name=name, out_shape=shapes * 2, grid=(B, S // VIEW_TILE),
        in_specs=specs * 2, out_specs=specs * 2,
        scratch_shapes=[_tile_scratch(W)] * (2 * nd),
        compiler_params=_cparams(("parallel", "parallel")),
    )(*os_, *lses)
    return res[:nd], res[nd:]


def _dil_bwd(qkv, do, out_a, L, bias, branch, dilation, name, comm=None):
    B, n, _ = qkv.shape
    d = dilation
    nb = n // BLK
    qkv_v, do_v, oa_v, L_v = qkv, do, out_a, L
    npair = N_HEADS // 2
    multi = nb > 1

    tiles = ("P", "C", "N") if multi else ("C",)
    n_t = len(tiles)

    def body(*refs):
        if multi:
            (cur_ref, prev_ref, next_ref, do_ref, don_ref, oa_ref, oan_ref, L_ref, Ln_ref, bias_ref,
             dqkv_ref, dbias_ref, s_scr, dp_scr, p_scr, ds_scr) = refs
        else:
            cur_ref, do_ref, oa_ref, L_ref, bias_ref, dqkv_ref, dbias_ref, s_scr, dp_scr, p_scr, ds_scr = refs
        b, r, i = pl.program_id(0), pl.program_id(1), pl.program_id(2)

        @pl.when((b == 0) & (r == 0) & (i == 0))
        def _():
            dbias_ref[...] = jnp.zeros_like(dbias_ref)

        vprev, vcur = _band_valid_t()
        valid = {"P": vprev & (i > 0), "C": vcur, "N": vprev & (i < nb - 1)}
        band = {"P": slice(0, BLK), "C": slice(BLK, 2 * BLK), "N": slice(0, BLK)}
        psl = lambda p: slice(p * LANES, (p + 1) * LANES)
        ksl = lambda p: slice(D_A + p * LANES, D_A + (p + 1) * LANES)
        vsl = lambda p: slice(2 * D_A + p * LANES, 2 * D_A + (p + 1) * LANES)

        def operands(p, hh):
            hm = _head_mask((BLK, LANES), hh)
            mask = lambda x: jnp.where(hm, x, jnp.zeros_like(x))
            qm, dom = mask(cur_ref[0, :, psl(p)]), mask(do_ref[0, :, psl(p)])
            ops = {"C": (cur_ref[0, :, ksl(p)], cur_ref[0, :, vsl(p)], qm, dom)}
            if multi:
                ops["P"] = (prev_ref[0, :, ksl(p)], prev_ref[0, :, vsl(p)], qm, dom)
                ops["N"] = (cur_ref[0, :, ksl(p)], cur_ref[0, :, vsl(p)], mask(next_ref[0, :, psl(p)]),
                            mask(don_ref[0, :, psl(p)]))
            return ops

        for p in range(npair):
            for hh in range(2):
                h = 2 * p + hh
                ops = operands(p, hh)
                for t, name_t in enumerate(tiles):
                    k_t, v_t, q_t, do_t = ops[name_t]
                    s_scr[h, t] = _dot_nt(k_t, q_t)
                    dp_scr[h, t] = _dot_nt(v_t, do_t)

        def rows(L_r, do_r, oa_r, p):
            lt = jnp.transpose(L_r[0, :, psl(p)])
            dt = jnp.transpose(do_r[0, :, psl(p)].astype(F32) * oa_r[0, :, psl(p)])
            return ([lt[0:1, :], lt[HEAD_DIM:HEAD_DIM + 1, :]],
                    [jnp.sum(dt[:HEAD_DIM], axis=0, keepdims=True), jnp.sum(dt[HEAD_DIM:], axis=0, keepdims=True)])

        for p in range(npair):
            lse_c, delta_c = rows(L_ref, do_ref, oa_ref, p)
            if multi:
                lse_n, delta_n = rows(Ln_ref, don_ref, oan_ref, p)
            for hh in range(2):
                h = 2 * p + hh
                for t, name_t in enumerate(tiles):
                    lse, delta = (lse_n[hh], delta_n[hh]) if name_t == "N" else (lse_c[hh], delta_c[hh])
                    s = s_scr[h, t] * DIL_SCALE + bias_ref[h, band[name_t], :]
                    pr = jnp.where(valid[name_t], jnp.exp(s - lse), 0.0)
                    ds = pr * (dp_scr[h, t] - delta)
                    p_scr[h, t] = pr.astype(BF16)
                    ds_scr[h, t] = ds.astype(BF16)
                    if name_t != "N":
                        dbias_ref[h, band[name_t], :] += ds

        for p in range(npair):
            dqt = jnp.zeros((LANES, BLK), F32)
            dk = jnp.zeros((BLK, LANES), F32)
            dv = jnp.zeros((BLK, LANES), F32)
            kct = jnp.transpose(cur_ref[0, :, ksl(p)].astype(F32)).astype(BF16)
            if multi:
                kpt = jnp.transpose(prev_ref[0, :, ksl(p)].astype(F32)).astype(BF16)
            for hh in range(2):
                h = 2 * p + hh
                ops = operands(p, hh)
                mine = _row_mask((LANES, BLK), hh)
                for t, name_t in enumerate(tiles):
                    _, _, q_t, do_t = ops[name_t]
                    if name_t != "P":
                        dv = dv + _dot_nn(p_scr[h, t], do_t)
                        dk = dk + _dot_nn(ds_scr[h, t], q_t)
                    if name_t != "N":
                        kt = kpt if name_t == "P" else kct
                        dqt = dqt + _dot_nn(jnp.where(mine, kt, jnp.zeros_like(kt)), ds_scr[h, t])
            dqkv_ref[0, :, psl(p)] = jnp.transpose(dqt) * DIL_SCALE
            dqkv_ref[0, :, ksl(p)] = dk * DIL_SCALE
            dqkv_ref[0, :, vsl(p)] = dv

    def at(off):
        return lambda b, r, i: (b, jnp.clip(i + off, 0, nb - 1), r)

    qkv_spec = lambda off: pl.BlockSpec((1, BLK, P_QKV), at(off))
    da_spec = lambda off: pl.BlockSpec((1, BLK, D_A), at(off))
    bias_spec = pl.BlockSpec((None, N_HEADS, 2 * BLK, BLK), lambda b, r, i: (branch, 0, 0, 0))
    dbias_spec = pl.BlockSpec((N_HEADS, 2 * BLK, BLK), lambda b, r, i: (0, 0, 0))
    if multi:
        in_specs = [qkv_spec(0), qkv_spec(-1), qkv_spec(1), da_spec(0), da_spec(1), da_spec(0), da_spec(1),
                    da_spec(0), da_spec(1), bias_spec]
        args = [qkv_v, qkv_v, qkv_v, do_v, do_v, oa_v, oa_v, L_v, L_v, bias]
    else:
        in_specs = [qkv_spec(0), da_spec(0), da_spec(0), da_spec(0), bias_spec]
        args = [qkv_v, do_v, oa_v, L_v, bias]
    return _host_call(
        body, comm, name=name,
        out_shape=[jax.ShapeDtypeStruct((B, n, d * P_QKV), F32),
                   jax.ShapeDtypeStruct((N_HEADS, 2 * BLK, BLK), F32)],
        grid=(B, d, nb),
        in_specs=in_specs,
        out_specs=[qkv_spec(0), dbias_spec],
        scratch_shapes=[pltpu.VMEM((N_HEADS, n_t, BLK, BLK), F32), pltpu.VMEM((N_HEADS, n_t, BLK, BLK), F32),
                        pltpu.VMEM((N_HEADS, n_t, BLK, BLK), BF16), pltpu.VMEM((N_HEADS, n_t, BLK, BLK), BF16)],
        args=args)


def _sum_views_bf16(parts, name):
    B, S, W = parts[0].shape

    def body(a_ref, b_ref, c_ref, o_ref, sb, sc):
        _view_to_tile(b_ref, sb, DILATIONS[1], W)
        _view_to_tile(c_ref, sc, DILATIONS[2], W)
        o_ref[0] = (a_ref[0] + _get_tile(sb) + _get_tile(sc)).astype(o_ref.dtype)

    return pl.pallas_call(
        body, name=name, out_shape=jax.ShapeDtypeStruct((B, S, W), BF16), grid=(B, S // VIEW_TILE),
        in_specs=[_view_spec(d, W) for d in DILATIONS], out_specs=_view_spec(1, W),
        scratch_shapes=[_tile_scratch(W)] * 2,
        compiler_params=_cparams(("parallel", "parallel")),
    )(*parts)


def _bias_tables(rel_bias, buckets, name):
    nbr = buckets.shape[0]

    def body(rb_ref, bk_ref, o_ref):
        h = pl.program_id(1)
        tab = bk_ref[0]

        def step(bkt, acc):
            return jnp.where(tab == bkt, rb_ref[bkt, h], acc)

        o_ref[0, 0] = lax.fori_loop(0, N_BUCKETS, step, jnp.zeros((2 * BLK, BLK), F32))

    return pl.pallas_call(
        body, name=name, out_shape=jax.ShapeDtypeStruct((nbr, N_HEADS, 2 * BLK, BLK), F32),
        grid=(nbr, N_HEADS),
        in_specs=[pl.BlockSpec(memory_space=pltpu.SMEM),
                  pl.BlockSpec((1, 2 * BLK, BLK), lambda i, h: (i, 0, 0))],
        out_specs=pl.BlockSpec((1, 1, 2 * BLK, BLK), lambda i, h: (i, h, 0, 0)),
        compiler_params=_cparams(("parallel", "arbitrary")),
    )(rel_bias, buckets)


def _bias_grad(dbias_list, buckets, name):
    nbr = len(dbias_list)

    def body(*refs):
        d_refs, bk_ref, o_ref, part = refs[:nbr], refs[nbr], refs[nbr + 1], refs[nbr + 2]

        def step(bkt, carry):
            hit = [bk_ref[bi] == bkt for bi in range(nbr)]
            for h in range(N_HEADS):
                tot = jnp.zeros((1, BLK), F32)
                for bi in range(nbr):
                    tot = tot + jnp.sum(jnp.where(hit[bi], d_refs[bi][h], 0.0), axis=0, keepdims=True)
                part[bkt, h:h + 1, :] = tot
            return carry

        lax.fori_loop(0, N_BUCKETS, step, 0)
        lane = lax.broadcasted_iota(jnp.int32, (N_HEADS, LANES), 1)
        acc = jnp.zeros((N_HEADS, LANES), F32)
        for bkt in range(N_BUCKETS):
            acc = acc + jnp.where(lane == bkt, jnp.sum(part[bkt], axis=1, keepdims=True), 0.0)
        o_ref[...] = acc

    band = pl.BlockSpec((N_HEADS, 2 * BLK, BLK), lambda i: (0, 0, 0))
    return pl.pallas_call(
        body, name=name, out_shape=jax.ShapeDtypeStruct((N_HEADS, LANES), F32), grid=(1,),
        in_specs=[band] * nbr + [pl.BlockSpec((nbr, 2 * BLK, BLK), lambda i: (0, 0, 0))],
        out_specs=pl.BlockSpec((N_HEADS, LANES), lambda i: (0, 0)),
        scratch_shapes=[pltpu.VMEM((N_BUCKETS, N_HEADS, BLK), F32)],
        compiler_params=_cparams(("arbitrary",)),
    )(*dbias_list, buckets)


MLA_TQ = 256
MLA_TK = 256


LOG2E = math.log2(math.e)
MLA_C = MLA_SCALE * LOG2E


def _key_le_query(tk, tq):
    return lax.broadcasted_iota(jnp.int32, (tk, tq), 0) <= lax.broadcasted_iota(jnp.int32, (tk, tq), 1)


def _row_mask(shape, hh):
    row = lax.broadcasted_iota(jnp.int32, shape, 0)
    return (row >= hh * HEAD_DIM) & (row < (hh + 1) * HEAD_DIM)


def _host_call(body, comm, *, name, grid, in_specs, out_specs, out_shape, scratch_shapes, args):
    sem = ("arbitrary",) * len(grid)
    if comm is None:
        res = pl.pallas_call(body, name=name, grid=grid, in_specs=in_specs, out_specs=out_specs,
                             out_shape=out_shape, scratch_shapes=scratch_shapes,
                             compiler_params=_cparams(sem))(*args)
        return res, []
    n_in, n_out, n_s, cn = len(in_specs), len(out_specs), len(scratch_shapes), comm.n

    def hosted(*refs):
        ins, refs = refs[:n_in], refs[n_in:]
        c_ins, refs = refs[:cn], refs[cn:]
        outs, refs = refs[:n_out], refs[n_out:]
        c_outs, refs = refs[:cn], refs[cn:]
        scr, c_sems = refs[:n_s], refs[n_s:]
        ids = [pl.program_id(a) for a in range(len(grid))]
        first = functools.reduce(jnp.logical_and, [i == 0 for i in ids])
        last = functools.reduce(jnp.logical_and, [i == g - 1 for i, g in zip(ids, grid)])

        @pl.when(first)
        def _():
            comm.start(c_ins, c_outs, c_sems)

        body(*ins, *outs, *scr)

        @pl.when(last)
        def _():
            comm.finish(c_ins, c_outs, c_sems)

    res = pl.pallas_call(
        hosted, name=name, grid=grid, in_specs=list(in_specs) + _hbm_specs(cn),
        out_specs=list(out_specs) + _hbm_specs(cn), out_shape=list(out_shape) + list(comm.out_shape),
        scratch_shapes=list(scratch_shapes) + list(comm.scratch), compiler_params=_cparams(sem),
    )(*args, *comm.inputs)
    return res[:n_out], res[n_out:]


def _mla_fwd_t(q, k, vt, name, comm=None):
    B, S, _ = q.shape
    tq, tk = MLA_TQ, MLA_TK
    assert tq == tk
    npair = N_HEADS // 2
    nq = S // tq

    def body(q_ref, k_ref, vt_ref, o_ref, lse_ref, s_scr, e_scr, acc_scr, m_scr, a_scr):
        i = pl.program_id(1)
        diag = _key_le_query(tk, tq)
        m_scr[...] = jnp.full_like(m_scr, NEG)
        acc_scr[...] = jnp.zeros_like(acc_scr)

        def step(j, masked):
            rows = pl.ds(pl.multiple_of(j * tk, tk), tk)
            for h in range(N_HEADS):
                hsl = slice(h * LANES, (h + 1) * LANES)
                s_scr[h] = _dot_nt(k_ref[0, rows, hsl], q_ref[0, :, hsl])
            for h in range(N_HEADS):
                s = s_scr[h]
                if masked:
                    s = jnp.where(diag, s, NEG)
                m_old = m_scr[h:h + 1, :]
                m_new = jnp.maximum(m_old, jnp.max(s, axis=0, keepdims=True))
                a_scr[h:h + 1, :] = jnp.exp2((m_old - m_new) * MLA_C)
                e_scr[h] = jnp.exp2((s - m_new) * MLA_C).astype(BF16)
                m_scr[h:h + 1, :] = m_new
            for h in range(N_HEADS):
                vj = vt_ref[0, h // 2, j]
                vh = jnp.where(_row_mask(vj.shape, h % 2), vj, jnp.ones_like(vj))
                acc_scr[h] = acc_scr[h] * a_scr[h:h + 1, :] + _dot_nn(vh, e_scr[h])

        def loop_body(j, carry):
            step(j, False)
            return carry

        lax.fori_loop(0, i, loop_body, 0)
        step(i, True)
        rows0 = _row_mask((LANES, tq), 0)
        for p in range(npair):
            l0 = acc_scr[2 * p, HEAD_DIM:HEAD_DIM + 1, :]
            l1 = acc_scr[2 * p + 1, 0:1, :]
            o_ref[0, p * LANES:(p + 1) * LANES, :] = jnp.where(rows0, acc_scr[2 * p] / l0, acc_scr[2 * p + 1] / l1)
            lse_ref[0, p, 0] = jnp.zeros((8, tq), F32)
            lse_ref[0, p, 0, 0:1, :] = m_scr[2 * p:2 * p + 1, :] * MLA_C + jnp.log(l0) * LOG2E
            lse_ref[0, p, 0, 1:2, :] = m_scr[2 * p + 1:2 * p + 2, :] * MLA_C + jnp.log(l1) * LOG2E

    return _host_call(
        body, comm, name=name,
        out_shape=[jax.ShapeDtypeStruct((B, D_B, S), F32), jax.ShapeDtypeStruct((B, npair, nq, 8, tq), F32)],
        grid=(B, nq),
        in_specs=[pl.BlockSpec((1, tq, N_HEADS * LANES), lambda b, i: (b, i, 0)),
                  pl.BlockSpec((1, S, N_HEADS * LANES), lambda b, i: (b, 0, 0)),
                  pl.BlockSpec((1, npair, S // tk, LANES, tk), lambda b, i: (b, 0, 0, 0, 0))],
        out_specs=[pl.BlockSpec((1, D_B, tq), lambda b, i: (b, 0, i)),
                   pl.BlockSpec((1, npair, 1, 8, tq), lambda b, i: (b, 0, i, 0, 0))],
        scratch_shapes=[pltpu.VMEM((N_HEADS, tk, tq), F32), pltpu.VMEM((N_HEADS, tk, tq), BF16),
                        pltpu.VMEM((N_HEADS, LANES, tq), F32), pltpu.VMEM((N_HEADS, tq), F32),
                        pltpu.VMEM((N_HEADS, tq), F32)],
        args=(q, k, vt))


def _mla_delta(do, o, name):
    B, S, _ = o.shape
    tq = MLA_TQ
    npair = N_HEADS // 2

    def body(do_ref, o_ref, d_ref):
        d_ref[...] = jnp.zeros_like(d_ref)
        for p in range(npair):
            sl = slice(p * LANES, (p + 1) * LANES)
            prod_t = jnp.transpose(do_ref[0, :, sl].astype(F32) * o_ref[0, :, sl])
            d_ref[0, p, 0, 0:1, :] = jnp.sum(prod_t[:HEAD_DIM], axis=0, keepdims=True)
            d_ref[0, p, 0, 1:2, :] = jnp.sum(prod_t[HEAD_DIM:], axis=0, keepdims=True)

    tok = pl.BlockSpec((1, tq, D_B), lambda b, i: (b, i, 0))
    return pl.pallas_call(
        body, name=name, out_shape=jax.ShapeDtypeStruct((B, npair, S // tq, 8, tq), F32),
        grid=(B, S // tq), in_specs=[tok, tok],
        out_specs=pl.BlockSpec((1, npair, 1, 8, tq), lambda b, i: (b, 0, i, 0, 0)),
        compiler_params=_cparams(("parallel", "parallel")),
    )(do, o)


def _mla_bwd_t(q, k, v, do, lse, delta, name, comm=None):
    B, S, _ = q.shape
    tq, tk = MLA_TQ, MLA_TK
    assert tq == tk
    npair = N_HEADS // 2
    nq = S // tq

    hg = 4
    pg = hg // 2
    ngroup = N_HEADS // hg

    def body(q_ref, do_ref, lse_ref, dl_ref, k_ref, v_ref, dk_ref, dv_ref, dq_ref,
             s_scr, dp_scr, p_scr, ds_scr, dk_s, dv_s, kt_s):
        j = pl.program_id(2)

        @pl.when(j == 0)
        def _():
            dq_ref[...] = jnp.zeros_like(dq_ref)

        dk_s[...] = jnp.zeros_like(dk_s)
        dv_s[...] = jnp.zeros_like(dv_s)
        diag = _key_le_query(tk, tq)
        hsl = lambda h: slice(h * LANES, (h + 1) * LANES)
        for h in range(hg):
            kt_s[h] = jnp.transpose(k_ref[0, :, hsl(h)].astype(F32)).astype(BF16)

        def step(i, masked):
            rows = pl.ds(pl.multiple_of(i * tq, tq), tq)

            def dom(h):
                dov = do_ref[0, rows, hsl(h // 2)]
                return jnp.where(_head_mask((tq, LANES), h % 2), dov, jnp.zeros_like(dov))

            for h in range(hg):
                s_scr[h] = _dot_nt(k_ref[0, :, hsl(h)], q_ref[0, rows, hsl(h)])
                dp_scr[h] = _dot_nt(v_ref[0, :, hsl(h // 2)], dom(h))
            for h in range(hg):
                pr = jnp.exp2(s_scr[h] * MLA_C - lse_ref[0, h // 2, i, h % 2:h % 2 + 1, :])
                if masked:
                    pr = jnp.where(diag, pr, 0.0)
                p_scr[h] = pr.astype(BF16)
                ds_scr[h] = (pr * (dp_scr[h] - dl_ref[0, h // 2, i, h % 2:h % 2 + 1, :])).astype(BF16)
            for h in range(hg):
                dv_s[h // 2] += _dot_nn(p_scr[h], dom(h))
                dk_s[h] += _dot_nn(ds_scr[h], q_ref[0, rows, hsl(h)])
                dq_ref[0, h // 2, i, hsl(h % 2), :] += _dot_nn(kt_s[h], ds_scr[h]) * MLA_SCALE

        step(j, True)

        def loop_body(i, carry):
            step(i, False)
            return carry

        lax.fori_loop(j + 1, nq, loop_body, 0)
        for h in range(hg):
            dk_ref[0, :, hsl(h)] = dk_s[h] * MLA_SCALE
        for p in range(pg):
            dv_ref[0, :, hsl(p)] = dv_s[p]

    stat = pl.BlockSpec((1, pg, nq, 8, tq), lambda b, g, j: (b, g, 0, 0, 0))
    return _host_call(
        body, comm, name=name,
        out_shape=[jax.ShapeDtypeStruct((B, S, N_HEADS * LANES), F32), jax.ShapeDtypeStruct((B, S, D_B), F32),
                   jax.ShapeDtypeStruct((B, npair, nq, 2 * LANES, tq), F32)],
        grid=(B, ngroup, S // tk),
        in_specs=[pl.BlockSpec((1, S, hg * LANES), lambda b, g, j: (b, 0, g)),
                  pl.BlockSpec((1, S, pg * LANES), lambda b, g, j: (b, 0, g)),
                  stat, stat,
                  pl.BlockSpec((1, tk, hg * LANES), lambda b, g, j: (b, j, g)),
                  pl.BlockSpec((1, tk, pg * LANES), lambda b, g, j: (b, j, g))],
        out_specs=[pl.BlockSpec((1, tk, hg * LANES), lambda b, g, j: (b, j, g)),
                   pl.BlockSpec((1, tk, pg * LANES), lambda b, g, j: (b, j, g)),
                   pl.BlockSpec((1, pg, nq, 2 * LANES, tq), lambda b, g, j: (b, g, 0, 0, 0))],
        scratch_shapes=[pltpu.VMEM((hg, tk, tq), F32), pltpu.VMEM((hg, tk, tq), F32),
                        pltpu.VMEM((hg, tk, tq), BF16), pltpu.VMEM((hg, tk, tq), BF16),
                        pltpu.VMEM((hg, tk, LANES), F32), pltpu.VMEM((pg, tk, LANES), F32),
                        pltpu.VMEM((hg, LANES, tk), BF16)],
        args=(q, do, lse, delta, k, v))


def _local_step(x, target, mod, wts, gains, rel_bias, ffn_shards=None):
    B, S, D = x.shape
    T = B * S
    sh1, sc1, g1, sh2, sc2, g2 = [mod[:, i * D:(i + 1) * D].reshape(B, 1, D) for i in range(N_MOD)]
    cs, sn = _rope_tables()
    buckets = np.stack([_band_buckets(d) for d in DILATIONS])
    buckets_dev = jnp.asarray(buckets)
    bias = _bias_tables(rel_bias, buckets_dev, "rel_bias_tables")
    w_in = wts["w_in"]

    h1 = _adaln_fwd(x, gains["g_norm1"], sc1, sh1, "adaln1_fwd")
    h1f = h1.reshape(T, D)
    qkv_v = _mm_qkv_views(h1, w_in[:, :P_QKV], "mm_qkv")
    rest = _mm(h1f, w_in[:, P_QKV:], "nn", F32, "mm_rest")
    o_d, lse_d = [], []
    late_got = []
    for i, d in enumerate(DILATIONS):
        comm = _GatherComm(ffn_shards[i + 1:i + 2]) if (ffn_shards and i < 2) else None
        (o_i, lse_i), got = _dil_fwd(qkv_v[i], bias, i, d, f"dil_fwd_{d}", comm)
        late_got += list(got)
        o_d.append(o_i)
        lse_d.append(lse_i)
    if ffn_shards:
        wts = dict(wts, w_out=late_got[1].reshape(D, D))
    out_a_v, lse_a_v = _dil_merge(o_d, lse_d, "dil_merge")
    out_a = out_a_v[0]
    cqn = _rms_fwd(rest, 1, Q_LORA, gains["g_cq"], "rms_cq_fwd")
    ckvn = _rms_fwd(rest, 0, KV_LORA, gains["g_ckv"], "rms_ckv_fwd")
    rest3 = rest.reshape(B, S, P_REST)
    q_raw = _mm(cqn, wts["w_uq"], "nn", F32, "mm_uq").reshape(B, S, N_HEADS * LANES)
    qc = _rope_apply(q_raw, cs, sn, BF16, "rope_q")
    kn_raw = _mm(ckvn, wts["w_kv"][:, :N_HEADS * LANES], "nn", F32, "mm_uk").reshape(B, S, N_HEADS * LANES)
    kc = _rope_apply(kn_raw, cs, sn, BF16, "rope_k", add=rest3, add_blk=KV_LORA // LANES)
    v = _mm(ckvn, wts["w_kv"][:, N_HEADS * LANES:], "nn", BF16, "mm_uv").reshape(B, S, D_B)
    vt = jnp.transpose(v.reshape(B, S // MLA_TK, MLA_TK, N_HEADS // 2, LANES), (0, 3, 1, 4, 2))
    (o_t, lse_b), got = _mla_fwd_t(qc, kc, vt, "mla_fwd", _GatherComm(ffn_shards[:1]) if ffn_shards else None)
    if ffn_shards:
        wts = dict(wts, w_ffn_in=got[0].reshape(N_CHIP, D, -1), w_ffn_out=late_got[0].reshape(D_FF, D))
    out_b = jnp.transpose(o_t, (0, 2, 1))
    out_af, out_bf = out_a.reshape(T, D_A), out_b.reshape(T, D_B)
    ya = _rms_fwd(out_af, 0, D_A, gains["g_out_a"], "rms_outa_fwd")
    yb = _rms_fwd(out_bf, 0, D_B, gains["g_out_b"], "rms_outb_fwd")
    y = jnp.concatenate([ya, yb], axis=1)
    mix = _mm(y, wts["w_out"], "nn", F32, "mm_out").reshape(B, S, D)
    h2, x1 = _adaln_fwd(x, gains["g_norm2"], sc2, sh2, "adaln2_fwd", mix=mix, gate=g1)
    h2f = h2.reshape(T, D)
    gu, act = _ffn_in_fwd(h2f, wts["w_ffn_in"], "mm_ffn_in")
    f = _mm(act, wts["w_ffn_out"], "nn", F32, "mm_ffn_out").reshape(B, S, D)
    dx2, df, dg2, dg_final, loss = _final_loss(x1, f, g2, gains["g_final"], target, "final_loss")

    dff = df.reshape(T, D)
    dgu = _ffn_out_bwd(dff, wts["w_ffn_out"], gu, "mm_ffn_out_dx")
    gw_ffn_out = _mm(act, dff, "tn", F32, "mm_ffn_out_dw")
    dh2 = _mm(dgu, wts["w_ffn_in"], "nt", F32, "mm_ffn_in_dx", col_blocks=N_CHIP, halves=True).reshape(B, S, D)
    gw_ffn_in = _mm(h2f, dgu, "tn", F32, "mm_ffn_in_dw", col_blocks=N_CHIP, halves=True)
    ffn_g8 = ffn_r1 = None
    if ffn_shards:
        ffn_g8 = [gw_ffn_in.reshape(N_DEV, -1, gw_ffn_in.shape[-1]), gw_ffn_out.reshape(N_DEV, -1, D)]
        dx1, dsh2, dsc2, dg_norm2, dg1, dmix, ffn_r1 = _adaln_bwd(
            dh2, x1, gains["g_norm2"], sc2, dx2, "adaln2_bwd", mix=mix, gate=g1, comm=_ToSiblingComm(ffn_g8))
    else:
        dx1, dsh2, dsc2, dg_norm2, dg1, dmix = _adaln_bwd(dh2, x1, gains["g_norm2"], sc2, dx2, "adaln2_bwd",
                                                          mix=mix, gate=g1)
    dmixf = dmix.reshape(T, D)
    dy = _mm(dmixf, wts["w_out"], "nt", F32, "mm_out_dx")
    gw_out = _mm(y, dmixf, "tn", F32, "mm_out_dw")
    do_a_v, dg_out_a = _rms_bwd_views(dy, 0, out_a, gains["g_out_a"], "rms_outa_bwd")
    do_b, dg_out_b = _rms_bwd(dy, 1, out_bf, 0, D_B, gains["g_out_b"], "rms_outb_bwd")
    do_b3 = do_b.reshape(B, S, D_B)
    delta_b = _mla_delta(do_b3, out_b, "mla_delta")
    ffn_a4 = ffn_send = None
    if ffn_shards:
        ffn_a4, ffn_send = _rs_first(ffn_g8, "ffn", r1=ffn_r1)
    (dkc, dv, dq_t), ffn_r2 = _mla_bwd_t(qc, kc, v, do_b3, lse_b, delta_b, "mla_bwd",
                                         _ToChipsComm(ffn_send[:1]) if ffn_shards else None)
    dqc = jnp.transpose(dq_t, (0, 2, 4, 1, 3)).reshape(B, S, N_HEADS * LANES)
    dq_raw = _rope_apply(dqc, cs, -sn, BF16, "rope_q_bwd").reshape(T, N_HEADS * LANES)
    dkrw = _krope_bwd(dkc, cs, -sn, "rope_k_bwd").reshape(T, LANES)
    dcqn = _mm(dq_raw, wts["w_uq"], "nt", F32, "mm_uq_dx")
    gw_uq = _mm(cqn, dq_raw, "tn", F32, "mm_uq_dw")
    dkv = jnp.concatenate([dkc.reshape(T, -1), dv.reshape(T, -1)], axis=1).astype(BF16)
    dckvn = _mm(dkv, wts["w_kv"], "nt", F32, "mm_ukv_dx")
    gw_kv = _mm(ckvn, dkv, "tn", F32, "mm_ukv_dw")
    dcq, dg_cq = _rms_bwd(dcqn, 0, rest, 1, Q_LORA, gains["g_cq"], "rms_cq_bwd")
    dckv, dg_ckv = _rms_bwd(dckvn, 0, rest, 0, KV_LORA, gains["g_ckv"], "rms_ckv_bwd")
    dqkv_d, dbias_d = [], []
    for i, d in enumerate(DILATIONS):
        comm = _ToChipsComm(ffn_send[1:]) if (ffn_shards and i == 0) else None
        (dqkv_i, dbias_i), got = _dil_bwd(qkv_v[i], do_a_v[i], out_a_v[i], lse_a_v[i], bias, i, d,
                                          f"dil_bwd_{d}", comm)
        if comm is not None:
            ffn_r2 = list(ffn_r2) + list(got)
        dqkv_d.append(dqkv_i)
        dbias_d.append(dbias_i)
    dqkv = _sum_views_bf16(dqkv_d, "dil_bwd_sum").reshape(T, P_QKV)
    g_rel_bias = _bias_grad(dbias_d, buckets_dev, "rel_bias_grad")[:, :N_BUCKETS].T
    dproj = jnp.concatenate([dqkv, dckv, dkrw, dcq], axis=1)
    gw_in = _mm(h1f, dproj, "tn", F32, "mm_in_dw")
    mix_a4 = mix_r2 = None
    if ffn_shards:
        nat = [_w_in_from_kernel(gw_in), _w_uq_from_kernel(gw_uq), _w_ukv_from_kernel(gw_kv)]
        g8 = [_shards_from_full(g) for g in nat] + [gw_out]
        mix_a4, mix_send = _rs_first([g.reshape(N_DEV, -1, g.shape[-1]) for g in g8], "mix")
        dh1, mix_r2 = _mm(dproj, w_in, "nt", F32, "mm_in_dx", comm=_ToChipsComm(mix_send))
    else:
        dh1 = _mm(dproj, w_in, "nt", F32, "mm_in_dx")
    dh1 = dh1.reshape(B, S, D)
    grad_x, dsh1, dsc1, dg_norm1 = _adaln_bwd(dh1, x, gains["g_norm1"], sc1, dx1, "adaln1_bwd")
    gmod = jnp.concatenate([dsh1, dsc1, dg1, dsh2, dsc2, dg2], axis=-1).reshape(B, N_MOD * D)
    grads = dict(w_in=gw_in, w_uq=gw_uq, w_kv=gw_kv, w_out=gw_out, w_ffn_in=gw_ffn_in, w_ffn_out=gw_ffn_out,
                 g_norm1=dg_norm1, g_cq=dg_cq, g_ckv=dg_ckv, rel_bias=g_rel_bias, g_out_a=dg_out_a,
                 g_out_b=dg_out_b, g_norm2=dg_norm2, g_final=dg_final, ffn_pending=(ffn_a4, ffn_r2),
                 mix_pending=(mix_a4, mix_r2))
    return loss, grad_x, gmod, grads


def _w_in_to_kernel(w):
    z = lambda n: jnp.zeros((w.shape[0], n), w.dtype)
    i3, i4, i5 = 3 * D_A, 3 * D_A + Q_LORA, 3 * D_A + Q_LORA + KV_LORA
    return jnp.concatenate([w[:, :i3], w[:, i4:i5], z(NOPE_DIM), w[:, i5:], z(LANES - NOPE_DIM - ROPE_DIM),
                            w[:, i3:i4]], axis=1)


def _w_in_from_kernel(g):
    o = P_QKV + KV_LORA
    return jnp.concatenate([g[:, :P_QKV], g[:, o + LANES:], g[:, P_QKV:o],
                            g[:, o + NOPE_DIM:o + NOPE_DIM + ROPE_DIM]], axis=1)


def _w_uq_to_kernel(w):
    w3 = w.reshape(Q_LORA, N_HEADS, NOPE_DIM + ROPE_DIM)
    return jnp.pad(w3, ((0, 0), (0, 0), (0, LANES - NOPE_DIM - ROPE_DIM))).reshape(Q_LORA, N_HEADS * LANES)


def _w_uq_from_kernel(g):
    return g.reshape(Q_LORA, N_HEADS, LANES)[:, :, :NOPE_DIM + ROPE_DIM].reshape(Q_LORA, -1)


def _w_ukv_to_kernel(w):
    w3 = w.reshape(KV_LORA, N_HEADS, 2 * HEAD_DIM)
    wk = jnp.pad(w3[:, :, :NOPE_DIM], ((0, 0), (0, 0), (0, LANES - NOPE_DIM))).reshape(KV_LORA, N_HEADS * LANES)
    wv = w3[:, :, NOPE_DIM:].reshape(KV_LORA, D_B)
    return jnp.concatenate([wk, wv], axis=1)


def _w_ukv_from_kernel(g):
    gk = g[:, :N_HEADS * LANES].reshape(KV_LORA, N_HEADS, LANES)[:, :, :NOPE_DIM]
    gv = g[:, N_HEADS * LANES:].reshape(KV_LORA, N_HEADS, HEAD_DIM)
    return jnp.concatenate([gk, gv], axis=2).reshape(KV_LORA, -1)


MESH = pl.DeviceIdType.MESH


def _my_place():
    return lax.axis_index("x"), lax.axis_index("y"), lax.axis_index("c")


def _other_chips(x, y):
    return [(1 - x, y), (x, 1 - y), (1 - x, 1 - y)]


def _allgather8(x_shard, name, in_hbm):
    m_per, n = x_shard.shape
    space = pl.ANY if in_hbm else pltpu.VMEM

    def body(x_ref, out_ref, send_sems, recv_sems, local_sem):
        x, y, c = _my_place()
        me, sibling = (x, y, c), (x, y, 1 - c)
        chips = _other_chips(x, y)

        def rows(px, py, pc):
            return out_ref.at[pl.ds((4 * px + 2 * py + pc) * m_per, m_per), :]

        def copy(k, block, to, src=None):
            return pltpu.make_async_remote_copy(
                src_ref=rows(*block) if src is None else src, dst_ref=rows(*block),
                send_sem=send_sems.at[k], recv_sem=recv_sems.at[k], device_id=to, device_id_type=MESH)

        mine = pltpu.make_async_copy(x_ref, rows(*me), local_sem)
        mine.start()
        first = [copy(0, me, sibling, src=x_ref)]
        first += [copy(1 + j, me, (*chip, c), src=x_ref) for j, chip in enumerate(chips)]
        for cp in first:
            cp.start()
        passed = [copy(4 + j, (*chip, c), sibling) for j, chip in enumerate(chips)]
        for j, chip in enumerate(chips):
            copy(1 + j, (*chip, c), me).wait_recv()
            passed[j].start()
        copy(0, sibling, me).wait_recv()
        for j, chip in enumerate(chips):
            copy(4 + j, (*chip, 1 - c), me).wait_recv()
        for cp in first + passed:
            cp.wait_send()
        mine.wait()

    return pl.pallas_call(
        body, name=name,
        out_shape=jax.ShapeDtypeStruct((N_DEV * m_per, n), x_shard.dtype),
        in_specs=[pl.BlockSpec(memory_space=space)],
        out_specs=pl.BlockSpec(memory_space=space),
        scratch_shapes=[pltpu.SemaphoreType.DMA((7,)), pltpu.SemaphoreType.DMA((7,)), pltpu.SemaphoreType.DMA],
        compiler_params=pltpu.CompilerParams(vmem_limit_bytes=VMEM_LIMIT),
    )(x_shard)


def _hbm_specs(n):
    return [pl.BlockSpec(memory_space=pl.ANY)] * n


class _GatherComm:
    def __init__(self, shards):
        self.n = n = len(shards)
        self.inputs = [s.reshape(2, s.shape[0] // 2, s.shape[1]) for s in shards]
        self.out_shape = [jax.ShapeDtypeStruct((N_DEV,) + s.shape[1:], s.dtype) for s in self.inputs]
        self.scratch = [pltpu.SemaphoreType.DMA((7 * n,)), pltpu.SemaphoreType.DMA((7 * n,))]

    def _parts(self, xs, outs, sems):
        send_sems, recv_sems = sems
        x, y, c = _my_place()

        def blk(k, px, py, pc):
            return outs[k].at[4 * px + 2 * py + pc]

        def copy(k, kind, block, to, own=False):
            return pltpu.make_async_remote_copy(
                src_ref=xs[k].at[c] if own else blk(k, *block), dst_ref=blk(k, *block),
                send_sem=send_sems.at[7 * k + kind], recv_sem=recv_sems.at[7 * k + kind],
                device_id=to, device_id_type=MESH)

        def whole(k):
            return pltpu.make_async_remote_copy(
                src_ref=xs[k], dst_ref=outs[k].at[pl.ds(4 * x + 2 * y, 2)],
                send_sem=send_sems.at[7 * k], recv_sem=recv_sems.at[7 * k],
                device_id=(x, y, 1 - c), device_id_type=MESH)

        me, sibling = (x, y, c), (x, y, 1 - c)
        chips = _other_chips(x, y)
        first = []
        for k in range(self.n):
            first.append(whole(k))
            first += [copy(k, 1 + j, me, (*chip, c), own=True) for j, chip in enumerate(chips)]
        return copy, whole, me, sibling, chips, c, first

    def start(self, xs, outs, sems):
        for cp in self._parts(xs, outs, sems)[-1]:
            cp.start()

    def finish(self, xs, outs, sems):
        copy, whole, me, sibling, chips, c, first = self._parts(xs, outs, sems)
        passed = []
        for j, chip in enumerate(chips):
            for k in range(self.n):
                copy(k, 1 + j, (*chip, c), me).wait_recv()
                fwd = copy(k, 4 + j, (*chip, c), sibling)
                fwd.start()
                passed.append(fwd)
        for k in range(self.n):
            whole(k).wait_recv()
        for j, chip in enumerate(chips):
            for k in range(self.n):
                copy(k, 4 + j, (*chip, 1 - c), me).wait_recv()
        for cp in first + passed:
            cp.wait_send()


class _ToChipsComm:
    def __init__(self, a4s):
        self.inputs = list(a4s)
        self.n = n = len(a4s)
        nc = N_CHIP - 1
        self.out_shape = [jax.ShapeDtypeStruct((nc,) + a.shape[1:], a.dtype) for a in a4s]
        self.scratch = [pltpu.SemaphoreType.DMA((nc * n,)), pltpu.SemaphoreType.DMA((nc * n,))]

    def _copies(self, as_, rs, sems):
        send_sems, recv_sems = sems
        x, y, c = _my_place()
        nc = N_CHIP - 1
        return [pltpu.make_async_remote_copy(
            src_ref=as_[k].at[2 * cx + cy], dst_ref=rs[k].at[j], send_sem=send_sems.at[nc * k + j],
            recv_sem=recv_sems.at[nc * k + j], device_id=(cx, cy, c), device_id_type=MESH)
            for k in range(self.n) for j, (cx, cy) in enumerate(_other_chips(x, y))]

    def start(self, as_, rs, sems):
        for cp in self._copies(as_, rs, sems):
            cp.start()

    def finish(self, as_, rs, sems):
        for cp in self._copies(as_, rs, sems):
            cp.wait()


def _run_comm(comm, name):
    n = comm.n

    def body(*refs):
        ins, outs, sems = refs[:n], refs[n:2 * n], refs[2 * n:]
        comm.start(ins, outs, sems)
        comm.finish(ins, outs, sems)

    return pl.pallas_call(
        body, name=name, out_shape=comm.out_shape, in_specs=_hbm_specs(n), out_specs=_hbm_specs(n),
        scratch_shapes=comm.scratch,
    )(*comm.inputs)


def _gather_weights(shards, name):
    return _run_comm(_GatherComm(shards), name)


class _ToSiblingComm:
    def __init__(self, g8s):
        self.inputs = list(g8s)
        self.n = n = len(g8s)
        self.out_shape = [jax.ShapeDtypeStruct((N_CHIP,) + g.shape[1:], g.dtype) for g in g8s]
        self.scratch = [pltpu.SemaphoreType.DMA((N_CHIP * n,)), pltpu.SemaphoreType.DMA((N_CHIP * n,))]

    def _copies(self, gs, rs, sems):
        send_sems, recv_sems = sems
        x, y, c = _my_place()
        return [pltpu.make_async_remote_copy(
            src_ref=gs[k].at[2 * s + 1 - c], dst_ref=rs[k].at[s], send_sem=send_sems.at[N_CHIP * k + s],
            recv_sem=recv_sems.at[N_CHIP * k + s], device_id=(x, y, 1 - c), device_id_type=MESH)
            for k in range(self.n) for s in range(N_CHIP)]

    def start(self, gs, rs, sems):
        for cp in self._copies(gs, rs, sems):
            cp.start()

    def finish(self, gs, rs, sems):
        for cp in self._copies(gs, rs, sems):
            cp.wait()


def _rs_to_chips(a4s, name):
    return _run_comm(_ToChipsComm(a4s), name)


def _swap_halves(hs, name):
    n = len(hs)

    def body(*refs):
        o_refs = refs[n:2 * n]
        send_sems, recv_sems = refs[2 * n:]
        x, y, c = _my_place()

        def remote(k, slot):
            return pltpu.make_async_remote_copy(
                src_ref=o_refs[k].at[slot], dst_ref=o_refs[k].at[slot], send_sem=send_sems.at[k],
                recv_sem=recv_sems.at[k], device_id=(x, y, 1 - c), device_id_type=MESH)

        sends = [remote(k, c) for k in range(n)]
        for cp in sends:
            cp.start()
        for k in range(n):
            remote(k, 1 - c).wait_recv()
        for cp in sends:
            cp.wait_send()

    return pl.pallas_call(
        body, name=name,
        out_shape=[jax.ShapeDtypeStruct(h.shape, h.dtype) for h in hs],
        in_specs=_hbm_specs(n), out_specs=_hbm_specs(n),
        input_output_aliases={k: k for k in range(n)},
        scratch_shapes=[pltpu.SemaphoreType.DMA((n,)), pltpu.SemaphoreType.DMA((n,))],
    )(*hs)


ADD_TILES = 2


def _add_blocks(a_list, a_idx_fn, others_list, ns, sel, name, out_blocks=None, out_idx_fn=None,
                bf16_copy=False):
    out_blocks = out_blocks or ns
    out_idx_fn = out_idx_fn or (lambda s, sel_ref: s)
    n = len(a_list)
    n_o = len(others_list[0])
    per = 1 + n_o

    def body(sel_ref, *refs):
        for k in range(n):
            ins = refs[k * per:(k + 1) * per]
            acc = ins[0][0]
            for r in ins[1:]:
                acc = acc + r[0].astype(F32)
            refs[n * per + k][0] = acc
            if bf16_copy:
                refs[n * per + n + k][0] = acc.astype(BF16)

    in_specs, args, out_specs, out_shape = [], [], [], []
    for a, others in zip(a_list, others_list):
        _, R, N = a.shape
        tr = R // ADD_TILES
        assert tr % 8 == 0, a.shape
        in_specs.append(pl.BlockSpec((1, tr, N), lambda s, i, sel_ref: (a_idx_fn(s, sel_ref), i, 0)))
        args.append(a)
        for arr, fixed in others:
            if fixed is None:
                in_specs.append(pl.BlockSpec((1, tr, N), lambda s, i, sel_ref: (s, i, 0)))
            else:
                in_specs.append(pl.BlockSpec((1, tr, N), lambda s, i, sel_ref, fixed=fixed: (fixed, i, 0)))
            args.append(arr)
        out_specs.append(pl.BlockSpec((1, tr, N), lambda s, i, sel_ref: (out_idx_fn(s, sel_ref), i, 0)))
        out_shape.append(jax.ShapeDtypeStruct((out_blocks, R, N), a.dtype))
    if bf16_copy:
        out_specs = out_specs + out_specs
        out_shape = out_shape + [jax.ShapeDtypeStruct(o.shape, BF16) for o in out_shape]
    grid_spec = pltpu.PrefetchScalarGridSpec(num_scalar_prefetch=1, grid=(ns, ADD_TILES), in_specs=in_specs,
                                             out_specs=out_specs)
    return pl.pallas_call(
        body, name=name, out_shape=out_shape, grid_spec=grid_spec,
        compiler_params=_cparams(("parallel", "parallel")),
    )(sel, *args)


def _rs_first(g8s, tag, r1=None):
    c_sel = jnp.reshape(lax.axis_index("c"), (1,)).astype(jnp.int32)
    if r1 is None:
        r1 = _run_comm(_ToSiblingComm(g8s), f"rs_to_sibling_{tag}")
    res = _add_blocks(g8s, lambda s, sel: 2 * s + sel[0], [[(r, None)] for r in r1], N_CHIP, c_sel,
                      f"rs_add_sibling_{tag}", bf16_copy=True)
    return list(res[:len(g8s)]), list(res[len(g8s):])


def _rs_last(a4s, r2s, tag):
    sel = jnp.stack([2 * lax.axis_index("x") + lax.axis_index("y"), lax.axis_index("c")]).astype(jnp.int32)
    h = _add_blocks(a4s, lambda s, sel: sel[0], [[(r, 0), (r, 1), (r, 2)] for r in r2s], 1, sel,
                    f"rs_add_chips_{tag}", out_blocks=2, out_idx_fn=lambda s, sel: sel[1])
    full = _swap_halves(h, f"rs_swap_halves_{tag}")
    return [f.reshape(2 * f.shape[1], f.shape[2]) for f in full]


def _reduce_scatter(g8s, tag):
    a4, a4_bf16 = _rs_first(g8s, tag)
    return _rs_last(a4, _rs_to_chips(a4_bf16, f"rs_to_chips_{tag}"), tag)


def _ada_fwd(c_all, w_ada, b_ada, name):
    nb, D = c_all.shape
    ncol = w_ada.shape[1]
    tc = 512

    def body(c_ref, w_ref, b_ref, o_ref):
        cv = c_ref[...]
        cond = (cv * jax.nn.sigmoid(cv)).astype(BF16)
        o_ref[...] = jnp.dot(cond, w_ref[...].astype(BF16), preferred_element_type=F32) + b_ref[...]

    return pl.pallas_call(
        body, name=name, out_shape=jax.ShapeDtypeStruct((nb, ncol), F32), grid=(ncol // tc,),
        in_specs=[pl.BlockSpec((nb, D), lambda j: (0, 0)), pl.BlockSpec((D, tc), lambda j: (0, j)),
                  pl.BlockSpec((1, tc), lambda j: (0, j))],
        out_specs=pl.BlockSpec((nb, tc), lambda j: (0, j)),
        compiler_params=_cparams(("parallel",)),
    )(c_all, w_ada, b_ada)


def _ada_bwd(c_all, gmod_cols, name):
    nb, D = c_all.shape
    ncol = gmod_cols.shape[1]
    tc = 512

    def body(c_ref, g_ref, o_ref):
        cv = c_ref[...]
        cond = (cv * jax.nn.sigmoid(cv)).astype(BF16)
        o_ref[...] = _dot_tn(cond, g_ref[...].astype(BF16))

    return pl.pallas_call(
        body, name=name, out_shape=jax.ShapeDtypeStruct((D, ncol), F32), grid=(ncol // tc,),
        in_specs=[pl.BlockSpec((nb, D), lambda j: (0, 0)), pl.BlockSpec((nb, tc), lambda j: (0, j))],
        out_specs=pl.BlockSpec((D, tc), lambda j: (0, j)),
        compiler_params=_cparams(("parallel",)),
    )(c_all, gmod_cols)


def _adam_math(w, g, m, v):
    m = ADAM_B1 * m + (1.0 - ADAM_B1) * g
    v = ADAM_B2 * v + (1.0 - ADAM_B2) * (g * g)
    m_hat = m / (1.0 - ADAM_B1 ** ADAM_STEP)
    v_hat = v / (1.0 - ADAM_B2 ** ADAM_STEP)
    delta = -ADAM_LR * (m_hat / (jnp.sqrt(v_hat) + ADAM_EPS) + ADAM_WD * w)
    return delta, m, v


def _adamw(w, g, m, v, name):
    rows, cols = w.shape
    tr = _pick(rows, (256, 192, 176, 128, 64, 8))

    def body(w_ref, g_ref, m_ref, v_ref, d_ref, mo_ref, vo_ref):
        d, mn, vn = _adam_math(w_ref[...], g_ref[...], m_ref[...], v_ref[...])
        d_ref[...] = d
        mo_ref[...] = mn
        vo_ref[...] = vn

    spec = pl.BlockSpec((tr, cols), lambda i: (i, 0))
    return pl.pallas_call(
        body, name=name, out_shape=[jax.ShapeDtypeStruct((rows, cols), F32)] * 3, grid=(rows // tr,),
        in_specs=[spec] * 4, out_specs=[spec] * 3, compiler_params=_cparams(("parallel",)),
    )(w, g, m, v)


VEC_ROWS = 8


def _adamw_rows(w, parts, m, v, name):
    n = w.shape[1]
    P = parts.shape[0]
    assert n % (VEC_ROWS * LANES) == 0, n
    shp = (VEC_ROWS, n // VEC_ROWS)

    def body(w_ref, p_ref, m_ref, v_ref, g_ref, d_ref, mo_ref, vo_ref):
        g = p_ref[0]
        for k in range(1, P):
            g = g + p_ref[k]
        d, mn, vn = _adam_math(w_ref[...], g, m_ref[...], v_ref[...])
        g_ref[...] = g
        d_ref[...] = d
        mo_ref[...] = mn
        vo_ref[...] = vn

    vec = pl.BlockSpec(shp, lambda i: (0, 0))
    out = pl.pallas_call(
        body, name=name, out_shape=[jax.ShapeDtypeStruct(shp, F32)] * 4, grid=(1,),
        in_specs=[vec, pl.BlockSpec((P,) + shp, lambda i: (0, 0, 0)), vec, vec], out_specs=[vec] * 4,
        compiler_params=_cparams(("arbitrary",)),
    )(w.reshape(shp), parts.reshape((P,) + shp), m.reshape(shp), v.reshape(shp))
    return [o.reshape(1, n) for o in out]


_SHARDED = ("w_in", "w_uq", "w_ukv", "w_out", "w_ffn_in", "w_ffn_out")
_SMALL = (("g_norm1", 1024), ("g_cq", 384), ("g_ckv", 256), ("rel_bias", 256), ("g_out_a", 512),
          ("g_out_b", 512), ("g_norm2", 1024), ("g_final", 1024))
_SMALL_PAD = 5120


def _full_from_shards(sh):
    return jnp.transpose(sh, (1, 0, 2)).reshape(sh.shape[1], -1)


def _shards_from_full(full):
    rows, cols = full.shape
    return jnp.transpose(full.reshape(rows, N_CHIP, cols // N_CHIP), (1, 0, 2))


def kernel(x, c, w_ada, b_ada, g_norm1, w_in, g_cq, w_uq, g_ckv, w_ukv, rel_bias, g_out_a, g_out_b, w_out, g_norm2, w_ffn_in, w_ffn_out, g_final, loss_target, m_w_ada, m_b_ada, m_g_norm1, m_w_in, m_g_cq, m_w_uq, m_g_ckv, m_w_ukv, m_rel_bias, m_g_out_a, m_g_out_b, m_w_out, m_g_norm2, m_w_ffn_in, m_w_ffn_out, m_g_final, v_w_ada, v_b_ada, v_g_norm1, v_w_in, v_g_cq, v_w_uq, v_g_ckv, v_w_ukv, v_rel_bias, v_g_out_a, v_g_out_b, v_w_out, v_g_norm2, v_w_ffn_in, v_w_ffn_out, v_g_final):
    names = ["w_ada", "b_ada", "g_norm1", "w_in", "g_cq", "w_uq", "g_ckv", "w_ukv", "rel_bias", "g_out_a",
             "g_out_b", "w_out", "g_norm2", "w_ffn_in", "w_ffn_out", "g_final"]
    W = dict(zip(names, [w_ada, b_ada, g_norm1, w_in, g_cq, w_uq, g_ckv, w_ukv, rel_bias, g_out_a, g_out_b,
                         w_out, g_norm2, w_ffn_in, w_ffn_out, g_final]))
    M = dict(zip(names, [m_w_ada, m_b_ada, m_g_norm1, m_w_in, m_g_cq, m_w_uq, m_g_ckv, m_w_ukv, m_rel_bias,
                         m_g_out_a, m_g_out_b, m_w_out, m_g_norm2, m_w_ffn_in, m_w_ffn_out, m_g_final]))
    V = dict(zip(names, [v_w_ada, v_b_ada, v_g_norm1, v_w_in, v_g_cq, v_w_uq, v_g_ckv, v_w_ukv, v_rel_bias,
                         v_g_out_a, v_g_out_b, v_w_out, v_g_norm2, v_w_ffn_in, v_w_ffn_out, v_g_final]))
    B, S, D = x.shape
    mx, my, mc = _my_place()
    dev = 4 * mx + 2 * my + mc
    chip = 2 * mx + my
    pad_rows = 8

    c_all = _allgather8(jnp.pad(c, ((0, pad_rows - B), (0, 0))), "ag_c", False)
    c_all = c_all.reshape(N_DEV, pad_rows, D)[:, :B].reshape(N_DEV * B, D)
    ada_cols = w_ada.shape[-1]
    b_cols = lax.dynamic_slice_in_dim(b_ada, chip * ada_cols, ada_cols, axis=1)
    mod_cols = _ada_fwd(c_all, w_ada[0], b_cols, "ada_fwd")
    mod_all = _allgather8(mod_cols, "ag_mod", False).reshape(N_DEV, N_DEV * B, ada_cols)[0::2]
    mod_all = jnp.transpose(mod_all, (1, 0, 2)).reshape(N_DEV * B, N_MOD * D)
    mod = lax.dynamic_slice_in_dim(mod_all, dev * B, B, axis=0)

    early = ("w_in", "w_uq", "w_ukv")
    got = _gather_weights([W[n][0].astype(BF16) for n in early], "ag_weights")
    full = {n: g.reshape((N_CHIP,) + W[n].shape[1:]) for n, g in zip(early, got)}
    wts = dict(w_in=_w_in_to_kernel(_full_from_shards(full["w_in"])),
               w_uq=_w_uq_to_kernel(_full_from_shards(full["w_uq"])),
               w_kv=_w_ukv_to_kernel(_full_from_shards(full["w_ukv"])))
    gains = dict(g_norm1=g_norm1, g_cq=g_cq, g_ckv=g_ckv, g_out_a=g_out_a, g_out_b=g_out_b, g_norm2=g_norm2,
                 g_final=g_final.reshape(1, D))

    loss, grad_x, gmod, grads = _local_step(x, loss_target, mod, wts, gains, rel_bias,
                                            ffn_shards=[w_ffn_in[0].astype(BF16), w_ffn_out[0].astype(BF16),
                                                        w_out[0].astype(BF16)])
    loss = lax.psum(loss[0, 0], ("x", "y", "c"))

    n_small = _SMALL_PAD
    cat = lambda dct: jnp.concatenate([dct[n].reshape(1, -1) for n, _ in _SMALL]
                                      + [jnp.zeros((1, _SMALL_PAD - sum(s for _, s in _SMALL)), F32)], axis=1)
    small = cat(grads)
    rows = jnp.concatenate([gmod, jnp.pad(small, ((0, 0), (0, N_MOD * D - n_small))),
                            jnp.zeros((pad_rows - B - 1, N_MOD * D), F32)], axis=0)
    rows_all = _allgather8(rows, "ag_small", False).reshape(N_DEV, pad_rows, N_MOD * D)
    gmod_all = rows_all[:, :B].reshape(N_DEV * B, N_MOD * D)
    small_parts = rows_all[:, B, :n_small]

    a4, r2 = grads["mix_pending"]
    ffn_a4, ffn_r2 = grads["ffn_pending"]
    G = dict(zip(_SHARDED, _rs_last(list(a4) + list(ffn_a4), list(r2) + list(ffn_r2), "all")))

    gmod_cols = lax.dynamic_slice_in_dim(gmod_all, chip * ada_cols, ada_cols, axis=1)
    G["w_ada"] = _ada_bwd(c_all, gmod_cols, "ada_bwd")
    delta, new_m, new_v = {}, {}, {}
    for n in ("w_ada",) + _SHARDED:
        shp = W[n].shape
        w2 = W[n].reshape(shp[-2], shp[-1])
        d_, m_, v_ = _adamw(w2, G[n], M[n].reshape(w2.shape), V[n].reshape(w2.shape), f"adamw_{n}")
        G[n], delta[n], new_m[n], new_v[n] = [a.reshape(shp) for a in (G[n], d_, m_, v_)]
    gs, ds_, ms_, vs_ = _adamw_rows(cat(W), small_parts, cat(M), cat(V), "adamw_small")
    off = 0
    for n, sz in _SMALL:
        shp = W[n].shape
        G[n], delta[n], new_m[n], new_v[n] = [a[:, off:off + sz].reshape(shp) for a in (gs, ds_, ms_, vs_)]
        off += sz
    G["b_ada"], delta["b_ada"], new_m["b_ada"], new_v["b_ada"] = _adamw_rows(b_ada, gmod_all, m_b_ada, v_b_ada,
                                                                          "adamw_b_ada")
    return (loss, grad_x, *[G[n] for n in names], *[delta[n] for n in names], *[new_m[n] for n in names],
            *[new_v[n] for n in names])
```

```python
import functools
import math

import numpy as np
import jax
import jax.numpy as jnp
from jax import lax
from jax.experimental import pallas as pl
from jax.experimental.pallas import tpu as pltpu

F32 = jnp.float32
BF16 = jnp.bfloat16

D_MODEL = 1024
SEQ = 2048
N_HEADS = 8
HEAD_DIM = 64
D_A = 512
D_B = 512
Q_LORA = 384
KV_LORA = 256
ROPE_DIM = 32
NOPE_DIM = 64
D_FF = 2816
N_MOD = 6
N_BUCKETS = 32
MAX_DISTANCE = 2048
ROPE_THETA = 10000.0
EPS = 1e-6
NEG = -1e30
BLK = 128
DILATIONS = (1, 4, 16)
SPAN = 128
MLA_SCALE = (NOPE_DIM + ROPE_DIM) ** -0.5
DIL_SCALE = HEAD_DIM ** -0.5

ADAM_LR = 0.001
ADAM_B1 = 0.9
ADAM_B2 = 0.999
ADAM_EPS = 1e-08
ADAM_WD = 0.01
ADAM_STEP = 10

N_DEV = 8
N_CHIP = 4
LANES = 128
VMEM_LIMIT = 48 * 1024 * 1024
MM_VMEM_BUDGET = 32 * 1024 * 1024

P_QKV = 3 * D_A
P_REST = KV_LORA + LANES + Q_LORA


def _cparams(sem=None):
    return pltpu.CompilerParams(dimension_semantics=sem, vmem_limit_bytes=VMEM_LIMIT)


def _pick(n, cands):
    for c in cands:
        if n % c == 0:
            return c
    raise ValueError(f"no tile for {n} in {cands}")


def _mm(a, b, mode, out_dtype, name, col_blocks=None, comm=None, halves=False):
    blocked = col_blocks is not None
    if mode == "nn":
        (M, K) = a.shape
        K2, N = (b.shape[1], b.shape[0] * b.shape[2]) if blocked else b.shape
    elif mode == "nt":
        (M, K) = (a.shape[1], 2 * a.shape[2]) if halves else a.shape
        N, K2 = (b.shape[1], b.shape[0] * b.shape[2]) if blocked else b.shape
    else:
        (K, M) = a.shape
        K2, N = (b.shape[1], 2 * b.shape[2]) if halves else b.shape
    assert K == K2, (a.shape, b.shape, mode)
    assert not halves or (blocked and col_blocks == 4 and mode in ("nt", "tn"))
    tn = _pick(N, (1408, 1024, 768, 512, 384, 256, 128))
    tk = _pick(K, (1408, 1152, 1024, 768, 512, 384, 256, 128))
    if blocked and mode == "nt":
        tk = K // col_blocks
    elif blocked:
        tn = N // col_blocks
    nk = K // tk

    def vmem_bytes(tm_):
        tiles = tm_ * tk * a.dtype.itemsize + tk * tn * b.dtype.itemsize + tm_ * tn * jnp.dtype(out_dtype).itemsize
        return 2 * tiles + tm_ * tn * 4

    tm = next(t for t in (1408, 1024, 512, 384, 256, 128) if M % t == 0 and vmem_bytes(t) <= MM_VMEM_BUDGET)
    out_shape = (M, N)
    out_spec = pl.BlockSpec((tm, tn), lambda i, j, k: (i, j))
    if mode == "nn":
        a_spec = pl.BlockSpec((tm, tk), lambda i, j, k: (i, k))
        b_spec = (pl.BlockSpec((None, tk, tn), lambda i, j, k: (j, k, 0)) if blocked
                  else pl.BlockSpec((tk, tn), lambda i, j, k: (k, j)))
        dn = (((1,), (0,)), ((), ()))
    elif mode == "nt":
        a_spec = (pl.BlockSpec((None, tm, tk), lambda i, j, k: (k // 2, i, k % 2)) if halves
                  else pl.BlockSpec((tm, tk), lambda i, j, k: (i, k)))
        b_spec = (pl.BlockSpec((None, tn, tk), lambda i, j, k: (k, j, 0)) if blocked
                  else pl.BlockSpec((tn, tk), lambda i, j, k: (j, k)))
        dn = (((1,), (1,)), ((), ()))
    else:
        a_spec = pl.BlockSpec((tk, tm), lambda i, j, k: (k, i))
        b_spec = (pl.BlockSpec((None, tk, tn), lambda i, j, k: (j // 2, k, j % 2)) if halves
                  else pl.BlockSpec((tk, tn), lambda i, j, k: (k, j)))
        dn = (((0,), (0,)), ((), ()))
        if blocked:
            out_shape = (col_blocks, M, tn)
            out_spec = pl.BlockSpec((None, tm, tn), lambda i, j, k: (j, i, 0))

    def body(a_ref, b_ref, o_ref, acc_ref):
        k = pl.program_id(2)

        @pl.when(k == 0)
        def _():
            acc_ref[...] = jnp.zeros_like(acc_ref)

        acc_ref[...] += lax.dot_general(a_ref[...].astype(BF16), b_ref[...].astype(BF16), dn,
                                        preferred_element_type=F32)

        @pl.when(k == nk - 1)
        def _():
            o_ref[...] = acc_ref[...].astype(o_ref.dtype)

    if comm is not None:
        (out,), got = _host_call(
            body, comm, name=name, out_shape=[jax.ShapeDtypeStruct(out_shape, out_dtype)],
            grid=(M // tm, N // tn, nk), in_specs=[a_spec, b_spec], out_specs=[out_spec],
            scratch_shapes=[pltpu.VMEM((tm, tn), F32)], args=(a, b))
        return out, got
    return pl.pallas_call(
        body, name=name,
        out_shape=jax.ShapeDtypeStruct(out_shape, out_dtype),
        grid=(M // tm, N // tn, nk),
        in_specs=[a_spec, b_spec],
        out_specs=out_spec,
        scratch_shapes=[pltpu.VMEM((tm, tn), F32)],
        compiler_params=_cparams(("parallel", "parallel", "arbitrary")),
    )(a, b)


ROW_TILE = 512


def _adaln_fwd(x, g, sc, sh, name, mix=None, gate=None):
    B, S, D = x.shape
    ts = ROW_TILE
    has_res = mix is not None

    def body(*refs):
        if has_res:
            x_ref, g_ref, sc_ref, sh_ref, mix_ref, gate_ref, h_ref, xr_ref = refs
            xr = x_ref[0] + gate_ref[0] * mix_ref[0]
            xr_ref[0] = xr
        else:
            x_ref, g_ref, sc_ref, sh_ref, h_ref = refs
            xr = x_ref[0]
        r = lax.rsqrt(jnp.mean(xr * xr, axis=-1, keepdims=True) + EPS)
        xn = (xr * r) * g_ref[...]
        h_ref[0] = (xn * (1.0 + sc_ref[0]) + sh_ref[0]).astype(h_ref.dtype)

    tok = pl.BlockSpec((1, ts, D), lambda b, s: (b, s, 0))
    per_b = pl.BlockSpec((1, 1, D), lambda b, s: (b, 0, 0))
    vec = pl.BlockSpec((1, D), lambda b, s: (0, 0))
    in_specs = [tok, vec, per_b, per_b]
    args = [x, g, sc, sh]
    out_shape = [jax.ShapeDtypeStruct((B, S, D), BF16)]
    out_specs = [tok]
    if has_res:
        in_specs += [tok, per_b]
        args += [mix, gate]
        out_shape.append(jax.ShapeDtypeStruct((B, S, D), F32))
        out_specs.append(tok)
    out = pl.pallas_call(
        body, name=name, out_shape=out_shape, grid=(B, S // ts),
        in_specs=in_specs, out_specs=out_specs,
        compiler_params=_cparams(("parallel", "parallel")),
    )(*args)
    return out if has_res else out[0]


def _adaln_bwd(dh, x, g, sc, dres, name, mix=None, gate=None, comm=None):
    B, S, D = x.shape
    ts = ROW_TILE
    has_res = mix is not None

    def body(*refs):
        if has_res:
            (dh_ref, x_ref, g_ref, sc_ref, dres_ref, mix_ref, gate_ref,
             dx_ref, dsh_ref, dsc_ref, dg_ref, dgate_ref, dmix_ref) = refs
        else:
            (dh_ref, x_ref, g_ref, sc_ref, dres_ref, dx_ref, dsh_ref, dsc_ref, dg_ref) = refs
        b, s = pl.program_id(0), pl.program_id(1)
        xv = x_ref[0]
        dhv = dh_ref[0]
        gv = g_ref[...]
        r = lax.rsqrt(jnp.mean(xv * xv, axis=-1, keepdims=True) + EPS)
        n = xv * r
        xn = n * gv
        dxn = dhv * (1.0 + sc_ref[0])
        dn = dxn * gv
        dx = r * (dn - n * jnp.mean(dn * n, axis=-1, keepdims=True)) + dres_ref[0]
        dx_ref[0] = dx

        @pl.when(s == 0)
        def _():
            dsh_ref[...] = jnp.zeros_like(dsh_ref)
            dsc_ref[...] = jnp.zeros_like(dsc_ref)
            if has_res:
                dgate_ref[...] = jnp.zeros_like(dgate_ref)

        @pl.when((s == 0) & (b == 0))
        def _():
            dg_ref[...] = jnp.zeros_like(dg_ref)

        dsh_ref[0] += jnp.sum(dhv, axis=0, keepdims=True)
        dsc_ref[0] += jnp.sum(dhv * xn, axis=0, keepdims=True)
        dg_ref[...] += jnp.sum(dxn * n, axis=0, keepdims=True)
        if has_res:
            dgate_ref[0] += jnp.sum(dx * mix_ref[0], axis=0, keepdims=True)
            dmix_ref[0] = (dx * gate_ref[0]).astype(dmix_ref.dtype)

    tok = pl.BlockSpec((1, ts, D), lambda b, s: (b, s, 0))
    per_b = pl.BlockSpec((1, 1, D), lambda b, s: (b, 0, 0))
    vec = pl.BlockSpec((1, D), lambda b, s: (0, 0))
    in_specs = [tok, tok, vec, per_b, tok]
    args = [dh, x, g, sc, dres]
    out_shape = [jax.ShapeDtypeStruct((B, S, D), F32), jax.ShapeDtypeStruct((B, 1, D), F32),
                 jax.ShapeDtypeStruct((B, 1, D), F32), jax.ShapeDtypeStruct((1, D), F32)]
    out_specs = [tok, per_b, per_b, vec]
    if has_res:
        in_specs += [tok, per_b]
        args += [mix, gate]
        out_shape += [jax.ShapeDtypeStruct((B, 1, D), F32), jax.ShapeDtypeStruct((B, S, D), BF16)]
        out_specs += [per_b, tok]
    res, got = _host_call(body, comm, name=name, out_shape=out_shape, grid=(B, S // ts), in_specs=in_specs,
                          out_specs=out_specs, scratch_shapes=[], args=args)
    return (list(res) + [got]) if comm is not None else res


def _rms_fwd(x, col_blk, n, g, name, n_real=None):
    T = x.shape[0]
    tr = 512
    nr = float(n_real or n)

    def body(x_ref, g_ref, y_ref):
        xv = x_ref[...]
        r = lax.rsqrt(jnp.sum(xv * xv, axis=-1, keepdims=True) / nr + EPS)
        y_ref[...] = ((xv * r) * g_ref[...]).astype(y_ref.dtype)

    return pl.pallas_call(
        body, name=name, out_shape=jax.ShapeDtypeStruct((T, n), BF16), grid=(T // tr,),
        in_specs=[pl.BlockSpec((tr, n), lambda i: (i, col_blk)), pl.BlockSpec((1, n), lambda i: (0, 0))],
        out_specs=pl.BlockSpec((tr, n), lambda i: (i, 0)),
        compiler_params=_cparams(("parallel",)),
    )(x, g)


def _rms_bwd(dy, dy_blk, x, x_blk, n, g, name, out_dtype=BF16):
    T = x.shape[0]
    tr = 512

    def body(dy_ref, x_ref, g_ref, dx_ref, dg_ref):
        xv = x_ref[...]
        dyv = dy_ref[...].astype(F32)
        r = lax.rsqrt(jnp.mean(xv * xv, axis=-1, keepdims=True) + EPS)
        nrm = xv * r
        dn = dyv * g_ref[...]
        dx_ref[...] = (r * (dn - nrm * jnp.mean(dn * nrm, axis=-1, keepdims=True))).astype(dx_ref.dtype)

        @pl.when(pl.program_id(0) == 0)
        def _():
            dg_ref[...] = jnp.zeros_like(dg_ref)

        dg_ref[...] += jnp.sum(dyv * nrm, axis=0, keepdims=True)

    return pl.pallas_call(
        body, name=name,
        out_shape=[jax.ShapeDtypeStruct((T, n), out_dtype), jax.ShapeDtypeStruct((1, n), F32)],
        grid=(T // tr,),
        in_specs=[pl.BlockSpec((tr, n), lambda i: (i, dy_blk)), pl.BlockSpec((tr, n), lambda i: (i, x_blk)),
                  pl.BlockSpec((1, n), lambda i: (0, 0))],
        out_specs=[pl.BlockSpec((tr, n), lambda i: (i, 0)), pl.BlockSpec((1, n), lambda i: (0, 0))],
        compiler_params=_cparams(("arbitrary",)),
    )(dy, x, g)


def _rms_bwd_views(dy, dy_blk, x, g, name):
    B, S, n = x.shape
    tiles = S // VIEW_TILE

    def body(dy_ref, x_ref, g_ref, d1_ref, d4_ref, d16_ref, dg_ref, dx_s):
        xv = x_ref[0]
        dyv = dy_ref[...]
        r = lax.rsqrt(jnp.mean(xv * xv, axis=-1, keepdims=True) + EPS)
        nrm = xv * r
        dn = dyv * g_ref[...]
        dx = r * (dn - nrm * jnp.mean(dn * nrm, axis=-1, keepdims=True))
        d1_ref[0] = dx.astype(d1_ref.dtype)
        _put_tile(dx_s, dx)
        _tile_to_view(dx_s, d4_ref, DILATIONS[1], n)
        _tile_to_view(dx_s, d16_ref, DILATIONS[2], n)

        @pl.when((pl.program_id(0) == 0) & (pl.program_id(1) == 0))
        def _():
            dg_ref[...] = jnp.zeros_like(dg_ref)

        dg_ref[...] += jnp.sum(dyv * nrm, axis=0, keepdims=True)

    res = pl.pallas_call(
        body, name=name,
        out_shape=[_view_shape(B, S, d, n, BF16) for d in DILATIONS] + [jax.ShapeDtypeStruct((1, n), F32)],
        grid=(B, tiles),
        in_specs=[pl.BlockSpec((VIEW_TILE, n), lambda b, t: (b * tiles + t, dy_blk)), _view_spec(1, n),
                  pl.BlockSpec((1, n), lambda b, t: (0, 0))],
        out_specs=[_view_spec(d, n) for d in DILATIONS] + [pl.BlockSpec((1, n), lambda b, t: (0, 0))],
        scratch_shapes=[_tile_scratch(n)],
        compiler_params=_cparams(("arbitrary", "arbitrary")),
    )(dy, x, g)
    return res[:len(DILATIONS)], res[len(DILATIONS)]


FFN_TILE = 1408


def _ffn_in_fwd(h, w4, name):
    T, D = h.shape
    tm, tc = 512, FFN_TILE
    nc = D_FF // tc

    def body(h_ref, wg_ref, wu_ref, gu_ref, act_ref):
        hv = h_ref[...]
        g = jnp.dot(hv, wg_ref[...], preferred_element_type=F32)
        u = jnp.dot(hv, wu_ref[...], preferred_element_type=F32)
        gu_ref[0] = g
        gu_ref[1] = u
        act_ref[...] = (g * jax.nn.sigmoid(g) * u).astype(act_ref.dtype)

    return pl.pallas_call(
        body, name=name,
        out_shape=[jax.ShapeDtypeStruct((2, T, D_FF), F32), jax.ShapeDtypeStruct((T, D_FF), BF16)],
        grid=(nc, T // tm),
        in_specs=[pl.BlockSpec((tm, D), lambda j, i: (i, 0)),
                  pl.BlockSpec((None, D, tc), lambda j, i: (j, 0, 0)),
                  pl.BlockSpec((None, D, tc), lambda j, i: (j + nc, 0, 0))],
        out_specs=[pl.BlockSpec((2, tm, tc), lambda j, i: (0, i, j)), pl.BlockSpec((tm, tc), lambda j, i: (i, j))],
        compiler_params=_cparams(("parallel", "parallel")),
    )(h, w4, w4)


def _ffn_out_bwd(df, w_out, gu, name):
    T, D = df.shape
    tm, tc = 512, FFN_TILE

    def body(df_ref, w_ref, gu_ref, dgu_ref):
        da = _dot_nt(df_ref[...], w_ref[...])
        g, u = gu_ref[0], gu_ref[1]
        sg = jax.nn.sigmoid(g)
        dgu_ref[0] = (da * u * (sg * (1.0 + g * (1.0 - sg)))).astype(dgu_ref.dtype)
        dgu_ref[1] = (da * (g * sg)).astype(dgu_ref.dtype)

    halves = pl.BlockSpec((2, tm, tc), lambda j, i: (0, i, j))
    return pl.pallas_call(
        body, name=name, out_shape=jax.ShapeDtypeStruct((2, T, D_FF), BF16), grid=(D_FF // tc, T // tm),
        in_specs=[pl.BlockSpec((tm, D), lambda j, i: (i, 0)), pl.BlockSpec((tc, D), lambda j, i: (j, 0)), halves],
        out_specs=halves,
        compiler_params=_cparams(("parallel", "parallel")),
    )(df, w_out, gu)


def _final_loss(x1, f, g2, gf, target, name):
    B, S, D = x1.shape
    ts = ROW_TILE

    def body(x1_ref, f_ref, g2_ref, gf_ref, t_ref, dx_ref, df_ref, dg2_ref, dgf_ref, loss_ref):
        b, s = pl.program_id(0), pl.program_id(1)
        fv = f_ref[0]
        g2v = g2_ref[0]
        gfv = gf_ref[...]
        x2 = x1_ref[0] + g2v * fv
        r = lax.rsqrt(jnp.mean(x2 * x2, axis=-1, keepdims=True) + EPS)
        n = x2 * r
        e = n * gfv - t_ref[0]
        dy = e * (1.0 / D)
        dn = dy * gfv
        dx = r * (dn - n * jnp.mean(dn * n, axis=-1, keepdims=True))
        dx_ref[0] = dx
        df_ref[0] = (dx * g2v).astype(df_ref.dtype)

        @pl.when(s == 0)
        def _():
            dg2_ref[...] = jnp.zeros_like(dg2_ref)

        @pl.when((s == 0) & (b == 0))
        def _():
            dgf_ref[...] = jnp.zeros_like(dgf_ref)
            loss_ref[...] = jnp.zeros_like(loss_ref)

        dg2_ref[0] += jnp.sum(dx * fv, axis=0, keepdims=True)
        dgf_ref[...] += jnp.sum(dy * n, axis=0, keepdims=True)
        loss_ref[...] += 0.5 * jnp.sum(jnp.mean(e * e, axis=-1, keepdims=True), axis=0, keepdims=True)

    tok = pl.BlockSpec((1, ts, D), lambda b, s: (b, s, 0))
    per_b = pl.BlockSpec((1, 1, D), lambda b, s: (b, 0, 0))
    vec = pl.BlockSpec((1, D), lambda b, s: (0, 0))
    return pl.pallas_call(
        body, name=name,
        out_shape=[jax.ShapeDtypeStruct((B, S, D), F32), jax.ShapeDtypeStruct((B, S, D), BF16),
                   jax.ShapeDtypeStruct((B, 1, D), F32), jax.ShapeDtypeStruct((1, D), F32),
                   jax.ShapeDtypeStruct((1, LANES), F32)],
        grid=(B, S // ts),
        in_specs=[tok, tok, per_b, vec, tok],
        out_specs=[tok, tok, per_b, vec, pl.BlockSpec((1, LANES), lambda b, s: (0, 0))],
        compiler_params=_cparams(("arbitrary", "arbitrary")),
    )(x1, f, g2, gf, target)


def _rope_tables():
    half = ROPE_DIM // 2
    inv = ROPE_THETA ** (-jnp.arange(half, dtype=F32) / half)
    ang = jnp.arange(SEQ, dtype=F32)[:, None] * inv[None, :]
    cos, sin = jnp.cos(ang), jnp.sin(ang)
    one = jnp.ones((SEQ, NOPE_DIM), F32)
    zero = jnp.zeros((SEQ, NOPE_DIM), F32)
    cs = jnp.concatenate([one, cos, cos, one[:, :LANES - NOPE_DIM - ROPE_DIM]], axis=1)
    sn = jnp.concatenate([zero, -sin, sin, zero[:, :LANES - NOPE_DIM - ROPE_DIM]], axis=1)
    return cs, sn


def _rope_group(t, cs, sn):
    half = ROPE_DIM // 2
    lane = lax.broadcasted_iota(jnp.int32, t.shape, 1)
    partner = jnp.where(lane < NOPE_DIM + half, pltpu.roll(t, LANES - half, 1), pltpu.roll(t, half, 1))
    return t * cs + partner * sn


def _rope_apply(t, cs, sn, out_dtype, name, add=None, add_blk=0):
    B, S, W = t.shape
    G = W // LANES
    ts = ROW_TILE

    def body(*refs):
        if add is None:
            t_ref, cs_ref, sn_ref, o_ref = refs
            for gi in range(G):
                sl = slice(gi * LANES, (gi + 1) * LANES)
                o_ref[0, :, sl] = _rope_group(t_ref[0, :, sl], cs_ref[...], sn_ref[...]).astype(o_ref.dtype)
        else:
            t_ref, a_ref, cs_ref, sn_ref, o_ref = refs
            ra = _rope_group(a_ref[0], cs_ref[...], sn_ref[...])
            for gi in range(G):
                sl = slice(gi * LANES, (gi + 1) * LANES)
                o_ref[0, :, sl] = (t_ref[0, :, sl] + ra).astype(o_ref.dtype)

    tok = pl.BlockSpec((1, ts, W), lambda b, s: (b, s, 0))
    tab = pl.BlockSpec((ts, LANES), lambda b, s: (s, 0))
    in_specs, args = [tok], [t]
    if add is not None:
        in_specs.append(pl.BlockSpec((1, ts, LANES), lambda b, s: (b, s, add_blk)))
        args.append(add)
    in_specs += [tab, tab]
    args += [cs, sn]
    return pl.pallas_call(
        body, name=name, out_shape=jax.ShapeDtypeStruct((B, S, W), out_dtype), grid=(B, S // ts),
        in_specs=in_specs, out_specs=tok, compiler_params=_cparams(("parallel", "parallel")),
    )(*args)


def _krope_bwd(dkc, cs, sn_neg, name):
    B, S, W = dkc.shape
    G = W // LANES
    ts = ROW_TILE

    def body(d_ref, cs_ref, sn_ref, o_ref):
        acc = d_ref[0, :, 0:LANES]
        for gi in range(1, G):
            acc = acc + d_ref[0, :, gi * LANES:(gi + 1) * LANES]
        lane = lax.broadcasted_iota(jnp.int32, acc.shape, 1)
        rot = (lane >= NOPE_DIM) & (lane < NOPE_DIM + ROPE_DIM)
        acc = jnp.where(rot, acc, 0.0)
        o_ref[0] = _rope_group(acc, cs_ref[...], sn_ref[...]).astype(o_ref.dtype)

    tab = pl.BlockSpec((ts, LANES), lambda b, s: (s, 0))
    return pl.pallas_call(
        body, name=name, out_shape=jax.ShapeDtypeStruct((B, S, LANES), BF16), grid=(B, S // ts),
        in_specs=[pl.BlockSpec((1, ts, W), lambda b, s: (b, s, 0)), tab, tab],
        out_specs=pl.BlockSpec((1, ts, LANES), lambda b, s: (b, s, 0)),
        compiler_params=_cparams(("parallel", "parallel")),
    )(dkc, cs, sn_neg)


def _t5_bucket(dist):
    max_exact = N_BUCKETS // 2
    d = np.maximum(dist, 1).astype(np.float64)
    large = max_exact + (np.log(d / max_exact) / np.log(MAX_DISTANCE / max_exact)
                         * (N_BUCKETS - max_exact)).astype(np.int64)
    large = np.minimum(large, N_BUCKETS - 1)
    return np.where(dist < max_exact, dist, large).astype(np.int32)


def _band_buckets(dilation):
    a = np.arange(BLK)[None, :]
    bk = np.arange(2 * BLK)[:, None]
    steps = BLK + a - bk
    return _t5_bucket(np.clip(steps, 0, SPAN) * dilation)


def _head_mask(shape, hh):
    lane = lax.broadcasted_iota(jnp.int32, shape, 1)
    return (lane >= hh * HEAD_DIM) & (lane < (hh + 1) * HEAD_DIM)


def _dot_nt(a, b):
    return lax.dot_general(a, b, (((1,), (1,)), ((), ())), preferred_element_type=F32)


def _dot_tn(a, b):
    return lax.dot_general(a, b, (((0,), (0,)), ((), ())), preferred_element_type=F32)


def _dot_nn(a, b):
    return lax.dot_general(a, b, (((1,), (0,)), ((), ())), preferred_element_type=F32)


def _band_valid_t():
    bk = lax.broadcasted_iota(jnp.int32, (BLK, BLK), 0)
    a = lax.broadcasted_iota(jnp.int32, (BLK, BLK), 1)
    return bk >= a, bk <= a


def _dil_fwd(qkv, bias, branch, dilation, name, comm=None):
    B, n, _ = qkv.shape
    d = dilation
    nb = n // BLK
    qkv_v = qkv
    npair = N_HEADS // 2

    def body(cur_ref, prev_ref, bias_ref, o_ref, lse_ref, s_scr, e_scr):
        i = pl.program_id(2)
        vprev, vcur = _band_valid_t()
        vprev = vprev & (i > 0)
        for p in range(npair):
            q = cur_ref[0, :, p * LANES:(p + 1) * LANES]
            kc = cur_ref[0, :, D_A + p * LANES:D_A + (p + 1) * LANES]
            kp = prev_ref[0, :, D_A + p * LANES:D_A + (p + 1) * LANES]
            for hh in range(2):
                h = 2 * p + hh
                qm = jnp.where(_head_mask((BLK, LANES), hh), q, jnp.zeros_like(q))
                s_scr[h, 0:BLK, :] = _dot_nt(kp, qm)
                s_scr[h, BLK:2 * BLK, :] = _dot_nt(kc, qm)
        ms = []
        for h in range(N_HEADS):
            s_p = jnp.where(vprev, s_scr[h, 0:BLK, :] * DIL_SCALE + bias_ref[h, 0:BLK, :], NEG)
            s_c = jnp.where(vcur, s_scr[h, BLK:2 * BLK, :] * DIL_SCALE + bias_ref[h, BLK:2 * BLK, :], NEG)
            m = jnp.maximum(jnp.max(s_p, axis=0, keepdims=True), jnp.max(s_c, axis=0, keepdims=True))
            e_scr[h, 0:BLK, :] = jnp.exp(s_p - m).astype(BF16)
            e_scr[h, BLK:2 * BLK, :] = jnp.exp(s_c - m).astype(BF16)
            ms.append(m)
        rows0 = _row_mask((LANES, BLK), 0)
        for p in range(npair):
            sl = slice(p * LANES, (p + 1) * LANES)
            vsl = slice(2 * D_A + p * LANES, 2 * D_A + (p + 1) * LANES)
            vct = jnp.transpose(cur_ref[0, :, vsl].astype(F32)).astype(BF16)
            vpt = jnp.transpose(prev_ref[0, :, vsl].astype(F32)).astype(BF16)
            acc = []
            for hh in range(2):
                h = 2 * p + hh
                mine = _row_mask((LANES, BLK), hh)
                one = jnp.ones_like(vct)
                acc.append(_dot_nn(jnp.where(mine, vpt, one), e_scr[h, 0:BLK, :])
                           + _dot_nn(jnp.where(mine, vct, one), e_scr[h, BLK:2 * BLK, :]))
            l0 = acc[0][HEAD_DIM:HEAD_DIM + 1, :]
            l1 = acc[1][0:1, :]
            o_t = jnp.where(rows0, acc[0] / l0, acc[1] / l1)
            lse_t = jnp.where(rows0, ms[2 * p] + jnp.log(l0), ms[2 * p + 1] + jnp.log(l1))
            o_ref[0, :, sl] = jnp.transpose(o_t)
            lse_ref[0, :, sl] = jnp.transpose(lse_t)

    cur = pl.BlockSpec((1, BLK, P_QKV), lambda b, r, i: (b, i, r))
    prev = pl.BlockSpec((1, BLK, P_QKV), lambda b, r, i: (b, jnp.maximum(i - 1, 0), r))
    out = pl.BlockSpec((1, BLK, D_A), lambda b, r, i: (b, i, r))
    return _host_call(
        body, comm, name=name,
        out_shape=[jax.ShapeDtypeStruct((B, n, d * D_A), F32)] * 2,
        grid=(B, d, nb),
        in_specs=[cur, prev,
                  pl.BlockSpec((None, N_HEADS, 2 * BLK, BLK), lambda b, r, i: (branch, 0, 0, 0))],
        out_specs=[out, out],
        scratch_shapes=[pltpu.VMEM((N_HEADS, 2 * BLK, BLK), F32), pltpu.VMEM((N_HEADS, 2 * BLK, BLK), BF16)],
        args=(qkv_v, qkv_v, bias))


VIEW_TILE = 512


def _view_spec(d, w):
    return pl.BlockSpec((1, VIEW_TILE // d, d * w), lambda b, t: (b, t, 0))


def _view_shape(B, S, d, w, dtype):
    return jax.ShapeDtypeStruct((B, S // d, d * w), dtype)


def _tile_scratch(w):
    return pltpu.VMEM((w // LANES, VIEW_TILE, LANES), F32)


def _put_tile(tile_ref, val):
    for c in range(tile_ref.shape[0]):
        tile_ref[c] = val[:, c * LANES:(c + 1) * LANES]


def _get_tile(tile_ref):
    return jnp.concatenate([tile_ref[c] for c in range(tile_ref.shape[0])], axis=1)


def _tile_to_view(tile_ref, view_ref, d, w):
    for c in range(w // LANES):
        for r in range(d):
            lo = r * w + c * LANES
            rows = tile_ref.at[c][pl.ds(r, VIEW_TILE // d, stride=d), :]
            view_ref[0, :, lo:lo + LANES] = rows.astype(view_ref.dtype)


def _view_to_tile(view_ref, tile_ref, d, w):
    for c in range(w // LANES):
        for r in range(d):
            lo = r * w + c * LANES
            tile_ref.at[c][pl.ds(r, VIEW_TILE // d, stride=d), :] = view_ref[0, :, lo:lo + LANES].astype(F32)


def _mm_qkv_views(h, w, name):
    B, S, D = h.shape
    N = w.shape[1]

    def body(h_ref, w_ref, o1_ref, o4_ref, o16_ref, acc_ref):
        acc = jnp.dot(h_ref[0], w_ref[...], preferred_element_type=F32)
        o1_ref[0] = acc.astype(o1_ref.dtype)
        _put_tile(acc_ref, acc)
        _tile_to_view(acc_ref, o4_ref, DILATIONS[1], N)
        _tile_to_view(acc_ref, o16_ref, DILATIONS[2], N)

    return pl.pallas_call(
        body, name=name,
        out_shape=[_view_shape(B, S, d, N, BF16) for d in DILATIONS],
        grid=(B, S // VIEW_TILE),
        in_specs=[pl.BlockSpec((1, VIEW_TILE, D), lambda b, t: (b, t, 0)), pl.BlockSpec((D, N), lambda b, t: (0, 0))],
        out_specs=[_view_spec(d, N) for d in DILATIONS],
        scratch_shapes=[_tile_scratch(N)],
        compiler_params=_cparams(("parallel", "parallel")),
    )(h, w)


def _dil_merge(os_, lses, name):
    B, S, W = os_[0].shape
    nd = len(DILATIONS)

    def body(*refs):
        o_refs, l_refs = refs[:nd], refs[nd:2 * nd]
        out_refs, L_refs = refs[2 * nd:3 * nd], refs[3 * nd:4 * nd]
        scr = refs[4 * nd:]
        o_tok, l_tok = [o_refs[0][0]], [l_refs[0][0]]
        for i, d in enumerate(DILATIONS[1:]):
            _view_to_tile(o_refs[i + 1], scr[2 * i], d, W)
            _view_to_tile(l_refs[i + 1], scr[2 * i + 1], d, W)
            o_tok.append(_get_tile(scr[2 * i]))
            l_tok.append(_get_tile(scr[2 * i + 1]))
        a0, a1, a2 = l_tok
        m = jnp.maximum(jnp.maximum(a0, a1), a2)
        e0, e1, e2 = jnp.exp(a0 - m), jnp.exp(a1 - m), jnp.exp(a2 - m)
        ssum = e0 + e1 + e2
        out = (e0 * o_tok[0] + e1 * o_tok[1] + e2 * o_tok[2]) / ssum
        lse = m + jnp.log(ssum)
        out_refs[0][0] = out
        L_refs[0][0] = lse
        res_o, res_l = scr[2 * (nd - 1)], scr[2 * (nd - 1) + 1]
        _put_tile(res_o, out)
        _put_tile(res_l, lse)
        for i, d in enumerate(DILATIONS[1:]):
            _tile_to_view(res_o, out_refs[i + 1], d, W)
            _tile_to_view(res_l, L_refs[i + 1], d, W)

    specs = [_view_spec(d, W) for d in DILATIONS]
    shapes = [_view_shape(B, S * DILATIONS[0], d, W, F32) for d in DILATIONS]
    res = pl.pallas_call(
        body, name=name, out_shape=shapes * 2, grid=(B, S // VIEW_TILE),
        in_specs=specs * 2, out_specs=specs * 2,
        scratch_shapes=[_tile_scratch(W)] * (2 * nd),
        compiler_params=_cparams(("parallel", "parallel")),
    )(*os_, *lses)
    return res[:nd], res[nd:]


def _dil_bwd(qkv, do, out_a, L, bias, branch, dilation, name, comm=None):
    B, n, _ = qkv.shape
    d = dilation
    nb = n // BLK
    qkv_v, do_v, oa_v, L_v = qkv, do, out_a, L
    npair = N_HEADS // 2
    multi = nb > 1

    tiles = ("P", "C", "N") if multi else ("C",)
    n_t = len(tiles)

    def body(*refs):
        if multi:
            (cur_ref, prev_ref, next_ref, do_ref, don_ref, oa_ref, oan_ref, L_ref, Ln_ref, bias_ref,
             dqkv_ref, dbias_ref, s_scr, dp_scr, p_scr, ds_scr) = refs
        else:
            cur_ref, do_ref, oa_ref, L_ref, bias_ref, dqkv_ref, dbias_ref, s_scr, dp_scr, p_scr, ds_scr = refs
        b, r, i = pl.program_id(0), pl.program_id(1), pl.program_id(2)

        @pl.when((b == 0) & (r == 0) & (i == 0))
        def _():
            dbias_ref[...] = jnp.zeros_like(dbias_ref)

        vprev, vcur = _band_valid_t()
        valid = {"P": vprev & (i > 0), "C": vcur, "N": vprev & (i < nb - 1)}
        band = {"P": slice(0, BLK), "C": slice(BLK, 2 * BLK), "N": slice(0, BLK)}
        psl = lambda p: slice(p * LANES, (p + 1) * LANES)
        ksl = lambda p: slice(D_A + p * LANES, D_A + (p + 1) * LANES)
        vsl = lambda p: slice(2 * D_A + p * LANES, 2 * D_A + (p + 1) * LANES)

        def operands(p, hh):
            hm = _head_mask((BLK, LANES), hh)
            mask = lambda x: jnp.where(hm, x, jnp.zeros_like(x))
            qm, dom = mask(cur_ref[0, :, psl(p)]), mask(do_ref[0, :, psl(p)])
            ops = {"C": (cur_ref[0, :, ksl(p)], cur_ref[0, :, vsl(p)], qm, dom)}
            if multi:
                ops["P"] = (prev_ref[0, :, ksl(p)], prev_ref[0, :, vsl(p)], qm, dom)
                ops["N"] = (cur_ref[0, :, ksl(p)], cur_ref[0, :, vsl(p)], mask(next_ref[0, :, psl(p)]),
                            mask(don_ref[0, :, psl(p)]))
            return ops

        for p in range(npair):
            for hh in range(2):
                h = 2 * p + hh
                ops = operands(p, hh)
                for t, name_t in enumerate(tiles):
                    k_t, v_t, q_t, do_t = ops[name_t]
                    s_scr[h, t] = _dot_nt(k_t, q_t)
                    dp_scr[h, t] = _dot_nt(v_t, do_t)

        def rows(L_r, do_r, oa_r, p):
            lt = jnp.transpose(L_r[0, :, psl(p)])
            dt = jnp.transpose(do_r[0, :, psl(p)].astype(F32) * oa_r[0, :, psl(p)])
            return ([lt[0:1, :], lt[HEAD_DIM:HEAD_DIM + 1, :]],
                    [jnp.sum(dt[:HEAD_DIM], axis=0, keepdims=True), jnp.sum(dt[HEAD_DIM:], axis=0, keepdims=True)])

        for p in range(npair):
            lse_c, delta_c = rows(L_ref, do_ref, oa_ref, p)
            if multi:
                lse_n, delta_n = rows(Ln_ref, don_ref, oan_ref, p)
            for hh in range(2):
                h = 2 * p + hh
                for t, name_t in enumerate(tiles):
                    lse, delta = (lse_n[hh], delta_n[hh]) if name_t == "N" else (lse_c[hh], delta_c[hh])
                    s = s_scr[h, t] * DIL_SCALE + bias_ref[h, band[name_t], :]
                    pr = jnp.where(valid[name_t], jnp.exp(s - lse), 0.0)
                    ds = pr * (dp_scr[h, t] - delta)
                    p_scr[h, t] = pr.astype(BF16)
                    ds_scr[h, t] = ds.astype(BF16)
                    if name_t != "N":
                        dbias_ref[h, band[name_t], :] += ds

        for p in range(npair):
            dqt = jnp.zeros((LANES, BLK), F32)
            dk = jnp.zeros((BLK, LANES), F32)
            dv = jnp.zeros((BLK, LANES), F32)
            kct = jnp.transpose(cur_ref[0, :, ksl(p)].astype(F32)).astype(BF16)
            if multi:
                kpt = jnp.transpose(prev_ref[0, :, ksl(p)].astype(F32)).astype(BF16)
            for hh in range(2):
                h = 2 * p + hh
                ops = operands(p, hh)
                mine = _row_mask((LANES, BLK), hh)
                for t, name_t in enumerate(tiles):
                    _, _, q_t, do_t = ops[name_t]
                    if name_t != "P":
                        dv = dv + _dot_nn(p_scr[h, t], do_t)
                        dk = dk + _dot_nn(ds_scr[h, t], q_t)
                    if name_t != "N":
                        kt = kpt if name_t == "P" else kct
                        dqt = dqt + _dot_nn(jnp.where(mine, kt, jnp.zeros_like(kt)), ds_scr[h, t])
            dqkv_ref[0, :, psl(p)] = jnp.transpose(dqt) * DIL_SCALE
            dqkv_ref[0, :, ksl(p)] = dk * DIL_SCALE
            dqkv_ref[0, :, vsl(p)] = dv

    def at(off):
        return lambda b, r, i: (b, jnp.clip(i + off, 0, nb - 1), r)

    qkv_spec = lambda off: pl.BlockSpec((1, BLK, P_QKV), at(off))
    da_spec = lambda off: pl.BlockSpec((1, BLK, D_A), at(off))
    bias_spec = pl.BlockSpec((None, N_HEADS, 2 * BLK, BLK), lambda b, r, i: (branch, 0, 0, 0))
    dbias_spec = pl.BlockSpec((N_HEADS, 2 * BLK, BLK), lambda b, r, i: (0, 0, 0))
    if multi:
        in_specs = [qkv_spec(0), qkv_spec(-1), qkv_spec(1), da_spec(0), da_spec(1), da_spec(0), da_spec(1),
                    da_spec(0), da_spec(1), bias_spec]
        args = [qkv_v, qkv_v, qkv_v, do_v, do_v, oa_v, oa_v, L_v, L_v, bias]
    else:
        in_specs = [qkv_spec(0), da_spec(0), da_spec(0), da_spec(0), bias_spec]
        args = [qkv_v, do_v, oa_v, L_v, bias]
    return _host_call(
        body, comm, name=name,
        out_shape=[jax.ShapeDtypeStruct((B, n, d * P_QKV), F32),
                   jax.ShapeDtypeStruct((N_HEADS, 2 * BLK, BLK), F32)],
        grid=(B, d, nb),
        in_specs=in_specs,
        out_specs=[qkv_spec(0), dbias_spec],
        scratch_shapes=[pltpu.VMEM((N_HEADS, n_t, BLK, BLK), F32), pltpu.VMEM((N_HEADS, n_t, BLK, BLK), F32),
                        pltpu.VMEM((N_HEADS, n_t, BLK, BLK), BF16), pltpu.VMEM((N_HEADS, n_t, BLK, BLK), BF16)],
        args=args)


def _sum_views_bf16(parts, name):
    B, S, W = parts[0].shape

    def body(a_ref, b_ref, c_ref, o_ref, sb, sc):
        _view_to_tile(b_ref, sb, DILATIONS[1], W)
        _view_to_tile(c_ref, sc, DILATIONS[2], W)
        o_ref[0] = (a_ref[0] + _get_tile(sb) + _get_tile(sc)).astype(o_ref.dtype)

    return pl.pallas_call(
        body, name=name, out_shape=jax.ShapeDtypeStruct((B, S, W), BF16), grid=(B, S // VIEW_TILE),
        in_specs=[_view_spec(d, W) for d in DILATIONS], out_specs=_view_spec(1, W),
        scratch_shapes=[_tile_scratch(W)] * 2,
        compiler_params=_cparams(("parallel", "parallel")),
    )(*parts)


def _bias_tables(rel_bias, buckets, name, comm=None):
    nbr = buckets.shape[0]

    def body(rb_ref, bk_ref, o_ref):
        h = pl.program_id(1)
        tab = bk_ref[0]

        def step(bkt, acc):
            return jnp.where(tab == bkt, rb_ref[bkt, h], acc)

        o_ref[0, 0] = lax.fori_loop(0, N_BUCKETS, step, jnp.zeros((2 * BLK, BLK), F32))

    (bias,), got = _host_call(
        body, comm, name=name, out_shape=[jax.ShapeDtypeStruct((nbr, N_HEADS, 2 * BLK, BLK), F32)],
        grid=(nbr, N_HEADS),
        in_specs=[pl.BlockSpec(memory_space=pltpu.SMEM),
                  pl.BlockSpec((1, 2 * BLK, BLK), lambda i, h: (i, 0, 0))],
        out_specs=[pl.BlockSpec((1, 1, 2 * BLK, BLK), lambda i, h: (i, h, 0, 0))],
        scratch_shapes=[], args=(rel_bias, buckets))
    return bias, got


def _bias_grad(dbias_list, buckets, name):
    nbr = len(dbias_list)

    def body(*refs):
        d_refs, bk_ref, o_ref, part = refs[:nbr], refs[nbr], refs[nbr + 1], refs[nbr + 2]

        def step(bkt, carry):
            hit = [bk_ref[bi] == bkt for bi in range(nbr)]
            for h in range(N_HEADS):
                tot = jnp.zeros((1, BLK), F32)
                for bi in range(nbr):
                    tot = tot + jnp.sum(jnp.where(hit[bi], d_refs[bi][h], 0.0), axis=0, keepdims=True)
                part[bkt, h:h + 1, :] = tot
            return carry

        lax.fori_loop(0, N_BUCKETS, step, 0)
        lane = lax.broadcasted_iota(jnp.int32, (N_HEADS, LANES), 1)
        acc = jnp.zeros((N_HEADS, LANES), F32)
        for bkt in range(N_BUCKETS):
            acc = acc + jnp.where(lane == bkt, jnp.sum(part[bkt], axis=1, keepdims=True), 0.0)
        o_ref[...] = acc

    band = pl.BlockSpec((N_HEADS, 2 * BLK, BLK), lambda i: (0, 0, 0))
    return pl.pallas_call(
        body, name=name, out_shape=jax.ShapeDtypeStruct((N_HEADS, LANES), F32), grid=(1,),
        in_specs=[band] * nbr + [pl.BlockSpec((nbr, 2 * BLK, BLK), lambda i: (0, 0, 0))],
        out_specs=pl.BlockSpec((N_HEADS, LANES), lambda i: (0, 0)),
        scratch_shapes=[pltpu.VMEM((N_BUCKETS, N_HEADS, BLK), F32)],
        compiler_params=_cparams(("arbitrary",)),
    )(*dbias_list, buckets)


MLA_TQ = 256
MLA_TK = 256


LOG2E = math.log2(math.e)
MLA_C = MLA_SCALE * LOG2E


def _key_le_query(tk, tq):
    return lax.broadcasted_iota(jnp.int32, (tk, tq), 0) <= lax.broadcasted_iota(jnp.int32, (tk, tq), 1)


def _row_mask(shape, hh):
    row = lax.broadcasted_iota(jnp.int32, shape, 0)
    return (row >= hh * HEAD_DIM) & (row < (hh + 1) * HEAD_DIM)


def _host_call(body, comm, *, name, grid, in_specs, out_specs, out_shape, scratch_shapes, args):
    sem = ("arbitrary",) * len(grid)
    if comm is None:
        res = pl.pallas_call(body, name=name, grid=grid, in_specs=in_specs, out_specs=out_specs,
                             out_shape=out_shape, scratch_shapes=scratch_shapes,
                             compiler_params=_cparams(sem))(*args)
        return res, []
    n_in, n_out, n_s, cn = len(in_specs), len(out_specs), len(scratch_shapes), comm.n

    def hosted(*refs):
        ins, refs = refs[:n_in], refs[n_in:]
        c_ins, refs = refs[:cn], refs[cn:]
        outs, refs = refs[:n_out], refs[n_out:]
        c_outs, refs = refs[:cn], refs[cn:]
        scr, c_sems = refs[:n_s], refs[n_s:]
        ids = [pl.program_id(a) for a in range(len(grid))]
        first = functools.reduce(jnp.logical_and, [i == 0 for i in ids])
        last = functools.reduce(jnp.logical_and, [i == g - 1 for i, g in zip(ids, grid)])

        @pl.when(first)
        def _():
            comm.start(c_ins, c_outs, c_sems)

        body(*ins, *outs, *scr)

        @pl.when(last)
        def _():
            comm.finish(c_ins, c_outs, c_sems)

    res = pl.pallas_call(
        hosted, name=name, grid=grid, in_specs=list(in_specs) + _hbm_specs(cn),
        out_specs=list(out_specs) + _hbm_specs(cn), out_shape=list(out_shape) + list(comm.out_shape),
        scratch_shapes=list(scratch_shapes) + list(comm.scratch), compiler_params=_cparams(sem),
    )(*args, *comm.inputs)
    return res[:n_out], res[n_out:]


def _mla_fwd_t(q, k, vt, name, comm=None):
    B, S, _ = q.shape
    tq, tk = MLA_TQ, MLA_TK
    assert tq == tk
    npair = N_HEADS // 2
    nq = S // tq

    def body(q_ref, k_ref, vt_ref, o_ref, lse_ref, s_scr, e_scr, acc_scr, m_scr, a_scr):
        i = pl.program_id(1)
        diag = _key_le_query(tk, tq)
        m_scr[...] = jnp.full_like(m_scr, NEG)
        acc_scr[...] = jnp.zeros_like(acc_scr)

        def step(j, masked):
            rows = pl.ds(pl.multiple_of(j * tk, tk), tk)
            for h in range(N_HEADS):
                hsl = slice(h * LANES, (h + 1) * LANES)
                s_scr[h] = _dot_nt(k_ref[0, rows, hsl], q_ref[0, :, hsl])
            for h in range(N_HEADS):
                s = s_scr[h]
                if masked:
                    s = jnp.where(diag, s, NEG)
                m_old = m_scr[h:h + 1, :]
                m_new = jnp.maximum(m_old, jnp.max(s, axis=0, keepdims=True))
                a_scr[h:h + 1, :] = jnp.exp2((m_old - m_new) * MLA_C)
                e_scr[h] = jnp.exp2((s - m_new) * MLA_C).astype(BF16)
                m_scr[h:h + 1, :] = m_new
            for h in range(N_HEADS):
                vj = vt_ref[0, h // 2, j]
                vh = jnp.where(_row_mask(vj.shape, h % 2), vj, jnp.ones_like(vj))
                acc_scr[h] = acc_scr[h] * a_scr[h:h + 1, :] + _dot_nn(vh, e_scr[h])

        def loop_body(j, carry):
            step(j, False)
            return carry

        lax.fori_loop(0, i, loop_body, 0)
        step(i, True)
        rows0 = _row_mask((LANES, tq), 0)
        for p in range(npair):
            l0 = acc_scr[2 * p, HEAD_DIM:HEAD_DIM + 1, :]
            l1 = acc_scr[2 * p + 1, 0:1, :]
            o_ref[0, p * LANES:(p + 1) * LANES, :] = jnp.where(rows0, acc_scr[2 * p] / l0, acc_scr[2 * p + 1] / l1)
            lse_ref[0, p, 0] = jnp.zeros((8, tq), F32)
            lse_ref[0, p, 0, 0:1, :] = m_scr[2 * p:2 * p + 1, :] * MLA_C + jnp.log(l0) * LOG2E
            lse_ref[0, p, 0, 1:2, :] = m_scr[2 * p + 1:2 * p + 2, :] * MLA_C + jnp.log(l1) * LOG2E

    return _host_call(
        body, comm, name=name,
        out_shape=[jax.ShapeDtypeStruct((B, D_B, S), F32), jax.ShapeDtypeStruct((B, npair, nq, 8, tq), F32)],
        grid=(B, nq),
        in_specs=[pl.BlockSpec((1, tq, N_HEADS * LANES), lambda b, i: (b, i, 0)),
                  pl.BlockSpec((1, S, N_HEADS * LANES), lambda b, i: (b, 0, 0)),
                  pl.BlockSpec((1, npair, S // tk, LANES, tk), lambda b, i: (b, 0, 0, 0, 0))],
        out_specs=[pl.BlockSpec((1, D_B, tq), lambda b, i: (b, 0, i)),
                   pl.BlockSpec((1, npair, 1, 8, tq), lambda b, i: (b, 0, i, 0, 0))],
        scratch_shapes=[pltpu.VMEM((N_HEADS, tk, tq), F32), pltpu.VMEM((N_HEADS, tk, tq), BF16),
                        pltpu.VMEM((N_HEADS, LANES, tq), F32), pltpu.VMEM((N_HEADS, tq), F32),
                        pltpu.VMEM((N_HEADS, tq), F32)],
        args=(q, k, vt))


def _mla_delta(do, o, name):
    B, S, _ = o.shape
    tq = MLA_TQ
    npair = N_HEADS // 2

    def body(do_ref, o_ref, d_ref):
        d_ref[...] = jnp.zeros_like(d_ref)
        for p in range(npair):
            sl = slice(p * LANES, (p + 1) * LANES)
            prod_t = jnp.transpose(do_ref[0, :, sl].astype(F32) * o_ref[0, :, sl])
            d_ref[0, p, 0, 0:1, :] = jnp.sum(prod_t[:HEAD_DIM], axis=0, keepdims=True)
            d_ref[0, p, 0, 1:2, :] = jnp.sum(prod_t[HEAD_DIM:], axis=0, keepdims=True)

    tok = pl.BlockSpec((1, tq, D_B), lambda b, i: (b, i, 0))
    return pl.pallas_call(
        body, name=name, out_shape=jax.ShapeDtypeStruct((B, npair, S // tq, 8, tq), F32),
        grid=(B, S // tq), in_specs=[tok, tok],
        out_specs=pl.BlockSpec((1, npair, 1, 8, tq), lambda b, i: (b, 0, i, 0, 0)),
        compiler_params=_cparams(("parallel", "parallel")),
    )(do, o)


def _mla_bwd_t(q, k, v, do, lse, delta, name, comm=None):
    B, S, _ = q.shape
    tq, tk = MLA_TQ, MLA_TK
    assert tq == tk
    npair = N_HEADS // 2
    nq = S // tq

    hg = 4
    pg = hg // 2
    ngroup = N_HEADS // hg

    def body(q_ref, do_ref, lse_ref, dl_ref, k_ref, v_ref, dk_ref, dv_ref, dq_ref,
             s_scr, dp_scr, p_scr, ds_scr, dk_s, dv_s, kt_s):
        j = pl.program_id(2)

        @pl.when(j == 0)
        def _():
            dq_ref[...] = jnp.zeros_like(dq_ref)

        dk_s[...] = jnp.zeros_like(dk_s)
        dv_s[...] = jnp.zeros_like(dv_s)
        diag = _key_le_query(tk, tq)
        hsl = lambda h: slice(h * LANES, (h + 1) * LANES)
        for h in range(hg):
            kt_s[h] = jnp.transpose(k_ref[0, :, hsl(h)].astype(F32)).astype(BF16)

        def step(i, masked):
            rows = pl.ds(pl.multiple_of(i * tq, tq), tq)

            def dom(h):
                dov = do_ref[0, rows, hsl(h // 2)]
                return jnp.where(_head_mask((tq, LANES), h % 2), dov, jnp.zeros_like(dov))

            for h in range(hg):
                s_scr[h] = _dot_nt(k_ref[0, :, hsl(h)], q_ref[0, rows, hsl(h)])
                dp_scr[h] = _dot_nt(v_ref[0, :, hsl(h // 2)], dom(h))
            for h in range(hg):
                pr = jnp.exp2(s_scr[h] * MLA_C - lse_ref[0, h // 2, i, h % 2:h % 2 + 1, :])
                if masked:
                    pr = jnp.where(diag, pr, 0.0)
                p_scr[h] = pr.astype(BF16)
                ds_scr[h] = (pr * (dp_scr[h] - dl_ref[0, h // 2, i, h % 2:h % 2 + 1, :])).astype(BF16)
            for h in range(hg):
                dv_s[h // 2] += _dot_nn(p_scr[h], dom(h))
                dk_s[h] += _dot_nn(ds_scr[h], q_ref[0, rows, hsl(h)])
                dq_ref[0, h // 2, i, hsl(h % 2), :] += _dot_nn(kt_s[h], ds_scr[h]) * MLA_SCALE

        step(j, True)

        def loop_body(i, carry):
            step(i, False)
            return carry

        lax.fori_loop(j + 1, nq, loop_body, 0)
        for h in range(hg):
            dk_ref[0, :, hsl(h)] = dk_s[h] * MLA_SCALE
        for p in range(pg):
            dv_ref[0, :, hsl(p)] = dv_s[p]

    stat = pl.BlockSpec((1, pg, nq, 8, tq), lambda b, g, j: (b, g, 0, 0, 0))
    return _host_call(
        body, comm, name=name,
        out_shape=[jax.ShapeDtypeStruct((B, S, N_HEADS * LANES), F32), jax.ShapeDtypeStruct((B, S, D_B), F32),
                   jax.ShapeDtypeStruct((B, npair, nq, 2 * LANES, tq), F32)],
        grid=(B, ngroup, S // tk),
        in_specs=[pl.BlockSpec((1, S, hg * LANES), lambda b, g, j: (b, 0, g)),
                  pl.BlockSpec((1, S, pg * LANES), lambda b, g, j: (b, 0, g)),
                  stat, stat,
                  pl.BlockSpec((1, tk, hg * LANES), lambda b, g, j: (b, j, g)),
                  pl.BlockSpec((1, tk, pg * LANES), lambda b, g, j: (b, j, g))],
        out_specs=[pl.BlockSpec((1, tk, hg * LANES), lambda b, g, j: (b, j, g)),
                   pl.BlockSpec((1, tk, pg * LANES), lambda b, g, j: (b, j, g)),
                   pl.BlockSpec((1, pg, nq, 2 * LANES, tq), lambda b, g, j: (b, g, 0, 0, 0))],
        scratch_shapes=[pltpu.VMEM((hg, tk, tq), F32), pltpu.VMEM((hg, tk, tq), F32),
                        pltpu.VMEM((hg, tk, tq), BF16), pltpu.VMEM((hg, tk, tq), BF16),
                        pltpu.VMEM((hg, tk, LANES), F32), pltpu.VMEM((pg, tk, LANES), F32),
                        pltpu.VMEM((hg, LANES, tk), BF16)],
        args=(q, do, lse, delta, k, v))


def _bucket_tables():
    return jnp.asarray(np.stack([_band_buckets(d) for d in DILATIONS]))


def _local_step(x, target, mod, wts, gains, rel_bias, ffn_shards=None, bias=None):
    B, S, D = x.shape
    T = B * S
    sh1, sc1, g1, sh2, sc2, g2 = [mod[:, i * D:(i + 1) * D].reshape(B, 1, D) for i in range(N_MOD)]
    cs, sn = _rope_tables()
    buckets_dev = _bucket_tables()
    if bias is None:
        bias, _ = _bias_tables(rel_bias, buckets_dev, "rel_bias_tables")
    w_in = wts["w_in"]

    h1 = _adaln_fwd(x, gains["g_norm1"], sc1, sh1, "adaln1_fwd")
    h1f = h1.reshape(T, D)
    qkv_v = _mm_qkv_views(h1, w_in[:, :P_QKV], "mm_qkv")
    rest = _mm(h1f, w_in[:, P_QKV:], "nn", F32, "mm_rest")
    o_d, lse_d = [], []
    late_got = []
    for i, d in enumerate(DILATIONS):
        comm = _GatherComm(ffn_shards[i + 1:i + 2]) if (ffn_shards and i < 2) else None
        (o_i, lse_i), got = _dil_fwd(qkv_v[i], bias, i, d, f"dil_fwd_{d}", comm)
        late_got += list(got)
        o_d.append(o_i)
        lse_d.append(lse_i)
    if ffn_shards:
        wts = dict(wts, w_out=late_got[1].reshape(D, D))
    out_a_v, lse_a_v = _dil_merge(o_d, lse_d, "dil_merge")
    out_a = out_a_v[0]
    cqn = _rms_fwd(rest, 1, Q_LORA, gains["g_cq"], "rms_cq_fwd")
    ckvn = _rms_fwd(rest, 0, KV_LORA, gains["g_ckv"], "rms_ckv_fwd")
    rest3 = rest.reshape(B, S, P_REST)
    q_raw = _mm(cqn, wts["w_uq"], "nn", F32, "mm_uq").reshape(B, S, N_HEADS * LANES)
    qc = _rope_apply(q_raw, cs, sn, BF16, "rope_q")
    kn_raw = _mm(ckvn, wts["w_kv"][:, :N_HEADS * LANES], "nn", F32, "mm_uk").reshape(B, S, N_HEADS * LANES)
    kc = _rope_apply(kn_raw, cs, sn, BF16, "rope_k", add=rest3, add_blk=KV_LORA // LANES)
    v = _mm(ckvn, wts["w_kv"][:, N_HEADS * LANES:], "nn", BF16, "mm_uv").reshape(B, S, D_B)
    vt = jnp.transpose(v.reshape(B, S // MLA_TK, MLA_TK, N_HEADS // 2, LANES), (0, 3, 1, 4, 2))
    (o_t, lse_b), got = _mla_fwd_t(qc, kc, vt, "mla_fwd", _GatherComm(ffn_shards[:1]) if ffn_shards else None)
    if ffn_shards:
        wts = dict(wts, w_ffn_in=got[0].reshape(N_CHIP, D, -1), w_ffn_out=late_got[0].reshape(D_FF, D))
    out_b = jnp.transpose(o_t, (0, 2, 1))
    out_af, out_bf = out_a.reshape(T, D_A), out_b.reshape(T, D_B)
    ya = _rms_fwd(out_af, 0, D_A, gains["g_out_a"], "rms_outa_fwd")
    yb = _rms_fwd(out_bf, 0, D_B, gains["g_out_b"], "rms_outb_fwd")
    y = jnp.concatenate([ya, yb], axis=1)
    mix = _mm(y, wts["w_out"], "nn", F32, "mm_out").reshape(B, S, D)
    h2, x1 = _adaln_fwd(x, gains["g_norm2"], sc2, sh2, "adaln2_fwd", mix=mix, gate=g1)
    h2f = h2.reshape(T, D)
    gu, act = _ffn_in_fwd(h2f, wts["w_ffn_in"], "mm_ffn_in")
    f = _mm(act, wts["w_ffn_out"], "nn", F32, "mm_ffn_out").reshape(B, S, D)
    dx2, df, dg2, dg_final, loss = _final_loss(x1, f, g2, gains["g_final"], target, "final_loss")

    dff = df.reshape(T, D)
    dgu = _ffn_out_bwd(dff, wts["w_ffn_out"], gu, "mm_ffn_out_dx")
    gw_ffn_out = _mm(act, dff, "tn", F32, "mm_ffn_out_dw")
    dh2 = _mm(dgu, wts["w_ffn_in"], "nt", F32, "mm_ffn_in_dx", col_blocks=N_CHIP, halves=True).reshape(B, S, D)
    gw_ffn_in = _mm(h2f, dgu, "tn", F32, "mm_ffn_in_dw", col_blocks=N_CHIP, halves=True)
    ffn_g8 = ffn_r1 = None
    if ffn_shards:
        ffn_g8 = [gw_ffn_in.reshape(N_DEV, -1, gw_ffn_in.shape[-1]), gw_ffn_out.reshape(N_DEV, -1, D)]
        dx1, dsh2, dsc2, dg_norm2, dg1, dmix, ffn_r1 = _adaln_bwd(
            dh2, x1, gains["g_norm2"], sc2, dx2, "adaln2_bwd", mix=mix, gate=g1, comm=_ToSiblingComm(ffn_g8))
    else:
        dx1, dsh2, dsc2, dg_norm2, dg1, dmix = _adaln_bwd(dh2, x1, gains["g_norm2"], sc2, dx2, "adaln2_bwd",
                                                          mix=mix, gate=g1)
    dmixf = dmix.reshape(T, D)
    dy = _mm(dmixf, wts["w_out"], "nt", F32, "mm_out_dx")
    gw_out = _mm(y, dmixf, "tn", F32, "mm_out_dw")
    do_a_v, dg_out_a = _rms_bwd_views(dy, 0, out_a, gains["g_out_a"], "rms_outa_bwd")
    do_b, dg_out_b = _rms_bwd(dy, 1, out_bf, 0, D_B, gains["g_out_b"], "rms_outb_bwd")
    do_b3 = do_b.reshape(B, S, D_B)
    delta_b = _mla_delta(do_b3, out_b, "mla_delta")
    ffn_a4 = ffn_send = None
    if ffn_shards:
        ffn_a4, ffn_send = _rs_first(ffn_g8, "ffn", r1=ffn_r1)
    (dkc, dv, dq_t), ffn_r2 = _mla_bwd_t(qc, kc, v, do_b3, lse_b, delta_b, "mla_bwd",
                                         _ToChipsComm(ffn_send[:1]) if ffn_shards else None)
    dqc = jnp.transpose(dq_t, (0, 2, 4, 1, 3)).reshape(B, S, N_HEADS * LANES)
    dq_raw = _rope_apply(dqc, cs, -sn, BF16, "rope_q_bwd").reshape(T, N_HEADS * LANES)
    dkrw = _krope_bwd(dkc, cs, -sn, "rope_k_bwd").reshape(T, LANES)
    dcqn = _mm(dq_raw, wts["w_uq"], "nt", F32, "mm_uq_dx")
    gw_uq = _mm(cqn, dq_raw, "tn", F32, "mm_uq_dw")
    dkv = jnp.concatenate([dkc.reshape(T, -1), dv.reshape(T, -1)], axis=1).astype(BF16)
    dckvn = _mm(dkv, wts["w_kv"], "nt", F32, "mm_ukv_dx")
    gw_kv = _mm(ckvn, dkv, "tn", F32, "mm_ukv_dw")
    dcq, dg_cq = _rms_bwd(dcqn, 0, rest, 1, Q_LORA, gains["g_cq"], "rms_cq_bwd")
    dckv, dg_ckv = _rms_bwd(dckvn, 0, rest, 0, KV_LORA, gains["g_ckv"], "rms_ckv_bwd")
    dqkv_d, dbias_d = [], []
    for i, d in enumerate(DILATIONS):
        comm = _ToChipsComm(ffn_send[1:]) if (ffn_shards and i == 0) else None
        (dqkv_i, dbias_i), got = _dil_bwd(qkv_v[i], do_a_v[i], out_a_v[i], lse_a_v[i], bias, i, d,
                                          f"dil_bwd_{d}", comm)
        if comm is not None:
            ffn_r2 = list(ffn_r2) + list(got)
        dqkv_d.append(dqkv_i)
        dbias_d.append(dbias_i)
    dqkv = _sum_views_bf16(dqkv_d, "dil_bwd_sum").reshape(T, P_QKV)
    g_rel_bias = _bias_grad(dbias_d, buckets_dev, "rel_bias_grad")[:, :N_BUCKETS].T
    dproj = jnp.concatenate([dqkv, dckv, dkrw, dcq], axis=1)
    gw_in = _mm(h1f, dproj, "tn", F32, "mm_in_dw")
    mix_a4 = mix_r2 = None
    if ffn_shards:
        nat = [_w_in_from_kernel(gw_in), _w_uq_from_kernel(gw_uq), _w_ukv_from_kernel(gw_kv)]
        g8 = [_shards_from_full(g) for g in nat] + [gw_out]
        mix_a4, mix_send = _rs_first([g.reshape(N_DEV, -1, g.shape[-1]) for g in g8], "mix")
        dh1, mix_r2 = _mm(dproj, w_in, "nt", F32, "mm_in_dx", comm=_ToChipsComm(mix_send))
    else:
        dh1 = _mm(dproj, w_in, "nt", F32, "mm_in_dx")
    dh1 = dh1.reshape(B, S, D)
    grad_x, dsh1, dsc1, dg_norm1 = _adaln_bwd(dh1, x, gains["g_norm1"], sc1, dx1, "adaln1_bwd")
    gmod = jnp.concatenate([dsh1, dsc1, dg1, dsh2, dsc2, dg2], axis=-1).reshape(B, N_MOD * D)
    grads = dict(w_in=gw_in, w_uq=gw_uq, w_kv=gw_kv, w_out=gw_out, w_ffn_in=gw_ffn_in, w_ffn_out=gw_ffn_out,
                 g_norm1=dg_norm1, g_cq=dg_cq, g_ckv=dg_ckv, rel_bias=g_rel_bias, g_out_a=dg_out_a,
                 g_out_b=dg_out_b, g_norm2=dg_norm2, g_final=dg_final, ffn_pending=(ffn_a4, ffn_r2),
                 mix_pending=(mix_a4, mix_r2))
    return loss, grad_x, gmod, grads


def _w_in_to_kernel(w):
    z = lambda n: jnp.zeros((w.shape[0], n), w.dtype)
    i3, i4, i5 = 3 * D_A, 3 * D_A + Q_LORA, 3 * D_A + Q_LORA + KV_LORA
    return jnp.concatenate([w[:, :i3], w[:, i4:i5], z(NOPE_DIM), w[:, i5:], z(LANES - NOPE_DIM - ROPE_DIM),
                            w[:, i3:i4]], axis=1)


def _w_in_from_kernel(g):
    o = P_QKV + KV_LORA
    return jnp.concatenate([g[:, :P_QKV], g[:, o + LANES:], g[:, P_QKV:o],
                            g[:, o + NOPE_DIM:o + NOPE_DIM + ROPE_DIM]], axis=1)


def _w_uq_to_kernel(w):
    w3 = w.reshape(Q_LORA, N_HEADS, NOPE_DIM + ROPE_DIM)
    return jnp.pad(w3, ((0, 0), (0, 0), (0, LANES - NOPE_DIM - ROPE_DIM))).reshape(Q_LORA, N_HEADS * LANES)


def _w_uq_from_kernel(g):
    return g.reshape(Q_LORA, N_HEADS, LANES)[:, :, :NOPE_DIM + ROPE_DIM].reshape(Q_LORA, -1)


def _w_ukv_to_kernel(w):
    w3 = w.reshape(KV_LORA, N_HEADS, 2 * HEAD_DIM)
    wk = jnp.pad(w3[:, :, :NOPE_DIM], ((0, 0), (0, 0), (0, LANES - NOPE_DIM))).reshape(KV_LORA, N_HEADS * LANES)
    wv = w3[:, :, NOPE_DIM:].reshape(KV_LORA, D_B)
    return jnp.concatenate([wk, wv], axis=1)


def _w_ukv_from_kernel(g):
    gk = g[:, :N_HEADS * LANES].reshape(KV_LORA, N_HEADS, LANES)[:, :, :NOPE_DIM]
    gv = g[:, N_HEADS * LANES:].reshape(KV_LORA, N_HEADS, HEAD_DIM)
    return jnp.concatenate([gk, gv], axis=2).reshape(KV_LORA, -1)


MESH = pl.DeviceIdType.MESH


def _my_place():
    return lax.axis_index("x"), lax.axis_index("y"), lax.axis_index("c")


def _other_chips(x, y):
    return [(1 - x, y), (x, 1 - y), (1 - x, 1 - y)]


def _allgather8(x_shard, name, in_hbm):
    m_per, n = x_shard.shape
    space = pl.ANY if in_hbm else pltpu.VMEM

    def body(x_ref, out_ref, send_sems, recv_sems, local_sem):
        x, y, c = _my_place()
        me, sibling = (x, y, c), (x, y, 1 - c)
        chips = _other_chips(x, y)

        def rows(px, py, pc):
            return out_ref.at[pl.ds((4 * px + 2 * py + pc) * m_per, m_per), :]

        def copy(k, block, to, src=None):
            return pltpu.make_async_remote_copy(
                src_ref=rows(*block) if src is None else src, dst_ref=rows(*block),
                send_sem=send_sems.at[k], recv_sem=recv_sems.at[k], device_id=to, device_id_type=MESH)

        mine = pltpu.make_async_copy(x_ref, rows(*me), local_sem)
        mine.start()
        first = [copy(0, me, sibling, src=x_ref)]
        first += [copy(1 + j, me, (*chip, c), src=x_ref) for j, chip in enumerate(chips)]
        for cp in first:
            cp.start()
        passed = [copy(4 + j, (*chip, c), sibling) for j, chip in enumerate(chips)]
        for j, chip in enumerate(chips):
            copy(1 + j, (*chip, c), me).wait_recv()
            passed[j].start()
        copy(0, sibling, me).wait_recv()
        for j, chip in enumerate(chips):
            copy(4 + j, (*chip, 1 - c), me).wait_recv()
        for cp in first + passed:
            cp.wait_send()
        mine.wait()

    return pl.pallas_call(
        body, name=name,
        out_shape=jax.ShapeDtypeStruct((N_DEV * m_per, n), x_shard.dtype),
        in_specs=[pl.BlockSpec(memory_space=space)],
        out_specs=pl.BlockSpec(memory_space=space),
        scratch_shapes=[pltpu.SemaphoreType.DMA((7,)), pltpu.SemaphoreType.DMA((7,)), pltpu.SemaphoreType.DMA],
        compiler_params=pltpu.CompilerParams(vmem_limit_bytes=VMEM_LIMIT),
    )(x_shard)


def _hbm_specs(n):
    return [pl.BlockSpec(memory_space=pl.ANY)] * n


class _GatherComm:
    def __init__(self, shards):
        self.n = n = len(shards)
        self.inputs = [s.reshape(2, s.shape[0] // 2, s.shape[1]) for s in shards]
        self.out_shape = [jax.ShapeDtypeStruct((N_DEV,) + s.shape[1:], s.dtype) for s in self.inputs]
        self.scratch = [pltpu.SemaphoreType.DMA((7 * n,)), pltpu.SemaphoreType.DMA((7 * n,))]

    def _parts(self, xs, outs, sems):
        send_sems, recv_sems = sems
        x, y, c = _my_place()

        def blk(k, px, py, pc):
            return outs[k].at[4 * px + 2 * py + pc]

        def copy(k, kind, block, to, own=False):
            return pltpu.make_async_remote_copy(
                src_ref=xs[k].at[c] if own else blk(k, *block), dst_ref=blk(k, *block),
                send_sem=send_sems.at[7 * k + kind], recv_sem=recv_sems.at[7 * k + kind],
                device_id=to, device_id_type=MESH)

        def whole(k):
            return pltpu.make_async_remote_copy(
                src_ref=xs[k], dst_ref=outs[k].at[pl.ds(4 * x + 2 * y, 2)],
                send_sem=send_sems.at[7 * k], recv_sem=recv_sems.at[7 * k],
                device_id=(x, y, 1 - c), device_id_type=MESH)

        me, sibling = (x, y, c), (x, y, 1 - c)
        chips = _other_chips(x, y)
        first = []
        for k in range(self.n):
            first.append(whole(k))
            first += [copy(k, 1 + j, me, (*chip, c), own=True) for j, chip in enumerate(chips)]
        return copy, whole, me, sibling, chips, c, first

    def start(self, xs, outs, sems):
        for cp in self._parts(xs, outs, sems)[-1]:
            cp.start()

    def finish(self, xs, outs, sems):
        copy, whole, me, sibling, chips, c, first = self._parts(xs, outs, sems)
        passed = []
        for j, chip in enumerate(chips):
            for k in range(self.n):
                copy(k, 1 + j, (*chip, c), me).wait_recv()
                fwd = copy(k, 4 + j, (*chip, c), sibling)
                fwd.start()
                passed.append(fwd)
        for k in range(self.n):
            whole(k).wait_recv()
        for j, chip in enumerate(chips):
            for k in range(self.n):
                copy(k, 4 + j, (*chip, 1 - c), me).wait_recv()
        for cp in first + passed:
            cp.wait_send()


class _ToChipsComm:
    def __init__(self, a4s):
        self.inputs = list(a4s)
        self.n = n = len(a4s)
        nc = N_CHIP - 1
        self.out_shape = [jax.ShapeDtypeStruct((nc,) + a.shape[1:], a.dtype) for a in a4s]
        self.scratch = [pltpu.SemaphoreType.DMA((nc * n,)), pltpu.SemaphoreType.DMA((nc * n,))]

    def _copies(self, as_, rs, sems):
        send_sems, recv_sems = sems
        x, y, c = _my_place()
        nc = N_CHIP - 1
        return [pltpu.make_async_remote_copy(
            src_ref=as_[k].at[2 * cx + cy], dst_ref=rs[k].at[j], send_sem=send_sems.at[nc * k + j],
            recv_sem=recv_sems.at[nc * k + j], device_id=(cx, cy, c), device_id_type=MESH)
            for k in range(self.n) for j, (cx, cy) in enumerate(_other_chips(x, y))]

    def start(self, as_, rs, sems):
        for cp in self._copies(as_, rs, sems):
            cp.start()

    def finish(self, as_, rs, sems):
        for cp in self._copies(as_, rs, sems):
            cp.wait()


def _run_comm(comm, name):
    n = comm.n

    def body(*refs):
        ins, outs, sems = refs[:n], refs[n:2 * n], refs[2 * n:]
        comm.start(ins, outs, sems)
        comm.finish(ins, outs, sems)

    return pl.pallas_call(
        body, name=name, out_shape=comm.out_shape, in_specs=_hbm_specs(n), out_specs=_hbm_specs(n),
        scratch_shapes=comm.scratch,
    )(*comm.inputs)


def _gather_weights(shards, name):
    return _run_comm(_GatherComm(shards), name)


class _ToSiblingComm:
    def __init__(self, g8s):
        self.inputs = list(g8s)
        self.n = n = len(g8s)
        self.out_shape = [jax.ShapeDtypeStruct((N_CHIP,) + g.shape[1:], g.dtype) for g in g8s]
        self.scratch = [pltpu.SemaphoreType.DMA((N_CHIP * n,)), pltpu.SemaphoreType.DMA((N_CHIP * n,))]

    def _copies(self, gs, rs, sems):
        send_sems, recv_sems = sems
        x, y, c = _my_place()
        return [pltpu.make_async_remote_copy(
            src_ref=gs[k].at[2 * s + 1 - c], dst_ref=rs[k].at[s], send_sem=send_sems.at[N_CHIP * k + s],
            recv_sem=recv_sems.at[N_CHIP * k + s], device_id=(x, y, 1 - c), device_id_type=MESH)
            for k in range(self.n) for s in range(N_CHIP)]

    def start(self, gs, rs, sems):
        for cp in self._copies(gs, rs, sems):
            cp.start()

    def finish(self, gs, rs, sems):
        for cp in self._copies(gs, rs, sems):
            cp.wait()


def _rs_to_chips(a4s, name):
    return _run_comm(_ToChipsComm(a4s), name)


def _swap_halves(hs, name):
    n = len(hs)

    def body(*refs):
        o_refs = refs[n:2 * n]
        send_sems, recv_sems = refs[2 * n:]
        x, y, c = _my_place()

        def remote(k, slot):
            return pltpu.make_async_remote_copy(
                src_ref=o_refs[k].at[slot], dst_ref=o_refs[k].at[slot], send_sem=send_sems.at[k],
                recv_sem=recv_sems.at[k], device_id=(x, y, 1 - c), device_id_type=MESH)

        sends = [remote(k, c) for k in range(n)]
        for cp in sends:
            cp.start()
        for k in range(n):
            remote(k, 1 - c).wait_recv()
        for cp in sends:
            cp.wait_send()

    return pl.pallas_call(
        body, name=name,
        out_shape=[jax.ShapeDtypeStruct(h.shape, h.dtype) for h in hs],
        in_specs=_hbm_specs(n), out_specs=_hbm_specs(n),
        input_output_aliases={k: k for k in range(n)},
        scratch_shapes=[pltpu.SemaphoreType.DMA((n,)), pltpu.SemaphoreType.DMA((n,))],
    )(*hs)


ADD_TILES = 2


def _add_blocks(a_list, a_idx_fn, others_list, ns, sel, name, out_blocks=None, out_idx_fn=None,
                bf16_copy=False):
    out_blocks = out_blocks or ns
    out_idx_fn = out_idx_fn or (lambda s, sel_ref: s)
    n = len(a_list)
    n_o = len(others_list[0])
    per = 1 + n_o

    def body(sel_ref, *refs):
        for k in range(n):
            ins = refs[k * per:(k + 1) * per]
            acc = ins[0][0]
            for r in ins[1:]:
                acc = acc + r[0].astype(F32)
            refs[n * per + k][0] = acc
            if bf16_copy:
                refs[n * per + n + k][0] = acc.astype(BF16)

    in_specs, args, out_specs, out_shape = [], [], [], []
    for a, others in zip(a_list, others_list):
        _, R, N = a.shape
        tr = R // ADD_TILES
        assert tr % 8 == 0, a.shape
        in_specs.append(pl.BlockSpec((1, tr, N), lambda s, i, sel_ref: (a_idx_fn(s, sel_ref), i, 0)))
        args.append(a)
        for arr, fixed in others:
            if fixed is None:
                in_specs.append(pl.BlockSpec((1, tr, N), lambda s, i, sel_ref: (s, i, 0)))
            else:
                in_specs.append(pl.BlockSpec((1, tr, N), lambda s, i, sel_ref, fixed=fixed: (fixed, i, 0)))
            args.append(arr)
        out_specs.append(pl.BlockSpec((1, tr, N), lambda s, i, sel_ref: (out_idx_fn(s, sel_ref), i, 0)))
        out_shape.append(jax.ShapeDtypeStruct((out_blocks, R, N), a.dtype))
    if bf16_copy:
        out_specs = out_specs + out_specs
        out_shape = out_shape + [jax.ShapeDtypeStruct(o.shape, BF16) for o in out_shape]
    grid_spec = pltpu.PrefetchScalarGridSpec(num_scalar_prefetch=1, grid=(ns, ADD_TILES), in_specs=in_specs,
                                             out_specs=out_specs)
    return pl.pallas_call(
        body, name=name, out_shape=out_shape, grid_spec=grid_spec,
        compiler_params=_cparams(("parallel", "parallel")),
    )(sel, *args)


def _rs_first(g8s, tag, r1=None):
    c_sel = jnp.reshape(lax.axis_index("c"), (1,)).astype(jnp.int32)
    if r1 is None:
        r1 = _run_comm(_ToSiblingComm(g8s), f"rs_to_sibling_{tag}")
    res = _add_blocks(g8s, lambda s, sel: 2 * s + sel[0], [[(r, None)] for r in r1], N_CHIP, c_sel,
                      f"rs_add_sibling_{tag}", bf16_copy=True)
    return list(res[:len(g8s)]), list(res[len(g8s):])


def _rs_last(a4s, r2s, tag):
    sel = jnp.stack([2 * lax.axis_index("x") + lax.axis_index("y"), lax.axis_index("c")]).astype(jnp.int32)
    h = _add_blocks(a4s, lambda s, sel: sel[0], [[(r, 0), (r, 1), (r, 2)] for r in r2s], 1, sel,
                    f"rs_add_chips_{tag}", out_blocks=2, out_idx_fn=lambda s, sel: sel[1])
    full = _swap_halves(h, f"rs_swap_halves_{tag}")
    return [f.reshape(2 * f.shape[1], f.shape[2]) for f in full]


def _reduce_scatter(g8s, tag):
    a4, a4_bf16 = _rs_first(g8s, tag)
    return _rs_last(a4, _rs_to_chips(a4_bf16, f"rs_to_chips_{tag}"), tag)


def _ada_fwd(c_all, w_ada, b_ada, name):
    nb, D = c_all.shape
    ncol = w_ada.shape[1]
    tc = 512

    def body(c_ref, w_ref, b_ref, o_ref):
        cv = c_ref[...]
        cond = (cv * jax.nn.sigmoid(cv)).astype(BF16)
        o_ref[...] = jnp.dot(cond, w_ref[...].astype(BF16), preferred_element_type=F32) + b_ref[...]

    return pl.pallas_call(
        body, name=name, out_shape=jax.ShapeDtypeStruct((nb, ncol), F32), grid=(ncol // tc,),
        in_specs=[pl.BlockSpec((nb, D), lambda j: (0, 0)), pl.BlockSpec((D, tc), lambda j: (0, j)),
                  pl.BlockSpec((1, tc), lambda j: (0, j))],
        out_specs=pl.BlockSpec((nb, tc), lambda j: (0, j)),
        compiler_params=_cparams(("parallel",)),
    )(c_all, w_ada, b_ada)


def _ada_bwd(c_all, gmod_cols, name):
    nb, D = c_all.shape
    ncol = gmod_cols.shape[1]
    tc = 512

    def body(c_ref, g_ref, o_ref):
        cv = c_ref[...]
        cond = (cv * jax.nn.sigmoid(cv)).astype(BF16)
        o_ref[...] = _dot_tn(cond, g_ref[...].astype(BF16))

    return pl.pallas_call(
        body, name=name, out_shape=jax.ShapeDtypeStruct((D, ncol), F32), grid=(ncol // tc,),
        in_specs=[pl.BlockSpec((nb, D), lambda j: (0, 0)), pl.BlockSpec((nb, tc), lambda j: (0, j))],
        out_specs=pl.BlockSpec((D, tc), lambda j: (0, j)),
        compiler_params=_cparams(("parallel",)),
    )(c_all, gmod_cols)


def _adam_math(w, g, m, v):
    m = ADAM_B1 * m + (1.0 - ADAM_B1) * g
    v = ADAM_B2 * v + (1.0 - ADAM_B2) * (g * g)
    m_hat = m / (1.0 - ADAM_B1 ** ADAM_STEP)
    v_hat = v / (1.0 - ADAM_B2 ** ADAM_STEP)
    delta = -ADAM_LR * (m_hat / (jnp.sqrt(v_hat) + ADAM_EPS) + ADAM_WD * w)
    return delta, m, v


def _adamw(w, g, m, v, name):
    rows, cols = w.shape
    tr = _pick(rows, (256, 192, 176, 128, 64, 8))

    def body(w_ref, g_ref, m_ref, v_ref, d_ref, mo_ref, vo_ref):
        d, mn, vn = _adam_math(w_ref[...], g_ref[...], m_ref[...], v_ref[...])
        d_ref[...] = d
        mo_ref[...] = mn
        vo_ref[...] = vn

    spec = pl.BlockSpec((tr, cols), lambda i: (i, 0))
    return pl.pallas_call(
        body, name=name, out_shape=[jax.ShapeDtypeStruct((rows, cols), F32)] * 3, grid=(rows // tr,),
        in_specs=[spec] * 4, out_specs=[spec] * 3, compiler_params=_cparams(("parallel",)),
    )(w, g, m, v)


VEC_ROWS = 8


def _adamw_rows(w, parts, m, v, name):
    n = w.shape[1]
    P = parts.shape[0]
    assert n % (VEC_ROWS * LANES) == 0, n
    shp = (VEC_ROWS, n // VEC_ROWS)

    def body(w_ref, p_ref, m_ref, v_ref, g_ref, d_ref, mo_ref, vo_ref):
        g = p_ref[0]
        for k in range(1, P):
            g = g + p_ref[k]
        d, mn, vn = _adam_math(w_ref[...], g, m_ref[...], v_ref[...])
        g_ref[...] = g
        d_ref[...] = d
        mo_ref[...] = mn
        vo_ref[...] = vn

    vec = pl.BlockSpec(shp, lambda i: (0, 0))
    out = pl.pallas_call(
        body, name=name, out_shape=[jax.ShapeDtypeStruct(shp, F32)] * 4, grid=(1,),
        in_specs=[vec, pl.BlockSpec((P,) + shp, lambda i: (0, 0, 0)), vec, vec], out_specs=[vec] * 4,
        compiler_params=_cparams(("arbitrary",)),
    )(w.reshape(shp), parts.reshape((P,) + shp), m.reshape(shp), v.reshape(shp))
    return [o.reshape(1, n) for o in out]


_SHARDED = ("w_in", "w_uq", "w_ukv", "w_out", "w_ffn_in", "w_ffn_out")
_SMALL = (("g_norm1", 1024), ("g_cq", 384), ("g_ckv", 256), ("rel_bias", 256), ("g_out_a", 512),
          ("g_out_b", 512), ("g_norm2", 1024), ("g_final", 1024))
_SMALL_PAD = 5120


def _full_from_shards(sh):
    return jnp.transpose(sh, (1, 0, 2)).reshape(sh.shape[1], -1)


def _shards_from_full(full):
    rows, cols = full.shape
    return jnp.transpose(full.reshape(rows, N_CHIP, cols // N_CHIP), (1, 0, 2))


def kernel(x, c, w_ada, b_ada, g_norm1, w_in, g_cq, w_uq, g_ckv, w_ukv, rel_bias, g_out_a, g_out_b, w_out, g_norm2, w_ffn_in, w_ffn_out, g_final, loss_target, m_w_ada, m_b_ada, m_g_norm1, m_w_in, m_g_cq, m_w_uq, m_g_ckv, m_w_ukv, m_rel_bias, m_g_out_a, m_g_out_b, m_w_out, m_g_norm2, m_w_ffn_in, m_w_ffn_out, m_g_final, v_w_ada, v_b_ada, v_g_norm1, v_w_in, v_g_cq, v_w_uq, v_g_ckv, v_w_ukv, v_rel_bias, v_g_out_a, v_g_out_b, v_w_out, v_g_norm2, v_w_ffn_in, v_w_ffn_out, v_g_final):
    names = ["w_ada", "b_ada", "g_norm1", "w_in", "g_cq", "w_uq", "g_ckv", "w_ukv", "rel_bias", "g_out_a",
             "g_out_b", "w_out", "g_norm2", "w_ffn_in", "w_ffn_out", "g_final"]
    W = dict(zip(names, [w_ada, b_ada, g_norm1, w_in, g_cq, w_uq, g_ckv, w_ukv, rel_bias, g_out_a, g_out_b,
                         w_out, g_norm2, w_ffn_in, w_ffn_out, g_final]))
    M = dict(zip(names, [m_w_ada, m_b_ada, m_g_norm1, m_w_in, m_g_cq, m_w_uq, m_g_ckv, m_w_ukv, m_rel_bias,
                         m_g_out_a, m_g_out_b, m_w_out, m_g_norm2, m_w_ffn_in, m_w_ffn_out, m_g_final]))
    V = dict(zip(names, [v_w_ada, v_b_ada, v_g_norm1, v_w_in, v_g_cq, v_w_uq, v_g_ckv, v_w_ukv, v_rel_bias,
                         v_g_out_a, v_g_out_b, v_w_out, v_g_norm2, v_w_ffn_in, v_w_ffn_out, v_g_final]))
    B, S, D = x.shape
    mx, my, mc = _my_place()
    dev = 4 * mx + 2 * my + mc
    chip = 2 * mx + my
    pad_rows = 8

    c_all = _allgather8(jnp.pad(c, ((0, pad_rows - B), (0, 0))), "ag_c", False)
    c_all = c_all.reshape(N_DEV, pad_rows, D)[:, :B].reshape(N_DEV * B, D)
    ada_cols = w_ada.shape[-1]
    b_cols = lax.dynamic_slice_in_dim(b_ada, chip * ada_cols, ada_cols, axis=1)
    mod_cols = _ada_fwd(c_all, w_ada[0], b_cols, "ada_fwd")
    mod_all = _allgather8(mod_cols, "ag_mod", False).reshape(N_DEV, N_DEV * B, ada_cols)[0::2]
    mod_all = jnp.transpose(mod_all, (1, 0, 2)).reshape(N_DEV * B, N_MOD * D)
    mod = lax.dynamic_slice_in_dim(mod_all, dev * B, B, axis=0)

    early = ("w_in", "w_uq", "w_ukv")
    bias, got = _bias_tables(rel_bias, _bucket_tables(), "rel_bias_tables",
                             _GatherComm([W[n][0].astype(BF16) for n in early]))
    full = {n: g.reshape((N_CHIP,) + W[n].shape[1:]) for n, g in zip(early, got)}
    wts = dict(w_in=_w_in_to_kernel(_full_from_shards(full["w_in"])),
               w_uq=_w_uq_to_kernel(_full_from_shards(full["w_uq"])),
               w_kv=_w_ukv_to_kernel(_full_from_shards(full["w_ukv"])))
    gains = dict(g_norm1=g_norm1, g_cq=g_cq, g_ckv=g_ckv, g_out_a=g_out_a, g_out_b=g_out_b, g_norm2=g_norm2,
                 g_final=g_final.reshape(1, D))

    loss, grad_x, gmod, grads = _local_step(x, loss_target, mod, wts, gains, rel_bias,
                                            ffn_shards=[w_ffn_in[0].astype(BF16), w_ffn_out[0].astype(BF16),
                                                        w_out[0].astype(BF16)], bias=bias)
    loss = lax.psum(loss[0, 0], ("x", "y", "c"))

    n_small = _SMALL_PAD
    cat = lambda dct: jnp.concatenate([dct[n].reshape(1, -1) for n, _ in _SMALL]
                                      + [jnp.zeros((1, _SMALL_PAD - sum(s for _, s in _SMALL)), F32)], axis=1)
    small = cat(grads)
    rows = jnp.concatenate([gmod, jnp.pad(small, ((0, 0), (0, N_MOD * D - n_small))),
                            jnp.zeros((pad_rows - B - 1, N_MOD * D), F32)], axis=0)
    rows_all = _allgather8(rows, "ag_small", False).reshape(N_DEV, pad_rows, N_MOD * D)
    gmod_all = rows_all[:, :B].reshape(N_DEV * B, N_MOD * D)
    small_parts = rows_all[:, B, :n_small]

    a4, r2 = grads["mix_pending"]
    ffn_a4, ffn_r2 = grads["ffn_pending"]
    G = dict(zip(_SHARDED, _rs_last(list(a4) + list(ffn_a4), list(r2) + list(ffn_r2), "all")))

    gmod_cols = lax.dynamic_slice_in_dim(gmod_all, chip * ada_cols, ada_cols, axis=1)
    G["w_ada"] = _ada_bwd(c_all, gmod_cols, "ada_bwd")
    delta, new_m, new_v = {}, {}, {}
    for n in ("w_ada",) + _SHARDED:
        shp = W[n].shape
        w2 = W[n].reshape(shp[-2], shp[-1])
        d_, m_, v_ = _adamw(w2, G[n], M[n].reshape(w2.shape), V[n].reshape(w2.shape), f"adamw_{n}")
        G[n], delta[n], new_m[n], new_v[n] = [a.reshape(shp) for a in (G[n], d_, m_, v_)]
    gs, ds_, ms_, vs_ = _adamw_rows(cat(W), small_parts, cat(M), cat(V), "adamw_small")
    off = 0
    for n, sz in _SMALL:
        shp = W[n].shape
        G[n], delta[n], new_m[n], new_v[n] = [a[:, off:off + sz].reshape(shp) for a in (gs, ds_, ms_, vs_)]
        off += sz
    G["b_ada"], delta["b_ada"], new_m["b_ada"], new_v["b_ada"] = _adamw_rows(b_ada, gmod_all, m_b_ada, v_b_ada,
                                                                          "adamw_b_ada")
    return (loss, grad_x, *[G[n] for n in names], *[delta[n] for n in names], *[new_m[n] for n in names],
            *[new_v[n] for n in names])
```

```python
import functools
import math

import numpy as np
import jax
import jax.numpy as jnp
from jax import lax
from jax.experimental import pallas as pl
from jax.experimental.pallas import tpu as pltpu

F32 = jnp.float32
BF16 = jnp.bfloat16

D_MODEL = 1024
SEQ = 2048
N_HEADS = 8
HEAD_DIM = 64
D_A = 512
D_B = 512
Q_LORA = 384
KV_LORA = 256
ROPE_DIM = 32
NOPE_DIM = 64
D_FF = 2816
N_MOD = 6
N_BUCKETS = 32
MAX_DISTANCE = 2048
ROPE_THETA = 10000.0
EPS = 1e-6
NEG = -1e30
BLK = 128
DILATIONS = (1, 4, 16)
SPAN = 128
MLA_SCALE = (NOPE_DIM + ROPE_DIM) ** -0.5
DIL_SCALE = HEAD_DIM ** -0.5

ADAM_LR = 0.001
ADAM_B1 = 0.9
ADAM_B2 = 0.999
ADAM_EPS = 1e-08
ADAM_WD = 0.01
ADAM_STEP = 10

N_DEV = 8
N_CHIP = 4
LANES = 128
VMEM_LIMIT = 48 * 1024 * 1024
MM_VMEM_BUDGET = 32 * 1024 * 1024

P_QKV = 3 * D_A
P_REST = KV_LORA + LANES + Q_LORA


def _cparams(sem=None):
    return pltpu.CompilerParams(dimension_semantics=sem, vmem_limit_bytes=VMEM_LIMIT)


def _pick(n, cands):
    for c in cands:
        if n % c == 0:
            return c
    raise ValueError(f"no tile for {n} in {cands}")


def _mm(a, b, mode, out_dtype, name, col_blocks=None, comm=None, halves=False):
    blocked = col_blocks is not None
    if mode == "nn":
        (M, K) = a.shape
        K2, N = (b.shape[1], b.shape[0] * b.shape[2]) if blocked else b.shape
    elif mode == "nt":
        (M, K) = (a.shape[1], 2 * a.shape[2]) if halves else a.shape
        N, K2 = (b.shape[1], b.shape[0] * b.shape[2]) if blocked else b.shape
    else:
        (K, M) = a.shape
        K2, N = (b.shape[1], 2 * b.shape[2]) if halves else b.shape
    assert K == K2, (a.shape, b.shape, mode)
    assert not halves or (blocked and col_blocks == 4 and mode in ("nt", "tn"))
    tn = _pick(N, (1408, 1024, 768, 512, 384, 256, 128))
    tk = _pick(K, (1408, 1152, 1024, 768, 512, 384, 256, 128))
    if blocked and mode == "nt":
        tk = K // col_blocks
    elif blocked:
        tn = N // col_blocks
    nk = K // tk

    def vmem_bytes(tm_):
        tiles = tm_ * tk * a.dtype.itemsize + tk * tn * b.dtype.itemsize + tm_ * tn * jnp.dtype(out_dtype).itemsize
        return 2 * tiles + tm_ * tn * 4

    tm = next(t for t in (1408, 1024, 512, 384, 256, 128) if M % t == 0 and vmem_bytes(t) <= MM_VMEM_BUDGET)
    out_shape = (M, N)
    out_spec = pl.BlockSpec((tm, tn), lambda i, j, k: (i, j))
    if mode == "nn":
        a_spec = pl.BlockSpec((tm, tk), lambda i, j, k: (i, k))
        b_spec = (pl.BlockSpec((None, tk, tn), lambda i, j, k: (j, k, 0)) if blocked
                  else pl.BlockSpec((tk, tn), lambda i, j, k: (k, j)))
        dn = (((1,), (0,)), ((), ()))
    elif mode == "nt":
        a_spec = (pl.BlockSpec((None, tm, tk), lambda i, j, k: (k // 2, i, k % 2)) if halves
                  else pl.BlockSpec((tm, tk), lambda i, j, k: (i, k)))
        b_spec = (pl.BlockSpec((None, tn, tk), lambda i, j, k: (k, j, 0)) if blocked
                  else pl.BlockSpec((tn, tk), lambda i, j, k: (j, k)))
        dn = (((1,), (1,)), ((), ()))
    else:
        a_spec = pl.BlockSpec((tk, tm), lambda i, j, k: (k, i))
        b_spec = (pl.BlockSpec((None, tk, tn), lambda i, j, k: (j // 2, k, j % 2)) if halves
                  else pl.BlockSpec((tk, tn), lambda i, j, k: (k, j)))
        dn = (((0,), (0,)), ((), ()))
        if blocked:
            out_shape = (col_blocks, M, tn)
            out_spec = pl.BlockSpec((None, tm, tn), lambda i, j, k: (j, i, 0))

    def body(a_ref, b_ref, o_ref, acc_ref):
        k = pl.program_id(2)

        @pl.when(k == 0)
        def _():
            acc_ref[...] = jnp.zeros_like(acc_ref)

        acc_ref[...] += lax.dot_general(a_ref[...].astype(BF16), b_ref[...].astype(BF16), dn,
                                        preferred_element_type=F32)

        @pl.when(k == nk - 1)
        def _():
            o_ref[...] = acc_ref[...].astype(o_ref.dtype)

    if comm is not None:
        (out,), got = _host_call(
            body, comm, name=name, out_shape=[jax.ShapeDtypeStruct(out_shape, out_dtype)],
            grid=(M // tm, N // tn, nk), in_specs=[a_spec, b_spec], out_specs=[out_spec],
            scratch_shapes=[pltpu.VMEM((tm, tn), F32)], args=(a, b))
        return out, got
    return pl.pallas_call(
        body, name=name,
        out_shape=jax.ShapeDtypeStruct(out_shape, out_dtype),
        grid=(M // tm, N // tn, nk),
        in_specs=[a_spec, b_spec],
        out_specs=out_spec,
        scratch_shapes=[pltpu.VMEM((tm, tn), F32)],
        compiler_params=_cparams(("parallel", "parallel", "arbitrary")),
    )(a, b)


ROW_TILE = 512


def _adaln_fwd(x, g, sc, sh, name, mix=None, gate=None):
    B, S, D = x.shape
    ts = ROW_TILE
    has_res = mix is not None

    def body(*refs):
        if has_res:
            x_ref, g_ref, sc_ref, sh_ref, mix_ref, gate_ref, h_ref, xr_ref = refs
            xr = x_ref[0] + gate_ref[0] * mix_ref[0]
            xr_ref[0] = xr
        else:
            x_ref, g_ref, sc_ref, sh_ref, h_ref = refs
            xr = x_ref[0]
        r = lax.rsqrt(jnp.mean(xr * xr, axis=-1, keepdims=True) + EPS)
        xn = (xr * r) * g_ref[...]
        h_ref[0] = (xn * (1.0 + sc_ref[0]) + sh_ref[0]).astype(h_ref.dtype)

    tok = pl.BlockSpec((1, ts, D), lambda b, s: (b, s, 0))
    per_b = pl.BlockSpec((1, 1, D), lambda b, s: (b, 0, 0))
    vec = pl.BlockSpec((1, D), lambda b, s: (0, 0))
    in_specs = [tok, vec, per_b, per_b]
    args = [x, g, sc, sh]
    out_shape = [jax.ShapeDtypeStruct((B, S, D), BF16)]
    out_specs = [tok]
    if has_res:
        in_specs += [tok, per_b]
        args += [mix, gate]
        out_shape.append(jax.ShapeDtypeStruct((B, S, D), F32))
        out_specs.append(tok)
    out = pl.pallas_call(
        body, name=name, out_shape=out_shape, grid=(B, S // ts),
        in_specs=in_specs, out_specs=out_specs,
        compiler_params=_cparams(("parallel", "parallel")),
    )(*args)
    return out if has_res else out[0]


def _adaln_bwd(dh, x, g, sc, dres, name, mix=None, gate=None, comm=None):
    B, S, D = x.shape
    ts = ROW_TILE
    has_res = mix is not None

    def body(*refs):
        if has_res:
            (dh_ref, x_ref, g_ref, sc_ref, dres_ref, mix_ref, gate_ref,
             dx_ref, dsh_ref, dsc_ref, dg_ref, dgate_ref, dmix_ref) = refs
        else:
            (dh_ref, x_ref, g_ref, sc_ref, dres_ref, dx_ref, dsh_ref, dsc_ref, dg_ref) = refs
        b, s = pl.program_id(0), pl.program_id(1)
        xv = x_ref[0]
        dhv = dh_ref[0]
        gv = g_ref[...]
        r = lax.rsqrt(jnp.mean(xv * xv, axis=-1, keepdims=True) + EPS)
        n = xv * r
        xn = n * gv
        dxn = dhv * (1.0 + sc_ref[0])
        dn = dxn * gv
        dx = r * (dn - n * jnp.mean(dn * n, axis=-1, keepdims=True)) + dres_ref[0]
        dx_ref[0] = dx

        @pl.when(s == 0)
        def _():
            dsh_ref[...] = jnp.zeros_like(dsh_ref)
            dsc_ref[...] = jnp.zeros_like(dsc_ref)
            if has_res:
                dgate_ref[...] = jnp.zeros_like(dgate_ref)

        @pl.when((s == 0) & (b == 0))
        def _():
            dg_ref[...] = jnp.zeros_like(dg_ref)

        dsh_ref[0] += jnp.sum(dhv, axis=0, keepdims=True)
        dsc_ref[0] += jnp.sum(dhv * xn, axis=0, keepdims=True)
        dg_ref[...] += jnp.sum(dxn * n, axis=0, keepdims=True)
        if has_res:
            dgate_ref[0] += jnp.sum(dx * mix_ref[0], axis=0, keepdims=True)
            dmix_ref[0] = (dx * gate_ref[0]).astype(dmix_ref.dtype)

    tok = pl.BlockSpec((1, ts, D), lambda b, s: (b, s, 0))
    per_b = pl.BlockSpec((1, 1, D), lambda b, s: (b, 0, 0))
    vec = pl.BlockSpec((1, D), lambda b, s: (0, 0))
    in_specs = [tok, tok, vec, per_b, tok]
    args = [dh, x, g, sc, dres]
    out_shape = [jax.ShapeDtypeStruct((B, S, D), F32), jax.ShapeDtypeStruct((B, 1, D), F32),
                 jax.ShapeDtypeStruct((B, 1, D), F32), jax.ShapeDtypeStruct((1, D), F32)]
    out_specs = [tok, per_b, per_b, vec]
    if has_res:
        in_specs += [tok, per_b]
        args += [mix, gate]
        out_shape += [jax.ShapeDtypeStruct((B, 1, D), F32), jax.ShapeDtypeStruct((B, S, D), BF16)]
        out_specs += [per_b, tok]
    res, got = _host_call(body, comm, name=name, out_shape=out_shape, grid=(B, S // ts), in_specs=in_specs,
                          out_specs=out_specs, scratch_shapes=[], args=args)
    return (list(res) + [got]) if comm is not None else res


def _rms_fwd(x, col_blk, n, g, name, n_real=None):
    T = x.shape[0]
    tr = 512
    nr = float(n_real or n)

    def body(x_ref, g_ref, y_ref):
        xv = x_ref[...]
        r = lax.rsqrt(jnp.sum(xv * xv, axis=-1, keepdims=True) / nr + EPS)
        y_ref[...] = ((xv * r) * g_ref[...]).astype(y_ref.dtype)

    return pl.pallas_call(
        body, name=name, out_shape=jax.ShapeDtypeStruct((T, n), BF16), grid=(T // tr,),
        in_specs=[pl.BlockSpec((tr, n), lambda i: (i, col_blk)), pl.BlockSpec((1, n), lambda i: (0, 0))],
        out_specs=pl.BlockSpec((tr, n), lambda i: (i, 0)),
        compiler_params=_cparams(("parallel",)),
    )(x, g)


def _rms_bwd(dy, dy_blk, x, x_blk, n, g, name, out_dtype=BF16):
    T = x.shape[0]
    tr = 512

    def body(dy_ref, x_ref, g_ref, dx_ref, dg_ref):
        xv = x_ref[...]
        dyv = dy_ref[...].astype(F32)
        r = lax.rsqrt(jnp.mean(xv * xv, axis=-1, keepdims=True) + EPS)
        nrm = xv * r
        dn = dyv * g_ref[...]
        dx_ref[...] = (r * (dn - nrm * jnp.mean(dn * nrm, axis=-1, keepdims=True))).astype(dx_ref.dtype)

        @pl.when(pl.program_id(0) == 0)
        def _():
            dg_ref[...] = jnp.zeros_like(dg_ref)

        dg_ref[...] += jnp.sum(dyv * nrm, axis=0, keepdims=True)

    return pl.pallas_call(
        body, name=name,
        out_shape=[jax.ShapeDtypeStruct((T, n), out_dtype), jax.ShapeDtypeStruct((1, n), F32)],
        grid=(T // tr,),
        in_specs=[pl.BlockSpec((tr, n), lambda i: (i, dy_blk)), pl.BlockSpec((tr, n), lambda i: (i, x_blk)),
                  pl.BlockSpec((1, n), lambda i: (0, 0))],
        out_specs=[pl.BlockSpec((tr, n), lambda i: (i, 0)), pl.BlockSpec((1, n), lambda i: (0, 0))],
        compiler_params=_cparams(("arbitrary",)),
    )(dy, x, g)


def _rms_bwd_views(dy, dy_blk, x, g, name):
    B, S, n = x.shape
    tiles = S // VIEW_TILE

    def body(dy_ref, x_ref, g_ref, d1_ref, d4_ref, d16_ref, dg_ref, dx_s):
        xv = x_ref[0]
        dyv = dy_ref[...]
        r = lax.rsqrt(jnp.mean(xv * xv, axis=-1, keepdims=True) + EPS)
        nrm = xv * r
        dn = dyv * g_ref[...]
        dx = r * (dn - nrm * jnp.mean(dn * nrm, axis=-1, keepdims=True))
        d1_ref[0] = dx.astype(d1_ref.dtype)
        _put_tile(dx_s, dx)
        _tile_to_view(dx_s, d4_ref, DILATIONS[1], n)
        _tile_to_view(dx_s, d16_ref, DILATIONS[2], n)

        @pl.when((pl.program_id(0) == 0) & (pl.program_id(1) == 0))
        def _():
            dg_ref[...] = jnp.zeros_like(dg_ref)

        dg_ref[...] += jnp.sum(dyv * nrm, axis=0, keepdims=True)

    res = pl.pallas_call(
        body, name=name,
        out_shape=[_view_shape(B, S, d, n, BF16) for d in DILATIONS] + [jax.ShapeDtypeStruct((1, n), F32)],
        grid=(B, tiles),
        in_specs=[pl.BlockSpec((VIEW_TILE, n), lambda b, t: (b * tiles + t, dy_blk)), _view_spec(1, n),
                  pl.BlockSpec((1, n), lambda b, t: (0, 0))],
        out_specs=[_view_spec(d, n) for d in DILATIONS] + [pl.BlockSpec((1, n), lambda b, t: (0, 0))],
        scratch_shapes=[_tile_scratch(n)],
        compiler_params=_cparams(("arbitrary", "arbitrary")),
    )(dy, x, g)
    return res[:len(DILATIONS)], res[len(DILATIONS)]


FFN_TILE = 1408


def _ffn_in_fwd(h, w4, name):
    T, D = h.shape
    tm, tc = 512, FFN_TILE
    nc = D_FF // tc

    def body(h_ref, wg_ref, wu_ref, gu_ref, act_ref):
        hv = h_ref[...]
        g = jnp.dot(hv, wg_ref[...], preferred_element_type=F32)
        u = jnp.dot(hv, wu_ref[...], preferred_element_type=F32)
        gu_ref[0] = g
        gu_ref[1] = u
        act_ref[...] = (g * jax.nn.sigmoid(g) * u).astype(act_ref.dtype)

    return pl.pallas_call(
        body, name=name,
        out_shape=[jax.ShapeDtypeStruct((2, T, D_FF), F32), jax.ShapeDtypeStruct((T, D_FF), BF16)],
        grid=(nc, T // tm),
        in_specs=[pl.BlockSpec((tm, D), lambda j, i: (i, 0)),
                  pl.BlockSpec((None, D, tc), lambda j, i: (j, 0, 0)),
                  pl.BlockSpec((None, D, tc), lambda j, i: (j + nc, 0, 0))],
        out_specs=[pl.BlockSpec((2, tm, tc), lambda j, i: (0, i, j)), pl.BlockSpec((tm, tc), lambda j, i: (i, j))],
        compiler_params=_cparams(("parallel", "parallel")),
    )(h, w4, w4)


def _ffn_out_bwd(df, w_out, gu, name):
    T, D = df.shape
    tm, tc = 512, FFN_TILE

    def body(df_ref, w_ref, gu_ref, dgu_ref):
        da = _dot_nt(df_ref[...], w_ref[...])
        g, u = gu_ref[0], gu_ref[1]
        sg = jax.nn.sigmoid(g)
        dgu_ref[0] = (da * u * (sg * (1.0 + g * (1.0 - sg)))).astype(dgu_ref.dtype)
        dgu_ref[1] = (da * (g * sg)).astype(dgu_ref.dtype)

    halves = pl.BlockSpec((2, tm, tc), lambda j, i: (0, i, j))
    return pl.pallas_call(
        body, name=name, out_shape=jax.ShapeDtypeStruct((2, T, D_FF), BF16), grid=(D_FF // tc, T // tm),
        in_specs=[pl.BlockSpec((tm, D), lambda j, i: (i, 0)), pl.BlockSpec((tc, D), lambda j, i: (j, 0)), halves],
        out_specs=halves,
        compiler_params=_cparams(("parallel", "parallel")),
    )(df, w_out, gu)


def _final_loss(x1, f, g2, gf, target, name):
    B, S, D = x1.shape
    ts = ROW_TILE

    def body(x1_ref, f_ref, g2_ref, gf_ref, t_ref, dx_ref, df_ref, dg2_ref, dgf_ref, loss_ref):
        b, s = pl.program_id(0), pl.program_id(1)
        fv = f_ref[0]
        g2v = g2_ref[0]
        gfv = gf_ref[...]
        x2 = x1_ref[0] + g2v * fv
        r = lax.rsqrt(jnp.mean(x2 * x2, axis=-1, keepdims=True) + EPS)
        n = x2 * r
        e = n * gfv - t_ref[0]
        dy = e * (1.0 / D)
        dn = dy * gfv
        dx = r * (dn - n * jnp.mean(dn * n, axis=-1, keepdims=True))
        dx_ref[0] = dx
        df_ref[0] = (dx * g2v).astype(df_ref.dtype)

        @pl.when(s == 0)
        def _():
            dg2_ref[...] = jnp.zeros_like(dg2_ref)

        @pl.when((s == 0) & (b == 0))
        def _():
            dgf_ref[...] = jnp.zeros_like(dgf_ref)
            loss_ref[...] = jnp.zeros_like(loss_ref)

        dg2_ref[0] += jnp.sum(dx * fv, axis=0, keepdims=True)
        dgf_ref[...] += jnp.sum(dy * n, axis=0, keepdims=True)
        loss_ref[...] += 0.5 * jnp.sum(jnp.mean(e * e, axis=-1, keepdims=True), axis=0, keepdims=True)

    tok = pl.BlockSpec((1, ts, D), lambda b, s: (b, s, 0))
    per_b = pl.BlockSpec((1, 1, D), lambda b, s: (b, 0, 0))
    vec = pl.BlockSpec((1, D), lambda b, s: (0, 0))
    return pl.pallas_call(
        body, name=name,
        out_shape=[jax.ShapeDtypeStruct((B, S, D), F32), jax.ShapeDtypeStruct((B, S, D), BF16),
                   jax.ShapeDtypeStruct((B, 1, D), F32), jax.ShapeDtypeStruct((1, D), F32),
                   jax.ShapeDtypeStruct((1, LANES), F32)],
        grid=(B, S // ts),
        in_specs=[tok, tok, per_b, vec, tok],
        out_specs=[tok, tok, per_b, vec, pl.BlockSpec((1, LANES), lambda b, s: (0, 0))],
        compiler_params=_cparams(("arbitrary", "arbitrary")),
    )(x1, f, g2, gf, target)


def _rope_tables():
    half = ROPE_DIM // 2
    inv = ROPE_THETA ** (-jnp.arange(half, dtype=F32) / half)
    ang = jnp.arange(SEQ, dtype=F32)[:, None] * inv[None, :]
    cos, sin = jnp.cos(ang), jnp.sin(ang)
    one = jnp.ones((SEQ, NOPE_DIM), F32)
    zero = jnp.zeros((SEQ, NOPE_DIM), F32)
    cs = jnp.concatenate([one, cos, cos, one[:, :LANES - NOPE_DIM - ROPE_DIM]], axis=1)
    sn = jnp.concatenate([zero, -sin, sin, zero[:, :LANES - NOPE_DIM - ROPE_DIM]], axis=1)
    return cs, sn


def _rope_group(t, cs, sn):
    half = ROPE_DIM // 2
    lane = lax.broadcasted_iota(jnp.int32, t.shape, 1)
    partner = jnp.where(lane < NOPE_DIM + half, pltpu.roll(t, LANES - half, 1), pltpu.roll(t, half, 1))
    return t * cs + partner * sn


def _rope_apply(t, cs, sn, out_dtype, name, add=None, add_blk=0):
    B, S, W = t.shape
    G = W // LANES
    ts = ROW_TILE

    def body(*refs):
        if add is None:
            t_ref, cs_ref, sn_ref, o_ref = refs
            for gi in range(G):
                sl = slice(gi * LANES, (gi + 1) * LANES)
                o_ref[0, :, sl] = _rope_group(t_ref[0, :, sl], cs_ref[...], sn_ref[...]).astype(o_ref.dtype)
        else:
            t_ref, a_ref, cs_ref, sn_ref, o_ref = refs
            ra = _rope_group(a_ref[0], cs_ref[...], sn_ref[...])
            for gi in range(G):
                sl = slice(gi * LANES, (gi + 1) * LANES)
                o_ref[0, :, sl] = (t_ref[0, :, sl] + ra).astype(o_ref.dtype)

    tok = pl.BlockSpec((1, ts, W), lambda b, s: (b, s, 0))
    tab = pl.BlockSpec((ts, LANES), lambda b, s: (s, 0))
    in_specs, args = [tok], [t]
    if add is not None:
        in_specs.append(pl.BlockSpec((1, ts, LANES), lambda b, s: (b, s, add_blk)))
        args.append(add)
    in_specs += [tab, tab]
    args += [cs, sn]
    return pl.pallas_call(
        body, name=name, out_shape=jax.ShapeDtypeStruct((B, S, W), out_dtype), grid=(B, S // ts),
        in_specs=in_specs, out_specs=tok, compiler_params=_cparams(("parallel", "parallel")),
    )(*args)


def _qrope_bwd(dq_t, cs, sn_neg, name):
    B, npair, nq, _, tq = dq_t.shape

    def body(d_ref, cs_ref, sn_ref, o_ref):
        for p in range(npair):
            tile = jnp.transpose(d_ref[0, p, 0])
            for hh in range(2):
                lo = (2 * p + hh) * LANES
                o_ref[0, :, lo:lo + LANES] = _rope_group(tile[:, hh * LANES:(hh + 1) * LANES], cs_ref[...],
                                                         sn_ref[...]).astype(o_ref.dtype)

    tab = pl.BlockSpec((tq, LANES), lambda b, i: (i, 0))
    return pl.pallas_call(
        body, name=name, out_shape=jax.ShapeDtypeStruct((B, nq * tq, N_HEADS * LANES), BF16), grid=(B, nq),
        in_specs=[pl.BlockSpec((1, npair, 1, 2 * LANES, tq), lambda b, i: (b, 0, i, 0, 0)), tab, tab],
        out_specs=pl.BlockSpec((1, tq, N_HEADS * LANES), lambda b, i: (b, i, 0)),
        compiler_params=_cparams(("parallel", "parallel")),
    )(dq_t, cs, sn_neg)


def _krope_bwd(dkc, cs, sn_neg, name):
    B, S, W = dkc.shape
    G = W // LANES
    ts = ROW_TILE

    def body(d_ref, cs_ref, sn_ref, o_ref):
        acc = d_ref[0, :, 0:LANES]
        for gi in range(1, G):
            acc = acc + d_ref[0, :, gi * LANES:(gi + 1) * LANES]
        lane = lax.broadcasted_iota(jnp.int32, acc.shape, 1)
        rot = (lane >= NOPE_DIM) & (lane < NOPE_DIM + ROPE_DIM)
        acc = jnp.where(rot, acc, 0.0)
        o_ref[0] = _rope_group(acc, cs_ref[...], sn_ref[...]).astype(o_ref.dtype)

    tab = pl.BlockSpec((ts, LANES), lambda b, s: (s, 0))
    return pl.pallas_call(
        body, name=name, out_shape=jax.ShapeDtypeStruct((B, S, LANES), BF16), grid=(B, S // ts),
        in_specs=[pl.BlockSpec((1, ts, W), lambda b, s: (b, s, 0)), tab, tab],
        out_specs=pl.BlockSpec((1, ts, LANES), lambda b, s: (b, s, 0)),
        compiler_params=_cparams(("parallel", "parallel")),
    )(dkc, cs, sn_neg)


def _t5_bucket(dist):
    max_exact = N_BUCKETS // 2
    d = np.maximum(dist, 1).astype(np.float64)
    large = max_exact + (np.log(d / max_exact) / np.log(MAX_DISTANCE / max_exact)
                         * (N_BUCKETS - max_exact)).astype(np.int64)
    large = np.minimum(large, N_BUCKETS - 1)
    return np.where(dist < max_exact, dist, large).astype(np.int32)


def _band_buckets(dilation):
    a = np.arange(BLK)[None, :]
    bk = np.arange(2 * BLK)[:, None]
    steps = BLK + a - bk
    return _t5_bucket(np.clip(steps, 0, SPAN) * dilation)


def _head_mask(shape, hh):
    lane = lax.broadcasted_iota(jnp.int32, shape, 1)
    return (lane >= hh * HEAD_DIM) & (lane < (hh + 1) * HEAD_DIM)


def _dot_nt(a, b):
    return lax.dot_general(a, b, (((1,), (1,)), ((), ())), preferred_element_type=F32)


def _dot_tn(a, b):
    return lax.dot_general(a, b, (((0,), (0,)), ((), ())), preferred_element_type=F32)


def _dot_nn(a, b):
    return lax.dot_general(a, b, (((1,), (0,)), ((), ())), preferred_element_type=F32)


def _dil_fwd(qkv, bias, branch, dilation, name, comm=None):
    B, n, _ = qkv.shape
    d = dilation
    nb = n // BLK
    qkv_v = qkv
    npair = N_HEADS // 2

    def body(cur_ref, prev_ref, bias_ref, o_ref, lse_ref, s_scr, e_scr):
        first = jnp.where(pl.program_id(2) == 0, 1, 0)
        for p in range(npair):
            q = cur_ref[0, :, p * LANES:(p + 1) * LANES] * DIL_SCALE
            kc = cur_ref[0, :, D_A + p * LANES:D_A + (p + 1) * LANES]
            kp = prev_ref[0, :, D_A + p * LANES:D_A + (p + 1) * LANES]
            for hh in range(2):
                h = 2 * p + hh
                qm = jnp.where(_head_mask((BLK, LANES), hh), q, jnp.zeros_like(q))
                s_scr[h, 0:BLK, :] = _dot_nt(kp, qm)
                s_scr[h, BLK:2 * BLK, :] = _dot_nt(kc, qm)
        ms = []
        for h in range(N_HEADS):
            s_p = s_scr[h, 0:BLK, :] + bias_ref[first, h, 0:BLK, :]
            s_c = s_scr[h, BLK:2 * BLK, :] + bias_ref[first, h, BLK:2 * BLK, :]
            m = jnp.maximum(jnp.max(s_p, axis=0, keepdims=True), jnp.max(s_c, axis=0, keepdims=True))
            e_scr[h, 0:BLK, :] = jnp.exp(s_p - m).astype(BF16)
            e_scr[h, BLK:2 * BLK, :] = jnp.exp(s_c - m).astype(BF16)
            ms.append(m)
        rows0 = _row_mask((LANES, BLK), 0)
        for p in range(npair):
            sl = slice(p * LANES, (p + 1) * LANES)
            vsl = slice(2 * D_A + p * LANES, 2 * D_A + (p + 1) * LANES)
            vct = jnp.transpose(cur_ref[0, :, vsl].astype(F32)).astype(BF16)
            vpt = jnp.transpose(prev_ref[0, :, vsl].astype(F32)).astype(BF16)
            acc = []
            for hh in range(2):
                h = 2 * p + hh
                mine = _row_mask((LANES, BLK), hh)
                one = jnp.ones_like(vct)
                acc.append(_dot_nn(jnp.where(mine, vpt, one), e_scr[h, 0:BLK, :])
                           + _dot_nn(jnp.where(mine, vct, one), e_scr[h, BLK:2 * BLK, :]))
            l0 = acc[0][HEAD_DIM:HEAD_DIM + 1, :]
            l1 = acc[1][0:1, :]
            o_t = jnp.where(rows0, acc[0] / l0, acc[1] / l1)
            lse_t = jnp.where(rows0, ms[2 * p] + jnp.log(l0), ms[2 * p + 1] + jnp.log(l1))
            o_ref[0, :, sl] = jnp.transpose(o_t)
            lse_ref[0, :, sl] = jnp.transpose(lse_t)

    cur = pl.BlockSpec((1, BLK, P_QKV), lambda b, r, i: (b, i, r))
    prev = pl.BlockSpec((1, BLK, P_QKV), lambda b, r, i: (b, jnp.maximum(i - 1, 0), r))
    out = pl.BlockSpec((1, BLK, D_A), lambda b, r, i: (b, i, r))
    return _host_call(
        body, comm, name=name,
        out_shape=[jax.ShapeDtypeStruct((B, n, d * D_A), F32)] * 2,
        grid=(B, d, nb),
        in_specs=[cur, prev,
                  pl.BlockSpec((None, 2, N_HEADS, 2 * BLK, BLK), lambda b, r, i: (branch, 0, 0, 0, 0))],
        out_specs=[out, out],
        scratch_shapes=[pltpu.VMEM((N_HEADS, 2 * BLK, BLK), F32), pltpu.VMEM((N_HEADS, 2 * BLK, BLK), BF16)],
        args=(qkv_v, qkv_v, bias))


VIEW_TILE = 512


def _view_spec(d, w):
    return pl.BlockSpec((1, VIEW_TILE // d, d * w), lambda b, t: (b, t, 0))


def _view_shape(B, S, d, w, dtype):
    return jax.ShapeDtypeStruct((B, S // d, d * w), dtype)


def _tile_scratch(w):
    return pltpu.VMEM((w // LANES, VIEW_TILE, LANES), F32)


def _put_tile(tile_ref, val):
    for c in range(tile_ref.shape[0]):
        tile_ref[c] = val[:, c * LANES:(c + 1) * LANES]


def _get_tile(tile_ref):
    return jnp.concatenate([tile_ref[c] for c in range(tile_ref.shape[0])], axis=1)


def _tile_to_view(tile_ref, view_ref, d, w):
    for c in range(w // LANES):
        for r in range(d):
            lo = r * w + c * LANES
            rows = tile_ref.at[c][pl.ds(r, VIEW_TILE // d, stride=d), :]
            view_ref[0, :, lo:lo + LANES] = rows.astype(view_ref.dtype)


def _view_to_tile(view_ref, tile_ref, d, w):
    for c in range(w // LANES):
        for r in range(d):
            lo = r * w + c * LANES
            tile_ref.at[c][pl.ds(r, VIEW_TILE // d, stride=d), :] = view_ref[0, :, lo:lo + LANES].astype(F32)


def _mm_qkv_views(h, w, name):
    B, S, D = h.shape
    N = w.shape[1]

    def body(h_ref, w_ref, o1_ref, o4_ref, o16_ref, acc_ref):
        acc = jnp.dot(h_ref[0], w_ref[...], preferred_element_type=F32)
        o1_ref[0] = acc.astype(o1_ref.dtype)
        _put_tile(acc_ref, acc)
        _tile_to_view(acc_ref, o4_ref, DILATIONS[1], N)
        _tile_to_view(acc_ref, o16_ref, DILATIONS[2], N)

    return pl.pallas_call(
        body, name=name,
        out_shape=[_view_shape(B, S, d, N, BF16) for d in DILATIONS],
        grid=(B, S // VIEW_TILE),
        in_specs=[pl.BlockSpec((1, VIEW_TILE, D), lambda b, t: (b, t, 0)), pl.BlockSpec((D, N), lambda b, t: (0, 0))],
        out_specs=[_view_spec(d, N) for d in DILATIONS],
        scratch_shapes=[_tile_scratch(N)],
        compiler_params=_cparams(("parallel", "parallel")),
    )(h, w)


def _dil_merge(os_, lses, name):
    B, S, W = os_[0].shape
    nd = len(DILATIONS)

    def body(*refs):
        o_refs, l_refs = refs[:nd], refs[nd:2 * nd]
        out_refs, L_refs = refs[2 * nd:3 * nd], refs[3 * nd:4 * nd]
        scr = refs[4 * nd:]
        o_tok, l_tok = [o_refs[0][0]], [l_refs[0][0]]
        for i, d in enumerate(DILATIONS[1:]):
            _view_to_tile(o_refs[i + 1], scr[2 * i], d, W)
            _view_to_tile(l_refs[i + 1], scr[2 * i + 1], d, W)
            o_tok.append(_get_tile(scr[2 * i]))
            l_tok.append(_get_tile(scr[2 * i + 1]))
        a0, a1, a2 = l_tok
        m = jnp.maximum(jnp.maximum(a0, a1), a2)
        e0, e1, e2 = jnp.exp(a0 - m), jnp.exp(a1 - m), jnp.exp(a2 - m)
        ssum = e0 + e1 + e2
        out = (e0 * o_tok[0] + e1 * o_tok[1] + e2 * o_tok[2]) / ssum
        lse = m + jnp.log(ssum)
        out_refs[0][0] = out
        L_refs[0][0] = lse
        res_o, res_l = scr[2 * (nd - 1)], scr[2 * (nd - 1) + 1]
        _put_tile(res_o, out)
        _put_tile(res_l, lse)
        for i, d in enumerate(DILATIONS[1:]):
            _tile_to_view(res_o, out_refs[i + 1], d, W)
            _tile_to_view(res_l, L_refs[i + 1], d, W)

    specs = [_view_spec(d, W) for d in DILATIONS]
    shapes = [_view_shape(B, S * DILATIONS[0], d, W, F32) for d in DILATIONS]
    res = pl.pallas_call(
        body, name=name, out_shape=shapes * 2, grid=(B, S // VIEW_TILE),
        in_specs=specs * 2, out_specs=specs * 2,
        scratch_shapes=[_tile_scratch(W)] * (2 * nd),
        compiler_params=_cparams(("parallel", "parallel")),
    )(*os_, *lses)
    return res[:nd], res[nd:]


def _dil_bwd(qkv, do, out_a, L, bias, branch, dilation, name, comm=None):
    B, n, _ = qkv.shape
    d = dilation
    nb = n // BLK
    qkv_v, do_v, oa_v, L_v = qkv, do, out_a, L
    npair = N_HEADS // 2
    multi = nb > 1

    tiles = ("P", "C", "N") if multi else ("C",)
    n_t = len(tiles)

    def body(*refs):
        if multi:
            (cur_ref, prev_ref, next_ref, do_ref, don_ref, oa_ref, oan_ref, L_ref, Ln_ref, bias_ref,
             dqkv_ref, dbias_ref, s_scr, dp_scr, p_scr, ds_scr) = refs
        else:
            cur_ref, do_ref, oa_ref, L_ref, bias_ref, dqkv_ref, dbias_ref, s_scr, dp_scr, p_scr, ds_scr = refs
        b, r, i = pl.program_id(0), pl.program_id(1), pl.program_id(2)

        @pl.when((b == 0) & (r == 0) & (i == 0))
        def _():
            dbias_ref[...] = jnp.zeros_like(dbias_ref)

        first = jnp.where(i == 0, 1, 0)
        variant = {"P": first, "C": first, "N": 0}
        band = {"P": slice(0, BLK), "C": slice(BLK, 2 * BLK), "N": slice(0, BLK)}
        psl = lambda p: slice(p * LANES, (p + 1) * LANES)
        ksl = lambda p: slice(D_A + p * LANES, D_A + (p + 1) * LANES)
        vsl = lambda p: slice(2 * D_A + p * LANES, 2 * D_A + (p + 1) * LANES)

        def operands(p, hh):
            hm = _head_mask((BLK, LANES), hh)
            mask = lambda x: jnp.where(hm, x, jnp.zeros_like(x))
            qm, dom = mask(cur_ref[0, :, psl(p)] * DIL_SCALE), mask(do_ref[0, :, psl(p)])
            ops = {"C": (cur_ref[0, :, ksl(p)], cur_ref[0, :, vsl(p)], qm, dom)}
            if multi:
                ops["P"] = (prev_ref[0, :, ksl(p)], prev_ref[0, :, vsl(p)], qm, dom)
                ops["N"] = (cur_ref[0, :, ksl(p)], cur_ref[0, :, vsl(p)],
                            mask(next_ref[0, :, psl(p)] * DIL_SCALE), mask(don_ref[0, :, psl(p)]))
            return ops

        for p in range(npair):
            for hh in range(2):
                h = 2 * p + hh
                ops = operands(p, hh)
                for t, name_t in enumerate(tiles):
                    k_t, v_t, q_t, do_t = ops[name_t]
                    s_scr[h, t] = _dot_nt(k_t, q_t)
                    dp_scr[h, t] = _dot_nt(v_t, do_t)

        def rows(L_r, do_r, oa_r, p):
            lt = jnp.transpose(L_r[0, :, psl(p)])
            dt = jnp.transpose(do_r[0, :, psl(p)].astype(F32) * oa_r[0, :, psl(p)])
            return ([lt[0:1, :], lt[HEAD_DIM:HEAD_DIM + 1, :]],
                    [jnp.sum(dt[:HEAD_DIM], axis=0, keepdims=True), jnp.sum(dt[HEAD_DIM:], axis=0, keepdims=True)])

        for p in range(npair):
            lse_c, delta_c = rows(L_ref, do_ref, oa_ref, p)
            if multi:
                lse_n, delta_n = rows(Ln_ref, don_ref, oan_ref, p)
            for hh in range(2):
                h = 2 * p + hh
                for t, name_t in enumerate(tiles):
                    lse, delta = (lse_n[hh], delta_n[hh]) if name_t == "N" else (lse_c[hh], delta_c[hh])
                    pr = jnp.exp(s_scr[h, t] + bias_ref[variant[name_t], h, band[name_t], :] - lse)
                    if name_t == "N":
                        pr = jnp.where(i < nb - 1, pr, 0.0)
                    ds = pr * (dp_scr[h, t] - delta)
                    p_scr[h, t] = pr.astype(BF16)
                    ds_scr[h, t] = ds.astype(BF16)
                    if name_t != "N":
                        dbias_ref[h, band[name_t], :] += ds

        for p in range(npair):
            dqt = jnp.zeros((LANES, BLK), F32)
            dk = jnp.zeros((BLK, LANES), F32)
            dv = jnp.zeros((BLK, LANES), F32)
            kct = jnp.transpose(cur_ref[0, :, ksl(p)].astype(F32)).astype(BF16)
            if multi:
                kpt = jnp.transpose(prev_ref[0, :, ksl(p)].astype(F32)).astype(BF16)
            for hh in range(2):
                h = 2 * p + hh
                ops = operands(p, hh)
                mine = _row_mask((LANES, BLK), hh)
                for t, name_t in enumerate(tiles):
                    _, _, q_t, do_t = ops[name_t]
                    if name_t != "P":
                        dv = dv + _dot_nn(p_scr[h, t], do_t)
                        dk = dk + _dot_nn(ds_scr[h, t], q_t)
                    if name_t != "N":
                        kt = kpt if name_t == "P" else kct
                        dqt = dqt + _dot_nn(jnp.where(mine, kt, jnp.zeros_like(kt)), ds_scr[h, t])
            dqkv_ref[0, :, psl(p)] = jnp.transpose(dqt) * DIL_SCALE
            dqkv_ref[0, :, ksl(p)] = dk
            dqkv_ref[0, :, vsl(p)] = dv

    def at(off):
        return lambda b, r, i: (b, jnp.clip(i + off, 0, nb - 1), r)

    qkv_spec = lambda off: pl.BlockSpec((1, BLK, P_QKV), at(off))
    da_spec = lambda off: pl.BlockSpec((1, BLK, D_A), at(off))
    bias_spec = pl.BlockSpec((None, 2, N_HEADS, 2 * BLK, BLK), lambda b, r, i: (branch, 0, 0, 0, 0))
    dbias_spec = pl.BlockSpec((N_HEADS, 2 * BLK, BLK), lambda b, r, i: (0, 0, 0))
    if multi:
        in_specs = [qkv_spec(0), qkv_spec(-1), qkv_spec(1), da_spec(0), da_spec(1), da_spec(0), da_spec(1),
                    da_spec(0), da_spec(1), bias_spec]
        args = [qkv_v, qkv_v, qkv_v, do_v, do_v, oa_v, oa_v, L_v, L_v, bias]
    else:
        in_specs = [qkv_spec(0), da_spec(0), da_spec(0), da_spec(0), bias_spec]
        args = [qkv_v, do_v, oa_v, L_v, bias]
    return _host_call(
        body, comm, name=name,
        out_shape=[jax.ShapeDtypeStruct((B, n, d * P_QKV), F32),
                   jax.ShapeDtypeStruct((N_HEADS, 2 * BLK, BLK), F32)],
        grid=(B, d, nb),
        in_specs=in_specs,
        out_specs=[qkv_spec(0), dbias_spec],
        scratch_shapes=[pltpu.VMEM((N_HEADS, n_t, BLK, BLK), F32), pltpu.VMEM((N_HEADS, n_t, BLK, BLK), F32),
                        pltpu.VMEM((N_HEADS, n_t, BLK, BLK), BF16), pltpu.VMEM((N_HEADS, n_t, BLK, BLK), BF16)],
        args=args)


def _sum_views_bf16(parts, name):
    B, S, W = parts[0].shape

    def body(a_ref, b_ref, c_ref, o_ref, sb, sc):
        _view_to_tile(b_ref, sb, DILATIONS[1], W)
        _view_to_tile(c_ref, sc, DILATIONS[2], W)
        o_ref[0] = (a_ref[0] + _get_tile(sb) + _get_tile(sc)).astype(o_ref.dtype)

    return pl.pallas_call(
        body, name=name, out_shape=jax.ShapeDtypeStruct((B, S, W), BF16), grid=(B, S // VIEW_TILE),
        in_specs=[_view_spec(d, W) for d in DILATIONS], out_specs=_view_spec(1, W),
        scratch_shapes=[_tile_scratch(W)] * 2,
        compiler_params=_cparams(("parallel", "parallel")),
    )(*parts)


def _bias_tables(rel_bias, buckets, name, comm=None):
    nbr = buckets.shape[0]

    def body(rb_ref, bk_ref, o_ref):
        first, h = pl.program_id(1), pl.program_id(2)
        tab = bk_ref[0]

        def step(bkt, acc):
            return jnp.where(tab == bkt, rb_ref[bkt, h], acc)

        bias = lax.fori_loop(0, N_BUCKETS, step, jnp.zeros((2 * BLK, BLK), F32))
        row = lax.broadcasted_iota(jnp.int32, (2 * BLK, BLK), 0)
        col = lax.broadcasted_iota(jnp.int32, (2 * BLK, BLK), 1)
        valid = ((row < BLK) & (row >= col) & (first == 0)) | ((row >= BLK) & (row - BLK <= col))
        o_ref[0, 0, 0] = jnp.where(valid, bias, NEG)

    (bias,), got = _host_call(
        body, comm, name=name, out_shape=[jax.ShapeDtypeStruct((nbr, 2, N_HEADS, 2 * BLK, BLK), F32)],
        grid=(nbr, 2, N_HEADS),
        in_specs=[pl.BlockSpec(memory_space=pltpu.SMEM),
                  pl.BlockSpec((1, 2 * BLK, BLK), lambda i, f, h: (i, 0, 0))],
        out_specs=[pl.BlockSpec((1, 1, 1, 2 * BLK, BLK), lambda i, f, h: (i, f, h, 0, 0))],
        scratch_shapes=[], args=(rel_bias, buckets))
    return bias, got


def _bias_grad(dbias_list, buckets, name):
    nbr = len(dbias_list)

    def body(*refs):
        d_refs, bk_ref, o_ref, part = refs[:nbr], refs[nbr], refs[nbr + 1], refs[nbr + 2]

        def step(bkt, carry):
            hit = [bk_ref[bi] == bkt for bi in range(nbr)]
            for h in range(N_HEADS):
                tot = jnp.zeros((1, BLK), F32)
                for bi in range(nbr):
                    tot = tot + jnp.sum(jnp.where(hit[bi], d_refs[bi][h], 0.0), axis=0, keepdims=True)
                part[bkt, h:h + 1, :] = tot
            return carry

        lax.fori_loop(0, N_BUCKETS, step, 0)
        lane = lax.broadcasted_iota(jnp.int32, (N_HEADS, LANES), 1)
        acc = jnp.zeros((N_HEADS, LANES), F32)
        for bkt in range(N_BUCKETS):
            acc = acc + jnp.where(lane == bkt, jnp.sum(part[bkt], axis=1, keepdims=True), 0.0)
        o_ref[...] = acc

    band = pl.BlockSpec((N_HEADS, 2 * BLK, BLK), lambda i: (0, 0, 0))
    return pl.pallas_call(
        body, name=name, out_shape=jax.ShapeDtypeStruct((N_HEADS, LANES), F32), grid=(1,),
        in_specs=[band] * nbr + [pl.BlockSpec((nbr, 2 * BLK, BLK), lambda i: (0, 0, 0))],
        out_specs=pl.BlockSpec((N_HEADS, LANES), lambda i: (0, 0)),
        scratch_shapes=[pltpu.VMEM((N_BUCKETS, N_HEADS, BLK), F32)],
        compiler_params=_cparams(("arbitrary",)),
    )(*dbias_list, buckets)


MLA_TQ = 256
MLA_TK = 256


LOG2E = math.log2(math.e)
MLA_C = MLA_SCALE * LOG2E


def _key_le_query(tk, tq):
    return lax.broadcasted_iota(jnp.int32, (tk, tq), 0) <= lax.broadcasted_iota(jnp.int32, (tk, tq), 1)


def _row_mask(shape, hh):
    row = lax.broadcasted_iota(jnp.int32, shape, 0)
    return (row >= hh * HEAD_DIM) & (row < (hh + 1) * HEAD_DIM)


def _host_call(body, comm, *, name, grid, in_specs, out_specs, out_shape, scratch_shapes, args):
    sem = ("arbitrary",) * len(grid)
    if comm is None:
        res = pl.pallas_call(body, name=name, grid=grid, in_specs=in_specs, out_specs=out_specs,
                             out_shape=out_shape, scratch_shapes=scratch_shapes,
                             compiler_params=_cparams(sem))(*args)
        return res, []
    n_in, n_out, n_s, cn = len(in_specs), len(out_specs), len(scratch_shapes), comm.n

    def hosted(*refs):
        ins, refs = refs[:n_in], refs[n_in:]
        c_ins, refs = refs[:cn], refs[cn:]
        outs, refs = refs[:n_out], refs[n_out:]
        c_outs, refs = refs[:cn], refs[cn:]
        scr, c_sems = refs[:n_s], refs[n_s:]
        ids = [pl.program_id(a) for a in range(len(grid))]
        first = functools.reduce(jnp.logical_and, [i == 0 for i in ids])
        last = functools.reduce(jnp.logical_and, [i == g - 1 for i, g in zip(ids, grid)])

        @pl.when(first)
        def _():
            comm.start(c_ins, c_outs, c_sems)

        body(*ins, *outs, *scr)

        @pl.when(last)
        def _():
            comm.finish(c_ins, c_outs, c_sems)

    res = pl.pallas_call(
        hosted, name=name, grid=grid, in_specs=list(in_specs) + _hbm_specs(cn),
        out_specs=list(out_specs) + _hbm_specs(cn), out_shape=list(out_shape) + list(comm.out_shape),
        scratch_shapes=list(scratch_shapes) + list(comm.scratch), compiler_params=_cparams(sem),
    )(*args, *comm.inputs)
    return res[:n_out], res[n_out:]


def _mla_fwd_t(q, k, vt, name, comm=None):
    B, S, _ = q.shape
    tq, tk = MLA_TQ, MLA_TK
    assert tq == tk
    npair = N_HEADS // 2
    nq = S // tq

    def body(q_ref, k_ref, vt_ref, o_ref, lse_ref, s_scr, e_scr, acc_scr, m_scr, a_scr):
        i = pl.program_id(1)
        diag = _key_le_query(tk, tq)
        m_scr[...] = jnp.full_like(m_scr, NEG)
        acc_scr[...] = jnp.zeros_like(acc_scr)

        def step(j, masked):
            rows = pl.ds(pl.multiple_of(j * tk, tk), tk)
            for h in range(N_HEADS):
                hsl = slice(h * LANES, (h + 1) * LANES)
                s_scr[h] = _dot_nt(k_ref[0, rows, hsl], q_ref[0, :, hsl])
            for h in range(N_HEADS):
                s = s_scr[h]
                if masked:
                    s = jnp.where(diag, s, NEG)
                m_old = m_scr[h:h + 1, :]
                m_new = jnp.maximum(m_old, jnp.max(s, axis=0, keepdims=True))
                a_scr[h:h + 1, :] = jnp.exp2((m_old - m_new) * MLA_C)
                e_scr[h] = jnp.exp2((s - m_new) * MLA_C).astype(BF16)
                m_scr[h:h + 1, :] = m_new
            for h in range(N_HEADS):
                vj = vt_ref[0, h // 2, j]
                vh = jnp.where(_row_mask(vj.shape, h % 2), vj, jnp.ones_like(vj))
                acc_scr[h] = acc_scr[h] * a_scr[h:h + 1, :] + _dot_nn(vh, e_scr[h])

        def loop_body(j, carry):
            step(j, False)
            return carry

        lax.fori_loop(0, i, loop_body, 0)
        step(i, True)
        rows0 = _row_mask((LANES, tq), 0)
        for p in range(npair):
            l0 = acc_scr[2 * p, HEAD_DIM:HEAD_DIM + 1, :]
            l1 = acc_scr[2 * p + 1, 0:1, :]
            o_t = jnp.where(rows0, acc_scr[2 * p] / l0, acc_scr[2 * p + 1] / l1)
            o_ref[0, :, p * LANES:(p + 1) * LANES] = jnp.transpose(o_t)
            lse_ref[0, p, 0] = jnp.zeros((8, tq), F32)
            lse_ref[0, p, 0, 0:1, :] = m_scr[2 * p:2 * p + 1, :] * MLA_C + jnp.log(l0) * LOG2E
            lse_ref[0, p, 0, 1:2, :] = m_scr[2 * p + 1:2 * p + 2, :] * MLA_C + jnp.log(l1) * LOG2E

    return _host_call(
        body, comm, name=name,
        out_shape=[jax.ShapeDtypeStruct((B, S, D_B), F32), jax.ShapeDtypeStruct((B, npair, nq, 8, tq), F32)],
        grid=(B, nq),
        in_specs=[pl.BlockSpec((1, tq, N_HEADS * LANES), lambda b, i: (b, i, 0)),
                  pl.BlockSpec((1, S, N_HEADS * LANES), lambda b, i: (b, 0, 0)),
                  pl.BlockSpec((1, npair, S // tk, LANES, tk), lambda b, i: (b, 0, 0, 0, 0))],
        out_specs=[pl.BlockSpec((1, tq, D_B), lambda b, i: (b, i, 0)),
                   pl.BlockSpec((1, npair, 1, 8, tq), lambda b, i: (b, 0, i, 0, 0))],
        scratch_shapes=[pltpu.VMEM((N_HEADS, tk, tq), F32), pltpu.VMEM((N_HEADS, tk, tq), BF16),
                        pltpu.VMEM((N_HEADS, LANES, tq), F32), pltpu.VMEM((N_HEADS, tq), F32),
                        pltpu.VMEM((N_HEADS, tq), F32)],
        args=(q, k, vt))


def _mla_delta(do, o, name):
    B, S, _ = o.shape
    tq = MLA_TQ
    npair = N_HEADS // 2

    def body(do_ref, o_ref, d_ref):
        d_ref[...] = jnp.zeros_like(d_ref)
        for p in range(npair):
            sl = slice(p * LANES, (p + 1) * LANES)
            prod_t = jnp.transpose(do_ref[0, :, sl].astype(F32) * o_ref[0, :, sl])
            d_ref[0, p, 0, 0:1, :] = jnp.sum(prod_t[:HEAD_DIM], axis=0, keepdims=True)
            d_ref[0, p, 0, 1:2, :] = jnp.sum(prod_t[HEAD_DIM:], axis=0, keepdims=True)

    tok = pl.BlockSpec((1, tq, D_B), lambda b, i: (b, i, 0))
    return pl.pallas_call(
        body, name=name, out_shape=jax.ShapeDtypeStruct((B, npair, S // tq, 8, tq), F32),
        grid=(B, S // tq), in_specs=[tok, tok],
        out_specs=pl.BlockSpec((1, npair, 1, 8, tq), lambda b, i: (b, 0, i, 0, 0)),
        compiler_params=_cparams(("parallel", "parallel")),
    )(do, o)


def _mla_bwd_t(q, k, v, do, lse, delta, name, comm=None):
    B, S, _ = q.shape
    tq, tk = MLA_TQ, MLA_TK
    assert tq == tk
    npair = N_HEADS // 2
    nq = S // tq

    hg = 4
    pg = hg // 2
    ngroup = N_HEADS // hg

    def body(q_ref, do_ref, lse_ref, dl_ref, k_ref, v_ref, dk_ref, dv_ref, dq_ref,
             s_scr, dp_scr, p_scr, ds_scr, dk_s, dv_s, kt_s):
        j = pl.program_id(2)

        @pl.when(j == 0)
        def _():
            dq_ref[...] = jnp.zeros_like(dq_ref)

        dk_s[...] = jnp.zeros_like(dk_s)
        dv_s[...] = jnp.zeros_like(dv_s)
        diag = _key_le_query(tk, tq)
        hsl = lambda h: slice(h * LANES, (h + 1) * LANES)
        for h in range(hg):
            kt_s[h] = jnp.transpose(k_ref[0, :, hsl(h)].astype(F32)).astype(BF16)

        def step(i, masked):
            rows = pl.ds(pl.multiple_of(i * tq, tq), tq)

            def dom(h):
                dov = do_ref[0, rows, hsl(h // 2)]
                return jnp.where(_head_mask((tq, LANES), h % 2), dov, jnp.zeros_like(dov))

            for h in range(hg):
                s_scr[h] = _dot_nt(k_ref[0, :, hsl(h)], q_ref[0, rows, hsl(h)])
                dp_scr[h] = _dot_nt(v_ref[0, :, hsl(h // 2)], dom(h))
            for h in range(hg):
                pr = jnp.exp2(s_scr[h] * MLA_C - lse_ref[0, h // 2, i, h % 2:h % 2 + 1, :])
                if masked:
                    pr = jnp.where(diag, pr, 0.0)
                p_scr[h] = pr.astype(BF16)
                ds_scr[h] = (pr * (dp_scr[h] - dl_ref[0, h // 2, i, h % 2:h % 2 + 1, :])).astype(BF16)
            for h in range(hg):
                dv_s[h // 2] += _dot_nn(p_scr[h], dom(h))
                dk_s[h] += _dot_nn(ds_scr[h], q_ref[0, rows, hsl(h)])
                dq_ref[0, h // 2, i, hsl(h % 2), :] += _dot_nn(kt_s[h], ds_scr[h]) * MLA_SCALE

        step(j, True)

        def loop_body(i, carry):
            step(i, False)
            return carry

        lax.fori_loop(j + 1, nq, loop_body, 0)
        for h in range(hg):
            dk_ref[0, :, hsl(h)] = dk_s[h] * MLA_SCALE
        for p in range(pg):
            dv_ref[0, :, hsl(p)] = dv_s[p]

    stat = pl.BlockSpec((1, pg, nq, 8, tq), lambda b, g, j: (b, g, 0, 0, 0))
    return _host_call(
        body, comm, name=name,
        out_shape=[jax.ShapeDtypeStruct((B, S, N_HEADS * LANES), F32), jax.ShapeDtypeStruct((B, S, D_B), F32),
                   jax.ShapeDtypeStruct((B, npair, nq, 2 * LANES, tq), F32)],
        grid=(B, ngroup, S // tk),
        in_specs=[pl.BlockSpec((1, S, hg * LANES), lambda b, g, j: (b, 0, g)),
                  pl.BlockSpec((1, S, pg * LANES), lambda b, g, j: (b, 0, g)),
                  stat, stat,
                  pl.BlockSpec((1, tk, hg * LANES), lambda b, g, j: (b, j, g)),
                  pl.BlockSpec((1, tk, pg * LANES), lambda b, g, j: (b, j, g))],
        out_specs=[pl.BlockSpec((1, tk, hg * LANES), lambda b, g, j: (b, j, g)),
                   pl.BlockSpec((1, tk, pg * LANES), lambda b, g, j: (b, j, g)),
                   pl.BlockSpec((1, pg, nq, 2 * LANES, tq), lambda b, g, j: (b, g, 0, 0, 0))],
        scratch_shapes=[pltpu.VMEM((hg, tk, tq), F32), pltpu.VMEM((hg, tk, tq), F32),
                        pltpu.VMEM((hg, tk, tq), BF16), pltpu.VMEM((hg, tk, tq), BF16),
                        pltpu.VMEM((hg, tk, LANES), F32), pltpu.VMEM((pg, tk, LANES), F32),
                        pltpu.VMEM((hg, LANES, tk), BF16)],
        args=(q, do, lse, delta, k, v))


def _bucket_tables():
    return jnp.asarray(np.stack([_band_buckets(d) for d in DILATIONS]))


def _local_step(x, target, mod, wts, gains, rel_bias, ffn_shards=None, bias=None):
    B, S, D = x.shape
    T = B * S
    sh1, sc1, g1, sh2, sc2, g2 = [mod[:, i * D:(i + 1) * D].reshape(B, 1, D) for i in range(N_MOD)]
    cs, sn = _rope_tables()
    buckets_dev = _bucket_tables()
    if bias is None:
        bias, _ = _bias_tables(rel_bias, buckets_dev, "rel_bias_tables")
    w_in = wts["w_in"]

    h1 = _adaln_fwd(x, gains["g_norm1"], sc1, sh1, "adaln1_fwd")
    h1f = h1.reshape(T, D)
    qkv_v = _mm_qkv_views(h1, w_in[:, :P_QKV], "mm_qkv")
    rest = _mm(h1f, w_in[:, P_QKV:], "nn", F32, "mm_rest")
    o_d, lse_d = [], []
    late_got = []
    for i, d in enumerate(DILATIONS):
        comm = _GatherComm(ffn_shards[i + 1:i + 2]) if (ffn_shards and i < 2) else None
        (o_i, lse_i), got = _dil_fwd(qkv_v[i], bias, i, d, f"dil_fwd_{d}", comm)
        late_got += list(got)
        o_d.append(o_i)
        lse_d.append(lse_i)
    if ffn_shards:
        wts = dict(wts, w_out=late_got[1].reshape(D, D))
    out_a_v, lse_a_v = _dil_merge(o_d, lse_d, "dil_merge")
    out_a = out_a_v[0]
    cqn = _rms_fwd(rest, 1, Q_LORA, gains["g_cq"], "rms_cq_fwd")
    ckvn = _rms_fwd(rest, 0, KV_LORA, gains["g_ckv"], "rms_ckv_fwd")
    rest3 = rest.reshape(B, S, P_REST)
    q_raw = _mm(cqn, wts["w_uq"], "nn", F32, "mm_uq").reshape(B, S, N_HEADS * LANES)
    qc = _rope_apply(q_raw, cs, sn, BF16, "rope_q")
    kn_raw = _mm(ckvn, wts["w_kv"][:, :N_HEADS * LANES], "nn", F32, "mm_uk").reshape(B, S, N_HEADS * LANES)
    kc = _rope_apply(kn_raw, cs, sn, BF16, "rope_k", add=rest3, add_blk=KV_LORA // LANES)
    v = _mm(ckvn, wts["w_kv"][:, N_HEADS * LANES:], "nn", BF16, "mm_uv").reshape(B, S, D_B)
    vt = jnp.transpose(v.reshape(B, S // MLA_TK, MLA_TK, N_HEADS // 2, LANES), (0, 3, 1, 4, 2))
    (out_b, lse_b), got = _mla_fwd_t(qc, kc, vt, "mla_fwd", _GatherComm(ffn_shards[:1]) if ffn_shards else None)
    if ffn_shards:
        wts = dict(wts, w_ffn_in=got[0].reshape(N_CHIP, D, -1), w_ffn_out=late_got[0].reshape(D_FF, D))
    out_af, out_bf = out_a.reshape(T, D_A), out_b.reshape(T, D_B)
    ya = _rms_fwd(out_af, 0, D_A, gains["g_out_a"], "rms_outa_fwd")
    yb = _rms_fwd(out_bf, 0, D_B, gains["g_out_b"], "rms_outb_fwd")
    y = jnp.concatenate([ya, yb], axis=1)
    mix = _mm(y, wts["w_out"], "nn", F32, "mm_out").reshape(B, S, D)
    h2, x1 = _adaln_fwd(x, gains["g_norm2"], sc2, sh2, "adaln2_fwd", mix=mix, gate=g1)
    h2f = h2.reshape(T, D)
    gu, act = _ffn_in_fwd(h2f, wts["w_ffn_in"], "mm_ffn_in")
    f = _mm(act, wts["w_ffn_out"], "nn", F32, "mm_ffn_out").reshape(B, S, D)
    dx2, df, dg2, dg_final, loss = _final_loss(x1, f, g2, gains["g_final"], target, "final_loss")

    dff = df.reshape(T, D)
    dgu = _ffn_out_bwd(dff, wts["w_ffn_out"], gu, "mm_ffn_out_dx")
    gw_ffn_out = _mm(act, dff, "tn", F32, "mm_ffn_out_dw")
    dh2 = _mm(dgu, wts["w_ffn_in"], "nt", F32, "mm_ffn_in_dx", col_blocks=N_CHIP, halves=True).reshape(B, S, D)
    gw_ffn_in = _mm(h2f, dgu, "tn", F32, "mm_ffn_in_dw", col_blocks=N_CHIP, halves=True)
    ffn_g8 = ffn_r1 = None
    if ffn_shards:
        ffn_g8 = [gw_ffn_in.reshape(N_DEV, -1, gw_ffn_in.shape[-1]), gw_ffn_out.reshape(N_DEV, -1, D)]
        dx1, dsh2, dsc2, dg_norm2, dg1, dmix, ffn_r1 = _adaln_bwd(
            dh2, x1, gains["g_norm2"], sc2, dx2, "adaln2_bwd", mix=mix, gate=g1, comm=_ToSiblingComm(ffn_g8))
    else:
        dx1, dsh2, dsc2, dg_norm2, dg1, dmix = _adaln_bwd(dh2, x1, gains["g_norm2"], sc2, dx2, "adaln2_bwd",
                                                          mix=mix, gate=g1)
    dmixf = dmix.reshape(T, D)
    dy = _mm(dmixf, wts["w_out"], "nt", F32, "mm_out_dx")
    gw_out = _mm(y, dmixf, "tn", F32, "mm_out_dw")
    do_a_v, dg_out_a = _rms_bwd_views(dy, 0, out_a, gains["g_out_a"], "rms_outa_bwd")
    do_b, dg_out_b = _rms_bwd(dy, 1, out_bf, 0, D_B, gains["g_out_b"], "rms_outb_bwd")
    do_b3 = do_b.reshape(B, S, D_B)
    delta_b = _mla_delta(do_b3, out_b, "mla_delta")
    ffn_a4 = ffn_send = None
    if ffn_shards:
        ffn_a4, ffn_send = _rs_first(ffn_g8, "ffn", r1=ffn_r1)
    (dkc, dv, dq_t), ffn_r2 = _mla_bwd_t(qc, kc, v, do_b3, lse_b, delta_b, "mla_bwd",
                                         _ToChipsComm(ffn_send[:1]) if ffn_shards else None)
    dq_raw = _qrope_bwd(dq_t, cs, -sn, "rope_q_bwd").reshape(T, N_HEADS * LANES)
    dkrw = _krope_bwd(dkc, cs, -sn, "rope_k_bwd").reshape(T, LANES)
    dcqn = _mm(dq_raw, wts["w_uq"], "nt", F32, "mm_uq_dx")
    gw_uq = _mm(cqn, dq_raw, "tn", F32, "mm_uq_dw")
    dkv = jnp.concatenate([dkc.reshape(T, -1), dv.reshape(T, -1)], axis=1).astype(BF16)
    dckvn = _mm(dkv, wts["w_kv"], "nt", F32, "mm_ukv_dx")
    gw_kv = _mm(ckvn, dkv, "tn", F32, "mm_ukv_dw")
    dcq, dg_cq = _rms_bwd(dcqn, 0, rest, 1, Q_LORA, gains["g_cq"], "rms_cq_bwd")
    dckv, dg_ckv = _rms_bwd(dckvn, 0, rest, 0, KV_LORA, gains["g_ckv"], "rms_ckv_bwd")
    dqkv_d, dbias_d = [], []
    for i, d in enumerate(DILATIONS):
        comm = _ToChipsComm(ffn_send[1:]) if (ffn_shards and i == 0) else None
        (dqkv_i, dbias_i), got = _dil_bwd(qkv_v[i], do_a_v[i], out_a_v[i], lse_a_v[i], bias, i, d,
                                          f"dil_bwd_{d}", comm)
        if comm is not None:
            ffn_r2 = list(ffn_r2) + list(got)
        dqkv_d.append(dqkv_i)
        dbias_d.append(dbias_i)
    dqkv = _sum_views_bf16(dqkv_d, "dil_bwd_sum").reshape(T, P_QKV)
    g_rel_bias = _bias_grad(dbias_d, buckets_dev, "rel_bias_grad")[:, :N_BUCKETS].T
    dproj = jnp.concatenate([dqkv, dckv, dkrw, dcq], axis=1)
    gw_in = _mm(h1f, dproj, "tn", F32, "mm_in_dw")
    mix_a4 = mix_r2 = None
    if ffn_shards:
        nat = [_w_in_from_kernel(gw_in), _w_uq_from_kernel(gw_uq), _w_ukv_from_kernel(gw_kv)]
        g8 = [_shards_from_full(g) for g in nat] + [gw_out]
        mix_a4, mix_send = _rs_first([g.reshape(N_DEV, -1, g.shape[-1]) for g in g8], "mix")
        dh1, mix_r2 = _mm(dproj, w_in, "nt", F32, "mm_in_dx", comm=_ToChipsComm(mix_send))
    else:
        dh1 = _mm(dproj, w_in, "nt", F32, "mm_in_dx")
    dh1 = dh1.reshape(B, S, D)
    grad_x, dsh1, dsc1, dg_norm1 = _adaln_bwd(dh1, x, gains["g_norm1"], sc1, dx1, "adaln1_bwd")
    gmod = jnp.concatenate([dsh1, dsc1, dg1, dsh2, dsc2, dg2], axis=-1).reshape(B, N_MOD * D)
    grads = dict(w_in=gw_in, w_uq=gw_uq, w_kv=gw_kv, w_out=gw_out, w_ffn_in=gw_ffn_in, w_ffn_out=gw_ffn_out,
                 g_norm1=dg_norm1, g_cq=dg_cq, g_ckv=dg_ckv, rel_bias=g_rel_bias, g_out_a=dg_out_a,
                 g_out_b=dg_out_b, g_norm2=dg_norm2, g_final=dg_final, ffn_pending=(ffn_a4, ffn_r2),
                 mix_pending=(mix_a4, mix_r2))
    return loss, grad_x, gmod, grads


def _w_in_to_kernel(w):
    z = lambda n: jnp.zeros((w.shape[0], n), w.dtype)
    i3, i4, i5 = 3 * D_A, 3 * D_A + Q_LORA, 3 * D_A + Q_LORA + KV_LORA
    return jnp.concatenate([w[:, :i3], w[:, i4:i5], z(NOPE_DIM), w[:, i5:], z(LANES - NOPE_DIM - ROPE_DIM),
                            w[:, i3:i4]], axis=1)


def _w_in_from_kernel(g):
    o = P_QKV + KV_LORA
    return jnp.concatenate([g[:, :P_QKV], g[:, o + LANES:], g[:, P_QKV:o],
                            g[:, o + NOPE_DIM:o + NOPE_DIM + ROPE_DIM]], axis=1)


def _w_uq_to_kernel(w):
    w3 = w.reshape(Q_LORA, N_HEADS, NOPE_DIM + ROPE_DIM)
    return jnp.pad(w3, ((0, 0), (0, 0), (0, LANES - NOPE_DIM - ROPE_DIM))).reshape(Q_LORA, N_HEADS * LANES)


def _w_uq_from_kernel(g):
    return g.reshape(Q_LORA, N_HEADS, LANES)[:, :, :NOPE_DIM + ROPE_DIM].reshape(Q_LORA, -1)


def _w_ukv_to_kernel(w):
    w3 = w.reshape(KV_LORA, N_HEADS, 2 * HEAD_DIM)
    wk = jnp.pad(w3[:, :, :NOPE_DIM], ((0, 0), (0, 0), (0, LANES - NOPE_DIM))).reshape(KV_LORA, N_HEADS * LANES)
    wv = w3[:, :, NOPE_DIM:].reshape(KV_LORA, D_B)
    return jnp.concatenate([wk, wv], axis=1)


def _w_ukv_from_kernel(g):
    gk = g[:, :N_HEADS * LANES].reshape(KV_LORA, N_HEADS, LANES)[:, :, :NOPE_DIM]
    gv = g[:, N_HEADS * LANES:].reshape(KV_LORA, N_HEADS, HEAD_DIM)
    return jnp.concatenate([gk, gv], axis=2).reshape(KV_LORA, -1)


MESH = pl.DeviceIdType.MESH


def _my_place():
    return lax.axis_index("x"), lax.axis_index("y"), lax.axis_index("c")


def _other_chips(x, y):
    return [(1 - x, y), (x, 1 - y), (1 - x, 1 - y)]


def _allgather8(x_shard, name, in_hbm):
    m_per, n = x_shard.shape
    space = pl.ANY if in_hbm else pltpu.VMEM

    def body(x_ref, out_ref, send_sems, recv_sems, local_sem):
        x, y, c = _my_place()
        me, sibling = (x, y, c), (x, y, 1 - c)
        chips = _other_chips(x, y)

        def rows(px, py, pc):
            return out_ref.at[pl.ds((4 * px + 2 * py + pc) * m_per, m_per), :]

        def copy(k, block, to, src=None):
            return pltpu.make_async_remote_copy(
                src_ref=rows(*block) if src is None else src, dst_ref=rows(*block),
                send_sem=send_sems.at[k], recv_sem=recv_sems.at[k], device_id=to, device_id_type=MESH)

        mine = pltpu.make_async_copy(x_ref, rows(*me), local_sem)
        mine.start()
        first = [copy(0, me, sibling, src=x_ref)]
        first += [copy(1 + j, me, (*chip, c), src=x_ref) for j, chip in enumerate(chips)]
        for cp in first:
            cp.start()
        passed = [copy(4 + j, (*chip, c), sibling) for j, chip in enumerate(chips)]
        for j, chip in enumerate(chips):
            copy(1 + j, (*chip, c), me).wait_recv()
            passed[j].start()
        copy(0, sibling, me).wait_recv()
        for j, chip in enumerate(chips):
            copy(4 + j, (*chip, 1 - c), me).wait_recv()
        for cp in first + passed:
            cp.wait_send()
        mine.wait()

    return pl.pallas_call(
        body, name=name,
        out_shape=jax.ShapeDtypeStruct((N_DEV * m_per, n), x_shard.dtype),
        in_specs=[pl.BlockSpec(memory_space=space)],
        out_specs=pl.BlockSpec(memory_space=space),
        scratch_shapes=[pltpu.SemaphoreType.DMA((7,)), pltpu.SemaphoreType.DMA((7,)), pltpu.SemaphoreType.DMA],
        compiler_params=pltpu.CompilerParams(vmem_limit_bytes=VMEM_LIMIT),
    )(x_shard)


def _hbm_specs(n):
    return [pl.BlockSpec(memory_space=pl.ANY)] * n


class _GatherComm:
    def __init__(self, shards):
        self.n = n = len(shards)
        self.inputs = [s.reshape(2, s.shape[0] // 2, s.shape[1]) for s in shards]
        self.out_shape = [jax.ShapeDtypeStruct((N_DEV,) + s.shape[1:], s.dtype) for s in self.inputs]
        self.scratch = [pltpu.SemaphoreType.DMA((7 * n,)), pltpu.SemaphoreType.DMA((7 * n,))]

    def _parts(self, xs, outs, sems):
        send_sems, recv_sems = sems
        x, y, c = _my_place()

        def blk(k, px, py, pc):
            return outs[k].at[4 * px + 2 * py + pc]

        def copy(k, kind, block, to, own=False):
            return pltpu.make_async_remote_copy(
                src_ref=xs[k].at[c] if own else blk(k, *block), dst_ref=blk(k, *block),
                send_sem=send_sems.at[7 * k + kind], recv_sem=recv_sems.at[7 * k + kind],
                device_id=to, device_id_type=MESH)

        def whole(k):
            return pltpu.make_async_remote_copy(
                src_ref=xs[k], dst_ref=outs[k].at[pl.ds(4 * x + 2 * y, 2)],
                send_sem=send_sems.at[7 * k], recv_sem=recv_sems.at[7 * k],
                device_id=(x, y, 1 - c), device_id_type=MESH)

        me, sibling = (x, y, c), (x, y, 1 - c)
        chips = _other_chips(x, y)
        first = []
        for k in range(self.n):
            first.append(whole(k))
            first += [copy(k, 1 + j, me, (*chip, c), own=True) for j, chip in enumerate(chips)]
        return copy, whole, me, sibling, chips, c, first

    def start(self, xs, outs, sems):
        for cp in self._parts(xs, outs, sems)[-1]:
            cp.start()

    def finish(self, xs, outs, sems):
        copy, whole, me, sibling, chips, c, first = self._parts(xs, outs, sems)
        passed = []
        for j, chip in enumerate(chips):
            for k in range(self.n):
                copy(k, 1 + j, (*chip, c), me).wait_recv()
                fwd = copy(k, 4 + j, (*chip, c), sibling)
                fwd.start()
                passed.append(fwd)
        for k in range(self.n):
            whole(k).wait_recv()
        for j, chip in enumerate(chips):
            for k in range(self.n):
                copy(k, 4 + j, (*chip, 1 - c), me).wait_recv()
        for cp in first + passed:
            cp.wait_send()


class _ToChipsComm:
    def __init__(self, a4s):
        self.inputs = list(a4s)
        self.n = n = len(a4s)
        nc = N_CHIP - 1
        self.out_shape = [jax.ShapeDtypeStruct((nc,) + a.shape[1:], a.dtype) for a in a4s]
        self.scratch = [pltpu.SemaphoreType.DMA((nc * n,)), pltpu.SemaphoreType.DMA((nc * n,))]

    def _copies(self, as_, rs, sems):
        send_sems, recv_sems = sems
        x, y, c = _my_place()
        nc = N_CHIP - 1
        return [pltpu.make_async_remote_copy(
            src_ref=as_[k].at[2 * cx + cy], dst_ref=rs[k].at[j], send_sem=send_sems.at[nc * k + j],
            recv_sem=recv_sems.at[nc * k + j], device_id=(cx, cy, c), device_id_type=MESH)
            for k in range(self.n) for j, (cx, cy) in enumerate(_other_chips(x, y))]

    def start(self, as_, rs, sems):
        for cp in self._copies(as_, rs, sems):
            cp.start()

    def finish(self, as_, rs, sems):
        for cp in self._copies(as_, rs, sems):
            cp.wait()


def _run_comm(comm, name):
    n = comm.n

    def body(*refs):
        ins, outs, sems = refs[:n], refs[n:2 * n], refs[2 * n:]
        comm.start(ins, outs, sems)
        comm.finish(ins, outs, sems)

    return pl.pallas_call(
        body, name=name, out_shape=comm.out_shape, in_specs=_hbm_specs(n), out_specs=_hbm_specs(n),
        scratch_shapes=comm.scratch,
    )(*comm.inputs)


def _gather_weights(shards, name):
    return _run_comm(_GatherComm(shards), name)


class _ToSiblingComm:
    def __init__(self, g8s):
        self.inputs = list(g8s)
        self.n = n = len(g8s)
        self.out_shape = [jax.ShapeDtypeStruct((N_CHIP,) + g.shape[1:], g.dtype) for g in g8s]
        self.scratch = [pltpu.SemaphoreType.DMA((N_CHIP * n,)), pltpu.SemaphoreType.DMA((N_CHIP * n,))]

    def _copies(self, gs, rs, sems):
        send_sems, recv_sems = sems
        x, y, c = _my_place()
        return [pltpu.make_async_remote_copy(
            src_ref=gs[k].at[2 * s + 1 - c], dst_ref=rs[k].at[s], send_sem=send_sems.at[N_CHIP * k + s],
            recv_sem=recv_sems.at[N_CHIP * k + s], device_id=(x, y, 1 - c), device_id_type=MESH)
            for k in range(self.n) for s in range(N_CHIP)]

    def start(self, gs, rs, sems):
        for cp in self._copies(gs, rs, sems):
            cp.start()

    def finish(self, gs, rs, sems):
        for cp in self._copies(gs, rs, sems):
            cp.wait()


def _rs_to_chips(a4s, name):
    return _run_comm(_ToChipsComm(a4s), name)


def _swap_halves(hs, name):
    n = len(hs)

    def body(*refs):
        o_refs = refs[n:2 * n]
        send_sems, recv_sems = refs[2 * n:]
        x, y, c = _my_place()

        def remote(k, slot):
            return pltpu.make_async_remote_copy(
                src_ref=o_refs[k].at[slot], dst_ref=o_refs[k].at[slot], send_sem=send_sems.at[k],
                recv_sem=recv_sems.at[k], device_id=(x, y, 1 - c), device_id_type=MESH)

        sends = [remote(k, c) for k in range(n)]
        for cp in sends:
            cp.start()
        for k in range(n):
            remote(k, 1 - c).wait_recv()
        for cp in sends:
            cp.wait_send()

    return pl.pallas_call(
        body, name=name,
        out_shape=[jax.ShapeDtypeStruct(h.shape, h.dtype) for h in hs],
        in_specs=_hbm_specs(n), out_specs=_hbm_specs(n),
        input_output_aliases={k: k for k in range(n)},
        scratch_shapes=[pltpu.SemaphoreType.DMA((n,)), pltpu.SemaphoreType.DMA((n,))],
    )(*hs)


ADD_TILES = 2


def _add_blocks(a_list, a_idx_fn, others_list, ns, sel, name, out_blocks=None, out_idx_fn=None,
                bf16_copy=False):
    out_blocks = out_blocks or ns
    out_idx_fn = out_idx_fn or (lambda s, sel_ref: s)
    n = len(a_list)
    n_o = len(others_list[0])
    per = 1 + n_o

    def body(sel_ref, *refs):
        for k in range(n):
            ins = refs[k * per:(k + 1) * per]
            acc = ins[0][0]
            for r in ins[1:]:
                acc = acc + r[0].astype(F32)
            refs[n * per + k][0] = acc
            if bf16_copy:
                refs[n * per + n + k][0] = acc.astype(BF16)

    in_specs, args, out_specs, out_shape = [], [], [], []
    for a, others in zip(a_list, others_list):
        _, R, N = a.shape
        tr = R // ADD_TILES
        assert tr % 8 == 0, a.shape
        in_specs.append(pl.BlockSpec((1, tr, N), lambda s, i, sel_ref: (a_idx_fn(s, sel_ref), i, 0)))
        args.append(a)
        for arr, fixed in others:
            if fixed is None:
                in_specs.append(pl.BlockSpec((1, tr, N), lambda s, i, sel_ref: (s, i, 0)))
            else:
                in_specs.append(pl.BlockSpec((1, tr, N), lambda s, i, sel_ref, fixed=fixed: (fixed, i, 0)))
            args.append(arr)
        out_specs.append(pl.BlockSpec((1, tr, N), lambda s, i, sel_ref: (out_idx_fn(s, sel_ref), i, 0)))
        out_shape.append(jax.ShapeDtypeStruct((out_blocks, R, N), a.dtype))
    if bf16_copy:
        out_specs = out_specs + out_specs
        out_shape = out_shape + [jax.ShapeDtypeStruct(o.shape, BF16) for o in out_shape]
    grid_spec = pltpu.PrefetchScalarGridSpec(num_scalar_prefetch=1, grid=(ns, ADD_TILES), in_specs=in_specs,
                                             out_specs=out_specs)
    return pl.pallas_call(
        body, name=name, out_shape=out_shape, grid_spec=grid_spec,
        compiler_params=_cparams(("parallel", "parallel")),
    )(sel, *args)


def _rs_first(g8s, tag, r1=None):
    c_sel = jnp.reshape(lax.axis_index("c"), (1,)).astype(jnp.int32)
    if r1 is None:
        r1 = _run_comm(_ToSiblingComm(g8s), f"rs_to_sibling_{tag}")
    res = _add_blocks(g8s, lambda s, sel: 2 * s + sel[0], [[(r, None)] for r in r1], N_CHIP, c_sel,
                      f"rs_add_sibling_{tag}", bf16_copy=True)
    return list(res[:len(g8s)]), list(res[len(g8s):])


def _rs_last(a4s, r2s, tag):
    sel = jnp.stack([2 * lax.axis_index("x") + lax.axis_index("y"), lax.axis_index("c")]).astype(jnp.int32)
    h = _add_blocks(a4s, lambda s, sel: sel[0], [[(r, 0), (r, 1), (r, 2)] for r in r2s], 1, sel,
                    f"rs_add_chips_{tag}", out_blocks=2, out_idx_fn=lambda s, sel: sel[1])
    full = _swap_halves(h, f"rs_swap_halves_{tag}")
    return [f.reshape(2 * f.shape[1], f.shape[2]) for f in full]


def _reduce_scatter(g8s, tag):
    a4, a4_bf16 = _rs_first(g8s, tag)
    return _rs_last(a4, _rs_to_chips(a4_bf16, f"rs_to_chips_{tag}"), tag)


def _ada_fwd(c_all, w_ada, b_ada, name):
    nb, D = c_all.shape
    ncol = w_ada.shape[1]
    tc = 512

    def body(c_ref, w_ref, b_ref, o_ref):
        cv = c_ref[...]
        cond = (cv * jax.nn.sigmoid(cv)).astype(BF16)
        o_ref[...] = jnp.dot(cond, w_ref[...].astype(BF16), preferred_element_type=F32) + b_ref[...]

    return pl.pallas_call(
        body, name=name, out_shape=jax.ShapeDtypeStruct((nb, ncol), F32), grid=(ncol // tc,),
        in_specs=[pl.BlockSpec((nb, D), lambda j: (0, 0)), pl.BlockSpec((D, tc), lambda j: (0, j)),
                  pl.BlockSpec((1, tc), lambda j: (0, j))],
        out_specs=pl.BlockSpec((nb, tc), lambda j: (0, j)),
        compiler_params=_cparams(("parallel",)),
    )(c_all, w_ada, b_ada)


def _ada_bwd(c_all, gmod_cols, name):
    nb, D = c_all.shape
    ncol = gmod_cols.shape[1]
    tc = 512

    def body(c_ref, g_ref, o_ref):
        cv = c_ref[...]
        cond = (cv * jax.nn.sigmoid(cv)).astype(BF16)
        o_ref[...] = _dot_tn(cond, g_ref[...].astype(BF16))

    return pl.pallas_call(
        body, name=name, out_shape=jax.ShapeDtypeStruct((D, ncol), F32), grid=(ncol // tc,),
        in_specs=[pl.BlockSpec((nb, D), lambda j: (0, 0)), pl.BlockSpec((nb, tc), lambda j: (0, j))],
        out_specs=pl.BlockSpec((D, tc), lambda j: (0, j)),
        compiler_params=_cparams(("parallel",)),
    )(c_all, gmod_cols)


def _adam_math(w, g, m, v):
    m = ADAM_B1 * m + (1.0 - ADAM_B1) * g
    v = ADAM_B2 * v + (1.0 - ADAM_B2) * (g * g)
    m_hat = m / (1.0 - ADAM_B1 ** ADAM_STEP)
    v_hat = v / (1.0 - ADAM_B2 ** ADAM_STEP)
    delta = -ADAM_LR * (m_hat / (jnp.sqrt(v_hat) + ADAM_EPS) + ADAM_WD * w)
    return delta, m, v


def _adamw(w, g, m, v, name):
    rows, cols = w.shape
    tr = _pick(rows, (256, 192, 176, 128, 64, 8))

    def body(w_ref, g_ref, m_ref, v_ref, d_ref, mo_ref, vo_ref):
        d, mn, vn = _adam_math(w_ref[...], g_ref[...], m_ref[...], v_ref[...])
        d_ref[...] = d
        mo_ref[...] = mn
        vo_ref[...] = vn

    spec = pl.BlockSpec((tr, cols), lambda i: (i, 0))
    return pl.pallas_call(
        body, name=name, out_shape=[jax.ShapeDtypeStruct((rows, cols), F32)] * 3, grid=(rows // tr,),
        in_specs=[spec] * 4, out_specs=[spec] * 3, compiler_params=_cparams(("parallel",)),
    )(w, g, m, v)


VEC_ROWS = 8


def _adamw_rows(w, parts, m, v, name):
    n = w.shape[1]
    P = parts.shape[0]
    assert n % (VEC_ROWS * LANES) == 0, n
    shp = (VEC_ROWS, n // VEC_ROWS)

    def body(w_ref, p_ref, m_ref, v_ref, g_ref, d_ref, mo_ref, vo_ref):
        g = p_ref[0]
        for k in range(1, P):
            g = g + p_ref[k]
        d, mn, vn = _adam_math(w_ref[...], g, m_ref[...], v_ref[...])
        g_ref[...] = g
        d_ref[...] = d
        mo_ref[...] = mn
        vo_ref[...] = vn

    vec = pl.BlockSpec(shp, lambda i: (0, 0))
    out = pl.pallas_call(
        body, name=name, out_shape=[jax.ShapeDtypeStruct(shp, F32)] * 4, grid=(1,),
        in_specs=[vec, pl.BlockSpec((P,) + shp, lambda i: (0, 0, 0)), vec, vec], out_specs=[vec] * 4,
        compiler_params=_cparams(("arbitrary",)),
    )(w.reshape(shp), parts.reshape((P,) + shp), m.reshape(shp), v.reshape(shp))
    return [o.reshape(1, n) for o in out]


_SHARDED = ("w_in", "w_uq", "w_ukv", "w_out", "w_ffn_in", "w_ffn_out")
_SMALL = (("g_norm1", 1024), ("g_cq", 384), ("g_ckv", 256), ("rel_bias", 256), ("g_out_a", 512),
          ("g_out_b", 512), ("g_norm2", 1024), ("g_final", 1024))
_SMALL_PAD = 5120


def _full_from_shards(sh):
    return jnp.transpose(sh, (1, 0, 2)).reshape(sh.shape[1], -1)


def _shards_from_full(full):
    rows, cols = full.shape
    return jnp.transpose(full.reshape(rows, N_CHIP, cols // N_CHIP), (1, 0, 2))


def kernel(x, c, w_ada, b_ada, g_norm1, w_in, g_cq, w_uq, g_ckv, w_ukv, rel_bias, g_out_a, g_out_b, w_out, g_norm2, w_ffn_in, w_ffn_out, g_final, loss_target, m_w_ada, m_b_ada, m_g_norm1, m_w_in, m_g_cq, m_w_uq, m_g_ckv, m_w_ukv, m_rel_bias, m_g_out_a, m_g_out_b, m_w_out, m_g_norm2, m_w_ffn_in, m_w_ffn_out, m_g_final, v_w_ada, v_b_ada, v_g_norm1, v_w_in, v_g_cq, v_w_uq, v_g_ckv, v_w_ukv, v_rel_bias, v_g_out_a, v_g_out_b, v_w_out, v_g_norm2, v_w_ffn_in, v_w_ffn_out, v_g_final):
    names = ["w_ada", "b_ada", "g_norm1", "w_in", "g_cq", "w_uq", "g_ckv", "w_ukv", "rel_bias", "g_out_a",
             "g_out_b", "w_out", "g_norm2", "w_ffn_in", "w_ffn_out", "g_final"]
    W = dict(zip(names, [w_ada, b_ada, g_norm1, w_in, g_cq, w_uq, g_ckv, w_ukv, rel_bias, g_out_a, g_out_b,
                         w_out, g_norm2, w_ffn_in, w_ffn_out, g_final]))
    M = dict(zip(names, [m_w_ada, m_b_ada, m_g_norm1, m_w_in, m_g_cq, m_w_uq, m_g_ckv, m_w_ukv, m_rel_bias,
                         m_g_out_a, m_g_out_b, m_w_out, m_g_norm2, m_w_ffn_in, m_w_ffn_out, m_g_final]))
    V = dict(zip(names, [v_w_ada, v_b_ada, v_g_norm1, v_w_in, v_g_cq, v_w_uq, v_g_ckv, v_w_ukv, v_rel_bias,
                         v_g_out_a, v_g_out_b, v_w_out, v_g_norm2, v_w_ffn_in, v_w_ffn_out, v_g_final]))
    B, S, D = x.shape
    mx, my, mc = _my_place()
    dev = 4 * mx + 2 * my + mc
    chip = 2 * mx + my
    pad_rows = 8

    c_all = _allgather8(jnp.pad(c, ((0, pad_rows - B), (0, 0))), "ag_c", False)
    c_all = c_all.reshape(N_DEV, pad_rows, D)[:, :B].reshape(N_DEV * B, D)
    ada_cols = w_ada.shape[-1]
    b_cols = lax.dynamic_slice_in_dim(b_ada, chip * ada_cols, ada_cols, axis=1)
    mod_cols = _ada_fwd(c_all, w_ada[0], b_cols, "ada_fwd")
    mod_all = _allgather8(mod_cols, "ag_mod", False).reshape(N_DEV, N_DEV * B, ada_cols)[0::2]
    mod_all = jnp.transpose(mod_all, (1, 0, 2)).reshape(N_DEV * B, N_MOD * D)
    mod = lax.dynamic_slice_in_dim(mod_all, dev * B, B, axis=0)

    early = ("w_in", "w_uq", "w_ukv")
    bias, got = _bias_tables(rel_bias, _bucket_tables(), "rel_bias_tables",
                             _GatherComm([W[n][0].astype(BF16) for n in early]))
    full = {n: g.reshape((N_CHIP,) + W[n].shape[1:]) for n, g in zip(early, got)}
    wts = dict(w_in=_w_in_to_kernel(_full_from_shards(full["w_in"])),
               w_uq=_w_uq_to_kernel(_full_from_shards(full["w_uq"])),
               w_kv=_w_ukv_to_kernel(_full_from_shards(full["w_ukv"])))
    gains = dict(g_norm1=g_norm1, g_cq=g_cq, g_ckv=g_ckv, g_out_a=g_out_a, g_out_b=g_out_b, g_norm2=g_norm2,
                 g_final=g_final.reshape(1, D))

    loss, grad_x, gmod, grads = _local_step(x, loss_target, mod, wts, gains, rel_bias,
                                            ffn_shards=[w_ffn_in[0].astype(BF16), w_ffn_out[0].astype(BF16),
                                                        w_out[0].astype(BF16)], bias=bias)
    loss = lax.psum(loss[0, 0], ("x", "y", "c"))

    n_small = _SMALL_PAD
    cat = lambda dct: jnp.concatenate([dct[n].reshape(1, -1) for n, _ in _SMALL]
                                      + [jnp.zeros((1, _SMALL_PAD - sum(s for _, s in _SMALL)), F32)], axis=1)
    small = cat(grads)
    rows = jnp.concatenate([gmod, jnp.pad(small, ((0, 0), (0, N_MOD * D - n_small))),
                            jnp.zeros((pad_rows - B - 1, N_MOD * D), F32)], axis=0)
    rows_all = _allgather8(rows, "ag_small", False).reshape(N_DEV, pad_rows, N_MOD * D)
    gmod_all = rows_all[:, :B].reshape(N_DEV * B, N_MOD * D)
    small_parts = rows_all[:, B, :n_small]

    a4, r2 = grads["mix_pending"]
    ffn_a4, ffn_r2 = grads["ffn_pending"]
    G = dict(zip(_SHARDED, _rs_last(list(a4) + list(ffn_a4), list(r2) + list(ffn_r2), "all")))

    gmod_cols = lax.dynamic_slice_in_dim(gmod_all, chip * ada_cols, ada_cols, axis=1)
    G["w_ada"] = _ada_bwd(c_all, gmod_cols, "ada_bwd")
    delta, new_m, new_v = {}, {}, {}
    for n in ("w_ada",) + _SHARDED:
        shp = W[n].shape
        w2 = W[n].reshape(shp[-2], shp[-1])
        d_, m_, v_ = _adamw(w2, G[n], M[n].reshape(w2.shape), V[n].reshape(w2.shape), f"adamw_{n}")
        G[n], delta[n], new_m[n], new_v[n] = [a.reshape(shp) for a in (G[n], d_, m_, v_)]
    gs, ds_, ms_, vs_ = _adamw_rows(cat(W), small_parts, cat(M), cat(V), "adamw_small")
    off = 0
    for n, sz in _SMALL:
        shp = W[n].shape
        G[n], delta[n], new_m[n], new_v[n] = [a[:, off:off + sz].reshape(shp) for a in (gs, ds_, ms_, vs_)]
        off += sz
    G["b_ada"], delta["b_ada"], new_m["b_ada"], new_v["b_ada"] = _adamw_rows(b_ada, gmod_all, m_b_ada, v_b_ada,
                                                                          "adamw_b_ada")
    return (loss, grad_x, *[G[n] for n in names], *[delta[n] for n in names], *[new_m[n] for n in names],
            *[new_v[n] for n in names])
```

```python
import functools
import math

import numpy as np
import jax
import jax.numpy as jnp
from jax import lax
from jax.experimental import pallas as pl
from jax.experimental.pallas import tpu as pltpu

F32 = jnp.float32
BF16 = jnp.bfloat16

D_MODEL = 1024
SEQ = 2048
N_HEADS = 8
HEAD_DIM = 64
D_A = 512
D_B = 512
Q_LORA = 384
KV_LORA = 256
ROPE_DIM = 32
NOPE_DIM = 64
D_FF = 2816
N_MOD = 6
N_BUCKETS = 32
MAX_DISTANCE = 2048
ROPE_THETA = 10000.0
EPS = 1e-6
NEG = -1e30
BLK = 128
DILATIONS = (1, 4, 16)
SPAN = 128
MLA_SCALE = (NOPE_DIM + ROPE_DIM) ** -0.5
DIL_SCALE = HEAD_DIM ** -0.5

ADAM_LR = 0.001
ADAM_B1 = 0.9
ADAM_B2 = 0.999
ADAM_EPS = 1e-08
ADAM_WD = 0.01
ADAM_STEP = 10

N_DEV = 8
N_CHIP = 4
LANES = 128
VMEM_LIMIT = 48 * 1024 * 1024
MM_VMEM_BUDGET = 32 * 1024 * 1024

P_QKV = 3 * D_A
P_REST = KV_LORA + LANES + Q_LORA


def _cparams(sem=None):
    return pltpu.CompilerParams(dimension_semantics=sem, vmem_limit_bytes=VMEM_LIMIT)


def _pick(n, cands):
    for c in cands:
        if n % c == 0:
            return c
    raise ValueError(f"no tile for {n} in {cands}")


def _mm(a, b, mode, out_dtype, name, col_blocks=None, comm=None, halves=False):
    blocked = col_blocks is not None
    if mode == "nn":
        (M, K) = a.shape
        K2, N = (b.shape[1], b.shape[0] * b.shape[2]) if blocked else b.shape
    elif mode == "nt":
        (M, K) = (a.shape[1], 2 * a.shape[2]) if halves else a.shape
        N, K2 = (b.shape[1], b.shape[0] * b.shape[2]) if blocked else b.shape
    else:
        (K, M) = a.shape
        K2, N = (b.shape[1], 2 * b.shape[2]) if halves else b.shape
    assert K == K2, (a.shape, b.shape, mode)
    assert not halves or (blocked and col_blocks == 4 and mode in ("nt", "tn"))
    tn = _pick(N, (1408, 1024, 768, 512, 384, 256, 128))
    tk = _pick(K, (1408, 1152, 1024, 768, 512, 384, 256, 128))
    if blocked and mode == "nt":
        tk = K // col_blocks
    elif blocked:
        tn = N // col_blocks
    nk = K // tk

    def vmem_bytes(tm_):
        tiles = tm_ * tk * a.dtype.itemsize + tk * tn * b.dtype.itemsize + tm_ * tn * jnp.dtype(out_dtype).itemsize
        return 2 * tiles + tm_ * tn * 4

    tm = next(t for t in (1408, 1024, 512, 384, 256, 128) if M % t == 0 and vmem_bytes(t) <= MM_VMEM_BUDGET)
    out_shape = (M, N)
    out_spec = pl.BlockSpec((tm, tn), lambda i, j, k: (i, j))
    if mode == "nn":
        a_spec = pl.BlockSpec((tm, tk), lambda i, j, k: (i, k))
        b_spec = (pl.BlockSpec((None, tk, tn), lambda i, j, k: (j, k, 0)) if blocked
                  else pl.BlockSpec((tk, tn), lambda i, j, k: (k, j)))
        dn = (((1,), (0,)), ((), ()))
    elif mode == "nt":
        a_spec = (pl.BlockSpec((None, tm, tk), lambda i, j, k: (k // 2, i, k % 2)) if halves
                  else pl.BlockSpec((tm, tk), lambda i, j, k: (i, k)))
        b_spec = (pl.BlockSpec((None, tn, tk), lambda i, j, k: (k, j, 0)) if blocked
                  else pl.BlockSpec((tn, tk), lambda i, j, k: (j, k)))
        dn = (((1,), (1,)), ((), ()))
    else:
        a_spec = pl.BlockSpec((tk, tm), lambda i, j, k: (k, i))
        b_spec = (pl.BlockSpec((None, tk, tn), lambda i, j, k: (j // 2, k, j % 2)) if halves
                  else pl.BlockSpec((tk, tn), lambda i, j, k: (k, j)))
        dn = (((0,), (0,)), ((), ()))
        if blocked:
            out_shape = (col_blocks, M, tn)
            out_spec = pl.BlockSpec((None, tm, tn), lambda i, j, k: (j, i, 0))

    def body(a_ref, b_ref, o_ref, acc_ref):
        k = pl.program_id(2)

        @pl.when(k == 0)
        def _():
            acc_ref[...] = jnp.zeros_like(acc_ref)

        acc_ref[...] += lax.dot_general(a_ref[...].astype(BF16), b_ref[...].astype(BF16), dn,
                                        preferred_element_type=F32)

        @pl.when(k == nk - 1)
        def _():
            o_ref[...] = acc_ref[...].astype(o_ref.dtype)

    if comm is not None:
        (out,), got = _host_call(
            body, comm, name=name, out_shape=[jax.ShapeDtypeStruct(out_shape, out_dtype)],
            grid=(M // tm, N // tn, nk), in_specs=[a_spec, b_spec], out_specs=[out_spec],
            scratch_shapes=[pltpu.VMEM((tm, tn), F32)], args=(a, b))
        return out, got
    return pl.pallas_call(
        body, name=name,
        out_shape=jax.ShapeDtypeStruct(out_shape, out_dtype),
        grid=(M // tm, N // tn, nk),
        in_specs=[a_spec, b_spec],
        out_specs=out_spec,
        scratch_shapes=[pltpu.VMEM((tm, tn), F32)],
        compiler_params=_cparams(("parallel", "parallel", "arbitrary")),
    )(a, b)


ROW_TILE = 512


def _adaln_fwd(x, g, sc, sh, name, mix=None, gate=None):
    B, S, D = x.shape
    ts = ROW_TILE
    has_res = mix is not None

    def body(*refs):
        if has_res:
            x_ref, g_ref, sc_ref, sh_ref, mix_ref, gate_ref, h_ref, xr_ref = refs
            xr = x_ref[0] + gate_ref[0] * mix_ref[0]
            xr_ref[0] = xr
        else:
            x_ref, g_ref, sc_ref, sh_ref, h_ref = refs
            xr = x_ref[0]
        r = lax.rsqrt(jnp.mean(xr * xr, axis=-1, keepdims=True) + EPS)
        xn = (xr * r) * g_ref[...]
        h_ref[0] = (xn * (1.0 + sc_ref[0]) + sh_ref[0]).astype(h_ref.dtype)

    tok = pl.BlockSpec((1, ts, D), lambda b, s: (b, s, 0))
    per_b = pl.BlockSpec((1, 1, D), lambda b, s: (b, 0, 0))
    vec = pl.BlockSpec((1, D), lambda b, s: (0, 0))
    in_specs = [tok, vec, per_b, per_b]
    args = [x, g, sc, sh]
    out_shape = [jax.ShapeDtypeStruct((B, S, D), BF16)]
    out_specs = [tok]
    if has_res:
        in_specs += [tok, per_b]
        args += [mix, gate]
        out_shape.append(jax.ShapeDtypeStruct((B, S, D), F32))
        out_specs.append(tok)
    out = pl.pallas_call(
        body, name=name, out_shape=out_shape, grid=(B, S // ts),
        in_specs=in_specs, out_specs=out_specs,
        compiler_params=_cparams(("parallel", "parallel")),
    )(*args)
    return out if has_res else out[0]


def _adaln_bwd(dh, x, g, sc, dres, name, mix=None, gate=None, comm=None):
    B, S, D = x.shape
    ts = ROW_TILE
    has_res = mix is not None

    def body(*refs):
        if has_res:
            (dh_ref, x_ref, g_ref, sc_ref, dres_ref, mix_ref, gate_ref,
             dx_ref, dsh_ref, dsc_ref, dg_ref, dgate_ref, dmix_ref) = refs
        else:
            (dh_ref, x_ref, g_ref, sc_ref, dres_ref, dx_ref, dsh_ref, dsc_ref, dg_ref) = refs
        b, s = pl.program_id(0), pl.program_id(1)
        xv = x_ref[0]
        dhv = dh_ref[0]
        gv = g_ref[...]
        r = lax.rsqrt(jnp.mean(xv * xv, axis=-1, keepdims=True) + EPS)
        n = xv * r
        xn = n * gv
        dxn = dhv * (1.0 + sc_ref[0])
        dn = dxn * gv
        dx = r * (dn - n * jnp.mean(dn * n, axis=-1, keepdims=True)) + dres_ref[0]
        dx_ref[0] = dx

        @pl.when(s == 0)
        def _():
            dsh_ref[...] = jnp.zeros_like(dsh_ref)
            dsc_ref[...] = jnp.zeros_like(dsc_ref)
            if has_res:
                dgate_ref[...] = jnp.zeros_like(dgate_ref)

        @pl.when((s == 0) & (b == 0))
        def _():
            dg_ref[...] = jnp.zeros_like(dg_ref)

        dsh_ref[0] += jnp.sum(dhv, axis=0, keepdims=True)
        dsc_ref[0] += jnp.sum(dhv * xn, axis=0, keepdims=True)
        dg_ref[...] += jnp.sum(dxn * n, axis=0, keepdims=True)
        if has_res:
            dgate_ref[0] += jnp.sum(dx * mix_ref[0], axis=0, keepdims=True)
            dmix_ref[0] = (dx * gate_ref[0]).astype(dmix_ref.dtype)

    tok = pl.BlockSpec((1, ts, D), lambda b, s: (b, s, 0))
    per_b = pl.BlockSpec((1, 1, D), lambda b, s: (b, 0, 0))
    vec = pl.BlockSpec((1, D), lambda b, s: (0, 0))
    in_specs = [tok, tok, vec, per_b, tok]
    args = [dh, x, g, sc, dres]
    out_shape = [jax.ShapeDtypeStruct((B, S, D), F32), jax.ShapeDtypeStruct((B, 1, D), F32),
                 jax.ShapeDtypeStruct((B, 1, D), F32), jax.ShapeDtypeStruct((1, D), F32)]
    out_specs = [tok, per_b, per_b, vec]
    if has_res:
        in_specs += [tok, per_b]
        args += [mix, gate]
        out_shape += [jax.ShapeDtypeStruct((B, 1, D), F32), jax.ShapeDtypeStruct((B, S, D), BF16)]
        out_specs += [per_b, tok]
    res, got = _host_call(body, comm, name=name, out_shape=out_shape, grid=(B, S // ts), in_specs=in_specs,
                          out_specs=out_specs, scratch_shapes=[], args=args)
    return (list(res) + [got]) if comm is not None else res


def _rms_fwd(x, col_blk, n, g, name):
    T = x.shape[0]
    tr = 512

    def body(x_ref, g_ref, y_ref):
        xv = x_ref[...]
        r = lax.rsqrt(jnp.mean(xv * xv, axis=-1, keepdims=True) + EPS)
        y_ref[...] = ((xv * r) * g_ref[...]).astype(y_ref.dtype)

    return pl.pallas_call(
        body, name=name, out_shape=jax.ShapeDtypeStruct((T, n), BF16), grid=(T // tr,),
        in_specs=[pl.BlockSpec((tr, n), lambda i: (i, col_blk)), pl.BlockSpec((1, n), lambda i: (0, 0))],
        out_specs=pl.BlockSpec((tr, n), lambda i: (i, 0)),
        compiler_params=_cparams(("parallel",)),
    )(x, g)


def _rms_fwd_pair(xa, xb, ga, gb, name):
    T, na = xa.shape
    nb = xb.shape[1]
    tr = 512

    def body(xa_ref, xb_ref, ga_ref, gb_ref, y_ref):
        for x_ref, g_ref, lo, n in ((xa_ref, ga_ref, 0, na), (xb_ref, gb_ref, na, nb)):
            xv = x_ref[...]
            r = lax.rsqrt(jnp.mean(xv * xv, axis=-1, keepdims=True) + EPS)
            y_ref[:, lo:lo + n] = ((xv * r) * g_ref[...]).astype(y_ref.dtype)

    row = lambda n: pl.BlockSpec((tr, n), lambda i: (i, 0))
    vec = lambda n: pl.BlockSpec((1, n), lambda i: (0, 0))
    return pl.pallas_call(
        body, name=name, out_shape=jax.ShapeDtypeStruct((T, na + nb), BF16), grid=(T // tr,),
        in_specs=[row(na), row(nb), vec(na), vec(nb)], out_specs=row(na + nb),
        compiler_params=_cparams(("parallel",)),
    )(xa, xb, ga, gb)


def _rms_bwd(dy, dy_blk, x, x_blk, n, g, name, out_dtype=BF16):
    T = x.shape[0]
    tr = 512

    def body(dy_ref, x_ref, g_ref, dx_ref, dg_ref):
        xv = x_ref[...]
        dyv = dy_ref[...].astype(F32)
        r = lax.rsqrt(jnp.mean(xv * xv, axis=-1, keepdims=True) + EPS)
        nrm = xv * r
        dn = dyv * g_ref[...]
        dx_ref[...] = (r * (dn - nrm * jnp.mean(dn * nrm, axis=-1, keepdims=True))).astype(dx_ref.dtype)

        @pl.when(pl.program_id(0) == 0)
        def _():
            dg_ref[...] = jnp.zeros_like(dg_ref)

        dg_ref[...] += jnp.sum(dyv * nrm, axis=0, keepdims=True)

    return pl.pallas_call(
        body, name=name,
        out_shape=[jax.ShapeDtypeStruct((T, n), out_dtype), jax.ShapeDtypeStruct((1, n), F32)],
        grid=(T // tr,),
        in_specs=[pl.BlockSpec((tr, n), lambda i: (i, dy_blk)), pl.BlockSpec((tr, n), lambda i: (i, x_blk)),
                  pl.BlockSpec((1, n), lambda i: (0, 0))],
        out_specs=[pl.BlockSpec((tr, n), lambda i: (i, 0)), pl.BlockSpec((1, n), lambda i: (0, 0))],
        compiler_params=_cparams(("arbitrary",)),
    )(dy, x, g)


def _rms_bwd_views(dy, dy_blk, x, g, name):
    B, S, n = x.shape
    tiles = S // VIEW_TILE

    def body(dy_ref, x_ref, g_ref, d1_ref, d4_ref, d16_ref, dg_ref, dx_s):
        xv = x_ref[0]
        dyv = dy_ref[...]
        r = lax.rsqrt(jnp.mean(xv * xv, axis=-1, keepdims=True) + EPS)
        nrm = xv * r
        dn = dyv * g_ref[...]
        dx = r * (dn - nrm * jnp.mean(dn * nrm, axis=-1, keepdims=True))
        d1_ref[0] = dx.astype(d1_ref.dtype)
        _put_tile(dx_s, dx)
        _tile_to_view(dx_s, d4_ref, DILATIONS[1], n)
        _tile_to_view(dx_s, d16_ref, DILATIONS[2], n)

        @pl.when((pl.program_id(0) == 0) & (pl.program_id(1) == 0))
        def _():
            dg_ref[...] = jnp.zeros_like(dg_ref)

        dg_ref[...] += jnp.sum(dyv * nrm, axis=0, keepdims=True)

    res = pl.pallas_call(
        body, name=name,
        out_shape=[_view_shape(B, S, d, n, BF16) for d in DILATIONS] + [jax.ShapeDtypeStruct((1, n), F32)],
        grid=(B, tiles),
        in_specs=[pl.BlockSpec((VIEW_TILE, n), lambda b, t: (b * tiles + t, dy_blk)), _view_spec(1, n),
                  pl.BlockSpec((1, n), lambda b, t: (0, 0))],
        out_specs=[_view_spec(d, n) for d in DILATIONS] + [pl.BlockSpec((1, n), lambda b, t: (0, 0))],
        scratch_shapes=[_tile_scratch(n)],
        compiler_params=_cparams(("arbitrary", "arbitrary")),
    )(dy, x, g)
    return res[:len(DILATIONS)], res[len(DILATIONS)]


FFN_TILE = 1408


def _ffn_in_fwd(h, w4, name):
    T, D = h.shape
    tm, tc = 512, FFN_TILE
    nc = D_FF // tc

    def body(h_ref, wg_ref, wu_ref, gu_ref, act_ref):
        hv = h_ref[...]
        g = jnp.dot(hv, wg_ref[...], preferred_element_type=F32)
        u = jnp.dot(hv, wu_ref[...], preferred_element_type=F32)
        gu_ref[0] = g
        gu_ref[1] = u
        act_ref[...] = (g * jax.nn.sigmoid(g) * u).astype(act_ref.dtype)

    return pl.pallas_call(
        body, name=name,
        out_shape=[jax.ShapeDtypeStruct((2, T, D_FF), F32), jax.ShapeDtypeStruct((T, D_FF), BF16)],
        grid=(nc, T // tm),
        in_specs=[pl.BlockSpec((tm, D), lambda j, i: (i, 0)),
                  pl.BlockSpec((None, D, tc), lambda j, i: (j, 0, 0)),
                  pl.BlockSpec((None, D, tc), lambda j, i: (j + nc, 0, 0))],
        out_specs=[pl.BlockSpec((2, tm, tc), lambda j, i: (0, i, j)), pl.BlockSpec((tm, tc), lambda j, i: (i, j))],
        compiler_params=_cparams(("parallel", "parallel")),
    )(h, w4, w4)


def _ffn_out_bwd(df, w_out, gu, name):
    T, D = df.shape
    tm, tc = 512, FFN_TILE

    def body(df_ref, w_ref, gu_ref, dgu_ref):
        da = _dot_nt(df_ref[...], w_ref[...])
        g, u = gu_ref[0], gu_ref[1]
        sg = jax.nn.sigmoid(g)
        dgu_ref[0] = (da * u * (sg * (1.0 + g * (1.0 - sg)))).astype(dgu_ref.dtype)
        dgu_ref[1] = (da * (g * sg)).astype(dgu_ref.dtype)

    halves = pl.BlockSpec((2, tm, tc), lambda j, i: (0, i, j))
    return pl.pallas_call(
        body, name=name, out_shape=jax.ShapeDtypeStruct((2, T, D_FF), BF16), grid=(D_FF // tc, T // tm),
        in_specs=[pl.BlockSpec((tm, D), lambda j, i: (i, 0)), pl.BlockSpec((tc, D), lambda j, i: (j, 0)), halves],
        out_specs=halves,
        compiler_params=_cparams(("parallel", "parallel")),
    )(df, w_out, gu)


def _final_loss(x1, f, g2, gf, target, name):
    B, S, D = x1.shape
    ts = ROW_TILE

    def body(x1_ref, f_ref, g2_ref, gf_ref, t_ref, dx_ref, df_ref, dg2_ref, dgf_ref, loss_ref):
        b, s = pl.program_id(0), pl.program_id(1)
        fv = f_ref[0]
        g2v = g2_ref[0]
        gfv = gf_ref[...]
        x2 = x1_ref[0] + g2v * fv
        r = lax.rsqrt(jnp.mean(x2 * x2, axis=-1, keepdims=True) + EPS)
        n = x2 * r
        e = n * gfv - t_ref[0]
        dy = e * (1.0 / D)
        dn = dy * gfv
        dx = r * (dn - n * jnp.mean(dn * n, axis=-1, keepdims=True))
        dx_ref[0] = dx
        df_ref[0] = (dx * g2v).astype(df_ref.dtype)

        @pl.when(s == 0)
        def _():
            dg2_ref[...] = jnp.zeros_like(dg2_ref)

        @pl.when((s == 0) & (b == 0))
        def _():
            dgf_ref[...] = jnp.zeros_like(dgf_ref)
            loss_ref[...] = jnp.zeros_like(loss_ref)

        dg2_ref[0] += jnp.sum(dx * fv, axis=0, keepdims=True)
        dgf_ref[...] += jnp.sum(dy * n, axis=0, keepdims=True)
        loss_ref[...] += 0.5 * jnp.sum(jnp.mean(e * e, axis=-1, keepdims=True), axis=0, keepdims=True)

    tok = pl.BlockSpec((1, ts, D), lambda b, s: (b, s, 0))
    per_b = pl.BlockSpec((1, 1, D), lambda b, s: (b, 0, 0))
    vec = pl.BlockSpec((1, D), lambda b, s: (0, 0))
    return pl.pallas_call(
        body, name=name,
        out_shape=[jax.ShapeDtypeStruct((B, S, D), F32), jax.ShapeDtypeStruct((B, S, D), BF16),
                   jax.ShapeDtypeStruct((B, 1, D), F32), jax.ShapeDtypeStruct((1, D), F32),
                   jax.ShapeDtypeStruct((1, LANES), F32)],
        grid=(B, S // ts),
        in_specs=[tok, tok, per_b, vec, tok],
        out_specs=[tok, tok, per_b, vec, pl.BlockSpec((1, LANES), lambda b, s: (0, 0))],
        compiler_params=_cparams(("arbitrary", "arbitrary")),
    )(x1, f, g2, gf, target)


def _rope_tables():
    half = ROPE_DIM // 2
    inv = ROPE_THETA ** (-jnp.arange(half, dtype=F32) / half)
    ang = jnp.arange(SEQ, dtype=F32)[:, None] * inv[None, :]
    cos, sin = jnp.cos(ang), jnp.sin(ang)
    one = jnp.ones((SEQ, NOPE_DIM), F32)
    zero = jnp.zeros((SEQ, NOPE_DIM), F32)
    cs = jnp.concatenate([one, cos, cos, one[:, :LANES - NOPE_DIM - ROPE_DIM]], axis=1)
    sn = jnp.concatenate([zero, -sin, sin, zero[:, :LANES - NOPE_DIM - ROPE_DIM]], axis=1)
    return cs, sn


def _rope_group(t, cs, sn):
    half = ROPE_DIM // 2
    lane = lax.broadcasted_iota(jnp.int32, t.shape, 1)
    partner = jnp.where(lane < NOPE_DIM + half, pltpu.roll(t, LANES - half, 1), pltpu.roll(t, half, 1))
    return t * cs + partner * sn


def _rope_apply(t, cs, sn, out_dtype, name, add=None, add_blk=0):
    B, S, W = t.shape
    G = W // LANES
    ts = ROW_TILE

    def body(*refs):
        if add is None:
            t_ref, cs_ref, sn_ref, o_ref = refs
            for gi in range(G):
                sl = slice(gi * LANES, (gi + 1) * LANES)
                o_ref[0, :, sl] = _rope_group(t_ref[0, :, sl], cs_ref[...], sn_ref[...]).astype(o_ref.dtype)
        else:
            t_ref, a_ref, cs_ref, sn_ref, o_ref = refs
            ra = _rope_group(a_ref[0], cs_ref[...], sn_ref[...])
            for gi in range(G):
                sl = slice(gi * LANES, (gi + 1) * LANES)
                o_ref[0, :, sl] = (t_ref[0, :, sl] + ra).astype(o_ref.dtype)

    tok = pl.BlockSpec((1, ts, W), lambda b, s: (b, s, 0))
    tab = pl.BlockSpec((ts, LANES), lambda b, s: (s, 0))
    in_specs, args = [tok], [t]
    if add is not None:
        in_specs.append(pl.BlockSpec((1, ts, LANES), lambda b, s: (b, s, add_blk)))
        args.append(add)
    in_specs += [tab, tab]
    args += [cs, sn]
    return pl.pallas_call(
        body, name=name, out_shape=jax.ShapeDtypeStruct((B, S, W), out_dtype), grid=(B, S // ts),
        in_specs=in_specs, out_specs=tok, compiler_params=_cparams(("parallel", "parallel")),
    )(*args)


def _qrope_bwd(dq_t, cs, sn_neg, name):
    B, npair, nq, _, tq = dq_t.shape

    def body(d_ref, cs_ref, sn_ref, o_ref):
        for p in range(npair):
            tile = jnp.transpose(d_ref[0, p, 0])
            for hh in range(2):
                lo = (2 * p + hh) * LANES
                o_ref[0, :, lo:lo + LANES] = _rope_group(tile[:, hh * LANES:(hh + 1) * LANES], cs_ref[...],
                                                         sn_ref[...]).astype(o_ref.dtype)

    tab = pl.BlockSpec((tq, LANES), lambda b, i: (i, 0))
    return pl.pallas_call(
        body, name=name, out_shape=jax.ShapeDtypeStruct((B, nq * tq, N_HEADS * LANES), BF16), grid=(B, nq),
        in_specs=[pl.BlockSpec((1, npair, 1, 2 * LANES, tq), lambda b, i: (b, 0, i, 0, 0)), tab, tab],
        out_specs=pl.BlockSpec((1, tq, N_HEADS * LANES), lambda b, i: (b, i, 0)),
        compiler_params=_cparams(("parallel", "parallel")),
    )(dq_t, cs, sn_neg)


def _krope_bwd(dkc, cs, sn_neg, name):
    B, S, W = dkc.shape
    G = W // LANES
    ts = ROW_TILE

    def body(d_ref, cs_ref, sn_ref, o_ref):
        acc = d_ref[0, :, 0:LANES]
        for gi in range(1, G):
            acc = acc + d_ref[0, :, gi * LANES:(gi + 1) * LANES]
        lane = lax.broadcasted_iota(jnp.int32, acc.shape, 1)
        rot = (lane >= NOPE_DIM) & (lane < NOPE_DIM + ROPE_DIM)
        acc = jnp.where(rot, acc, 0.0)
        o_ref[0] = _rope_group(acc, cs_ref[...], sn_ref[...]).astype(o_ref.dtype)

    tab = pl.BlockSpec((ts, LANES), lambda b, s: (s, 0))
    return pl.pallas_call(
        body, name=name, out_shape=jax.ShapeDtypeStruct((B, S, LANES), BF16), grid=(B, S // ts),
        in_specs=[pl.BlockSpec((1, ts, W), lambda b, s: (b, s, 0)), tab, tab],
        out_specs=pl.BlockSpec((1, ts, LANES), lambda b, s: (b, s, 0)),
        compiler_params=_cparams(("parallel", "parallel")),
    )(dkc, cs, sn_neg)


def _t5_bucket(dist):
    max_exact = N_BUCKETS // 2
    d = np.maximum(dist, 1).astype(np.float64)
    large = max_exact + (np.log(d / max_exact) / np.log(MAX_DISTANCE / max_exact)
                         * (N_BUCKETS - max_exact)).astype(np.int64)
    large = np.minimum(large, N_BUCKETS - 1)
    return np.where(dist < max_exact, dist, large).astype(np.int32)


def _band_buckets(dilation):
    a = np.arange(BLK)[None, :]
    bk = np.arange(2 * BLK)[:, None]
    steps = BLK + a - bk
    return _t5_bucket(np.clip(steps, 0, SPAN) * dilation)


def _head_mask(shape, hh):
    lane = lax.broadcasted_iota(jnp.int32, shape, 1)
    return (lane >= hh * HEAD_DIM) & (lane < (hh + 1) * HEAD_DIM)


def _dot_nt(a, b):
    return lax.dot_general(a, b, (((1,), (1,)), ((), ())), preferred_element_type=F32)


def _dot_tn(a, b):
    return lax.dot_general(a, b, (((0,), (0,)), ((), ())), preferred_element_type=F32)


def _dot_nn(a, b):
    return lax.dot_general(a, b, (((1,), (0,)), ((), ())), preferred_element_type=F32)


def _dil_fwd(qkv, bias, branch, dilation, name, comm=None):
    B, n, _ = qkv.shape
    d = dilation
    nb = n // BLK
    qkv_v = qkv
    npair = N_HEADS // 2

    def body(cur_ref, prev_ref, bias_ref, o_ref, lse_ref, s_scr, e_scr):
        first = jnp.where(pl.program_id(1) == 0, 1, 0)
        units = [(b, h) for b in range(B) for h in range(N_HEADS)]
        for b in range(B):
            for p in range(npair):
                q = cur_ref[b, :, p * LANES:(p + 1) * LANES] * DIL_SCALE
                kc = cur_ref[b, :, D_A + p * LANES:D_A + (p + 1) * LANES]
                kp = prev_ref[b, :, D_A + p * LANES:D_A + (p + 1) * LANES]
                for hh in range(2):
                    u = b * N_HEADS + 2 * p + hh
                    qm = jnp.where(_head_mask((BLK, LANES), hh), q, jnp.zeros_like(q))
                    s_scr[u, 0:BLK, :] = _dot_nt(kp, qm)
                    s_scr[u, BLK:2 * BLK, :] = _dot_nt(kc, qm)
        ms = []
        for u, (b, h) in enumerate(units):
            s_p = s_scr[u, 0:BLK, :] + bias_ref[first, h, 0:BLK, :]
            s_c = s_scr[u, BLK:2 * BLK, :] + bias_ref[first, h, BLK:2 * BLK, :]
            m = jnp.maximum(jnp.max(s_p, axis=0, keepdims=True), jnp.max(s_c, axis=0, keepdims=True))
            e_scr[u, 0:BLK, :] = jnp.exp(s_p - m).astype(BF16)
            e_scr[u, BLK:2 * BLK, :] = jnp.exp(s_c - m).astype(BF16)
            ms.append(m)
        rows0 = _row_mask((LANES, BLK), 0)
        for b in range(B):
            for p in range(npair):
                sl = slice(p * LANES, (p + 1) * LANES)
                vsl = slice(2 * D_A + p * LANES, 2 * D_A + (p + 1) * LANES)
                vct = jnp.transpose(cur_ref[b, :, vsl].astype(F32)).astype(BF16)
                vpt = jnp.transpose(prev_ref[b, :, vsl].astype(F32)).astype(BF16)
                acc = []
                for hh in range(2):
                    u = b * N_HEADS + 2 * p + hh
                    mine = _row_mask((LANES, BLK), hh)
                    one = jnp.ones_like(vct)
                    acc.append(_dot_nn(jnp.where(mine, vpt, one), e_scr[u, 0:BLK, :])
                               + _dot_nn(jnp.where(mine, vct, one), e_scr[u, BLK:2 * BLK, :]))
                l0 = acc[0][HEAD_DIM:HEAD_DIM + 1, :]
                l1 = acc[1][0:1, :]
                u0 = b * N_HEADS + 2 * p
                o_t = jnp.where(rows0, acc[0] / l0, acc[1] / l1)
                lse_t = jnp.where(rows0, ms[u0] + jnp.log(l0), ms[u0 + 1] + jnp.log(l1))
                o_ref[b, :, sl] = jnp.transpose(o_t)
                lse_ref[b, :, sl] = jnp.transpose(lse_t)

    cur = pl.BlockSpec((B, BLK, P_QKV), lambda r, i: (0, i, r))
    prev = pl.BlockSpec((B, BLK, P_QKV), lambda r, i: (0, jnp.maximum(i - 1, 0), r))
    out = pl.BlockSpec((B, BLK, D_A), lambda r, i: (0, i, r))
    return _host_call(
        body, comm, name=name,
        out_shape=[jax.ShapeDtypeStruct((B, n, d * D_A), F32)] * 2,
        grid=(d, nb),
        in_specs=[cur, prev,
                  pl.BlockSpec((None, 2, N_HEADS, 2 * BLK, BLK), lambda r, i: (branch, 0, 0, 0, 0))],
        out_specs=[out, out],
        scratch_shapes=[pltpu.VMEM((B * N_HEADS, 2 * BLK, BLK), F32),
                        pltpu.VMEM((B * N_HEADS, 2 * BLK, BLK), BF16)],
        args=(qkv_v, qkv_v, bias))


VIEW_TILE = 512


def _view_spec(d, w):
    return pl.BlockSpec((1, VIEW_TILE // d, d * w), lambda b, t: (b, t, 0))


def _view_shape(B, S, d, w, dtype):
    return jax.ShapeDtypeStruct((B, S // d, d * w), dtype)


def _tile_scratch(w):
    return pltpu.VMEM((w // LANES, VIEW_TILE, LANES), F32)


def _put_tile(tile_ref, val):
    for c in range(tile_ref.shape[0]):
        tile_ref[c] = val[:, c * LANES:(c + 1) * LANES]


def _get_tile(tile_ref):
    return jnp.concatenate([tile_ref[c] for c in range(tile_ref.shape[0])], axis=1)


def _tile_to_view(tile_ref, view_ref, d, w):
    for c in range(w // LANES):
        for r in range(d):
            lo = r * w + c * LANES
            rows = tile_ref.at[c][pl.ds(r, VIEW_TILE // d, stride=d), :]
            view_ref[0, :, lo:lo + LANES] = rows.astype(view_ref.dtype)


def _view_to_tile(view_ref, tile_ref, d, w):
    for c in range(w // LANES):
        for r in range(d):
            lo = r * w + c * LANES
            tile_ref.at[c][pl.ds(r, VIEW_TILE // d, stride=d), :] = view_ref[0, :, lo:lo + LANES].astype(F32)


def _mm_qkv_views(h, w, name):
    B, S, D = h.shape
    N = w.shape[1]

    def body(h_ref, w_ref, o1_ref, o4_ref, o16_ref, acc_ref):
        acc = jnp.dot(h_ref[0], w_ref[...], preferred_element_type=F32)
        o1_ref[0] = acc.astype(o1_ref.dtype)
        _put_tile(acc_ref, acc)
        _tile_to_view(acc_ref, o4_ref, DILATIONS[1], N)
        _tile_to_view(acc_ref, o16_ref, DILATIONS[2], N)

    return pl.pallas_call(
        body, name=name,
        out_shape=[_view_shape(B, S, d, N, BF16) for d in DILATIONS],
        grid=(B, S // VIEW_TILE),
        in_specs=[pl.BlockSpec((1, VIEW_TILE, D), lambda b, t: (b, t, 0)), pl.BlockSpec((D, N), lambda b, t: (0, 0))],
        out_specs=[_view_spec(d, N) for d in DILATIONS],
        scratch_shapes=[_tile_scratch(N)],
        compiler_params=_cparams(("parallel", "parallel")),
    )(h, w)


def _dil_merge(os_, lses, name):
    B, S, W = os_[0].shape
    nd = len(DILATIONS)

    def body(*refs):
        o_refs, l_refs = refs[:nd], refs[nd:2 * nd]
        out_refs, L_refs = refs[2 * nd:3 * nd], refs[3 * nd:4 * nd]
        scr = refs[4 * nd:]
        o_tok, l_tok = [o_refs[0][0]], [l_refs[0][0]]
        for i, d in enumerate(DILATIONS[1:]):
            _view_to_tile(o_refs[i + 1], scr[2 * i], d, W)
            _view_to_tile(l_refs[i + 1], scr[2 * i + 1], d, W)
            o_tok.append(_get_tile(scr[2 * i]))
            l_tok.append(_get_tile(scr[2 * i + 1]))
        a0, a1, a2 = l_tok
        m = jnp.maximum(jnp.maximum(a0, a1), a2)
        e0, e1, e2 = jnp.exp(a0 - m), jnp.exp(a1 - m), jnp.exp(a2 - m)
        ssum = e0 + e1 + e2
        out = (e0 * o_tok[0] + e1 * o_tok[1] + e2 * o_tok[2]) / ssum
        lse = m + jnp.log(ssum)
        out_refs[0][0] = out
        L_refs[0][0] = lse
        res_o, res_l = scr[2 * (nd - 1)], scr[2 * (nd - 1) + 1]
        _put_tile(res_o, out)
        _put_tile(res_l, lse)
        for i, d in enumerate(DILATIONS[1:]):
            _tile_to_view(res_o, out_refs[i + 1], d, W)
            _tile_to_view(res_l, L_refs[i + 1], d, W)

    specs = [_view_spec(d, W) for d in DILATIONS]
    shapes = [_view_shape(B, S * DILATIONS[0], d, W, F32) for d in DILATIONS]
    res = pl.pallas_call(
        body, name=name, out_shape=shapes * 2, grid=(B, S // VIEW_TILE),
        in_specs=specs * 2, out_specs=specs * 2,
        scratch_shapes=[_tile_scratch(W)] * (2 * nd),
        compiler_params=_cparams(("parallel", "parallel")),
    )(*os_, *lses)
    return res[:nd], res[nd:]


def _dil_bwd(qkv, do, out_a, L, bias, branch, dilation, name, comm=None):
    B, n, _ = qkv.shape
    d = dilation
    nb = n // BLK
    qkv_v, do_v, oa_v, L_v = qkv, do, out_a, L
    npair = N_HEADS // 2
    multi = nb > 1

    tiles = ("P", "C", "N") if multi else ("C",)
    n_t = len(tiles)

    def body(*refs):
        if multi:
            (cur_ref, prev_ref, next_ref, do_ref, don_ref, oa_ref, oan_ref, L_ref, Ln_ref, bias_ref,
             dqkv_ref, dbias_ref, s_scr, dp_scr, p_scr, ds_scr) = refs
        else:
            cur_ref, do_ref, oa_ref, L_ref, bias_ref, dqkv_ref, dbias_ref, s_scr, dp_scr, p_scr, ds_scr = refs
        r, i = pl.program_id(0), pl.program_id(1)

        @pl.when((r == 0) & (i == 0))
        def _():
            dbias_ref[...] = jnp.zeros_like(dbias_ref)

        first = jnp.where(i == 0, 1, 0)
        variant = {"P": first, "C": first, "N": 0}
        band = {"P": slice(0, BLK), "C": slice(BLK, 2 * BLK), "N": slice(0, BLK)}
        psl = lambda p: slice(p * LANES, (p + 1) * LANES)
        ksl = lambda p: slice(D_A + p * LANES, D_A + (p + 1) * LANES)
        vsl = lambda p: slice(2 * D_A + p * LANES, 2 * D_A + (p + 1) * LANES)

        def operands(b, p, hh):
            hm = _head_mask((BLK, LANES), hh)
            mask = lambda x: jnp.where(hm, x, jnp.zeros_like(x))
            qm, dom = mask(cur_ref[b, :, psl(p)] * DIL_SCALE), mask(do_ref[b, :, psl(p)])
            ops = {"C": (cur_ref[b, :, ksl(p)], cur_ref[b, :, vsl(p)], qm, dom)}
            if multi:
                ops["P"] = (prev_ref[b, :, ksl(p)], prev_ref[b, :, vsl(p)], qm, dom)
                ops["N"] = (cur_ref[b, :, ksl(p)], cur_ref[b, :, vsl(p)],
                            mask(next_ref[b, :, psl(p)] * DIL_SCALE), mask(don_ref[b, :, psl(p)]))
            return ops

        pairs = [(b, p) for b in range(B) for p in range(npair)]
        for b, p in pairs:
            for hh in range(2):
                u = b * N_HEADS + 2 * p + hh
                ops = operands(b, p, hh)
                for t, name_t in enumerate(tiles):
                    k_t, v_t, q_t, do_t = ops[name_t]
                    s_scr[u, t] = _dot_nt(k_t, q_t)
                    dp_scr[u, t] = _dot_nt(v_t, do_t)

        def rows(L_r, do_r, oa_r, b, p):
            lt = jnp.transpose(L_r[b, :, psl(p)])
            dt = jnp.transpose(do_r[b, :, psl(p)].astype(F32) * oa_r[b, :, psl(p)])
            return ([lt[0:1, :], lt[HEAD_DIM:HEAD_DIM + 1, :]],
                    [jnp.sum(dt[:HEAD_DIM], axis=0, keepdims=True), jnp.sum(dt[HEAD_DIM:], axis=0, keepdims=True)])

        for b, p in pairs:
            lse_c, delta_c = rows(L_ref, do_ref, oa_ref, b, p)
            if multi:
                lse_n, delta_n = rows(Ln_ref, don_ref, oan_ref, b, p)
            for hh in range(2):
                h = 2 * p + hh
                u = b * N_HEADS + h
                for t, name_t in enumerate(tiles):
                    lse, delta = (lse_n[hh], delta_n[hh]) if name_t == "N" else (lse_c[hh], delta_c[hh])
                    pr = jnp.exp(s_scr[u, t] + bias_ref[variant[name_t], h, band[name_t], :] - lse)
                    if name_t == "N":
                        pr = jnp.where(i < nb - 1, pr, 0.0)
                    ds = pr * (dp_scr[u, t] - delta)
                    p_scr[u, t] = pr.astype(BF16)
                    ds_scr[u, t] = ds.astype(BF16)
                    if name_t != "N":
                        dbias_ref[h, band[name_t], :] += ds

        for b, p in pairs:
            dqt = jnp.zeros((LANES, BLK), F32)
            dk = jnp.zeros((BLK, LANES), F32)
            dv = jnp.zeros((BLK, LANES), F32)
            kct = jnp.transpose(cur_ref[b, :, ksl(p)].astype(F32)).astype(BF16)
            if multi:
                kpt = jnp.transpose(prev_ref[b, :, ksl(p)].astype(F32)).astype(BF16)
            for hh in range(2):
                u = b * N_HEADS + 2 * p + hh
                ops = operands(b, p, hh)
                mine = _row_mask((LANES, BLK), hh)
                for t, name_t in enumerate(tiles):
                    _, _, q_t, do_t = ops[name_t]
                    if name_t != "P":
                        dv = dv + _dot_nn(p_scr[u, t], do_t)
                        dk = dk + _dot_nn(ds_scr[u, t], q_t)
                    if name_t != "N":
                        kt = kpt if name_t == "P" else kct
                        dqt = dqt + _dot_nn(jnp.where(mine, kt, jnp.zeros_like(kt)), ds_scr[u, t])
            dqkv_ref[b, :, psl(p)] = jnp.transpose(dqt) * DIL_SCALE
            dqkv_ref[b, :, ksl(p)] = dk
            dqkv_ref[b, :, vsl(p)] = dv

    def at(off):
        return lambda r, i: (0, jnp.clip(i + off, 0, nb - 1), r)

    qkv_spec = lambda off: pl.BlockSpec((B, BLK, P_QKV), at(off))
    da_spec = lambda off: pl.BlockSpec((B, BLK, D_A), at(off))
    bias_spec = pl.BlockSpec((None, 2, N_HEADS, 2 * BLK, BLK), lambda r, i: (branch, 0, 0, 0, 0))
    dbias_spec = pl.BlockSpec((N_HEADS, 2 * BLK, BLK), lambda r, i: (0, 0, 0))
    if multi:
        in_specs = [qkv_spec(0), qkv_spec(-1), qkv_spec(1), da_spec(0), da_spec(1), da_spec(0), da_spec(1),
                    da_spec(0), da_spec(1), bias_spec]
        args = [qkv_v, qkv_v, qkv_v, do_v, do_v, oa_v, oa_v, L_v, L_v, bias]
    else:
        in_specs = [qkv_spec(0), da_spec(0), da_spec(0), da_spec(0), bias_spec]
        args = [qkv_v, do_v, oa_v, L_v, bias]
    return _host_call(
        body, comm, name=name,
        out_shape=[jax.ShapeDtypeStruct((B, n, d * P_QKV), F32),
                   jax.ShapeDtypeStruct((N_HEADS, 2 * BLK, BLK), F32)],
        grid=(d, nb),
        in_specs=in_specs,
        out_specs=[qkv_spec(0), dbias_spec],
        scratch_shapes=[pltpu.VMEM((B * N_HEADS, n_t, BLK, BLK), F32), pltpu.VMEM((B * N_HEADS, n_t, BLK, BLK), F32),
                        pltpu.VMEM((B * N_HEADS, n_t, BLK, BLK), BF16),
                        pltpu.VMEM((B * N_HEADS, n_t, BLK, BLK), BF16)],
        args=args)


def _sum_views_bf16(parts, name):
    B, S, W = parts[0].shape

    def body(a_ref, b_ref, c_ref, o_ref, sb, sc):
        _view_to_tile(b_ref, sb, DILATIONS[1], W)
        _view_to_tile(c_ref, sc, DILATIONS[2], W)
        o_ref[0] = (a_ref[0] + _get_tile(sb) + _get_tile(sc)).astype(o_ref.dtype)

    return pl.pallas_call(
        body, name=name, out_shape=jax.ShapeDtypeStruct((B, S, W), BF16), grid=(B, S // VIEW_TILE),
        in_specs=[_view_spec(d, W) for d in DILATIONS], out_specs=_view_spec(1, W),
        scratch_shapes=[_tile_scratch(W)] * 2,
        compiler_params=_cparams(("parallel", "parallel")),
    )(*parts)


def _bias_tables(rel_bias, buckets, name, comm=None):
    nbr = buckets.shape[0]

    def body(rb_ref, bk_ref, o_ref):
        first, h = pl.program_id(1), pl.program_id(2)
        tab = bk_ref[0]

        def step(bkt, acc):
            return jnp.where(tab == bkt, rb_ref[bkt, h], acc)

        bias = lax.fori_loop(0, N_BUCKETS, step, jnp.zeros((2 * BLK, BLK), F32))
        row = lax.broadcasted_iota(jnp.int32, (2 * BLK, BLK), 0)
        col = lax.broadcasted_iota(jnp.int32, (2 * BLK, BLK), 1)
        valid = ((row < BLK) & (row >= col) & (first == 0)) | ((row >= BLK) & (row - BLK <= col))
        o_ref[0, 0, 0] = jnp.where(valid, bias, NEG)

    (bias,), got = _host_call(
        body, comm, name=name, out_shape=[jax.ShapeDtypeStruct((nbr, 2, N_HEADS, 2 * BLK, BLK), F32)],
        grid=(nbr, 2, N_HEADS),
        in_specs=[pl.BlockSpec(memory_space=pltpu.SMEM),
                  pl.BlockSpec((1, 2 * BLK, BLK), lambda i, f, h: (i, 0, 0))],
        out_specs=[pl.BlockSpec((1, 1, 1, 2 * BLK, BLK), lambda i, f, h: (i, f, h, 0, 0))],
        scratch_shapes=[], args=(rel_bias, buckets))
    return bias, got


def _bias_grad(dbias_list, buckets, name):
    nbr = len(dbias_list)

    def body(*refs):
        d_refs, bk_ref, o_ref, part = refs[:nbr], refs[nbr], refs[nbr + 1], refs[nbr + 2]

        def step(bkt, carry):
            hit = [bk_ref[bi] == bkt for bi in range(nbr)]
            for h in range(N_HEADS):
                tot = jnp.zeros((1, BLK), F32)
                for bi in range(nbr):
                    tot = tot + jnp.sum(jnp.where(hit[bi], d_refs[bi][h], 0.0), axis=0, keepdims=True)
                part[bkt, h:h + 1, :] = tot
            return carry

        lax.fori_loop(0, N_BUCKETS, step, 0)
        lane = lax.broadcasted_iota(jnp.int32, (N_HEADS, LANES), 1)
        acc = jnp.zeros((N_HEADS, LANES), F32)
        for bkt in range(N_BUCKETS):
            acc = acc + jnp.where(lane == bkt, jnp.sum(part[bkt], axis=1, keepdims=True), 0.0)
        o_ref[...] = acc

    band = pl.BlockSpec((N_HEADS, 2 * BLK, BLK), lambda i: (0, 0, 0))
    return pl.pallas_call(
        body, name=name, out_shape=jax.ShapeDtypeStruct((N_HEADS, LANES), F32), grid=(1,),
        in_specs=[band] * nbr + [pl.BlockSpec((nbr, 2 * BLK, BLK), lambda i: (0, 0, 0))],
        out_specs=pl.BlockSpec((N_HEADS, LANES), lambda i: (0, 0)),
        scratch_shapes=[pltpu.VMEM((N_BUCKETS, N_HEADS, BLK), F32)],
        compiler_params=_cparams(("arbitrary",)),
    )(*dbias_list, buckets)


MLA_TQ = 256
MLA_TK = 256


LOG2E = math.log2(math.e)
MLA_C = MLA_SCALE * LOG2E


def _key_le_query(tk, tq):
    return lax.broadcasted_iota(jnp.int32, (tk, tq), 0) <= lax.broadcasted_iota(jnp.int32, (tk, tq), 1)


def _row_mask(shape, hh):
    row = lax.broadcasted_iota(jnp.int32, shape, 0)
    return (row >= hh * HEAD_DIM) & (row < (hh + 1) * HEAD_DIM)


def _host_call(body, comm, *, name, grid, in_specs, out_specs, out_shape, scratch_shapes, args):
    sem = ("arbitrary",) * len(grid)
    if comm is None:
        res = pl.pallas_call(body, name=name, grid=grid, in_specs=in_specs, out_specs=out_specs,
                             out_shape=out_shape, scratch_shapes=scratch_shapes,
                             compiler_params=_cparams(sem))(*args)
        return res, []
    n_in, n_out, n_s, cn = len(in_specs), len(out_specs), len(scratch_shapes), comm.n

    def hosted(*refs):
        ins, refs = refs[:n_in], refs[n_in:]
        c_ins, refs = refs[:cn], refs[cn:]
        outs, refs = refs[:n_out], refs[n_out:]
        c_outs, refs = refs[:cn], refs[cn:]
        scr, c_sems = refs[:n_s], refs[n_s:]
        ids = [pl.program_id(a) for a in range(len(grid))]
        first = functools.reduce(jnp.logical_and, [i == 0 for i in ids])
        last = functools.reduce(jnp.logical_and, [i == g - 1 for i, g in zip(ids, grid)])

        @pl.when(first)
        def _():
            comm.start(c_ins, c_outs, c_sems)

        body(*ins, *outs, *scr)

        @pl.when(last)
        def _():
            comm.finish(c_ins, c_outs, c_sems)

    res = pl.pallas_call(
        hosted, name=name, grid=grid, in_specs=list(in_specs) + _hbm_specs(cn),
        out_specs=list(out_specs) + _hbm_specs(cn), out_shape=list(out_shape) + list(comm.out_shape),
        scratch_shapes=list(scratch_shapes) + list(comm.scratch), compiler_params=_cparams(sem),
    )(*args, *comm.inputs)
    return res[:n_out], res[n_out:]


def _mla_fwd_t(q, k, vt, name, comm=None):
    B, S, _ = q.shape
    tq, tk = MLA_TQ, MLA_TK
    assert tq == tk
    npair = N_HEADS // 2
    nq = S // tq

    def body(q_ref, k_ref, vt_ref, o_ref, lse_ref, s_scr, e_scr, acc_scr, m_scr, a_scr):
        i = pl.program_id(1)
        diag = _key_le_query(tk, tq)
        m_scr[...] = jnp.full_like(m_scr, NEG)
        acc_scr[...] = jnp.zeros_like(acc_scr)

        def step(j, masked):
            rows = pl.ds(pl.multiple_of(j * tk, tk), tk)
            for h in range(N_HEADS):
                hsl = slice(h * LANES, (h + 1) * LANES)
                s_scr[h] = _dot_nt(k_ref[0, rows, hsl], q_ref[0, :, hsl])
            for h in range(N_HEADS):
                s = s_scr[h]
                if masked:
                    s = jnp.where(diag, s, NEG)
                m_old = m_scr[h:h + 1, :]
                m_new = jnp.maximum(m_old, jnp.max(s, axis=0, keepdims=True))
                a_scr[h:h + 1, :] = jnp.exp2((m_old - m_new) * MLA_C)
                e_scr[h] = jnp.exp2((s - m_new) * MLA_C).astype(BF16)
                m_scr[h:h + 1, :] = m_new
            for h in range(N_HEADS):
                vj = vt_ref[0, h // 2, j]
                vh = jnp.where(_row_mask(vj.shape, h % 2), vj, jnp.ones_like(vj))
                acc_scr[h] = acc_scr[h] * a_scr[h:h + 1, :] + _dot_nn(vh, e_scr[h])

        def loop_body(j, carry):
            step(j, False)
            return carry

        lax.fori_loop(0, i, loop_body, 0)
        step(i, True)
        rows0 = _row_mask((LANES, tq), 0)
        for p in range(npair):
            l0 = acc_scr[2 * p, HEAD_DIM:HEAD_DIM + 1, :]
            l1 = acc_scr[2 * p + 1, 0:1, :]
            o_t = jnp.where(rows0, acc_scr[2 * p] / l0, acc_scr[2 * p + 1] / l1)
            o_ref[0, :, p * LANES:(p + 1) * LANES] = jnp.transpose(o_t)
            lse_ref[0, p, 0] = jnp.zeros((8, tq), F32)
            lse_ref[0, p, 0, 0:1, :] = m_scr[2 * p:2 * p + 1, :] * MLA_C + jnp.log(l0) * LOG2E
            lse_ref[0, p, 0, 1:2, :] = m_scr[2 * p + 1:2 * p + 2, :] * MLA_C + jnp.log(l1) * LOG2E

    return _host_call(
        body, comm, name=name,
        out_shape=[jax.ShapeDtypeStruct((B, S, D_B), F32), jax.ShapeDtypeStruct((B, npair, nq, 8, tq), F32)],
        grid=(B, nq),
        in_specs=[pl.BlockSpec((1, tq, N_HEADS * LANES), lambda b, i: (b, i, 0)),
                  pl.BlockSpec((1, S, N_HEADS * LANES), lambda b, i: (b, 0, 0)),
                  pl.BlockSpec((1, npair, S // tk, LANES, tk), lambda b, i: (b, 0, 0, 0, 0))],
        out_specs=[pl.BlockSpec((1, tq, D_B), lambda b, i: (b, i, 0)),
                   pl.BlockSpec((1, npair, 1, 8, tq), lambda b, i: (b, 0, i, 0, 0))],
        scratch_shapes=[pltpu.VMEM((N_HEADS, tk, tq), F32), pltpu.VMEM((N_HEADS, tk, tq), BF16),
                        pltpu.VMEM((N_HEADS, LANES, tq), F32), pltpu.VMEM((N_HEADS, tq), F32),
                        pltpu.VMEM((N_HEADS, tq), F32)],
        args=(q, k, vt))


def _mla_delta(do, o, name):
    B, S, _ = o.shape
    tq = MLA_TQ
    npair = N_HEADS // 2

    def body(do_ref, o_ref, d_ref):
        d_ref[...] = jnp.zeros_like(d_ref)
        for p in range(npair):
            sl = slice(p * LANES, (p + 1) * LANES)
            prod_t = jnp.transpose(do_ref[0, :, sl].astype(F32) * o_ref[0, :, sl])
            d_ref[0, p, 0, 0:1, :] = jnp.sum(prod_t[:HEAD_DIM], axis=0, keepdims=True)
            d_ref[0, p, 0, 1:2, :] = jnp.sum(prod_t[HEAD_DIM:], axis=0, keepdims=True)

    tok = pl.BlockSpec((1, tq, D_B), lambda b, i: (b, i, 0))
    return pl.pallas_call(
        body, name=name, out_shape=jax.ShapeDtypeStruct((B, npair, S // tq, 8, tq), F32),
        grid=(B, S // tq), in_specs=[tok, tok],
        out_specs=pl.BlockSpec((1, npair, 1, 8, tq), lambda b, i: (b, 0, i, 0, 0)),
        compiler_params=_cparams(("parallel", "parallel")),
    )(do, o)


def _mla_bwd_t(q, k, v, do, lse, delta, name, comm=None):
    B, S, _ = q.shape
    tq, tk = MLA_TQ, MLA_TK
    assert tq == tk
    npair = N_HEADS // 2
    nq = S // tq

    hg = 4
    pg = hg // 2
    ngroup = N_HEADS // hg

    def body(q_ref, do_ref, lse_ref, dl_ref, k_ref, v_ref, dk_ref, dv_ref, dq_ref,
             s_scr, dp_scr, p_scr, ds_scr, dk_s, dv_s, kt_s):
        j = pl.program_id(2)

        @pl.when(j == 0)
        def _():
            dq_ref[...] = jnp.zeros_like(dq_ref)

        dk_s[...] = jnp.zeros_like(dk_s)
        dv_s[...] = jnp.zeros_like(dv_s)
        diag = _key_le_query(tk, tq)
        hsl = lambda h: slice(h * LANES, (h + 1) * LANES)
        for h in range(hg):
            kt_s[h] = jnp.transpose(k_ref[0, :, hsl(h)].astype(F32)).astype(BF16)

        def step(i, masked):
            rows = pl.ds(pl.multiple_of(i * tq, tq), tq)

            def dom(h):
                dov = do_ref[0, rows, hsl(h // 2)]
                return jnp.where(_head_mask((tq, LANES), h % 2), dov, jnp.zeros_like(dov))

            for h in range(hg):
                s_scr[h] = _dot_nt(k_ref[0, :, hsl(h)], q_ref[0, rows, hsl(h)])
                dp_scr[h] = _dot_nt(v_ref[0, :, hsl(h // 2)], dom(h))
            for h in range(hg):
                pr = jnp.exp2(s_scr[h] * MLA_C - lse_ref[0, h // 2, i, h % 2:h % 2 + 1, :])
                if masked:
                    pr = jnp.where(diag, pr, 0.0)
                p_scr[h] = pr.astype(BF16)
                ds_scr[h] = (pr * (dp_scr[h] - dl_ref[0, h // 2, i, h % 2:h % 2 + 1, :])).astype(BF16)
            for h in range(hg):
                dv_s[h // 2] += _dot_nn(p_scr[h], dom(h))
                dk_s[h] += _dot_nn(ds_scr[h], q_ref[0, rows, hsl(h)])
                dq_ref[0, h // 2, i, hsl(h % 2), :] += _dot_nn(kt_s[h], ds_scr[h]) * MLA_SCALE

        step(j, True)

        def loop_body(i, carry):
            step(i, False)
            return carry

        lax.fori_loop(j + 1, nq, loop_body, 0)
        for h in range(hg):
            dk_ref[0, :, hsl(h)] = dk_s[h] * MLA_SCALE
        for p in range(pg):
            dv_ref[0, :, hsl(p)] = dv_s[p]

    stat = pl.BlockSpec((1, pg, nq, 8, tq), lambda b, g, j: (b, g, 0, 0, 0))
    return _host_call(
        body, comm, name=name,
        out_shape=[jax.ShapeDtypeStruct((B, S, N_HEADS * LANES), F32), jax.ShapeDtypeStruct((B, S, D_B), F32),
                   jax.ShapeDtypeStruct((B, npair, nq, 2 * LANES, tq), F32)],
        grid=(B, ngroup, S // tk),
        in_specs=[pl.BlockSpec((1, S, hg * LANES), lambda b, g, j: (b, 0, g)),
                  pl.BlockSpec((1, S, pg * LANES), lambda b, g, j: (b, 0, g)),
                  stat, stat,
                  pl.BlockSpec((1, tk, hg * LANES), lambda b, g, j: (b, j, g)),
                  pl.BlockSpec((1, tk, pg * LANES), lambda b, g, j: (b, j, g))],
        out_specs=[pl.BlockSpec((1, tk, hg * LANES), lambda b, g, j: (b, j, g)),
                   pl.BlockSpec((1, tk, pg * LANES), lambda b, g, j: (b, j, g)),
                   pl.BlockSpec((1, pg, nq, 2 * LANES, tq), lambda b, g, j: (b, g, 0, 0, 0))],
        scratch_shapes=[pltpu.VMEM((hg, tk, tq), F32), pltpu.VMEM((hg, tk, tq), F32),
                        pltpu.VMEM((hg, tk, tq), BF16), pltpu.VMEM((hg, tk, tq), BF16),
                        pltpu.VMEM((hg, tk, LANES), F32), pltpu.VMEM((pg, tk, LANES), F32),
                        pltpu.VMEM((hg, LANES, tk), BF16)],
        args=(q, do, lse, delta, k, v))


def _bucket_tables():
    return jnp.asarray(np.stack([_band_buckets(d) for d in DILATIONS]))


def _local_step(x, target, mod, wts, gains, rel_bias, ffn_shards=None, bias=None):
    B, S, D = x.shape
    T = B * S
    sh1, sc1, g1, sh2, sc2, g2 = [mod[:, i * D:(i + 1) * D].reshape(B, 1, D) for i in range(N_MOD)]
    cs, sn = _rope_tables()
    buckets_dev = _bucket_tables()
    if bias is None:
        bias, _ = _bias_tables(rel_bias, buckets_dev, "rel_bias_tables")
    w_in = wts["w_in"]

    h1 = _adaln_fwd(x, gains["g_norm1"], sc1, sh1, "adaln1_fwd")
    h1f = h1.reshape(T, D)
    qkv_v = _mm_qkv_views(h1, w_in[:, :P_QKV], "mm_qkv")
    rest = _mm(h1f, w_in[:, P_QKV:], "nn", F32, "mm_rest")
    o_d, lse_d = [], []
    late_got = []
    for i, d in enumerate(DILATIONS):
        comm = _GatherComm(ffn_shards[i + 1:i + 2]) if (ffn_shards and i < 2) else None
        (o_i, lse_i), got = _dil_fwd(qkv_v[i], bias, i, d, f"dil_fwd_{d}", comm)
        late_got += list(got)
        o_d.append(o_i)
        lse_d.append(lse_i)
    if ffn_shards:
        wts = dict(wts, w_out=late_got[1].reshape(D, D))
    out_a_v, lse_a_v = _dil_merge(o_d, lse_d, "dil_merge")
    out_a = out_a_v[0]
    cqn = _rms_fwd(rest, 1, Q_LORA, gains["g_cq"], "rms_cq_fwd")
    ckvn = _rms_fwd(rest, 0, KV_LORA, gains["g_ckv"], "rms_ckv_fwd")
    rest3 = rest.reshape(B, S, P_REST)
    q_raw = _mm(cqn, wts["w_uq"], "nn", F32, "mm_uq").reshape(B, S, N_HEADS * LANES)
    qc = _rope_apply(q_raw, cs, sn, BF16, "rope_q")
    kn_raw = _mm(ckvn, wts["w_kv"][:, :N_HEADS * LANES], "nn", F32, "mm_uk").reshape(B, S, N_HEADS * LANES)
    kc = _rope_apply(kn_raw, cs, sn, BF16, "rope_k", add=rest3, add_blk=KV_LORA // LANES)
    v = _mm(ckvn, wts["w_kv"][:, N_HEADS * LANES:], "nn", BF16, "mm_uv").reshape(B, S, D_B)
    vt = jnp.transpose(v.reshape(B, S // MLA_TK, MLA_TK, N_HEADS // 2, LANES), (0, 3, 1, 4, 2))
    (out_b, lse_b), got = _mla_fwd_t(qc, kc, vt, "mla_fwd", _GatherComm(ffn_shards[:1]) if ffn_shards else None)
    if ffn_shards:
        wts = dict(wts, w_ffn_in=got[0].reshape(N_CHIP, D, -1), w_ffn_out=late_got[0].reshape(D_FF, D))
    out_af, out_bf = out_a.reshape(T, D_A), out_b.reshape(T, D_B)
    y = _rms_fwd_pair(out_af, out_bf, gains["g_out_a"], gains["g_out_b"], "rms_out_fwd")
    mix = _mm(y, wts["w_out"], "nn", F32, "mm_out").reshape(B, S, D)
    h2, x1 = _adaln_fwd(x, gains["g_norm2"], sc2, sh2, "adaln2_fwd", mix=mix, gate=g1)
    h2f = h2.reshape(T, D)
    gu, act = _ffn_in_fwd(h2f, wts["w_ffn_in"], "mm_ffn_in")
    f = _mm(act, wts["w_ffn_out"], "nn", F32, "mm_ffn_out").reshape(B, S, D)
    dx2, df, dg2, dg_final, loss = _final_loss(x1, f, g2, gains["g_final"], target, "final_loss")

    dff = df.reshape(T, D)
    dgu = _ffn_out_bwd(dff, wts["w_ffn_out"], gu, "mm_ffn_out_dx")
    gw_ffn_out = _mm(act, dff, "tn", F32, "mm_ffn_out_dw")
    dh2 = _mm(dgu, wts["w_ffn_in"], "nt", F32, "mm_ffn_in_dx", col_blocks=N_CHIP, halves=True).reshape(B, S, D)
    gw_ffn_in = _mm(h2f, dgu, "tn", F32, "mm_ffn_in_dw", col_blocks=N_CHIP, halves=True)
    ffn_g8 = ffn_r1 = None
    if ffn_shards:
        ffn_g8 = [gw_ffn_in.reshape(N_DEV, -1, gw_ffn_in.shape[-1]), gw_ffn_out.reshape(N_DEV, -1, D)]
        dx1, dsh2, dsc2, dg_norm2, dg1, dmix, ffn_r1 = _adaln_bwd(
            dh2, x1, gains["g_norm2"], sc2, dx2, "adaln2_bwd", mix=mix, gate=g1, comm=_ToSiblingComm(ffn_g8))
    else:
        dx1, dsh2, dsc2, dg_norm2, dg1, dmix = _adaln_bwd(dh2, x1, gains["g_norm2"], sc2, dx2, "adaln2_bwd",
                                                          mix=mix, gate=g1)
    dmixf = dmix.reshape(T, D)
    dy = _mm(dmixf, wts["w_out"], "nt", F32, "mm_out_dx")
    gw_out = _mm(y, dmixf, "tn", F32, "mm_out_dw")
    do_a_v, dg_out_a = _rms_bwd_views(dy, 0, out_a, gains["g_out_a"], "rms_outa_bwd")
    do_b, dg_out_b = _rms_bwd(dy, 1, out_bf, 0, D_B, gains["g_out_b"], "rms_outb_bwd")
    do_b3 = do_b.reshape(B, S, D_B)
    delta_b = _mla_delta(do_b3, out_b, "mla_delta")
    ffn_a4 = ffn_send = None
    if ffn_shards:
        ffn_a4, ffn_send = _rs_first(ffn_g8, "ffn", r1=ffn_r1)
    (dkc, dv, dq_t), ffn_r2 = _mla_bwd_t(qc, kc, v, do_b3, lse_b, delta_b, "mla_bwd",
                                         _ToChipsComm(ffn_send[:1]) if ffn_shards else None)
    dq_raw = _qrope_bwd(dq_t, cs, -sn, "rope_q_bwd").reshape(T, N_HEADS * LANES)
    dkrw = _krope_bwd(dkc, cs, -sn, "rope_k_bwd").reshape(T, LANES)
    dcqn = _mm(dq_raw, wts["w_uq"], "nt", F32, "mm_uq_dx")
    gw_uq = _mm(cqn, dq_raw, "tn", F32, "mm_uq_dw")
    dkv = jnp.concatenate([dkc.reshape(T, -1), dv.reshape(T, -1)], axis=1).astype(BF16)
    dckvn = _mm(dkv, wts["w_kv"], "nt", F32, "mm_ukv_dx")
    gw_kv = _mm(ckvn, dkv, "tn", F32, "mm_ukv_dw")
    dcq, dg_cq = _rms_bwd(dcqn, 0, rest, 1, Q_LORA, gains["g_cq"], "rms_cq_bwd")
    dckv, dg_ckv = _rms_bwd(dckvn, 0, rest, 0, KV_LORA, gains["g_ckv"], "rms_ckv_bwd")
    dqkv_d, dbias_d = [], []
    for i, d in enumerate(DILATIONS):
        comm = _ToChipsComm(ffn_send[1:]) if (ffn_shards and i == 0) else None
        (dqkv_i, dbias_i), got = _dil_bwd(qkv_v[i], do_a_v[i], out_a_v[i], lse_a_v[i], bias, i, d,
                                          f"dil_bwd_{d}", comm)
        if comm is not None:
            ffn_r2 = list(ffn_r2) + list(got)
        dqkv_d.append(dqkv_i)
        dbias_d.append(dbias_i)
    dqkv = _sum_views_bf16(dqkv_d, "dil_bwd_sum").reshape(T, P_QKV)
    g_rel_bias = _bias_grad(dbias_d, buckets_dev, "rel_bias_grad")[:, :N_BUCKETS].T
    dproj = jnp.concatenate([dqkv, dckv, dkrw, dcq], axis=1)
    gw_in = _mm(h1f, dproj, "tn", F32, "mm_in_dw")
    mix_a4 = mix_r2 = None
    if ffn_shards:
        nat = [_w_in_from_kernel(gw_in), _w_uq_from_kernel(gw_uq), _w_ukv_from_kernel(gw_kv)]
        g8 = [_shards_from_full(g) for g in nat] + [gw_out]
        mix_a4, mix_send = _rs_first([g.reshape(N_DEV, -1, g.shape[-1]) for g in g8], "mix")
        dh1, mix_r2 = _mm(dproj, w_in, "nt", F32, "mm_in_dx", comm=_ToChipsComm(mix_send))
    else:
        dh1 = _mm(dproj, w_in, "nt", F32, "mm_in_dx")
    dh1 = dh1.reshape(B, S, D)
    grad_x, dsh1, dsc1, dg_norm1 = _adaln_bwd(dh1, x, gains["g_norm1"], sc1, dx1, "adaln1_bwd")
    gmod = jnp.concatenate([dsh1, dsc1, dg1, dsh2, dsc2, dg2], axis=-1).reshape(B, N_MOD * D)
    grads = dict(w_in=gw_in, w_uq=gw_uq, w_kv=gw_kv, w_out=gw_out, w_ffn_in=gw_ffn_in, w_ffn_out=gw_ffn_out,
                 g_norm1=dg_norm1, g_cq=dg_cq, g_ckv=dg_ckv, rel_bias=g_rel_bias, g_out_a=dg_out_a,
                 g_out_b=dg_out_b, g_norm2=dg_norm2, g_final=dg_final, ffn_pending=(ffn_a4, ffn_r2),
                 mix_pending=(mix_a4, mix_r2))
    return loss, grad_x, gmod, grads


def _w_in_to_kernel(w):
    z = lambda n: jnp.zeros((w.shape[0], n), w.dtype)
    i3, i4, i5 = 3 * D_A, 3 * D_A + Q_LORA, 3 * D_A + Q_LORA + KV_LORA
    return jnp.concatenate([w[:, :i3], w[:, i4:i5], z(NOPE_DIM), w[:, i5:], z(LANES - NOPE_DIM - ROPE_DIM),
                            w[:, i3:i4]], axis=1)


def _w_in_from_kernel(g):
    o = P_QKV + KV_LORA
    return jnp.concatenate([g[:, :P_QKV], g[:, o + LANES:], g[:, P_QKV:o],
                            g[:, o + NOPE_DIM:o + NOPE_DIM + ROPE_DIM]], axis=1)


def _w_uq_to_kernel(w):
    w3 = w.reshape(Q_LORA, N_HEADS, NOPE_DIM + ROPE_DIM)
    return jnp.pad(w3, ((0, 0), (0, 0), (0, LANES - NOPE_DIM - ROPE_DIM))).reshape(Q_LORA, N_HEADS * LANES)


def _w_uq_from_kernel(g):
    return g.reshape(Q_LORA, N_HEADS, LANES)[:, :, :NOPE_DIM + ROPE_DIM].reshape(Q_LORA, -1)


def _w_ukv_to_kernel(w):
    w3 = w.reshape(KV_LORA, N_HEADS, 2 * HEAD_DIM)
    wk = jnp.pad(w3[:, :, :NOPE_DIM], ((0, 0), (0, 0), (0, LANES - NOPE_DIM))).reshape(KV_LORA, N_HEADS * LANES)
    wv = w3[:, :, NOPE_DIM:].reshape(KV_LORA, D_B)
    return jnp.concatenate([wk, wv], axis=1)


def _w_ukv_from_kernel(g):
    gk = g[:, :N_HEADS * LANES].reshape(KV_LORA, N_HEADS, LANES)[:, :, :NOPE_DIM]
    gv = g[:, N_HEADS * LANES:].reshape(KV_LORA, N_HEADS, HEAD_DIM)
    return jnp.concatenate([gk, gv], axis=2).reshape(KV_LORA, -1)


MESH = pl.DeviceIdType.MESH


def _my_place():
    return lax.axis_index("x"), lax.axis_index("y"), lax.axis_index("c")


def _other_chips(x, y):
    return [(1 - x, y), (x, 1 - y), (1 - x, 1 - y)]


def _allgather8(x_shard, name, in_hbm):
    m_per, n = x_shard.shape
    space = pl.ANY if in_hbm else pltpu.VMEM

    def body(x_ref, out_ref, send_sems, recv_sems, local_sem):
        x, y, c = _my_place()
        me, sibling = (x, y, c), (x, y, 1 - c)
        chips = _other_chips(x, y)

        def rows(px, py, pc):
            return out_ref.at[pl.ds((4 * px + 2 * py + pc) * m_per, m_per), :]

        def copy(k, block, to, src=None):
            return pltpu.make_async_remote_copy(
                src_ref=rows(*block) if src is None else src, dst_ref=rows(*block),
                send_sem=send_sems.at[k], recv_sem=recv_sems.at[k], device_id=to, device_id_type=MESH)

        mine = pltpu.make_async_copy(x_ref, rows(*me), local_sem)
        mine.start()
        first = [copy(0, me, sibling, src=x_ref)]
        first += [copy(1 + j, me, (*chip, c), src=x_ref) for j, chip in enumerate(chips)]
        for cp in first:
            cp.start()
        passed = [copy(4 + j, (*chip, c), sibling) for j, chip in enumerate(chips)]
        for j, chip in enumerate(chips):
            copy(1 + j, (*chip, c), me).wait_recv()
            passed[j].start()
        copy(0, sibling, me).wait_recv()
        for j, chip in enumerate(chips):
            copy(4 + j, (*chip, 1 - c), me).wait_recv()
        for cp in first + passed:
            cp.wait_send()
        mine.wait()

    return pl.pallas_call(
        body, name=name,
        out_shape=jax.ShapeDtypeStruct((N_DEV * m_per, n), x_shard.dtype),
        in_specs=[pl.BlockSpec(memory_space=space)],
        out_specs=pl.BlockSpec(memory_space=space),
        scratch_shapes=[pltpu.SemaphoreType.DMA((7,)), pltpu.SemaphoreType.DMA((7,)), pltpu.SemaphoreType.DMA],
        compiler_params=pltpu.CompilerParams(vmem_limit_bytes=VMEM_LIMIT),
    )(x_shard)


def _hbm_specs(n):
    return [pl.BlockSpec(memory_space=pl.ANY)] * n


class _GatherComm:
    def __init__(self, shards):
        self.n = n = len(shards)
        self.inputs = [s.reshape(2, s.shape[0] // 2, s.shape[1]) for s in shards]
        self.out_shape = [jax.ShapeDtypeStruct((N_DEV,) + s.shape[1:], s.dtype) for s in self.inputs]
        self.scratch = [pltpu.SemaphoreType.DMA((7 * n,)), pltpu.SemaphoreType.DMA((7 * n,))]

    def _parts(self, xs, outs, sems):
        send_sems, recv_sems = sems
        x, y, c = _my_place()

        def blk(k, px, py, pc):
            return outs[k].at[4 * px + 2 * py + pc]

        def copy(k, kind, block, to, own=False):
            return pltpu.make_async_remote_copy(
                src_ref=xs[k].at[c] if own else blk(k, *block), dst_ref=blk(k, *block),
                send_sem=send_sems.at[7 * k + kind], recv_sem=recv_sems.at[7 * k + kind],
                device_id=to, device_id_type=MESH)

        def whole(k):
            return pltpu.make_async_remote_copy(
                src_ref=xs[k], dst_ref=outs[k].at[pl.ds(4 * x + 2 * y, 2)],
                send_sem=send_sems.at[7 * k], recv_sem=recv_sems.at[7 * k],
                device_id=(x, y, 1 - c), device_id_type=MESH)

        me, sibling = (x, y, c), (x, y, 1 - c)
        chips = _other_chips(x, y)
        first = []
        for k in range(self.n):
            first.append(whole(k))
            first += [copy(k, 1 + j, me, (*chip, c), own=True) for j, chip in enumerate(chips)]
        return copy, whole, me, sibling, chips, c, first

    def start(self, xs, outs, sems):
        for cp in self._parts(xs, outs, sems)[-1]:
            cp.start()

    def finish(self, xs, outs, sems):
        copy, whole, me, sibling, chips, c, first = self._parts(xs, outs, sems)
        passed = []
        for j, chip in enumerate(chips):
            for k in range(self.n):
                copy(k, 1 + j, (*chip, c), me).wait_recv()
                fwd = copy(k, 4 + j, (*chip, c), sibling)
                fwd.start()
                passed.append(fwd)
        for k in range(self.n):
            whole(k).wait_recv()
        for j, chip in enumerate(chips):
            for k in range(self.n):
                copy(k, 4 + j, (*chip, 1 - c), me).wait_recv()
        for cp in first + passed:
            cp.wait_send()


class _ToChipsComm:
    def __init__(self, a4s):
        self.inputs = list(a4s)
        self.n = n = len(a4s)
        nc = N_CHIP - 1
        self.out_shape = [jax.ShapeDtypeStruct((nc,) + a.shape[1:], a.dtype) for a in a4s]
        self.scratch = [pltpu.SemaphoreType.DMA((nc * n,)), pltpu.SemaphoreType.DMA((nc * n,))]

    def _copies(self, as_, rs, sems):
        send_sems, recv_sems = sems
        x, y, c = _my_place()
        nc = N_CHIP - 1
        return [pltpu.make_async_remote_copy(
            src_ref=as_[k].at[2 * cx + cy], dst_ref=rs[k].at[j], send_sem=send_sems.at[nc * k + j],
            recv_sem=recv_sems.at[nc * k + j], device_id=(cx, cy, c), device_id_type=MESH)
            for k in range(self.n) for j, (cx, cy) in enumerate(_other_chips(x, y))]

    def start(self, as_, rs, sems):
        for cp in self._copies(as_, rs, sems):
            cp.start()

    def finish(self, as_, rs, sems):
        for cp in self._copies(as_, rs, sems):
            cp.wait()


def _run_comm(comm, name):
    n = comm.n

    def body(*refs):
        ins, outs, sems = refs[:n], refs[n:2 * n], refs[2 * n:]
        comm.start(ins, outs, sems)
        comm.finish(ins, outs, sems)

    return pl.pallas_call(
        body, name=name, out_shape=comm.out_shape, in_specs=_hbm_specs(n), out_specs=_hbm_specs(n),
        scratch_shapes=comm.scratch,
    )(*comm.inputs)


class _ToSiblingComm:
    def __init__(self, g8s):
        self.inputs = list(g8s)
        self.n = n = len(g8s)
        self.out_shape = [jax.ShapeDtypeStruct((N_CHIP,) + g.shape[1:], g.dtype) for g in g8s]
        self.scratch = [pltpu.SemaphoreType.DMA((N_CHIP * n,)), pltpu.SemaphoreType.DMA((N_CHIP * n,))]

    def _copies(self, gs, rs, sems):
        send_sems, recv_sems = sems
        x, y, c = _my_place()
        return [pltpu.make_async_remote_copy(
            src_ref=gs[k].at[2 * s + 1 - c], dst_ref=rs[k].at[s], send_sem=send_sems.at[N_CHIP * k + s],
            recv_sem=recv_sems.at[N_CHIP * k + s], device_id=(x, y, 1 - c), device_id_type=MESH)
            for k in range(self.n) for s in range(N_CHIP)]

    def start(self, gs, rs, sems):
        for cp in self._copies(gs, rs, sems):
            cp.start()

    def finish(self, gs, rs, sems):
        for cp in self._copies(gs, rs, sems):
            cp.wait()


def _swap_halves(hs, name):
    n = len(hs)

    def body(*refs):
        o_refs = refs[n:2 * n]
        send_sems, recv_sems = refs[2 * n:]
        x, y, c = _my_place()

        def remote(k, slot):
            return pltpu.make_async_remote_copy(
                src_ref=o_refs[k].at[slot], dst_ref=o_refs[k].at[slot], send_sem=send_sems.at[k],
                recv_sem=recv_sems.at[k], device_id=(x, y, 1 - c), device_id_type=MESH)

        sends = [remote(k, c) for k in range(n)]
        for cp in sends:
            cp.start()
        for k in range(n):
            remote(k, 1 - c).wait_recv()
        for cp in sends:
            cp.wait_send()

    return pl.pallas_call(
        body, name=name,
        out_shape=[jax.ShapeDtypeStruct(h.shape, h.dtype) for h in hs],
        in_specs=_hbm_specs(n), out_specs=_hbm_specs(n),
        input_output_aliases={k: k for k in range(n)},
        scratch_shapes=[pltpu.SemaphoreType.DMA((n,)), pltpu.SemaphoreType.DMA((n,))],
    )(*hs)


ADD_TILES = 2


def _add_blocks(a_list, a_idx_fn, others_list, ns, sel, name, out_blocks=None, out_idx_fn=None,
                bf16_copy=False):
    out_blocks = out_blocks or ns
    out_idx_fn = out_idx_fn or (lambda s, sel_ref: s)
    n = len(a_list)
    n_o = len(others_list[0])
    per = 1 + n_o

    def body(sel_ref, *refs):
        for k in range(n):
            ins = refs[k * per:(k + 1) * per]
            acc = ins[0][0]
            for r in ins[1:]:
                acc = acc + r[0].astype(F32)
            refs[n * per + k][0] = acc
            if bf16_copy:
                refs[n * per + n + k][0] = acc.astype(BF16)

    in_specs, args, out_specs, out_shape = [], [], [], []
    for a, others in zip(a_list, others_list):
        _, R, N = a.shape
        tr = R // ADD_TILES
        assert tr % 8 == 0, a.shape
        in_specs.append(pl.BlockSpec((1, tr, N), lambda s, i, sel_ref: (a_idx_fn(s, sel_ref), i, 0)))
        args.append(a)
        for arr, fixed in others:
            if fixed is None:
                in_specs.append(pl.BlockSpec((1, tr, N), lambda s, i, sel_ref: (s, i, 0)))
            else:
                in_specs.append(pl.BlockSpec((1, tr, N), lambda s, i, sel_ref, fixed=fixed: (fixed, i, 0)))
            args.append(arr)
        out_specs.append(pl.BlockSpec((1, tr, N), lambda s, i, sel_ref: (out_idx_fn(s, sel_ref), i, 0)))
        out_shape.append(jax.ShapeDtypeStruct((out_blocks, R, N), a.dtype))
    if bf16_copy:
        out_specs = out_specs + out_specs
        out_shape = out_shape + [jax.ShapeDtypeStruct(o.shape, BF16) for o in out_shape]
    grid_spec = pltpu.PrefetchScalarGridSpec(num_scalar_prefetch=1, grid=(ns, ADD_TILES), in_specs=in_specs,
                                             out_specs=out_specs)
    return pl.pallas_call(
        body, name=name, out_shape=out_shape, grid_spec=grid_spec,
        compiler_params=_cparams(("parallel", "parallel")),
    )(sel, *args)


def _rs_first(g8s, tag, r1=None):
    c_sel = jnp.reshape(lax.axis_index("c"), (1,)).astype(jnp.int32)
    if r1 is None:
        r1 = _run_comm(_ToSiblingComm(g8s), f"rs_to_sibling_{tag}")
    res = _add_blocks(g8s, lambda s, sel: 2 * s + sel[0], [[(r, None)] for r in r1], N_CHIP, c_sel,
                      f"rs_add_sibling_{tag}", bf16_copy=True)
    return list(res[:len(g8s)]), list(res[len(g8s):])


def _rs_last(a4s, r2s, tag):
    sel = jnp.stack([2 * lax.axis_index("x") + lax.axis_index("y"), lax.axis_index("c")]).astype(jnp.int32)
    h = _add_blocks(a4s, lambda s, sel: sel[0], [[(r, 0), (r, 1), (r, 2)] for r in r2s], 1, sel,
                    f"rs_add_chips_{tag}", out_blocks=2, out_idx_fn=lambda s, sel: sel[1])
    full = _swap_halves(h, f"rs_swap_halves_{tag}")
    return [f.reshape(2 * f.shape[1], f.shape[2]) for f in full]


def _ada_fwd(c_all, w_ada, b_ada, name):
    nb, D = c_all.shape
    ncol = w_ada.shape[1]
    tc = 512

    def body(c_ref, w_ref, b_ref, o_ref):
        cv = c_ref[...]
        cond = (cv * jax.nn.sigmoid(cv)).astype(BF16)
        o_ref[...] = jnp.dot(cond, w_ref[...].astype(BF16), preferred_element_type=F32) + b_ref[...]

    return pl.pallas_call(
        body, name=name, out_shape=jax.ShapeDtypeStruct((nb, ncol), F32), grid=(ncol // tc,),
        in_specs=[pl.BlockSpec((nb, D), lambda j: (0, 0)), pl.BlockSpec((D, tc), lambda j: (0, j)),
                  pl.BlockSpec((1, tc), lambda j: (0, j))],
        out_specs=pl.BlockSpec((nb, tc), lambda j: (0, j)),
        compiler_params=_cparams(("parallel",)),
    )(c_all, w_ada, b_ada)


def _ada_bwd(c_all, gmod_cols, name):
    nb, D = c_all.shape
    ncol = gmod_cols.shape[1]
    tc = 512

    def body(c_ref, g_ref, o_ref):
        cv = c_ref[...]
        cond = (cv * jax.nn.sigmoid(cv)).astype(BF16)
        o_ref[...] = _dot_tn(cond, g_ref[...].astype(BF16))

    return pl.pallas_call(
        body, name=name, out_shape=jax.ShapeDtypeStruct((D, ncol), F32), grid=(ncol // tc,),
        in_specs=[pl.BlockSpec((nb, D), lambda j: (0, 0)), pl.BlockSpec((nb, tc), lambda j: (0, j))],
        out_specs=pl.BlockSpec((D, tc), lambda j: (0, j)),
        compiler_params=_cparams(("parallel",)),
    )(c_all, gmod_cols)


def _adam_math(w, g, m, v):
    m = ADAM_B1 * m + (1.0 - ADAM_B1) * g
    v = ADAM_B2 * v + (1.0 - ADAM_B2) * (g * g)
    m_hat = m / (1.0 - ADAM_B1 ** ADAM_STEP)
    v_hat = v / (1.0 - ADAM_B2 ** ADAM_STEP)
    delta = -ADAM_LR * (m_hat / (jnp.sqrt(v_hat) + ADAM_EPS) + ADAM_WD * w)
    return delta, m, v


def _adamw(w, g, m, v, name):
    rows, cols = w.shape
    tr = _pick(rows, (256, 192, 176, 128, 64, 8))

    def body(w_ref, g_ref, m_ref, v_ref, d_ref, mo_ref, vo_ref):
        d, mn, vn = _adam_math(w_ref[...], g_ref[...], m_ref[...], v_ref[...])
        d_ref[...] = d
        mo_ref[...] = mn
        vo_ref[...] = vn

    spec = pl.BlockSpec((tr, cols), lambda i: (i, 0))
    return pl.pallas_call(
        body, name=name, out_shape=[jax.ShapeDtypeStruct((rows, cols), F32)] * 3, grid=(rows // tr,),
        in_specs=[spec] * 4, out_specs=[spec] * 3, compiler_params=_cparams(("parallel",)),
    )(w, g, m, v)


VEC_ROWS = 8


def _adamw_rows(w, parts, m, v, name):
    n = w.shape[1]
    P = parts.shape[0]
    assert n % (VEC_ROWS * LANES) == 0, n
    shp = (VEC_ROWS, n // VEC_ROWS)

    def body(w_ref, p_ref, m_ref, v_ref, g_ref, d_ref, mo_ref, vo_ref):
        g = p_ref[0]
        for k in range(1, P):
            g = g + p_ref[k]
        d, mn, vn = _adam_math(w_ref[...], g, m_ref[...], v_ref[...])
        g_ref[...] = g
        d_ref[...] = d
        mo_ref[...] = mn
        vo_ref[...] = vn

    vec = pl.BlockSpec(shp, lambda i: (0, 0))
    out = pl.pallas_call(
        body, name=name, out_shape=[jax.ShapeDtypeStruct(shp, F32)] * 4, grid=(1,),
        in_specs=[vec, pl.BlockSpec((P,) + shp, lambda i: (0, 0, 0)), vec, vec], out_specs=[vec] * 4,
        compiler_params=_cparams(("arbitrary",)),
    )(w.reshape(shp), parts.reshape((P,) + shp), m.reshape(shp), v.reshape(shp))
    return [o.reshape(1, n) for o in out]


_SHARDED = ("w_in", "w_uq", "w_ukv", "w_out", "w_ffn_in", "w_ffn_out")
_SMALL = (("g_norm1", 1024), ("g_cq", 384), ("g_ckv", 256), ("rel_bias", 256), ("g_out_a", 512),
          ("g_out_b", 512), ("g_norm2", 1024), ("g_final", 1024))
_SMALL_PAD = 5120


def _full_from_shards(sh):
    return jnp.transpose(sh, (1, 0, 2)).reshape(sh.shape[1], -1)


def _shards_from_full(full):
    rows, cols = full.shape
    return jnp.transpose(full.reshape(rows, N_CHIP, cols // N_CHIP), (1, 0, 2))


def kernel(x, c, w_ada, b_ada, g_norm1, w_in, g_cq, w_uq, g_ckv, w_ukv, rel_bias, g_out_a, g_out_b, w_out, g_norm2, w_ffn_in, w_ffn_out, g_final, loss_target, m_w_ada, m_b_ada, m_g_norm1, m_w_in, m_g_cq, m_w_uq, m_g_ckv, m_w_ukv, m_rel_bias, m_g_out_a, m_g_out_b, m_w_out, m_g_norm2, m_w_ffn_in, m_w_ffn_out, m_g_final, v_w_ada, v_b_ada, v_g_norm1, v_w_in, v_g_cq, v_w_uq, v_g_ckv, v_w_ukv, v_rel_bias, v_g_out_a, v_g_out_b, v_w_out, v_g_norm2, v_w_ffn_in, v_w_ffn_out, v_g_final):
    names = ["w_ada", "b_ada", "g_norm1", "w_in", "g_cq", "w_uq", "g_ckv", "w_ukv", "rel_bias", "g_out_a",
             "g_out_b", "w_out", "g_norm2", "w_ffn_in", "w_ffn_out", "g_final"]
    W = dict(zip(names, [w_ada, b_ada, g_norm1, w_in, g_cq, w_uq, g_ckv, w_ukv, rel_bias, g_out_a, g_out_b,
                         w_out, g_norm2, w_ffn_in, w_ffn_out, g_final]))
    M = dict(zip(names, [m_w_ada, m_b_ada, m_g_norm1, m_w_in, m_g_cq, m_w_uq, m_g_ckv, m_w_ukv, m_rel_bias,
                         m_g_out_a, m_g_out_b, m_w_out, m_g_norm2, m_w_ffn_in, m_w_ffn_out, m_g_final]))
    V = dict(zip(names, [v_w_ada, v_b_ada, v_g_norm1, v_w_in, v_g_cq, v_w_uq, v_g_ckv, v_w_ukv, v_rel_bias,
                         v_g_out_a, v_g_out_b, v_w_out, v_g_norm2, v_w_ffn_in, v_w_ffn_out, v_g_final]))
    B, S, D = x.shape
    mx, my, mc = _my_place()
    dev = 4 * mx + 2 * my + mc
    chip = 2 * mx + my
    pad_rows = 8

    c_all = _allgather8(jnp.pad(c, ((0, pad_rows - B), (0, 0))), "ag_c", False)
    c_all = c_all.reshape(N_DEV, pad_rows, D)[:, :B].reshape(N_DEV * B, D)
    ada_cols = w_ada.shape[-1]
    b_cols = lax.dynamic_slice_in_dim(b_ada, chip * ada_cols, ada_cols, axis=1)
    mod_cols = _ada_fwd(c_all, w_ada[0], b_cols, "ada_fwd")
    mod_all = _allgather8(mod_cols, "ag_mod", False).reshape(N_DEV, N_DEV * B, ada_cols)[0::2]
    mod_all = jnp.transpose(mod_all, (1, 0, 2)).reshape(N_DEV * B, N_MOD * D)
    mod = lax.dynamic_slice_in_dim(mod_all, dev * B, B, axis=0)

    early = ("w_in", "w_uq", "w_ukv")
    bias, got = _bias_tables(rel_bias, _bucket_tables(), "rel_bias_tables",
                             _GatherComm([W[n][0].astype(BF16) for n in early]))
    full = {n: g.reshape((N_CHIP,) + W[n].shape[1:]) for n, g in zip(early, got)}
    wts = dict(w_in=_w_in_to_kernel(_full_from_shards(full["w_in"])),
               w_uq=_w_uq_to_kernel(_full_from_shards(full["w_uq"])),
               w_kv=_w_ukv_to_kernel(_full_from_shards(full["w_ukv"])))
    gains = dict(g_norm1=g_norm1, g_cq=g_cq, g_ckv=g_ckv, g_out_a=g_out_a, g_out_b=g_out_b, g_norm2=g_norm2,
                 g_final=g_final.reshape(1, D))

    loss, grad_x, gmod, grads = _local_step(x, loss_target, mod, wts, gains, rel_bias,
                                            ffn_shards=[w_ffn_in[0].astype(BF16), w_ffn_out[0].astype(BF16),
                                                        w_out[0].astype(BF16)], bias=bias)
    loss = lax.psum(loss[0, 0], ("x", "y", "c"))

    n_small = _SMALL_PAD
    cat = lambda dct: jnp.concatenate([dct[n].reshape(1, -1) for n, _ in _SMALL]
                                      + [jnp.zeros((1, _SMALL_PAD - sum(s for _, s in _SMALL)), F32)], axis=1)
    small = cat(grads)
    rows = jnp.concatenate([gmod, jnp.pad(small, ((0, 0), (0, N_MOD * D - n_small))),
                            jnp.zeros((pad_rows - B - 1, N_MOD * D), F32)], axis=0)
    rows_all = _allgather8(rows, "ag_small", False).reshape(N_DEV, pad_rows, N_MOD * D)
    gmod_all = rows_all[:, :B].reshape(N_DEV * B, N_MOD * D)
    small_parts = rows_all[:, B, :n_small]

    a4, r2 = grads["mix_pending"]
    ffn_a4, ffn_r2 = grads["ffn_pending"]
    G = dict(zip(_SHARDED, _rs_last(list(a4) + list(ffn_a4), list(r2) + list(ffn_r2), "all")))

    gmod_cols = lax.dynamic_slice_in_dim(gmod_all, chip * ada_cols, ada_cols, axis=1)
    G["w_ada"] = _ada_bwd(c_all, gmod_cols, "ada_bwd")
    delta, new_m, new_v = {}, {}, {}
    for n in ("w_ada",) + _SHARDED:
        shp = W[n].shape
        w2 = W[n].reshape(shp[-2], shp[-1])
        d_, m_, v_ = _adamw(w2, G[n], M[n].reshape(w2.shape), V[n].reshape(w2.shape), f"adamw_{n}")
        G[n], delta[n], new_m[n], new_v[n] = [a.reshape(shp) for a in (G[n], d_, m_, v_)]
    gs, ds_, ms_, vs_ = _adamw_rows(cat(W), small_parts, cat(M), cat(V), "adamw_small")
    off = 0
    for n, sz in _SMALL:
        shp = W[n].shape
        G[n], delta[n], new_m[n], new_v[n] = [a[:, off:off + sz].reshape(shp) for a in (gs, ds_, ms_, vs_)]
        off += sz
    G["b_ada"], delta["b_ada"], new_m["b_ada"], new_v["b_ada"] = _adamw_rows(b_ada, gmod_all, m_b_ada, v_b_ada,
                                                                          "adamw_b_ada")
    return (loss, grad_x, *[G[n] for n in names], *[delta[n] for n in names], *[new_m[n] for n in names],
            *[new_v[n] for n in names])
```

```python
import functools
import math

import numpy as np
import jax
import jax.numpy as jnp
from jax import lax
from jax.experimental import pallas as pl
from jax.experimental.pallas import tpu as pltpu

F32 = jnp.float32
BF16 = jnp.bfloat16

D_MODEL = 1024
SEQ = 2048
N_HEADS = 8
HEAD_DIM = 64
D_A = 512
D_B = 512
Q_LORA = 384
KV_LORA = 256
ROPE_DIM = 32
NOPE_DIM = 64
D_FF = 2816
N_MOD = 6
N_BUCKETS = 32
MAX_DISTANCE = 2048
ROPE_THETA = 10000.0
EPS = 1e-6
NEG = -1e30
BLK = 128
DILATIONS = (1, 4, 16)
SPAN = 128
MLA_SCALE = (NOPE_DIM + ROPE_DIM) ** -0.5
DIL_SCALE = HEAD_DIM ** -0.5

ADAM_LR = 0.001
ADAM_B1 = 0.9
ADAM_B2 = 0.999
ADAM_EPS = 1e-08
ADAM_WD = 0.01
ADAM_STEP = 10

N_DEV = 8
N_CHIP = 4
LANES = 128
VMEM_LIMIT = 48 * 1024 * 1024
MM_VMEM_BUDGET = 32 * 1024 * 1024

P_QKV = 3 * D_A
P_REST = KV_LORA + LANES + Q_LORA


def _cparams(sem=None):
    return pltpu.CompilerParams(dimension_semantics=sem, vmem_limit_bytes=VMEM_LIMIT)


def _pick(n, cands):
    for c in cands:
        if n % c == 0:
            return c
    raise ValueError(f"no tile for {n} in {cands}")


def _mm(a, b, mode, out_dtype, name, col_blocks=None, comm=None, halves=False):
    blocked = col_blocks is not None
    if mode == "nn":
        (M, K) = a.shape
        K2, N = (b.shape[1], b.shape[0] * b.shape[2]) if blocked else b.shape
    elif mode == "nt":
        (M, K) = (a.shape[1], 2 * a.shape[2]) if halves else a.shape
        N, K2 = (b.shape[1], b.shape[0] * b.shape[2]) if blocked else b.shape
    else:
        (K, M) = a.shape
        K2, N = (b.shape[1], 2 * b.shape[2]) if halves else b.shape
    assert K == K2, (a.shape, b.shape, mode)
    assert not halves or (blocked and col_blocks == 4 and mode in ("nt", "tn"))
    tn = _pick(N, (1408, 1024, 768, 512, 384, 256, 128))
    tk = _pick(K, (1408, 1152, 1024, 768, 512, 384, 256, 128))
    if blocked and mode == "nt":
        tk = K // col_blocks
    elif blocked:
        tn = N // col_blocks
    nk = K // tk

    def vmem_bytes(tm_):
        tiles = tm_ * tk * a.dtype.itemsize + tk * tn * b.dtype.itemsize + tm_ * tn * jnp.dtype(out_dtype).itemsize
        return 2 * tiles + tm_ * tn * 4

    tm = next(t for t in (1408, 1024, 512, 384, 256, 128) if M % t == 0 and vmem_bytes(t) <= MM_VMEM_BUDGET)
    out_shape = (M, N)
    out_spec = pl.BlockSpec((tm, tn), lambda i, j, k: (i, j))
    if mode == "nn":
        a_spec = pl.BlockSpec((tm, tk), lambda i, j, k: (i, k))
        b_spec = (pl.BlockSpec((None, tk, tn), lambda i, j, k: (j, k, 0)) if blocked
                  else pl.BlockSpec((tk, tn), lambda i, j, k: (k, j)))
        dn = (((1,), (0,)), ((), ()))
    elif mode == "nt":
        a_spec = (pl.BlockSpec((None, tm, tk), lambda i, j, k: (k // 2, i, k % 2)) if halves
                  else pl.BlockSpec((tm, tk), lambda i, j, k: (i, k)))
        b_spec = (pl.BlockSpec((None, tn, tk), lambda i, j, k: (k, j, 0)) if blocked
                  else pl.BlockSpec((tn, tk), lambda i, j, k: (j, k)))
        dn = (((1,), (1,)), ((), ()))
    else:
        a_spec = pl.BlockSpec((tk, tm), lambda i, j, k: (k, i))
        b_spec = (pl.BlockSpec((None, tk, tn), lambda i, j, k: (j // 2, k, j % 2)) if halves
                  else pl.BlockSpec((tk, tn), lambda i, j, k: (k, j)))
        dn = (((0,), (0,)), ((), ()))
        if blocked:
            out_shape = (col_blocks, M, tn)
            out_spec = pl.BlockSpec((None, tm, tn), lambda i, j, k: (j, i, 0))

    def body(a_ref, b_ref, o_ref, acc_ref):
        k = pl.program_id(2)

        @pl.when(k == 0)
        def _():
            acc_ref[...] = jnp.zeros_like(acc_ref)

        acc_ref[...] += lax.dot_general(a_ref[...].astype(BF16), b_ref[...].astype(BF16), dn,
                                        preferred_element_type=F32)

        @pl.when(k == nk - 1)
        def _():
            o_ref[...] = acc_ref[...].astype(o_ref.dtype)

    if comm is not None:
        (out,), got = _host_call(
            body, comm, name=name, out_shape=[jax.ShapeDtypeStruct(out_shape, out_dtype)],
            grid=(M // tm, N // tn, nk), in_specs=[a_spec, b_spec], out_specs=[out_spec],
            scratch_shapes=[pltpu.VMEM((tm, tn), F32)], args=(a, b))
        return out, got
    return pl.pallas_call(
        body, name=name,
        out_shape=jax.ShapeDtypeStruct(out_shape, out_dtype),
        grid=(M // tm, N // tn, nk),
        in_specs=[a_spec, b_spec],
        out_specs=out_spec,
        scratch_shapes=[pltpu.VMEM((tm, tn), F32)],
        compiler_params=_cparams(("parallel", "parallel", "arbitrary")),
    )(a, b)


ROW_TILE = 512


def _adaln_fwd(x, g, sc, sh, name, mix=None, gate=None):
    B, S, D = x.shape
    ts = ROW_TILE
    has_res = mix is not None

    def body(*refs):
        if has_res:
            x_ref, g_ref, sc_ref, sh_ref, mix_ref, gate_ref, h_ref, xr_ref = refs
            xr = x_ref[0] + gate_ref[0] * mix_ref[0]
            xr_ref[0] = xr
        else:
            x_ref, g_ref, sc_ref, sh_ref, h_ref = refs
            xr = x_ref[0]
        r = lax.rsqrt(jnp.mean(xr * xr, axis=-1, keepdims=True) + EPS)
        xn = (xr * r) * g_ref[...]
        h_ref[0] = (xn * (1.0 + sc_ref[0]) + sh_ref[0]).astype(h_ref.dtype)

    tok = pl.BlockSpec((1, ts, D), lambda b, s: (b, s, 0))
    per_b = pl.BlockSpec((1, 1, D), lambda b, s: (b, 0, 0))
    vec = pl.BlockSpec((1, D), lambda b, s: (0, 0))
    in_specs = [tok, vec, per_b, per_b]
    args = [x, g, sc, sh]
    out_shape = [jax.ShapeDtypeStruct((B, S, D), BF16)]
    out_specs = [tok]
    if has_res:
        in_specs += [tok, per_b]
        args += [mix, gate]
        out_shape.append(jax.ShapeDtypeStruct((B, S, D), F32))
        out_specs.append(tok)
    out = pl.pallas_call(
        body, name=name, out_shape=out_shape, grid=(B, S // ts),
        in_specs=in_specs, out_specs=out_specs,
        compiler_params=_cparams(("parallel", "parallel")),
    )(*args)
    return out if has_res else out[0]


def _adaln_bwd(dh, x, g, sc, dres, name, mix=None, gate=None, comm=None):
    B, S, D = x.shape
    ts = ROW_TILE
    has_res = mix is not None

    def body(*refs):
        if has_res:
            (dh_ref, x_ref, g_ref, sc_ref, dres_ref, mix_ref, gate_ref,
             dx_ref, dsh_ref, dsc_ref, dg_ref, dgate_ref, dmix_ref) = refs
        else:
            (dh_ref, x_ref, g_ref, sc_ref, dres_ref, dx_ref, dsh_ref, dsc_ref, dg_ref) = refs
        b, s = pl.program_id(0), pl.program_id(1)
        xv = x_ref[0]
        dhv = dh_ref[0]
        gv = g_ref[...]
        r = lax.rsqrt(jnp.mean(xv * xv, axis=-1, keepdims=True) + EPS)
        n = xv * r
        xn = n * gv
        dxn = dhv * (1.0 + sc_ref[0])
        dn = dxn * gv
        dx = r * (dn - n * jnp.mean(dn * n, axis=-1, keepdims=True)) + dres_ref[0]
        dx_ref[0] = dx

        @pl.when(s == 0)
        def _():
            dsh_ref[...] = jnp.zeros_like(dsh_ref)
            dsc_ref[...] = jnp.zeros_like(dsc_ref)
            if has_res:
                dgate_ref[...] = jnp.zeros_like(dgate_ref)

        @pl.when((s == 0) & (b == 0))
        def _():
            dg_ref[...] = jnp.zeros_like(dg_ref)

        dsh_ref[0] += jnp.sum(dhv, axis=0, keepdims=True)
        dsc_ref[0] += jnp.sum(dhv * xn, axis=0, keepdims=True)
        dg_ref[...] += jnp.sum(dxn * n, axis=0, keepdims=True)
        if has_res:
            dgate_ref[0] += jnp.sum(dx * mix_ref[0], axis=0, keepdims=True)
            dmix_ref[0] = (dx * gate_ref[0]).astype(dmix_ref.dtype)

    tok = pl.BlockSpec((1, ts, D), lambda b, s: (b, s, 0))
    per_b = pl.BlockSpec((1, 1, D), lambda b, s: (b, 0, 0))
    vec = pl.BlockSpec((1, D), lambda b, s: (0, 0))
    in_specs = [tok, tok, vec, per_b, tok]
    args = [dh, x, g, sc, dres]
    out_shape = [jax.ShapeDtypeStruct((B, S, D), F32), jax.ShapeDtypeStruct((B, 1, D), F32),
                 jax.ShapeDtypeStruct((B, 1, D), F32), jax.ShapeDtypeStruct((1, D), F32)]
    out_specs = [tok, per_b, per_b, vec]
    if has_res:
        in_specs += [tok, per_b]
        args += [mix, gate]
        out_shape += [jax.ShapeDtypeStruct((B, 1, D), F32), jax.ShapeDtypeStruct((B, S, D), BF16)]
        out_specs += [per_b, tok]
    res, got = _host_call(body, comm, name=name, out_shape=out_shape, grid=(B, S // ts), in_specs=in_specs,
                          out_specs=out_specs, scratch_shapes=[], args=args)
    return (list(res) + [got]) if comm is not None else res


def _rms_fwd(x, col_blk, n, g, name):
    T = x.shape[0]
    tr = 512

    def body(x_ref, g_ref, y_ref):
        xv = x_ref[...]
        r = lax.rsqrt(jnp.mean(xv * xv, axis=-1, keepdims=True) + EPS)
        y_ref[...] = ((xv * r) * g_ref[...]).astype(y_ref.dtype)

    return pl.pallas_call(
        body, name=name, out_shape=jax.ShapeDtypeStruct((T, n), BF16), grid=(T // tr,),
        in_specs=[pl.BlockSpec((tr, n), lambda i: (i, col_blk)), pl.BlockSpec((1, n), lambda i: (0, 0))],
        out_specs=pl.BlockSpec((tr, n), lambda i: (i, 0)),
        compiler_params=_cparams(("parallel",)),
    )(x, g)


def _rms_fwd_pair(xa, xb, ga, gb, name):
    T, na = xa.shape
    nb = xb.shape[1]
    tr = 512

    def body(xa_ref, xb_ref, ga_ref, gb_ref, y_ref):
        for x_ref, g_ref, lo, n in ((xa_ref, ga_ref, 0, na), (xb_ref, gb_ref, na, nb)):
            xv = x_ref[...]
            r = lax.rsqrt(jnp.mean(xv * xv, axis=-1, keepdims=True) + EPS)
            y_ref[:, lo:lo + n] = ((xv * r) * g_ref[...]).astype(y_ref.dtype)

    row = lambda n: pl.BlockSpec((tr, n), lambda i: (i, 0))
    vec = lambda n: pl.BlockSpec((1, n), lambda i: (0, 0))
    return pl.pallas_call(
        body, name=name, out_shape=jax.ShapeDtypeStruct((T, na + nb), BF16), grid=(T // tr,),
        in_specs=[row(na), row(nb), vec(na), vec(nb)], out_specs=row(na + nb),
        compiler_params=_cparams(("parallel",)),
    )(xa, xb, ga, gb)


def _rms_bwd(dy, dy_blk, x, x_blk, n, g, name, out_dtype=BF16):
    T = x.shape[0]
    tr = 512

    def body(dy_ref, x_ref, g_ref, dx_ref, dg_ref):
        xv = x_ref[...]
        dyv = dy_ref[...].astype(F32)
        r = lax.rsqrt(jnp.mean(xv * xv, axis=-1, keepdims=True) + EPS)
        nrm = xv * r
        dn = dyv * g_ref[...]
        dx_ref[...] = (r * (dn - nrm * jnp.mean(dn * nrm, axis=-1, keepdims=True))).astype(dx_ref.dtype)

        @pl.when(pl.program_id(0) == 0)
        def _():
            dg_ref[...] = jnp.zeros_like(dg_ref)

        dg_ref[...] += jnp.sum(dyv * nrm, axis=0, keepdims=True)

    return pl.pallas_call(
        body, name=name,
        out_shape=[jax.ShapeDtypeStruct((T, n), out_dtype), jax.ShapeDtypeStruct((1, n), F32)],
        grid=(T // tr,),
        in_specs=[pl.BlockSpec((tr, n), lambda i: (i, dy_blk)), pl.BlockSpec((tr, n), lambda i: (i, x_blk)),
                  pl.BlockSpec((1, n), lambda i: (0, 0))],
        out_specs=[pl.BlockSpec((tr, n), lambda i: (i, 0)), pl.BlockSpec((1, n), lambda i: (0, 0))],
        compiler_params=_cparams(("arbitrary",)),
    )(dy, x, g)


def _rms_bwd_views(dy, dy_blk, x, g, name):
    B, S, n = x.shape
    tiles = S // VIEW_TILE

    def body(dy_ref, x_ref, g_ref, d1_ref, d4_ref, d16_ref, dg_ref, dx_s):
        xv = x_ref[0]
        dyv = dy_ref[...]
        r = lax.rsqrt(jnp.mean(xv * xv, axis=-1, keepdims=True) + EPS)
        nrm = xv * r
        dn = dyv * g_ref[...]
        dx = r * (dn - nrm * jnp.mean(dn * nrm, axis=-1, keepdims=True))
        d1_ref[0] = dx.astype(d1_ref.dtype)
        _put_tile(dx_s, dx)
        _tile_to_view(dx_s, d4_ref, DILATIONS[1], n)
        _tile_to_view(dx_s, d16_ref, DILATIONS[2], n)

        @pl.when((pl.program_id(0) == 0) & (pl.program_id(1) == 0))
        def _():
            dg_ref[...] = jnp.zeros_like(dg_ref)

        dg_ref[...] += jnp.sum(dyv * nrm, axis=0, keepdims=True)

    res = pl.pallas_call(
        body, name=name,
        out_shape=[_view_shape(B, S, d, n, BF16) for d in DILATIONS] + [jax.ShapeDtypeStruct((1, n), F32)],
        grid=(B, tiles),
        in_specs=[pl.BlockSpec((VIEW_TILE, n), lambda b, t: (b * tiles + t, dy_blk)), _view_spec(1, n),
                  pl.BlockSpec((1, n), lambda b, t: (0, 0))],
        out_specs=[_view_spec(d, n) for d in DILATIONS] + [pl.BlockSpec((1, n), lambda b, t: (0, 0))],
        scratch_shapes=[_tile_scratch(n)],
        compiler_params=_cparams(("arbitrary", "arbitrary")),
    )(dy, x, g)
    return res[:len(DILATIONS)], res[len(DILATIONS)]


FFN_TILE = 1408


def _ffn_in_fwd(h, w4, name):
    T, D = h.shape
    tm, tc = 512, FFN_TILE
    nc = D_FF // tc

    def body(h_ref, wg_ref, wu_ref, gu_ref, act_ref):
        hv = h_ref[...]
        g = jnp.dot(hv, wg_ref[...], preferred_element_type=F32)
        u = jnp.dot(hv, wu_ref[...], preferred_element_type=F32)
        gu_ref[0] = g.astype(gu_ref.dtype)
        gu_ref[1] = u.astype(gu_ref.dtype)
        act_ref[...] = (g * jax.nn.sigmoid(g) * u).astype(act_ref.dtype)

    return pl.pallas_call(
        body, name=name,
        out_shape=[jax.ShapeDtypeStruct((2, T, D_FF), BF16), jax.ShapeDtypeStruct((T, D_FF), BF16)],
        grid=(nc, T // tm),
        in_specs=[pl.BlockSpec((tm, D), lambda j, i: (i, 0)),
                  pl.BlockSpec((None, D, tc), lambda j, i: (j, 0, 0)),
                  pl.BlockSpec((None, D, tc), lambda j, i: (j + nc, 0, 0))],
        out_specs=[pl.BlockSpec((2, tm, tc), lambda j, i: (0, i, j)), pl.BlockSpec((tm, tc), lambda j, i: (i, j))],
        compiler_params=_cparams(("parallel", "parallel")),
    )(h, w4, w4)


def _ffn_out_bwd(df, w_out, gu, name):
    T, D = df.shape
    tm, tc = 512, FFN_TILE

    def body(df_ref, w_ref, gu_ref, dgu_ref):
        da = _dot_nt(df_ref[...], w_ref[...])
        g, u = gu_ref[0].astype(F32), gu_ref[1].astype(F32)
        sg = jax.nn.sigmoid(g)
        dgu_ref[0] = (da * u * (sg * (1.0 + g * (1.0 - sg)))).astype(dgu_ref.dtype)
        dgu_ref[1] = (da * (g * sg)).astype(dgu_ref.dtype)

    halves = pl.BlockSpec((2, tm, tc), lambda j, i: (0, i, j))
    return pl.pallas_call(
        body, name=name, out_shape=jax.ShapeDtypeStruct((2, T, D_FF), BF16), grid=(D_FF // tc, T // tm),
        in_specs=[pl.BlockSpec((tm, D), lambda j, i: (i, 0)), pl.BlockSpec((tc, D), lambda j, i: (j, 0)), halves],
        out_specs=halves,
        compiler_params=_cparams(("parallel", "parallel")),
    )(df, w_out, gu)


def _final_loss(x1, f, g2, gf, target, name):
    B, S, D = x1.shape
    ts = ROW_TILE

    def body(x1_ref, f_ref, g2_ref, gf_ref, t_ref, dx_ref, df_ref, dg2_ref, dgf_ref, loss_ref):
        b, s = pl.program_id(0), pl.program_id(1)
        fv = f_ref[0]
        g2v = g2_ref[0]
        gfv = gf_ref[...]
        x2 = x1_ref[0] + g2v * fv
        r = lax.rsqrt(jnp.mean(x2 * x2, axis=-1, keepdims=True) + EPS)
        n = x2 * r
        e = n * gfv - t_ref[0]
        dy = e * (1.0 / D)
        dn = dy * gfv
        dx = r * (dn - n * jnp.mean(dn * n, axis=-1, keepdims=True))
        dx_ref[0] = dx
        df_ref[0] = (dx * g2v).astype(df_ref.dtype)

        @pl.when(s == 0)
        def _():
            dg2_ref[...] = jnp.zeros_like(dg2_ref)

        @pl.when((s == 0) & (b == 0))
        def _():
            dgf_ref[...] = jnp.zeros_like(dgf_ref)
            loss_ref[...] = jnp.zeros_like(loss_ref)

        dg2_ref[0] += jnp.sum(dx * fv, axis=0, keepdims=True)
        dgf_ref[...] += jnp.sum(dy * n, axis=0, keepdims=True)
        loss_ref[...] += 0.5 * jnp.sum(jnp.mean(e * e, axis=-1, keepdims=True), axis=0, keepdims=True)

    tok = pl.BlockSpec((1, ts, D), lambda b, s: (b, s, 0))
    per_b = pl.BlockSpec((1, 1, D), lambda b, s: (b, 0, 0))
    vec = pl.BlockSpec((1, D), lambda b, s: (0, 0))
    return pl.pallas_call(
        body, name=name,
        out_shape=[jax.ShapeDtypeStruct((B, S, D), F32), jax.ShapeDtypeStruct((B, S, D), BF16),
                   jax.ShapeDtypeStruct((B, 1, D), F32), jax.ShapeDtypeStruct((1, D), F32),
                   jax.ShapeDtypeStruct((1, LANES), F32)],
        grid=(B, S // ts),
        in_specs=[tok, tok, per_b, vec, tok],
        out_specs=[tok, tok, per_b, vec, pl.BlockSpec((1, LANES), lambda b, s: (0, 0))],
        compiler_params=_cparams(("arbitrary", "arbitrary")),
    )(x1, f, g2, gf, target)


def _rope_tables():
    half = ROPE_DIM // 2
    inv = ROPE_THETA ** (-jnp.arange(half, dtype=F32) / half)
    ang = jnp.arange(SEQ, dtype=F32)[:, None] * inv[None, :]
    cos, sin = jnp.cos(ang), jnp.sin(ang)
    one = jnp.ones((SEQ, NOPE_DIM), F32)
    zero = jnp.zeros((SEQ, NOPE_DIM), F32)
    cs = jnp.concatenate([one, cos, cos, one[:, :LANES - NOPE_DIM - ROPE_DIM]], axis=1)
    sn = jnp.concatenate([zero, -sin, sin, zero[:, :LANES - NOPE_DIM - ROPE_DIM]], axis=1)
    return cs, sn


def _rope_group(t, cs, sn):
    half = ROPE_DIM // 2
    lane = lax.broadcasted_iota(jnp.int32, t.shape, 1)
    partner = jnp.where(lane < NOPE_DIM + half, pltpu.roll(t, LANES - half, 1), pltpu.roll(t, half, 1))
    return t * cs + partner * sn


def _rope_apply(t, cs, sn, out_dtype, name, add=None, add_blk=0):
    B, S, W = t.shape
    G = W // LANES
    ts = ROW_TILE

    def body(*refs):
        if add is None:
            t_ref, cs_ref, sn_ref, o_ref = refs
            for gi in range(G):
                sl = slice(gi * LANES, (gi + 1) * LANES)
                o_ref[0, :, sl] = _rope_group(t_ref[0, :, sl], cs_ref[...], sn_ref[...]).astype(o_ref.dtype)
        else:
            t_ref, a_ref, cs_ref, sn_ref, o_ref = refs
            ra = _rope_group(a_ref[0], cs_ref[...], sn_ref[...])
            for gi in range(G):
                sl = slice(gi * LANES, (gi + 1) * LANES)
                o_ref[0, :, sl] = (t_ref[0, :, sl] + ra).astype(o_ref.dtype)

    tok = pl.BlockSpec((1, ts, W), lambda b, s: (b, s, 0))
    tab = pl.BlockSpec((ts, LANES), lambda b, s: (s, 0))
    in_specs, args = [tok], [t]
    if add is not None:
        in_specs.append(pl.BlockSpec((1, ts, LANES), lambda b, s: (b, s, add_blk)))
        args.append(add)
    in_specs += [tab, tab]
    args += [cs, sn]
    return pl.pallas_call(
        body, name=name, out_shape=jax.ShapeDtypeStruct((B, S, W), out_dtype), grid=(B, S // ts),
        in_specs=in_specs, out_specs=tok, compiler_params=_cparams(("parallel", "parallel")),
    )(*args)


def _qrope_bwd(dq_t, cs, sn_neg, name):
    B, npair, nq, _, tq = dq_t.shape

    def body(d_ref, cs_ref, sn_ref, o_ref):
        for p in range(npair):
            tile = jnp.transpose(d_ref[0, p, 0])
            for hh in range(2):
                lo = (2 * p + hh) * LANES
                o_ref[0, :, lo:lo + LANES] = _rope_group(tile[:, hh * LANES:(hh + 1) * LANES], cs_ref[...],
                                                         sn_ref[...]).astype(o_ref.dtype)

    tab = pl.BlockSpec((tq, LANES), lambda b, i: (i, 0))
    return pl.pallas_call(
        body, name=name, out_shape=jax.ShapeDtypeStruct((B, nq * tq, N_HEADS * LANES), BF16), grid=(B, nq),
        in_specs=[pl.BlockSpec((1, npair, 1, 2 * LANES, tq), lambda b, i: (b, 0, i, 0, 0)), tab, tab],
        out_specs=pl.BlockSpec((1, tq, N_HEADS * LANES), lambda b, i: (b, i, 0)),
        compiler_params=_cparams(("parallel", "parallel")),
    )(dq_t, cs, sn_neg)


def _krope_bwd(dkc, cs, sn_neg, name):
    B, S, W = dkc.shape
    G = W // LANES
    ts = ROW_TILE

    def body(d_ref, cs_ref, sn_ref, o_ref):
        acc = d_ref[0, :, 0:LANES]
        for gi in range(1, G):
            acc = acc + d_ref[0, :, gi * LANES:(gi + 1) * LANES]
        lane = lax.broadcasted_iota(jnp.int32, acc.shape, 1)
        rot = (lane >= NOPE_DIM) & (lane < NOPE_DIM + ROPE_DIM)
        acc = jnp.where(rot, acc, 0.0)
        o_ref[0] = _rope_group(acc, cs_ref[...], sn_ref[...]).astype(o_ref.dtype)

    tab = pl.BlockSpec((ts, LANES), lambda b, s: (s, 0))
    return pl.pallas_call(
        body, name=name, out_shape=jax.ShapeDtypeStruct((B, S, LANES), BF16), grid=(B, S // ts),
        in_specs=[pl.BlockSpec((1, ts, W), lambda b, s: (b, s, 0)), tab, tab],
        out_specs=pl.BlockSpec((1, ts, LANES), lambda b, s: (b, s, 0)),
        compiler_params=_cparams(("parallel", "parallel")),
    )(dkc, cs, sn_neg)


def _t5_bucket(dist):
    max_exact = N_BUCKETS // 2
    d = np.maximum(dist, 1).astype(np.float64)
    large = max_exact + (np.log(d / max_exact) / np.log(MAX_DISTANCE / max_exact)
                         * (N_BUCKETS - max_exact)).astype(np.int64)
    large = np.minimum(large, N_BUCKETS - 1)
    return np.where(dist < max_exact, dist, large).astype(np.int32)


def _band_buckets(dilation):
    a = np.arange(BLK)[None, :]
    bk = np.arange(2 * BLK)[:, None]
    steps = BLK + a - bk
    return _t5_bucket(np.clip(steps, 0, SPAN) * dilation)


def _head_mask(shape, hh):
    lane = lax.broadcasted_iota(jnp.int32, shape, 1)
    return (lane >= hh * HEAD_DIM) & (lane < (hh + 1) * HEAD_DIM)


def _dot_nt(a, b):
    return lax.dot_general(a, b, (((1,), (1,)), ((), ())), preferred_element_type=F32)


def _dot_tn(a, b):
    return lax.dot_general(a, b, (((0,), (0,)), ((), ())), preferred_element_type=F32)


def _dot_nn(a, b):
    return lax.dot_general(a, b, (((1,), (0,)), ((), ())), preferred_element_type=F32)


def _dil_fwd(qkv, bias, branch, dilation, name, comm=None):
    B, n, _ = qkv.shape
    d = dilation
    nb = n // BLK
    qkv_v = qkv
    npair = N_HEADS // 2

    def body(cur_ref, prev_ref, bias_ref, o_ref, lse_ref, s_scr, e_scr):
        first = jnp.where(pl.program_id(1) == 0, 1, 0)
        units = [(b, h) for b in range(B) for h in range(N_HEADS)]
        for b in range(B):
            for p in range(npair):
                q = cur_ref[b, :, p * LANES:(p + 1) * LANES] * DIL_SCALE
                kc = cur_ref[b, :, D_A + p * LANES:D_A + (p + 1) * LANES]
                kp = prev_ref[b, :, D_A + p * LANES:D_A + (p + 1) * LANES]
                for hh in range(2):
                    u = b * N_HEADS + 2 * p + hh
                    qm = jnp.where(_head_mask((BLK, LANES), hh), q, jnp.zeros_like(q))
                    s_scr[u, 0:BLK, :] = _dot_nt(kp, qm)
                    s_scr[u, BLK:2 * BLK, :] = _dot_nt(kc, qm)
        ms = []
        for u, (b, h) in enumerate(units):
            s_p = s_scr[u, 0:BLK, :] + bias_ref[first, h, 0:BLK, :]
            s_c = s_scr[u, BLK:2 * BLK, :] + bias_ref[first, h, BLK:2 * BLK, :]
            m = jnp.maximum(jnp.max(s_p, axis=0, keepdims=True), jnp.max(s_c, axis=0, keepdims=True))
            e_scr[u, 0:BLK, :] = jnp.exp(s_p - m).astype(BF16)
            e_scr[u, BLK:2 * BLK, :] = jnp.exp(s_c - m).astype(BF16)
            ms.append(m)
        rows0 = _row_mask((LANES, BLK), 0)
        for b in range(B):
            for p in range(npair):
                sl = slice(p * LANES, (p + 1) * LANES)
                vsl = slice(2 * D_A + p * LANES, 2 * D_A + (p + 1) * LANES)
                vct = jnp.transpose(cur_ref[b, :, vsl].astype(F32)).astype(BF16)
                vpt = jnp.transpose(prev_ref[b, :, vsl].astype(F32)).astype(BF16)
                acc = []
                for hh in range(2):
                    u = b * N_HEADS + 2 * p + hh
                    mine = _row_mask((LANES, BLK), hh)
                    one = jnp.ones_like(vct)
                    acc.append(_dot_nn(jnp.where(mine, vpt, one), e_scr[u, 0:BLK, :])
                               + _dot_nn(jnp.where(mine, vct, one), e_scr[u, BLK:2 * BLK, :]))
                l0 = acc[0][HEAD_DIM:HEAD_DIM + 1, :]
                l1 = acc[1][0:1, :]
                u0 = b * N_HEADS + 2 * p
                o_t = jnp.where(rows0, acc[0] / l0, acc[1] / l1)
                lse_t = jnp.where(rows0, ms[u0] + jnp.log(l0), ms[u0 + 1] + jnp.log(l1))
                o_ref[b, :, sl] = jnp.transpose(o_t)
                lse_ref[b, :, sl] = jnp.transpose(lse_t)

    cur = pl.BlockSpec((B, BLK, P_QKV), lambda r, i: (0, i, r))
    prev = pl.BlockSpec((B, BLK, P_QKV), lambda r, i: (0, jnp.maximum(i - 1, 0), r))
    out = pl.BlockSpec((B, BLK, D_A), lambda r, i: (0, i, r))
    return _host_call(
        body, comm, name=name,
        out_shape=[jax.ShapeDtypeStruct((B, n, d * D_A), F32)] * 2,
        grid=(d, nb),
        in_specs=[cur, prev,
                  pl.BlockSpec((None, 2, N_HEADS, 2 * BLK, BLK), lambda r, i: (branch, 0, 0, 0, 0))],
        out_specs=[out, out],
        scratch_shapes=[pltpu.VMEM((B * N_HEADS, 2 * BLK, BLK), F32),
                        pltpu.VMEM((B * N_HEADS, 2 * BLK, BLK), BF16)],
        args=(qkv_v, qkv_v, bias))


VIEW_TILE = 512


def _view_spec(d, w):
    return pl.BlockSpec((1, VIEW_TILE // d, d * w), lambda b, t: (b, t, 0))


def _view_shape(B, S, d, w, dtype):
    return jax.ShapeDtypeStruct((B, S // d, d * w), dtype)


def _tile_scratch(w):
    return pltpu.VMEM((w // LANES, VIEW_TILE, LANES), F32)


def _put_tile(tile_ref, val):
    for c in range(tile_ref.shape[0]):
        tile_ref[c] = val[:, c * LANES:(c + 1) * LANES]


def _get_tile(tile_ref):
    return jnp.concatenate([tile_ref[c] for c in range(tile_ref.shape[0])], axis=1)


def _tile_to_view(tile_ref, view_ref, d, w):
    for c in range(w // LANES):
        for r in range(d):
            lo = r * w + c * LANES
            rows = tile_ref.at[c][pl.ds(r, VIEW_TILE // d, stride=d), :]
            view_ref[0, :, lo:lo + LANES] = rows.astype(view_ref.dtype)


def _view_to_tile(view_ref, tile_ref, d, w):
    for c in range(w // LANES):
        for r in range(d):
            lo = r * w + c * LANES
            tile_ref.at[c][pl.ds(r, VIEW_TILE // d, stride=d), :] = view_ref[0, :, lo:lo + LANES].astype(F32)


def _mm_qkv_views(h, w, name):
    B, S, D = h.shape
    N = w.shape[1]

    def body(h_ref, w_ref, o1_ref, o4_ref, o16_ref, acc_ref):
        acc = jnp.dot(h_ref[0], w_ref[...], preferred_element_type=F32)
        o1_ref[0] = acc.astype(o1_ref.dtype)
        _put_tile(acc_ref, acc)
        _tile_to_view(acc_ref, o4_ref, DILATIONS[1], N)
        _tile_to_view(acc_ref, o16_ref, DILATIONS[2], N)

    return pl.pallas_call(
        body, name=name,
        out_shape=[_view_shape(B, S, d, N, BF16) for d in DILATIONS],
        grid=(B, S // VIEW_TILE),
        in_specs=[pl.BlockSpec((1, VIEW_TILE, D), lambda b, t: (b, t, 0)), pl.BlockSpec((D, N), lambda b, t: (0, 0))],
        out_specs=[_view_spec(d, N) for d in DILATIONS],
        scratch_shapes=[_tile_scratch(N)],
        compiler_params=_cparams(("parallel", "parallel")),
    )(h, w)


def _dil_merge(os_, lses, name):
    B, S, W = os_[0].shape
    nd = len(DILATIONS)

    def body(*refs):
        o_refs, l_refs = refs[:nd], refs[nd:2 * nd]
        out_refs, L_refs = refs[2 * nd:3 * nd], refs[3 * nd:4 * nd]
        scr = refs[4 * nd:]
        o_tok, l_tok = [o_refs[0][0]], [l_refs[0][0]]
        for i, d in enumerate(DILATIONS[1:]):
            _view_to_tile(o_refs[i + 1], scr[2 * i], d, W)
            _view_to_tile(l_refs[i + 1], scr[2 * i + 1], d, W)
            o_tok.append(_get_tile(scr[2 * i]))
            l_tok.append(_get_tile(scr[2 * i + 1]))
        a0, a1, a2 = l_tok
        m = jnp.maximum(jnp.maximum(a0, a1), a2)
        e0, e1, e2 = jnp.exp(a0 - m), jnp.exp(a1 - m), jnp.exp(a2 - m)
        ssum = e0 + e1 + e2
        out = (e0 * o_tok[0] + e1 * o_tok[1] + e2 * o_tok[2]) / ssum
        lse = m + jnp.log(ssum)
        out_refs[0][0] = out
        L_refs[0][0] = lse
        res_o, res_l = scr[2 * (nd - 1)], scr[2 * (nd - 1) + 1]
        _put_tile(res_o, out)
        _put_tile(res_l, lse)
        for i, d in enumerate(DILATIONS[1:]):
            _tile_to_view(res_o, out_refs[i + 1], d, W)
            _tile_to_view(res_l, L_refs[i + 1], d, W)

    specs = [_view_spec(d, W) for d in DILATIONS]
    shapes = [_view_shape(B, S * DILATIONS[0], d, W, F32) for d in DILATIONS]
    res = pl.pallas_call(
        body, name=name, out_shape=shapes * 2, grid=(B, S // VIEW_TILE),
        in_specs=specs * 2, out_specs=specs * 2,
        scratch_shapes=[_tile_scratch(W)] * (2 * nd),
        compiler_params=_cparams(("parallel", "parallel")),
    )(*os_, *lses)
    return res[:nd], res[nd:]


def _dil_bwd(qkv, do, out_a, L, bias, branch, dilation, name, comm=None):
    B, n, _ = qkv.shape
    d = dilation
    nb = n // BLK
    qkv_v, do_v, oa_v, L_v = qkv, do, out_a, L
    npair = N_HEADS // 2
    multi = nb > 1

    tiles = ("P", "C", "N") if multi else ("C",)
    n_t = len(tiles)

    def body(*refs):
        if multi:
            (cur_ref, prev_ref, next_ref, do_ref, don_ref, oa_ref, oan_ref, L_ref, Ln_ref, bias_ref,
             dqkv_ref, dbias_ref, s_scr, dp_scr, p_scr, ds_scr) = refs
        else:
            cur_ref, do_ref, oa_ref, L_ref, bias_ref, dqkv_ref, dbias_ref, s_scr, dp_scr, p_scr, ds_scr = refs
        r, i = pl.program_id(0), pl.program_id(1)

        @pl.when((r == 0) & (i == 0))
        def _():
            dbias_ref[...] = jnp.zeros_like(dbias_ref)

        first = jnp.where(i == 0, 1, 0)
        variant = {"P": first, "C": first, "N": 0}
        band = {"P": slice(0, BLK), "C": slice(BLK, 2 * BLK), "N": slice(0, BLK)}
        psl = lambda p: slice(p * LANES, (p + 1) * LANES)
        ksl = lambda p: slice(D_A + p * LANES, D_A + (p + 1) * LANES)
        vsl = lambda p: slice(2 * D_A + p * LANES, 2 * D_A + (p + 1) * LANES)

        def operands(b, p, hh):
            hm = _head_mask((BLK, LANES), hh)
            mask = lambda x: jnp.where(hm, x, jnp.zeros_like(x))
            qm, dom = mask(cur_ref[b, :, psl(p)] * DIL_SCALE), mask(do_ref[b, :, psl(p)])
            ops = {"C": (cur_ref[b, :, ksl(p)], cur_ref[b, :, vsl(p)], qm, dom)}
            if multi:
                ops["P"] = (prev_ref[b, :, ksl(p)], prev_ref[b, :, vsl(p)], qm, dom)
                ops["N"] = (cur_ref[b, :, ksl(p)], cur_ref[b, :, vsl(p)],
                            mask(next_ref[b, :, psl(p)] * DIL_SCALE), mask(don_ref[b, :, psl(p)]))
            return ops

        pairs = [(b, p) for b in range(B) for p in range(npair)]
        for b, p in pairs:
            for hh in range(2):
                u = b * N_HEADS + 2 * p + hh
                ops = operands(b, p, hh)
                for t, name_t in enumerate(tiles):
                    k_t, v_t, q_t, do_t = ops[name_t]
                    s_scr[u, t] = _dot_nt(k_t, q_t)
                    dp_scr[u, t] = _dot_nt(v_t, do_t)

        def rows(L_r, do_r, oa_r, b, p):
            lt = jnp.transpose(L_r[b, :, psl(p)])
            dt = jnp.transpose(do_r[b, :, psl(p)].astype(F32) * oa_r[b, :, psl(p)])
            return ([lt[0:1, :], lt[HEAD_DIM:HEAD_DIM + 1, :]],
                    [jnp.sum(dt[:HEAD_DIM], axis=0, keepdims=True), jnp.sum(dt[HEAD_DIM:], axis=0, keepdims=True)])

        for b, p in pairs:
            lse_c, delta_c = rows(L_ref, do_ref, oa_ref, b, p)
            if multi:
                lse_n, delta_n = rows(Ln_ref, don_ref, oan_ref, b, p)
            for hh in range(2):
                h = 2 * p + hh
                u = b * N_HEADS + h
                for t, name_t in enumerate(tiles):
                    lse, delta = (lse_n[hh], delta_n[hh]) if name_t == "N" else (lse_c[hh], delta_c[hh])
                    pr = jnp.exp(s_scr[u, t] + bias_ref[variant[name_t], h, band[name_t], :] - lse)
                    if name_t == "N":
                        pr = jnp.where(i < nb - 1, pr, 0.0)
                    ds = pr * (dp_scr[u, t] - delta)
                    p_scr[u, t] = pr.astype(BF16)
                    ds_scr[u, t] = ds.astype(BF16)
                    if name_t != "N":
                        dbias_ref[h, band[name_t], :] += ds

        for b, p in pairs:
            dqt = jnp.zeros((LANES, BLK), F32)
            dk = jnp.zeros((BLK, LANES), F32)
            dv = jnp.zeros((BLK, LANES), F32)
            kct = jnp.transpose(cur_ref[b, :, ksl(p)].astype(F32)).astype(BF16)
            if multi:
                kpt = jnp.transpose(prev_ref[b, :, ksl(p)].astype(F32)).astype(BF16)
            for hh in range(2):
                u = b * N_HEADS + 2 * p + hh
                ops = operands(b, p, hh)
                mine = _row_mask((LANES, BLK), hh)
                for t, name_t in enumerate(tiles):
                    _, _, q_t, do_t = ops[name_t]
                    if name_t != "P":
                        dv = dv + _dot_nn(p_scr[u, t], do_t)
                        dk = dk + _dot_nn(ds_scr[u, t], q_t)
                    if name_t != "N":
                        kt = kpt if name_t == "P" else kct
                        dqt = dqt + _dot_nn(jnp.where(mine, kt, jnp.zeros_like(kt)), ds_scr[u, t])
            dqkv_ref[b, :, psl(p)] = jnp.transpose(dqt) * DIL_SCALE
            dqkv_ref[b, :, ksl(p)] = dk
            dqkv_ref[b, :, vsl(p)] = dv

    def at(off):
        return lambda r, i: (0, jnp.clip(i + off, 0, nb - 1), r)

    qkv_spec = lambda off: pl.BlockSpec((B, BLK, P_QKV), at(off))
    da_spec = lambda off: pl.BlockSpec((B, BLK, D_A), at(off))
    bias_spec = pl.BlockSpec((None, 2, N_HEADS, 2 * BLK, BLK), lambda r, i: (branch, 0, 0, 0, 0))
    dbias_spec = pl.BlockSpec((N_HEADS, 2 * BLK, BLK), lambda r, i: (0, 0, 0))
    if multi:
        in_specs = [qkv_spec(0), qkv_spec(-1), qkv_spec(1), da_spec(0), da_spec(1), da_spec(0), da_spec(1),
                    da_spec(0), da_spec(1), bias_spec]
        args = [qkv_v, qkv_v, qkv_v, do_v, do_v, oa_v, oa_v, L_v, L_v, bias]
    else:
        in_specs = [qkv_spec(0), da_spec(0), da_spec(0), da_spec(0), bias_spec]
        args = [qkv_v, do_v, oa_v, L_v, bias]
    return _host_call(
        body, comm, name=name,
        out_shape=[jax.ShapeDtypeStruct((B, n, d * P_QKV), F32),
                   jax.ShapeDtypeStruct((N_HEADS, 2 * BLK, BLK), F32)],
        grid=(d, nb),
        in_specs=in_specs,
        out_specs=[qkv_spec(0), dbias_spec],
        scratch_shapes=[pltpu.VMEM((B * N_HEADS, n_t, BLK, BLK), F32), pltpu.VMEM((B * N_HEADS, n_t, BLK, BLK), F32),
                        pltpu.VMEM((B * N_HEADS, n_t, BLK, BLK), BF16),
                        pltpu.VMEM((B * N_HEADS, n_t, BLK, BLK), BF16)],
        args=args)


def _sum_views_bf16(parts, name):
    B, S, W = parts[0].shape

    def body(a_ref, b_ref, c_ref, o_ref, sb, sc):
        _view_to_tile(b_ref, sb, DILATIONS[1], W)
        _view_to_tile(c_ref, sc, DILATIONS[2], W)
        o_ref[0] = (a_ref[0] + _get_tile(sb) + _get_tile(sc)).astype(o_ref.dtype)

    return pl.pallas_call(
        body, name=name, out_shape=jax.ShapeDtypeStruct((B, S, W), BF16), grid=(B, S // VIEW_TILE),
        in_specs=[_view_spec(d, W) for d in DILATIONS], out_specs=_view_spec(1, W),
        scratch_shapes=[_tile_scratch(W)] * 2,
        compiler_params=_cparams(("parallel", "parallel")),
    )(*parts)


def _bias_tables(rel_bias, buckets, name, comm=None):
    nbr = buckets.shape[0]

    def body(rb_ref, bk_ref, o_ref):
        first, h = pl.program_id(1), pl.program_id(2)
        tab = bk_ref[0]

        def step(bkt, acc):
            return jnp.where(tab == bkt, rb_ref[bkt, h], acc)

        bias = lax.fori_loop(0, N_BUCKETS, step, jnp.zeros((2 * BLK, BLK), F32))
        row = lax.broadcasted_iota(jnp.int32, (2 * BLK, BLK), 0)
        col = lax.broadcasted_iota(jnp.int32, (2 * BLK, BLK), 1)
        valid = ((row < BLK) & (row >= col) & (first == 0)) | ((row >= BLK) & (row - BLK <= col))
        o_ref[0, 0, 0] = jnp.where(valid, bias, NEG)

    (bias,), got = _host_call(
        body, comm, name=name, out_shape=[jax.ShapeDtypeStruct((nbr, 2, N_HEADS, 2 * BLK, BLK), F32)],
        grid=(nbr, 2, N_HEADS),
        in_specs=[pl.BlockSpec(memory_space=pltpu.SMEM),
                  pl.BlockSpec((1, 2 * BLK, BLK), lambda i, f, h: (i, 0, 0))],
        out_specs=[pl.BlockSpec((1, 1, 1, 2 * BLK, BLK), lambda i, f, h: (i, f, h, 0, 0))],
        scratch_shapes=[], args=(rel_bias, buckets))
    return bias, got


def _bias_grad(dbias_list, buckets, name):
    nbr = len(dbias_list)

    def body(*refs):
        d_refs, bk_ref, o_ref, part = refs[:nbr], refs[nbr], refs[nbr + 1], refs[nbr + 2]

        def step(bkt, carry):
            hit = [bk_ref[bi] == bkt for bi in range(nbr)]
            for h in range(N_HEADS):
                tot = jnp.zeros((1, BLK), F32)
                for bi in range(nbr):
                    tot = tot + jnp.sum(jnp.where(hit[bi], d_refs[bi][h], 0.0), axis=0, keepdims=True)
                part[bkt, h:h + 1, :] = tot
            return carry

        lax.fori_loop(0, N_BUCKETS, step, 0)
        lane = lax.broadcasted_iota(jnp.int32, (N_HEADS, LANES), 1)
        acc = jnp.zeros((N_HEADS, LANES), F32)
        for bkt in range(N_BUCKETS):
            acc = acc + jnp.where(lane == bkt, jnp.sum(part[bkt], axis=1, keepdims=True), 0.0)
        o_ref[...] = acc

    band = pl.BlockSpec((N_HEADS, 2 * BLK, BLK), lambda i: (0, 0, 0))
    return pl.pallas_call(
        body, name=name, out_shape=jax.ShapeDtypeStruct((N_HEADS, LANES), F32), grid=(1,),
        in_specs=[band] * nbr + [pl.BlockSpec((nbr, 2 * BLK, BLK), lambda i: (0, 0, 0))],
        out_specs=pl.BlockSpec((N_HEADS, LANES), lambda i: (0, 0)),
        scratch_shapes=[pltpu.VMEM((N_BUCKETS, N_HEADS, BLK), F32)],
        compiler_params=_cparams(("arbitrary",)),
    )(*dbias_list, buckets)


MLA_TQ = 256
MLA_TK = 256


LOG2E = math.log2(math.e)
MLA_C = MLA_SCALE * LOG2E


def _key_le_query(tk, tq):
    return lax.broadcasted_iota(jnp.int32, (tk, tq), 0) <= lax.broadcasted_iota(jnp.int32, (tk, tq), 1)


def _row_mask(shape, hh):
    row = lax.broadcasted_iota(jnp.int32, shape, 0)
    return (row >= hh * HEAD_DIM) & (row < (hh + 1) * HEAD_DIM)


def _host_call(body, comm, *, name, grid, in_specs, out_specs, out_shape, scratch_shapes, args):
    sem = ("arbitrary",) * len(grid)
    if comm is None:
        res = pl.pallas_call(body, name=name, grid=grid, in_specs=in_specs, out_specs=out_specs,
                             out_shape=out_shape, scratch_shapes=scratch_shapes,
                             compiler_params=_cparams(sem))(*args)
        return res, []
    n_in, n_out, n_s, cn = len(in_specs), len(out_specs), len(scratch_shapes), comm.n

    def hosted(*refs):
        ins, refs = refs[:n_in], refs[n_in:]
        c_ins, refs = refs[:cn], refs[cn:]
        outs, refs = refs[:n_out], refs[n_out:]
        c_outs, refs = refs[:cn], refs[cn:]
        scr, c_sems = refs[:n_s], refs[n_s:]
        ids = [pl.program_id(a) for a in range(len(grid))]
        first = functools.reduce(jnp.logical_and, [i == 0 for i in ids])
        last = functools.reduce(jnp.logical_and, [i == g - 1 for i, g in zip(ids, grid)])

        @pl.when(first)
        def _():
            comm.start(c_ins, c_outs, c_sems)

        body(*ins, *outs, *scr)

        @pl.when(last)
        def _():
            comm.finish(c_ins, c_outs, c_sems)

    res = pl.pallas_call(
        hosted, name=name, grid=grid, in_specs=list(in_specs) + _hbm_specs(cn),
        out_specs=list(out_specs) + _hbm_specs(cn), out_shape=list(out_shape) + list(comm.out_shape),
        scratch_shapes=list(scratch_shapes) + list(comm.scratch), compiler_params=_cparams(sem),
    )(*args, *comm.inputs)
    return res[:n_out], res[n_out:]


def _mla_fwd_t(q, k, vt, name, comm=None):
    B, S, _ = q.shape
    tq, tk = MLA_TQ, MLA_TK
    assert tq == tk
    npair = N_HEADS // 2
    nq = S // tq

    def body(q_ref, k_ref, vt_ref, o_ref, lse_ref, s_scr, e_scr, acc_scr, m_scr, a_scr):
        i = pl.program_id(1)
        diag = _key_le_query(tk, tq)
        m_scr[...] = jnp.full_like(m_scr, NEG)
        acc_scr[...] = jnp.zeros_like(acc_scr)

        def step(j, masked):
            rows = pl.ds(pl.multiple_of(j * tk, tk), tk)
            for h in range(N_HEADS):
                hsl = slice(h * LANES, (h + 1) * LANES)
                s_scr[h] = _dot_nt(k_ref[0, rows, hsl], q_ref[0, :, hsl])
            for h in range(N_HEADS):
                s = s_scr[h]
                if masked:
                    s = jnp.where(diag, s, NEG)
                m_old = m_scr[h:h + 1, :]
                m_new = jnp.maximum(m_old, jnp.max(s, axis=0, keepdims=True))
                a_scr[h:h + 1, :] = jnp.exp2((m_old - m_new) * MLA_C)
                e_scr[h] = jnp.exp2((s - m_new) * MLA_C).astype(BF16)
                m_scr[h:h + 1, :] = m_new
            for h in range(N_HEADS):
                vj = vt_ref[0, h // 2, j]
                vh = jnp.where(_row_mask(vj.shape, h % 2), vj, jnp.ones_like(vj))
                acc_scr[h] = acc_scr[h] * a_scr[h:h + 1, :] + _dot_nn(vh, e_scr[h])

        def loop_body(j, carry):
            step(j, False)
            return carry

        lax.fori_loop(0, i, loop_body, 0)
        step(i, True)
        rows0 = _row_mask((LANES, tq), 0)
        for p in range(npair):
            l0 = acc_scr[2 * p, HEAD_DIM:HEAD_DIM + 1, :]
            l1 = acc_scr[2 * p + 1, 0:1, :]
            o_t = jnp.where(rows0, acc_scr[2 * p] / l0, acc_scr[2 * p + 1] / l1)
            o_ref[0, :, p * LANES:(p + 1) * LANES] = jnp.transpose(o_t)
            lse_ref[0, p, 0] = jnp.zeros((8, tq), F32)
            lse_ref[0, p, 0, 0:1, :] = m_scr[2 * p:2 * p + 1, :] * MLA_C + jnp.log(l0) * LOG2E
            lse_ref[0, p, 0, 1:2, :] = m_scr[2 * p + 1:2 * p + 2, :] * MLA_C + jnp.log(l1) * LOG2E

    return _host_call(
        body, comm, name=name,
        out_shape=[jax.ShapeDtypeStruct((B, S, D_B), F32), jax.ShapeDtypeStruct((B, npair, nq, 8, tq), F32)],
        grid=(B, nq),
        in_specs=[pl.BlockSpec((1, tq, N_HEADS * LANES), lambda b, i: (b, i, 0)),
                  pl.BlockSpec((1, S, N_HEADS * LANES), lambda b, i: (b, 0, 0)),
                  pl.BlockSpec((1, npair, S // tk, LANES, tk), lambda b, i: (b, 0, 0, 0, 0))],
        out_specs=[pl.BlockSpec((1, tq, D_B), lambda b, i: (b, i, 0)),
                   pl.BlockSpec((1, npair, 1, 8, tq), lambda b, i: (b, 0, i, 0, 0))],
        scratch_shapes=[pltpu.VMEM((N_HEADS, tk, tq), F32), pltpu.VMEM((N_HEADS, tk, tq), BF16),
                        pltpu.VMEM((N_HEADS, LANES, tq), F32), pltpu.VMEM((N_HEADS, tq), F32),
                        pltpu.VMEM((N_HEADS, tq), F32)],
        args=(q, k, vt))


def _mla_delta(do, o, name):
    B, S, _ = o.shape
    tq = MLA_TQ
    npair = N_HEADS // 2

    def body(do_ref, o_ref, d_ref):
        d_ref[...] = jnp.zeros_like(d_ref)
        for p in range(npair):
            sl = slice(p * LANES, (p + 1) * LANES)
            prod_t = jnp.transpose(do_ref[0, :, sl].astype(F32) * o_ref[0, :, sl])
            d_ref[0, p, 0, 0:1, :] = jnp.sum(prod_t[:HEAD_DIM], axis=0, keepdims=True)
            d_ref[0, p, 0, 1:2, :] = jnp.sum(prod_t[HEAD_DIM:], axis=0, keepdims=True)

    tok = pl.BlockSpec((1, tq, D_B), lambda b, i: (b, i, 0))
    return pl.pallas_call(
        body, name=name, out_shape=jax.ShapeDtypeStruct((B, npair, S // tq, 8, tq), F32),
        grid=(B, S // tq), in_specs=[tok, tok],
        out_specs=pl.BlockSpec((1, npair, 1, 8, tq), lambda b, i: (b, 0, i, 0, 0)),
        compiler_params=_cparams(("parallel", "parallel")),
    )(do, o)


def _mla_bwd_t(q, k, v, do, lse, delta, name, comm=None):
    B, S, _ = q.shape
    tq, tk = MLA_TQ, MLA_TK
    assert tq == tk
    npair = N_HEADS // 2
    nq = S // tq

    hg = 4
    pg = hg // 2
    ngroup = N_HEADS // hg

    def body(q_ref, do_ref, lse_ref, dl_ref, k_ref, v_ref, dk_ref, dv_ref, dq_ref,
             s_scr, dp_scr, p_scr, ds_scr, dk_s, dv_s, kt_s):
        j = pl.program_id(2)

        @pl.when(j == 0)
        def _():
            dq_ref[...] = jnp.zeros_like(dq_ref)

        dk_s[...] = jnp.zeros_like(dk_s)
        dv_s[...] = jnp.zeros_like(dv_s)
        diag = _key_le_query(tk, tq)
        hsl = lambda h: slice(h * LANES, (h + 1) * LANES)
        for h in range(hg):
            kt_s[h] = jnp.transpose(k_ref[0, :, hsl(h)].astype(F32)).astype(BF16)

        def step(i, masked):
            rows = pl.ds(pl.multiple_of(i * tq, tq), tq)

            def dom(h):
                dov = do_ref[0, rows, hsl(h // 2)]
                return jnp.where(_head_mask((tq, LANES), h % 2), dov, jnp.zeros_like(dov))

            for h in range(hg):
                s_scr[h] = _dot_nt(k_ref[0, :, hsl(h)], q_ref[0, rows, hsl(h)])
                dp_scr[h] = _dot_nt(v_ref[0, :, hsl(h // 2)], dom(h))
            for h in range(hg):
                pr = jnp.exp2(s_scr[h] * MLA_C - lse_ref[0, h // 2, i, h % 2:h % 2 + 1, :])
                if masked:
                    pr = jnp.where(diag, pr, 0.0)
                p_scr[h] = pr.astype(BF16)
                ds_scr[h] = (pr * (dp_scr[h] - dl_ref[0, h // 2, i, h % 2:h % 2 + 1, :])).astype(BF16)
            for h in range(hg):
                dv_s[h // 2] += _dot_nn(p_scr[h], dom(h))
                dk_s[h] += _dot_nn(ds_scr[h], q_ref[0, rows, hsl(h)])
                dq_ref[0, h // 2, i, hsl(h % 2), :] += _dot_nn(kt_s[h], ds_scr[h]) * MLA_SCALE

        step(j, True)

        def loop_body(i, carry):
            step(i, False)
            return carry

        lax.fori_loop(j + 1, nq, loop_body, 0)
        for h in range(hg):
            dk_ref[0, :, hsl(h)] = dk_s[h] * MLA_SCALE
        for p in range(pg):
            dv_ref[0, :, hsl(p)] = dv_s[p]

    stat = pl.BlockSpec((1, pg, nq, 8, tq), lambda b, g, j: (b, g, 0, 0, 0))
    return _host_call(
        body, comm, name=name,
        out_shape=[jax.ShapeDtypeStruct((B, S, N_HEADS * LANES), F32), jax.ShapeDtypeStruct((B, S, D_B), F32),
                   jax.ShapeDtypeStruct((B, npair, nq, 2 * LANES, tq), F32)],
        grid=(B, ngroup, S // tk),
        in_specs=[pl.BlockSpec((1, S, hg * LANES), lambda b, g, j: (b, 0, g)),
                  pl.BlockSpec((1, S, pg * LANES), lambda b, g, j: (b, 0, g)),
                  stat, stat,
                  pl.BlockSpec((1, tk, hg * LANES), lambda b, g, j: (b, j, g)),
                  pl.BlockSpec((1, tk, pg * LANES), lambda b, g, j: (b, j, g))],
        out_specs=[pl.BlockSpec((1, tk, hg * LANES), lambda b, g, j: (b, j, g)),
                   pl.BlockSpec((1, tk, pg * LANES), lambda b, g, j: (b, j, g)),
                   pl.BlockSpec((1, pg, nq, 2 * LANES, tq), lambda b, g, j: (b, g, 0, 0, 0))],
        scratch_shapes=[pltpu.VMEM((hg, tk, tq), F32), pltpu.VMEM((hg, tk, tq), F32),
                        pltpu.VMEM((hg, tk, tq), BF16), pltpu.VMEM((hg, tk, tq), BF16),
                        pltpu.VMEM((hg, tk, LANES), F32), pltpu.VMEM((pg, tk, LANES), F32),
                        pltpu.VMEM((hg, LANES, tk), BF16)],
        args=(q, do, lse, delta, k, v))


def _bucket_tables():
    return jnp.asarray(np.stack([_band_buckets(d) for d in DILATIONS]))


def _local_step(x, target, mod, wts, gains, rel_bias, ffn_shards=None, bias=None):
    B, S, D = x.shape
    T = B * S
    sh1, sc1, g1, sh2, sc2, g2 = [mod[:, i * D:(i + 1) * D].reshape(B, 1, D) for i in range(N_MOD)]
    cs, sn = _rope_tables()
    buckets_dev = _bucket_tables()
    if bias is None:
        bias, _ = _bias_tables(rel_bias, buckets_dev, "rel_bias_tables")
    w_in = wts["w_in"]

    h1 = _adaln_fwd(x, gains["g_norm1"], sc1, sh1, "adaln1_fwd")
    h1f = h1.reshape(T, D)
    qkv_v = _mm_qkv_views(h1, w_in[:, :P_QKV], "mm_qkv")
    rest = _mm(h1f, w_in[:, P_QKV:], "nn", F32, "mm_rest")
    o_d, lse_d = [], []
    late_got = []
    for i, d in enumerate(DILATIONS):
        comm = _GatherComm(ffn_shards[i + 1:i + 2]) if (ffn_shards and i < 2) else None
        (o_i, lse_i), got = _dil_fwd(qkv_v[i], bias, i, d, f"dil_fwd_{d}", comm)
        late_got += list(got)
        o_d.append(o_i)
        lse_d.append(lse_i)
    if ffn_shards:
        wts = dict(wts, w_out=late_got[1].reshape(D, D))
    out_a_v, lse_a_v = _dil_merge(o_d, lse_d, "dil_merge")
    out_a = out_a_v[0]
    cqn = _rms_fwd(rest, 1, Q_LORA, gains["g_cq"], "rms_cq_fwd")
    ckvn = _rms_fwd(rest, 0, KV_LORA, gains["g_ckv"], "rms_ckv_fwd")
    rest3 = rest.reshape(B, S, P_REST)
    q_raw = _mm(cqn, wts["w_uq"], "nn", F32, "mm_uq").reshape(B, S, N_HEADS * LANES)
    qc = _rope_apply(q_raw, cs, sn, BF16, "rope_q")
    kn_raw = _mm(ckvn, wts["w_kv"][:, :N_HEADS * LANES], "nn", F32, "mm_uk").reshape(B, S, N_HEADS * LANES)
    kc = _rope_apply(kn_raw, cs, sn, BF16, "rope_k", add=rest3, add_blk=KV_LORA // LANES)
    v = _mm(ckvn, wts["w_kv"][:, N_HEADS * LANES:], "nn", BF16, "mm_uv").reshape(B, S, D_B)
    vt = jnp.transpose(v.reshape(B, S // MLA_TK, MLA_TK, N_HEADS // 2, LANES), (0, 3, 1, 4, 2))
    (out_b, lse_b), got = _mla_fwd_t(qc, kc, vt, "mla_fwd", _GatherComm(ffn_shards[:1]) if ffn_shards else None)
    if ffn_shards:
        wts = dict(wts, w_ffn_in=got[0].reshape(N_CHIP, D, -1), w_ffn_out=late_got[0].reshape(D_FF, D))
    out_af, out_bf = out_a.reshape(T, D_A), out_b.reshape(T, D_B)
    y = _rms_fwd_pair(out_af, out_bf, gains["g_out_a"], gains["g_out_b"], "rms_out_fwd")
    mix = _mm(y, wts["w_out"], "nn", F32, "mm_out").reshape(B, S, D)
    h2, x1 = _adaln_fwd(x, gains["g_norm2"], sc2, sh2, "adaln2_fwd", mix=mix, gate=g1)
    h2f = h2.reshape(T, D)
    gu, act = _ffn_in_fwd(h2f, wts["w_ffn_in"], "mm_ffn_in")
    f = _mm(act, wts["w_ffn_out"], "nn", F32, "mm_ffn_out").reshape(B, S, D)
    dx2, df, dg2, dg_final, loss = _final_loss(x1, f, g2, gains["g_final"], target, "final_loss")

    dff = df.reshape(T, D)
    dgu = _ffn_out_bwd(dff, wts["w_ffn_out"], gu, "mm_ffn_out_dx")
    gw_ffn_out = _mm(act, dff, "tn", F32, "mm_ffn_out_dw")
    dh2 = _mm(dgu, wts["w_ffn_in"], "nt", F32, "mm_ffn_in_dx", col_blocks=N_CHIP, halves=True).reshape(B, S, D)
    gw_ffn_in = _mm(h2f, dgu, "tn", F32, "mm_ffn_in_dw", col_blocks=N_CHIP, halves=True)
    ffn_g8 = ffn_r1 = None
    if ffn_shards:
        ffn_g8 = [gw_ffn_in.reshape(N_DEV, -1, gw_ffn_in.shape[-1]), gw_ffn_out.reshape(N_DEV, -1, D)]
        dx1, dsh2, dsc2, dg_norm2, dg1, dmix, ffn_r1 = _adaln_bwd(
            dh2, x1, gains["g_norm2"], sc2, dx2, "adaln2_bwd", mix=mix, gate=g1, comm=_ToSiblingComm(ffn_g8))
    else:
        dx1, dsh2, dsc2, dg_norm2, dg1, dmix = _adaln_bwd(dh2, x1, gains["g_norm2"], sc2, dx2, "adaln2_bwd",
                                                          mix=mix, gate=g1)
    dmixf = dmix.reshape(T, D)
    dy = _mm(dmixf, wts["w_out"], "nt", F32, "mm_out_dx")
    gw_out = _mm(y, dmixf, "tn", F32, "mm_out_dw")
    do_a_v, dg_out_a = _rms_bwd_views(dy, 0, out_a, gains["g_out_a"], "rms_outa_bwd")
    do_b, dg_out_b = _rms_bwd(dy, 1, out_bf, 0, D_B, gains["g_out_b"], "rms_outb_bwd")
    do_b3 = do_b.reshape(B, S, D_B)
    delta_b = _mla_delta(do_b3, out_b, "mla_delta")
    ffn_a4 = ffn_send = None
    if ffn_shards:
        ffn_a4, ffn_send = _rs_first(ffn_g8, "ffn", r1=ffn_r1)
    (dkc, dv, dq_t), ffn_r2 = _mla_bwd_t(qc, kc, v, do_b3, lse_b, delta_b, "mla_bwd",
                                         _ToChipsComm(ffn_send[:1]) if ffn_shards else None)
    dq_raw = _qrope_bwd(dq_t, cs, -sn, "rope_q_bwd").reshape(T, N_HEADS * LANES)
    dkrw = _krope_bwd(dkc, cs, -sn, "rope_k_bwd").reshape(T, LANES)
    dcqn = _mm(dq_raw, wts["w_uq"], "nt", F32, "mm_uq_dx")
    gw_uq = _mm(cqn, dq_raw, "tn", F32, "mm_uq_dw")
    dkv = jnp.concatenate([dkc.reshape(T, -1), dv.reshape(T, -1)], axis=1).astype(BF16)
    dckvn = _mm(dkv, wts["w_kv"], "nt", F32, "mm_ukv_dx")
    gw_kv = _mm(ckvn, dkv, "tn", F32, "mm_ukv_dw")
    dcq, dg_cq = _rms_bwd(dcqn, 0, rest, 1, Q_LORA, gains["g_cq"], "rms_cq_bwd")
    dckv, dg_ckv = _rms_bwd(dckvn, 0, rest, 0, KV_LORA, gains["g_ckv"], "rms_ckv_bwd")
    dqkv_d, dbias_d = [], []
    for i, d in enumerate(DILATIONS):
        comm = _ToChipsComm(ffn_send[1:]) if (ffn_shards and i == 0) else None
        (dqkv_i, dbias_i), got = _dil_bwd(qkv_v[i], do_a_v[i], out_a_v[i], lse_a_v[i], bias, i, d,
                                          f"dil_bwd_{d}", comm)
        if comm is not None:
            ffn_r2 = list(ffn_r2) + list(got)
        dqkv_d.append(dqkv_i)
        dbias_d.append(dbias_i)
    dqkv = _sum_views_bf16(dqkv_d, "dil_bwd_sum").reshape(T, P_QKV)
    g_rel_bias = _bias_grad(dbias_d, buckets_dev, "rel_bias_grad")[:, :N_BUCKETS].T
    dproj = jnp.concatenate([dqkv, dckv, dkrw, dcq], axis=1)
    gw_in = _mm(h1f, dproj, "tn", F32, "mm_in_dw")
    mix_a4 = mix_r2 = None
    if ffn_shards:
        nat = [_w_in_from_kernel(gw_in), _w_uq_from_kernel(gw_uq), _w_ukv_from_kernel(gw_kv)]
        g8 = [_shards_from_full(g) for g in nat] + [gw_out]
        mix_a4, mix_send = _rs_first([g.reshape(N_DEV, -1, g.shape[-1]) for g in g8], "mix")
        dh1, mix_r2 = _mm(dproj, w_in, "nt", F32, "mm_in_dx", comm=_ToChipsComm(mix_send))
    else:
        dh1 = _mm(dproj, w_in, "nt", F32, "mm_in_dx")
    dh1 = dh1.reshape(B, S, D)
    grad_x, dsh1, dsc1, dg_norm1 = _adaln_bwd(dh1, x, gains["g_norm1"], sc1, dx1, "adaln1_bwd")
    gmod = jnp.concatenate([dsh1, dsc1, dg1, dsh2, dsc2, dg2], axis=-1).reshape(B, N_MOD * D)
    grads = dict(w_in=gw_in, w_uq=gw_uq, w_kv=gw_kv, w_out=gw_out, w_ffn_in=gw_ffn_in, w_ffn_out=gw_ffn_out,
                 g_norm1=dg_norm1, g_cq=dg_cq, g_ckv=dg_ckv, rel_bias=g_rel_bias, g_out_a=dg_out_a,
                 g_out_b=dg_out_b, g_norm2=dg_norm2, g_final=dg_final, ffn_pending=(ffn_a4, ffn_r2),
                 mix_pending=(mix_a4, mix_r2))
    return loss, grad_x, gmod, grads


def _w_in_to_kernel(w):
    z = lambda n: jnp.zeros((w.shape[0], n), w.dtype)
    i3, i4, i5 = 3 * D_A, 3 * D_A + Q_LORA, 3 * D_A + Q_LORA + KV_LORA
    return jnp.concatenate([w[:, :i3], w[:, i4:i5], z(NOPE_DIM), w[:, i5:], z(LANES - NOPE_DIM - ROPE_DIM),
                            w[:, i3:i4]], axis=1)


def _w_in_from_kernel(g):
    o = P_QKV + KV_LORA
    return jnp.concatenate([g[:, :P_QKV], g[:, o + LANES:], g[:, P_QKV:o],
                            g[:, o + NOPE_DIM:o + NOPE_DIM + ROPE_DIM]], axis=1)


def _w_uq_to_kernel(w):
    w3 = w.reshape(Q_LORA, N_HEADS, NOPE_DIM + ROPE_DIM)
    return jnp.pad(w3, ((0, 0), (0, 0), (0, LANES - NOPE_DIM - ROPE_DIM))).reshape(Q_LORA, N_HEADS * LANES)


def _w_uq_from_kernel(g):
    return g.reshape(Q_LORA, N_HEADS, LANES)[:, :, :NOPE_DIM + ROPE_DIM].reshape(Q_LORA, -1)


def _w_ukv_to_kernel(w):
    w3 = w.reshape(KV_LORA, N_HEADS, 2 * HEAD_DIM)
    wk = jnp.pad(w3[:, :, :NOPE_DIM], ((0, 0), (0, 0), (0, LANES - NOPE_DIM))).reshape(KV_LORA, N_HEADS * LANES)
    wv = w3[:, :, NOPE_DIM:].reshape(KV_LORA, D_B)
    return jnp.concatenate([wk, wv], axis=1)


def _w_ukv_from_kernel(g):
    gk = g[:, :N_HEADS * LANES].reshape(KV_LORA, N_HEADS, LANES)[:, :, :NOPE_DIM]
    gv = g[:, N_HEADS * LANES:].reshape(KV_LORA, N_HEADS, HEAD_DIM)
    return jnp.concatenate([gk, gv], axis=2).reshape(KV_LORA, -1)


MESH = pl.DeviceIdType.MESH


def _my_place():
    return lax.axis_index("x"), lax.axis_index("y"), lax.axis_index("c")


def _other_chips(x, y):
    return [(1 - x, y), (x, 1 - y), (1 - x, 1 - y)]


def _allgather8(x_shard, name, in_hbm):
    m_per, n = x_shard.shape
    space = pl.ANY if in_hbm else pltpu.VMEM

    def body(x_ref, out_ref, send_sems, recv_sems, local_sem):
        x, y, c = _my_place()
        me, sibling = (x, y, c), (x, y, 1 - c)
        chips = _other_chips(x, y)

        def rows(px, py, pc):
            return out_ref.at[pl.ds((4 * px + 2 * py + pc) * m_per, m_per), :]

        def copy(k, block, to, src=None):
            return pltpu.make_async_remote_copy(
                src_ref=rows(*block) if src is None else src, dst_ref=rows(*block),
                send_sem=send_sems.at[k], recv_sem=recv_sems.at[k], device_id=to, device_id_type=MESH)

        mine = pltpu.make_async_copy(x_ref, rows(*me), local_sem)
        mine.start()
        first = [copy(0, me, sibling, src=x_ref)]
        first += [copy(1 + j, me, (*chip, c), src=x_ref) for j, chip in enumerate(chips)]
        for cp in first:
            cp.start()
        passed = [copy(4 + j, (*chip, c), sibling) for j, chip in enumerate(chips)]
        for j, chip in enumerate(chips):
            copy(1 + j, (*chip, c), me).wait_recv()
            passed[j].start()
        copy(0, sibling, me).wait_recv()
        for j, chip in enumerate(chips):
            copy(4 + j, (*chip, 1 - c), me).wait_recv()
        for cp in first + passed:
            cp.wait_send()
        mine.wait()

    return pl.pallas_call(
        body, name=name,
        out_shape=jax.ShapeDtypeStruct((N_DEV * m_per, n), x_shard.dtype),
        in_specs=[pl.BlockSpec(memory_space=space)],
        out_specs=pl.BlockSpec(memory_space=space),
        scratch_shapes=[pltpu.SemaphoreType.DMA((7,)), pltpu.SemaphoreType.DMA((7,)), pltpu.SemaphoreType.DMA],
        compiler_params=pltpu.CompilerParams(vmem_limit_bytes=VMEM_LIMIT),
    )(x_shard)


def _hbm_specs(n):
    return [pl.BlockSpec(memory_space=pl.ANY)] * n


class _GatherComm:
    def __init__(self, shards):
        self.n = n = len(shards)
        self.inputs = [s.reshape(2, s.shape[0] // 2, s.shape[1]) for s in shards]
        self.out_shape = [jax.ShapeDtypeStruct((N_DEV,) + s.shape[1:], s.dtype) for s in self.inputs]
        self.scratch = [pltpu.SemaphoreType.DMA((7 * n,)), pltpu.SemaphoreType.DMA((7 * n,))]

    def _parts(self, xs, outs, sems):
        send_sems, recv_sems = sems
        x, y, c = _my_place()

        def blk(k, px, py, pc):
            return outs[k].at[4 * px + 2 * py + pc]

        def copy(k, kind, block, to, own=False):
            return pltpu.make_async_remote_copy(
                src_ref=xs[k].at[c] if own else blk(k, *block), dst_ref=blk(k, *block),
                send_sem=send_sems.at[7 * k + kind], recv_sem=recv_sems.at[7 * k + kind],
                device_id=to, device_id_type=MESH)

        def whole(k):
            return pltpu.make_async_remote_copy(
                src_ref=xs[k], dst_ref=outs[k].at[pl.ds(4 * x + 2 * y, 2)],
                send_sem=send_sems.at[7 * k], recv_sem=recv_sems.at[7 * k],
                device_id=(x, y, 1 - c), device_id_type=MESH)

        me, sibling = (x, y, c), (x, y, 1 - c)
        chips = _other_chips(x, y)
        first = []
        for k in range(self.n):
            first.append(whole(k))
            first += [copy(k, 1 + j, me, (*chip, c), own=True) for j, chip in enumerate(chips)]
        return copy, whole, me, sibling, chips, c, first

    def start(self, xs, outs, sems):
        for cp in self._parts(xs, outs, sems)[-1]:
            cp.start()

    def finish(self, xs, outs, sems):
        copy, whole, me, sibling, chips, c, first = self._parts(xs, outs, sems)
        passed = []
        for j, chip in enumerate(chips):
            for k in range(self.n):
                copy(k, 1 + j, (*chip, c), me).wait_recv()
                fwd = copy(k, 4 + j, (*chip, c), sibling)
                fwd.start()
                passed.append(fwd)
        for k in range(self.n):
            whole(k).wait_recv()
        for j, chip in enumerate(chips):
            for k in range(self.n):
                copy(k, 4 + j, (*chip, 1 - c), me).wait_recv()
        for cp in first + passed:
            cp.wait_send()


class _ToChipsComm:
    def __init__(self, a4s):
        self.inputs = list(a4s)
        self.n = n = len(a4s)
        nc = N_CHIP - 1
        self.out_shape = [jax.ShapeDtypeStruct((nc,) + a.shape[1:], a.dtype) for a in a4s]
        self.scratch = [pltpu.SemaphoreType.DMA((nc * n,)), pltpu.SemaphoreType.DMA((nc * n,))]

    def _copies(self, as_, rs, sems):
        send_sems, recv_sems = sems
        x, y, c = _my_place()
        nc = N_CHIP - 1
        return [pltpu.make_async_remote_copy(
            src_ref=as_[k].at[2 * cx + cy], dst_ref=rs[k].at[j], send_sem=send_sems.at[nc * k + j],
            recv_sem=recv_sems.at[nc * k + j], device_id=(cx, cy, c), device_id_type=MESH)
            for k in range(self.n) for j, (cx, cy) in enumerate(_other_chips(x, y))]

    def start(self, as_, rs, sems):
        for cp in self._copies(as_, rs, sems):
            cp.start()

    def finish(self, as_, rs, sems):
        for cp in self._copies(as_, rs, sems):
            cp.wait()


def _run_comm(comm, name):
    n = comm.n

    def body(*refs):
        ins, outs, sems = refs[:n], refs[n:2 * n], refs[2 * n:]
        comm.start(ins, outs, sems)
        comm.finish(ins, outs, sems)

    return pl.pallas_call(
        body, name=name, out_shape=comm.out_shape, in_specs=_hbm_specs(n), out_specs=_hbm_specs(n),
        scratch_shapes=comm.scratch,
    )(*comm.inputs)


class _ToSiblingComm:
    def __init__(self, g8s):
        self.inputs = list(g8s)
        self.n = n = len(g8s)
        self.out_shape = [jax.ShapeDtypeStruct((N_CHIP,) + g.shape[1:], g.dtype) for g in g8s]
        self.scratch = [pltpu.SemaphoreType.DMA((N_CHIP * n,)), pltpu.SemaphoreType.DMA((N_CHIP * n,))]

    def _copies(self, gs, rs, sems):
        send_sems, recv_sems = sems
        x, y, c = _my_place()
        return [pltpu.make_async_remote_copy(
            src_ref=gs[k].at[2 * s + 1 - c], dst_ref=rs[k].at[s], send_sem=send_sems.at[N_CHIP * k + s],
            recv_sem=recv_sems.at[N_CHIP * k + s], device_id=(x, y, 1 - c), device_id_type=MESH)
            for k in range(self.n) for s in range(N_CHIP)]

    def start(self, gs, rs, sems):
        for cp in self._copies(gs, rs, sems):
            cp.start()

    def finish(self, gs, rs, sems):
        for cp in self._copies(gs, rs, sems):
            cp.wait()


def _swap_halves(hs, name):
    n = len(hs)

    def body(*refs):
        o_refs = refs[n:2 * n]
        send_sems, recv_sems = refs[2 * n:]
        x, y, c = _my_place()

        def remote(k, slot):
            return pltpu.make_async_remote_copy(
                src_ref=o_refs[k].at[slot], dst_ref=o_refs[k].at[slot], send_sem=send_sems.at[k],
                recv_sem=recv_sems.at[k], device_id=(x, y, 1 - c), device_id_type=MESH)

        sends = [remote(k, c) for k in range(n)]
        for cp in sends:
            cp.start()
        for k in range(n):
            remote(k, 1 - c).wait_recv()
        for cp in sends:
            cp.wait_send()

    return pl.pallas_call(
        body, name=name,
        out_shape=[jax.ShapeDtypeStruct(h.shape, h.dtype) for h in hs],
        in_specs=_hbm_specs(n), out_specs=_hbm_specs(n),
        input_output_aliases={k: k for k in range(n)},
        scratch_shapes=[pltpu.SemaphoreType.DMA((n,)), pltpu.SemaphoreType.DMA((n,))],
    )(*hs)


ADD_TILES = 2


def _add_blocks(a_list, a_idx_fn, others_list, ns, sel, name, out_blocks=None, out_idx_fn=None,
                bf16_copy=False):
    out_blocks = out_blocks or ns
    out_idx_fn = out_idx_fn or (lambda s, sel_ref: s)
    n = len(a_list)
    n_o = len(others_list[0])
    per = 1 + n_o

    def body(sel_ref, *refs):
        for k in range(n):
            ins = refs[k * per:(k + 1) * per]
            acc = ins[0][0]
            for r in ins[1:]:
                acc = acc + r[0].astype(F32)
            refs[n * per + k][0] = acc
            if bf16_copy:
                refs[n * per + n + k][0] = acc.astype(BF16)

    in_specs, args, out_specs, out_shape = [], [], [], []
    for a, others in zip(a_list, others_list):
        _, R, N = a.shape
        tr = R // ADD_TILES
        assert tr % 8 == 0, a.shape
        in_specs.append(pl.BlockSpec((1, tr, N), lambda s, i, sel_ref: (a_idx_fn(s, sel_ref), i, 0)))
        args.append(a)
        for arr, fixed in others:
            if fixed is None:
                in_specs.append(pl.BlockSpec((1, tr, N), lambda s, i, sel_ref: (s, i, 0)))
            else:
                in_specs.append(pl.BlockSpec((1, tr, N), lambda s, i, sel_ref, fixed=fixed: (fixed, i, 0)))
            args.append(arr)
        out_specs.append(pl.BlockSpec((1, tr, N), lambda s, i, sel_ref: (out_idx_fn(s, sel_ref), i, 0)))
        out_shape.append(jax.ShapeDtypeStruct((out_blocks, R, N), a.dtype))
    if bf16_copy:
        out_specs = out_specs + out_specs
        out_shape = out_shape + [jax.ShapeDtypeStruct(o.shape, BF16) for o in out_shape]
    grid_spec = pltpu.PrefetchScalarGridSpec(num_scalar_prefetch=1, grid=(ns, ADD_TILES), in_specs=in_specs,
                                             out_specs=out_specs)
    return pl.pallas_call(
        body, name=name, out_shape=out_shape, grid_spec=grid_spec,
        compiler_params=_cparams(("parallel", "parallel")),
    )(sel, *args)


def _rs_first(g8s, tag, r1=None):
    c_sel = jnp.reshape(lax.axis_index("c"), (1,)).astype(jnp.int32)
    if r1 is None:
        r1 = _run_comm(_ToSiblingComm(g8s), f"rs_to_sibling_{tag}")
    res = _add_blocks(g8s, lambda s, sel: 2 * s + sel[0], [[(r, None)] for r in r1], N_CHIP, c_sel,
                      f"rs_add_sibling_{tag}", bf16_copy=True)
    return list(res[:len(g8s)]), list(res[len(g8s):])


def _rs_last(a4s, r2s, tag):
    sel = jnp.stack([2 * lax.axis_index("x") + lax.axis_index("y"), lax.axis_index("c")]).astype(jnp.int32)
    h = _add_blocks(a4s, lambda s, sel: sel[0], [[(r, 0), (r, 1), (r, 2)] for r in r2s], 1, sel,
                    f"rs_add_chips_{tag}", out_blocks=2, out_idx_fn=lambda s, sel: sel[1])
    full = _swap_halves(h, f"rs_swap_halves_{tag}")
    return [f.reshape(2 * f.shape[1], f.shape[2]) for f in full]


def _ada_fwd(c_all, w_ada, b_ada, name):
    nb, D = c_all.shape
    ncol = w_ada.shape[1]
    tc = 512

    def body(c_ref, w_ref, b_ref, o_ref):
        cv = c_ref[...]
        cond = (cv * jax.nn.sigmoid(cv)).astype(BF16)
        o_ref[...] = jnp.dot(cond, w_ref[...].astype(BF16), preferred_element_type=F32) + b_ref[...]

    return pl.pallas_call(
        body, name=name, out_shape=jax.ShapeDtypeStruct((nb, ncol), F32), grid=(ncol // tc,),
        in_specs=[pl.BlockSpec((nb, D), lambda j: (0, 0)), pl.BlockSpec((D, tc), lambda j: (0, j)),
                  pl.BlockSpec((1, tc), lambda j: (0, j))],
        out_specs=pl.BlockSpec((nb, tc), lambda j: (0, j)),
        compiler_params=_cparams(("parallel",)),
    )(c_all, w_ada, b_ada)


def _ada_bwd(c_all, gmod_cols, name):
    nb, D = c_all.shape
    ncol = gmod_cols.shape[1]
    tc = 512

    def body(c_ref, g_ref, o_ref):
        cv = c_ref[...]
        cond = (cv * jax.nn.sigmoid(cv)).astype(BF16)
        o_ref[...] = _dot_tn(cond, g_ref[...].astype(BF16))

    return pl.pallas_call(
        body, name=name, out_shape=jax.ShapeDtypeStruct((D, ncol), F32), grid=(ncol // tc,),
        in_specs=[pl.BlockSpec((nb, D), lambda j: (0, 0)), pl.BlockSpec((nb, tc), lambda j: (0, j))],
        out_specs=pl.BlockSpec((D, tc), lambda j: (0, j)),
        compiler_params=_cparams(("parallel",)),
    )(c_all, gmod_cols)


def _adam_math(w, g, m, v):
    m = ADAM_B1 * m + (1.0 - ADAM_B1) * g
    v = ADAM_B2 * v + (1.0 - ADAM_B2) * (g * g)
    m_hat = m / (1.0 - ADAM_B1 ** ADAM_STEP)
    v_hat = v / (1.0 - ADAM_B2 ** ADAM_STEP)
    delta = -ADAM_LR * (m_hat / (jnp.sqrt(v_hat) + ADAM_EPS) + ADAM_WD * w)
    return delta, m, v


def _adamw(w, g, m, v, name):
    rows, cols = w.shape
    tr = _pick(rows, (256, 192, 176, 128, 64, 8))

    def body(w_ref, g_ref, m_ref, v_ref, d_ref, mo_ref, vo_ref):
        d, mn, vn = _adam_math(w_ref[...], g_ref[...], m_ref[...], v_ref[...])
        d_ref[...] = d
        mo_ref[...] = mn
        vo_ref[...] = vn

    spec = pl.BlockSpec((tr, cols), lambda i: (i, 0))
    return pl.pallas_call(
        body, name=name, out_shape=[jax.ShapeDtypeStruct((rows, cols), F32)] * 3, grid=(rows // tr,),
        in_specs=[spec] * 4, out_specs=[spec] * 3, compiler_params=_cparams(("parallel",)),
    )(w, g, m, v)


VEC_ROWS = 8


def _adamw_rows(w, parts, m, v, name):
    n = w.shape[1]
    P = parts.shape[0]
    assert n % (VEC_ROWS * LANES) == 0, n
    shp = (VEC_ROWS, n // VEC_ROWS)

    def body(w_ref, p_ref, m_ref, v_ref, g_ref, d_ref, mo_ref, vo_ref):
        g = p_ref[0]
        for k in range(1, P):
            g = g + p_ref[k]
        d, mn, vn = _adam_math(w_ref[...], g, m_ref[...], v_ref[...])
        g_ref[...] = g
        d_ref[...] = d
        mo_ref[...] = mn
        vo_ref[...] = vn

    vec = pl.BlockSpec(shp, lambda i: (0, 0))
    out = pl.pallas_call(
        body, name=name, out_shape=[jax.ShapeDtypeStruct(shp, F32)] * 4, grid=(1,),
        in_specs=[vec, pl.BlockSpec((P,) + shp, lambda i: (0, 0, 0)), vec, vec], out_specs=[vec] * 4,
        compiler_params=_cparams(("arbitrary",)),
    )(w.reshape(shp), parts.reshape((P,) + shp), m.reshape(shp), v.reshape(shp))
    return [o.reshape(1, n) for o in out]


_SHARDED = ("w_in", "w_uq", "w_ukv", "w_out", "w_ffn_in", "w_ffn_out")
_SMALL = (("g_norm1", 1024), ("g_cq", 384), ("g_ckv", 256), ("rel_bias", 256), ("g_out_a", 512),
          ("g_out_b", 512), ("g_norm2", 1024), ("g_final", 1024))
_SMALL_PAD = 5120


def _full_from_shards(sh):
    return jnp.transpose(sh, (1, 0, 2)).reshape(sh.shape[1], -1)


def _shards_from_full(full):
    rows, cols = full.shape
    return jnp.transpose(full.reshape(rows, N_CHIP, cols // N_CHIP), (1, 0, 2))


def kernel(x, c, w_ada, b_ada, g_norm1, w_in, g_cq, w_uq, g_ckv, w_ukv, rel_bias, g_out_a, g_out_b, w_out, g_norm2, w_ffn_in, w_ffn_out, g_final, loss_target, m_w_ada, m_b_ada, m_g_norm1, m_w_in, m_g_cq, m_w_uq, m_g_ckv, m_w_ukv, m_rel_bias, m_g_out_a, m_g_out_b, m_w_out, m_g_norm2, m_w_ffn_in, m_w_ffn_out, m_g_final, v_w_ada, v_b_ada, v_g_norm1, v_w_in, v_g_cq, v_w_uq, v_g_ckv, v_w_ukv, v_rel_bias, v_g_out_a, v_g_out_b, v_w_out, v_g_norm2, v_w_ffn_in, v_w_ffn_out, v_g_final):
    names = ["w_ada", "b_ada", "g_norm1", "w_in", "g_cq", "w_uq", "g_ckv", "w_ukv", "rel_bias", "g_out_a",
             "g_out_b", "w_out", "g_norm2", "w_ffn_in", "w_ffn_out", "g_final"]
    W = dict(zip(names, [w_ada, b_ada, g_norm1, w_in, g_cq, w_uq, g_ckv, w_ukv, rel_bias, g_out_a, g_out_b,
                         w_out, g_norm2, w_ffn_in, w_ffn_out, g_final]))
    M = dict(zip(names, [m_w_ada, m_b_ada, m_g_norm1, m_w_in, m_g_cq, m_w_uq, m_g_ckv, m_w_ukv, m_rel_bias,
                         m_g_out_a, m_g_out_b, m_w_out, m_g_norm2, m_w_ffn_in, m_w_ffn_out, m_g_final]))
    V = dict(zip(names, [v_w_ada, v_b_ada, v_g_norm1, v_w_in, v_g_cq, v_w_uq, v_g_ckv, v_w_ukv, v_rel_bias,
                         v_g_out_a, v_g_out_b, v_w_out, v_g_norm2, v_w_ffn_in, v_w_ffn_out, v_g_final]))
    B, S, D = x.shape
    mx, my, mc = _my_place()
    dev = 4 * mx + 2 * my + mc
    chip = 2 * mx + my
    pad_rows = 8

    c_all = _allgather8(jnp.pad(c, ((0, pad_rows - B), (0, 0))), "ag_c", False)
    c_all = c_all.reshape(N_DEV, pad_rows, D)[:, :B].reshape(N_DEV * B, D)
    ada_cols = w_ada.shape[-1]
    b_cols = lax.dynamic_slice_in_dim(b_ada, chip * ada_cols, ada_cols, axis=1)
    mod_cols = _ada_fwd(c_all, w_ada[0], b_cols, "ada_fwd")
    mod_all = _allgather8(mod_cols, "ag_mod", False).reshape(N_DEV, N_DEV * B, ada_cols)[0::2]
    mod_all = jnp.transpose(mod_all, (1, 0, 2)).reshape(N_DEV * B, N_MOD * D)
    mod = lax.dynamic_slice_in_dim(mod_all, dev * B, B, axis=0)

    early = ("w_in", "w_uq", "w_ukv")
    bias, got = _bias_tables(rel_bias, _bucket_tables(), "rel_bias_tables",
                             _GatherComm([W[n][0].astype(BF16) for n in early]))
    full = {n: g.reshape((N_CHIP,) + W[n].shape[1:]) for n, g in zip(early, got)}
    wts = dict(w_in=_w_in_to_kernel(_full_from_shards(full["w_in"])),
               w_uq=_w_uq_to_kernel(_full_from_shards(full["w_uq"])),
               w_kv=_w_ukv_to_kernel(_full_from_shards(full["w_ukv"])))
    gains = dict(g_norm1=g_norm1, g_cq=g_cq, g_ckv=g_ckv, g_out_a=g_out_a, g_out_b=g_out_b, g_norm2=g_norm2,
                 g_final=g_final.reshape(1, D))

    loss, grad_x, gmod, grads = _local_step(x, loss_target, mod, wts, gains, rel_bias,
                                            ffn_shards=[w_ffn_in[0].astype(BF16), w_ffn_out[0].astype(BF16),
                                                        w_out[0].astype(BF16)], bias=bias)
    loss = lax.psum(loss[0, 0], ("x", "y", "c"))

    n_small = _SMALL_PAD
    cat = lambda dct: jnp.concatenate([dct[n].reshape(1, -1) for n, _ in _SMALL]
                                      + [jnp.zeros((1, _SMALL_PAD - sum(s for _, s in _SMALL)), F32)], axis=1)
    small = cat(grads)
    rows = jnp.concatenate([gmod, jnp.pad(small, ((0, 0), (0, N_MOD * D - n_small))),
                            jnp.zeros((pad_rows - B - 1, N_MOD * D), F32)], axis=0)
    rows_all = _allgather8(rows, "ag_small", False).reshape(N_DEV, pad_rows, N_MOD * D)
    gmod_all = rows_all[:, :B].reshape(N_DEV * B, N_MOD * D)
    small_parts = rows_all[:, B, :n_small]

    a4, r2 = grads["mix_pending"]
    ffn_a4, ffn_r2 = grads["ffn_pending"]
    G = dict(zip(_SHARDED, _rs_last(list(a4) + list(ffn_a4), list(r2) + list(ffn_r2), "all")))

    gmod_cols = lax.dynamic_slice_in_dim(gmod_all, chip * ada_cols, ada_cols, axis=1)
    G["w_ada"] = _ada_bwd(c_all, gmod_cols, "ada_bwd")
    delta, new_m, new_v = {}, {}, {}
    for n in ("w_ada",) + _SHARDED:
        shp = W[n].shape
        w2 = W[n].reshape(shp[-2], shp[-1])
        d_, m_, v_ = _adamw(w2, G[n], M[n].reshape(w2.shape), V[n].reshape(w2.shape), f"adamw_{n}")
        G[n], delta[n], new_m[n], new_v[n] = [a.reshape(shp) for a in (G[n], d_, m_, v_)]
    gs, ds_, ms_, vs_ = _adamw_rows(cat(W), small_parts, cat(M), cat(V), "adamw_small")
    off = 0
    for n, sz in _SMALL:
        shp = W[n].shape
        G[n], delta[n], new_m[n], new_v[n] = [a[:, off:off + sz].reshape(shp) for a in (gs, ds_, ms_, vs_)]
        off += sz
    G["b_ada"], delta["b_ada"], new_m["b_ada"], new_v["b_ada"] = _adamw_rows(b_ada, gmod_all, m_b_ada, v_b_ada,
                                                                          "adamw_b_ada")
    return (loss, grad_x, *[G[n] for n in names], *[delta[n] for n in names], *[new_m[n] for n in names],
            *[new_v[n] for n in names])
```

```python
import functools
import math

import numpy as np
import jax
import jax.numpy as jnp
from jax import lax
from jax.experimental import pallas as pl
from jax.experimental.pallas import tpu as pltpu

F32 = jnp.float32
BF16 = jnp.bfloat16

D_MODEL = 1024
SEQ = 2048
N_HEADS = 8
HEAD_DIM = 64
D_A = 512
D_B = 512
Q_LORA = 384
KV_LORA = 256
ROPE_DIM = 32
NOPE_DIM = 64
D_FF = 2816
N_MOD = 6
N_BUCKETS = 32
MAX_DISTANCE = 2048
ROPE_THETA = 10000.0
EPS = 1e-6
NEG = -1e30
BLK = 128
DILATIONS = (1, 4, 16)
SPAN = 128
MLA_SCALE = (NOPE_DIM + ROPE_DIM) ** -0.5
DIL_SCALE = HEAD_DIM ** -0.5

ADAM_LR = 0.001
ADAM_B1 = 0.9
ADAM_B2 = 0.999
ADAM_EPS = 1e-08
ADAM_WD = 0.01
ADAM_STEP = 10

N_DEV = 8
N_CHIP = 4
LANES = 128
VMEM_LIMIT = 48 * 1024 * 1024
MM_VMEM_BUDGET = 32 * 1024 * 1024

P_QKV = 3 * D_A
P_REST = KV_LORA + LANES + Q_LORA


def _cparams(sem=None):
    return pltpu.CompilerParams(dimension_semantics=sem, vmem_limit_bytes=VMEM_LIMIT)


def _pick(n, cands):
    for c in cands:
        if n % c == 0:
            return c
    raise ValueError(f"no tile for {n} in {cands}")


def _mm(a, b, mode, out_dtype, name, col_blocks=None, comm=None, halves=False):
    blocked = col_blocks is not None
    if mode == "nn":
        (M, K) = a.shape
        K2, N = (b.shape[1], b.shape[0] * b.shape[2]) if blocked else b.shape
    elif mode == "nt":
        (M, K) = (a.shape[1], 2 * a.shape[2]) if halves else a.shape
        N, K2 = (b.shape[1], b.shape[0] * b.shape[2]) if blocked else b.shape
    else:
        (K, M) = a.shape
        K2, N = (b.shape[1], 2 * b.shape[2]) if halves else b.shape
    assert K == K2, (a.shape, b.shape, mode)
    assert not halves or (blocked and col_blocks == 4 and mode in ("nt", "tn"))
    tn = _pick(N, (1408, 1024, 768, 512, 384, 256, 128))
    tk = _pick(K, (1408, 1152, 1024, 768, 512, 384, 256, 128))
    if blocked and mode == "nt":
        tk = K // col_blocks
    elif blocked:
        tn = N // col_blocks
    nk = K // tk

    def vmem_bytes(tm_):
        tiles = tm_ * tk * a.dtype.itemsize + tk * tn * b.dtype.itemsize + tm_ * tn * jnp.dtype(out_dtype).itemsize
        return 2 * tiles + tm_ * tn * 4

    tm = next(t for t in (1408, 1024, 512, 384, 256, 128) if M % t == 0 and vmem_bytes(t) <= MM_VMEM_BUDGET)
    out_shape = (M, N)
    out_spec = pl.BlockSpec((tm, tn), lambda i, j, k: (i, j))
    if mode == "nn":
        a_spec = pl.BlockSpec((tm, tk), lambda i, j, k: (i, k))
        b_spec = (pl.BlockSpec((None, tk, tn), lambda i, j, k: (j, k, 0)) if blocked
                  else pl.BlockSpec((tk, tn), lambda i, j, k: (k, j)))
        dn = (((1,), (0,)), ((), ()))
    elif mode == "nt":
        a_spec = (pl.BlockSpec((None, tm, tk), lambda i, j, k: (k // 2, i, k % 2)) if halves
                  else pl.BlockSpec((tm, tk), lambda i, j, k: (i, k)))
        b_spec = (pl.BlockSpec((None, tn, tk), lambda i, j, k: (k, j, 0)) if blocked
                  else pl.BlockSpec((tn, tk), lambda i, j, k: (j, k)))
        dn = (((1,), (1,)), ((), ()))
    else:
        a_spec = pl.BlockSpec((tk, tm), lambda i, j, k: (k, i))
        b_spec = (pl.BlockSpec((None, tk, tn), lambda i, j, k: (j // 2, k, j % 2)) if halves
                  else pl.BlockSpec((tk, tn), lambda i, j, k: (k, j)))
        dn = (((0,), (0,)), ((), ()))
        if blocked:
            out_shape = (col_blocks, M, tn)
            out_spec = pl.BlockSpec((None, tm, tn), lambda i, j, k: (j, i, 0))

    def body(a_ref, b_ref, o_ref, acc_ref):
        k = pl.program_id(2)

        @pl.when(k == 0)
        def _():
            acc_ref[...] = jnp.zeros_like(acc_ref)

        acc_ref[...] += lax.dot_general(a_ref[...].astype(BF16), b_ref[...].astype(BF16), dn,
                                        preferred_element_type=F32)

        @pl.when(k == nk - 1)
        def _():
            o_ref[...] = acc_ref[...].astype(o_ref.dtype)

    if comm is not None:
        (out,), got = _host_call(
            body, comm, name=name, out_shape=[jax.ShapeDtypeStruct(out_shape, out_dtype)],
            grid=(M // tm, N // tn, nk), in_specs=[a_spec, b_spec], out_specs=[out_spec],
            scratch_shapes=[pltpu.VMEM((tm, tn), F32)], args=(a, b))
        return out, got
    return pl.pallas_call(
        body, name=name,
        out_shape=jax.ShapeDtypeStruct(out_shape, out_dtype),
        grid=(M // tm, N // tn, nk),
        in_specs=[a_spec, b_spec],
        out_specs=out_spec,
        scratch_shapes=[pltpu.VMEM((tm, tn), F32)],
        compiler_params=_cparams(("parallel", "parallel", "arbitrary")),
    )(a, b)


ROW_TILE = 512


def _adaln_fwd(x, g, sc, sh, name, mix=None, gate=None):
    B, S, D = x.shape
    ts = ROW_TILE
    has_res = mix is not None

    def body(*refs):
        if has_res:
            x_ref, g_ref, sc_ref, sh_ref, mix_ref, gate_ref, h_ref, xr_ref = refs
            xr = x_ref[0] + gate_ref[0] * mix_ref[0]
            xr_ref[0] = xr
        else:
            x_ref, g_ref, sc_ref, sh_ref, h_ref = refs
            xr = x_ref[0]
        r = lax.rsqrt(jnp.mean(xr * xr, axis=-1, keepdims=True) + EPS)
        xn = (xr * r) * g_ref[...]
        h_ref[0] = (xn * (1.0 + sc_ref[0]) + sh_ref[0]).astype(h_ref.dtype)

    tok = pl.BlockSpec((1, ts, D), lambda b, s: (b, s, 0))
    per_b = pl.BlockSpec((1, 1, D), lambda b, s: (b, 0, 0))
    vec = pl.BlockSpec((1, D), lambda b, s: (0, 0))
    in_specs = [tok, vec, per_b, per_b]
    args = [x, g, sc, sh]
    out_shape = [jax.ShapeDtypeStruct((B, S, D), BF16)]
    out_specs = [tok]
    if has_res:
        in_specs += [tok, per_b]
        args += [mix, gate]
        out_shape.append(jax.ShapeDtypeStruct((B, S, D), F32))
        out_specs.append(tok)
    out = pl.pallas_call(
        body, name=name, out_shape=out_shape, grid=(B, S // ts),
        in_specs=in_specs, out_specs=out_specs,
        compiler_params=_cparams(("parallel", "parallel")),
    )(*args)
    return out if has_res else out[0]


def _adaln_bwd(dh, x, g, sc, dres, name, mix=None, gate=None, comm=None):
    B, S, D = x.shape
    ts = ROW_TILE
    has_res = mix is not None

    def body(*refs):
        if has_res:
            (dh_ref, x_ref, g_ref, sc_ref, dres_ref, mix_ref, gate_ref,
             dx_ref, dsh_ref, dsc_ref, dg_ref, dgate_ref, dmix_ref) = refs
        else:
            (dh_ref, x_ref, g_ref, sc_ref, dres_ref, dx_ref, dsh_ref, dsc_ref, dg_ref) = refs
        b, s = pl.program_id(0), pl.program_id(1)
        xv = x_ref[0]
        dhv = dh_ref[0]
        gv = g_ref[...]
        r = lax.rsqrt(jnp.mean(xv * xv, axis=-1, keepdims=True) + EPS)
        n = xv * r
        xn = n * gv
        dxn = dhv * (1.0 + sc_ref[0])
        dn = dxn * gv
        dx = r * (dn - n * jnp.mean(dn * n, axis=-1, keepdims=True)) + dres_ref[0]
        dx_ref[0] = dx

        @pl.when(s == 0)
        def _():
            dsh_ref[...] = jnp.zeros_like(dsh_ref)
            dsc_ref[...] = jnp.zeros_like(dsc_ref)
            if has_res:
                dgate_ref[...] = jnp.zeros_like(dgate_ref)

        @pl.when((s == 0) & (b == 0))
        def _():
            dg_ref[...] = jnp.zeros_like(dg_ref)

        dsh_ref[0] += jnp.sum(dhv, axis=0, keepdims=True)
        dsc_ref[0] += jnp.sum(dhv * xn, axis=0, keepdims=True)
        dg_ref[...] += jnp.sum(dxn * n, axis=0, keepdims=True)
        if has_res:
            dgate_ref[0] += jnp.sum(dx * mix_ref[0], axis=0, keepdims=True)
            dmix_ref[0] = (dx * gate_ref[0]).astype(dmix_ref.dtype)

    tok = pl.BlockSpec((1, ts, D), lambda b, s: (b, s, 0))
    per_b = pl.BlockSpec((1, 1, D), lambda b, s: (b, 0, 0))
    vec = pl.BlockSpec((1, D), lambda b, s: (0, 0))
    in_specs = [tok, tok, vec, per_b, tok]
    args = [dh, x, g, sc, dres]
    out_shape = [jax.ShapeDtypeStruct((B, S, D), F32), jax.ShapeDtypeStruct((B, 1, D), F32),
                 jax.ShapeDtypeStruct((B, 1, D), F32), jax.ShapeDtypeStruct((1, D), F32)]
    out_specs = [tok, per_b, per_b, vec]
    if has_res:
        in_specs += [tok, per_b]
        args += [mix, gate]
        out_shape += [jax.ShapeDtypeStruct((B, 1, D), F32), jax.ShapeDtypeStruct((B, S, D), BF16)]
        out_specs += [per_b, tok]
    res, got = _host_call(body, comm, name=name, out_shape=out_shape, grid=(B, S // ts), in_specs=in_specs,
                          out_specs=out_specs, scratch_shapes=[], args=args)
    return (list(res) + [got]) if comm is not None else res


def _rms_fwd(x, col_blk, n, g, name):
    T = x.shape[0]
    tr = 512

    def body(x_ref, g_ref, y_ref):
        xv = x_ref[...]
        r = lax.rsqrt(jnp.mean(xv * xv, axis=-1, keepdims=True) + EPS)
        y_ref[...] = ((xv * r) * g_ref[...]).astype(y_ref.dtype)

    return pl.pallas_call(
        body, name=name, out_shape=jax.ShapeDtypeStruct((T, n), BF16), grid=(T // tr,),
        in_specs=[pl.BlockSpec((tr, n), lambda i: (i, col_blk)), pl.BlockSpec((1, n), lambda i: (0, 0))],
        out_specs=pl.BlockSpec((tr, n), lambda i: (i, 0)),
        compiler_params=_cparams(("parallel",)),
    )(x, g)


def _rms_fwd_pair(xa, xb, ga, gb, name):
    T, na = xa.shape
    nb = xb.shape[1]
    tr = 512

    def body(xa_ref, xb_ref, ga_ref, gb_ref, y_ref):
        for x_ref, g_ref, lo, n in ((xa_ref, ga_ref, 0, na), (xb_ref, gb_ref, na, nb)):
            xv = x_ref[...]
            r = lax.rsqrt(jnp.mean(xv * xv, axis=-1, keepdims=True) + EPS)
            y_ref[:, lo:lo + n] = ((xv * r) * g_ref[...]).astype(y_ref.dtype)

    row = lambda n: pl.BlockSpec((tr, n), lambda i: (i, 0))
    vec = lambda n: pl.BlockSpec((1, n), lambda i: (0, 0))
    return pl.pallas_call(
        body, name=name, out_shape=jax.ShapeDtypeStruct((T, na + nb), BF16), grid=(T // tr,),
        in_specs=[row(na), row(nb), vec(na), vec(nb)], out_specs=row(na + nb),
        compiler_params=_cparams(("parallel",)),
    )(xa, xb, ga, gb)


def _rms_bwd(dy, dy_blk, x, x_blk, n, g, name, out_dtype=BF16):
    T = x.shape[0]
    tr = 512

    def body(dy_ref, x_ref, g_ref, dx_ref, dg_ref):
        xv = x_ref[...]
        dyv = dy_ref[...].astype(F32)
        r = lax.rsqrt(jnp.mean(xv * xv, axis=-1, keepdims=True) + EPS)
        nrm = xv * r
        dn = dyv * g_ref[...]
        dx_ref[...] = (r * (dn - nrm * jnp.mean(dn * nrm, axis=-1, keepdims=True))).astype(dx_ref.dtype)

        @pl.when(pl.program_id(0) == 0)
        def _():
            dg_ref[...] = jnp.zeros_like(dg_ref)

        dg_ref[...] += jnp.sum(dyv * nrm, axis=0, keepdims=True)

    return pl.pallas_call(
        body, name=name,
        out_shape=[jax.ShapeDtypeStruct((T, n), out_dtype), jax.ShapeDtypeStruct((1, n), F32)],
        grid=(T // tr,),
        in_specs=[pl.BlockSpec((tr, n), lambda i: (i, dy_blk)), pl.BlockSpec((tr, n), lambda i: (i, x_blk)),
                  pl.BlockSpec((1, n), lambda i: (0, 0))],
        out_specs=[pl.BlockSpec((tr, n), lambda i: (i, 0)), pl.BlockSpec((1, n), lambda i: (0, 0))],
        compiler_params=_cparams(("arbitrary",)),
    )(dy, x, g)


def _rms_bwd_views(dy, dy_blk, x, g, name):
    B, S, n = x.shape
    tiles = S // VIEW_TILE

    def body(dy_ref, x_ref, g_ref, d1_ref, d4_ref, d16_ref, dg_ref, dx_s):
        xv = x_ref[0]
        dyv = dy_ref[...]
        r = lax.rsqrt(jnp.mean(xv * xv, axis=-1, keepdims=True) + EPS)
        nrm = xv * r
        dn = dyv * g_ref[...]
        dx = r * (dn - nrm * jnp.mean(dn * nrm, axis=-1, keepdims=True))
        d1_ref[0] = dx.astype(d1_ref.dtype)
        _put_tile(dx_s, dx)
        _tile_to_view(dx_s, d4_ref, DILATIONS[1], n)
        _tile_to_view(dx_s, d16_ref, DILATIONS[2], n)

        @pl.when((pl.program_id(0) == 0) & (pl.program_id(1) == 0))
        def _():
            dg_ref[...] = jnp.zeros_like(dg_ref)

        dg_ref[...] += jnp.sum(dyv * nrm, axis=0, keepdims=True)

    res = pl.pallas_call(
        body, name=name,
        out_shape=[_view_shape(B, S, d, n, BF16) for d in DILATIONS] + [jax.ShapeDtypeStruct((1, n), F32)],
        grid=(B, tiles),
        in_specs=[pl.BlockSpec((VIEW_TILE, n), lambda b, t: (b * tiles + t, dy_blk)), _view_spec(1, n),
                  pl.BlockSpec((1, n), lambda b, t: (0, 0))],
        out_specs=[_view_spec(d, n) for d in DILATIONS] + [pl.BlockSpec((1, n), lambda b, t: (0, 0))],
        scratch_shapes=[_tile_scratch(n)],
        compiler_params=_cparams(("arbitrary", "arbitrary")),
    )(dy, x, g)
    return res[:len(DILATIONS)], res[len(DILATIONS)]


FFN_TILE = 1408


def _ffn_in_fwd(h, w4, name):
    T, D = h.shape
    tm, tc = 512, FFN_TILE
    nc = D_FF // tc

    def body(h_ref, wg_ref, wu_ref, gu_ref, act_ref):
        hv = h_ref[...]
        g = jnp.dot(hv, wg_ref[...], preferred_element_type=F32)
        u = jnp.dot(hv, wu_ref[...], preferred_element_type=F32)
        gu_ref[0] = g.astype(gu_ref.dtype)
        gu_ref[1] = u.astype(gu_ref.dtype)
        act_ref[...] = (g * jax.nn.sigmoid(g) * u).astype(act_ref.dtype)

    return pl.pallas_call(
        body, name=name,
        out_shape=[jax.ShapeDtypeStruct((2, T, D_FF), BF16), jax.ShapeDtypeStruct((T, D_FF), BF16)],
        grid=(nc, T // tm),
        in_specs=[pl.BlockSpec((tm, D), lambda j, i: (i, 0)),
                  pl.BlockSpec((None, D, tc), lambda j, i: (j, 0, 0)),
                  pl.BlockSpec((None, D, tc), lambda j, i: (j + nc, 0, 0))],
        out_specs=[pl.BlockSpec((2, tm, tc), lambda j, i: (0, i, j)), pl.BlockSpec((tm, tc), lambda j, i: (i, j))],
        compiler_params=_cparams(("parallel", "parallel")),
    )(h, w4, w4)


def _ffn_out_bwd(df, w_out, gu, name):
    T, D = df.shape
    tm, tc = 512, FFN_TILE

    def body(df_ref, w_ref, gu_ref, dgu_ref):
        da = _dot_nt(df_ref[...], w_ref[...])
        g, u = gu_ref[0].astype(F32), gu_ref[1].astype(F32)
        sg = jax.nn.sigmoid(g)
        dgu_ref[0] = (da * u * (sg * (1.0 + g * (1.0 - sg)))).astype(dgu_ref.dtype)
        dgu_ref[1] = (da * (g * sg)).astype(dgu_ref.dtype)

    halves = pl.BlockSpec((2, tm, tc), lambda j, i: (0, i, j))
    return pl.pallas_call(
        body, name=name, out_shape=jax.ShapeDtypeStruct((2, T, D_FF), BF16), grid=(D_FF // tc, T // tm),
        in_specs=[pl.BlockSpec((tm, D), lambda j, i: (i, 0)), pl.BlockSpec((tc, D), lambda j, i: (j, 0)), halves],
        out_specs=halves,
        compiler_params=_cparams(("parallel", "parallel")),
    )(df, w_out, gu)


def _final_loss(x1, f, g2, gf, target, name):
    B, S, D = x1.shape
    ts = ROW_TILE

    def body(x1_ref, f_ref, g2_ref, gf_ref, t_ref, dx_ref, df_ref, dg2_ref, dgf_ref, loss_ref):
        b, s = pl.program_id(0), pl.program_id(1)
        fv = f_ref[0]
        g2v = g2_ref[0]
        gfv = gf_ref[...]
        x2 = x1_ref[0] + g2v * fv
        r = lax.rsqrt(jnp.mean(x2 * x2, axis=-1, keepdims=True) + EPS)
        n = x2 * r
        e = n * gfv - t_ref[0]
        dy = e * (1.0 / D)
        dn = dy * gfv
        dx = r * (dn - n * jnp.mean(dn * n, axis=-1, keepdims=True))
        dx_ref[0] = dx
        df_ref[0] = (dx * g2v).astype(df_ref.dtype)

        @pl.when(s == 0)
        def _():
            dg2_ref[...] = jnp.zeros_like(dg2_ref)

        @pl.when((s == 0) & (b == 0))
        def _():
            dgf_ref[...] = jnp.zeros_like(dgf_ref)
            loss_ref[...] = jnp.zeros_like(loss_ref)

        dg2_ref[0] += jnp.sum(dx * fv, axis=0, keepdims=True)
        dgf_ref[...] += jnp.sum(dy * n, axis=0, keepdims=True)
        loss_ref[...] += 0.5 * jnp.sum(jnp.mean(e * e, axis=-1, keepdims=True), axis=0, keepdims=True)

    tok = pl.BlockSpec((1, ts, D), lambda b, s: (b, s, 0))
    per_b = pl.BlockSpec((1, 1, D), lambda b, s: (b, 0, 0))
    vec = pl.BlockSpec((1, D), lambda b, s: (0, 0))
    return pl.pallas_call(
        body, name=name,
        out_shape=[jax.ShapeDtypeStruct((B, S, D), F32), jax.ShapeDtypeStruct((B, S, D), BF16),
                   jax.ShapeDtypeStruct((B, 1, D), F32), jax.ShapeDtypeStruct((1, D), F32),
                   jax.ShapeDtypeStruct((1, LANES), F32)],
        grid=(B, S // ts),
        in_specs=[tok, tok, per_b, vec, tok],
        out_specs=[tok, tok, per_b, vec, pl.BlockSpec((1, LANES), lambda b, s: (0, 0))],
        compiler_params=_cparams(("arbitrary", "arbitrary")),
    )(x1, f, g2, gf, target)


def _rope_tables():
    half = ROPE_DIM // 2
    inv = ROPE_THETA ** (-jnp.arange(half, dtype=F32) / half)
    ang = jnp.arange(SEQ, dtype=F32)[:, None] * inv[None, :]
    cos, sin = jnp.cos(ang), jnp.sin(ang)
    one = jnp.ones((SEQ, NOPE_DIM), F32)
    zero = jnp.zeros((SEQ, NOPE_DIM), F32)
    cs = jnp.concatenate([one, cos, cos, one[:, :LANES - NOPE_DIM - ROPE_DIM]], axis=1)
    sn = jnp.concatenate([zero, -sin, sin, zero[:, :LANES - NOPE_DIM - ROPE_DIM]], axis=1)
    return cs, sn


def _rope_group(t, cs, sn):
    half = ROPE_DIM // 2
    lane = lax.broadcasted_iota(jnp.int32, t.shape, 1)
    partner = jnp.where(lane < NOPE_DIM + half, pltpu.roll(t, LANES - half, 1), pltpu.roll(t, half, 1))
    return t * cs + partner * sn


def _rope_apply(t, cs, sn, out_dtype, name, add=None, add_blk=0):
    B, S, W = t.shape
    G = W // LANES
    ts = ROW_TILE

    def body(*refs):
        if add is None:
            t_ref, cs_ref, sn_ref, o_ref = refs
            for gi in range(G):
                sl = slice(gi * LANES, (gi + 1) * LANES)
                o_ref[0, :, sl] = _rope_group(t_ref[0, :, sl], cs_ref[...], sn_ref[...]).astype(o_ref.dtype)
        else:
            t_ref, a_ref, cs_ref, sn_ref, o_ref = refs
            ra = _rope_group(a_ref[0], cs_ref[...], sn_ref[...])
            for gi in range(G):
                sl = slice(gi * LANES, (gi + 1) * LANES)
                o_ref[0, :, sl] = (t_ref[0, :, sl] + ra).astype(o_ref.dtype)

    tok = pl.BlockSpec((1, ts, W), lambda b, s: (b, s, 0))
    tab = pl.BlockSpec((ts, LANES), lambda b, s: (s, 0))
    in_specs, args = [tok], [t]
    if add is not None:
        in_specs.append(pl.BlockSpec((1, ts, LANES), lambda b, s: (b, s, add_blk)))
        args.append(add)
    in_specs += [tab, tab]
    args += [cs, sn]
    return pl.pallas_call(
        body, name=name, out_shape=jax.ShapeDtypeStruct((B, S, W), out_dtype), grid=(B, S // ts),
        in_specs=in_specs, out_specs=tok, compiler_params=_cparams(("parallel", "parallel")),
    )(*args)


def _qrope_bwd(dq_t, cs, sn_neg, name):
    B, npair, nq, _, tq = dq_t.shape

    def body(d_ref, cs_ref, sn_ref, o_ref):
        for p in range(npair):
            tile = jnp.transpose(d_ref[0, p, 0])
            for hh in range(2):
                lo = (2 * p + hh) * LANES
                o_ref[0, :, lo:lo + LANES] = _rope_group(tile[:, hh * LANES:(hh + 1) * LANES], cs_ref[...],
                                                         sn_ref[...]).astype(o_ref.dtype)

    tab = pl.BlockSpec((tq, LANES), lambda b, i: (i, 0))
    return pl.pallas_call(
        body, name=name, out_shape=jax.ShapeDtypeStruct((B, nq * tq, N_HEADS * LANES), BF16), grid=(B, nq),
        in_specs=[pl.BlockSpec((1, npair, 1, 2 * LANES, tq), lambda b, i: (b, 0, i, 0, 0)), tab, tab],
        out_specs=pl.BlockSpec((1, tq, N_HEADS * LANES), lambda b, i: (b, i, 0)),
        compiler_params=_cparams(("parallel", "parallel")),
    )(dq_t, cs, sn_neg)


def _krope_bwd(dkc, cs, sn_neg, name):
    B, S, W = dkc.shape
    G = W // LANES
    ts = ROW_TILE

    def body(d_ref, cs_ref, sn_ref, o_ref):
        acc = d_ref[0, :, 0:LANES]
        for gi in range(1, G):
            acc = acc + d_ref[0, :, gi * LANES:(gi + 1) * LANES]
        lane = lax.broadcasted_iota(jnp.int32, acc.shape, 1)
        rot = (lane >= NOPE_DIM) & (lane < NOPE_DIM + ROPE_DIM)
        acc = jnp.where(rot, acc, 0.0)
        o_ref[0] = _rope_group(acc, cs_ref[...], sn_ref[...]).astype(o_ref.dtype)

    tab = pl.BlockSpec((ts, LANES), lambda b, s: (s, 0))
    return pl.pallas_call(
        body, name=name, out_shape=jax.ShapeDtypeStruct((B, S, LANES), BF16), grid=(B, S // ts),
        in_specs=[pl.BlockSpec((1, ts, W), lambda b, s: (b, s, 0)), tab, tab],
        out_specs=pl.BlockSpec((1, ts, LANES), lambda b, s: (b, s, 0)),
        compiler_params=_cparams(("parallel", "parallel")),
    )(dkc, cs, sn_neg)


def _t5_bucket(dist):
    max_exact = N_BUCKETS // 2
    d = np.maximum(dist, 1).astype(np.float64)
    large = max_exact + (np.log(d / max_exact) / np.log(MAX_DISTANCE / max_exact)
                         * (N_BUCKETS - max_exact)).astype(np.int64)
    large = np.minimum(large, N_BUCKETS - 1)
    return np.where(dist < max_exact, dist, large).astype(np.int32)


def _band_buckets(dilation):
    a = np.arange(BLK)[None, :]
    bk = np.arange(2 * BLK)[:, None]
    steps = BLK + a - bk
    return _t5_bucket(np.clip(steps, 0, SPAN) * dilation)


def _head_mask(shape, hh):
    lane = lax.broadcasted_iota(jnp.int32, shape, 1)
    return (lane >= hh * HEAD_DIM) & (lane < (hh + 1) * HEAD_DIM)


def _dot_nt(a, b):
    return lax.dot_general(a, b, (((1,), (1,)), ((), ())), preferred_element_type=F32)


def _dot_tn(a, b):
    return lax.dot_general(a, b, (((0,), (0,)), ((), ())), preferred_element_type=F32)


def _dot_nn(a, b):
    return lax.dot_general(a, b, (((1,), (0,)), ((), ())), preferred_element_type=F32)


def _dil_fwd(qkv, bias, branch, dilation, name, comm=None):
    B, n, _ = qkv.shape
    d = dilation
    nb = n // BLK
    qkv_v = qkv
    npair = N_HEADS // 2

    def body(cur_ref, prev_ref, bias_ref, o_ref, lse_ref, s_scr, e_scr):
        first = jnp.where(pl.program_id(1) == 0, 1, 0)
        units = [(b, h) for b in range(B) for h in range(N_HEADS)]
        for b in range(B):
            for p in range(npair):
                q = cur_ref[b, :, p * LANES:(p + 1) * LANES] * DIL_SCALE
                kc = cur_ref[b, :, D_A + p * LANES:D_A + (p + 1) * LANES]
                kp = prev_ref[b, :, D_A + p * LANES:D_A + (p + 1) * LANES]
                for hh in range(2):
                    u = b * N_HEADS + 2 * p + hh
                    qm = jnp.where(_head_mask((BLK, LANES), hh), q, jnp.zeros_like(q))
                    s_scr[u, 0:BLK, :] = _dot_nt(kp, qm)
                    s_scr[u, BLK:2 * BLK, :] = _dot_nt(kc, qm)
        ms = []
        for u, (b, h) in enumerate(units):
            s_p = s_scr[u, 0:BLK, :] + bias_ref[first, h, 0:BLK, :]
            s_c = s_scr[u, BLK:2 * BLK, :] + bias_ref[first, h, BLK:2 * BLK, :]
            m = jnp.maximum(jnp.max(s_p, axis=0, keepdims=True), jnp.max(s_c, axis=0, keepdims=True))
            e_scr[u, 0:BLK, :] = jnp.exp(s_p - m).astype(BF16)
            e_scr[u, BLK:2 * BLK, :] = jnp.exp(s_c - m).astype(BF16)
            ms.append(m)
        rows0 = _row_mask((LANES, BLK), 0)
        for b in range(B):
            for p in range(npair):
                sl = slice(p * LANES, (p + 1) * LANES)
                vsl = slice(2 * D_A + p * LANES, 2 * D_A + (p + 1) * LANES)
                vct = jnp.transpose(cur_ref[b, :, vsl].astype(F32)).astype(BF16)
                vpt = jnp.transpose(prev_ref[b, :, vsl].astype(F32)).astype(BF16)
                acc = []
                for hh in range(2):
                    u = b * N_HEADS + 2 * p + hh
                    mine = _row_mask((LANES, BLK), hh)
                    one = jnp.ones_like(vct)
                    acc.append(_dot_nn(jnp.where(mine, vpt, one), e_scr[u, 0:BLK, :])
                               + _dot_nn(jnp.where(mine, vct, one), e_scr[u, BLK:2 * BLK, :]))
                l0 = acc[0][HEAD_DIM:HEAD_DIM + 1, :]
                l1 = acc[1][0:1, :]
                u0 = b * N_HEADS + 2 * p
                o_t = jnp.where(rows0, acc[0] / l0, acc[1] / l1)
                lse_t = jnp.where(rows0, ms[u0] + jnp.log(l0), ms[u0 + 1] + jnp.log(l1))
                o_ref[b, :, sl] = jnp.transpose(o_t)
                lse_ref[b, :, sl] = jnp.transpose(lse_t)

    cur = pl.BlockSpec((B, BLK, P_QKV), lambda r, i: (0, i, r))
    prev = pl.BlockSpec((B, BLK, P_QKV), lambda r, i: (0, jnp.maximum(i - 1, 0), r))
    out = pl.BlockSpec((B, BLK, D_A), lambda r, i: (0, i, r))
    return _host_call(
        body, comm, name=name,
        out_shape=[jax.ShapeDtypeStruct((B, n, d * D_A), F32)] * 2,
        grid=(d, nb),
        in_specs=[cur, prev,
                  pl.BlockSpec((None, 2, N_HEADS, 2 * BLK, BLK), lambda r, i: (branch, 0, 0, 0, 0))],
        out_specs=[out, out],
        scratch_shapes=[pltpu.VMEM((B * N_HEADS, 2 * BLK, BLK), F32),
                        pltpu.VMEM((B * N_HEADS, 2 * BLK, BLK), BF16)],
        args=(qkv_v, qkv_v, bias))


VIEW_TILE = 512


def _view_spec(d, w):
    return pl.BlockSpec((1, VIEW_TILE // d, d * w), lambda b, t: (b, t, 0))


def _view_shape(B, S, d, w, dtype):
    return jax.ShapeDtypeStruct((B, S // d, d * w), dtype)


def _tile_scratch(w):
    return pltpu.VMEM((w // LANES, VIEW_TILE, LANES), F32)


def _put_tile(tile_ref, val):
    for c in range(tile_ref.shape[0]):
        tile_ref[c] = val[:, c * LANES:(c + 1) * LANES]


def _get_tile(tile_ref):
    return jnp.concatenate([tile_ref[c] for c in range(tile_ref.shape[0])], axis=1)


def _tile_to_view(tile_ref, view_ref, d, w):
    for c in range(w // LANES):
        for r in range(d):
            lo = r * w + c * LANES
            rows = tile_ref.at[c][pl.ds(r, VIEW_TILE // d, stride=d), :]
            view_ref[0, :, lo:lo + LANES] = rows.astype(view_ref.dtype)


def _view_to_tile(view_ref, tile_ref, d, w):
    for c in range(w // LANES):
        for r in range(d):
            lo = r * w + c * LANES
            tile_ref.at[c][pl.ds(r, VIEW_TILE // d, stride=d), :] = view_ref[0, :, lo:lo + LANES].astype(F32)


def _mm_qkv_views(h, w, name):
    B, S, D = h.shape
    N = w.shape[1]

    def body(h_ref, w_ref, o1_ref, o4_ref, o16_ref, acc_ref):
        acc = jnp.dot(h_ref[0], w_ref[...], preferred_element_type=F32)
        o1_ref[0] = acc.astype(o1_ref.dtype)
        _put_tile(acc_ref, acc)
        _tile_to_view(acc_ref, o4_ref, DILATIONS[1], N)
        _tile_to_view(acc_ref, o16_ref, DILATIONS[2], N)

    return pl.pallas_call(
        body, name=name,
        out_shape=[_view_shape(B, S, d, N, BF16) for d in DILATIONS],
        grid=(B, S // VIEW_TILE),
        in_specs=[pl.BlockSpec((1, VIEW_TILE, D), lambda b, t: (b, t, 0)), pl.BlockSpec((D, N), lambda b, t: (0, 0))],
        out_specs=[_view_spec(d, N) for d in DILATIONS],
        scratch_shapes=[_tile_scratch(N)],
        compiler_params=_cparams(("parallel", "parallel")),
    )(h, w)


def _dil_merge(os_, lses, name):
    B, S, W = os_[0].shape
    nd = len(DILATIONS)

    def body(*refs):
        o_refs, l_refs = refs[:nd], refs[nd:2 * nd]
        out_refs, L_refs = refs[2 * nd:3 * nd], refs[3 * nd:4 * nd]
        scr = refs[4 * nd:]
        o_tok, l_tok = [o_refs[0][0]], [l_refs[0][0]]
        for i, d in enumerate(DILATIONS[1:]):
            _view_to_tile(o_refs[i + 1], scr[2 * i], d, W)
            _view_to_tile(l_refs[i + 1], scr[2 * i + 1], d, W)
            o_tok.append(_get_tile(scr[2 * i]))
            l_tok.append(_get_tile(scr[2 * i + 1]))
        a0, a1, a2 = l_tok
        m = jnp.maximum(jnp.maximum(a0, a1), a2)
        e0, e1, e2 = jnp.exp(a0 - m), jnp.exp(a1 - m), jnp.exp(a2 - m)
        ssum = e0 + e1 + e2
        out = (e0 * o_tok[0] + e1 * o_tok[1] + e2 * o_tok[2]) / ssum
        lse = m + jnp.log(ssum)
        out_refs[0][0] = out
        L_refs[0][0] = lse
        res_o, res_l = scr[2 * (nd - 1)], scr[2 * (nd - 1) + 1]
        _put_tile(res_o, out)
        _put_tile(res_l, lse)
        for i, d in enumerate(DILATIONS[1:]):
            _tile_to_view(res_o, out_refs[i + 1], d, W)
            _tile_to_view(res_l, L_refs[i + 1], d, W)

    specs = [_view_spec(d, W) for d in DILATIONS]
    shapes = [_view_shape(B, S * DILATIONS[0], d, W, F32) for d in DILATIONS]
    res = pl.pallas_call(
        body, name=name, out_shape=shapes * 2, grid=(B, S // VIEW_TILE),
        in_specs=specs * 2, out_specs=specs * 2,
        scratch_shapes=[_tile_scratch(W)] * (2 * nd),
        compiler_params=_cparams(("parallel", "parallel")),
    )(*os_, *lses)
    return res[:nd], res[nd:]


def _dil_bwd(qkv, do, out_a, L, bias, branch, dilation, name, comm=None):
    B, n, _ = qkv.shape
    d = dilation
    nb = n // BLK
    qkv_v, do_v, oa_v, L_v = qkv, do, out_a, L
    npair = N_HEADS // 2
    multi = nb > 1

    tiles = ("P", "C", "N") if multi else ("C",)
    n_t = len(tiles)

    def body(*refs):
        if multi:
            (cur_ref, prev_ref, next_ref, do_ref, don_ref, oa_ref, oan_ref, L_ref, Ln_ref, bias_ref,
             dqkv_ref, dbias_ref, s_scr, dp_scr, p_scr, ds_scr) = refs
        else:
            cur_ref, do_ref, oa_ref, L_ref, bias_ref, dqkv_ref, dbias_ref, s_scr, dp_scr, p_scr, ds_scr = refs
        r, i = pl.program_id(0), pl.program_id(1)

        @pl.when((r == 0) & (i == 0))
        def _():
            dbias_ref[...] = jnp.zeros_like(dbias_ref)

        first = jnp.where(i == 0, 1, 0)
        variant = {"P": first, "C": first, "N": 0}
        band = {"P": slice(0, BLK), "C": slice(BLK, 2 * BLK), "N": slice(0, BLK)}
        psl = lambda p: slice(p * LANES, (p + 1) * LANES)
        ksl = lambda p: slice(D_A + p * LANES, D_A + (p + 1) * LANES)
        vsl = lambda p: slice(2 * D_A + p * LANES, 2 * D_A + (p + 1) * LANES)

        def operands(b, p, hh):
            hm = _head_mask((BLK, LANES), hh)
            mask = lambda x: jnp.where(hm, x, jnp.zeros_like(x))
            qm, dom = mask(cur_ref[b, :, psl(p)] * DIL_SCALE), mask(do_ref[b, :, psl(p)])
            ops = {"C": (cur_ref[b, :, ksl(p)], cur_ref[b, :, vsl(p)], qm, dom)}
            if multi:
                ops["P"] = (prev_ref[b, :, ksl(p)], prev_ref[b, :, vsl(p)], qm, dom)
                ops["N"] = (cur_ref[b, :, ksl(p)], cur_ref[b, :, vsl(p)],
                            mask(next_ref[b, :, psl(p)] * DIL_SCALE), mask(don_ref[b, :, psl(p)]))
            return ops

        pairs = [(b, p) for b in range(B) for p in range(npair)]
        for b, p in pairs:
            for hh in range(2):
                u = b * N_HEADS + 2 * p + hh
                ops = operands(b, p, hh)
                for t, name_t in enumerate(tiles):
                    k_t, v_t, q_t, do_t = ops[name_t]
                    s_scr[u, t] = _dot_nt(k_t, q_t)
                    dp_scr[u, t] = _dot_nt(v_t, do_t)

        def rows(L_r, do_r, oa_r, b, p):
            lt = jnp.transpose(L_r[b, :, psl(p)])
            dt = jnp.transpose(do_r[b, :, psl(p)].astype(F32) * oa_r[b, :, psl(p)])
            return ([lt[0:1, :], lt[HEAD_DIM:HEAD_DIM + 1, :]],
                    [jnp.sum(dt[:HEAD_DIM], axis=0, keepdims=True), jnp.sum(dt[HEAD_DIM:], axis=0, keepdims=True)])

        for b, p in pairs:
            lse_c, delta_c = rows(L_ref, do_ref, oa_ref, b, p)
            if multi:
                lse_n, delta_n = rows(Ln_ref, don_ref, oan_ref, b, p)
            for hh in range(2):
                h = 2 * p + hh
                u = b * N_HEADS + h
                for t, name_t in enumerate(tiles):
                    lse, delta = (lse_n[hh], delta_n[hh]) if name_t == "N" else (lse_c[hh], delta_c[hh])
                    pr = jnp.exp(s_scr[u, t] + bias_ref[variant[name_t], h, band[name_t], :] - lse)
                    if name_t == "N":
                        pr = jnp.where(i < nb - 1, pr, 0.0)
                    ds = pr * (dp_scr[u, t] - delta)
                    p_scr[u, t] = pr.astype(BF16)
                    ds_scr[u, t] = ds.astype(BF16)
                    if name_t != "N":
                        dbias_ref[h, band[name_t], :] += ds

        for b, p in pairs:
            dqt = jnp.zeros((LANES, BLK), F32)
            dk = jnp.zeros((BLK, LANES), F32)
            dv = jnp.zeros((BLK, LANES), F32)
            kct = jnp.transpose(cur_ref[b, :, ksl(p)].astype(F32)).astype(BF16)
            if multi:
                kpt = jnp.transpose(prev_ref[b, :, ksl(p)].astype(F32)).astype(BF16)
            for hh in range(2):
                u = b * N_HEADS + 2 * p + hh
                ops = operands(b, p, hh)
                mine = _row_mask((LANES, BLK), hh)
                for t, name_t in enumerate(tiles):
                    _, _, q_t, do_t = ops[name_t]
                    if name_t != "P":
                        dv = dv + _dot_nn(p_scr[u, t], do_t)
                        dk = dk + _dot_nn(ds_scr[u, t], q_t)
                    if name_t != "N":
                        kt = kpt if name_t == "P" else kct
                        dqt = dqt + _dot_nn(jnp.where(mine, kt, jnp.zeros_like(kt)), ds_scr[u, t])
            dqkv_ref[b, :, psl(p)] = jnp.transpose(dqt) * DIL_SCALE
            dqkv_ref[b, :, ksl(p)] = dk
            dqkv_ref[b, :, vsl(p)] = dv

    def at(off):
        return lambda r, i: (0, jnp.clip(i + off, 0, nb - 1), r)

    qkv_spec = lambda off: pl.BlockSpec((B, BLK, P_QKV), at(off))
    da_spec = lambda off: pl.BlockSpec((B, BLK, D_A), at(off))
    bias_spec = pl.BlockSpec((None, 2, N_HEADS, 2 * BLK, BLK), lambda r, i: (branch, 0, 0, 0, 0))
    dbias_spec = pl.BlockSpec((N_HEADS, 2 * BLK, BLK), lambda r, i: (0, 0, 0))
    if multi:
        in_specs = [qkv_spec(0), qkv_spec(-1), qkv_spec(1), da_spec(0), da_spec(1), da_spec(0), da_spec(1),
                    da_spec(0), da_spec(1), bias_spec]
        args = [qkv_v, qkv_v, qkv_v, do_v, do_v, oa_v, oa_v, L_v, L_v, bias]
    else:
        in_specs = [qkv_spec(0), da_spec(0), da_spec(0), da_spec(0), bias_spec]
        args = [qkv_v, do_v, oa_v, L_v, bias]
    return _host_call(
        body, comm, name=name,
        out_shape=[jax.ShapeDtypeStruct((B, n, d * P_QKV), F32),
                   jax.ShapeDtypeStruct((N_HEADS, 2 * BLK, BLK), F32)],
        grid=(d, nb),
        in_specs=in_specs,
        out_specs=[qkv_spec(0), dbias_spec],
        scratch_shapes=[pltpu.VMEM((B * N_HEADS, n_t, BLK, BLK), F32), pltpu.VMEM((B * N_HEADS, n_t, BLK, BLK), F32),
                        pltpu.VMEM((B * N_HEADS, n_t, BLK, BLK), BF16),
                        pltpu.VMEM((B * N_HEADS, n_t, BLK, BLK), BF16)],
        args=args)


def _sum_views_bf16(parts, name):
    B, S, W = parts[0].shape

    def body(a_ref, b_ref, c_ref, o_ref, sb, sc):
        _view_to_tile(b_ref, sb, DILATIONS[1], W)
        _view_to_tile(c_ref, sc, DILATIONS[2], W)
        o_ref[0] = (a_ref[0] + _get_tile(sb) + _get_tile(sc)).astype(o_ref.dtype)

    return pl.pallas_call(
        body, name=name, out_shape=jax.ShapeDtypeStruct((B, S, W), BF16), grid=(B, S // VIEW_TILE),
        in_specs=[_view_spec(d, W) for d in DILATIONS], out_specs=_view_spec(1, W),
        scratch_shapes=[_tile_scratch(W)] * 2,
        compiler_params=_cparams(("parallel", "parallel")),
    )(*parts)


def _bias_tables(rel_bias, buckets, name, comm=None):
    nbr = buckets.shape[0]

    def body(rb_ref, bk_ref, o_ref):
        first, h = pl.program_id(1), pl.program_id(2)
        tab = bk_ref[0]

        def step(bkt, acc):
            return jnp.where(tab == bkt, rb_ref[bkt, h], acc)

        bias = lax.fori_loop(0, N_BUCKETS, step, jnp.zeros((2 * BLK, BLK), F32))
        row = lax.broadcasted_iota(jnp.int32, (2 * BLK, BLK), 0)
        col = lax.broadcasted_iota(jnp.int32, (2 * BLK, BLK), 1)
        valid = ((row < BLK) & (row >= col) & (first == 0)) | ((row >= BLK) & (row - BLK <= col))
        o_ref[0, 0, 0] = jnp.where(valid, bias, NEG)

    (bias,), got = _host_call(
        body, comm, name=name, out_shape=[jax.ShapeDtypeStruct((nbr, 2, N_HEADS, 2 * BLK, BLK), F32)],
        grid=(nbr, 2, N_HEADS),
        in_specs=[pl.BlockSpec(memory_space=pltpu.SMEM),
                  pl.BlockSpec((1, 2 * BLK, BLK), lambda i, f, h: (i, 0, 0))],
        out_specs=[pl.BlockSpec((1, 1, 1, 2 * BLK, BLK), lambda i, f, h: (i, f, h, 0, 0))],
        scratch_shapes=[], args=(rel_bias, buckets))
    return bias, got


def _bias_grad(dbias_list, buckets, name):
    nbr = len(dbias_list)

    def body(*refs):
        d_refs, bk_ref, o_ref, part = refs[:nbr], refs[nbr], refs[nbr + 1], refs[nbr + 2]

        def step(bkt, carry):
            hit = [bk_ref[bi] == bkt for bi in range(nbr)]
            for h in range(N_HEADS):
                tot = jnp.zeros((1, BLK), F32)
                for bi in range(nbr):
                    tot = tot + jnp.sum(jnp.where(hit[bi], d_refs[bi][h], 0.0), axis=0, keepdims=True)
                part[bkt, h:h + 1, :] = tot
            return carry

        lax.fori_loop(0, N_BUCKETS, step, 0)
        lane = lax.broadcasted_iota(jnp.int32, (N_HEADS, LANES), 1)
        acc = jnp.zeros((N_HEADS, LANES), F32)
        for bkt in range(N_BUCKETS):
            acc = acc + jnp.where(lane == bkt, jnp.sum(part[bkt], axis=1, keepdims=True), 0.0)
        o_ref[...] = acc

    band = pl.BlockSpec((N_HEADS, 2 * BLK, BLK), lambda i: (0, 0, 0))
    return pl.pallas_call(
        body, name=name, out_shape=jax.ShapeDtypeStruct((N_HEADS, LANES), F32), grid=(1,),
        in_specs=[band] * nbr + [pl.BlockSpec((nbr, 2 * BLK, BLK), lambda i: (0, 0, 0))],
        out_specs=pl.BlockSpec((N_HEADS, LANES), lambda i: (0, 0)),
        scratch_shapes=[pltpu.VMEM((N_BUCKETS, N_HEADS, BLK), F32)],
        compiler_params=_cparams(("arbitrary",)),
    )(*dbias_list, buckets)


MLA_TQ = 256
MLA_TK = 256


LOG2E = math.log2(math.e)
MLA_C = MLA_SCALE * LOG2E


def _key_le_query(tk, tq):
    return lax.broadcasted_iota(jnp.int32, (tk, tq), 0) <= lax.broadcasted_iota(jnp.int32, (tk, tq), 1)


def _row_mask(shape, hh):
    row = lax.broadcasted_iota(jnp.int32, shape, 0)
    return (row >= hh * HEAD_DIM) & (row < (hh + 1) * HEAD_DIM)


def _host_call(body, comm, *, name, grid, in_specs, out_specs, out_shape, scratch_shapes, args):
    sem = ("arbitrary",) * len(grid)
    if comm is None:
        res = pl.pallas_call(body, name=name, grid=grid, in_specs=in_specs, out_specs=out_specs,
                             out_shape=out_shape, scratch_shapes=scratch_shapes,
                             compiler_params=_cparams(sem))(*args)
        return res, []
    n_in, n_out, n_s, cn = len(in_specs), len(out_specs), len(scratch_shapes), comm.n

    def hosted(*refs):
        ins, refs = refs[:n_in], refs[n_in:]
        c_ins, refs = refs[:cn], refs[cn:]
        outs, refs = refs[:n_out], refs[n_out:]
        c_outs, refs = refs[:cn], refs[cn:]
        scr, c_sems = refs[:n_s], refs[n_s:]
        ids = [pl.program_id(a) for a in range(len(grid))]
        first = functools.reduce(jnp.logical_and, [i == 0 for i in ids])
        last = functools.reduce(jnp.logical_and, [i == g - 1 for i, g in zip(ids, grid)])

        @pl.when(first)
        def _():
            comm.start(c_ins, c_outs, c_sems)

        body(*ins, *outs, *scr)

        @pl.when(last)
        def _():
            comm.finish(c_ins, c_outs, c_sems)

    res = pl.pallas_call(
        hosted, name=name, grid=grid, in_specs=list(in_specs) + _hbm_specs(cn),
        out_specs=list(out_specs) + _hbm_specs(cn), out_shape=list(out_shape) + list(comm.out_shape),
        scratch_shapes=list(scratch_shapes) + list(comm.scratch), compiler_params=_cparams(sem),
    )(*args, *comm.inputs)
    return res[:n_out], res[n_out:]


def _mla_fwd_t(q, k, vt, name, comm=None):
    B, S, _ = q.shape
    tq, tk = MLA_TQ, MLA_TK
    assert tq == tk
    npair = N_HEADS // 2
    nq = S // tq

    def body(q_ref, k_ref, vt_ref, o_ref, lse_ref, s_scr, e_scr, acc_scr, m_scr, a_scr):
        i = pl.program_id(1)
        diag = _key_le_query(tk, tq)
        m_scr[...] = jnp.full_like(m_scr, NEG)
        acc_scr[...] = jnp.zeros_like(acc_scr)

        def step(j, masked):
            rows = pl.ds(pl.multiple_of(j * tk, tk), tk)
            for h in range(N_HEADS):
                hsl = slice(h * LANES, (h + 1) * LANES)
                s_scr[h] = _dot_nt(k_ref[0, rows, hsl], q_ref[0, :, hsl])
            for h in range(N_HEADS):
                s = s_scr[h]
                if masked:
                    s = jnp.where(diag, s, NEG)
                m_old = m_scr[h:h + 1, :]
                m_new = jnp.maximum(m_old, jnp.max(s, axis=0, keepdims=True))
                a_scr[h:h + 1, :] = jnp.exp2((m_old - m_new) * MLA_C)
                e_scr[h] = jnp.exp2((s - m_new) * MLA_C).astype(BF16)
                m_scr[h:h + 1, :] = m_new
            for h in range(N_HEADS):
                vj = vt_ref[0, h // 2, j]
                vh = jnp.where(_row_mask(vj.shape, h % 2), vj, jnp.ones_like(vj))
                acc_scr[h] = acc_scr[h] * a_scr[h:h + 1, :] + _dot_nn(vh, e_scr[h])

        def loop_body(j, carry):
            step(j, False)
            return carry

        lax.fori_loop(0, i, loop_body, 0)
        step(i, True)
        rows0 = _row_mask((LANES, tq), 0)
        for p in range(npair):
            l0 = acc_scr[2 * p, HEAD_DIM:HEAD_DIM + 1, :]
            l1 = acc_scr[2 * p + 1, 0:1, :]
            o_t = jnp.where(rows0, acc_scr[2 * p] / l0, acc_scr[2 * p + 1] / l1)
            o_ref[0, :, p * LANES:(p + 1) * LANES] = jnp.transpose(o_t)
            lse_ref[0, p, 0] = jnp.zeros((8, tq), F32)
            lse_ref[0, p, 0, 0:1, :] = m_scr[2 * p:2 * p + 1, :] * MLA_C + jnp.log(l0) * LOG2E
            lse_ref[0, p, 0, 1:2, :] = m_scr[2 * p + 1:2 * p + 2, :] * MLA_C + jnp.log(l1) * LOG2E

    return _host_call(
        body, comm, name=name,
        out_shape=[jax.ShapeDtypeStruct((B, S, D_B), F32), jax.ShapeDtypeStruct((B, npair, nq, 8, tq), F32)],
        grid=(B, nq),
        in_specs=[pl.BlockSpec((1, tq, N_HEADS * LANES), lambda b, i: (b, i, 0)),
                  pl.BlockSpec((1, S, N_HEADS * LANES), lambda b, i: (b, 0, 0)),
                  pl.BlockSpec((1, npair, S // tk, LANES, tk), lambda b, i: (b, 0, 0, 0, 0))],
        out_specs=[pl.BlockSpec((1, tq, D_B), lambda b, i: (b, i, 0)),
                   pl.BlockSpec((1, npair, 1, 8, tq), lambda b, i: (b, 0, i, 0, 0))],
        scratch_shapes=[pltpu.VMEM((N_HEADS, tk, tq), F32), pltpu.VMEM((N_HEADS, tk, tq), BF16),
                        pltpu.VMEM((N_HEADS, LANES, tq), F32), pltpu.VMEM((N_HEADS, tq), F32),
                        pltpu.VMEM((N_HEADS, tq), F32)],
        args=(q, k, vt))


def _mla_delta(do, o, name):
    B, S, _ = o.shape
    tq = MLA_TQ
    npair = N_HEADS // 2

    def body(do_ref, o_ref, d_ref):
        d_ref[...] = jnp.zeros_like(d_ref)
        for p in range(npair):
            sl = slice(p * LANES, (p + 1) * LANES)
            prod_t = jnp.transpose(do_ref[0, :, sl].astype(F32) * o_ref[0, :, sl])
            d_ref[0, p, 0, 0:1, :] = jnp.sum(prod_t[:HEAD_DIM], axis=0, keepdims=True)
            d_ref[0, p, 0, 1:2, :] = jnp.sum(prod_t[HEAD_DIM:], axis=0, keepdims=True)

    tok = pl.BlockSpec((1, tq, D_B), lambda b, i: (b, i, 0))
    return pl.pallas_call(
        body, name=name, out_shape=jax.ShapeDtypeStruct((B, npair, S // tq, 8, tq), F32),
        grid=(B, S // tq), in_specs=[tok, tok],
        out_specs=pl.BlockSpec((1, npair, 1, 8, tq), lambda b, i: (b, 0, i, 0, 0)),
        compiler_params=_cparams(("parallel", "parallel")),
    )(do, o)


def _mla_bwd_t(q, k, v, do, lse, delta, name, comm=None):
    B, S, _ = q.shape
    tq, tk = MLA_TQ, MLA_TK
    assert tq == tk
    npair = N_HEADS // 2
    nq = S // tq

    hg = N_HEADS
    pg = hg // 2
    ngroup = N_HEADS // hg

    def body(q_ref, do_ref, lse_ref, dl_ref, k_ref, v_ref, dk_ref, dv_ref, dq_ref,
             s_scr, dp_scr, p_scr, ds_scr, dk_s, dv_s, kt_s):
        j = pl.program_id(2)

        @pl.when(j == 0)
        def _():
            dq_ref[...] = jnp.zeros_like(dq_ref)

        dk_s[...] = jnp.zeros_like(dk_s)
        dv_s[...] = jnp.zeros_like(dv_s)
        diag = _key_le_query(tk, tq)
        hsl = lambda h: slice(h * LANES, (h + 1) * LANES)
        for h in range(hg):
            kt_s[h] = jnp.transpose(k_ref[0, :, hsl(h)].astype(F32)).astype(BF16)

        def step(i, masked):
            rows = pl.ds(pl.multiple_of(i * tq, tq), tq)

            def dom(h):
                dov = do_ref[0, rows, hsl(h // 2)]
                return jnp.where(_head_mask((tq, LANES), h % 2), dov, jnp.zeros_like(dov))

            for h in range(hg):
                s_scr[h] = _dot_nt(k_ref[0, :, hsl(h)], q_ref[0, rows, hsl(h)])
                dp_scr[h] = _dot_nt(v_ref[0, :, hsl(h // 2)], dom(h))
            for h in range(hg):
                pr = jnp.exp2(s_scr[h] * MLA_C - lse_ref[0, h // 2, i, h % 2:h % 2 + 1, :])
                if masked:
                    pr = jnp.where(diag, pr, 0.0)
                p_scr[h] = pr.astype(BF16)
                ds_scr[h] = (pr * (dp_scr[h] - dl_ref[0, h // 2, i, h % 2:h % 2 + 1, :])).astype(BF16)
            for h in range(hg):
                dv_s[h // 2] += _dot_nn(p_scr[h], dom(h))
                dk_s[h] += _dot_nn(ds_scr[h], q_ref[0, rows, hsl(h)])
                dq_ref[0, h // 2, i, hsl(h % 2), :] += _dot_nn(kt_s[h], ds_scr[h]) * MLA_SCALE

        step(j, True)

        def loop_body(i, carry):
            step(i, False)
            return carry

        lax.fori_loop(j + 1, nq, loop_body, 0)
        for h in range(hg):
            dk_ref[0, :, hsl(h)] = dk_s[h] * MLA_SCALE
        for p in range(pg):
            dv_ref[0, :, hsl(p)] = dv_s[p]

    stat = pl.BlockSpec((1, pg, nq, 8, tq), lambda b, g, j: (b, g, 0, 0, 0))
    return _host_call(
        body, comm, name=name,
        out_shape=[jax.ShapeDtypeStruct((B, S, N_HEADS * LANES), F32), jax.ShapeDtypeStruct((B, S, D_B), F32),
                   jax.ShapeDtypeStruct((B, npair, nq, 2 * LANES, tq), F32)],
        grid=(B, ngroup, S // tk),
        in_specs=[pl.BlockSpec((1, S, hg * LANES), lambda b, g, j: (b, 0, g)),
                  pl.BlockSpec((1, S, pg * LANES), lambda b, g, j: (b, 0, g)),
                  stat, stat,
                  pl.BlockSpec((1, tk, hg * LANES), lambda b, g, j: (b, j, g)),
                  pl.BlockSpec((1, tk, pg * LANES), lambda b, g, j: (b, j, g))],
        out_specs=[pl.BlockSpec((1, tk, hg * LANES), lambda b, g, j: (b, j, g)),
                   pl.BlockSpec((1, tk, pg * LANES), lambda b, g, j: (b, j, g)),
                   pl.BlockSpec((1, pg, nq, 2 * LANES, tq), lambda b, g, j: (b, g, 0, 0, 0))],
        scratch_shapes=[pltpu.VMEM((hg, tk, tq), F32), pltpu.VMEM((hg, tk, tq), F32),
                        pltpu.VMEM((hg, tk, tq), BF16), pltpu.VMEM((hg, tk, tq), BF16),
                        pltpu.VMEM((hg, tk, LANES), F32), pltpu.VMEM((pg, tk, LANES), F32),
                        pltpu.VMEM((hg, LANES, tk), BF16)],
        args=(q, do, lse, delta, k, v))


def _bucket_tables():
    return jnp.asarray(np.stack([_band_buckets(d) for d in DILATIONS]))


def _local_step(x, target, mod, wts, gains, rel_bias, ffn_shards=None, bias=None):
    B, S, D = x.shape
    T = B * S
    sh1, sc1, g1, sh2, sc2, g2 = [mod[:, i * D:(i + 1) * D].reshape(B, 1, D) for i in range(N_MOD)]
    cs, sn = _rope_tables()
    buckets_dev = _bucket_tables()
    if bias is None:
        bias, _ = _bias_tables(rel_bias, buckets_dev, "rel_bias_tables")
    w_in = wts["w_in"]

    h1 = _adaln_fwd(x, gains["g_norm1"], sc1, sh1, "adaln1_fwd")
    h1f = h1.reshape(T, D)
    qkv_v = _mm_qkv_views(h1, w_in[:, :P_QKV], "mm_qkv")
    rest = _mm(h1f, w_in[:, P_QKV:], "nn", F32, "mm_rest")
    o_d, lse_d = [], []
    late_got = []
    for i, d in enumerate(DILATIONS):
        comm = _GatherComm(ffn_shards[i + 1:i + 2]) if (ffn_shards and i < 2) else None
        (o_i, lse_i), got = _dil_fwd(qkv_v[i], bias, i, d, f"dil_fwd_{d}", comm)
        late_got += list(got)
        o_d.append(o_i)
        lse_d.append(lse_i)
    if ffn_shards:
        wts = dict(wts, w_out=late_got[1].reshape(D, D))
    out_a_v, lse_a_v = _dil_merge(o_d, lse_d, "dil_merge")
    out_a = out_a_v[0]
    cqn = _rms_fwd(rest, 1, Q_LORA, gains["g_cq"], "rms_cq_fwd")
    ckvn = _rms_fwd(rest, 0, KV_LORA, gains["g_ckv"], "rms_ckv_fwd")
    rest3 = rest.reshape(B, S, P_REST)
    q_raw = _mm(cqn, wts["w_uq"], "nn", F32, "mm_uq").reshape(B, S, N_HEADS * LANES)
    qc = _rope_apply(q_raw, cs, sn, BF16, "rope_q")
    kn_raw = _mm(ckvn, wts["w_kv"][:, :N_HEADS * LANES], "nn", F32, "mm_uk").reshape(B, S, N_HEADS * LANES)
    kc = _rope_apply(kn_raw, cs, sn, BF16, "rope_k", add=rest3, add_blk=KV_LORA // LANES)
    v = _mm(ckvn, wts["w_kv"][:, N_HEADS * LANES:], "nn", BF16, "mm_uv").reshape(B, S, D_B)
    vt = jnp.transpose(v.reshape(B, S // MLA_TK, MLA_TK, N_HEADS // 2, LANES), (0, 3, 1, 4, 2))
    (out_b, lse_b), got = _mla_fwd_t(qc, kc, vt, "mla_fwd", _GatherComm(ffn_shards[:1]) if ffn_shards else None)
    if ffn_shards:
        wts = dict(wts, w_ffn_in=got[0].reshape(N_CHIP, D, -1), w_ffn_out=late_got[0].reshape(D_FF, D))
    out_af, out_bf = out_a.reshape(T, D_A), out_b.reshape(T, D_B)
    y = _rms_fwd_pair(out_af, out_bf, gains["g_out_a"], gains["g_out_b"], "rms_out_fwd")
    mix = _mm(y, wts["w_out"], "nn", F32, "mm_out").reshape(B, S, D)
    h2, x1 = _adaln_fwd(x, gains["g_norm2"], sc2, sh2, "adaln2_fwd", mix=mix, gate=g1)
    h2f = h2.reshape(T, D)
    gu, act = _ffn_in_fwd(h2f, wts["w_ffn_in"], "mm_ffn_in")
    f = _mm(act, wts["w_ffn_out"], "nn", F32, "mm_ffn_out").reshape(B, S, D)
    dx2, df, dg2, dg_final, loss = _final_loss(x1, f, g2, gains["g_final"], target, "final_loss")

    dff = df.reshape(T, D)
    dgu = _ffn_out_bwd(dff, wts["w_ffn_out"], gu, "mm_ffn_out_dx")
    gw_ffn_out = _mm(act, dff, "tn", F32, "mm_ffn_out_dw")
    dh2 = _mm(dgu, wts["w_ffn_in"], "nt", F32, "mm_ffn_in_dx", col_blocks=N_CHIP, halves=True).reshape(B, S, D)
    gw_ffn_in = _mm(h2f, dgu, "tn", F32, "mm_ffn_in_dw", col_blocks=N_CHIP, halves=True)
    ffn_g8 = ffn_r1 = None
    if ffn_shards:
        ffn_g8 = [gw_ffn_in.reshape(N_DEV, -1, gw_ffn_in.shape[-1]), gw_ffn_out.reshape(N_DEV, -1, D)]
        dx1, dsh2, dsc2, dg_norm2, dg1, dmix, ffn_r1 = _adaln_bwd(
            dh2, x1, gains["g_norm2"], sc2, dx2, "adaln2_bwd", mix=mix, gate=g1, comm=_ToSiblingComm(ffn_g8))
    else:
        dx1, dsh2, dsc2, dg_norm2, dg1, dmix = _adaln_bwd(dh2, x1, gains["g_norm2"], sc2, dx2, "adaln2_bwd",
                                                          mix=mix, gate=g1)
    dmixf = dmix.reshape(T, D)
    dy = _mm(dmixf, wts["w_out"], "nt", F32, "mm_out_dx")
    gw_out = _mm(y, dmixf, "tn", F32, "mm_out_dw")
    do_a_v, dg_out_a = _rms_bwd_views(dy, 0, out_a, gains["g_out_a"], "rms_outa_bwd")
    do_b, dg_out_b = _rms_bwd(dy, 1, out_bf, 0, D_B, gains["g_out_b"], "rms_outb_bwd")
    do_b3 = do_b.reshape(B, S, D_B)
    delta_b = _mla_delta(do_b3, out_b, "mla_delta")
    ffn_a4 = ffn_send = None
    if ffn_shards:
        ffn_a4, ffn_send = _rs_first(ffn_g8, "ffn", r1=ffn_r1)
    (dkc, dv, dq_t), ffn_r2 = _mla_bwd_t(qc, kc, v, do_b3, lse_b, delta_b, "mla_bwd",
                                         _ToChipsComm(ffn_send[:1]) if ffn_shards else None)
    dq_raw = _qrope_bwd(dq_t, cs, -sn, "rope_q_bwd").reshape(T, N_HEADS * LANES)
    dkrw = _krope_bwd(dkc, cs, -sn, "rope_k_bwd").reshape(T, LANES)
    dcqn = _mm(dq_raw, wts["w_uq"], "nt", F32, "mm_uq_dx")
    gw_uq = _mm(cqn, dq_raw, "tn", F32, "mm_uq_dw")
    dkv = jnp.concatenate([dkc.reshape(T, -1), dv.reshape(T, -1)], axis=1).astype(BF16)
    dckvn = _mm(dkv, wts["w_kv"], "nt", F32, "mm_ukv_dx")
    gw_kv = _mm(ckvn, dkv, "tn", F32, "mm_ukv_dw")
    dcq, dg_cq = _rms_bwd(dcqn, 0, rest, 1, Q_LORA, gains["g_cq"], "rms_cq_bwd")
    dckv, dg_ckv = _rms_bwd(dckvn, 0, rest, 0, KV_LORA, gains["g_ckv"], "rms_ckv_bwd")
    dqkv_d, dbias_d = [], []
    for i, d in enumerate(DILATIONS):
        comm = _ToChipsComm(ffn_send[1:]) if (ffn_shards and i == 0) else None
        (dqkv_i, dbias_i), got = _dil_bwd(qkv_v[i], do_a_v[i], out_a_v[i], lse_a_v[i], bias, i, d,
                                          f"dil_bwd_{d}", comm)
        if comm is not None:
            ffn_r2 = list(ffn_r2) + list(got)
        dqkv_d.append(dqkv_i)
        dbias_d.append(dbias_i)
    dqkv = _sum_views_bf16(dqkv_d, "dil_bwd_sum").reshape(T, P_QKV)
    g_rel_bias = _bias_grad(dbias_d, buckets_dev, "rel_bias_grad")[:, :N_BUCKETS].T
    dproj = jnp.concatenate([dqkv, dckv, dkrw, dcq], axis=1)
    gw_in = _mm(h1f, dproj, "tn", F32, "mm_in_dw")
    mix_a4 = mix_r2 = None
    if ffn_shards:
        nat = [_w_in_from_kernel(gw_in), _w_uq_from_kernel(gw_uq), _w_ukv_from_kernel(gw_kv)]
        g8 = [_shards_from_full(g) for g in nat] + [gw_out]
        mix_a4, mix_send = _rs_first([g.reshape(N_DEV, -1, g.shape[-1]) for g in g8], "mix")
        dh1, mix_r2 = _mm(dproj, w_in, "nt", F32, "mm_in_dx", comm=_ToChipsComm(mix_send))
    else:
        dh1 = _mm(dproj, w_in, "nt", F32, "mm_in_dx")
    dh1 = dh1.reshape(B, S, D)
    grad_x, dsh1, dsc1, dg_norm1 = _adaln_bwd(dh1, x, gains["g_norm1"], sc1, dx1, "adaln1_bwd")
    gmod = jnp.concatenate([dsh1, dsc1, dg1, dsh2, dsc2, dg2], axis=-1).reshape(B, N_MOD * D)
    grads = dict(w_in=gw_in, w_uq=gw_uq, w_kv=gw_kv, w_out=gw_out, w_ffn_in=gw_ffn_in, w_ffn_out=gw_ffn_out,
                 g_norm1=dg_norm1, g_cq=dg_cq, g_ckv=dg_ckv, rel_bias=g_rel_bias, g_out_a=dg_out_a,
                 g_out_b=dg_out_b, g_norm2=dg_norm2, g_final=dg_final, ffn_pending=(ffn_a4, ffn_r2),
                 mix_pending=(mix_a4, mix_r2))
    return loss, grad_x, gmod, grads


def _w_in_to_kernel(w):
    z = lambda n: jnp.zeros((w.shape[0], n), w.dtype)
    i3, i4, i5 = 3 * D_A, 3 * D_A + Q_LORA, 3 * D_A + Q_LORA + KV_LORA
    return jnp.concatenate([w[:, :i3], w[:, i4:i5], z(NOPE_DIM), w[:, i5:], z(LANES - NOPE_DIM - ROPE_DIM),
                            w[:, i3:i4]], axis=1)


def _w_in_from_kernel(g):
    o = P_QKV + KV_LORA
    return jnp.concatenate([g[:, :P_QKV], g[:, o + LANES:], g[:, P_QKV:o],
                            g[:, o + NOPE_DIM:o + NOPE_DIM + ROPE_DIM]], axis=1)


def _w_uq_to_kernel(w):
    w3 = w.reshape(Q_LORA, N_HEADS, NOPE_DIM + ROPE_DIM)
    return jnp.pad(w3, ((0, 0), (0, 0), (0, LANES - NOPE_DIM - ROPE_DIM))).reshape(Q_LORA, N_HEADS * LANES)


def _w_uq_from_kernel(g):
    return g.reshape(Q_LORA, N_HEADS, LANES)[:, :, :NOPE_DIM + ROPE_DIM].reshape(Q_LORA, -1)


def _w_ukv_to_kernel(w):
    w3 = w.reshape(KV_LORA, N_HEADS, 2 * HEAD_DIM)
    wk = jnp.pad(w3[:, :, :NOPE_DIM], ((0, 0), (0, 0), (0, LANES - NOPE_DIM))).reshape(KV_LORA, N_HEADS * LANES)
    wv = w3[:, :, NOPE_DIM:].reshape(KV_LORA, D_B)
    return jnp.concatenate([wk, wv], axis=1)


def _w_ukv_from_kernel(g):
    gk = g[:, :N_HEADS * LANES].reshape(KV_LORA, N_HEADS, LANES)[:, :, :NOPE_DIM]
    gv = g[:, N_HEADS * LANES:].reshape(KV_LORA, N_HEADS, HEAD_DIM)
    return jnp.concatenate([gk, gv], axis=2).reshape(KV_LORA, -1)


MESH = pl.DeviceIdType.MESH


def _my_place():
    return lax.axis_index("x"), lax.axis_index("y"), lax.axis_index("c")


def _other_chips(x, y):
    return [(1 - x, y), (x, 1 - y), (1 - x, 1 - y)]


def _allgather8(x_shard, name, in_hbm):
    m_per, n = x_shard.shape
    space = pl.ANY if in_hbm else pltpu.VMEM

    def body(x_ref, out_ref, send_sems, recv_sems, local_sem):
        x, y, c = _my_place()
        me, sibling = (x, y, c), (x, y, 1 - c)
        chips = _other_chips(x, y)

        def rows(px, py, pc):
            return out_ref.at[pl.ds((4 * px + 2 * py + pc) * m_per, m_per), :]

        def copy(k, block, to, src=None):
            return pltpu.make_async_remote_copy(
                src_ref=rows(*block) if src is None else src, dst_ref=rows(*block),
                send_sem=send_sems.at[k], recv_sem=recv_sems.at[k], device_id=to, device_id_type=MESH)

        mine = pltpu.make_async_copy(x_ref, rows(*me), local_sem)
        mine.start()
        first = [copy(0, me, sibling, src=x_ref)]
        first += [copy(1 + j, me, (*chip, c), src=x_ref) for j, chip in enumerate(chips)]
        for cp in first:
            cp.start()
        passed = [copy(4 + j, (*chip, c), sibling) for j, chip in enumerate(chips)]
        for j, chip in enumerate(chips):
            copy(1 + j, (*chip, c), me).wait_recv()
            passed[j].start()
        copy(0, sibling, me).wait_recv()
        for j, chip in enumerate(chips):
            copy(4 + j, (*chip, 1 - c), me).wait_recv()
        for cp in first + passed:
            cp.wait_send()
        mine.wait()

    return pl.pallas_call(
        body, name=name,
        out_shape=jax.ShapeDtypeStruct((N_DEV * m_per, n), x_shard.dtype),
        in_specs=[pl.BlockSpec(memory_space=space)],
        out_specs=pl.BlockSpec(memory_space=space),
        scratch_shapes=[pltpu.SemaphoreType.DMA((7,)), pltpu.SemaphoreType.DMA((7,)), pltpu.SemaphoreType.DMA],
        compiler_params=pltpu.CompilerParams(vmem_limit_bytes=VMEM_LIMIT),
    )(x_shard)


def _hbm_specs(n):
    return [pl.BlockSpec(memory_space=pl.ANY)] * n


class _GatherComm:
    def __init__(self, shards):
        self.n = n = len(shards)
        self.inputs = [s.reshape(2, s.shape[0] // 2, s.shape[1]) for s in shards]
        self.out_shape = [jax.ShapeDtypeStruct((N_DEV,) + s.shape[1:], s.dtype) for s in self.inputs]
        self.scratch = [pltpu.SemaphoreType.DMA((7 * n,)), pltpu.SemaphoreType.DMA((7 * n,))]

    def _parts(self, xs, outs, sems):
        send_sems, recv_sems = sems
        x, y, c = _my_place()

        def blk(k, px, py, pc):
            return outs[k].at[4 * px + 2 * py + pc]

        def copy(k, kind, block, to, own=False):
            return pltpu.make_async_remote_copy(
                src_ref=xs[k].at[c] if own else blk(k, *block), dst_ref=blk(k, *block),
                send_sem=send_sems.at[7 * k + kind], recv_sem=recv_sems.at[7 * k + kind],
                device_id=to, device_id_type=MESH)

        def whole(k):
            return pltpu.make_async_remote_copy(
                src_ref=xs[k], dst_ref=outs[k].at[pl.ds(4 * x + 2 * y, 2)],
                send_sem=send_sems.at[7 * k], recv_sem=recv_sems.at[7 * k],
                device_id=(x, y, 1 - c), device_id_type=MESH)

        me, sibling = (x, y, c), (x, y, 1 - c)
        chips = _other_chips(x, y)
        first = []
        for k in range(self.n):
            first.append(whole(k))
            first += [copy(k, 1 + j, me, (*chip, c), own=True) for j, chip in enumerate(chips)]
        return copy, whole, me, sibling, chips, c, first

    def start(self, xs, outs, sems):
        for cp in self._parts(xs, outs, sems)[-1]:
            cp.start()

    def finish(self, xs, outs, sems):
        copy, whole, me, sibling, chips, c, first = self._parts(xs, outs, sems)
        passed = []
        for j, chip in enumerate(chips):
            for k in range(self.n):
                copy(k, 1 + j, (*chip, c), me).wait_recv()
                fwd = copy(k, 4 + j, (*chip, c), sibling)
                fwd.start()
                passed.append(fwd)
        for k in range(self.n):
            whole(k).wait_recv()
        for j, chip in enumerate(chips):
            for k in range(self.n):
                copy(k, 4 + j, (*chip, 1 - c), me).wait_recv()
        for cp in first + passed:
            cp.wait_send()


class _AllGatherComm:
    def __init__(self, x):
        self.inputs = [x]
        self.n = 1
        self.out_shape = [jax.ShapeDtypeStruct((N_DEV,) + x.shape, x.dtype)]
        self.scratch = [pltpu.SemaphoreType.DMA((7,)), pltpu.SemaphoreType.DMA((7,)), pltpu.SemaphoreType.DMA]

    def _parts(self, xs, outs, sems):
        send_sems, recv_sems, local_sem = sems
        x_ref, out_ref = xs[0], outs[0]
        x, y, c = _my_place()

        def copy(k, block, to, own=False):
            blk = out_ref.at[4 * block[0] + 2 * block[1] + block[2]]
            return pltpu.make_async_remote_copy(
                src_ref=x_ref if own else blk, dst_ref=blk, send_sem=send_sems.at[k], recv_sem=recv_sems.at[k],
                device_id=to, device_id_type=MESH)

        me, sibling = (x, y, c), (x, y, 1 - c)
        chips = _other_chips(x, y)
        local = pltpu.make_async_copy(x_ref, out_ref.at[4 * x + 2 * y + c], local_sem)
        first = [copy(0, me, sibling, own=True)]
        first += [copy(1 + j, me, (*chip, c), own=True) for j, chip in enumerate(chips)]
        return copy, me, sibling, chips, c, local, first

    def start(self, xs, outs, sems):
        _, _, _, _, _, local, first = self._parts(xs, outs, sems)
        for cp in [local] + first:
            cp.start()

    def finish(self, xs, outs, sems):
        copy, me, sibling, chips, c, local, first = self._parts(xs, outs, sems)
        passed = []
        for j, chip in enumerate(chips):
            copy(1 + j, (*chip, c), me).wait_recv()
            fwd = copy(4 + j, (*chip, c), sibling)
            fwd.start()
            passed.append(fwd)
        copy(0, sibling, me).wait_recv()
        for j, chip in enumerate(chips):
            copy(4 + j, (*chip, 1 - c), me).wait_recv()
        for cp in first + passed:
            cp.wait_send()
        local.wait()


class _BothComm:
    def __init__(self, a, b):
        self.a, self.b = a, b
        self.inputs = a.inputs + b.inputs
        self.n = a.n + b.n
        self.out_shape = a.out_shape + b.out_shape
        self.scratch = a.scratch + b.scratch

    def _split(self, xs, outs, sems):
        na, ns = self.a.n, len(self.a.scratch)
        return (xs[:na], outs[:na], sems[:ns]), (xs[na:], outs[na:], sems[ns:])

    def start(self, xs, outs, sems):
        pa, pb = self._split(xs, outs, sems)
        self.a.start(*pa)
        self.b.start(*pb)

    def finish(self, xs, outs, sems):
        pa, pb = self._split(xs, outs, sems)
        self.a.finish(*pa)
        self.b.finish(*pb)


class _ToChipsComm:
    def __init__(self, a4s):
        self.inputs = list(a4s)
        self.n = n = len(a4s)
        nc = N_CHIP - 1
        self.out_shape = [jax.ShapeDtypeStruct((nc,) + a.shape[1:], a.dtype) for a in a4s]
        self.scratch = [pltpu.SemaphoreType.DMA((nc * n,)), pltpu.SemaphoreType.DMA((nc * n,))]

    def _copies(self, as_, rs, sems):
        send_sems, recv_sems = sems
        x, y, c = _my_place()
        nc = N_CHIP - 1
        return [pltpu.make_async_remote_copy(
            src_ref=as_[k].at[2 * cx + cy], dst_ref=rs[k].at[j], send_sem=send_sems.at[nc * k + j],
            recv_sem=recv_sems.at[nc * k + j], device_id=(cx, cy, c), device_id_type=MESH)
            for k in range(self.n) for j, (cx, cy) in enumerate(_other_chips(x, y))]

    def start(self, as_, rs, sems):
        for cp in self._copies(as_, rs, sems):
            cp.start()

    def finish(self, as_, rs, sems):
        for cp in self._copies(as_, rs, sems):
            cp.wait()


def _run_comm(comm, name):
    n = comm.n

    def body(*refs):
        ins, outs, sems = refs[:n], refs[n:2 * n], refs[2 * n:]
        comm.start(ins, outs, sems)
        comm.finish(ins, outs, sems)

    return pl.pallas_call(
        body, name=name, out_shape=comm.out_shape, in_specs=_hbm_specs(n), out_specs=_hbm_specs(n),
        scratch_shapes=comm.scratch,
    )(*comm.inputs)


class _ToSiblingComm:
    def __init__(self, g8s):
        self.inputs = list(g8s)
        self.n = n = len(g8s)
        self.out_shape = [jax.ShapeDtypeStruct((N_CHIP,) + g.shape[1:], g.dtype) for g in g8s]
        self.scratch = [pltpu.SemaphoreType.DMA((N_CHIP * n,)), pltpu.SemaphoreType.DMA((N_CHIP * n,))]

    def _copies(self, gs, rs, sems):
        send_sems, recv_sems = sems
        x, y, c = _my_place()
        return [pltpu.make_async_remote_copy(
            src_ref=gs[k].at[2 * s + 1 - c], dst_ref=rs[k].at[s], send_sem=send_sems.at[N_CHIP * k + s],
            recv_sem=recv_sems.at[N_CHIP * k + s], device_id=(x, y, 1 - c), device_id_type=MESH)
            for k in range(self.n) for s in range(N_CHIP)]

    def start(self, gs, rs, sems):
        for cp in self._copies(gs, rs, sems):
            cp.start()

    def finish(self, gs, rs, sems):
        for cp in self._copies(gs, rs, sems):
            cp.wait()


def _swap_halves(hs, name):
    n = len(hs)

    def body(*refs):
        o_refs = refs[n:2 * n]
        send_sems, recv_sems = refs[2 * n:]
        x, y, c = _my_place()

        def remote(k, slot):
            return pltpu.make_async_remote_copy(
                src_ref=o_refs[k].at[slot], dst_ref=o_refs[k].at[slot], send_sem=send_sems.at[k],
                recv_sem=recv_sems.at[k], device_id=(x, y, 1 - c), device_id_type=MESH)

        sends = [remote(k, c) for k in range(n)]
        for cp in sends:
            cp.start()
        for k in range(n):
            remote(k, 1 - c).wait_recv()
        for cp in sends:
            cp.wait_send()

    return pl.pallas_call(
        body, name=name,
        out_shape=[jax.ShapeDtypeStruct(h.shape, h.dtype) for h in hs],
        in_specs=_hbm_specs(n), out_specs=_hbm_specs(n),
        input_output_aliases={k: k for k in range(n)},
        scratch_shapes=[pltpu.SemaphoreType.DMA((n,)), pltpu.SemaphoreType.DMA((n,))],
    )(*hs)


ADD_TILES = 2


def _add_blocks(a_list, a_idx_fn, others_list, ns, sel, name, out_blocks=None, out_idx_fn=None,
                bf16_copy=False):
    out_blocks = out_blocks or ns
    out_idx_fn = out_idx_fn or (lambda s, sel_ref: s)
    n = len(a_list)
    n_o = len(others_list[0])
    per = 1 + n_o

    def body(sel_ref, *refs):
        for k in range(n):
            ins = refs[k * per:(k + 1) * per]
            acc = ins[0][0]
            for r in ins[1:]:
                acc = acc + r[0].astype(F32)
            refs[n * per + k][0] = acc
            if bf16_copy:
                refs[n * per + n + k][0] = acc.astype(BF16)

    in_specs, args, out_specs, out_shape = [], [], [], []
    for a, others in zip(a_list, others_list):
        _, R, N = a.shape
        tr = R // ADD_TILES
        assert tr % 8 == 0, a.shape
        in_specs.append(pl.BlockSpec((1, tr, N), lambda s, i, sel_ref: (a_idx_fn(s, sel_ref), i, 0)))
        args.append(a)
        for arr, fixed in others:
            if fixed is None:
                in_specs.append(pl.BlockSpec((1, tr, N), lambda s, i, sel_ref: (s, i, 0)))
            else:
                in_specs.append(pl.BlockSpec((1, tr, N), lambda s, i, sel_ref, fixed=fixed: (fixed, i, 0)))
            args.append(arr)
        out_specs.append(pl.BlockSpec((1, tr, N), lambda s, i, sel_ref: (out_idx_fn(s, sel_ref), i, 0)))
        out_shape.append(jax.ShapeDtypeStruct((out_blocks, R, N), a.dtype))
    if bf16_copy:
        out_specs = out_specs + out_specs
        out_shape = out_shape + [jax.ShapeDtypeStruct(o.shape, BF16) for o in out_shape]
    grid_spec = pltpu.PrefetchScalarGridSpec(num_scalar_prefetch=1, grid=(ns, ADD_TILES), in_specs=in_specs,
                                             out_specs=out_specs)
    return pl.pallas_call(
        body, name=name, out_shape=out_shape, grid_spec=grid_spec,
        compiler_params=_cparams(("parallel", "parallel")),
    )(sel, *args)


def _rs_first(g8s, tag, r1=None):
    c_sel = jnp.reshape(lax.axis_index("c"), (1,)).astype(jnp.int32)
    if r1 is None:
        r1 = _run_comm(_ToSiblingComm(g8s), f"rs_to_sibling_{tag}")
    res = _add_blocks(g8s, lambda s, sel: 2 * s + sel[0], [[(r, None)] for r in r1], N_CHIP, c_sel,
                      f"rs_add_sibling_{tag}", bf16_copy=True)
    return list(res[:len(g8s)]), list(res[len(g8s):])


def _rs_last(a4s, r2s, tag):
    sel = jnp.stack([2 * lax.axis_index("x") + lax.axis_index("y"), lax.axis_index("c")]).astype(jnp.int32)
    h = _add_blocks(a4s, lambda s, sel: sel[0], [[(r, 0), (r, 1), (r, 2)] for r in r2s], 1, sel,
                    f"rs_add_chips_{tag}", out_blocks=2, out_idx_fn=lambda s, sel: sel[1])
    full = _swap_halves(h, f"rs_swap_halves_{tag}")
    return [f.reshape(2 * f.shape[1], f.shape[2]) for f in full]


def _ada_fwd(c_all, w_ada, b_ada, name):
    nb, D = c_all.shape
    ncol = w_ada.shape[1]
    tc = 512

    def body(c_ref, w_ref, b_ref, o_ref):
        cv = c_ref[...]
        cond = (cv * jax.nn.sigmoid(cv)).astype(BF16)
        o_ref[...] = jnp.dot(cond, w_ref[...].astype(BF16), preferred_element_type=F32) + b_ref[...]

    return pl.pallas_call(
        body, name=name, out_shape=jax.ShapeDtypeStruct((nb, ncol), F32), grid=(ncol // tc,),
        in_specs=[pl.BlockSpec((nb, D), lambda j: (0, 0)), pl.BlockSpec((D, tc), lambda j: (0, j)),
                  pl.BlockSpec((1, tc), lambda j: (0, j))],
        out_specs=pl.BlockSpec((nb, tc), lambda j: (0, j)),
        compiler_params=_cparams(("parallel",)),
    )(c_all, w_ada, b_ada)


def _ada_bwd(c_all, gmod_cols, name):
    nb, D = c_all.shape
    ncol = gmod_cols.shape[1]
    tc = 512

    def body(c_ref, g_ref, o_ref):
        cv = c_ref[...]
        cond = (cv * jax.nn.sigmoid(cv)).astype(BF16)
        o_ref[...] = _dot_tn(cond, g_ref[...].astype(BF16))

    return pl.pallas_call(
        body, name=name, out_shape=jax.ShapeDtypeStruct((D, ncol), F32), grid=(ncol // tc,),
        in_specs=[pl.BlockSpec((nb, D), lambda j: (0, 0)), pl.BlockSpec((nb, tc), lambda j: (0, j))],
        out_specs=pl.BlockSpec((D, tc), lambda j: (0, j)),
        compiler_params=_cparams(("parallel",)),
    )(c_all, gmod_cols)


def _adam_math(w, g, m, v):
    m = ADAM_B1 * m + (1.0 - ADAM_B1) * g
    v = ADAM_B2 * v + (1.0 - ADAM_B2) * (g * g)
    m_hat = m / (1.0 - ADAM_B1 ** ADAM_STEP)
    v_hat = v / (1.0 - ADAM_B2 ** ADAM_STEP)
    delta = -ADAM_LR * (m_hat / (jnp.sqrt(v_hat) + ADAM_EPS) + ADAM_WD * w)
    return delta, m, v


def _adamw(w, g, m, v, name):
    rows, cols = w.shape
    tr = _pick(rows, (256, 192, 176, 128, 64, 8))

    def body(w_ref, g_ref, m_ref, v_ref, d_ref, mo_ref, vo_ref):
        d, mn, vn = _adam_math(w_ref[...], g_ref[...], m_ref[...], v_ref[...])
        d_ref[...] = d
        mo_ref[...] = mn
        vo_ref[...] = vn

    spec = pl.BlockSpec((tr, cols), lambda i: (i, 0))
    return pl.pallas_call(
        body, name=name, out_shape=[jax.ShapeDtypeStruct((rows, cols), F32)] * 3, grid=(rows // tr,),
        in_specs=[spec] * 4, out_specs=[spec] * 3, compiler_params=_cparams(("parallel",)),
    )(w, g, m, v)


VEC_ROWS = 8


def _adamw_rows(w, parts, m, v, name):
    n = w.shape[1]
    P = parts.shape[0]
    assert n % (VEC_ROWS * LANES) == 0, n
    shp = (VEC_ROWS, n // VEC_ROWS)

    def body(w_ref, p_ref, m_ref, v_ref, g_ref, d_ref, mo_ref, vo_ref):
        g = p_ref[0]
        for k in range(1, P):
            g = g + p_ref[k]
        d, mn, vn = _adam_math(w_ref[...], g, m_ref[...], v_ref[...])
        g_ref[...] = g
        d_ref[...] = d
        mo_ref[...] = mn
        vo_ref[...] = vn

    vec = pl.BlockSpec(shp, lambda i: (0, 0))
    out = pl.pallas_call(
        body, name=name, out_shape=[jax.ShapeDtypeStruct(shp, F32)] * 4, grid=(1,),
        in_specs=[vec, pl.BlockSpec((P,) + shp, lambda i: (0, 0, 0)), vec, vec], out_specs=[vec] * 4,
        compiler_params=_cparams(("arbitrary",)),
    )(w.reshape(shp), parts.reshape((P,) + shp), m.reshape(shp), v.reshape(shp))
    return [o.reshape(1, n) for o in out]


_SHARDED = ("w_in", "w_uq", "w_ukv", "w_out", "w_ffn_in", "w_ffn_out")
_SMALL = (("g_norm1", 1024), ("g_cq", 384), ("g_ckv", 256), ("rel_bias", 256), ("g_out_a", 512),
          ("g_out_b", 512), ("g_norm2", 1024), ("g_final", 1024))
_SMALL_PAD = 5120


def _full_from_shards(sh):
    return jnp.transpose(sh, (1, 0, 2)).reshape(sh.shape[1], -1)


def _shards_from_full(full):
    rows, cols = full.shape
    return jnp.transpose(full.reshape(rows, N_CHIP, cols // N_CHIP), (1, 0, 2))


def kernel(x, c, w_ada, b_ada, g_norm1, w_in, g_cq, w_uq, g_ckv, w_ukv, rel_bias, g_out_a, g_out_b, w_out, g_norm2, w_ffn_in, w_ffn_out, g_final, loss_target, m_w_ada, m_b_ada, m_g_norm1, m_w_in, m_g_cq, m_w_uq, m_g_ckv, m_w_ukv, m_rel_bias, m_g_out_a, m_g_out_b, m_w_out, m_g_norm2, m_w_ffn_in, m_w_ffn_out, m_g_final, v_w_ada, v_b_ada, v_g_norm1, v_w_in, v_g_cq, v_w_uq, v_g_ckv, v_w_ukv, v_rel_bias, v_g_out_a, v_g_out_b, v_w_out, v_g_norm2, v_w_ffn_in, v_w_ffn_out, v_g_final):
    names = ["w_ada", "b_ada", "g_norm1", "w_in", "g_cq", "w_uq", "g_ckv", "w_ukv", "rel_bias", "g_out_a",
             "g_out_b", "w_out", "g_norm2", "w_ffn_in", "w_ffn_out", "g_final"]
    W = dict(zip(names, [w_ada, b_ada, g_norm1, w_in, g_cq, w_uq, g_ckv, w_ukv, rel_bias, g_out_a, g_out_b,
                         w_out, g_norm2, w_ffn_in, w_ffn_out, g_final]))
    M = dict(zip(names, [m_w_ada, m_b_ada, m_g_norm1, m_w_in, m_g_cq, m_w_uq, m_g_ckv, m_w_ukv, m_rel_bias,
                         m_g_out_a, m_g_out_b, m_w_out, m_g_norm2, m_w_ffn_in, m_w_ffn_out, m_g_final]))
    V = dict(zip(names, [v_w_ada, v_b_ada, v_g_norm1, v_w_in, v_g_cq, v_w_uq, v_g_ckv, v_w_ukv, v_rel_bias,
                         v_g_out_a, v_g_out_b, v_w_out, v_g_norm2, v_w_ffn_in, v_w_ffn_out, v_g_final]))
    B, S, D = x.shape
    mx, my, mc = _my_place()
    dev = 4 * mx + 2 * my + mc
    chip = 2 * mx + my
    pad_rows = 8

    early = ("w_in", "w_uq", "w_ukv")
    bias, got = _bias_tables(rel_bias, _bucket_tables(), "rel_bias_tables",
                             _BothComm(_AllGatherComm(jnp.pad(c, ((0, pad_rows - B), (0, 0)))),
                                       _GatherComm([W[n][0].astype(BF16) for n in early])))
    c_all = got[0][:, :B].reshape(N_DEV * B, D)

    ada_cols = w_ada.shape[-1]
    b_cols = lax.dynamic_slice_in_dim(b_ada, chip * ada_cols, ada_cols, axis=1)
    mod_cols = _ada_fwd(c_all, w_ada[0], b_cols, "ada_fwd")
    mod_all = _allgather8(mod_cols, "ag_mod", False).reshape(N_DEV, N_DEV * B, ada_cols)[0::2]
    mod_all = jnp.transpose(mod_all, (1, 0, 2)).reshape(N_DEV * B, N_MOD * D)
    mod = lax.dynamic_slice_in_dim(mod_all, dev * B, B, axis=0)

    full = {n: g.reshape((N_CHIP,) + W[n].shape[1:]) for n, g in zip(early, got[1:])}
    wts = dict(w_in=_w_in_to_kernel(_full_from_shards(full["w_in"])),
               w_uq=_w_uq_to_kernel(_full_from_shards(full["w_uq"])),
               w_kv=_w_ukv_to_kernel(_full_from_shards(full["w_ukv"])))
    gains = dict(g_norm1=g_norm1, g_cq=g_cq, g_ckv=g_ckv, g_out_a=g_out_a, g_out_b=g_out_b, g_norm2=g_norm2,
                 g_final=g_final.reshape(1, D))

    loss, grad_x, gmod, grads = _local_step(x, loss_target, mod, wts, gains, rel_bias,
                                            ffn_shards=[w_ffn_in[0].astype(BF16), w_ffn_out[0].astype(BF16),
                                                        w_out[0].astype(BF16)], bias=bias)
    loss = lax.psum(loss[0, 0], ("x", "y", "c"))

    n_small = _SMALL_PAD
    cat = lambda dct: jnp.concatenate([dct[n].reshape(1, -1) for n, _ in _SMALL]
                                      + [jnp.zeros((1, _SMALL_PAD - sum(s for _, s in _SMALL)), F32)], axis=1)
    small = cat(grads)
    rows = jnp.concatenate([gmod, jnp.pad(small, ((0, 0), (0, N_MOD * D - n_small))),
                            jnp.zeros((pad_rows - B - 1, N_MOD * D), F32)], axis=0)
    rows_all = _allgather8(rows, "ag_small", False).reshape(N_DEV, pad_rows, N_MOD * D)
    gmod_all = rows_all[:, :B].reshape(N_DEV * B, N_MOD * D)
    small_parts = rows_all[:, B, :n_small]

    a4, r2 = grads["mix_pending"]
    ffn_a4, ffn_r2 = grads["ffn_pending"]
    G = dict(zip(_SHARDED, _rs_last(list(a4) + list(ffn_a4), list(r2) + list(ffn_r2), "all")))

    gmod_cols = lax.dynamic_slice_in_dim(gmod_all, chip * ada_cols, ada_cols, axis=1)
    G["w_ada"] = _ada_bwd(c_all, gmod_cols, "ada_bwd")
    delta, new_m, new_v = {}, {}, {}
    for n in ("w_ada",) + _SHARDED:
        shp = W[n].shape
        w2 = W[n].reshape(shp[-2], shp[-1])
        d_, m_, v_ = _adamw(w2, G[n], M[n].reshape(w2.shape), V[n].reshape(w2.shape), f"adamw_{n}")
        G[n], delta[n], new_m[n], new_v[n] = [a.reshape(shp) for a in (G[n], d_, m_, v_)]
    gs, ds_, ms_, vs_ = _adamw_rows(cat(W), small_parts, cat(M), cat(V), "adamw_small")
    off = 0
    for n, sz in _SMALL:
        shp = W[n].shape
        G[n], delta[n], new_m[n], new_v[n] = [a[:, off:off + sz].reshape(shp) for a in (gs, ds_, ms_, vs_)]
        off += sz
    G["b_ada"], delta["b_ada"], new_m["b_ada"], new_v["b_ada"] = _adamw_rows(b_ada, gmod_all, m_b_ada, v_b_ada,
                                                                          "adamw_b_ada")
    return (loss, grad_x, *[G[n] for n in names], *[delta[n] for n in names], *[new_m[n] for n in names],
            *[new_v[n] for n in names])
```

```python
import functools
import math

import numpy as np
import jax
import jax.numpy as jnp
from jax import lax
from jax.experimental import pallas as pl
from jax.experimental.pallas import tpu as pltpu

F32 = jnp.float32
BF16 = jnp.bfloat16

D_MODEL = 1024
SEQ = 2048
N_HEADS = 8
HEAD_DIM = 64
D_A = 512
D_B = 512
Q_LORA = 384
KV_LORA = 256
ROPE_DIM = 32
NOPE_DIM = 64
D_FF = 2816
N_MOD = 6
N_BUCKETS = 32
MAX_DISTANCE = 2048
ROPE_THETA = 10000.0
EPS = 1e-6
NEG = -1e30
BLK = 128
DILATIONS = (1, 4, 16)
SPAN = 128
MLA_SCALE = (NOPE_DIM + ROPE_DIM) ** -0.5
DIL_SCALE = HEAD_DIM ** -0.5

ADAM_LR = 0.001
ADAM_B1 = 0.9
ADAM_B2 = 0.999
ADAM_EPS = 1e-08
ADAM_WD = 0.01
ADAM_STEP = 10

N_DEV = 8
N_CHIP = 4
LANES = 128
VMEM_LIMIT = 48 * 1024 * 1024
MM_VMEM_BUDGET = 32 * 1024 * 1024

P_QKV = 3 * D_A
P_REST = KV_LORA + LANES + Q_LORA


def _cparams(sem=None):
    return pltpu.CompilerParams(dimension_semantics=sem, vmem_limit_bytes=VMEM_LIMIT)


def _pick(n, cands):
    for c in cands:
        if n % c == 0:
            return c
    raise ValueError(f"no tile for {n} in {cands}")


def _mm(a, b, mode, out_dtype, name, col_blocks=None, comm=None, halves=False):
    blocked = col_blocks is not None
    if mode == "nn":
        (M, K) = a.shape
        K2, N = (b.shape[1], b.shape[0] * b.shape[2]) if blocked else b.shape
    elif mode == "nt":
        (M, K) = (a.shape[1], 2 * a.shape[2]) if halves else a.shape
        N, K2 = (b.shape[1], b.shape[0] * b.shape[2]) if blocked else b.shape
    else:
        (K, M) = a.shape
        K2, N = (b.shape[1], 2 * b.shape[2]) if halves else b.shape
    assert K == K2, (a.shape, b.shape, mode)
    assert not halves or (blocked and col_blocks == 4 and mode in ("nt", "tn"))
    tn = _pick(N, (1408, 1024, 768, 512, 384, 256, 128))
    tk = _pick(K, (1408, 1152, 1024, 768, 512, 384, 256, 128))
    if blocked and mode == "nt":
        tk = K // col_blocks
    elif blocked:
        tn = N // col_blocks
    nk = K // tk

    def vmem_bytes(tm_):
        tiles = tm_ * tk * a.dtype.itemsize + tk * tn * b.dtype.itemsize + tm_ * tn * jnp.dtype(out_dtype).itemsize
        return 2 * tiles + tm_ * tn * 4

    tm = next(t for t in (1408, 1024, 512, 384, 256, 128) if M % t == 0 and vmem_bytes(t) <= MM_VMEM_BUDGET)
    out_shape = (M, N)
    out_spec = pl.BlockSpec((tm, tn), lambda i, j, k: (i, j))
    if mode == "nn":
        a_spec = pl.BlockSpec((tm, tk), lambda i, j, k: (i, k))
        b_spec = (pl.BlockSpec((None, tk, tn), lambda i, j, k: (j, k, 0)) if blocked
                  else pl.BlockSpec((tk, tn), lambda i, j, k: (k, j)))
        dn = (((1,), (0,)), ((), ()))
    elif mode == "nt":
        a_spec = (pl.BlockSpec((None, tm, tk), lambda i, j, k: (k // 2, i, k % 2)) if halves
                  else pl.BlockSpec((tm, tk), lambda i, j, k: (i, k)))
        b_spec = (pl.BlockSpec((None, tn, tk), lambda i, j, k: (k, j, 0)) if blocked
                  else pl.BlockSpec((tn, tk), lambda i, j, k: (j, k)))
        dn = (((1,), (1,)), ((), ()))
    else:
        a_spec = pl.BlockSpec((tk, tm), lambda i, j, k: (k, i))
        b_spec = (pl.BlockSpec((None, tk, tn), lambda i, j, k: (j // 2, k, j % 2)) if halves
                  else pl.BlockSpec((tk, tn), lambda i, j, k: (k, j)))
        dn = (((0,), (0,)), ((), ()))
        if blocked:
            out_shape = (col_blocks, M, tn)
            out_spec = pl.BlockSpec((None, tm, tn), lambda i, j, k: (j, i, 0))

    def body(a_ref, b_ref, o_ref, acc_ref):
        k = pl.program_id(2)

        @pl.when(k == 0)
        def _():
            acc_ref[...] = jnp.zeros_like(acc_ref)

        acc_ref[...] += lax.dot_general(a_ref[...].astype(BF16), b_ref[...].astype(BF16), dn,
                                        preferred_element_type=F32)

        @pl.when(k == nk - 1)
        def _():
            o_ref[...] = acc_ref[...].astype(o_ref.dtype)

    if comm is not None:
        (out,), got = _host_call(
            body, comm, name=name, out_shape=[jax.ShapeDtypeStruct(out_shape, out_dtype)],
            grid=(M // tm, N // tn, nk), in_specs=[a_spec, b_spec], out_specs=[out_spec],
            scratch_shapes=[pltpu.VMEM((tm, tn), F32)], args=(a, b))
        return out, got
    return pl.pallas_call(
        body, name=name,
        out_shape=jax.ShapeDtypeStruct(out_shape, out_dtype),
        grid=(M // tm, N // tn, nk),
        in_specs=[a_spec, b_spec],
        out_specs=out_spec,
        scratch_shapes=[pltpu.VMEM((tm, tn), F32)],
        compiler_params=_cparams(("parallel", "parallel", "arbitrary")),
    )(a, b)


ROW_TILE = 512


def _adaln_fwd(x, g, sc, sh, name, mix=None, gate=None):
    B, S, D = x.shape
    ts = ROW_TILE
    has_res = mix is not None

    def body(*refs):
        if has_res:
            x_ref, g_ref, sc_ref, sh_ref, mix_ref, gate_ref, h_ref, xr_ref = refs
            xr = x_ref[0] + gate_ref[0] * mix_ref[0]
            xr_ref[0] = xr
        else:
            x_ref, g_ref, sc_ref, sh_ref, h_ref = refs
            xr = x_ref[0]
        r = lax.rsqrt(jnp.mean(xr * xr, axis=-1, keepdims=True) + EPS)
        xn = (xr * r) * g_ref[...]
        h_ref[0] = (xn * (1.0 + sc_ref[0]) + sh_ref[0]).astype(h_ref.dtype)

    tok = pl.BlockSpec((1, ts, D), lambda b, s: (b, s, 0))
    per_b = pl.BlockSpec((1, 1, D), lambda b, s: (b, 0, 0))
    vec = pl.BlockSpec((1, D), lambda b, s: (0, 0))
    in_specs = [tok, vec, per_b, per_b]
    args = [x, g, sc, sh]
    out_shape = [jax.ShapeDtypeStruct((B, S, D), BF16)]
    out_specs = [tok]
    if has_res:
        in_specs += [tok, per_b]
        args += [mix, gate]
        out_shape.append(jax.ShapeDtypeStruct((B, S, D), F32))
        out_specs.append(tok)
    out = pl.pallas_call(
        body, name=name, out_shape=out_shape, grid=(B, S // ts),
        in_specs=in_specs, out_specs=out_specs,
        compiler_params=_cparams(("parallel", "parallel")),
    )(*args)
    return out if has_res else out[0]


def _adaln_bwd(dh, x, g, sc, dres, name, mix=None, gate=None, comm=None):
    B, S, D = x.shape
    ts = ROW_TILE
    has_res = mix is not None

    def body(*refs):
        if has_res:
            (dh_ref, x_ref, g_ref, sc_ref, dres_ref, mix_ref, gate_ref,
             dx_ref, dsh_ref, dsc_ref, dg_ref, dgate_ref, dmix_ref) = refs
        else:
            (dh_ref, x_ref, g_ref, sc_ref, dres_ref, dx_ref, dsh_ref, dsc_ref, dg_ref) = refs
        b, s = pl.program_id(0), pl.program_id(1)
        xv = x_ref[0]
        dhv = dh_ref[0]
        gv = g_ref[...]
        r = lax.rsqrt(jnp.mean(xv * xv, axis=-1, keepdims=True) + EPS)
        n = xv * r
        xn = n * gv
        dxn = dhv * (1.0 + sc_ref[0])
        dn = dxn * gv
        dx = r * (dn - n * jnp.mean(dn * n, axis=-1, keepdims=True)) + dres_ref[0]
        dx_ref[0] = dx

        @pl.when(s == 0)
        def _():
            dsh_ref[...] = jnp.zeros_like(dsh_ref)
            dsc_ref[...] = jnp.zeros_like(dsc_ref)
            if has_res:
                dgate_ref[...] = jnp.zeros_like(dgate_ref)

        @pl.when((s == 0) & (b == 0))
        def _():
            dg_ref[...] = jnp.zeros_like(dg_ref)

        dsh_ref[0] += jnp.sum(dhv, axis=0, keepdims=True)
        dsc_ref[0] += jnp.sum(dhv * xn, axis=0, keepdims=True)
        dg_ref[...] += jnp.sum(dxn * n, axis=0, keepdims=True)
        if has_res:
            dgate_ref[0] += jnp.sum(dx * mix_ref[0], axis=0, keepdims=True)
            dmix_ref[0] = (dx * gate_ref[0]).astype(dmix_ref.dtype)

    tok = pl.BlockSpec((1, ts, D), lambda b, s: (b, s, 0))
    per_b = pl.BlockSpec((1, 1, D), lambda b, s: (b, 0, 0))
    vec = pl.BlockSpec((1, D), lambda b, s: (0, 0))
    in_specs = [tok, tok, vec, per_b, tok]
    args = [dh, x, g, sc, dres]
    out_shape = [jax.ShapeDtypeStruct((B, S, D), F32), jax.ShapeDtypeStruct((B, 1, D), F32),
                 jax.ShapeDtypeStruct((B, 1, D), F32), jax.ShapeDtypeStruct((1, D), F32)]
    out_specs = [tok, per_b, per_b, vec]
    if has_res:
        in_specs += [tok, per_b]
        args += [mix, gate]
        out_shape += [jax.ShapeDtypeStruct((B, 1, D), F32), jax.ShapeDtypeStruct((B, S, D), BF16)]
        out_specs += [per_b, tok]
    res, got = _host_call(body, comm, name=name, out_shape=out_shape, grid=(B, S // ts), in_specs=in_specs,
                          out_specs=out_specs, scratch_shapes=[], args=args)
    return (list(res) + [got]) if comm is not None else res


def _rms_fwd_pair(xa, xb, ga, gb, name):
    T, na = xa.shape
    nb = xb.shape[1]
    tr = 512

    def body(xa_ref, xb_ref, ga_ref, gb_ref, y_ref):
        for x_ref, g_ref, lo, n in ((xa_ref, ga_ref, 0, na), (xb_ref, gb_ref, na, nb)):
            xv = x_ref[...]
            r = lax.rsqrt(jnp.mean(xv * xv, axis=-1, keepdims=True) + EPS)
            y_ref[:, lo:lo + n] = ((xv * r) * g_ref[...]).astype(y_ref.dtype)

    row = lambda n: pl.BlockSpec((tr, n), lambda i: (i, 0))
    vec = lambda n: pl.BlockSpec((1, n), lambda i: (0, 0))
    return pl.pallas_call(
        body, name=name, out_shape=jax.ShapeDtypeStruct((T, na + nb), BF16), grid=(T // tr,),
        in_specs=[row(na), row(nb), vec(na), vec(nb)], out_specs=row(na + nb),
        compiler_params=_cparams(("parallel",)),
    )(xa, xb, ga, gb)


def _rms_bwd(dy, dy_blk, x, x_blk, n, g, name, out_dtype=BF16):
    T = x.shape[0]
    tr = 512

    def body(dy_ref, x_ref, g_ref, dx_ref, dg_ref):
        xv = x_ref[...]
        dyv = dy_ref[...].astype(F32)
        r = lax.rsqrt(jnp.mean(xv * xv, axis=-1, keepdims=True) + EPS)
        nrm = xv * r
        dn = dyv * g_ref[...]
        dx_ref[...] = (r * (dn - nrm * jnp.mean(dn * nrm, axis=-1, keepdims=True))).astype(dx_ref.dtype)

        @pl.when(pl.program_id(0) == 0)
        def _():
            dg_ref[...] = jnp.zeros_like(dg_ref)

        dg_ref[...] += jnp.sum(dyv * nrm, axis=0, keepdims=True)

    return pl.pallas_call(
        body, name=name,
        out_shape=[jax.ShapeDtypeStruct((T, n), out_dtype), jax.ShapeDtypeStruct((1, n), F32)],
        grid=(T // tr,),
        in_specs=[pl.BlockSpec((tr, n), lambda i: (i, dy_blk)), pl.BlockSpec((tr, n), lambda i: (i, x_blk)),
                  pl.BlockSpec((1, n), lambda i: (0, 0))],
        out_specs=[pl.BlockSpec((tr, n), lambda i: (i, 0)), pl.BlockSpec((1, n), lambda i: (0, 0))],
        compiler_params=_cparams(("arbitrary",)),
    )(dy, x, g)


def _rms_bwd_views(dy, dy_blk, x, g, name):
    B, S, n = x.shape
    tiles = S // VIEW_TILE

    def body(dy_ref, x_ref, g_ref, d1_ref, d4_ref, d16_ref, dg_ref, dx_s):
        xv = x_ref[0]
        dyv = dy_ref[...]
        r = lax.rsqrt(jnp.mean(xv * xv, axis=-1, keepdims=True) + EPS)
        nrm = xv * r
        dn = dyv * g_ref[...]
        dx = r * (dn - nrm * jnp.mean(dn * nrm, axis=-1, keepdims=True))
        d1_ref[0] = dx.astype(d1_ref.dtype)
        _put_tile(dx_s, dx)
        _tile_to_view(dx_s, d4_ref, DILATIONS[1], n)
        _tile_to_view(dx_s, d16_ref, DILATIONS[2], n)

        @pl.when((pl.program_id(0) == 0) & (pl.program_id(1) == 0))
        def _():
            dg_ref[...] = jnp.zeros_like(dg_ref)

        dg_ref[...] += jnp.sum(dyv * nrm, axis=0, keepdims=True)

    res = pl.pallas_call(
        body, name=name,
        out_shape=[_view_shape(B, S, d, n, BF16) for d in DILATIONS] + [jax.ShapeDtypeStruct((1, n), F32)],
        grid=(B, tiles),
        in_specs=[pl.BlockSpec((VIEW_TILE, n), lambda b, t: (b * tiles + t, dy_blk)), _view_spec(1, n),
                  pl.BlockSpec((1, n), lambda b, t: (0, 0))],
        out_specs=[_view_spec(d, n) for d in DILATIONS] + [pl.BlockSpec((1, n), lambda b, t: (0, 0))],
        scratch_shapes=[_tile_scratch(n)],
        compiler_params=_cparams(("arbitrary", "arbitrary")),
    )(dy, x, g)
    return res[:len(DILATIONS)], res[len(DILATIONS)]


FFN_TILE = 1408


def _ffn_in_fwd(h, w4, name):
    T, D = h.shape
    tm, tc = 512, FFN_TILE
    nc = D_FF // tc

    def body(h_ref, wg_ref, wu_ref, gu_ref, act_ref):
        hv = h_ref[...]
        g = jnp.dot(hv, wg_ref[...], preferred_element_type=F32)
        u = jnp.dot(hv, wu_ref[...], preferred_element_type=F32)
        gu_ref[0] = g.astype(gu_ref.dtype)
        gu_ref[1] = u.astype(gu_ref.dtype)
        act_ref[...] = (g * jax.nn.sigmoid(g) * u).astype(act_ref.dtype)

    return pl.pallas_call(
        body, name=name,
        out_shape=[jax.ShapeDtypeStruct((2, T, D_FF), BF16), jax.ShapeDtypeStruct((T, D_FF), BF16)],
        grid=(nc, T // tm),
        in_specs=[pl.BlockSpec((tm, D), lambda j, i: (i, 0)),
                  pl.BlockSpec((None, D, tc), lambda j, i: (j, 0, 0)),
                  pl.BlockSpec((None, D, tc), lambda j, i: (j + nc, 0, 0))],
        out_specs=[pl.BlockSpec((2, tm, tc), lambda j, i: (0, i, j)), pl.BlockSpec((tm, tc), lambda j, i: (i, j))],
        compiler_params=_cparams(("parallel", "parallel")),
    )(h, w4, w4)


def _ffn_out_bwd(df, w_out, gu, name):
    T, D = df.shape
    tm, tc = 512, FFN_TILE

    def body(df_ref, w_ref, gu_ref, dgu_ref):
        da = _dot_nt(df_ref[...], w_ref[...])
        g, u = gu_ref[0].astype(F32), gu_ref[1].astype(F32)
        sg = jax.nn.sigmoid(g)
        dgu_ref[0] = (da * u * (sg * (1.0 + g * (1.0 - sg)))).astype(dgu_ref.dtype)
        dgu_ref[1] = (da * (g * sg)).astype(dgu_ref.dtype)

    halves = pl.BlockSpec((2, tm, tc), lambda j, i: (0, i, j))
    return pl.pallas_call(
        body, name=name, out_shape=jax.ShapeDtypeStruct((2, T, D_FF), BF16), grid=(D_FF // tc, T // tm),
        in_specs=[pl.BlockSpec((tm, D), lambda j, i: (i, 0)), pl.BlockSpec((tc, D), lambda j, i: (j, 0)), halves],
        out_specs=halves,
        compiler_params=_cparams(("parallel", "parallel")),
    )(df, w_out, gu)


def _final_loss(x1, f, g2, gf, target, name):
    B, S, D = x1.shape
    ts = ROW_TILE

    def body(x1_ref, f_ref, g2_ref, gf_ref, t_ref, dx_ref, df_ref, dg2_ref, dgf_ref, loss_ref):
        b, s = pl.program_id(0), pl.program_id(1)
        fv = f_ref[0]
        g2v = g2_ref[0]
        gfv = gf_ref[...]
        x2 = x1_ref[0] + g2v * fv
        r = lax.rsqrt(jnp.mean(x2 * x2, axis=-1, keepdims=True) + EPS)
        n = x2 * r
        e = n * gfv - t_ref[0]
        dy = e * (1.0 / D)
        dn = dy * gfv
        dx = r * (dn - n * jnp.mean(dn * n, axis=-1, keepdims=True))
        dx_ref[0] = dx
        df_ref[0] = (dx * g2v).astype(df_ref.dtype)

        @pl.when(s == 0)
        def _():
            dg2_ref[...] = jnp.zeros_like(dg2_ref)

        @pl.when((s == 0) & (b == 0))
        def _():
            dgf_ref[...] = jnp.zeros_like(dgf_ref)
            loss_ref[...] = jnp.zeros_like(loss_ref)

        dg2_ref[0] += jnp.sum(dx * fv, axis=0, keepdims=True)
        dgf_ref[...] += jnp.sum(dy * n, axis=0, keepdims=True)
        loss_ref[...] += 0.5 * jnp.sum(jnp.mean(e * e, axis=-1, keepdims=True), axis=0, keepdims=True)

    tok = pl.BlockSpec((1, ts, D), lambda b, s: (b, s, 0))
    per_b = pl.BlockSpec((1, 1, D), lambda b, s: (b, 0, 0))
    vec = pl.BlockSpec((1, D), lambda b, s: (0, 0))
    return pl.pallas_call(
        body, name=name,
        out_shape=[jax.ShapeDtypeStruct((B, S, D), F32), jax.ShapeDtypeStruct((B, S, D), BF16),
                   jax.ShapeDtypeStruct((B, 1, D), F32), jax.ShapeDtypeStruct((1, D), F32),
                   jax.ShapeDtypeStruct((1, LANES), F32)],
        grid=(B, S // ts),
        in_specs=[tok, tok, per_b, vec, tok],
        out_specs=[tok, tok, per_b, vec, pl.BlockSpec((1, LANES), lambda b, s: (0, 0))],
        compiler_params=_cparams(("arbitrary", "arbitrary")),
    )(x1, f, g2, gf, target)


def _rope_tables():
    half = ROPE_DIM // 2
    inv = ROPE_THETA ** (-jnp.arange(half, dtype=F32) / half)
    ang = jnp.arange(SEQ, dtype=F32)[:, None] * inv[None, :]
    cos, sin = jnp.cos(ang), jnp.sin(ang)
    one = jnp.ones((SEQ, NOPE_DIM), F32)
    zero = jnp.zeros((SEQ, NOPE_DIM), F32)
    cs = jnp.concatenate([one, cos, cos, one[:, :LANES - NOPE_DIM - ROPE_DIM]], axis=1)
    sn = jnp.concatenate([zero, -sin, sin, zero[:, :LANES - NOPE_DIM - ROPE_DIM]], axis=1)
    return cs, sn


def _rope_group(t, cs, sn):
    half = ROPE_DIM // 2
    lane = lax.broadcasted_iota(jnp.int32, t.shape, 1)
    partner = jnp.where(lane < NOPE_DIM + half, pltpu.roll(t, LANES - half, 1), pltpu.roll(t, half, 1))
    return t * cs + partner * sn


def _rope_apply(t, cs, sn, out_dtype, name, add=None, add_blk=0):
    B, S, W = t.shape
    G = W // LANES
    ts = ROW_TILE

    def body(*refs):
        if add is None:
            t_ref, cs_ref, sn_ref, o_ref = refs
            for gi in range(G):
                sl = slice(gi * LANES, (gi + 1) * LANES)
                o_ref[0, :, sl] = _rope_group(t_ref[0, :, sl], cs_ref[...], sn_ref[...]).astype(o_ref.dtype)
        else:
            t_ref, a_ref, cs_ref, sn_ref, o_ref = refs
            ra = _rope_group(a_ref[0], cs_ref[...], sn_ref[...])
            for gi in range(G):
                sl = slice(gi * LANES, (gi + 1) * LANES)
                o_ref[0, :, sl] = (t_ref[0, :, sl] + ra).astype(o_ref.dtype)

    tok = pl.BlockSpec((1, ts, W), lambda b, s: (b, s, 0))
    tab = pl.BlockSpec((ts, LANES), lambda b, s: (s, 0))
    in_specs, args = [tok], [t]
    if add is not None:
        in_specs.append(pl.BlockSpec((1, ts, LANES), lambda b, s: (b, s, add_blk)))
        args.append(add)
    in_specs += [tab, tab]
    args += [cs, sn]
    return pl.pallas_call(
        body, name=name, out_shape=jax.ShapeDtypeStruct((B, S, W), out_dtype), grid=(B, S // ts),
        in_specs=in_specs, out_specs=tok, compiler_params=_cparams(("parallel", "parallel")),
    )(*args)


def _qrope_bwd(dq_t, cs, sn_neg, name):
    B, npair, nq, _, tq = dq_t.shape

    def body(d_ref, cs_ref, sn_ref, o_ref):
        for p in range(npair):
            tile = jnp.transpose(d_ref[0, p, 0])
            for hh in range(2):
                lo = (2 * p + hh) * LANES
                o_ref[0, :, lo:lo + LANES] = _rope_group(tile[:, hh * LANES:(hh + 1) * LANES], cs_ref[...],
                                                         sn_ref[...]).astype(o_ref.dtype)

    tab = pl.BlockSpec((tq, LANES), lambda b, i: (i, 0))
    return pl.pallas_call(
        body, name=name, out_shape=jax.ShapeDtypeStruct((B, nq * tq, N_HEADS * LANES), BF16), grid=(B, nq),
        in_specs=[pl.BlockSpec((1, npair, 1, 2 * LANES, tq), lambda b, i: (b, 0, i, 0, 0)), tab, tab],
        out_specs=pl.BlockSpec((1, tq, N_HEADS * LANES), lambda b, i: (b, i, 0)),
        compiler_params=_cparams(("parallel", "parallel")),
    )(dq_t, cs, sn_neg)


def _krope_bwd(dkc, cs, sn_neg, name):
    B, S, W = dkc.shape
    G = W // LANES
    ts = ROW_TILE

    def body(d_ref, cs_ref, sn_ref, o_ref):
        acc = d_ref[0, :, 0:LANES]
        for gi in range(1, G):
            acc = acc + d_ref[0, :, gi * LANES:(gi + 1) * LANES]
        lane = lax.broadcasted_iota(jnp.int32, acc.shape, 1)
        rot = (lane >= NOPE_DIM) & (lane < NOPE_DIM + ROPE_DIM)
        acc = jnp.where(rot, acc, 0.0)
        o_ref[0] = _rope_group(acc, cs_ref[...], sn_ref[...]).astype(o_ref.dtype)

    tab = pl.BlockSpec((ts, LANES), lambda b, s: (s, 0))
    return pl.pallas_call(
        body, name=name, out_shape=jax.ShapeDtypeStruct((B, S, LANES), BF16), grid=(B, S // ts),
        in_specs=[pl.BlockSpec((1, ts, W), lambda b, s: (b, s, 0)), tab, tab],
        out_specs=pl.BlockSpec((1, ts, LANES), lambda b, s: (b, s, 0)),
        compiler_params=_cparams(("parallel", "parallel")),
    )(dkc, cs, sn_neg)


def _t5_bucket(dist):
    max_exact = N_BUCKETS // 2
    d = np.maximum(dist, 1).astype(np.float64)
    large = max_exact + (np.log(d / max_exact) / np.log(MAX_DISTANCE / max_exact)
                         * (N_BUCKETS - max_exact)).astype(np.int64)
    large = np.minimum(large, N_BUCKETS - 1)
    return np.where(dist < max_exact, dist, large).astype(np.int32)


def _band_buckets(dilation):
    a = np.arange(BLK)[None, :]
    bk = np.arange(2 * BLK)[:, None]
    steps = BLK + a - bk
    return _t5_bucket(np.clip(steps, 0, SPAN) * dilation)


def _head_mask(shape, hh):
    lane = lax.broadcasted_iota(jnp.int32, shape, 1)
    return (lane >= hh * HEAD_DIM) & (lane < (hh + 1) * HEAD_DIM)


def _dot_nt(a, b):
    return lax.dot_general(a, b, (((1,), (1,)), ((), ())), preferred_element_type=F32)


def _dot_tn(a, b):
    return lax.dot_general(a, b, (((0,), (0,)), ((), ())), preferred_element_type=F32)


def _dot_nn(a, b):
    return lax.dot_general(a, b, (((1,), (0,)), ((), ())), preferred_element_type=F32)


def _dil_fwd(qkv, bias, branch, dilation, name, comm=None):
    B, n, _ = qkv.shape
    d = dilation
    nb = n // BLK
    qkv_v = qkv
    npair = N_HEADS // 2

    def body(cur_ref, prev_ref, bias_ref, o_ref, lse_ref, s_scr, e_scr):
        first = jnp.where(pl.program_id(1) == 0, 1, 0)
        units = [(b, h) for b in range(B) for h in range(N_HEADS)]
        for b in range(B):
            for p in range(npair):
                q = cur_ref[b, :, p * LANES:(p + 1) * LANES] * DIL_SCALE
                kc = cur_ref[b, :, D_A + p * LANES:D_A + (p + 1) * LANES]
                kp = prev_ref[b, :, D_A + p * LANES:D_A + (p + 1) * LANES]
                for hh in range(2):
                    u = b * N_HEADS + 2 * p + hh
                    qm = jnp.where(_head_mask((BLK, LANES), hh), q, jnp.zeros_like(q))
                    s_scr[u, 0:BLK, :] = _dot_nt(kp, qm)
                    s_scr[u, BLK:2 * BLK, :] = _dot_nt(kc, qm)
        ms = []
        for u, (b, h) in enumerate(units):
            s_p = s_scr[u, 0:BLK, :] + bias_ref[first, h, 0:BLK, :]
            s_c = s_scr[u, BLK:2 * BLK, :] + bias_ref[first, h, BLK:2 * BLK, :]
            m = jnp.maximum(jnp.max(s_p, axis=0, keepdims=True), jnp.max(s_c, axis=0, keepdims=True))
            e_scr[u, 0:BLK, :] = jnp.exp(s_p - m).astype(BF16)
            e_scr[u, BLK:2 * BLK, :] = jnp.exp(s_c - m).astype(BF16)
            ms.append(m)
        rows0 = _row_mask((LANES, BLK), 0)
        for b in range(B):
            for p in range(npair):
                sl = slice(p * LANES, (p + 1) * LANES)
                vsl = slice(2 * D_A + p * LANES, 2 * D_A + (p + 1) * LANES)
                vct = jnp.transpose(cur_ref[b, :, vsl].astype(F32)).astype(BF16)
                vpt = jnp.transpose(prev_ref[b, :, vsl].astype(F32)).astype(BF16)
                acc = []
                for hh in range(2):
                    u = b * N_HEADS + 2 * p + hh
                    mine = _row_mask((LANES, BLK), hh)
                    one = jnp.ones_like(vct)
                    acc.append(_dot_nn(jnp.where(mine, vpt, one), e_scr[u, 0:BLK, :])
                               + _dot_nn(jnp.where(mine, vct, one), e_scr[u, BLK:2 * BLK, :]))
                l0 = acc[0][HEAD_DIM:HEAD_DIM + 1, :]
                l1 = acc[1][0:1, :]
                u0 = b * N_HEADS + 2 * p
                o_t = jnp.where(rows0, acc[0] / l0, acc[1] / l1)
                lse_t = jnp.where(rows0, ms[u0] + jnp.log(l0), ms[u0 + 1] + jnp.log(l1))
                o_ref[b, :, sl] = jnp.transpose(o_t)
                lse_ref[b, :, sl] = jnp.transpose(lse_t)

    cur = pl.BlockSpec((B, BLK, P_QKV), lambda r, i: (0, i, r))
    prev = pl.BlockSpec((B, BLK, P_QKV), lambda r, i: (0, jnp.maximum(i - 1, 0), r))
    out = pl.BlockSpec((B, BLK, D_A), lambda r, i: (0, i, r))
    return _host_call(
        body, comm, name=name,
        out_shape=[jax.ShapeDtypeStruct((B, n, d * D_A), F32)] * 2,
        grid=(d, nb),
        in_specs=[cur, prev,
                  pl.BlockSpec((None, 2, N_HEADS, 2 * BLK, BLK), lambda r, i: (branch, 0, 0, 0, 0))],
        out_specs=[out, out],
        scratch_shapes=[pltpu.VMEM((B * N_HEADS, 2 * BLK, BLK), F32),
                        pltpu.VMEM((B * N_HEADS, 2 * BLK, BLK), BF16)],
        args=(qkv_v, qkv_v, bias))


VIEW_TILE = 512


def _view_spec(d, w):
    return pl.BlockSpec((1, VIEW_TILE // d, d * w), lambda b, t: (b, t, 0))


def _view_shape(B, S, d, w, dtype):
    return jax.ShapeDtypeStruct((B, S // d, d * w), dtype)


def _tile_scratch(w):
    return pltpu.VMEM((w // LANES, VIEW_TILE, LANES), F32)


def _put_tile(tile_ref, val):
    for c in range(tile_ref.shape[0]):
        tile_ref[c] = val[:, c * LANES:(c + 1) * LANES]


def _get_tile(tile_ref):
    return jnp.concatenate([tile_ref[c] for c in range(tile_ref.shape[0])], axis=1)


def _tile_to_view(tile_ref, view_ref, d, w):
    for c in range(w // LANES):
        for r in range(d):
            lo = r * w + c * LANES
            rows = tile_ref.at[c][pl.ds(r, VIEW_TILE // d, stride=d), :]
            view_ref[0, :, lo:lo + LANES] = rows.astype(view_ref.dtype)


def _view_to_tile(view_ref, tile_ref, d, w):
    for c in range(w // LANES):
        for r in range(d):
            lo = r * w + c * LANES
            tile_ref.at[c][pl.ds(r, VIEW_TILE // d, stride=d), :] = view_ref[0, :, lo:lo + LANES].astype(F32)


def _in_proj(h, w_qkv, w_rest, g_cq, g_ckv, name):
    B, S, D = h.shape
    N = w_qkv.shape[1]
    cq_lo = KV_LORA + LANES

    def rms(xv, g_ref):
        return ((xv * lax.rsqrt(jnp.mean(xv * xv, axis=-1, keepdims=True) + EPS)) * g_ref[...]).astype(BF16)

    def body(h_ref, wq_ref, wr_ref, gcq_ref, gckv_ref, o1_ref, o4_ref, o16_ref, rest_ref, cqn_ref, ckvn_ref,
             acc_ref):
        hv = h_ref[0]
        acc = jnp.dot(hv, wq_ref[...], preferred_element_type=F32)
        o1_ref[0] = acc.astype(o1_ref.dtype)
        _put_tile(acc_ref, acc)
        _tile_to_view(acc_ref, o4_ref, DILATIONS[1], N)
        _tile_to_view(acc_ref, o16_ref, DILATIONS[2], N)
        rest = jnp.dot(hv, wr_ref[...], preferred_element_type=F32)
        rest_ref[0] = rest
        ckvn_ref[...] = rms(rest[:, :KV_LORA], gckv_ref)
        cqn_ref[...] = rms(rest[:, cq_lo:], gcq_ref)

    tiles = S // VIEW_TILE
    full = lambda a: pl.BlockSpec(a.shape, lambda b, t: (0, 0))
    rows = lambda n: pl.BlockSpec((VIEW_TILE, n), lambda b, t: (b * tiles + t, 0))
    res = pl.pallas_call(
        body, name=name,
        out_shape=[_view_shape(B, S, d, N, BF16) for d in DILATIONS]
        + [jax.ShapeDtypeStruct((B, S, P_REST), F32), jax.ShapeDtypeStruct((B * S, Q_LORA), BF16),
           jax.ShapeDtypeStruct((B * S, KV_LORA), BF16)],
        grid=(B, tiles),
        in_specs=[pl.BlockSpec((1, VIEW_TILE, D), lambda b, t: (b, t, 0)), full(w_qkv), full(w_rest), full(g_cq),
                  full(g_ckv)],
        out_specs=[_view_spec(d, N) for d in DILATIONS] + [_view_spec(1, P_REST), rows(Q_LORA), rows(KV_LORA)],
        scratch_shapes=[_tile_scratch(N)],
        compiler_params=_cparams(("parallel", "parallel")),
    )(h, w_qkv, w_rest, g_cq, g_ckv)
    return res[:len(DILATIONS)], res[len(DILATIONS)], res[len(DILATIONS) + 1], res[len(DILATIONS) + 2]


def _dil_merge(os_, lses, name):
    B, S, W = os_[0].shape
    nd = len(DILATIONS)

    def body(*refs):
        o_refs, l_refs = refs[:nd], refs[nd:2 * nd]
        out_refs, L_refs = refs[2 * nd:3 * nd], refs[3 * nd:4 * nd]
        scr = refs[4 * nd:]
        o_tok, l_tok = [o_refs[0][0]], [l_refs[0][0]]
        for i, d in enumerate(DILATIONS[1:]):
            _view_to_tile(o_refs[i + 1], scr[2 * i], d, W)
            _view_to_tile(l_refs[i + 1], scr[2 * i + 1], d, W)
            o_tok.append(_get_tile(scr[2 * i]))
            l_tok.append(_get_tile(scr[2 * i + 1]))
        a0, a1, a2 = l_tok
        m = jnp.maximum(jnp.maximum(a0, a1), a2)
        e0, e1, e2 = jnp.exp(a0 - m), jnp.exp(a1 - m), jnp.exp(a2 - m)
        ssum = e0 + e1 + e2
        out = (e0 * o_tok[0] + e1 * o_tok[1] + e2 * o_tok[2]) / ssum
        lse = m + jnp.log(ssum)
        out_refs[0][0] = out
        L_refs[0][0] = lse
        res_o, res_l = scr[2 * (nd - 1)], scr[2 * (nd - 1) + 1]
        _put_tile(res_o, out)
        _put_tile(res_l, lse)
        for i, d in enumerate(DILATIONS[1:]):
            _tile_to_view(res_o, out_refs[i + 1], d, W)
            _tile_to_view(res_l, L_refs[i + 1], d, W)

    specs = [_view_spec(d, W) for d in DILATIONS]
    shapes = [_view_shape(B, S * DILATIONS[0], d, W, F32) for d in DILATIONS]
    res = pl.pallas_call(
        body, name=name, out_shape=shapes * 2, grid=(B, S // VIEW_TILE),
        in_specs=specs * 2, out_specs=specs * 2,
        scratch_shapes=[_tile_scratch(W)] * (2 * nd),
        compiler_params=_cparams(("parallel", "parallel")),
    )(*os_, *lses)
    return res[:nd], res[nd:]


def _dil_bwd(qkv, do, out_a, L, bias, branch, dilation, name, comm=None):
    B, n, _ = qkv.shape
    d = dilation
    nb = n // BLK
    qkv_v, do_v, oa_v, L_v = qkv, do, out_a, L
    npair = N_HEADS // 2
    multi = nb > 1

    tiles = ("P", "C", "N") if multi else ("C",)
    n_t = len(tiles)

    def body(*refs):
        if multi:
            (cur_ref, prev_ref, next_ref, do_ref, don_ref, oa_ref, oan_ref, L_ref, Ln_ref, bias_ref,
             dqkv_ref, dbias_ref, s_scr, dp_scr, p_scr, ds_scr) = refs
        else:
            cur_ref, do_ref, oa_ref, L_ref, bias_ref, dqkv_ref, dbias_ref, s_scr, dp_scr, p_scr, ds_scr = refs
        r, i = pl.program_id(0), pl.program_id(1)

        @pl.when((r == 0) & (i == 0))
        def _():
            dbias_ref[...] = jnp.zeros_like(dbias_ref)

        first = jnp.where(i == 0, 1, 0)
        variant = {"P": first, "C": first, "N": 0}
        band = {"P": slice(0, BLK), "C": slice(BLK, 2 * BLK), "N": slice(0, BLK)}
        psl = lambda p: slice(p * LANES, (p + 1) * LANES)
        ksl = lambda p: slice(D_A + p * LANES, D_A + (p + 1) * LANES)
        vsl = lambda p: slice(2 * D_A + p * LANES, 2 * D_A + (p + 1) * LANES)

        def operands(b, p, hh):
            hm = _head_mask((BLK, LANES), hh)
            mask = lambda x: jnp.where(hm, x, jnp.zeros_like(x))
            qm, dom = mask(cur_ref[b, :, psl(p)] * DIL_SCALE), mask(do_ref[b, :, psl(p)])
            ops = {"C": (cur_ref[b, :, ksl(p)], cur_ref[b, :, vsl(p)], qm, dom)}
            if multi:
                ops["P"] = (prev_ref[b, :, ksl(p)], prev_ref[b, :, vsl(p)], qm, dom)
                ops["N"] = (cur_ref[b, :, ksl(p)], cur_ref[b, :, vsl(p)],
                            mask(next_ref[b, :, psl(p)] * DIL_SCALE), mask(don_ref[b, :, psl(p)]))
            return ops

        pairs = [(b, p) for b in range(B) for p in range(npair)]
        for b, p in pairs:
            for hh in range(2):
                u = b * N_HEADS + 2 * p + hh
                ops = operands(b, p, hh)
                for t, name_t in enumerate(tiles):
                    k_t, v_t, q_t, do_t = ops[name_t]
                    s_scr[u, t] = _dot_nt(k_t, q_t)
                    dp_scr[u, t] = _dot_nt(v_t, do_t)

        def rows(L_r, do_r, oa_r, b, p):
            lt = jnp.transpose(L_r[b, :, psl(p)])
            dt = jnp.transpose(do_r[b, :, psl(p)].astype(F32) * oa_r[b, :, psl(p)])
            return ([lt[0:1, :], lt[HEAD_DIM:HEAD_DIM + 1, :]],
                    [jnp.sum(dt[:HEAD_DIM], axis=0, keepdims=True), jnp.sum(dt[HEAD_DIM:], axis=0, keepdims=True)])

        for b, p in pairs:
            lse_c, delta_c = rows(L_ref, do_ref, oa_ref, b, p)
            if multi:
                lse_n, delta_n = rows(Ln_ref, don_ref, oan_ref, b, p)
            for hh in range(2):
                h = 2 * p + hh
                u = b * N_HEADS + h
                for t, name_t in enumerate(tiles):
                    lse, delta = (lse_n[hh], delta_n[hh]) if name_t == "N" else (lse_c[hh], delta_c[hh])
                    pr = jnp.exp(s_scr[u, t] + bias_ref[variant[name_t], h, band[name_t], :] - lse)
                    if name_t == "N":
                        pr = jnp.where(i < nb - 1, pr, 0.0)
                    ds = pr * (dp_scr[u, t] - delta)
                    p_scr[u, t] = pr.astype(BF16)
                    ds_scr[u, t] = ds.astype(BF16)
                    if name_t != "N":
                        dbias_ref[h, band[name_t], :] += ds

        for b, p in pairs:
            dqt = jnp.zeros((LANES, BLK), F32)
            dk = jnp.zeros((BLK, LANES), F32)
            dv = jnp.zeros((BLK, LANES), F32)
            kct = jnp.transpose(cur_ref[b, :, ksl(p)].astype(F32)).astype(BF16)
            if multi:
                kpt = jnp.transpose(prev_ref[b, :, ksl(p)].astype(F32)).astype(BF16)
            for hh in range(2):
                u = b * N_HEADS + 2 * p + hh
                ops = operands(b, p, hh)
                mine = _row_mask((LANES, BLK), hh)
                for t, name_t in enumerate(tiles):
                    _, _, q_t, do_t = ops[name_t]
                    if name_t != "P":
                        dv = dv + _dot_nn(p_scr[u, t], do_t)
                        dk = dk + _dot_nn(ds_scr[u, t], q_t)
                    if name_t != "N":
                        kt = kpt if name_t == "P" else kct
                        dqt = dqt + _dot_nn(jnp.where(mine, kt, jnp.zeros_like(kt)), ds_scr[u, t])
            dqkv_ref[b, :, psl(p)] = jnp.transpose(dqt) * DIL_SCALE
            dqkv_ref[b, :, ksl(p)] = dk
            dqkv_ref[b, :, vsl(p)] = dv

    def at(off):
        return lambda r, i: (0, jnp.clip(i + off, 0, nb - 1), r)

    qkv_spec = lambda off: pl.BlockSpec((B, BLK, P_QKV), at(off))
    da_spec = lambda off: pl.BlockSpec((B, BLK, D_A), at(off))
    bias_spec = pl.BlockSpec((None, 2, N_HEADS, 2 * BLK, BLK), lambda r, i: (branch, 0, 0, 0, 0))
    dbias_spec = pl.BlockSpec((N_HEADS, 2 * BLK, BLK), lambda r, i: (0, 0, 0))
    if multi:
        in_specs = [qkv_spec(0), qkv_spec(-1), qkv_spec(1), da_spec(0), da_spec(1), da_spec(0), da_spec(1),
                    da_spec(0), da_spec(1), bias_spec]
        args = [qkv_v, qkv_v, qkv_v, do_v, do_v, oa_v, oa_v, L_v, L_v, bias]
    else:
        in_specs = [qkv_spec(0), da_spec(0), da_spec(0), da_spec(0), bias_spec]
        args = [qkv_v, do_v, oa_v, L_v, bias]
    return _host_call(
        body, comm, name=name,
        out_shape=[jax.ShapeDtypeStruct((B, n, d * P_QKV), F32),
                   jax.ShapeDtypeStruct((N_HEADS, 2 * BLK, BLK), F32)],
        grid=(d, nb),
        in_specs=in_specs,
        out_specs=[qkv_spec(0), dbias_spec],
        scratch_shapes=[pltpu.VMEM((B * N_HEADS, n_t, BLK, BLK), F32), pltpu.VMEM((B * N_HEADS, n_t, BLK, BLK), F32),
                        pltpu.VMEM((B * N_HEADS, n_t, BLK, BLK), BF16),
                        pltpu.VMEM((B * N_HEADS, n_t, BLK, BLK), BF16)],
        args=args)


def _sum_views_bf16(parts, name):
    B, S, W = parts[0].shape

    def body(a_ref, b_ref, c_ref, o_ref, sb, sc):
        _view_to_tile(b_ref, sb, DILATIONS[1], W)
        _view_to_tile(c_ref, sc, DILATIONS[2], W)
        o_ref[0] = (a_ref[0] + _get_tile(sb) + _get_tile(sc)).astype(o_ref.dtype)

    return pl.pallas_call(
        body, name=name, out_shape=jax.ShapeDtypeStruct((B, S, W), BF16), grid=(B, S // VIEW_TILE),
        in_specs=[_view_spec(d, W) for d in DILATIONS], out_specs=_view_spec(1, W),
        scratch_shapes=[_tile_scratch(W)] * 2,
        compiler_params=_cparams(("parallel", "parallel")),
    )(*parts)


def _bias_tables(rel_bias, buckets, name, comm=None):
    nbr = buckets.shape[0]

    def body(rb_ref, bk_ref, o_ref):
        first, h = pl.program_id(1), pl.program_id(2)
        tab = bk_ref[0]

        def step(bkt, acc):
            return jnp.where(tab == bkt, rb_ref[bkt, h], acc)

        bias = lax.fori_loop(0, N_BUCKETS, step, jnp.zeros((2 * BLK, BLK), F32))
        row = lax.broadcasted_iota(jnp.int32, (2 * BLK, BLK), 0)
        col = lax.broadcasted_iota(jnp.int32, (2 * BLK, BLK), 1)
        valid = ((row < BLK) & (row >= col) & (first == 0)) | ((row >= BLK) & (row - BLK <= col))
        o_ref[0, 0, 0] = jnp.where(valid, bias, NEG)

    (bias,), got = _host_call(
        body, comm, name=name, out_shape=[jax.ShapeDtypeStruct((nbr, 2, N_HEADS, 2 * BLK, BLK), F32)],
        grid=(nbr, 2, N_HEADS),
        in_specs=[pl.BlockSpec(memory_space=pltpu.SMEM),
                  pl.BlockSpec((1, 2 * BLK, BLK), lambda i, f, h: (i, 0, 0))],
        out_specs=[pl.BlockSpec((1, 1, 1, 2 * BLK, BLK), lambda i, f, h: (i, f, h, 0, 0))],
        scratch_shapes=[], args=(rel_bias, buckets))
    return bias, got


def _bias_grad(dbias_list, buckets, name):
    nbr = len(dbias_list)

    def body(*refs):
        d_refs, bk_ref, o_ref, part = refs[:nbr], refs[nbr], refs[nbr + 1], refs[nbr + 2]

        def step(bkt, carry):
            hit = [bk_ref[bi] == bkt for bi in range(nbr)]
            for h in range(N_HEADS):
                tot = jnp.zeros((1, BLK), F32)
                for bi in range(nbr):
                    tot = tot + jnp.sum(jnp.where(hit[bi], d_refs[bi][h], 0.0), axis=0, keepdims=True)
                part[bkt, h:h + 1, :] = tot
            return carry

        lax.fori_loop(0, N_BUCKETS, step, 0)
        lane = lax.broadcasted_iota(jnp.int32, (N_HEADS, LANES), 1)
        acc = jnp.zeros((N_HEADS, LANES), F32)
        for bkt in range(N_BUCKETS):
            acc = acc + jnp.where(lane == bkt, jnp.sum(part[bkt], axis=1, keepdims=True), 0.0)
        o_ref[...] = acc

    band = pl.BlockSpec((N_HEADS, 2 * BLK, BLK), lambda i: (0, 0, 0))
    return pl.pallas_call(
        body, name=name, out_shape=jax.ShapeDtypeStruct((N_HEADS, LANES), F32), grid=(1,),
        in_specs=[band] * nbr + [pl.BlockSpec((nbr, 2 * BLK, BLK), lambda i: (0, 0, 0))],
        out_specs=pl.BlockSpec((N_HEADS, LANES), lambda i: (0, 0)),
        scratch_shapes=[pltpu.VMEM((N_BUCKETS, N_HEADS, BLK), F32)],
        compiler_params=_cparams(("arbitrary",)),
    )(*dbias_list, buckets)


MLA_TQ = 256
MLA_TK = 256


LOG2E = math.log2(math.e)
MLA_C = MLA_SCALE * LOG2E


def _key_le_query(tk, tq):
    return lax.broadcasted_iota(jnp.int32, (tk, tq), 0) <= lax.broadcasted_iota(jnp.int32, (tk, tq), 1)


def _row_mask(shape, hh):
    row = lax.broadcasted_iota(jnp.int32, shape, 0)
    return (row >= hh * HEAD_DIM) & (row < (hh + 1) * HEAD_DIM)


def _host_call(body, comm, *, name, grid, in_specs, out_specs, out_shape, scratch_shapes, args):
    sem = ("arbitrary",) * len(grid)
    if comm is None:
        res = pl.pallas_call(body, name=name, grid=grid, in_specs=in_specs, out_specs=out_specs,
                             out_shape=out_shape, scratch_shapes=scratch_shapes,
                             compiler_params=_cparams(sem))(*args)
        return res, []
    n_in, n_out, n_s, cn = len(in_specs), len(out_specs), len(scratch_shapes), comm.n

    def hosted(*refs):
        ins, refs = refs[:n_in], refs[n_in:]
        c_ins, refs = refs[:cn], refs[cn:]
        outs, refs = refs[:n_out], refs[n_out:]
        c_outs, refs = refs[:cn], refs[cn:]
        scr, c_sems = refs[:n_s], refs[n_s:]
        ids = [pl.program_id(a) for a in range(len(grid))]
        first = functools.reduce(jnp.logical_and, [i == 0 for i in ids])
        last = functools.reduce(jnp.logical_and, [i == g - 1 for i, g in zip(ids, grid)])

        @pl.when(first)
        def _():
            comm.start(c_ins, c_outs, c_sems)

        body(*ins, *outs, *scr)

        @pl.when(last)
        def _():
            comm.finish(c_ins, c_outs, c_sems)

    res = pl.pallas_call(
        hosted, name=name, grid=grid, in_specs=list(in_specs) + _hbm_specs(cn),
        out_specs=list(out_specs) + _hbm_specs(cn), out_shape=list(out_shape) + list(comm.out_shape),
        scratch_shapes=list(scratch_shapes) + list(comm.scratch), compiler_params=_cparams(sem),
    )(*args, *comm.inputs)
    return res[:n_out], res[n_out:]


def _mla_fwd_t(q, k, vt, name, comm=None):
    B, S, _ = q.shape
    tq, tk = MLA_TQ, MLA_TK
    assert tq == tk
    npair = N_HEADS // 2
    nq = S // tq

    def body(q_ref, k_ref, vt_ref, o_ref, lse_ref, s_scr, e_scr, acc_scr, m_scr, a_scr):
        i = pl.program_id(1)
        diag = _key_le_query(tk, tq)
        m_scr[...] = jnp.full_like(m_scr, NEG)
        acc_scr[...] = jnp.zeros_like(acc_scr)

        def step(j, masked):
            rows = pl.ds(pl.multiple_of(j * tk, tk), tk)
            for h in range(N_HEADS):
                hsl = slice(h * LANES, (h + 1) * LANES)
                s_scr[h] = _dot_nt(k_ref[0, rows, hsl], q_ref[0, :, hsl])
            for h in range(N_HEADS):
                s = s_scr[h]
                if masked:
                    s = jnp.where(diag, s, NEG)
                m_old = m_scr[h:h + 1, :]
                m_new = jnp.maximum(m_old, jnp.max(s, axis=0, keepdims=True))
                a_scr[h:h + 1, :] = jnp.exp2((m_old - m_new) * MLA_C)
                e_scr[h] = jnp.exp2((s - m_new) * MLA_C).astype(BF16)
                m_scr[h:h + 1, :] = m_new
            for h in range(N_HEADS):
                vj = vt_ref[0, h // 2, j]
                vh = jnp.where(_row_mask(vj.shape, h % 2), vj, jnp.ones_like(vj))
                acc_scr[h] = acc_scr[h] * a_scr[h:h + 1, :] + _dot_nn(vh, e_scr[h])

        def loop_body(j, carry):
            step(j, False)
            return carry

        lax.fori_loop(0, i, loop_body, 0)
        step(i, True)
        rows0 = _row_mask((LANES, tq), 0)
        for p in range(npair):
            l0 = acc_scr[2 * p, HEAD_DIM:HEAD_DIM + 1, :]
            l1 = acc_scr[2 * p + 1, 0:1, :]
            o_t = jnp.where(rows0, acc_scr[2 * p] / l0, acc_scr[2 * p + 1] / l1)
            o_ref[0, :, p * LANES:(p + 1) * LANES] = jnp.transpose(o_t)
            lse_ref[0, p, 0] = jnp.zeros((8, tq), F32)
            lse_ref[0, p, 0, 0:1, :] = m_scr[2 * p:2 * p + 1, :] * MLA_C + jnp.log(l0) * LOG2E
            lse_ref[0, p, 0, 1:2, :] = m_scr[2 * p + 1:2 * p + 2, :] * MLA_C + jnp.log(l1) * LOG2E

    return _host_call(
        body, comm, name=name,
        out_shape=[jax.ShapeDtypeStruct((B, S, D_B), F32), jax.ShapeDtypeStruct((B, npair, nq, 8, tq), F32)],
        grid=(B, nq),
        in_specs=[pl.BlockSpec((1, tq, N_HEADS * LANES), lambda b, i: (b, i, 0)),
                  pl.BlockSpec((1, S, N_HEADS * LANES), lambda b, i: (b, 0, 0)),
                  pl.BlockSpec((1, npair, S // tk, LANES, tk), lambda b, i: (b, 0, 0, 0, 0))],
        out_specs=[pl.BlockSpec((1, tq, D_B), lambda b, i: (b, i, 0)),
                   pl.BlockSpec((1, npair, 1, 8, tq), lambda b, i: (b, 0, i, 0, 0))],
        scratch_shapes=[pltpu.VMEM((N_HEADS, tk, tq), F32), pltpu.VMEM((N_HEADS, tk, tq), BF16),
                        pltpu.VMEM((N_HEADS, LANES, tq), F32), pltpu.VMEM((N_HEADS, tq), F32),
                        pltpu.VMEM((N_HEADS, tq), F32)],
        args=(q, k, vt))


def _rms_bwd_delta(dy, dy_blk, o, g, name):
    B, S, n = o.shape
    tq = MLA_TQ
    tr = ROW_TILE
    sub = tr // tq
    npair = N_HEADS // 2
    tiles = S // tr

    def body(dy_ref, o_ref, g_ref, do_ref, dg_ref, d_ref):
        ov = o_ref[0]
        dyv = dy_ref[...]
        r = lax.rsqrt(jnp.mean(ov * ov, axis=-1, keepdims=True) + EPS)
        nrm = ov * r
        dn = dyv * g_ref[...]
        do = (r * (dn - nrm * jnp.mean(dn * nrm, axis=-1, keepdims=True))).astype(do_ref.dtype)
        do_ref[0] = do

        @pl.when((pl.program_id(0) == 0) & (pl.program_id(1) == 0))
        def _():
            dg_ref[...] = jnp.zeros_like(dg_ref)

        dg_ref[...] += jnp.sum(dyv * nrm, axis=0, keepdims=True)
        prod = do.astype(F32) * ov
        d_ref[...] = jnp.zeros_like(d_ref)
        for p in range(npair):
            for s in range(sub):
                prod_t = jnp.transpose(prod[s * tq:(s + 1) * tq, p * LANES:(p + 1) * LANES])
                d_ref[0, p, s, 0:1, :] = jnp.sum(prod_t[:HEAD_DIM], axis=0, keepdims=True)
                d_ref[0, p, s, 1:2, :] = jnp.sum(prod_t[HEAD_DIM:], axis=0, keepdims=True)

    tok = pl.BlockSpec((1, tr, n), lambda b, t: (b, t, 0))
    return pl.pallas_call(
        body, name=name,
        out_shape=[jax.ShapeDtypeStruct((B, S, n), BF16), jax.ShapeDtypeStruct((1, n), F32),
                   jax.ShapeDtypeStruct((B, npair, S // tq, 8, tq), F32)],
        grid=(B, tiles),
        in_specs=[pl.BlockSpec((tr, n), lambda b, t: (b * tiles + t, dy_blk)), tok,
                  pl.BlockSpec((1, n), lambda b, t: (0, 0))],
        out_specs=[tok, pl.BlockSpec((1, n), lambda b, t: (0, 0)),
                   pl.BlockSpec((1, npair, sub, 8, tq), lambda b, t: (b, 0, t, 0, 0))],
        compiler_params=_cparams(("arbitrary", "arbitrary")),
    )(dy, o, g)


def _mla_bwd_t(q, k, v, do, lse, delta, name, comm=None):
    B, S, _ = q.shape
    tq, tk = MLA_TQ, MLA_TK
    assert tq == tk
    npair = N_HEADS // 2
    nq = S // tq

    hg = N_HEADS
    pg = hg // 2
    ngroup = N_HEADS // hg

    def body(q_ref, do_ref, lse_ref, dl_ref, k_ref, v_ref, dk_ref, dv_ref, dq_ref,
             s_scr, dp_scr, p_scr, ds_scr, dk_s, dv_s, kt_s):
        j = pl.program_id(2)

        @pl.when(j == 0)
        def _():
            dq_ref[...] = jnp.zeros_like(dq_ref)

        dk_s[...] = jnp.zeros_like(dk_s)
        dv_s[...] = jnp.zeros_like(dv_s)
        diag = _key_le_query(tk, tq)
        hsl = lambda h: slice(h * LANES, (h + 1) * LANES)
        for h in range(hg):
            kt_s[h] = jnp.transpose(k_ref[0, :, hsl(h)].astype(F32)).astype(BF16)

        def step(i, masked):
            rows = pl.ds(pl.multiple_of(i * tq, tq), tq)

            def dom(h):
                dov = do_ref[0, rows, hsl(h // 2)]
                return jnp.where(_head_mask((tq, LANES), h % 2), dov, jnp.zeros_like(dov))

            for h in range(hg):
                s_scr[h] = _dot_nt(k_ref[0, :, hsl(h)], q_ref[0, rows, hsl(h)])
                dp_scr[h] = _dot_nt(v_ref[0, :, hsl(h // 2)], dom(h))
            for h in range(hg):
                pr = jnp.exp2(s_scr[h] * MLA_C - lse_ref[0, h // 2, i, h % 2:h % 2 + 1, :])
                if masked:
                    pr = jnp.where(diag, pr, 0.0)
                p_scr[h] = pr.astype(BF16)
                ds_scr[h] = (pr * (dp_scr[h] - dl_ref[0, h // 2, i, h % 2:h % 2 + 1, :])).astype(BF16)
            for h in range(hg):
                dv_s[h // 2] += _dot_nn(p_scr[h], dom(h))
                dk_s[h] += _dot_nn(ds_scr[h], q_ref[0, rows, hsl(h)])
                dq_ref[0, h // 2, i, hsl(h % 2), :] += _dot_nn(kt_s[h], ds_scr[h]) * MLA_SCALE

        step(j, True)

        def loop_body(i, carry):
            step(i, False)
            return carry

        lax.fori_loop(j + 1, nq, loop_body, 0)
        for h in range(hg):
            dk_ref[0, :, hsl(h)] = dk_s[h] * MLA_SCALE
        for p in range(pg):
            dv_ref[0, :, hsl(p)] = dv_s[p]

    stat = pl.BlockSpec((1, pg, nq, 8, tq), lambda b, g, j: (b, g, 0, 0, 0))
    return _host_call(
        body, comm, name=name,
        out_shape=[jax.ShapeDtypeStruct((B, S, N_HEADS * LANES), F32), jax.ShapeDtypeStruct((B, S, D_B), F32),
                   jax.ShapeDtypeStruct((B, npair, nq, 2 * LANES, tq), F32)],
        grid=(B, ngroup, S // tk),
        in_specs=[pl.BlockSpec((1, S, hg * LANES), lambda b, g, j: (b, 0, g)),
                  pl.BlockSpec((1, S, pg * LANES), lambda b, g, j: (b, 0, g)),
                  stat, stat,
                  pl.BlockSpec((1, tk, hg * LANES), lambda b, g, j: (b, j, g)),
                  pl.BlockSpec((1, tk, pg * LANES), lambda b, g, j: (b, j, g))],
        out_specs=[pl.BlockSpec((1, tk, hg * LANES), lambda b, g, j: (b, j, g)),
                   pl.BlockSpec((1, tk, pg * LANES), lambda b, g, j: (b, j, g)),
                   pl.BlockSpec((1, pg, nq, 2 * LANES, tq), lambda b, g, j: (b, g, 0, 0, 0))],
        scratch_shapes=[pltpu.VMEM((hg, tk, tq), F32), pltpu.VMEM((hg, tk, tq), F32),
                        pltpu.VMEM((hg, tk, tq), BF16), pltpu.VMEM((hg, tk, tq), BF16),
                        pltpu.VMEM((hg, tk, LANES), F32), pltpu.VMEM((pg, tk, LANES), F32),
                        pltpu.VMEM((hg, LANES, tk), BF16)],
        args=(q, do, lse, delta, k, v))


def _bucket_tables():
    return jnp.asarray(np.stack([_band_buckets(d) for d in DILATIONS]))


def _local_step(x, target, mod, wts, gains, rel_bias, ffn_shards=None, bias=None):
    B, S, D = x.shape
    T = B * S
    sh1, sc1, g1, sh2, sc2, g2 = [mod[:, i * D:(i + 1) * D].reshape(B, 1, D) for i in range(N_MOD)]
    cs, sn = _rope_tables()
    buckets_dev = _bucket_tables()
    if bias is None:
        bias, _ = _bias_tables(rel_bias, buckets_dev, "rel_bias_tables")
    w_in = wts["w_in"]

    h1 = _adaln_fwd(x, gains["g_norm1"], sc1, sh1, "adaln1_fwd")
    h1f = h1.reshape(T, D)
    qkv_v, rest3, cqn, ckvn = _in_proj(h1, w_in[:, :P_QKV], w_in[:, P_QKV:], gains["g_cq"], gains["g_ckv"],
                                       "mm_in")
    rest = rest3.reshape(T, P_REST)
    o_d, lse_d = [], []
    late_got = []
    for i, d in enumerate(DILATIONS):
        comm = _GatherComm(ffn_shards[i + 1:i + 2]) if (ffn_shards and i < 2) else None
        (o_i, lse_i), got = _dil_fwd(qkv_v[i], bias, i, d, f"dil_fwd_{d}", comm)
        late_got += list(got)
        o_d.append(o_i)
        lse_d.append(lse_i)
    if ffn_shards:
        wts = dict(wts, w_out=late_got[1].reshape(D, D))
    out_a_v, lse_a_v = _dil_merge(o_d, lse_d, "dil_merge")
    out_a = out_a_v[0]
    q_raw = _mm(cqn, wts["w_uq"], "nn", F32, "mm_uq").reshape(B, S, N_HEADS * LANES)
    qc = _rope_apply(q_raw, cs, sn, BF16, "rope_q")
    kn_raw = _mm(ckvn, wts["w_kv"][:, :N_HEADS * LANES], "nn", F32, "mm_uk").reshape(B, S, N_HEADS * LANES)
    kc = _rope_apply(kn_raw, cs, sn, BF16, "rope_k", add=rest3, add_blk=KV_LORA // LANES)
    v = _mm(ckvn, wts["w_kv"][:, N_HEADS * LANES:], "nn", BF16, "mm_uv").reshape(B, S, D_B)
    vt = jnp.transpose(v.reshape(B, S // MLA_TK, MLA_TK, N_HEADS // 2, LANES), (0, 3, 1, 4, 2))
    (out_b, lse_b), got = _mla_fwd_t(qc, kc, vt, "mla_fwd", _GatherComm(ffn_shards[:1]) if ffn_shards else None)
    if ffn_shards:
        wts = dict(wts, w_ffn_in=got[0].reshape(N_CHIP, D, -1), w_ffn_out=late_got[0].reshape(D_FF, D))
    out_af, out_bf = out_a.reshape(T, D_A), out_b.reshape(T, D_B)
    y = _rms_fwd_pair(out_af, out_bf, gains["g_out_a"], gains["g_out_b"], "rms_out_fwd")
    mix = _mm(y, wts["w_out"], "nn", F32, "mm_out").reshape(B, S, D)
    h2, x1 = _adaln_fwd(x, gains["g_norm2"], sc2, sh2, "adaln2_fwd", mix=mix, gate=g1)
    h2f = h2.reshape(T, D)
    gu, act = _ffn_in_fwd(h2f, wts["w_ffn_in"], "mm_ffn_in")
    f = _mm(act, wts["w_ffn_out"], "nn", F32, "mm_ffn_out").reshape(B, S, D)
    dx2, df, dg2, dg_final, loss = _final_loss(x1, f, g2, gains["g_final"], target, "final_loss")

    dff = df.reshape(T, D)
    dgu = _ffn_out_bwd(dff, wts["w_ffn_out"], gu, "mm_ffn_out_dx")
    gw_ffn_out = _mm(act, dff, "tn", F32, "mm_ffn_out_dw")
    dh2 = _mm(dgu, wts["w_ffn_in"], "nt", F32, "mm_ffn_in_dx", col_blocks=N_CHIP, halves=True).reshape(B, S, D)
    gw_ffn_in = _mm(h2f, dgu, "tn", F32, "mm_ffn_in_dw", col_blocks=N_CHIP, halves=True)
    ffn_g8 = ffn_r1 = None
    if ffn_shards:
        ffn_g8 = [gw_ffn_in.reshape(N_DEV, -1, gw_ffn_in.shape[-1]), gw_ffn_out.reshape(N_DEV, -1, D)]
        dx1, dsh2, dsc2, dg_norm2, dg1, dmix, ffn_r1 = _adaln_bwd(
            dh2, x1, gains["g_norm2"], sc2, dx2, "adaln2_bwd", mix=mix, gate=g1, comm=_ToSiblingComm(ffn_g8))
    else:
        dx1, dsh2, dsc2, dg_norm2, dg1, dmix = _adaln_bwd(dh2, x1, gains["g_norm2"], sc2, dx2, "adaln2_bwd",
                                                          mix=mix, gate=g1)
    dmixf = dmix.reshape(T, D)
    dy = _mm(dmixf, wts["w_out"], "nt", F32, "mm_out_dx")
    gw_out = _mm(y, dmixf, "tn", F32, "mm_out_dw")
    do_a_v, dg_out_a = _rms_bwd_views(dy, 0, out_a, gains["g_out_a"], "rms_outa_bwd")
    do_b3, dg_out_b, delta_b = _rms_bwd_delta(dy, 1, out_b, gains["g_out_b"], "rms_outb_bwd")
    ffn_a4 = ffn_send = None
    if ffn_shards:
        ffn_a4, ffn_send = _rs_first(ffn_g8, "ffn", r1=ffn_r1)
    (dkc, dv, dq_t), ffn_r2 = _mla_bwd_t(qc, kc, v, do_b3, lse_b, delta_b, "mla_bwd",
                                         _ToChipsComm(ffn_send[:1]) if ffn_shards else None)
    dq_raw = _qrope_bwd(dq_t, cs, -sn, "rope_q_bwd").reshape(T, N_HEADS * LANES)
    dkrw = _krope_bwd(dkc, cs, -sn, "rope_k_bwd").reshape(T, LANES)
    dcqn = _mm(dq_raw, wts["w_uq"], "nt", F32, "mm_uq_dx")
    gw_uq = _mm(cqn, dq_raw, "tn", F32, "mm_uq_dw")
    dkv = jnp.concatenate([dkc.reshape(T, -1), dv.reshape(T, -1)], axis=1).astype(BF16)
    dckvn = _mm(dkv, wts["w_kv"], "nt", F32, "mm_ukv_dx")
    gw_kv = _mm(ckvn, dkv, "tn", F32, "mm_ukv_dw")
    dcq, dg_cq = _rms_bwd(dcqn, 0, rest, 1, Q_LORA, gains["g_cq"], "rms_cq_bwd")
    dckv, dg_ckv = _rms_bwd(dckvn, 0, rest, 0, KV_LORA, gains["g_ckv"], "rms_ckv_bwd")
    dqkv_d, dbias_d = [], []
    for i, d in enumerate(DILATIONS):
        comm = _ToChipsComm(ffn_send[1:]) if (ffn_shards and i == 0) else None
        (dqkv_i, dbias_i), got = _dil_bwd(qkv_v[i], do_a_v[i], out_a_v[i], lse_a_v[i], bias, i, d,
                                          f"dil_bwd_{d}", comm)
        if comm is not None:
            ffn_r2 = list(ffn_r2) + list(got)
        dqkv_d.append(dqkv_i)
        dbias_d.append(dbias_i)
    dqkv = _sum_views_bf16(dqkv_d, "dil_bwd_sum").reshape(T, P_QKV)
    g_rel_bias = _bias_grad(dbias_d, buckets_dev, "rel_bias_grad")[:, :N_BUCKETS].T
    dproj = jnp.concatenate([dqkv, dckv, dkrw, dcq], axis=1)
    gw_in = _mm(h1f, dproj, "tn", F32, "mm_in_dw")
    mix_a4 = mix_r2 = None
    if ffn_shards:
        nat = [_w_in_from_kernel(gw_in), _w_uq_from_kernel(gw_uq), _w_ukv_from_kernel(gw_kv)]
        g8 = [_shards_from_full(g) for g in nat] + [gw_out]
        mix_a4, mix_send = _rs_first([g.reshape(N_DEV, -1, g.shape[-1]) for g in g8], "mix")
        dh1, mix_r2 = _mm(dproj, w_in, "nt", F32, "mm_in_dx", comm=_ToChipsComm(mix_send))
    else:
        dh1 = _mm(dproj, w_in, "nt", F32, "mm_in_dx")
    dh1 = dh1.reshape(B, S, D)
    grad_x, dsh1, dsc1, dg_norm1 = _adaln_bwd(dh1, x, gains["g_norm1"], sc1, dx1, "adaln1_bwd")
    gmod = jnp.concatenate([dsh1, dsc1, dg1, dsh2, dsc2, dg2], axis=-1).reshape(B, N_MOD * D)
    grads = dict(w_in=gw_in, w_uq=gw_uq, w_kv=gw_kv, w_out=gw_out, w_ffn_in=gw_ffn_in, w_ffn_out=gw_ffn_out,
                 g_norm1=dg_norm1, g_cq=dg_cq, g_ckv=dg_ckv, rel_bias=g_rel_bias, g_out_a=dg_out_a,
                 g_out_b=dg_out_b, g_norm2=dg_norm2, g_final=dg_final, ffn_pending=(ffn_a4, ffn_r2),
                 mix_pending=(mix_a4, mix_r2))
    return loss, grad_x, gmod, grads


def _w_in_to_kernel(w):
    z = lambda n: jnp.zeros((w.shape[0], n), w.dtype)
    i3, i4, i5 = 3 * D_A, 3 * D_A + Q_LORA, 3 * D_A + Q_LORA + KV_LORA
    return jnp.concatenate([w[:, :i3], w[:, i4:i5], z(NOPE_DIM), w[:, i5:], z(LANES - NOPE_DIM - ROPE_DIM),
                            w[:, i3:i4]], axis=1)


def _w_in_from_kernel(g):
    o = P_QKV + KV_LORA
    return jnp.concatenate([g[:, :P_QKV], g[:, o + LANES:], g[:, P_QKV:o],
                            g[:, o + NOPE_DIM:o + NOPE_DIM + ROPE_DIM]], axis=1)


def _w_uq_to_kernel(w):
    w3 = w.reshape(Q_LORA, N_HEADS, NOPE_DIM + ROPE_DIM)
    return jnp.pad(w3, ((0, 0), (0, 0), (0, LANES - NOPE_DIM - ROPE_DIM))).reshape(Q_LORA, N_HEADS * LANES)


def _w_uq_from_kernel(g):
    return g.reshape(Q_LORA, N_HEADS, LANES)[:, :, :NOPE_DIM + ROPE_DIM].reshape(Q_LORA, -1)


def _w_ukv_to_kernel(w):
    w3 = w.reshape(KV_LORA, N_HEADS, 2 * HEAD_DIM)
    wk = jnp.pad(w3[:, :, :NOPE_DIM], ((0, 0), (0, 0), (0, LANES - NOPE_DIM))).reshape(KV_LORA, N_HEADS * LANES)
    wv = w3[:, :, NOPE_DIM:].reshape(KV_LORA, D_B)
    return jnp.concatenate([wk, wv], axis=1)


def _w_ukv_from_kernel(g):
    gk = g[:, :N_HEADS * LANES].reshape(KV_LORA, N_HEADS, LANES)[:, :, :NOPE_DIM]
    gv = g[:, N_HEADS * LANES:].reshape(KV_LORA, N_HEADS, HEAD_DIM)
    return jnp.concatenate([gk, gv], axis=2).reshape(KV_LORA, -1)


MESH = pl.DeviceIdType.MESH


def _my_place():
    return lax.axis_index("x"), lax.axis_index("y"), lax.axis_index("c")


def _other_chips(x, y):
    return [(1 - x, y), (x, 1 - y), (1 - x, 1 - y)]


def _allgather8(x_shard, name, in_hbm):
    m_per, n = x_shard.shape
    space = pl.ANY if in_hbm else pltpu.VMEM

    def body(x_ref, out_ref, send_sems, recv_sems, local_sem):
        x, y, c = _my_place()
        me, sibling = (x, y, c), (x, y, 1 - c)
        chips = _other_chips(x, y)

        def rows(px, py, pc):
            return out_ref.at[pl.ds((4 * px + 2 * py + pc) * m_per, m_per), :]

        def copy(k, block, to, src=None):
            return pltpu.make_async_remote_copy(
                src_ref=rows(*block) if src is None else src, dst_ref=rows(*block),
                send_sem=send_sems.at[k], recv_sem=recv_sems.at[k], device_id=to, device_id_type=MESH)

        mine = pltpu.make_async_copy(x_ref, rows(*me), local_sem)
        mine.start()
        first = [copy(0, me, sibling, src=x_ref)]
        first += [copy(1 + j, me, (*chip, c), src=x_ref) for j, chip in enumerate(chips)]
        for cp in first:
            cp.start()
        passed = [copy(4 + j, (*chip, c), sibling) for j, chip in enumerate(chips)]
        for j, chip in enumerate(chips):
            copy(1 + j, (*chip, c), me).wait_recv()
            passed[j].start()
        copy(0, sibling, me).wait_recv()
        for j, chip in enumerate(chips):
            copy(4 + j, (*chip, 1 - c), me).wait_recv()
        for cp in first + passed:
            cp.wait_send()
        mine.wait()

    return pl.pallas_call(
        body, name=name,
        out_shape=jax.ShapeDtypeStruct((N_DEV * m_per, n), x_shard.dtype),
        in_specs=[pl.BlockSpec(memory_space=space)],
        out_specs=pl.BlockSpec(memory_space=space),
        scratch_shapes=[pltpu.SemaphoreType.DMA((7,)), pltpu.SemaphoreType.DMA((7,)), pltpu.SemaphoreType.DMA],
        compiler_params=pltpu.CompilerParams(vmem_limit_bytes=VMEM_LIMIT),
    )(x_shard)


def _hbm_specs(n):
    return [pl.BlockSpec(memory_space=pl.ANY)] * n


class _GatherComm:
    def __init__(self, shards):
        self.n = n = len(shards)
        self.inputs = [s.reshape(2, s.shape[0] // 2, s.shape[1]) for s in shards]
        self.out_shape = [jax.ShapeDtypeStruct((N_DEV,) + s.shape[1:], s.dtype) for s in self.inputs]
        self.scratch = [pltpu.SemaphoreType.DMA((7 * n,)), pltpu.SemaphoreType.DMA((7 * n,))]

    def _parts(self, xs, outs, sems):
        send_sems, recv_sems = sems
        x, y, c = _my_place()

        def blk(k, px, py, pc):
            return outs[k].at[4 * px + 2 * py + pc]

        def copy(k, kind, block, to, own=False):
            return pltpu.make_async_remote_copy(
                src_ref=xs[k].at[c] if own else blk(k, *block), dst_ref=blk(k, *block),
                send_sem=send_sems.at[7 * k + kind], recv_sem=recv_sems.at[7 * k + kind],
                device_id=to, device_id_type=MESH)

        def whole(k):
            return pltpu.make_async_remote_copy(
                src_ref=xs[k], dst_ref=outs[k].at[pl.ds(4 * x + 2 * y, 2)],
                send_sem=send_sems.at[7 * k], recv_sem=recv_sems.at[7 * k],
                device_id=(x, y, 1 - c), device_id_type=MESH)

        me, sibling = (x, y, c), (x, y, 1 - c)
        chips = _other_chips(x, y)
        first = []
        for k in range(self.n):
            first.append(whole(k))
            first += [copy(k, 1 + j, me, (*chip, c), own=True) for j, chip in enumerate(chips)]
        return copy, whole, me, sibling, chips, c, first

    def start(self, xs, outs, sems):
        for cp in self._parts(xs, outs, sems)[-1]:
            cp.start()

    def finish(self, xs, outs, sems):
        copy, whole, me, sibling, chips, c, first = self._parts(xs, outs, sems)
        passed = []
        for j, chip in enumerate(chips):
            for k in range(self.n):
                copy(k, 1 + j, (*chip, c), me).wait_recv()
                fwd = copy(k, 4 + j, (*chip, c), sibling)
                fwd.start()
                passed.append(fwd)
        for k in range(self.n):
            whole(k).wait_recv()
        for j, chip in enumerate(chips):
            for k in range(self.n):
                copy(k, 4 + j, (*chip, 1 - c), me).wait_recv()
        for cp in first + passed:
            cp.wait_send()


class _AllGatherComm:
    def __init__(self, x):
        self.inputs = [x]
        self.n = 1
        self.out_shape = [jax.ShapeDtypeStruct((N_DEV,) + x.shape, x.dtype)]
        self.scratch = [pltpu.SemaphoreType.DMA((7,)), pltpu.SemaphoreType.DMA((7,)), pltpu.SemaphoreType.DMA]

    def _parts(self, xs, outs, sems):
        send_sems, recv_sems, local_sem = sems
        x_ref, out_ref = xs[0], outs[0]
        x, y, c = _my_place()

        def copy(k, block, to, own=False):
            blk = out_ref.at[4 * block[0] + 2 * block[1] + block[2]]
            return pltpu.make_async_remote_copy(
                src_ref=x_ref if own else blk, dst_ref=blk, send_sem=send_sems.at[k], recv_sem=recv_sems.at[k],
                device_id=to, device_id_type=MESH)

        me, sibling = (x, y, c), (x, y, 1 - c)
        chips = _other_chips(x, y)
        local = pltpu.make_async_copy(x_ref, out_ref.at[4 * x + 2 * y + c], local_sem)
        first = [copy(0, me, sibling, own=True)]
        first += [copy(1 + j, me, (*chip, c), own=True) for j, chip in enumerate(chips)]
        return copy, me, sibling, chips, c, local, first

    def start(self, xs, outs, sems):
        _, _, _, _, _, local, first = self._parts(xs, outs, sems)
        for cp in [local] + first:
            cp.start()

    def finish(self, xs, outs, sems):
        copy, me, sibling, chips, c, local, first = self._parts(xs, outs, sems)
        passed = []
        for j, chip in enumerate(chips):
            copy(1 + j, (*chip, c), me).wait_recv()
            fwd = copy(4 + j, (*chip, c), sibling)
            fwd.start()
            passed.append(fwd)
        copy(0, sibling, me).wait_recv()
        for j, chip in enumerate(chips):
            copy(4 + j, (*chip, 1 - c), me).wait_recv()
        for cp in first + passed:
            cp.wait_send()
        local.wait()


class _BothComm:
    def __init__(self, a, b):
        self.a, self.b = a, b
        self.inputs = a.inputs + b.inputs
        self.n = a.n + b.n
        self.out_shape = a.out_shape + b.out_shape
        self.scratch = a.scratch + b.scratch

    def _split(self, xs, outs, sems):
        na, ns = self.a.n, len(self.a.scratch)
        return (xs[:na], outs[:na], sems[:ns]), (xs[na:], outs[na:], sems[ns:])

    def start(self, xs, outs, sems):
        pa, pb = self._split(xs, outs, sems)
        self.a.start(*pa)
        self.b.start(*pb)

    def finish(self, xs, outs, sems):
        pa, pb = self._split(xs, outs, sems)
        self.a.finish(*pa)
        self.b.finish(*pb)


class _ToChipsComm:
    def __init__(self, a4s):
        self.inputs = list(a4s)
        self.n = n = len(a4s)
        nc = N_CHIP - 1
        self.out_shape = [jax.ShapeDtypeStruct((nc,) + a.shape[1:], a.dtype) for a in a4s]
        self.scratch = [pltpu.SemaphoreType.DMA((nc * n,)), pltpu.SemaphoreType.DMA((nc * n,))]

    def _copies(self, as_, rs, sems):
        send_sems, recv_sems = sems
        x, y, c = _my_place()
        nc = N_CHIP - 1
        return [pltpu.make_async_remote_copy(
            src_ref=as_[k].at[2 * cx + cy], dst_ref=rs[k].at[j], send_sem=send_sems.at[nc * k + j],
            recv_sem=recv_sems.at[nc * k + j], device_id=(cx, cy, c), device_id_type=MESH)
            for k in range(self.n) for j, (cx, cy) in enumerate(_other_chips(x, y))]

    def start(self, as_, rs, sems):
        for cp in self._copies(as_, rs, sems):
            cp.start()

    def finish(self, as_, rs, sems):
        for cp in self._copies(as_, rs, sems):
            cp.wait()


def _run_comm(comm, name):
    n = comm.n

    def body(*refs):
        ins, outs, sems = refs[:n], refs[n:2 * n], refs[2 * n:]
        comm.start(ins, outs, sems)
        comm.finish(ins, outs, sems)

    return pl.pallas_call(
        body, name=name, out_shape=comm.out_shape, in_specs=_hbm_specs(n), out_specs=_hbm_specs(n),
        scratch_shapes=comm.scratch,
    )(*comm.inputs)


class _ToSiblingComm:
    def __init__(self, g8s):
        self.inputs = list(g8s)
        self.n = n = len(g8s)
        self.out_shape = [jax.ShapeDtypeStruct((N_CHIP,) + g.shape[1:], g.dtype) for g in g8s]
        self.scratch = [pltpu.SemaphoreType.DMA((N_CHIP * n,)), pltpu.SemaphoreType.DMA((N_CHIP * n,))]

    def _copies(self, gs, rs, sems):
        send_sems, recv_sems = sems
        x, y, c = _my_place()
        return [pltpu.make_async_remote_copy(
            src_ref=gs[k].at[2 * s + 1 - c], dst_ref=rs[k].at[s], send_sem=send_sems.at[N_CHIP * k + s],
            recv_sem=recv_sems.at[N_CHIP * k + s], device_id=(x, y, 1 - c), device_id_type=MESH)
            for k in range(self.n) for s in range(N_CHIP)]

    def start(self, gs, rs, sems):
        for cp in self._copies(gs, rs, sems):
            cp.start()

    def finish(self, gs, rs, sems):
        for cp in self._copies(gs, rs, sems):
            cp.wait()


def _swap_halves(hs, name):
    n = len(hs)

    def body(*refs):
        o_refs = refs[n:2 * n]
        send_sems, recv_sems = refs[2 * n:]
        x, y, c = _my_place()

        def remote(k, slot):
            return pltpu.make_async_remote_copy(
                src_ref=o_refs[k].at[slot], dst_ref=o_refs[k].at[slot], send_sem=send_sems.at[k],
                recv_sem=recv_sems.at[k], device_id=(x, y, 1 - c), device_id_type=MESH)

        sends = [remote(k, c) for k in range(n)]
        for cp in sends:
            cp.start()
        for k in range(n):
            remote(k, 1 - c).wait_recv()
        for cp in sends:
            cp.wait_send()

    return pl.pallas_call(
        body, name=name,
        out_shape=[jax.ShapeDtypeStruct(h.shape, h.dtype) for h in hs],
        in_specs=_hbm_specs(n), out_specs=_hbm_specs(n),
        input_output_aliases={k: k for k in range(n)},
        scratch_shapes=[pltpu.SemaphoreType.DMA((n,)), pltpu.SemaphoreType.DMA((n,))],
    )(*hs)


ADD_TILES = 2


def _add_blocks(a_list, a_idx_fn, others_list, ns, sel, name, out_blocks=None, out_idx_fn=None,
                bf16_copy=False):
    out_blocks = out_blocks or ns
    out_idx_fn = out_idx_fn or (lambda s, sel_ref: s)
    n = len(a_list)
    n_o = len(others_list[0])
    per = 1 + n_o

    def body(sel_ref, *refs):
        for k in range(n):
            ins = refs[k * per:(k + 1) * per]
            acc = ins[0][0]
            for r in ins[1:]:
                acc = acc + r[0].astype(F32)
            refs[n * per + k][0] = acc
            if bf16_copy:
                refs[n * per + n + k][0] = acc.astype(BF16)

    in_specs, args, out_specs, out_shape = [], [], [], []
    for a, others in zip(a_list, others_list):
        _, R, N = a.shape
        tr = R // ADD_TILES
        assert tr % 8 == 0, a.shape
        in_specs.append(pl.BlockSpec((1, tr, N), lambda s, i, sel_ref: (a_idx_fn(s, sel_ref), i, 0)))
        args.append(a)
        for arr, fixed in others:
            if fixed is None:
                in_specs.append(pl.BlockSpec((1, tr, N), lambda s, i, sel_ref: (s, i, 0)))
            else:
                in_specs.append(pl.BlockSpec((1, tr, N), lambda s, i, sel_ref, fixed=fixed: (fixed, i, 0)))
            args.append(arr)
        out_specs.append(pl.BlockSpec((1, tr, N), lambda s, i, sel_ref: (out_idx_fn(s, sel_ref), i, 0)))
        out_shape.append(jax.ShapeDtypeStruct((out_blocks, R, N), a.dtype))
    if bf16_copy:
        out_specs = out_specs + out_specs
        out_shape = out_shape + [jax.ShapeDtypeStruct(o.shape, BF16) for o in out_shape]
    grid_spec = pltpu.PrefetchScalarGridSpec(num_scalar_prefetch=1, grid=(ns, ADD_TILES), in_specs=in_specs,
                                             out_specs=out_specs)
    return pl.pallas_call(
        body, name=name, out_shape=out_shape, grid_spec=grid_spec,
        compiler_params=_cparams(("parallel", "parallel")),
    )(sel, *args)


def _rs_first(g8s, tag, r1=None):
    c_sel = jnp.reshape(lax.axis_index("c"), (1,)).astype(jnp.int32)
    if r1 is None:
        r1 = _run_comm(_ToSiblingComm(g8s), f"rs_to_sibling_{tag}")
    res = _add_blocks(g8s, lambda s, sel: 2 * s + sel[0], [[(r, None)] for r in r1], N_CHIP, c_sel,
                      f"rs_add_sibling_{tag}", bf16_copy=True)
    return list(res[:len(g8s)]), list(res[len(g8s):])


def _rs_last(a4s, r2s, tag):
    sel = jnp.stack([2 * lax.axis_index("x") + lax.axis_index("y"), lax.axis_index("c")]).astype(jnp.int32)
    h = _add_blocks(a4s, lambda s, sel: sel[0], [[(r, 0), (r, 1), (r, 2)] for r in r2s], 1, sel,
                    f"rs_add_chips_{tag}", out_blocks=2, out_idx_fn=lambda s, sel: sel[1])
    full = _swap_halves(h, f"rs_swap_halves_{tag}")
    return [f.reshape(2 * f.shape[1], f.shape[2]) for f in full]


def _ada_fwd(c_all, w_ada, b_ada, name):
    nb, D = c_all.shape
    ncol = w_ada.shape[1]
    tc = 512

    def body(c_ref, w_ref, b_ref, o_ref):
        cv = c_ref[...]
        cond = (cv * jax.nn.sigmoid(cv)).astype(BF16)
        o_ref[...] = jnp.dot(cond, w_ref[...].astype(BF16), preferred_element_type=F32) + b_ref[...]

    return pl.pallas_call(
        body, name=name, out_shape=jax.ShapeDtypeStruct((nb, ncol), F32), grid=(ncol // tc,),
        in_specs=[pl.BlockSpec((nb, D), lambda j: (0, 0)), pl.BlockSpec((D, tc), lambda j: (0, j)),
                  pl.BlockSpec((1, tc), lambda j: (0, j))],
        out_specs=pl.BlockSpec((nb, tc), lambda j: (0, j)),
        compiler_params=_cparams(("parallel",)),
    )(c_all, w_ada, b_ada)


def _ada_bwd(c_all, gmod_cols, name):
    nb, D = c_all.shape
    ncol = gmod_cols.shape[1]
    tc = 512

    def body(c_ref, g_ref, o_ref):
        cv = c_ref[...]
        cond = (cv * jax.nn.sigmoid(cv)).astype(BF16)
        o_ref[...] = _dot_tn(cond, g_ref[...].astype(BF16))

    return pl.pallas_call(
        body, name=name, out_shape=jax.ShapeDtypeStruct((D, ncol), F32), grid=(ncol // tc,),
        in_specs=[pl.BlockSpec((nb, D), lambda j: (0, 0)), pl.BlockSpec((nb, tc), lambda j: (0, j))],
        out_specs=pl.BlockSpec((D, tc), lambda j: (0, j)),
        compiler_params=_cparams(("parallel",)),
    )(c_all, gmod_cols)


def _adam_math(w, g, m, v):
    m = ADAM_B1 * m + (1.0 - ADAM_B1) * g
    v = ADAM_B2 * v + (1.0 - ADAM_B2) * (g * g)
    m_hat = m / (1.0 - ADAM_B1 ** ADAM_STEP)
    v_hat = v / (1.0 - ADAM_B2 ** ADAM_STEP)
    delta = -ADAM_LR * (m_hat / (jnp.sqrt(v_hat) + ADAM_EPS) + ADAM_WD * w)
    return delta, m, v


def _adamw(w, g, m, v, name):
    rows, cols = w.shape
    tr = _pick(rows, (256, 192, 176, 128, 64, 8))

    def body(w_ref, g_ref, m_ref, v_ref, d_ref, mo_ref, vo_ref):
        d, mn, vn = _adam_math(w_ref[...], g_ref[...], m_ref[...], v_ref[...])
        d_ref[...] = d
        mo_ref[...] = mn
        vo_ref[...] = vn

    spec = pl.BlockSpec((tr, cols), lambda i: (i, 0))
    return pl.pallas_call(
        body, name=name, out_shape=[jax.ShapeDtypeStruct((rows, cols), F32)] * 3, grid=(rows // tr,),
        in_specs=[spec] * 4, out_specs=[spec] * 3, compiler_params=_cparams(("parallel",)),
    )(w, g, m, v)


VEC_ROWS = 8


def _adamw_rows(w, parts, m, v, name):
    n = w.shape[1]
    P = parts.shape[0]
    assert n % (VEC_ROWS * LANES) == 0, n
    shp = (VEC_ROWS, n // VEC_ROWS)

    def body(w_ref, p_ref, m_ref, v_ref, g_ref, d_ref, mo_ref, vo_ref):
        g = p_ref[0]
        for k in range(1, P):
            g = g + p_ref[k]
        d, mn, vn = _adam_math(w_ref[...], g, m_ref[...], v_ref[...])
        g_ref[...] = g
        d_ref[...] = d
        mo_ref[...] = mn
        vo_ref[...] = vn

    vec = pl.BlockSpec(shp, lambda i: (0, 0))
    out = pl.pallas_call(
        body, name=name, out_shape=[jax.ShapeDtypeStruct(shp, F32)] * 4, grid=(1,),
        in_specs=[vec, pl.BlockSpec((P,) + shp, lambda i: (0, 0, 0)), vec, vec], out_specs=[vec] * 4,
        compiler_params=_cparams(("arbitrary",)),
    )(w.reshape(shp), parts.reshape((P,) + shp), m.reshape(shp), v.reshape(shp))
    return [o.reshape(1, n) for o in out]


_SHARDED = ("w_in", "w_uq", "w_ukv", "w_out", "w_ffn_in", "w_ffn_out")
_SMALL = (("g_norm1", 1024), ("g_cq", 384), ("g_ckv", 256), ("rel_bias", 256), ("g_out_a", 512),
          ("g_out_b", 512), ("g_norm2", 1024), ("g_final", 1024))
_SMALL_PAD = 5120


def _full_from_shards(sh):
    return jnp.transpose(sh, (1, 0, 2)).reshape(sh.shape[1], -1)


def _shards_from_full(full):
    rows, cols = full.shape
    return jnp.transpose(full.reshape(rows, N_CHIP, cols // N_CHIP), (1, 0, 2))


def kernel(x, c, w_ada, b_ada, g_norm1, w_in, g_cq, w_uq, g_ckv, w_ukv, rel_bias, g_out_a, g_out_b, w_out, g_norm2, w_ffn_in, w_ffn_out, g_final, loss_target, m_w_ada, m_b_ada, m_g_norm1, m_w_in, m_g_cq, m_w_uq, m_g_ckv, m_w_ukv, m_rel_bias, m_g_out_a, m_g_out_b, m_w_out, m_g_norm2, m_w_ffn_in, m_w_ffn_out, m_g_final, v_w_ada, v_b_ada, v_g_norm1, v_w_in, v_g_cq, v_w_uq, v_g_ckv, v_w_ukv, v_rel_bias, v_g_out_a, v_g_out_b, v_w_out, v_g_norm2, v_w_ffn_in, v_w_ffn_out, v_g_final):
    names = ["w_ada", "b_ada", "g_norm1", "w_in", "g_cq", "w_uq", "g_ckv", "w_ukv", "rel_bias", "g_out_a",
             "g_out_b", "w_out", "g_norm2", "w_ffn_in", "w_ffn_out", "g_final"]
    W = dict(zip(names, [w_ada, b_ada, g_norm1, w_in, g_cq, w_uq, g_ckv, w_ukv, rel_bias, g_out_a, g_out_b,
                         w_out, g_norm2, w_ffn_in, w_ffn_out, g_final]))
    M = dict(zip(names, [m_w_ada, m_b_ada, m_g_norm1, m_w_in, m_g_cq, m_w_uq, m_g_ckv, m_w_ukv, m_rel_bias,
                         m_g_out_a, m_g_out_b, m_w_out, m_g_norm2, m_w_ffn_in, m_w_ffn_out, m_g_final]))
    V = dict(zip(names, [v_w_ada, v_b_ada, v_g_norm1, v_w_in, v_g_cq, v_w_uq, v_g_ckv, v_w_ukv, v_rel_bias,
                         v_g_out_a, v_g_out_b, v_w_out, v_g_norm2, v_w_ffn_in, v_w_ffn_out, v_g_final]))
    B, S, D = x.shape
    mx, my, mc = _my_place()
    dev = 4 * mx + 2 * my + mc
    chip = 2 * mx + my
    pad_rows = 8

    early = ("w_in", "w_uq", "w_ukv")
    bias, got = _bias_tables(rel_bias, _bucket_tables(), "rel_bias_tables",
                             _BothComm(_AllGatherComm(jnp.pad(c, ((0, pad_rows - B), (0, 0)))),
                                       _GatherComm([W[n][0].astype(BF16) for n in early])))
    c_all = got[0][:, :B].reshape(N_DEV * B, D)

    ada_cols = w_ada.shape[-1]
    b_cols = lax.dynamic_slice_in_dim(b_ada, chip * ada_cols, ada_cols, axis=1)
    mod_cols = _ada_fwd(c_all, w_ada[0], b_cols, "ada_fwd")
    mod_all = _allgather8(mod_cols, "ag_mod", False).reshape(N_DEV, N_DEV * B, ada_cols)[0::2]
    mod_all = jnp.transpose(mod_all, (1, 0, 2)).reshape(N_DEV * B, N_MOD * D)
    mod = lax.dynamic_slice_in_dim(mod_all, dev * B, B, axis=0)

    full = {n: g.reshape((N_CHIP,) + W[n].shape[1:]) for n, g in zip(early, got[1:])}
    wts = dict(w_in=_w_in_to_kernel(_full_from_shards(full["w_in"])),
               w_uq=_w_uq_to_kernel(_full_from_shards(full["w_uq"])),
               w_kv=_w_ukv_to_kernel(_full_from_shards(full["w_ukv"])))
    gains = dict(g_norm1=g_norm1, g_cq=g_cq, g_ckv=g_ckv, g_out_a=g_out_a, g_out_b=g_out_b, g_norm2=g_norm2,
                 g_final=g_final.reshape(1, D))

    loss, grad_x, gmod, grads = _local_step(x, loss_target, mod, wts, gains, rel_bias,
                                            ffn_shards=[w_ffn_in[0].astype(BF16), w_ffn_out[0].astype(BF16),
                                                        w_out[0].astype(BF16)], bias=bias)
    loss = lax.psum(loss[0, 0], ("x", "y", "c"))

    n_small = _SMALL_PAD
    cat = lambda dct: jnp.concatenate([dct[n].reshape(1, -1) for n, _ in _SMALL]
                                      + [jnp.zeros((1, _SMALL_PAD - sum(s for _, s in _SMALL)), F32)], axis=1)
    small = cat(grads)
    rows = jnp.concatenate([gmod, jnp.pad(small, ((0, 0), (0, N_MOD * D - n_small))),
                            jnp.zeros((pad_rows - B - 1, N_MOD * D), F32)], axis=0)
    rows_all = _allgather8(rows, "ag_small", False).reshape(N_DEV, pad_rows, N_MOD * D)
    gmod_all = rows_all[:, :B].reshape(N_DEV * B, N_MOD * D)
    small_parts = rows_all[:, B, :n_small]

    a4, r2 = grads["mix_pending"]
    ffn_a4, ffn_r2 = grads["ffn_pending"]
    G = dict(zip(_SHARDED, _rs_last(list(a4) + list(ffn_a4), list(r2) + list(ffn_r2), "all")))

    gmod_cols = lax.dynamic_slice_in_dim(gmod_all, chip * ada_cols, ada_cols, axis=1)
    G["w_ada"] = _ada_bwd(c_all, gmod_cols, "ada_bwd")
    delta, new_m, new_v = {}, {}, {}
    for n in ("w_ada",) + _SHARDED:
        shp = W[n].shape
        w2 = W[n].reshape(shp[-2], shp[-1])
        d_, m_, v_ = _adamw(w2, G[n], M[n].reshape(w2.shape), V[n].reshape(w2.shape), f"adamw_{n}")
        G[n], delta[n], new_m[n], new_v[n] = [a.reshape(shp) for a in (G[n], d_, m_, v_)]
    gs, ds_, ms_, vs_ = _adamw_rows(cat(W), small_parts, cat(M), cat(V), "adamw_small")
    off = 0
    for n, sz in _SMALL:
        shp = W[n].shape
        G[n], delta[n], new_m[n], new_v[n] = [a[:, off:off + sz].reshape(shp) for a in (gs, ds_, ms_, vs_)]
        off += sz
    G["b_ada"], delta["b_ada"], new_m["b_ada"], new_v["b_ada"] = _adamw_rows(b_ada, gmod_all, m_b_ada, v_b_ada,
                                                                          "adamw_b_ada")
    return (loss, grad_x, *[G[n] for n in names], *[delta[n] for n in names], *[new_m[n] for n in names],
            *[new_v[n] for n in names])
```

```python
import functools
import math

import numpy as np
import jax
import jax.numpy as jnp
from jax import lax
from jax.experimental import pallas as pl
from jax.experimental.pallas import tpu as pltpu

F32 = jnp.float32
BF16 = jnp.bfloat16

D_MODEL = 1024
SEQ = 2048
N_HEADS = 8
HEAD_DIM = 64
D_A = 512
D_B = 512
Q_LORA = 384
KV_LORA = 256
ROPE_DIM = 32
NOPE_DIM = 64
D_FF = 2816
N_MOD = 6
N_BUCKETS = 32
MAX_DISTANCE = 2048
ROPE_THETA = 10000.0
EPS = 1e-6
NEG = -1e30
BLK = 128
DILATIONS = (1, 4, 16)
SPAN = 128
MLA_SCALE = (NOPE_DIM + ROPE_DIM) ** -0.5
DIL_SCALE = HEAD_DIM ** -0.5

ADAM_LR = 0.001
ADAM_B1 = 0.9
ADAM_B2 = 0.999
ADAM_EPS = 1e-08
ADAM_WD = 0.01
ADAM_STEP = 10

N_DEV = 8
N_CHIP = 4
LANES = 128
VMEM_LIMIT = 48 * 1024 * 1024
MM_VMEM_BUDGET = 32 * 1024 * 1024

P_QKV = 3 * D_A
P_REST = KV_LORA + LANES + Q_LORA


def _cparams(sem=None):
    return pltpu.CompilerParams(dimension_semantics=sem, vmem_limit_bytes=VMEM_LIMIT)


def _pick(n, cands):
    for c in cands:
        if n % c == 0:
            return c
    raise ValueError(f"no tile for {n} in {cands}")


def _mm(a, b, mode, out_dtype, name, col_blocks=None, comm=None, halves=False):
    blocked = col_blocks is not None
    if mode == "nn":
        (M, K) = a.shape
        K2, N = (b.shape[1], b.shape[0] * b.shape[2]) if blocked else b.shape
    elif mode == "nt":
        (M, K) = (a.shape[1], 2 * a.shape[2]) if halves else a.shape
        N, K2 = (b.shape[1], b.shape[0] * b.shape[2]) if blocked else b.shape
    else:
        (K, M) = a.shape
        K2, N = (b.shape[1], 2 * b.shape[2]) if halves else b.shape
    assert K == K2, (a.shape, b.shape, mode)
    assert not halves or (blocked and col_blocks == 4 and mode in ("nt", "tn"))
    tn = _pick(N, (1408, 1024, 768, 512, 384, 256, 128))
    tk = _pick(K, (1408, 1152, 1024, 768, 512, 384, 256, 128))
    if blocked and mode == "nt":
        tk = K // col_blocks
    elif blocked:
        tn = N // col_blocks
    nk = K // tk

    def vmem_bytes(tm_):
        tiles = tm_ * tk * a.dtype.itemsize + tk * tn * b.dtype.itemsize + tm_ * tn * jnp.dtype(out_dtype).itemsize
        return 2 * tiles + tm_ * tn * 4

    tm = next(t for t in (1408, 1024, 512, 384, 256, 128) if M % t == 0 and vmem_bytes(t) <= MM_VMEM_BUDGET)
    out_shape = (M, N)
    out_spec = pl.BlockSpec((tm, tn), lambda i, j, k: (i, j))
    if mode == "nn":
        a_spec = pl.BlockSpec((tm, tk), lambda i, j, k: (i, k))
        b_spec = (pl.BlockSpec((None, tk, tn), lambda i, j, k: (j, k, 0)) if blocked
                  else pl.BlockSpec((tk, tn), lambda i, j, k: (k, j)))
        dn = (((1,), (0,)), ((), ()))
    elif mode == "nt":
        a_spec = (pl.BlockSpec((None, tm, tk), lambda i, j, k: (k // 2, i, k % 2)) if halves
                  else pl.BlockSpec((tm, tk), lambda i, j, k: (i, k)))
        b_spec = (pl.BlockSpec((None, tn, tk), lambda i, j, k: (k, j, 0)) if blocked
                  else pl.BlockSpec((tn, tk), lambda i, j, k: (j, k)))
        dn = (((1,), (1,)), ((), ()))
    else:
        a_spec = pl.BlockSpec((tk, tm), lambda i, j, k: (k, i))
        b_spec = (pl.BlockSpec((None, tk, tn), lambda i, j, k: (j // 2, k, j % 2)) if halves
                  else pl.BlockSpec((tk, tn), lambda i, j, k: (k, j)))
        dn = (((0,), (0,)), ((), ()))
        if blocked:
            out_shape = (col_blocks, M, tn)
            out_spec = pl.BlockSpec((None, tm, tn), lambda i, j, k: (j, i, 0))

    def body(a_ref, b_ref, o_ref, acc_ref):
        k = pl.program_id(2)

        @pl.when(k == 0)
        def _():
            acc_ref[...] = jnp.zeros_like(acc_ref)

        acc_ref[...] += lax.dot_general(a_ref[...].astype(BF16), b_ref[...].astype(BF16), dn,
                                        preferred_element_type=F32)

        @pl.when(k == nk - 1)
        def _():
            o_ref[...] = acc_ref[...].astype(o_ref.dtype)

    if comm is not None:
        (out,), got = _host_call(
            body, comm, name=name, out_shape=[jax.ShapeDtypeStruct(out_shape, out_dtype)],
            grid=(M // tm, N // tn, nk), in_specs=[a_spec, b_spec], out_specs=[out_spec],
            scratch_shapes=[pltpu.VMEM((tm, tn), F32)], args=(a, b))
        return out, got
    return pl.pallas_call(
        body, name=name,
        out_shape=jax.ShapeDtypeStruct(out_shape, out_dtype),
        grid=(M // tm, N // tn, nk),
        in_specs=[a_spec, b_spec],
        out_specs=out_spec,
        scratch_shapes=[pltpu.VMEM((tm, tn), F32)],
        compiler_params=_cparams(("parallel", "parallel", "arbitrary")),
    )(a, b)


ROW_TILE = 512


def _adaln_fwd(x, g, sc, sh, name, mix=None, gate=None):
    B, S, D = x.shape
    ts = ROW_TILE
    has_res = mix is not None

    def body(*refs):
        if has_res:
            x_ref, g_ref, sc_ref, sh_ref, mix_ref, gate_ref, h_ref, xr_ref = refs
            xr = x_ref[0] + gate_ref[0] * mix_ref[0]
            xr_ref[0] = xr
        else:
            x_ref, g_ref, sc_ref, sh_ref, h_ref = refs
            xr = x_ref[0]
        r = lax.rsqrt(jnp.mean(xr * xr, axis=-1, keepdims=True) + EPS)
        xn = (xr * r) * g_ref[...]
        h_ref[0] = (xn * (1.0 + sc_ref[0]) + sh_ref[0]).astype(h_ref.dtype)

    tok = pl.BlockSpec((1, ts, D), lambda b, s: (b, s, 0))
    per_b = pl.BlockSpec((1, 1, D), lambda b, s: (b, 0, 0))
    vec = pl.BlockSpec((1, D), lambda b, s: (0, 0))
    in_specs = [tok, vec, per_b, per_b]
    args = [x, g, sc, sh]
    out_shape = [jax.ShapeDtypeStruct((B, S, D), BF16)]
    out_specs = [tok]
    if has_res:
        in_specs += [tok, per_b]
        args += [mix, gate]
        out_shape.append(jax.ShapeDtypeStruct((B, S, D), F32))
        out_specs.append(tok)
    out = pl.pallas_call(
        body, name=name, out_shape=out_shape, grid=(B, S // ts),
        in_specs=in_specs, out_specs=out_specs,
        compiler_params=_cparams(("parallel", "parallel")),
    )(*args)
    return out if has_res else out[0]


def _adaln_bwd(dh, x, g, sc, dres, name, mix=None, gate=None, comm=None):
    B, S, D = x.shape
    ts = ROW_TILE
    has_res = mix is not None

    def body(*refs):
        if has_res:
            (dh_ref, x_ref, g_ref, sc_ref, dres_ref, mix_ref, gate_ref,
             dx_ref, dsh_ref, dsc_ref, dg_ref, dgate_ref, dmix_ref) = refs
        else:
            (dh_ref, x_ref, g_ref, sc_ref, dres_ref, dx_ref, dsh_ref, dsc_ref, dg_ref) = refs
        b, s = pl.program_id(0), pl.program_id(1)
        xv = x_ref[0]
        dhv = dh_ref[0]
        gv = g_ref[...]
        r = lax.rsqrt(jnp.mean(xv * xv, axis=-1, keepdims=True) + EPS)
        n = xv * r
        xn = n * gv
        dxn = dhv * (1.0 + sc_ref[0])
        dn = dxn * gv
        dx = r * (dn - n * jnp.mean(dn * n, axis=-1, keepdims=True)) + dres_ref[0]
        dx_ref[0] = dx

        @pl.when(s == 0)
        def _():
            dsh_ref[...] = jnp.zeros_like(dsh_ref)
            dsc_ref[...] = jnp.zeros_like(dsc_ref)
            if has_res:
                dgate_ref[...] = jnp.zeros_like(dgate_ref)

        @pl.when((s == 0) & (b == 0))
        def _():
            dg_ref[...] = jnp.zeros_like(dg_ref)

        dsh_ref[0] += jnp.sum(dhv, axis=0, keepdims=True)
        dsc_ref[0] += jnp.sum(dhv * xn, axis=0, keepdims=True)
        dg_ref[...] += jnp.sum(dxn * n, axis=0, keepdims=True)
        if has_res:
            dgate_ref[0] += jnp.sum(dx * mix_ref[0], axis=0, keepdims=True)
            dmix_ref[0] = (dx * gate_ref[0]).astype(dmix_ref.dtype)

    tok = pl.BlockSpec((1, ts, D), lambda b, s: (b, s, 0))
    per_b = pl.BlockSpec((1, 1, D), lambda b, s: (b, 0, 0))
    vec = pl.BlockSpec((1, D), lambda b, s: (0, 0))
    in_specs = [tok, tok, vec, per_b, tok]
    args = [dh, x, g, sc, dres]
    out_shape = [jax.ShapeDtypeStruct((B, S, D), F32), jax.ShapeDtypeStruct((B, 1, D), F32),
                 jax.ShapeDtypeStruct((B, 1, D), F32), jax.ShapeDtypeStruct((1, D), F32)]
    out_specs = [tok, per_b, per_b, vec]
    if has_res:
        in_specs += [tok, per_b]
        args += [mix, gate]
        out_shape += [jax.ShapeDtypeStruct((B, 1, D), F32), jax.ShapeDtypeStruct((B, S, D), BF16)]
        out_specs += [per_b, tok]
    res, got = _host_call(body, comm, name=name, out_shape=out_shape, grid=(B, S // ts), in_specs=in_specs,
                          out_specs=out_specs, scratch_shapes=[], args=args)
    return (list(res) + [got]) if comm is not None else res


def _rms_fwd_pair(xa, xb, ga, gb, name):
    T, na = xa.shape
    nb = xb.shape[1]
    tr = 512

    def body(xa_ref, xb_ref, ga_ref, gb_ref, y_ref):
        for x_ref, g_ref, lo, n in ((xa_ref, ga_ref, 0, na), (xb_ref, gb_ref, na, nb)):
            xv = x_ref[...]
            r = lax.rsqrt(jnp.mean(xv * xv, axis=-1, keepdims=True) + EPS)
            y_ref[:, lo:lo + n] = ((xv * r) * g_ref[...]).astype(y_ref.dtype)

    row = lambda n: pl.BlockSpec((tr, n), lambda i: (i, 0))
    vec = lambda n: pl.BlockSpec((1, n), lambda i: (0, 0))
    return pl.pallas_call(
        body, name=name, out_shape=jax.ShapeDtypeStruct((T, na + nb), BF16), grid=(T // tr,),
        in_specs=[row(na), row(nb), vec(na), vec(nb)], out_specs=row(na + nb),
        compiler_params=_cparams(("parallel",)),
    )(xa, xb, ga, gb)


def _rms_bwd(dy, dy_blk, x, x_blk, n, g, name, out_dtype=BF16):
    T = x.shape[0]
    tr = 512

    def body(dy_ref, x_ref, g_ref, dx_ref, dg_ref):
        xv = x_ref[...]
        dyv = dy_ref[...].astype(F32)
        r = lax.rsqrt(jnp.mean(xv * xv, axis=-1, keepdims=True) + EPS)
        nrm = xv * r
        dn = dyv * g_ref[...]
        dx_ref[...] = (r * (dn - nrm * jnp.mean(dn * nrm, axis=-1, keepdims=True))).astype(dx_ref.dtype)

        @pl.when(pl.program_id(0) == 0)
        def _():
            dg_ref[...] = jnp.zeros_like(dg_ref)

        dg_ref[...] += jnp.sum(dyv * nrm, axis=0, keepdims=True)

    return pl.pallas_call(
        body, name=name,
        out_shape=[jax.ShapeDtypeStruct((T, n), out_dtype), jax.ShapeDtypeStruct((1, n), F32)],
        grid=(T // tr,),
        in_specs=[pl.BlockSpec((tr, n), lambda i: (i, dy_blk)), pl.BlockSpec((tr, n), lambda i: (i, x_blk)),
                  pl.BlockSpec((1, n), lambda i: (0, 0))],
        out_specs=[pl.BlockSpec((tr, n), lambda i: (i, 0)), pl.BlockSpec((1, n), lambda i: (0, 0))],
        compiler_params=_cparams(("arbitrary",)),
    )(dy, x, g)


def _rms_bwd_views(dy, dy_blk, x, g, name):
    B, S, n = x.shape
    tiles = S // VIEW_TILE

    def body(dy_ref, x_ref, g_ref, d1_ref, d4_ref, d16_ref, dg_ref, dx_s):
        xv = x_ref[0]
        dyv = dy_ref[...]
        r = lax.rsqrt(jnp.mean(xv * xv, axis=-1, keepdims=True) + EPS)
        nrm = xv * r
        dn = dyv * g_ref[...]
        dx = r * (dn - nrm * jnp.mean(dn * nrm, axis=-1, keepdims=True))
        d1_ref[0] = dx.astype(d1_ref.dtype)
        _put_tile(dx_s, dx)
        _tile_to_view(dx_s, d4_ref, DILATIONS[1], n)
        _tile_to_view(dx_s, d16_ref, DILATIONS[2], n)

        @pl.when((pl.program_id(0) == 0) & (pl.program_id(1) == 0))
        def _():
            dg_ref[...] = jnp.zeros_like(dg_ref)

        dg_ref[...] += jnp.sum(dyv * nrm, axis=0, keepdims=True)

    res = pl.pallas_call(
        body, name=name,
        out_shape=[_view_shape(B, S, d, n, BF16) for d in DILATIONS] + [jax.ShapeDtypeStruct((1, n), F32)],
        grid=(B, tiles),
        in_specs=[pl.BlockSpec((VIEW_TILE, n), lambda b, t: (b * tiles + t, dy_blk)), _view_spec(1, n),
                  pl.BlockSpec((1, n), lambda b, t: (0, 0))],
        out_specs=[_view_spec(d, n) for d in DILATIONS] + [pl.BlockSpec((1, n), lambda b, t: (0, 0))],
        scratch_shapes=[_tile_scratch(n)],
        compiler_params=_cparams(("arbitrary", "arbitrary")),
    )(dy, x, g)
    return res[:len(DILATIONS)], res[len(DILATIONS)]


FFN_TILE = 1408


def _ffn_in_fwd(h, w4, name):
    T, D = h.shape
    tm, tc = 512, FFN_TILE
    nc = D_FF // tc

    def body(h_ref, wg_ref, wu_ref, gu_ref, act_ref):
        hv = h_ref[...]
        g = jnp.dot(hv, wg_ref[...], preferred_element_type=F32)
        u = jnp.dot(hv, wu_ref[...], preferred_element_type=F32)
        gu_ref[0] = g.astype(gu_ref.dtype)
        gu_ref[1] = u.astype(gu_ref.dtype)
        act_ref[...] = (g * jax.nn.sigmoid(g) * u).astype(act_ref.dtype)

    return pl.pallas_call(
        body, name=name,
        out_shape=[jax.ShapeDtypeStruct((2, T, D_FF), BF16), jax.ShapeDtypeStruct((T, D_FF), BF16)],
        grid=(nc, T // tm),
        in_specs=[pl.BlockSpec((tm, D), lambda j, i: (i, 0)),
                  pl.BlockSpec((None, D, tc), lambda j, i: (j, 0, 0)),
                  pl.BlockSpec((None, D, tc), lambda j, i: (j + nc, 0, 0))],
        out_specs=[pl.BlockSpec((2, tm, tc), lambda j, i: (0, i, j)), pl.BlockSpec((tm, tc), lambda j, i: (i, j))],
        compiler_params=_cparams(("parallel", "parallel")),
    )(h, w4, w4)


def _ffn_out_bwd(df, w_out, gu, name):
    T, D = df.shape
    tm, tc = 512, FFN_TILE

    def body(df_ref, w_ref, gu_ref, dgu_ref):
        da = _dot_nt(df_ref[...], w_ref[...])
        g, u = gu_ref[0].astype(F32), gu_ref[1].astype(F32)
        sg = jax.nn.sigmoid(g)
        dgu_ref[0] = (da * u * (sg * (1.0 + g * (1.0 - sg)))).astype(dgu_ref.dtype)
        dgu_ref[1] = (da * (g * sg)).astype(dgu_ref.dtype)

    halves = pl.BlockSpec((2, tm, tc), lambda j, i: (0, i, j))
    return pl.pallas_call(
        body, name=name, out_shape=jax.ShapeDtypeStruct((2, T, D_FF), BF16), grid=(D_FF // tc, T // tm),
        in_specs=[pl.BlockSpec((tm, D), lambda j, i: (i, 0)), pl.BlockSpec((tc, D), lambda j, i: (j, 0)), halves],
        out_specs=halves,
        compiler_params=_cparams(("parallel", "parallel")),
    )(df, w_out, gu)


def _final_loss(x1, f, g2, gf, target, name):
    B, S, D = x1.shape
    ts = ROW_TILE

    def body(x1_ref, f_ref, g2_ref, gf_ref, t_ref, dx_ref, df_ref, dg2_ref, dgf_ref, loss_ref):
        b, s = pl.program_id(0), pl.program_id(1)
        fv = f_ref[0]
        g2v = g2_ref[0]
        gfv = gf_ref[...]
        x2 = x1_ref[0] + g2v * fv
        r = lax.rsqrt(jnp.mean(x2 * x2, axis=-1, keepdims=True) + EPS)
        n = x2 * r
        e = n * gfv - t_ref[0]
        dy = e * (1.0 / D)
        dn = dy * gfv
        dx = r * (dn - n * jnp.mean(dn * n, axis=-1, keepdims=True))
        dx_ref[0] = dx
        df_ref[0] = (dx * g2v).astype(df_ref.dtype)

        @pl.when(s == 0)
        def _():
            dg2_ref[...] = jnp.zeros_like(dg2_ref)

        @pl.when((s == 0) & (b == 0))
        def _():
            dgf_ref[...] = jnp.zeros_like(dgf_ref)
            loss_ref[...] = jnp.zeros_like(loss_ref)

        dg2_ref[0] += jnp.sum(dx * fv, axis=0, keepdims=True)
        dgf_ref[...] += jnp.sum(dy * n, axis=0, keepdims=True)
        loss_ref[...] += 0.5 * jnp.sum(jnp.mean(e * e, axis=-1, keepdims=True), axis=0, keepdims=True)

    tok = pl.BlockSpec((1, ts, D), lambda b, s: (b, s, 0))
    per_b = pl.BlockSpec((1, 1, D), lambda b, s: (b, 0, 0))
    vec = pl.BlockSpec((1, D), lambda b, s: (0, 0))
    return pl.pallas_call(
        body, name=name,
        out_shape=[jax.ShapeDtypeStruct((B, S, D), F32), jax.ShapeDtypeStruct((B, S, D), BF16),
                   jax.ShapeDtypeStruct((B, 1, D), F32), jax.ShapeDtypeStruct((1, D), F32),
                   jax.ShapeDtypeStruct((1, LANES), F32)],
        grid=(B, S // ts),
        in_specs=[tok, tok, per_b, vec, tok],
        out_specs=[tok, tok, per_b, vec, pl.BlockSpec((1, LANES), lambda b, s: (0, 0))],
        compiler_params=_cparams(("arbitrary", "arbitrary")),
    )(x1, f, g2, gf, target)


def _rope_tables():
    half = ROPE_DIM // 2
    inv = ROPE_THETA ** (-jnp.arange(half, dtype=F32) / half)
    ang = jnp.arange(SEQ, dtype=F32)[:, None] * inv[None, :]
    cos, sin = jnp.cos(ang), jnp.sin(ang)
    one = jnp.ones((SEQ, NOPE_DIM), F32)
    zero = jnp.zeros((SEQ, NOPE_DIM), F32)
    cs = jnp.concatenate([one, cos, cos, one[:, :LANES - NOPE_DIM - ROPE_DIM]], axis=1)
    sn = jnp.concatenate([zero, -sin, sin, zero[:, :LANES - NOPE_DIM - ROPE_DIM]], axis=1)
    return cs, sn


def _rope_group(t, cs, sn):
    half = ROPE_DIM // 2
    lane = lax.broadcasted_iota(jnp.int32, t.shape, 1)
    partner = jnp.where(lane < NOPE_DIM + half, pltpu.roll(t, LANES - half, 1), pltpu.roll(t, half, 1))
    return t * cs + partner * sn


def _mla_proj(cqn, ckvn, rest, w_uq, w_kv, cs, sn, name):
    B, S, _ = rest.shape
    ts, tk = ROW_TILE, MLA_TK
    tiles = S // ts
    G = N_HEADS
    kw = G * LANES
    npair = N_HEADS // 2

    def body(cq_ref, ckv_ref, r_ref, wq_ref, wkv_ref, cs_ref, sn_ref, q_ref, k_ref, v_ref, vt_ref):
        csv, snv = cs_ref[...], sn_ref[...]
        q_raw = jnp.dot(cq_ref[...], wq_ref[...], preferred_element_type=F32)
        kv = jnp.dot(ckv_ref[...], wkv_ref[...], preferred_element_type=F32)
        ra = _rope_group(r_ref[0], csv, snv)
        for gi in range(G):
            sl = slice(gi * LANES, (gi + 1) * LANES)
            q_ref[0, :, sl] = _rope_group(q_raw[:, sl], csv, snv).astype(q_ref.dtype)
            k_ref[0, :, sl] = (kv[:, sl] + ra).astype(k_ref.dtype)
        v = kv[:, kw:]
        v_ref[0] = v.astype(v_ref.dtype)
        for p in range(npair):
            for s in range(ts // tk):
                vt_ref[0, p, s] = jnp.transpose(v[s * tk:(s + 1) * tk, p * LANES:(p + 1) * LANES]).astype(vt_ref.dtype)

    rows = lambda n: pl.BlockSpec((ts, n), lambda b, t: (b * tiles + t, 0))
    full = lambda a: pl.BlockSpec(a.shape, lambda b, t: (0, 0))
    tab = pl.BlockSpec((ts, LANES), lambda b, t: (t, 0))
    tok = lambda n: pl.BlockSpec((1, ts, n), lambda b, t: (b, t, 0))
    return pl.pallas_call(
        body, name=name,
        out_shape=[jax.ShapeDtypeStruct((B, S, kw), BF16), jax.ShapeDtypeStruct((B, S, kw), BF16),
                   jax.ShapeDtypeStruct((B, S, D_B), BF16),
                   jax.ShapeDtypeStruct((B, npair, S // tk, LANES, tk), BF16)],
        grid=(B, tiles),
        in_specs=[rows(Q_LORA), rows(KV_LORA), pl.BlockSpec((1, ts, LANES), lambda b, t: (b, t, KV_LORA // LANES)),
                  full(w_uq), full(w_kv), tab, tab],
        out_specs=[tok(kw), tok(kw), tok(D_B),
                   pl.BlockSpec((1, npair, ts // tk, LANES, tk), lambda b, t: (b, 0, t, 0, 0))],
        compiler_params=_cparams(("parallel", "parallel")),
    )(cqn, ckvn, rest, w_uq, w_kv, cs, sn)


def _qrope_bwd(dq_t, cs, sn_neg, name):
    B, npair, nq, _, tq = dq_t.shape

    def body(d_ref, cs_ref, sn_ref, o_ref):
        for p in range(npair):
            tile = jnp.transpose(d_ref[0, p, 0])
            for hh in range(2):
                lo = (2 * p + hh) * LANES
                o_ref[0, :, lo:lo + LANES] = _rope_group(tile[:, hh * LANES:(hh + 1) * LANES], cs_ref[...],
                                                         sn_ref[...]).astype(o_ref.dtype)

    tab = pl.BlockSpec((tq, LANES), lambda b, i: (i, 0))
    return pl.pallas_call(
        body, name=name, out_shape=jax.ShapeDtypeStruct((B, nq * tq, N_HEADS * LANES), BF16), grid=(B, nq),
        in_specs=[pl.BlockSpec((1, npair, 1, 2 * LANES, tq), lambda b, i: (b, 0, i, 0, 0)), tab, tab],
        out_specs=pl.BlockSpec((1, tq, N_HEADS * LANES), lambda b, i: (b, i, 0)),
        compiler_params=_cparams(("parallel", "parallel")),
    )(dq_t, cs, sn_neg)


def _krope_bwd(dkc, cs, sn_neg, name):
    B, S, W = dkc.shape
    G = W // LANES
    ts = ROW_TILE

    def body(d_ref, cs_ref, sn_ref, o_ref):
        acc = d_ref[0, :, 0:LANES]
        for gi in range(1, G):
            acc = acc + d_ref[0, :, gi * LANES:(gi + 1) * LANES]
        lane = lax.broadcasted_iota(jnp.int32, acc.shape, 1)
        rot = (lane >= NOPE_DIM) & (lane < NOPE_DIM + ROPE_DIM)
        acc = jnp.where(rot, acc, 0.0)
        o_ref[0] = _rope_group(acc, cs_ref[...], sn_ref[...]).astype(o_ref.dtype)

    tab = pl.BlockSpec((ts, LANES), lambda b, s: (s, 0))
    return pl.pallas_call(
        body, name=name, out_shape=jax.ShapeDtypeStruct((B, S, LANES), BF16), grid=(B, S // ts),
        in_specs=[pl.BlockSpec((1, ts, W), lambda b, s: (b, s, 0)), tab, tab],
        out_specs=pl.BlockSpec((1, ts, LANES), lambda b, s: (b, s, 0)),
        compiler_params=_cparams(("parallel", "parallel")),
    )(dkc, cs, sn_neg)


def _t5_bucket(dist):
    max_exact = N_BUCKETS // 2
    d = np.maximum(dist, 1).astype(np.float64)
    large = max_exact + (np.log(d / max_exact) / np.log(MAX_DISTANCE / max_exact)
                         * (N_BUCKETS - max_exact)).astype(np.int64)
    large = np.minimum(large, N_BUCKETS - 1)
    return np.where(dist < max_exact, dist, large).astype(np.int32)


def _band_buckets(dilation):
    a = np.arange(BLK)[None, :]
    bk = np.arange(2 * BLK)[:, None]
    steps = BLK + a - bk
    return _t5_bucket(np.clip(steps, 0, SPAN) * dilation)


def _head_mask(shape, hh):
    lane = lax.broadcasted_iota(jnp.int32, shape, 1)
    return (lane >= hh * HEAD_DIM) & (lane < (hh + 1) * HEAD_DIM)


def _dot_nt(a, b):
    return lax.dot_general(a, b, (((1,), (1,)), ((), ())), preferred_element_type=F32)


def _dot_tn(a, b):
    return lax.dot_general(a, b, (((0,), (0,)), ((), ())), preferred_element_type=F32)


def _dot_nn(a, b):
    return lax.dot_general(a, b, (((1,), (0,)), ((), ())), preferred_element_type=F32)


def _dil_fwd(qkv, bias, branch, dilation, name, comm=None):
    B, n, _ = qkv.shape
    d = dilation
    nb = n // BLK
    qkv_v = qkv
    npair = N_HEADS // 2

    def body(cur_ref, prev_ref, bias_ref, o_ref, lse_ref, s_scr, e_scr):
        first = jnp.where(pl.program_id(1) == 0, 1, 0)
        units = [(b, h) for b in range(B) for h in range(N_HEADS)]
        for b in range(B):
            for p in range(npair):
                q = cur_ref[b, :, p * LANES:(p + 1) * LANES] * DIL_SCALE
                kc = cur_ref[b, :, D_A + p * LANES:D_A + (p + 1) * LANES]
                kp = prev_ref[b, :, D_A + p * LANES:D_A + (p + 1) * LANES]
                for hh in range(2):
                    u = b * N_HEADS + 2 * p + hh
                    qm = jnp.where(_head_mask((BLK, LANES), hh), q, jnp.zeros_like(q))
                    s_scr[u, 0:BLK, :] = _dot_nt(kp, qm)
                    s_scr[u, BLK:2 * BLK, :] = _dot_nt(kc, qm)
        ms = []
        for u, (b, h) in enumerate(units):
            s_p = s_scr[u, 0:BLK, :] + bias_ref[first, h, 0:BLK, :]
            s_c = s_scr[u, BLK:2 * BLK, :] + bias_ref[first, h, BLK:2 * BLK, :]
            m = jnp.maximum(jnp.max(s_p, axis=0, keepdims=True), jnp.max(s_c, axis=0, keepdims=True))
            e_scr[u, 0:BLK, :] = jnp.exp(s_p - m).astype(BF16)
            e_scr[u, BLK:2 * BLK, :] = jnp.exp(s_c - m).astype(BF16)
            ms.append(m)
        rows0 = _row_mask((LANES, BLK), 0)
        for b in range(B):
            for p in range(npair):
                sl = slice(p * LANES, (p + 1) * LANES)
                vsl = slice(2 * D_A + p * LANES, 2 * D_A + (p + 1) * LANES)
                vct = jnp.transpose(cur_ref[b, :, vsl].astype(F32)).astype(BF16)
                vpt = jnp.transpose(prev_ref[b, :, vsl].astype(F32)).astype(BF16)
                acc = []
                for hh in range(2):
                    u = b * N_HEADS + 2 * p + hh
                    mine = _row_mask((LANES, BLK), hh)
                    one = jnp.ones_like(vct)
                    acc.append(_dot_nn(jnp.where(mine, vpt, one), e_scr[u, 0:BLK, :])
                               + _dot_nn(jnp.where(mine, vct, one), e_scr[u, BLK:2 * BLK, :]))
                l0 = acc[0][HEAD_DIM:HEAD_DIM + 1, :]
                l1 = acc[1][0:1, :]
                u0 = b * N_HEADS + 2 * p
                o_t = jnp.where(rows0, acc[0] / l0, acc[1] / l1)
                lse_t = jnp.where(rows0, ms[u0] + jnp.log(l0), ms[u0 + 1] + jnp.log(l1))
                o_ref[b, :, sl] = jnp.transpose(o_t)
                lse_ref[b, :, sl] = jnp.transpose(lse_t)

    cur = pl.BlockSpec((B, BLK, P_QKV), lambda r, i: (0, i, r))
    prev = pl.BlockSpec((B, BLK, P_QKV), lambda r, i: (0, jnp.maximum(i - 1, 0), r))
    out = pl.BlockSpec((B, BLK, D_A), lambda r, i: (0, i, r))
    return _host_call(
        body, comm, name=name,
        out_shape=[jax.ShapeDtypeStruct((B, n, d * D_A), F32)] * 2,
        grid=(d, nb),
        in_specs=[cur, prev,
                  pl.BlockSpec((None, 2, N_HEADS, 2 * BLK, BLK), lambda r, i: (branch, 0, 0, 0, 0))],
        out_specs=[out, out],
        scratch_shapes=[pltpu.VMEM((B * N_HEADS, 2 * BLK, BLK), F32),
                        pltpu.VMEM((B * N_HEADS, 2 * BLK, BLK), BF16)],
        args=(qkv_v, qkv_v, bias))


VIEW_TILE = 512


def _view_spec(d, w):
    return pl.BlockSpec((1, VIEW_TILE // d, d * w), lambda b, t: (b, t, 0))


def _view_shape(B, S, d, w, dtype):
    return jax.ShapeDtypeStruct((B, S // d, d * w), dtype)


def _tile_scratch(w):
    return pltpu.VMEM((w // LANES, VIEW_TILE, LANES), F32)


def _put_tile(tile_ref, val):
    for c in range(tile_ref.shape[0]):
        tile_ref[c] = val[:, c * LANES:(c + 1) * LANES]


def _get_tile(tile_ref):
    return jnp.concatenate([tile_ref[c] for c in range(tile_ref.shape[0])], axis=1)


def _tile_to_view(tile_ref, view_ref, d, w):
    for c in range(w // LANES):
        for r in range(d):
            lo = r * w + c * LANES
            rows = tile_ref.at[c][pl.ds(r, VIEW_TILE // d, stride=d), :]
            view_ref[0, :, lo:lo + LANES] = rows.astype(view_ref.dtype)


def _view_to_tile(view_ref, tile_ref, d, w):
    for c in range(w // LANES):
        for r in range(d):
            lo = r * w + c * LANES
            tile_ref.at[c][pl.ds(r, VIEW_TILE // d, stride=d), :] = view_ref[0, :, lo:lo + LANES].astype(F32)


def _in_proj(h, w_qkv, w_rest, g_cq, g_ckv, name):
    B, S, D = h.shape
    N = w_qkv.shape[1]
    cq_lo = KV_LORA + LANES

    def rms(xv, g_ref):
        return ((xv * lax.rsqrt(jnp.mean(xv * xv, axis=-1, keepdims=True) + EPS)) * g_ref[...]).astype(BF16)

    def body(h_ref, wq_ref, wr_ref, gcq_ref, gckv_ref, o1_ref, o4_ref, o16_ref, rest_ref, cqn_ref, ckvn_ref,
             acc_ref):
        hv = h_ref[0]
        acc = jnp.dot(hv, wq_ref[...], preferred_element_type=F32)
        o1_ref[0] = acc.astype(o1_ref.dtype)
        _put_tile(acc_ref, acc)
        _tile_to_view(acc_ref, o4_ref, DILATIONS[1], N)
        _tile_to_view(acc_ref, o16_ref, DILATIONS[2], N)
        rest = jnp.dot(hv, wr_ref[...], preferred_element_type=F32)
        rest_ref[0] = rest
        ckvn_ref[...] = rms(rest[:, :KV_LORA], gckv_ref)
        cqn_ref[...] = rms(rest[:, cq_lo:], gcq_ref)

    tiles = S // VIEW_TILE
    full = lambda a: pl.BlockSpec(a.shape, lambda b, t: (0, 0))
    rows = lambda n: pl.BlockSpec((VIEW_TILE, n), lambda b, t: (b * tiles + t, 0))
    res = pl.pallas_call(
        body, name=name,
        out_shape=[_view_shape(B, S, d, N, BF16) for d in DILATIONS]
        + [jax.ShapeDtypeStruct((B, S, P_REST), F32), jax.ShapeDtypeStruct((B * S, Q_LORA), BF16),
           jax.ShapeDtypeStruct((B * S, KV_LORA), BF16)],
        grid=(B, tiles),
        in_specs=[pl.BlockSpec((1, VIEW_TILE, D), lambda b, t: (b, t, 0)), full(w_qkv), full(w_rest), full(g_cq),
                  full(g_ckv)],
        out_specs=[_view_spec(d, N) for d in DILATIONS] + [_view_spec(1, P_REST), rows(Q_LORA), rows(KV_LORA)],
        scratch_shapes=[_tile_scratch(N)],
        compiler_params=_cparams(("parallel", "parallel")),
    )(h, w_qkv, w_rest, g_cq, g_ckv)
    return res[:len(DILATIONS)], res[len(DILATIONS)], res[len(DILATIONS) + 1], res[len(DILATIONS) + 2]


def _dil_merge(os_, lses, name):
    B, S, W = os_[0].shape
    nd = len(DILATIONS)

    def body(*refs):
        o_refs, l_refs = refs[:nd], refs[nd:2 * nd]
        out_refs, L_refs = refs[2 * nd:3 * nd], refs[3 * nd:4 * nd]
        scr = refs[4 * nd:]
        o_tok, l_tok = [o_refs[0][0]], [l_refs[0][0]]
        for i, d in enumerate(DILATIONS[1:]):
            _view_to_tile(o_refs[i + 1], scr[2 * i], d, W)
            _view_to_tile(l_refs[i + 1], scr[2 * i + 1], d, W)
            o_tok.append(_get_tile(scr[2 * i]))
            l_tok.append(_get_tile(scr[2 * i + 1]))
        a0, a1, a2 = l_tok
        m = jnp.maximum(jnp.maximum(a0, a1), a2)
        e0, e1, e2 = jnp.exp(a0 - m), jnp.exp(a1 - m), jnp.exp(a2 - m)
        ssum = e0 + e1 + e2
        out = (e0 * o_tok[0] + e1 * o_tok[1] + e2 * o_tok[2]) / ssum
        lse = m + jnp.log(ssum)
        out_refs[0][0] = out
        L_refs[0][0] = lse
        res_o, res_l = scr[2 * (nd - 1)], scr[2 * (nd - 1) + 1]
        _put_tile(res_o, out)
        _put_tile(res_l, lse)
        for i, d in enumerate(DILATIONS[1:]):
            _tile_to_view(res_o, out_refs[i + 1], d, W)
            _tile_to_view(res_l, L_refs[i + 1], d, W)

    specs = [_view_spec(d, W) for d in DILATIONS]
    shapes = [_view_shape(B, S * DILATIONS[0], d, W, F32) for d in DILATIONS]
    res = pl.pallas_call(
        body, name=name, out_shape=shapes * 2, grid=(B, S // VIEW_TILE),
        in_specs=specs * 2, out_specs=specs * 2,
        scratch_shapes=[_tile_scratch(W)] * (2 * nd),
        compiler_params=_cparams(("parallel", "parallel")),
    )(*os_, *lses)
    return res[:nd], res[nd:]


def _dil_bwd(qkv, do, out_a, L, bias, branch, dilation, name, comm=None):
    B, n, _ = qkv.shape
    d = dilation
    nb = n // BLK
    qkv_v, do_v, oa_v, L_v = qkv, do, out_a, L
    npair = N_HEADS // 2
    multi = nb > 1

    tiles = ("P", "C", "N") if multi else ("C",)
    n_t = len(tiles)

    def body(*refs):
        if multi:
            (cur_ref, prev_ref, next_ref, do_ref, don_ref, oa_ref, oan_ref, L_ref, Ln_ref, bias_ref,
             dqkv_ref, dbias_ref, s_scr, dp_scr, p_scr, ds_scr) = refs
        else:
            cur_ref, do_ref, oa_ref, L_ref, bias_ref, dqkv_ref, dbias_ref, s_scr, dp_scr, p_scr, ds_scr = refs
        r, i = pl.program_id(0), pl.program_id(1)

        @pl.when((r == 0) & (i == 0))
        def _():
            dbias_ref[...] = jnp.zeros_like(dbias_ref)

        first = jnp.where(i == 0, 1, 0)
        variant = {"P": first, "C": first, "N": 0}
        band = {"P": slice(0, BLK), "C": slice(BLK, 2 * BLK), "N": slice(0, BLK)}
        psl = lambda p: slice(p * LANES, (p + 1) * LANES)
        ksl = lambda p: slice(D_A + p * LANES, D_A + (p + 1) * LANES)
        vsl = lambda p: slice(2 * D_A + p * LANES, 2 * D_A + (p + 1) * LANES)

        def operands(b, p, hh):
            hm = _head_mask((BLK, LANES), hh)
            mask = lambda x: jnp.where(hm, x, jnp.zeros_like(x))
            qm, dom = mask(cur_ref[b, :, psl(p)] * DIL_SCALE), mask(do_ref[b, :, psl(p)])
            ops = {"C": (cur_ref[b, :, ksl(p)], cur_ref[b, :, vsl(p)], qm, dom)}
            if multi:
                ops["P"] = (prev_ref[b, :, ksl(p)], prev_ref[b, :, vsl(p)], qm, dom)
                ops["N"] = (cur_ref[b, :, ksl(p)], cur_ref[b, :, vsl(p)],
                            mask(next_ref[b, :, psl(p)] * DIL_SCALE), mask(don_ref[b, :, psl(p)]))
            return ops

        pairs = [(b, p) for b in range(B) for p in range(npair)]
        for b, p in pairs:
            for hh in range(2):
                u = b * N_HEADS + 2 * p + hh
                ops = operands(b, p, hh)
                for t, name_t in enumerate(tiles):
                    k_t, v_t, q_t, do_t = ops[name_t]
                    s_scr[u, t] = _dot_nt(k_t, q_t)
                    dp_scr[u, t] = _dot_nt(v_t, do_t)

        def rows(L_r, do_r, oa_r, b, p):
            lt = jnp.transpose(L_r[b, :, psl(p)])
            dt = jnp.transpose(do_r[b, :, psl(p)].astype(F32) * oa_r[b, :, psl(p)])
            return ([lt[0:1, :], lt[HEAD_DIM:HEAD_DIM + 1, :]],
                    [jnp.sum(dt[:HEAD_DIM], axis=0, keepdims=True), jnp.sum(dt[HEAD_DIM:], axis=0, keepdims=True)])

        for b, p in pairs:
            lse_c, delta_c = rows(L_ref, do_ref, oa_ref, b, p)
            if multi:
                lse_n, delta_n = rows(Ln_ref, don_ref, oan_ref, b, p)
            for hh in range(2):
                h = 2 * p + hh
                u = b * N_HEADS + h
                for t, name_t in enumerate(tiles):
                    lse, delta = (lse_n[hh], delta_n[hh]) if name_t == "N" else (lse_c[hh], delta_c[hh])
                    pr = jnp.exp(s_scr[u, t] + bias_ref[variant[name_t], h, band[name_t], :] - lse)
                    if name_t == "N":
                        pr = jnp.where(i < nb - 1, pr, 0.0)
                    ds = pr * (dp_scr[u, t] - delta)
                    p_scr[u, t] = pr.astype(BF16)
                    ds_scr[u, t] = ds.astype(BF16)
                    if name_t != "N":
                        dbias_ref[h, band[name_t], :] += ds

        for b, p in pairs:
            dqt = jnp.zeros((LANES, BLK), F32)
            dk = jnp.zeros((BLK, LANES), F32)
            dv = jnp.zeros((BLK, LANES), F32)
            kct = jnp.transpose(cur_ref[b, :, ksl(p)].astype(F32)).astype(BF16)
            if multi:
                kpt = jnp.transpose(prev_ref[b, :, ksl(p)].astype(F32)).astype(BF16)
            for hh in range(2):
                u = b * N_HEADS + 2 * p + hh
                ops = operands(b, p, hh)
                mine = _row_mask((LANES, BLK), hh)
                for t, name_t in enumerate(tiles):
                    _, _, q_t, do_t = ops[name_t]
                    if name_t != "P":
                        dv = dv + _dot_nn(p_scr[u, t], do_t)
                        dk = dk + _dot_nn(ds_scr[u, t], q_t)
                    if name_t != "N":
                        kt = kpt if name_t == "P" else kct
                        dqt = dqt + _dot_nn(jnp.where(mine, kt, jnp.zeros_like(kt)), ds_scr[u, t])
            dqkv_ref[b, :, psl(p)] = jnp.transpose(dqt) * DIL_SCALE
            dqkv_ref[b, :, ksl(p)] = dk
            dqkv_ref[b, :, vsl(p)] = dv

    def at(off):
        return lambda r, i: (0, jnp.clip(i + off, 0, nb - 1), r)

    qkv_spec = lambda off: pl.BlockSpec((B, BLK, P_QKV), at(off))
    da_spec = lambda off: pl.BlockSpec((B, BLK, D_A), at(off))
    bias_spec = pl.BlockSpec((None, 2, N_HEADS, 2 * BLK, BLK), lambda r, i: (branch, 0, 0, 0, 0))
    dbias_spec = pl.BlockSpec((N_HEADS, 2 * BLK, BLK), lambda r, i: (0, 0, 0))
    if multi:
        in_specs = [qkv_spec(0), qkv_spec(-1), qkv_spec(1), da_spec(0), da_spec(1), da_spec(0), da_spec(1),
                    da_spec(0), da_spec(1), bias_spec]
        args = [qkv_v, qkv_v, qkv_v, do_v, do_v, oa_v, oa_v, L_v, L_v, bias]
    else:
        in_specs = [qkv_spec(0), da_spec(0), da_spec(0), da_spec(0), bias_spec]
        args = [qkv_v, do_v, oa_v, L_v, bias]
    return _host_call(
        body, comm, name=name,
        out_shape=[jax.ShapeDtypeStruct((B, n, d * P_QKV), F32),
                   jax.ShapeDtypeStruct((N_HEADS, 2 * BLK, BLK), F32)],
        grid=(d, nb),
        in_specs=in_specs,
        out_specs=[qkv_spec(0), dbias_spec],
        scratch_shapes=[pltpu.VMEM((B * N_HEADS, n_t, BLK, BLK), F32), pltpu.VMEM((B * N_HEADS, n_t, BLK, BLK), F32),
                        pltpu.VMEM((B * N_HEADS, n_t, BLK, BLK), BF16),
                        pltpu.VMEM((B * N_HEADS, n_t, BLK, BLK), BF16)],
        args=args)


def _sum_views_bf16(parts, name):
    B, S, W = parts[0].shape

    def body(a_ref, b_ref, c_ref, o_ref, sb, sc):
        _view_to_tile(b_ref, sb, DILATIONS[1], W)
        _view_to_tile(c_ref, sc, DILATIONS[2], W)
        o_ref[0] = (a_ref[0] + _get_tile(sb) + _get_tile(sc)).astype(o_ref.dtype)

    return pl.pallas_call(
        body, name=name, out_shape=jax.ShapeDtypeStruct((B, S, W), BF16), grid=(B, S // VIEW_TILE),
        in_specs=[_view_spec(d, W) for d in DILATIONS], out_specs=_view_spec(1, W),
        scratch_shapes=[_tile_scratch(W)] * 2,
        compiler_params=_cparams(("parallel", "parallel")),
    )(*parts)


def _bias_tables(rel_bias, buckets, name, comm=None):
    nbr = buckets.shape[0]

    def body(rb_ref, bk_ref, o_ref):
        first, h = pl.program_id(1), pl.program_id(2)
        tab = bk_ref[0]

        def step(bkt, acc):
            return jnp.where(tab == bkt, rb_ref[bkt, h], acc)

        bias = lax.fori_loop(0, N_BUCKETS, step, jnp.zeros((2 * BLK, BLK), F32))
        row = lax.broadcasted_iota(jnp.int32, (2 * BLK, BLK), 0)
        col = lax.broadcasted_iota(jnp.int32, (2 * BLK, BLK), 1)
        valid = ((row < BLK) & (row >= col) & (first == 0)) | ((row >= BLK) & (row - BLK <= col))
        o_ref[0, 0, 0] = jnp.where(valid, bias, NEG)

    (bias,), got = _host_call(
        body, comm, name=name, out_shape=[jax.ShapeDtypeStruct((nbr, 2, N_HEADS, 2 * BLK, BLK), F32)],
        grid=(nbr, 2, N_HEADS),
        in_specs=[pl.BlockSpec(memory_space=pltpu.SMEM),
                  pl.BlockSpec((1, 2 * BLK, BLK), lambda i, f, h: (i, 0, 0))],
        out_specs=[pl.BlockSpec((1, 1, 1, 2 * BLK, BLK), lambda i, f, h: (i, f, h, 0, 0))],
        scratch_shapes=[], args=(rel_bias, buckets))
    return bias, got


def _bias_grad(dbias_list, buckets, name):
    nbr = len(dbias_list)

    def body(*refs):
        d_refs, bk_ref, o_ref, part = refs[:nbr], refs[nbr], refs[nbr + 1], refs[nbr + 2]

        def step(bkt, carry):
            hit = [bk_ref[bi] == bkt for bi in range(nbr)]
            for h in range(N_HEADS):
                tot = jnp.zeros((1, BLK), F32)
                for bi in range(nbr):
                    tot = tot + jnp.sum(jnp.where(hit[bi], d_refs[bi][h], 0.0), axis=0, keepdims=True)
                part[bkt, h:h + 1, :] = tot
            return carry

        lax.fori_loop(0, N_BUCKETS, step, 0)
        lane = lax.broadcasted_iota(jnp.int32, (N_HEADS, LANES), 1)
        acc = jnp.zeros((N_HEADS, LANES), F32)
        for bkt in range(N_BUCKETS):
            acc = acc + jnp.where(lane == bkt, jnp.sum(part[bkt], axis=1, keepdims=True), 0.0)
        o_ref[...] = acc

    band = pl.BlockSpec((N_HEADS, 2 * BLK, BLK), lambda i: (0, 0, 0))
    return pl.pallas_call(
        body, name=name, out_shape=jax.ShapeDtypeStruct((N_HEADS, LANES), F32), grid=(1,),
        in_specs=[band] * nbr + [pl.BlockSpec((nbr, 2 * BLK, BLK), lambda i: (0, 0, 0))],
        out_specs=pl.BlockSpec((N_HEADS, LANES), lambda i: (0, 0)),
        scratch_shapes=[pltpu.VMEM((N_BUCKETS, N_HEADS, BLK), F32)],
        compiler_params=_cparams(("arbitrary",)),
    )(*dbias_list, buckets)


MLA_TQ = 256
MLA_TK = 256


LOG2E = math.log2(math.e)
MLA_C = MLA_SCALE * LOG2E


def _key_le_query(tk, tq):
    return lax.broadcasted_iota(jnp.int32, (tk, tq), 0) <= lax.broadcasted_iota(jnp.int32, (tk, tq), 1)


def _row_mask(shape, hh):
    row = lax.broadcasted_iota(jnp.int32, shape, 0)
    return (row >= hh * HEAD_DIM) & (row < (hh + 1) * HEAD_DIM)


def _host_call(body, comm, *, name, grid, in_specs, out_specs, out_shape, scratch_shapes, args):
    sem = ("arbitrary",) * len(grid)
    if comm is None:
        res = pl.pallas_call(body, name=name, grid=grid, in_specs=in_specs, out_specs=out_specs,
                             out_shape=out_shape, scratch_shapes=scratch_shapes,
                             compiler_params=_cparams(sem))(*args)
        return res, []
    n_in, n_out, n_s, cn = len(in_specs), len(out_specs), len(scratch_shapes), comm.n

    def hosted(*refs):
        ins, refs = refs[:n_in], refs[n_in:]
        c_ins, refs = refs[:cn], refs[cn:]
        outs, refs = refs[:n_out], refs[n_out:]
        c_outs, refs = refs[:cn], refs[cn:]
        scr, c_sems = refs[:n_s], refs[n_s:]
        ids = [pl.program_id(a) for a in range(len(grid))]
        first = functools.reduce(jnp.logical_and, [i == 0 for i in ids])
        last = functools.reduce(jnp.logical_and, [i == g - 1 for i, g in zip(ids, grid)])

        @pl.when(first)
        def _():
            comm.start(c_ins, c_outs, c_sems)

        body(*ins, *outs, *scr)

        @pl.when(last)
        def _():
            comm.finish(c_ins, c_outs, c_sems)

    res = pl.pallas_call(
        hosted, name=name, grid=grid, in_specs=list(in_specs) + _hbm_specs(cn),
        out_specs=list(out_specs) + _hbm_specs(cn), out_shape=list(out_shape) + list(comm.out_shape),
        scratch_shapes=list(scratch_shapes) + list(comm.scratch), compiler_params=_cparams(sem),
    )(*args, *comm.inputs)
    return res[:n_out], res[n_out:]


def _mla_fwd_t(q, k, vt, name, comm=None):
    B, S, _ = q.shape
    tq, tk = MLA_TQ, MLA_TK
    assert tq == tk
    npair = N_HEADS // 2
    nq = S // tq

    def body(q_ref, k_ref, vt_ref, o_ref, lse_ref, s_scr, e_scr, acc_scr, m_scr, a_scr):
        i = pl.program_id(1)
        diag = _key_le_query(tk, tq)
        m_scr[...] = jnp.full_like(m_scr, NEG)
        acc_scr[...] = jnp.zeros_like(acc_scr)

        def step(j, masked):
            rows = pl.ds(pl.multiple_of(j * tk, tk), tk)
            for h in range(N_HEADS):
                hsl = slice(h * LANES, (h + 1) * LANES)
                s_scr[h] = _dot_nt(k_ref[0, rows, hsl], q_ref[0, :, hsl])
            for h in range(N_HEADS):
                s = s_scr[h]
                if masked:
                    s = jnp.where(diag, s, NEG)
                m_old = m_scr[h:h + 1, :]
                m_new = jnp.maximum(m_old, jnp.max(s, axis=0, keepdims=True))
                a_scr[h:h + 1, :] = jnp.exp2((m_old - m_new) * MLA_C)
                e_scr[h] = jnp.exp2((s - m_new) * MLA_C).astype(BF16)
                m_scr[h:h + 1, :] = m_new
            for h in range(N_HEADS):
                vj = vt_ref[0, h // 2, j]
                vh = jnp.where(_row_mask(vj.shape, h % 2), vj, jnp.ones_like(vj))
                acc_scr[h] = acc_scr[h] * a_scr[h:h + 1, :] + _dot_nn(vh, e_scr[h])

        def loop_body(j, carry):
            step(j, False)
            return carry

        lax.fori_loop(0, i, loop_body, 0)
        step(i, True)
        rows0 = _row_mask((LANES, tq), 0)
        for p in range(npair):
            l0 = acc_scr[2 * p, HEAD_DIM:HEAD_DIM + 1, :]
            l1 = acc_scr[2 * p + 1, 0:1, :]
            o_t = jnp.where(rows0, acc_scr[2 * p] / l0, acc_scr[2 * p + 1] / l1)
            o_ref[0, :, p * LANES:(p + 1) * LANES] = jnp.transpose(o_t)
            lse_ref[0, p, 0] = jnp.zeros((8, tq), F32)
            lse_ref[0, p, 0, 0:1, :] = m_scr[2 * p:2 * p + 1, :] * MLA_C + jnp.log(l0) * LOG2E
            lse_ref[0, p, 0, 1:2, :] = m_scr[2 * p + 1:2 * p + 2, :] * MLA_C + jnp.log(l1) * LOG2E

    return _host_call(
        body, comm, name=name,
        out_shape=[jax.ShapeDtypeStruct((B, S, D_B), F32), jax.ShapeDtypeStruct((B, npair, nq, 8, tq), F32)],
        grid=(B, nq),
        in_specs=[pl.BlockSpec((1, tq, N_HEADS * LANES), lambda b, i: (b, i, 0)),
                  pl.BlockSpec((1, S, N_HEADS * LANES), lambda b, i: (b, 0, 0)),
                  pl.BlockSpec((1, npair, S // tk, LANES, tk), lambda b, i: (b, 0, 0, 0, 0))],
        out_specs=[pl.BlockSpec((1, tq, D_B), lambda b, i: (b, i, 0)),
                   pl.BlockSpec((1, npair, 1, 8, tq), lambda b, i: (b, 0, i, 0, 0))],
        scratch_shapes=[pltpu.VMEM((N_HEADS, tk, tq), F32), pltpu.VMEM((N_HEADS, tk, tq), BF16),
                        pltpu.VMEM((N_HEADS, LANES, tq), F32), pltpu.VMEM((N_HEADS, tq), F32),
                        pltpu.VMEM((N_HEADS, tq), F32)],
        args=(q, k, vt))


def _rms_bwd_delta(dy, dy_blk, o, g, name):
    B, S, n = o.shape
    tq = MLA_TQ
    tr = ROW_TILE
    sub = tr // tq
    npair = N_HEADS // 2
    tiles = S // tr

    def body(dy_ref, o_ref, g_ref, do_ref, dg_ref, d_ref):
        ov = o_ref[0]
        dyv = dy_ref[...]
        r = lax.rsqrt(jnp.mean(ov * ov, axis=-1, keepdims=True) + EPS)
        nrm = ov * r
        dn = dyv * g_ref[...]
        do = (r * (dn - nrm * jnp.mean(dn * nrm, axis=-1, keepdims=True))).astype(do_ref.dtype)
        do_ref[0] = do

        @pl.when((pl.program_id(0) == 0) & (pl.program_id(1) == 0))
        def _():
            dg_ref[...] = jnp.zeros_like(dg_ref)

        dg_ref[...] += jnp.sum(dyv * nrm, axis=0, keepdims=True)
        prod = do.astype(F32) * ov
        d_ref[...] = jnp.zeros_like(d_ref)
        for p in range(npair):
            for s in range(sub):
                prod_t = jnp.transpose(prod[s * tq:(s + 1) * tq, p * LANES:(p + 1) * LANES])
                d_ref[0, p, s, 0:1, :] = jnp.sum(prod_t[:HEAD_DIM], axis=0, keepdims=True)
                d_ref[0, p, s, 1:2, :] = jnp.sum(prod_t[HEAD_DIM:], axis=0, keepdims=True)

    tok = pl.BlockSpec((1, tr, n), lambda b, t: (b, t, 0))
    return pl.pallas_call(
        body, name=name,
        out_shape=[jax.ShapeDtypeStruct((B, S, n), BF16), jax.ShapeDtypeStruct((1, n), F32),
                   jax.ShapeDtypeStruct((B, npair, S // tq, 8, tq), F32)],
        grid=(B, tiles),
        in_specs=[pl.BlockSpec((tr, n), lambda b, t: (b * tiles + t, dy_blk)), tok,
                  pl.BlockSpec((1, n), lambda b, t: (0, 0))],
        out_specs=[tok, pl.BlockSpec((1, n), lambda b, t: (0, 0)),
                   pl.BlockSpec((1, npair, sub, 8, tq), lambda b, t: (b, 0, t, 0, 0))],
        compiler_params=_cparams(("arbitrary", "arbitrary")),
    )(dy, o, g)


def _mla_bwd_t(q, k, v, do, lse, delta, name, comm=None):
    B, S, _ = q.shape
    tq, tk = MLA_TQ, MLA_TK
    assert tq == tk
    npair = N_HEADS // 2
    nq = S // tq

    hg = N_HEADS
    pg = hg // 2
    ngroup = N_HEADS // hg

    def body(q_ref, do_ref, lse_ref, dl_ref, k_ref, v_ref, dk_ref, dv_ref, dq_ref,
             s_scr, dp_scr, p_scr, ds_scr, dk_s, dv_s, kt_s):
        j = pl.program_id(2)

        @pl.when(j == 0)
        def _():
            dq_ref[...] = jnp.zeros_like(dq_ref)

        dk_s[...] = jnp.zeros_like(dk_s)
        dv_s[...] = jnp.zeros_like(dv_s)
        diag = _key_le_query(tk, tq)
        hsl = lambda h: slice(h * LANES, (h + 1) * LANES)
        for h in range(hg):
            kt_s[h] = jnp.transpose(k_ref[0, :, hsl(h)].astype(F32)).astype(BF16)

        def step(i, masked):
            rows = pl.ds(pl.multiple_of(i * tq, tq), tq)

            def dom(h):
                dov = do_ref[0, rows, hsl(h // 2)]
                return jnp.where(_head_mask((tq, LANES), h % 2), dov, jnp.zeros_like(dov))

            for h in range(hg):
                s_scr[h] = _dot_nt(k_ref[0, :, hsl(h)], q_ref[0, rows, hsl(h)])
                dp_scr[h] = _dot_nt(v_ref[0, :, hsl(h // 2)], dom(h))
            for h in range(hg):
                pr = jnp.exp2(s_scr[h] * MLA_C - lse_ref[0, h // 2, i, h % 2:h % 2 + 1, :])
                if masked:
                    pr = jnp.where(diag, pr, 0.0)
                p_scr[h] = pr.astype(BF16)
                ds_scr[h] = (pr * (dp_scr[h] - dl_ref[0, h // 2, i, h % 2:h % 2 + 1, :])).astype(BF16)
            for h in range(hg):
                dv_s[h // 2] += _dot_nn(p_scr[h], dom(h))
                dk_s[h] += _dot_nn(ds_scr[h], q_ref[0, rows, hsl(h)])
                dq_ref[0, h // 2, i, hsl(h % 2), :] += _dot_nn(kt_s[h], ds_scr[h]) * MLA_SCALE

        step(j, True)

        def loop_body(i, carry):
            step(i, False)
            return carry

        lax.fori_loop(j + 1, nq, loop_body, 0)
        for h in range(hg):
            dk_ref[0, :, hsl(h)] = dk_s[h] * MLA_SCALE
        for p in range(pg):
            dv_ref[0, :, hsl(p)] = dv_s[p]

    stat = pl.BlockSpec((1, pg, nq, 8, tq), lambda b, g, j: (b, g, 0, 0, 0))
    return _host_call(
        body, comm, name=name,
        out_shape=[jax.ShapeDtypeStruct((B, S, N_HEADS * LANES), F32), jax.ShapeDtypeStruct((B, S, D_B), F32),
                   jax.ShapeDtypeStruct((B, npair, nq, 2 * LANES, tq), F32)],
        grid=(B, ngroup, S // tk),
        in_specs=[pl.BlockSpec((1, S, hg * LANES), lambda b, g, j: (b, 0, g)),
                  pl.BlockSpec((1, S, pg * LANES), lambda b, g, j: (b, 0, g)),
                  stat, stat,
                  pl.BlockSpec((1, tk, hg * LANES), lambda b, g, j: (b, j, g)),
                  pl.BlockSpec((1, tk, pg * LANES), lambda b, g, j: (b, j, g))],
        out_specs=[pl.BlockSpec((1, tk, hg * LANES), lambda b, g, j: (b, j, g)),
                   pl.BlockSpec((1, tk, pg * LANES), lambda b, g, j: (b, j, g)),
                   pl.BlockSpec((1, pg, nq, 2 * LANES, tq), lambda b, g, j: (b, g, 0, 0, 0))],
        scratch_shapes=[pltpu.VMEM((hg, tk, tq), F32), pltpu.VMEM((hg, tk, tq), F32),
                        pltpu.VMEM((hg, tk, tq), BF16), pltpu.VMEM((hg, tk, tq), BF16),
                        pltpu.VMEM((hg, tk, LANES), F32), pltpu.VMEM((pg, tk, LANES), F32),
                        pltpu.VMEM((hg, LANES, tk), BF16)],
        args=(q, do, lse, delta, k, v))


def _bucket_tables():
    return jnp.asarray(np.stack([_band_buckets(d) for d in DILATIONS]))


def _local_step(x, target, mod, wts, gains, rel_bias, ffn_shards=None, bias=None):
    B, S, D = x.shape
    T = B * S
    sh1, sc1, g1, sh2, sc2, g2 = [mod[:, i * D:(i + 1) * D].reshape(B, 1, D) for i in range(N_MOD)]
    cs, sn = _rope_tables()
    buckets_dev = _bucket_tables()
    if bias is None:
        bias, _ = _bias_tables(rel_bias, buckets_dev, "rel_bias_tables")
    w_in = wts["w_in"]

    h1 = _adaln_fwd(x, gains["g_norm1"], sc1, sh1, "adaln1_fwd")
    h1f = h1.reshape(T, D)
    qkv_v, rest3, cqn, ckvn = _in_proj(h1, w_in[:, :P_QKV], w_in[:, P_QKV:], gains["g_cq"], gains["g_ckv"],
                                       "mm_in")
    rest = rest3.reshape(T, P_REST)
    o_d, lse_d = [], []
    late_got = []
    for i, d in enumerate(DILATIONS):
        comm = _GatherComm(ffn_shards[i + 1:i + 2]) if (ffn_shards and i < 2) else None
        (o_i, lse_i), got = _dil_fwd(qkv_v[i], bias, i, d, f"dil_fwd_{d}", comm)
        late_got += list(got)
        o_d.append(o_i)
        lse_d.append(lse_i)
    if ffn_shards:
        wts = dict(wts, w_out=late_got[1].reshape(D, D))
    out_a_v, lse_a_v = _dil_merge(o_d, lse_d, "dil_merge")
    out_a = out_a_v[0]
    qc, kc, v, vt = _mla_proj(cqn, ckvn, rest3, wts["w_uq"], wts["w_kv"], cs, sn, "mla_proj")
    (out_b, lse_b), got = _mla_fwd_t(qc, kc, vt, "mla_fwd", _GatherComm(ffn_shards[:1]) if ffn_shards else None)
    if ffn_shards:
        wts = dict(wts, w_ffn_in=got[0].reshape(N_CHIP, D, -1), w_ffn_out=late_got[0].reshape(D_FF, D))
    out_af, out_bf = out_a.reshape(T, D_A), out_b.reshape(T, D_B)
    y = _rms_fwd_pair(out_af, out_bf, gains["g_out_a"], gains["g_out_b"], "rms_out_fwd")
    mix = _mm(y, wts["w_out"], "nn", F32, "mm_out").reshape(B, S, D)
    h2, x1 = _adaln_fwd(x, gains["g_norm2"], sc2, sh2, "adaln2_fwd", mix=mix, gate=g1)
    h2f = h2.reshape(T, D)
    gu, act = _ffn_in_fwd(h2f, wts["w_ffn_in"], "mm_ffn_in")
    f = _mm(act, wts["w_ffn_out"], "nn", F32, "mm_ffn_out").reshape(B, S, D)
    dx2, df, dg2, dg_final, loss = _final_loss(x1, f, g2, gains["g_final"], target, "final_loss")

    dff = df.reshape(T, D)
    dgu = _ffn_out_bwd(dff, wts["w_ffn_out"], gu, "mm_ffn_out_dx")
    gw_ffn_out = _mm(act, dff, "tn", F32, "mm_ffn_out_dw")
    dh2 = _mm(dgu, wts["w_ffn_in"], "nt", F32, "mm_ffn_in_dx", col_blocks=N_CHIP, halves=True).reshape(B, S, D)
    gw_ffn_in = _mm(h2f, dgu, "tn", F32, "mm_ffn_in_dw", col_blocks=N_CHIP, halves=True)
    ffn_g8 = ffn_r1 = None
    if ffn_shards:
        ffn_g8 = [gw_ffn_in.reshape(N_DEV, -1, gw_ffn_in.shape[-1]), gw_ffn_out.reshape(N_DEV, -1, D)]
        dx1, dsh2, dsc2, dg_norm2, dg1, dmix, ffn_r1 = _adaln_bwd(
            dh2, x1, gains["g_norm2"], sc2, dx2, "adaln2_bwd", mix=mix, gate=g1, comm=_ToSiblingComm(ffn_g8))
    else:
        dx1, dsh2, dsc2, dg_norm2, dg1, dmix = _adaln_bwd(dh2, x1, gains["g_norm2"], sc2, dx2, "adaln2_bwd",
                                                          mix=mix, gate=g1)
    dmixf = dmix.reshape(T, D)
    dy = _mm(dmixf, wts["w_out"], "nt", F32, "mm_out_dx")
    gw_out = _mm(y, dmixf, "tn", F32, "mm_out_dw")
    do_a_v, dg_out_a = _rms_bwd_views(dy, 0, out_a, gains["g_out_a"], "rms_outa_bwd")
    do_b3, dg_out_b, delta_b = _rms_bwd_delta(dy, 1, out_b, gains["g_out_b"], "rms_outb_bwd")
    ffn_a4 = ffn_send = None
    if ffn_shards:
        ffn_a4, ffn_send = _rs_first(ffn_g8, "ffn", r1=ffn_r1)
    (dkc, dv, dq_t), ffn_r2 = _mla_bwd_t(qc, kc, v, do_b3, lse_b, delta_b, "mla_bwd",
                                         _ToChipsComm(ffn_send[:1]) if ffn_shards else None)
    dq_raw = _qrope_bwd(dq_t, cs, -sn, "rope_q_bwd").reshape(T, N_HEADS * LANES)
    dkrw = _krope_bwd(dkc, cs, -sn, "rope_k_bwd").reshape(T, LANES)
    dcqn = _mm(dq_raw, wts["w_uq"], "nt", F32, "mm_uq_dx")
    gw_uq = _mm(cqn, dq_raw, "tn", F32, "mm_uq_dw")
    dkv = jnp.concatenate([dkc.reshape(T, -1), dv.reshape(T, -1)], axis=1).astype(BF16)
    dckvn = _mm(dkv, wts["w_kv"], "nt", F32, "mm_ukv_dx")
    gw_kv = _mm(ckvn, dkv, "tn", F32, "mm_ukv_dw")
    dcq, dg_cq = _rms_bwd(dcqn, 0, rest, 1, Q_LORA, gains["g_cq"], "rms_cq_bwd")
    dckv, dg_ckv = _rms_bwd(dckvn, 0, rest, 0, KV_LORA, gains["g_ckv"], "rms_ckv_bwd")
    dqkv_d, dbias_d = [], []
    for i, d in enumerate(DILATIONS):
        comm = _ToChipsComm(ffn_send[1:]) if (ffn_shards and i == 0) else None
        (dqkv_i, dbias_i), got = _dil_bwd(qkv_v[i], do_a_v[i], out_a_v[i], lse_a_v[i], bias, i, d,
                                          f"dil_bwd_{d}", comm)
        if comm is not None:
            ffn_r2 = list(ffn_r2) + list(got)
        dqkv_d.append(dqkv_i)
        dbias_d.append(dbias_i)
    dqkv = _sum_views_bf16(dqkv_d, "dil_bwd_sum").reshape(T, P_QKV)
    g_rel_bias = _bias_grad(dbias_d, buckets_dev, "rel_bias_grad")[:, :N_BUCKETS].T
    dproj = jnp.concatenate([dqkv, dckv, dkrw, dcq], axis=1)
    gw_in = _mm(h1f, dproj, "tn", F32, "mm_in_dw")
    mix_a4 = mix_r2 = None
    if ffn_shards:
        nat = [_w_in_from_kernel(gw_in), _w_uq_from_kernel(gw_uq), _w_ukv_from_kernel(gw_kv)]
        g8 = [_shards_from_full(g) for g in nat] + [gw_out]
        mix_a4, mix_send = _rs_first([g.reshape(N_DEV, -1, g.shape[-1]) for g in g8], "mix")
        dh1, mix_r2 = _mm(dproj, w_in, "nt", F32, "mm_in_dx", comm=_ToChipsComm(mix_send))
    else:
        dh1 = _mm(dproj, w_in, "nt", F32, "mm_in_dx")
    dh1 = dh1.reshape(B, S, D)
    grad_x, dsh1, dsc1, dg_norm1 = _adaln_bwd(dh1, x, gains["g_norm1"], sc1, dx1, "adaln1_bwd")
    gmod = jnp.concatenate([dsh1, dsc1, dg1, dsh2, dsc2, dg2], axis=-1).reshape(B, N_MOD * D)
    grads = dict(w_in=gw_in, w_uq=gw_uq, w_kv=gw_kv, w_out=gw_out, w_ffn_in=gw_ffn_in, w_ffn_out=gw_ffn_out,
                 g_norm1=dg_norm1, g_cq=dg_cq, g_ckv=dg_ckv, rel_bias=g_rel_bias, g_out_a=dg_out_a,
                 g_out_b=dg_out_b, g_norm2=dg_norm2, g_final=dg_final, ffn_pending=(ffn_a4, ffn_r2),
                 mix_pending=(mix_a4, mix_r2))
    return loss, grad_x, gmod, grads


def _w_in_to_kernel(w):
    z = lambda n: jnp.zeros((w.shape[0], n), w.dtype)
    i3, i4, i5 = 3 * D_A, 3 * D_A + Q_LORA, 3 * D_A + Q_LORA + KV_LORA
    return jnp.concatenate([w[:, :i3], w[:, i4:i5], z(NOPE_DIM), w[:, i5:], z(LANES - NOPE_DIM - ROPE_DIM),
                            w[:, i3:i4]], axis=1)


def _w_in_from_kernel(g):
    o = P_QKV + KV_LORA
    return jnp.concatenate([g[:, :P_QKV], g[:, o + LANES:], g[:, P_QKV:o],
                            g[:, o + NOPE_DIM:o + NOPE_DIM + ROPE_DIM]], axis=1)


def _w_uq_to_kernel(w):
    w3 = w.reshape(Q_LORA, N_HEADS, NOPE_DIM + ROPE_DIM)
    return jnp.pad(w3, ((0, 0), (0, 0), (0, LANES - NOPE_DIM - ROPE_DIM))).reshape(Q_LORA, N_HEADS * LANES)


def _w_uq_from_kernel(g):
    return g.reshape(Q_LORA, N_HEADS, LANES)[:, :, :NOPE_DIM + ROPE_DIM].reshape(Q_LORA, -1)


def _w_ukv_to_kernel(w):
    w3 = w.reshape(KV_LORA, N_HEADS, 2 * HEAD_DIM)
    wk = jnp.pad(w3[:, :, :NOPE_DIM], ((0, 0), (0, 0), (0, LANES - NOPE_DIM))).reshape(KV_LORA, N_HEADS * LANES)
    wv = w3[:, :, NOPE_DIM:].reshape(KV_LORA, D_B)
    return jnp.concatenate([wk, wv], axis=1)


def _w_ukv_from_kernel(g):
    gk = g[:, :N_HEADS * LANES].reshape(KV_LORA, N_HEADS, LANES)[:, :, :NOPE_DIM]
    gv = g[:, N_HEADS * LANES:].reshape(KV_LORA, N_HEADS, HEAD_DIM)
    return jnp.concatenate([gk, gv], axis=2).reshape(KV_LORA, -1)


MESH = pl.DeviceIdType.MESH


def _my_place():
    return lax.axis_index("x"), lax.axis_index("y"), lax.axis_index("c")


def _other_chips(x, y):
    return [(1 - x, y), (x, 1 - y), (1 - x, 1 - y)]


def _allgather8(x_shard, name, in_hbm):
    m_per, n = x_shard.shape
    space = pl.ANY if in_hbm else pltpu.VMEM

    def body(x_ref, out_ref, send_sems, recv_sems, local_sem):
        x, y, c = _my_place()
        me, sibling = (x, y, c), (x, y, 1 - c)
        chips = _other_chips(x, y)

        def rows(px, py, pc):
            return out_ref.at[pl.ds((4 * px + 2 * py + pc) * m_per, m_per), :]

        def copy(k, block, to, src=None):
            return pltpu.make_async_remote_copy(
                src_ref=rows(*block) if src is None else src, dst_ref=rows(*block),
                send_sem=send_sems.at[k], recv_sem=recv_sems.at[k], device_id=to, device_id_type=MESH)

        mine = pltpu.make_async_copy(x_ref, rows(*me), local_sem)
        mine.start()
        first = [copy(0, me, sibling, src=x_ref)]
        first += [copy(1 + j, me, (*chip, c), src=x_ref) for j, chip in enumerate(chips)]
        for cp in first:
            cp.start()
        passed = [copy(4 + j, (*chip, c), sibling) for j, chip in enumerate(chips)]
        for j, chip in enumerate(chips):
            copy(1 + j, (*chip, c), me).wait_recv()
            passed[j].start()
        copy(0, sibling, me).wait_recv()
        for j, chip in enumerate(chips):
            copy(4 + j, (*chip, 1 - c), me).wait_recv()
        for cp in first + passed:
            cp.wait_send()
        mine.wait()

    return pl.pallas_call(
        body, name=name,
        out_shape=jax.ShapeDtypeStruct((N_DEV * m_per, n), x_shard.dtype),
        in_specs=[pl.BlockSpec(memory_space=space)],
        out_specs=pl.BlockSpec(memory_space=space),
        scratch_shapes=[pltpu.SemaphoreType.DMA((7,)), pltpu.SemaphoreType.DMA((7,)), pltpu.SemaphoreType.DMA],
        compiler_params=pltpu.CompilerParams(vmem_limit_bytes=VMEM_LIMIT),
    )(x_shard)


def _hbm_specs(n):
    return [pl.BlockSpec(memory_space=pl.ANY)] * n


class _GatherComm:
    def __init__(self, shards):
        self.n = n = len(shards)
        self.inputs = [s.reshape(2, s.shape[0] // 2, s.shape[1]) for s in shards]
        self.out_shape = [jax.ShapeDtypeStruct((N_DEV,) + s.shape[1:], s.dtype) for s in self.inputs]
        self.scratch = [pltpu.SemaphoreType.DMA((7 * n,)), pltpu.SemaphoreType.DMA((7 * n,))]

    def _parts(self, xs, outs, sems):
        send_sems, recv_sems = sems
        x, y, c = _my_place()

        def blk(k, px, py, pc):
            return outs[k].at[4 * px + 2 * py + pc]

        def copy(k, kind, block, to, own=False):
            return pltpu.make_async_remote_copy(
                src_ref=xs[k].at[c] if own else blk(k, *block), dst_ref=blk(k, *block),
                send_sem=send_sems.at[7 * k + kind], recv_sem=recv_sems.at[7 * k + kind],
                device_id=to, device_id_type=MESH)

        def whole(k):
            return pltpu.make_async_remote_copy(
                src_ref=xs[k], dst_ref=outs[k].at[pl.ds(4 * x + 2 * y, 2)],
                send_sem=send_sems.at[7 * k], recv_sem=recv_sems.at[7 * k],
                device_id=(x, y, 1 - c), device_id_type=MESH)

        me, sibling = (x, y, c), (x, y, 1 - c)
        chips = _other_chips(x, y)
        first = []
        for k in range(self.n):
            first.append(whole(k))
            first += [copy(k, 1 + j, me, (*chip, c), own=True) for j, chip in enumerate(chips)]
        return copy, whole, me, sibling, chips, c, first

    def start(self, xs, outs, sems):
        for cp in self._parts(xs, outs, sems)[-1]:
            cp.start()

    def finish(self, xs, outs, sems):
        copy, whole, me, sibling, chips, c, first = self._parts(xs, outs, sems)
        passed = []
        for j, chip in enumerate(chips):
            for k in range(self.n):
                copy(k, 1 + j, (*chip, c), me).wait_recv()
                fwd = copy(k, 4 + j, (*chip, c), sibling)
                fwd.start()
                passed.append(fwd)
        for k in range(self.n):
            whole(k).wait_recv()
        for j, chip in enumerate(chips):
            for k in range(self.n):
                copy(k, 4 + j, (*chip, 1 - c), me).wait_recv()
        for cp in first + passed:
            cp.wait_send()


class _AllGatherComm:
    def __init__(self, x):
        self.inputs = [x]
        self.n = 1
        self.out_shape = [jax.ShapeDtypeStruct((N_DEV,) + x.shape, x.dtype)]
        self.scratch = [pltpu.SemaphoreType.DMA((7,)), pltpu.SemaphoreType.DMA((7,)), pltpu.SemaphoreType.DMA]

    def _parts(self, xs, outs, sems):
        send_sems, recv_sems, local_sem = sems
        x_ref, out_ref = xs[0], outs[0]
        x, y, c = _my_place()

        def copy(k, block, to, own=False):
            blk = out_ref.at[4 * block[0] + 2 * block[1] + block[2]]
            return pltpu.make_async_remote_copy(
                src_ref=x_ref if own else blk, dst_ref=blk, send_sem=send_sems.at[k], recv_sem=recv_sems.at[k],
                device_id=to, device_id_type=MESH)

        me, sibling = (x, y, c), (x, y, 1 - c)
        chips = _other_chips(x, y)
        local = pltpu.make_async_copy(x_ref, out_ref.at[4 * x + 2 * y + c], local_sem)
        first = [copy(0, me, sibling, own=True)]
        first += [copy(1 + j, me, (*chip, c), own=True) for j, chip in enumerate(chips)]
        return copy, me, sibling, chips, c, local, first

    def start(self, xs, outs, sems):
        _, _, _, _, _, local, first = self._parts(xs, outs, sems)
        for cp in [local] + first:
            cp.start()

    def finish(self, xs, outs, sems):
        copy, me, sibling, chips, c, local, first = self._parts(xs, outs, sems)
        passed = []
        for j, chip in enumerate(chips):
            copy(1 + j, (*chip, c), me).wait_recv()
            fwd = copy(4 + j, (*chip, c), sibling)
            fwd.start()
            passed.append(fwd)
        copy(0, sibling, me).wait_recv()
        for j, chip in enumerate(chips):
            copy(4 + j, (*chip, 1 - c), me).wait_recv()
        for cp in first + passed:
            cp.wait_send()
        local.wait()


class _BothComm:
    def __init__(self, a, b):
        self.a, self.b = a, b
        self.inputs = a.inputs + b.inputs
        self.n = a.n + b.n
        self.out_shape = a.out_shape + b.out_shape
        self.scratch = a.scratch + b.scratch

    def _split(self, xs, outs, sems):
        na, ns = self.a.n, len(self.a.scratch)
        return (xs[:na], outs[:na], sems[:ns]), (xs[na:], outs[na:], sems[ns:])

    def start(self, xs, outs, sems):
        pa, pb = self._split(xs, outs, sems)
        self.a.start(*pa)
        self.b.start(*pb)

    def finish(self, xs, outs, sems):
        pa, pb = self._split(xs, outs, sems)
        self.a.finish(*pa)
        self.b.finish(*pb)


class _ToChipsComm:
    def __init__(self, a4s):
        self.inputs = list(a4s)
        self.n = n = len(a4s)
        nc = N_CHIP - 1
        self.out_shape = [jax.ShapeDtypeStruct((nc,) + a.shape[1:], a.dtype) for a in a4s]
        self.scratch = [pltpu.SemaphoreType.DMA((nc * n,)), pltpu.SemaphoreType.DMA((nc * n,))]

    def _copies(self, as_, rs, sems):
        send_sems, recv_sems = sems
        x, y, c = _my_place()
        nc = N_CHIP - 1
        return [pltpu.make_async_remote_copy(
            src_ref=as_[k].at[2 * cx + cy], dst_ref=rs[k].at[j], send_sem=send_sems.at[nc * k + j],
            recv_sem=recv_sems.at[nc * k + j], device_id=(cx, cy, c), device_id_type=MESH)
            for k in range(self.n) for j, (cx, cy) in enumerate(_other_chips(x, y))]

    def start(self, as_, rs, sems):
        for cp in self._copies(as_, rs, sems):
            cp.start()

    def finish(self, as_, rs, sems):
        for cp in self._copies(as_, rs, sems):
            cp.wait()


def _run_comm(comm, name):
    n = comm.n

    def body(*refs):
        ins, outs, sems = refs[:n], refs[n:2 * n], refs[2 * n:]
        comm.start(ins, outs, sems)
        comm.finish(ins, outs, sems)

    return pl.pallas_call(
        body, name=name, out_shape=comm.out_shape, in_specs=_hbm_specs(n), out_specs=_hbm_specs(n),
        scratch_shapes=comm.scratch,
    )(*comm.inputs)


class _ToSiblingComm:
    def __init__(self, g8s):
        self.inputs = list(g8s)
        self.n = n = len(g8s)
        self.out_shape = [jax.ShapeDtypeStruct((N_CHIP,) + g.shape[1:], g.dtype) for g in g8s]
        self.scratch = [pltpu.SemaphoreType.DMA((N_CHIP * n,)), pltpu.SemaphoreType.DMA((N_CHIP * n,))]

    def _copies(self, gs, rs, sems):
        send_sems, recv_sems = sems
        x, y, c = _my_place()
        return [pltpu.make_async_remote_copy(
            src_ref=gs[k].at[2 * s + 1 - c], dst_ref=rs[k].at[s], send_sem=send_sems.at[N_CHIP * k + s],
            recv_sem=recv_sems.at[N_CHIP * k + s], device_id=(x, y, 1 - c), device_id_type=MESH)
            for k in range(self.n) for s in range(N_CHIP)]

    def start(self, gs, rs, sems):
        for cp in self._copies(gs, rs, sems):
            cp.start()

    def finish(self, gs, rs, sems):
        for cp in self._copies(gs, rs, sems):
            cp.wait()


def _swap_halves(hs, name):
    n = len(hs)

    def body(*refs):
        o_refs = refs[n:2 * n]
        send_sems, recv_sems = refs[2 * n:]
        x, y, c = _my_place()

        def remote(k, slot):
            return pltpu.make_async_remote_copy(
                src_ref=o_refs[k].at[slot], dst_ref=o_refs[k].at[slot], send_sem=send_sems.at[k],
                recv_sem=recv_sems.at[k], device_id=(x, y, 1 - c), device_id_type=MESH)

        sends = [remote(k, c) for k in range(n)]
        for cp in sends:
            cp.start()
        for k in range(n):
            remote(k, 1 - c).wait_recv()
        for cp in sends:
            cp.wait_send()

    return pl.pallas_call(
        body, name=name,
        out_shape=[jax.ShapeDtypeStruct(h.shape, h.dtype) for h in hs],
        in_specs=_hbm_specs(n), out_specs=_hbm_specs(n),
        input_output_aliases={k: k for k in range(n)},
        scratch_shapes=[pltpu.SemaphoreType.DMA((n,)), pltpu.SemaphoreType.DMA((n,))],
    )(*hs)


ADD_TILES = 2


def _add_blocks(a_list, a_idx_fn, others_list, ns, sel, name, out_blocks=None, out_idx_fn=None,
                bf16_copy=False):
    out_blocks = out_blocks or ns
    out_idx_fn = out_idx_fn or (lambda s, sel_ref: s)
    n = len(a_list)
    n_o = len(others_list[0])
    per = 1 + n_o

    def body(sel_ref, *refs):
        for k in range(n):
            ins = refs[k * per:(k + 1) * per]
            acc = ins[0][0]
            for r in ins[1:]:
                acc = acc + r[0].astype(F32)
            refs[n * per + k][0] = acc
            if bf16_copy:
                refs[n * per + n + k][0] = acc.astype(BF16)

    in_specs, args, out_specs, out_shape = [], [], [], []
    for a, others in zip(a_list, others_list):
        _, R, N = a.shape
        tr = R // ADD_TILES
        assert tr % 8 == 0, a.shape
        in_specs.append(pl.BlockSpec((1, tr, N), lambda s, i, sel_ref: (a_idx_fn(s, sel_ref), i, 0)))
        args.append(a)
        for arr, fixed in others:
            if fixed is None:
                in_specs.append(pl.BlockSpec((1, tr, N), lambda s, i, sel_ref: (s, i, 0)))
            else:
                in_specs.append(pl.BlockSpec((1, tr, N), lambda s, i, sel_ref, fixed=fixed: (fixed, i, 0)))
            args.append(arr)
        out_specs.append(pl.BlockSpec((1, tr, N), lambda s, i, sel_ref: (out_idx_fn(s, sel_ref), i, 0)))
        out_shape.append(jax.ShapeDtypeStruct((out_blocks, R, N), a.dtype))
    if bf16_copy:
        out_specs = out_specs + out_specs
        out_shape = out_shape + [jax.ShapeDtypeStruct(o.shape, BF16) for o in out_shape]
    grid_spec = pltpu.PrefetchScalarGridSpec(num_scalar_prefetch=1, grid=(ns, ADD_TILES), in_specs=in_specs,
                                             out_specs=out_specs)
    return pl.pallas_call(
        body, name=name, out_shape=out_shape, grid_spec=grid_spec,
        compiler_params=_cparams(("parallel", "parallel")),
    )(sel, *args)


def _rs_first(g8s, tag, r1=None):
    c_sel = jnp.reshape(lax.axis_index("c"), (1,)).astype(jnp.int32)
    if r1 is None:
        r1 = _run_comm(_ToSiblingComm(g8s), f"rs_to_sibling_{tag}")
    res = _add_blocks(g8s, lambda s, sel: 2 * s + sel[0], [[(r, None)] for r in r1], N_CHIP, c_sel,
                      f"rs_add_sibling_{tag}", bf16_copy=True)
    return list(res[:len(g8s)]), list(res[len(g8s):])


def _rs_last(a4s, r2s, tag):
    sel = jnp.stack([2 * lax.axis_index("x") + lax.axis_index("y"), lax.axis_index("c")]).astype(jnp.int32)
    h = _add_blocks(a4s, lambda s, sel: sel[0], [[(r, 0), (r, 1), (r, 2)] for r in r2s], 1, sel,
                    f"rs_add_chips_{tag}", out_blocks=2, out_idx_fn=lambda s, sel: sel[1])
    full = _swap_halves(h, f"rs_swap_halves_{tag}")
    return [f.reshape(2 * f.shape[1], f.shape[2]) for f in full]


def _ada_fwd(c_all, w_ada, b_ada, name):
    nb, D = c_all.shape
    ncol = w_ada.shape[1]
    tc = 512

    def body(c_ref, w_ref, b_ref, o_ref):
        cv = c_ref[...]
        cond = (cv * jax.nn.sigmoid(cv)).astype(BF16)
        o_ref[...] = jnp.dot(cond, w_ref[...].astype(BF16), preferred_element_type=F32) + b_ref[...]

    return pl.pallas_call(
        body, name=name, out_shape=jax.ShapeDtypeStruct((nb, ncol), F32), grid=(ncol // tc,),
        in_specs=[pl.BlockSpec((nb, D), lambda j: (0, 0)), pl.BlockSpec((D, tc), lambda j: (0, j)),
                  pl.BlockSpec((1, tc), lambda j: (0, j))],
        out_specs=pl.BlockSpec((nb, tc), lambda j: (0, j)),
        compiler_params=_cparams(("parallel",)),
    )(c_all, w_ada, b_ada)


def _ada_bwd(c_all, gmod_cols, name):
    nb, D = c_all.shape
    ncol = gmod_cols.shape[1]
    tc = 512

    def body(c_ref, g_ref, o_ref):
        cv = c_ref[...]
        cond = (cv * jax.nn.sigmoid(cv)).astype(BF16)
        o_ref[...] = _dot_tn(cond, g_ref[...].astype(BF16))

    return pl.pallas_call(
        body, name=name, out_shape=jax.ShapeDtypeStruct((D, ncol), F32), grid=(ncol // tc,),
        in_specs=[pl.BlockSpec((nb, D), lambda j: (0, 0)), pl.BlockSpec((nb, tc), lambda j: (0, j))],
        out_specs=pl.BlockSpec((D, tc), lambda j: (0, j)),
        compiler_params=_cparams(("parallel",)),
    )(c_all, gmod_cols)


def _adam_math(w, g, m, v):
    m = ADAM_B1 * m + (1.0 - ADAM_B1) * g
    v = ADAM_B2 * v + (1.0 - ADAM_B2) * (g * g)
    m_hat = m / (1.0 - ADAM_B1 ** ADAM_STEP)
    v_hat = v / (1.0 - ADAM_B2 ** ADAM_STEP)
    delta = -ADAM_LR * (m_hat / (jnp.sqrt(v_hat) + ADAM_EPS) + ADAM_WD * w)
    return delta, m, v


def _adamw(w, g, m, v, name):
    rows, cols = w.shape
    tr = _pick(rows, (256, 192, 176, 128, 64, 8))

    def body(w_ref, g_ref, m_ref, v_ref, d_ref, mo_ref, vo_ref):
        d, mn, vn = _adam_math(w_ref[...], g_ref[...], m_ref[...], v_ref[...])
        d_ref[...] = d
        mo_ref[...] = mn
        vo_ref[...] = vn

    spec = pl.BlockSpec((tr, cols), lambda i: (i, 0))
    return pl.pallas_call(
        body, name=name, out_shape=[jax.ShapeDtypeStruct((rows, cols), F32)] * 3, grid=(rows // tr,),
        in_specs=[spec] * 4, out_specs=[spec] * 3, compiler_params=_cparams(("parallel",)),
    )(w, g, m, v)


VEC_ROWS = 8


def _adamw_rows(w, parts, m, v, name):
    n = w.shape[1]
    P = parts.shape[0]
    assert n % (VEC_ROWS * LANES) == 0, n
    shp = (VEC_ROWS, n // VEC_ROWS)

    def body(w_ref, p_ref, m_ref, v_ref, g_ref, d_ref, mo_ref, vo_ref):
        g = p_ref[0]
        for k in range(1, P):
            g = g + p_ref[k]
        d, mn, vn = _adam_math(w_ref[...], g, m_ref[...], v_ref[...])
        g_ref[...] = g
        d_ref[...] = d
        mo_ref[...] = mn
        vo_ref[...] = vn

    vec = pl.BlockSpec(shp, lambda i: (0, 0))
    out = pl.pallas_call(
        body, name=name, out_shape=[jax.ShapeDtypeStruct(shp, F32)] * 4, grid=(1,),
        in_specs=[vec, pl.BlockSpec((P,) + shp, lambda i: (0, 0, 0)), vec, vec], out_specs=[vec] * 4,
        compiler_params=_cparams(("arbitrary",)),
    )(w.reshape(shp), parts.reshape((P,) + shp), m.reshape(shp), v.reshape(shp))
    return [o.reshape(1, n) for o in out]


_SHARDED = ("w_in", "w_uq", "w_ukv", "w_out", "w_ffn_in", "w_ffn_out")
_SMALL = (("g_norm1", 1024), ("g_cq", 384), ("g_ckv", 256), ("rel_bias", 256), ("g_out_a", 512),
          ("g_out_b", 512), ("g_norm2", 1024), ("g_final", 1024))
_SMALL_PAD = 5120


def _full_from_shards(sh):
    return jnp.transpose(sh, (1, 0, 2)).reshape(sh.shape[1], -1)


def _shards_from_full(full):
    rows, cols = full.shape
    return jnp.transpose(full.reshape(rows, N_CHIP, cols // N_CHIP), (1, 0, 2))


def kernel(x, c, w_ada, b_ada, g_norm1, w_in, g_cq, w_uq, g_ckv, w_ukv, rel_bias, g_out_a, g_out_b, w_out, g_norm2, w_ffn_in, w_ffn_out, g_final, loss_target, m_w_ada, m_b_ada, m_g_norm1, m_w_in, m_g_cq, m_w_uq, m_g_ckv, m_w_ukv, m_rel_bias, m_g_out_a, m_g_out_b, m_w_out, m_g_norm2, m_w_ffn_in, m_w_ffn_out, m_g_final, v_w_ada, v_b_ada, v_g_norm1, v_w_in, v_g_cq, v_w_uq, v_g_ckv, v_w_ukv, v_rel_bias, v_g_out_a, v_g_out_b, v_w_out, v_g_norm2, v_w_ffn_in, v_w_ffn_out, v_g_final):
    names = ["w_ada", "b_ada", "g_norm1", "w_in", "g_cq", "w_uq", "g_ckv", "w_ukv", "rel_bias", "g_out_a",
             "g_out_b", "w_out", "g_norm2", "w_ffn_in", "w_ffn_out", "g_final"]
    W = dict(zip(names, [w_ada, b_ada, g_norm1, w_in, g_cq, w_uq, g_ckv, w_ukv, rel_bias, g_out_a, g_out_b,
                         w_out, g_norm2, w_ffn_in, w_ffn_out, g_final]))
    M = dict(zip(names, [m_w_ada, m_b_ada, m_g_norm1, m_w_in, m_g_cq, m_w_uq, m_g_ckv, m_w_ukv, m_rel_bias,
                         m_g_out_a, m_g_out_b, m_w_out, m_g_norm2, m_w_ffn_in, m_w_ffn_out, m_g_final]))
    V = dict(zip(names, [v_w_ada, v_b_ada, v_g_norm1, v_w_in, v_g_cq, v_w_uq, v_g_ckv, v_w_ukv, v_rel_bias,
                         v_g_out_a, v_g_out_b, v_w_out, v_g_norm2, v_w_ffn_in, v_w_ffn_out, v_g_final]))
    B, S, D = x.shape
    mx, my, mc = _my_place()
    dev = 4 * mx + 2 * my + mc
    chip = 2 * mx + my
    pad_rows = 8

    early = ("w_in", "w_uq", "w_ukv")
    bias, got = _bias_tables(rel_bias, _bucket_tables(), "rel_bias_tables",
                             _BothComm(_AllGatherComm(jnp.pad(c, ((0, pad_rows - B), (0, 0)))),
                                       _GatherComm([W[n][0].astype(BF16) for n in early])))
    c_all = got[0][:, :B].reshape(N_DEV * B, D)

    ada_cols = w_ada.shape[-1]
    b_cols = lax.dynamic_slice_in_dim(b_ada, chip * ada_cols, ada_cols, axis=1)
    mod_cols = _ada_fwd(c_all, w_ada[0], b_cols, "ada_fwd")
    mod_all = _allgather8(mod_cols, "ag_mod", False).reshape(N_DEV, N_DEV * B, ada_cols)[0::2]
    mod_all = jnp.transpose(mod_all, (1, 0, 2)).reshape(N_DEV * B, N_MOD * D)
    mod = lax.dynamic_slice_in_dim(mod_all, dev * B, B, axis=0)

    full = {n: g.reshape((N_CHIP,) + W[n].shape[1:]) for n, g in zip(early, got[1:])}
    wts = dict(w_in=_w_in_to_kernel(_full_from_shards(full["w_in"])),
               w_uq=_w_uq_to_kernel(_full_from_shards(full["w_uq"])),
               w_kv=_w_ukv_to_kernel(_full_from_shards(full["w_ukv"])))
    gains = dict(g_norm1=g_norm1, g_cq=g_cq, g_ckv=g_ckv, g_out_a=g_out_a, g_out_b=g_out_b, g_norm2=g_norm2,
                 g_final=g_final.reshape(1, D))

    loss, grad_x, gmod, grads = _local_step(x, loss_target, mod, wts, gains, rel_bias,
                                            ffn_shards=[w_ffn_in[0].astype(BF16), w_ffn_out[0].astype(BF16),
                                                        w_out[0].astype(BF16)], bias=bias)
    loss = lax.psum(loss[0, 0], ("x", "y", "c"))

    n_small = _SMALL_PAD
    cat = lambda dct: jnp.concatenate([dct[n].reshape(1, -1) for n, _ in _SMALL]
                                      + [jnp.zeros((1, _SMALL_PAD - sum(s for _, s in _SMALL)), F32)], axis=1)
    small = cat(grads)
    rows = jnp.concatenate([gmod, jnp.pad(small, ((0, 0), (0, N_MOD * D - n_small))),
                            jnp.zeros((pad_rows - B - 1, N_MOD * D), F32)], axis=0)
    rows_all = _allgather8(rows, "ag_small", False).reshape(N_DEV, pad_rows, N_MOD * D)
    gmod_all = rows_all[:, :B].reshape(N_DEV * B, N_MOD * D)
    small_parts = rows_all[:, B, :n_small]

    a4, r2 = grads["mix_pending"]
    ffn_a4, ffn_r2 = grads["ffn_pending"]
    G = dict(zip(_SHARDED, _rs_last(list(a4) + list(ffn_a4), list(r2) + list(ffn_r2), "all")))

    gmod_cols = lax.dynamic_slice_in_dim(gmod_all, chip * ada_cols, ada_cols, axis=1)
    G["w_ada"] = _ada_bwd(c_all, gmod_cols, "ada_bwd")
    delta, new_m, new_v = {}, {}, {}
    for n in ("w_ada",) + _SHARDED:
        shp = W[n].shape
        w2 = W[n].reshape(shp[-2], shp[-1])
        d_, m_, v_ = _adamw(w2, G[n], M[n].reshape(w2.shape), V[n].reshape(w2.shape), f"adamw_{n}")
        G[n], delta[n], new_m[n], new_v[n] = [a.reshape(shp) for a in (G[n], d_, m_, v_)]
    gs, ds_, ms_, vs_ = _adamw_rows(cat(W), small_parts, cat(M), cat(V), "adamw_small")
    off = 0
    for n, sz in _SMALL:
        shp = W[n].shape
        G[n], delta[n], new_m[n], new_v[n] = [a[:, off:off + sz].reshape(shp) for a in (gs, ds_, ms_, vs_)]
        off += sz
    G["b_ada"], delta["b_ada"], new_m["b_ada"], new_v["b_ada"] = _adamw_rows(b_ada, gmod_all, m_b_ada, v_b_ada,
                                                                          "adamw_b_ada")
    return (loss, grad_x, *[G[n] for n in names], *[delta[n] for n in names], *[new_m[n] for n in names],
            *[new_v[n] for n in names])
```

```python
import functools
import math

import numpy as np
import jax
import jax.numpy as jnp
from jax import lax
from jax.experimental import pallas as pl
from jax.experimental.pallas import tpu as pltpu

F32 = jnp.float32
BF16 = jnp.bfloat16

D_MODEL = 1024
SEQ = 2048
N_HEADS = 8
HEAD_DIM = 64
D_A = 512
D_B = 512
Q_LORA = 384
KV_LORA = 256
ROPE_DIM = 32
NOPE_DIM = 64
D_FF = 2816
N_MOD = 6
N_BUCKETS = 32
MAX_DISTANCE = 2048
ROPE_THETA = 10000.0
EPS = 1e-6
NEG = -1e30
BLK = 128
DILATIONS = (1, 4, 16)
SPAN = 128
MLA_SCALE = (NOPE_DIM + ROPE_DIM) ** -0.5
DIL_SCALE = HEAD_DIM ** -0.5

ADAM_LR = 0.001
ADAM_B1 = 0.9
ADAM_B2 = 0.999
ADAM_EPS = 1e-08
ADAM_WD = 0.01
ADAM_STEP = 10

N_DEV = 8
N_CHIP = 4
LANES = 128
VMEM_LIMIT = 48 * 1024 * 1024
MM_VMEM_BUDGET = 32 * 1024 * 1024

P_QKV = 3 * D_A
P_REST = KV_LORA + LANES + Q_LORA


def _cparams(sem=None):
    return pltpu.CompilerParams(dimension_semantics=sem, vmem_limit_bytes=VMEM_LIMIT)


def _pick(n, cands):
    for c in cands:
        if n % c == 0:
            return c
    raise ValueError(f"no tile for {n} in {cands}")


def _mm(a, b, mode, out_dtype, name, col_blocks=None, comm=None, halves=False):
    blocked = col_blocks is not None
    if mode == "nn":
        (M, K) = a.shape
        K2, N = (b.shape[1], b.shape[0] * b.shape[2]) if blocked else b.shape
    elif mode == "nt":
        (M, K) = (a.shape[1], 2 * a.shape[2]) if halves else a.shape
        N, K2 = (b.shape[1], b.shape[0] * b.shape[2]) if blocked else b.shape
    else:
        (K, M) = a.shape
        K2, N = (b.shape[1], 2 * b.shape[2]) if halves else b.shape
    assert K == K2, (a.shape, b.shape, mode)
    assert not halves or (blocked and col_blocks == 4 and mode in ("nt", "tn"))
    tn = _pick(N, (1408, 1024, 768, 512, 384, 256, 128))
    tk = _pick(K, (1408, 1152, 1024, 768, 512, 384, 256, 128))
    if blocked and mode == "nt":
        tk = K // col_blocks
    elif blocked:
        tn = N // col_blocks
    nk = K // tk

    def vmem_bytes(tm_):
        tiles = tm_ * tk * a.dtype.itemsize + tk * tn * b.dtype.itemsize + tm_ * tn * jnp.dtype(out_dtype).itemsize
        return 2 * tiles + tm_ * tn * 4

    tm = next(t for t in (1408, 1024, 512, 384, 256, 128) if M % t == 0 and vmem_bytes(t) <= MM_VMEM_BUDGET)
    out_shape = (M, N)
    out_spec = pl.BlockSpec((tm, tn), lambda i, j, k: (i, j))
    if mode == "nn":
        a_spec = pl.BlockSpec((tm, tk), lambda i, j, k: (i, k))
        b_spec = (pl.BlockSpec((None, tk, tn), lambda i, j, k: (j, k, 0)) if blocked
                  else pl.BlockSpec((tk, tn), lambda i, j, k: (k, j)))
        dn = (((1,), (0,)), ((), ()))
    elif mode == "nt":
        a_spec = (pl.BlockSpec((None, tm, tk), lambda i, j, k: (k // 2, i, k % 2)) if halves
                  else pl.BlockSpec((tm, tk), lambda i, j, k: (i, k)))
        b_spec = (pl.BlockSpec((None, tn, tk), lambda i, j, k: (k, j, 0)) if blocked
                  else pl.BlockSpec((tn, tk), lambda i, j, k: (j, k)))
        dn = (((1,), (1,)), ((), ()))
    else:
        a_spec = pl.BlockSpec((tk, tm), lambda i, j, k: (k, i))
        b_spec = (pl.BlockSpec((None, tk, tn), lambda i, j, k: (j // 2, k, j % 2)) if halves
                  else pl.BlockSpec((tk, tn), lambda i, j, k: (k, j)))
        dn = (((0,), (0,)), ((), ()))
        if blocked:
            out_shape = (col_blocks, M, tn)
            out_spec = pl.BlockSpec((None, tm, tn), lambda i, j, k: (j, i, 0))

    def body(a_ref, b_ref, o_ref, acc_ref):
        k = pl.program_id(2)

        @pl.when(k == 0)
        def _():
            acc_ref[...] = jnp.zeros_like(acc_ref)

        acc_ref[...] += lax.dot_general(a_ref[...].astype(BF16), b_ref[...].astype(BF16), dn,
                                        preferred_element_type=F32)

        @pl.when(k == nk - 1)
        def _():
            o_ref[...] = acc_ref[...].astype(o_ref.dtype)

    if comm is not None:
        (out,), got = _host_call(
            body, comm, name=name, out_shape=[jax.ShapeDtypeStruct(out_shape, out_dtype)],
            grid=(M // tm, N // tn, nk), in_specs=[a_spec, b_spec], out_specs=[out_spec],
            scratch_shapes=[pltpu.VMEM((tm, tn), F32)], args=(a, b))
        return out, got
    return pl.pallas_call(
        body, name=name,
        out_shape=jax.ShapeDtypeStruct(out_shape, out_dtype),
        grid=(M // tm, N // tn, nk),
        in_specs=[a_spec, b_spec],
        out_specs=out_spec,
        scratch_shapes=[pltpu.VMEM((tm, tn), F32)],
        compiler_params=_cparams(("parallel", "parallel", "arbitrary")),
    )(a, b)


ROW_TILE = 512


def _adaln_fwd(x, g, sc, sh, name, mix=None, gate=None):
    B, S, D = x.shape
    ts = ROW_TILE
    has_res = mix is not None

    def body(*refs):
        if has_res:
            x_ref, g_ref, sc_ref, sh_ref, mix_ref, gate_ref, h_ref, xr_ref = refs
            xr = x_ref[0] + gate_ref[0] * mix_ref[0]
            xr_ref[0] = xr
        else:
            x_ref, g_ref, sc_ref, sh_ref, h_ref = refs
            xr = x_ref[0]
        r = lax.rsqrt(jnp.mean(xr * xr, axis=-1, keepdims=True) + EPS)
        xn = (xr * r) * g_ref[...]
        h_ref[0] = (xn * (1.0 + sc_ref[0]) + sh_ref[0]).astype(h_ref.dtype)

    tok = pl.BlockSpec((1, ts, D), lambda b, s: (b, s, 0))
    per_b = pl.BlockSpec((1, 1, D), lambda b, s: (b, 0, 0))
    vec = pl.BlockSpec((1, D), lambda b, s: (0, 0))
    in_specs = [tok, vec, per_b, per_b]
    args = [x, g, sc, sh]
    out_shape = [jax.ShapeDtypeStruct((B, S, D), BF16)]
    out_specs = [tok]
    if has_res:
        in_specs += [tok, per_b]
        args += [mix, gate]
        out_shape.append(jax.ShapeDtypeStruct((B, S, D), F32))
        out_specs.append(tok)
    out = pl.pallas_call(
        body, name=name, out_shape=out_shape, grid=(B, S // ts),
        in_specs=in_specs, out_specs=out_specs,
        compiler_params=_cparams(("parallel", "parallel")),
    )(*args)
    return out if has_res else out[0]


def _adaln_bwd(dh, x, g, sc, dres, name, mix=None, gate=None, comm=None):
    B, S, D = x.shape
    ts = ROW_TILE
    has_res = mix is not None

    def body(*refs):
        if has_res:
            (dh_ref, x_ref, g_ref, sc_ref, dres_ref, mix_ref, gate_ref,
             dx_ref, dsh_ref, dsc_ref, dg_ref, dgate_ref, dmix_ref) = refs
        else:
            (dh_ref, x_ref, g_ref, sc_ref, dres_ref, dx_ref, dsh_ref, dsc_ref, dg_ref) = refs
        b, s = pl.program_id(0), pl.program_id(1)
        xv = x_ref[0]
        dhv = dh_ref[0]
        gv = g_ref[...]
        r = lax.rsqrt(jnp.mean(xv * xv, axis=-1, keepdims=True) + EPS)
        n = xv * r
        xn = n * gv
        dxn = dhv * (1.0 + sc_ref[0])
        dn = dxn * gv
        dx = r * (dn - n * jnp.mean(dn * n, axis=-1, keepdims=True)) + dres_ref[0]
        dx_ref[0] = dx

        @pl.when(s == 0)
        def _():
            dsh_ref[...] = jnp.zeros_like(dsh_ref)
            dsc_ref[...] = jnp.zeros_like(dsc_ref)
            if has_res:
                dgate_ref[...] = jnp.zeros_like(dgate_ref)

        @pl.when((s == 0) & (b == 0))
        def _():
            dg_ref[...] = jnp.zeros_like(dg_ref)

        dsh_ref[0] += jnp.sum(dhv, axis=0, keepdims=True)
        dsc_ref[0] += jnp.sum(dhv * xn, axis=0, keepdims=True)
        dg_ref[...] += jnp.sum(dxn * n, axis=0, keepdims=True)
        if has_res:
            dgate_ref[0] += jnp.sum(dx * mix_ref[0], axis=0, keepdims=True)
            dmix_ref[0] = (dx * gate_ref[0]).astype(dmix_ref.dtype)

    tok = pl.BlockSpec((1, ts, D), lambda b, s: (b, s, 0))
    per_b = pl.BlockSpec((1, 1, D), lambda b, s: (b, 0, 0))
    vec = pl.BlockSpec((1, D), lambda b, s: (0, 0))
    in_specs = [tok, tok, vec, per_b, tok]
    args = [dh, x, g, sc, dres]
    out_shape = [jax.ShapeDtypeStruct((B, S, D), F32), jax.ShapeDtypeStruct((B, 1, D), F32),
                 jax.ShapeDtypeStruct((B, 1, D), F32), jax.ShapeDtypeStruct((1, D), F32)]
    out_specs = [tok, per_b, per_b, vec]
    if has_res:
        in_specs += [tok, per_b]
        args += [mix, gate]
        out_shape += [jax.ShapeDtypeStruct((B, 1, D), F32), jax.ShapeDtypeStruct((B, S, D), BF16)]
        out_specs += [per_b, tok]
    res, got = _host_call(body, comm, name=name, out_shape=out_shape, grid=(B, S // ts), in_specs=in_specs,
                          out_specs=out_specs, scratch_shapes=[], args=args)
    return (list(res) + [got]) if comm is not None else res


def _rms_fwd_pair(xa, xb, ga, gb, name):
    T, na = xa.shape
    nb = xb.shape[1]
    tr = 512

    def body(xa_ref, xb_ref, ga_ref, gb_ref, y_ref):
        for x_ref, g_ref, lo, n in ((xa_ref, ga_ref, 0, na), (xb_ref, gb_ref, na, nb)):
            xv = x_ref[...]
            r = lax.rsqrt(jnp.mean(xv * xv, axis=-1, keepdims=True) + EPS)
            y_ref[:, lo:lo + n] = ((xv * r) * g_ref[...]).astype(y_ref.dtype)

    row = lambda n: pl.BlockSpec((tr, n), lambda i: (i, 0))
    vec = lambda n: pl.BlockSpec((1, n), lambda i: (0, 0))
    return pl.pallas_call(
        body, name=name, out_shape=jax.ShapeDtypeStruct((T, na + nb), BF16), grid=(T // tr,),
        in_specs=[row(na), row(nb), vec(na), vec(nb)], out_specs=row(na + nb),
        compiler_params=_cparams(("parallel",)),
    )(xa, xb, ga, gb)


def _rms_bwd_views(dy, dy_blk, x, g, name):
    B, S, n = x.shape
    tiles = S // VIEW_TILE

    def body(dy_ref, x_ref, g_ref, d1_ref, d4_ref, d16_ref, dg_ref, dx_s):
        xv = x_ref[0]
        dyv = dy_ref[...]
        r = lax.rsqrt(jnp.mean(xv * xv, axis=-1, keepdims=True) + EPS)
        nrm = xv * r
        dn = dyv * g_ref[...]
        dx = r * (dn - nrm * jnp.mean(dn * nrm, axis=-1, keepdims=True))
        d1_ref[0] = dx.astype(d1_ref.dtype)
        _put_tile(dx_s, dx)
        _tile_to_view(dx_s, d4_ref, DILATIONS[1], n)
        _tile_to_view(dx_s, d16_ref, DILATIONS[2], n)

        @pl.when((pl.program_id(0) == 0) & (pl.program_id(1) == 0))
        def _():
            dg_ref[...] = jnp.zeros_like(dg_ref)

        dg_ref[...] += jnp.sum(dyv * nrm, axis=0, keepdims=True)

    res = pl.pallas_call(
        body, name=name,
        out_shape=[_view_shape(B, S, d, n, BF16) for d in DILATIONS] + [jax.ShapeDtypeStruct((1, n), F32)],
        grid=(B, tiles),
        in_specs=[pl.BlockSpec((VIEW_TILE, n), lambda b, t: (b * tiles + t, dy_blk)), _view_spec(1, n),
                  pl.BlockSpec((1, n), lambda b, t: (0, 0))],
        out_specs=[_view_spec(d, n) for d in DILATIONS] + [pl.BlockSpec((1, n), lambda b, t: (0, 0))],
        scratch_shapes=[_tile_scratch(n)],
        compiler_params=_cparams(("arbitrary", "arbitrary")),
    )(dy, x, g)
    return res[:len(DILATIONS)], res[len(DILATIONS)]


FFN_TILE = 1408


def _ffn_in_fwd(h, w4, name):
    T, D = h.shape
    tm, tc = 512, FFN_TILE
    nc = D_FF // tc

    def body(h_ref, wg_ref, wu_ref, gu_ref, act_ref):
        hv = h_ref[...]
        g = jnp.dot(hv, wg_ref[...], preferred_element_type=F32)
        u = jnp.dot(hv, wu_ref[...], preferred_element_type=F32)
        gu_ref[0] = g.astype(gu_ref.dtype)
        gu_ref[1] = u.astype(gu_ref.dtype)
        act_ref[...] = (g * jax.nn.sigmoid(g) * u).astype(act_ref.dtype)

    return pl.pallas_call(
        body, name=name,
        out_shape=[jax.ShapeDtypeStruct((2, T, D_FF), BF16), jax.ShapeDtypeStruct((T, D_FF), BF16)],
        grid=(nc, T // tm),
        in_specs=[pl.BlockSpec((tm, D), lambda j, i: (i, 0)),
                  pl.BlockSpec((None, D, tc), lambda j, i: (j, 0, 0)),
                  pl.BlockSpec((None, D, tc), lambda j, i: (j + nc, 0, 0))],
        out_specs=[pl.BlockSpec((2, tm, tc), lambda j, i: (0, i, j)), pl.BlockSpec((tm, tc), lambda j, i: (i, j))],
        compiler_params=_cparams(("parallel", "parallel")),
    )(h, w4, w4)


def _ffn_out_bwd(df, w_out, gu, name):
    T, D = df.shape
    tm, tc = 512, FFN_TILE

    def body(df_ref, w_ref, gu_ref, dgu_ref):
        da = _dot_nt(df_ref[...], w_ref[...])
        g, u = gu_ref[0].astype(F32), gu_ref[1].astype(F32)
        sg = jax.nn.sigmoid(g)
        dgu_ref[0] = (da * u * (sg * (1.0 + g * (1.0 - sg)))).astype(dgu_ref.dtype)
        dgu_ref[1] = (da * (g * sg)).astype(dgu_ref.dtype)

    halves = pl.BlockSpec((2, tm, tc), lambda j, i: (0, i, j))
    return pl.pallas_call(
        body, name=name, out_shape=jax.ShapeDtypeStruct((2, T, D_FF), BF16), grid=(D_FF // tc, T // tm),
        in_specs=[pl.BlockSpec((tm, D), lambda j, i: (i, 0)), pl.BlockSpec((tc, D), lambda j, i: (j, 0)), halves],
        out_specs=halves,
        compiler_params=_cparams(("parallel", "parallel")),
    )(df, w_out, gu)


def _final_loss(x1, f, g2, gf, target, name):
    B, S, D = x1.shape
    ts = ROW_TILE

    def body(x1_ref, f_ref, g2_ref, gf_ref, t_ref, dx_ref, df_ref, dg2_ref, dgf_ref, loss_ref):
        b, s = pl.program_id(0), pl.program_id(1)
        fv = f_ref[0]
        g2v = g2_ref[0]
        gfv = gf_ref[...]
        x2 = x1_ref[0] + g2v * fv
        r = lax.rsqrt(jnp.mean(x2 * x2, axis=-1, keepdims=True) + EPS)
        n = x2 * r
        e = n * gfv - t_ref[0]
        dy = e * (1.0 / D)
        dn = dy * gfv
        dx = r * (dn - n * jnp.mean(dn * n, axis=-1, keepdims=True))
        dx_ref[0] = dx
        df_ref[0] = (dx * g2v).astype(df_ref.dtype)

        @pl.when(s == 0)
        def _():
            dg2_ref[...] = jnp.zeros_like(dg2_ref)

        @pl.when((s == 0) & (b == 0))
        def _():
            dgf_ref[...] = jnp.zeros_like(dgf_ref)
            loss_ref[...] = jnp.zeros_like(loss_ref)

        dg2_ref[0] += jnp.sum(dx * fv, axis=0, keepdims=True)
        dgf_ref[...] += jnp.sum(dy * n, axis=0, keepdims=True)
        loss_ref[...] += 0.5 * jnp.sum(jnp.mean(e * e, axis=-1, keepdims=True), axis=0, keepdims=True)

    tok = pl.BlockSpec((1, ts, D), lambda b, s: (b, s, 0))
    per_b = pl.BlockSpec((1, 1, D), lambda b, s: (b, 0, 0))
    vec = pl.BlockSpec((1, D), lambda b, s: (0, 0))
    return pl.pallas_call(
        body, name=name,
        out_shape=[jax.ShapeDtypeStruct((B, S, D), F32), jax.ShapeDtypeStruct((B, S, D), BF16),
                   jax.ShapeDtypeStruct((B, 1, D), F32), jax.ShapeDtypeStruct((1, D), F32),
                   jax.ShapeDtypeStruct((1, LANES), F32)],
        grid=(B, S // ts),
        in_specs=[tok, tok, per_b, vec, tok],
        out_specs=[tok, tok, per_b, vec, pl.BlockSpec((1, LANES), lambda b, s: (0, 0))],
        compiler_params=_cparams(("arbitrary", "arbitrary")),
    )(x1, f, g2, gf, target)


def _rope_tables():
    half = ROPE_DIM // 2
    inv = ROPE_THETA ** (-jnp.arange(half, dtype=F32) / half)
    ang = jnp.arange(SEQ, dtype=F32)[:, None] * inv[None, :]
    cos, sin = jnp.cos(ang), jnp.sin(ang)
    one = jnp.ones((SEQ, NOPE_DIM), F32)
    zero = jnp.zeros((SEQ, NOPE_DIM), F32)
    cs = jnp.concatenate([one, cos, cos, one[:, :LANES - NOPE_DIM - ROPE_DIM]], axis=1)
    sn = jnp.concatenate([zero, -sin, sin, zero[:, :LANES - NOPE_DIM - ROPE_DIM]], axis=1)
    return cs, sn


def _rope_group(t, cs, sn):
    half = ROPE_DIM // 2
    lane = lax.broadcasted_iota(jnp.int32, t.shape, 1)
    partner = jnp.where(lane < NOPE_DIM + half, pltpu.roll(t, LANES - half, 1), pltpu.roll(t, half, 1))
    return t * cs + partner * sn


def _mla_proj(cqn, ckvn, rest, w_uq, w_kv, cs, sn, name):
    B, S, _ = rest.shape
    ts, tk = ROW_TILE, MLA_TK
    tiles = S // ts
    G = N_HEADS
    kw = G * LANES
    npair = N_HEADS // 2

    def body(cq_ref, ckv_ref, r_ref, wq_ref, wkv_ref, cs_ref, sn_ref, q_ref, k_ref, v_ref, vt_ref):
        csv, snv = cs_ref[...], sn_ref[...]
        q_raw = jnp.dot(cq_ref[...], wq_ref[...], preferred_element_type=F32)
        kv = jnp.dot(ckv_ref[...], wkv_ref[...], preferred_element_type=F32)
        ra = _rope_group(r_ref[0], csv, snv)
        for gi in range(G):
            sl = slice(gi * LANES, (gi + 1) * LANES)
            q_ref[0, :, sl] = _rope_group(q_raw[:, sl], csv, snv).astype(q_ref.dtype)
            k_ref[0, :, sl] = (kv[:, sl] + ra).astype(k_ref.dtype)
        v = kv[:, kw:]
        v_ref[0] = v.astype(v_ref.dtype)
        for p in range(npair):
            for s in range(ts // tk):
                vt_ref[0, p, s] = jnp.transpose(v[s * tk:(s + 1) * tk, p * LANES:(p + 1) * LANES]).astype(vt_ref.dtype)

    rows = lambda n: pl.BlockSpec((ts, n), lambda b, t: (b * tiles + t, 0))
    full = lambda a: pl.BlockSpec(a.shape, lambda b, t: (0, 0))
    tab = pl.BlockSpec((ts, LANES), lambda b, t: (t, 0))
    tok = lambda n: pl.BlockSpec((1, ts, n), lambda b, t: (b, t, 0))
    return pl.pallas_call(
        body, name=name,
        out_shape=[jax.ShapeDtypeStruct((B, S, kw), BF16), jax.ShapeDtypeStruct((B, S, kw), BF16),
                   jax.ShapeDtypeStruct((B, S, D_B), BF16),
                   jax.ShapeDtypeStruct((B, npair, S // tk, LANES, tk), BF16)],
        grid=(B, tiles),
        in_specs=[rows(Q_LORA), rows(KV_LORA), pl.BlockSpec((1, ts, LANES), lambda b, t: (b, t, KV_LORA // LANES)),
                  full(w_uq), full(w_kv), tab, tab],
        out_specs=[tok(kw), tok(kw), tok(D_B),
                   pl.BlockSpec((1, npair, ts // tk, LANES, tk), lambda b, t: (b, 0, t, 0, 0))],
        compiler_params=_cparams(("parallel", "parallel")),
    )(cqn, ckvn, rest, w_uq, w_kv, cs, sn)


def _mla_proj_bwd(dq_t, dkc, dv, cqn, ckvn, rest, w_uq, w_kv, g_cq, g_ckv, cs, sn_neg, name):
    B, S, _ = rest.shape
    npair, tq = dq_t.shape[1], dq_t.shape[-1]
    ts = ROW_TILE
    tiles = S // ts
    kw = N_HEADS * LANES
    cq_lo = KV_LORA + LANES

    def rms_bwd(dy, xv, g_ref):
        r = lax.rsqrt(jnp.mean(xv * xv, axis=-1, keepdims=True) + EPS)
        nrm = xv * r
        dn = dy * g_ref[...]
        return r * (dn - nrm * jnp.mean(dn * nrm, axis=-1, keepdims=True)), jnp.sum(dy * nrm, axis=0, keepdims=True)

    def body(dqt_ref, dk_ref, dv_ref, cq_ref, ckv_ref, r_ref, wq_ref, wkv_ref, gcq_ref, gckv_ref, cs_ref, sn_ref,
             dr_ref, gwq_ref, gwkv_ref, dgcq_ref, dgckv_ref, dq_s):
        @pl.when((pl.program_id(0) == 0) & (pl.program_id(1) == 0))
        def _():
            for ref in (gwq_ref, gwkv_ref, dgcq_ref, dgckv_ref):
                ref[...] = jnp.zeros_like(ref)

        csv, snv = cs_ref[...], sn_ref[...]
        for p in range(npair):
            for s in range(ts // tq):
                tile = jnp.transpose(dqt_ref[0, p, s])
                rows = slice(s * tq, (s + 1) * tq)
                for hh in range(2):
                    lo = (2 * p + hh) * LANES
                    dq_s[rows, lo:lo + LANES] = _rope_group(tile[:, hh * LANES:(hh + 1) * LANES], csv[rows],
                                                            snv[rows]).astype(dq_s.dtype)
        dq_raw = dq_s[...]
        dkcv = dk_ref[0]
        dkv = jnp.concatenate([dkcv, dv_ref[0]], axis=1).astype(BF16)
        cqn, ckvn = cq_ref[...], ckv_ref[...]
        gwq_ref[...] += _dot_tn(cqn, dq_raw)
        gwkv_ref[...] += _dot_tn(ckvn, dkv)
        restv = r_ref[0]
        dcq, dg = rms_bwd(_dot_nt(dq_raw, wq_ref[...]), restv[:, cq_lo:], gcq_ref)
        dgcq_ref[...] += dg
        dckv, dg = rms_bwd(_dot_nt(dkv, wkv_ref[...]), restv[:, :KV_LORA], gckv_ref)
        dgckv_ref[...] += dg
        acc = dkcv[:, 0:LANES]
        for gi in range(1, N_HEADS):
            acc = acc + dkcv[:, gi * LANES:(gi + 1) * LANES]
        lane = lax.broadcasted_iota(jnp.int32, acc.shape, 1)
        acc = jnp.where((lane >= NOPE_DIM) & (lane < NOPE_DIM + ROPE_DIM), acc, 0.0)
        dr_ref[0, :, 0:KV_LORA] = dckv.astype(dr_ref.dtype)
        dr_ref[0, :, KV_LORA:cq_lo] = _rope_group(acc, csv, snv).astype(dr_ref.dtype)
        dr_ref[0, :, cq_lo:] = dcq.astype(dr_ref.dtype)

    rows = lambda n: pl.BlockSpec((ts, n), lambda b, t: (b * tiles + t, 0))
    full = lambda a: pl.BlockSpec(a.shape, lambda b, t: (0, 0))
    tab = pl.BlockSpec((ts, LANES), lambda b, t: (t, 0))
    tok = lambda n: pl.BlockSpec((1, ts, n), lambda b, t: (b, t, 0))
    acc_out = lambda shp: pl.BlockSpec(shp, lambda b, t: (0, 0))
    out_shape = [jax.ShapeDtypeStruct((B, S, P_REST), BF16), jax.ShapeDtypeStruct(w_uq.shape, F32),
                 jax.ShapeDtypeStruct(w_kv.shape, F32), jax.ShapeDtypeStruct((1, Q_LORA), F32),
                 jax.ShapeDtypeStruct((1, KV_LORA), F32)]
    return pl.pallas_call(
        body, name=name, out_shape=out_shape, grid=(B, tiles),
        in_specs=[pl.BlockSpec((1, npair, ts // tq, 2 * LANES, tq), lambda b, t: (b, 0, t, 0, 0)), tok(kw),
                  tok(D_B), rows(Q_LORA), rows(KV_LORA), tok(P_REST), full(w_uq), full(w_kv), full(g_cq),
                  full(g_ckv), tab, tab],
        out_specs=[tok(P_REST)] + [acc_out(o.shape) for o in out_shape[1:]],
        scratch_shapes=[pltpu.VMEM((ts, kw), BF16)],
        compiler_params=_cparams(("arbitrary", "arbitrary")),
    )(dq_t, dkc, dv, cqn, ckvn, rest, w_uq, w_kv, g_cq, g_ckv, cs, sn_neg)


def _t5_bucket(dist):
    max_exact = N_BUCKETS // 2
    d = np.maximum(dist, 1).astype(np.float64)
    large = max_exact + (np.log(d / max_exact) / np.log(MAX_DISTANCE / max_exact)
                         * (N_BUCKETS - max_exact)).astype(np.int64)
    large = np.minimum(large, N_BUCKETS - 1)
    return np.where(dist < max_exact, dist, large).astype(np.int32)


def _band_buckets(dilation):
    a = np.arange(BLK)[None, :]
    bk = np.arange(2 * BLK)[:, None]
    steps = BLK + a - bk
    return _t5_bucket(np.clip(steps, 0, SPAN) * dilation)


def _head_mask(shape, hh):
    lane = lax.broadcasted_iota(jnp.int32, shape, 1)
    return (lane >= hh * HEAD_DIM) & (lane < (hh + 1) * HEAD_DIM)


def _dot_nt(a, b):
    return lax.dot_general(a, b, (((1,), (1,)), ((), ())), preferred_element_type=F32)


def _dot_tn(a, b):
    return lax.dot_general(a, b, (((0,), (0,)), ((), ())), preferred_element_type=F32)


def _dot_nn(a, b):
    return lax.dot_general(a, b, (((1,), (0,)), ((), ())), preferred_element_type=F32)


def _dil_fwd(qkv, bias, branch, dilation, name, comm=None):
    B, n, _ = qkv.shape
    d = dilation
    nb = n // BLK
    qkv_v = qkv
    npair = N_HEADS // 2

    def body(cur_ref, prev_ref, bias_ref, o_ref, lse_ref, s_scr, e_scr):
        first = jnp.where(pl.program_id(1) == 0, 1, 0)
        units = [(b, h) for b in range(B) for h in range(N_HEADS)]
        for b in range(B):
            for p in range(npair):
                q = cur_ref[b, :, p * LANES:(p + 1) * LANES] * DIL_SCALE
                kc = cur_ref[b, :, D_A + p * LANES:D_A + (p + 1) * LANES]
                kp = prev_ref[b, :, D_A + p * LANES:D_A + (p + 1) * LANES]
                for hh in range(2):
                    u = b * N_HEADS + 2 * p + hh
                    qm = jnp.where(_head_mask((BLK, LANES), hh), q, jnp.zeros_like(q))
                    s_scr[u, 0:BLK, :] = _dot_nt(kp, qm)
                    s_scr[u, BLK:2 * BLK, :] = _dot_nt(kc, qm)
        ms = []
        for u, (b, h) in enumerate(units):
            s_p = s_scr[u, 0:BLK, :] + bias_ref[first, h, 0:BLK, :]
            s_c = s_scr[u, BLK:2 * BLK, :] + bias_ref[first, h, BLK:2 * BLK, :]
            m = jnp.maximum(jnp.max(s_p, axis=0, keepdims=True), jnp.max(s_c, axis=0, keepdims=True))
            e_scr[u, 0:BLK, :] = jnp.exp(s_p - m).astype(BF16)
            e_scr[u, BLK:2 * BLK, :] = jnp.exp(s_c - m).astype(BF16)
            ms.append(m)
        rows0 = _row_mask((LANES, BLK), 0)
        for b in range(B):
            for p in range(npair):
                sl = slice(p * LANES, (p + 1) * LANES)
                vsl = slice(2 * D_A + p * LANES, 2 * D_A + (p + 1) * LANES)
                vct = jnp.transpose(cur_ref[b, :, vsl].astype(F32)).astype(BF16)
                vpt = jnp.transpose(prev_ref[b, :, vsl].astype(F32)).astype(BF16)
                acc = []
                for hh in range(2):
                    u = b * N_HEADS + 2 * p + hh
                    mine = _row_mask((LANES, BLK), hh)
                    one = jnp.ones_like(vct)
                    acc.append(_dot_nn(jnp.where(mine, vpt, one), e_scr[u, 0:BLK, :])
                               + _dot_nn(jnp.where(mine, vct, one), e_scr[u, BLK:2 * BLK, :]))
                l0 = acc[0][HEAD_DIM:HEAD_DIM + 1, :]
                l1 = acc[1][0:1, :]
                u0 = b * N_HEADS + 2 * p
                o_t = jnp.where(rows0, acc[0] / l0, acc[1] / l1)
                lse_t = jnp.where(rows0, ms[u0] + jnp.log(l0), ms[u0 + 1] + jnp.log(l1))
                o_ref[b, :, sl] = jnp.transpose(o_t)
                lse_ref[b, :, sl] = jnp.transpose(lse_t)

    cur = pl.BlockSpec((B, BLK, P_QKV), lambda r, i: (0, i, r))
    prev = pl.BlockSpec((B, BLK, P_QKV), lambda r, i: (0, jnp.maximum(i - 1, 0), r))
    out = pl.BlockSpec((B, BLK, D_A), lambda r, i: (0, i, r))
    return _host_call(
        body, comm, name=name,
        out_shape=[jax.ShapeDtypeStruct((B, n, d * D_A), F32)] * 2,
        grid=(d, nb),
        in_specs=[cur, prev,
                  pl.BlockSpec((None, 2, N_HEADS, 2 * BLK, BLK), lambda r, i: (branch, 0, 0, 0, 0))],
        out_specs=[out, out],
        scratch_shapes=[pltpu.VMEM((B * N_HEADS, 2 * BLK, BLK), F32),
                        pltpu.VMEM((B * N_HEADS, 2 * BLK, BLK), BF16)],
        args=(qkv_v, qkv_v, bias))


VIEW_TILE = 512


def _view_spec(d, w):
    return pl.BlockSpec((1, VIEW_TILE // d, d * w), lambda b, t: (b, t, 0))


def _view_shape(B, S, d, w, dtype):
    return jax.ShapeDtypeStruct((B, S // d, d * w), dtype)


def _tile_scratch(w):
    return pltpu.VMEM((w // LANES, VIEW_TILE, LANES), F32)


def _put_tile(tile_ref, val):
    for c in range(tile_ref.shape[0]):
        tile_ref[c] = val[:, c * LANES:(c + 1) * LANES]


def _get_tile(tile_ref):
    return jnp.concatenate([tile_ref[c] for c in range(tile_ref.shape[0])], axis=1)


def _tile_to_view(tile_ref, view_ref, d, w):
    for c in range(w // LANES):
        for r in range(d):
            lo = r * w + c * LANES
            rows = tile_ref.at[c][pl.ds(r, VIEW_TILE // d, stride=d), :]
            view_ref[0, :, lo:lo + LANES] = rows.astype(view_ref.dtype)


def _view_to_tile(view_ref, tile_ref, d, w):
    for c in range(w // LANES):
        for r in range(d):
            lo = r * w + c * LANES
            tile_ref.at[c][pl.ds(r, VIEW_TILE // d, stride=d), :] = view_ref[0, :, lo:lo + LANES].astype(F32)


def _in_proj(h, w_qkv, w_rest, g_cq, g_ckv, name):
    B, S, D = h.shape
    N = w_qkv.shape[1]
    cq_lo = KV_LORA + LANES

    def rms(xv, g_ref):
        return ((xv * lax.rsqrt(jnp.mean(xv * xv, axis=-1, keepdims=True) + EPS)) * g_ref[...]).astype(BF16)

    def body(h_ref, wq_ref, wr_ref, gcq_ref, gckv_ref, o1_ref, o4_ref, o16_ref, rest_ref, cqn_ref, ckvn_ref,
             acc_ref):
        hv = h_ref[0]
        acc = jnp.dot(hv, wq_ref[...], preferred_element_type=F32)
        o1_ref[0] = acc.astype(o1_ref.dtype)
        _put_tile(acc_ref, acc)
        _tile_to_view(acc_ref, o4_ref, DILATIONS[1], N)
        _tile_to_view(acc_ref, o16_ref, DILATIONS[2], N)
        rest = jnp.dot(hv, wr_ref[...], preferred_element_type=F32)
        rest_ref[0] = rest
        ckvn_ref[...] = rms(rest[:, :KV_LORA], gckv_ref)
        cqn_ref[...] = rms(rest[:, cq_lo:], gcq_ref)

    tiles = S // VIEW_TILE
    full = lambda a: pl.BlockSpec(a.shape, lambda b, t: (0, 0))
    rows = lambda n: pl.BlockSpec((VIEW_TILE, n), lambda b, t: (b * tiles + t, 0))
    res = pl.pallas_call(
        body, name=name,
        out_shape=[_view_shape(B, S, d, N, BF16) for d in DILATIONS]
        + [jax.ShapeDtypeStruct((B, S, P_REST), F32), jax.ShapeDtypeStruct((B * S, Q_LORA), BF16),
           jax.ShapeDtypeStruct((B * S, KV_LORA), BF16)],
        grid=(B, tiles),
        in_specs=[pl.BlockSpec((1, VIEW_TILE, D), lambda b, t: (b, t, 0)), full(w_qkv), full(w_rest), full(g_cq),
                  full(g_ckv)],
        out_specs=[_view_spec(d, N) for d in DILATIONS] + [_view_spec(1, P_REST), rows(Q_LORA), rows(KV_LORA)],
        scratch_shapes=[_tile_scratch(N)],
        compiler_params=_cparams(("parallel", "parallel")),
    )(h, w_qkv, w_rest, g_cq, g_ckv)
    return res[:len(DILATIONS)], res[len(DILATIONS)], res[len(DILATIONS) + 1], res[len(DILATIONS) + 2]


def _dil_merge(os_, lses, name):
    B, S, W = os_[0].shape
    nd = len(DILATIONS)

    def body(*refs):
        o_refs, l_refs = refs[:nd], refs[nd:2 * nd]
        out_refs, L_refs = refs[2 * nd:3 * nd], refs[3 * nd:4 * nd]
        scr = refs[4 * nd:]
        o_tok, l_tok = [o_refs[0][0]], [l_refs[0][0]]
        for i, d in enumerate(DILATIONS[1:]):
            _view_to_tile(o_refs[i + 1], scr[2 * i], d, W)
            _view_to_tile(l_refs[i + 1], scr[2 * i + 1], d, W)
            o_tok.append(_get_tile(scr[2 * i]))
            l_tok.append(_get_tile(scr[2 * i + 1]))
        a0, a1, a2 = l_tok
        m = jnp.maximum(jnp.maximum(a0, a1), a2)
        e0, e1, e2 = jnp.exp(a0 - m), jnp.exp(a1 - m), jnp.exp(a2 - m)
        ssum = e0 + e1 + e2
        out = (e0 * o_tok[0] + e1 * o_tok[1] + e2 * o_tok[2]) / ssum
        lse = m + jnp.log(ssum)
        out_refs[0][0] = out
        L_refs[0][0] = lse
        res_o, res_l = scr[2 * (nd - 1)], scr[2 * (nd - 1) + 1]
        _put_tile(res_o, out)
        _put_tile(res_l, lse)
        for i, d in enumerate(DILATIONS[1:]):
            _tile_to_view(res_o, out_refs[i + 1], d, W)
            _tile_to_view(res_l, L_refs[i + 1], d, W)

    specs = [_view_spec(d, W) for d in DILATIONS]
    shapes = [_view_shape(B, S * DILATIONS[0], d, W, F32) for d in DILATIONS]
    res = pl.pallas_call(
        body, name=name, out_shape=shapes * 2, grid=(B, S // VIEW_TILE),
        in_specs=specs * 2, out_specs=specs * 2,
        scratch_shapes=[_tile_scratch(W)] * (2 * nd),
        compiler_params=_cparams(("parallel", "parallel")),
    )(*os_, *lses)
    return res[:nd], res[nd:]


def _dil_bwd(qkv, do, out_a, L, bias, branch, dilation, name, comm=None):
    B, n, _ = qkv.shape
    d = dilation
    nb = n // BLK
    qkv_v, do_v, oa_v, L_v = qkv, do, out_a, L
    npair = N_HEADS // 2
    multi = nb > 1

    tiles = ("P", "C", "N") if multi else ("C",)
    n_t = len(tiles)

    def body(*refs):
        if multi:
            (cur_ref, prev_ref, next_ref, do_ref, don_ref, oa_ref, oan_ref, L_ref, Ln_ref, bias_ref,
             dqkv_ref, dbias_ref, s_scr, dp_scr, p_scr, ds_scr) = refs
        else:
            cur_ref, do_ref, oa_ref, L_ref, bias_ref, dqkv_ref, dbias_ref, s_scr, dp_scr, p_scr, ds_scr = refs
        r, i = pl.program_id(0), pl.program_id(1)

        @pl.when((r == 0) & (i == 0))
        def _():
            dbias_ref[...] = jnp.zeros_like(dbias_ref)

        first = jnp.where(i == 0, 1, 0)
        variant = {"P": first, "C": first, "N": 0}
        band = {"P": slice(0, BLK), "C": slice(BLK, 2 * BLK), "N": slice(0, BLK)}
        psl = lambda p: slice(p * LANES, (p + 1) * LANES)
        ksl = lambda p: slice(D_A + p * LANES, D_A + (p + 1) * LANES)
        vsl = lambda p: slice(2 * D_A + p * LANES, 2 * D_A + (p + 1) * LANES)

        def operands(b, p, hh):
            hm = _head_mask((BLK, LANES), hh)
            mask = lambda x: jnp.where(hm, x, jnp.zeros_like(x))
            qm, dom = mask(cur_ref[b, :, psl(p)] * DIL_SCALE), mask(do_ref[b, :, psl(p)])
            ops = {"C": (cur_ref[b, :, ksl(p)], cur_ref[b, :, vsl(p)], qm, dom)}
            if multi:
                ops["P"] = (prev_ref[b, :, ksl(p)], prev_ref[b, :, vsl(p)], qm, dom)
                ops["N"] = (cur_ref[b, :, ksl(p)], cur_ref[b, :, vsl(p)],
                            mask(next_ref[b, :, psl(p)] * DIL_SCALE), mask(don_ref[b, :, psl(p)]))
            return ops

        pairs = [(b, p) for b in range(B) for p in range(npair)]
        for b, p in pairs:
            for hh in range(2):
                u = b * N_HEADS + 2 * p + hh
                ops = operands(b, p, hh)
                for t, name_t in enumerate(tiles):
                    k_t, v_t, q_t, do_t = ops[name_t]
                    s_scr[u, t] = _dot_nt(k_t, q_t)
                    dp_scr[u, t] = _dot_nt(v_t, do_t)

        def rows(L_r, do_r, oa_r, b, p):
            lt = jnp.transpose(L_r[b, :, psl(p)])
            dt = jnp.transpose(do_r[b, :, psl(p)].astype(F32) * oa_r[b, :, psl(p)])
            return ([lt[0:1, :], lt[HEAD_DIM:HEAD_DIM + 1, :]],
                    [jnp.sum(dt[:HEAD_DIM], axis=0, keepdims=True), jnp.sum(dt[HEAD_DIM:], axis=0, keepdims=True)])

        for b, p in pairs:
            lse_c, delta_c = rows(L_ref, do_ref, oa_ref, b, p)
            if multi:
                lse_n, delta_n = rows(Ln_ref, don_ref, oan_ref, b, p)
            for hh in range(2):
                h = 2 * p + hh
                u = b * N_HEADS + h
                for t, name_t in enumerate(tiles):
                    lse, delta = (lse_n[hh], delta_n[hh]) if name_t == "N" else (lse_c[hh], delta_c[hh])
                    pr = jnp.exp(s_scr[u, t] + bias_ref[variant[name_t], h, band[name_t], :] - lse)
                    if name_t == "N":
                        pr = jnp.where(i < nb - 1, pr, 0.0)
                    ds = pr * (dp_scr[u, t] - delta)
                    p_scr[u, t] = pr.astype(BF16)
                    ds_scr[u, t] = ds.astype(BF16)
                    if name_t != "N":
                        dbias_ref[h, band[name_t], :] += ds

        for b, p in pairs:
            dqt = jnp.zeros((LANES, BLK), F32)
            dk = jnp.zeros((BLK, LANES), F32)
            dv = jnp.zeros((BLK, LANES), F32)
            kct = jnp.transpose(cur_ref[b, :, ksl(p)].astype(F32)).astype(BF16)
            if multi:
                kpt = jnp.transpose(prev_ref[b, :, ksl(p)].astype(F32)).astype(BF16)
            for hh in range(2):
                u = b * N_HEADS + 2 * p + hh
                ops = operands(b, p, hh)
                mine = _row_mask((LANES, BLK), hh)
                for t, name_t in enumerate(tiles):
                    _, _, q_t, do_t = ops[name_t]
                    if name_t != "P":
                        dv = dv + _dot_nn(p_scr[u, t], do_t)
                        dk = dk + _dot_nn(ds_scr[u, t], q_t)
                    if name_t != "N":
                        kt = kpt if name_t == "P" else kct
                        dqt = dqt + _dot_nn(jnp.where(mine, kt, jnp.zeros_like(kt)), ds_scr[u, t])
            dqkv_ref[b, :, psl(p)] = jnp.transpose(dqt) * DIL_SCALE
            dqkv_ref[b, :, ksl(p)] = dk
            dqkv_ref[b, :, vsl(p)] = dv

    def at(off):
        return lambda r, i: (0, jnp.clip(i + off, 0, nb - 1), r)

    qkv_spec = lambda off: pl.BlockSpec((B, BLK, P_QKV), at(off))
    da_spec = lambda off: pl.BlockSpec((B, BLK, D_A), at(off))
    bias_spec = pl.BlockSpec((None, 2, N_HEADS, 2 * BLK, BLK), lambda r, i: (branch, 0, 0, 0, 0))
    dbias_spec = pl.BlockSpec((N_HEADS, 2 * BLK, BLK), lambda r, i: (0, 0, 0))
    if multi:
        in_specs = [qkv_spec(0), qkv_spec(-1), qkv_spec(1), da_spec(0), da_spec(1), da_spec(0), da_spec(1),
                    da_spec(0), da_spec(1), bias_spec]
        args = [qkv_v, qkv_v, qkv_v, do_v, do_v, oa_v, oa_v, L_v, L_v, bias]
    else:
        in_specs = [qkv_spec(0), da_spec(0), da_spec(0), da_spec(0), bias_spec]
        args = [qkv_v, do_v, oa_v, L_v, bias]
    return _host_call(
        body, comm, name=name,
        out_shape=[jax.ShapeDtypeStruct((B, n, d * P_QKV), F32),
                   jax.ShapeDtypeStruct((N_HEADS, 2 * BLK, BLK), F32)],
        grid=(d, nb),
        in_specs=in_specs,
        out_specs=[qkv_spec(0), dbias_spec],
        scratch_shapes=[pltpu.VMEM((B * N_HEADS, n_t, BLK, BLK), F32), pltpu.VMEM((B * N_HEADS, n_t, BLK, BLK), F32),
                        pltpu.VMEM((B * N_HEADS, n_t, BLK, BLK), BF16),
                        pltpu.VMEM((B * N_HEADS, n_t, BLK, BLK), BF16)],
        args=args)


def _sum_views_bf16(parts, name):
    B, S, W = parts[0].shape

    def body(a_ref, b_ref, c_ref, o_ref, sb, sc):
        _view_to_tile(b_ref, sb, DILATIONS[1], W)
        _view_to_tile(c_ref, sc, DILATIONS[2], W)
        o_ref[0] = (a_ref[0] + _get_tile(sb) + _get_tile(sc)).astype(o_ref.dtype)

    return pl.pallas_call(
        body, name=name, out_shape=jax.ShapeDtypeStruct((B, S, W), BF16), grid=(B, S // VIEW_TILE),
        in_specs=[_view_spec(d, W) for d in DILATIONS], out_specs=_view_spec(1, W),
        scratch_shapes=[_tile_scratch(W)] * 2,
        compiler_params=_cparams(("parallel", "parallel")),
    )(*parts)


def _bias_tables(rel_bias, buckets, name, comm=None):
    nbr = buckets.shape[0]

    def body(rb_ref, bk_ref, o_ref):
        first, h = pl.program_id(1), pl.program_id(2)
        tab = bk_ref[0]

        def step(bkt, acc):
            return jnp.where(tab == bkt, rb_ref[bkt, h], acc)

        bias = lax.fori_loop(0, N_BUCKETS, step, jnp.zeros((2 * BLK, BLK), F32))
        row = lax.broadcasted_iota(jnp.int32, (2 * BLK, BLK), 0)
        col = lax.broadcasted_iota(jnp.int32, (2 * BLK, BLK), 1)
        valid = ((row < BLK) & (row >= col) & (first == 0)) | ((row >= BLK) & (row - BLK <= col))
        o_ref[0, 0, 0] = jnp.where(valid, bias, NEG)

    (bias,), got = _host_call(
        body, comm, name=name, out_shape=[jax.ShapeDtypeStruct((nbr, 2, N_HEADS, 2 * BLK, BLK), F32)],
        grid=(nbr, 2, N_HEADS),
        in_specs=[pl.BlockSpec(memory_space=pltpu.SMEM),
                  pl.BlockSpec((1, 2 * BLK, BLK), lambda i, f, h: (i, 0, 0))],
        out_specs=[pl.BlockSpec((1, 1, 1, 2 * BLK, BLK), lambda i, f, h: (i, f, h, 0, 0))],
        scratch_shapes=[], args=(rel_bias, buckets))
    return bias, got


def _bias_grad(dbias_list, buckets, name):
    nbr = len(dbias_list)

    def body(*refs):
        d_refs, bk_ref, o_ref, part = refs[:nbr], refs[nbr], refs[nbr + 1], refs[nbr + 2]

        def step(bkt, carry):
            hit = [bk_ref[bi] == bkt for bi in range(nbr)]
            for h in range(N_HEADS):
                tot = jnp.zeros((1, BLK), F32)
                for bi in range(nbr):
                    tot = tot + jnp.sum(jnp.where(hit[bi], d_refs[bi][h], 0.0), axis=0, keepdims=True)
                part[bkt, h:h + 1, :] = tot
            return carry

        lax.fori_loop(0, N_BUCKETS, step, 0)
        lane = lax.broadcasted_iota(jnp.int32, (N_HEADS, LANES), 1)
        acc = jnp.zeros((N_HEADS, LANES), F32)
        for bkt in range(N_BUCKETS):
            acc = acc + jnp.where(lane == bkt, jnp.sum(part[bkt], axis=1, keepdims=True), 0.0)
        o_ref[...] = acc

    band = pl.BlockSpec((N_HEADS, 2 * BLK, BLK), lambda i: (0, 0, 0))
    return pl.pallas_call(
        body, name=name, out_shape=jax.ShapeDtypeStruct((N_HEADS, LANES), F32), grid=(1,),
        in_specs=[band] * nbr + [pl.BlockSpec((nbr, 2 * BLK, BLK), lambda i: (0, 0, 0))],
        out_specs=pl.BlockSpec((N_HEADS, LANES), lambda i: (0, 0)),
        scratch_shapes=[pltpu.VMEM((N_BUCKETS, N_HEADS, BLK), F32)],
        compiler_params=_cparams(("arbitrary",)),
    )(*dbias_list, buckets)


MLA_TQ = 256
MLA_TK = 256


LOG2E = math.log2(math.e)
MLA_C = MLA_SCALE * LOG2E


def _key_le_query(tk, tq):
    return lax.broadcasted_iota(jnp.int32, (tk, tq), 0) <= lax.broadcasted_iota(jnp.int32, (tk, tq), 1)


def _row_mask(shape, hh):
    row = lax.broadcasted_iota(jnp.int32, shape, 0)
    return (row >= hh * HEAD_DIM) & (row < (hh + 1) * HEAD_DIM)


def _host_call(body, comm, *, name, grid, in_specs, out_specs, out_shape, scratch_shapes, args):
    sem = ("arbitrary",) * len(grid)
    if comm is None:
        res = pl.pallas_call(body, name=name, grid=grid, in_specs=in_specs, out_specs=out_specs,
                             out_shape=out_shape, scratch_shapes=scratch_shapes,
                             compiler_params=_cparams(sem))(*args)
        return res, []
    n_in, n_out, n_s, cn = len(in_specs), len(out_specs), len(scratch_shapes), comm.n

    def hosted(*refs):
        ins, refs = refs[:n_in], refs[n_in:]
        c_ins, refs = refs[:cn], refs[cn:]
        outs, refs = refs[:n_out], refs[n_out:]
        c_outs, refs = refs[:cn], refs[cn:]
        scr, c_sems = refs[:n_s], refs[n_s:]
        ids = [pl.program_id(a) for a in range(len(grid))]
        first = functools.reduce(jnp.logical_and, [i == 0 for i in ids])
        last = functools.reduce(jnp.logical_and, [i == g - 1 for i, g in zip(ids, grid)])

        @pl.when(first)
        def _():
            comm.start(c_ins, c_outs, c_sems)

        body(*ins, *outs, *scr)

        @pl.when(last)
        def _():
            comm.finish(c_ins, c_outs, c_sems)

    res = pl.pallas_call(
        hosted, name=name, grid=grid, in_specs=list(in_specs) + _hbm_specs(cn),
        out_specs=list(out_specs) + _hbm_specs(cn), out_shape=list(out_shape) + list(comm.out_shape),
        scratch_shapes=list(scratch_shapes) + list(comm.scratch), compiler_params=_cparams(sem),
    )(*args, *comm.inputs)
    return res[:n_out], res[n_out:]


def _mla_fwd_t(q, k, vt, name, comm=None):
    B, S, _ = q.shape
    tq, tk = MLA_TQ, MLA_TK
    assert tq == tk
    npair = N_HEADS // 2
    nq = S // tq

    def body(q_ref, k_ref, vt_ref, o_ref, lse_ref, s_scr, e_scr, acc_scr, m_scr, a_scr):
        i = pl.program_id(1)
        diag = _key_le_query(tk, tq)
        m_scr[...] = jnp.full_like(m_scr, NEG)
        acc_scr[...] = jnp.zeros_like(acc_scr)

        def step(j, masked):
            rows = pl.ds(pl.multiple_of(j * tk, tk), tk)
            for h in range(N_HEADS):
                hsl = slice(h * LANES, (h + 1) * LANES)
                s_scr[h] = _dot_nt(k_ref[0, rows, hsl], q_ref[0, :, hsl])
            for h in range(N_HEADS):
                s = s_scr[h]
                if masked:
                    s = jnp.where(diag, s, NEG)
                m_old = m_scr[h:h + 1, :]
                m_new = jnp.maximum(m_old, jnp.max(s, axis=0, keepdims=True))
                a_scr[h:h + 1, :] = jnp.exp2((m_old - m_new) * MLA_C)
                e_scr[h] = jnp.exp2((s - m_new) * MLA_C).astype(BF16)
                m_scr[h:h + 1, :] = m_new
            for h in range(N_HEADS):
                vj = vt_ref[0, h // 2, j]
                vh = jnp.where(_row_mask(vj.shape, h % 2), vj, jnp.ones_like(vj))
                acc_scr[h] = acc_scr[h] * a_scr[h:h + 1, :] + _dot_nn(vh, e_scr[h])

        def loop_body(j, carry):
            step(j, False)
            return carry

        lax.fori_loop(0, i, loop_body, 0)
        step(i, True)
        rows0 = _row_mask((LANES, tq), 0)
        for p in range(npair):
            l0 = acc_scr[2 * p, HEAD_DIM:HEAD_DIM + 1, :]
            l1 = acc_scr[2 * p + 1, 0:1, :]
            o_t = jnp.where(rows0, acc_scr[2 * p] / l0, acc_scr[2 * p + 1] / l1)
            o_ref[0, :, p * LANES:(p + 1) * LANES] = jnp.transpose(o_t)
            lse_ref[0, p, 0] = jnp.zeros((8, tq), F32)
            lse_ref[0, p, 0, 0:1, :] = m_scr[2 * p:2 * p + 1, :] * MLA_C + jnp.log(l0) * LOG2E
            lse_ref[0, p, 0, 1:2, :] = m_scr[2 * p + 1:2 * p + 2, :] * MLA_C + jnp.log(l1) * LOG2E

    return _host_call(
        body, comm, name=name,
        out_shape=[jax.ShapeDtypeStruct((B, S, D_B), F32), jax.ShapeDtypeStruct((B, npair, nq, 8, tq), F32)],
        grid=(B, nq),
        in_specs=[pl.BlockSpec((1, tq, N_HEADS * LANES), lambda b, i: (b, i, 0)),
                  pl.BlockSpec((1, S, N_HEADS * LANES), lambda b, i: (b, 0, 0)),
                  pl.BlockSpec((1, npair, S // tk, LANES, tk), lambda b, i: (b, 0, 0, 0, 0))],
        out_specs=[pl.BlockSpec((1, tq, D_B), lambda b, i: (b, i, 0)),
                   pl.BlockSpec((1, npair, 1, 8, tq), lambda b, i: (b, 0, i, 0, 0))],
        scratch_shapes=[pltpu.VMEM((N_HEADS, tk, tq), F32), pltpu.VMEM((N_HEADS, tk, tq), BF16),
                        pltpu.VMEM((N_HEADS, LANES, tq), F32), pltpu.VMEM((N_HEADS, tq), F32),
                        pltpu.VMEM((N_HEADS, tq), F32)],
        args=(q, k, vt))


def _rms_bwd_delta(dy, dy_blk, o, g, name):
    B, S, n = o.shape
    tq = MLA_TQ
    tr = ROW_TILE
    sub = tr // tq
    npair = N_HEADS // 2
    tiles = S // tr

    def body(dy_ref, o_ref, g_ref, do_ref, dg_ref, d_ref):
        ov = o_ref[0]
        dyv = dy_ref[...]
        r = lax.rsqrt(jnp.mean(ov * ov, axis=-1, keepdims=True) + EPS)
        nrm = ov * r
        dn = dyv * g_ref[...]
        do = (r * (dn - nrm * jnp.mean(dn * nrm, axis=-1, keepdims=True))).astype(do_ref.dtype)
        do_ref[0] = do

        @pl.when((pl.program_id(0) == 0) & (pl.program_id(1) == 0))
        def _():
            dg_ref[...] = jnp.zeros_like(dg_ref)

        dg_ref[...] += jnp.sum(dyv * nrm, axis=0, keepdims=True)
        prod = do.astype(F32) * ov
        d_ref[...] = jnp.zeros_like(d_ref)
        for p in range(npair):
            for s in range(sub):
                prod_t = jnp.transpose(prod[s * tq:(s + 1) * tq, p * LANES:(p + 1) * LANES])
                d_ref[0, p, s, 0:1, :] = jnp.sum(prod_t[:HEAD_DIM], axis=0, keepdims=True)
                d_ref[0, p, s, 1:2, :] = jnp.sum(prod_t[HEAD_DIM:], axis=0, keepdims=True)

    tok = pl.BlockSpec((1, tr, n), lambda b, t: (b, t, 0))
    return pl.pallas_call(
        body, name=name,
        out_shape=[jax.ShapeDtypeStruct((B, S, n), BF16), jax.ShapeDtypeStruct((1, n), F32),
                   jax.ShapeDtypeStruct((B, npair, S // tq, 8, tq), F32)],
        grid=(B, tiles),
        in_specs=[pl.BlockSpec((tr, n), lambda b, t: (b * tiles + t, dy_blk)), tok,
                  pl.BlockSpec((1, n), lambda b, t: (0, 0))],
        out_specs=[tok, pl.BlockSpec((1, n), lambda b, t: (0, 0)),
                   pl.BlockSpec((1, npair, sub, 8, tq), lambda b, t: (b, 0, t, 0, 0))],
        compiler_params=_cparams(("arbitrary", "arbitrary")),
    )(dy, o, g)


def _mla_bwd_t(q, k, v, do, lse, delta, name, comm=None):
    B, S, _ = q.shape
    tq, tk = MLA_TQ, MLA_TK
    assert tq == tk
    npair = N_HEADS // 2
    nq = S // tq

    hg = N_HEADS
    pg = hg // 2
    ngroup = N_HEADS // hg

    def body(q_ref, do_ref, lse_ref, dl_ref, k_ref, v_ref, dk_ref, dv_ref, dq_ref,
             s_scr, dp_scr, p_scr, ds_scr, dk_s, dv_s, kt_s):
        j = pl.program_id(2)

        @pl.when(j == 0)
        def _():
            dq_ref[...] = jnp.zeros_like(dq_ref)

        dk_s[...] = jnp.zeros_like(dk_s)
        dv_s[...] = jnp.zeros_like(dv_s)
        diag = _key_le_query(tk, tq)
        hsl = lambda h: slice(h * LANES, (h + 1) * LANES)
        for h in range(hg):
            kt_s[h] = jnp.transpose(k_ref[0, :, hsl(h)].astype(F32)).astype(BF16)

        def step(i, masked):
            rows = pl.ds(pl.multiple_of(i * tq, tq), tq)

            def dom(h):
                dov = do_ref[0, rows, hsl(h // 2)]
                return jnp.where(_head_mask((tq, LANES), h % 2), dov, jnp.zeros_like(dov))

            for h in range(hg):
                s_scr[h] = _dot_nt(k_ref[0, :, hsl(h)], q_ref[0, rows, hsl(h)])
                dp_scr[h] = _dot_nt(v_ref[0, :, hsl(h // 2)], dom(h))
            for h in range(hg):
                pr = jnp.exp2(s_scr[h] * MLA_C - lse_ref[0, h // 2, i, h % 2:h % 2 + 1, :])
                if masked:
                    pr = jnp.where(diag, pr, 0.0)
                p_scr[h] = pr.astype(BF16)
                ds_scr[h] = (pr * (dp_scr[h] - dl_ref[0, h // 2, i, h % 2:h % 2 + 1, :])).astype(BF16)
            for h in range(hg):
                dv_s[h // 2] += _dot_nn(p_scr[h], dom(h))
                dk_s[h] += _dot_nn(ds_scr[h], q_ref[0, rows, hsl(h)])
                dq_ref[0, h // 2, i, hsl(h % 2), :] += _dot_nn(kt_s[h], ds_scr[h]) * MLA_SCALE

        step(j, True)

        def loop_body(i, carry):
            step(i, False)
            return carry

        lax.fori_loop(j + 1, nq, loop_body, 0)
        for h in range(hg):
            dk_ref[0, :, hsl(h)] = dk_s[h] * MLA_SCALE
        for p in range(pg):
            dv_ref[0, :, hsl(p)] = dv_s[p]

    stat = pl.BlockSpec((1, pg, nq, 8, tq), lambda b, g, j: (b, g, 0, 0, 0))
    return _host_call(
        body, comm, name=name,
        out_shape=[jax.ShapeDtypeStruct((B, S, N_HEADS * LANES), F32), jax.ShapeDtypeStruct((B, S, D_B), F32),
                   jax.ShapeDtypeStruct((B, npair, nq, 2 * LANES, tq), F32)],
        grid=(B, ngroup, S // tk),
        in_specs=[pl.BlockSpec((1, S, hg * LANES), lambda b, g, j: (b, 0, g)),
                  pl.BlockSpec((1, S, pg * LANES), lambda b, g, j: (b, 0, g)),
                  stat, stat,
                  pl.BlockSpec((1, tk, hg * LANES), lambda b, g, j: (b, j, g)),
                  pl.BlockSpec((1, tk, pg * LANES), lambda b, g, j: (b, j, g))],
        out_specs=[pl.BlockSpec((1, tk, hg * LANES), lambda b, g, j: (b, j, g)),
                   pl.BlockSpec((1, tk, pg * LANES), lambda b, g, j: (b, j, g)),
                   pl.BlockSpec((1, pg, nq, 2 * LANES, tq), lambda b, g, j: (b, g, 0, 0, 0))],
        scratch_shapes=[pltpu.VMEM((hg, tk, tq), F32), pltpu.VMEM((hg, tk, tq), F32),
                        pltpu.VMEM((hg, tk, tq), BF16), pltpu.VMEM((hg, tk, tq), BF16),
                        pltpu.VMEM((hg, tk, LANES), F32), pltpu.VMEM((pg, tk, LANES), F32),
                        pltpu.VMEM((hg, LANES, tk), BF16)],
        args=(q, do, lse, delta, k, v))


def _bucket_tables():
    return jnp.asarray(np.stack([_band_buckets(d) for d in DILATIONS]))


def _local_step(x, target, mod, wts, gains, rel_bias, ffn_shards=None, bias=None):
    B, S, D = x.shape
    T = B * S
    sh1, sc1, g1, sh2, sc2, g2 = [mod[:, i * D:(i + 1) * D].reshape(B, 1, D) for i in range(N_MOD)]
    cs, sn = _rope_tables()
    buckets_dev = _bucket_tables()
    if bias is None:
        bias, _ = _bias_tables(rel_bias, buckets_dev, "rel_bias_tables")
    w_in = wts["w_in"]

    h1 = _adaln_fwd(x, gains["g_norm1"], sc1, sh1, "adaln1_fwd")
    h1f = h1.reshape(T, D)
    qkv_v, rest3, cqn, ckvn = _in_proj(h1, w_in[:, :P_QKV], w_in[:, P_QKV:], gains["g_cq"], gains["g_ckv"],
                                       "mm_in")
    o_d, lse_d = [], []
    late_got = []
    for i, d in enumerate(DILATIONS):
        comm = _GatherComm(ffn_shards[i + 1:i + 2]) if (ffn_shards and i < 2) else None
        (o_i, lse_i), got = _dil_fwd(qkv_v[i], bias, i, d, f"dil_fwd_{d}", comm)
        late_got += list(got)
        o_d.append(o_i)
        lse_d.append(lse_i)
    if ffn_shards:
        wts = dict(wts, w_out=late_got[1].reshape(D, D))
    out_a_v, lse_a_v = _dil_merge(o_d, lse_d, "dil_merge")
    out_a = out_a_v[0]
    qc, kc, v, vt = _mla_proj(cqn, ckvn, rest3, wts["w_uq"], wts["w_kv"], cs, sn, "mla_proj")
    (out_b, lse_b), got = _mla_fwd_t(qc, kc, vt, "mla_fwd", _GatherComm(ffn_shards[:1]) if ffn_shards else None)
    if ffn_shards:
        wts = dict(wts, w_ffn_in=got[0].reshape(N_CHIP, D, -1), w_ffn_out=late_got[0].reshape(D_FF, D))
    out_af, out_bf = out_a.reshape(T, D_A), out_b.reshape(T, D_B)
    y = _rms_fwd_pair(out_af, out_bf, gains["g_out_a"], gains["g_out_b"], "rms_out_fwd")
    mix = _mm(y, wts["w_out"], "nn", F32, "mm_out").reshape(B, S, D)
    h2, x1 = _adaln_fwd(x, gains["g_norm2"], sc2, sh2, "adaln2_fwd", mix=mix, gate=g1)
    h2f = h2.reshape(T, D)
    gu, act = _ffn_in_fwd(h2f, wts["w_ffn_in"], "mm_ffn_in")
    f = _mm(act, wts["w_ffn_out"], "nn", F32, "mm_ffn_out").reshape(B, S, D)
    dx2, df, dg2, dg_final, loss = _final_loss(x1, f, g2, gains["g_final"], target, "final_loss")

    dff = df.reshape(T, D)
    dgu = _ffn_out_bwd(dff, wts["w_ffn_out"], gu, "mm_ffn_out_dx")
    gw_ffn_out = _mm(act, dff, "tn", F32, "mm_ffn_out_dw")
    dh2 = _mm(dgu, wts["w_ffn_in"], "nt", F32, "mm_ffn_in_dx", col_blocks=N_CHIP, halves=True).reshape(B, S, D)
    gw_ffn_in = _mm(h2f, dgu, "tn", F32, "mm_ffn_in_dw", col_blocks=N_CHIP, halves=True)
    ffn_g8 = ffn_r1 = None
    if ffn_shards:
        ffn_g8 = [gw_ffn_in.reshape(N_DEV, -1, gw_ffn_in.shape[-1]), gw_ffn_out.reshape(N_DEV, -1, D)]
        dx1, dsh2, dsc2, dg_norm2, dg1, dmix, ffn_r1 = _adaln_bwd(
            dh2, x1, gains["g_norm2"], sc2, dx2, "adaln2_bwd", mix=mix, gate=g1, comm=_ToSiblingComm(ffn_g8))
    else:
        dx1, dsh2, dsc2, dg_norm2, dg1, dmix = _adaln_bwd(dh2, x1, gains["g_norm2"], sc2, dx2, "adaln2_bwd",
                                                          mix=mix, gate=g1)
    dmixf = dmix.reshape(T, D)
    dy = _mm(dmixf, wts["w_out"], "nt", F32, "mm_out_dx")
    gw_out = _mm(y, dmixf, "tn", F32, "mm_out_dw")
    do_a_v, dg_out_a = _rms_bwd_views(dy, 0, out_a, gains["g_out_a"], "rms_outa_bwd")
    do_b3, dg_out_b, delta_b = _rms_bwd_delta(dy, 1, out_b, gains["g_out_b"], "rms_outb_bwd")
    ffn_a4 = ffn_send = None
    if ffn_shards:
        ffn_a4, ffn_send = _rs_first(ffn_g8, "ffn", r1=ffn_r1)
    (dkc, dv, dq_t), ffn_r2 = _mla_bwd_t(qc, kc, v, do_b3, lse_b, delta_b, "mla_bwd",
                                         _ToChipsComm(ffn_send[:1]) if ffn_shards else None)
    d_rest, gw_uq, gw_kv, dg_cq, dg_ckv = _mla_proj_bwd(dq_t, dkc, dv, cqn, ckvn, rest3, wts["w_uq"], wts["w_kv"],
                                                        gains["g_cq"], gains["g_ckv"], cs, -sn, "mla_proj_bwd")
    dqkv_d, dbias_d = [], []
    for i, d in enumerate(DILATIONS):
        comm = _ToChipsComm(ffn_send[1:]) if (ffn_shards and i == 0) else None
        (dqkv_i, dbias_i), got = _dil_bwd(qkv_v[i], do_a_v[i], out_a_v[i], lse_a_v[i], bias, i, d,
                                          f"dil_bwd_{d}", comm)
        if comm is not None:
            ffn_r2 = list(ffn_r2) + list(got)
        dqkv_d.append(dqkv_i)
        dbias_d.append(dbias_i)
    dqkv = _sum_views_bf16(dqkv_d, "dil_bwd_sum").reshape(T, P_QKV)
    g_rel_bias = _bias_grad(dbias_d, buckets_dev, "rel_bias_grad")[:, :N_BUCKETS].T
    dproj = jnp.concatenate([dqkv, d_rest.reshape(T, P_REST)], axis=1)
    gw_in = _mm(h1f, dproj, "tn", F32, "mm_in_dw")
    mix_a4 = mix_r2 = None
    if ffn_shards:
        nat = [_w_in_from_kernel(gw_in), _w_uq_from_kernel(gw_uq), _w_ukv_from_kernel(gw_kv)]
        g8 = [_shards_from_full(g) for g in nat] + [gw_out]
        mix_a4, mix_send = _rs_first([g.reshape(N_DEV, -1, g.shape[-1]) for g in g8], "mix")
        dh1, mix_r2 = _mm(dproj, w_in, "nt", F32, "mm_in_dx", comm=_ToChipsComm(mix_send))
    else:
        dh1 = _mm(dproj, w_in, "nt", F32, "mm_in_dx")
    dh1 = dh1.reshape(B, S, D)
    grad_x, dsh1, dsc1, dg_norm1 = _adaln_bwd(dh1, x, gains["g_norm1"], sc1, dx1, "adaln1_bwd")
    gmod = jnp.concatenate([dsh1, dsc1, dg1, dsh2, dsc2, dg2], axis=-1).reshape(B, N_MOD * D)
    grads = dict(w_in=gw_in, w_uq=gw_uq, w_kv=gw_kv, w_out=gw_out, w_ffn_in=gw_ffn_in, w_ffn_out=gw_ffn_out,
                 g_norm1=dg_norm1, g_cq=dg_cq, g_ckv=dg_ckv, rel_bias=g_rel_bias, g_out_a=dg_out_a,
                 g_out_b=dg_out_b, g_norm2=dg_norm2, g_final=dg_final, ffn_pending=(ffn_a4, ffn_r2),
                 mix_pending=(mix_a4, mix_r2))
    return loss, grad_x, gmod, grads


def _w_in_to_kernel(w):
    z = lambda n: jnp.zeros((w.shape[0], n), w.dtype)
    i3, i4, i5 = 3 * D_A, 3 * D_A + Q_LORA, 3 * D_A + Q_LORA + KV_LORA
    return jnp.concatenate([w[:, :i3], w[:, i4:i5], z(NOPE_DIM), w[:, i5:], z(LANES - NOPE_DIM - ROPE_DIM),
                            w[:, i3:i4]], axis=1)


def _w_in_from_kernel(g):
    o = P_QKV + KV_LORA
    return jnp.concatenate([g[:, :P_QKV], g[:, o + LANES:], g[:, P_QKV:o],
                            g[:, o + NOPE_DIM:o + NOPE_DIM + ROPE_DIM]], axis=1)


def _w_uq_to_kernel(w):
    w3 = w.reshape(Q_LORA, N_HEADS, NOPE_DIM + ROPE_DIM)
    return jnp.pad(w3, ((0, 0), (0, 0), (0, LANES - NOPE_DIM - ROPE_DIM))).reshape(Q_LORA, N_HEADS * LANES)


def _w_uq_from_kernel(g):
    return g.reshape(Q_LORA, N_HEADS, LANES)[:, :, :NOPE_DIM + ROPE_DIM].reshape(Q_LORA, -1)


def _w_ukv_to_kernel(w):
    w3 = w.reshape(KV_LORA, N_HEADS, 2 * HEAD_DIM)
    wk = jnp.pad(w3[:, :, :NOPE_DIM], ((0, 0), (0, 0), (0, LANES - NOPE_DIM))).reshape(KV_LORA, N_HEADS * LANES)
    wv = w3[:, :, NOPE_DIM:].reshape(KV_LORA, D_B)
    return jnp.concatenate([wk, wv], axis=1)


def _w_ukv_from_kernel(g):
    gk = g[:, :N_HEADS * LANES].reshape(KV_LORA, N_HEADS, LANES)[:, :, :NOPE_DIM]
    gv = g[:, N_HEADS * LANES:].reshape(KV_LORA, N_HEADS, HEAD_DIM)
    return jnp.concatenate([gk, gv], axis=2).reshape(KV_LORA, -1)


MESH = pl.DeviceIdType.MESH


def _my_place():
    return lax.axis_index("x"), lax.axis_index("y"), lax.axis_index("c")


def _other_chips(x, y):
    return [(1 - x, y), (x, 1 - y), (1 - x, 1 - y)]


def _allgather8(x_shard, name, in_hbm):
    m_per, n = x_shard.shape
    space = pl.ANY if in_hbm else pltpu.VMEM

    def body(x_ref, out_ref, send_sems, recv_sems, local_sem):
        x, y, c = _my_place()
        me, sibling = (x, y, c), (x, y, 1 - c)
        chips = _other_chips(x, y)

        def rows(px, py, pc):
            return out_ref.at[pl.ds((4 * px + 2 * py + pc) * m_per, m_per), :]

        def copy(k, block, to, src=None):
            return pltpu.make_async_remote_copy(
                src_ref=rows(*block) if src is None else src, dst_ref=rows(*block),
                send_sem=send_sems.at[k], recv_sem=recv_sems.at[k], device_id=to, device_id_type=MESH)

        mine = pltpu.make_async_copy(x_ref, rows(*me), local_sem)
        mine.start()
        first = [copy(0, me, sibling, src=x_ref)]
        first += [copy(1 + j, me, (*chip, c), src=x_ref) for j, chip in enumerate(chips)]
        for cp in first:
            cp.start()
        passed = [copy(4 + j, (*chip, c), sibling) for j, chip in enumerate(chips)]
        for j, chip in enumerate(chips):
            copy(1 + j, (*chip, c), me).wait_recv()
            passed[j].start()
        copy(0, sibling, me).wait_recv()
        for j, chip in enumerate(chips):
            copy(4 + j, (*chip, 1 - c), me).wait_recv()
        for cp in first + passed:
            cp.wait_send()
        mine.wait()

    return pl.pallas_call(
        body, name=name,
        out_shape=jax.ShapeDtypeStruct((N_DEV * m_per, n), x_shard.dtype),
        in_specs=[pl.BlockSpec(memory_space=space)],
        out_specs=pl.BlockSpec(memory_space=space),
        scratch_shapes=[pltpu.SemaphoreType.DMA((7,)), pltpu.SemaphoreType.DMA((7,)), pltpu.SemaphoreType.DMA],
        compiler_params=pltpu.CompilerParams(vmem_limit_bytes=VMEM_LIMIT),
    )(x_shard)


def _hbm_specs(n):
    return [pl.BlockSpec(memory_space=pl.ANY)] * n


class _GatherComm:
    def __init__(self, shards):
        self.n = n = len(shards)
        self.inputs = [s.reshape(2, s.shape[0] // 2, s.shape[1]) for s in shards]
        self.out_shape = [jax.ShapeDtypeStruct((N_DEV,) + s.shape[1:], s.dtype) for s in self.inputs]
        self.scratch = [pltpu.SemaphoreType.DMA((7 * n,)), pltpu.SemaphoreType.DMA((7 * n,))]

    def _parts(self, xs, outs, sems):
        send_sems, recv_sems = sems
        x, y, c = _my_place()

        def blk(k, px, py, pc):
            return outs[k].at[4 * px + 2 * py + pc]

        def copy(k, kind, block, to, own=False):
            return pltpu.make_async_remote_copy(
                src_ref=xs[k].at[c] if own else blk(k, *block), dst_ref=blk(k, *block),
                send_sem=send_sems.at[7 * k + kind], recv_sem=recv_sems.at[7 * k + kind],
                device_id=to, device_id_type=MESH)

        def whole(k):
            return pltpu.make_async_remote_copy(
                src_ref=xs[k], dst_ref=outs[k].at[pl.ds(4 * x + 2 * y, 2)],
                send_sem=send_sems.at[7 * k], recv_sem=recv_sems.at[7 * k],
                device_id=(x, y, 1 - c), device_id_type=MESH)

        me, sibling = (x, y, c), (x, y, 1 - c)
        chips = _other_chips(x, y)
        first = []
        for k in range(self.n):
            first.append(whole(k))
            first += [copy(k, 1 + j, me, (*chip, c), own=True) for j, chip in enumerate(chips)]
        return copy, whole, me, sibling, chips, c, first

    def start(self, xs, outs, sems):
        for cp in self._parts(xs, outs, sems)[-1]:
            cp.start()

    def finish(self, xs, outs, sems):
        copy, whole, me, sibling, chips, c, first = self._parts(xs, outs, sems)
        passed = []
        for j, chip in enumerate(chips):
            for k in range(self.n):
                copy(k, 1 + j, (*chip, c), me).wait_recv()
                fwd = copy(k, 4 + j, (*chip, c), sibling)
                fwd.start()
                passed.append(fwd)
        for k in range(self.n):
            whole(k).wait_recv()
        for j, chip in enumerate(chips):
            for k in range(self.n):
                copy(k, 4 + j, (*chip, 1 - c), me).wait_recv()
        for cp in first + passed:
            cp.wait_send()


class _AllGatherComm:
    def __init__(self, x):
        self.inputs = [x]
        self.n = 1
        self.out_shape = [jax.ShapeDtypeStruct((N_DEV,) + x.shape, x.dtype)]
        self.scratch = [pltpu.SemaphoreType.DMA((7,)), pltpu.SemaphoreType.DMA((7,)), pltpu.SemaphoreType.DMA]

    def _parts(self, xs, outs, sems):
        send_sems, recv_sems, local_sem = sems
        x_ref, out_ref = xs[0], outs[0]
        x, y, c = _my_place()

        def copy(k, block, to, own=False):
            blk = out_ref.at[4 * block[0] + 2 * block[1] + block[2]]
            return pltpu.make_async_remote_copy(
                src_ref=x_ref if own else blk, dst_ref=blk, send_sem=send_sems.at[k], recv_sem=recv_sems.at[k],
                device_id=to, device_id_type=MESH)

        me, sibling = (x, y, c), (x, y, 1 - c)
        chips = _other_chips(x, y)
        local = pltpu.make_async_copy(x_ref, out_ref.at[4 * x + 2 * y + c], local_sem)
        first = [copy(0, me, sibling, own=True)]
        first += [copy(1 + j, me, (*chip, c), own=True) for j, chip in enumerate(chips)]
        return copy, me, sibling, chips, c, local, first

    def start(self, xs, outs, sems):
        _, _, _, _, _, local, first = self._parts(xs, outs, sems)
        for cp in [local] + first:
            cp.start()

    def finish(self, xs, outs, sems):
        copy, me, sibling, chips, c, local, first = self._parts(xs, outs, sems)
        passed = []
        for j, chip in enumerate(chips):
            copy(1 + j, (*chip, c), me).wait_recv()
            fwd = copy(4 + j, (*chip, c), sibling)
            fwd.start()
            passed.append(fwd)
        copy(0, sibling, me).wait_recv()
        for j, chip in enumerate(chips):
            copy(4 + j, (*chip, 1 - c), me).wait_recv()
        for cp in first + passed:
            cp.wait_send()
        local.wait()


class _BothComm:
    def __init__(self, a, b):
        self.a, self.b = a, b
        self.inputs = a.inputs + b.inputs
        self.n = a.n + b.n
        self.out_shape = a.out_shape + b.out_shape
        self.scratch = a.scratch + b.scratch

    def _split(self, xs, outs, sems):
        na, ns = self.a.n, len(self.a.scratch)
        return (xs[:na], outs[:na], sems[:ns]), (xs[na:], outs[na:], sems[ns:])

    def start(self, xs, outs, sems):
        pa, pb = self._split(xs, outs, sems)
        self.a.start(*pa)
        self.b.start(*pb)

    def finish(self, xs, outs, sems):
        pa, pb = self._split(xs, outs, sems)
        self.a.finish(*pa)
        self.b.finish(*pb)


class _ToChipsComm:
    def __init__(self, a4s):
        self.inputs = list(a4s)
        self.n = n = len(a4s)
        nc = N_CHIP - 1
        self.out_shape = [jax.ShapeDtypeStruct((nc,) + a.shape[1:], a.dtype) for a in a4s]
        self.scratch = [pltpu.SemaphoreType.DMA((nc * n,)), pltpu.SemaphoreType.DMA((nc * n,))]

    def _copies(self, as_, rs, sems):
        send_sems, recv_sems = sems
        x, y, c = _my_place()
        nc = N_CHIP - 1
        return [pltpu.make_async_remote_copy(
            src_ref=as_[k].at[2 * cx + cy], dst_ref=rs[k].at[j], send_sem=send_sems.at[nc * k + j],
            recv_sem=recv_sems.at[nc * k + j], device_id=(cx, cy, c), device_id_type=MESH)
            for k in range(self.n) for j, (cx, cy) in enumerate(_other_chips(x, y))]

    def start(self, as_, rs, sems):
        for cp in self._copies(as_, rs, sems):
            cp.start()

    def finish(self, as_, rs, sems):
        for cp in self._copies(as_, rs, sems):
            cp.wait()


def _run_comm(comm, name):
    n = comm.n

    def body(*refs):
        ins, outs, sems = refs[:n], refs[n:2 * n], refs[2 * n:]
        comm.start(ins, outs, sems)
        comm.finish(ins, outs, sems)

    return pl.pallas_call(
        body, name=name, out_shape=comm.out_shape, in_specs=_hbm_specs(n), out_specs=_hbm_specs(n),
        scratch_shapes=comm.scratch,
    )(*comm.inputs)


class _ToSiblingComm:
    def __init__(self, g8s):
        self.inputs = list(g8s)
        self.n = n = len(g8s)
        self.out_shape = [jax.ShapeDtypeStruct((N_CHIP,) + g.shape[1:], g.dtype) for g in g8s]
        self.scratch = [pltpu.SemaphoreType.DMA((N_CHIP * n,)), pltpu.SemaphoreType.DMA((N_CHIP * n,))]

    def _copies(self, gs, rs, sems):
        send_sems, recv_sems = sems
        x, y, c = _my_place()
        return [pltpu.make_async_remote_copy(
            src_ref=gs[k].at[2 * s + 1 - c], dst_ref=rs[k].at[s], send_sem=send_sems.at[N_CHIP * k + s],
            recv_sem=recv_sems.at[N_CHIP * k + s], device_id=(x, y, 1 - c), device_id_type=MESH)
            for k in range(self.n) for s in range(N_CHIP)]

    def start(self, gs, rs, sems):
        for cp in self._copies(gs, rs, sems):
            cp.start()

    def finish(self, gs, rs, sems):
        for cp in self._copies(gs, rs, sems):
            cp.wait()


def _swap_halves(hs, name):
    n = len(hs)

    def body(*refs):
        o_refs = refs[n:2 * n]
        send_sems, recv_sems = refs[2 * n:]
        x, y, c = _my_place()

        def remote(k, slot):
            return pltpu.make_async_remote_copy(
                src_ref=o_refs[k].at[slot], dst_ref=o_refs[k].at[slot], send_sem=send_sems.at[k],
                recv_sem=recv_sems.at[k], device_id=(x, y, 1 - c), device_id_type=MESH)

        sends = [remote(k, c) for k in range(n)]
        for cp in sends:
            cp.start()
        for k in range(n):
            remote(k, 1 - c).wait_recv()
        for cp in sends:
            cp.wait_send()

    return pl.pallas_call(
        body, name=name,
        out_shape=[jax.ShapeDtypeStruct(h.shape, h.dtype) for h in hs],
        in_specs=_hbm_specs(n), out_specs=_hbm_specs(n),
        input_output_aliases={k: k for k in range(n)},
        scratch_shapes=[pltpu.SemaphoreType.DMA((n,)), pltpu.SemaphoreType.DMA((n,))],
    )(*hs)


ADD_TILES = 2


def _add_blocks(a_list, a_idx_fn, others_list, ns, sel, name, out_blocks=None, out_idx_fn=None,
                bf16_copy=False):
    out_blocks = out_blocks or ns
    out_idx_fn = out_idx_fn or (lambda s, sel_ref: s)
    n = len(a_list)
    n_o = len(others_list[0])
    per = 1 + n_o

    def body(sel_ref, *refs):
        for k in range(n):
            ins = refs[k * per:(k + 1) * per]
            acc = ins[0][0]
            for r in ins[1:]:
                acc = acc + r[0].astype(F32)
            refs[n * per + k][0] = acc
            if bf16_copy:
                refs[n * per + n + k][0] = acc.astype(BF16)

    in_specs, args, out_specs, out_shape = [], [], [], []
    for a, others in zip(a_list, others_list):
        _, R, N = a.shape
        tr = R // ADD_TILES
        assert tr % 8 == 0, a.shape
        in_specs.append(pl.BlockSpec((1, tr, N), lambda s, i, sel_ref: (a_idx_fn(s, sel_ref), i, 0)))
        args.append(a)
        for arr, fixed in others:
            if fixed is None:
                in_specs.append(pl.BlockSpec((1, tr, N), lambda s, i, sel_ref: (s, i, 0)))
            else:
                in_specs.append(pl.BlockSpec((1, tr, N), lambda s, i, sel_ref, fixed=fixed: (fixed, i, 0)))
            args.append(arr)
        out_specs.append(pl.BlockSpec((1, tr, N), lambda s, i, sel_ref: (out_idx_fn(s, sel_ref), i, 0)))
        out_shape.append(jax.ShapeDtypeStruct((out_blocks, R, N), a.dtype))
    if bf16_copy:
        out_specs = out_specs + out_specs
        out_shape = out_shape + [jax.ShapeDtypeStruct(o.shape, BF16) for o in out_shape]
    grid_spec = pltpu.PrefetchScalarGridSpec(num_scalar_prefetch=1, grid=(ns, ADD_TILES), in_specs=in_specs,
                                             out_specs=out_specs)
    return pl.pallas_call(
        body, name=name, out_shape=out_shape, grid_spec=grid_spec,
        compiler_params=_cparams(("parallel", "parallel")),
    )(sel, *args)


def _rs_first(g8s, tag, r1=None):
    c_sel = jnp.reshape(lax.axis_index("c"), (1,)).astype(jnp.int32)
    if r1 is None:
        r1 = _run_comm(_ToSiblingComm(g8s), f"rs_to_sibling_{tag}")
    res = _add_blocks(g8s, lambda s, sel: 2 * s + sel[0], [[(r, None)] for r in r1], N_CHIP, c_sel,
                      f"rs_add_sibling_{tag}", bf16_copy=True)
    return list(res[:len(g8s)]), list(res[len(g8s):])


def _rs_last(a4s, r2s, tag):
    sel = jnp.stack([2 * lax.axis_index("x") + lax.axis_index("y"), lax.axis_index("c")]).astype(jnp.int32)
    h = _add_blocks(a4s, lambda s, sel: sel[0], [[(r, 0), (r, 1), (r, 2)] for r in r2s], 1, sel,
                    f"rs_add_chips_{tag}", out_blocks=2, out_idx_fn=lambda s, sel: sel[1])
    full = _swap_halves(h, f"rs_swap_halves_{tag}")
    return [f.reshape(2 * f.shape[1], f.shape[2]) for f in full]


def _ada_fwd(c_all, w_ada, b_ada, name):
    nb, D = c_all.shape
    ncol = w_ada.shape[1]
    tc = 512

    def body(c_ref, w_ref, b_ref, o_ref):
        cv = c_ref[...]
        cond = (cv * jax.nn.sigmoid(cv)).astype(BF16)
        o_ref[...] = jnp.dot(cond, w_ref[...].astype(BF16), preferred_element_type=F32) + b_ref[...]

    return pl.pallas_call(
        body, name=name, out_shape=jax.ShapeDtypeStruct((nb, ncol), F32), grid=(ncol // tc,),
        in_specs=[pl.BlockSpec((nb, D), lambda j: (0, 0)), pl.BlockSpec((D, tc), lambda j: (0, j)),
                  pl.BlockSpec((1, tc), lambda j: (0, j))],
        out_specs=pl.BlockSpec((nb, tc), lambda j: (0, j)),
        compiler_params=_cparams(("parallel",)),
    )(c_all, w_ada, b_ada)


def _ada_bwd(c_all, gmod_cols, name):
    nb, D = c_all.shape
    ncol = gmod_cols.shape[1]
    tc = 512

    def body(c_ref, g_ref, o_ref):
        cv = c_ref[...]
        cond = (cv * jax.nn.sigmoid(cv)).astype(BF16)
        o_ref[...] = _dot_tn(cond, g_ref[...].astype(BF16))

    return pl.pallas_call(
        body, name=name, out_shape=jax.ShapeDtypeStruct((D, ncol), F32), grid=(ncol // tc,),
        in_specs=[pl.BlockSpec((nb, D), lambda j: (0, 0)), pl.BlockSpec((nb, tc), lambda j: (0, j))],
        out_specs=pl.BlockSpec((D, tc), lambda j: (0, j)),
        compiler_params=_cparams(("parallel",)),
    )(c_all, gmod_cols)


def _adam_math(w, g, m, v):
    m = ADAM_B1 * m + (1.0 - ADAM_B1) * g
    v = ADAM_B2 * v + (1.0 - ADAM_B2) * (g * g)
    m_hat = m / (1.0 - ADAM_B1 ** ADAM_STEP)
    v_hat = v / (1.0 - ADAM_B2 ** ADAM_STEP)
    delta = -ADAM_LR * (m_hat / (jnp.sqrt(v_hat) + ADAM_EPS) + ADAM_WD * w)
    return delta, m, v


def _adamw(w, g, m, v, name):
    rows, cols = w.shape
    tr = _pick(rows, (256, 192, 176, 128, 64, 8))

    def body(w_ref, g_ref, m_ref, v_ref, d_ref, mo_ref, vo_ref):
        d, mn, vn = _adam_math(w_ref[...], g_ref[...], m_ref[...], v_ref[...])
        d_ref[...] = d
        mo_ref[...] = mn
        vo_ref[...] = vn

    spec = pl.BlockSpec((tr, cols), lambda i: (i, 0))
    return pl.pallas_call(
        body, name=name, out_shape=[jax.ShapeDtypeStruct((rows, cols), F32)] * 3, grid=(rows // tr,),
        in_specs=[spec] * 4, out_specs=[spec] * 3, compiler_params=_cparams(("parallel",)),
    )(w, g, m, v)


VEC_ROWS = 8


def _adamw_rows(w, parts, m, v, name):
    n = w.shape[1]
    P = parts.shape[0]
    assert n % (VEC_ROWS * LANES) == 0, n
    shp = (VEC_ROWS, n // VEC_ROWS)

    def body(w_ref, p_ref, m_ref, v_ref, g_ref, d_ref, mo_ref, vo_ref):
        g = p_ref[0]
        for k in range(1, P):
            g = g + p_ref[k]
        d, mn, vn = _adam_math(w_ref[...], g, m_ref[...], v_ref[...])
        g_ref[...] = g
        d_ref[...] = d
        mo_ref[...] = mn
        vo_ref[...] = vn

    vec = pl.BlockSpec(shp, lambda i: (0, 0))
    out = pl.pallas_call(
        body, name=name, out_shape=[jax.ShapeDtypeStruct(shp, F32)] * 4, grid=(1,),
        in_specs=[vec, pl.BlockSpec((P,) + shp, lambda i: (0, 0, 0)), vec, vec], out_specs=[vec] * 4,
        compiler_params=_cparams(("arbitrary",)),
    )(w.reshape(shp), parts.reshape((P,) + shp), m.reshape(shp), v.reshape(shp))
    return [o.reshape(1, n) for o in out]


_SHARDED = ("w_in", "w_uq", "w_ukv", "w_out", "w_ffn_in", "w_ffn_out")
_SMALL = (("g_norm1", 1024), ("g_cq", 384), ("g_ckv", 256), ("rel_bias", 256), ("g_out_a", 512),
          ("g_out_b", 512), ("g_norm2", 1024), ("g_final", 1024))
_SMALL_PAD = 5120


def _full_from_shards(sh):
    return jnp.transpose(sh, (1, 0, 2)).reshape(sh.shape[1], -1)


def _shards_from_full(full):
    rows, cols = full.shape
    return jnp.transpose(full.reshape(rows, N_CHIP, cols // N_CHIP), (1, 0, 2))


def kernel(x, c, w_ada, b_ada, g_norm1, w_in, g_cq, w_uq, g_ckv, w_ukv, rel_bias, g_out_a, g_out_b, w_out, g_norm2, w_ffn_in, w_ffn_out, g_final, loss_target, m_w_ada, m_b_ada, m_g_norm1, m_w_in, m_g_cq, m_w_uq, m_g_ckv, m_w_ukv, m_rel_bias, m_g_out_a, m_g_out_b, m_w_out, m_g_norm2, m_w_ffn_in, m_w_ffn_out, m_g_final, v_w_ada, v_b_ada, v_g_norm1, v_w_in, v_g_cq, v_w_uq, v_g_ckv, v_w_ukv, v_rel_bias, v_g_out_a, v_g_out_b, v_w_out, v_g_norm2, v_w_ffn_in, v_w_ffn_out, v_g_final):
    names = ["w_ada", "b_ada", "g_norm1", "w_in", "g_cq", "w_uq", "g_ckv", "w_ukv", "rel_bias", "g_out_a",
             "g_out_b", "w_out", "g_norm2", "w_ffn_in", "w_ffn_out", "g_final"]
    W = dict(zip(names, [w_ada, b_ada, g_norm1, w_in, g_cq, w_uq, g_ckv, w_ukv, rel_bias, g_out_a, g_out_b,
                         w_out, g_norm2, w_ffn_in, w_ffn_out, g_final]))
    M = dict(zip(names, [m_w_ada, m_b_ada, m_g_norm1, m_w_in, m_g_cq, m_w_uq, m_g_ckv, m_w_ukv, m_rel_bias,
                         m_g_out_a, m_g_out_b, m_w_out, m_g_norm2, m_w_ffn_in, m_w_ffn_out, m_g_final]))
    V = dict(zip(names, [v_w_ada, v_b_ada, v_g_norm1, v_w_in, v_g_cq, v_w_uq, v_g_ckv, v_w_ukv, v_rel_bias,
                         v_g_out_a, v_g_out_b, v_w_out, v_g_norm2, v_w_ffn_in, v_w_ffn_out, v_g_final]))
    B, S, D = x.shape
    mx, my, mc = _my_place()
    dev = 4 * mx + 2 * my + mc
    chip = 2 * mx + my
    pad_rows = 8

    early = ("w_in", "w_uq", "w_ukv")
    bias, got = _bias_tables(rel_bias, _bucket_tables(), "rel_bias_tables",
                             _BothComm(_AllGatherComm(jnp.pad(c, ((0, pad_rows - B), (0, 0)))),
                                       _GatherComm([W[n][0].astype(BF16) for n in early])))
    c_all = got[0][:, :B].reshape(N_DEV * B, D)

    ada_cols = w_ada.shape[-1]
    b_cols = lax.dynamic_slice_in_dim(b_ada, chip * ada_cols, ada_cols, axis=1)
    mod_cols = _ada_fwd(c_all, w_ada[0], b_cols, "ada_fwd")
    mod_all = _allgather8(mod_cols, "ag_mod", False).reshape(N_DEV, N_DEV * B, ada_cols)[0::2]
    mod_all = jnp.transpose(mod_all, (1, 0, 2)).reshape(N_DEV * B, N_MOD * D)
    mod = lax.dynamic_slice_in_dim(mod_all, dev * B, B, axis=0)

    full = {n: g.reshape((N_CHIP,) + W[n].shape[1:]) for n, g in zip(early, got[1:])}
    wts = dict(w_in=_w_in_to_kernel(_full_from_shards(full["w_in"])),
               w_uq=_w_uq_to_kernel(_full_from_shards(full["w_uq"])),
               w_kv=_w_ukv_to_kernel(_full_from_shards(full["w_ukv"])))
    gains = dict(g_norm1=g_norm1, g_cq=g_cq, g_ckv=g_ckv, g_out_a=g_out_a, g_out_b=g_out_b, g_norm2=g_norm2,
                 g_final=g_final.reshape(1, D))

    loss, grad_x, gmod, grads = _local_step(x, loss_target, mod, wts, gains, rel_bias,
                                            ffn_shards=[w_ffn_in[0].astype(BF16), w_ffn_out[0].astype(BF16),
                                                        w_out[0].astype(BF16)], bias=bias)
    loss = lax.psum(loss[0, 0], ("x", "y", "c"))

    n_small = _SMALL_PAD
    cat = lambda dct: jnp.concatenate([dct[n].reshape(1, -1) for n, _ in _SMALL]
                                      + [jnp.zeros((1, _SMALL_PAD - sum(s for _, s in _SMALL)), F32)], axis=1)
    small = cat(grads)
    rows = jnp.concatenate([gmod, jnp.pad(small, ((0, 0), (0, N_MOD * D - n_small))),
                            jnp.zeros((pad_rows - B - 1, N_MOD * D), F32)], axis=0)
    rows_all = _allgather8(rows, "ag_small", False).reshape(N_DEV, pad_rows, N_MOD * D)
    gmod_all = rows_all[:, :B].reshape(N_DEV * B, N_MOD * D)
    small_parts = rows_all[:, B, :n_small]

    a4, r2 = grads["mix_pending"]
    ffn_a4, ffn_r2 = grads["ffn_pending"]
    G = dict(zip(_SHARDED, _rs_last(list(a4) + list(ffn_a4), list(r2) + list(ffn_r2), "all")))

    gmod_cols = lax.dynamic_slice_in_dim(gmod_all, chip * ada_cols, ada_cols, axis=1)
    G["w_ada"] = _ada_bwd(c_all, gmod_cols, "ada_bwd")
    delta, new_m, new_v = {}, {}, {}
    for n in ("w_ada",) + _SHARDED:
        shp = W[n].shape
        w2 = W[n].reshape(shp[-2], shp[-1])
        d_, m_, v_ = _adamw(w2, G[n], M[n].reshape(w2.shape), V[n].reshape(w2.shape), f"adamw_{n}")
        G[n], delta[n], new_m[n], new_v[n] = [a.reshape(shp) for a in (G[n], d_, m_, v_)]
    gs, ds_, ms_, vs_ = _adamw_rows(cat(W), small_parts, cat(M), cat(V), "adamw_small")
    off = 0
    for n, sz in _SMALL:
        shp = W[n].shape
        G[n], delta[n], new_m[n], new_v[n] = [a[:, off:off + sz].reshape(shp) for a in (gs, ds_, ms_, vs_)]
        off += sz
    G["b_ada"], delta["b_ada"], new_m["b_ada"], new_v["b_ada"] = _adamw_rows(b_ada, gmod_all, m_b_ada, v_b_ada,
                                                                          "adamw_b_ada")
    return (loss, grad_x, *[G[n] for n in names], *[delta[n] for n in names], *[new_m[n] for n in names],
            *[new_v[n] for n in names])
```

```python
import functools
import math

import numpy as np
import jax
import jax.numpy as jnp
from jax import lax
from jax.experimental import pallas as pl
from jax.experimental.pallas import tpu as pltpu

F32 = jnp.float32
BF16 = jnp.bfloat16

D_MODEL = 1024
SEQ = 2048
N_HEADS = 8
HEAD_DIM = 64
D_A = 512
D_B = 512
Q_LORA = 384
KV_LORA = 256
ROPE_DIM = 32
NOPE_DIM = 64
D_FF = 2816
N_MOD = 6
N_BUCKETS = 32
MAX_DISTANCE = 2048
ROPE_THETA = 10000.0
EPS = 1e-6
NEG = -1e30
BLK = 128
DILATIONS = (1, 4, 16)
SPAN = 128
MLA_SCALE = (NOPE_DIM + ROPE_DIM) ** -0.5
DIL_SCALE = HEAD_DIM ** -0.5

ADAM_LR = 0.001
ADAM_B1 = 0.9
ADAM_B2 = 0.999
ADAM_EPS = 1e-08
ADAM_WD = 0.01
ADAM_STEP = 10

N_DEV = 8
N_CHIP = 4
LANES = 128
VMEM_LIMIT = 48 * 1024 * 1024
MM_VMEM_BUDGET = 32 * 1024 * 1024

P_QKV = 3 * D_A
P_REST = KV_LORA + LANES + Q_LORA


def _cparams(sem=None):
    return pltpu.CompilerParams(dimension_semantics=sem, vmem_limit_bytes=VMEM_LIMIT)


def _pick(n, cands):
    for c in cands:
        if n % c == 0:
            return c
    raise ValueError(f"no tile for {n} in {cands}")


def _mm(a, b, mode, out_dtype, name, col_blocks=None, comm=None, halves=False):
    blocked = col_blocks is not None
    if mode == "nn":
        (M, K) = a.shape
        K2, N = (b.shape[1], b.shape[0] * b.shape[2]) if blocked else b.shape
    elif mode == "nt":
        (M, K) = (a.shape[1], 2 * a.shape[2]) if halves else a.shape
        N, K2 = (b.shape[1], b.shape[0] * b.shape[2]) if blocked else b.shape
    else:
        (K, M) = a.shape
        K2, N = (b.shape[1], 2 * b.shape[2]) if halves else b.shape
    assert K == K2, (a.shape, b.shape, mode)
    assert not halves or (blocked and col_blocks == 4 and mode in ("nt", "tn"))
    tn = _pick(N, (1408, 1024, 768, 512, 384, 256, 128))
    tk = _pick(K, (1408, 1152, 1024, 768, 512, 384, 256, 128))
    if blocked and mode == "nt":
        tk = K // col_blocks
    elif blocked:
        tn = N // col_blocks
    nk = K // tk

    def vmem_bytes(tm_):
        tiles = tm_ * tk * a.dtype.itemsize + tk * tn * b.dtype.itemsize + tm_ * tn * jnp.dtype(out_dtype).itemsize
        return 2 * tiles + tm_ * tn * 4

    tm = next(t for t in (1408, 1024, 512, 384, 256, 128) if M % t == 0 and vmem_bytes(t) <= MM_VMEM_BUDGET)
    out_shape = (M, N)
    out_spec = pl.BlockSpec((tm, tn), lambda i, j, k: (i, j))
    if mode == "nn":
        a_spec = pl.BlockSpec((tm, tk), lambda i, j, k: (i, k))
        b_spec = (pl.BlockSpec((None, tk, tn), lambda i, j, k: (j, k, 0)) if blocked
                  else pl.BlockSpec((tk, tn), lambda i, j, k: (k, j)))
        dn = (((1,), (0,)), ((), ()))
    elif mode == "nt":
        a_spec = (pl.BlockSpec((None, tm, tk), lambda i, j, k: (k // 2, i, k % 2)) if halves
                  else pl.BlockSpec((tm, tk), lambda i, j, k: (i, k)))
        b_spec = (pl.BlockSpec((None, tn, tk), lambda i, j, k: (k, j, 0)) if blocked
                  else pl.BlockSpec((tn, tk), lambda i, j, k: (j, k)))
        dn = (((1,), (1,)), ((), ()))
    else:
        a_spec = pl.BlockSpec((tk, tm), lambda i, j, k: (k, i))
        b_spec = (pl.BlockSpec((None, tk, tn), lambda i, j, k: (j // 2, k, j % 2)) if halves
                  else pl.BlockSpec((tk, tn), lambda i, j, k: (k, j)))
        dn = (((0,), (0,)), ((), ()))
        if blocked:
            out_shape = (col_blocks, M, tn)
            out_spec = pl.BlockSpec((None, tm, tn), lambda i, j, k: (j, i, 0))

    def body(a_ref, b_ref, o_ref, acc_ref):
        k = pl.program_id(2)

        @pl.when(k == 0)
        def _():
            acc_ref[...] = jnp.zeros_like(acc_ref)

        acc_ref[...] += lax.dot_general(a_ref[...].astype(BF16), b_ref[...].astype(BF16), dn,
                                        preferred_element_type=F32)

        @pl.when(k == nk - 1)
        def _():
            o_ref[...] = acc_ref[...].astype(o_ref.dtype)

    if comm is not None:
        (out,), got = _host_call(
            body, comm, name=name, out_shape=[jax.ShapeDtypeStruct(out_shape, out_dtype)],
            grid=(M // tm, N // tn, nk), in_specs=[a_spec, b_spec], out_specs=[out_spec],
            scratch_shapes=[pltpu.VMEM((tm, tn), F32)], args=(a, b))
        return out, got
    return pl.pallas_call(
        body, name=name,
        out_shape=jax.ShapeDtypeStruct(out_shape, out_dtype),
        grid=(M // tm, N // tn, nk),
        in_specs=[a_spec, b_spec],
        out_specs=out_spec,
        scratch_shapes=[pltpu.VMEM((tm, tn), F32)],
        compiler_params=_cparams(("parallel", "parallel", "arbitrary")),
    )(a, b)


ROW_TILE = 512


def _adaln_fwd(x, g, sc, sh, name, mix=None, gate=None):
    B, S, D = x.shape
    ts = ROW_TILE
    has_res = mix is not None

    def body(*refs):
        if has_res:
            x_ref, g_ref, sc_ref, sh_ref, mix_ref, gate_ref, h_ref, xr_ref = refs
            xr = x_ref[0] + gate_ref[0] * mix_ref[0]
            xr_ref[0] = xr
        else:
            x_ref, g_ref, sc_ref, sh_ref, h_ref = refs
            xr = x_ref[0]
        r = lax.rsqrt(jnp.mean(xr * xr, axis=-1, keepdims=True) + EPS)
        xn = (xr * r) * g_ref[...]
        h_ref[0] = (xn * (1.0 + sc_ref[0]) + sh_ref[0]).astype(h_ref.dtype)

    tok = pl.BlockSpec((1, ts, D), lambda b, s: (b, s, 0))
    per_b = pl.BlockSpec((1, 1, D), lambda b, s: (b, 0, 0))
    vec = pl.BlockSpec((1, D), lambda b, s: (0, 0))
    in_specs = [tok, vec, per_b, per_b]
    args = [x, g, sc, sh]
    out_shape = [jax.ShapeDtypeStruct((B, S, D), BF16)]
    out_specs = [tok]
    if has_res:
        in_specs += [tok, per_b]
        args += [mix, gate]
        out_shape.append(jax.ShapeDtypeStruct((B, S, D), F32))
        out_specs.append(tok)
    out = pl.pallas_call(
        body, name=name, out_shape=out_shape, grid=(B, S // ts),
        in_specs=in_specs, out_specs=out_specs,
        compiler_params=_cparams(("parallel", "parallel")),
    )(*args)
    return out if has_res else out[0]


def _adaln_bwd(dh, x, g, sc, dres, name, mix=None, gate=None, comm=None):
    B, S, D = x.shape
    ts = ROW_TILE
    has_res = mix is not None

    def body(*refs):
        if has_res:
            (dh_ref, x_ref, g_ref, sc_ref, dres_ref, mix_ref, gate_ref,
             dx_ref, dsh_ref, dsc_ref, dg_ref, dgate_ref, dmix_ref) = refs
        else:
            (dh_ref, x_ref, g_ref, sc_ref, dres_ref, dx_ref, dsh_ref, dsc_ref, dg_ref) = refs
        b, s = pl.program_id(0), pl.program_id(1)
        xv = x_ref[0]
        dhv = dh_ref[0]
        gv = g_ref[...]
        r = lax.rsqrt(jnp.mean(xv * xv, axis=-1, keepdims=True) + EPS)
        n = xv * r
        xn = n * gv
        dxn = dhv * (1.0 + sc_ref[0])
        dn = dxn * gv
        dx = r * (dn - n * jnp.mean(dn * n, axis=-1, keepdims=True)) + dres_ref[0]
        dx_ref[0] = dx

        @pl.when(s == 0)
        def _():
            dsh_ref[...] = jnp.zeros_like(dsh_ref)
            dsc_ref[...] = jnp.zeros_like(dsc_ref)
            if has_res:
                dgate_ref[...] = jnp.zeros_like(dgate_ref)

        @pl.when((s == 0) & (b == 0))
        def _():
            dg_ref[...] = jnp.zeros_like(dg_ref)

        dsh_ref[0] += jnp.sum(dhv, axis=0, keepdims=True)
        dsc_ref[0] += jnp.sum(dhv * xn, axis=0, keepdims=True)
        dg_ref[...] += jnp.sum(dxn * n, axis=0, keepdims=True)
        if has_res:
            dgate_ref[0] += jnp.sum(dx * mix_ref[0], axis=0, keepdims=True)
            dmix_ref[0] = (dx * gate_ref[0]).astype(dmix_ref.dtype)

    tok = pl.BlockSpec((1, ts, D), lambda b, s: (b, s, 0))
    per_b = pl.BlockSpec((1, 1, D), lambda b, s: (b, 0, 0))
    vec = pl.BlockSpec((1, D), lambda b, s: (0, 0))
    in_specs = [tok, tok, vec, per_b, tok]
    args = [dh, x, g, sc, dres]
    out_shape = [jax.ShapeDtypeStruct((B, S, D), F32), jax.ShapeDtypeStruct((B, 1, D), F32),
                 jax.ShapeDtypeStruct((B, 1, D), F32), jax.ShapeDtypeStruct((1, D), F32)]
    out_specs = [tok, per_b, per_b, vec]
    if has_res:
        in_specs += [tok, per_b]
        args += [mix, gate]
        out_shape += [jax.ShapeDtypeStruct((B, 1, D), F32), jax.ShapeDtypeStruct((B, S, D), BF16)]
        out_specs += [per_b, tok]
    res, got = _host_call(body, comm, name=name, out_shape=out_shape, grid=(B, S // ts), in_specs=in_specs,
                          out_specs=out_specs, scratch_shapes=[], args=args)
    return (list(res) + [got]) if comm is not None else res


def _rms_fwd_pair(xa, xb, ga, gb, name):
    T, na = xa.shape
    nb = xb.shape[1]
    tr = 512

    def body(xa_ref, xb_ref, ga_ref, gb_ref, y_ref):
        for x_ref, g_ref, lo, n in ((xa_ref, ga_ref, 0, na), (xb_ref, gb_ref, na, nb)):
            xv = x_ref[...]
            r = lax.rsqrt(jnp.mean(xv * xv, axis=-1, keepdims=True) + EPS)
            y_ref[:, lo:lo + n] = ((xv * r) * g_ref[...]).astype(y_ref.dtype)

    row = lambda n: pl.BlockSpec((tr, n), lambda i: (i, 0))
    vec = lambda n: pl.BlockSpec((1, n), lambda i: (0, 0))
    return pl.pallas_call(
        body, name=name, out_shape=jax.ShapeDtypeStruct((T, na + nb), BF16), grid=(T // tr,),
        in_specs=[row(na), row(nb), vec(na), vec(nb)], out_specs=row(na + nb),
        compiler_params=_cparams(("parallel",)),
    )(xa, xb, ga, gb)


def _rms_bwd_views(dy, dy_blk, x, g, name):
    B, S, n = x.shape
    tiles = S // VIEW_TILE

    def body(dy_ref, x_ref, g_ref, d1_ref, d4_ref, d16_ref, dg_ref, dx_s):
        xv = x_ref[0]
        dyv = dy_ref[...]
        r = lax.rsqrt(jnp.mean(xv * xv, axis=-1, keepdims=True) + EPS)
        nrm = xv * r
        dn = dyv * g_ref[...]
        dx = r * (dn - nrm * jnp.mean(dn * nrm, axis=-1, keepdims=True))
        d1_ref[0] = dx.astype(d1_ref.dtype)
        _put_tile(dx_s, dx)
        _tile_to_view(dx_s, d4_ref, DILATIONS[1], n)
        _tile_to_view(dx_s, d16_ref, DILATIONS[2], n)

        @pl.when((pl.program_id(0) == 0) & (pl.program_id(1) == 0))
        def _():
            dg_ref[...] = jnp.zeros_like(dg_ref)

        dg_ref[...] += jnp.sum(dyv * nrm, axis=0, keepdims=True)

    res = pl.pallas_call(
        body, name=name,
        out_shape=[_view_shape(B, S, d, n, BF16) for d in DILATIONS] + [jax.ShapeDtypeStruct((1, n), F32)],
        grid=(B, tiles),
        in_specs=[pl.BlockSpec((VIEW_TILE, n), lambda b, t: (b * tiles + t, dy_blk)), _view_spec(1, n),
                  pl.BlockSpec((1, n), lambda b, t: (0, 0))],
        out_specs=[_view_spec(d, n) for d in DILATIONS] + [pl.BlockSpec((1, n), lambda b, t: (0, 0))],
        scratch_shapes=[_tile_scratch(n)],
        compiler_params=_cparams(("arbitrary", "arbitrary")),
    )(dy, x, g)
    return res[:len(DILATIONS)], res[len(DILATIONS)]


FFN_TILE = 1408


def _ffn_in_fwd(h, w4, name):
    T, D = h.shape
    tm, tc = 512, FFN_TILE
    nc = D_FF // tc

    def body(h_ref, wg_ref, wu_ref, gu_ref, act_ref):
        hv = h_ref[...]
        g = jnp.dot(hv, wg_ref[...], preferred_element_type=F32)
        u = jnp.dot(hv, wu_ref[...], preferred_element_type=F32)
        gu_ref[0] = g.astype(gu_ref.dtype)
        gu_ref[1] = u.astype(gu_ref.dtype)
        act_ref[...] = (g * jax.nn.sigmoid(g) * u).astype(act_ref.dtype)

    return pl.pallas_call(
        body, name=name,
        out_shape=[jax.ShapeDtypeStruct((2, T, D_FF), BF16), jax.ShapeDtypeStruct((T, D_FF), BF16)],
        grid=(nc, T // tm),
        in_specs=[pl.BlockSpec((tm, D), lambda j, i: (i, 0)),
                  pl.BlockSpec((None, D, tc), lambda j, i: (j, 0, 0)),
                  pl.BlockSpec((None, D, tc), lambda j, i: (j + nc, 0, 0))],
        out_specs=[pl.BlockSpec((2, tm, tc), lambda j, i: (0, i, j)), pl.BlockSpec((tm, tc), lambda j, i: (i, j))],
        compiler_params=_cparams(("parallel", "parallel")),
    )(h, w4, w4)


def _ffn_out_bwd(df, w_out, gu, name):
    T, D = df.shape
    tm, tc = 512, FFN_TILE

    def body(df_ref, w_ref, gu_ref, dgu_ref):
        da = _dot_nt(df_ref[...], w_ref[...])
        g, u = gu_ref[0].astype(F32), gu_ref[1].astype(F32)
        sg = jax.nn.sigmoid(g)
        dgu_ref[0] = (da * u * (sg * (1.0 + g * (1.0 - sg)))).astype(dgu_ref.dtype)
        dgu_ref[1] = (da * (g * sg)).astype(dgu_ref.dtype)

    halves = pl.BlockSpec((2, tm, tc), lambda j, i: (0, i, j))
    return pl.pallas_call(
        body, name=name, out_shape=jax.ShapeDtypeStruct((2, T, D_FF), BF16), grid=(D_FF // tc, T // tm),
        in_specs=[pl.BlockSpec((tm, D), lambda j, i: (i, 0)), pl.BlockSpec((tc, D), lambda j, i: (j, 0)), halves],
        out_specs=halves,
        compiler_params=_cparams(("parallel", "parallel")),
    )(df, w_out, gu)


def _final_loss(x1, act, w_out, g2, gf, target, name):
    B, S, D = x1.shape
    ts = ROW_TILE
    tiles = S // ts

    def body(x1_ref, act_ref, w_ref, g2_ref, gf_ref, t_ref, dx_ref, df_ref, dg2_ref, dgf_ref, loss_ref):
        b, s = pl.program_id(0), pl.program_id(1)
        fv = jnp.dot(act_ref[...], w_ref[...], preferred_element_type=F32)
        g2v = g2_ref[0]
        gfv = gf_ref[...]
        x2 = x1_ref[0] + g2v * fv
        r = lax.rsqrt(jnp.mean(x2 * x2, axis=-1, keepdims=True) + EPS)
        n = x2 * r
        e = n * gfv - t_ref[0]
        dy = e * (1.0 / D)
        dn = dy * gfv
        dx = r * (dn - n * jnp.mean(dn * n, axis=-1, keepdims=True))
        dx_ref[0] = dx
        df_ref[0] = (dx * g2v).astype(df_ref.dtype)

        @pl.when(s == 0)
        def _():
            dg2_ref[...] = jnp.zeros_like(dg2_ref)

        @pl.when((s == 0) & (b == 0))
        def _():
            dgf_ref[...] = jnp.zeros_like(dgf_ref)
            loss_ref[...] = jnp.zeros_like(loss_ref)

        dg2_ref[0] += jnp.sum(dx * fv, axis=0, keepdims=True)
        dgf_ref[...] += jnp.sum(dy * n, axis=0, keepdims=True)
        loss_ref[...] += 0.5 * jnp.sum(jnp.mean(e * e, axis=-1, keepdims=True), axis=0, keepdims=True)

    tok = pl.BlockSpec((1, ts, D), lambda b, s: (b, s, 0))
    per_b = pl.BlockSpec((1, 1, D), lambda b, s: (b, 0, 0))
    vec = pl.BlockSpec((1, D), lambda b, s: (0, 0))
    return pl.pallas_call(
        body, name=name,
        out_shape=[jax.ShapeDtypeStruct((B, S, D), F32), jax.ShapeDtypeStruct((B, S, D), BF16),
                   jax.ShapeDtypeStruct((B, 1, D), F32), jax.ShapeDtypeStruct((1, D), F32),
                   jax.ShapeDtypeStruct((1, LANES), F32)],
        grid=(B, tiles),
        in_specs=[tok, pl.BlockSpec((ts, act.shape[1]), lambda b, s: (b * tiles + s, 0)),
                  pl.BlockSpec(w_out.shape, lambda b, s: (0, 0)), per_b, vec, tok],
        out_specs=[tok, tok, per_b, vec, pl.BlockSpec((1, LANES), lambda b, s: (0, 0))],
        compiler_params=_cparams(("arbitrary", "arbitrary")),
    )(x1, act, w_out, g2, gf, target)


def _rope_tables():
    half = ROPE_DIM // 2
    inv = ROPE_THETA ** (-jnp.arange(half, dtype=F32) / half)
    ang = jnp.arange(SEQ, dtype=F32)[:, None] * inv[None, :]
    cos, sin = jnp.cos(ang), jnp.sin(ang)
    one = jnp.ones((SEQ, NOPE_DIM), F32)
    zero = jnp.zeros((SEQ, NOPE_DIM), F32)
    cs = jnp.concatenate([one, cos, cos, one[:, :LANES - NOPE_DIM - ROPE_DIM]], axis=1)
    sn = jnp.concatenate([zero, -sin, sin, zero[:, :LANES - NOPE_DIM - ROPE_DIM]], axis=1)
    return cs, sn


def _rope_group(t, cs, sn):
    half = ROPE_DIM // 2
    lane = lax.broadcasted_iota(jnp.int32, t.shape, 1)
    partner = jnp.where(lane < NOPE_DIM + half, pltpu.roll(t, LANES - half, 1), pltpu.roll(t, half, 1))
    return t * cs + partner * sn


def _mla_proj(cqn, ckvn, rest, w_uq, w_kv, cs, sn, name):
    B, S, _ = rest.shape
    ts, tk = ROW_TILE, MLA_TK
    tiles = S // ts
    G = N_HEADS
    kw = G * LANES
    npair = N_HEADS // 2

    def body(cq_ref, ckv_ref, r_ref, wq_ref, wkv_ref, cs_ref, sn_ref, q_ref, k_ref, v_ref, vt_ref):
        csv, snv = cs_ref[...], sn_ref[...]
        q_raw = jnp.dot(cq_ref[...], wq_ref[...], preferred_element_type=F32)
        kv = jnp.dot(ckv_ref[...], wkv_ref[...], preferred_element_type=F32)
        ra = _rope_group(r_ref[0], csv, snv)
        for gi in range(G):
            sl = slice(gi * LANES, (gi + 1) * LANES)
            q_ref[0, :, sl] = _rope_group(q_raw[:, sl], csv, snv).astype(q_ref.dtype)
            k_ref[0, :, sl] = (kv[:, sl] + ra).astype(k_ref.dtype)
        v = kv[:, kw:]
        v_ref[0] = v.astype(v_ref.dtype)
        for p in range(npair):
            for s in range(ts // tk):
                vt_ref[0, p, s] = jnp.transpose(v[s * tk:(s + 1) * tk, p * LANES:(p + 1) * LANES]).astype(vt_ref.dtype)

    rows = lambda n: pl.BlockSpec((ts, n), lambda b, t: (b * tiles + t, 0))
    full = lambda a: pl.BlockSpec(a.shape, lambda b, t: (0, 0))
    tab = pl.BlockSpec((ts, LANES), lambda b, t: (t, 0))
    tok = lambda n: pl.BlockSpec((1, ts, n), lambda b, t: (b, t, 0))
    return pl.pallas_call(
        body, name=name,
        out_shape=[jax.ShapeDtypeStruct((B, S, kw), BF16), jax.ShapeDtypeStruct((B, S, kw), BF16),
                   jax.ShapeDtypeStruct((B, S, D_B), BF16),
                   jax.ShapeDtypeStruct((B, npair, S // tk, LANES, tk), BF16)],
        grid=(B, tiles),
        in_specs=[rows(Q_LORA), rows(KV_LORA), pl.BlockSpec((1, ts, LANES), lambda b, t: (b, t, KV_LORA // LANES)),
                  full(w_uq), full(w_kv), tab, tab],
        out_specs=[tok(kw), tok(kw), tok(D_B),
                   pl.BlockSpec((1, npair, ts // tk, LANES, tk), lambda b, t: (b, 0, t, 0, 0))],
        compiler_params=_cparams(("parallel", "parallel")),
    )(cqn, ckvn, rest, w_uq, w_kv, cs, sn)


def _mla_proj_bwd(dq_t, dkc, dv, cqn, ckvn, rest, w_uq, w_kv, g_cq, g_ckv, cs, sn_neg, name):
    B, S, _ = rest.shape
    npair, tq = dq_t.shape[1], dq_t.shape[-1]
    ts = ROW_TILE
    tiles = S // ts
    kw = N_HEADS * LANES
    cq_lo = KV_LORA + LANES

    def rms_bwd(dy, xv, g_ref):
        r = lax.rsqrt(jnp.mean(xv * xv, axis=-1, keepdims=True) + EPS)
        nrm = xv * r
        dn = dy * g_ref[...]
        return r * (dn - nrm * jnp.mean(dn * nrm, axis=-1, keepdims=True)), jnp.sum(dy * nrm, axis=0, keepdims=True)

    def body(dqt_ref, dk_ref, dv_ref, cq_ref, ckv_ref, r_ref, wq_ref, wkv_ref, gcq_ref, gckv_ref, cs_ref, sn_ref,
             dr_ref, gwq_ref, gwkv_ref, dgcq_ref, dgckv_ref, dq_s):
        @pl.when((pl.program_id(0) == 0) & (pl.program_id(1) == 0))
        def _():
            for ref in (gwq_ref, gwkv_ref, dgcq_ref, dgckv_ref):
                ref[...] = jnp.zeros_like(ref)

        csv, snv = cs_ref[...], sn_ref[...]
        for p in range(npair):
            for s in range(ts // tq):
                tile = jnp.transpose(dqt_ref[0, p, s])
                rows = slice(s * tq, (s + 1) * tq)
                for hh in range(2):
                    lo = (2 * p + hh) * LANES
                    dq_s[rows, lo:lo + LANES] = _rope_group(tile[:, hh * LANES:(hh + 1) * LANES], csv[rows],
                                                            snv[rows]).astype(dq_s.dtype)
        dq_raw = dq_s[...]
        dkcv = dk_ref[0]
        dkv = jnp.concatenate([dkcv, dv_ref[0]], axis=1).astype(BF16)
        cqn, ckvn = cq_ref[...], ckv_ref[...]
        gwq_ref[...] += _dot_tn(cqn, dq_raw)
        gwkv_ref[...] += _dot_tn(ckvn, dkv)
        restv = r_ref[0]
        dcq, dg = rms_bwd(_dot_nt(dq_raw, wq_ref[...]), restv[:, cq_lo:], gcq_ref)
        dgcq_ref[...] += dg
        dckv, dg = rms_bwd(_dot_nt(dkv, wkv_ref[...]), restv[:, :KV_LORA], gckv_ref)
        dgckv_ref[...] += dg
        acc = dkcv[:, 0:LANES]
        for gi in range(1, N_HEADS):
            acc = acc + dkcv[:, gi * LANES:(gi + 1) * LANES]
        lane = lax.broadcasted_iota(jnp.int32, acc.shape, 1)
        acc = jnp.where((lane >= NOPE_DIM) & (lane < NOPE_DIM + ROPE_DIM), acc, 0.0)
        dr_ref[0, :, 0:KV_LORA] = dckv.astype(dr_ref.dtype)
        dr_ref[0, :, KV_LORA:cq_lo] = _rope_group(acc, csv, snv).astype(dr_ref.dtype)
        dr_ref[0, :, cq_lo:] = dcq.astype(dr_ref.dtype)

    rows = lambda n: pl.BlockSpec((ts, n), lambda b, t: (b * tiles + t, 0))
    full = lambda a: pl.BlockSpec(a.shape, lambda b, t: (0, 0))
    tab = pl.BlockSpec((ts, LANES), lambda b, t: (t, 0))
    tok = lambda n: pl.BlockSpec((1, ts, n), lambda b, t: (b, t, 0))
    acc_out = lambda shp: pl.BlockSpec(shp, lambda b, t: (0, 0))
    out_shape = [jax.ShapeDtypeStruct((B, S, P_REST), BF16), jax.ShapeDtypeStruct(w_uq.shape, F32),
                 jax.ShapeDtypeStruct(w_kv.shape, F32), jax.ShapeDtypeStruct((1, Q_LORA), F32),
                 jax.ShapeDtypeStruct((1, KV_LORA), F32)]
    return pl.pallas_call(
        body, name=name, out_shape=out_shape, grid=(B, tiles),
        in_specs=[pl.BlockSpec((1, npair, ts // tq, 2 * LANES, tq), lambda b, t: (b, 0, t, 0, 0)), tok(kw),
                  tok(D_B), rows(Q_LORA), rows(KV_LORA), tok(P_REST), full(w_uq), full(w_kv), full(g_cq),
                  full(g_ckv), tab, tab],
        out_specs=[tok(P_REST)] + [acc_out(o.shape) for o in out_shape[1:]],
        scratch_shapes=[pltpu.VMEM((ts, kw), BF16)],
        compiler_params=_cparams(("arbitrary", "arbitrary")),
    )(dq_t, dkc, dv, cqn, ckvn, rest, w_uq, w_kv, g_cq, g_ckv, cs, sn_neg)


def _t5_bucket(dist):
    max_exact = N_BUCKETS // 2
    d = np.maximum(dist, 1).astype(np.float64)
    large = max_exact + (np.log(d / max_exact) / np.log(MAX_DISTANCE / max_exact)
                         * (N_BUCKETS - max_exact)).astype(np.int64)
    large = np.minimum(large, N_BUCKETS - 1)
    return np.where(dist < max_exact, dist, large).astype(np.int32)


def _band_buckets(dilation):
    a = np.arange(BLK)[None, :]
    bk = np.arange(2 * BLK)[:, None]
    steps = BLK + a - bk
    return _t5_bucket(np.clip(steps, 0, SPAN) * dilation)


def _head_mask(shape, hh):
    lane = lax.broadcasted_iota(jnp.int32, shape, 1)
    return (lane >= hh * HEAD_DIM) & (lane < (hh + 1) * HEAD_DIM)


def _dot_nt(a, b):
    return lax.dot_general(a, b, (((1,), (1,)), ((), ())), preferred_element_type=F32)


def _dot_tn(a, b):
    return lax.dot_general(a, b, (((0,), (0,)), ((), ())), preferred_element_type=F32)


def _dot_nn(a, b):
    return lax.dot_general(a, b, (((1,), (0,)), ((), ())), preferred_element_type=F32)


def _dil_fwd(qkv, bias, branch, dilation, name, comm=None):
    B, n, _ = qkv.shape
    d = dilation
    nb = n // BLK
    qkv_v = qkv
    npair = N_HEADS // 2

    def body(cur_ref, prev_ref, bias_ref, o_ref, lse_ref, s_scr, e_scr):
        first = jnp.where(pl.program_id(1) == 0, 1, 0)
        units = [(b, h) for b in range(B) for h in range(N_HEADS)]
        for b in range(B):
            for p in range(npair):
                q = cur_ref[b, :, p * LANES:(p + 1) * LANES] * DIL_SCALE
                kc = cur_ref[b, :, D_A + p * LANES:D_A + (p + 1) * LANES]
                kp = prev_ref[b, :, D_A + p * LANES:D_A + (p + 1) * LANES]
                for hh in range(2):
                    u = b * N_HEADS + 2 * p + hh
                    qm = jnp.where(_head_mask((BLK, LANES), hh), q, jnp.zeros_like(q))
                    s_scr[u, 0:BLK, :] = _dot_nt(kp, qm)
                    s_scr[u, BLK:2 * BLK, :] = _dot_nt(kc, qm)
        ms = []
        for u, (b, h) in enumerate(units):
            s_p = s_scr[u, 0:BLK, :] + bias_ref[first, h, 0:BLK, :]
            s_c = s_scr[u, BLK:2 * BLK, :] + bias_ref[first, h, BLK:2 * BLK, :]
            m = jnp.maximum(jnp.max(s_p, axis=0, keepdims=True), jnp.max(s_c, axis=0, keepdims=True))
            e_scr[u, 0:BLK, :] = jnp.exp(s_p - m).astype(BF16)
            e_scr[u, BLK:2 * BLK, :] = jnp.exp(s_c - m).astype(BF16)
            ms.append(m)
        rows0 = _row_mask((LANES, BLK), 0)
        for b in range(B):
            for p in range(npair):
                sl = slice(p * LANES, (p + 1) * LANES)
                vsl = slice(2 * D_A + p * LANES, 2 * D_A + (p + 1) * LANES)
                vct = jnp.transpose(cur_ref[b, :, vsl].astype(F32)).astype(BF16)
                vpt = jnp.transpose(prev_ref[b, :, vsl].astype(F32)).astype(BF16)
                acc = []
                for hh in range(2):
                    u = b * N_HEADS + 2 * p + hh
                    mine = _row_mask((LANES, BLK), hh)
                    one = jnp.ones_like(vct)
                    acc.append(_dot_nn(jnp.where(mine, vpt, one), e_scr[u, 0:BLK, :])
                               + _dot_nn(jnp.where(mine, vct, one), e_scr[u, BLK:2 * BLK, :]))
                l0 = acc[0][HEAD_DIM:HEAD_DIM + 1, :]
                l1 = acc[1][0:1, :]
                u0 = b * N_HEADS + 2 * p
                o_t = jnp.where(rows0, acc[0] / l0, acc[1] / l1)
                lse_t = jnp.where(rows0, ms[u0] + jnp.log(l0), ms[u0 + 1] + jnp.log(l1))
                o_ref[b, :, sl] = jnp.transpose(o_t)
                lse_ref[b, :, sl] = jnp.transpose(lse_t)

    cur = pl.BlockSpec((B, BLK, P_QKV), lambda r, i: (0, i, r))
    prev = pl.BlockSpec((B, BLK, P_QKV), lambda r, i: (0, jnp.maximum(i - 1, 0), r))
    out = pl.BlockSpec((B, BLK, D_A), lambda r, i: (0, i, r))
    return _host_call(
        body, comm, name=name,
        out_shape=[jax.ShapeDtypeStruct((B, n, d * D_A), F32)] * 2,
        grid=(d, nb),
        in_specs=[cur, prev,
                  pl.BlockSpec((None, 2, N_HEADS, 2 * BLK, BLK), lambda r, i: (branch, 0, 0, 0, 0))],
        out_specs=[out, out],
        scratch_shapes=[pltpu.VMEM((B * N_HEADS, 2 * BLK, BLK), F32),
                        pltpu.VMEM((B * N_HEADS, 2 * BLK, BLK), BF16)],
        args=(qkv_v, qkv_v, bias))


VIEW_TILE = 512


def _view_spec(d, w):
    return pl.BlockSpec((1, VIEW_TILE // d, d * w), lambda b, t: (b, t, 0))


def _view_shape(B, S, d, w, dtype):
    return jax.ShapeDtypeStruct((B, S // d, d * w), dtype)


def _tile_scratch(w):
    return pltpu.VMEM((w // LANES, VIEW_TILE, LANES), F32)


def _put_tile(tile_ref, val):
    for c in range(tile_ref.shape[0]):
        tile_ref[c] = val[:, c * LANES:(c + 1) * LANES]


def _get_tile(tile_ref):
    return jnp.concatenate([tile_ref[c] for c in range(tile_ref.shape[0])], axis=1)


def _tile_to_view(tile_ref, view_ref, d, w):
    for c in range(w // LANES):
        for r in range(d):
            lo = r * w + c * LANES
            rows = tile_ref.at[c][pl.ds(r, VIEW_TILE // d, stride=d), :]
            view_ref[0, :, lo:lo + LANES] = rows.astype(view_ref.dtype)


def _view_to_tile(view_ref, tile_ref, d, w):
    for c in range(w // LANES):
        for r in range(d):
            lo = r * w + c * LANES
            tile_ref.at[c][pl.ds(r, VIEW_TILE // d, stride=d), :] = view_ref[0, :, lo:lo + LANES].astype(F32)


def _in_proj(h, w_qkv, w_rest, g_cq, g_ckv, name):
    B, S, D = h.shape
    N = w_qkv.shape[1]
    cq_lo = KV_LORA + LANES

    def rms(xv, g_ref):
        return ((xv * lax.rsqrt(jnp.mean(xv * xv, axis=-1, keepdims=True) + EPS)) * g_ref[...]).astype(BF16)

    def body(h_ref, wq_ref, wr_ref, gcq_ref, gckv_ref, o1_ref, o4_ref, o16_ref, rest_ref, cqn_ref, ckvn_ref,
             acc_ref):
        hv = h_ref[0]
        acc = jnp.dot(hv, wq_ref[...], preferred_element_type=F32)
        o1_ref[0] = acc.astype(o1_ref.dtype)
        _put_tile(acc_ref, acc)
        _tile_to_view(acc_ref, o4_ref, DILATIONS[1], N)
        _tile_to_view(acc_ref, o16_ref, DILATIONS[2], N)
        rest = jnp.dot(hv, wr_ref[...], preferred_element_type=F32)
        rest_ref[0] = rest
        ckvn_ref[...] = rms(rest[:, :KV_LORA], gckv_ref)
        cqn_ref[...] = rms(rest[:, cq_lo:], gcq_ref)

    tiles = S // VIEW_TILE
    full = lambda a: pl.BlockSpec(a.shape, lambda b, t: (0, 0))
    rows = lambda n: pl.BlockSpec((VIEW_TILE, n), lambda b, t: (b * tiles + t, 0))
    res = pl.pallas_call(
        body, name=name,
        out_shape=[_view_shape(B, S, d, N, BF16) for d in DILATIONS]
        + [jax.ShapeDtypeStruct((B, S, P_REST), F32), jax.ShapeDtypeStruct((B * S, Q_LORA), BF16),
           jax.ShapeDtypeStruct((B * S, KV_LORA), BF16)],
        grid=(B, tiles),
        in_specs=[pl.BlockSpec((1, VIEW_TILE, D), lambda b, t: (b, t, 0)), full(w_qkv), full(w_rest), full(g_cq),
                  full(g_ckv)],
        out_specs=[_view_spec(d, N) for d in DILATIONS] + [_view_spec(1, P_REST), rows(Q_LORA), rows(KV_LORA)],
        scratch_shapes=[_tile_scratch(N)],
        compiler_params=_cparams(("parallel", "parallel")),
    )(h, w_qkv, w_rest, g_cq, g_ckv)
    return res[:len(DILATIONS)], res[len(DILATIONS)], res[len(DILATIONS) + 1], res[len(DILATIONS) + 2]


def _dil_merge(os_, lses, name):
    B, S, W = os_[0].shape
    nd = len(DILATIONS)

    def body(*refs):
        o_refs, l_refs = refs[:nd], refs[nd:2 * nd]
        out_refs, L_refs = refs[2 * nd:3 * nd], refs[3 * nd:4 * nd]
        scr = refs[4 * nd:]
        o_tok, l_tok = [o_refs[0][0]], [l_refs[0][0]]
        for i, d in enumerate(DILATIONS[1:]):
            _view_to_tile(o_refs[i + 1], scr[2 * i], d, W)
            _view_to_tile(l_refs[i + 1], scr[2 * i + 1], d, W)
            o_tok.append(_get_tile(scr[2 * i]))
            l_tok.append(_get_tile(scr[2 * i + 1]))
        a0, a1, a2 = l_tok
        m = jnp.maximum(jnp.maximum(a0, a1), a2)
        e0, e1, e2 = jnp.exp(a0 - m), jnp.exp(a1 - m), jnp.exp(a2 - m)
        ssum = e0 + e1 + e2
        out = (e0 * o_tok[0] + e1 * o_tok[1] + e2 * o_tok[2]) / ssum
        lse = m + jnp.log(ssum)
        out_refs[0][0] = out
        L_refs[0][0] = lse
        res_o, res_l = scr[2 * (nd - 1)], scr[2 * (nd - 1) + 1]
        _put_tile(res_o, out)
        _put_tile(res_l, lse)
        for i, d in enumerate(DILATIONS[1:]):
            _tile_to_view(res_o, out_refs[i + 1], d, W)
            _tile_to_view(res_l, L_refs[i + 1], d, W)

    specs = [_view_spec(d, W) for d in DILATIONS]
    shapes = [_view_shape(B, S * DILATIONS[0], d, W, F32) for d in DILATIONS]
    res = pl.pallas_call(
        body, name=name, out_shape=shapes * 2, grid=(B, S // VIEW_TILE),
        in_specs=specs * 2, out_specs=specs * 2,
        scratch_shapes=[_tile_scratch(W)] * (2 * nd),
        compiler_params=_cparams(("parallel", "parallel")),
    )(*os_, *lses)
    return res[:nd], res[nd:]


def _dil_bwd(qkv, do, out_a, L, bias, branch, dilation, name, comm=None):
    B, n, _ = qkv.shape
    d = dilation
    nb = n // BLK
    qkv_v, do_v, oa_v, L_v = qkv, do, out_a, L
    npair = N_HEADS // 2
    multi = nb > 1

    tiles = ("P", "C", "N") if multi else ("C",)
    n_t = len(tiles)

    def body(*refs):
        if multi:
            (cur_ref, prev_ref, next_ref, do_ref, don_ref, oa_ref, oan_ref, L_ref, Ln_ref, bias_ref,
             dqkv_ref, dbias_ref, s_scr, dp_scr, p_scr, ds_scr) = refs
        else:
            cur_ref, do_ref, oa_ref, L_ref, bias_ref, dqkv_ref, dbias_ref, s_scr, dp_scr, p_scr, ds_scr = refs
        r, i = pl.program_id(0), pl.program_id(1)

        @pl.when((r == 0) & (i == 0))
        def _():
            dbias_ref[...] = jnp.zeros_like(dbias_ref)

        first = jnp.where(i == 0, 1, 0)
        variant = {"P": first, "C": first, "N": 0}
        band = {"P": slice(0, BLK), "C": slice(BLK, 2 * BLK), "N": slice(0, BLK)}
        psl = lambda p: slice(p * LANES, (p + 1) * LANES)
        ksl = lambda p: slice(D_A + p * LANES, D_A + (p + 1) * LANES)
        vsl = lambda p: slice(2 * D_A + p * LANES, 2 * D_A + (p + 1) * LANES)

        def operands(b, p, hh):
            hm = _head_mask((BLK, LANES), hh)
            mask = lambda x: jnp.where(hm, x, jnp.zeros_like(x))
            qm, dom = mask(cur_ref[b, :, psl(p)] * DIL_SCALE), mask(do_ref[b, :, psl(p)])
            ops = {"C": (cur_ref[b, :, ksl(p)], cur_ref[b, :, vsl(p)], qm, dom)}
            if multi:
                ops["P"] = (prev_ref[b, :, ksl(p)], prev_ref[b, :, vsl(p)], qm, dom)
                ops["N"] = (cur_ref[b, :, ksl(p)], cur_ref[b, :, vsl(p)],
                            mask(next_ref[b, :, psl(p)] * DIL_SCALE), mask(don_ref[b, :, psl(p)]))
            return ops

        pairs = [(b, p) for b in range(B) for p in range(npair)]
        for b, p in pairs:
            for hh in range(2):
                u = b * N_HEADS + 2 * p + hh
                ops = operands(b, p, hh)
                for t, name_t in enumerate(tiles):
                    k_t, v_t, q_t, do_t = ops[name_t]
                    s_scr[u, t] = _dot_nt(k_t, q_t)
                    dp_scr[u, t] = _dot_nt(v_t, do_t)

        def rows(L_r, do_r, oa_r, b, p):
            lt = jnp.transpose(L_r[b, :, psl(p)])
            dt = jnp.transpose(do_r[b, :, psl(p)].astype(F32) * oa_r[b, :, psl(p)])
            return ([lt[0:1, :], lt[HEAD_DIM:HEAD_DIM + 1, :]],
                    [jnp.sum(dt[:HEAD_DIM], axis=0, keepdims=True), jnp.sum(dt[HEAD_DIM:], axis=0, keepdims=True)])

        for b, p in pairs:
            lse_c, delta_c = rows(L_ref, do_ref, oa_ref, b, p)
            if multi:
                lse_n, delta_n = rows(Ln_ref, don_ref, oan_ref, b, p)
            for hh in range(2):
                h = 2 * p + hh
                u = b * N_HEADS + h
                for t, name_t in enumerate(tiles):
                    lse, delta = (lse_n[hh], delta_n[hh]) if name_t == "N" else (lse_c[hh], delta_c[hh])
                    pr = jnp.exp(s_scr[u, t] + bias_ref[variant[name_t], h, band[name_t], :] - lse)
                    if name_t == "N":
                        pr = jnp.where(i < nb - 1, pr, 0.0)
                    ds = pr * (dp_scr[u, t] - delta)
                    p_scr[u, t] = pr.astype(BF16)
                    ds_scr[u, t] = ds.astype(BF16)
                    if name_t != "N":
                        dbias_ref[h, band[name_t], :] += ds

        for b, p in pairs:
            dqt = jnp.zeros((LANES, BLK), F32)
            dk = jnp.zeros((BLK, LANES), F32)
            dv = jnp.zeros((BLK, LANES), F32)
            kct = jnp.transpose(cur_ref[b, :, ksl(p)].astype(F32)).astype(BF16)
            if multi:
                kpt = jnp.transpose(prev_ref[b, :, ksl(p)].astype(F32)).astype(BF16)
            for hh in range(2):
                u = b * N_HEADS + 2 * p + hh
                ops = operands(b, p, hh)
                mine = _row_mask((LANES, BLK), hh)
                for t, name_t in enumerate(tiles):
                    _, _, q_t, do_t = ops[name_t]
                    if name_t != "P":
                        dv = dv + _dot_nn(p_scr[u, t], do_t)
                        dk = dk + _dot_nn(ds_scr[u, t], q_t)
                    if name_t != "N":
                        kt = kpt if name_t == "P" else kct
                        dqt = dqt + _dot_nn(jnp.where(mine, kt, jnp.zeros_like(kt)), ds_scr[u, t])
            dqkv_ref[b, :, psl(p)] = jnp.transpose(dqt) * DIL_SCALE
            dqkv_ref[b, :, ksl(p)] = dk
            dqkv_ref[b, :, vsl(p)] = dv

    def at(off):
        return lambda r, i: (0, jnp.clip(i + off, 0, nb - 1), r)

    qkv_spec = lambda off: pl.BlockSpec((B, BLK, P_QKV), at(off))
    da_spec = lambda off: pl.BlockSpec((B, BLK, D_A), at(off))
    bias_spec = pl.BlockSpec((None, 2, N_HEADS, 2 * BLK, BLK), lambda r, i: (branch, 0, 0, 0, 0))
    dbias_spec = pl.BlockSpec((N_HEADS, 2 * BLK, BLK), lambda r, i: (0, 0, 0))
    if multi:
        in_specs = [qkv_spec(0), qkv_spec(-1), qkv_spec(1), da_spec(0), da_spec(1), da_spec(0), da_spec(1),
                    da_spec(0), da_spec(1), bias_spec]
        args = [qkv_v, qkv_v, qkv_v, do_v, do_v, oa_v, oa_v, L_v, L_v, bias]
    else:
        in_specs = [qkv_spec(0), da_spec(0), da_spec(0), da_spec(0), bias_spec]
        args = [qkv_v, do_v, oa_v, L_v, bias]
    return _host_call(
        body, comm, name=name,
        out_shape=[jax.ShapeDtypeStruct((B, n, d * P_QKV), F32),
                   jax.ShapeDtypeStruct((N_HEADS, 2 * BLK, BLK), F32)],
        grid=(d, nb),
        in_specs=in_specs,
        out_specs=[qkv_spec(0), dbias_spec],
        scratch_shapes=[pltpu.VMEM((B * N_HEADS, n_t, BLK, BLK), F32), pltpu.VMEM((B * N_HEADS, n_t, BLK, BLK), F32),
                        pltpu.VMEM((B * N_HEADS, n_t, BLK, BLK), BF16),
                        pltpu.VMEM((B * N_HEADS, n_t, BLK, BLK), BF16)],
        args=args)


def _sum_views_bf16(parts, name):
    B, S, W = parts[0].shape

    def body(a_ref, b_ref, c_ref, o_ref, sb, sc):
        _view_to_tile(b_ref, sb, DILATIONS[1], W)
        _view_to_tile(c_ref, sc, DILATIONS[2], W)
        o_ref[0] = (a_ref[0] + _get_tile(sb) + _get_tile(sc)).astype(o_ref.dtype)

    return pl.pallas_call(
        body, name=name, out_shape=jax.ShapeDtypeStruct((B, S, W), BF16), grid=(B, S // VIEW_TILE),
        in_specs=[_view_spec(d, W) for d in DILATIONS], out_specs=_view_spec(1, W),
        scratch_shapes=[_tile_scratch(W)] * 2,
        compiler_params=_cparams(("parallel", "parallel")),
    )(*parts)


def _bias_tables(rel_bias, buckets, name, comm=None):
    nbr = buckets.shape[0]

    def body(rb_ref, bk_ref, o_ref):
        first, h = pl.program_id(1), pl.program_id(2)
        tab = bk_ref[0]

        def step(bkt, acc):
            return jnp.where(tab == bkt, rb_ref[bkt, h], acc)

        bias = lax.fori_loop(0, N_BUCKETS, step, jnp.zeros((2 * BLK, BLK), F32))
        row = lax.broadcasted_iota(jnp.int32, (2 * BLK, BLK), 0)
        col = lax.broadcasted_iota(jnp.int32, (2 * BLK, BLK), 1)
        valid = ((row < BLK) & (row >= col) & (first == 0)) | ((row >= BLK) & (row - BLK <= col))
        o_ref[0, 0, 0] = jnp.where(valid, bias, NEG)

    (bias,), got = _host_call(
        body, comm, name=name, out_shape=[jax.ShapeDtypeStruct((nbr, 2, N_HEADS, 2 * BLK, BLK), F32)],
        grid=(nbr, 2, N_HEADS),
        in_specs=[pl.BlockSpec(memory_space=pltpu.SMEM),
                  pl.BlockSpec((1, 2 * BLK, BLK), lambda i, f, h: (i, 0, 0))],
        out_specs=[pl.BlockSpec((1, 1, 1, 2 * BLK, BLK), lambda i, f, h: (i, f, h, 0, 0))],
        scratch_shapes=[], args=(rel_bias, buckets))
    return bias, got


def _bias_grad(dbias_list, buckets, name):
    nbr = len(dbias_list)

    def body(*refs):
        d_refs, bk_ref, o_ref, part = refs[:nbr], refs[nbr], refs[nbr + 1], refs[nbr + 2]

        def step(bkt, carry):
            hit = [bk_ref[bi] == bkt for bi in range(nbr)]
            for h in range(N_HEADS):
                tot = jnp.zeros((1, BLK), F32)
                for bi in range(nbr):
                    tot = tot + jnp.sum(jnp.where(hit[bi], d_refs[bi][h], 0.0), axis=0, keepdims=True)
                part[bkt, h:h + 1, :] = tot
            return carry

        lax.fori_loop(0, N_BUCKETS, step, 0)
        lane = lax.broadcasted_iota(jnp.int32, (N_HEADS, LANES), 1)
        acc = jnp.zeros((N_HEADS, LANES), F32)
        for bkt in range(N_BUCKETS):
            acc = acc + jnp.where(lane == bkt, jnp.sum(part[bkt], axis=1, keepdims=True), 0.0)
        o_ref[...] = acc

    band = pl.BlockSpec((N_HEADS, 2 * BLK, BLK), lambda i: (0, 0, 0))
    return pl.pallas_call(
        body, name=name, out_shape=jax.ShapeDtypeStruct((N_HEADS, LANES), F32), grid=(1,),
        in_specs=[band] * nbr + [pl.BlockSpec((nbr, 2 * BLK, BLK), lambda i: (0, 0, 0))],
        out_specs=pl.BlockSpec((N_HEADS, LANES), lambda i: (0, 0)),
        scratch_shapes=[pltpu.VMEM((N_BUCKETS, N_HEADS, BLK), F32)],
        compiler_params=_cparams(("arbitrary",)),
    )(*dbias_list, buckets)


MLA_TQ = 256
MLA_TK = 256


LOG2E = math.log2(math.e)
MLA_C = MLA_SCALE * LOG2E


def _key_le_query(tk, tq):
    return lax.broadcasted_iota(jnp.int32, (tk, tq), 0) <= lax.broadcasted_iota(jnp.int32, (tk, tq), 1)


def _row_mask(shape, hh):
    row = lax.broadcasted_iota(jnp.int32, shape, 0)
    return (row >= hh * HEAD_DIM) & (row < (hh + 1) * HEAD_DIM)


def _host_call(body, comm, *, name, grid, in_specs, out_specs, out_shape, scratch_shapes, args):
    sem = ("arbitrary",) * len(grid)
    if comm is None:
        res = pl.pallas_call(body, name=name, grid=grid, in_specs=in_specs, out_specs=out_specs,
                             out_shape=out_shape, scratch_shapes=scratch_shapes,
                             compiler_params=_cparams(sem))(*args)
        return res, []
    n_in, n_out, n_s, cn = len(in_specs), len(out_specs), len(scratch_shapes), comm.n

    def hosted(*refs):
        ins, refs = refs[:n_in], refs[n_in:]
        c_ins, refs = refs[:cn], refs[cn:]
        outs, refs = refs[:n_out], refs[n_out:]
        c_outs, refs = refs[:cn], refs[cn:]
        scr, c_sems = refs[:n_s], refs[n_s:]
        ids = [pl.program_id(a) for a in range(len(grid))]
        first = functools.reduce(jnp.logical_and, [i == 0 for i in ids])
        last = functools.reduce(jnp.logical_and, [i == g - 1 for i, g in zip(ids, grid)])

        @pl.when(first)
        def _():
            comm.start(c_ins, c_outs, c_sems)

        body(*ins, *outs, *scr)

        @pl.when(last)
        def _():
            comm.finish(c_ins, c_outs, c_sems)

    res = pl.pallas_call(
        hosted, name=name, grid=grid, in_specs=list(in_specs) + _hbm_specs(cn),
        out_specs=list(out_specs) + _hbm_specs(cn), out_shape=list(out_shape) + list(comm.out_shape),
        scratch_shapes=list(scratch_shapes) + list(comm.scratch), compiler_params=_cparams(sem),
    )(*args, *comm.inputs)
    return res[:n_out], res[n_out:]


def _mla_fwd_t(q, k, vt, name, comm=None):
    B, S, _ = q.shape
    tq, tk = MLA_TQ, MLA_TK
    assert tq == tk
    npair = N_HEADS // 2
    nq = S // tq

    def body(q_ref, k_ref, vt_ref, o_ref, lse_ref, s_scr, e_scr, acc_scr, m_scr, a_scr):
        i = pl.program_id(1)
        diag = _key_le_query(tk, tq)
        m_scr[...] = jnp.full_like(m_scr, NEG)
        acc_scr[...] = jnp.zeros_like(acc_scr)

        def step(j, masked):
            rows = pl.ds(pl.multiple_of(j * tk, tk), tk)
            for h in range(N_HEADS):
                hsl = slice(h * LANES, (h + 1) * LANES)
                s_scr[h] = _dot_nt(k_ref[0, rows, hsl], q_ref[0, :, hsl])
            for h in range(N_HEADS):
                s = s_scr[h]
                if masked:
                    s = jnp.where(diag, s, NEG)
                m_old = m_scr[h:h + 1, :]
                m_new = jnp.maximum(m_old, jnp.max(s, axis=0, keepdims=True))
                a_scr[h:h + 1, :] = jnp.exp2((m_old - m_new) * MLA_C)
                e_scr[h] = jnp.exp2((s - m_new) * MLA_C).astype(BF16)
                m_scr[h:h + 1, :] = m_new
            for h in range(N_HEADS):
                vj = vt_ref[0, h // 2, j]
                vh = jnp.where(_row_mask(vj.shape, h % 2), vj, jnp.ones_like(vj))
                acc_scr[h] = acc_scr[h] * a_scr[h:h + 1, :] + _dot_nn(vh, e_scr[h])

        def loop_body(j, carry):
            step(j, False)
            return carry

        lax.fori_loop(0, i, loop_body, 0)
        step(i, True)
        rows0 = _row_mask((LANES, tq), 0)
        for p in range(npair):
            l0 = acc_scr[2 * p, HEAD_DIM:HEAD_DIM + 1, :]
            l1 = acc_scr[2 * p + 1, 0:1, :]
            o_t = jnp.where(rows0, acc_scr[2 * p] / l0, acc_scr[2 * p + 1] / l1)
            o_ref[0, :, p * LANES:(p + 1) * LANES] = jnp.transpose(o_t)
            lse_ref[0, p, 0] = jnp.zeros((8, tq), F32)
            lse_ref[0, p, 0, 0:1, :] = m_scr[2 * p:2 * p + 1, :] * MLA_C + jnp.log(l0) * LOG2E
            lse_ref[0, p, 0, 1:2, :] = m_scr[2 * p + 1:2 * p + 2, :] * MLA_C + jnp.log(l1) * LOG2E

    return _host_call(
        body, comm, name=name,
        out_shape=[jax.ShapeDtypeStruct((B, S, D_B), F32), jax.ShapeDtypeStruct((B, npair, nq, 8, tq), F32)],
        grid=(B, nq),
        in_specs=[pl.BlockSpec((1, tq, N_HEADS * LANES), lambda b, i: (b, i, 0)),
                  pl.BlockSpec((1, S, N_HEADS * LANES), lambda b, i: (b, 0, 0)),
                  pl.BlockSpec((1, npair, S // tk, LANES, tk), lambda b, i: (b, 0, 0, 0, 0))],
        out_specs=[pl.BlockSpec((1, tq, D_B), lambda b, i: (b, i, 0)),
                   pl.BlockSpec((1, npair, 1, 8, tq), lambda b, i: (b, 0, i, 0, 0))],
        scratch_shapes=[pltpu.VMEM((N_HEADS, tk, tq), F32), pltpu.VMEM((N_HEADS, tk, tq), BF16),
                        pltpu.VMEM((N_HEADS, LANES, tq), F32), pltpu.VMEM((N_HEADS, tq), F32),
                        pltpu.VMEM((N_HEADS, tq), F32)],
        args=(q, k, vt))


def _rms_bwd_delta(dy, dy_blk, o, g, name):
    B, S, n = o.shape
    tq = MLA_TQ
    tr = ROW_TILE
    sub = tr // tq
    npair = N_HEADS // 2
    tiles = S // tr

    def body(dy_ref, o_ref, g_ref, do_ref, dg_ref, d_ref):
        ov = o_ref[0]
        dyv = dy_ref[...]
        r = lax.rsqrt(jnp.mean(ov * ov, axis=-1, keepdims=True) + EPS)
        nrm = ov * r
        dn = dyv * g_ref[...]
        do = (r * (dn - nrm * jnp.mean(dn * nrm, axis=-1, keepdims=True))).astype(do_ref.dtype)
        do_ref[0] = do

        @pl.when((pl.program_id(0) == 0) & (pl.program_id(1) == 0))
        def _():
            dg_ref[...] = jnp.zeros_like(dg_ref)

        dg_ref[...] += jnp.sum(dyv * nrm, axis=0, keepdims=True)
        prod = do.astype(F32) * ov
        d_ref[...] = jnp.zeros_like(d_ref)
        for p in range(npair):
            for s in range(sub):
                prod_t = jnp.transpose(prod[s * tq:(s + 1) * tq, p * LANES:(p + 1) * LANES])
                d_ref[0, p, s, 0:1, :] = jnp.sum(prod_t[:HEAD_DIM], axis=0, keepdims=True)
                d_ref[0, p, s, 1:2, :] = jnp.sum(prod_t[HEAD_DIM:], axis=0, keepdims=True)

    tok = pl.BlockSpec((1, tr, n), lambda b, t: (b, t, 0))
    return pl.pallas_call(
        body, name=name,
        out_shape=[jax.ShapeDtypeStruct((B, S, n), BF16), jax.ShapeDtypeStruct((1, n), F32),
                   jax.ShapeDtypeStruct((B, npair, S // tq, 8, tq), F32)],
        grid=(B, tiles),
        in_specs=[pl.BlockSpec((tr, n), lambda b, t: (b * tiles + t, dy_blk)), tok,
                  pl.BlockSpec((1, n), lambda b, t: (0, 0))],
        out_specs=[tok, pl.BlockSpec((1, n), lambda b, t: (0, 0)),
                   pl.BlockSpec((1, npair, sub, 8, tq), lambda b, t: (b, 0, t, 0, 0))],
        compiler_params=_cparams(("arbitrary", "arbitrary")),
    )(dy, o, g)


def _mla_bwd_t(q, k, v, do, lse, delta, name, comm=None):
    B, S, _ = q.shape
    tq, tk = MLA_TQ, MLA_TK
    assert tq == tk
    npair = N_HEADS // 2
    nq = S // tq

    hg = N_HEADS
    pg = hg // 2
    ngroup = N_HEADS // hg

    def body(q_ref, do_ref, lse_ref, dl_ref, k_ref, v_ref, dk_ref, dv_ref, dq_ref,
             s_scr, dp_scr, p_scr, ds_scr, dk_s, dv_s, kt_s):
        j = pl.program_id(2)

        @pl.when(j == 0)
        def _():
            dq_ref[...] = jnp.zeros_like(dq_ref)

        dk_s[...] = jnp.zeros_like(dk_s)
        dv_s[...] = jnp.zeros_like(dv_s)
        diag = _key_le_query(tk, tq)
        hsl = lambda h: slice(h * LANES, (h + 1) * LANES)
        for h in range(hg):
            kt_s[h] = jnp.transpose(k_ref[0, :, hsl(h)].astype(F32)).astype(BF16)

        def step(i, masked):
            rows = pl.ds(pl.multiple_of(i * tq, tq), tq)

            def dom(h):
                dov = do_ref[0, rows, hsl(h // 2)]
                return jnp.where(_head_mask((tq, LANES), h % 2), dov, jnp.zeros_like(dov))

            for h in range(hg):
                s_scr[h] = _dot_nt(k_ref[0, :, hsl(h)], q_ref[0, rows, hsl(h)])
                dp_scr[h] = _dot_nt(v_ref[0, :, hsl(h // 2)], dom(h))
            for h in range(hg):
                pr = jnp.exp2(s_scr[h] * MLA_C - lse_ref[0, h // 2, i, h % 2:h % 2 + 1, :])
                if masked:
                    pr = jnp.where(diag, pr, 0.0)
                p_scr[h] = pr.astype(BF16)
                ds_scr[h] = (pr * (dp_scr[h] - dl_ref[0, h // 2, i, h % 2:h % 2 + 1, :])).astype(BF16)
            for h in range(hg):
                dv_s[h // 2] += _dot_nn(p_scr[h], dom(h))
                dk_s[h] += _dot_nn(ds_scr[h], q_ref[0, rows, hsl(h)])
                dq_ref[0, h // 2, i, hsl(h % 2), :] += _dot_nn(kt_s[h], ds_scr[h]) * MLA_SCALE

        step(j, True)

        def loop_body(i, carry):
            step(i, False)
            return carry

        lax.fori_loop(j + 1, nq, loop_body, 0)
        for h in range(hg):
            dk_ref[0, :, hsl(h)] = dk_s[h] * MLA_SCALE
        for p in range(pg):
            dv_ref[0, :, hsl(p)] = dv_s[p]

    stat = pl.BlockSpec((1, pg, nq, 8, tq), lambda b, g, j: (b, g, 0, 0, 0))
    return _host_call(
        body, comm, name=name,
        out_shape=[jax.ShapeDtypeStruct((B, S, N_HEADS * LANES), F32), jax.ShapeDtypeStruct((B, S, D_B), F32),
                   jax.ShapeDtypeStruct((B, npair, nq, 2 * LANES, tq), F32)],
        grid=(B, ngroup, S // tk),
        in_specs=[pl.BlockSpec((1, S, hg * LANES), lambda b, g, j: (b, 0, g)),
                  pl.BlockSpec((1, S, pg * LANES), lambda b, g, j: (b, 0, g)),
                  stat, stat,
                  pl.BlockSpec((1, tk, hg * LANES), lambda b, g, j: (b, j, g)),
                  pl.BlockSpec((1, tk, pg * LANES), lambda b, g, j: (b, j, g))],
        out_specs=[pl.BlockSpec((1, tk, hg * LANES), lambda b, g, j: (b, j, g)),
                   pl.BlockSpec((1, tk, pg * LANES), lambda b, g, j: (b, j, g)),
                   pl.BlockSpec((1, pg, nq, 2 * LANES, tq), lambda b, g, j: (b, g, 0, 0, 0))],
        scratch_shapes=[pltpu.VMEM((hg, tk, tq), F32), pltpu.VMEM((hg, tk, tq), F32),
                        pltpu.VMEM((hg, tk, tq), BF16), pltpu.VMEM((hg, tk, tq), BF16),
                        pltpu.VMEM((hg, tk, LANES), F32), pltpu.VMEM((pg, tk, LANES), F32),
                        pltpu.VMEM((hg, LANES, tk), BF16)],
        args=(q, do, lse, delta, k, v))


def _bucket_tables():
    return jnp.asarray(np.stack([_band_buckets(d) for d in DILATIONS]))


def _local_step(x, target, mod, wts, gains, rel_bias, ffn_shards=None, bias=None):
    B, S, D = x.shape
    T = B * S
    sh1, sc1, g1, sh2, sc2, g2 = [mod[:, i * D:(i + 1) * D].reshape(B, 1, D) for i in range(N_MOD)]
    cs, sn = _rope_tables()
    buckets_dev = _bucket_tables()
    if bias is None:
        bias, _ = _bias_tables(rel_bias, buckets_dev, "rel_bias_tables")
    w_in = wts["w_in"]

    h1 = _adaln_fwd(x, gains["g_norm1"], sc1, sh1, "adaln1_fwd")
    h1f = h1.reshape(T, D)
    qkv_v, rest3, cqn, ckvn = _in_proj(h1, w_in[:, :P_QKV], w_in[:, P_QKV:], gains["g_cq"], gains["g_ckv"],
                                       "mm_in")
    o_d, lse_d = [], []
    late_got = []
    for i, d in enumerate(DILATIONS):
        comm = _GatherComm(ffn_shards[i + 1:i + 2]) if (ffn_shards and i < 2) else None
        (o_i, lse_i), got = _dil_fwd(qkv_v[i], bias, i, d, f"dil_fwd_{d}", comm)
        late_got += list(got)
        o_d.append(o_i)
        lse_d.append(lse_i)
    if ffn_shards:
        wts = dict(wts, w_out=late_got[1].reshape(D, D))
    out_a_v, lse_a_v = _dil_merge(o_d, lse_d, "dil_merge")
    out_a = out_a_v[0]
    qc, kc, v, vt = _mla_proj(cqn, ckvn, rest3, wts["w_uq"], wts["w_kv"], cs, sn, "mla_proj")
    (out_b, lse_b), got = _mla_fwd_t(qc, kc, vt, "mla_fwd", _GatherComm(ffn_shards[:1]) if ffn_shards else None)
    if ffn_shards:
        wts = dict(wts, w_ffn_in=got[0].reshape(N_CHIP, D, -1), w_ffn_out=late_got[0].reshape(D_FF, D))
    out_af, out_bf = out_a.reshape(T, D_A), out_b.reshape(T, D_B)
    y = _rms_fwd_pair(out_af, out_bf, gains["g_out_a"], gains["g_out_b"], "rms_out_fwd")
    mix = _mm(y, wts["w_out"], "nn", F32, "mm_out").reshape(B, S, D)
    h2, x1 = _adaln_fwd(x, gains["g_norm2"], sc2, sh2, "adaln2_fwd", mix=mix, gate=g1)
    h2f = h2.reshape(T, D)
    gu, act = _ffn_in_fwd(h2f, wts["w_ffn_in"], "mm_ffn_in")
    dx2, df, dg2, dg_final, loss = _final_loss(x1, act, wts["w_ffn_out"], g2, gains["g_final"], target,
                                               "ffn_out_loss")

    dff = df.reshape(T, D)
    dgu = _ffn_out_bwd(dff, wts["w_ffn_out"], gu, "mm_ffn_out_dx")
    gw_ffn_out = _mm(act, dff, "tn", F32, "mm_ffn_out_dw")
    dh2 = _mm(dgu, wts["w_ffn_in"], "nt", F32, "mm_ffn_in_dx", col_blocks=N_CHIP, halves=True).reshape(B, S, D)
    gw_ffn_in = _mm(h2f, dgu, "tn", F32, "mm_ffn_in_dw", col_blocks=N_CHIP, halves=True)
    ffn_g8 = ffn_r1 = None
    if ffn_shards:
        ffn_g8 = [gw_ffn_in.reshape(N_DEV, -1, gw_ffn_in.shape[-1]), gw_ffn_out.reshape(N_DEV, -1, D)]
        dx1, dsh2, dsc2, dg_norm2, dg1, dmix, ffn_r1 = _adaln_bwd(
            dh2, x1, gains["g_norm2"], sc2, dx2, "adaln2_bwd", mix=mix, gate=g1, comm=_ToSiblingComm(ffn_g8))
    else:
        dx1, dsh2, dsc2, dg_norm2, dg1, dmix = _adaln_bwd(dh2, x1, gains["g_norm2"], sc2, dx2, "adaln2_bwd",
                                                          mix=mix, gate=g1)
    dmixf = dmix.reshape(T, D)
    dy = _mm(dmixf, wts["w_out"], "nt", F32, "mm_out_dx")
    gw_out = _mm(y, dmixf, "tn", F32, "mm_out_dw")
    do_a_v, dg_out_a = _rms_bwd_views(dy, 0, out_a, gains["g_out_a"], "rms_outa_bwd")
    do_b3, dg_out_b, delta_b = _rms_bwd_delta(dy, 1, out_b, gains["g_out_b"], "rms_outb_bwd")
    ffn_a4 = ffn_send = None
    if ffn_shards:
        ffn_a4, ffn_send = _rs_first(ffn_g8, "ffn", r1=ffn_r1)
    (dkc, dv, dq_t), ffn_r2 = _mla_bwd_t(qc, kc, v, do_b3, lse_b, delta_b, "mla_bwd",
                                         _ToChipsComm(ffn_send[:1]) if ffn_shards else None)
    d_rest, gw_uq, gw_kv, dg_cq, dg_ckv = _mla_proj_bwd(dq_t, dkc, dv, cqn, ckvn, rest3, wts["w_uq"], wts["w_kv"],
                                                        gains["g_cq"], gains["g_ckv"], cs, -sn, "mla_proj_bwd")
    dqkv_d, dbias_d = [], []
    for i, d in enumerate(DILATIONS):
        comm = _ToChipsComm(ffn_send[1:]) if (ffn_shards and i == 0) else None
        (dqkv_i, dbias_i), got = _dil_bwd(qkv_v[i], do_a_v[i], out_a_v[i], lse_a_v[i], bias, i, d,
                                          f"dil_bwd_{d}", comm)
        if comm is not None:
            ffn_r2 = list(ffn_r2) + list(got)
        dqkv_d.append(dqkv_i)
        dbias_d.append(dbias_i)
    dqkv = _sum_views_bf16(dqkv_d, "dil_bwd_sum").reshape(T, P_QKV)
    g_rel_bias = _bias_grad(dbias_d, buckets_dev, "rel_bias_grad")[:, :N_BUCKETS].T
    dproj = jnp.concatenate([dqkv, d_rest.reshape(T, P_REST)], axis=1)
    gw_in = _mm(h1f, dproj, "tn", F32, "mm_in_dw")
    mix_a4 = mix_r2 = None
    if ffn_shards:
        nat = [_w_in_from_kernel(gw_in), _w_uq_from_kernel(gw_uq), _w_ukv_from_kernel(gw_kv)]
        g8 = [_shards_from_full(g) for g in nat] + [gw_out]
        mix_a4, mix_send = _rs_first([g.reshape(N_DEV, -1, g.shape[-1]) for g in g8], "mix")
        dh1, mix_r2 = _mm(dproj, w_in, "nt", F32, "mm_in_dx", comm=_ToChipsComm(mix_send))
    else:
        dh1 = _mm(dproj, w_in, "nt", F32, "mm_in_dx")
    dh1 = dh1.reshape(B, S, D)
    grad_x, dsh1, dsc1, dg_norm1 = _adaln_bwd(dh1, x, gains["g_norm1"], sc1, dx1, "adaln1_bwd")
    gmod = jnp.concatenate([dsh1, dsc1, dg1, dsh2, dsc2, dg2], axis=-1).reshape(B, N_MOD * D)
    grads = dict(w_in=gw_in, w_uq=gw_uq, w_kv=gw_kv, w_out=gw_out, w_ffn_in=gw_ffn_in, w_ffn_out=gw_ffn_out,
                 g_norm1=dg_norm1, g_cq=dg_cq, g_ckv=dg_ckv, rel_bias=g_rel_bias, g_out_a=dg_out_a,
                 g_out_b=dg_out_b, g_norm2=dg_norm2, g_final=dg_final, ffn_pending=(ffn_a4, ffn_r2),
                 mix_pending=(mix_a4, mix_r2))
    return loss, grad_x, gmod, grads


def _w_in_to_kernel(w):
    z = lambda n: jnp.zeros((w.shape[0], n), w.dtype)
    i3, i4, i5 = 3 * D_A, 3 * D_A + Q_LORA, 3 * D_A + Q_LORA + KV_LORA
    return jnp.concatenate([w[:, :i3], w[:, i4:i5], z(NOPE_DIM), w[:, i5:], z(LANES - NOPE_DIM - ROPE_DIM),
                            w[:, i3:i4]], axis=1)


def _w_in_from_kernel(g):
    o = P_QKV + KV_LORA
    return jnp.concatenate([g[:, :P_QKV], g[:, o + LANES:], g[:, P_QKV:o],
                            g[:, o + NOPE_DIM:o + NOPE_DIM + ROPE_DIM]], axis=1)


def _w_uq_to_kernel(w):
    w3 = w.reshape(Q_LORA, N_HEADS, NOPE_DIM + ROPE_DIM)
    return jnp.pad(w3, ((0, 0), (0, 0), (0, LANES - NOPE_DIM - ROPE_DIM))).reshape(Q_LORA, N_HEADS * LANES)


def _w_uq_from_kernel(g):
    return g.reshape(Q_LORA, N_HEADS, LANES)[:, :, :NOPE_DIM + ROPE_DIM].reshape(Q_LORA, -1)


def _w_ukv_to_kernel(w):
    w3 = w.reshape(KV_LORA, N_HEADS, 2 * HEAD_DIM)
    wk = jnp.pad(w3[:, :, :NOPE_DIM], ((0, 0), (0, 0), (0, LANES - NOPE_DIM))).reshape(KV_LORA, N_HEADS * LANES)
    wv = w3[:, :, NOPE_DIM:].reshape(KV_LORA, D_B)
    return jnp.concatenate([wk, wv], axis=1)


def _w_ukv_from_kernel(g):
    gk = g[:, :N_HEADS * LANES].reshape(KV_LORA, N_HEADS, LANES)[:, :, :NOPE_DIM]
    gv = g[:, N_HEADS * LANES:].reshape(KV_LORA, N_HEADS, HEAD_DIM)
    return jnp.concatenate([gk, gv], axis=2).reshape(KV_LORA, -1)


MESH = pl.DeviceIdType.MESH


def _my_place():
    return lax.axis_index("x"), lax.axis_index("y"), lax.axis_index("c")


def _other_chips(x, y):
    return [(1 - x, y), (x, 1 - y), (1 - x, 1 - y)]


def _allgather8(x_shard, name, in_hbm):
    m_per, n = x_shard.shape
    space = pl.ANY if in_hbm else pltpu.VMEM

    def body(x_ref, out_ref, send_sems, recv_sems, local_sem):
        x, y, c = _my_place()
        me, sibling = (x, y, c), (x, y, 1 - c)
        chips = _other_chips(x, y)

        def rows(px, py, pc):
            return out_ref.at[pl.ds((4 * px + 2 * py + pc) * m_per, m_per), :]

        def copy(k, block, to, src=None):
            return pltpu.make_async_remote_copy(
                src_ref=rows(*block) if src is None else src, dst_ref=rows(*block),
                send_sem=send_sems.at[k], recv_sem=recv_sems.at[k], device_id=to, device_id_type=MESH)

        mine = pltpu.make_async_copy(x_ref, rows(*me), local_sem)
        mine.start()
        first = [copy(0, me, sibling, src=x_ref)]
        first += [copy(1 + j, me, (*chip, c), src=x_ref) for j, chip in enumerate(chips)]
        for cp in first:
            cp.start()
        passed = [copy(4 + j, (*chip, c), sibling) for j, chip in enumerate(chips)]
        for j, chip in enumerate(chips):
            copy(1 + j, (*chip, c), me).wait_recv()
            passed[j].start()
        copy(0, sibling, me).wait_recv()
        for j, chip in enumerate(chips):
            copy(4 + j, (*chip, 1 - c), me).wait_recv()
        for cp in first + passed:
            cp.wait_send()
        mine.wait()

    return pl.pallas_call(
        body, name=name,
        out_shape=jax.ShapeDtypeStruct((N_DEV * m_per, n), x_shard.dtype),
        in_specs=[pl.BlockSpec(memory_space=space)],
        out_specs=pl.BlockSpec(memory_space=space),
        scratch_shapes=[pltpu.SemaphoreType.DMA((7,)), pltpu.SemaphoreType.DMA((7,)), pltpu.SemaphoreType.DMA],
        compiler_params=pltpu.CompilerParams(vmem_limit_bytes=VMEM_LIMIT),
    )(x_shard)


def _hbm_specs(n):
    return [pl.BlockSpec(memory_space=pl.ANY)] * n


class _GatherComm:
    def __init__(self, shards):
        self.n = n = len(shards)
        self.inputs = [s.reshape(2, s.shape[0] // 2, s.shape[1]) for s in shards]
        self.out_shape = [jax.ShapeDtypeStruct((N_DEV,) + s.shape[1:], s.dtype) for s in self.inputs]
        self.scratch = [pltpu.SemaphoreType.DMA((7 * n,)), pltpu.SemaphoreType.DMA((7 * n,))]

    def _parts(self, xs, outs, sems):
        send_sems, recv_sems = sems
        x, y, c = _my_place()

        def blk(k, px, py, pc):
            return outs[k].at[4 * px + 2 * py + pc]

        def copy(k, kind, block, to, own=False):
            return pltpu.make_async_remote_copy(
                src_ref=xs[k].at[c] if own else blk(k, *block), dst_ref=blk(k, *block),
                send_sem=send_sems.at[7 * k + kind], recv_sem=recv_sems.at[7 * k + kind],
                device_id=to, device_id_type=MESH)

        def whole(k):
            return pltpu.make_async_remote_copy(
                src_ref=xs[k], dst_ref=outs[k].at[pl.ds(4 * x + 2 * y, 2)],
                send_sem=send_sems.at[7 * k], recv_sem=recv_sems.at[7 * k],
                device_id=(x, y, 1 - c), device_id_type=MESH)

        me, sibling = (x, y, c), (x, y, 1 - c)
        chips = _other_chips(x, y)
        first = []
        for k in range(self.n):
            first.append(whole(k))
            first += [copy(k, 1 + j, me, (*chip, c), own=True) for j, chip in enumerate(chips)]
        return copy, whole, me, sibling, chips, c, first

    def start(self, xs, outs, sems):
        for cp in self._parts(xs, outs, sems)[-1]:
            cp.start()

    def finish(self, xs, outs, sems):
        copy, whole, me, sibling, chips, c, first = self._parts(xs, outs, sems)
        passed = []
        for j, chip in enumerate(chips):
            for k in range(self.n):
                copy(k, 1 + j, (*chip, c), me).wait_recv()
                fwd = copy(k, 4 + j, (*chip, c), sibling)
                fwd.start()
                passed.append(fwd)
        for k in range(self.n):
            whole(k).wait_recv()
        for j, chip in enumerate(chips):
            for k in range(self.n):
                copy(k, 4 + j, (*chip, 1 - c), me).wait_recv()
        for cp in first + passed:
            cp.wait_send()


class _AllGatherComm:
    def __init__(self, x):
        self.inputs = [x]
        self.n = 1
        self.out_shape = [jax.ShapeDtypeStruct((N_DEV,) + x.shape, x.dtype)]
        self.scratch = [pltpu.SemaphoreType.DMA((7,)), pltpu.SemaphoreType.DMA((7,)), pltpu.SemaphoreType.DMA]

    def _parts(self, xs, outs, sems):
        send_sems, recv_sems, local_sem = sems
        x_ref, out_ref = xs[0], outs[0]
        x, y, c = _my_place()

        def copy(k, block, to, own=False):
            blk = out_ref.at[4 * block[0] + 2 * block[1] + block[2]]
            return pltpu.make_async_remote_copy(
                src_ref=x_ref if own else blk, dst_ref=blk, send_sem=send_sems.at[k], recv_sem=recv_sems.at[k],
                device_id=to, device_id_type=MESH)

        me, sibling = (x, y, c), (x, y, 1 - c)
        chips = _other_chips(x, y)
        local = pltpu.make_async_copy(x_ref, out_ref.at[4 * x + 2 * y + c], local_sem)
        first = [copy(0, me, sibling, own=True)]
        first += [copy(1 + j, me, (*chip, c), own=True) for j, chip in enumerate(chips)]
        return copy, me, sibling, chips, c, local, first

    def start(self, xs, outs, sems):
        _, _, _, _, _, local, first = self._parts(xs, outs, sems)
        for cp in [local] + first:
            cp.start()

    def finish(self, xs, outs, sems):
        copy, me, sibling, chips, c, local, first = self._parts(xs, outs, sems)
        passed = []
        for j, chip in enumerate(chips):
            copy(1 + j, (*chip, c), me).wait_recv()
            fwd = copy(4 + j, (*chip, c), sibling)
            fwd.start()
            passed.append(fwd)
        copy(0, sibling, me).wait_recv()
        for j, chip in enumerate(chips):
            copy(4 + j, (*chip, 1 - c), me).wait_recv()
        for cp in first + passed:
            cp.wait_send()
        local.wait()


class _BothComm:
    def __init__(self, a, b):
        self.a, self.b = a, b
        self.inputs = a.inputs + b.inputs
        self.n = a.n + b.n
        self.out_shape = a.out_shape + b.out_shape
        self.scratch = a.scratch + b.scratch

    def _split(self, xs, outs, sems):
        na, ns = self.a.n, len(self.a.scratch)
        return (xs[:na], outs[:na], sems[:ns]), (xs[na:], outs[na:], sems[ns:])

    def start(self, xs, outs, sems):
        pa, pb = self._split(xs, outs, sems)
        self.a.start(*pa)
        self.b.start(*pb)

    def finish(self, xs, outs, sems):
        pa, pb = self._split(xs, outs, sems)
        self.a.finish(*pa)
        self.b.finish(*pb)


class _ToChipsComm:
    def __init__(self, a4s):
        self.inputs = list(a4s)
        self.n = n = len(a4s)
        nc = N_CHIP - 1
        self.out_shape = [jax.ShapeDtypeStruct((nc,) + a.shape[1:], a.dtype) for a in a4s]
        self.scratch = [pltpu.SemaphoreType.DMA((nc * n,)), pltpu.SemaphoreType.DMA((nc * n,))]

    def _copies(self, as_, rs, sems):
        send_sems, recv_sems = sems
        x, y, c = _my_place()
        nc = N_CHIP - 1
        return [pltpu.make_async_remote_copy(
            src_ref=as_[k].at[2 * cx + cy], dst_ref=rs[k].at[j], send_sem=send_sems.at[nc * k + j],
            recv_sem=recv_sems.at[nc * k + j], device_id=(cx, cy, c), device_id_type=MESH)
            for k in range(self.n) for j, (cx, cy) in enumerate(_other_chips(x, y))]

    def start(self, as_, rs, sems):
        for cp in self._copies(as_, rs, sems):
            cp.start()

    def finish(self, as_, rs, sems):
        for cp in self._copies(as_, rs, sems):
            cp.wait()


def _run_comm(comm, name):
    n = comm.n

    def body(*refs):
        ins, outs, sems = refs[:n], refs[n:2 * n], refs[2 * n:]
        comm.start(ins, outs, sems)
        comm.finish(ins, outs, sems)

    return pl.pallas_call(
        body, name=name, out_shape=comm.out_shape, in_specs=_hbm_specs(n), out_specs=_hbm_specs(n),
        scratch_shapes=comm.scratch,
    )(*comm.inputs)


class _ToSiblingComm:
    def __init__(self, g8s):
        self.inputs = list(g8s)
        self.n = n = len(g8s)
        self.out_shape = [jax.ShapeDtypeStruct((N_CHIP,) + g.shape[1:], g.dtype) for g in g8s]
        self.scratch = [pltpu.SemaphoreType.DMA((N_CHIP * n,)), pltpu.SemaphoreType.DMA((N_CHIP * n,))]

    def _copies(self, gs, rs, sems):
        send_sems, recv_sems = sems
        x, y, c = _my_place()
        return [pltpu.make_async_remote_copy(
            src_ref=gs[k].at[2 * s + 1 - c], dst_ref=rs[k].at[s], send_sem=send_sems.at[N_CHIP * k + s],
            recv_sem=recv_sems.at[N_CHIP * k + s], device_id=(x, y, 1 - c), device_id_type=MESH)
            for k in range(self.n) for s in range(N_CHIP)]

    def start(self, gs, rs, sems):
        for cp in self._copies(gs, rs, sems):
            cp.start()

    def finish(self, gs, rs, sems):
        for cp in self._copies(gs, rs, sems):
            cp.wait()


def _swap_halves(hs, name):
    n = len(hs)

    def body(*refs):
        o_refs = refs[n:2 * n]
        send_sems, recv_sems = refs[2 * n:]
        x, y, c = _my_place()

        def remote(k, slot):
            return pltpu.make_async_remote_copy(
                src_ref=o_refs[k].at[slot], dst_ref=o_refs[k].at[slot], send_sem=send_sems.at[k],
                recv_sem=recv_sems.at[k], device_id=(x, y, 1 - c), device_id_type=MESH)

        sends = [remote(k, c) for k in range(n)]
        for cp in sends:
            cp.start()
        for k in range(n):
            remote(k, 1 - c).wait_recv()
        for cp in sends:
            cp.wait_send()

    return pl.pallas_call(
        body, name=name,
        out_shape=[jax.ShapeDtypeStruct(h.shape, h.dtype) for h in hs],
        in_specs=_hbm_specs(n), out_specs=_hbm_specs(n),
        input_output_aliases={k: k for k in range(n)},
        scratch_shapes=[pltpu.SemaphoreType.DMA((n,)), pltpu.SemaphoreType.DMA((n,))],
    )(*hs)


ADD_TILES = 2


def _add_blocks(a_list, a_idx_fn, others_list, ns, sel, name, out_blocks=None, out_idx_fn=None,
                bf16_copy=False):
    out_blocks = out_blocks or ns
    out_idx_fn = out_idx_fn or (lambda s, sel_ref: s)
    n = len(a_list)
    n_o = len(others_list[0])
    per = 1 + n_o

    def body(sel_ref, *refs):
        for k in range(n):
            ins = refs[k * per:(k + 1) * per]
            acc = ins[0][0]
            for r in ins[1:]:
                acc = acc + r[0].astype(F32)
            refs[n * per + k][0] = acc
            if bf16_copy:
                refs[n * per + n + k][0] = acc.astype(BF16)

    in_specs, args, out_specs, out_shape = [], [], [], []
    for a, others in zip(a_list, others_list):
        _, R, N = a.shape
        tr = R // ADD_TILES
        assert tr % 8 == 0, a.shape
        in_specs.append(pl.BlockSpec((1, tr, N), lambda s, i, sel_ref: (a_idx_fn(s, sel_ref), i, 0)))
        args.append(a)
        for arr, fixed in others:
            if fixed is None:
                in_specs.append(pl.BlockSpec((1, tr, N), lambda s, i, sel_ref: (s, i, 0)))
            else:
                in_specs.append(pl.BlockSpec((1, tr, N), lambda s, i, sel_ref, fixed=fixed: (fixed, i, 0)))
            args.append(arr)
        out_specs.append(pl.BlockSpec((1, tr, N), lambda s, i, sel_ref: (out_idx_fn(s, sel_ref), i, 0)))
        out_shape.append(jax.ShapeDtypeStruct((out_blocks, R, N), a.dtype))
    if bf16_copy:
        out_specs = out_specs + out_specs
        out_shape = out_shape + [jax.ShapeDtypeStruct(o.shape, BF16) for o in out_shape]
    grid_spec = pltpu.PrefetchScalarGridSpec(num_scalar_prefetch=1, grid=(ns, ADD_TILES), in_specs=in_specs,
                                             out_specs=out_specs)
    return pl.pallas_call(
        body, name=name, out_shape=out_shape, grid_spec=grid_spec,
        compiler_params=_cparams(("parallel", "parallel")),
    )(sel, *args)


def _rs_first(g8s, tag, r1=None):
    c_sel = jnp.reshape(lax.axis_index("c"), (1,)).astype(jnp.int32)
    if r1 is None:
        r1 = _run_comm(_ToSiblingComm(g8s), f"rs_to_sibling_{tag}")
    res = _add_blocks(g8s, lambda s, sel: 2 * s + sel[0], [[(r, None)] for r in r1], N_CHIP, c_sel,
                      f"rs_add_sibling_{tag}", bf16_copy=True)
    return list(res[:len(g8s)]), list(res[len(g8s):])


def _rs_last(a4s, r2s, tag):
    sel = jnp.stack([2 * lax.axis_index("x") + lax.axis_index("y"), lax.axis_index("c")]).astype(jnp.int32)
    h = _add_blocks(a4s, lambda s, sel: sel[0], [[(r, 0), (r, 1), (r, 2)] for r in r2s], 1, sel,
                    f"rs_add_chips_{tag}", out_blocks=2, out_idx_fn=lambda s, sel: sel[1])
    full = _swap_halves(h, f"rs_swap_halves_{tag}")
    return [f.reshape(2 * f.shape[1], f.shape[2]) for f in full]


def _ada_fwd(c_all, w_ada, b_ada, name):
    nb, D = c_all.shape
    ncol = w_ada.shape[1]
    tc = 512

    def body(c_ref, w_ref, b_ref, o_ref):
        cv = c_ref[...]
        cond = (cv * jax.nn.sigmoid(cv)).astype(BF16)
        o_ref[...] = jnp.dot(cond, w_ref[...].astype(BF16), preferred_element_type=F32) + b_ref[...]

    return pl.pallas_call(
        body, name=name, out_shape=jax.ShapeDtypeStruct((nb, ncol), F32), grid=(ncol // tc,),
        in_specs=[pl.BlockSpec((nb, D), lambda j: (0, 0)), pl.BlockSpec((D, tc), lambda j: (0, j)),
                  pl.BlockSpec((1, tc), lambda j: (0, j))],
        out_specs=pl.BlockSpec((nb, tc), lambda j: (0, j)),
        compiler_params=_cparams(("parallel",)),
    )(c_all, w_ada, b_ada)


def _ada_bwd(c_all, gmod_cols, name):
    nb, D = c_all.shape
    ncol = gmod_cols.shape[1]
    tc = 512

    def body(c_ref, g_ref, o_ref):
        cv = c_ref[...]
        cond = (cv * jax.nn.sigmoid(cv)).astype(BF16)
        o_ref[...] = _dot_tn(cond, g_ref[...].astype(BF16))

    return pl.pallas_call(
        body, name=name, out_shape=jax.ShapeDtypeStruct((D, ncol), F32), grid=(ncol // tc,),
        in_specs=[pl.BlockSpec((nb, D), lambda j: (0, 0)), pl.BlockSpec((nb, tc), lambda j: (0, j))],
        out_specs=pl.BlockSpec((D, tc), lambda j: (0, j)),
        compiler_params=_cparams(("parallel",)),
    )(c_all, gmod_cols)


def _adam_math(w, g, m, v):
    m = ADAM_B1 * m + (1.0 - ADAM_B1) * g
    v = ADAM_B2 * v + (1.0 - ADAM_B2) * (g * g)
    m_hat = m / (1.0 - ADAM_B1 ** ADAM_STEP)
    v_hat = v / (1.0 - ADAM_B2 ** ADAM_STEP)
    delta = -ADAM_LR * (m_hat / (jnp.sqrt(v_hat) + ADAM_EPS) + ADAM_WD * w)
    return delta, m, v


ADAMW_TILES = 8


def _adamw(params, name):
    n = len(params)

    def body(*refs):
        for k in range(n):
            w_ref, g_ref, m_ref, v_ref = refs[4 * k:4 * k + 4]
            d, mn, vn = _adam_math(w_ref[...], g_ref[...], m_ref[...], v_ref[...])
            for o_ref, val in zip(refs[4 * n + 3 * k:4 * n + 3 * k + 3], (d, mn, vn)):
                o_ref[...] = val

    in_specs, out_specs, out_shape, args = [], [], [], []
    for p in params:
        rows, cols = p[0].shape
        assert rows % (8 * ADAMW_TILES) == 0, p[0].shape
        spec = pl.BlockSpec((rows // ADAMW_TILES, cols), lambda i: (i, 0))
        in_specs += [spec] * 4
        out_specs += [spec] * 3
        out_shape += [jax.ShapeDtypeStruct((rows, cols), F32)] * 3
        args += list(p)
    res = pl.pallas_call(
        body, name=name, out_shape=out_shape, grid=(ADAMW_TILES,), in_specs=in_specs, out_specs=out_specs,
        compiler_params=_cparams(("parallel",)),
    )(*args)
    return [tuple(res[3 * k:3 * k + 3]) for k in range(n)]


VEC_ROWS = 8


def _adamw_rows(w, parts, m, v, name):
    n = w.shape[1]
    P = parts.shape[0]
    assert n % (VEC_ROWS * LANES) == 0, n
    shp = (VEC_ROWS, n // VEC_ROWS)

    def body(w_ref, p_ref, m_ref, v_ref, g_ref, d_ref, mo_ref, vo_ref):
        g = p_ref[0]
        for k in range(1, P):
            g = g + p_ref[k]
        d, mn, vn = _adam_math(w_ref[...], g, m_ref[...], v_ref[...])
        g_ref[...] = g
        d_ref[...] = d
        mo_ref[...] = mn
        vo_ref[...] = vn

    vec = pl.BlockSpec(shp, lambda i: (0, 0))
    out = pl.pallas_call(
        body, name=name, out_shape=[jax.ShapeDtypeStruct(shp, F32)] * 4, grid=(1,),
        in_specs=[vec, pl.BlockSpec((P,) + shp, lambda i: (0, 0, 0)), vec, vec], out_specs=[vec] * 4,
        compiler_params=_cparams(("arbitrary",)),
    )(w.reshape(shp), parts.reshape((P,) + shp), m.reshape(shp), v.reshape(shp))
    return [o.reshape(1, n) for o in out]


_SHARDED = ("w_in", "w_uq", "w_ukv", "w_out", "w_ffn_in", "w_ffn_out")
_SMALL = (("g_norm1", 1024), ("g_cq", 384), ("g_ckv", 256), ("rel_bias", 256), ("g_out_a", 512),
          ("g_out_b", 512), ("g_norm2", 1024), ("g_final", 1024))
_SMALL_PAD = 5120


def _full_from_shards(sh):
    return jnp.transpose(sh, (1, 0, 2)).reshape(sh.shape[1], -1)


def _shards_from_full(full):
    rows, cols = full.shape
    return jnp.transpose(full.reshape(rows, N_CHIP, cols // N_CHIP), (1, 0, 2))


def kernel(x, c, w_ada, b_ada, g_norm1, w_in, g_cq, w_uq, g_ckv, w_ukv, rel_bias, g_out_a, g_out_b, w_out, g_norm2, w_ffn_in, w_ffn_out, g_final, loss_target, m_w_ada, m_b_ada, m_g_norm1, m_w_in, m_g_cq, m_w_uq, m_g_ckv, m_w_ukv, m_rel_bias, m_g_out_a, m_g_out_b, m_w_out, m_g_norm2, m_w_ffn_in, m_w_ffn_out, m_g_final, v_w_ada, v_b_ada, v_g_norm1, v_w_in, v_g_cq, v_w_uq, v_g_ckv, v_w_ukv, v_rel_bias, v_g_out_a, v_g_out_b, v_w_out, v_g_norm2, v_w_ffn_in, v_w_ffn_out, v_g_final):
    names = ["w_ada", "b_ada", "g_norm1", "w_in", "g_cq", "w_uq", "g_ckv", "w_ukv", "rel_bias", "g_out_a",
             "g_out_b", "w_out", "g_norm2", "w_ffn_in", "w_ffn_out", "g_final"]
    W = dict(zip(names, [w_ada, b_ada, g_norm1, w_in, g_cq, w_uq, g_ckv, w_ukv, rel_bias, g_out_a, g_out_b,
                         w_out, g_norm2, w_ffn_in, w_ffn_out, g_final]))
    M = dict(zip(names, [m_w_ada, m_b_ada, m_g_norm1, m_w_in, m_g_cq, m_w_uq, m_g_ckv, m_w_ukv, m_rel_bias,
                         m_g_out_a, m_g_out_b, m_w_out, m_g_norm2, m_w_ffn_in, m_w_ffn_out, m_g_final]))
    V = dict(zip(names, [v_w_ada, v_b_ada, v_g_norm1, v_w_in, v_g_cq, v_w_uq, v_g_ckv, v_w_ukv, v_rel_bias,
                         v_g_out_a, v_g_out_b, v_w_out, v_g_norm2, v_w_ffn_in, v_w_ffn_out, v_g_final]))
    B, S, D = x.shape
    mx, my, mc = _my_place()
    dev = 4 * mx + 2 * my + mc
    chip = 2 * mx + my
    pad_rows = 8

    early = ("w_in", "w_uq", "w_ukv")
    bias, got = _bias_tables(rel_bias, _bucket_tables(), "rel_bias_tables",
                             _BothComm(_AllGatherComm(jnp.pad(c, ((0, pad_rows - B), (0, 0)))),
                                       _GatherComm([W[n][0].astype(BF16) for n in early])))
    c_all = got[0][:, :B].reshape(N_DEV * B, D)

    ada_cols = w_ada.shape[-1]
    b_cols = lax.dynamic_slice_in_dim(b_ada, chip * ada_cols, ada_cols, axis=1)
    mod_cols = _ada_fwd(c_all, w_ada[0], b_cols, "ada_fwd")
    mod_all = _allgather8(mod_cols, "ag_mod", False).reshape(N_DEV, N_DEV * B, ada_cols)[0::2]
    mod_all = jnp.transpose(mod_all, (1, 0, 2)).reshape(N_DEV * B, N_MOD * D)
    mod = lax.dynamic_slice_in_dim(mod_all, dev * B, B, axis=0)

    full = {n: g.reshape((N_CHIP,) + W[n].shape[1:]) for n, g in zip(early, got[1:])}
    wts = dict(w_in=_w_in_to_kernel(_full_from_shards(full["w_in"])),
               w_uq=_w_uq_to_kernel(_full_from_shards(full["w_uq"])),
               w_kv=_w_ukv_to_kernel(_full_from_shards(full["w_ukv"])))
    gains = dict(g_norm1=g_norm1, g_cq=g_cq, g_ckv=g_ckv, g_out_a=g_out_a, g_out_b=g_out_b, g_norm2=g_norm2,
                 g_final=g_final.reshape(1, D))

    loss, grad_x, gmod, grads = _local_step(x, loss_target, mod, wts, gains, rel_bias,
                                            ffn_shards=[w_ffn_in[0].astype(BF16), w_ffn_out[0].astype(BF16),
                                                        w_out[0].astype(BF16)], bias=bias)
    loss = lax.psum(loss[0, 0], ("x", "y", "c"))

    n_small = _SMALL_PAD
    cat = lambda dct: jnp.concatenate([dct[n].reshape(1, -1) for n, _ in _SMALL]
                                      + [jnp.zeros((1, _SMALL_PAD - sum(s for _, s in _SMALL)), F32)], axis=1)
    small = cat(grads)
    rows = jnp.concatenate([gmod, jnp.pad(small, ((0, 0), (0, N_MOD * D - n_small))),
                            jnp.zeros((pad_rows - B - 1, N_MOD * D), F32)], axis=0)
    rows_all = _allgather8(rows, "ag_small", False).reshape(N_DEV, pad_rows, N_MOD * D)
    gmod_all = rows_all[:, :B].reshape(N_DEV * B, N_MOD * D)
    small_parts = rows_all[:, B, :n_small]

    a4, r2 = grads["mix_pending"]
    ffn_a4, ffn_r2 = grads["ffn_pending"]
    G = dict(zip(_SHARDED, _rs_last(list(a4) + list(ffn_a4), list(r2) + list(ffn_r2), "all")))

    gmod_cols = lax.dynamic_slice_in_dim(gmod_all, chip * ada_cols, ada_cols, axis=1)
    G["w_ada"] = _ada_bwd(c_all, gmod_cols, "ada_bwd")
    delta, new_m, new_v = {}, {}, {}
    big = ("w_ada",) + _SHARDED
    two_d = lambda a: a.reshape(a.shape[-2], a.shape[-1])
    updates = _adamw([(two_d(W[n]), G[n], two_d(M[n]), two_d(V[n])) for n in big], "adamw_sharded")
    for n, (d_, m_, v_) in zip(big, updates):
        G[n], delta[n], new_m[n], new_v[n] = [a.reshape(W[n].shape) for a in (G[n], d_, m_, v_)]
    gs, ds_, ms_, vs_ = _adamw_rows(cat(W), small_parts, cat(M), cat(V), "adamw_small")
    off = 0
    for n, sz in _SMALL:
        shp = W[n].shape
        G[n], delta[n], new_m[n], new_v[n] = [a[:, off:off + sz].reshape(shp) for a in (gs, ds_, ms_, vs_)]
        off += sz
    G["b_ada"], delta["b_ada"], new_m["b_ada"], new_v["b_ada"] = _adamw_rows(b_ada, gmod_all, m_b_ada, v_b_ada,
                                                                          "adamw_b_ada")
    return (loss, grad_x, *[G[n] for n in names], *[delta[n] for n in names], *[new_m[n] for n in names],
            *[new_v[n] for n in names])
```

```python
import functools
import math

import numpy as np
import jax
import jax.numpy as jnp
from jax import lax
from jax.experimental import pallas as pl
from jax.experimental.pallas import tpu as pltpu

F32 = jnp.float32
BF16 = jnp.bfloat16

D_MODEL = 1024
SEQ = 2048
N_HEADS = 8
HEAD_DIM = 64
D_A = 512
D_B = 512
Q_LORA = 384
KV_LORA = 256
ROPE_DIM = 32
NOPE_DIM = 64
D_FF = 2816
N_MOD = 6
N_BUCKETS = 32
MAX_DISTANCE = 2048
ROPE_THETA = 10000.0
EPS = 1e-6
NEG = -1e30
BLK = 128
DILATIONS = (1, 4, 16)
SPAN = 128
MLA_SCALE = (NOPE_DIM + ROPE_DIM) ** -0.5
DIL_SCALE = HEAD_DIM ** -0.5

ADAM_LR = 0.001
ADAM_B1 = 0.9
ADAM_B2 = 0.999
ADAM_EPS = 1e-08
ADAM_WD = 0.01
ADAM_STEP = 10

N_DEV = 8
N_CHIP = 4
LANES = 128
VMEM_LIMIT = 48 * 1024 * 1024
MM_VMEM_BUDGET = 32 * 1024 * 1024

P_QKV = 3 * D_A
P_REST = KV_LORA + LANES + Q_LORA


def _cparams(sem=None):
    return pltpu.CompilerParams(dimension_semantics=sem, vmem_limit_bytes=VMEM_LIMIT)


def _pick(n, cands):
    for c in cands:
        if n % c == 0:
            return c
    raise ValueError(f"no tile for {n} in {cands}")


def _mm(a, b, mode, out_dtype, name, col_blocks=None, comm=None, halves=False):
    blocked = col_blocks is not None
    if mode == "nn":
        (M, K) = a.shape
        K2, N = (b.shape[1], b.shape[0] * b.shape[2]) if blocked else b.shape
    elif mode == "nt":
        (M, K) = (a.shape[1], 2 * a.shape[2]) if halves else a.shape
        N, K2 = (b.shape[1], b.shape[0] * b.shape[2]) if blocked else b.shape
    else:
        (K, M) = a.shape
        K2, N = (b.shape[1], 2 * b.shape[2]) if halves else b.shape
    assert K == K2, (a.shape, b.shape, mode)
    assert not halves or (blocked and col_blocks == 4 and mode in ("nt", "tn"))
    tn = _pick(N, (1408, 1024, 768, 512, 384, 256, 128))
    tk = _pick(K, (1408, 1152, 1024, 768, 512, 384, 256, 128))
    if blocked and mode == "nt":
        tk = K // col_blocks
    elif blocked:
        tn = N // col_blocks
    nk = K // tk

    def vmem_bytes(tm_):
        tiles = tm_ * tk * a.dtype.itemsize + tk * tn * b.dtype.itemsize + tm_ * tn * jnp.dtype(out_dtype).itemsize
        return 2 * tiles + tm_ * tn * 4

    tm = next(t for t in (1408, 1024, 512, 384, 256, 128) if M % t == 0 and vmem_bytes(t) <= MM_VMEM_BUDGET)
    out_shape = (M, N)
    out_spec = pl.BlockSpec((tm, tn), lambda i, j, k: (i, j))
    if mode == "nn":
        a_spec = pl.BlockSpec((tm, tk), lambda i, j, k: (i, k))
        b_spec = (pl.BlockSpec((None, tk, tn), lambda i, j, k: (j, k, 0)) if blocked
                  else pl.BlockSpec((tk, tn), lambda i, j, k: (k, j)))
        dn = (((1,), (0,)), ((), ()))
    elif mode == "nt":
        a_spec = (pl.BlockSpec((None, tm, tk), lambda i, j, k: (k // 2, i, k % 2)) if halves
                  else pl.BlockSpec((tm, tk), lambda i, j, k: (i, k)))
        b_spec = (pl.BlockSpec((None, tn, tk), lambda i, j, k: (k, j, 0)) if blocked
                  else pl.BlockSpec((tn, tk), lambda i, j, k: (j, k)))
        dn = (((1,), (1,)), ((), ()))
    else:
        a_spec = pl.BlockSpec((tk, tm), lambda i, j, k: (k, i))
        b_spec = (pl.BlockSpec((None, tk, tn), lambda i, j, k: (j // 2, k, j % 2)) if halves
                  else pl.BlockSpec((tk, tn), lambda i, j, k: (k, j)))
        dn = (((0,), (0,)), ((), ()))
        if blocked:
            out_shape = (col_blocks, M, tn)
            out_spec = pl.BlockSpec((None, tm, tn), lambda i, j, k: (j, i, 0))

    def body(a_ref, b_ref, o_ref, acc_ref):
        k = pl.program_id(2)

        @pl.when(k == 0)
        def _():
            acc_ref[...] = jnp.zeros_like(acc_ref)

        acc_ref[...] += lax.dot_general(a_ref[...].astype(BF16), b_ref[...].astype(BF16), dn,
                                        preferred_element_type=F32)

        @pl.when(k == nk - 1)
        def _():
            o_ref[...] = acc_ref[...].astype(o_ref.dtype)

    if comm is not None:
        (out,), got = _host_call(
            body, comm, name=name, out_shape=[jax.ShapeDtypeStruct(out_shape, out_dtype)],
            grid=(M // tm, N // tn, nk), in_specs=[a_spec, b_spec], out_specs=[out_spec],
            scratch_shapes=[pltpu.VMEM((tm, tn), F32)], args=(a, b))
        return out, got
    return pl.pallas_call(
        body, name=name,
        out_shape=jax.ShapeDtypeStruct(out_shape, out_dtype),
        grid=(M // tm, N // tn, nk),
        in_specs=[a_spec, b_spec],
        out_specs=out_spec,
        scratch_shapes=[pltpu.VMEM((tm, tn), F32)],
        compiler_params=_cparams(("parallel", "parallel", "arbitrary")),
    )(a, b)


ROW_TILE = 512


def _adaln_fwd(x, g, sc, sh, name):
    B, S, D = x.shape
    ts = ROW_TILE

    def body(x_ref, g_ref, sc_ref, sh_ref, h_ref):
        xv = x_ref[0]
        r = lax.rsqrt(jnp.mean(xv * xv, axis=-1, keepdims=True) + EPS)
        xn = (xv * r) * g_ref[...]
        h_ref[0] = (xn * (1.0 + sc_ref[0]) + sh_ref[0]).astype(h_ref.dtype)

    tok = pl.BlockSpec((1, ts, D), lambda b, s: (b, s, 0))
    per_b = pl.BlockSpec((1, 1, D), lambda b, s: (b, 0, 0))
    return pl.pallas_call(
        body, name=name, out_shape=jax.ShapeDtypeStruct((B, S, D), BF16), grid=(B, S // ts),
        in_specs=[tok, pl.BlockSpec((1, D), lambda b, s: (0, 0)), per_b, per_b], out_specs=tok,
        compiler_params=_cparams(("parallel", "parallel")),
    )(x, g, sc, sh)


def _adaln_bwd(dh, x, g, sc, dres, name, mix=None, gate=None, comm=None):
    B, S, D = x.shape
    ts = ROW_TILE
    has_res = mix is not None

    def body(*refs):
        if has_res:
            (dh_ref, x_ref, g_ref, sc_ref, dres_ref, mix_ref, gate_ref,
             dx_ref, dsh_ref, dsc_ref, dg_ref, dgate_ref, dmix_ref) = refs
        else:
            (dh_ref, x_ref, g_ref, sc_ref, dres_ref, dx_ref, dsh_ref, dsc_ref, dg_ref) = refs
        b, s = pl.program_id(0), pl.program_id(1)
        xv = x_ref[0]
        dhv = dh_ref[0]
        gv = g_ref[...]
        r = lax.rsqrt(jnp.mean(xv * xv, axis=-1, keepdims=True) + EPS)
        n = xv * r
        xn = n * gv
        dxn = dhv * (1.0 + sc_ref[0])
        dn = dxn * gv
        dx = r * (dn - n * jnp.mean(dn * n, axis=-1, keepdims=True)) + dres_ref[0]
        dx_ref[0] = dx

        @pl.when(s == 0)
        def _():
            dsh_ref[...] = jnp.zeros_like(dsh_ref)
            dsc_ref[...] = jnp.zeros_like(dsc_ref)
            if has_res:
                dgate_ref[...] = jnp.zeros_like(dgate_ref)

        @pl.when((s == 0) & (b == 0))
        def _():
            dg_ref[...] = jnp.zeros_like(dg_ref)

        dsh_ref[0] += jnp.sum(dhv, axis=0, keepdims=True)
        dsc_ref[0] += jnp.sum(dhv * xn, axis=0, keepdims=True)
        dg_ref[...] += jnp.sum(dxn * n, axis=0, keepdims=True)
        if has_res:
            dgate_ref[0] += jnp.sum(dx * mix_ref[0], axis=0, keepdims=True)
            dmix_ref[0] = (dx * gate_ref[0]).astype(dmix_ref.dtype)

    tok = pl.BlockSpec((1, ts, D), lambda b, s: (b, s, 0))
    per_b = pl.BlockSpec((1, 1, D), lambda b, s: (b, 0, 0))
    vec = pl.BlockSpec((1, D), lambda b, s: (0, 0))
    in_specs = [tok, tok, vec, per_b, tok]
    args = [dh, x, g, sc, dres]
    out_shape = [jax.ShapeDtypeStruct((B, S, D), F32), jax.ShapeDtypeStruct((B, 1, D), F32),
                 jax.ShapeDtypeStruct((B, 1, D), F32), jax.ShapeDtypeStruct((1, D), F32)]
    out_specs = [tok, per_b, per_b, vec]
    if has_res:
        in_specs += [tok, per_b]
        args += [mix, gate]
        out_shape += [jax.ShapeDtypeStruct((B, 1, D), F32), jax.ShapeDtypeStruct((B, S, D), BF16)]
        out_specs += [per_b, tok]
    res, got = _host_call(body, comm, name=name, out_shape=out_shape, grid=(B, S // ts), in_specs=in_specs,
                          out_specs=out_specs, scratch_shapes=[], args=args)
    return (list(res) + [got]) if comm is not None else res


def _rms_parts(xv):
    r = lax.rsqrt(jnp.mean(xv * xv, axis=-1, keepdims=True) + EPS)
    return r, xv * r


def _mixer_out(out_a, out_b, g_a, g_b, w_out, x, gate, g_norm, sc, sh, name):
    B, S, D = x.shape
    na, nb = out_a.shape[-1], out_b.shape[-1]
    ts = ROW_TILE
    tiles = S // ts

    def body(a_ref, b_ref, ga_ref, gb_ref, w_ref, x_ref, gate_ref, gn_ref, sc_ref, sh_ref,
             y_ref, mix_ref, x1_ref, h_ref):
        y = jnp.concatenate([(_rms_parts(a_ref[0])[1] * ga_ref[...]).astype(BF16),
                             (_rms_parts(b_ref[0])[1] * gb_ref[...]).astype(BF16)], axis=1)
        y_ref[...] = y
        mix = jnp.dot(y, w_ref[...], preferred_element_type=F32)
        mix_ref[0] = mix
        x1 = x_ref[0] + gate_ref[0] * mix
        x1_ref[0] = x1
        h_ref[0] = ((_rms_parts(x1)[1] * gn_ref[...]) * (1.0 + sc_ref[0]) + sh_ref[0]).astype(h_ref.dtype)

    tok = lambda n: pl.BlockSpec((1, ts, n), lambda b, t: (b, t, 0))
    per_b = pl.BlockSpec((1, 1, D), lambda b, t: (b, 0, 0))
    full = lambda a: pl.BlockSpec(a.shape, lambda b, t: (0, 0))
    return pl.pallas_call(
        body, name=name,
        out_shape=[jax.ShapeDtypeStruct((B * S, na + nb), BF16), jax.ShapeDtypeStruct((B, S, D), F32),
                   jax.ShapeDtypeStruct((B, S, D), F32), jax.ShapeDtypeStruct((B, S, D), BF16)],
        grid=(B, tiles),
        in_specs=[tok(na), tok(nb), full(g_a), full(g_b), full(w_out), tok(D), per_b, full(g_norm), per_b, per_b],
        out_specs=[pl.BlockSpec((ts, na + nb), lambda b, t: (b * tiles + t, 0)), tok(D), tok(D), tok(D)],
        compiler_params=_cparams(("parallel", "parallel")),
    )(out_a, out_b, g_a, g_b, w_out, x, gate, g_norm, sc, sh)


def _mixer_out_bwd(dmix, w_out, out_a, out_b, g_a, g_b, name):
    B, S, D = dmix.shape
    na, nb = out_a.shape[-1], out_b.shape[-1]
    tq = MLA_TQ
    sub = VIEW_TILE // tq
    npair = N_HEADS // 2

    def rms_bwd(dyv, xv, g_ref, dg_ref):
        r, nrm = _rms_parts(xv)
        dn = dyv * g_ref[...]
        dg_ref[...] += jnp.sum(dyv * nrm, axis=0, keepdims=True)
        return r * (dn - nrm * jnp.mean(dn * nrm, axis=-1, keepdims=True))

    def body(dm_ref, w_ref, a_ref, b_ref, ga_ref, gb_ref, d1_ref, d4_ref, d16_ref, dga_ref, dob_ref, dgb_ref,
             dl_ref, dx_s):
        @pl.when((pl.program_id(0) == 0) & (pl.program_id(1) == 0))
        def _():
            dga_ref[...] = jnp.zeros_like(dga_ref)
            dgb_ref[...] = jnp.zeros_like(dgb_ref)

        dy = _dot_nt(dm_ref[0], w_ref[...])
        do_a = rms_bwd(dy[:, :na], a_ref[0], ga_ref, dga_ref)
        d1_ref[0] = do_a.astype(d1_ref.dtype)
        _put_tile(dx_s, do_a)
        _tile_to_view(dx_s, d4_ref, DILATIONS[1], na)
        _tile_to_view(dx_s, d16_ref, DILATIONS[2], na)
        ov = b_ref[0]
        do_b = rms_bwd(dy[:, na:], ov, gb_ref, dgb_ref).astype(dob_ref.dtype)
        dob_ref[0] = do_b
        prod = do_b.astype(F32) * ov
        dl_ref[...] = jnp.zeros_like(dl_ref)
        for p in range(npair):
            for s in range(sub):
                prod_t = jnp.transpose(prod[s * tq:(s + 1) * tq, p * LANES:(p + 1) * LANES])
                dl_ref[0, p, s, 0:1, :] = jnp.sum(prod_t[:HEAD_DIM], axis=0, keepdims=True)
                dl_ref[0, p, s, 1:2, :] = jnp.sum(prod_t[HEAD_DIM:], axis=0, keepdims=True)

    full = lambda a: pl.BlockSpec(a.shape, lambda b, t: (0, 0))
    vec = lambda n: pl.BlockSpec((1, n), lambda b, t: (0, 0))
    res = pl.pallas_call(
        body, name=name,
        out_shape=[_view_shape(B, S, d, na, BF16) for d in DILATIONS]
        + [jax.ShapeDtypeStruct((1, na), F32), jax.ShapeDtypeStruct((B, S, nb), BF16),
           jax.ShapeDtypeStruct((1, nb), F32), jax.ShapeDtypeStruct((B, npair, S // tq, 8, tq), F32)],
        grid=(B, S // VIEW_TILE),
        in_specs=[_view_spec(1, D), full(w_out), _view_spec(1, na), _view_spec(1, nb), full(g_a), full(g_b)],
        out_specs=[_view_spec(d, na) for d in DILATIONS]
        + [vec(na), _view_spec(1, nb), vec(nb),
           pl.BlockSpec((1, npair, sub, 8, tq), lambda b, t: (b, 0, t, 0, 0))],
        scratch_shapes=[_tile_scratch(na)],
        compiler_params=_cparams(("arbitrary", "arbitrary")),
    )(dmix, w_out, out_a, out_b, g_a, g_b)
    nd = len(DILATIONS)
    return res[:nd], res[nd], res[nd + 1], res[nd + 2], res[nd + 3]


FFN_TILE = 1408


def _ffn_in_fwd(h, w4, name):
    T, D = h.shape
    tm, tc = 512, FFN_TILE
    nc = D_FF // tc

    def body(h_ref, wg_ref, wu_ref, gu_ref, act_ref):
        hv = h_ref[...]
        g = jnp.dot(hv, wg_ref[...], preferred_element_type=F32)
        u = jnp.dot(hv, wu_ref[...], preferred_element_type=F32)
        gu_ref[0] = g.astype(gu_ref.dtype)
        gu_ref[1] = u.astype(gu_ref.dtype)
        act_ref[...] = (g * jax.nn.sigmoid(g) * u).astype(act_ref.dtype)

    return pl.pallas_call(
        body, name=name,
        out_shape=[jax.ShapeDtypeStruct((2, T, D_FF), BF16), jax.ShapeDtypeStruct((T, D_FF), BF16)],
        grid=(nc, T // tm),
        in_specs=[pl.BlockSpec((tm, D), lambda j, i: (i, 0)),
                  pl.BlockSpec((None, D, tc), lambda j, i: (j, 0, 0)),
                  pl.BlockSpec((None, D, tc), lambda j, i: (j + nc, 0, 0))],
        out_specs=[pl.BlockSpec((2, tm, tc), lambda j, i: (0, i, j)), pl.BlockSpec((tm, tc), lambda j, i: (i, j))],
        compiler_params=_cparams(("parallel", "parallel")),
    )(h, w4, w4)


def _ffn_out_bwd(df, w_out, gu, name):
    T, D = df.shape
    tm, tc = 512, FFN_TILE

    def body(df_ref, w_ref, gu_ref, dgu_ref):
        da = _dot_nt(df_ref[...], w_ref[...])
        g, u = gu_ref[0].astype(F32), gu_ref[1].astype(F32)
        sg = jax.nn.sigmoid(g)
        dgu_ref[0] = (da * u * (sg * (1.0 + g * (1.0 - sg)))).astype(dgu_ref.dtype)
        dgu_ref[1] = (da * (g * sg)).astype(dgu_ref.dtype)

    halves = pl.BlockSpec((2, tm, tc), lambda j, i: (0, i, j))
    return pl.pallas_call(
        body, name=name, out_shape=jax.ShapeDtypeStruct((2, T, D_FF), BF16), grid=(D_FF // tc, T // tm),
        in_specs=[pl.BlockSpec((tm, D), lambda j, i: (i, 0)), pl.BlockSpec((tc, D), lambda j, i: (j, 0)), halves],
        out_specs=halves,
        compiler_params=_cparams(("parallel", "parallel")),
    )(df, w_out, gu)


def _final_loss(x1, act, w_out, g2, gf, target, name):
    B, S, D = x1.shape
    ts = ROW_TILE
    tiles = S // ts

    def body(x1_ref, act_ref, w_ref, g2_ref, gf_ref, t_ref, dx_ref, df_ref, dg2_ref, dgf_ref, loss_ref):
        b, s = pl.program_id(0), pl.program_id(1)
        fv = jnp.dot(act_ref[...], w_ref[...], preferred_element_type=F32)
        g2v = g2_ref[0]
        gfv = gf_ref[...]
        x2 = x1_ref[0] + g2v * fv
        r = lax.rsqrt(jnp.mean(x2 * x2, axis=-1, keepdims=True) + EPS)
        n = x2 * r
        e = n * gfv - t_ref[0]
        dy = e * (1.0 / D)
        dn = dy * gfv
        dx = r * (dn - n * jnp.mean(dn * n, axis=-1, keepdims=True))
        dx_ref[0] = dx
        df_ref[0] = (dx * g2v).astype(df_ref.dtype)

        @pl.when(s == 0)
        def _():
            dg2_ref[...] = jnp.zeros_like(dg2_ref)

        @pl.when((s == 0) & (b == 0))
        def _():
            dgf_ref[...] = jnp.zeros_like(dgf_ref)
            loss_ref[...] = jnp.zeros_like(loss_ref)

        dg2_ref[0] += jnp.sum(dx * fv, axis=0, keepdims=True)
        dgf_ref[...] += jnp.sum(dy * n, axis=0, keepdims=True)
        loss_ref[...] += 0.5 * jnp.sum(jnp.mean(e * e, axis=-1, keepdims=True), axis=0, keepdims=True)

    tok = pl.BlockSpec((1, ts, D), lambda b, s: (b, s, 0))
    per_b = pl.BlockSpec((1, 1, D), lambda b, s: (b, 0, 0))
    vec = pl.BlockSpec((1, D), lambda b, s: (0, 0))
    return pl.pallas_call(
        body, name=name,
        out_shape=[jax.ShapeDtypeStruct((B, S, D), F32), jax.ShapeDtypeStruct((B, S, D), BF16),
                   jax.ShapeDtypeStruct((B, 1, D), F32), jax.ShapeDtypeStruct((1, D), F32),
                   jax.ShapeDtypeStruct((1, LANES), F32)],
        grid=(B, tiles),
        in_specs=[tok, pl.BlockSpec((ts, act.shape[1]), lambda b, s: (b * tiles + s, 0)),
                  pl.BlockSpec(w_out.shape, lambda b, s: (0, 0)), per_b, vec, tok],
        out_specs=[tok, tok, per_b, vec, pl.BlockSpec((1, LANES), lambda b, s: (0, 0))],
        compiler_params=_cparams(("arbitrary", "arbitrary")),
    )(x1, act, w_out, g2, gf, target)


def _rope_tables():
    half = ROPE_DIM // 2
    inv = ROPE_THETA ** (-jnp.arange(half, dtype=F32) / half)
    ang = jnp.arange(SEQ, dtype=F32)[:, None] * inv[None, :]
    cos, sin = jnp.cos(ang), jnp.sin(ang)
    one = jnp.ones((SEQ, NOPE_DIM), F32)
    zero = jnp.zeros((SEQ, NOPE_DIM), F32)
    cs = jnp.concatenate([one, cos, cos, one[:, :LANES - NOPE_DIM - ROPE_DIM]], axis=1)
    sn = jnp.concatenate([zero, -sin, sin, zero[:, :LANES - NOPE_DIM - ROPE_DIM]], axis=1)
    return cs, sn


def _rope_group(t, cs, sn):
    half = ROPE_DIM // 2
    lane = lax.broadcasted_iota(jnp.int32, t.shape, 1)
    partner = jnp.where(lane < NOPE_DIM + half, pltpu.roll(t, LANES - half, 1), pltpu.roll(t, half, 1))
    return t * cs + partner * sn


def _mla_proj(cqn, ckvn, rest, w_uq, w_kv, cs, sn, name):
    B, S, _ = rest.shape
    ts, tk = ROW_TILE, MLA_TK
    tiles = S // ts
    G = N_HEADS
    kw = G * LANES
    npair = N_HEADS // 2

    def body(cq_ref, ckv_ref, r_ref, wq_ref, wkv_ref, cs_ref, sn_ref, q_ref, k_ref, v_ref, vt_ref):
        csv, snv = cs_ref[...], sn_ref[...]
        q_raw = jnp.dot(cq_ref[...], wq_ref[...], preferred_element_type=F32)
        kv = jnp.dot(ckv_ref[...], wkv_ref[...], preferred_element_type=F32)
        ra = _rope_group(r_ref[0], csv, snv)
        for gi in range(G):
            sl = slice(gi * LANES, (gi + 1) * LANES)
            q_ref[0, :, sl] = _rope_group(q_raw[:, sl], csv, snv).astype(q_ref.dtype)
            k_ref[0, :, sl] = (kv[:, sl] + ra).astype(k_ref.dtype)
        v = kv[:, kw:]
        v_ref[0] = v.astype(v_ref.dtype)
        for p in range(npair):
            for s in range(ts // tk):
                vt_ref[0, p, s] = jnp.transpose(v[s * tk:(s + 1) * tk, p * LANES:(p + 1) * LANES]).astype(vt_ref.dtype)

    rows = lambda n: pl.BlockSpec((ts, n), lambda b, t: (b * tiles + t, 0))
    full = lambda a: pl.BlockSpec(a.shape, lambda b, t: (0, 0))
    tab = pl.BlockSpec((ts, LANES), lambda b, t: (t, 0))
    tok = lambda n: pl.BlockSpec((1, ts, n), lambda b, t: (b, t, 0))
    return pl.pallas_call(
        body, name=name,
        out_shape=[jax.ShapeDtypeStruct((B, S, kw), BF16), jax.ShapeDtypeStruct((B, S, kw), BF16),
                   jax.ShapeDtypeStruct((B, S, D_B), BF16),
                   jax.ShapeDtypeStruct((B, npair, S // tk, LANES, tk), BF16)],
        grid=(B, tiles),
        in_specs=[rows(Q_LORA), rows(KV_LORA), pl.BlockSpec((1, ts, LANES), lambda b, t: (b, t, KV_LORA // LANES)),
                  full(w_uq), full(w_kv), tab, tab],
        out_specs=[tok(kw), tok(kw), tok(D_B),
                   pl.BlockSpec((1, npair, ts // tk, LANES, tk), lambda b, t: (b, 0, t, 0, 0))],
        compiler_params=_cparams(("parallel", "parallel")),
    )(cqn, ckvn, rest, w_uq, w_kv, cs, sn)


def _mla_proj_bwd(dq_t, dkc, dv, cqn, ckvn, rest, w_uq, w_kv, g_cq, g_ckv, cs, sn_neg, name):
    B, S, _ = rest.shape
    npair, tq = dq_t.shape[1], dq_t.shape[-1]
    ts = ROW_TILE
    tiles = S // ts
    kw = N_HEADS * LANES
    cq_lo = KV_LORA + LANES

    def rms_bwd(dy, xv, g_ref):
        r = lax.rsqrt(jnp.mean(xv * xv, axis=-1, keepdims=True) + EPS)
        nrm = xv * r
        dn = dy * g_ref[...]
        return r * (dn - nrm * jnp.mean(dn * nrm, axis=-1, keepdims=True)), jnp.sum(dy * nrm, axis=0, keepdims=True)

    def body(dqt_ref, dk_ref, dv_ref, cq_ref, ckv_ref, r_ref, wq_ref, wkv_ref, gcq_ref, gckv_ref, cs_ref, sn_ref,
             dr_ref, gwq_ref, gwkv_ref, dgcq_ref, dgckv_ref, dq_s):
        @pl.when((pl.program_id(0) == 0) & (pl.program_id(1) == 0))
        def _():
            for ref in (gwq_ref, gwkv_ref, dgcq_ref, dgckv_ref):
                ref[...] = jnp.zeros_like(ref)

        csv, snv = cs_ref[...], sn_ref[...]
        for p in range(npair):
            for s in range(ts // tq):
                tile = jnp.transpose(dqt_ref[0, p, s])
                rows = slice(s * tq, (s + 1) * tq)
                for hh in range(2):
                    lo = (2 * p + hh) * LANES
                    dq_s[rows, lo:lo + LANES] = _rope_group(tile[:, hh * LANES:(hh + 1) * LANES], csv[rows],
                                                            snv[rows]).astype(dq_s.dtype)
        dq_raw = dq_s[...]
        dkcv = dk_ref[0]
        dkv = jnp.concatenate([dkcv, dv_ref[0]], axis=1).astype(BF16)
        cqn, ckvn = cq_ref[...], ckv_ref[...]
        gwq_ref[...] += _dot_tn(cqn, dq_raw)
        gwkv_ref[...] += _dot_tn(ckvn, dkv)
        restv = r_ref[0]
        dcq, dg = rms_bwd(_dot_nt(dq_raw, wq_ref[...]), restv[:, cq_lo:], gcq_ref)
        dgcq_ref[...] += dg
        dckv, dg = rms_bwd(_dot_nt(dkv, wkv_ref[...]), restv[:, :KV_LORA], gckv_ref)
        dgckv_ref[...] += dg
        acc = dkcv[:, 0:LANES]
        for gi in range(1, N_HEADS):
            acc = acc + dkcv[:, gi * LANES:(gi + 1) * LANES]
        lane = lax.broadcasted_iota(jnp.int32, acc.shape, 1)
        acc = jnp.where((lane >= NOPE_DIM) & (lane < NOPE_DIM + ROPE_DIM), acc, 0.0)
        dr_ref[0, :, 0:KV_LORA] = dckv.astype(dr_ref.dtype)
        dr_ref[0, :, KV_LORA:cq_lo] = _rope_group(acc, csv, snv).astype(dr_ref.dtype)
        dr_ref[0, :, cq_lo:] = dcq.astype(dr_ref.dtype)

    rows = lambda n: pl.BlockSpec((ts, n), lambda b, t: (b * tiles + t, 0))
    full = lambda a: pl.BlockSpec(a.shape, lambda b, t: (0, 0))
    tab = pl.BlockSpec((ts, LANES), lambda b, t: (t, 0))
    tok = lambda n: pl.BlockSpec((1, ts, n), lambda b, t: (b, t, 0))
    acc_out = lambda shp: pl.BlockSpec(shp, lambda b, t: (0, 0))
    out_shape = [jax.ShapeDtypeStruct((B, S, P_REST), BF16), jax.ShapeDtypeStruct(w_uq.shape, F32),
                 jax.ShapeDtypeStruct(w_kv.shape, F32), jax.ShapeDtypeStruct((1, Q_LORA), F32),
                 jax.ShapeDtypeStruct((1, KV_LORA), F32)]
    return pl.pallas_call(
        body, name=name, out_shape=out_shape, grid=(B, tiles),
        in_specs=[pl.BlockSpec((1, npair, ts // tq, 2 * LANES, tq), lambda b, t: (b, 0, t, 0, 0)), tok(kw),
                  tok(D_B), rows(Q_LORA), rows(KV_LORA), tok(P_REST), full(w_uq), full(w_kv), full(g_cq),
                  full(g_ckv), tab, tab],
        out_specs=[tok(P_REST)] + [acc_out(o.shape) for o in out_shape[1:]],
        scratch_shapes=[pltpu.VMEM((ts, kw), BF16)],
        compiler_params=_cparams(("arbitrary", "arbitrary")),
    )(dq_t, dkc, dv, cqn, ckvn, rest, w_uq, w_kv, g_cq, g_ckv, cs, sn_neg)


def _t5_bucket(dist):
    max_exact = N_BUCKETS // 2
    d = np.maximum(dist, 1).astype(np.float64)
    large = max_exact + (np.log(d / max_exact) / np.log(MAX_DISTANCE / max_exact)
                         * (N_BUCKETS - max_exact)).astype(np.int64)
    large = np.minimum(large, N_BUCKETS - 1)
    return np.where(dist < max_exact, dist, large).astype(np.int32)


def _band_buckets(dilation):
    a = np.arange(BLK)[None, :]
    bk = np.arange(2 * BLK)[:, None]
    steps = BLK + a - bk
    return _t5_bucket(np.clip(steps, 0, SPAN) * dilation)


def _head_mask(shape, hh):
    lane = lax.broadcasted_iota(jnp.int32, shape, 1)
    return (lane >= hh * HEAD_DIM) & (lane < (hh + 1) * HEAD_DIM)


def _dot_nt(a, b):
    return lax.dot_general(a, b, (((1,), (1,)), ((), ())), preferred_element_type=F32)


def _dot_tn(a, b):
    return lax.dot_general(a, b, (((0,), (0,)), ((), ())), preferred_element_type=F32)


def _dot_nn(a, b):
    return lax.dot_general(a, b, (((1,), (0,)), ((), ())), preferred_element_type=F32)


def _dil_fwd(qkv, bias, branch, dilation, name, comm=None):
    B, n, _ = qkv.shape
    d = dilation
    nb = n // BLK
    qkv_v = qkv
    npair = N_HEADS // 2

    def body(cur_ref, prev_ref, bias_ref, o_ref, lse_ref, s_scr, e_scr):
        first = jnp.where(pl.program_id(1) == 0, 1, 0)
        units = [(b, h) for b in range(B) for h in range(N_HEADS)]
        for b in range(B):
            for p in range(npair):
                q = cur_ref[b, :, p * LANES:(p + 1) * LANES] * DIL_SCALE
                kc = cur_ref[b, :, D_A + p * LANES:D_A + (p + 1) * LANES]
                kp = prev_ref[b, :, D_A + p * LANES:D_A + (p + 1) * LANES]
                for hh in range(2):
                    u = b * N_HEADS + 2 * p + hh
                    qm = jnp.where(_head_mask((BLK, LANES), hh), q, jnp.zeros_like(q))
                    s_scr[u, 0:BLK, :] = _dot_nt(kp, qm)
                    s_scr[u, BLK:2 * BLK, :] = _dot_nt(kc, qm)
        ms = []
        for u, (b, h) in enumerate(units):
            s_p = s_scr[u, 0:BLK, :] + bias_ref[first, h, 0:BLK, :]
            s_c = s_scr[u, BLK:2 * BLK, :] + bias_ref[first, h, BLK:2 * BLK, :]
            m = jnp.maximum(jnp.max(s_p, axis=0, keepdims=True), jnp.max(s_c, axis=0, keepdims=True))
            e_scr[u, 0:BLK, :] = jnp.exp(s_p - m).astype(BF16)
            e_scr[u, BLK:2 * BLK, :] = jnp.exp(s_c - m).astype(BF16)
            ms.append(m)
        rows0 = _row_mask((LANES, BLK), 0)
        for b in range(B):
            for p in range(npair):
                sl = slice(p * LANES, (p + 1) * LANES)
                vsl = slice(2 * D_A + p * LANES, 2 * D_A + (p + 1) * LANES)
                vct = jnp.transpose(cur_ref[b, :, vsl].astype(F32)).astype(BF16)
                vpt = jnp.transpose(prev_ref[b, :, vsl].astype(F32)).astype(BF16)
                acc = []
                for hh in range(2):
                    u = b * N_HEADS + 2 * p + hh
                    mine = _row_mask((LANES, BLK), hh)
                    one = jnp.ones_like(vct)
                    acc.append(_dot_nn(jnp.where(mine, vpt, one), e_scr[u, 0:BLK, :])
                               + _dot_nn(jnp.where(mine, vct, one), e_scr[u, BLK:2 * BLK, :]))
                l0 = acc[0][HEAD_DIM:HEAD_DIM + 1, :]
                l1 = acc[1][0:1, :]
                u0 = b * N_HEADS + 2 * p
                o_t = jnp.where(rows0, acc[0] / l0, acc[1] / l1)
                lse_t = jnp.where(rows0, ms[u0] + jnp.log(l0), ms[u0 + 1] + jnp.log(l1))
                o_ref[b, :, sl] = jnp.transpose(o_t)
                lse_ref[b, :, sl] = jnp.transpose(lse_t)

    cur = pl.BlockSpec((B, BLK, P_QKV), lambda r, i: (0, i, r))
    prev = pl.BlockSpec((B, BLK, P_QKV), lambda r, i: (0, jnp.maximum(i - 1, 0), r))
    out = pl.BlockSpec((B, BLK, D_A), lambda r, i: (0, i, r))
    return _host_call(
        body, comm, name=name,
        out_shape=[jax.ShapeDtypeStruct((B, n, d * D_A), F32)] * 2,
        grid=(d, nb),
        in_specs=[cur, prev,
                  pl.BlockSpec((None, 2, N_HEADS, 2 * BLK, BLK), lambda r, i: (branch, 0, 0, 0, 0))],
        out_specs=[out, out],
        scratch_shapes=[pltpu.VMEM((B * N_HEADS, 2 * BLK, BLK), F32),
                        pltpu.VMEM((B * N_HEADS, 2 * BLK, BLK), BF16)],
        args=(qkv_v, qkv_v, bias))


VIEW_TILE = 512


def _view_spec(d, w):
    return pl.BlockSpec((1, VIEW_TILE // d, d * w), lambda b, t: (b, t, 0))


def _view_shape(B, S, d, w, dtype):
    return jax.ShapeDtypeStruct((B, S // d, d * w), dtype)


def _tile_scratch(w):
    return pltpu.VMEM((w // LANES, VIEW_TILE, LANES), F32)


def _put_tile(tile_ref, val):
    for c in range(tile_ref.shape[0]):
        tile_ref[c] = val[:, c * LANES:(c + 1) * LANES]


def _get_tile(tile_ref):
    return jnp.concatenate([tile_ref[c] for c in range(tile_ref.shape[0])], axis=1)


def _tile_to_view(tile_ref, view_ref, d, w):
    for c in range(w // LANES):
        for r in range(d):
            lo = r * w + c * LANES
            rows = tile_ref.at[c][pl.ds(r, VIEW_TILE // d, stride=d), :]
            view_ref[0, :, lo:lo + LANES] = rows.astype(view_ref.dtype)


def _view_to_tile(view_ref, tile_ref, d, w):
    for c in range(w // LANES):
        for r in range(d):
            lo = r * w + c * LANES
            tile_ref.at[c][pl.ds(r, VIEW_TILE // d, stride=d), :] = view_ref[0, :, lo:lo + LANES].astype(F32)


def _in_proj(h, w_qkv, w_rest, g_cq, g_ckv, name):
    B, S, D = h.shape
    N = w_qkv.shape[1]
    cq_lo = KV_LORA + LANES

    def rms(xv, g_ref):
        return ((xv * lax.rsqrt(jnp.mean(xv * xv, axis=-1, keepdims=True) + EPS)) * g_ref[...]).astype(BF16)

    def body(h_ref, wq_ref, wr_ref, gcq_ref, gckv_ref, o1_ref, o4_ref, o16_ref, rest_ref, cqn_ref, ckvn_ref,
             acc_ref):
        hv = h_ref[0]
        acc = jnp.dot(hv, wq_ref[...], preferred_element_type=F32)
        o1_ref[0] = acc.astype(o1_ref.dtype)
        _put_tile(acc_ref, acc)
        _tile_to_view(acc_ref, o4_ref, DILATIONS[1], N)
        _tile_to_view(acc_ref, o16_ref, DILATIONS[2], N)
        rest = jnp.dot(hv, wr_ref[...], preferred_element_type=F32)
        rest_ref[0] = rest
        ckvn_ref[...] = rms(rest[:, :KV_LORA], gckv_ref)
        cqn_ref[...] = rms(rest[:, cq_lo:], gcq_ref)

    tiles = S // VIEW_TILE
    full = lambda a: pl.BlockSpec(a.shape, lambda b, t: (0, 0))
    rows = lambda n: pl.BlockSpec((VIEW_TILE, n), lambda b, t: (b * tiles + t, 0))
    res = pl.pallas_call(
        body, name=name,
        out_shape=[_view_shape(B, S, d, N, BF16) for d in DILATIONS]
        + [jax.ShapeDtypeStruct((B, S, P_REST), F32), jax.ShapeDtypeStruct((B * S, Q_LORA), BF16),
           jax.ShapeDtypeStruct((B * S, KV_LORA), BF16)],
        grid=(B, tiles),
        in_specs=[pl.BlockSpec((1, VIEW_TILE, D), lambda b, t: (b, t, 0)), full(w_qkv), full(w_rest), full(g_cq),
                  full(g_ckv)],
        out_specs=[_view_spec(d, N) for d in DILATIONS] + [_view_spec(1, P_REST), rows(Q_LORA), rows(KV_LORA)],
        scratch_shapes=[_tile_scratch(N)],
        compiler_params=_cparams(("parallel", "parallel")),
    )(h, w_qkv, w_rest, g_cq, g_ckv)
    return res[:len(DILATIONS)], res[len(DILATIONS)], res[len(DILATIONS) + 1], res[len(DILATIONS) + 2]


def _dil_merge(os_, lses, name):
    B, S, W = os_[0].shape
    nd = len(DILATIONS)

    def body(*refs):
        o_refs, l_refs = refs[:nd], refs[nd:2 * nd]
        out_refs, L_refs = refs[2 * nd:3 * nd], refs[3 * nd:4 * nd]
        scr = refs[4 * nd:]
        o_tok, l_tok = [o_refs[0][0]], [l_refs[0][0]]
        for i, d in enumerate(DILATIONS[1:]):
            _view_to_tile(o_refs[i + 1], scr[2 * i], d, W)
            _view_to_tile(l_refs[i + 1], scr[2 * i + 1], d, W)
            o_tok.append(_get_tile(scr[2 * i]))
            l_tok.append(_get_tile(scr[2 * i + 1]))
        a0, a1, a2 = l_tok
        m = jnp.maximum(jnp.maximum(a0, a1), a2)
        e0, e1, e2 = jnp.exp(a0 - m), jnp.exp(a1 - m), jnp.exp(a2 - m)
        ssum = e0 + e1 + e2
        out = (e0 * o_tok[0] + e1 * o_tok[1] + e2 * o_tok[2]) / ssum
        lse = m + jnp.log(ssum)
        out_refs[0][0] = out
        L_refs[0][0] = lse
        res_o, res_l = scr[2 * (nd - 1)], scr[2 * (nd - 1) + 1]
        _put_tile(res_o, out)
        _put_tile(res_l, lse)
        for i, d in enumerate(DILATIONS[1:]):
            _tile_to_view(res_o, out_refs[i + 1], d, W)
            _tile_to_view(res_l, L_refs[i + 1], d, W)

    specs = [_view_spec(d, W) for d in DILATIONS]
    shapes = [_view_shape(B, S * DILATIONS[0], d, W, F32) for d in DILATIONS]
    res = pl.pallas_call(
        body, name=name, out_shape=shapes * 2, grid=(B, S // VIEW_TILE),
        in_specs=specs * 2, out_specs=specs * 2,
        scratch_shapes=[_tile_scratch(W)] * (2 * nd),
        compiler_params=_cparams(("parallel", "parallel")),
    )(*os_, *lses)
    return res[:nd], res[nd:]


def _dil_bwd(qkv, do, out_a, L, bias, branch, dilation, name, comm=None):
    B, n, _ = qkv.shape
    d = dilation
    nb = n // BLK
    qkv_v, do_v, oa_v, L_v = qkv, do, out_a, L
    npair = N_HEADS // 2
    multi = nb > 1

    tiles = ("P", "C", "N") if multi else ("C",)
    n_t = len(tiles)

    def body(*refs):
        if multi:
            (cur_ref, prev_ref, next_ref, do_ref, don_ref, oa_ref, oan_ref, L_ref, Ln_ref, bias_ref,
             dqkv_ref, dbias_ref, s_scr, dp_scr, p_scr, ds_scr) = refs
        else:
            cur_ref, do_ref, oa_ref, L_ref, bias_ref, dqkv_ref, dbias_ref, s_scr, dp_scr, p_scr, ds_scr = refs
        r, i = pl.program_id(0), pl.program_id(1)

        @pl.when((r == 0) & (i == 0))
        def _():
            dbias_ref[...] = jnp.zeros_like(dbias_ref)

        first = jnp.where(i == 0, 1, 0)
        variant = {"P": first, "C": first, "N": 0}
        band = {"P": slice(0, BLK), "C": slice(BLK, 2 * BLK), "N": slice(0, BLK)}
        psl = lambda p: slice(p * LANES, (p + 1) * LANES)
        ksl = lambda p: slice(D_A + p * LANES, D_A + (p + 1) * LANES)
        vsl = lambda p: slice(2 * D_A + p * LANES, 2 * D_A + (p + 1) * LANES)

        def operands(b, p, hh):
            hm = _head_mask((BLK, LANES), hh)
            mask = lambda x: jnp.where(hm, x, jnp.zeros_like(x))
            qm, dom = mask(cur_ref[b, :, psl(p)] * DIL_SCALE), mask(do_ref[b, :, psl(p)])
            ops = {"C": (cur_ref[b, :, ksl(p)], cur_ref[b, :, vsl(p)], qm, dom)}
            if multi:
                ops["P"] = (prev_ref[b, :, ksl(p)], prev_ref[b, :, vsl(p)], qm, dom)
                ops["N"] = (cur_ref[b, :, ksl(p)], cur_ref[b, :, vsl(p)],
                            mask(next_ref[b, :, psl(p)] * DIL_SCALE), mask(don_ref[b, :, psl(p)]))
            return ops

        pairs = [(b, p) for b in range(B) for p in range(npair)]
        for b, p in pairs:
            for hh in range(2):
                u = b * N_HEADS + 2 * p + hh
                ops = operands(b, p, hh)
                for t, name_t in enumerate(tiles):
                    k_t, v_t, q_t, do_t = ops[name_t]
                    s_scr[u, t] = _dot_nt(k_t, q_t)
                    dp_scr[u, t] = _dot_nt(v_t, do_t)

        def rows(L_r, do_r, oa_r, b, p):
            lt = jnp.transpose(L_r[b, :, psl(p)])
            dt = jnp.transpose(do_r[b, :, psl(p)].astype(F32) * oa_r[b, :, psl(p)])
            return ([lt[0:1, :], lt[HEAD_DIM:HEAD_DIM + 1, :]],
                    [jnp.sum(dt[:HEAD_DIM], axis=0, keepdims=True), jnp.sum(dt[HEAD_DIM:], axis=0, keepdims=True)])

        for b, p in pairs:
            lse_c, delta_c = rows(L_ref, do_ref, oa_ref, b, p)
            if multi:
                lse_n, delta_n = rows(Ln_ref, don_ref, oan_ref, b, p)
            for hh in range(2):
                h = 2 * p + hh
                u = b * N_HEADS + h
                for t, name_t in enumerate(tiles):
                    lse, delta = (lse_n[hh], delta_n[hh]) if name_t == "N" else (lse_c[hh], delta_c[hh])
                    pr = jnp.exp(s_scr[u, t] + bias_ref[variant[name_t], h, band[name_t], :] - lse)
                    if name_t == "N":
                        pr = jnp.where(i < nb - 1, pr, 0.0)
                    ds = pr * (dp_scr[u, t] - delta)
                    p_scr[u, t] = pr.astype(BF16)
                    ds_scr[u, t] = ds.astype(BF16)
                    if name_t != "N":
                        dbias_ref[h, band[name_t], :] += ds

        for b, p in pairs:
            dqt = jnp.zeros((LANES, BLK), F32)
            dk = jnp.zeros((BLK, LANES), F32)
            dv = jnp.zeros((BLK, LANES), F32)
            kct = jnp.transpose(cur_ref[b, :, ksl(p)].astype(F32)).astype(BF16)
            if multi:
                kpt = jnp.transpose(prev_ref[b, :, ksl(p)].astype(F32)).astype(BF16)
            for hh in range(2):
                u = b * N_HEADS + 2 * p + hh
                ops = operands(b, p, hh)
                mine = _row_mask((LANES, BLK), hh)
                for t, name_t in enumerate(tiles):
                    _, _, q_t, do_t = ops[name_t]
                    if name_t != "P":
                        dv = dv + _dot_nn(p_scr[u, t], do_t)
                        dk = dk + _dot_nn(ds_scr[u, t], q_t)
                    if name_t != "N":
                        kt = kpt if name_t == "P" else kct
                        dqt = dqt + _dot_nn(jnp.where(mine, kt, jnp.zeros_like(kt)), ds_scr[u, t])
            dqkv_ref[b, :, psl(p)] = jnp.transpose(dqt) * DIL_SCALE
            dqkv_ref[b, :, ksl(p)] = dk
            dqkv_ref[b, :, vsl(p)] = dv

    def at(off):
        return lambda r, i: (0, jnp.clip(i + off, 0, nb - 1), r)

    qkv_spec = lambda off: pl.BlockSpec((B, BLK, P_QKV), at(off))
    da_spec = lambda off: pl.BlockSpec((B, BLK, D_A), at(off))
    bias_spec = pl.BlockSpec((None, 2, N_HEADS, 2 * BLK, BLK), lambda r, i: (branch, 0, 0, 0, 0))
    dbias_spec = pl.BlockSpec((N_HEADS, 2 * BLK, BLK), lambda r, i: (0, 0, 0))
    if multi:
        in_specs = [qkv_spec(0), qkv_spec(-1), qkv_spec(1), da_spec(0), da_spec(1), da_spec(0), da_spec(1),
                    da_spec(0), da_spec(1), bias_spec]
        args = [qkv_v, qkv_v, qkv_v, do_v, do_v, oa_v, oa_v, L_v, L_v, bias]
    else:
        in_specs = [qkv_spec(0), da_spec(0), da_spec(0), da_spec(0), bias_spec]
        args = [qkv_v, do_v, oa_v, L_v, bias]
    return _host_call(
        body, comm, name=name,
        out_shape=[jax.ShapeDtypeStruct((B, n, d * P_QKV), F32),
                   jax.ShapeDtypeStruct((N_HEADS, 2 * BLK, BLK), F32)],
        grid=(d, nb),
        in_specs=in_specs,
        out_specs=[qkv_spec(0), dbias_spec],
        scratch_shapes=[pltpu.VMEM((B * N_HEADS, n_t, BLK, BLK), F32), pltpu.VMEM((B * N_HEADS, n_t, BLK, BLK), F32),
                        pltpu.VMEM((B * N_HEADS, n_t, BLK, BLK), BF16),
                        pltpu.VMEM((B * N_HEADS, n_t, BLK, BLK), BF16)],
        args=args)


def _sum_views_bf16(parts, name):
    B, S, W = parts[0].shape

    def body(a_ref, b_ref, c_ref, o_ref, sb, sc):
        _view_to_tile(b_ref, sb, DILATIONS[1], W)
        _view_to_tile(c_ref, sc, DILATIONS[2], W)
        o_ref[0] = (a_ref[0] + _get_tile(sb) + _get_tile(sc)).astype(o_ref.dtype)

    return pl.pallas_call(
        body, name=name, out_shape=jax.ShapeDtypeStruct((B, S, W), BF16), grid=(B, S // VIEW_TILE),
        in_specs=[_view_spec(d, W) for d in DILATIONS], out_specs=_view_spec(1, W),
        scratch_shapes=[_tile_scratch(W)] * 2,
        compiler_params=_cparams(("parallel", "parallel")),
    )(*parts)


def _bias_tables(rel_bias, buckets, name, comm=None):
    nbr = buckets.shape[0]

    def body(rb_ref, bk_ref, o_ref):
        first, h = pl.program_id(1), pl.program_id(2)
        tab = bk_ref[0]

        def step(bkt, acc):
            return jnp.where(tab == bkt, rb_ref[bkt, h], acc)

        bias = lax.fori_loop(0, N_BUCKETS, step, jnp.zeros((2 * BLK, BLK), F32))
        row = lax.broadcasted_iota(jnp.int32, (2 * BLK, BLK), 0)
        col = lax.broadcasted_iota(jnp.int32, (2 * BLK, BLK), 1)
        valid = ((row < BLK) & (row >= col) & (first == 0)) | ((row >= BLK) & (row - BLK <= col))
        o_ref[0, 0, 0] = jnp.where(valid, bias, NEG)

    (bias,), got = _host_call(
        body, comm, name=name, out_shape=[jax.ShapeDtypeStruct((nbr, 2, N_HEADS, 2 * BLK, BLK), F32)],
        grid=(nbr, 2, N_HEADS),
        in_specs=[pl.BlockSpec(memory_space=pltpu.SMEM),
                  pl.BlockSpec((1, 2 * BLK, BLK), lambda i, f, h: (i, 0, 0))],
        out_specs=[pl.BlockSpec((1, 1, 1, 2 * BLK, BLK), lambda i, f, h: (i, f, h, 0, 0))],
        scratch_shapes=[], args=(rel_bias, buckets))
    return bias, got


def _bias_grad(dbias_list, buckets, name):
    nbr = len(dbias_list)

    def body(*refs):
        d_refs, bk_ref, o_ref, part = refs[:nbr], refs[nbr], refs[nbr + 1], refs[nbr + 2]

        def step(bkt, carry):
            hit = [bk_ref[bi] == bkt for bi in range(nbr)]
            for h in range(N_HEADS):
                tot = jnp.zeros((1, BLK), F32)
                for bi in range(nbr):
                    tot = tot + jnp.sum(jnp.where(hit[bi], d_refs[bi][h], 0.0), axis=0, keepdims=True)
                part[bkt, h:h + 1, :] = tot
            return carry

        lax.fori_loop(0, N_BUCKETS, step, 0)
        lane = lax.broadcasted_iota(jnp.int32, (N_HEADS, LANES), 1)
        acc = jnp.zeros((N_HEADS, LANES), F32)
        for bkt in range(N_BUCKETS):
            acc = acc + jnp.where(lane == bkt, jnp.sum(part[bkt], axis=1, keepdims=True), 0.0)
        o_ref[...] = acc

    band = pl.BlockSpec((N_HEADS, 2 * BLK, BLK), lambda i: (0, 0, 0))
    return pl.pallas_call(
        body, name=name, out_shape=jax.ShapeDtypeStruct((N_HEADS, LANES), F32), grid=(1,),
        in_specs=[band] * nbr + [pl.BlockSpec((nbr, 2 * BLK, BLK), lambda i: (0, 0, 0))],
        out_specs=pl.BlockSpec((N_HEADS, LANES), lambda i: (0, 0)),
        scratch_shapes=[pltpu.VMEM((N_BUCKETS, N_HEADS, BLK), F32)],
        compiler_params=_cparams(("arbitrary",)),
    )(*dbias_list, buckets)


MLA_TQ = 256
MLA_TK = 256


LOG2E = math.log2(math.e)
MLA_C = MLA_SCALE * LOG2E


def _key_le_query(tk, tq):
    return lax.broadcasted_iota(jnp.int32, (tk, tq), 0) <= lax.broadcasted_iota(jnp.int32, (tk, tq), 1)


def _row_mask(shape, hh):
    row = lax.broadcasted_iota(jnp.int32, shape, 0)
    return (row >= hh * HEAD_DIM) & (row < (hh + 1) * HEAD_DIM)


def _host_call(body, comm, *, name, grid, in_specs, out_specs, out_shape, scratch_shapes, args):
    sem = ("arbitrary",) * len(grid)
    if comm is None:
        res = pl.pallas_call(body, name=name, grid=grid, in_specs=in_specs, out_specs=out_specs,
                             out_shape=out_shape, scratch_shapes=scratch_shapes,
                             compiler_params=_cparams(sem))(*args)
        return res, []
    n_in, n_out, n_s, cn = len(in_specs), len(out_specs), len(scratch_shapes), comm.n

    def hosted(*refs):
        ins, refs = refs[:n_in], refs[n_in:]
        c_ins, refs = refs[:cn], refs[cn:]
        outs, refs = refs[:n_out], refs[n_out:]
        c_outs, refs = refs[:cn], refs[cn:]
        scr, c_sems = refs[:n_s], refs[n_s:]
        ids = [pl.program_id(a) for a in range(len(grid))]
        first = functools.reduce(jnp.logical_and, [i == 0 for i in ids])
        last = functools.reduce(jnp.logical_and, [i == g - 1 for i, g in zip(ids, grid)])

        @pl.when(first)
        def _():
            comm.start(c_ins, c_outs, c_sems)

        body(*ins, *outs, *scr)

        @pl.when(last)
        def _():
            comm.finish(c_ins, c_outs, c_sems)

    res = pl.pallas_call(
        hosted, name=name, grid=grid, in_specs=list(in_specs) + _hbm_specs(cn),
        out_specs=list(out_specs) + _hbm_specs(cn), out_shape=list(out_shape) + list(comm.out_shape),
        scratch_shapes=list(scratch_shapes) + list(comm.scratch), compiler_params=_cparams(sem),
    )(*args, *comm.inputs)
    return res[:n_out], res[n_out:]


def _mla_fwd_t(q, k, vt, name, comm=None):
    B, S, _ = q.shape
    tq, tk = MLA_TQ, MLA_TK
    assert tq == tk
    npair = N_HEADS // 2
    nq = S // tq

    def body(q_ref, k_ref, vt_ref, o_ref, lse_ref, s_scr, e_scr, acc_scr, m_scr, a_scr):
        i = pl.program_id(1)
        diag = _key_le_query(tk, tq)
        m_scr[...] = jnp.full_like(m_scr, NEG)
        acc_scr[...] = jnp.zeros_like(acc_scr)

        def step(j, masked):
            rows = pl.ds(pl.multiple_of(j * tk, tk), tk)
            for h in range(N_HEADS):
                hsl = slice(h * LANES, (h + 1) * LANES)
                s_scr[h] = _dot_nt(k_ref[0, rows, hsl], q_ref[0, :, hsl])
            for h in range(N_HEADS):
                s = s_scr[h]
                if masked:
                    s = jnp.where(diag, s, NEG)
                m_old = m_scr[h:h + 1, :]
                m_new = jnp.maximum(m_old, jnp.max(s, axis=0, keepdims=True))
                a_scr[h:h + 1, :] = jnp.exp2((m_old - m_new) * MLA_C)
                e_scr[h] = jnp.exp2((s - m_new) * MLA_C).astype(BF16)
                m_scr[h:h + 1, :] = m_new
            for h in range(N_HEADS):
                vj = vt_ref[0, h // 2, j]
                vh = jnp.where(_row_mask(vj.shape, h % 2), vj, jnp.ones_like(vj))
                acc_scr[h] = acc_scr[h] * a_scr[h:h + 1, :] + _dot_nn(vh, e_scr[h])

        def loop_body(j, carry):
            step(j, False)
            return carry

        lax.fori_loop(0, i, loop_body, 0)
        step(i, True)
        rows0 = _row_mask((LANES, tq), 0)
        for p in range(npair):
            l0 = acc_scr[2 * p, HEAD_DIM:HEAD_DIM + 1, :]
            l1 = acc_scr[2 * p + 1, 0:1, :]
            o_t = jnp.where(rows0, acc_scr[2 * p] / l0, acc_scr[2 * p + 1] / l1)
            o_ref[0, :, p * LANES:(p + 1) * LANES] = jnp.transpose(o_t)
            lse_ref[0, p, 0] = jnp.zeros((8, tq), F32)
            lse_ref[0, p, 0, 0:1, :] = m_scr[2 * p:2 * p + 1, :] * MLA_C + jnp.log(l0) * LOG2E
            lse_ref[0, p, 0, 1:2, :] = m_scr[2 * p + 1:2 * p + 2, :] * MLA_C + jnp.log(l1) * LOG2E

    return _host_call(
        body, comm, name=name,
        out_shape=[jax.ShapeDtypeStruct((B, S, D_B), F32), jax.ShapeDtypeStruct((B, npair, nq, 8, tq), F32)],
        grid=(B, nq),
        in_specs=[pl.BlockSpec((1, tq, N_HEADS * LANES), lambda b, i: (b, i, 0)),
                  pl.BlockSpec((1, S, N_HEADS * LANES), lambda b, i: (b, 0, 0)),
                  pl.BlockSpec((1, npair, S // tk, LANES, tk), lambda b, i: (b, 0, 0, 0, 0))],
        out_specs=[pl.BlockSpec((1, tq, D_B), lambda b, i: (b, i, 0)),
                   pl.BlockSpec((1, npair, 1, 8, tq), lambda b, i: (b, 0, i, 0, 0))],
        scratch_shapes=[pltpu.VMEM((N_HEADS, tk, tq), F32), pltpu.VMEM((N_HEADS, tk, tq), BF16),
                        pltpu.VMEM((N_HEADS, LANES, tq), F32), pltpu.VMEM((N_HEADS, tq), F32),
                        pltpu.VMEM((N_HEADS, tq), F32)],
        args=(q, k, vt))


def _mla_bwd_t(q, k, v, do, lse, delta, name, comm=None):
    B, S, _ = q.shape
    tq, tk = MLA_TQ, MLA_TK
    assert tq == tk
    npair = N_HEADS // 2
    nq = S // tq

    hg = N_HEADS
    pg = hg // 2
    ngroup = N_HEADS // hg

    def body(q_ref, do_ref, lse_ref, dl_ref, k_ref, v_ref, dk_ref, dv_ref, dq_ref,
             s_scr, dp_scr, p_scr, ds_scr, dk_s, dv_s, kt_s):
        j = pl.program_id(2)

        @pl.when(j == 0)
        def _():
            dq_ref[...] = jnp.zeros_like(dq_ref)

        dk_s[...] = jnp.zeros_like(dk_s)
        dv_s[...] = jnp.zeros_like(dv_s)
        diag = _key_le_query(tk, tq)
        hsl = lambda h: slice(h * LANES, (h + 1) * LANES)
        for h in range(hg):
            kt_s[h] = jnp.transpose(k_ref[0, :, hsl(h)].astype(F32)).astype(BF16)

        def step(i, masked):
            rows = pl.ds(pl.multiple_of(i * tq, tq), tq)

            def dom(h):
                dov = do_ref[0, rows, hsl(h // 2)]
                return jnp.where(_head_mask((tq, LANES), h % 2), dov, jnp.zeros_like(dov))

            for h in range(hg):
                s_scr[h] = _dot_nt(k_ref[0, :, hsl(h)], q_ref[0, rows, hsl(h)])
                dp_scr[h] = _dot_nt(v_ref[0, :, hsl(h // 2)], dom(h))
            for h in range(hg):
                pr = jnp.exp2(s_scr[h] * MLA_C - lse_ref[0, h // 2, i, h % 2:h % 2 + 1, :])
                if masked:
                    pr = jnp.where(diag, pr, 0.0)
                p_scr[h] = pr.astype(BF16)
                ds_scr[h] = (pr * (dp_scr[h] - dl_ref[0, h // 2, i, h % 2:h % 2 + 1, :])).astype(BF16)
            for h in range(hg):
                dv_s[h // 2] += _dot_nn(p_scr[h], dom(h))
                dk_s[h] += _dot_nn(ds_scr[h], q_ref[0, rows, hsl(h)])
                dq_ref[0, h // 2, i, hsl(h % 2), :] += _dot_nn(kt_s[h], ds_scr[h]) * MLA_SCALE

        step(j, True)

        def loop_body(i, carry):
            step(i, False)
            return carry

        lax.fori_loop(j + 1, nq, loop_body, 0)
        for h in range(hg):
            dk_ref[0, :, hsl(h)] = dk_s[h] * MLA_SCALE
        for p in range(pg):
            dv_ref[0, :, hsl(p)] = dv_s[p]

    stat = pl.BlockSpec((1, pg, nq, 8, tq), lambda b, g, j: (b, g, 0, 0, 0))
    return _host_call(
        body, comm, name=name,
        out_shape=[jax.ShapeDtypeStruct((B, S, N_HEADS * LANES), F32), jax.ShapeDtypeStruct((B, S, D_B), F32),
                   jax.ShapeDtypeStruct((B, npair, nq, 2 * LANES, tq), F32)],
        grid=(B, ngroup, S // tk),
        in_specs=[pl.BlockSpec((1, S, hg * LANES), lambda b, g, j: (b, 0, g)),
                  pl.BlockSpec((1, S, pg * LANES), lambda b, g, j: (b, 0, g)),
                  stat, stat,
                  pl.BlockSpec((1, tk, hg * LANES), lambda b, g, j: (b, j, g)),
                  pl.BlockSpec((1, tk, pg * LANES), lambda b, g, j: (b, j, g))],
        out_specs=[pl.BlockSpec((1, tk, hg * LANES), lambda b, g, j: (b, j, g)),
                   pl.BlockSpec((1, tk, pg * LANES), lambda b, g, j: (b, j, g)),
                   pl.BlockSpec((1, pg, nq, 2 * LANES, tq), lambda b, g, j: (b, g, 0, 0, 0))],
        scratch_shapes=[pltpu.VMEM((hg, tk, tq), F32), pltpu.VMEM((hg, tk, tq), F32),
                        pltpu.VMEM((hg, tk, tq), BF16), pltpu.VMEM((hg, tk, tq), BF16),
                        pltpu.VMEM((hg, tk, LANES), F32), pltpu.VMEM((pg, tk, LANES), F32),
                        pltpu.VMEM((hg, LANES, tk), BF16)],
        args=(q, do, lse, delta, k, v))


def _bucket_tables():
    return jnp.asarray(np.stack([_band_buckets(d) for d in DILATIONS]))


def _local_step(x, target, mod, wts, gains, rel_bias, ffn_shards=None, bias=None):
    B, S, D = x.shape
    T = B * S
    sh1, sc1, g1, sh2, sc2, g2 = [mod[:, i * D:(i + 1) * D].reshape(B, 1, D) for i in range(N_MOD)]
    cs, sn = _rope_tables()
    buckets_dev = _bucket_tables()
    if bias is None:
        bias, _ = _bias_tables(rel_bias, buckets_dev, "rel_bias_tables")
    w_in = wts["w_in"]

    h1 = _adaln_fwd(x, gains["g_norm1"], sc1, sh1, "adaln1_fwd")
    h1f = h1.reshape(T, D)
    qkv_v, rest3, cqn, ckvn = _in_proj(h1, w_in[:, :P_QKV], w_in[:, P_QKV:], gains["g_cq"], gains["g_ckv"],
                                       "mm_in")
    o_d, lse_d = [], []
    late_got = []
    for i, d in enumerate(DILATIONS):
        comm = _GatherComm(ffn_shards[i + 1:i + 2]) if (ffn_shards and i < 2) else None
        (o_i, lse_i), got = _dil_fwd(qkv_v[i], bias, i, d, f"dil_fwd_{d}", comm)
        late_got += list(got)
        o_d.append(o_i)
        lse_d.append(lse_i)
    if ffn_shards:
        wts = dict(wts, w_out=late_got[1].reshape(D, D))
    out_a_v, lse_a_v = _dil_merge(o_d, lse_d, "dil_merge")
    out_a = out_a_v[0]
    qc, kc, v, vt = _mla_proj(cqn, ckvn, rest3, wts["w_uq"], wts["w_kv"], cs, sn, "mla_proj")
    (out_b, lse_b), got = _mla_fwd_t(qc, kc, vt, "mla_fwd", _GatherComm(ffn_shards[:1]) if ffn_shards else None)
    if ffn_shards:
        wts = dict(wts, w_ffn_in=got[0].reshape(N_CHIP, D, -1), w_ffn_out=late_got[0].reshape(D_FF, D))
    y, mix, x1, h2 = _mixer_out(out_a, out_b, gains["g_out_a"], gains["g_out_b"], wts["w_out"], x, g1,
                                gains["g_norm2"], sc2, sh2, "mixer_out")
    h2f = h2.reshape(T, D)
    gu, act = _ffn_in_fwd(h2f, wts["w_ffn_in"], "mm_ffn_in")
    dx2, df, dg2, dg_final, loss = _final_loss(x1, act, wts["w_ffn_out"], g2, gains["g_final"], target,
                                               "ffn_out_loss")

    dff = df.reshape(T, D)
    dgu = _ffn_out_bwd(dff, wts["w_ffn_out"], gu, "mm_ffn_out_dx")
    gw_ffn_out = _mm(act, dff, "tn", F32, "mm_ffn_out_dw")
    dh2 = _mm(dgu, wts["w_ffn_in"], "nt", F32, "mm_ffn_in_dx", col_blocks=N_CHIP, halves=True).reshape(B, S, D)
    gw_ffn_in = _mm(h2f, dgu, "tn", F32, "mm_ffn_in_dw", col_blocks=N_CHIP, halves=True)
    ffn_g8 = ffn_r1 = None
    if ffn_shards:
        ffn_g8 = [gw_ffn_in.reshape(N_DEV, -1, gw_ffn_in.shape[-1]), gw_ffn_out.reshape(N_DEV, -1, D)]
        dx1, dsh2, dsc2, dg_norm2, dg1, dmix, ffn_r1 = _adaln_bwd(
            dh2, x1, gains["g_norm2"], sc2, dx2, "adaln2_bwd", mix=mix, gate=g1, comm=_ToSiblingComm(ffn_g8))
    else:
        dx1, dsh2, dsc2, dg_norm2, dg1, dmix = _adaln_bwd(dh2, x1, gains["g_norm2"], sc2, dx2, "adaln2_bwd",
                                                          mix=mix, gate=g1)
    dmixf = dmix.reshape(T, D)
    gw_out = _mm(y, dmixf, "tn", F32, "mm_out_dw")
    do_a_v, dg_out_a, do_b3, dg_out_b, delta_b = _mixer_out_bwd(dmix, wts["w_out"], out_a, out_b, gains["g_out_a"],
                                                               gains["g_out_b"], "mixer_out_bwd")
    ffn_a4 = ffn_send = None
    if ffn_shards:
        ffn_a4, ffn_send = _rs_first(ffn_g8, "ffn", r1=ffn_r1)
    (dkc, dv, dq_t), ffn_r2 = _mla_bwd_t(qc, kc, v, do_b3, lse_b, delta_b, "mla_bwd",
                                         _ToChipsComm(ffn_send[:1]) if ffn_shards else None)
    d_rest, gw_uq, gw_kv, dg_cq, dg_ckv = _mla_proj_bwd(dq_t, dkc, dv, cqn, ckvn, rest3, wts["w_uq"], wts["w_kv"],
                                                        gains["g_cq"], gains["g_ckv"], cs, -sn, "mla_proj_bwd")
    dqkv_d, dbias_d = [], []
    for i, d in enumerate(DILATIONS):
        comm = _ToChipsComm(ffn_send[1:]) if (ffn_shards and i == 0) else None
        (dqkv_i, dbias_i), got = _dil_bwd(qkv_v[i], do_a_v[i], out_a_v[i], lse_a_v[i], bias, i, d,
                                          f"dil_bwd_{d}", comm)
        if comm is not None:
            ffn_r2 = list(ffn_r2) + list(got)
        dqkv_d.append(dqkv_i)
        dbias_d.append(dbias_i)
    dqkv = _sum_views_bf16(dqkv_d, "dil_bwd_sum").reshape(T, P_QKV)
    g_rel_bias = _bias_grad(dbias_d, buckets_dev, "rel_bias_grad")[:, :N_BUCKETS].T
    dproj = jnp.concatenate([dqkv, d_rest.reshape(T, P_REST)], axis=1)
    gw_in = _mm(h1f, dproj, "tn", F32, "mm_in_dw")
    mix_a4 = mix_r2 = None
    if ffn_shards:
        nat = [_w_in_from_kernel(gw_in), _w_uq_from_kernel(gw_uq), _w_ukv_from_kernel(gw_kv)]
        g8 = [_shards_from_full(g) for g in nat] + [gw_out]
        mix_a4, mix_send = _rs_first([g.reshape(N_DEV, -1, g.shape[-1]) for g in g8], "mix")
        dh1, mix_r2 = _mm(dproj, w_in, "nt", F32, "mm_in_dx", comm=_ToChipsComm(mix_send))
    else:
        dh1 = _mm(dproj, w_in, "nt", F32, "mm_in_dx")
    dh1 = dh1.reshape(B, S, D)
    grad_x, dsh1, dsc1, dg_norm1 = _adaln_bwd(dh1, x, gains["g_norm1"], sc1, dx1, "adaln1_bwd")
    gmod = jnp.concatenate([dsh1, dsc1, dg1, dsh2, dsc2, dg2], axis=-1).reshape(B, N_MOD * D)
    grads = dict(w_in=gw_in, w_uq=gw_uq, w_kv=gw_kv, w_out=gw_out, w_ffn_in=gw_ffn_in, w_ffn_out=gw_ffn_out,
                 g_norm1=dg_norm1, g_cq=dg_cq, g_ckv=dg_ckv, rel_bias=g_rel_bias, g_out_a=dg_out_a,
                 g_out_b=dg_out_b, g_norm2=dg_norm2, g_final=dg_final, ffn_pending=(ffn_a4, ffn_r2),
                 mix_pending=(mix_a4, mix_r2))
    return loss, grad_x, gmod, grads


def _w_in_to_kernel(w):
    z = lambda n: jnp.zeros((w.shape[0], n), w.dtype)
    i3, i4, i5 = 3 * D_A, 3 * D_A + Q_LORA, 3 * D_A + Q_LORA + KV_LORA
    return jnp.concatenate([w[:, :i3], w[:, i4:i5], z(NOPE_DIM), w[:, i5:], z(LANES - NOPE_DIM - ROPE_DIM),
                            w[:, i3:i4]], axis=1)


def _w_in_from_kernel(g):
    o = P_QKV + KV_LORA
    return jnp.concatenate([g[:, :P_QKV], g[:, o + LANES:], g[:, P_QKV:o],
                            g[:, o + NOPE_DIM:o + NOPE_DIM + ROPE_DIM]], axis=1)


def _w_uq_to_kernel(w):
    w3 = w.reshape(Q_LORA, N_HEADS, NOPE_DIM + ROPE_DIM)
    return jnp.pad(w3, ((0, 0), (0, 0), (0, LANES - NOPE_DIM - ROPE_DIM))).reshape(Q_LORA, N_HEADS * LANES)


def _w_uq_from_kernel(g):
    return g.reshape(Q_LORA, N_HEADS, LANES)[:, :, :NOPE_DIM + ROPE_DIM].reshape(Q_LORA, -1)


def _w_ukv_to_kernel(w):
    w3 = w.reshape(KV_LORA, N_HEADS, 2 * HEAD_DIM)
    wk = jnp.pad(w3[:, :, :NOPE_DIM], ((0, 0), (0, 0), (0, LANES - NOPE_DIM))).reshape(KV_LORA, N_HEADS * LANES)
    wv = w3[:, :, NOPE_DIM:].reshape(KV_LORA, D_B)
    return jnp.concatenate([wk, wv], axis=1)


def _w_ukv_from_kernel(g):
    gk = g[:, :N_HEADS * LANES].reshape(KV_LORA, N_HEADS, LANES)[:, :, :NOPE_DIM]
    gv = g[:, N_HEADS * LANES:].reshape(KV_LORA, N_HEADS, HEAD_DIM)
    return jnp.concatenate([gk, gv], axis=2).reshape(KV_LORA, -1)


MESH = pl.DeviceIdType.MESH


def _my_place():
    return lax.axis_index("x"), lax.axis_index("y"), lax.axis_index("c")


def _other_chips(x, y):
    return [(1 - x, y), (x, 1 - y), (1 - x, 1 - y)]


def _allgather8(x_shard, name, in_hbm):
    m_per, n = x_shard.shape
    space = pl.ANY if in_hbm else pltpu.VMEM

    def body(x_ref, out_ref, send_sems, recv_sems, local_sem):
        x, y, c = _my_place()
        me, sibling = (x, y, c), (x, y, 1 - c)
        chips = _other_chips(x, y)

        def rows(px, py, pc):
            return out_ref.at[pl.ds((4 * px + 2 * py + pc) * m_per, m_per), :]

        def copy(k, block, to, src=None):
            return pltpu.make_async_remote_copy(
                src_ref=rows(*block) if src is None else src, dst_ref=rows(*block),
                send_sem=send_sems.at[k], recv_sem=recv_sems.at[k], device_id=to, device_id_type=MESH)

        mine = pltpu.make_async_copy(x_ref, rows(*me), local_sem)
        mine.start()
        first = [copy(0, me, sibling, src=x_ref)]
        first += [copy(1 + j, me, (*chip, c), src=x_ref) for j, chip in enumerate(chips)]
        for cp in first:
            cp.start()
        passed = [copy(4 + j, (*chip, c), sibling) for j, chip in enumerate(chips)]
        for j, chip in enumerate(chips):
            copy(1 + j, (*chip, c), me).wait_recv()
            passed[j].start()
        copy(0, sibling, me).wait_recv()
        for j, chip in enumerate(chips):
            copy(4 + j, (*chip, 1 - c), me).wait_recv()
        for cp in first + passed:
            cp.wait_send()
        mine.wait()

    return pl.pallas_call(
        body, name=name,
        out_shape=jax.ShapeDtypeStruct((N_DEV * m_per, n), x_shard.dtype),
        in_specs=[pl.BlockSpec(memory_space=space)],
        out_specs=pl.BlockSpec(memory_space=space),
        scratch_shapes=[pltpu.SemaphoreType.DMA((7,)), pltpu.SemaphoreType.DMA((7,)), pltpu.SemaphoreType.DMA],
        compiler_params=pltpu.CompilerParams(vmem_limit_bytes=VMEM_LIMIT),
    )(x_shard)


def _hbm_specs(n):
    return [pl.BlockSpec(memory_space=pl.ANY)] * n


class _GatherComm:
    def __init__(self, shards):
        self.n = n = len(shards)
        self.inputs = [s.reshape(2, s.shape[0] // 2, s.shape[1]) for s in shards]
        self.out_shape = [jax.ShapeDtypeStruct((N_DEV,) + s.shape[1:], s.dtype) for s in self.inputs]
        self.scratch = [pltpu.SemaphoreType.DMA((7 * n,)), pltpu.SemaphoreType.DMA((7 * n,))]

    def _parts(self, xs, outs, sems):
        send_sems, recv_sems = sems
        x, y, c = _my_place()

        def blk(k, px, py, pc):
            return outs[k].at[4 * px + 2 * py + pc]

        def copy(k, kind, block, to, own=False):
            return pltpu.make_async_remote_copy(
                src_ref=xs[k].at[c] if own else blk(k, *block), dst_ref=blk(k, *block),
                send_sem=send_sems.at[7 * k + kind], recv_sem=recv_sems.at[7 * k + kind],
                device_id=to, device_id_type=MESH)

        def whole(k):
            return pltpu.make_async_remote_copy(
                src_ref=xs[k], dst_ref=outs[k].at[pl.ds(4 * x + 2 * y, 2)],
                send_sem=send_sems.at[7 * k], recv_sem=recv_sems.at[7 * k],
                device_id=(x, y, 1 - c), device_id_type=MESH)

        me, sibling = (x, y, c), (x, y, 1 - c)
        chips = _other_chips(x, y)
        first = []
        for k in range(self.n):
            first.append(whole(k))
            first += [copy(k, 1 + j, me, (*chip, c), own=True) for j, chip in enumerate(chips)]
        return copy, whole, me, sibling, chips, c, first

    def start(self, xs, outs, sems):
        for cp in self._parts(xs, outs, sems)[-1]:
            cp.start()

    def finish(self, xs, outs, sems):
        copy, whole, me, sibling, chips, c, first = self._parts(xs, outs, sems)
        passed = []
        for j, chip in enumerate(chips):
            for k in range(self.n):
                copy(k, 1 + j, (*chip, c), me).wait_recv()
                fwd = copy(k, 4 + j, (*chip, c), sibling)
                fwd.start()
                passed.append(fwd)
        for k in range(self.n):
            whole(k).wait_recv()
        for j, chip in enumerate(chips):
            for k in range(self.n):
                copy(k, 4 + j, (*chip, 1 - c), me).wait_recv()
        for cp in first + passed:
            cp.wait_send()


class _AllGatherComm:
    def __init__(self, x):
        self.inputs = [x]
        self.n = 1
        self.out_shape = [jax.ShapeDtypeStruct((N_DEV,) + x.shape, x.dtype)]
        self.scratch = [pltpu.SemaphoreType.DMA((7,)), pltpu.SemaphoreType.DMA((7,)), pltpu.SemaphoreType.DMA]

    def _parts(self, xs, outs, sems):
        send_sems, recv_sems, local_sem = sems
        x_ref, out_ref = xs[0], outs[0]
        x, y, c = _my_place()

        def copy(k, block, to, own=False):
            blk = out_ref.at[4 * block[0] + 2 * block[1] + block[2]]
            return pltpu.make_async_remote_copy(
                src_ref=x_ref if own else blk, dst_ref=blk, send_sem=send_sems.at[k], recv_sem=recv_sems.at[k],
                device_id=to, device_id_type=MESH)

        me, sibling = (x, y, c), (x, y, 1 - c)
        chips = _other_chips(x, y)
        local = pltpu.make_async_copy(x_ref, out_ref.at[4 * x + 2 * y + c], local_sem)
        first = [copy(0, me, sibling, own=True)]
        first += [copy(1 + j, me, (*chip, c), own=True) for j, chip in enumerate(chips)]
        return copy, me, sibling, chips, c, local, first

    def start(self, xs, outs, sems):
        _, _, _, _, _, local, first = self._parts(xs, outs, sems)
        for cp in [local] + first:
            cp.start()

    def finish(self, xs, outs, sems):
        copy, me, sibling, chips, c, local, first = self._parts(xs, outs, sems)
        passed = []
        for j, chip in enumerate(chips):
            copy(1 + j, (*chip, c), me).wait_recv()
            fwd = copy(4 + j, (*chip, c), sibling)
            fwd.start()
            passed.append(fwd)
        copy(0, sibling, me).wait_recv()
        for j, chip in enumerate(chips):
            copy(4 + j, (*chip, 1 - c), me).wait_recv()
        for cp in first + passed:
            cp.wait_send()
        local.wait()


class _BothComm:
    def __init__(self, a, b):
        self.a, self.b = a, b
        self.inputs = a.inputs + b.inputs
        self.n = a.n + b.n
        self.out_shape = a.out_shape + b.out_shape
        self.scratch = a.scratch + b.scratch

    def _split(self, xs, outs, sems):
        na, ns = self.a.n, len(self.a.scratch)
        return (xs[:na], outs[:na], sems[:ns]), (xs[na:], outs[na:], sems[ns:])

    def start(self, xs, outs, sems):
        pa, pb = self._split(xs, outs, sems)
        self.a.start(*pa)
        self.b.start(*pb)

    def finish(self, xs, outs, sems):
        pa, pb = self._split(xs, outs, sems)
        self.a.finish(*pa)
        self.b.finish(*pb)


class _ToChipsComm:
    def __init__(self, a4s):
        self.inputs = list(a4s)
        self.n = n = len(a4s)
        nc = N_CHIP - 1
        self.out_shape = [jax.ShapeDtypeStruct((nc,) + a.shape[1:], a.dtype) for a in a4s]
        self.scratch = [pltpu.SemaphoreType.DMA((nc * n,)), pltpu.SemaphoreType.DMA((nc * n,))]

    def _copies(self, as_, rs, sems):
        send_sems, recv_sems = sems
        x, y, c = _my_place()
        nc = N_CHIP - 1
        return [pltpu.make_async_remote_copy(
            src_ref=as_[k].at[2 * cx + cy], dst_ref=rs[k].at[j], send_sem=send_sems.at[nc * k + j],
            recv_sem=recv_sems.at[nc * k + j], device_id=(cx, cy, c), device_id_type=MESH)
            for k in range(self.n) for j, (cx, cy) in enumerate(_other_chips(x, y))]

    def start(self, as_, rs, sems):
        for cp in self._copies(as_, rs, sems):
            cp.start()

    def finish(self, as_, rs, sems):
        for cp in self._copies(as_, rs, sems):
            cp.wait()


def _run_comm(comm, name):
    n = comm.n

    def body(*refs):
        ins, outs, sems = refs[:n], refs[n:2 * n], refs[2 * n:]
        comm.start(ins, outs, sems)
        comm.finish(ins, outs, sems)

    return pl.pallas_call(
        body, name=name, out_shape=comm.out_shape, in_specs=_hbm_specs(n), out_specs=_hbm_specs(n),
        scratch_shapes=comm.scratch,
    )(*comm.inputs)


class _ToSiblingComm:
    def __init__(self, g8s):
        self.inputs = list(g8s)
        self.n = n = len(g8s)
        self.out_shape = [jax.ShapeDtypeStruct((N_CHIP,) + g.shape[1:], g.dtype) for g in g8s]
        self.scratch = [pltpu.SemaphoreType.DMA((N_CHIP * n,)), pltpu.SemaphoreType.DMA((N_CHIP * n,))]

    def _copies(self, gs, rs, sems):
        send_sems, recv_sems = sems
        x, y, c = _my_place()
        return [pltpu.make_async_remote_copy(
            src_ref=gs[k].at[2 * s + 1 - c], dst_ref=rs[k].at[s], send_sem=send_sems.at[N_CHIP * k + s],
            recv_sem=recv_sems.at[N_CHIP * k + s], device_id=(x, y, 1 - c), device_id_type=MESH)
            for k in range(self.n) for s in range(N_CHIP)]

    def start(self, gs, rs, sems):
        for cp in self._copies(gs, rs, sems):
            cp.start()

    def finish(self, gs, rs, sems):
        for cp in self._copies(gs, rs, sems):
            cp.wait()


def _swap_halves(hs, name):
    n = len(hs)

    def body(*refs):
        o_refs = refs[n:2 * n]
        send_sems, recv_sems = refs[2 * n:]
        x, y, c = _my_place()

        def remote(k, slot):
            return pltpu.make_async_remote_copy(
                src_ref=o_refs[k].at[slot], dst_ref=o_refs[k].at[slot], send_sem=send_sems.at[k],
                recv_sem=recv_sems.at[k], device_id=(x, y, 1 - c), device_id_type=MESH)

        sends = [remote(k, c) for k in range(n)]
        for cp in sends:
            cp.start()
        for k in range(n):
            remote(k, 1 - c).wait_recv()
        for cp in sends:
            cp.wait_send()

    return pl.pallas_call(
        body, name=name,
        out_shape=[jax.ShapeDtypeStruct(h.shape, h.dtype) for h in hs],
        in_specs=_hbm_specs(n), out_specs=_hbm_specs(n),
        input_output_aliases={k: k for k in range(n)},
        scratch_shapes=[pltpu.SemaphoreType.DMA((n,)), pltpu.SemaphoreType.DMA((n,))],
    )(*hs)


ADD_TILES = 2


def _add_blocks(a_list, a_idx_fn, others_list, ns, sel, name, out_blocks=None, out_idx_fn=None,
                bf16_copy=False):
    out_blocks = out_blocks or ns
    out_idx_fn = out_idx_fn or (lambda s, sel_ref: s)
    n = len(a_list)
    n_o = len(others_list[0])
    per = 1 + n_o

    def body(sel_ref, *refs):
        for k in range(n):
            ins = refs[k * per:(k + 1) * per]
            acc = ins[0][0]
            for r in ins[1:]:
                acc = acc + r[0].astype(F32)
            refs[n * per + k][0] = acc
            if bf16_copy:
                refs[n * per + n + k][0] = acc.astype(BF16)

    in_specs, args, out_specs, out_shape = [], [], [], []
    for a, others in zip(a_list, others_list):
        _, R, N = a.shape
        tr = R // ADD_TILES
        assert tr % 8 == 0, a.shape
        in_specs.append(pl.BlockSpec((1, tr, N), lambda s, i, sel_ref: (a_idx_fn(s, sel_ref), i, 0)))
        args.append(a)
        for arr, fixed in others:
            if fixed is None:
                in_specs.append(pl.BlockSpec((1, tr, N), lambda s, i, sel_ref: (s, i, 0)))
            else:
                in_specs.append(pl.BlockSpec((1, tr, N), lambda s, i, sel_ref, fixed=fixed: (fixed, i, 0)))
            args.append(arr)
        out_specs.append(pl.BlockSpec((1, tr, N), lambda s, i, sel_ref: (out_idx_fn(s, sel_ref), i, 0)))
        out_shape.append(jax.ShapeDtypeStruct((out_blocks, R, N), a.dtype))
    if bf16_copy:
        out_specs = out_specs + out_specs
        out_shape = out_shape + [jax.ShapeDtypeStruct(o.shape, BF16) for o in out_shape]
    grid_spec = pltpu.PrefetchScalarGridSpec(num_scalar_prefetch=1, grid=(ns, ADD_TILES), in_specs=in_specs,
                                             out_specs=out_specs)
    return pl.pallas_call(
        body, name=name, out_shape=out_shape, grid_spec=grid_spec,
        compiler_params=_cparams(("parallel", "parallel")),
    )(sel, *args)


def _rs_first(g8s, tag, r1=None):
    c_sel = jnp.reshape(lax.axis_index("c"), (1,)).astype(jnp.int32)
    if r1 is None:
        r1 = _run_comm(_ToSiblingComm(g8s), f"rs_to_sibling_{tag}")
    res = _add_blocks(g8s, lambda s, sel: 2 * s + sel[0], [[(r, None)] for r in r1], N_CHIP, c_sel,
                      f"rs_add_sibling_{tag}", bf16_copy=True)
    return list(res[:len(g8s)]), list(res[len(g8s):])


def _rs_last(a4s, r2s, tag):
    sel = jnp.stack([2 * lax.axis_index("x") + lax.axis_index("y"), lax.axis_index("c")]).astype(jnp.int32)
    h = _add_blocks(a4s, lambda s, sel: sel[0], [[(r, 0), (r, 1), (r, 2)] for r in r2s], 1, sel,
                    f"rs_add_chips_{tag}", out_blocks=2, out_idx_fn=lambda s, sel: sel[1])
    full = _swap_halves(h, f"rs_swap_halves_{tag}")
    return [f.reshape(2 * f.shape[1], f.shape[2]) for f in full]


def _ada_fwd(c_all, w_ada, b_ada, name):
    nb, D = c_all.shape
    ncol = w_ada.shape[1]
    tc = 512

    def body(c_ref, w_ref, b_ref, o_ref):
        cv = c_ref[...]
        cond = (cv * jax.nn.sigmoid(cv)).astype(BF16)
        o_ref[...] = jnp.dot(cond, w_ref[...].astype(BF16), preferred_element_type=F32) + b_ref[...]

    return pl.pallas_call(
        body, name=name, out_shape=jax.ShapeDtypeStruct((nb, ncol), F32), grid=(ncol // tc,),
        in_specs=[pl.BlockSpec((nb, D), lambda j: (0, 0)), pl.BlockSpec((D, tc), lambda j: (0, j)),
                  pl.BlockSpec((1, tc), lambda j: (0, j))],
        out_specs=pl.BlockSpec((nb, tc), lambda j: (0, j)),
        compiler_params=_cparams(("parallel",)),
    )(c_all, w_ada, b_ada)


def _ada_bwd(c_all, gmod_cols, name):
    nb, D = c_all.shape
    ncol = gmod_cols.shape[1]
    tc = 512

    def body(c_ref, g_ref, o_ref):
        cv = c_ref[...]
        cond = (cv * jax.nn.sigmoid(cv)).astype(BF16)
        o_ref[...] = _dot_tn(cond, g_ref[...].astype(BF16))

    return pl.pallas_call(
        body, name=name, out_shape=jax.ShapeDtypeStruct((D, ncol), F32), grid=(ncol // tc,),
        in_specs=[pl.BlockSpec((nb, D), lambda j: (0, 0)), pl.BlockSpec((nb, tc), lambda j: (0, j))],
        out_specs=pl.BlockSpec((D, tc), lambda j: (0, j)),
        compiler_params=_cparams(("parallel",)),
    )(c_all, gmod_cols)


def _adam_math(w, g, m, v):
    m = ADAM_B1 * m + (1.0 - ADAM_B1) * g
    v = ADAM_B2 * v + (1.0 - ADAM_B2) * (g * g)
    m_hat = m / (1.0 - ADAM_B1 ** ADAM_STEP)
    v_hat = v / (1.0 - ADAM_B2 ** ADAM_STEP)
    delta = -ADAM_LR * (m_hat / (jnp.sqrt(v_hat) + ADAM_EPS) + ADAM_WD * w)
    return delta, m, v


ADAMW_TILES = 8


def _adamw(params, name):
    n = len(params)

    def body(*refs):
        for k in range(n):
            w_ref, g_ref, m_ref, v_ref = refs[4 * k:4 * k + 4]
            d, mn, vn = _adam_math(w_ref[...], g_ref[...], m_ref[...], v_ref[...])
            for o_ref, val in zip(refs[4 * n + 3 * k:4 * n + 3 * k + 3], (d, mn, vn)):
                o_ref[...] = val

    in_specs, out_specs, out_shape, args = [], [], [], []
    for p in params:
        rows, cols = p[0].shape
        assert rows % (8 * ADAMW_TILES) == 0, p[0].shape
        spec = pl.BlockSpec((rows // ADAMW_TILES, cols), lambda i: (i, 0))
        in_specs += [spec] * 4
        out_specs += [spec] * 3
        out_shape += [jax.ShapeDtypeStruct((rows, cols), F32)] * 3
        args += list(p)
    res = pl.pallas_call(
        body, name=name, out_shape=out_shape, grid=(ADAMW_TILES,), in_specs=in_specs, out_specs=out_specs,
        compiler_params=_cparams(("parallel",)),
    )(*args)
    return [tuple(res[3 * k:3 * k + 3]) for k in range(n)]


VEC_ROWS = 8


def _adamw_rows(w, parts, m, v, name):
    n = w.shape[1]
    P = parts.shape[0]
    assert n % (VEC_ROWS * LANES) == 0, n
    shp = (VEC_ROWS, n // VEC_ROWS)

    def body(w_ref, p_ref, m_ref, v_ref, g_ref, d_ref, mo_ref, vo_ref):
        g = p_ref[0]
        for k in range(1, P):
            g = g + p_ref[k]
        d, mn, vn = _adam_math(w_ref[...], g, m_ref[...], v_ref[...])
        g_ref[...] = g
        d_ref[...] = d
        mo_ref[...] = mn
        vo_ref[...] = vn

    vec = pl.BlockSpec(shp, lambda i: (0, 0))
    out = pl.pallas_call(
        body, name=name, out_shape=[jax.ShapeDtypeStruct(shp, F32)] * 4, grid=(1,),
        in_specs=[vec, pl.BlockSpec((P,) + shp, lambda i: (0, 0, 0)), vec, vec], out_specs=[vec] * 4,
        compiler_params=_cparams(("arbitrary",)),
    )(w.reshape(shp), parts.reshape((P,) + shp), m.reshape(shp), v.reshape(shp))
    return [o.reshape(1, n) for o in out]


_SHARDED = ("w_in", "w_uq", "w_ukv", "w_out", "w_ffn_in", "w_ffn_out")
_SMALL = (("g_norm1", 1024), ("g_cq", 384), ("g_ckv", 256), ("rel_bias", 256), ("g_out_a", 512),
          ("g_out_b", 512), ("g_norm2", 1024), ("g_final", 1024))
_SMALL_PAD = 5120


def _full_from_shards(sh):
    return jnp.transpose(sh, (1, 0, 2)).reshape(sh.shape[1], -1)


def _shards_from_full(full):
    rows, cols = full.shape
    return jnp.transpose(full.reshape(rows, N_CHIP, cols // N_CHIP), (1, 0, 2))


def kernel(x, c, w_ada, b_ada, g_norm1, w_in, g_cq, w_uq, g_ckv, w_ukv, rel_bias, g_out_a, g_out_b, w_out, g_norm2, w_ffn_in, w_ffn_out, g_final, loss_target, m_w_ada, m_b_ada, m_g_norm1, m_w_in, m_g_cq, m_w_uq, m_g_ckv, m_w_ukv, m_rel_bias, m_g_out_a, m_g_out_b, m_w_out, m_g_norm2, m_w_ffn_in, m_w_ffn_out, m_g_final, v_w_ada, v_b_ada, v_g_norm1, v_w_in, v_g_cq, v_w_uq, v_g_ckv, v_w_ukv, v_rel_bias, v_g_out_a, v_g_out_b, v_w_out, v_g_norm2, v_w_ffn_in, v_w_ffn_out, v_g_final):
    names = ["w_ada", "b_ada", "g_norm1", "w_in", "g_cq", "w_uq", "g_ckv", "w_ukv", "rel_bias", "g_out_a",
             "g_out_b", "w_out", "g_norm2", "w_ffn_in", "w_ffn_out", "g_final"]
    W = dict(zip(names, [w_ada, b_ada, g_norm1, w_in, g_cq, w_uq, g_ckv, w_ukv, rel_bias, g_out_a, g_out_b,
                         w_out, g_norm2, w_ffn_in, w_ffn_out, g_final]))
    M = dict(zip(names, [m_w_ada, m_b_ada, m_g_norm1, m_w_in, m_g_cq, m_w_uq, m_g_ckv, m_w_ukv, m_rel_bias,
                         m_g_out_a, m_g_out_b, m_w_out, m_g_norm2, m_w_ffn_in, m_w_ffn_out, m_g_final]))
    V = dict(zip(names, [v_w_ada, v_b_ada, v_g_norm1, v_w_in, v_g_cq, v_w_uq, v_g_ckv, v_w_ukv, v_rel_bias,
                         v_g_out_a, v_g_out_b, v_w_out, v_g_norm2, v_w_ffn_in, v_w_ffn_out, v_g_final]))
    B, S, D = x.shape
    mx, my, mc = _my_place()
    dev = 4 * mx + 2 * my + mc
    chip = 2 * mx + my
    pad_rows = 8

    early = ("w_in", "w_uq", "w_ukv")
    bias, got = _bias_tables(rel_bias, _bucket_tables(), "rel_bias_tables",
                             _BothComm(_AllGatherComm(jnp.pad(c, ((0, pad_rows - B), (0, 0)))),
                                       _GatherComm([W[n][0].astype(BF16) for n in early])))
    c_all = got[0][:, :B].reshape(N_DEV * B, D)

    ada_cols = w_ada.shape[-1]
    b_cols = lax.dynamic_slice_in_dim(b_ada, chip * ada_cols, ada_cols, axis=1)
    mod_cols = _ada_fwd(c_all, w_ada[0], b_cols, "ada_fwd")
    mod_all = _allgather8(mod_cols, "ag_mod", False).reshape(N_DEV, N_DEV * B, ada_cols)[0::2]
    mod_all = jnp.transpose(mod_all, (1, 0, 2)).reshape(N_DEV * B, N_MOD * D)
    mod = lax.dynamic_slice_in_dim(mod_all, dev * B, B, axis=0)

    full = {n: g.reshape((N_CHIP,) + W[n].shape[1:]) for n, g in zip(early, got[1:])}
    wts = dict(w_in=_w_in_to_kernel(_full_from_shards(full["w_in"])),
               w_uq=_w_uq_to_kernel(_full_from_shards(full["w_uq"])),
               w_kv=_w_ukv_to_kernel(_full_from_shards(full["w_ukv"])))
    gains = dict(g_norm1=g_norm1, g_cq=g_cq, g_ckv=g_ckv, g_out_a=g_out_a, g_out_b=g_out_b, g_norm2=g_norm2,
                 g_final=g_final.reshape(1, D))

    loss, grad_x, gmod, grads = _local_step(x, loss_target, mod, wts, gains, rel_bias,
                                            ffn_shards=[w_ffn_in[0].astype(BF16), w_ffn_out[0].astype(BF16),
                                                        w_out[0].astype(BF16)], bias=bias)
    loss = lax.psum(loss[0, 0], ("x", "y", "c"))

    n_small = _SMALL_PAD
    cat = lambda dct: jnp.concatenate([dct[n].reshape(1, -1) for n, _ in _SMALL]
                                      + [jnp.zeros((1, _SMALL_PAD - sum(s for _, s in _SMALL)), F32)], axis=1)
    small = cat(grads)
    rows = jnp.concatenate([gmod, jnp.pad(small, ((0, 0), (0, N_MOD * D - n_small))),
                            jnp.zeros((pad_rows - B - 1, N_MOD * D), F32)], axis=0)
    rows_all = _allgather8(rows, "ag_small", False).reshape(N_DEV, pad_rows, N_MOD * D)
    gmod_all = rows_all[:, :B].reshape(N_DEV * B, N_MOD * D)
    small_parts = rows_all[:, B, :n_small]

    a4, r2 = grads["mix_pending"]
    ffn_a4, ffn_r2 = grads["ffn_pending"]
    G = dict(zip(_SHARDED, _rs_last(list(a4) + list(ffn_a4), list(r2) + list(ffn_r2), "all")))

    gmod_cols = lax.dynamic_slice_in_dim(gmod_all, chip * ada_cols, ada_cols, axis=1)
    G["w_ada"] = _ada_bwd(c_all, gmod_cols, "ada_bwd")
    delta, new_m, new_v = {}, {}, {}
    big = ("w_ada",) + _SHARDED
    two_d = lambda a: a.reshape(a.shape[-2], a.shape[-1])
    updates = _adamw([(two_d(W[n]), G[n], two_d(M[n]), two_d(V[n])) for n in big], "adamw_sharded")
    for n, (d_, m_, v_) in zip(big, updates):
        G[n], delta[n], new_m[n], new_v[n] = [a.reshape(W[n].shape) for a in (G[n], d_, m_, v_)]
    gs, ds_, ms_, vs_ = _adamw_rows(cat(W), small_parts, cat(M), cat(V), "adamw_small")
    off = 0
    for n, sz in _SMALL:
        shp = W[n].shape
        G[n], delta[n], new_m[n], new_v[n] = [a[:, off:off + sz].reshape(shp) for a in (gs, ds_, ms_, vs_)]
        off += sz
    G["b_ada"], delta["b_ada"], new_m["b_ada"], new_v["b_ada"] = _adamw_rows(b_ada, gmod_all, m_b_ada, v_b_ada,
                                                                          "adamw_b_ada")
    return (loss, grad_x, *[G[n] for n in names], *[delta[n] for n in names], *[new_m[n] for n in names],
            *[new_v[n] for n in names])
```

```python
import functools
import math

import numpy as np
import jax
import jax.numpy as jnp
from jax import lax
from jax.experimental import pallas as pl
from jax.experimental.pallas import tpu as pltpu

F32 = jnp.float32
BF16 = jnp.bfloat16

D_MODEL = 1024
SEQ = 2048
N_HEADS = 8
HEAD_DIM = 64
D_A = 512
D_B = 512
Q_LORA = 384
KV_LORA = 256
ROPE_DIM = 32
NOPE_DIM = 64
D_FF = 2816
N_MOD = 6
N_BUCKETS = 32
MAX_DISTANCE = 2048
ROPE_THETA = 10000.0
EPS = 1e-6
NEG = -1e30
BLK = 128
DILATIONS = (1, 4, 16)
SPAN = 128
MLA_SCALE = (NOPE_DIM + ROPE_DIM) ** -0.5
DIL_SCALE = HEAD_DIM ** -0.5

ADAM_LR = 0.001
ADAM_B1 = 0.9
ADAM_B2 = 0.999
ADAM_EPS = 1e-08
ADAM_WD = 0.01
ADAM_STEP = 10

N_DEV = 8
N_CHIP = 4
LANES = 128
VMEM_LIMIT = 48 * 1024 * 1024
MM_VMEM_BUDGET = 32 * 1024 * 1024

P_QKV = 3 * D_A
P_REST = KV_LORA + LANES + Q_LORA


def _cparams(sem=None):
    return pltpu.CompilerParams(dimension_semantics=sem, vmem_limit_bytes=VMEM_LIMIT)


def _pick(n, cands):
    for c in cands:
        if n % c == 0:
            return c
    raise ValueError(f"no tile for {n} in {cands}")


def _mm(a, b, mode, out_dtype, name, col_blocks=None, comm=None, halves=False):
    blocked = col_blocks is not None
    if mode == "nn":
        (M, K) = a.shape
        K2, N = (b.shape[1], b.shape[0] * b.shape[2]) if blocked else b.shape
    elif mode == "nt":
        (M, K) = (a.shape[1], 2 * a.shape[2]) if halves else a.shape
        N, K2 = (b.shape[1], b.shape[0] * b.shape[2]) if blocked else b.shape
    else:
        (K, M) = a.shape
        K2, N = (b.shape[1], 2 * b.shape[2]) if halves else b.shape
    assert K == K2, (a.shape, b.shape, mode)
    assert not halves or (blocked and col_blocks == 4 and mode in ("nt", "tn"))
    tn = _pick(N, (1408, 1024, 768, 512, 384, 256, 128))
    tk = _pick(K, (1408, 1152, 1024, 768, 512, 384, 256, 128))
    if blocked and mode == "nt":
        tk = K // col_blocks
    elif blocked:
        tn = N // col_blocks
    nk = K // tk

    def vmem_bytes(tm_):
        tiles = tm_ * tk * a.dtype.itemsize + tk * tn * b.dtype.itemsize + tm_ * tn * jnp.dtype(out_dtype).itemsize
        return 2 * tiles + tm_ * tn * 4

    tm = next(t for t in (1408, 1024, 512, 384, 256, 128) if M % t == 0 and vmem_bytes(t) <= MM_VMEM_BUDGET)
    out_shape = (M, N)
    out_spec = pl.BlockSpec((tm, tn), lambda i, j, k: (i, j))
    if mode == "nn":
        a_spec = pl.BlockSpec((tm, tk), lambda i, j, k: (i, k))
        b_spec = (pl.BlockSpec((None, tk, tn), lambda i, j, k: (j, k, 0)) if blocked
                  else pl.BlockSpec((tk, tn), lambda i, j, k: (k, j)))
        dn = (((1,), (0,)), ((), ()))
    elif mode == "nt":
        a_spec = (pl.BlockSpec((None, tm, tk), lambda i, j, k: (k // 2, i, k % 2)) if halves
                  else pl.BlockSpec((tm, tk), lambda i, j, k: (i, k)))
        b_spec = (pl.BlockSpec((None, tn, tk), lambda i, j, k: (k, j, 0)) if blocked
                  else pl.BlockSpec((tn, tk), lambda i, j, k: (j, k)))
        dn = (((1,), (1,)), ((), ()))
    else:
        a_spec = pl.BlockSpec((tk, tm), lambda i, j, k: (k, i))
        b_spec = (pl.BlockSpec((None, tk, tn), lambda i, j, k: (j // 2, k, j % 2)) if halves
                  else pl.BlockSpec((tk, tn), lambda i, j, k: (k, j)))
        dn = (((0,), (0,)), ((), ()))
        if blocked:
            out_shape = (col_blocks, M, tn)
            out_spec = pl.BlockSpec((None, tm, tn), lambda i, j, k: (j, i, 0))

    def body(a_ref, b_ref, o_ref, acc_ref):
        k = pl.program_id(2)

        @pl.when(k == 0)
        def _():
            acc_ref[...] = jnp.zeros_like(acc_ref)

        acc_ref[...] += lax.dot_general(a_ref[...].astype(BF16), b_ref[...].astype(BF16), dn,
                                        preferred_element_type=F32)

        @pl.when(k == nk - 1)
        def _():
            o_ref[...] = acc_ref[...].astype(o_ref.dtype)

    if comm is not None:
        (out,), got = _host_call(
            body, comm, name=name, out_shape=[jax.ShapeDtypeStruct(out_shape, out_dtype)],
            grid=(M // tm, N // tn, nk), in_specs=[a_spec, b_spec], out_specs=[out_spec],
            scratch_shapes=[pltpu.VMEM((tm, tn), F32)], args=(a, b))
        return out, got
    return pl.pallas_call(
        body, name=name,
        out_shape=jax.ShapeDtypeStruct(out_shape, out_dtype),
        grid=(M // tm, N // tn, nk),
        in_specs=[a_spec, b_spec],
        out_specs=out_spec,
        scratch_shapes=[pltpu.VMEM((tm, tn), F32)],
        compiler_params=_cparams(("parallel", "parallel", "arbitrary")),
    )(a, b)


ROW_TILE = 512


def _adaln_bwd(dh, x, g, sc, dres, name, mix=None, gate=None, comm=None):
    B, S, D = x.shape
    ts = ROW_TILE
    has_res = mix is not None

    def body(*refs):
        if has_res:
            (dh_ref, x_ref, g_ref, sc_ref, dres_ref, mix_ref, gate_ref,
             dx_ref, dsh_ref, dsc_ref, dg_ref, dgate_ref, dmix_ref) = refs
        else:
            (dh_ref, x_ref, g_ref, sc_ref, dres_ref, dx_ref, dsh_ref, dsc_ref, dg_ref) = refs
        b, s = pl.program_id(0), pl.program_id(1)
        xv = x_ref[0]
        dhv = dh_ref[0]
        gv = g_ref[...]
        r = lax.rsqrt(jnp.mean(xv * xv, axis=-1, keepdims=True) + EPS)
        n = xv * r
        xn = n * gv
        dxn = dhv * (1.0 + sc_ref[0])
        dn = dxn * gv
        dx = r * (dn - n * jnp.mean(dn * n, axis=-1, keepdims=True)) + dres_ref[0]
        dx_ref[0] = dx

        @pl.when(s == 0)
        def _():
            dsh_ref[...] = jnp.zeros_like(dsh_ref)
            dsc_ref[...] = jnp.zeros_like(dsc_ref)
            if has_res:
                dgate_ref[...] = jnp.zeros_like(dgate_ref)

        @pl.when((s == 0) & (b == 0))
        def _():
            dg_ref[...] = jnp.zeros_like(dg_ref)

        dsh_ref[0] += jnp.sum(dhv, axis=0, keepdims=True)
        dsc_ref[0] += jnp.sum(dhv * xn, axis=0, keepdims=True)
        dg_ref[...] += jnp.sum(dxn * n, axis=0, keepdims=True)
        if has_res:
            dgate_ref[0] += jnp.sum(dx * mix_ref[0], axis=0, keepdims=True)
            dmix_ref[0] = (dx * gate_ref[0]).astype(dmix_ref.dtype)

    tok = pl.BlockSpec((1, ts, D), lambda b, s: (b, s, 0))
    per_b = pl.BlockSpec((1, 1, D), lambda b, s: (b, 0, 0))
    vec = pl.BlockSpec((1, D), lambda b, s: (0, 0))
    in_specs = [tok, tok, vec, per_b, tok]
    args = [dh, x, g, sc, dres]
    out_shape = [jax.ShapeDtypeStruct((B, S, D), F32), jax.ShapeDtypeStruct((B, 1, D), F32),
                 jax.ShapeDtypeStruct((B, 1, D), F32), jax.ShapeDtypeStruct((1, D), F32)]
    out_specs = [tok, per_b, per_b, vec]
    if has_res:
        in_specs += [tok, per_b]
        args += [mix, gate]
        out_shape += [jax.ShapeDtypeStruct((B, 1, D), F32), jax.ShapeDtypeStruct((B, S, D), BF16)]
        out_specs += [per_b, tok]
    res, got = _host_call(body, comm, name=name, out_shape=out_shape, grid=(B, S // ts), in_specs=in_specs,
                          out_specs=out_specs, scratch_shapes=[], args=args)
    return (list(res) + [got]) if comm is not None else res


def _rms_parts(xv):
    r = lax.rsqrt(jnp.mean(xv * xv, axis=-1, keepdims=True) + EPS)
    return r, xv * r


def _mixer_out(out_a, out_b, g_a, g_b, w_out, x, gate, g_norm, sc, sh, name):
    B, S, D = x.shape
    na, nb = out_a.shape[-1], out_b.shape[-1]
    ts = ROW_TILE
    tiles = S // ts

    def body(a_ref, b_ref, ga_ref, gb_ref, w_ref, x_ref, gate_ref, gn_ref, sc_ref, sh_ref,
             y_ref, mix_ref, x1_ref, h_ref):
        y = jnp.concatenate([(_rms_parts(a_ref[0])[1] * ga_ref[...]).astype(BF16),
                             (_rms_parts(b_ref[0])[1] * gb_ref[...]).astype(BF16)], axis=1)
        y_ref[...] = y
        mix = jnp.dot(y, w_ref[...], preferred_element_type=F32)
        mix_ref[0] = mix
        x1 = x_ref[0] + gate_ref[0] * mix
        x1_ref[0] = x1
        h_ref[0] = ((_rms_parts(x1)[1] * gn_ref[...]) * (1.0 + sc_ref[0]) + sh_ref[0]).astype(h_ref.dtype)

    tok = lambda n: pl.BlockSpec((1, ts, n), lambda b, t: (b, t, 0))
    per_b = pl.BlockSpec((1, 1, D), lambda b, t: (b, 0, 0))
    full = lambda a: pl.BlockSpec(a.shape, lambda b, t: (0, 0))
    return pl.pallas_call(
        body, name=name,
        out_shape=[jax.ShapeDtypeStruct((B * S, na + nb), BF16), jax.ShapeDtypeStruct((B, S, D), F32),
                   jax.ShapeDtypeStruct((B, S, D), F32), jax.ShapeDtypeStruct((B, S, D), BF16)],
        grid=(B, tiles),
        in_specs=[tok(na), tok(nb), full(g_a), full(g_b), full(w_out), tok(D), per_b, full(g_norm), per_b, per_b],
        out_specs=[pl.BlockSpec((ts, na + nb), lambda b, t: (b * tiles + t, 0)), tok(D), tok(D), tok(D)],
        compiler_params=_cparams(("parallel", "parallel")),
    )(out_a, out_b, g_a, g_b, w_out, x, gate, g_norm, sc, sh)


def _mixer_out_bwd(dmix, w_out, out_a, out_b, g_a, g_b, name):
    B, S, D = dmix.shape
    na, nb = out_a.shape[-1], out_b.shape[-1]
    tq = MLA_TQ
    sub = VIEW_TILE // tq
    npair = N_HEADS // 2

    def rms_bwd(dyv, xv, g_ref, dg_ref):
        r, nrm = _rms_parts(xv)
        dn = dyv * g_ref[...]
        dg_ref[...] += jnp.sum(dyv * nrm, axis=0, keepdims=True)
        return r * (dn - nrm * jnp.mean(dn * nrm, axis=-1, keepdims=True))

    def body(dm_ref, w_ref, a_ref, b_ref, ga_ref, gb_ref, d1_ref, d4_ref, d16_ref, dga_ref, dob_ref, dgb_ref,
             dl_ref, dx_s):
        @pl.when((pl.program_id(0) == 0) & (pl.program_id(1) == 0))
        def _():
            dga_ref[...] = jnp.zeros_like(dga_ref)
            dgb_ref[...] = jnp.zeros_like(dgb_ref)

        dy = _dot_nt(dm_ref[0], w_ref[...])
        do_a = rms_bwd(dy[:, :na], a_ref[0], ga_ref, dga_ref)
        d1_ref[0] = do_a.astype(d1_ref.dtype)
        _put_tile(dx_s, do_a)
        _tile_to_view(dx_s, d4_ref, DILATIONS[1], na)
        _tile_to_view(dx_s, d16_ref, DILATIONS[2], na)
        ov = b_ref[0]
        do_b = rms_bwd(dy[:, na:], ov, gb_ref, dgb_ref).astype(dob_ref.dtype)
        dob_ref[0] = do_b
        prod = do_b.astype(F32) * ov
        dl_ref[...] = jnp.zeros_like(dl_ref)
        for p in range(npair):
            for s in range(sub):
                prod_t = jnp.transpose(prod[s * tq:(s + 1) * tq, p * LANES:(p + 1) * LANES])
                dl_ref[0, p, s, 0:1, :] = jnp.sum(prod_t[:HEAD_DIM], axis=0, keepdims=True)
                dl_ref[0, p, s, 1:2, :] = jnp.sum(prod_t[HEAD_DIM:], axis=0, keepdims=True)

    full = lambda a: pl.BlockSpec(a.shape, lambda b, t: (0, 0))
    vec = lambda n: pl.BlockSpec((1, n), lambda b, t: (0, 0))
    res = pl.pallas_call(
        body, name=name,
        out_shape=[_view_shape(B, S, d, na, BF16) for d in DILATIONS]
        + [jax.ShapeDtypeStruct((1, na), F32), jax.ShapeDtypeStruct((B, S, nb), BF16),
           jax.ShapeDtypeStruct((1, nb), F32), jax.ShapeDtypeStruct((B, npair, S // tq, 8, tq), F32)],
        grid=(B, S // VIEW_TILE),
        in_specs=[_view_spec(1, D), full(w_out), _view_spec(1, na), _view_spec(1, nb), full(g_a), full(g_b)],
        out_specs=[_view_spec(d, na) for d in DILATIONS]
        + [vec(na), _view_spec(1, nb), vec(nb),
           pl.BlockSpec((1, npair, sub, 8, tq), lambda b, t: (b, 0, t, 0, 0))],
        scratch_shapes=[_tile_scratch(na)],
        compiler_params=_cparams(("arbitrary", "arbitrary")),
    )(dmix, w_out, out_a, out_b, g_a, g_b)
    nd = len(DILATIONS)
    return res[:nd], res[nd], res[nd + 1], res[nd + 2], res[nd + 3]


FFN_TILE = 1408


def _ffn_in_fwd(h, w4, name):
    T, D = h.shape
    tm, tc = 512, FFN_TILE
    nc = D_FF // tc

    def body(h_ref, wg_ref, wu_ref, gu_ref, act_ref):
        hv = h_ref[...]
        g = jnp.dot(hv, wg_ref[...], preferred_element_type=F32)
        u = jnp.dot(hv, wu_ref[...], preferred_element_type=F32)
        gu_ref[0] = g.astype(gu_ref.dtype)
        gu_ref[1] = u.astype(gu_ref.dtype)
        act_ref[...] = (g * jax.nn.sigmoid(g) * u).astype(act_ref.dtype)

    return pl.pallas_call(
        body, name=name,
        out_shape=[jax.ShapeDtypeStruct((2, T, D_FF), BF16), jax.ShapeDtypeStruct((T, D_FF), BF16)],
        grid=(nc, T // tm),
        in_specs=[pl.BlockSpec((tm, D), lambda j, i: (i, 0)),
                  pl.BlockSpec((None, D, tc), lambda j, i: (j, 0, 0)),
                  pl.BlockSpec((None, D, tc), lambda j, i: (j + nc, 0, 0))],
        out_specs=[pl.BlockSpec((2, tm, tc), lambda j, i: (0, i, j)), pl.BlockSpec((tm, tc), lambda j, i: (i, j))],
        compiler_params=_cparams(("parallel", "parallel")),
    )(h, w4, w4)


def _ffn_out_bwd(df, w_out, gu, name):
    T, D = df.shape
    tm, tc = 512, FFN_TILE

    def body(df_ref, w_ref, gu_ref, dgu_ref):
        da = _dot_nt(df_ref[...], w_ref[...])
        g, u = gu_ref[0].astype(F32), gu_ref[1].astype(F32)
        sg = jax.nn.sigmoid(g)
        dgu_ref[0] = (da * u * (sg * (1.0 + g * (1.0 - sg)))).astype(dgu_ref.dtype)
        dgu_ref[1] = (da * (g * sg)).astype(dgu_ref.dtype)

    halves = pl.BlockSpec((2, tm, tc), lambda j, i: (0, i, j))
    return pl.pallas_call(
        body, name=name, out_shape=jax.ShapeDtypeStruct((2, T, D_FF), BF16), grid=(D_FF // tc, T // tm),
        in_specs=[pl.BlockSpec((tm, D), lambda j, i: (i, 0)), pl.BlockSpec((tc, D), lambda j, i: (j, 0)), halves],
        out_specs=halves,
        compiler_params=_cparams(("parallel", "parallel")),
    )(df, w_out, gu)


def _final_loss(x1, act, w_out, g2, gf, target, name):
    B, S, D = x1.shape
    ts = ROW_TILE
    tiles = S // ts

    def body(x1_ref, act_ref, w_ref, g2_ref, gf_ref, t_ref, dx_ref, df_ref, dg2_ref, dgf_ref, loss_ref):
        b, s = pl.program_id(0), pl.program_id(1)
        fv = jnp.dot(act_ref[...], w_ref[...], preferred_element_type=F32)
        g2v = g2_ref[0]
        gfv = gf_ref[...]
        x2 = x1_ref[0] + g2v * fv
        r = lax.rsqrt(jnp.mean(x2 * x2, axis=-1, keepdims=True) + EPS)
        n = x2 * r
        e = n * gfv - t_ref[0]
        dy = e * (1.0 / D)
        dn = dy * gfv
        dx = r * (dn - n * jnp.mean(dn * n, axis=-1, keepdims=True))
        dx_ref[0] = dx
        df_ref[0] = (dx * g2v).astype(df_ref.dtype)

        @pl.when(s == 0)
        def _():
            dg2_ref[...] = jnp.zeros_like(dg2_ref)

        @pl.when((s == 0) & (b == 0))
        def _():
            dgf_ref[...] = jnp.zeros_like(dgf_ref)
            loss_ref[...] = jnp.zeros_like(loss_ref)

        dg2_ref[0] += jnp.sum(dx * fv, axis=0, keepdims=True)
        dgf_ref[...] += jnp.sum(dy * n, axis=0, keepdims=True)
        loss_ref[...] += 0.5 * jnp.sum(jnp.mean(e * e, axis=-1, keepdims=True), axis=0, keepdims=True)

    tok = pl.BlockSpec((1, ts, D), lambda b, s: (b, s, 0))
    per_b = pl.BlockSpec((1, 1, D), lambda b, s: (b, 0, 0))
    vec = pl.BlockSpec((1, D), lambda b, s: (0, 0))
    return pl.pallas_call(
        body, name=name,
        out_shape=[jax.ShapeDtypeStruct((B, S, D), F32), jax.ShapeDtypeStruct((B, S, D), BF16),
                   jax.ShapeDtypeStruct((B, 1, D), F32), jax.ShapeDtypeStruct((1, D), F32),
                   jax.ShapeDtypeStruct((1, LANES), F32)],
        grid=(B, tiles),
        in_specs=[tok, pl.BlockSpec((ts, act.shape[1]), lambda b, s: (b * tiles + s, 0)),
                  pl.BlockSpec(w_out.shape, lambda b, s: (0, 0)), per_b, vec, tok],
        out_specs=[tok, tok, per_b, vec, pl.BlockSpec((1, LANES), lambda b, s: (0, 0))],
        compiler_params=_cparams(("arbitrary", "arbitrary")),
    )(x1, act, w_out, g2, gf, target)


def _rope_tables():
    half = ROPE_DIM // 2
    inv = ROPE_THETA ** (-jnp.arange(half, dtype=F32) / half)
    ang = jnp.arange(SEQ, dtype=F32)[:, None] * inv[None, :]
    cos, sin = jnp.cos(ang), jnp.sin(ang)
    one = jnp.ones((SEQ, NOPE_DIM), F32)
    zero = jnp.zeros((SEQ, NOPE_DIM), F32)
    cs = jnp.concatenate([one, cos, cos, one[:, :LANES - NOPE_DIM - ROPE_DIM]], axis=1)
    sn = jnp.concatenate([zero, -sin, sin, zero[:, :LANES - NOPE_DIM - ROPE_DIM]], axis=1)
    return cs, sn


def _rope_group(t, cs, sn):
    half = ROPE_DIM // 2
    lane = lax.broadcasted_iota(jnp.int32, t.shape, 1)
    partner = jnp.where(lane < NOPE_DIM + half, pltpu.roll(t, LANES - half, 1), pltpu.roll(t, half, 1))
    return t * cs + partner * sn


def _mla_proj(cqn, ckvn, rest, w_uq, w_kv, cs, sn, name):
    B, S, _ = rest.shape
    ts, tk = ROW_TILE, MLA_TK
    tiles = S // ts
    G = N_HEADS
    kw = G * LANES
    npair = N_HEADS // 2

    def body(cq_ref, ckv_ref, r_ref, wq_ref, wkv_ref, cs_ref, sn_ref, q_ref, k_ref, v_ref, vt_ref):
        csv, snv = cs_ref[...], sn_ref[...]
        q_raw = jnp.dot(cq_ref[...], wq_ref[...], preferred_element_type=F32)
        kv = jnp.dot(ckv_ref[...], wkv_ref[...], preferred_element_type=F32)
        ra = _rope_group(r_ref[0], csv, snv)
        for gi in range(G):
            sl = slice(gi * LANES, (gi + 1) * LANES)
            q_ref[0, :, sl] = _rope_group(q_raw[:, sl], csv, snv).astype(q_ref.dtype)
            k_ref[0, :, sl] = (kv[:, sl] + ra).astype(k_ref.dtype)
        v = kv[:, kw:]
        v_ref[0] = v.astype(v_ref.dtype)
        for p in range(npair):
            for s in range(ts // tk):
                vt_ref[0, p, s] = jnp.transpose(v[s * tk:(s + 1) * tk, p * LANES:(p + 1) * LANES]).astype(vt_ref.dtype)

    rows = lambda n: pl.BlockSpec((ts, n), lambda b, t: (b * tiles + t, 0))
    full = lambda a: pl.BlockSpec(a.shape, lambda b, t: (0, 0))
    tab = pl.BlockSpec((ts, LANES), lambda b, t: (t, 0))
    tok = lambda n: pl.BlockSpec((1, ts, n), lambda b, t: (b, t, 0))
    return pl.pallas_call(
        body, name=name,
        out_shape=[jax.ShapeDtypeStruct((B, S, kw), BF16), jax.ShapeDtypeStruct((B, S, kw), BF16),
                   jax.ShapeDtypeStruct((B, S, D_B), BF16),
                   jax.ShapeDtypeStruct((B, npair, S // tk, LANES, tk), BF16)],
        grid=(B, tiles),
        in_specs=[rows(Q_LORA), rows(KV_LORA), pl.BlockSpec((1, ts, LANES), lambda b, t: (b, t, KV_LORA // LANES)),
                  full(w_uq), full(w_kv), tab, tab],
        out_specs=[tok(kw), tok(kw), tok(D_B),
                   pl.BlockSpec((1, npair, ts // tk, LANES, tk), lambda b, t: (b, 0, t, 0, 0))],
        compiler_params=_cparams(("parallel", "parallel")),
    )(cqn, ckvn, rest, w_uq, w_kv, cs, sn)


def _mla_proj_bwd(dq_t, dkc, dv, cqn, ckvn, rest, w_uq, w_kv, g_cq, g_ckv, cs, sn_neg, name):
    B, S, _ = rest.shape
    npair, tq = dq_t.shape[1], dq_t.shape[-1]
    ts = ROW_TILE
    tiles = S // ts
    kw = N_HEADS * LANES
    cq_lo = KV_LORA + LANES

    def rms_bwd(dy, xv, g_ref):
        r = lax.rsqrt(jnp.mean(xv * xv, axis=-1, keepdims=True) + EPS)
        nrm = xv * r
        dn = dy * g_ref[...]
        return r * (dn - nrm * jnp.mean(dn * nrm, axis=-1, keepdims=True)), jnp.sum(dy * nrm, axis=0, keepdims=True)

    def body(dqt_ref, dk_ref, dv_ref, cq_ref, ckv_ref, r_ref, wq_ref, wkv_ref, gcq_ref, gckv_ref, cs_ref, sn_ref,
             dr_ref, gwq_ref, gwkv_ref, dgcq_ref, dgckv_ref, dq_s):
        @pl.when((pl.program_id(0) == 0) & (pl.program_id(1) == 0))
        def _():
            for ref in (gwq_ref, gwkv_ref, dgcq_ref, dgckv_ref):
                ref[...] = jnp.zeros_like(ref)

        csv, snv = cs_ref[...], sn_ref[...]
        for p in range(npair):
            for s in range(ts // tq):
                tile = jnp.transpose(dqt_ref[0, p, s])
                rows = slice(s * tq, (s + 1) * tq)
                for hh in range(2):
                    lo = (2 * p + hh) * LANES
                    dq_s[rows, lo:lo + LANES] = _rope_group(tile[:, hh * LANES:(hh + 1) * LANES], csv[rows],
                                                            snv[rows]).astype(dq_s.dtype)
        dq_raw = dq_s[...]
        dkcv = dk_ref[0]
        dkv = jnp.concatenate([dkcv, dv_ref[0]], axis=1).astype(BF16)
        cqn, ckvn = cq_ref[...], ckv_ref[...]
        gwq_ref[...] += _dot_tn(cqn, dq_raw)
        gwkv_ref[...] += _dot_tn(ckvn, dkv)
        restv = r_ref[0]
        dcq, dg = rms_bwd(_dot_nt(dq_raw, wq_ref[...]), restv[:, cq_lo:], gcq_ref)
        dgcq_ref[...] += dg
        dckv, dg = rms_bwd(_dot_nt(dkv, wkv_ref[...]), restv[:, :KV_LORA], gckv_ref)
        dgckv_ref[...] += dg
        acc = dkcv[:, 0:LANES]
        for gi in range(1, N_HEADS):
            acc = acc + dkcv[:, gi * LANES:(gi + 1) * LANES]
        lane = lax.broadcasted_iota(jnp.int32, acc.shape, 1)
        acc = jnp.where((lane >= NOPE_DIM) & (lane < NOPE_DIM + ROPE_DIM), acc, 0.0)
        dr_ref[0, :, 0:KV_LORA] = dckv.astype(dr_ref.dtype)
        dr_ref[0, :, KV_LORA:cq_lo] = _rope_group(acc, csv, snv).astype(dr_ref.dtype)
        dr_ref[0, :, cq_lo:] = dcq.astype(dr_ref.dtype)

    rows = lambda n: pl.BlockSpec((ts, n), lambda b, t: (b * tiles + t, 0))
    full = lambda a: pl.BlockSpec(a.shape, lambda b, t: (0, 0))
    tab = pl.BlockSpec((ts, LANES), lambda b, t: (t, 0))
    tok = lambda n: pl.BlockSpec((1, ts, n), lambda b, t: (b, t, 0))
    acc_out = lambda shp: pl.BlockSpec(shp, lambda b, t: (0, 0))
    out_shape = [jax.ShapeDtypeStruct((B, S, P_REST), BF16), jax.ShapeDtypeStruct(w_uq.shape, F32),
                 jax.ShapeDtypeStruct(w_kv.shape, F32), jax.ShapeDtypeStruct((1, Q_LORA), F32),
                 jax.ShapeDtypeStruct((1, KV_LORA), F32)]
    return pl.pallas_call(
        body, name=name, out_shape=out_shape, grid=(B, tiles),
        in_specs=[pl.BlockSpec((1, npair, ts // tq, 2 * LANES, tq), lambda b, t: (b, 0, t, 0, 0)), tok(kw),
                  tok(D_B), rows(Q_LORA), rows(KV_LORA), tok(P_REST), full(w_uq), full(w_kv), full(g_cq),
                  full(g_ckv), tab, tab],
        out_specs=[tok(P_REST)] + [acc_out(o.shape) for o in out_shape[1:]],
        scratch_shapes=[pltpu.VMEM((ts, kw), BF16)],
        compiler_params=_cparams(("arbitrary", "arbitrary")),
    )(dq_t, dkc, dv, cqn, ckvn, rest, w_uq, w_kv, g_cq, g_ckv, cs, sn_neg)


def _t5_bucket(dist):
    max_exact = N_BUCKETS // 2
    d = np.maximum(dist, 1).astype(np.float64)
    large = max_exact + (np.log(d / max_exact) / np.log(MAX_DISTANCE / max_exact)
                         * (N_BUCKETS - max_exact)).astype(np.int64)
    large = np.minimum(large, N_BUCKETS - 1)
    return np.where(dist < max_exact, dist, large).astype(np.int32)


def _band_buckets(dilation):
    a = np.arange(BLK)[None, :]
    bk = np.arange(2 * BLK)[:, None]
    steps = BLK + a - bk
    return _t5_bucket(np.clip(steps, 0, SPAN) * dilation)


def _head_mask(shape, hh):
    lane = lax.broadcasted_iota(jnp.int32, shape, 1)
    return (lane >= hh * HEAD_DIM) & (lane < (hh + 1) * HEAD_DIM)


def _dot_nt(a, b):
    return lax.dot_general(a, b, (((1,), (1,)), ((), ())), preferred_element_type=F32)


def _dot_tn(a, b):
    return lax.dot_general(a, b, (((0,), (0,)), ((), ())), preferred_element_type=F32)


def _dot_nn(a, b):
    return lax.dot_general(a, b, (((1,), (0,)), ((), ())), preferred_element_type=F32)


def _dil_fwd(qkv, bias, branch, dilation, name, comm=None):
    B, n, _ = qkv.shape
    d = dilation
    nb = n // BLK
    qkv_v = qkv
    npair = N_HEADS // 2

    def body(cur_ref, prev_ref, bias_ref, o_ref, lse_ref, s_scr, e_scr):
        first = jnp.where(pl.program_id(1) == 0, 1, 0)
        units = [(b, h) for b in range(B) for h in range(N_HEADS)]
        for b in range(B):
            for p in range(npair):
                q = cur_ref[b, :, p * LANES:(p + 1) * LANES] * DIL_SCALE
                kc = cur_ref[b, :, D_A + p * LANES:D_A + (p + 1) * LANES]
                kp = prev_ref[b, :, D_A + p * LANES:D_A + (p + 1) * LANES]
                for hh in range(2):
                    u = b * N_HEADS + 2 * p + hh
                    qm = jnp.where(_head_mask((BLK, LANES), hh), q, jnp.zeros_like(q))
                    s_scr[u, 0:BLK, :] = _dot_nt(kp, qm)
                    s_scr[u, BLK:2 * BLK, :] = _dot_nt(kc, qm)
        ms = []
        for u, (b, h) in enumerate(units):
            s_p = s_scr[u, 0:BLK, :] + bias_ref[first, h, 0:BLK, :]
            s_c = s_scr[u, BLK:2 * BLK, :] + bias_ref[first, h, BLK:2 * BLK, :]
            m = jnp.maximum(jnp.max(s_p, axis=0, keepdims=True), jnp.max(s_c, axis=0, keepdims=True))
            e_scr[u, 0:BLK, :] = jnp.exp(s_p - m).astype(BF16)
            e_scr[u, BLK:2 * BLK, :] = jnp.exp(s_c - m).astype(BF16)
            ms.append(m)
        rows0 = _row_mask((LANES, BLK), 0)
        for b in range(B):
            for p in range(npair):
                sl = slice(p * LANES, (p + 1) * LANES)
                vsl = slice(2 * D_A + p * LANES, 2 * D_A + (p + 1) * LANES)
                vct = jnp.transpose(cur_ref[b, :, vsl].astype(F32)).astype(BF16)
                vpt = jnp.transpose(prev_ref[b, :, vsl].astype(F32)).astype(BF16)
                acc = []
                for hh in range(2):
                    u = b * N_HEADS + 2 * p + hh
                    mine = _row_mask((LANES, BLK), hh)
                    one = jnp.ones_like(vct)
                    acc.append(_dot_nn(jnp.where(mine, vpt, one), e_scr[u, 0:BLK, :])
                               + _dot_nn(jnp.where(mine, vct, one), e_scr[u, BLK:2 * BLK, :]))
                l0 = acc[0][HEAD_DIM:HEAD_DIM + 1, :]
                l1 = acc[1][0:1, :]
                u0 = b * N_HEADS + 2 * p
                o_t = jnp.where(rows0, acc[0] / l0, acc[1] / l1)
                lse_t = jnp.where(rows0, ms[u0] + jnp.log(l0), ms[u0 + 1] + jnp.log(l1))
                o_ref[b, :, sl] = jnp.transpose(o_t)
                lse_ref[b, :, sl] = jnp.transpose(lse_t)

    cur = pl.BlockSpec((B, BLK, P_QKV), lambda r, i: (0, i, r))
    prev = pl.BlockSpec((B, BLK, P_QKV), lambda r, i: (0, jnp.maximum(i - 1, 0), r))
    out = pl.BlockSpec((B, BLK, D_A), lambda r, i: (0, i, r))
    return _host_call(
        body, comm, name=name,
        out_shape=[jax.ShapeDtypeStruct((B, n, d * D_A), F32)] * 2,
        grid=(d, nb),
        in_specs=[cur, prev,
                  pl.BlockSpec((None, 2, N_HEADS, 2 * BLK, BLK), lambda r, i: (branch, 0, 0, 0, 0))],
        out_specs=[out, out],
        scratch_shapes=[pltpu.VMEM((B * N_HEADS, 2 * BLK, BLK), F32),
                        pltpu.VMEM((B * N_HEADS, 2 * BLK, BLK), BF16)],
        args=(qkv_v, qkv_v, bias))


VIEW_TILE = 512


def _view_spec(d, w):
    return pl.BlockSpec((1, VIEW_TILE // d, d * w), lambda b, t: (b, t, 0))


def _view_shape(B, S, d, w, dtype):
    return jax.ShapeDtypeStruct((B, S // d, d * w), dtype)


def _tile_scratch(w):
    return pltpu.VMEM((w // LANES, VIEW_TILE, LANES), F32)


def _put_tile(tile_ref, val):
    for c in range(tile_ref.shape[0]):
        tile_ref[c] = val[:, c * LANES:(c + 1) * LANES]


def _get_tile(tile_ref):
    return jnp.concatenate([tile_ref[c] for c in range(tile_ref.shape[0])], axis=1)


def _tile_to_view(tile_ref, view_ref, d, w):
    for c in range(w // LANES):
        for r in range(d):
            lo = r * w + c * LANES
            rows = tile_ref.at[c][pl.ds(r, VIEW_TILE // d, stride=d), :]
            view_ref[0, :, lo:lo + LANES] = rows.astype(view_ref.dtype)


def _view_to_tile(view_ref, tile_ref, d, w):
    for c in range(w // LANES):
        for r in range(d):
            lo = r * w + c * LANES
            tile_ref.at[c][pl.ds(r, VIEW_TILE // d, stride=d), :] = view_ref[0, :, lo:lo + LANES].astype(F32)


def _in_proj(x, g_norm, sc, sh, w_qkv, w_rest, g_cq, g_ckv, name):
    B, S, D = x.shape
    N = w_qkv.shape[1]
    cq_lo = KV_LORA + LANES

    def rms(xv, g_ref):
        return ((xv * lax.rsqrt(jnp.mean(xv * xv, axis=-1, keepdims=True) + EPS)) * g_ref[...]).astype(BF16)

    def body(x_ref, gn_ref, sc_ref, sh_ref, wq_ref, wr_ref, gcq_ref, gckv_ref, h_ref, o1_ref, o4_ref, o16_ref,
             rest_ref, cqn_ref, ckvn_ref, acc_ref):
        hv = ((_rms_parts(x_ref[0])[1] * gn_ref[...]) * (1.0 + sc_ref[0]) + sh_ref[0]).astype(BF16)
        h_ref[0] = hv
        acc = jnp.dot(hv, wq_ref[...], preferred_element_type=F32)
        o1_ref[0] = acc.astype(o1_ref.dtype)
        _put_tile(acc_ref, acc)
        _tile_to_view(acc_ref, o4_ref, DILATIONS[1], N)
        _tile_to_view(acc_ref, o16_ref, DILATIONS[2], N)
        rest = jnp.dot(hv, wr_ref[...], preferred_element_type=F32)
        rest_ref[0] = rest
        ckvn_ref[...] = rms(rest[:, :KV_LORA], gckv_ref)
        cqn_ref[...] = rms(rest[:, cq_lo:], gcq_ref)

    tiles = S // VIEW_TILE
    full = lambda a: pl.BlockSpec(a.shape, lambda b, t: (0, 0))
    rows = lambda n: pl.BlockSpec((VIEW_TILE, n), lambda b, t: (b * tiles + t, 0))
    per_b = pl.BlockSpec((1, 1, D), lambda b, t: (b, 0, 0))
    res = pl.pallas_call(
        body, name=name,
        out_shape=[jax.ShapeDtypeStruct((B, S, D), BF16)] + [_view_shape(B, S, d, N, BF16) for d in DILATIONS]
        + [jax.ShapeDtypeStruct((B, S, P_REST), F32), jax.ShapeDtypeStruct((B * S, Q_LORA), BF16),
           jax.ShapeDtypeStruct((B * S, KV_LORA), BF16)],
        grid=(B, tiles),
        in_specs=[_view_spec(1, D), full(g_norm), per_b, per_b, full(w_qkv), full(w_rest), full(g_cq), full(g_ckv)],
        out_specs=[_view_spec(1, D)] + [_view_spec(d, N) for d in DILATIONS]
        + [_view_spec(1, P_REST), rows(Q_LORA), rows(KV_LORA)],
        scratch_shapes=[_tile_scratch(N)],
        compiler_params=_cparams(("parallel", "parallel")),
    )(x, g_norm, sc, sh, w_qkv, w_rest, g_cq, g_ckv)
    nd = len(DILATIONS)
    return res[0], res[1:1 + nd], res[1 + nd], res[2 + nd], res[3 + nd]


def _dil_merge(os_, lses, name):
    B, S, W = os_[0].shape
    nd = len(DILATIONS)

    def body(*refs):
        o_refs, l_refs = refs[:nd], refs[nd:2 * nd]
        out_refs, L_refs = refs[2 * nd:3 * nd], refs[3 * nd:4 * nd]
        scr = refs[4 * nd:]
        o_tok, l_tok = [o_refs[0][0]], [l_refs[0][0]]
        for i, d in enumerate(DILATIONS[1:]):
            _view_to_tile(o_refs[i + 1], scr[2 * i], d, W)
            _view_to_tile(l_refs[i + 1], scr[2 * i + 1], d, W)
            o_tok.append(_get_tile(scr[2 * i]))
            l_tok.append(_get_tile(scr[2 * i + 1]))
        a0, a1, a2 = l_tok
        m = jnp.maximum(jnp.maximum(a0, a1), a2)
        e0, e1, e2 = jnp.exp(a0 - m), jnp.exp(a1 - m), jnp.exp(a2 - m)
        ssum = e0 + e1 + e2
        out = (e0 * o_tok[0] + e1 * o_tok[1] + e2 * o_tok[2]) / ssum
        lse = m + jnp.log(ssum)
        out_refs[0][0] = out
        L_refs[0][0] = lse
        res_o, res_l = scr[2 * (nd - 1)], scr[2 * (nd - 1) + 1]
        _put_tile(res_o, out)
        _put_tile(res_l, lse)
        for i, d in enumerate(DILATIONS[1:]):
            _tile_to_view(res_o, out_refs[i + 1], d, W)
            _tile_to_view(res_l, L_refs[i + 1], d, W)

    specs = [_view_spec(d, W) for d in DILATIONS]
    shapes = [_view_shape(B, S * DILATIONS[0], d, W, F32) for d in DILATIONS]
    res = pl.pallas_call(
        body, name=name, out_shape=shapes * 2, grid=(B, S // VIEW_TILE),
        in_specs=specs * 2, out_specs=specs * 2,
        scratch_shapes=[_tile_scratch(W)] * (2 * nd),
        compiler_params=_cparams(("parallel", "parallel")),
    )(*os_, *lses)
    return res[:nd], res[nd:]


def _dil_bwd(qkv, do, out_a, L, bias, branch, dilation, name, comm=None):
    B, n, _ = qkv.shape
    d = dilation
    nb = n // BLK
    qkv_v, do_v, oa_v, L_v = qkv, do, out_a, L
    npair = N_HEADS // 2
    multi = nb > 1

    tiles = ("P", "C", "N") if multi else ("C",)
    n_t = len(tiles)

    def body(*refs):
        if multi:
            (cur_ref, prev_ref, next_ref, do_ref, don_ref, oa_ref, oan_ref, L_ref, Ln_ref, bias_ref,
             dqkv_ref, dbias_ref, s_scr, dp_scr, p_scr, ds_scr) = refs
        else:
            cur_ref, do_ref, oa_ref, L_ref, bias_ref, dqkv_ref, dbias_ref, s_scr, dp_scr, p_scr, ds_scr = refs
        r, i = pl.program_id(0), pl.program_id(1)

        @pl.when((r == 0) & (i == 0))
        def _():
            dbias_ref[...] = jnp.zeros_like(dbias_ref)

        first = jnp.where(i == 0, 1, 0)
        variant = {"P": first, "C": first, "N": 0}
        band = {"P": slice(0, BLK), "C": slice(BLK, 2 * BLK), "N": slice(0, BLK)}
        psl = lambda p: slice(p * LANES, (p + 1) * LANES)
        ksl = lambda p: slice(D_A + p * LANES, D_A + (p + 1) * LANES)
        vsl = lambda p: slice(2 * D_A + p * LANES, 2 * D_A + (p + 1) * LANES)

        def operands(b, p, hh):
            hm = _head_mask((BLK, LANES), hh)
            mask = lambda x: jnp.where(hm, x, jnp.zeros_like(x))
            qm, dom = mask(cur_ref[b, :, psl(p)] * DIL_SCALE), mask(do_ref[b, :, psl(p)])
            ops = {"C": (cur_ref[b, :, ksl(p)], cur_ref[b, :, vsl(p)], qm, dom)}
            if multi:
                ops["P"] = (prev_ref[b, :, ksl(p)], prev_ref[b, :, vsl(p)], qm, dom)
                ops["N"] = (cur_ref[b, :, ksl(p)], cur_ref[b, :, vsl(p)],
                            mask(next_ref[b, :, psl(p)] * DIL_SCALE), mask(don_ref[b, :, psl(p)]))
            return ops

        pairs = [(b, p) for b in range(B) for p in range(npair)]
        for b, p in pairs:
            for hh in range(2):
                u = b * N_HEADS + 2 * p + hh
                ops = operands(b, p, hh)
                for t, name_t in enumerate(tiles):
                    k_t, v_t, q_t, do_t = ops[name_t]
                    s_scr[u, t] = _dot_nt(k_t, q_t)
                    dp_scr[u, t] = _dot_nt(v_t, do_t)

        def rows(L_r, do_r, oa_r, b, p):
            lt = jnp.transpose(L_r[b, :, psl(p)])
            dt = jnp.transpose(do_r[b, :, psl(p)].astype(F32) * oa_r[b, :, psl(p)])
            return ([lt[0:1, :], lt[HEAD_DIM:HEAD_DIM + 1, :]],
                    [jnp.sum(dt[:HEAD_DIM], axis=0, keepdims=True), jnp.sum(dt[HEAD_DIM:], axis=0, keepdims=True)])

        for b, p in pairs:
            lse_c, delta_c = rows(L_ref, do_ref, oa_ref, b, p)
            if multi:
                lse_n, delta_n = rows(Ln_ref, don_ref, oan_ref, b, p)
            for hh in range(2):
                h = 2 * p + hh
                u = b * N_HEADS + h
                for t, name_t in enumerate(tiles):
                    lse, delta = (lse_n[hh], delta_n[hh]) if name_t == "N" else (lse_c[hh], delta_c[hh])
                    pr = jnp.exp(s_scr[u, t] + bias_ref[variant[name_t], h, band[name_t], :] - lse)
                    if name_t == "N":
                        pr = jnp.where(i < nb - 1, pr, 0.0)
                    ds = pr * (dp_scr[u, t] - delta)
                    p_scr[u, t] = pr.astype(BF16)
                    ds_scr[u, t] = ds.astype(BF16)
                    if name_t != "N":
                        dbias_ref[h, band[name_t], :] += ds

        for b, p in pairs:
            dqt = jnp.zeros((LANES, BLK), F32)
            dk = jnp.zeros((BLK, LANES), F32)
            dv = jnp.zeros((BLK, LANES), F32)
            kct = jnp.transpose(cur_ref[b, :, ksl(p)].astype(F32)).astype(BF16)
            if multi:
                kpt = jnp.transpose(prev_ref[b, :, ksl(p)].astype(F32)).astype(BF16)
            for hh in range(2):
                u = b * N_HEADS + 2 * p + hh
                ops = operands(b, p, hh)
                mine = _row_mask((LANES, BLK), hh)
                for t, name_t in enumerate(tiles):
                    _, _, q_t, do_t = ops[name_t]
                    if name_t != "P":
                        dv = dv + _dot_nn(p_scr[u, t], do_t)
                        dk = dk + _dot_nn(ds_scr[u, t], q_t)
                    if name_t != "N":
                        kt = kpt if name_t == "P" else kct
                        dqt = dqt + _dot_nn(jnp.where(mine, kt, jnp.zeros_like(kt)), ds_scr[u, t])
            dqkv_ref[b, :, psl(p)] = jnp.transpose(dqt) * DIL_SCALE
            dqkv_ref[b, :, ksl(p)] = dk
            dqkv_ref[b, :, vsl(p)] = dv

    def at(off):
        return lambda r, i: (0, jnp.clip(i + off, 0, nb - 1), r)

    qkv_spec = lambda off: pl.BlockSpec((B, BLK, P_QKV), at(off))
    da_spec = lambda off: pl.BlockSpec((B, BLK, D_A), at(off))
    bias_spec = pl.BlockSpec((None, 2, N_HEADS, 2 * BLK, BLK), lambda r, i: (branch, 0, 0, 0, 0))
    dbias_spec = pl.BlockSpec((N_HEADS, 2 * BLK, BLK), lambda r, i: (0, 0, 0))
    if multi:
        in_specs = [qkv_spec(0), qkv_spec(-1), qkv_spec(1), da_spec(0), da_spec(1), da_spec(0), da_spec(1),
                    da_spec(0), da_spec(1), bias_spec]
        args = [qkv_v, qkv_v, qkv_v, do_v, do_v, oa_v, oa_v, L_v, L_v, bias]
    else:
        in_specs = [qkv_spec(0), da_spec(0), da_spec(0), da_spec(0), bias_spec]
        args = [qkv_v, do_v, oa_v, L_v, bias]
    return _host_call(
        body, comm, name=name,
        out_shape=[jax.ShapeDtypeStruct((B, n, d * P_QKV), F32),
                   jax.ShapeDtypeStruct((N_HEADS, 2 * BLK, BLK), F32)],
        grid=(d, nb),
        in_specs=in_specs,
        out_specs=[qkv_spec(0), dbias_spec],
        scratch_shapes=[pltpu.VMEM((B * N_HEADS, n_t, BLK, BLK), F32), pltpu.VMEM((B * N_HEADS, n_t, BLK, BLK), F32),
                        pltpu.VMEM((B * N_HEADS, n_t, BLK, BLK), BF16),
                        pltpu.VMEM((B * N_HEADS, n_t, BLK, BLK), BF16)],
        args=args)


def _sum_views_bf16(parts, tail, name):
    B, S, W = parts[0].shape
    Wt = tail.shape[-1]

    def body(a_ref, b_ref, c_ref, t_ref, o_ref, sb, sc):
        _view_to_tile(b_ref, sb, DILATIONS[1], W)
        _view_to_tile(c_ref, sc, DILATIONS[2], W)
        o_ref[0, :, :W] = (a_ref[0] + _get_tile(sb) + _get_tile(sc)).astype(o_ref.dtype)
        o_ref[0, :, W:] = t_ref[0]

    return pl.pallas_call(
        body, name=name, out_shape=jax.ShapeDtypeStruct((B, S, W + Wt), BF16), grid=(B, S // VIEW_TILE),
        in_specs=[_view_spec(d, W) for d in DILATIONS] + [_view_spec(1, Wt)], out_specs=_view_spec(1, W + Wt),
        scratch_shapes=[_tile_scratch(W)] * 2,
        compiler_params=_cparams(("parallel", "parallel")),
    )(*parts, tail)


def _bias_tables(rel_bias, buckets, name, comm=None):
    nbr = buckets.shape[0]

    def body(rb_ref, bk_ref, o_ref):
        first, h = pl.program_id(1), pl.program_id(2)
        tab = bk_ref[0]

        def step(bkt, acc):
            return jnp.where(tab == bkt, rb_ref[bkt, h], acc)

        bias = lax.fori_loop(0, N_BUCKETS, step, jnp.zeros((2 * BLK, BLK), F32))
        row = lax.broadcasted_iota(jnp.int32, (2 * BLK, BLK), 0)
        col = lax.broadcasted_iota(jnp.int32, (2 * BLK, BLK), 1)
        valid = ((row < BLK) & (row >= col) & (first == 0)) | ((row >= BLK) & (row - BLK <= col))
        o_ref[0, 0, 0] = jnp.where(valid, bias, NEG)

    (bias,), got = _host_call(
        body, comm, name=name, out_shape=[jax.ShapeDtypeStruct((nbr, 2, N_HEADS, 2 * BLK, BLK), F32)],
        grid=(nbr, 2, N_HEADS),
        in_specs=[pl.BlockSpec(memory_space=pltpu.SMEM),
                  pl.BlockSpec((1, 2 * BLK, BLK), lambda i, f, h: (i, 0, 0))],
        out_specs=[pl.BlockSpec((1, 1, 1, 2 * BLK, BLK), lambda i, f, h: (i, f, h, 0, 0))],
        scratch_shapes=[], args=(rel_bias, buckets))
    return bias, got


def _bias_grad(dbias_list, buckets, name):
    nbr = len(dbias_list)

    def body(*refs):
        d_refs, bk_ref, o_ref, part = refs[:nbr], refs[nbr], refs[nbr + 1], refs[nbr + 2]

        def step(bkt, carry):
            hit = [bk_ref[bi] == bkt for bi in range(nbr)]
            for h in range(N_HEADS):
                tot = jnp.zeros((1, BLK), F32)
                for bi in range(nbr):
                    tot = tot + jnp.sum(jnp.where(hit[bi], d_refs[bi][h], 0.0), axis=0, keepdims=True)
                part[bkt, h:h + 1, :] = tot
            return carry

        lax.fori_loop(0, N_BUCKETS, step, 0)
        lane = lax.broadcasted_iota(jnp.int32, (N_HEADS, LANES), 1)
        acc = jnp.zeros((N_HEADS, LANES), F32)
        for bkt in range(N_BUCKETS):
            acc = acc + jnp.where(lane == bkt, jnp.sum(part[bkt], axis=1, keepdims=True), 0.0)
        o_ref[...] = acc

    band = pl.BlockSpec((N_HEADS, 2 * BLK, BLK), lambda i: (0, 0, 0))
    return pl.pallas_call(
        body, name=name, out_shape=jax.ShapeDtypeStruct((N_HEADS, LANES), F32), grid=(1,),
        in_specs=[band] * nbr + [pl.BlockSpec((nbr, 2 * BLK, BLK), lambda i: (0, 0, 0))],
        out_specs=pl.BlockSpec((N_HEADS, LANES), lambda i: (0, 0)),
        scratch_shapes=[pltpu.VMEM((N_BUCKETS, N_HEADS, BLK), F32)],
        compiler_params=_cparams(("arbitrary",)),
    )(*dbias_list, buckets)


MLA_TQ = 256
MLA_TK = 256


LOG2E = math.log2(math.e)
MLA_C = MLA_SCALE * LOG2E


def _key_le_query(tk, tq):
    return lax.broadcasted_iota(jnp.int32, (tk, tq), 0) <= lax.broadcasted_iota(jnp.int32, (tk, tq), 1)


def _row_mask(shape, hh):
    row = lax.broadcasted_iota(jnp.int32, shape, 0)
    return (row >= hh * HEAD_DIM) & (row < (hh + 1) * HEAD_DIM)


def _host_call(body, comm, *, name, grid, in_specs, out_specs, out_shape, scratch_shapes, args):
    sem = ("arbitrary",) * len(grid)
    if comm is None:
        res = pl.pallas_call(body, name=name, grid=grid, in_specs=in_specs, out_specs=out_specs,
                             out_shape=out_shape, scratch_shapes=scratch_shapes,
                             compiler_params=_cparams(sem))(*args)
        return res, []
    n_in, n_out, n_s, cn = len(in_specs), len(out_specs), len(scratch_shapes), comm.n

    def hosted(*refs):
        ins, refs = refs[:n_in], refs[n_in:]
        c_ins, refs = refs[:cn], refs[cn:]
        outs, refs = refs[:n_out], refs[n_out:]
        c_outs, refs = refs[:cn], refs[cn:]
        scr, c_sems = refs[:n_s], refs[n_s:]
        ids = [pl.program_id(a) for a in range(len(grid))]
        first = functools.reduce(jnp.logical_and, [i == 0 for i in ids])
        last = functools.reduce(jnp.logical_and, [i == g - 1 for i, g in zip(ids, grid)])

        @pl.when(first)
        def _():
            comm.start(c_ins, c_outs, c_sems)

        body(*ins, *outs, *scr)

        @pl.when(last)
        def _():
            comm.finish(c_ins, c_outs, c_sems)

    res = pl.pallas_call(
        hosted, name=name, grid=grid, in_specs=list(in_specs) + _hbm_specs(cn),
        out_specs=list(out_specs) + _hbm_specs(cn), out_shape=list(out_shape) + list(comm.out_shape),
        scratch_shapes=list(scratch_shapes) + list(comm.scratch), compiler_params=_cparams(sem),
    )(*args, *comm.inputs)
    return res[:n_out], res[n_out:]


def _mla_fwd_t(q, k, vt, name, comm=None):
    B, S, _ = q.shape
    tq, tk = MLA_TQ, MLA_TK
    assert tq == tk
    npair = N_HEADS // 2
    nq = S // tq

    def body(q_ref, k_ref, vt_ref, o_ref, lse_ref, s_scr, e_scr, acc_scr, m_scr, a_scr):
        i = pl.program_id(1)
        diag = _key_le_query(tk, tq)
        m_scr[...] = jnp.full_like(m_scr, NEG)
        acc_scr[...] = jnp.zeros_like(acc_scr)

        def step(j, masked):
            rows = pl.ds(pl.multiple_of(j * tk, tk), tk)
            for h in range(N_HEADS):
                hsl = slice(h * LANES, (h + 1) * LANES)
                s_scr[h] = _dot_nt(k_ref[0, rows, hsl], q_ref[0, :, hsl])
            for h in range(N_HEADS):
                s = s_scr[h]
                if masked:
                    s = jnp.where(diag, s, NEG)
                m_old = m_scr[h:h + 1, :]
                m_new = jnp.maximum(m_old, jnp.max(s, axis=0, keepdims=True))
                a_scr[h:h + 1, :] = jnp.exp2((m_old - m_new) * MLA_C)
                e_scr[h] = jnp.exp2((s - m_new) * MLA_C).astype(BF16)
                m_scr[h:h + 1, :] = m_new
            for h in range(N_HEADS):
                vj = vt_ref[0, h // 2, j]
                vh = jnp.where(_row_mask(vj.shape, h % 2), vj, jnp.ones_like(vj))
                acc_scr[h] = acc_scr[h] * a_scr[h:h + 1, :] + _dot_nn(vh, e_scr[h])

        def loop_body(j, carry):
            step(j, False)
            return carry

        lax.fori_loop(0, i, loop_body, 0)
        step(i, True)
        rows0 = _row_mask((LANES, tq), 0)
        for p in range(npair):
            l0 = acc_scr[2 * p, HEAD_DIM:HEAD_DIM + 1, :]
            l1 = acc_scr[2 * p + 1, 0:1, :]
            o_t = jnp.where(rows0, acc_scr[2 * p] / l0, acc_scr[2 * p + 1] / l1)
            o_ref[0, :, p * LANES:(p + 1) * LANES] = jnp.transpose(o_t)
            lse_ref[0, p, 0] = jnp.zeros((8, tq), F32)
            lse_ref[0, p, 0, 0:1, :] = m_scr[2 * p:2 * p + 1, :] * MLA_C + jnp.log(l0) * LOG2E
            lse_ref[0, p, 0, 1:2, :] = m_scr[2 * p + 1:2 * p + 2, :] * MLA_C + jnp.log(l1) * LOG2E

    return _host_call(
        body, comm, name=name,
        out_shape=[jax.ShapeDtypeStruct((B, S, D_B), F32), jax.ShapeDtypeStruct((B, npair, nq, 8, tq), F32)],
        grid=(B, nq),
        in_specs=[pl.BlockSpec((1, tq, N_HEADS * LANES), lambda b, i: (b, i, 0)),
                  pl.BlockSpec((1, S, N_HEADS * LANES), lambda b, i: (b, 0, 0)),
                  pl.BlockSpec((1, npair, S // tk, LANES, tk), lambda b, i: (b, 0, 0, 0, 0))],
        out_specs=[pl.BlockSpec((1, tq, D_B), lambda b, i: (b, i, 0)),
                   pl.BlockSpec((1, npair, 1, 8, tq), lambda b, i: (b, 0, i, 0, 0))],
        scratch_shapes=[pltpu.VMEM((N_HEADS, tk, tq), F32), pltpu.VMEM((N_HEADS, tk, tq), BF16),
                        pltpu.VMEM((N_HEADS, LANES, tq), F32), pltpu.VMEM((N_HEADS, tq), F32),
                        pltpu.VMEM((N_HEADS, tq), F32)],
        args=(q, k, vt))


def _mla_bwd_t(q, k, v, do, lse, delta, name, comm=None):
    B, S, _ = q.shape
    tq, tk = MLA_TQ, MLA_TK
    assert tq == tk
    npair = N_HEADS // 2
    nq = S // tq

    hg = N_HEADS
    pg = hg // 2
    ngroup = N_HEADS // hg

    def body(q_ref, do_ref, lse_ref, dl_ref, k_ref, v_ref, dk_ref, dv_ref, dq_ref,
             s_scr, dp_scr, p_scr, ds_scr, dk_s, dv_s, kt_s):
        j = pl.program_id(2)

        @pl.when(j == 0)
        def _():
            dq_ref[...] = jnp.zeros_like(dq_ref)

        dk_s[...] = jnp.zeros_like(dk_s)
        dv_s[...] = jnp.zeros_like(dv_s)
        diag = _key_le_query(tk, tq)
        hsl = lambda h: slice(h * LANES, (h + 1) * LANES)
        for h in range(hg):
            kt_s[h] = jnp.transpose(k_ref[0, :, hsl(h)].astype(F32)).astype(BF16)

        def step(i, masked):
            rows = pl.ds(pl.multiple_of(i * tq, tq), tq)

            def dom(h):
                dov = do_ref[0, rows, hsl(h // 2)]
                return jnp.where(_head_mask((tq, LANES), h % 2), dov, jnp.zeros_like(dov))

            for h in range(hg):
                s_scr[h] = _dot_nt(k_ref[0, :, hsl(h)], q_ref[0, rows, hsl(h)])
                dp_scr[h] = _dot_nt(v_ref[0, :, hsl(h // 2)], dom(h))
            for h in range(hg):
                pr = jnp.exp2(s_scr[h] * MLA_C - lse_ref[0, h // 2, i, h % 2:h % 2 + 1, :])
                if masked:
                    pr = jnp.where(diag, pr, 0.0)
                p_scr[h] = pr.astype(BF16)
                ds_scr[h] = (pr * (dp_scr[h] - dl_ref[0, h // 2, i, h % 2:h % 2 + 1, :])).astype(BF16)
            for h in range(hg):
                dv_s[h // 2] += _dot_nn(p_scr[h], dom(h))
                dk_s[h] += _dot_nn(ds_scr[h], q_ref[0, rows, hsl(h)])
                dq_ref[0, h // 2, i, hsl(h % 2), :] += _dot_nn(kt_s[h], ds_scr[h]) * MLA_SCALE

        step(j, True)

        def loop_body(i, carry):
            step(i, False)
            return carry

        lax.fori_loop(j + 1, nq, loop_body, 0)
        for h in range(hg):
            dk_ref[0, :, hsl(h)] = dk_s[h] * MLA_SCALE
        for p in range(pg):
            dv_ref[0, :, hsl(p)] = dv_s[p]

    stat = pl.BlockSpec((1, pg, nq, 8, tq), lambda b, g, j: (b, g, 0, 0, 0))
    return _host_call(
        body, comm, name=name,
        out_shape=[jax.ShapeDtypeStruct((B, S, N_HEADS * LANES), F32), jax.ShapeDtypeStruct((B, S, D_B), F32),
                   jax.ShapeDtypeStruct((B, npair, nq, 2 * LANES, tq), F32)],
        grid=(B, ngroup, S // tk),
        in_specs=[pl.BlockSpec((1, S, hg * LANES), lambda b, g, j: (b, 0, g)),
                  pl.BlockSpec((1, S, pg * LANES), lambda b, g, j: (b, 0, g)),
                  stat, stat,
                  pl.BlockSpec((1, tk, hg * LANES), lambda b, g, j: (b, j, g)),
                  pl.BlockSpec((1, tk, pg * LANES), lambda b, g, j: (b, j, g))],
        out_specs=[pl.BlockSpec((1, tk, hg * LANES), lambda b, g, j: (b, j, g)),
                   pl.BlockSpec((1, tk, pg * LANES), lambda b, g, j: (b, j, g)),
                   pl.BlockSpec((1, pg, nq, 2 * LANES, tq), lambda b, g, j: (b, g, 0, 0, 0))],
        scratch_shapes=[pltpu.VMEM((hg, tk, tq), F32), pltpu.VMEM((hg, tk, tq), F32),
                        pltpu.VMEM((hg, tk, tq), BF16), pltpu.VMEM((hg, tk, tq), BF16),
                        pltpu.VMEM((hg, tk, LANES), F32), pltpu.VMEM((pg, tk, LANES), F32),
                        pltpu.VMEM((hg, LANES, tk), BF16)],
        args=(q, do, lse, delta, k, v))


def _bucket_tables():
    return jnp.asarray(np.stack([_band_buckets(d) for d in DILATIONS]))


def _local_step(x, target, mod, wts, gains, rel_bias, ffn_shards=None, bias=None):
    B, S, D = x.shape
    T = B * S
    sh1, sc1, g1, sh2, sc2, g2 = [mod[:, i * D:(i + 1) * D].reshape(B, 1, D) for i in range(N_MOD)]
    cs, sn = _rope_tables()
    buckets_dev = _bucket_tables()
    if bias is None:
        bias, _ = _bias_tables(rel_bias, buckets_dev, "rel_bias_tables")
    w_in = wts["w_in"]

    h1, qkv_v, rest3, cqn, ckvn = _in_proj(x, gains["g_norm1"], sc1, sh1, w_in[:, :P_QKV], w_in[:, P_QKV:],
                                           gains["g_cq"], gains["g_ckv"], "mm_in")
    h1f = h1.reshape(T, D)
    o_d, lse_d = [], []
    late_got = []
    for i, d in enumerate(DILATIONS):
        comm = _GatherComm(ffn_shards[i + 1:i + 2]) if (ffn_shards and i < 2) else None
        (o_i, lse_i), got = _dil_fwd(qkv_v[i], bias, i, d, f"dil_fwd_{d}", comm)
        late_got += list(got)
        o_d.append(o_i)
        lse_d.append(lse_i)
    if ffn_shards:
        wts = dict(wts, w_out=late_got[1].reshape(D, D))
    out_a_v, lse_a_v = _dil_merge(o_d, lse_d, "dil_merge")
    out_a = out_a_v[0]
    qc, kc, v, vt = _mla_proj(cqn, ckvn, rest3, wts["w_uq"], wts["w_kv"], cs, sn, "mla_proj")
    (out_b, lse_b), got = _mla_fwd_t(qc, kc, vt, "mla_fwd", _GatherComm(ffn_shards[:1]) if ffn_shards else None)
    if ffn_shards:
        wts = dict(wts, w_ffn_in=got[0].reshape(N_CHIP, D, -1), w_ffn_out=late_got[0].reshape(D_FF, D))
    y, mix, x1, h2 = _mixer_out(out_a, out_b, gains["g_out_a"], gains["g_out_b"], wts["w_out"], x, g1,
                                gains["g_norm2"], sc2, sh2, "mixer_out")
    h2f = h2.reshape(T, D)
    gu, act = _ffn_in_fwd(h2f, wts["w_ffn_in"], "mm_ffn_in")
    dx2, df, dg2, dg_final, loss = _final_loss(x1, act, wts["w_ffn_out"], g2, gains["g_final"], target,
                                               "ffn_out_loss")

    dff = df.reshape(T, D)
    dgu = _ffn_out_bwd(dff, wts["w_ffn_out"], gu, "mm_ffn_out_dx")
    gw_ffn_out = _mm(act, dff, "tn", F32, "mm_ffn_out_dw")
    dh2 = _mm(dgu, wts["w_ffn_in"], "nt", F32, "mm_ffn_in_dx", col_blocks=N_CHIP, halves=True).reshape(B, S, D)
    gw_ffn_in = _mm(h2f, dgu, "tn", F32, "mm_ffn_in_dw", col_blocks=N_CHIP, halves=True)
    ffn_g8 = ffn_r1 = None
    if ffn_shards:
        ffn_g8 = [gw_ffn_in.reshape(N_DEV, -1, gw_ffn_in.shape[-1]), gw_ffn_out.reshape(N_DEV, -1, D)]
        dx1, dsh2, dsc2, dg_norm2, dg1, dmix, ffn_r1 = _adaln_bwd(
            dh2, x1, gains["g_norm2"], sc2, dx2, "adaln2_bwd", mix=mix, gate=g1, comm=_ToSiblingComm(ffn_g8))
    else:
        dx1, dsh2, dsc2, dg_norm2, dg1, dmix = _adaln_bwd(dh2, x1, gains["g_norm2"], sc2, dx2, "adaln2_bwd",
                                                          mix=mix, gate=g1)
    dmixf = dmix.reshape(T, D)
    gw_out = _mm(y, dmixf, "tn", F32, "mm_out_dw")
    do_a_v, dg_out_a, do_b3, dg_out_b, delta_b = _mixer_out_bwd(dmix, wts["w_out"], out_a, out_b, gains["g_out_a"],
                                                               gains["g_out_b"], "mixer_out_bwd")
    ffn_a4 = ffn_send = None
    if ffn_shards:
        ffn_a4, ffn_send = _rs_first(ffn_g8, "ffn", r1=ffn_r1)
    (dkc, dv, dq_t), ffn_r2 = _mla_bwd_t(qc, kc, v, do_b3, lse_b, delta_b, "mla_bwd",
                                         _ToChipsComm(ffn_send[:1]) if ffn_shards else None)
    d_rest, gw_uq, gw_kv, dg_cq, dg_ckv = _mla_proj_bwd(dq_t, dkc, dv, cqn, ckvn, rest3, wts["w_uq"], wts["w_kv"],
                                                        gains["g_cq"], gains["g_ckv"], cs, -sn, "mla_proj_bwd")
    dqkv_d, dbias_d = [], []
    for i, d in enumerate(DILATIONS):
        comm = _ToChipsComm(ffn_send[1:]) if (ffn_shards and i == 0) else None
        (dqkv_i, dbias_i), got = _dil_bwd(qkv_v[i], do_a_v[i], out_a_v[i], lse_a_v[i], bias, i, d,
                                          f"dil_bwd_{d}", comm)
        if comm is not None:
            ffn_r2 = list(ffn_r2) + list(got)
        dqkv_d.append(dqkv_i)
        dbias_d.append(dbias_i)
    dproj = _sum_views_bf16(dqkv_d, d_rest, "dil_bwd_sum").reshape(T, P_QKV + P_REST)
    g_rel_bias = _bias_grad(dbias_d, buckets_dev, "rel_bias_grad")[:, :N_BUCKETS].T
    gw_in = _mm(h1f, dproj, "tn", F32, "mm_in_dw")
    mix_a4 = mix_r2 = None
    if ffn_shards:
        nat = [_w_in_from_kernel(gw_in), _w_uq_from_kernel(gw_uq), _w_ukv_from_kernel(gw_kv)]
        g8 = [_shards_from_full(g) for g in nat] + [gw_out]
        mix_a4, mix_send = _rs_first([g.reshape(N_DEV, -1, g.shape[-1]) for g in g8], "mix")
        dh1, mix_r2 = _mm(dproj, w_in, "nt", F32, "mm_in_dx", comm=_ToChipsComm(mix_send))
    else:
        dh1 = _mm(dproj, w_in, "nt", F32, "mm_in_dx")
    dh1 = dh1.reshape(B, S, D)
    grad_x, dsh1, dsc1, dg_norm1 = _adaln_bwd(dh1, x, gains["g_norm1"], sc1, dx1, "adaln1_bwd")
    gmod = jnp.concatenate([dsh1, dsc1, dg1, dsh2, dsc2, dg2], axis=-1).reshape(B, N_MOD * D)
    grads = dict(w_in=gw_in, w_uq=gw_uq, w_kv=gw_kv, w_out=gw_out, w_ffn_in=gw_ffn_in, w_ffn_out=gw_ffn_out,
                 g_norm1=dg_norm1, g_cq=dg_cq, g_ckv=dg_ckv, rel_bias=g_rel_bias, g_out_a=dg_out_a,
                 g_out_b=dg_out_b, g_norm2=dg_norm2, g_final=dg_final, ffn_pending=(ffn_a4, ffn_r2),
                 mix_pending=(mix_a4, mix_r2))
    return loss, grad_x, gmod, grads


def _w_in_to_kernel(w):
    z = lambda n: jnp.zeros((w.shape[0], n), w.dtype)
    i3, i4, i5 = 3 * D_A, 3 * D_A + Q_LORA, 3 * D_A + Q_LORA + KV_LORA
    return jnp.concatenate([w[:, :i3], w[:, i4:i5], z(NOPE_DIM), w[:, i5:], z(LANES - NOPE_DIM - ROPE_DIM),
                            w[:, i3:i4]], axis=1)


def _w_in_from_kernel(g):
    o = P_QKV + KV_LORA
    return jnp.concatenate([g[:, :P_QKV], g[:, o + LANES:], g[:, P_QKV:o],
                            g[:, o + NOPE_DIM:o + NOPE_DIM + ROPE_DIM]], axis=1)


def _w_uq_to_kernel(w):
    w3 = w.reshape(Q_LORA, N_HEADS, NOPE_DIM + ROPE_DIM)
    return jnp.pad(w3, ((0, 0), (0, 0), (0, LANES - NOPE_DIM - ROPE_DIM))).reshape(Q_LORA, N_HEADS * LANES)


def _w_uq_from_kernel(g):
    return g.reshape(Q_LORA, N_HEADS, LANES)[:, :, :NOPE_DIM + ROPE_DIM].reshape(Q_LORA, -1)


def _w_ukv_to_kernel(w):
    w3 = w.reshape(KV_LORA, N_HEADS, 2 * HEAD_DIM)
    wk = jnp.pad(w3[:, :, :NOPE_DIM], ((0, 0), (0, 0), (0, LANES - NOPE_DIM))).reshape(KV_LORA, N_HEADS * LANES)
    wv = w3[:, :, NOPE_DIM:].reshape(KV_LORA, D_B)
    return jnp.concatenate([wk, wv], axis=1)


def _w_ukv_from_kernel(g):
    gk = g[:, :N_HEADS * LANES].reshape(KV_LORA, N_HEADS, LANES)[:, :, :NOPE_DIM]
    gv = g[:, N_HEADS * LANES:].reshape(KV_LORA, N_HEADS, HEAD_DIM)
    return jnp.concatenate([gk, gv], axis=2).reshape(KV_LORA, -1)


MESH = pl.DeviceIdType.MESH


def _my_place():
    return lax.axis_index("x"), lax.axis_index("y"), lax.axis_index("c")


def _other_chips(x, y):
    return [(1 - x, y), (x, 1 - y), (1 - x, 1 - y)]


def _allgather8(x_shard, name, in_hbm):
    m_per, n = x_shard.shape
    space = pl.ANY if in_hbm else pltpu.VMEM

    def body(x_ref, out_ref, send_sems, recv_sems, local_sem):
        x, y, c = _my_place()
        me, sibling = (x, y, c), (x, y, 1 - c)
        chips = _other_chips(x, y)

        def rows(px, py, pc):
            return out_ref.at[pl.ds((4 * px + 2 * py + pc) * m_per, m_per), :]

        def copy(k, block, to, src=None):
            return pltpu.make_async_remote_copy(
                src_ref=rows(*block) if src is None else src, dst_ref=rows(*block),
                send_sem=send_sems.at[k], recv_sem=recv_sems.at[k], device_id=to, device_id_type=MESH)

        mine = pltpu.make_async_copy(x_ref, rows(*me), local_sem)
        mine.start()
        first = [copy(0, me, sibling, src=x_ref)]
        first += [copy(1 + j, me, (*chip, c), src=x_ref) for j, chip in enumerate(chips)]
        for cp in first:
            cp.start()
        passed = [copy(4 + j, (*chip, c), sibling) for j, chip in enumerate(chips)]
        for j, chip in enumerate(chips):
            copy(1 + j, (*chip, c), me).wait_recv()
            passed[j].start()
        copy(0, sibling, me).wait_recv()
        for j, chip in enumerate(chips):
            copy(4 + j, (*chip, 1 - c), me).wait_recv()
        for cp in first + passed:
            cp.wait_send()
        mine.wait()

    return pl.pallas_call(
        body, name=name,
        out_shape=jax.ShapeDtypeStruct((N_DEV * m_per, n), x_shard.dtype),
        in_specs=[pl.BlockSpec(memory_space=space)],
        out_specs=pl.BlockSpec(memory_space=space),
        scratch_shapes=[pltpu.SemaphoreType.DMA((7,)), pltpu.SemaphoreType.DMA((7,)), pltpu.SemaphoreType.DMA],
        compiler_params=pltpu.CompilerParams(vmem_limit_bytes=VMEM_LIMIT),
    )(x_shard)


def _hbm_specs(n):
    return [pl.BlockSpec(memory_space=pl.ANY)] * n


class _GatherComm:
    def __init__(self, shards):
        self.n = n = len(shards)
        self.inputs = [s.reshape(2, s.shape[0] // 2, s.shape[1]) for s in shards]
        self.out_shape = [jax.ShapeDtypeStruct((N_DEV,) + s.shape[1:], s.dtype) for s in self.inputs]
        self.scratch = [pltpu.SemaphoreType.DMA((7 * n,)), pltpu.SemaphoreType.DMA((7 * n,))]

    def _parts(self, xs, outs, sems):
        send_sems, recv_sems = sems
        x, y, c = _my_place()

        def blk(k, px, py, pc):
            return outs[k].at[4 * px + 2 * py + pc]

        def copy(k, kind, block, to, own=False):
            return pltpu.make_async_remote_copy(
                src_ref=xs[k].at[c] if own else blk(k, *block), dst_ref=blk(k, *block),
                send_sem=send_sems.at[7 * k + kind], recv_sem=recv_sems.at[7 * k + kind],
                device_id=to, device_id_type=MESH)

        def whole(k):
            return pltpu.make_async_remote_copy(
                src_ref=xs[k], dst_ref=outs[k].at[pl.ds(4 * x + 2 * y, 2)],
                send_sem=send_sems.at[7 * k], recv_sem=recv_sems.at[7 * k],
                device_id=(x, y, 1 - c), device_id_type=MESH)

        me, sibling = (x, y, c), (x, y, 1 - c)
        chips = _other_chips(x, y)
        first = []
        for k in range(self.n):
            first.append(whole(k))
            first += [copy(k, 1 + j, me, (*chip, c), own=True) for j, chip in enumerate(chips)]
        return copy, whole, me, sibling, chips, c, first

    def start(self, xs, outs, sems):
        for cp in self._parts(xs, outs, sems)[-1]:
            cp.start()

    def finish(self, xs, outs, sems):
        copy, whole, me, sibling, chips, c, first = self._parts(xs, outs, sems)
        passed = []
        for j, chip in enumerate(chips):
            for k in range(self.n):
                copy(k, 1 + j, (*chip, c), me).wait_recv()
                fwd = copy(k, 4 + j, (*chip, c), sibling)
                fwd.start()
                passed.append(fwd)
        for k in range(self.n):
            whole(k).wait_recv()
        for j, chip in enumerate(chips):
            for k in range(self.n):
                copy(k, 4 + j, (*chip, 1 - c), me).wait_recv()
        for cp in first + passed:
            cp.wait_send()


class _AllGatherComm:
    def __init__(self, x):
        self.inputs = [x]
        self.n = 1
        self.out_shape = [jax.ShapeDtypeStruct((N_DEV,) + x.shape, x.dtype)]
        self.scratch = [pltpu.SemaphoreType.DMA((7,)), pltpu.SemaphoreType.DMA((7,)), pltpu.SemaphoreType.DMA]

    def _parts(self, xs, outs, sems):
        send_sems, recv_sems, local_sem = sems
        x_ref, out_ref = xs[0], outs[0]
        x, y, c = _my_place()

        def copy(k, block, to, own=False):
            blk = out_ref.at[4 * block[0] + 2 * block[1] + block[2]]
            return pltpu.make_async_remote_copy(
                src_ref=x_ref if own else blk, dst_ref=blk, send_sem=send_sems.at[k], recv_sem=recv_sems.at[k],
                device_id=to, device_id_type=MESH)

        me, sibling = (x, y, c), (x, y, 1 - c)
        chips = _other_chips(x, y)
        local = pltpu.make_async_copy(x_ref, out_ref.at[4 * x + 2 * y + c], local_sem)
        first = [copy(0, me, sibling, own=True)]
        first += [copy(1 + j, me, (*chip, c), own=True) for j, chip in enumerate(chips)]
        return copy, me, sibling, chips, c, local, first

    def start(self, xs, outs, sems):
        _, _, _, _, _, local, first = self._parts(xs, outs, sems)
        for cp in [local] + first:
            cp.start()

    def finish(self, xs, outs, sems):
        copy, me, sibling, chips, c, local, first = self._parts(xs, outs, sems)
        passed = []
        for j, chip in enumerate(chips):
            copy(1 + j, (*chip, c), me).wait_recv()
            fwd = copy(4 + j, (*chip, c), sibling)
            fwd.start()
            passed.append(fwd)
        copy(0, sibling, me).wait_recv()
        for j, chip in enumerate(chips):
            copy(4 + j, (*chip, 1 - c), me).wait_recv()
        for cp in first + passed:
            cp.wait_send()
        local.wait()


class _BothComm:
    def __init__(self, a, b):
        self.a, self.b = a, b
        self.inputs = a.inputs + b.inputs
        self.n = a.n + b.n
        self.out_shape = a.out_shape + b.out_shape
        self.scratch = a.scratch + b.scratch

    def _split(self, xs, outs, sems):
        na, ns = self.a.n, len(self.a.scratch)
        return (xs[:na], outs[:na], sems[:ns]), (xs[na:], outs[na:], sems[ns:])

    def start(self, xs, outs, sems):
        pa, pb = self._split(xs, outs, sems)
        self.a.start(*pa)
        self.b.start(*pb)

    def finish(self, xs, outs, sems):
        pa, pb = self._split(xs, outs, sems)
        self.a.finish(*pa)
        self.b.finish(*pb)


class _ToChipsComm:
    def __init__(self, a4s):
        self.inputs = list(a4s)
        self.n = n = len(a4s)
        nc = N_CHIP - 1
        self.out_shape = [jax.ShapeDtypeStruct((nc,) + a.shape[1:], a.dtype) for a in a4s]
        self.scratch = [pltpu.SemaphoreType.DMA((nc * n,)), pltpu.SemaphoreType.DMA((nc * n,))]

    def _copies(self, as_, rs, sems):
        send_sems, recv_sems = sems
        x, y, c = _my_place()
        nc = N_CHIP - 1
        return [pltpu.make_async_remote_copy(
            src_ref=as_[k].at[2 * cx + cy], dst_ref=rs[k].at[j], send_sem=send_sems.at[nc * k + j],
            recv_sem=recv_sems.at[nc * k + j], device_id=(cx, cy, c), device_id_type=MESH)
            for k in range(self.n) for j, (cx, cy) in enumerate(_other_chips(x, y))]

    def start(self, as_, rs, sems):
        for cp in self._copies(as_, rs, sems):
            cp.start()

    def finish(self, as_, rs, sems):
        for cp in self._copies(as_, rs, sems):
            cp.wait()


def _run_comm(comm, name):
    n = comm.n

    def body(*refs):
        ins, outs, sems = refs[:n], refs[n:2 * n], refs[2 * n:]
        comm.start(ins, outs, sems)
        comm.finish(ins, outs, sems)

    return pl.pallas_call(
        body, name=name, out_shape=comm.out_shape, in_specs=_hbm_specs(n), out_specs=_hbm_specs(n),
        scratch_shapes=comm.scratch,
    )(*comm.inputs)


class _ToSiblingComm:
    def __init__(self, g8s):
        self.inputs = list(g8s)
        self.n = n = len(g8s)
        self.out_shape = [jax.ShapeDtypeStruct((N_CHIP,) + g.shape[1:], g.dtype) for g in g8s]
        self.scratch = [pltpu.SemaphoreType.DMA((N_CHIP * n,)), pltpu.SemaphoreType.DMA((N_CHIP * n,))]

    def _copies(self, gs, rs, sems):
        send_sems, recv_sems = sems
        x, y, c = _my_place()
        return [pltpu.make_async_remote_copy(
            src_ref=gs[k].at[2 * s + 1 - c], dst_ref=rs[k].at[s], send_sem=send_sems.at[N_CHIP * k + s],
            recv_sem=recv_sems.at[N_CHIP * k + s], device_id=(x, y, 1 - c), device_id_type=MESH)
            for k in range(self.n) for s in range(N_CHIP)]

    def start(self, gs, rs, sems):
        for cp in self._copies(gs, rs, sems):
            cp.start()

    def finish(self, gs, rs, sems):
        for cp in self._copies(gs, rs, sems):
            cp.wait()


def _swap_halves(hs, name):
    n = len(hs)

    def body(*refs):
        o_refs = refs[n:2 * n]
        send_sems, recv_sems = refs[2 * n:]
        x, y, c = _my_place()

        def remote(k, slot):
            return pltpu.make_async_remote_copy(
                src_ref=o_refs[k].at[slot], dst_ref=o_refs[k].at[slot], send_sem=send_sems.at[k],
                recv_sem=recv_sems.at[k], device_id=(x, y, 1 - c), device_id_type=MESH)

        sends = [remote(k, c) for k in range(n)]
        for cp in sends:
            cp.start()
        for k in range(n):
            remote(k, 1 - c).wait_recv()
        for cp in sends:
            cp.wait_send()

    return pl.pallas_call(
        body, name=name,
        out_shape=[jax.ShapeDtypeStruct(h.shape, h.dtype) for h in hs],
        in_specs=_hbm_specs(n), out_specs=_hbm_specs(n),
        input_output_aliases={k: k for k in range(n)},
        scratch_shapes=[pltpu.SemaphoreType.DMA((n,)), pltpu.SemaphoreType.DMA((n,))],
    )(*hs)


ADD_TILES = 2


def _add_blocks(a_list, a_idx_fn, others_list, ns, sel, name, out_blocks=None, out_idx_fn=None,
                bf16_copy=False):
    out_blocks = out_blocks or ns
    out_idx_fn = out_idx_fn or (lambda s, sel_ref: s)
    n = len(a_list)
    n_o = len(others_list[0])
    per = 1 + n_o

    def body(sel_ref, *refs):
        for k in range(n):
            ins = refs[k * per:(k + 1) * per]
            acc = ins[0][0]
            for r in ins[1:]:
                acc = acc + r[0].astype(F32)
            refs[n * per + k][0] = acc
            if bf16_copy:
                refs[n * per + n + k][0] = acc.astype(BF16)

    in_specs, args, out_specs, out_shape = [], [], [], []
    for a, others in zip(a_list, others_list):
        _, R, N = a.shape
        tr = R // ADD_TILES
        assert tr % 8 == 0, a.shape
        in_specs.append(pl.BlockSpec((1, tr, N), lambda s, i, sel_ref: (a_idx_fn(s, sel_ref), i, 0)))
        args.append(a)
        for arr, fixed in others:
            if fixed is None:
                in_specs.append(pl.BlockSpec((1, tr, N), lambda s, i, sel_ref: (s, i, 0)))
            else:
                in_specs.append(pl.BlockSpec((1, tr, N), lambda s, i, sel_ref, fixed=fixed: (fixed, i, 0)))
            args.append(arr)
        out_specs.append(pl.BlockSpec((1, tr, N), lambda s, i, sel_ref: (out_idx_fn(s, sel_ref), i, 0)))
        out_shape.append(jax.ShapeDtypeStruct((out_blocks, R, N), a.dtype))
    if bf16_copy:
        out_specs = out_specs + out_specs
        out_shape = out_shape + [jax.ShapeDtypeStruct(o.shape, BF16) for o in out_shape]
    grid_spec = pltpu.PrefetchScalarGridSpec(num_scalar_prefetch=1, grid=(ns, ADD_TILES), in_specs=in_specs,
                                             out_specs=out_specs)
    return pl.pallas_call(
        body, name=name, out_shape=out_shape, grid_spec=grid_spec,
        compiler_params=_cparams(("parallel", "parallel")),
    )(sel, *args)


def _rs_first(g8s, tag, r1=None):
    c_sel = jnp.reshape(lax.axis_index("c"), (1,)).astype(jnp.int32)
    if r1 is None:
        r1 = _run_comm(_ToSiblingComm(g8s), f"rs_to_sibling_{tag}")
    res = _add_blocks(g8s, lambda s, sel: 2 * s + sel[0], [[(r, None)] for r in r1], N_CHIP, c_sel,
                      f"rs_add_sibling_{tag}", bf16_copy=True)
    return list(res[:len(g8s)]), list(res[len(g8s):])


def _rs_last(a4s, r2s, tag):
    sel = jnp.stack([2 * lax.axis_index("x") + lax.axis_index("y"), lax.axis_index("c")]).astype(jnp.int32)
    h = _add_blocks(a4s, lambda s, sel: sel[0], [[(r, 0), (r, 1), (r, 2)] for r in r2s], 1, sel,
                    f"rs_add_chips_{tag}", out_blocks=2, out_idx_fn=lambda s, sel: sel[1])
    full = _swap_halves(h, f"rs_swap_halves_{tag}")
    return [f.reshape(2 * f.shape[1], f.shape[2]) for f in full]


def _ada_fwd(c_all, w_ada, b_ada, name):
    nb, D = c_all.shape
    ncol = w_ada.shape[1]
    tc = 512

    def body(c_ref, w_ref, b_ref, o_ref):
        cv = c_ref[...]
        cond = (cv * jax.nn.sigmoid(cv)).astype(BF16)
        o_ref[...] = jnp.dot(cond, w_ref[...].astype(BF16), preferred_element_type=F32) + b_ref[...]

    return pl.pallas_call(
        body, name=name, out_shape=jax.ShapeDtypeStruct((nb, ncol), F32), grid=(ncol // tc,),
        in_specs=[pl.BlockSpec((nb, D), lambda j: (0, 0)), pl.BlockSpec((D, tc), lambda j: (0, j)),
                  pl.BlockSpec((1, tc), lambda j: (0, j))],
        out_specs=pl.BlockSpec((nb, tc), lambda j: (0, j)),
        compiler_params=_cparams(("parallel",)),
    )(c_all, w_ada, b_ada)


def _ada_bwd(c_all, gmod_cols, name):
    nb, D = c_all.shape
    ncol = gmod_cols.shape[1]
    tc = 512

    def body(c_ref, g_ref, o_ref):
        cv = c_ref[...]
        cond = (cv * jax.nn.sigmoid(cv)).astype(BF16)
        o_ref[...] = _dot_tn(cond, g_ref[...].astype(BF16))

    return pl.pallas_call(
        body, name=name, out_shape=jax.ShapeDtypeStruct((D, ncol), F32), grid=(ncol // tc,),
        in_specs=[pl.BlockSpec((nb, D), lambda j: (0, 0)), pl.BlockSpec((nb, tc), lambda j: (0, j))],
        out_specs=pl.BlockSpec((D, tc), lambda j: (0, j)),
        compiler_params=_cparams(("parallel",)),
    )(c_all, gmod_cols)


def _adam_math(w, g, m, v):
    m = ADAM_B1 * m + (1.0 - ADAM_B1) * g
    v = ADAM_B2 * v + (1.0 - ADAM_B2) * (g * g)
    m_hat = m / (1.0 - ADAM_B1 ** ADAM_STEP)
    v_hat = v / (1.0 - ADAM_B2 ** ADAM_STEP)
    delta = -ADAM_LR * (m_hat / (jnp.sqrt(v_hat) + ADAM_EPS) + ADAM_WD * w)
    return delta, m, v


ADAMW_TILES = 8


def _adamw(params, name):
    n = len(params)

    def body(*refs):
        for k in range(n):
            w_ref, g_ref, m_ref, v_ref = refs[4 * k:4 * k + 4]
            d, mn, vn = _adam_math(w_ref[...], g_ref[...], m_ref[...], v_ref[...])
            for o_ref, val in zip(refs[4 * n + 3 * k:4 * n + 3 * k + 3], (d, mn, vn)):
                o_ref[...] = val

    in_specs, out_specs, out_shape, args = [], [], [], []
    for p in params:
        rows, cols = p[0].shape
        assert rows % (8 * ADAMW_TILES) == 0, p[0].shape
        spec = pl.BlockSpec((rows // ADAMW_TILES, cols), lambda i: (i, 0))
        in_specs += [spec] * 4
        out_specs += [spec] * 3
        out_shape += [jax.ShapeDtypeStruct((rows, cols), F32)] * 3
        args += list(p)
    res = pl.pallas_call(
        body, name=name, out_shape=out_shape, grid=(ADAMW_TILES,), in_specs=in_specs, out_specs=out_specs,
        compiler_params=_cparams(("parallel",)),
    )(*args)
    return [tuple(res[3 * k:3 * k + 3]) for k in range(n)]


VEC_ROWS = 8


def _adamw_rows(w, parts, m, v, name):
    n = w.shape[1]
    P = parts.shape[0]
    assert n % (VEC_ROWS * LANES) == 0, n
    shp = (VEC_ROWS, n // VEC_ROWS)

    def body(w_ref, p_ref, m_ref, v_ref, g_ref, d_ref, mo_ref, vo_ref):
        g = p_ref[0]
        for k in range(1, P):
            g = g + p_ref[k]
        d, mn, vn = _adam_math(w_ref[...], g, m_ref[...], v_ref[...])
        g_ref[...] = g
        d_ref[...] = d
        mo_ref[...] = mn
        vo_ref[...] = vn

    vec = pl.BlockSpec(shp, lambda i: (0, 0))
    out = pl.pallas_call(
        body, name=name, out_shape=[jax.ShapeDtypeStruct(shp, F32)] * 4, grid=(1,),
        in_specs=[vec, pl.BlockSpec((P,) + shp, lambda i: (0, 0, 0)), vec, vec], out_specs=[vec] * 4,
        compiler_params=_cparams(("arbitrary",)),
    )(w.reshape(shp), parts.reshape((P,) + shp), m.reshape(shp), v.reshape(shp))
    return [o.reshape(1, n) for o in out]


_SHARDED = ("w_in", "w_uq", "w_ukv", "w_out", "w_ffn_in", "w_ffn_out")
_SMALL = (("g_norm1", 1024), ("g_cq", 384), ("g_ckv", 256), ("rel_bias", 256), ("g_out_a", 512),
          ("g_out_b", 512), ("g_norm2", 1024), ("g_final", 1024))
_SMALL_PAD = 5120


def _full_from_shards(sh):
    return jnp.transpose(sh, (1, 0, 2)).reshape(sh.shape[1], -1)


def _shards_from_full(full):
    rows, cols = full.shape
    return jnp.transpose(full.reshape(rows, N_CHIP, cols // N_CHIP), (1, 0, 2))


def kernel(x, c, w_ada, b_ada, g_norm1, w_in, g_cq, w_uq, g_ckv, w_ukv, rel_bias, g_out_a, g_out_b, w_out, g_norm2, w_ffn_in, w_ffn_out, g_final, loss_target, m_w_ada, m_b_ada, m_g_norm1, m_w_in, m_g_cq, m_w_uq, m_g_ckv, m_w_ukv, m_rel_bias, m_g_out_a, m_g_out_b, m_w_out, m_g_norm2, m_w_ffn_in, m_w_ffn_out, m_g_final, v_w_ada, v_b_ada, v_g_norm1, v_w_in, v_g_cq, v_w_uq, v_g_ckv, v_w_ukv, v_rel_bias, v_g_out_a, v_g_out_b, v_w_out, v_g_norm2, v_w_ffn_in, v_w_ffn_out, v_g_final):
    names = ["w_ada", "b_ada", "g_norm1", "w_in", "g_cq", "w_uq", "g_ckv", "w_ukv", "rel_bias", "g_out_a",
             "g_out_b", "w_out", "g_norm2", "w_ffn_in", "w_ffn_out", "g_final"]
    W = dict(zip(names, [w_ada, b_ada, g_norm1, w_in, g_cq, w_uq, g_ckv, w_ukv, rel_bias, g_out_a, g_out_b,
                         w_out, g_norm2, w_ffn_in, w_ffn_out, g_final]))
    M = dict(zip(names, [m_w_ada, m_b_ada, m_g_norm1, m_w_in, m_g_cq, m_w_uq, m_g_ckv, m_w_ukv, m_rel_bias,
                         m_g_out_a, m_g_out_b, m_w_out, m_g_norm2, m_w_ffn_in, m_w_ffn_out, m_g_final]))
    V = dict(zip(names, [v_w_ada, v_b_ada, v_g_norm1, v_w_in, v_g_cq, v_w_uq, v_g_ckv, v_w_ukv, v_rel_bias,
                         v_g_out_a, v_g_out_b, v_w_out, v_g_norm2, v_w_ffn_in, v_w_ffn_out, v_g_final]))
    B, S, D = x.shape
    mx, my, mc = _my_place()
    dev = 4 * mx + 2 * my + mc
    chip = 2 * mx + my
    pad_rows = 8

    early = ("w_in", "w_uq", "w_ukv")
    bias, got = _bias_tables(rel_bias, _bucket_tables(), "rel_bias_tables",
                             _BothComm(_AllGatherComm(jnp.pad(c, ((0, pad_rows - B), (0, 0)))),
                                       _GatherComm([W[n][0].astype(BF16) for n in early])))
    c_all = got[0][:, :B].reshape(N_DEV * B, D)

    ada_cols = w_ada.shape[-1]
    b_cols = lax.dynamic_slice_in_dim(b_ada, chip * ada_cols, ada_cols, axis=1)
    mod_cols = _ada_fwd(c_all, w_ada[0], b_cols, "ada_fwd")
    mod_all = _allgather8(mod_cols, "ag_mod", False).reshape(N_DEV, N_DEV * B, ada_cols)[0::2]
    mod_all = jnp.transpose(mod_all, (1, 0, 2)).reshape(N_DEV * B, N_MOD * D)
    mod = lax.dynamic_slice_in_dim(mod_all, dev * B, B, axis=0)

    full = {n: g.reshape((N_CHIP,) + W[n].shape[1:]) for n, g in zip(early, got[1:])}
    wts = dict(w_in=_w_in_to_kernel(_full_from_shards(full["w_in"])),
               w_uq=_w_uq_to_kernel(_full_from_shards(full["w_uq"])),
               w_kv=_w_ukv_to_kernel(_full_from_shards(full["w_ukv"])))
    gains = dict(g_norm1=g_norm1, g_cq=g_cq, g_ckv=g_ckv, g_out_a=g_out_a, g_out_b=g_out_b, g_norm2=g_norm2,
                 g_final=g_final.reshape(1, D))

    loss, grad_x, gmod, grads = _local_step(x, loss_target, mod, wts, gains, rel_bias,
                                            ffn_shards=[w_ffn_in[0].astype(BF16), w_ffn_out[0].astype(BF16),
                                                        w_out[0].astype(BF16)], bias=bias)
    loss = lax.psum(loss[0, 0], ("x", "y", "c"))

    n_small = _SMALL_PAD
    cat = lambda dct: jnp.concatenate([dct[n].reshape(1, -1) for n, _ in _SMALL]
                                      + [jnp.zeros((1, _SMALL_PAD - sum(s for _, s in _SMALL)), F32)], axis=1)
    small = cat(grads)
    rows = jnp.concatenate([gmod, jnp.pad(small, ((0, 0), (0, N_MOD * D - n_small))),
                            jnp.zeros((pad_rows - B - 1, N_MOD * D), F32)], axis=0)
    rows_all = _allgather8(rows, "ag_small", False).reshape(N_DEV, pad_rows, N_MOD * D)
    gmod_all = rows_all[:, :B].reshape(N_DEV * B, N_MOD * D)
    small_parts = rows_all[:, B, :n_small]

    a4, r2 = grads["mix_pending"]
    ffn_a4, ffn_r2 = grads["ffn_pending"]
    G = dict(zip(_SHARDED, _rs_last(list(a4) + list(ffn_a4), list(r2) + list(ffn_r2), "all")))

    gmod_cols = lax.dynamic_slice_in_dim(gmod_all, chip * ada_cols, ada_cols, axis=1)
    G["w_ada"] = _ada_bwd(c_all, gmod_cols, "ada_bwd")
    delta, new_m, new_v = {}, {}, {}
    big = ("w_ada",) + _SHARDED
    two_d = lambda a: a.reshape(a.shape[-2], a.shape[-1])
    updates = _adamw([(two_d(W[n]), G[n], two_d(M[n]), two_d(V[n])) for n in big], "adamw_sharded")
    for n, (d_, m_, v_) in zip(big, updates):
        G[n], delta[n], new_m[n], new_v[n] = [a.reshape(W[n].shape) for a in (G[n], d_, m_, v_)]
    gs, ds_, ms_, vs_ = _adamw_rows(cat(W), small_parts, cat(M), cat(V), "adamw_small")
    off = 0
    for n, sz in _SMALL:
        shp = W[n].shape
        G[n], delta[n], new_m[n], new_v[n] = [a[:, off:off + sz].reshape(shp) for a in (gs, ds_, ms_, vs_)]
        off += sz
    G["b_ada"], delta["b_ada"], new_m["b_ada"], new_v["b_ada"] = _adamw_rows(b_ada, gmod_all, m_b_ada, v_b_ada,
                                                                          "adamw_b_ada")
    return (loss, grad_x, *[G[n] for n in names], *[delta[n] for n in names], *[new_m[n] for n in names],
            *[new_v[n] for n in names])
```
